```python
import jax, jax.numpy as jnp
from jax import lax
import numpy as np

D_MODEL = 1024
BATCH = 8
SEQ = 8192
DEPTH = 1

CHUNK = 64
Q_BLOCK = 128
FOX_HEAD_DIM = 128
N_FOX_HEADS = D_MODEL // FOX_HEAD_DIM
FOX_WIDTH = N_FOX_HEADS * FOX_HEAD_DIM
SGU_GROUP_DIM = 128
N_SGU_GROUPS = D_MODEL // SGU_GROUP_DIM
SGU_WIDTH = N_SGU_GROUPS * SGU_GROUP_DIM
SGU_LEN = 128
D_FF = 4 * D_MODEL
RMS_EPS = 1e-6
LN_EPS = 1e-5

COL_Q = 0
COL_K = COL_Q + FOX_WIDTH
COL_V = COL_K + FOX_WIDTH
COL_F = COL_V + FOX_WIDTH
COL_U = COL_F + N_FOX_HEADS
COL_SV = COL_U + SGU_WIDTH
COL_GA = COL_SV + SGU_WIDTH
COL_GB = COL_GA + D_MODEL
IN_WIDTH = COL_GB + D_MODEL

kernel_name = "fox_gmlp_gated_macaron_block"


def rmsnorm(x, g):
    xf = x.astype(jnp.float32)
    y = xf * lax.rsqrt(jnp.mean(xf * xf, axis=-1, keepdims=True) + RMS_EPS)
    return (y * g.astype(jnp.float32)).astype(x.dtype)


def swiglu(h, w_gate, w_up, w_down):
    return (jax.nn.silu(h @ w_gate) * (h @ w_up)) @ w_down


def forgetting_attention(q, k, v, f_logit, b_forget):
    B, S, H, D = q.shape
    nb = S // Q_BLOCK
    scale = 1.0 / np.sqrt(D).astype(np.float32)
    log_f = jax.nn.log_sigmoid(f_logit.astype(jnp.float32) + b_forget.astype(jnp.float32))
    c = jnp.cumsum(log_f, axis=1).transpose(0, 2, 1)
    kh = k.transpose(0, 2, 1, 3)
    vh = v.transpose(0, 2, 1, 3)
    qb = q.transpose(0, 2, 1, 3).reshape(B, H, nb, Q_BLOCK, D).transpose(2, 0, 1, 3, 4)
    cb = c.reshape(B, H, nb, Q_BLOCK).transpose(2, 0, 1, 3)
    k_pos = jnp.arange(S)

    def block(args):
        q_blk, c_blk, idx = args
        s = jnp.einsum('bhqd,bhkd->bhqk', q_blk, kh).astype(jnp.float32) * scale
        s = s + c_blk[..., :, None] - c[:, :, None, :]
        q_pos = idx * Q_BLOCK + jnp.arange(Q_BLOCK)
        s = jnp.where(k_pos[None, :] <= q_pos[:, None], s, -jnp.inf)
        p = jax.nn.softmax(s, axis=-1).astype(vh.dtype)
        return jnp.einsum('bhqk,bhkd->bhqd', p, vh)

    out = lax.map(block, (qb, cb, jnp.arange(nb)))
    return out.transpose(1, 0, 3, 2, 4).reshape(B, S, H * D)


def spatial_gating(u, v, ln_g, ln_b, w_s, b_s):
    B, S, W = v.shape
    G, C, L = N_SGU_GROUPS, SGU_GROUP_DIM, SGU_LEN
    vf = v.astype(jnp.float32).reshape(B, S, G, C)
    mu = jnp.mean(vf, axis=-1, keepdims=True)
    var = jnp.mean(jnp.square(vf - mu), axis=-1, keepdims=True)
    vn = ((vf - mu) * lax.rsqrt(var + LN_EPS)).reshape(B, S, W)
    vn = (vn * ln_g.astype(jnp.float32) + ln_b.astype(jnp.float32)).reshape(B, S // L, L, G, C)
    pos = jnp.arange(L)
    mask = (pos[None, :] // CHUNK) <= (pos[:, None] // CHUNK)
    w = jnp.where(mask[None], w_s.astype(jnp.float32), 0.0)
    mixed = jnp.einsum('gts,bnsgc->bntgc', w, vn) + b_s.astype(jnp.float32).T[None, None, :, :, None]
    return u * mixed.reshape(B, S, W).astype(u.dtype)


def _fwd_setup_inputs(seed: int = 0) -> dict:
    key = jax.random.key(seed)
    ks = jax.random.split(key, 24)
    L, D, F = DEPTH, D_MODEL, D_FF
    nrm = lambda k, shape, fan_in: jax.random.normal(k, shape, jnp.float32) * (fan_in ** -0.5)
    gain = lambda k, shape: 1.0 + 0.05 * jax.random.normal(k, shape, jnp.float32)
    return {
        "x": jax.random.normal(ks[0], (BATCH, SEQ, D), jnp.float32),
        "ffn1_pre_g": gain(ks[1], (L, D)),
        "ffn1_w_gate": nrm(ks[2], (L, D, F), D),
        "ffn1_w_up": nrm(ks[3], (L, D, F), D),
        "ffn1_w_down": nrm(ks[4], (L, F, D), F),
        "ffn1_post_g": gain(ks[5], (L, D)),
        "mix_pre_g": gain(ks[6], (L, D)),
        "w_in": nrm(ks[7], (L, D, IN_WIDTH), D),
        "b_forget": jax.random.uniform(ks[8], (L, N_FOX_HEADS), jnp.float32, 2.0, 6.0),
        "sgu_ln_g": gain(ks[9], (L, SGU_WIDTH)),
        "sgu_ln_b": 0.02 * jax.random.normal(ks[10], (L, SGU_WIDTH), jnp.float32),
        "sgu_w_s": nrm(ks[11], (L, N_SGU_GROUPS, SGU_LEN, SGU_LEN), SGU_LEN),
        "sgu_b_s": 1.0 + 0.02 * jax.random.normal(ks[12], (L, N_SGU_GROUPS, SGU_LEN), jnp.float32),
        "w_out": nrm(ks[13], (L, D, D), D),
        "mix_post_g": gain(ks[14], (L, D)),
        "ffn2_pre_g": gain(ks[15], (L, D)),
        "ffn2_w_gate": nrm(ks[16], (L, D, F), D),
        "ffn2_w_up": nrm(ks[17], (L, D, F), D),
        "ffn2_w_down": nrm(ks[18], (L, F, D), F),
        "ffn2_post_g": gain(ks[19], (L, D)),
    }


def _fwd_reference(x, ffn1_pre_g, ffn1_w_gate, ffn1_w_up, ffn1_w_down, ffn1_post_g,
              mix_pre_g, w_in, b_forget, sgu_ln_g, sgu_ln_b, sgu_w_s, sgu_b_s,
              w_out, mix_post_g, ffn2_pre_g, ffn2_w_gate, ffn2_w_up, ffn2_w_down,
              ffn2_post_g):
    B, S, D = x.shape
    H, HD = N_FOX_HEADS, FOX_HEAD_DIM
    for l in range(DEPTH):
        h = rmsnorm(x, ffn1_pre_g[l])
        x = x + 0.5 * rmsnorm(swiglu(h, ffn1_w_gate[l], ffn1_w_up[l], ffn1_w_down[l]), ffn1_post_g[l])

        h = rmsnorm(x, mix_pre_g[l])
        z = h @ w_in[l]
        q = z[..., COL_Q:COL_K].reshape(B, S, H, HD)
        k = z[..., COL_K:COL_V].reshape(B, S, H, HD)
        v = z[..., COL_V:COL_F].reshape(B, S, H, HD)
        f_logit = z[..., COL_F:COL_U]
        u_s = jax.nn.gelu(z[..., COL_U:COL_SV], approximate=False)
        v_s = jax.nn.gelu(z[..., COL_SV:COL_GA], approximate=False)
        gate_a = jax.nn.sigmoid(z[..., COL_GA:COL_GB])
        gate_b = jax.nn.sigmoid(z[..., COL_GB:IN_WIDTH])

        o_a = forgetting_attention(q, k, v, f_logit, b_forget[l])
        o_b = spatial_gating(u_s, v_s, sgu_ln_g[l], sgu_ln_b[l], sgu_w_s[l], sgu_b_s[l])
        merged = gate_a * o_a + gate_b * o_b
        x = x + rmsnorm(merged @ w_out[l], mix_post_g[l])

        h = rmsnorm(x, ffn2_pre_g[l])
        x = x + 0.5 * rmsnorm(swiglu(h, ffn2_w_gate[l], ffn2_w_up[l], ffn2_w_down[l]), ffn2_post_g[l])
    return x


import jax as _jax
import jax.numpy as _jnp

TWIN_FORMAT = 'train_step'
FWD_PARAMS = ['x', 'ffn1_pre_g', 'ffn1_w_gate', 'ffn1_w_up', 'ffn1_w_down', 'ffn1_post_g', 'mix_pre_g', 'w_in', 'b_forget', 'sgu_ln_g', 'sgu_ln_b', 'sgu_w_s', 'sgu_b_s', 'w_out', 'mix_post_g', 'ffn2_pre_g', 'ffn2_w_gate', 'ffn2_w_up', 'ffn2_w_down', 'ffn2_post_g']
TWIN_WEIGHTS = ['ffn1_pre_g', 'ffn1_w_gate', 'ffn1_w_up', 'ffn1_w_down', 'ffn1_post_g', 'mix_pre_g', 'w_in', 'b_forget', 'sgu_ln_g', 'sgu_ln_b', 'sgu_w_s', 'sgu_b_s', 'w_out', 'mix_post_g', 'ffn2_pre_g', 'ffn2_w_gate', 'ffn2_w_up', 'ffn2_w_down', 'ffn2_post_g']
TWIN_DIFF_INPUT = 'x'
TWIN_INPUTS = ['x', 'ffn1_pre_g', 'ffn1_w_gate', 'ffn1_w_up', 'ffn1_w_down', 'ffn1_post_g', 'mix_pre_g', 'w_in', 'b_forget', 'sgu_ln_g', 'sgu_ln_b', 'sgu_w_s', 'sgu_b_s', 'w_out', 'mix_post_g', 'ffn2_pre_g', 'ffn2_w_gate', 'ffn2_w_up', 'ffn2_w_down', 'ffn2_post_g', 'loss_target', 'm_ffn1_pre_g', 'm_ffn1_w_gate', 'm_ffn1_w_up', 'm_ffn1_w_down', 'm_ffn1_post_g', 'm_mix_pre_g', 'm_w_in', 'm_b_forget', 'm_sgu_ln_g', 'm_sgu_ln_b', 'm_sgu_w_s', 'm_sgu_b_s', 'm_w_out', 'm_mix_post_g', 'm_ffn2_pre_g', 'm_ffn2_w_gate', 'm_ffn2_w_up', 'm_ffn2_w_down', 'm_ffn2_post_g', 'v_ffn1_pre_g', 'v_ffn1_w_gate', 'v_ffn1_w_up', 'v_ffn1_w_down', 'v_ffn1_post_g', 'v_mix_pre_g', 'v_w_in', 'v_b_forget', 'v_sgu_ln_g', 'v_sgu_ln_b', 'v_sgu_w_s', 'v_sgu_b_s', 'v_w_out', 'v_mix_post_g', 'v_ffn2_pre_g', 'v_ffn2_w_gate', 'v_ffn2_w_up', 'v_ffn2_w_down', 'v_ffn2_post_g']
TWIN_OUTPUTS = ['loss', 'grad_x', 'grad_ffn1_pre_g', 'grad_ffn1_w_gate', 'grad_ffn1_w_up', 'grad_ffn1_w_down', 'grad_ffn1_post_g', 'grad_mix_pre_g', 'grad_w_in', 'grad_b_forget', 'grad_sgu_ln_g', 'grad_sgu_ln_b', 'grad_sgu_w_s', 'grad_sgu_b_s', 'grad_w_out', 'grad_mix_post_g', 'grad_ffn2_pre_g', 'grad_ffn2_w_gate', 'grad_ffn2_w_up', 'grad_ffn2_w_down', 'grad_ffn2_post_g', 'delta_ffn1_pre_g', 'delta_ffn1_w_gate', 'delta_ffn1_w_up', 'delta_ffn1_w_down', 'delta_ffn1_post_g', 'delta_mix_pre_g', 'delta_w_in', 'delta_b_forget', 'delta_sgu_ln_g', 'delta_sgu_ln_b', 'delta_sgu_w_s', 'delta_sgu_b_s', 'delta_w_out', 'delta_mix_post_g', 'delta_ffn2_pre_g', 'delta_ffn2_w_gate', 'delta_ffn2_w_up', 'delta_ffn2_w_down', 'delta_ffn2_post_g', 'new_m_ffn1_pre_g', 'new_m_ffn1_w_gate', 'new_m_ffn1_w_up', 'new_m_ffn1_w_down', 'new_m_ffn1_post_g', 'new_m_mix_pre_g', 'new_m_w_in', 'new_m_b_forget', 'new_m_sgu_ln_g', 'new_m_sgu_ln_b', 'new_m_sgu_w_s', 'new_m_sgu_b_s', 'new_m_w_out', 'new_m_mix_post_g', 'new_m_ffn2_pre_g', 'new_m_ffn2_w_gate', 'new_m_ffn2_w_up', 'new_m_ffn2_w_down', 'new_m_ffn2_post_g', 'new_v_ffn1_pre_g', 'new_v_ffn1_w_gate', 'new_v_ffn1_w_up', 'new_v_ffn1_w_down', 'new_v_ffn1_post_g', 'new_v_mix_pre_g', 'new_v_w_in', 'new_v_b_forget', 'new_v_sgu_ln_g', 'new_v_sgu_ln_b', 'new_v_sgu_w_s', 'new_v_sgu_b_s', 'new_v_w_out', 'new_v_mix_post_g', 'new_v_ffn2_pre_g', 'new_v_ffn2_w_gate', 'new_v_ffn2_w_up', 'new_v_ffn2_w_down', 'new_v_ffn2_post_g']
TWIN_LEAF_KINDS = {'loss': 'loss', 'grad_x': 'grad_x', 'grad_ffn1_pre_g': 'grad_w', 'grad_ffn1_w_gate': 'grad_w', 'grad_ffn1_w_up': 'grad_w', 'grad_ffn1_w_down': 'grad_w', 'grad_ffn1_post_g': 'grad_w', 'grad_mix_pre_g': 'grad_w', 'grad_w_in': 'grad_w', 'grad_b_forget': 'grad_w', 'grad_sgu_ln_g': 'grad_w', 'grad_sgu_ln_b': 'grad_w', 'grad_sgu_w_s': 'grad_w', 'grad_sgu_b_s': 'grad_w', 'grad_w_out': 'grad_w', 'grad_mix_post_g': 'grad_w', 'grad_ffn2_pre_g': 'grad_w', 'grad_ffn2_w_gate': 'grad_w', 'grad_ffn2_w_up': 'grad_w', 'grad_ffn2_w_down': 'grad_w', 'grad_ffn2_post_g': 'grad_w', 'delta_ffn1_pre_g': 'delta_w', 'delta_ffn1_w_gate': 'delta_w', 'delta_ffn1_w_up': 'delta_w', 'delta_ffn1_w_down': 'delta_w', 'delta_ffn1_post_g': 'delta_w', 'delta_mix_pre_g': 'delta_w', 'delta_w_in': 'delta_w', 'delta_b_forget': 'delta_w', 'delta_sgu_ln_g': 'delta_w', 'delta_sgu_ln_b': 'delta_w', 'delta_sgu_w_s': 'delta_w', 'delta_sgu_b_s': 'delta_w', 'delta_w_out': 'delta_w', 'delta_mix_post_g': 'delta_w', 'delta_ffn2_pre_g': 'delta_w', 'delta_ffn2_w_gate': 'delta_w', 'delta_ffn2_w_up': 'delta_w', 'delta_ffn2_w_down': 'delta_w', 'delta_ffn2_post_g': 'delta_w', 'new_m_ffn1_pre_g': 'new_m', 'new_m_ffn1_w_gate': 'new_m', 'new_m_ffn1_w_up': 'new_m', 'new_m_ffn1_w_down': 'new_m', 'new_m_ffn1_post_g': 'new_m', 'new_m_mix_pre_g': 'new_m', 'new_m_w_in': 'new_m', 'new_m_b_forget': 'new_m', 'new_m_sgu_ln_g': 'new_m', 'new_m_sgu_ln_b': 'new_m', 'new_m_sgu_w_s': 'new_m', 'new_m_sgu_b_s': 'new_m', 'new_m_w_out': 'new_m', 'new_m_mix_post_g': 'new_m', 'new_m_ffn2_pre_g': 'new_m', 'new_m_ffn2_w_gate': 'new_m', 'new_m_ffn2_w_up': 'new_m', 'new_m_ffn2_w_down': 'new_m', 'new_m_ffn2_post_g': 'new_m', 'new_v_ffn1_pre_g': 'new_v', 'new_v_ffn1_w_gate': 'new_v', 'new_v_ffn1_w_up': 'new_v', 'new_v_ffn1_w_down': 'new_v', 'new_v_ffn1_post_g': 'new_v', 'new_v_mix_pre_g': 'new_v', 'new_v_w_in': 'new_v', 'new_v_b_forget': 'new_v', 'new_v_sgu_ln_g': 'new_v', 'new_v_sgu_ln_b': 'new_v', 'new_v_sgu_w_s': 'new_v', 'new_v_sgu_b_s': 'new_v', 'new_v_w_out': 'new_v', 'new_v_mix_post_g': 'new_v', 'new_v_ffn2_pre_g': 'new_v', 'new_v_ffn2_w_gate': 'new_v', 'new_v_ffn2_w_up': 'new_v', 'new_v_ffn2_w_down': 'new_v', 'new_v_ffn2_post_g': 'new_v'}


def _forward(args):
    return _fwd_reference(*[args[k] for k in FWD_PARAMS])


def _output_shape():
    def fwd():
        inp = _fwd_setup_inputs(0)
        return _fwd_reference(*[inp[k] for k in FWD_PARAMS])
    out = _jax.eval_shape(fwd)
    return out.shape, out.dtype

N_MICROBATCH = 1
ADAM_LR = 0.001
ADAM_B1 = 0.9
ADAM_B2 = 0.999
ADAM_EPS = 1e-08
ADAM_WD = 0.01
ADAM_STEP = 10
PER_EXAMPLE_BATCH_AXIS = {'x': 0, 'loss_target': 0}
SHARED_INPUTS = []
_WEIGHT_DTYPES = {'ffn1_pre_g': _jnp.float32, 'ffn1_w_gate': _jnp.float32, 'ffn1_w_up': _jnp.float32, 'ffn1_w_down': _jnp.float32, 'ffn1_post_g': _jnp.float32, 'mix_pre_g': _jnp.float32, 'w_in': _jnp.float32, 'b_forget': _jnp.float32, 'sgu_ln_g': _jnp.float32, 'sgu_ln_b': _jnp.float32, 'sgu_w_s': _jnp.float32, 'sgu_b_s': _jnp.float32, 'w_out': _jnp.float32, 'mix_post_g': _jnp.float32, 'ffn2_pre_g': _jnp.float32, 'ffn2_w_gate': _jnp.float32, 'ffn2_w_up': _jnp.float32, 'ffn2_w_down': _jnp.float32, 'ffn2_post_g': _jnp.float32}
MOMENT_SCALE = {'ffn1_pre_g': 6.314054e-01, 'ffn1_w_gate': 2.055821e-01, 'ffn1_w_up': 2.316024e-01, 'ffn1_w_down': 4.751644e-01, 'ffn1_post_g': 1.587600e+01, 'mix_pre_g': 9.129140e-01, 'w_in': 3.598785e-01, 'b_forget': 7.668932e-01, 'sgu_ln_g': 2.886826e-01, 'sgu_ln_b': 2.756536e-01, 'sgu_w_s': 2.794501e-01, 'sgu_b_s': 3.136055e-01, 'w_out': 1.387536e+00, 'mix_post_g': 6.457044e+01, 'ffn2_pre_g': 9.974419e-01, 'ffn2_w_gate': 2.168574e-01, 'ffn2_w_up': 4.067648e-01, 'ffn2_w_down': 8.218684e-01, 'ffn2_post_g': 1.599039e+01}


def _to_microbatches(a, axis):
    t = _jnp.moveaxis(a, axis, 0)
    t = t.reshape((N_MICROBATCH, t.shape[0] // N_MICROBATCH) + t.shape[1:])
    return _jnp.moveaxis(t, 1, axis + 1)


def setup_inputs(seed: int = 0) -> dict:
    inp = _fwd_setup_inputs(seed)
    key = _jax.random.fold_in(_jax.random.key(seed), 7919)
    shape, _ = _output_shape()
    out = dict(inp)
    out["loss_target"] = _jax.random.normal(_jax.random.fold_in(key, 0), shape, _jnp.float32)
    for i, name in enumerate(TWIN_WEIGHTS):
        w = inp[name].astype(_jnp.float32)
        if MOMENT_SCALE is None:
            s = _jnp.sqrt(_jnp.mean(_jnp.square(w)) + 1e-30)
        else:
            s = MOMENT_SCALE[name]
        km, kv = _jax.random.split(_jax.random.fold_in(key, i + 1))
        out[name] = w
        out["m_" + name] = s * _jax.random.normal(km, w.shape, _jnp.float32)
        out["v_" + name] = (s * s) * _jax.random.uniform(kv, w.shape, _jnp.float32, 0.5, 1.5)
    if N_MICROBATCH > 1:
        for name, axis in PER_EXAMPLE_BATCH_AXIS.items():
            out[name] = _to_microbatches(out[name], axis)
    return {'x': out['x'], 'ffn1_pre_g': out['ffn1_pre_g'], 'ffn1_w_gate': out['ffn1_w_gate'], 'ffn1_w_up': out['ffn1_w_up'], 'ffn1_w_down': out['ffn1_w_down'], 'ffn1_post_g': out['ffn1_post_g'], 'mix_pre_g': out['mix_pre_g'], 'w_in': out['w_in'], 'b_forget': out['b_forget'], 'sgu_ln_g': out['sgu_ln_g'], 'sgu_ln_b': out['sgu_ln_b'], 'sgu_w_s': out['sgu_w_s'], 'sgu_b_s': out['sgu_b_s'], 'w_out': out['w_out'], 'mix_post_g': out['mix_post_g'], 'ffn2_pre_g': out['ffn2_pre_g'], 'ffn2_w_gate': out['ffn2_w_gate'], 'ffn2_w_up': out['ffn2_w_up'], 'ffn2_w_down': out['ffn2_w_down'], 'ffn2_post_g': out['ffn2_post_g'], 'loss_target': out['loss_target'], 'm_ffn1_pre_g': out['m_ffn1_pre_g'], 'm_ffn1_w_gate': out['m_ffn1_w_gate'], 'm_ffn1_w_up': out['m_ffn1_w_up'], 'm_ffn1_w_down': out['m_ffn1_w_down'], 'm_ffn1_post_g': out['m_ffn1_post_g'], 'm_mix_pre_g': out['m_mix_pre_g'], 'm_w_in': out['m_w_in'], 'm_b_forget': out['m_b_forget'], 'm_sgu_ln_g': out['m_sgu_ln_g'], 'm_sgu_ln_b': out['m_sgu_ln_b'], 'm_sgu_w_s': out['m_sgu_w_s'], 'm_sgu_b_s': out['m_sgu_b_s'], 'm_w_out': out['m_w_out'], 'm_mix_post_g': out['m_mix_post_g'], 'm_ffn2_pre_g': out['m_ffn2_pre_g'], 'm_ffn2_w_gate': out['m_ffn2_w_gate'], 'm_ffn2_w_up': out['m_ffn2_w_up'], 'm_ffn2_w_down': out['m_ffn2_w_down'], 'm_ffn2_post_g': out['m_ffn2_post_g'], 'v_ffn1_pre_g': out['v_ffn1_pre_g'], 'v_ffn1_w_gate': out['v_ffn1_w_gate'], 'v_ffn1_w_up': out['v_ffn1_w_up'], 'v_ffn1_w_down': out['v_ffn1_w_down'], 'v_ffn1_post_g': out['v_ffn1_post_g'], 'v_mix_pre_g': out['v_mix_pre_g'], 'v_w_in': out['v_w_in'], 'v_b_forget': out['v_b_forget'], 'v_sgu_ln_g': out['v_sgu_ln_g'], 'v_sgu_ln_b': out['v_sgu_ln_b'], 'v_sgu_w_s': out['v_sgu_w_s'], 'v_sgu_b_s': out['v_sgu_b_s'], 'v_w_out': out['v_w_out'], 'v_mix_post_g': out['v_mix_post_g'], 'v_ffn2_pre_g': out['v_ffn2_pre_g'], 'v_ffn2_w_gate': out['v_ffn2_w_gate'], 'v_ffn2_w_up': out['v_ffn2_w_up'], 'v_ffn2_w_down': out['v_ffn2_w_down'], 'v_ffn2_post_g': out['v_ffn2_post_g']}


def _loss(weights, diff, rest, loss_target):
    with _jax.named_scope("forward"):
        args = {**rest, TWIN_DIFF_INPUT: diff, **{k: w.astype(_WEIGHT_DTYPES[k]) for k, w in weights.items()}}
        y = _forward(args)
    with _jax.named_scope("loss_head"):
        err = _jnp.square(y.astype(_jnp.float32) - loss_target)
        return 0.5 * _jnp.sum(_jnp.mean(err, axis=-1)) if err.ndim else 0.5 * err


def _adamw(w, g, m, v):
    m = ADAM_B1 * m + (1.0 - ADAM_B1) * g
    v = ADAM_B2 * v + (1.0 - ADAM_B2) * _jnp.square(g)
    m_hat = m / (1.0 - ADAM_B1 ** ADAM_STEP)
    v_hat = v / (1.0 - ADAM_B2 ** ADAM_STEP)
    delta = -ADAM_LR * (m_hat / (_jnp.sqrt(v_hat) + ADAM_EPS) + ADAM_WD * w)
    return delta, m, v


def reference(x, ffn1_pre_g, ffn1_w_gate, ffn1_w_up, ffn1_w_down, ffn1_post_g, mix_pre_g, w_in, b_forget, sgu_ln_g, sgu_ln_b, sgu_w_s, sgu_b_s, w_out, mix_post_g, ffn2_pre_g, ffn2_w_gate, ffn2_w_up, ffn2_w_down, ffn2_post_g, loss_target, m_ffn1_pre_g, m_ffn1_w_gate, m_ffn1_w_up, m_ffn1_w_down, m_ffn1_post_g, m_mix_pre_g, m_w_in, m_b_forget, m_sgu_ln_g, m_sgu_ln_b, m_sgu_w_s, m_sgu_b_s, m_w_out, m_mix_post_g, m_ffn2_pre_g, m_ffn2_w_gate, m_ffn2_w_up, m_ffn2_w_down, m_ffn2_post_g, v_ffn1_pre_g, v_ffn1_w_gate, v_ffn1_w_up, v_ffn1_w_down, v_ffn1_post_g, v_mix_pre_g, v_w_in, v_b_forget, v_sgu_ln_g, v_sgu_ln_b, v_sgu_w_s, v_sgu_b_s, v_w_out, v_mix_post_g, v_ffn2_pre_g, v_ffn2_w_gate, v_ffn2_w_up, v_ffn2_w_down, v_ffn2_post_g):
    given = dict(x=x, ffn1_pre_g=ffn1_pre_g, ffn1_w_gate=ffn1_w_gate, ffn1_w_up=ffn1_w_up, ffn1_w_down=ffn1_w_down, ffn1_post_g=ffn1_post_g, mix_pre_g=mix_pre_g, w_in=w_in, b_forget=b_forget, sgu_ln_g=sgu_ln_g, sgu_ln_b=sgu_ln_b, sgu_w_s=sgu_w_s, sgu_b_s=sgu_b_s, w_out=w_out, mix_post_g=mix_post_g, ffn2_pre_g=ffn2_pre_g, ffn2_w_gate=ffn2_w_gate, ffn2_w_up=ffn2_w_up, ffn2_w_down=ffn2_w_down, ffn2_post_g=ffn2_post_g, loss_target=loss_target, m_ffn1_pre_g=m_ffn1_pre_g, m_ffn1_w_gate=m_ffn1_w_gate, m_ffn1_w_up=m_ffn1_w_up, m_ffn1_w_down=m_ffn1_w_down, m_ffn1_post_g=m_ffn1_post_g, m_mix_pre_g=m_mix_pre_g, m_w_in=m_w_in, m_b_forget=m_b_forget, m_sgu_ln_g=m_sgu_ln_g, m_sgu_ln_b=m_sgu_ln_b, m_sgu_w_s=m_sgu_w_s, m_sgu_b_s=m_sgu_b_s, m_w_out=m_w_out, m_mix_post_g=m_mix_post_g, m_ffn2_pre_g=m_ffn2_pre_g, m_ffn2_w_gate=m_ffn2_w_gate, m_ffn2_w_up=m_ffn2_w_up, m_ffn2_w_down=m_ffn2_w_down, m_ffn2_post_g=m_ffn2_post_g, v_ffn1_pre_g=v_ffn1_pre_g, v_ffn1_w_gate=v_ffn1_w_gate, v_ffn1_w_up=v_ffn1_w_up, v_ffn1_w_down=v_ffn1_w_down, v_ffn1_post_g=v_ffn1_post_g, v_mix_pre_g=v_mix_pre_g, v_w_in=v_w_in, v_b_forget=v_b_forget, v_sgu_ln_g=v_sgu_ln_g, v_sgu_ln_b=v_sgu_ln_b, v_sgu_w_s=v_sgu_w_s, v_sgu_b_s=v_sgu_b_s, v_w_out=v_w_out, v_mix_post_g=v_mix_post_g, v_ffn2_pre_g=v_ffn2_pre_g, v_ffn2_w_gate=v_ffn2_w_gate, v_ffn2_w_up=v_ffn2_w_up, v_ffn2_w_down=v_ffn2_w_down, v_ffn2_post_g=v_ffn2_post_g)
    weights = {n: given[n] for n in TWIN_WEIGHTS}
    shared = {n: given[n] for n in SHARED_INPUTS}
    per_example = {n: given[n] for n in ['x']}
    grad_fn = _jax.value_and_grad(_loss, argnums=(0, 1))

    def one_microbatch(ex, loss_target):
        ex = dict(ex)
        diff = ex.pop(TWIN_DIFF_INPUT)
        return grad_fn(weights, diff, {**shared, **ex}, loss_target)

    if N_MICROBATCH == 1:
        loss, (grad_w, grad_x) = one_microbatch(per_example, given["loss_target"])
    else:
        def body(carry, xs):
            loss_sum, grad_sum = carry
            l_k, (gw_k, gx_k) = one_microbatch(xs[0], xs[1])
            with _jax.named_scope("update"):
                return (loss_sum + l_k, _jax.tree.map(_jnp.add, grad_sum, gw_k)), gx_k

        init = (_jnp.zeros((), _jnp.float32), _jax.tree.map(_jnp.zeros_like, weights))
        (loss, grad_w), grad_x = _jax.lax.scan(body, init, (per_example, given["loss_target"]))
    with _jax.named_scope("update"):
        delta_w, new_m, new_v = {}, {}, {}
        for n in TWIN_WEIGHTS:
            delta_w[n], new_m[n], new_v[n] = _adamw(weights[n], grad_w[n], given["m_" + n], given["v_" + n])
    return (loss, grad_x, *[grad_w[n] for n in TWIN_WEIGHTS], *[delta_w[n] for n in TWIN_WEIGHTS],
            *[new_m[n] for n in TWIN_WEIGHTS], *[new_v[n] for n in TWIN_WEIGHTS])
```

```python
import functools

import numpy as np
import jax
import jax.numpy as jnp
from jax import lax
from jax.experimental import pallas as pl
from jax.experimental.pallas import tpu as pltpu

F32 = jnp.float32
BF16 = jnp.bfloat16

RMS_EPS = 1e-6
LN_EPS = 1e-5
HEAD_DIM = 128
N_HEADS = 8
GROUP_DIM = 128
N_GROUPS = 8
SGU_LEN = 128
CHUNK = 64
N_DEV = 8
LANES = 128
VMEM_LIMIT = 56 * 1024 * 1024
NEG_BIG = -1e30

ADAM_LR = 0.001
ADAM_B1 = 0.9
ADAM_B2 = 0.999
ADAM_EPS = 1e-08
ADAM_WD = 0.01
ADAM_STEP = 10

MESH = pl.DeviceIdType.MESH
ANY = pl.BlockSpec(memory_space=pl.ANY)


def _blk(n, pref):
    return pref if (n >= pref and n % pref == 0) else n


def _mm(a, b):
    return jnp.dot(a, b, preferred_element_type=F32)


def _mm_nt(a, b):
    return lax.dot_general(a, b, (((1,), (1,)), ((), ())), preferred_element_type=F32)


def _mm_tn(a, b):
    return lax.dot_general(a, b, (((0,), (0,)), ((), ())), preferred_element_type=F32)


def _params(sem):
    return pltpu.CompilerParams(dimension_semantics=sem, vmem_limit_bytes=VMEM_LIMIT)


def _gelu(x):
    return 0.5 * x * (1.0 + lax.erf(x * np.float32(1.0 / np.sqrt(2.0))))


def _gelu_grad(x):
    cdf = 0.5 * (1.0 + lax.erf(x * np.float32(1.0 / np.sqrt(2.0))))
    return cdf + x * jnp.exp(-0.5 * x * x) * np.float32(1.0 / np.sqrt(2.0 * np.pi))


def _rms_scale(v):
    return lax.rsqrt(jnp.mean(v * v, axis=-1, keepdims=True) + RMS_EPS)


def _rms_bwd(dy, xhat, r, g):
    dxh = dy * g
    return r * (dxh - xhat * jnp.mean(dxh * xhat, axis=-1, keepdims=True))


def _ffn_fwd(x, g_pre, wg, wu, wd, g_post, name):
    T, D = x.shape
    ns, _, fs = wg.shape
    tm = _blk(T, 512)

    def body(x_ref, gpre_ref, wg_ref, wu_ref, wd_ref, gpost_ref, xo_ref, y_ref, g_ref, u_ref, h_scr, acc_scr):
        j = pl.program_id(1)

        @pl.when(j == 0)
        def _():
            xv = x_ref[...]
            h_scr[...] = (xv * _rms_scale(xv) * gpre_ref[...]).astype(BF16)
            acc_scr[...] = jnp.zeros_like(acc_scr)

        h = h_scr[...]
        gg = _mm(h, wg_ref[...])
        uu = _mm(h, wu_ref[...])
        a = gg * jax.nn.sigmoid(gg) * uu
        g_ref[...] = gg.astype(BF16)
        u_ref[...] = uu.astype(BF16)
        acc_scr[...] += _mm(a.astype(BF16), wd_ref[...])

        @pl.when(j == ns - 1)
        def _():
            y = acc_scr[...]
            y_ref[...] = y
            xo_ref[...] = x_ref[...] + 0.5 * (y * _rms_scale(y) * gpost_ref[...])

    row = pl.BlockSpec((tm, D), lambda i, j: (i, 0))
    vec = pl.BlockSpec((1, D), lambda i, j: (0, 0))
    return pl.pallas_call(
        body, name=name, grid=(T // tm, ns),
        in_specs=[row, vec,
                  pl.BlockSpec((None, D, fs), lambda i, j: (j, 0, 0)),
                  pl.BlockSpec((None, D, fs), lambda i, j: (j, 0, 0)),
                  pl.BlockSpec((None, fs, D), lambda i, j: (j, 0, 0)),
                  vec],
        out_specs=[row, row,
                   pl.BlockSpec((tm, fs), lambda i, j: (i, j)),
                   pl.BlockSpec((tm, fs), lambda i, j: (i, j))],
        out_shape=[jax.ShapeDtypeStruct((T, D), F32), jax.ShapeDtypeStruct((T, D), F32),
                   jax.ShapeDtypeStruct((T, ns * fs), BF16), jax.ShapeDtypeStruct((T, ns * fs), BF16)],
        scratch_shapes=[pltpu.VMEM((tm, D), BF16), pltpu.VMEM((tm, D), F32)],
        compiler_params=_params(("parallel", "arbitrary")),
    )(x, g_pre, wg, wu, wd, g_post)


def _ffn_bwd(dxo, x, y, gate, up, g_pre, wg, wu, wd, g_post, name):
    T, D = x.shape
    ns, _, fs = wg.shape
    tm = _blk(T, 512)
    n_i = T // tm

    def body(dxo_ref, x_ref, y_ref, g_ref, u_ref, gpre_ref, wg_ref, wu_ref, wd_ref, gpost_ref,
             dx_ref, hb_ref, dyb_ref, ab_ref, dgb_ref, dub_ref, dgpre_ref, dgpost_ref, dy_scr, acc_scr):
        j = pl.program_id(1)

        @pl.when(j == 0)
        def _():
            yv = y_ref[...]
            s = _rms_scale(yv)
            n = yv * s
            dn = 0.5 * dxo_ref[...]
            dgpost_ref[...] = jnp.sum(dn * n, axis=0, keepdims=True)
            dyv = _rms_bwd(dn, n, s, gpost_ref[...]).astype(BF16)
            dy_scr[...] = dyv
            dyb_ref[...] = dyv
            xv = x_ref[...]
            hb_ref[...] = (xv * _rms_scale(xv) * gpre_ref[...]).astype(BF16)
            acc_scr[...] = jnp.zeros_like(acc_scr)

        da = _mm_nt(dy_scr[...], wd_ref[...])
        gg = g_ref[...].astype(F32)
        uu = u_ref[...].astype(F32)
        sg = jax.nn.sigmoid(gg)
        silu = gg * sg
        dgate = (da * uu * (sg * (1.0 + gg * (1.0 - sg)))).astype(BF16)
        dup = (da * silu).astype(BF16)
        ab_ref[...] = (silu * uu).astype(BF16)
        dgb_ref[...] = dgate
        dub_ref[...] = dup
        acc_scr[...] += _mm_nt(dgate, wg_ref[...]) + _mm_nt(dup, wu_ref[...])

        @pl.when(j == ns - 1)
        def _():
            xv = x_ref[...]
            r = _rms_scale(xv)
            xhat = xv * r
            dh = acc_scr[...]
            dgpre_ref[...] = jnp.sum(dh * xhat, axis=0, keepdims=True)
            dx_ref[...] = _rms_bwd(dh, xhat, r, gpre_ref[...]) + dxo_ref[...]

    row = pl.BlockSpec((tm, D), lambda i, j: (i, 0))
    vec = pl.BlockSpec((1, D), lambda i, j: (0, 0))
    wide = pl.BlockSpec((tm, fs), lambda i, j: (i, j))
    part = pl.BlockSpec((None, 1, D), lambda i, j: (i, 0, 0))
    F = ns * fs
    return pl.pallas_call(
        body, name=name, grid=(n_i, ns),
        in_specs=[row, row, row, wide, wide, vec,
                  pl.BlockSpec((None, D, fs), lambda i, j: (j, 0, 0)),
                  pl.BlockSpec((None, D, fs), lambda i, j: (j, 0, 0)),
                  pl.BlockSpec((None, fs, D), lambda i, j: (j, 0, 0)),
                  vec],
        out_specs=[row, row, row, wide, wide, wide, part, part],
        out_shape=[jax.ShapeDtypeStruct((T, D), F32), jax.ShapeDtypeStruct((T, D), BF16),
                   jax.ShapeDtypeStruct((T, D), BF16), jax.ShapeDtypeStruct((T, F), BF16),
                   jax.ShapeDtypeStruct((T, F), BF16), jax.ShapeDtypeStruct((T, F), BF16),
                   jax.ShapeDtypeStruct((n_i, 1, D), F32), jax.ShapeDtypeStruct((n_i, 1, D), F32)],
        scratch_shapes=[pltpu.VMEM((tm, D), BF16), pltpu.VMEM((tm, D), F32)],
        compiler_params=_params(("parallel", "arbitrary")),
    )(dxo, x, y, gate, up, g_pre, wg, wu, wd, g_post)


def _wgrad(xm, ym, name, shard_cols=False):
    T, M = xm.shape
    _, N = ym.shape
    bm = _blk(M, 1024)
    bn = N // N_DEV if shard_cols else _blk(N, 512)
    tk = _blk(T, 1024)
    n_k = T // tk

    def body(x_ref, y_ref, o_ref, acc_scr):
        k = pl.program_id(2)

        @pl.when(k == 0)
        def _():
            acc_scr[...] = jnp.zeros_like(acc_scr)

        acc_scr[...] += _mm_tn(x_ref[...], y_ref[...])

        @pl.when(k == n_k - 1)
        def _():
            o_ref[...] = acc_scr[...].astype(BF16)

    if shard_cols:
        out_spec = pl.BlockSpec((None, bm, bn), lambda i, j, k: (j, i, 0))
        out_shape = jax.ShapeDtypeStruct((N // bn, M, bn), BF16)
    else:
        out_spec = pl.BlockSpec((bm, bn), lambda i, j, k: (i, j))
        out_shape = jax.ShapeDtypeStruct((M, N), BF16)
    return pl.pallas_call(
        body, name=name, grid=(M // bm, N // bn, n_k),
        in_specs=[pl.BlockSpec((tk, bm), lambda i, j, k: (k, i)),
                  pl.BlockSpec((tk, bn), lambda i, j, k: (k, j))],
        out_specs=out_spec, out_shape=out_shape,
        scratch_shapes=[pltpu.VMEM((bm, bn), F32)],
        compiler_params=_params(("parallel", "parallel", "arbitrary")),
    )(xm, ym)


def _mix_in_fwd(x1, g, w7, wf, name):
    T, D = x1.shape
    n_seg, _, W = w7.shape
    tm = _blk(T, 1024)

    def body(x_ref, g_ref, w_ref, wf_ref, z_ref, f_ref, hb_ref, h_scr):
        s = pl.program_id(1)

        @pl.when(s == 0)
        def _():
            xv = x_ref[...]
            h = (xv * _rms_scale(xv) * g_ref[...]).astype(BF16)
            h_scr[...] = h
            hb_ref[...] = h
            f_ref[...] = _mm(h, wf_ref[...])

        z_ref[...] = _mm(h_scr[...], w_ref[...]).astype(BF16)

    return pl.pallas_call(
        body, name=name, grid=(T // tm, n_seg),
        in_specs=[pl.BlockSpec((tm, D), lambda i, s: (i, 0)),
                  pl.BlockSpec((1, D), lambda i, s: (0, 0)),
                  pl.BlockSpec((None, D, W), lambda i, s: (s, 0, 0)),
                  pl.BlockSpec((D, LANES), lambda i, s: (0, 0))],
        out_specs=[pl.BlockSpec((None, tm, W), lambda i, s: (s, i, 0)),
                   pl.BlockSpec((tm, LANES), lambda i, s: (i, 0)),
                   pl.BlockSpec((tm, D), lambda i, s: (i, 0))],
        out_shape=[jax.ShapeDtypeStruct((n_seg, T, W), BF16), jax.ShapeDtypeStruct((T, LANES), F32),
                   jax.ShapeDtypeStruct((T, D), BF16)],
        scratch_shapes=[pltpu.VMEM((tm, D), BF16)],
        compiler_params=_params(("parallel", "arbitrary")),
    )(x1, g, w7, wf)


def _mix_in_bwd(dx2, x1, g, segs, dfb, w7, wf, name):
    T, D = x1.shape
    n_seg, _, W = w7.shape
    tm = _blk(T, 512)
    n_i = T // tm

    def body(*refs):
        dx2_ref, x_ref, g_ref = refs[:3]
        seg_refs = refs[3:3 + n_seg]
        df_ref, w_ref, wf_ref, dx1_ref, dg_ref, acc_scr = refs[3 + n_seg:]
        s = pl.program_id(1)

        @pl.when(s == 0)
        def _():
            acc_scr[...] = _mm_nt(df_ref[...], wf_ref[...])

        for q in range(n_seg):
            @pl.when(s == q)
            def _(q=q):
                acc_scr[...] += _mm_nt(seg_refs[q][...], w_ref[...])

        @pl.when(s == n_seg - 1)
        def _():
            xv = x_ref[...]
            r = _rms_scale(xv)
            xhat = xv * r
            dh = acc_scr[...]
            dg_ref[...] = jnp.sum(dh * xhat, axis=0, keepdims=True)
            dx1_ref[...] = _rms_bwd(dh, xhat, r, g_ref[...]) + dx2_ref[...]

    row = pl.BlockSpec((tm, D), lambda i, s: (i, 0))
    seg_specs = []
    seg_args = []
    for arr, idx in segs:
        if idx is None:
            seg_specs.append(pl.BlockSpec((tm, W), lambda i, s: (i, 0)))
        else:
            seg_specs.append(pl.BlockSpec((None, tm, W), lambda i, s, idx=idx: (idx, i, 0)))
        seg_args.append(arr)
    return pl.pallas_call(
        body, name=name, grid=(n_i, n_seg),
        in_specs=[row, row, pl.BlockSpec((1, D), lambda i, s: (0, 0))] + seg_specs + [
            pl.BlockSpec((tm, LANES), lambda i, s: (i, 0)),
            pl.BlockSpec((None, D, W), lambda i, s: (s, 0, 0)),
            pl.BlockSpec((D, LANES), lambda i, s: (0, 0))],
        out_specs=[row, pl.BlockSpec((None, 1, D), lambda i, s: (i, 0, 0))],
        out_shape=[jax.ShapeDtypeStruct((T, D), F32), jax.ShapeDtypeStruct((n_i, 1, D), F32)],
        scratch_shapes=[pltpu.VMEM((tm, D), F32)],
        compiler_params=_params(("parallel", "arbitrary")),
    )(dx2, x1, g, *seg_args, dfb, w7, wf)


def _forget_cumsum(f, b_pad, name):
    T, L = f.shape
    tb = _blk(T, 256)

    def body(f_ref, b_ref, c_ref, carry):
        @pl.when(pl.program_id(0) == 0)
        def _():
            carry[...] = jnp.zeros_like(carry)

        lf = jax.nn.log_sigmoid(f_ref[...] + b_ref[...])
        rows = lax.broadcasted_iota(jnp.int32, (tb, tb), 0)
        cols = lax.broadcasted_iota(jnp.int32, (tb, tb), 1)
        tri = (cols <= rows).astype(F32)
        c = jnp.dot(tri, lf, preferred_element_type=F32, precision=lax.Precision.HIGHEST) + carry[...]
        c_ref[...] = c
        carry[...] = c[tb - 1:tb, :]

    return pl.pallas_call(
        body, name=name, grid=(T // tb,),
        in_specs=[pl.BlockSpec((tb, L), lambda i: (i, 0)), pl.BlockSpec((1, L), lambda i: (0, 0))],
        out_specs=pl.BlockSpec((tb, L), lambda i: (i, 0)),
        out_shape=jax.ShapeDtypeStruct((T, L), F32),
        scratch_shapes=[pltpu.VMEM((1, L), F32)],
        compiler_params=_params(("arbitrary",)),
    )(f, b_pad)


def _forget_bwd(dc, f, b_pad, name):
    T, L = f.shape
    tb = _blk(T, 256)
    nb = T // tb

    def body(dc_ref, f_ref, b_ref, df_ref, db_ref, carry):
        @pl.when(pl.program_id(0) == 0)
        def _():
            carry[...] = jnp.zeros_like(carry)
            db_ref[...] = jnp.zeros_like(db_ref)

        rows = lax.broadcasted_iota(jnp.int32, (tb, tb), 0)
        cols = lax.broadcasted_iota(jnp.int32, (tb, tb), 1)
        tri = (cols >= rows).astype(F32)
        r = jnp.dot(tri, dc_ref[...], preferred_element_type=F32, precision=lax.Precision.HIGHEST) + carry[...]
        carry[...] = r[0:1, :]
        df = r * (1.0 - jax.nn.sigmoid(f_ref[...] + b_ref[...]))
        df_ref[...] = df.astype(BF16)
        db_ref[...] += jnp.sum(df, axis=0, keepdims=True)

    rev = pl.BlockSpec((tb, L), lambda i: (nb - 1 - i, 0))
    one = pl.BlockSpec((1, L), lambda i: (0, 0))
    return pl.pallas_call(
        body, name=name, grid=(nb,),
        in_specs=[rev, rev, one], out_specs=[rev, one],
        out_shape=[jax.ShapeDtypeStruct((T, L), BF16), jax.ShapeDtypeStruct((1, L), F32)],
        scratch_shapes=[pltpu.VMEM((1, L), F32)],
        compiler_params=_params(("arbitrary",)),
    )(dc, f, b_pad)


def _attn_fwd(z7, c_row, name):
    _, T, W = z7.shape
    H = W // HEAD_DIM
    ta = _blk(T, 512)
    nq = T // ta
    scale = np.float32(1.0 / np.sqrt(HEAD_DIM))

    def body(q_ref, k_ref, v_ref, crow_ref, o_ref, lse_ref, m_scr, l_scr, acc_scr):
        i = pl.program_id(1)
        j = pl.program_id(2)

        @pl.when(j == 0)
        def _():
            m_scr[...] = jnp.full_like(m_scr, NEG_BIG)
            l_scr[...] = jnp.zeros_like(l_scr)
            acc_scr[...] = jnp.zeros_like(acc_scr)

        def step(diagonal):
            s = _mm_nt(q_ref[...], k_ref[...]) * scale - crow_ref[...]
            if diagonal:
                rows = lax.broadcasted_iota(jnp.int32, (ta, ta), 0)
                cols = lax.broadcasted_iota(jnp.int32, (ta, ta), 1)
                s = jnp.where(cols <= rows, s, NEG_BIG)
            m_prev = m_scr[...]
            m_new = jnp.maximum(m_prev, jnp.max(s, axis=-1, keepdims=True))
            alpha = jnp.exp(m_prev - m_new)
            p = jnp.exp(s - m_new)
            l_scr[...] = alpha * l_scr[...] + jnp.sum(p, axis=-1, keepdims=True)
            acc_scr[...] = alpha * acc_scr[...] + _mm(p.astype(BF16), v_ref[...])
            m_scr[...] = m_new

        @pl.when(j < i)
        def _():
            step(False)

        @pl.when(j == i)
        def _():
            step(True)
            l = l_scr[...]
            o_ref[...] = acc_scr[...] / l
            lse_ref[...] = m_scr[...] + jnp.log(l)

    return pl.pallas_call(
        body, name=name, grid=(H, nq, nq),
        in_specs=[pl.BlockSpec((None, ta, HEAD_DIM), lambda h, i, j: (0, i, h)),
                  pl.BlockSpec((None, ta, HEAD_DIM), lambda h, i, j: (1, jnp.minimum(i, j), h)),
                  pl.BlockSpec((None, ta, HEAD_DIM), lambda h, i, j: (2, jnp.minimum(i, j), h)),
                  pl.BlockSpec((None, 1, ta), lambda h, i, j: (h, 0, jnp.minimum(i, j)))],
        out_specs=[pl.BlockSpec((ta, HEAD_DIM), lambda h, i, j: (i, h)),
                   pl.BlockSpec((None, ta, 1), lambda h, i, j: (h, i, 0))],
        out_shape=[jax.ShapeDtypeStruct((T, W), F32), jax.ShapeDtypeStruct((H, T, 1), F32)],
        scratch_shapes=[pltpu.VMEM((ta, 1), F32), pltpu.VMEM((ta, 1), F32), pltpu.VMEM((ta, HEAD_DIM), F32)],
        compiler_params=_params(("parallel", "parallel", "arbitrary")),
    )(z7, z7, z7, c_row)


def _attn_bwd_kv(z7, dob, c_col, lse_row, d_row, name):
    _, T, W = z7.shape
    H = W // HEAD_DIM
    ta = _blk(T, 512)
    nq = T // ta
    scale = np.float32(1.0 / np.sqrt(HEAD_DIM))

    def body(k_ref, v_ref, q_ref, do_ref, ccol_ref, lse_ref, d_ref, dk_ref, dv_ref, dc_ref, dk_scr, dv_scr, dc_scr):
        j = pl.program_id(1)
        i = pl.program_id(2)

        @pl.when(i == 0)
        def _():
            dk_scr[...] = jnp.zeros_like(dk_scr)
            dv_scr[...] = jnp.zeros_like(dv_scr)
            dc_scr[...] = jnp.zeros_like(dc_scr)

        def step(diagonal):
            q = q_ref[...]
            do = do_ref[...]
            st = _mm_nt(k_ref[...], q) * scale - ccol_ref[...] - lse_ref[...]
            if diagonal:
                rows = lax.broadcasted_iota(jnp.int32, (ta, ta), 0)
                cols = lax.broadcasted_iota(jnp.int32, (ta, ta), 1)
                st = jnp.where(rows <= cols, st, NEG_BIG)
            pt = jnp.exp(st)
            dv_scr[...] += _mm(pt.astype(BF16), do)
            dst = pt * (_mm_nt(v_ref[...], do) - d_ref[...])
            dk_scr[...] += _mm(dst.astype(BF16), q)
            dc_scr[...] += jnp.sum(dst, axis=-1, keepdims=True)

        @pl.when(i > j)
        def _():
            step(False)

        @pl.when(i == j)
        def _():
            step(True)

        @pl.when(i == nq - 1)
        def _():
            dk_ref[...] = (dk_scr[...] * scale).astype(BF16)
            dv_ref[...] = dv_scr[...].astype(BF16)
            dc_ref[...] = -dc_scr[...]

    return pl.pallas_call(
        body, name=name, grid=(H, nq, nq),
        in_specs=[pl.BlockSpec((None, ta, HEAD_DIM), lambda h, j, i: (1, j, h)),
                  pl.BlockSpec((None, ta, HEAD_DIM), lambda h, j, i: (2, j, h)),
                  pl.BlockSpec((None, ta, HEAD_DIM), lambda h, j, i: (0, jnp.maximum(i, j), h)),
                  pl.BlockSpec((ta, HEAD_DIM), lambda h, j, i: (jnp.maximum(i, j), h)),
                  pl.BlockSpec((None, ta, 1), lambda h, j, i: (h, j, 0)),
                  pl.BlockSpec((None, 1, ta), lambda h, j, i: (h, 0, jnp.maximum(i, j))),
                  pl.BlockSpec((None, 1, ta), lambda h, j, i: (h, 0, jnp.maximum(i, j)))],
        out_specs=[pl.BlockSpec((ta, HEAD_DIM), lambda h, j, i: (j, h)),
                   pl.BlockSpec((ta, HEAD_DIM), lambda h, j, i: (j, h)),
                   pl.BlockSpec((None, ta, 1), lambda h, j, i: (h, j, 0))],
        out_shape=[jax.ShapeDtypeStruct((T, W), BF16), jax.ShapeDtypeStruct((T, W), BF16),
                   jax.ShapeDtypeStruct((H, T, 1), F32)],
        scratch_shapes=[pltpu.VMEM((ta, HEAD_DIM), F32), pltpu.VMEM((ta, HEAD_DIM), F32), pltpu.VMEM((ta, 1), F32)],
        compiler_params=_params(("parallel", "parallel", "arbitrary")),
    )(z7, z7, z7, dob, c_col, lse_row, d_row)


def _attn_bwd_q(z7, dob, c_row, lse_col, d_col, name):
    _, T, W = z7.shape
    H = W // HEAD_DIM
    ta = _blk(T, 512)
    nq = T // ta
    scale = np.float32(1.0 / np.sqrt(HEAD_DIM))

    def body(q_ref, k_ref, v_ref, do_ref, crow_ref, lse_ref, d_ref, dq_ref, dc_ref, dq_scr, dc_scr):
        i = pl.program_id(1)
        j = pl.program_id(2)

        @pl.when(j == 0)
        def _():
            dq_scr[...] = jnp.zeros_like(dq_scr)
            dc_scr[...] = jnp.zeros_like(dc_scr)

        def step(diagonal):
            k = k_ref[...]
            do = do_ref[...]
            s = _mm_nt(q_ref[...], k) * scale - crow_ref[...] - lse_ref[...]
            if diagonal:
                rows = lax.broadcasted_iota(jnp.int32, (ta, ta), 0)
                cols = lax.broadcasted_iota(jnp.int32, (ta, ta), 1)
                s = jnp.where(cols <= rows, s, NEG_BIG)
            p = jnp.exp(s)
            ds = p * (_mm_nt(do, v_ref[...]) - d_ref[...])
            dq_scr[...] += _mm(ds.astype(BF16), k)
            dc_scr[...] += jnp.sum(ds, axis=-1, keepdims=True)

        @pl.when(j < i)
        def _():
            step(False)

        @pl.when(j == i)
        def _():
            step(True)
            dq_ref[...] = (dq_scr[...] * scale).astype(BF16)
            dc_ref[...] = dc_scr[...]

    return pl.pallas_call(
        body, name=name, grid=(H, nq, nq),
        in_specs=[pl.BlockSpec((None, ta, HEAD_DIM), lambda h, i, j: (0, i, h)),
                  pl.BlockSpec((None, ta, HEAD_DIM), lambda h, i, j: (1, jnp.minimum(i, j), h)),
                  pl.BlockSpec((None, ta, HEAD_DIM), lambda h, i, j: (2, jnp.minimum(i, j), h)),
                  pl.BlockSpec((ta, HEAD_DIM), lambda h, i, j: (i, h)),
                  pl.BlockSpec((None, 1, ta), lambda h, i, j: (h, 0, jnp.minimum(i, j))),
                  pl.BlockSpec((None, ta, 1), lambda h, i, j: (h, i, 0)),
                  pl.BlockSpec((None, ta, 1), lambda h, i, j: (h, i, 0))],
        out_specs=[pl.BlockSpec((ta, HEAD_DIM), lambda h, i, j: (i, h)),
                   pl.BlockSpec((None, ta, 1), lambda h, i, j: (h, i, 0))],
        out_shape=[jax.ShapeDtypeStruct((T, W), BF16), jax.ShapeDtypeStruct((H, T, 1), F32)],
        scratch_shapes=[pltpu.VMEM((ta, HEAD_DIM), F32), pltpu.VMEM((ta, 1), F32)],
        compiler_params=_params(("parallel", "parallel", "arbitrary")),
    )(z7, z7, z7, dob, c_row, lse_col, d_col)


def _chunk_causal_mask():
    rows = lax.broadcasted_iota(jnp.int32, (SGU_LEN, SGU_LEN), 0)
    cols = lax.broadcasted_iota(jnp.int32, (SGU_LEN, SGU_LEN), 1)
    return (cols // CHUNK) <= (rows // CHUNK)


def _sgu_norm_mix(sv, lng_ref, lnb_ref, ws_ref, bs_ref, vn_scr, mixed_scr, vhat_scr=None):
    tm = sv.shape[0]
    vs = _gelu(sv)
    mask = _chunk_causal_mask()
    rstds = []
    for g in range(N_GROUPS):
        lanes = slice(g * GROUP_DIM, (g + 1) * GROUP_DIM)
        blk = vs[:, lanes]
        cen = blk - jnp.mean(blk, axis=-1, keepdims=True)
        rstd = lax.rsqrt(jnp.mean(cen * cen, axis=-1, keepdims=True) + LN_EPS)
        vhat = cen * rstd
        rstds.append(rstd)
        if vhat_scr is not None:
            vhat_scr[:, lanes] = vhat
        vn_scr[:, lanes] = (vhat * lng_ref[:, lanes] + lnb_ref[:, lanes]).astype(BF16)
        wm = jnp.where(mask, ws_ref[g], 0.0).astype(BF16)
        for w in range(tm // SGU_LEN):
            rows = slice(w * SGU_LEN, (w + 1) * SGU_LEN)
            mixed_scr[rows, lanes] = _mm(wm, vn_scr[rows, lanes]) + bs_ref[g]
    return rstds


def _mix_out_fwd(z7, o_a, x1, lng, lnb, ws, bs, w_out, g_post, name):
    _, T, W = z7.shape
    D = x1.shape[1]
    tm = _blk(T, 256)

    def body(u_ref, sv_ref, ga_ref, gb_ref, oa_ref, x1_ref, lng_ref, lnb_ref, ws_ref, bs_ref, wo_ref, gp_ref,
             x2_ref, p_ref, mb_ref, vn_scr, mixed_scr):
        _sgu_norm_mix(sv_ref[...].astype(F32), lng_ref, lnb_ref, ws_ref, bs_ref, vn_scr, mixed_scr)
        o_b = _gelu(u_ref[...].astype(F32)) * mixed_scr[...]
        merged = (jax.nn.sigmoid(ga_ref[...].astype(F32)) * oa_ref[...]
                  + jax.nn.sigmoid(gb_ref[...].astype(F32)) * o_b).astype(BF16)
        mb_ref[...] = merged
        p = _mm(merged, wo_ref[...])
        p_ref[...] = p
        x2_ref[...] = x1_ref[...] + p * _rms_scale(p) * gp_ref[...]

    def seg(idx):
        return pl.BlockSpec((None, tm, W), lambda i, idx=idx: (idx, i, 0))

    row = pl.BlockSpec((tm, D), lambda i: (i, 0))
    vec = pl.BlockSpec((1, D), lambda i: (0, 0))
    return pl.pallas_call(
        body, name=name, grid=(T // tm,),
        in_specs=[seg(3), seg(4), seg(5), seg(6), row, row, vec, vec,
                  pl.BlockSpec((N_GROUPS, SGU_LEN, SGU_LEN), lambda i: (0, 0, 0)),
                  pl.BlockSpec((N_GROUPS, SGU_LEN, 1), lambda i: (0, 0, 0)),
                  pl.BlockSpec((D, D), lambda i: (0, 0)), vec],
        out_specs=[row, row, row],
        out_shape=[jax.ShapeDtypeStruct((T, D), F32), jax.ShapeDtypeStruct((T, D), F32),
                   jax.ShapeDtypeStruct((T, D), BF16)],
        scratch_shapes=[pltpu.VMEM((tm, W), BF16), pltpu.VMEM((tm, W), F32)],
        compiler_params=_params(("parallel",)),
    )(z7, z7, z7, z7, o_a, x1, lng, lnb, ws, bs, w_out, g_post)


def _mix_out_bwd(dx2, p, z7, o_a, lng, lnb, ws, bs, w_out, g_post, name):
    _, T, W = z7.shape
    D = dx2.shape[1]
    tm = _blk(T, 256)
    n_w = tm // SGU_LEN

    def body(dx2_ref, p_ref, u_ref, sv_ref, ga_ref, gb_ref, oa_ref, lng_ref, lnb_ref, ws_ref, bs_ref, wo_ref, gp_ref,
             dpb_ref, dob_ref, dvec_ref, dz_ref, dgp_ref, dlng_ref, dlnb_ref, dws_ref, dbs_ref,
             vn_scr, mixed_scr, vhat_scr, dmix_scr, dvn_scr):
        @pl.when(pl.program_id(0) == 0)
        def _():
            dgp_ref[...] = jnp.zeros_like(dgp_ref)
            dlng_ref[...] = jnp.zeros_like(dlng_ref)
            dlnb_ref[...] = jnp.zeros_like(dlnb_ref)
            dws_ref[...] = jnp.zeros_like(dws_ref)
            dbs_ref[...] = jnp.zeros_like(dbs_ref)

        pv = p_ref[...]
        s = _rms_scale(pv)
        n = pv * s
        dn = dx2_ref[...]
        dgp_ref[...] += jnp.sum(dn * n, axis=0, keepdims=True)
        dpb = _rms_bwd(dn, n, s, gp_ref[...]).astype(BF16)
        dpb_ref[...] = dpb
        dmerged = _mm_nt(dpb, wo_ref[...])

        sv = sv_ref[...].astype(F32)
        rstds = _sgu_norm_mix(sv, lng_ref, lnb_ref, ws_ref, bs_ref, vn_scr, mixed_scr, vhat_scr)
        u_pre = u_ref[...].astype(F32)
        u = _gelu(u_pre)
        mixed = mixed_scr[...]
        sa = jax.nn.sigmoid(ga_ref[...].astype(F32))
        sb = jax.nn.sigmoid(gb_ref[...].astype(F32))
        oa = oa_ref[...]
        do_a = (dmerged * sa).astype(BF16)
        dob_ref[...] = do_a
        prod = do_a.astype(F32) * oa
        for h in range(N_HEADS):
            dvec_ref[h] = jnp.sum(prod[:, h * HEAD_DIM:(h + 1) * HEAD_DIM], axis=-1, keepdims=True)
        dz_ref[2] = (dmerged * oa * (sa * (1.0 - sa))).astype(BF16)
        dz_ref[3] = (dmerged * (u * mixed) * (sb * (1.0 - sb))).astype(BF16)
        do_b = dmerged * sb
        dz_ref[0] = (do_b * mixed * _gelu_grad(u_pre)).astype(BF16)
        dmix_scr[...] = do_b * u

        mask = _chunk_causal_mask()
        for g in range(N_GROUPS):
            lanes = slice(g * GROUP_DIM, (g + 1) * GROUP_DIM)
            wm = jnp.where(mask, ws_ref[g], 0.0).astype(BF16)
            dws = jnp.zeros((SGU_LEN, SGU_LEN), F32)
            dbs = jnp.zeros((SGU_LEN, 1), F32)
            for w in range(n_w):
                rows = slice(w * SGU_LEN, (w + 1) * SGU_LEN)
                dmix = dmix_scr[rows, lanes]
                dmix_b = dmix.astype(BF16)
                dvn_scr[rows, lanes] = _mm_tn(wm, dmix_b)
                dws = dws + _mm_nt(dmix_b, vn_scr[rows, lanes])
                dbs = dbs + jnp.sum(dmix, axis=-1, keepdims=True)
            dws_ref[g] += jnp.where(mask, dws, 0.0)
            dbs_ref[g] += dbs
            dvn = dvn_scr[:, lanes]
            vhat = vhat_scr[:, lanes]
            dlng_ref[:, lanes] += jnp.sum(dvn * vhat, axis=0, keepdims=True)
            dlnb_ref[:, lanes] += jnp.sum(dvn, axis=0, keepdims=True)
            dvh = dvn * lng_ref[:, lanes]
            dvs = rstds[g] * (dvh - jnp.mean(dvh, axis=-1, keepdims=True)
                              - vhat * jnp.mean(dvh * vhat, axis=-1, keepdims=True))
            dvn_scr[:, lanes] = dvs
        dz_ref[1] = (dvn_scr[...] * _gelu_grad(sv)).astype(BF16)

    def seg(idx):
        return pl.BlockSpec((None, tm, W), lambda i, idx=idx: (idx, i, 0))

    row = pl.BlockSpec((tm, D), lambda i: (i, 0))
    vec = pl.BlockSpec((1, D), lambda i: (0, 0))
    ws_spec = pl.BlockSpec((N_GROUPS, SGU_LEN, SGU_LEN), lambda i: (0, 0, 0))
    bs_spec = pl.BlockSpec((N_GROUPS, SGU_LEN, 1), lambda i: (0, 0, 0))
    return pl.pallas_call(
        body, name=name, grid=(T // tm,),
        in_specs=[row, row, seg(3), seg(4), seg(5), seg(6), row, vec, vec, ws_spec, bs_spec,
                  pl.BlockSpec((D, D), lambda i: (0, 0)), vec],
        out_specs=[row, row, pl.BlockSpec((N_HEADS, tm, 1), lambda i: (0, i, 0)),
                   pl.BlockSpec((4, tm, W), lambda i: (0, i, 0)), vec, vec, vec, ws_spec, bs_spec],
        out_shape=[jax.ShapeDtypeStruct((T, D), BF16), jax.ShapeDtypeStruct((T, W), BF16),
                   jax.ShapeDtypeStruct((N_HEADS, T, 1), F32), jax.ShapeDtypeStruct((4, T, W), BF16),
                   jax.ShapeDtypeStruct((1, D), F32), jax.ShapeDtypeStruct((1, D), F32),
                   jax.ShapeDtypeStruct((1, D), F32),
                   jax.ShapeDtypeStruct((N_GROUPS, SGU_LEN, SGU_LEN), F32),
                   jax.ShapeDtypeStruct((N_GROUPS, SGU_LEN, 1), F32)],
        scratch_shapes=[pltpu.VMEM((tm, W), BF16), pltpu.VMEM((tm, W), F32), pltpu.VMEM((tm, W), F32),
                        pltpu.VMEM((tm, W), F32), pltpu.VMEM((tm, W), F32)],
        compiler_params=_params(("arbitrary",)),
    )(dx2, p, z7, z7, z7, z7, o_a, lng, lnb, ws, bs, w_out, g_post)


def _loss_head(y, target, name):
    T, D = y.shape
    tm = _blk(T, 1024)
    n_i = T // tm

    def body(y_ref, t_ref, dy_ref, loss_ref, acc_scr):
        i = pl.program_id(0)

        @pl.when(i == 0)
        def _():
            acc_scr[...] = jnp.zeros_like(acc_scr)

        e = y_ref[...] - t_ref[...]
        dy_ref[...] = e * np.float32(1.0 / D)
        acc_scr[...] += jnp.sum(e * e, axis=0, keepdims=True)

        @pl.when(i == n_i - 1)
        def _():
            total = jnp.sum(acc_scr[...], axis=-1, keepdims=True) * np.float32(0.5 / D)
            loss_ref[...] = jnp.broadcast_to(total, loss_ref.shape)

    row = pl.BlockSpec((tm, D), lambda i: (i, 0))
    return pl.pallas_call(
        body, name=name, grid=(n_i,),
        in_specs=[row, row],
        out_specs=[row, pl.BlockSpec((1, LANES), lambda i: (0, 0))],
        out_shape=[jax.ShapeDtypeStruct((T, D), F32), jax.ShapeDtypeStruct((1, LANES), F32)],
        scratch_shapes=[pltpu.VMEM((1, D), F32)],
        compiler_params=_params(("arbitrary",)),
    )(y, target)


def _adamw_math(w, g, m, v):
    m_new = ADAM_B1 * m + (1.0 - ADAM_B1) * g
    v_new = ADAM_B2 * v + (1.0 - ADAM_B2) * (g * g)
    m_hat = m_new / np.float32(1.0 - ADAM_B1 ** ADAM_STEP)
    v_hat = v_new / np.float32(1.0 - ADAM_B2 ** ADAM_STEP)
    delta = -ADAM_LR * (m_hat / (jnp.sqrt(v_hat) + ADAM_EPS) + ADAM_WD * w)
    return delta, m_new, v_new


def _sum_adamw(parts, w, m, v, name):
    n, R, C = parts.shape
    tr = _blk(R, 128)

    def body(p_ref, w_ref, m_ref, v_ref, g_ref, d_ref, mo_ref, vo_ref):
        g = p_ref[0].astype(F32)
        for s in range(1, n):
            g = g + p_ref[s].astype(F32)
        delta, m_new, v_new = _adamw_math(w_ref[...], g, m_ref[...], v_ref[...])
        g_ref[...] = g
        d_ref[...] = delta
        mo_ref[...] = m_new
        vo_ref[...] = v_new

    row = pl.BlockSpec((tr, C), lambda i: (i, 0))
    shp = jax.ShapeDtypeStruct((R, C), F32)
    return pl.pallas_call(
        body, name=name, grid=(R // tr,),
        in_specs=[pl.BlockSpec((n, tr, C), lambda i: (0, i, 0)), row, row, row],
        out_specs=[row, row, row, row], out_shape=[shp, shp, shp, shp],
        compiler_params=_params(("parallel",)),
    )(parts, w, m, v)


def _adamw(g, w, m, v, name):
    R, C = g.shape
    tr = _blk(R, 128)

    def body(g_ref, w_ref, m_ref, v_ref, d_ref, mo_ref, vo_ref):
        delta, m_new, v_new = _adamw_math(w_ref[...], g_ref[...], m_ref[...], v_ref[...])
        d_ref[...] = delta
        mo_ref[...] = m_new
        vo_ref[...] = v_new

    row = pl.BlockSpec((tr, C), lambda i: (i, 0))
    shp = jax.ShapeDtypeStruct((R, C), F32)
    return pl.pallas_call(
        body, name=name, grid=(R // tr,),
        in_specs=[row, row, row, row], out_specs=[row, row, row], out_shape=[shp, shp, shp],
        compiler_params=_params(("parallel",)),
    )(g, w, m, v)


def _position():
    return lax.axis_index("x"), lax.axis_index("y"), lax.axis_index("c")


def _slot(px, py, pc):
    return 4 * px + 2 * py + pc


def _all_gather(shards, name):
    n = len(shards)

    def body(*refs):
        ins, outs = refs[:n], refs[n:2 * n]
        send_sems, recv_sems, local_sems = refs[2 * n:]
        x, y, c = _position()
        me, sibling = (x, y, c), (x, y, 1 - c)
        chips = [(1 - x, y), (x, 1 - y), (1 - x, 1 - y)]

        def copy(a, k, block, to, src=None):
            dst = outs[a].at[_slot(*block)]
            return pltpu.make_async_remote_copy(
                src_ref=dst if src is None else src, dst_ref=dst,
                send_sem=send_sems.at[a, k], recv_sem=recv_sems.at[a, k],
                device_id=to, device_id_type=MESH)

        mine = [pltpu.make_async_copy(ins[a], outs[a].at[_slot(*me)], local_sems.at[a]) for a in range(n)]
        for cp in mine:
            cp.start()
        first = []
        for a in range(n):
            first.append(copy(a, 0, me, sibling, src=ins[a]))
            first += [copy(a, 1 + j, me, (*chip, c), src=ins[a]) for j, chip in enumerate(chips)]
        for cp in first:
            cp.start()
        passed = []
        for j, chip in enumerate(chips):
            for a in range(n):
                copy(a, 1 + j, (*chip, c), me).wait_recv()
                fwd = copy(a, 4 + j, (*chip, c), sibling)
                fwd.start()
                passed.append(fwd)
        for a in range(n):
            copy(a, 0, sibling, me).wait_recv()
            for j, chip in enumerate(chips):
                copy(a, 4 + j, (*chip, 1 - c), me).wait_recv()
        for cp in first + passed:
            cp.wait_send()
        for cp in mine:
            cp.wait()

    return pl.pallas_call(
        body, name=name,
        in_specs=[ANY] * n, out_specs=[ANY] * n,
        out_shape=[jax.ShapeDtypeStruct((N_DEV,) + s.shape, s.dtype) for s in shards],
        scratch_shapes=[pltpu.SemaphoreType.DMA((n, 7)), pltpu.SemaphoreType.DMA((n, 7)),
                        pltpu.SemaphoreType.DMA((n,))],
    )(*shards)


def _peer(x, y, c, k):
    return (1 - x if k & 4 else x, 1 - y if k & 2 else y, 1 - c if k & 1 else c)


def _exchange(parts, name):
    n = len(parts)

    def body(*refs):
        ins, outs = refs[:n], refs[n:2 * n]
        send_sems, recv_sems, local_sems = refs[2 * n:]
        x, y, c = _position()
        me = _slot(x, y, c)
        mine = [pltpu.make_async_copy(ins[a].at[me], outs[a].at[me], local_sems.at[a]) for a in range(n)]
        for cp in mine:
            cp.start()
        sends = []
        for k in range(1, N_DEV):
            to = _peer(x, y, c, k)
            for a in range(n):
                cp = pltpu.make_async_remote_copy(
                    src_ref=ins[a].at[_slot(*to)], dst_ref=outs[a].at[me],
                    send_sem=send_sems.at[a, k - 1], recv_sem=recv_sems.at[a, k - 1],
                    device_id=to, device_id_type=MESH)
                cp.start()
                sends.append(cp)
        for k in range(1, N_DEV):
            frm = _peer(x, y, c, k)
            for a in range(n):
                pltpu.make_async_remote_copy(
                    src_ref=ins[a].at[_slot(*frm)], dst_ref=outs[a].at[_slot(*frm)],
                    send_sem=send_sems.at[a, k - 1], recv_sem=recv_sems.at[a, k - 1],
                    device_id=frm, device_id_type=MESH).wait_recv()
        for cp in sends:
            cp.wait_send()
        for cp in mine:
            cp.wait()

    return pl.pallas_call(
        body, name=name,
        in_specs=[ANY] * n, out_specs=[ANY] * n,
        out_shape=[jax.ShapeDtypeStruct(p.shape, p.dtype) for p in parts],
        scratch_shapes=[pltpu.SemaphoreType.DMA((n, 7)), pltpu.SemaphoreType.DMA((n, 7)),
                        pltpu.SemaphoreType.DMA((n,))],
    )(*parts)


def _all_reduce_small(blob, name):
    R, C = blob.shape

    def body(in_ref, out_ref, gath, send_sems, recv_sems):
        x, y, c = _position()
        me = _slot(x, y, c)
        gath[me] = in_ref[...]
        sends = []
        for k in range(1, N_DEV):
            to = _peer(x, y, c, k)
            cp = pltpu.make_async_remote_copy(
                src_ref=in_ref, dst_ref=gath.at[me],
                send_sem=send_sems.at[k - 1], recv_sem=recv_sems.at[k - 1],
                device_id=to, device_id_type=MESH)
            cp.start()
            sends.append(cp)
        for k in range(1, N_DEV):
            frm = _peer(x, y, c, k)
            pltpu.make_async_remote_copy(
                src_ref=in_ref, dst_ref=gath.at[_slot(*frm)],
                send_sem=send_sems.at[k - 1], recv_sem=recv_sems.at[k - 1],
                device_id=frm, device_id_type=MESH).wait_recv()
        for cp in sends:
            cp.wait_send()
        total = gath[0]
        for s in range(1, N_DEV):
            total = total + gath[s]
        out_ref[...] = total

    return pl.pallas_call(
        body, name=name,
        in_specs=[pl.BlockSpec(memory_space=pltpu.VMEM)],
        out_specs=pl.BlockSpec(memory_space=pltpu.VMEM),
        out_shape=jax.ShapeDtypeStruct((R, C), F32),
        scratch_shapes=[pltpu.VMEM((N_DEV, R, C), F32), pltpu.SemaphoreType.DMA((7,)),
                        pltpu.SemaphoreType.DMA((7,))],
        compiler_params=pltpu.CompilerParams(vmem_limit_bytes=VMEM_LIMIT),
    )(blob)


SMALL_VECS = ("ffn1_pre_g", "ffn1_post_g", "mix_pre_g", "sgu_ln_g", "sgu_ln_b", "mix_post_g", "ffn2_pre_g",
              "ffn2_post_g")
ROW_BS = len(SMALL_VECS)
ROW_BF = ROW_BS + 1
ROW_LOSS = ROW_BF + 1
ROW_WS = 16
BLOB_ROWS = ROW_WS + SGU_LEN


def _pack_small(vals, D, loss_row=None):
    rows = [vals[n].reshape(1, D) for n in SMALL_VECS]
    rows.append(vals["sgu_b_s"].reshape(1, D))
    rows.append(jnp.pad(vals["b_forget"].reshape(1, N_HEADS), ((0, 0), (0, D - N_HEADS))))
    rows.append(jnp.zeros((1, D), F32) if loss_row is None else loss_row)
    rows.append(jnp.zeros((ROW_WS - ROW_LOSS - 1, D), F32))
    rows.append(vals["sgu_w_s"].reshape(SGU_LEN, D))
    return jnp.concatenate(rows, axis=0)


def _unpack_small(blob, D):
    out = {n: blob[r:r + 1] for r, n in enumerate(SMALL_VECS)}
    out["sgu_b_s"] = blob[ROW_BS].reshape(1, N_GROUPS, SGU_LEN)
    out["b_forget"] = blob[ROW_BF, :N_HEADS].reshape(1, N_HEADS)
    out["sgu_w_s"] = blob[ROW_WS:].reshape(1, N_GROUPS, SGU_LEN, SGU_LEN)
    return out


WEIGHT_NAMES = ("ffn1_pre_g", "ffn1_w_gate", "ffn1_w_up", "ffn1_w_down", "ffn1_post_g", "mix_pre_g", "w_in",
                "b_forget", "sgu_ln_g", "sgu_ln_b", "sgu_w_s", "sgu_b_s", "w_out", "mix_post_g", "ffn2_pre_g",
                "ffn2_w_gate", "ffn2_w_up", "ffn2_w_down", "ffn2_post_g")
BIG_NAMES = ("ffn1_w_gate", "ffn1_w_up", "ffn1_w_down", "w_in", "w_out", "ffn2_w_gate", "ffn2_w_up", "ffn2_w_down")


def _local_step(x, target, small, big):
    T, D = x.shape
    W = N_HEADS * HEAD_DIM
    vec = lambda n: small[n].reshape(1, D)

    w_in_all = big["w_in"]
    in_width = N_DEV * w_in_all.shape[2]
    w_in = w_in_all.transpose(1, 0, 2).reshape(D, in_width)
    col_f = 3 * W
    col_u = col_f + N_HEADS
    seg_starts = (0, W, 2 * W, col_u, col_u + W, col_u + 2 * W, col_u + 3 * W)
    w7 = jnp.stack([w_in[:, s:s + W] for s in seg_starts])
    wf = jnp.pad(w_in[:, col_f:col_u], ((0, 0), (0, LANES - N_HEADS)))
    w_out = big["w_out"].reshape(D, D)
    b_pad = jnp.pad(small["b_forget"].reshape(1, N_HEADS), ((0, 0), (0, LANES - N_HEADS)))
    lng, lnb = vec("sgu_ln_g"), vec("sgu_ln_b")
    ws = small["sgu_w_s"].reshape(N_GROUPS, SGU_LEN, SGU_LEN)
    bs = small["sgu_b_s"].reshape(N_GROUPS, SGU_LEN, 1)

    x1, y1, gate1, up1 = _ffn_fwd(x, vec("ffn1_pre_g"), big["ffn1_w_gate"], big["ffn1_w_up"], big["ffn1_w_down"],
                                  vec("ffn1_post_g"), "ffn1_fwd")
    z7, f_logit, h2b = _mix_in_fwd(x1, vec("mix_pre_g"), w7, wf, "mix_in_fwd")
    c = _forget_cumsum(f_logit, b_pad, "forget_cumsum")
    c_heads = c[:, :N_HEADS].T
    c_row = c_heads[:, None, :]
    c_col = c_heads[:, :, None]
    o_a, lse = _attn_fwd(z7, c_row, "attn_fwd")
    x2, p, merged_b = _mix_out_fwd(z7, o_a, x1, lng, lnb, ws, bs, w_out, vec("mix_post_g"), "mix_out_fwd")
    x3, y2, gate2, up2 = _ffn_fwd(x2, vec("ffn2_pre_g"), big["ffn2_w_gate"], big["ffn2_w_up"], big["ffn2_w_down"],
                                  vec("ffn2_post_g"), "ffn2_fwd")
    dy, loss_lanes = _loss_head(x3, target, "loss_head")

    grads_small = {}
    parts = {}

    dx2, h3b, dy2b, act2, dgate2, dup2, dgpre, dgpost = _ffn_bwd(
        dy, x2, y2, gate2, up2, vec("ffn2_pre_g"), big["ffn2_w_gate"], big["ffn2_w_up"], big["ffn2_w_down"],
        vec("ffn2_post_g"), "ffn2_bwd")
    grads_small["ffn2_pre_g"] = jnp.sum(dgpre, axis=0)
    grads_small["ffn2_post_g"] = jnp.sum(dgpost, axis=0)
    parts["ffn2_w_gate"] = _wgrad(h3b, dgate2, "ffn2_wgrad_gate", shard_cols=True)
    parts["ffn2_w_up"] = _wgrad(h3b, dup2, "ffn2_wgrad_up", shard_cols=True)
    parts["ffn2_w_down"] = _wgrad(act2, dy2b, "ffn2_wgrad_down").reshape(big["ffn2_w_down"].shape)

    dpb, dob, dvec, dz4, dgp, dlng, dlnb, dws, dbs = _mix_out_bwd(
        dx2, p, z7, o_a, lng, lnb, ws, bs, w_out, vec("mix_post_g"), "mix_out_bwd")
    grads_small["mix_post_g"] = dgp
    grads_small["sgu_ln_g"] = dlng
    grads_small["sgu_ln_b"] = dlnb
    grads_small["sgu_w_s"] = dws
    grads_small["sgu_b_s"] = dbs
    parts["w_out"] = _wgrad(merged_b, dpb, "w_out_wgrad").reshape(big["w_out"].shape)
    lse_row = lse.reshape(N_HEADS, 1, T)
    d_row = dvec.reshape(N_HEADS, 1, T)
    dk, dv, dc = _attn_bwd_kv(z7, dob, c_col, lse_row, d_row, "attn_bwd_kv")
    dq, dc_q = _attn_bwd_q(z7, dob, c_row, lse, dvec, "attn_bwd_q")
    dc_pad = jnp.pad((dc + dc_q).reshape(N_HEADS, T).T, ((0, 0), (0, LANES - N_HEADS)))
    dfb, dbf = _forget_bwd(dc_pad, f_logit, b_pad, "forget_bwd")
    grads_small["b_forget"] = dbf[:, :N_HEADS]
    segs = [(dq, None), (dk, None), (dv, None), (dz4, 0), (dz4, 1), (dz4, 2), (dz4, 3)]
    dx1, dgm = _mix_in_bwd(dx2, x1, vec("mix_pre_g"), segs, dfb, w7, wf, "mix_in_bwd")
    grads_small["mix_pre_g"] = jnp.sum(dgm, axis=0)
    seg_mats = [dq, dk, dv, dz4[0], dz4[1], dz4[2], dz4[3]]
    dw_seg = [_wgrad(h2b, sm, "w_in_wgrad_%d" % q) for q, sm in enumerate(seg_mats)]
    dwf = _wgrad(h2b, dfb, "w_in_wgrad_f")[:, :N_HEADS]
    dw_in = jnp.concatenate(dw_seg[:3] + [dwf] + dw_seg[3:], axis=1)
    parts["w_in"] = dw_in.reshape(D, N_DEV, in_width // N_DEV).transpose(1, 0, 2)

    dx0, h1b, dy1b, act1, dgate1, dup1, dgpre1, dgpost1 = _ffn_bwd(
        dx1, x, y1, gate1, up1, vec("ffn1_pre_g"), big["ffn1_w_gate"], big["ffn1_w_up"], big["ffn1_w_down"],
        vec("ffn1_post_g"), "ffn1_bwd")
    grads_small["ffn1_pre_g"] = jnp.sum(dgpre1, axis=0)
    grads_small["ffn1_post_g"] = jnp.sum(dgpost1, axis=0)
    parts["ffn1_w_gate"] = _wgrad(h1b, dgate1, "ffn1_wgrad_gate", shard_cols=True)
    parts["ffn1_w_up"] = _wgrad(h1b, dup1, "ffn1_wgrad_up", shard_cols=True)
    parts["ffn1_w_down"] = _wgrad(act1, dy1b, "ffn1_wgrad_down").reshape(big["ffn1_w_down"].shape)

    loss_row = jnp.pad(loss_lanes, ((0, 0), (0, D - LANES)))
    return loss_row, dx0, grads_small, parts


def kernel(x, ffn1_pre_g, ffn1_w_gate, ffn1_w_up, ffn1_w_down, ffn1_post_g, mix_pre_g, w_in, b_forget, sgu_ln_g, sgu_ln_b, sgu_w_s, sgu_b_s, w_out, mix_post_g, ffn2_pre_g, ffn2_w_gate, ffn2_w_up, ffn2_w_down, ffn2_post_g, loss_target, m_ffn1_pre_g, m_ffn1_w_gate, m_ffn1_w_up, m_ffn1_w_down, m_ffn1_post_g, m_mix_pre_g, m_w_in, m_b_forget, m_sgu_ln_g, m_sgu_ln_b, m_sgu_w_s, m_sgu_b_s, m_w_out, m_mix_post_g, m_ffn2_pre_g, m_ffn2_w_gate, m_ffn2_w_up, m_ffn2_w_down, m_ffn2_post_g, v_ffn1_pre_g, v_ffn1_w_gate, v_ffn1_w_up, v_ffn1_w_down, v_ffn1_post_g, v_mix_pre_g, v_w_in, v_b_forget, v_sgu_ln_g, v_sgu_ln_b, v_sgu_w_s, v_sgu_b_s, v_w_out, v_mix_post_g, v_ffn2_pre_g, v_ffn2_w_gate, v_ffn2_w_up, v_ffn2_w_down, v_ffn2_post_g):
    weights = dict(zip(WEIGHT_NAMES, (ffn1_pre_g, ffn1_w_gate, ffn1_w_up, ffn1_w_down, ffn1_post_g, mix_pre_g, w_in,
                                      b_forget, sgu_ln_g, sgu_ln_b, sgu_w_s, sgu_b_s, w_out, mix_post_g, ffn2_pre_g,
                                      ffn2_w_gate, ffn2_w_up, ffn2_w_down, ffn2_post_g)))
    mom1 = dict(zip(WEIGHT_NAMES, (m_ffn1_pre_g, m_ffn1_w_gate, m_ffn1_w_up, m_ffn1_w_down, m_ffn1_post_g,
                                   m_mix_pre_g, m_w_in, m_b_forget, m_sgu_ln_g, m_sgu_ln_b, m_sgu_w_s, m_sgu_b_s,
                                   m_w_out, m_mix_post_g, m_ffn2_pre_g, m_ffn2_w_gate, m_ffn2_w_up, m_ffn2_w_down,
                                   m_ffn2_post_g)))
    mom2 = dict(zip(WEIGHT_NAMES, (v_ffn1_pre_g, v_ffn1_w_gate, v_ffn1_w_up, v_ffn1_w_down, v_ffn1_post_g,
                                   v_mix_pre_g, v_w_in, v_b_forget, v_sgu_ln_g, v_sgu_ln_b, v_sgu_w_s, v_sgu_b_s,
                                   v_w_out, v_mix_post_g, v_ffn2_pre_g, v_ffn2_w_gate, v_ffn2_w_up, v_ffn2_w_down,
                                   v_ffn2_post_g)))
    D = x.shape[-1]
    small_names = [n for n in WEIGHT_NAMES if n not in BIG_NAMES]

    gathered = _all_gather([weights[n][0].astype(BF16) for n in BIG_NAMES], "weights_all_gather")
    big = dict(zip(BIG_NAMES, gathered))
    small = {n: weights[n] for n in small_names}
    loss_row, grad_x, grads_small, parts = _local_step(x[0], loss_target[0], small, big)

    received = _exchange([parts[n] for n in BIG_NAMES], "grad_exchange")
    out = {}
    for n, rcv in zip(BIG_NAMES, received):
        g, d, m_new, v_new = _sum_adamw(rcv, weights[n][0], mom1[n][0], mom2[n][0], "adamw_" + n)
        out[n] = tuple(a[None] for a in (g, d, m_new, v_new))

    blob = _all_reduce_small(_pack_small(grads_small, D, loss_row), "small_all_reduce")
    d_blob, m_blob, v_blob = _adamw(blob, _pack_small(small, D), _pack_small({n: mom1[n] for n in small_names}, D),
                                    _pack_small({n: mom2[n] for n in small_names}, D), "adamw_small")
    unpacked = [_unpack_small(b, D) for b in (blob, d_blob, m_blob, v_blob)]
    for n in small_names:
        out[n] = tuple(u[n].reshape(weights[n].shape) for u in unpacked)

    loss = blob[ROW_LOSS, 0]
    result = [loss, grad_x[None]]
    for k in range(4):
        result += [out[n][k] for n in WEIGHT_NAMES]
    return tuple(result)
```

```python
import functools

import numpy as np
import jax
import jax.numpy as jnp
from jax import lax
from jax.experimental import pallas as pl
from jax.experimental.pallas import tpu as pltpu

F32 = jnp.float32
BF16 = jnp.bfloat16

RMS_EPS = 1e-6
LN_EPS = 1e-5
HEAD_DIM = 128
N_HEADS = 8
GROUP_DIM = 128
N_GROUPS = 8
SGU_LEN = 128
CHUNK = 64
N_DEV = 8
LANES = 128
VMEM_LIMIT = 56 * 1024 * 1024
NEG_BIG = -1e30

ADAM_LR = 0.001
ADAM_B1 = 0.9
ADAM_B2 = 0.999
ADAM_EPS = 1e-08
ADAM_WD = 0.01
ADAM_STEP = 10

MESH = pl.DeviceIdType.MESH
ANY = pl.BlockSpec(memory_space=pl.ANY)


def _blk(n, pref):
    return pref if (n >= pref and n % pref == 0) else n


def _mm(a, b):
    return jnp.dot(a, b, preferred_element_type=F32)


def _mm_nt(a, b):
    return lax.dot_general(a, b, (((1,), (1,)), ((), ())), preferred_element_type=F32)


def _mm_tn(a, b):
    return lax.dot_general(a, b, (((0,), (0,)), ((), ())), preferred_element_type=F32)


def _params(sem):
    return pltpu.CompilerParams(dimension_semantics=sem, vmem_limit_bytes=VMEM_LIMIT)


def _gelu(x):
    return 0.5 * x * (1.0 + lax.erf(x * np.float32(1.0 / np.sqrt(2.0))))


def _gelu_grad(x):
    cdf = 0.5 * (1.0 + lax.erf(x * np.float32(1.0 / np.sqrt(2.0))))
    return cdf + x * jnp.exp(-0.5 * x * x) * np.float32(1.0 / np.sqrt(2.0 * np.pi))


def _rms_scale(v):
    return lax.rsqrt(jnp.mean(v * v, axis=-1, keepdims=True) + RMS_EPS)


def _rms_bwd(dy, xhat, r, g):
    dxh = dy * g
    return r * (dxh - xhat * jnp.mean(dxh * xhat, axis=-1, keepdims=True))


def _ffn_fwd(x, g_pre, wg, wu, wd, g_post, name):
    T, D = x.shape
    ns, _, fs = wg.shape
    tm = _blk(T, 512)

    def body(x_ref, gpre_ref, wg_ref, wu_ref, wd_ref, gpost_ref, xo_ref, y_ref, g_ref, u_ref, h_scr, acc_scr):
        j = pl.program_id(1)

        @pl.when(j == 0)
        def _():
            xv = x_ref[...]
            h_scr[...] = (xv * _rms_scale(xv) * gpre_ref[...]).astype(BF16)
            acc_scr[...] = jnp.zeros_like(acc_scr)

        h = h_scr[...]
        gg = _mm(h, wg_ref[...])
        uu = _mm(h, wu_ref[...])
        a = gg * jax.nn.sigmoid(gg) * uu
        g_ref[...] = gg.astype(BF16)
        u_ref[...] = uu.astype(BF16)
        acc_scr[...] += _mm(a.astype(BF16), wd_ref[...])

        @pl.when(j == ns - 1)
        def _():
            y = acc_scr[...]
            y_ref[...] = y
            xo_ref[...] = x_ref[...] + 0.5 * (y * _rms_scale(y) * gpost_ref[...])

    row = pl.BlockSpec((tm, D), lambda i, j: (i, 0))
    vec = pl.BlockSpec((1, D), lambda i, j: (0, 0))
    return pl.pallas_call(
        body, name=name, grid=(T // tm, ns),
        in_specs=[row, vec,
                  pl.BlockSpec((None, D, fs), lambda i, j: (j, 0, 0)),
                  pl.BlockSpec((None, D, fs), lambda i, j: (j, 0, 0)),
                  pl.BlockSpec((None, fs, D), lambda i, j: (j, 0, 0)),
                  vec],
        out_specs=[row, row,
                   pl.BlockSpec((tm, fs), lambda i, j: (i, j)),
                   pl.BlockSpec((tm, fs), lambda i, j: (i, j))],
        out_shape=[jax.ShapeDtypeStruct((T, D), F32), jax.ShapeDtypeStruct((T, D), F32),
                   jax.ShapeDtypeStruct((T, ns * fs), BF16), jax.ShapeDtypeStruct((T, ns * fs), BF16)],
        scratch_shapes=[pltpu.VMEM((tm, D), BF16), pltpu.VMEM((tm, D), F32)],
        compiler_params=_params(("parallel", "arbitrary")),
    )(x, g_pre, wg, wu, wd, g_post)


def _ffn_bwd(dxo, x, y, gate, up, g_pre, wg, wu, wd, g_post, name):
    T, D = x.shape
    ns, _, fs = wg.shape
    tm = _blk(T, 512)
    n_i = T // tm

    def body(dxo_ref, x_ref, y_ref, g_ref, u_ref, gpre_ref, wg_ref, wu_ref, wd_ref, gpost_ref,
             dx_ref, hb_ref, dyb_ref, ab_ref, dgb_ref, dub_ref, dgpre_ref, dgpost_ref, dy_scr, acc_scr):
        j = pl.program_id(1)

        @pl.when(j == 0)
        def _():
            yv = y_ref[...]
            s = _rms_scale(yv)
            n = yv * s
            dn = 0.5 * dxo_ref[...]
            dgpost_ref[...] = jnp.sum(dn * n, axis=0, keepdims=True)
            dyv = _rms_bwd(dn, n, s, gpost_ref[...]).astype(BF16)
            dy_scr[...] = dyv
            dyb_ref[...] = dyv
            xv = x_ref[...]
            hb_ref[...] = (xv * _rms_scale(xv) * gpre_ref[...]).astype(BF16)
            acc_scr[...] = jnp.zeros_like(acc_scr)

        da = _mm_nt(dy_scr[...], wd_ref[...])
        gg = g_ref[...].astype(F32)
        uu = u_ref[...].astype(F32)
        sg = jax.nn.sigmoid(gg)
        silu = gg * sg
        dgate = (da * uu * (sg * (1.0 + gg * (1.0 - sg)))).astype(BF16)
        dup = (da * silu).astype(BF16)
        ab_ref[...] = (silu * uu).astype(BF16)
        dgb_ref[...] = dgate
        dub_ref[...] = dup
        acc_scr[...] += _mm_nt(dgate, wg_ref[...]) + _mm_nt(dup, wu_ref[...])

        @pl.when(j == ns - 1)
        def _():
            xv = x_ref[...]
            r = _rms_scale(xv)
            xhat = xv * r
            dh = acc_scr[...]
            dgpre_ref[...] = jnp.sum(dh * xhat, axis=0, keepdims=True)
            dx_ref[...] = _rms_bwd(dh, xhat, r, gpre_ref[...]) + dxo_ref[...]

    row = pl.BlockSpec((tm, D), lambda i, j: (i, 0))
    vec = pl.BlockSpec((1, D), lambda i, j: (0, 0))
    wide = pl.BlockSpec((tm, fs), lambda i, j: (i, j))
    part = pl.BlockSpec((None, 1, D), lambda i, j: (i, 0, 0))
    F = ns * fs
    return pl.pallas_call(
        body, name=name, grid=(n_i, ns),
        in_specs=[row, row, row, wide, wide, vec,
                  pl.BlockSpec((None, D, fs), lambda i, j: (j, 0, 0)),
                  pl.BlockSpec((None, D, fs), lambda i, j: (j, 0, 0)),
                  pl.BlockSpec((None, fs, D), lambda i, j: (j, 0, 0)),
                  vec],
        out_specs=[row, row, row, wide, wide, wide, part, part],
        out_shape=[jax.ShapeDtypeStruct((T, D), F32), jax.ShapeDtypeStruct((T, D), BF16),
                   jax.ShapeDtypeStruct((T, D), BF16), jax.ShapeDtypeStruct((T, F), BF16),
                   jax.ShapeDtypeStruct((T, F), BF16), jax.ShapeDtypeStruct((T, F), BF16),
                   jax.ShapeDtypeStruct((n_i, 1, D), F32), jax.ShapeDtypeStruct((n_i, 1, D), F32)],
        scratch_shapes=[pltpu.VMEM((tm, D), BF16), pltpu.VMEM((tm, D), F32)],
        compiler_params=_params(("parallel", "arbitrary")),
    )(dxo, x, y, gate, up, g_pre, wg, wu, wd, g_post)


def _wgrad(xm, ym, name, shard_cols=False):
    T, M = xm.shape
    _, N = ym.shape
    bm = _blk(M, 1024)
    bn = N // N_DEV if shard_cols else _blk(N, 512)
    tk = _blk(T, 1024)
    n_k = T // tk

    def body(x_ref, y_ref, o_ref, acc_scr):
        k = pl.program_id(2)

        @pl.when(k == 0)
        def _():
            acc_scr[...] = jnp.zeros_like(acc_scr)

        acc_scr[...] += _mm_tn(x_ref[...], y_ref[...])

        @pl.when(k == n_k - 1)
        def _():
            o_ref[...] = acc_scr[...].astype(BF16)

    if shard_cols:
        out_spec = pl.BlockSpec((None, bm, bn), lambda i, j, k: (j, i, 0))
        out_shape = jax.ShapeDtypeStruct((N // bn, M, bn), BF16)
    else:
        out_spec = pl.BlockSpec((bm, bn), lambda i, j, k: (i, j))
        out_shape = jax.ShapeDtypeStruct((M, N), BF16)
    return pl.pallas_call(
        body, name=name, grid=(M // bm, N // bn, n_k),
        in_specs=[pl.BlockSpec((tk, bm), lambda i, j, k: (k, i)),
                  pl.BlockSpec((tk, bn), lambda i, j, k: (k, j))],
        out_specs=out_spec, out_shape=out_shape,
        scratch_shapes=[pltpu.VMEM((bm, bn), F32)],
        compiler_params=_params(("parallel", "parallel", "arbitrary")),
    )(xm, ym)


def _mix_in_fwd(x1, g, w7, wf, name):
    T, D = x1.shape
    n_seg, _, W = w7.shape
    tm = _blk(T, 1024)

    def body(x_ref, g_ref, w_ref, wf_ref, z_ref, f_ref, hb_ref, h_scr):
        s = pl.program_id(1)

        @pl.when(s == 0)
        def _():
            xv = x_ref[...]
            h = (xv * _rms_scale(xv) * g_ref[...]).astype(BF16)
            h_scr[...] = h
            hb_ref[...] = h
            f_ref[...] = _mm(h, wf_ref[...])

        z_ref[...] = _mm(h_scr[...], w_ref[...]).astype(BF16)

    return pl.pallas_call(
        body, name=name, grid=(T // tm, n_seg),
        in_specs=[pl.BlockSpec((tm, D), lambda i, s: (i, 0)),
                  pl.BlockSpec((1, D), lambda i, s: (0, 0)),
                  pl.BlockSpec((None, D, W), lambda i, s: (s, 0, 0)),
                  pl.BlockSpec((D, LANES), lambda i, s: (0, 0))],
        out_specs=[pl.BlockSpec((None, tm, W), lambda i, s: (s, i, 0)),
                   pl.BlockSpec((tm, LANES), lambda i, s: (i, 0)),
                   pl.BlockSpec((tm, D), lambda i, s: (i, 0))],
        out_shape=[jax.ShapeDtypeStruct((n_seg, T, W), BF16), jax.ShapeDtypeStruct((T, LANES), F32),
                   jax.ShapeDtypeStruct((T, D), BF16)],
        scratch_shapes=[pltpu.VMEM((tm, D), BF16)],
        compiler_params=_params(("parallel", "arbitrary")),
    )(x1, g, w7, wf)


def _mix_in_bwd(dx2, x1, g, segs, dfb, w7, wf, name):
    T, D = x1.shape
    n_seg, _, W = w7.shape
    tm = _blk(T, 512)
    n_i = T // tm

    def body(*refs):
        dx2_ref, x_ref, g_ref = refs[:3]
        seg_refs = refs[3:3 + n_seg]
        df_ref, w_ref, wf_ref, dx1_ref, dg_ref, acc_scr = refs[3 + n_seg:]
        s = pl.program_id(1)

        @pl.when(s == 0)
        def _():
            acc_scr[...] = _mm_nt(df_ref[...], wf_ref[...])

        for q in range(n_seg):
            @pl.when(s == q)
            def _(q=q):
                acc_scr[...] += _mm_nt(seg_refs[q][...], w_ref[...])

        @pl.when(s == n_seg - 1)
        def _():
            xv = x_ref[...]
            r = _rms_scale(xv)
            xhat = xv * r
            dh = acc_scr[...]
            dg_ref[...] = jnp.sum(dh * xhat, axis=0, keepdims=True)
            dx1_ref[...] = _rms_bwd(dh, xhat, r, g_ref[...]) + dx2_ref[...]

    row = pl.BlockSpec((tm, D), lambda i, s: (i, 0))
    seg_specs = []
    seg_args = []
    for arr, idx in segs:
        if idx is None:
            seg_specs.append(pl.BlockSpec((tm, W), lambda i, s: (i, 0)))
        else:
            seg_specs.append(pl.BlockSpec((None, tm, W), lambda i, s, idx=idx: (idx, i, 0)))
        seg_args.append(arr)
    return pl.pallas_call(
        body, name=name, grid=(n_i, n_seg),
        in_specs=[row, row, pl.BlockSpec((1, D), lambda i, s: (0, 0))] + seg_specs + [
            pl.BlockSpec((tm, LANES), lambda i, s: (i, 0)),
            pl.BlockSpec((None, D, W), lambda i, s: (s, 0, 0)),
            pl.BlockSpec((D, LANES), lambda i, s: (0, 0))],
        out_specs=[row, pl.BlockSpec((None, 1, D), lambda i, s: (i, 0, 0))],
        out_shape=[jax.ShapeDtypeStruct((T, D), F32), jax.ShapeDtypeStruct((n_i, 1, D), F32)],
        scratch_shapes=[pltpu.VMEM((tm, D), F32)],
        compiler_params=_params(("parallel", "arbitrary")),
    )(dx2, x1, g, *seg_args, dfb, w7, wf)


def _forget_cumsum(f, b_pad, name):
    T, L = f.shape
    tb = _blk(T, 256)

    def body(f_ref, b_ref, c_ref, carry):
        @pl.when(pl.program_id(0) == 0)
        def _():
            carry[...] = jnp.zeros_like(carry)

        lf = jax.nn.log_sigmoid(f_ref[...] + b_ref[...])
        rows = lax.broadcasted_iota(jnp.int32, (tb, tb), 0)
        cols = lax.broadcasted_iota(jnp.int32, (tb, tb), 1)
        tri = (cols <= rows).astype(F32)
        c = jnp.dot(tri, lf, preferred_element_type=F32, precision=lax.Precision.HIGHEST) + carry[...]
        c_ref[...] = c
        carry[...] = c[tb - 1:tb, :]

    return pl.pallas_call(
        body, name=name, grid=(T // tb,),
        in_specs=[pl.BlockSpec((tb, L), lambda i: (i, 0)), pl.BlockSpec((1, L), lambda i: (0, 0))],
        out_specs=pl.BlockSpec((tb, L), lambda i: (i, 0)),
        out_shape=jax.ShapeDtypeStruct((T, L), F32),
        scratch_shapes=[pltpu.VMEM((1, L), F32)],
        compiler_params=_params(("arbitrary",)),
    )(f, b_pad)


def _forget_bwd(dc, f, b_pad, name):
    T, L = f.shape
    tb = _blk(T, 256)
    nb = T // tb

    def body(dc_ref, f_ref, b_ref, df_ref, db_ref, carry):
        @pl.when(pl.program_id(0) == 0)
        def _():
            carry[...] = jnp.zeros_like(carry)
            db_ref[...] = jnp.zeros_like(db_ref)

        rows = lax.broadcasted_iota(jnp.int32, (tb, tb), 0)
        cols = lax.broadcasted_iota(jnp.int32, (tb, tb), 1)
        tri = (cols >= rows).astype(F32)
        r = jnp.dot(tri, dc_ref[...], preferred_element_type=F32, precision=lax.Precision.HIGHEST) + carry[...]
        carry[...] = r[0:1, :]
        df = r * (1.0 - jax.nn.sigmoid(f_ref[...] + b_ref[...]))
        df_ref[...] = df.astype(BF16)
        db_ref[...] += jnp.sum(df, axis=0, keepdims=True)

    rev = pl.BlockSpec((tb, L), lambda i: (nb - 1 - i, 0))
    one = pl.BlockSpec((1, L), lambda i: (0, 0))
    return pl.pallas_call(
        body, name=name, grid=(nb,),
        in_specs=[rev, rev, one], out_specs=[rev, one],
        out_shape=[jax.ShapeDtypeStruct((T, L), BF16), jax.ShapeDtypeStruct((1, L), F32)],
        scratch_shapes=[pltpu.VMEM((1, L), F32)],
        compiler_params=_params(("arbitrary",)),
    )(dc, f, b_pad)


def _attn_fwd(z7, c_row, name):
    _, T, W = z7.shape
    H = W // HEAD_DIM
    ta = _blk(T, 512)
    nq = T // ta
    scale = np.float32(1.0 / np.sqrt(HEAD_DIM))

    def body(q_ref, k_ref, v_ref, crow_ref, o_ref, lse_ref, m_scr, l_scr, acc_scr):
        i = pl.program_id(1)
        j = pl.program_id(2)

        @pl.when(j == 0)
        def _():
            m_scr[...] = jnp.full_like(m_scr, NEG_BIG)
            l_scr[...] = jnp.zeros_like(l_scr)
            acc_scr[...] = jnp.zeros_like(acc_scr)

        def step(diagonal):
            s = _mm_nt(q_ref[...], k_ref[...]) * scale - crow_ref[...]
            if diagonal:
                rows = lax.broadcasted_iota(jnp.int32, (ta, ta), 0)
                cols = lax.broadcasted_iota(jnp.int32, (ta, ta), 1)
                s = jnp.where(cols <= rows, s, NEG_BIG)
            m_prev = m_scr[...]
            m_new = jnp.maximum(m_prev, jnp.max(s, axis=-1, keepdims=True))
            alpha = jnp.exp(m_prev - m_new)
            p = jnp.exp(s - m_new)
            l_scr[...] = alpha * l_scr[...] + jnp.sum(p, axis=-1, keepdims=True)
            acc_scr[...] = alpha * acc_scr[...] + _mm(p.astype(BF16), v_ref[...])
            m_scr[...] = m_new

        @pl.when(j < i)
        def _():
            step(False)

        @pl.when(j == i)
        def _():
            step(True)
            l = l_scr[...]
            o_ref[...] = acc_scr[...] / l
            lse_ref[...] = m_scr[...] + jnp.log(l)

    return pl.pallas_call(
        body, name=name, grid=(H, nq, nq),
        in_specs=[pl.BlockSpec((None, ta, HEAD_DIM), lambda h, i, j: (0, i, h)),
                  pl.BlockSpec((None, ta, HEAD_DIM), lambda h, i, j: (1, jnp.minimum(i, j), h)),
                  pl.BlockSpec((None, ta, HEAD_DIM), lambda h, i, j: (2, jnp.minimum(i, j), h)),
                  pl.BlockSpec((None, 1, ta), lambda h, i, j: (h, 0, jnp.minimum(i, j)))],
        out_specs=[pl.BlockSpec((ta, HEAD_DIM), lambda h, i, j: (i, h)),
                   pl.BlockSpec((None, ta, 1), lambda h, i, j: (h, i, 0))],
        out_shape=[jax.ShapeDtypeStruct((T, W), F32), jax.ShapeDtypeStruct((H, T, 1), F32)],
        scratch_shapes=[pltpu.VMEM((ta, 1), F32), pltpu.VMEM((ta, 1), F32), pltpu.VMEM((ta, HEAD_DIM), F32)],
        compiler_params=_params(("parallel", "parallel", "arbitrary")),
    )(z7, z7, z7, c_row)


def _attn_bwd_kv(z7, dob, c_col, lse_row, d_row, name):
    _, T, W = z7.shape
    H = W // HEAD_DIM
    ta = _blk(T, 512)
    nq = T // ta
    scale = np.float32(1.0 / np.sqrt(HEAD_DIM))

    def body(k_ref, v_ref, q_ref, do_ref, ccol_ref, lse_ref, d_ref, dk_ref, dv_ref, dc_ref, dk_scr, dv_scr, dc_scr):
        j = pl.program_id(1)
        i = pl.program_id(2)

        @pl.when(i == 0)
        def _():
            dk_scr[...] = jnp.zeros_like(dk_scr)
            dv_scr[...] = jnp.zeros_like(dv_scr)
            dc_scr[...] = jnp.zeros_like(dc_scr)

        def step(diagonal):
            q = q_ref[...]
            do = do_ref[...]
            st = _mm_nt(k_ref[...], q) * scale - ccol_ref[...] - lse_ref[...]
            if diagonal:
                rows = lax.broadcasted_iota(jnp.int32, (ta, ta), 0)
                cols = lax.broadcasted_iota(jnp.int32, (ta, ta), 1)
                st = jnp.where(rows <= cols, st, NEG_BIG)
            pt = jnp.exp(st)
            dv_scr[...] += _mm(pt.astype(BF16), do)
            dst = pt * (_mm_nt(v_ref[...], do) - d_ref[...])
            dk_scr[...] += _mm(dst.astype(BF16), q)
            dc_scr[...] += jnp.sum(dst, axis=-1, keepdims=True)

        @pl.when(i > j)
        def _():
            step(False)

        @pl.when(i == j)
        def _():
            step(True)

        @pl.when(i == nq - 1)
        def _():
            dk_ref[...] = (dk_scr[...] * scale).astype(BF16)
            dv_ref[...] = dv_scr[...].astype(BF16)
            dc_ref[...] = -dc_scr[...]

    return pl.pallas_call(
        body, name=name, grid=(H, nq, nq),
        in_specs=[pl.BlockSpec((None, ta, HEAD_DIM), lambda h, j, i: (1, j, h)),
                  pl.BlockSpec((None, ta, HEAD_DIM), lambda h, j, i: (2, j, h)),
                  pl.BlockSpec((None, ta, HEAD_DIM), lambda h, j, i: (0, jnp.maximum(i, j), h)),
                  pl.BlockSpec((ta, HEAD_DIM), lambda h, j, i: (jnp.maximum(i, j), h)),
                  pl.BlockSpec((None, ta, 1), lambda h, j, i: (h, j, 0)),
                  pl.BlockSpec((None, 1, ta), lambda h, j, i: (h, 0, jnp.maximum(i, j))),
                  pl.BlockSpec((None, 1, ta), lambda h, j, i: (h, 0, jnp.maximum(i, j)))],
        out_specs=[pl.BlockSpec((ta, HEAD_DIM), lambda h, j, i: (j, h)),
                   pl.BlockSpec((ta, HEAD_DIM), lambda h, j, i: (j, h)),
                   pl.BlockSpec((None, ta, 1), lambda h, j, i: (h, j, 0))],
        out_shape=[jax.ShapeDtypeStruct((T, W), BF16), jax.ShapeDtypeStruct((T, W), BF16),
                   jax.ShapeDtypeStruct((H, T, 1), F32)],
        scratch_shapes=[pltpu.VMEM((ta, HEAD_DIM), F32), pltpu.VMEM((ta, HEAD_DIM), F32), pltpu.VMEM((ta, 1), F32)],
        compiler_params=_params(("parallel", "parallel", "arbitrary")),
    )(z7, z7, z7, dob, c_col, lse_row, d_row)


def _attn_bwd_q(z7, dob, c_row, lse_col, d_col, name):
    _, T, W = z7.shape
    H = W // HEAD_DIM
    ta = _blk(T, 512)
    nq = T // ta
    scale = np.float32(1.0 / np.sqrt(HEAD_DIM))

    def body(q_ref, k_ref, v_ref, do_ref, crow_ref, lse_ref, d_ref, dq_ref, dc_ref, dq_scr, dc_scr):
        i = pl.program_id(1)
        j = pl.program_id(2)

        @pl.when(j == 0)
        def _():
            dq_scr[...] = jnp.zeros_like(dq_scr)
            dc_scr[...] = jnp.zeros_like(dc_scr)

        def step(diagonal):
            k = k_ref[...]
            do = do_ref[...]
            s = _mm_nt(q_ref[...], k) * scale - crow_ref[...] - lse_ref[...]
            if diagonal:
                rows = lax.broadcasted_iota(jnp.int32, (ta, ta), 0)
                cols = lax.broadcasted_iota(jnp.int32, (ta, ta), 1)
                s = jnp.where(cols <= rows, s, NEG_BIG)
            p = jnp.exp(s)
            ds = p * (_mm_nt(do, v_ref[...]) - d_ref[...])
            dq_scr[...] += _mm(ds.astype(BF16), k)
            dc_scr[...] += jnp.sum(ds, axis=-1, keepdims=True)

        @pl.when(j < i)
        def _():
            step(False)

        @pl.when(j == i)
        def _():
            step(True)
            dq_ref[...] = (dq_scr[...] * scale).astype(BF16)
            dc_ref[...] = dc_scr[...]

    return pl.pallas_call(
        body, name=name, grid=(H, nq, nq),
        in_specs=[pl.BlockSpec((None, ta, HEAD_DIM), lambda h, i, j: (0, i, h)),
                  pl.BlockSpec((None, ta, HEAD_DIM), lambda h, i, j: (1, jnp.minimum(i, j), h)),
                  pl.BlockSpec((None, ta, HEAD_DIM), lambda h, i, j: (2, jnp.minimum(i, j), h)),
                  pl.BlockSpec((ta, HEAD_DIM), lambda h, i, j: (i, h)),
                  pl.BlockSpec((None, 1, ta), lambda h, i, j: (h, 0, jnp.minimum(i, j))),
                  pl.BlockSpec((None, ta, 1), lambda h, i, j: (h, i, 0)),
                  pl.BlockSpec((None, ta, 1), lambda h, i, j: (h, i, 0))],
        out_specs=[pl.BlockSpec((ta, HEAD_DIM), lambda h, i, j: (i, h)),
                   pl.BlockSpec((None, ta, 1), lambda h, i, j: (h, i, 0))],
        out_shape=[jax.ShapeDtypeStruct((T, W), BF16), jax.ShapeDtypeStruct((H, T, 1), F32)],
        scratch_shapes=[pltpu.VMEM((ta, HEAD_DIM), F32), pltpu.VMEM((ta, 1), F32)],
        compiler_params=_params(("parallel", "parallel", "arbitrary")),
    )(z7, z7, z7, dob, c_row, lse_col, d_col)


ATTN_TILE = 512
ATTN_CHAINS = 2


def _attn_geometry(T):
    ta = _blk(T, ATTN_TILE)
    nc = ATTN_CHAINS if (T // ta) % ATTN_CHAINS == 0 else 1
    return ta, nc, T // ta


def _causal_tile(ta, keys_on_rows=False):
    rows = lax.broadcasted_iota(jnp.int32, (ta, ta), 0)
    cols = lax.broadcasted_iota(jnp.int32, (ta, ta), 1)
    return rows <= cols if keys_on_rows else cols <= rows


def _chunk(ref, j, ta):
    return ref[pl.ds(pl.multiple_of(j * ta, ta), ta), :]


def _attn_fwd_loop(z7, c_chunks, name):
    _, T, W = z7.shape
    H = W // HEAD_DIM
    ta, nc, n_chunks = _attn_geometry(T)
    scale = np.float32(1.0 / np.sqrt(HEAD_DIM))

    def body(q_ref, k_ref, v_ref, c_ref, o_ref, lse_ref, m_scr, l_scr, acc_scr):
        g = pl.program_id(1)
        m_scr[...] = jnp.full_like(m_scr, NEG_BIG)
        l_scr[...] = jnp.zeros_like(l_scr)
        acc_scr[...] = jnp.zeros_like(acc_scr)

        def update(ch, k, v, crow, diagonal):
            q = q_ref[ch * ta:(ch + 1) * ta, :]
            s = _mm_nt(q, k) * scale - crow
            if diagonal:
                s = jnp.where(_causal_tile(ta), s, NEG_BIG)
            m_prev = m_scr[ch]
            m_new = jnp.maximum(m_prev, jnp.max(s, axis=-1, keepdims=True))
            alpha = jnp.exp(m_prev - m_new)
            p = jnp.exp(s - m_new)
            l_scr[ch] = alpha * l_scr[ch] + jnp.sum(p, axis=-1, keepdims=True)
            acc_scr[ch] = alpha * acc_scr[ch] + _mm(p.astype(BF16), v)
            m_scr[ch] = m_new

        def full_chunk(j, carry):
            k = _chunk(k_ref, j, ta)
            v = _chunk(v_ref, j, ta)
            crow = c_ref[j]
            for ch in range(nc):
                update(ch, k, v, crow, False)
            return carry

        lax.fori_loop(0, nc * g, full_chunk, 0)
        for jj in range(nc):
            j = nc * g + jj
            k = _chunk(k_ref, j, ta)
            v = _chunk(v_ref, j, ta)
            crow = c_ref[j]
            for ch in range(jj, nc):
                update(ch, k, v, crow, ch == jj)
        for ch in range(nc):
            l = l_scr[ch]
            o_ref[ch * ta:(ch + 1) * ta, :] = acc_scr[ch] / l
            lse_ref[ch * ta:(ch + 1) * ta, :] = m_scr[ch] + jnp.log(l)

    tq = nc * ta
    return pl.pallas_call(
        body, name=name, grid=(H, n_chunks // nc),
        in_specs=[pl.BlockSpec((None, tq, HEAD_DIM), lambda h, g: (0, g, h)),
                  pl.BlockSpec((None, T, HEAD_DIM), lambda h, g: (1, 0, h)),
                  pl.BlockSpec((None, T, HEAD_DIM), lambda h, g: (2, 0, h)),
                  pl.BlockSpec((None, n_chunks, 1, ta), lambda h, g: (h, 0, 0, 0))],
        out_specs=[pl.BlockSpec((tq, HEAD_DIM), lambda h, g: (g, h)),
                   pl.BlockSpec((None, tq, 1), lambda h, g: (h, g, 0))],
        out_shape=[jax.ShapeDtypeStruct((T, W), F32), jax.ShapeDtypeStruct((H, T, 1), F32)],
        scratch_shapes=[pltpu.VMEM((nc, ta, 1), F32), pltpu.VMEM((nc, ta, 1), F32),
                        pltpu.VMEM((nc, ta, HEAD_DIM), F32)],
        compiler_params=_params(("parallel", "arbitrary")),
    )(z7, z7, z7, c_chunks)


def _attn_bwd_q_loop(z7, dob, c_chunks, lse_col, d_col, name):
    _, T, W = z7.shape
    H = W // HEAD_DIM
    ta, nc, n_chunks = _attn_geometry(T)
    scale = np.float32(1.0 / np.sqrt(HEAD_DIM))

    def body(q_ref, k_ref, v_ref, do_ref, c_ref, lse_ref, d_ref, dq_ref, dc_ref, dq_scr, dc_scr):
        g = pl.program_id(1)
        dq_scr[...] = jnp.zeros_like(dq_scr)
        dc_scr[...] = jnp.zeros_like(dc_scr)

        def update(ch, k, v, crow, diagonal):
            rows = slice(ch * ta, (ch + 1) * ta)
            do = do_ref[rows, :]
            s = _mm_nt(q_ref[rows, :], k) * scale - crow - lse_ref[rows, :]
            if diagonal:
                s = jnp.where(_causal_tile(ta), s, NEG_BIG)
            p = jnp.exp(s)
            ds = p * (_mm_nt(do, v) - d_ref[rows, :])
            dq_scr[ch] += _mm(ds.astype(BF16), k)
            dc_scr[ch] += jnp.sum(ds, axis=-1, keepdims=True)

        def full_chunk(j, carry):
            k = _chunk(k_ref, j, ta)
            v = _chunk(v_ref, j, ta)
            crow = c_ref[j]
            for ch in range(nc):
                update(ch, k, v, crow, False)
            return carry

        lax.fori_loop(0, nc * g, full_chunk, 0)
        for jj in range(nc):
            j = nc * g + jj
            k = _chunk(k_ref, j, ta)
            v = _chunk(v_ref, j, ta)
            crow = c_ref[j]
            for ch in range(jj, nc):
                update(ch, k, v, crow, ch == jj)
        for ch in range(nc):
            dq_ref[ch * ta:(ch + 1) * ta, :] = (dq_scr[ch] * scale).astype(BF16)
            dc_ref[ch * ta:(ch + 1) * ta, :] = dc_scr[ch]

    tq = nc * ta
    col = pl.BlockSpec((None, tq, 1), lambda h, g: (h, g, 0))
    return pl.pallas_call(
        body, name=name, grid=(H, n_chunks // nc),
        in_specs=[pl.BlockSpec((None, tq, HEAD_DIM), lambda h, g: (0, g, h)),
                  pl.BlockSpec((None, T, HEAD_DIM), lambda h, g: (1, 0, h)),
                  pl.BlockSpec((None, T, HEAD_DIM), lambda h, g: (2, 0, h)),
                  pl.BlockSpec((tq, HEAD_DIM), lambda h, g: (g, h)),
                  pl.BlockSpec((None, n_chunks, 1, ta), lambda h, g: (h, 0, 0, 0)),
                  col, col],
        out_specs=[pl.BlockSpec((tq, HEAD_DIM), lambda h, g: (g, h)), col],
        out_shape=[jax.ShapeDtypeStruct((T, W), BF16), jax.ShapeDtypeStruct((H, T, 1), F32)],
        scratch_shapes=[pltpu.VMEM((nc, ta, HEAD_DIM), F32), pltpu.VMEM((nc, ta, 1), F32)],
        compiler_params=_params(("parallel", "arbitrary")),
    )(z7, z7, z7, dob, c_chunks, lse_col, d_col)


def _attn_bwd_kv_loop(z7, dob, c_col, lse_chunks, d_chunks, name):
    _, T, W = z7.shape
    H = W // HEAD_DIM
    ta, nc, n_chunks = _attn_geometry(T)
    scale = np.float32(1.0 / np.sqrt(HEAD_DIM))

    def body(k_ref, v_ref, q_ref, do_ref, ccol_ref, lse_ref, d_ref, dk_ref, dv_ref, dc_ref, dk_scr, dv_scr, dc_scr):
        g = pl.program_id(1)
        dk_scr[...] = jnp.zeros_like(dk_scr)
        dv_scr[...] = jnp.zeros_like(dv_scr)
        dc_scr[...] = jnp.zeros_like(dc_scr)

        def update(ch, q, do, lse_row, d_row, diagonal):
            rows = slice(ch * ta, (ch + 1) * ta)
            st = _mm_nt(k_ref[rows, :], q) * scale - ccol_ref[rows, :] - lse_row
            if diagonal:
                st = jnp.where(_causal_tile(ta, keys_on_rows=True), st, NEG_BIG)
            pt = jnp.exp(st)
            dv_scr[ch] += _mm(pt.astype(BF16), do)
            dst = pt * (_mm_nt(v_ref[rows, :], do) - d_row)
            dk_scr[ch] += _mm(dst.astype(BF16), q)
            dc_scr[ch] += jnp.sum(dst, axis=-1, keepdims=True)

        for ii in range(nc):
            i = nc * g + ii
            q = _chunk(q_ref, i, ta)
            do = _chunk(do_ref, i, ta)
            for ch in range(0, ii + 1):
                update(ch, q, do, lse_ref[i], d_ref[i], ch == ii)

        def full_chunk(i, carry):
            q = _chunk(q_ref, i, ta)
            do = _chunk(do_ref, i, ta)
            for ch in range(nc):
                update(ch, q, do, lse_ref[i], d_ref[i], False)
            return carry

        lax.fori_loop(nc * (g + 1), n_chunks, full_chunk, 0)
        for ch in range(nc):
            rows = slice(ch * ta, (ch + 1) * ta)
            dk_ref[rows, :] = (dk_scr[ch] * scale).astype(BF16)
            dv_ref[rows, :] = dv_scr[ch].astype(BF16)
            dc_ref[rows, :] = -dc_scr[ch]

    tk = nc * ta
    chunks = pl.BlockSpec((None, n_chunks, 1, ta), lambda h, g: (h, 0, 0, 0))
    col = pl.BlockSpec((None, tk, 1), lambda h, g: (h, g, 0))
    tile = pl.BlockSpec((tk, HEAD_DIM), lambda h, g: (g, h))
    return pl.pallas_call(
        body, name=name, grid=(H, n_chunks // nc),
        in_specs=[pl.BlockSpec((None, tk, HEAD_DIM), lambda h, g: (1, g, h)),
                  pl.BlockSpec((None, tk, HEAD_DIM), lambda h, g: (2, g, h)),
                  pl.BlockSpec((None, T, HEAD_DIM), lambda h, g: (0, 0, h)),
                  pl.BlockSpec((T, HEAD_DIM), lambda h, g: (0, h)),
                  col, chunks, chunks],
        out_specs=[tile, tile, col],
        out_shape=[jax.ShapeDtypeStruct((T, W), BF16), jax.ShapeDtypeStruct((T, W), BF16),
                   jax.ShapeDtypeStruct((H, T, 1), F32)],
        scratch_shapes=[pltpu.VMEM((nc, ta, HEAD_DIM), F32), pltpu.VMEM((nc, ta, HEAD_DIM), F32),
                        pltpu.VMEM((nc, ta, 1), F32)],
        compiler_params=_params(("parallel", "arbitrary")),
    )(z7, z7, z7, dob, c_col, lse_chunks, d_chunks)


def _chunk_causal_mask():
    rows = lax.broadcasted_iota(jnp.int32, (SGU_LEN, SGU_LEN), 0)
    cols = lax.broadcasted_iota(jnp.int32, (SGU_LEN, SGU_LEN), 1)
    return (cols // CHUNK) <= (rows // CHUNK)


def _sgu_norm_mix(sv, lng_ref, lnb_ref, ws_ref, bs_ref, vn_scr, mixed_scr, vhat_scr=None):
    tm = sv.shape[0]
    vs = _gelu(sv)
    mask = _chunk_causal_mask()
    rstds = []
    for g in range(N_GROUPS):
        lanes = slice(g * GROUP_DIM, (g + 1) * GROUP_DIM)
        blk = vs[:, lanes]
        cen = blk - jnp.mean(blk, axis=-1, keepdims=True)
        rstd = lax.rsqrt(jnp.mean(cen * cen, axis=-1, keepdims=True) + LN_EPS)
        vhat = cen * rstd
        rstds.append(rstd)
        if vhat_scr is not None:
            vhat_scr[:, lanes] = vhat
        vn_scr[:, lanes] = (vhat * lng_ref[:, lanes] + lnb_ref[:, lanes]).astype(BF16)
        wm = jnp.where(mask, ws_ref[g], 0.0).astype(BF16)
        for w in range(tm // SGU_LEN):
            rows = slice(w * SGU_LEN, (w + 1) * SGU_LEN)
            mixed_scr[rows, lanes] = _mm(wm, vn_scr[rows, lanes]) + bs_ref[g]
    return rstds


def _mix_out_fwd(z7, o_a, x1, lng, lnb, ws, bs, w_out, g_post, name):
    _, T, W = z7.shape
    D = x1.shape[1]
    tm = _blk(T, 256)

    def body(u_ref, sv_ref, ga_ref, gb_ref, oa_ref, x1_ref, lng_ref, lnb_ref, ws_ref, bs_ref, wo_ref, gp_ref,
             x2_ref, p_ref, mb_ref, vn_scr, mixed_scr):
        _sgu_norm_mix(sv_ref[...].astype(F32), lng_ref, lnb_ref, ws_ref, bs_ref, vn_scr, mixed_scr)
        o_b = _gelu(u_ref[...].astype(F32)) * mixed_scr[...]
        merged = (jax.nn.sigmoid(ga_ref[...].astype(F32)) * oa_ref[...]
                  + jax.nn.sigmoid(gb_ref[...].astype(F32)) * o_b).astype(BF16)
        mb_ref[...] = merged
        p = _mm(merged, wo_ref[...])
        p_ref[...] = p
        x2_ref[...] = x1_ref[...] + p * _rms_scale(p) * gp_ref[...]

    def seg(idx):
        return pl.BlockSpec((None, tm, W), lambda i, idx=idx: (idx, i, 0))

    row = pl.BlockSpec((tm, D), lambda i: (i, 0))
    vec = pl.BlockSpec((1, D), lambda i: (0, 0))
    return pl.pallas_call(
        body, name=name, grid=(T // tm,),
        in_specs=[seg(3), seg(4), seg(5), seg(6), row, row, vec, vec,
                  pl.BlockSpec((N_GROUPS, SGU_LEN, SGU_LEN), lambda i: (0, 0, 0)),
                  pl.BlockSpec((N_GROUPS, SGU_LEN, 1), lambda i: (0, 0, 0)),
                  pl.BlockSpec((D, D), lambda i: (0, 0)), vec],
        out_specs=[row, row, row],
        out_shape=[jax.ShapeDtypeStruct((T, D), F32), jax.ShapeDtypeStruct((T, D), F32),
                   jax.ShapeDtypeStruct((T, D), BF16)],
        scratch_shapes=[pltpu.VMEM((tm, W), BF16), pltpu.VMEM((tm, W), F32)],
        compiler_params=_params(("parallel",)),
    )(z7, z7, z7, z7, o_a, x1, lng, lnb, ws, bs, w_out, g_post)


def _mix_out_bwd(dx2, p, z7, o_a, lng, lnb, ws, bs, w_out, g_post, name):
    _, T, W = z7.shape
    D = dx2.shape[1]
    tm = _blk(T, 256)
    n_w = tm // SGU_LEN

    def body(dx2_ref, p_ref, u_ref, sv_ref, ga_ref, gb_ref, oa_ref, lng_ref, lnb_ref, ws_ref, bs_ref, wo_ref, gp_ref,
             dpb_ref, dob_ref, dvec_ref, dz_ref, dgp_ref, dlng_ref, dlnb_ref, dws_ref, dbs_ref,
             vn_scr, mixed_scr, vhat_scr, dmix_scr, dvn_scr):
        @pl.when(pl.program_id(0) == 0)
        def _():
            dgp_ref[...] = jnp.zeros_like(dgp_ref)
            dlng_ref[...] = jnp.zeros_like(dlng_ref)
            dlnb_ref[...] = jnp.zeros_like(dlnb_ref)
            dws_ref[...] = jnp.zeros_like(dws_ref)
            dbs_ref[...] = jnp.zeros_like(dbs_ref)

        pv = p_ref[...]
        s = _rms_scale(pv)
        n = pv * s
        dn = dx2_ref[...]
        dgp_ref[...] += jnp.sum(dn * n, axis=0, keepdims=True)
        dpb = _rms_bwd(dn, n, s, gp_ref[...]).astype(BF16)
        dpb_ref[...] = dpb
        dmerged = _mm_nt(dpb, wo_ref[...])

        sv = sv_ref[...].astype(F32)
        rstds = _sgu_norm_mix(sv, lng_ref, lnb_ref, ws_ref, bs_ref, vn_scr, mixed_scr, vhat_scr)
        u_pre = u_ref[...].astype(F32)
        u = _gelu(u_pre)
        mixed = mixed_scr[...]
        sa = jax.nn.sigmoid(ga_ref[...].astype(F32))
        sb = jax.nn.sigmoid(gb_ref[...].astype(F32))
        oa = oa_ref[...]
        do_a = (dmerged * sa).astype(BF16)
        dob_ref[...] = do_a
        prod = do_a.astype(F32) * oa
        for h in range(N_HEADS):
            dvec_ref[h] = jnp.sum(prod[:, h * HEAD_DIM:(h + 1) * HEAD_DIM], axis=-1, keepdims=True)
        dz_ref[2] = (dmerged * oa * (sa * (1.0 - sa))).astype(BF16)
        dz_ref[3] = (dmerged * (u * mixed) * (sb * (1.0 - sb))).astype(BF16)
        do_b = dmerged * sb
        dz_ref[0] = (do_b * mixed * _gelu_grad(u_pre)).astype(BF16)
        dmix_scr[...] = do_b * u

        mask = _chunk_causal_mask()
        for g in range(N_GROUPS):
            lanes = slice(g * GROUP_DIM, (g + 1) * GROUP_DIM)
            wm = jnp.where(mask, ws_ref[g], 0.0).astype(BF16)
            dws = jnp.zeros((SGU_LEN, SGU_LEN), F32)
            dbs = jnp.zeros((SGU_LEN, 1), F32)
            for w in range(n_w):
                rows = slice(w * SGU_LEN, (w + 1) * SGU_LEN)
                dmix = dmix_scr[rows, lanes]
                dmix_b = dmix.astype(BF16)
                dvn_scr[rows, lanes] = _mm_tn(wm, dmix_b)
                dws = dws + _mm_nt(dmix_b, vn_scr[rows, lanes])
                dbs = dbs + jnp.sum(dmix, axis=-1, keepdims=True)
            dws_ref[g] += jnp.where(mask, dws, 0.0)
            dbs_ref[g] += dbs
            dvn = dvn_scr[:, lanes]
            vhat = vhat_scr[:, lanes]
            dlng_ref[:, lanes] += jnp.sum(dvn * vhat, axis=0, keepdims=True)
            dlnb_ref[:, lanes] += jnp.sum(dvn, axis=0, keepdims=True)
            dvh = dvn * lng_ref[:, lanes]
            dvs = rstds[g] * (dvh - jnp.mean(dvh, axis=-1, keepdims=True)
                              - vhat * jnp.mean(dvh * vhat, axis=-1, keepdims=True))
            dvn_scr[:, lanes] = dvs
        dz_ref[1] = (dvn_scr[...] * _gelu_grad(sv)).astype(BF16)

    def seg(idx):
        return pl.BlockSpec((None, tm, W), lambda i, idx=idx: (idx, i, 0))

    row = pl.BlockSpec((tm, D), lambda i: (i, 0))
    vec = pl.BlockSpec((1, D), lambda i: (0, 0))
    ws_spec = pl.BlockSpec((N_GROUPS, SGU_LEN, SGU_LEN), lambda i: (0, 0, 0))
    bs_spec = pl.BlockSpec((N_GROUPS, SGU_LEN, 1), lambda i: (0, 0, 0))
    return pl.pallas_call(
        body, name=name, grid=(T // tm,),
        in_specs=[row, row, seg(3), seg(4), seg(5), seg(6), row, vec, vec, ws_spec, bs_spec,
                  pl.BlockSpec((D, D), lambda i: (0, 0)), vec],
        out_specs=[row, row, pl.BlockSpec((N_HEADS, tm, 1), lambda i: (0, i, 0)),
                   pl.BlockSpec((4, tm, W), lambda i: (0, i, 0)), vec, vec, vec, ws_spec, bs_spec],
        out_shape=[jax.ShapeDtypeStruct((T, D), BF16), jax.ShapeDtypeStruct((T, W), BF16),
                   jax.ShapeDtypeStruct((N_HEADS, T, 1), F32), jax.ShapeDtypeStruct((4, T, W), BF16),
                   jax.ShapeDtypeStruct((1, D), F32), jax.ShapeDtypeStruct((1, D), F32),
                   jax.ShapeDtypeStruct((1, D), F32),
                   jax.ShapeDtypeStruct((N_GROUPS, SGU_LEN, SGU_LEN), F32),
                   jax.ShapeDtypeStruct((N_GROUPS, SGU_LEN, 1), F32)],
        scratch_shapes=[pltpu.VMEM((tm, W), BF16), pltpu.VMEM((tm, W), F32), pltpu.VMEM((tm, W), F32),
                        pltpu.VMEM((tm, W), F32), pltpu.VMEM((tm, W), F32)],
        compiler_params=_params(("arbitrary",)),
    )(dx2, p, z7, z7, z7, z7, o_a, lng, lnb, ws, bs, w_out, g_post)


def _loss_head(y, target, name):
    T, D = y.shape
    tm = _blk(T, 1024)
    n_i = T // tm

    def body(y_ref, t_ref, dy_ref, loss_ref, acc_scr):
        i = pl.program_id(0)

        @pl.when(i == 0)
        def _():
            acc_scr[...] = jnp.zeros_like(acc_scr)

        e = y_ref[...] - t_ref[...]
        dy_ref[...] = e * np.float32(1.0 / D)
        acc_scr[...] += jnp.sum(e * e, axis=0, keepdims=True)

        @pl.when(i == n_i - 1)
        def _():
            total = jnp.sum(acc_scr[...], axis=-1, keepdims=True) * np.float32(0.5 / D)
            loss_ref[...] = jnp.broadcast_to(total, loss_ref.shape)

    row = pl.BlockSpec((tm, D), lambda i: (i, 0))
    return pl.pallas_call(
        body, name=name, grid=(n_i,),
        in_specs=[row, row],
        out_specs=[row, pl.BlockSpec((1, LANES), lambda i: (0, 0))],
        out_shape=[jax.ShapeDtypeStruct((T, D), F32), jax.ShapeDtypeStruct((1, LANES), F32)],
        scratch_shapes=[pltpu.VMEM((1, D), F32)],
        compiler_params=_params(("arbitrary",)),
    )(y, target)


def _adamw_math(w, g, m, v):
    m_new = ADAM_B1 * m + (1.0 - ADAM_B1) * g
    v_new = ADAM_B2 * v + (1.0 - ADAM_B2) * (g * g)
    m_hat = m_new / np.float32(1.0 - ADAM_B1 ** ADAM_STEP)
    v_hat = v_new / np.float32(1.0 - ADAM_B2 ** ADAM_STEP)
    delta = -ADAM_LR * (m_hat / (jnp.sqrt(v_hat) + ADAM_EPS) + ADAM_WD * w)
    return delta, m_new, v_new


def _sum_adamw(parts, w, m, v, name):
    n, R, C = parts.shape
    tr = _blk(R, 128)

    def body(p_ref, w_ref, m_ref, v_ref, g_ref, d_ref, mo_ref, vo_ref):
        g = p_ref[0].astype(F32)
        for s in range(1, n):
            g = g + p_ref[s].astype(F32)
        delta, m_new, v_new = _adamw_math(w_ref[...], g, m_ref[...], v_ref[...])
        g_ref[...] = g
        d_ref[...] = delta
        mo_ref[...] = m_new
        vo_ref[...] = v_new

    row = pl.BlockSpec((tr, C), lambda i: (i, 0))
    shp = jax.ShapeDtypeStruct((R, C), F32)
    return pl.pallas_call(
        body, name=name, grid=(R // tr,),
        in_specs=[pl.BlockSpec((n, tr, C), lambda i: (0, i, 0)), row, row, row],
        out_specs=[row, row, row, row], out_shape=[shp, shp, shp, shp],
        compiler_params=_params(("parallel",)),
    )(parts, w, m, v)


def _adamw(g, w, m, v, name):
    R, C = g.shape
    tr = _blk(R, 128)

    def body(g_ref, w_ref, m_ref, v_ref, d_ref, mo_ref, vo_ref):
        delta, m_new, v_new = _adamw_math(w_ref[...], g_ref[...], m_ref[...], v_ref[...])
        d_ref[...] = delta
        mo_ref[...] = m_new
        vo_ref[...] = v_new

    row = pl.BlockSpec((tr, C), lambda i: (i, 0))
    shp = jax.ShapeDtypeStruct((R, C), F32)
    return pl.pallas_call(
        body, name=name, grid=(R // tr,),
        in_specs=[row, row, row, row], out_specs=[row, row, row], out_shape=[shp, shp, shp],
        compiler_params=_params(("parallel",)),
    )(g, w, m, v)


def _position():
    return lax.axis_index("x"), lax.axis_index("y"), lax.axis_index("c")


def _slot(px, py, pc):
    return 4 * px + 2 * py + pc


def _all_gather(shards, name):
    n = len(shards)

    def body(*refs):
        ins, outs = refs[:n], refs[n:2 * n]
        send_sems, recv_sems, local_sems = refs[2 * n:]
        x, y, c = _position()
        me, sibling = (x, y, c), (x, y, 1 - c)
        chips = [(1 - x, y), (x, 1 - y), (1 - x, 1 - y)]

        def copy(a, k, block, to, src=None):
            dst = outs[a].at[_slot(*block)]
            return pltpu.make_async_remote_copy(
                src_ref=dst if src is None else src, dst_ref=dst,
                send_sem=send_sems.at[a, k], recv_sem=recv_sems.at[a, k],
                device_id=to, device_id_type=MESH)

        mine = [pltpu.make_async_copy(ins[a], outs[a].at[_slot(*me)], local_sems.at[a]) for a in range(n)]
        for cp in mine:
            cp.start()
        first = []
        for a in range(n):
            first.append(copy(a, 0, me, sibling, src=ins[a]))
            first += [copy(a, 1 + j, me, (*chip, c), src=ins[a]) for j, chip in enumerate(chips)]
        for cp in first:
            cp.start()
        passed = []
        for j, chip in enumerate(chips):
            for a in range(n):
                copy(a, 1 + j, (*chip, c), me).wait_recv()
                fwd = copy(a, 4 + j, (*chip, c), sibling)
                fwd.start()
                passed.append(fwd)
        for a in range(n):
            copy(a, 0, sibling, me).wait_recv()
            for j, chip in enumerate(chips):
                copy(a, 4 + j, (*chip, 1 - c), me).wait_recv()
        for cp in first + passed:
            cp.wait_send()
        for cp in mine:
            cp.wait()

    return pl.pallas_call(
        body, name=name,
        in_specs=[ANY] * n, out_specs=[ANY] * n,
        out_shape=[jax.ShapeDtypeStruct((N_DEV,) + s.shape, s.dtype) for s in shards],
        scratch_shapes=[pltpu.SemaphoreType.DMA((n, 7)), pltpu.SemaphoreType.DMA((n, 7)),
                        pltpu.SemaphoreType.DMA((n,))],
    )(*shards)


def _peer(x, y, c, k):
    return (1 - x if k & 4 else x, 1 - y if k & 2 else y, 1 - c if k & 1 else c)


def _exchange(parts, name):
    n = len(parts)

    def body(*refs):
        ins, outs = refs[:n], refs[n:2 * n]
        send_sems, recv_sems, local_sems = refs[2 * n:]
        x, y, c = _position()
        me = _slot(x, y, c)
        mine = [pltpu.make_async_copy(ins[a].at[me], outs[a].at[me], local_sems.at[a]) for a in range(n)]
        for cp in mine:
            cp.start()
        sends = []
        for k in range(1, N_DEV):
            to = _peer(x, y, c, k)
            for a in range(n):
                cp = pltpu.make_async_remote_copy(
                    src_ref=ins[a].at[_slot(*to)], dst_ref=outs[a].at[me],
                    send_sem=send_sems.at[a, k - 1], recv_sem=recv_sems.at[a, k - 1],
                    device_id=to, device_id_type=MESH)
                cp.start()
                sends.append(cp)
        for k in range(1, N_DEV):
            frm = _peer(x, y, c, k)
            for a in range(n):
                pltpu.make_async_remote_copy(
                    src_ref=ins[a].at[_slot(*frm)], dst_ref=outs[a].at[_slot(*frm)],
                    send_sem=send_sems.at[a, k - 1], recv_sem=recv_sems.at[a, k - 1],
                    device_id=frm, device_id_type=MESH).wait_recv()
        for cp in sends:
            cp.wait_send()
        for cp in mine:
            cp.wait()

    return pl.pallas_call(
        body, name=name,
        in_specs=[ANY] * n, out_specs=[ANY] * n,
        out_shape=[jax.ShapeDtypeStruct(p.shape, p.dtype) for p in parts],
        scratch_shapes=[pltpu.SemaphoreType.DMA((n, 7)), pltpu.SemaphoreType.DMA((n, 7)),
                        pltpu.SemaphoreType.DMA((n,))],
    )(*parts)


def _all_reduce_small(blob, name):
    R, C = blob.shape

    def body(in_ref, out_ref, gath, send_sems, recv_sems):
        x, y, c = _position()
        me = _slot(x, y, c)
        gath[me] = in_ref[...]
        sends = []
        for k in range(1, N_DEV):
            to = _peer(x, y, c, k)
            cp = pltpu.make_async_remote_copy(
                src_ref=in_ref, dst_ref=gath.at[me],
                send_sem=send_sems.at[k - 1], recv_sem=recv_sems.at[k - 1],
                device_id=to, device_id_type=MESH)
            cp.start()
            sends.append(cp)
        for k in range(1, N_DEV):
            frm = _peer(x, y, c, k)
            pltpu.make_async_remote_copy(
                src_ref=in_ref, dst_ref=gath.at[_slot(*frm)],
                send_sem=send_sems.at[k - 1], recv_sem=recv_sems.at[k - 1],
                device_id=frm, device_id_type=MESH).wait_recv()
        for cp in sends:
            cp.wait_send()
        total = gath[0]
        for s in range(1, N_DEV):
            total = total + gath[s]
        out_ref[...] = total

    return pl.pallas_call(
        body, name=name,
        in_specs=[pl.BlockSpec(memory_space=pltpu.VMEM)],
        out_specs=pl.BlockSpec(memory_space=pltpu.VMEM),
        out_shape=jax.ShapeDtypeStruct((R, C), F32),
        scratch_shapes=[pltpu.VMEM((N_DEV, R, C), F32), pltpu.SemaphoreType.DMA((7,)),
                        pltpu.SemaphoreType.DMA((7,))],
        compiler_params=pltpu.CompilerParams(vmem_limit_bytes=VMEM_LIMIT),
    )(blob)


SMALL_VECS = ("ffn1_pre_g", "ffn1_post_g", "mix_pre_g", "sgu_ln_g", "sgu_ln_b", "mix_post_g", "ffn2_pre_g",
              "ffn2_post_g")
ROW_BS = len(SMALL_VECS)
ROW_BF = ROW_BS + 1
ROW_LOSS = ROW_BF + 1
ROW_WS = 16
BLOB_ROWS = ROW_WS + SGU_LEN


def _pack_small(vals, D, loss_row=None):
    rows = [vals[n].reshape(1, D) for n in SMALL_VECS]
    rows.append(vals["sgu_b_s"].reshape(1, D))
    rows.append(jnp.pad(vals["b_forget"].reshape(1, N_HEADS), ((0, 0), (0, D - N_HEADS))))
    rows.append(jnp.zeros((1, D), F32) if loss_row is None else loss_row)
    rows.append(jnp.zeros((ROW_WS - ROW_LOSS - 1, D), F32))
    rows.append(vals["sgu_w_s"].reshape(SGU_LEN, D))
    return jnp.concatenate(rows, axis=0)


def _unpack_small(blob, D):
    out = {n: blob[r:r + 1] for r, n in enumerate(SMALL_VECS)}
    out["sgu_b_s"] = blob[ROW_BS].reshape(1, N_GROUPS, SGU_LEN)
    out["b_forget"] = blob[ROW_BF, :N_HEADS].reshape(1, N_HEADS)
    out["sgu_w_s"] = blob[ROW_WS:].reshape(1, N_GROUPS, SGU_LEN, SGU_LEN)
    return out


WEIGHT_NAMES = ("ffn1_pre_g", "ffn1_w_gate", "ffn1_w_up", "ffn1_w_down", "ffn1_post_g", "mix_pre_g", "w_in",
                "b_forget", "sgu_ln_g", "sgu_ln_b", "sgu_w_s", "sgu_b_s", "w_out", "mix_post_g", "ffn2_pre_g",
                "ffn2_w_gate", "ffn2_w_up", "ffn2_w_down", "ffn2_post_g")
BIG_NAMES = ("ffn1_w_gate", "ffn1_w_up", "ffn1_w_down", "w_in", "w_out", "ffn2_w_gate", "ffn2_w_up", "ffn2_w_down")


def _local_step(x, target, small, big):
    T, D = x.shape
    W = N_HEADS * HEAD_DIM
    vec = lambda n: small[n].reshape(1, D)

    w_in_all = big["w_in"]
    in_width = N_DEV * w_in_all.shape[2]
    w_in = w_in_all.transpose(1, 0, 2).reshape(D, in_width)
    col_f = 3 * W
    col_u = col_f + N_HEADS
    seg_starts = (0, W, 2 * W, col_u, col_u + W, col_u + 2 * W, col_u + 3 * W)
    w7 = jnp.stack([w_in[:, s:s + W] for s in seg_starts])
    wf = jnp.pad(w_in[:, col_f:col_u], ((0, 0), (0, LANES - N_HEADS)))
    w_out = big["w_out"].reshape(D, D)
    b_pad = jnp.pad(small["b_forget"].reshape(1, N_HEADS), ((0, 0), (0, LANES - N_HEADS)))
    lng, lnb = vec("sgu_ln_g"), vec("sgu_ln_b")
    ws = small["sgu_w_s"].reshape(N_GROUPS, SGU_LEN, SGU_LEN)
    bs = small["sgu_b_s"].reshape(N_GROUPS, SGU_LEN, 1)

    x1, y1, gate1, up1 = _ffn_fwd(x, vec("ffn1_pre_g"), big["ffn1_w_gate"], big["ffn1_w_up"], big["ffn1_w_down"],
                                  vec("ffn1_post_g"), "ffn1_fwd")
    z7, f_logit, h2b = _mix_in_fwd(x1, vec("mix_pre_g"), w7, wf, "mix_in_fwd")
    c = _forget_cumsum(f_logit, b_pad, "forget_cumsum")
    c_heads = c[:, :N_HEADS].T
    ta, _, n_chunks = _attn_geometry(T)
    c_chunks = c_heads.reshape(N_HEADS, n_chunks, 1, ta)
    c_col = c_heads[:, :, None]
    o_a, lse = _attn_fwd_loop(z7, c_chunks, "attn_fwd")
    x2, p, merged_b = _mix_out_fwd(z7, o_a, x1, lng, lnb, ws, bs, w_out, vec("mix_post_g"), "mix_out_fwd")
    x3, y2, gate2, up2 = _ffn_fwd(x2, vec("ffn2_pre_g"), big["ffn2_w_gate"], big["ffn2_w_up"], big["ffn2_w_down"],
                                  vec("ffn2_post_g"), "ffn2_fwd")
    dy, loss_lanes = _loss_head(x3, target, "loss_head")

    grads_small = {}
    parts = {}

    dx2, h3b, dy2b, act2, dgate2, dup2, dgpre, dgpost = _ffn_bwd(
        dy, x2, y2, gate2, up2, vec("ffn2_pre_g"), big["ffn2_w_gate"], big["ffn2_w_up"], big["ffn2_w_down"],
        vec("ffn2_post_g"), "ffn2_bwd")
    grads_small["ffn2_pre_g"] = jnp.sum(dgpre, axis=0)
    grads_small["ffn2_post_g"] = jnp.sum(dgpost, axis=0)
    parts["ffn2_w_gate"] = _wgrad(h3b, dgate2, "ffn2_wgrad_gate", shard_cols=True)
    parts["ffn2_w_up"] = _wgrad(h3b, dup2, "ffn2_wgrad_up", shard_cols=True)
    parts["ffn2_w_down"] = _wgrad(act2, dy2b, "ffn2_wgrad_down").reshape(big["ffn2_w_down"].shape)

    dpb, dob, dvec, dz4, dgp, dlng, dlnb, dws, dbs = _mix_out_bwd(
        dx2, p, z7, o_a, lng, lnb, ws, bs, w_out, vec("mix_post_g"), "mix_out_bwd")
    grads_small["mix_post_g"] = dgp
    grads_small["sgu_ln_g"] = dlng
    grads_small["sgu_ln_b"] = dlnb
    grads_small["sgu_w_s"] = dws
    grads_small["sgu_b_s"] = dbs
    parts["w_out"] = _wgrad(merged_b, dpb, "w_out_wgrad").reshape(big["w_out"].shape)
    lse_chunks = lse.reshape(N_HEADS, n_chunks, 1, ta)
    d_chunks = dvec.reshape(N_HEADS, n_chunks, 1, ta)
    dk, dv, dc = _attn_bwd_kv_loop(z7, dob, c_col, lse_chunks, d_chunks, "attn_bwd_kv")
    dq, dc_q = _attn_bwd_q_loop(z7, dob, c_chunks, lse, dvec, "attn_bwd_q")
    dc_pad = jnp.pad((dc + dc_q).reshape(N_HEADS, T).T, ((0, 0), (0, LANES - N_HEADS)))
    dfb, dbf = _forget_bwd(dc_pad, f_logit, b_pad, "forget_bwd")
    grads_small["b_forget"] = dbf[:, :N_HEADS]
    segs = [(dq, None), (dk, None), (dv, None), (dz4, 0), (dz4, 1), (dz4, 2), (dz4, 3)]
    dx1, dgm = _mix_in_bwd(dx2, x1, vec("mix_pre_g"), segs, dfb, w7, wf, "mix_in_bwd")
    grads_small["mix_pre_g"] = jnp.sum(dgm, axis=0)
    seg_mats = [dq, dk, dv, dz4[0], dz4[1], dz4[2], dz4[3]]
    dw_seg = [_wgrad(h2b, sm, "w_in_wgrad_%d" % q) for q, sm in enumerate(seg_mats)]
    dwf = _wgrad(h2b, dfb, "w_in_wgrad_f")[:, :N_HEADS]
    dw_in = jnp.concatenate(dw_seg[:3] + [dwf] + dw_seg[3:], axis=1)
    parts["w_in"] = dw_in.reshape(D, N_DEV, in_width // N_DEV).transpose(1, 0, 2)

    dx0, h1b, dy1b, act1, dgate1, dup1, dgpre1, dgpost1 = _ffn_bwd(
        dx1, x, y1, gate1, up1, vec("ffn1_pre_g"), big["ffn1_w_gate"], big["ffn1_w_up"], big["ffn1_w_down"],
        vec("ffn1_post_g"), "ffn1_bwd")
    grads_small["ffn1_pre_g"] = jnp.sum(dgpre1, axis=0)
    grads_small["ffn1_post_g"] = jnp.sum(dgpost1, axis=0)
    parts["ffn1_w_gate"] = _wgrad(h1b, dgate1, "ffn1_wgrad_gate", shard_cols=True)
    parts["ffn1_w_up"] = _wgrad(h1b, dup1, "ffn1_wgrad_up", shard_cols=True)
    parts["ffn1_w_down"] = _wgrad(act1, dy1b, "ffn1_wgrad_down").reshape(big["ffn1_w_down"].shape)

    loss_row = jnp.pad(loss_lanes, ((0, 0), (0, D - LANES)))
    return loss_row, dx0, grads_small, parts


def kernel(x, ffn1_pre_g, ffn1_w_gate, ffn1_w_up, ffn1_w_down, ffn1_post_g, mix_pre_g, w_in, b_forget, sgu_ln_g, sgu_ln_b, sgu_w_s, sgu_b_s, w_out, mix_post_g, ffn2_pre_g, ffn2_w_gate, ffn2_w_up, ffn2_w_down, ffn2_post_g, loss_target, m_ffn1_pre_g, m_ffn1_w_gate, m_ffn1_w_up, m_ffn1_w_down, m_ffn1_post_g, m_mix_pre_g, m_w_in, m_b_forget, m_sgu_ln_g, m_sgu_ln_b, m_sgu_w_s, m_sgu_b_s, m_w_out, m_mix_post_g, m_ffn2_pre_g, m_ffn2_w_gate, m_ffn2_w_up, m_ffn2_w_down, m_ffn2_post_g, v_ffn1_pre_g, v_ffn1_w_gate, v_ffn1_w_up, v_ffn1_w_down, v_ffn1_post_g, v_mix_pre_g, v_w_in, v_b_forget, v_sgu_ln_g, v_sgu_ln_b, v_sgu_w_s, v_sgu_b_s, v_w_out, v_mix_post_g, v_ffn2_pre_g, v_ffn2_w_gate, v_ffn2_w_up, v_ffn2_w_down, v_ffn2_post_g):
    weights = dict(zip(WEIGHT_NAMES, (ffn1_pre_g, ffn1_w_gate, ffn1_w_up, ffn1_w_down, ffn1_post_g, mix_pre_g, w_in,
                                      b_forget, sgu_ln_g, sgu_ln_b, sgu_w_s, sgu_b_s, w_out, mix_post_g, ffn2_pre_g,
                                      ffn2_w_gate, ffn2_w_up, ffn2_w_down, ffn2_post_g)))
    mom1 = dict(zip(WEIGHT_NAMES, (m_ffn1_pre_g, m_ffn1_w_gate, m_ffn1_w_up, m_ffn1_w_down, m_ffn1_post_g,
                                   m_mix_pre_g, m_w_in, m_b_forget, m_sgu_ln_g, m_sgu_ln_b, m_sgu_w_s, m_sgu_b_s,
                                   m_w_out, m_mix_post_g, m_ffn2_pre_g, m_ffn2_w_gate, m_ffn2_w_up, m_ffn2_w_down,
                                   m_ffn2_post_g)))
    mom2 = dict(zip(WEIGHT_NAMES, (v_ffn1_pre_g, v_ffn1_w_gate, v_ffn1_w_up, v_ffn1_w_down, v_ffn1_post_g,
                                   v_mix_pre_g, v_w_in, v_b_forget, v_sgu_ln_g, v_sgu_ln_b, v_sgu_w_s, v_sgu_b_s,
                                   v_w_out, v_mix_post_g, v_ffn2_pre_g, v_ffn2_w_gate, v_ffn2_w_up, v_ffn2_w_down,
                                   v_ffn2_post_g)))
    D = x.shape[-1]
    small_names = [n for n in WEIGHT_NAMES if n not in BIG_NAMES]

    gathered = _all_gather([weights[n][0].astype(BF16) for n in BIG_NAMES], "weights_all_gather")
    big = dict(zip(BIG_NAMES, gathered))
    small = {n: weights[n] for n in small_names}
    loss_row, grad_x, grads_small, parts = _local_step(x[0], loss_target[0], small, big)

    received = _exchange([parts[n] for n in BIG_NAMES], "grad_exchange")
    out = {}
    for n, rcv in zip(BIG_NAMES, received):
        g, d, m_new, v_new = _sum_adamw(rcv, weights[n][0], mom1[n][0], mom2[n][0], "adamw_" + n)
        out[n] = tuple(a[None] for a in (g, d, m_new, v_new))

    blob = _all_reduce_small(_pack_small(grads_small, D, loss_row), "small_all_reduce")
    d_blob, m_blob, v_blob = _adamw(blob, _pack_small(small, D), _pack_small({n: mom1[n] for n in small_names}, D),
                                    _pack_small({n: mom2[n] for n in small_names}, D), "adamw_small")
    unpacked = [_unpack_small(b, D) for b in (blob, d_blob, m_blob, v_blob)]
    for n in small_names:
        out[n] = tuple(u[n].reshape(weights[n].shape) for u in unpacked)

    loss = blob[ROW_LOSS, 0]
    result = [loss, grad_x[None]]
    for k in range(4):
        result += [out[n][k] for n in WEIGHT_NAMES]
    return tuple(result)
```

```python
import functools

import numpy as np
import jax
import jax.numpy as jnp
from jax import lax
from jax.experimental import pallas as pl
from jax.experimental.pallas import tpu as pltpu

F32 = jnp.float32
BF16 = jnp.bfloat16

RMS_EPS = 1e-6
LN_EPS = 1e-5
HEAD_DIM = 128
N_HEADS = 8
GROUP_DIM = 128
N_GROUPS = 8
SGU_LEN = 128
CHUNK = 64
N_DEV = 8
LANES = 128
VMEM_LIMIT = 56 * 1024 * 1024
NEG_BIG = -1e30

ADAM_LR = 0.001
ADAM_B1 = 0.9
ADAM_B2 = 0.999
ADAM_EPS = 1e-08
ADAM_WD = 0.01
ADAM_STEP = 10

MESH = pl.DeviceIdType.MESH
ANY = pl.BlockSpec(memory_space=pl.ANY)


def _blk(n, pref):
    return pref if (n >= pref and n % pref == 0) else n


def _mm(a, b):
    return jnp.dot(a, b, preferred_element_type=F32)


def _mm_nt(a, b):
    return lax.dot_general(a, b, (((1,), (1,)), ((), ())), preferred_element_type=F32)


def _mm_tn(a, b):
    return lax.dot_general(a, b, (((0,), (0,)), ((), ())), preferred_element_type=F32)


def _params(sem):
    return pltpu.CompilerParams(dimension_semantics=sem, vmem_limit_bytes=VMEM_LIMIT)


def _gelu(x):
    return 0.5 * x * (1.0 + lax.erf(x * np.float32(1.0 / np.sqrt(2.0))))


def _gelu_grad(x):
    cdf = 0.5 * (1.0 + lax.erf(x * np.float32(1.0 / np.sqrt(2.0))))
    return cdf + x * jnp.exp(-0.5 * x * x) * np.float32(1.0 / np.sqrt(2.0 * np.pi))


def _rms_scale(v):
    return lax.rsqrt(jnp.mean(v * v, axis=-1, keepdims=True) + RMS_EPS)


def _rms_bwd(dy, xhat, r, g):
    dxh = dy * g
    return r * (dxh - xhat * jnp.mean(dxh * xhat, axis=-1, keepdims=True))


def _ffn_fwd(x, g_pre, wg, wu, wd, g_post, name):
    T, D = x.shape
    ns, _, fs = wg.shape
    tm = _blk(T, 512)

    def body(x_ref, gpre_ref, wg_ref, wu_ref, wd_ref, gpost_ref, xo_ref, y_ref, g_ref, u_ref, h_scr, acc_scr):
        j = pl.program_id(1)

        @pl.when(j == 0)
        def _():
            xv = x_ref[...]
            h_scr[...] = (xv * _rms_scale(xv) * gpre_ref[...]).astype(BF16)
            acc_scr[...] = jnp.zeros_like(acc_scr)

        h = h_scr[...]
        gg = _mm(h, wg_ref[...])
        uu = _mm(h, wu_ref[...])
        a = gg * jax.nn.sigmoid(gg) * uu
        g_ref[...] = gg.astype(BF16)
        u_ref[...] = uu.astype(BF16)
        acc_scr[...] += _mm(a.astype(BF16), wd_ref[...])

        @pl.when(j == ns - 1)
        def _():
            y = acc_scr[...]
            y_ref[...] = y
            xo_ref[...] = x_ref[...] + 0.5 * (y * _rms_scale(y) * gpost_ref[...])

    row = pl.BlockSpec((tm, D), lambda i, j: (i, 0))
    vec = pl.BlockSpec((1, D), lambda i, j: (0, 0))
    return pl.pallas_call(
        body, name=name, grid=(T // tm, ns),
        in_specs=[row, vec,
                  pl.BlockSpec((None, D, fs), lambda i, j: (j, 0, 0)),
                  pl.BlockSpec((None, D, fs), lambda i, j: (j, 0, 0)),
                  pl.BlockSpec((None, fs, D), lambda i, j: (j, 0, 0)),
                  vec],
        out_specs=[row, row,
                   pl.BlockSpec((tm, fs), lambda i, j: (i, j)),
                   pl.BlockSpec((tm, fs), lambda i, j: (i, j))],
        out_shape=[jax.ShapeDtypeStruct((T, D), F32), jax.ShapeDtypeStruct((T, D), F32),
                   jax.ShapeDtypeStruct((T, ns * fs), BF16), jax.ShapeDtypeStruct((T, ns * fs), BF16)],
        scratch_shapes=[pltpu.VMEM((tm, D), BF16), pltpu.VMEM((tm, D), F32)],
        compiler_params=_params(("parallel", "arbitrary")),
    )(x, g_pre, wg, wu, wd, g_post)


def _ffn_bwd(dxo, x, y, gate, up, g_pre, wg, wu, wd, g_post, name):
    T, D = x.shape
    ns, _, fs = wg.shape
    tm = _blk(T, 512)
    n_i = T // tm

    def body(dxo_ref, x_ref, y_ref, g_ref, u_ref, gpre_ref, wg_ref, wu_ref, wd_ref, gpost_ref,
             dx_ref, hb_ref, dyb_ref, ab_ref, dgb_ref, dub_ref, dgpre_ref, dgpost_ref, dy_scr, acc_scr):
        j = pl.program_id(1)

        @pl.when(j == 0)
        def _():
            yv = y_ref[...]
            s = _rms_scale(yv)
            n = yv * s
            dn = 0.5 * dxo_ref[...]
            dgpost_ref[...] = jnp.sum(dn * n, axis=0, keepdims=True)
            dyv = _rms_bwd(dn, n, s, gpost_ref[...]).astype(BF16)
            dy_scr[...] = dyv
            dyb_ref[...] = dyv
            xv = x_ref[...]
            hb_ref[...] = (xv * _rms_scale(xv) * gpre_ref[...]).astype(BF16)
            acc_scr[...] = jnp.zeros_like(acc_scr)

        da = _mm_nt(dy_scr[...], wd_ref[...])
        gg = g_ref[...].astype(F32)
        uu = u_ref[...].astype(F32)
        sg = jax.nn.sigmoid(gg)
        silu = gg * sg
        dgate = (da * uu * (sg * (1.0 + gg * (1.0 - sg)))).astype(BF16)
        dup = (da * silu).astype(BF16)
        ab_ref[...] = (silu * uu).astype(BF16)
        dgb_ref[...] = dgate
        dub_ref[...] = dup
        acc_scr[...] += _mm_nt(dgate, wg_ref[...]) + _mm_nt(dup, wu_ref[...])

        @pl.when(j == ns - 1)
        def _():
            xv = x_ref[...]
            r = _rms_scale(xv)
            xhat = xv * r
            dh = acc_scr[...]
            dgpre_ref[...] = jnp.sum(dh * xhat, axis=0, keepdims=True)
            dx_ref[...] = _rms_bwd(dh, xhat, r, gpre_ref[...]) + dxo_ref[...]

    row = pl.BlockSpec((tm, D), lambda i, j: (i, 0))
    vec = pl.BlockSpec((1, D), lambda i, j: (0, 0))
    wide = pl.BlockSpec((tm, fs), lambda i, j: (i, j))
    part = pl.BlockSpec((None, 1, D), lambda i, j: (i, 0, 0))
    F = ns * fs
    return pl.pallas_call(
        body, name=name, grid=(n_i, ns),
        in_specs=[row, row, row, wide, wide, vec,
                  pl.BlockSpec((None, D, fs), lambda i, j: (j, 0, 0)),
                  pl.BlockSpec((None, D, fs), lambda i, j: (j, 0, 0)),
                  pl.BlockSpec((None, fs, D), lambda i, j: (j, 0, 0)),
                  vec],
        out_specs=[row, row, row, wide, wide, wide, part, part],
        out_shape=[jax.ShapeDtypeStruct((T, D), F32), jax.ShapeDtypeStruct((T, D), BF16),
                   jax.ShapeDtypeStruct((T, D), BF16), jax.ShapeDtypeStruct((T, F), BF16),
                   jax.ShapeDtypeStruct((T, F), BF16), jax.ShapeDtypeStruct((T, F), BF16),
                   jax.ShapeDtypeStruct((n_i, 1, D), F32), jax.ShapeDtypeStruct((n_i, 1, D), F32)],
        scratch_shapes=[pltpu.VMEM((tm, D), BF16), pltpu.VMEM((tm, D), F32)],
        compiler_params=_params(("parallel", "arbitrary")),
    )(dxo, x, y, gate, up, g_pre, wg, wu, wd, g_post)


def _wgrad(xm, ym, name, shard_cols=False):
    T, M = xm.shape
    _, N = ym.shape
    bm = _blk(M, 1024)
    bn = N // N_DEV if shard_cols else _blk(N, 512)
    tk = _blk(T, 1024)
    n_k = T // tk

    def body(x_ref, y_ref, o_ref, acc_scr):
        k = pl.program_id(2)

        @pl.when(k == 0)
        def _():
            acc_scr[...] = jnp.zeros_like(acc_scr)

        acc_scr[...] += _mm_tn(x_ref[...], y_ref[...])

        @pl.when(k == n_k - 1)
        def _():
            o_ref[...] = acc_scr[...].astype(BF16)

    if shard_cols:
        out_spec = pl.BlockSpec((None, bm, bn), lambda i, j, k: (j, i, 0))
        out_shape = jax.ShapeDtypeStruct((N // bn, M, bn), BF16)
    else:
        out_spec = pl.BlockSpec((bm, bn), lambda i, j, k: (i, j))
        out_shape = jax.ShapeDtypeStruct((M, N), BF16)
    return pl.pallas_call(
        body, name=name, grid=(M // bm, N // bn, n_k),
        in_specs=[pl.BlockSpec((tk, bm), lambda i, j, k: (k, i)),
                  pl.BlockSpec((tk, bn), lambda i, j, k: (k, j))],
        out_specs=out_spec, out_shape=out_shape,
        scratch_shapes=[pltpu.VMEM((bm, bn), F32)],
        compiler_params=_params(("parallel", "parallel", "arbitrary")),
    )(xm, ym)


def _mix_in_fwd(x1, g, w7, wf, name):
    T, D = x1.shape
    n_seg, _, W = w7.shape
    tm = _blk(T, 1024)

    def body(x_ref, g_ref, w_ref, wf_ref, z_ref, f_ref, hb_ref, h_scr):
        s = pl.program_id(1)

        @pl.when(s == 0)
        def _():
            xv = x_ref[...]
            h = (xv * _rms_scale(xv) * g_ref[...]).astype(BF16)
            h_scr[...] = h
            hb_ref[...] = h
            f_ref[...] = _mm(h, wf_ref[...])

        z_ref[...] = _mm(h_scr[...], w_ref[...]).astype(BF16)

    return pl.pallas_call(
        body, name=name, grid=(T // tm, n_seg),
        in_specs=[pl.BlockSpec((tm, D), lambda i, s: (i, 0)),
                  pl.BlockSpec((1, D), lambda i, s: (0, 0)),
                  pl.BlockSpec((None, D, W), lambda i, s: (s, 0, 0)),
                  pl.BlockSpec((D, LANES), lambda i, s: (0, 0))],
        out_specs=[pl.BlockSpec((None, tm, W), lambda i, s: (s, i, 0)),
                   pl.BlockSpec((tm, LANES), lambda i, s: (i, 0)),
                   pl.BlockSpec((tm, D), lambda i, s: (i, 0))],
        out_shape=[jax.ShapeDtypeStruct((n_seg, T, W), BF16), jax.ShapeDtypeStruct((T, LANES), F32),
                   jax.ShapeDtypeStruct((T, D), BF16)],
        scratch_shapes=[pltpu.VMEM((tm, D), BF16)],
        compiler_params=_params(("parallel", "arbitrary")),
    )(x1, g, w7, wf)


def _mix_in_bwd(dx2, x1, g, segs, dfb, w7, wf, name):
    T, D = x1.shape
    n_seg, _, W = w7.shape
    tm = _blk(T, 512)
    n_i = T // tm

    def body(*refs):
        dx2_ref, x_ref, g_ref = refs[:3]
        seg_refs = refs[3:3 + n_seg]
        df_ref, w_ref, wf_ref, dx1_ref, dg_ref, acc_scr = refs[3 + n_seg:]
        s = pl.program_id(1)

        @pl.when(s == 0)
        def _():
            acc_scr[...] = _mm_nt(df_ref[...], wf_ref[...])

        for q in range(n_seg):
            @pl.when(s == q)
            def _(q=q):
                acc_scr[...] += _mm_nt(seg_refs[q][...], w_ref[...])

        @pl.when(s == n_seg - 1)
        def _():
            xv = x_ref[...]
            r = _rms_scale(xv)
            xhat = xv * r
            dh = acc_scr[...]
            dg_ref[...] = jnp.sum(dh * xhat, axis=0, keepdims=True)
            dx1_ref[...] = _rms_bwd(dh, xhat, r, g_ref[...]) + dx2_ref[...]

    row = pl.BlockSpec((tm, D), lambda i, s: (i, 0))
    seg_specs = []
    seg_args = []
    for arr, idx in segs:
        if idx is None:
            seg_specs.append(pl.BlockSpec((tm, W), lambda i, s: (i, 0)))
        else:
            seg_specs.append(pl.BlockSpec((None, tm, W), lambda i, s, idx=idx: (idx, i, 0)))
        seg_args.append(arr)
    return pl.pallas_call(
        body, name=name, grid=(n_i, n_seg),
        in_specs=[row, row, pl.BlockSpec((1, D), lambda i, s: (0, 0))] + seg_specs + [
            pl.BlockSpec((tm, LANES), lambda i, s: (i, 0)),
            pl.BlockSpec((None, D, W), lambda i, s: (s, 0, 0)),
            pl.BlockSpec((D, LANES), lambda i, s: (0, 0))],
        out_specs=[row, pl.BlockSpec((None, 1, D), lambda i, s: (i, 0, 0))],
        out_shape=[jax.ShapeDtypeStruct((T, D), F32), jax.ShapeDtypeStruct((n_i, 1, D), F32)],
        scratch_shapes=[pltpu.VMEM((tm, D), F32)],
        compiler_params=_params(("parallel", "arbitrary")),
    )(dx2, x1, g, *seg_args, dfb, w7, wf)


def _forget_cumsum(f, b_pad, name):
    T, L = f.shape
    tb = _blk(T, 256)

    def body(f_ref, b_ref, c_ref, carry):
        @pl.when(pl.program_id(0) == 0)
        def _():
            carry[...] = jnp.zeros_like(carry)

        lf = jax.nn.log_sigmoid(f_ref[...] + b_ref[...])
        rows = lax.broadcasted_iota(jnp.int32, (tb, tb), 0)
        cols = lax.broadcasted_iota(jnp.int32, (tb, tb), 1)
        tri = (cols <= rows).astype(F32)
        c = jnp.dot(tri, lf, preferred_element_type=F32, precision=lax.Precision.HIGHEST) + carry[...]
        c_ref[...] = c
        carry[...] = c[tb - 1:tb, :]

    return pl.pallas_call(
        body, name=name, grid=(T // tb,),
        in_specs=[pl.BlockSpec((tb, L), lambda i: (i, 0)), pl.BlockSpec((1, L), lambda i: (0, 0))],
        out_specs=pl.BlockSpec((tb, L), lambda i: (i, 0)),
        out_shape=jax.ShapeDtypeStruct((T, L), F32),
        scratch_shapes=[pltpu.VMEM((1, L), F32)],
        compiler_params=_params(("arbitrary",)),
    )(f, b_pad)


def _forget_bwd(dc, f, b_pad, name):
    T, L = f.shape
    tb = _blk(T, 256)
    nb = T // tb

    def body(dc_ref, f_ref, b_ref, df_ref, db_ref, carry):
        @pl.when(pl.program_id(0) == 0)
        def _():
            carry[...] = jnp.zeros_like(carry)
            db_ref[...] = jnp.zeros_like(db_ref)

        rows = lax.broadcasted_iota(jnp.int32, (tb, tb), 0)
        cols = lax.broadcasted_iota(jnp.int32, (tb, tb), 1)
        tri = (cols >= rows).astype(F32)
        r = jnp.dot(tri, dc_ref[...], preferred_element_type=F32, precision=lax.Precision.HIGHEST) + carry[...]
        carry[...] = r[0:1, :]
        df = r * (1.0 - jax.nn.sigmoid(f_ref[...] + b_ref[...]))
        df_ref[...] = df.astype(BF16)
        db_ref[...] += jnp.sum(df, axis=0, keepdims=True)

    rev = pl.BlockSpec((tb, L), lambda i: (nb - 1 - i, 0))
    one = pl.BlockSpec((1, L), lambda i: (0, 0))
    return pl.pallas_call(
        body, name=name, grid=(nb,),
        in_specs=[rev, rev, one], out_specs=[rev, one],
        out_shape=[jax.ShapeDtypeStruct((T, L), BF16), jax.ShapeDtypeStruct((1, L), F32)],
        scratch_shapes=[pltpu.VMEM((1, L), F32)],
        compiler_params=_params(("arbitrary",)),
    )(dc, f, b_pad)


def _attn_fwd(z7, c_row, name):
    _, T, W = z7.shape
    H = W // HEAD_DIM
    ta = _blk(T, 512)
    nq = T // ta
    scale = np.float32(1.0 / np.sqrt(HEAD_DIM))

    def body(q_ref, k_ref, v_ref, crow_ref, o_ref, lse_ref, m_scr, l_scr, acc_scr):
        i = pl.program_id(1)
        j = pl.program_id(2)

        @pl.when(j == 0)
        def _():
            m_scr[...] = jnp.full_like(m_scr, NEG_BIG)
            l_scr[...] = jnp.zeros_like(l_scr)
            acc_scr[...] = jnp.zeros_like(acc_scr)

        def step(diagonal):
            s = _mm_nt(q_ref[...], k_ref[...]) * scale - crow_ref[...]
            if diagonal:
                rows = lax.broadcasted_iota(jnp.int32, (ta, ta), 0)
                cols = lax.broadcasted_iota(jnp.int32, (ta, ta), 1)
                s = jnp.where(cols <= rows, s, NEG_BIG)
            m_prev = m_scr[...]
            m_new = jnp.maximum(m_prev, jnp.max(s, axis=-1, keepdims=True))
            alpha = jnp.exp(m_prev - m_new)
            p = jnp.exp(s - m_new)
            l_scr[...] = alpha * l_scr[...] + jnp.sum(p, axis=-1, keepdims=True)
            acc_scr[...] = alpha * acc_scr[...] + _mm(p.astype(BF16), v_ref[...])
            m_scr[...] = m_new

        @pl.when(j < i)
        def _():
            step(False)

        @pl.when(j == i)
        def _():
            step(True)
            l = l_scr[...]
            o_ref[...] = acc_scr[...] / l
            lse_ref[...] = m_scr[...] + jnp.log(l)

    return pl.pallas_call(
        body, name=name, grid=(H, nq, nq),
        in_specs=[pl.BlockSpec((None, ta, HEAD_DIM), lambda h, i, j: (0, i, h)),
                  pl.BlockSpec((None, ta, HEAD_DIM), lambda h, i, j: (1, jnp.minimum(i, j), h)),
                  pl.BlockSpec((None, ta, HEAD_DIM), lambda h, i, j: (2, jnp.minimum(i, j), h)),
                  pl.BlockSpec((None, 1, ta), lambda h, i, j: (h, 0, jnp.minimum(i, j)))],
        out_specs=[pl.BlockSpec((ta, HEAD_DIM), lambda h, i, j: (i, h)),
                   pl.BlockSpec((None, ta, 1), lambda h, i, j: (h, i, 0))],
        out_shape=[jax.ShapeDtypeStruct((T, W), F32), jax.ShapeDtypeStruct((H, T, 1), F32)],
        scratch_shapes=[pltpu.VMEM((ta, 1), F32), pltpu.VMEM((ta, 1), F32), pltpu.VMEM((ta, HEAD_DIM), F32)],
        compiler_params=_params(("parallel", "parallel", "arbitrary")),
    )(z7, z7, z7, c_row)


def _attn_bwd_kv(z7, dob, c_col, lse_row, d_row, name):
    _, T, W = z7.shape
    H = W // HEAD_DIM
    ta = _blk(T, 512)
    nq = T // ta
    scale = np.float32(1.0 / np.sqrt(HEAD_DIM))

    def body(k_ref, v_ref, q_ref, do_ref, ccol_ref, lse_ref, d_ref, dk_ref, dv_ref, dc_ref, dk_scr, dv_scr, dc_scr):
        j = pl.program_id(1)
        i = pl.program_id(2)

        @pl.when(i == 0)
        def _():
            dk_scr[...] = jnp.zeros_like(dk_scr)
            dv_scr[...] = jnp.zeros_like(dv_scr)
            dc_scr[...] = jnp.zeros_like(dc_scr)

        def step(diagonal):
            q = q_ref[...]
            do = do_ref[...]
            st = _mm_nt(k_ref[...], q) * scale - ccol_ref[...] - lse_ref[...]
            if diagonal:
                rows = lax.broadcasted_iota(jnp.int32, (ta, ta), 0)
                cols = lax.broadcasted_iota(jnp.int32, (ta, ta), 1)
                st = jnp.where(rows <= cols, st, NEG_BIG)
            pt = jnp.exp(st)
            dv_scr[...] += _mm(pt.astype(BF16), do)
            dst = pt * (_mm_nt(v_ref[...], do) - d_ref[...])
            dk_scr[...] += _mm(dst.astype(BF16), q)
            dc_scr[...] += jnp.sum(dst, axis=-1, keepdims=True)

        @pl.when(i > j)
        def _():
            step(False)

        @pl.when(i == j)
        def _():
            step(True)

        @pl.when(i == nq - 1)
        def _():
            dk_ref[...] = (dk_scr[...] * scale).astype(BF16)
            dv_ref[...] = dv_scr[...].astype(BF16)
            dc_ref[...] = -dc_scr[...]

    return pl.pallas_call(
        body, name=name, grid=(H, nq, nq),
        in_specs=[pl.BlockSpec((None, ta, HEAD_DIM), lambda h, j, i: (1, j, h)),
                  pl.BlockSpec((None, ta, HEAD_DIM), lambda h, j, i: (2, j, h)),
                  pl.BlockSpec((None, ta, HEAD_DIM), lambda h, j, i: (0, jnp.maximum(i, j), h)),
                  pl.BlockSpec((ta, HEAD_DIM), lambda h, j, i: (jnp.maximum(i, j), h)),
                  pl.BlockSpec((None, ta, 1), lambda h, j, i: (h, j, 0)),
                  pl.BlockSpec((None, 1, ta), lambda h, j, i: (h, 0, jnp.maximum(i, j))),
                  pl.BlockSpec((None, 1, ta), lambda h, j, i: (h, 0, jnp.maximum(i, j)))],
        out_specs=[pl.BlockSpec((ta, HEAD_DIM), lambda h, j, i: (j, h)),
                   pl.BlockSpec((ta, HEAD_DIM), lambda h, j, i: (j, h)),
                   pl.BlockSpec((None, ta, 1), lambda h, j, i: (h, j, 0))],
        out_shape=[jax.ShapeDtypeStruct((T, W), BF16), jax.ShapeDtypeStruct((T, W), BF16),
                   jax.ShapeDtypeStruct((H, T, 1), F32)],
        scratch_shapes=[pltpu.VMEM((ta, HEAD_DIM), F32), pltpu.VMEM((ta, HEAD_DIM), F32), pltpu.VMEM((ta, 1), F32)],
        compiler_params=_params(("parallel", "parallel", "arbitrary")),
    )(z7, z7, z7, dob, c_col, lse_row, d_row)


def _attn_bwd_q(z7, dob, c_row, lse_col, d_col, name):
    _, T, W = z7.shape
    H = W // HEAD_DIM
    ta = _blk(T, 512)
    nq = T // ta
    scale = np.float32(1.0 / np.sqrt(HEAD_DIM))

    def body(q_ref, k_ref, v_ref, do_ref, crow_ref, lse_ref, d_ref, dq_ref, dc_ref, dq_scr, dc_scr):
        i = pl.program_id(1)
        j = pl.program_id(2)

        @pl.when(j == 0)
        def _():
            dq_scr[...] = jnp.zeros_like(dq_scr)
            dc_scr[...] = jnp.zeros_like(dc_scr)

        def step(diagonal):
            k = k_ref[...]
            do = do_ref[...]
            s = _mm_nt(q_ref[...], k) * scale - crow_ref[...] - lse_ref[...]
            if diagonal:
                rows = lax.broadcasted_iota(jnp.int32, (ta, ta), 0)
                cols = lax.broadcasted_iota(jnp.int32, (ta, ta), 1)
                s = jnp.where(cols <= rows, s, NEG_BIG)
            p = jnp.exp(s)
            ds = p * (_mm_nt(do, v_ref[...]) - d_ref[...])
            dq_scr[...] += _mm(ds.astype(BF16), k)
            dc_scr[...] += jnp.sum(ds, axis=-1, keepdims=True)

        @pl.when(j < i)
        def _():
            step(False)

        @pl.when(j == i)
        def _():
            step(True)
            dq_ref[...] = (dq_scr[...] * scale).astype(BF16)
            dc_ref[...] = dc_scr[...]

    return pl.pallas_call(
        body, name=name, grid=(H, nq, nq),
        in_specs=[pl.BlockSpec((None, ta, HEAD_DIM), lambda h, i, j: (0, i, h)),
                  pl.BlockSpec((None, ta, HEAD_DIM), lambda h, i, j: (1, jnp.minimum(i, j), h)),
                  pl.BlockSpec((None, ta, HEAD_DIM), lambda h, i, j: (2, jnp.minimum(i, j), h)),
                  pl.BlockSpec((ta, HEAD_DIM), lambda h, i, j: (i, h)),
                  pl.BlockSpec((None, 1, ta), lambda h, i, j: (h, 0, jnp.minimum(i, j))),
                  pl.BlockSpec((None, ta, 1), lambda h, i, j: (h, i, 0)),
                  pl.BlockSpec((None, ta, 1), lambda h, i, j: (h, i, 0))],
        out_specs=[pl.BlockSpec((ta, HEAD_DIM), lambda h, i, j: (i, h)),
                   pl.BlockSpec((None, ta, 1), lambda h, i, j: (h, i, 0))],
        out_shape=[jax.ShapeDtypeStruct((T, W), BF16), jax.ShapeDtypeStruct((H, T, 1), F32)],
        scratch_shapes=[pltpu.VMEM((ta, HEAD_DIM), F32), pltpu.VMEM((ta, 1), F32)],
        compiler_params=_params(("parallel", "parallel", "arbitrary")),
    )(z7, z7, z7, dob, c_row, lse_col, d_col)


ATTN_TILE = 512
ATTN_CHAINS = 2


def _attn_geometry(T):
    ta = _blk(T, ATTN_TILE)
    nc = ATTN_CHAINS if (T // ta) % ATTN_CHAINS == 0 else 1
    return ta, nc, T // ta


def _causal_tile(ta, keys_on_rows=False):
    rows = lax.broadcasted_iota(jnp.int32, (ta, ta), 0)
    cols = lax.broadcasted_iota(jnp.int32, (ta, ta), 1)
    return rows <= cols if keys_on_rows else cols <= rows


def _chunk(ref, j, ta):
    return ref[pl.ds(pl.multiple_of(j * ta, ta), ta), :]


def _attn_fwd_loop(z7, c_chunks, name):
    _, T, W = z7.shape
    H = W // HEAD_DIM
    ta, nc, n_chunks = _attn_geometry(T)
    scale = np.float32(1.0 / np.sqrt(HEAD_DIM))

    def body(q_ref, k_ref, v_ref, c_ref, o_ref, lse_ref, m_scr, l_scr, acc_scr):
        g = pl.program_id(1)
        m_scr[...] = jnp.full_like(m_scr, NEG_BIG)
        l_scr[...] = jnp.zeros_like(l_scr)
        acc_scr[...] = jnp.zeros_like(acc_scr)

        def update(ch, k, v, crow, diagonal):
            q = q_ref[ch * ta:(ch + 1) * ta, :]
            s = _mm_nt(q, k) * scale - crow
            if diagonal:
                s = jnp.where(_causal_tile(ta), s, NEG_BIG)
            m_prev = m_scr[ch]
            m_new = jnp.maximum(m_prev, jnp.max(s, axis=-1, keepdims=True))
            alpha = jnp.exp(m_prev - m_new)
            p = jnp.exp(s - m_new)
            l_scr[ch] = alpha * l_scr[ch] + jnp.sum(p, axis=-1, keepdims=True)
            acc_scr[ch] = alpha * acc_scr[ch] + _mm(p.astype(BF16), v)
            m_scr[ch] = m_new

        def full_chunk(j, carry):
            k = _chunk(k_ref, j, ta)
            v = _chunk(v_ref, j, ta)
            crow = c_ref[j]
            for ch in range(nc):
                update(ch, k, v, crow, False)
            return carry

        lax.fori_loop(0, nc * g, full_chunk, 0)
        for jj in range(nc):
            j = nc * g + jj
            k = _chunk(k_ref, j, ta)
            v = _chunk(v_ref, j, ta)
            crow = c_ref[j]
            for ch in range(jj, nc):
                update(ch, k, v, crow, ch == jj)
        for ch in range(nc):
            l = l_scr[ch]
            o_ref[ch * ta:(ch + 1) * ta, :] = acc_scr[ch] / l
            lse_ref[ch * ta:(ch + 1) * ta, :] = m_scr[ch] + jnp.log(l)

    tq = nc * ta
    return pl.pallas_call(
        body, name=name, grid=(H, n_chunks // nc),
        in_specs=[pl.BlockSpec((None, tq, HEAD_DIM), lambda h, g: (0, g, h)),
                  pl.BlockSpec((None, T, HEAD_DIM), lambda h, g: (1, 0, h)),
                  pl.BlockSpec((None, T, HEAD_DIM), lambda h, g: (2, 0, h)),
                  pl.BlockSpec((None, n_chunks, 1, ta), lambda h, g: (h, 0, 0, 0))],
        out_specs=[pl.BlockSpec((tq, HEAD_DIM), lambda h, g: (g, h)),
                   pl.BlockSpec((None, tq, 1), lambda h, g: (h, g, 0))],
        out_shape=[jax.ShapeDtypeStruct((T, W), F32), jax.ShapeDtypeStruct((H, T, 1), F32)],
        scratch_shapes=[pltpu.VMEM((nc, ta, 1), F32), pltpu.VMEM((nc, ta, 1), F32),
                        pltpu.VMEM((nc, ta, HEAD_DIM), F32)],
        compiler_params=_params(("parallel", "arbitrary")),
    )(z7, z7, z7, c_chunks)


def _attn_bwd_q_loop(z7, dob, c_chunks, lse_col, d_col, name):
    _, T, W = z7.shape
    H = W // HEAD_DIM
    ta, nc, n_chunks = _attn_geometry(T)
    scale = np.float32(1.0 / np.sqrt(HEAD_DIM))

    def body(q_ref, k_ref, v_ref, do_ref, c_ref, lse_ref, d_ref, dq_ref, dc_ref, dq_scr, dc_scr):
        g = pl.program_id(1)
        dq_scr[...] = jnp.zeros_like(dq_scr)
        dc_scr[...] = jnp.zeros_like(dc_scr)

        def update(ch, k, v, crow, diagonal):
            rows = slice(ch * ta, (ch + 1) * ta)
            do = do_ref[rows, :]
            s = _mm_nt(q_ref[rows, :], k) * scale - crow - lse_ref[rows, :]
            if diagonal:
                s = jnp.where(_causal_tile(ta), s, NEG_BIG)
            p = jnp.exp(s)
            ds = p * (_mm_nt(do, v) - d_ref[rows, :])
            dq_scr[ch] += _mm(ds.astype(BF16), k)
            dc_scr[ch] += jnp.sum(ds, axis=-1, keepdims=True)

        def full_chunk(j, carry):
            k = _chunk(k_ref, j, ta)
            v = _chunk(v_ref, j, ta)
            crow = c_ref[j]
            for ch in range(nc):
                update(ch, k, v, crow, False)
            return carry

        lax.fori_loop(0, nc * g, full_chunk, 0)
        for jj in range(nc):
            j = nc * g + jj
            k = _chunk(k_ref, j, ta)
            v = _chunk(v_ref, j, ta)
            crow = c_ref[j]
            for ch in range(jj, nc):
                update(ch, k, v, crow, ch == jj)
        for ch in range(nc):
            dq_ref[ch * ta:(ch + 1) * ta, :] = (dq_scr[ch] * scale).astype(BF16)
            dc_ref[ch * ta:(ch + 1) * ta, :] = dc_scr[ch]

    tq = nc * ta
    col = pl.BlockSpec((None, tq, 1), lambda h, g: (h, g, 0))
    return pl.pallas_call(
        body, name=name, grid=(H, n_chunks // nc),
        in_specs=[pl.BlockSpec((None, tq, HEAD_DIM), lambda h, g: (0, g, h)),
                  pl.BlockSpec((None, T, HEAD_DIM), lambda h, g: (1, 0, h)),
                  pl.BlockSpec((None, T, HEAD_DIM), lambda h, g: (2, 0, h)),
                  pl.BlockSpec((tq, HEAD_DIM), lambda h, g: (g, h)),
                  pl.BlockSpec((None, n_chunks, 1, ta), lambda h, g: (h, 0, 0, 0)),
                  col, col],
        out_specs=[pl.BlockSpec((tq, HEAD_DIM), lambda h, g: (g, h)), col],
        out_shape=[jax.ShapeDtypeStruct((T, W), BF16), jax.ShapeDtypeStruct((H, T, 1), F32)],
        scratch_shapes=[pltpu.VMEM((nc, ta, HEAD_DIM), F32), pltpu.VMEM((nc, ta, 1), F32)],
        compiler_params=_params(("parallel", "arbitrary")),
    )(z7, z7, z7, dob, c_chunks, lse_col, d_col)


def _attn_bwd_kv_loop(z7, dob, c_col, lse_chunks, d_chunks, name):
    _, T, W = z7.shape
    H = W // HEAD_DIM
    ta, nc, n_chunks = _attn_geometry(T)
    scale = np.float32(1.0 / np.sqrt(HEAD_DIM))

    def body(k_ref, v_ref, q_ref, do_ref, ccol_ref, lse_ref, d_ref, dk_ref, dv_ref, dc_ref, dk_scr, dv_scr, dc_scr):
        g = pl.program_id(1)
        dk_scr[...] = jnp.zeros_like(dk_scr)
        dv_scr[...] = jnp.zeros_like(dv_scr)
        dc_scr[...] = jnp.zeros_like(dc_scr)

        def update(ch, q, do, lse_row, d_row, diagonal):
            rows = slice(ch * ta, (ch + 1) * ta)
            st = _mm_nt(k_ref[rows, :], q) * scale - ccol_ref[rows, :] - lse_row
            if diagonal:
                st = jnp.where(_causal_tile(ta, keys_on_rows=True), st, NEG_BIG)
            pt = jnp.exp(st)
            dv_scr[ch] += _mm(pt.astype(BF16), do)
            dst = pt * (_mm_nt(v_ref[rows, :], do) - d_row)
            dk_scr[ch] += _mm(dst.astype(BF16), q)
            dc_scr[ch] += jnp.sum(dst, axis=-1, keepdims=True)

        for ii in range(nc):
            i = nc * g + ii
            q = _chunk(q_ref, i, ta)
            do = _chunk(do_ref, i, ta)
            for ch in range(0, ii + 1):
                update(ch, q, do, lse_ref[i], d_ref[i], ch == ii)

        def full_chunk(i, carry):
            q = _chunk(q_ref, i, ta)
            do = _chunk(do_ref, i, ta)
            for ch in range(nc):
                update(ch, q, do, lse_ref[i], d_ref[i], False)
            return carry

        lax.fori_loop(nc * (g + 1), n_chunks, full_chunk, 0)
        for ch in range(nc):
            rows = slice(ch * ta, (ch + 1) * ta)
            dk_ref[rows, :] = (dk_scr[ch] * scale).astype(BF16)
            dv_ref[rows, :] = dv_scr[ch].astype(BF16)
            dc_ref[rows, :] = -dc_scr[ch]

    tk = nc * ta
    chunks = pl.BlockSpec((None, n_chunks, 1, ta), lambda h, g: (h, 0, 0, 0))
    col = pl.BlockSpec((None, tk, 1), lambda h, g: (h, g, 0))
    tile = pl.BlockSpec((tk, HEAD_DIM), lambda h, g: (g, h))
    return pl.pallas_call(
        body, name=name, grid=(H, n_chunks // nc),
        in_specs=[pl.BlockSpec((None, tk, HEAD_DIM), lambda h, g: (1, g, h)),
                  pl.BlockSpec((None, tk, HEAD_DIM), lambda h, g: (2, g, h)),
                  pl.BlockSpec((None, T, HEAD_DIM), lambda h, g: (0, 0, h)),
                  pl.BlockSpec((T, HEAD_DIM), lambda h, g: (0, h)),
                  col, chunks, chunks],
        out_specs=[tile, tile, col],
        out_shape=[jax.ShapeDtypeStruct((T, W), BF16), jax.ShapeDtypeStruct((T, W), BF16),
                   jax.ShapeDtypeStruct((H, T, 1), F32)],
        scratch_shapes=[pltpu.VMEM((nc, ta, HEAD_DIM), F32), pltpu.VMEM((nc, ta, HEAD_DIM), F32),
                        pltpu.VMEM((nc, ta, 1), F32)],
        compiler_params=_params(("parallel", "arbitrary")),
    )(z7, z7, z7, dob, c_col, lse_chunks, d_chunks)


def _chunk_causal_mask():
    rows = lax.broadcasted_iota(jnp.int32, (SGU_LEN, SGU_LEN), 0)
    cols = lax.broadcasted_iota(jnp.int32, (SGU_LEN, SGU_LEN), 1)
    return (cols // CHUNK) <= (rows // CHUNK)


def _sgu_norm_mix(sv, lng_ref, lnb_ref, ws_ref, bs_ref, vn_scr, mixed_scr, vhat_scr=None):
    tm = sv.shape[0]
    vs = _gelu(sv)
    mask = _chunk_causal_mask()
    rstds = []
    for g in range(N_GROUPS):
        lanes = slice(g * GROUP_DIM, (g + 1) * GROUP_DIM)
        blk = vs[:, lanes]
        cen = blk - jnp.mean(blk, axis=-1, keepdims=True)
        rstd = lax.rsqrt(jnp.mean(cen * cen, axis=-1, keepdims=True) + LN_EPS)
        vhat = cen * rstd
        rstds.append(rstd)
        if vhat_scr is not None:
            vhat_scr[:, lanes] = vhat
        vn_scr[:, lanes] = (vhat * lng_ref[:, lanes] + lnb_ref[:, lanes]).astype(BF16)
        wm = jnp.where(mask, ws_ref[g], 0.0).astype(BF16)
        for w in range(tm // SGU_LEN):
            rows = slice(w * SGU_LEN, (w + 1) * SGU_LEN)
            mixed_scr[rows, lanes] = _mm(wm, vn_scr[rows, lanes]) + bs_ref[g]
    return rstds


def _mix_out_fwd(z7, o_a, x1, lng, lnb, ws, bs, w_out, g_post, name):
    _, T, W = z7.shape
    D = x1.shape[1]
    tm = _blk(T, 256)

    def body(u_ref, sv_ref, ga_ref, gb_ref, oa_ref, x1_ref, lng_ref, lnb_ref, ws_ref, bs_ref, wo_ref, gp_ref,
             x2_ref, p_ref, mb_ref, vn_scr, mixed_scr):
        _sgu_norm_mix(sv_ref[...].astype(F32), lng_ref, lnb_ref, ws_ref, bs_ref, vn_scr, mixed_scr)
        o_b = _gelu(u_ref[...].astype(F32)) * mixed_scr[...]
        merged = (jax.nn.sigmoid(ga_ref[...].astype(F32)) * oa_ref[...]
                  + jax.nn.sigmoid(gb_ref[...].astype(F32)) * o_b).astype(BF16)
        mb_ref[...] = merged
        p = _mm(merged, wo_ref[...])
        p_ref[...] = p
        x2_ref[...] = x1_ref[...] + p * _rms_scale(p) * gp_ref[...]

    def seg(idx):
        return pl.BlockSpec((None, tm, W), lambda i, idx=idx: (idx, i, 0))

    row = pl.BlockSpec((tm, D), lambda i: (i, 0))
    vec = pl.BlockSpec((1, D), lambda i: (0, 0))
    return pl.pallas_call(
        body, name=name, grid=(T // tm,),
        in_specs=[seg(3), seg(4), seg(5), seg(6), row, row, vec, vec,
                  pl.BlockSpec((N_GROUPS, SGU_LEN, SGU_LEN), lambda i: (0, 0, 0)),
                  pl.BlockSpec((N_GROUPS, SGU_LEN, 1), lambda i: (0, 0, 0)),
                  pl.BlockSpec((D, D), lambda i: (0, 0)), vec],
        out_specs=[row, row, row],
        out_shape=[jax.ShapeDtypeStruct((T, D), F32), jax.ShapeDtypeStruct((T, D), F32),
                   jax.ShapeDtypeStruct((T, D), BF16)],
        scratch_shapes=[pltpu.VMEM((tm, W), BF16), pltpu.VMEM((tm, W), F32)],
        compiler_params=_params(("parallel",)),
    )(z7, z7, z7, z7, o_a, x1, lng, lnb, ws, bs, w_out, g_post)


def _mix_out_bwd(dx2, p, z7, o_a, lng, lnb, ws, bs, w_out, g_post, name):
    _, T, W = z7.shape
    D = dx2.shape[1]
    tm = _blk(T, 256)
    n_w = tm // SGU_LEN

    def body(dx2_ref, p_ref, u_ref, sv_ref, ga_ref, gb_ref, oa_ref, lng_ref, lnb_ref, ws_ref, bs_ref, wo_ref, gp_ref,
             dpb_ref, dob_ref, dvec_ref, dz_ref, dgp_ref, dlng_ref, dlnb_ref, dws_ref, dbs_ref,
             vn_scr, mixed_scr, vhat_scr, dmix_scr, dvn_scr):
        @pl.when(pl.program_id(0) == 0)
        def _():
            dgp_ref[...] = jnp.zeros_like(dgp_ref)
            dlng_ref[...] = jnp.zeros_like(dlng_ref)
            dlnb_ref[...] = jnp.zeros_like(dlnb_ref)
            dws_ref[...] = jnp.zeros_like(dws_ref)
            dbs_ref[...] = jnp.zeros_like(dbs_ref)

        pv = p_ref[...]
        s = _rms_scale(pv)
        n = pv * s
        dn = dx2_ref[...]
        dgp_ref[...] += jnp.sum(dn * n, axis=0, keepdims=True)
        dpb = _rms_bwd(dn, n, s, gp_ref[...]).astype(BF16)
        dpb_ref[...] = dpb
        dmerged = _mm_nt(dpb, wo_ref[...])

        sv = sv_ref[...].astype(F32)
        rstds = _sgu_norm_mix(sv, lng_ref, lnb_ref, ws_ref, bs_ref, vn_scr, mixed_scr, vhat_scr)
        u_pre = u_ref[...].astype(F32)
        u = _gelu(u_pre)
        mixed = mixed_scr[...]
        sa = jax.nn.sigmoid(ga_ref[...].astype(F32))
        sb = jax.nn.sigmoid(gb_ref[...].astype(F32))
        oa = oa_ref[...]
        do_a = (dmerged * sa).astype(BF16)
        dob_ref[...] = do_a
        prod = do_a.astype(F32) * oa
        for h in range(N_HEADS):
            dvec_ref[h] = jnp.sum(prod[:, h * HEAD_DIM:(h + 1) * HEAD_DIM], axis=-1, keepdims=True)
        dz_ref[2] = (dmerged * oa * (sa * (1.0 - sa))).astype(BF16)
        dz_ref[3] = (dmerged * (u * mixed) * (sb * (1.0 - sb))).astype(BF16)
        do_b = dmerged * sb
        dz_ref[0] = (do_b * mixed * _gelu_grad(u_pre)).astype(BF16)
        dmix_scr[...] = do_b * u

        mask = _chunk_causal_mask()
        for g in range(N_GROUPS):
            lanes = slice(g * GROUP_DIM, (g + 1) * GROUP_DIM)
            wm = jnp.where(mask, ws_ref[g], 0.0).astype(BF16)
            dws = jnp.zeros((SGU_LEN, SGU_LEN), F32)
            dbs = jnp.zeros((SGU_LEN, 1), F32)
            for w in range(n_w):
                rows = slice(w * SGU_LEN, (w + 1) * SGU_LEN)
                dmix = dmix_scr[rows, lanes]
                dmix_b = dmix.astype(BF16)
                dvn_scr[rows, lanes] = _mm_tn(wm, dmix_b)
                dws = dws + _mm_nt(dmix_b, vn_scr[rows, lanes])
                dbs = dbs + jnp.sum(dmix, axis=-1, keepdims=True)
            dws_ref[g] += jnp.where(mask, dws, 0.0)
            dbs_ref[g] += dbs
            dvn = dvn_scr[:, lanes]
            vhat = vhat_scr[:, lanes]
            dlng_ref[:, lanes] += jnp.sum(dvn * vhat, axis=0, keepdims=True)
            dlnb_ref[:, lanes] += jnp.sum(dvn, axis=0, keepdims=True)
            dvh = dvn * lng_ref[:, lanes]
            dvs = rstds[g] * (dvh - jnp.mean(dvh, axis=-1, keepdims=True)
                              - vhat * jnp.mean(dvh * vhat, axis=-1, keepdims=True))
            dvn_scr[:, lanes] = dvs
        dz_ref[1] = (dvn_scr[...] * _gelu_grad(sv)).astype(BF16)

    def seg(idx):
        return pl.BlockSpec((None, tm, W), lambda i, idx=idx: (idx, i, 0))

    row = pl.BlockSpec((tm, D), lambda i: (i, 0))
    vec = pl.BlockSpec((1, D), lambda i: (0, 0))
    ws_spec = pl.BlockSpec((N_GROUPS, SGU_LEN, SGU_LEN), lambda i: (0, 0, 0))
    bs_spec = pl.BlockSpec((N_GROUPS, SGU_LEN, 1), lambda i: (0, 0, 0))
    return pl.pallas_call(
        body, name=name, grid=(T // tm,),
        in_specs=[row, row, seg(3), seg(4), seg(5), seg(6), row, vec, vec, ws_spec, bs_spec,
                  pl.BlockSpec((D, D), lambda i: (0, 0)), vec],
        out_specs=[row, row, pl.BlockSpec((N_HEADS, tm, 1), lambda i: (0, i, 0)),
                   pl.BlockSpec((4, tm, W), lambda i: (0, i, 0)), vec, vec, vec, ws_spec, bs_spec],
        out_shape=[jax.ShapeDtypeStruct((T, D), BF16), jax.ShapeDtypeStruct((T, W), BF16),
                   jax.ShapeDtypeStruct((N_HEADS, T, 1), F32), jax.ShapeDtypeStruct((4, T, W), BF16),
                   jax.ShapeDtypeStruct((1, D), F32), jax.ShapeDtypeStruct((1, D), F32),
                   jax.ShapeDtypeStruct((1, D), F32),
                   jax.ShapeDtypeStruct((N_GROUPS, SGU_LEN, SGU_LEN), F32),
                   jax.ShapeDtypeStruct((N_GROUPS, SGU_LEN, 1), F32)],
        scratch_shapes=[pltpu.VMEM((tm, W), BF16), pltpu.VMEM((tm, W), F32), pltpu.VMEM((tm, W), F32),
                        pltpu.VMEM((tm, W), F32), pltpu.VMEM((tm, W), F32)],
        compiler_params=_params(("arbitrary",)),
    )(dx2, p, z7, z7, z7, z7, o_a, lng, lnb, ws, bs, w_out, g_post)


def _loss_head(y, target, name):
    T, D = y.shape
    tm = _blk(T, 1024)
    n_i = T // tm

    def body(y_ref, t_ref, dy_ref, loss_ref, acc_scr):
        i = pl.program_id(0)

        @pl.when(i == 0)
        def _():
            acc_scr[...] = jnp.zeros_like(acc_scr)

        e = y_ref[...] - t_ref[...]
        dy_ref[...] = e * np.float32(1.0 / D)
        acc_scr[...] += jnp.sum(e * e, axis=0, keepdims=True)

        @pl.when(i == n_i - 1)
        def _():
            total = jnp.sum(acc_scr[...], axis=-1, keepdims=True) * np.float32(0.5 / D)
            loss_ref[...] = jnp.broadcast_to(total, loss_ref.shape)

    row = pl.BlockSpec((tm, D), lambda i: (i, 0))
    return pl.pallas_call(
        body, name=name, grid=(n_i,),
        in_specs=[row, row],
        out_specs=[row, pl.BlockSpec((1, LANES), lambda i: (0, 0))],
        out_shape=[jax.ShapeDtypeStruct((T, D), F32), jax.ShapeDtypeStruct((1, LANES), F32)],
        scratch_shapes=[pltpu.VMEM((1, D), F32)],
        compiler_params=_params(("arbitrary",)),
    )(y, target)


def _adamw_math(w, g, m, v):
    m_new = ADAM_B1 * m + (1.0 - ADAM_B1) * g
    v_new = ADAM_B2 * v + (1.0 - ADAM_B2) * (g * g)
    m_hat = m_new / np.float32(1.0 - ADAM_B1 ** ADAM_STEP)
    v_hat = v_new / np.float32(1.0 - ADAM_B2 ** ADAM_STEP)
    delta = -ADAM_LR * (m_hat / (jnp.sqrt(v_hat) + ADAM_EPS) + ADAM_WD * w)
    return delta, m_new, v_new


def _sum_adamw(parts, w, m, v, name):
    n, R, C = parts.shape
    tr = _blk(R, 128)

    def body(p_ref, w_ref, m_ref, v_ref, g_ref, d_ref, mo_ref, vo_ref):
        g = p_ref[0].astype(F32)
        for s in range(1, n):
            g = g + p_ref[s].astype(F32)
        delta, m_new, v_new = _adamw_math(w_ref[...], g, m_ref[...], v_ref[...])
        g_ref[...] = g
        d_ref[...] = delta
        mo_ref[...] = m_new
        vo_ref[...] = v_new

    row = pl.BlockSpec((tr, C), lambda i: (i, 0))
    shp = jax.ShapeDtypeStruct((R, C), F32)
    return pl.pallas_call(
        body, name=name, grid=(R // tr,),
        in_specs=[pl.BlockSpec((n, tr, C), lambda i: (0, i, 0)), row, row, row],
        out_specs=[row, row, row, row], out_shape=[shp, shp, shp, shp],
        compiler_params=_params(("parallel",)),
    )(parts, w, m, v)


def _adamw(g, w, m, v, name):
    R, C = g.shape
    tr = _blk(R, 128)

    def body(g_ref, w_ref, m_ref, v_ref, d_ref, mo_ref, vo_ref):
        delta, m_new, v_new = _adamw_math(w_ref[...], g_ref[...], m_ref[...], v_ref[...])
        d_ref[...] = delta
        mo_ref[...] = m_new
        vo_ref[...] = v_new

    row = pl.BlockSpec((tr, C), lambda i: (i, 0))
    shp = jax.ShapeDtypeStruct((R, C), F32)
    return pl.pallas_call(
        body, name=name, grid=(R // tr,),
        in_specs=[row, row, row, row], out_specs=[row, row, row], out_shape=[shp, shp, shp],
        compiler_params=_params(("parallel",)),
    )(g, w, m, v)


def _position():
    return lax.axis_index("x"), lax.axis_index("y"), lax.axis_index("c")


def _slot(px, py, pc):
    return 4 * px + 2 * py + pc


def _all_gather(shards, name):
    n = len(shards)

    def body(*refs):
        ins, outs = refs[:n], refs[n:2 * n]
        send_sems, recv_sems, local_sems = refs[2 * n:]
        x, y, c = _position()
        me, sibling = (x, y, c), (x, y, 1 - c)
        chips = [(1 - x, y), (x, 1 - y), (1 - x, 1 - y)]

        def copy(a, k, block, to, src=None):
            dst = outs[a].at[_slot(*block)]
            return pltpu.make_async_remote_copy(
                src_ref=dst if src is None else src, dst_ref=dst,
                send_sem=send_sems.at[a, k], recv_sem=recv_sems.at[a, k],
                device_id=to, device_id_type=MESH)

        mine = [pltpu.make_async_copy(ins[a], outs[a].at[_slot(*me)], local_sems.at[a]) for a in range(n)]
        for cp in mine:
            cp.start()
        first = []
        for a in range(n):
            first.append(copy(a, 0, me, sibling, src=ins[a]))
            first += [copy(a, 1 + j, me, (*chip, c), src=ins[a]) for j, chip in enumerate(chips)]
        for cp in first:
            cp.start()
        passed = []
        for j, chip in enumerate(chips):
            for a in range(n):
                copy(a, 1 + j, (*chip, c), me).wait_recv()
                fwd = copy(a, 4 + j, (*chip, c), sibling)
                fwd.start()
                passed.append(fwd)
        for a in range(n):
            copy(a, 0, sibling, me).wait_recv()
            for j, chip in enumerate(chips):
                copy(a, 4 + j, (*chip, 1 - c), me).wait_recv()
        for cp in first + passed:
            cp.wait_send()
        for cp in mine:
            cp.wait()

    return pl.pallas_call(
        body, name=name,
        in_specs=[ANY] * n, out_specs=[ANY] * n,
        out_shape=[jax.ShapeDtypeStruct((N_DEV,) + s.shape, s.dtype) for s in shards],
        scratch_shapes=[pltpu.SemaphoreType.DMA((n, 7)), pltpu.SemaphoreType.DMA((n, 7)),
                        pltpu.SemaphoreType.DMA((n,))],
    )(*shards)


def _peer(x, y, c, k):
    return (1 - x if k & 4 else x, 1 - y if k & 2 else y, 1 - c if k & 1 else c)


def _exchange(parts, name):
    n = len(parts)

    def body(*refs):
        ins, outs = refs[:n], refs[n:2 * n]
        send_sems, recv_sems, local_sems = refs[2 * n:]
        x, y, c = _position()
        me = _slot(x, y, c)
        mine = [pltpu.make_async_copy(ins[a].at[me], outs[a].at[me], local_sems.at[a]) for a in range(n)]
        for cp in mine:
            cp.start()
        sends = []
        for k in range(1, N_DEV):
            to = _peer(x, y, c, k)
            for a in range(n):
                cp = pltpu.make_async_remote_copy(
                    src_ref=ins[a].at[_slot(*to)], dst_ref=outs[a].at[me],
                    send_sem=send_sems.at[a, k - 1], recv_sem=recv_sems.at[a, k - 1],
                    device_id=to, device_id_type=MESH)
                cp.start()
                sends.append(cp)
        for k in range(1, N_DEV):
            frm = _peer(x, y, c, k)
            for a in range(n):
                pltpu.make_async_remote_copy(
                    src_ref=ins[a].at[_slot(*frm)], dst_ref=outs[a].at[_slot(*frm)],
                    send_sem=send_sems.at[a, k - 1], recv_sem=recv_sems.at[a, k - 1],
                    device_id=frm, device_id_type=MESH).wait_recv()
        for cp in sends:
            cp.wait_send()
        for cp in mine:
            cp.wait()

    return pl.pallas_call(
        body, name=name,
        in_specs=[ANY] * n, out_specs=[ANY] * n,
        out_shape=[jax.ShapeDtypeStruct(p.shape, p.dtype) for p in parts],
        scratch_shapes=[pltpu.SemaphoreType.DMA((n, 7)), pltpu.SemaphoreType.DMA((n, 7)),
                        pltpu.SemaphoreType.DMA((n,))],
    )(*parts)


HBM_SPEC = pl.BlockSpec(memory_space=pltpu.HBM)
SEM_SPEC = pl.BlockSpec(memory_space=pltpu.SEMAPHORE)
SIDE_EFFECT = pltpu.SideEffectType.DATAFLOW_SIDE_EFFECTING


def _remote_copies(src_refs, land_refs, send_sems, recv_sems, gather, outgoing):
    x, y, c = _position()
    me = _slot(x, y, c)
    copies = []
    for k in range(1, N_DEV):
        peer = _peer(x, y, c, k)
        for a in range(len(src_refs)):
            src = src_refs[a] if gather else src_refs[a].at[_slot(*peer)]
            dst = land_refs[a].at[me if outgoing else _slot(*peer)]
            sem = a * (N_DEV - 1) + k - 1
            copies.append(pltpu.make_async_remote_copy(
                src_ref=src, dst_ref=dst, send_sem=send_sems.at[sem], recv_sem=recv_sems.at[sem],
                device_id=peer, device_id_type=MESH))
    return copies


def _remote_start(srcs, after, name, gather):
    n = len(srcs)
    lands = [jax.ShapeDtypeStruct(((N_DEV,) + s.shape) if gather else s.shape, s.dtype) for s in srcs]

    def body(*refs):
        src_refs, land_refs = refs[:n], refs[n:2 * n]
        send_sems, recv_sems = refs[2 * n + 1], refs[2 * n + 2]
        token, local_sems = refs[4 * n + 3], refs[4 * n + 4]
        x, y, c = _position()
        me = _slot(x, y, c)
        mine = [pltpu.make_async_copy(src_refs[a] if gather else src_refs[a].at[me], land_refs[a].at[me],
                                      local_sems.at[a]) for a in range(n)]
        for cp in mine:
            cp.start()
        for cp in _remote_copies(src_refs, land_refs, send_sems, recv_sems, gather, outgoing=True):
            cp.start()
        for cp in mine:
            cp.wait()
        token[...] = jnp.zeros_like(token)

    sem_shape = pltpu.SemaphoreType.DMA((n * (N_DEV - 1),))
    outs = pl.pallas_call(
        body, name=name,
        out_shape=(sem_shape, sem_shape, *[pltpu.HBM(s.shape, s.dtype) for s in srcs],
                   *[pltpu.HBM(l.shape, l.dtype) for l in lands], jax.ShapeDtypeStruct((8, LANES), F32)),
        in_specs=[HBM_SPEC] * (2 * n) + [ANY],
        out_specs=(SEM_SPEC, SEM_SPEC, *([HBM_SPEC] * (2 * n)), pl.BlockSpec(memory_space=pltpu.VMEM)),
        input_output_aliases={a: 2 + a for a in range(2 * n)},
        scratch_shapes=[pltpu.SemaphoreType.DMA((n,))],
        compiler_params=pltpu.CompilerParams(has_side_effects=SIDE_EFFECT),
    )(*[pltpu.with_memory_space_constraint(s, pltpu.HBM) for s in srcs],
      *[pltpu.with_memory_space_constraint(lax.empty(l.shape, l.dtype), pltpu.HBM) for l in lands], after)
    return dict(send=outs[0], recv=outs[1], srcs=outs[2:2 + n], lands=outs[2 + n:2 + 2 * n], gather=gather)


def _remote_wait(flight, after, name):
    n = len(flight["srcs"])
    gather = flight["gather"]

    def body(*refs):
        src_refs, land_refs = refs[:n], refs[n:2 * n]
        send_sems, recv_sems = refs[2 * n], refs[2 * n + 1]
        for cp in _remote_copies(src_refs, land_refs, send_sems, recv_sems, gather, outgoing=False):
            cp.wait_send()
            cp.wait_recv()

    both = list(flight["srcs"]) + list(flight["lands"])
    outs = pl.pallas_call(
        body, name=name,
        out_shape=tuple(pltpu.HBM(a.shape, a.dtype) for a in both),
        in_specs=[HBM_SPEC] * (2 * n) + [SEM_SPEC, SEM_SPEC, ANY],
        out_specs=tuple([HBM_SPEC] * (2 * n)),
        input_output_aliases={a: a for a in range(2 * n)},
        compiler_params=pltpu.CompilerParams(has_side_effects=SIDE_EFFECT),
    )(*both, flight["send"], flight["recv"], after)
    return list(outs[n:])


def _all_reduce_small(blob, name):
    R, C = blob.shape

    def body(in_ref, out_ref, gath, send_sems, recv_sems):
        x, y, c = _position()
        me = _slot(x, y, c)
        gath[me] = in_ref[...]
        sends = []
        for k in range(1, N_DEV):
            to = _peer(x, y, c, k)
            cp = pltpu.make_async_remote_copy(
                src_ref=in_ref, dst_ref=gath.at[me],
                send_sem=send_sems.at[k - 1], recv_sem=recv_sems.at[k - 1],
                device_id=to, device_id_type=MESH)
            cp.start()
            sends.append(cp)
        for k in range(1, N_DEV):
            frm = _peer(x, y, c, k)
            pltpu.make_async_remote_copy(
                src_ref=in_ref, dst_ref=gath.at[_slot(*frm)],
                send_sem=send_sems.at[k - 1], recv_sem=recv_sems.at[k - 1],
                device_id=frm, device_id_type=MESH).wait_recv()
        for cp in sends:
            cp.wait_send()
        total = gath[0]
        for s in range(1, N_DEV):
            total = total + gath[s]
        out_ref[...] = total

    return pl.pallas_call(
        body, name=name,
        in_specs=[pl.BlockSpec(memory_space=pltpu.VMEM)],
        out_specs=pl.BlockSpec(memory_space=pltpu.VMEM),
        out_shape=jax.ShapeDtypeStruct((R, C), F32),
        scratch_shapes=[pltpu.VMEM((N_DEV, R, C), F32), pltpu.SemaphoreType.DMA((7,)),
                        pltpu.SemaphoreType.DMA((7,))],
        compiler_params=pltpu.CompilerParams(vmem_limit_bytes=VMEM_LIMIT),
    )(blob)


SMALL_VECS = ("ffn1_pre_g", "ffn1_post_g", "mix_pre_g", "sgu_ln_g", "sgu_ln_b", "mix_post_g", "ffn2_pre_g",
              "ffn2_post_g")
ROW_BS = len(SMALL_VECS)
ROW_BF = ROW_BS + 1
ROW_LOSS = ROW_BF + 1
ROW_WS = 16
BLOB_ROWS = ROW_WS + SGU_LEN


def _pack_small(vals, D, loss_row=None):
    rows = [vals[n].reshape(1, D) for n in SMALL_VECS]
    rows.append(vals["sgu_b_s"].reshape(1, D))
    rows.append(jnp.pad(vals["b_forget"].reshape(1, N_HEADS), ((0, 0), (0, D - N_HEADS))))
    rows.append(jnp.zeros((1, D), F32) if loss_row is None else loss_row)
    rows.append(jnp.zeros((ROW_WS - ROW_LOSS - 1, D), F32))
    rows.append(vals["sgu_w_s"].reshape(SGU_LEN, D))
    return jnp.concatenate(rows, axis=0)


def _unpack_small(blob, D):
    out = {n: blob[r:r + 1] for r, n in enumerate(SMALL_VECS)}
    out["sgu_b_s"] = blob[ROW_BS].reshape(1, N_GROUPS, SGU_LEN)
    out["b_forget"] = blob[ROW_BF, :N_HEADS].reshape(1, N_HEADS)
    out["sgu_w_s"] = blob[ROW_WS:].reshape(1, N_GROUPS, SGU_LEN, SGU_LEN)
    return out


WEIGHT_NAMES = ("ffn1_pre_g", "ffn1_w_gate", "ffn1_w_up", "ffn1_w_down", "ffn1_post_g", "mix_pre_g", "w_in",
                "b_forget", "sgu_ln_g", "sgu_ln_b", "sgu_w_s", "sgu_b_s", "w_out", "mix_post_g", "ffn2_pre_g",
                "ffn2_w_gate", "ffn2_w_up", "ffn2_w_down", "ffn2_post_g")
BIG_NAMES = ("ffn1_w_gate", "ffn1_w_up", "ffn1_w_down", "w_in", "w_out", "ffn2_w_gate", "ffn2_w_up", "ffn2_w_down")
WEIGHT_GROUPS = {"ffn1": ("ffn1_w_gate", "ffn1_w_up", "ffn1_w_down"), "mix": ("w_in", "w_out"),
                 "ffn2": ("ffn2_w_gate", "ffn2_w_up", "ffn2_w_down")}
GRAD_GROUPS = (("ffn2_w_gate", "ffn2_w_up", "ffn2_w_down"), ("w_out", "w_in"), ("ffn1_w_down", "ffn1_w_gate"),
               ("ffn1_w_up",))


def _local_step(x, target, small, fetch, emit):
    T, D = x.shape
    W = N_HEADS * HEAD_DIM
    vec = lambda n: small[n].reshape(1, D)
    big = dict(fetch("ffn1", x))

    x1, y1, gate1, up1 = _ffn_fwd(x, vec("ffn1_pre_g"), big["ffn1_w_gate"], big["ffn1_w_up"], big["ffn1_w_down"],
                                  vec("ffn1_post_g"), "ffn1_fwd")

    big.update(fetch("mix", x1))
    w_in_all = big["w_in"]
    in_width = N_DEV * w_in_all.shape[2]
    w_in = w_in_all.transpose(1, 0, 2).reshape(D, in_width)
    col_f = 3 * W
    col_u = col_f + N_HEADS
    seg_starts = (0, W, 2 * W, col_u, col_u + W, col_u + 2 * W, col_u + 3 * W)
    w7 = jnp.stack([w_in[:, s:s + W] for s in seg_starts])
    wf = jnp.pad(w_in[:, col_f:col_u], ((0, 0), (0, LANES - N_HEADS)))
    w_out = big["w_out"].reshape(D, D)
    b_pad = jnp.pad(small["b_forget"].reshape(1, N_HEADS), ((0, 0), (0, LANES - N_HEADS)))
    lng, lnb = vec("sgu_ln_g"), vec("sgu_ln_b")
    ws = small["sgu_w_s"].reshape(N_GROUPS, SGU_LEN, SGU_LEN)
    bs = small["sgu_b_s"].reshape(N_GROUPS, SGU_LEN, 1)

    z7, f_logit, h2b = _mix_in_fwd(x1, vec("mix_pre_g"), w7, wf, "mix_in_fwd")
    c = _forget_cumsum(f_logit, b_pad, "forget_cumsum")
    c_heads = c[:, :N_HEADS].T
    ta, _, n_chunks = _attn_geometry(T)
    c_chunks = c_heads.reshape(N_HEADS, n_chunks, 1, ta)
    c_col = c_heads[:, :, None]
    o_a, lse = _attn_fwd_loop(z7, c_chunks, "attn_fwd")
    x2, p, merged_b = _mix_out_fwd(z7, o_a, x1, lng, lnb, ws, bs, w_out, vec("mix_post_g"), "mix_out_fwd")
    big.update(fetch("ffn2", x2))
    x3, y2, gate2, up2 = _ffn_fwd(x2, vec("ffn2_pre_g"), big["ffn2_w_gate"], big["ffn2_w_up"], big["ffn2_w_down"],
                                  vec("ffn2_post_g"), "ffn2_fwd")
    dy, loss_lanes = _loss_head(x3, target, "loss_head")

    grads_small = {}

    dx2, h3b, dy2b, act2, dgate2, dup2, dgpre, dgpost = _ffn_bwd(
        dy, x2, y2, gate2, up2, vec("ffn2_pre_g"), big["ffn2_w_gate"], big["ffn2_w_up"], big["ffn2_w_down"],
        vec("ffn2_post_g"), "ffn2_bwd")
    grads_small["ffn2_pre_g"] = jnp.sum(dgpre, axis=0)
    grads_small["ffn2_post_g"] = jnp.sum(dgpost, axis=0)
    emit("ffn2_w_gate", _wgrad(h3b, dgate2, "ffn2_wgrad_gate", shard_cols=True))
    emit("ffn2_w_up", _wgrad(h3b, dup2, "ffn2_wgrad_up", shard_cols=True))
    emit("ffn2_w_down", _wgrad(act2, dy2b, "ffn2_wgrad_down").reshape(big["ffn2_w_down"].shape))

    dpb, dob, dvec, dz4, dgp, dlng, dlnb, dws, dbs = _mix_out_bwd(
        dx2, p, z7, o_a, lng, lnb, ws, bs, w_out, vec("mix_post_g"), "mix_out_bwd")
    grads_small["mix_post_g"] = dgp
    grads_small["sgu_ln_g"] = dlng
    grads_small["sgu_ln_b"] = dlnb
    grads_small["sgu_w_s"] = dws
    grads_small["sgu_b_s"] = dbs
    emit("w_out", _wgrad(merged_b, dpb, "w_out_wgrad").reshape(big["w_out"].shape))
    lse_chunks = lse.reshape(N_HEADS, n_chunks, 1, ta)
    d_chunks = dvec.reshape(N_HEADS, n_chunks, 1, ta)
    dk, dv, dc = _attn_bwd_kv_loop(z7, dob, c_col, lse_chunks, d_chunks, "attn_bwd_kv")
    dq, dc_q = _attn_bwd_q_loop(z7, dob, c_chunks, lse, dvec, "attn_bwd_q")
    dc_pad = jnp.pad((dc + dc_q).reshape(N_HEADS, T).T, ((0, 0), (0, LANES - N_HEADS)))
    dfb, dbf = _forget_bwd(dc_pad, f_logit, b_pad, "forget_bwd")
    grads_small["b_forget"] = dbf[:, :N_HEADS]
    segs = [(dq, None), (dk, None), (dv, None), (dz4, 0), (dz4, 1), (dz4, 2), (dz4, 3)]
    dx1, dgm = _mix_in_bwd(dx2, x1, vec("mix_pre_g"), segs, dfb, w7, wf, "mix_in_bwd")
    grads_small["mix_pre_g"] = jnp.sum(dgm, axis=0)
    seg_mats = [dq, dk, dv, dz4[0], dz4[1], dz4[2], dz4[3]]
    dw_seg = [_wgrad(h2b, sm, "w_in_wgrad_%d" % q) for q, sm in enumerate(seg_mats)]
    dwf = _wgrad(h2b, dfb, "w_in_wgrad_f")[:, :N_HEADS]
    dw_in = jnp.concatenate(dw_seg[:3] + [dwf] + dw_seg[3:], axis=1)
    emit("w_in", dw_in.reshape(D, N_DEV, in_width // N_DEV).transpose(1, 0, 2))

    dx0, h1b, dy1b, act1, dgate1, dup1, dgpre1, dgpost1 = _ffn_bwd(
        dx1, x, y1, gate1, up1, vec("ffn1_pre_g"), big["ffn1_w_gate"], big["ffn1_w_up"], big["ffn1_w_down"],
        vec("ffn1_post_g"), "ffn1_bwd")
    grads_small["ffn1_pre_g"] = jnp.sum(dgpre1, axis=0)
    grads_small["ffn1_post_g"] = jnp.sum(dgpost1, axis=0)
    emit("ffn1_w_down", _wgrad(act1, dy1b, "ffn1_wgrad_down").reshape(big["ffn1_w_down"].shape))
    emit("ffn1_w_gate", _wgrad(h1b, dgate1, "ffn1_wgrad_gate", shard_cols=True))
    emit("ffn1_w_up", _wgrad(h1b, dup1, "ffn1_wgrad_up", shard_cols=True))

    loss_row = jnp.pad(loss_lanes, ((0, 0), (0, D - LANES)))
    return loss_row, dx0, grads_small


def kernel(x, ffn1_pre_g, ffn1_w_gate, ffn1_w_up, ffn1_w_down, ffn1_post_g, mix_pre_g, w_in, b_forget, sgu_ln_g, sgu_ln_b, sgu_w_s, sgu_b_s, w_out, mix_post_g, ffn2_pre_g, ffn2_w_gate, ffn2_w_up, ffn2_w_down, ffn2_post_g, loss_target, m_ffn1_pre_g, m_ffn1_w_gate, m_ffn1_w_up, m_ffn1_w_down, m_ffn1_post_g, m_mix_pre_g, m_w_in, m_b_forget, m_sgu_ln_g, m_sgu_ln_b, m_sgu_w_s, m_sgu_b_s, m_w_out, m_mix_post_g, m_ffn2_pre_g, m_ffn2_w_gate, m_ffn2_w_up, m_ffn2_w_down, m_ffn2_post_g, v_ffn1_pre_g, v_ffn1_w_gate, v_ffn1_w_up, v_ffn1_w_down, v_ffn1_post_g, v_mix_pre_g, v_w_in, v_b_forget, v_sgu_ln_g, v_sgu_ln_b, v_sgu_w_s, v_sgu_b_s, v_w_out, v_mix_post_g, v_ffn2_pre_g, v_ffn2_w_gate, v_ffn2_w_up, v_ffn2_w_down, v_ffn2_post_g):
    weights = dict(zip(WEIGHT_NAMES, (ffn1_pre_g, ffn1_w_gate, ffn1_w_up, ffn1_w_down, ffn1_post_g, mix_pre_g, w_in,
                                      b_forget, sgu_ln_g, sgu_ln_b, sgu_w_s, sgu_b_s, w_out, mix_post_g, ffn2_pre_g,
                                      ffn2_w_gate, ffn2_w_up, ffn2_w_down, ffn2_post_g)))
    mom1 = dict(zip(WEIGHT_NAMES, (m_ffn1_pre_g, m_ffn1_w_gate, m_ffn1_w_up, m_ffn1_w_down, m_ffn1_post_g,
                                   m_mix_pre_g, m_w_in, m_b_forget, m_sgu_ln_g, m_sgu_ln_b, m_sgu_w_s, m_sgu_b_s,
                                   m_w_out, m_mix_post_g, m_ffn2_pre_g, m_ffn2_w_gate, m_ffn2_w_up, m_ffn2_w_down,
                                   m_ffn2_post_g)))
    mom2 = dict(zip(WEIGHT_NAMES, (v_ffn1_pre_g, v_ffn1_w_gate, v_ffn1_w_up, v_ffn1_w_down, v_ffn1_post_g,
                                   v_mix_pre_g, v_w_in, v_b_forget, v_sgu_ln_g, v_sgu_ln_b, v_sgu_w_s, v_sgu_b_s,
                                   v_w_out, v_mix_post_g, v_ffn2_pre_g, v_ffn2_w_gate, v_ffn2_w_up, v_ffn2_w_down,
                                   v_ffn2_post_g)))
    D = x.shape[-1]
    small_names = [n for n in WEIGHT_NAMES if n not in BIG_NAMES]

    small = {n: weights[n] for n in small_names}
    shard = lambda n: weights[n][0].astype(BF16)

    ffn1_full = _all_gather([shard(n) for n in WEIGHT_GROUPS["ffn1"]], "ffn1_all_gather")
    gathers = {grp: _remote_start([shard(n) for n in WEIGHT_GROUPS[grp]], ffn1_full[0], grp + "_gather_start",
                                  gather=True) for grp in ("mix", "ffn2")}

    def fetch(group, after):
        if group == "ffn1":
            return zip(WEIGHT_GROUPS[group], ffn1_full)
        return zip(WEIGHT_GROUPS[group], _remote_wait(gathers[group], after, group + "_gather_wait"))

    ready, flights = {}, []

    def emit(name, part):
        ready[name] = part
        for group in GRAD_GROUPS:
            if name == group[-1]:
                flights.append((group, _remote_start([ready[n] for n in group], part, name + "_grad_start",
                                                     gather=False)))

    loss_row, grad_x, grads_small = _local_step(x[0], loss_target[0], small, fetch, emit)

    blob = _all_reduce_small(_pack_small(grads_small, D, loss_row), "small_all_reduce")

    out = {}
    after = blob
    for group, flight in flights:
        received = _remote_wait(flight, after, group[-1] + "_grad_wait")
        for n, rcv in zip(group, received):
            g, d, m_new, v_new = _sum_adamw(rcv, weights[n][0], mom1[n][0], mom2[n][0], "adamw_" + n)
            out[n] = tuple(a[None] for a in (g, d, m_new, v_new))
            after = g

    d_blob, m_blob, v_blob = _adamw(blob, _pack_small(small, D), _pack_small({n: mom1[n] for n in small_names}, D),
                                    _pack_small({n: mom2[n] for n in small_names}, D), "adamw_small")
    unpacked = [_unpack_small(b, D) for b in (blob, d_blob, m_blob, v_blob)]
    for n in small_names:
        out[n] = tuple(u[n].reshape(weights[n].shape) for u in unpacked)

    loss = blob[ROW_LOSS, 0]
    result = [loss, grad_x[None]]
    for k in range(4):
        result += [out[n][k] for n in WEIGHT_NAMES]
    return tuple(result)
```

```python
import functools

import numpy as np
import jax
import jax.numpy as jnp
from jax import lax
from jax.experimental import pallas as pl
from jax.experimental.pallas import tpu as pltpu

F32 = jnp.float32
BF16 = jnp.bfloat16

RMS_EPS = 1e-6
LN_EPS = 1e-5
HEAD_DIM = 128
N_HEADS = 8
GROUP_DIM = 128
N_GROUPS = 8
SGU_LEN = 128
CHUNK = 64
N_DEV = 8
LANES = 128
VMEM_LIMIT = 56 * 1024 * 1024
NEG_BIG = -1e30

ADAM_LR = 0.001
ADAM_B1 = 0.9
ADAM_B2 = 0.999
ADAM_EPS = 1e-08
ADAM_WD = 0.01
ADAM_STEP = 10

MESH = pl.DeviceIdType.MESH
ANY = pl.BlockSpec(memory_space=pl.ANY)


def _blk(n, pref):
    return pref if (n >= pref and n % pref == 0) else n


def _mm(a, b):
    return jnp.dot(a, b, preferred_element_type=F32)


def _mm_nt(a, b):
    return lax.dot_general(a, b, (((1,), (1,)), ((), ())), preferred_element_type=F32)


def _mm_tn(a, b):
    return lax.dot_general(a, b, (((0,), (0,)), ((), ())), preferred_element_type=F32)


def _params(sem):
    return pltpu.CompilerParams(dimension_semantics=sem, vmem_limit_bytes=VMEM_LIMIT)


def _gelu(x):
    return 0.5 * x * (1.0 + lax.erf(x * np.float32(1.0 / np.sqrt(2.0))))


def _gelu_grad(x):
    cdf = 0.5 * (1.0 + lax.erf(x * np.float32(1.0 / np.sqrt(2.0))))
    return cdf + x * jnp.exp(-0.5 * x * x) * np.float32(1.0 / np.sqrt(2.0 * np.pi))


def _rms_scale(v):
    return lax.rsqrt(jnp.mean(v * v, axis=-1, keepdims=True) + RMS_EPS)


def _rms_bwd(dy, xhat, r, g):
    dxh = dy * g
    return r * (dxh - xhat * jnp.mean(dxh * xhat, axis=-1, keepdims=True))


def _ffn_fwd(x, g_pre, wg, wu, wd, g_post, name):
    T, D = x.shape
    ns, _, fs = wg.shape
    tm = _blk(T, 512)

    def body(x_ref, gpre_ref, wg_ref, wu_ref, wd_ref, gpost_ref, xo_ref, y_ref, g_ref, u_ref, h_scr, acc_scr):
        j = pl.program_id(1)

        @pl.when(j == 0)
        def _():
            xv = x_ref[...]
            h_scr[...] = (xv * _rms_scale(xv) * gpre_ref[...]).astype(BF16)
            acc_scr[...] = jnp.zeros_like(acc_scr)

        h = h_scr[...]
        gg = _mm(h, wg_ref[...])
        uu = _mm(h, wu_ref[...])
        a = gg * jax.nn.sigmoid(gg) * uu
        g_ref[...] = gg.astype(BF16)
        u_ref[...] = uu.astype(BF16)
        acc_scr[...] += _mm(a.astype(BF16), wd_ref[...])

        @pl.when(j == ns - 1)
        def _():
            y = acc_scr[...]
            y_ref[...] = y
            xo_ref[...] = x_ref[...] + 0.5 * (y * _rms_scale(y) * gpost_ref[...])

    row = pl.BlockSpec((tm, D), lambda i, j: (i, 0))
    vec = pl.BlockSpec((1, D), lambda i, j: (0, 0))
    return pl.pallas_call(
        body, name=name, grid=(T // tm, ns),
        in_specs=[row, vec,
                  pl.BlockSpec((None, D, fs), lambda i, j: (j, 0, 0)),
                  pl.BlockSpec((None, D, fs), lambda i, j: (j, 0, 0)),
                  pl.BlockSpec((None, fs, D), lambda i, j: (j, 0, 0)),
                  vec],
        out_specs=[row, row,
                   pl.BlockSpec((tm, fs), lambda i, j: (i, j)),
                   pl.BlockSpec((tm, fs), lambda i, j: (i, j))],
        out_shape=[jax.ShapeDtypeStruct((T, D), F32), jax.ShapeDtypeStruct((T, D), F32),
                   jax.ShapeDtypeStruct((T, ns * fs), BF16), jax.ShapeDtypeStruct((T, ns * fs), BF16)],
        scratch_shapes=[pltpu.VMEM((tm, D), BF16), pltpu.VMEM((tm, D), F32)],
        compiler_params=_params(("parallel", "arbitrary")),
    )(x, g_pre, wg, wu, wd, g_post)


def _after(dep):
    return jnp.zeros((8, LANES), F32) if dep is None else dep


def _ffn_bwd(dxo, x, y, gate, up, g_pre, wg, wu, wd, g_post, name, dep=None):
    T, D = x.shape
    ns, _, fs = wg.shape
    tm = _blk(T, 512)
    n_i = T // tm

    def body(dxo_ref, x_ref, y_ref, g_ref, u_ref, gpre_ref, wg_ref, wu_ref, wd_ref, gpost_ref, _,
             dx_ref, hb_ref, dyb_ref, ab_ref, dgb_ref, dub_ref, dgpre_ref, dgpost_ref, dy_scr, acc_scr):
        j = pl.program_id(1)

        @pl.when(j == 0)
        def _():
            yv = y_ref[...]
            s = _rms_scale(yv)
            n = yv * s
            dn = 0.5 * dxo_ref[...]
            dgpost_ref[...] = jnp.sum(dn * n, axis=0, keepdims=True)
            dyv = _rms_bwd(dn, n, s, gpost_ref[...]).astype(BF16)
            dy_scr[...] = dyv
            dyb_ref[...] = dyv
            xv = x_ref[...]
            hb_ref[...] = (xv * _rms_scale(xv) * gpre_ref[...]).astype(BF16)
            acc_scr[...] = jnp.zeros_like(acc_scr)

        da = _mm_nt(dy_scr[...], wd_ref[...])
        gg = g_ref[...].astype(F32)
        uu = u_ref[...].astype(F32)
        sg = jax.nn.sigmoid(gg)
        silu = gg * sg
        dgate = (da * uu * (sg * (1.0 + gg * (1.0 - sg)))).astype(BF16)
        dup = (da * silu).astype(BF16)
        ab_ref[...] = (silu * uu).astype(BF16)
        dgb_ref[...] = dgate
        dub_ref[...] = dup
        acc_scr[...] += _mm_nt(dgate, wg_ref[...]) + _mm_nt(dup, wu_ref[...])

        @pl.when(j == ns - 1)
        def _():
            xv = x_ref[...]
            r = _rms_scale(xv)
            xhat = xv * r
            dh = acc_scr[...]
            dgpre_ref[...] = jnp.sum(dh * xhat, axis=0, keepdims=True)
            dx_ref[...] = _rms_bwd(dh, xhat, r, gpre_ref[...]) + dxo_ref[...]

    row = pl.BlockSpec((tm, D), lambda i, j: (i, 0))
    vec = pl.BlockSpec((1, D), lambda i, j: (0, 0))
    wide = pl.BlockSpec((tm, fs), lambda i, j: (i, j))
    part = pl.BlockSpec((None, 1, D), lambda i, j: (i, 0, 0))
    F = ns * fs
    return pl.pallas_call(
        body, name=name, grid=(n_i, ns),
        in_specs=[row, row, row, wide, wide, vec,
                  pl.BlockSpec((None, D, fs), lambda i, j: (j, 0, 0)),
                  pl.BlockSpec((None, D, fs), lambda i, j: (j, 0, 0)),
                  pl.BlockSpec((None, fs, D), lambda i, j: (j, 0, 0)),
                  vec, ANY],
        out_specs=[row, row, row, wide, wide, wide, part, part],
        out_shape=[jax.ShapeDtypeStruct((T, D), F32), jax.ShapeDtypeStruct((T, D), BF16),
                   jax.ShapeDtypeStruct((T, D), BF16), jax.ShapeDtypeStruct((T, F), BF16),
                   jax.ShapeDtypeStruct((T, F), BF16), jax.ShapeDtypeStruct((T, F), BF16),
                   jax.ShapeDtypeStruct((n_i, 1, D), F32), jax.ShapeDtypeStruct((n_i, 1, D), F32)],
        scratch_shapes=[pltpu.VMEM((tm, D), BF16), pltpu.VMEM((tm, D), F32)],
        compiler_params=_params(("parallel", "arbitrary")),
    )(dxo, x, y, gate, up, g_pre, wg, wu, wd, g_post, _after(dep))


def _wgrad(xm, ym, name, shard_cols=False, dep=None):
    T, M = xm.shape
    _, N = ym.shape
    bm = _blk(M, 1024)
    bn = N // N_DEV if shard_cols else _blk(N, 512)
    tk = _blk(T, 1024)
    n_k = T // tk

    def body(x_ref, y_ref, _, o_ref, acc_scr):
        k = pl.program_id(2)

        @pl.when(k == 0)
        def _():
            acc_scr[...] = jnp.zeros_like(acc_scr)

        acc_scr[...] += _mm_tn(x_ref[...], y_ref[...])

        @pl.when(k == n_k - 1)
        def _():
            o_ref[...] = acc_scr[...].astype(BF16)

    if shard_cols:
        out_spec = pl.BlockSpec((None, bm, bn), lambda i, j, k: (j, i, 0))
        out_shape = jax.ShapeDtypeStruct((N // bn, M, bn), BF16)
    else:
        out_spec = pl.BlockSpec((bm, bn), lambda i, j, k: (i, j))
        out_shape = jax.ShapeDtypeStruct((M, N), BF16)
    return pl.pallas_call(
        body, name=name, grid=(M // bm, N // bn, n_k),
        in_specs=[pl.BlockSpec((tk, bm), lambda i, j, k: (k, i)),
                  pl.BlockSpec((tk, bn), lambda i, j, k: (k, j)), ANY],
        out_specs=out_spec, out_shape=out_shape,
        scratch_shapes=[pltpu.VMEM((bm, bn), F32)],
        compiler_params=_params(("parallel", "parallel", "arbitrary")),
    )(xm, ym, _after(dep))


def _mix_in_fwd(x1, g, w7, wf, name):
    T, D = x1.shape
    n_seg, _, W = w7.shape
    tm = _blk(T, 1024)

    def body(x_ref, g_ref, w_ref, wf_ref, z_ref, f_ref, hb_ref, h_scr):
        s = pl.program_id(1)

        @pl.when(s == 0)
        def _():
            xv = x_ref[...]
            h = (xv * _rms_scale(xv) * g_ref[...]).astype(BF16)
            h_scr[...] = h
            hb_ref[...] = h
            f_ref[...] = _mm(h, wf_ref[...])

        z_ref[...] = _mm(h_scr[...], w_ref[...]).astype(BF16)

    return pl.pallas_call(
        body, name=name, grid=(T // tm, n_seg),
        in_specs=[pl.BlockSpec((tm, D), lambda i, s: (i, 0)),
                  pl.BlockSpec((1, D), lambda i, s: (0, 0)),
                  pl.BlockSpec((None, D, W), lambda i, s: (s, 0, 0)),
                  pl.BlockSpec((D, LANES), lambda i, s: (0, 0))],
        out_specs=[pl.BlockSpec((None, tm, W), lambda i, s: (s, i, 0)),
                   pl.BlockSpec((tm, LANES), lambda i, s: (i, 0)),
                   pl.BlockSpec((tm, D), lambda i, s: (i, 0))],
        out_shape=[jax.ShapeDtypeStruct((n_seg, T, W), BF16), jax.ShapeDtypeStruct((T, LANES), F32),
                   jax.ShapeDtypeStruct((T, D), BF16)],
        scratch_shapes=[pltpu.VMEM((tm, D), BF16)],
        compiler_params=_params(("parallel", "arbitrary")),
    )(x1, g, w7, wf)


def _mix_in_bwd(dx2, x1, g, segs, dfb, w7, wf, name):
    T, D = x1.shape
    n_seg, _, W = w7.shape
    tm = _blk(T, 512)
    n_i = T // tm

    def body(*refs):
        dx2_ref, x_ref, g_ref = refs[:3]
        seg_refs = refs[3:3 + n_seg]
        df_ref, w_ref, wf_ref, dx1_ref, dg_ref, acc_scr = refs[3 + n_seg:]
        s = pl.program_id(1)

        @pl.when(s == 0)
        def _():
            acc_scr[...] = _mm_nt(df_ref[...], wf_ref[...])

        for q in range(n_seg):
            @pl.when(s == q)
            def _(q=q):
                acc_scr[...] += _mm_nt(seg_refs[q][...], w_ref[...])

        @pl.when(s == n_seg - 1)
        def _():
            xv = x_ref[...]
            r = _rms_scale(xv)
            xhat = xv * r
            dh = acc_scr[...]
            dg_ref[...] = jnp.sum(dh * xhat, axis=0, keepdims=True)
            dx1_ref[...] = _rms_bwd(dh, xhat, r, g_ref[...]) + dx2_ref[...]

    row = pl.BlockSpec((tm, D), lambda i, s: (i, 0))
    seg_specs = []
    seg_args = []
    for arr, idx in segs:
        if idx is None:
            seg_specs.append(pl.BlockSpec((tm, W), lambda i, s: (i, 0)))
        else:
            seg_specs.append(pl.BlockSpec((None, tm, W), lambda i, s, idx=idx: (idx, i, 0)))
        seg_args.append(arr)
    return pl.pallas_call(
        body, name=name, grid=(n_i, n_seg),
        in_specs=[row, row, pl.BlockSpec((1, D), lambda i, s: (0, 0))] + seg_specs + [
            pl.BlockSpec((tm, LANES), lambda i, s: (i, 0)),
            pl.BlockSpec((None, D, W), lambda i, s: (s, 0, 0)),
            pl.BlockSpec((D, LANES), lambda i, s: (0, 0))],
        out_specs=[row, pl.BlockSpec((None, 1, D), lambda i, s: (i, 0, 0))],
        out_shape=[jax.ShapeDtypeStruct((T, D), F32), jax.ShapeDtypeStruct((n_i, 1, D), F32)],
        scratch_shapes=[pltpu.VMEM((tm, D), F32)],
        compiler_params=_params(("parallel", "arbitrary")),
    )(dx2, x1, g, *seg_args, dfb, w7, wf)


def _forget_cumsum(f, b_pad, name):
    T, L = f.shape
    tb = _blk(T, 256)

    def body(f_ref, b_ref, c_ref, carry):
        @pl.when(pl.program_id(0) == 0)
        def _():
            carry[...] = jnp.zeros_like(carry)

        lf = jax.nn.log_sigmoid(f_ref[...] + b_ref[...])
        rows = lax.broadcasted_iota(jnp.int32, (tb, tb), 0)
        cols = lax.broadcasted_iota(jnp.int32, (tb, tb), 1)
        tri = (cols <= rows).astype(F32)
        c = jnp.dot(tri, lf, preferred_element_type=F32, precision=lax.Precision.HIGHEST) + carry[...]
        c_ref[...] = c
        carry[...] = c[tb - 1:tb, :]

    return pl.pallas_call(
        body, name=name, grid=(T // tb,),
        in_specs=[pl.BlockSpec((tb, L), lambda i: (i, 0)), pl.BlockSpec((1, L), lambda i: (0, 0))],
        out_specs=pl.BlockSpec((tb, L), lambda i: (i, 0)),
        out_shape=jax.ShapeDtypeStruct((T, L), F32),
        scratch_shapes=[pltpu.VMEM((1, L), F32)],
        compiler_params=_params(("arbitrary",)),
    )(f, b_pad)


def _forget_bwd(dc, f, b_pad, name):
    T, L = f.shape
    tb = _blk(T, 256)
    nb = T // tb

    def body(dc_ref, f_ref, b_ref, df_ref, db_ref, carry):
        @pl.when(pl.program_id(0) == 0)
        def _():
            carry[...] = jnp.zeros_like(carry)
            db_ref[...] = jnp.zeros_like(db_ref)

        rows = lax.broadcasted_iota(jnp.int32, (tb, tb), 0)
        cols = lax.broadcasted_iota(jnp.int32, (tb, tb), 1)
        tri = (cols >= rows).astype(F32)
        r = jnp.dot(tri, dc_ref[...], preferred_element_type=F32, precision=lax.Precision.HIGHEST) + carry[...]
        carry[...] = r[0:1, :]
        df = r * (1.0 - jax.nn.sigmoid(f_ref[...] + b_ref[...]))
        df_ref[...] = df.astype(BF16)
        db_ref[...] += jnp.sum(df, axis=0, keepdims=True)

    rev = pl.BlockSpec((tb, L), lambda i: (nb - 1 - i, 0))
    one = pl.BlockSpec((1, L), lambda i: (0, 0))
    return pl.pallas_call(
        body, name=name, grid=(nb,),
        in_specs=[rev, rev, one], out_specs=[rev, one],
        out_shape=[jax.ShapeDtypeStruct((T, L), BF16), jax.ShapeDtypeStruct((1, L), F32)],
        scratch_shapes=[pltpu.VMEM((1, L), F32)],
        compiler_params=_params(("arbitrary",)),
    )(dc, f, b_pad)


def _attn_fwd(z7, c_row, name):
    _, T, W = z7.shape
    H = W // HEAD_DIM
    ta = _blk(T, 512)
    nq = T // ta
    scale = np.float32(1.0 / np.sqrt(HEAD_DIM))

    def body(q_ref, k_ref, v_ref, crow_ref, o_ref, lse_ref, m_scr, l_scr, acc_scr):
        i = pl.program_id(1)
        j = pl.program_id(2)

        @pl.when(j == 0)
        def _():
            m_scr[...] = jnp.full_like(m_scr, NEG_BIG)
            l_scr[...] = jnp.zeros_like(l_scr)
            acc_scr[...] = jnp.zeros_like(acc_scr)

        def step(diagonal):
            s = _mm_nt(q_ref[...], k_ref[...]) * scale - crow_ref[...]
            if diagonal:
                rows = lax.broadcasted_iota(jnp.int32, (ta, ta), 0)
                cols = lax.broadcasted_iota(jnp.int32, (ta, ta), 1)
                s = jnp.where(cols <= rows, s, NEG_BIG)
            m_prev = m_scr[...]
            m_new = jnp.maximum(m_prev, jnp.max(s, axis=-1, keepdims=True))
            alpha = jnp.exp(m_prev - m_new)
            p = jnp.exp(s - m_new)
            l_scr[...] = alpha * l_scr[...] + jnp.sum(p, axis=-1, keepdims=True)
            acc_scr[...] = alpha * acc_scr[...] + _mm(p.astype(BF16), v_ref[...])
            m_scr[...] = m_new

        @pl.when(j < i)
        def _():
            step(False)

        @pl.when(j == i)
        def _():
            step(True)
            l = l_scr[...]
            o_ref[...] = acc_scr[...] / l
            lse_ref[...] = m_scr[...] + jnp.log(l)

    return pl.pallas_call(
        body, name=name, grid=(H, nq, nq),
        in_specs=[pl.BlockSpec((None, ta, HEAD_DIM), lambda h, i, j: (0, i, h)),
                  pl.BlockSpec((None, ta, HEAD_DIM), lambda h, i, j: (1, jnp.minimum(i, j), h)),
                  pl.BlockSpec((None, ta, HEAD_DIM), lambda h, i, j: (2, jnp.minimum(i, j), h)),
                  pl.BlockSpec((None, 1, ta), lambda h, i, j: (h, 0, jnp.minimum(i, j)))],
        out_specs=[pl.BlockSpec((ta, HEAD_DIM), lambda h, i, j: (i, h)),
                   pl.BlockSpec((None, ta, 1), lambda h, i, j: (h, i, 0))],
        out_shape=[jax.ShapeDtypeStruct((T, W), F32), jax.ShapeDtypeStruct((H, T, 1), F32)],
        scratch_shapes=[pltpu.VMEM((ta, 1), F32), pltpu.VMEM((ta, 1), F32), pltpu.VMEM((ta, HEAD_DIM), F32)],
        compiler_params=_params(("parallel", "parallel", "arbitrary")),
    )(z7, z7, z7, c_row)


def _attn_bwd_kv(z7, dob, c_col, lse_row, d_row, name):
    _, T, W = z7.shape
    H = W // HEAD_DIM
    ta = _blk(T, 512)
    nq = T // ta
    scale = np.float32(1.0 / np.sqrt(HEAD_DIM))

    def body(k_ref, v_ref, q_ref, do_ref, ccol_ref, lse_ref, d_ref, dk_ref, dv_ref, dc_ref, dk_scr, dv_scr, dc_scr):
        j = pl.program_id(1)
        i = pl.program_id(2)

        @pl.when(i == 0)
        def _():
            dk_scr[...] = jnp.zeros_like(dk_scr)
            dv_scr[...] = jnp.zeros_like(dv_scr)
            dc_scr[...] = jnp.zeros_like(dc_scr)

        def step(diagonal):
            q = q_ref[...]
            do = do_ref[...]
            st = _mm_nt(k_ref[...], q) * scale - ccol_ref[...] - lse_ref[...]
            if diagonal:
                rows = lax.broadcasted_iota(jnp.int32, (ta, ta), 0)
                cols = lax.broadcasted_iota(jnp.int32, (ta, ta), 1)
                st = jnp.where(rows <= cols, st, NEG_BIG)
            pt = jnp.exp(st)
            dv_scr[...] += _mm(pt.astype(BF16), do)
            dst = pt * (_mm_nt(v_ref[...], do) - d_ref[...])
            dk_scr[...] += _mm(dst.astype(BF16), q)
            dc_scr[...] += jnp.sum(dst, axis=-1, keepdims=True)

        @pl.when(i > j)
        def _():
            step(False)

        @pl.when(i == j)
        def _():
            step(True)

        @pl.when(i == nq - 1)
        def _():
            dk_ref[...] = (dk_scr[...] * scale).astype(BF16)
            dv_ref[...] = dv_scr[...].astype(BF16)
            dc_ref[...] = -dc_scr[...]

    return pl.pallas_call(
        body, name=name, grid=(H, nq, nq),
        in_specs=[pl.BlockSpec((None, ta, HEAD_DIM), lambda h, j, i: (1, j, h)),
                  pl.BlockSpec((None, ta, HEAD_DIM), lambda h, j, i: (2, j, h)),
                  pl.BlockSpec((None, ta, HEAD_DIM), lambda h, j, i: (0, jnp.maximum(i, j), h)),
                  pl.BlockSpec((ta, HEAD_DIM), lambda h, j, i: (jnp.maximum(i, j), h)),
                  pl.BlockSpec((None, ta, 1), lambda h, j, i: (h, j, 0)),
                  pl.BlockSpec((None, 1, ta), lambda h, j, i: (h, 0, jnp.maximum(i, j))),
                  pl.BlockSpec((None, 1, ta), lambda h, j, i: (h, 0, jnp.maximum(i, j)))],
        out_specs=[pl.BlockSpec((ta, HEAD_DIM), lambda h, j, i: (j, h)),
                   pl.BlockSpec((ta, HEAD_DIM), lambda h, j, i: (j, h)),
                   pl.BlockSpec((None, ta, 1), lambda h, j, i: (h, j, 0))],
        out_shape=[jax.ShapeDtypeStruct((T, W), BF16), jax.ShapeDtypeStruct((T, W), BF16),
                   jax.ShapeDtypeStruct((H, T, 1), F32)],
        scratch_shapes=[pltpu.VMEM((ta, HEAD_DIM), F32), pltpu.VMEM((ta, HEAD_DIM), F32), pltpu.VMEM((ta, 1), F32)],
        compiler_params=_params(("parallel", "parallel", "arbitrary")),
    )(z7, z7, z7, dob, c_col, lse_row, d_row)


def _attn_bwd_q(z7, dob, c_row, lse_col, d_col, name):
    _, T, W = z7.shape
    H = W // HEAD_DIM
    ta = _blk(T, 512)
    nq = T // ta
    scale = np.float32(1.0 / np.sqrt(HEAD_DIM))

    def body(q_ref, k_ref, v_ref, do_ref, crow_ref, lse_ref, d_ref, dq_ref, dc_ref, dq_scr, dc_scr):
        i = pl.program_id(1)
        j = pl.program_id(2)

        @pl.when(j == 0)
        def _():
            dq_scr[...] = jnp.zeros_like(dq_scr)
            dc_scr[...] = jnp.zeros_like(dc_scr)

        def step(diagonal):
            k = k_ref[...]
            do = do_ref[...]
            s = _mm_nt(q_ref[...], k) * scale - crow_ref[...] - lse_ref[...]
            if diagonal:
                rows = lax.broadcasted_iota(jnp.int32, (ta, ta), 0)
                cols = lax.broadcasted_iota(jnp.int32, (ta, ta), 1)
                s = jnp.where(cols <= rows, s, NEG_BIG)
            p = jnp.exp(s)
            ds = p * (_mm_nt(do, v_ref[...]) - d_ref[...])
            dq_scr[...] += _mm(ds.astype(BF16), k)
            dc_scr[...] += jnp.sum(ds, axis=-1, keepdims=True)

        @pl.when(j < i)
        def _():
            step(False)

        @pl.when(j == i)
        def _():
            step(True)
            dq_ref[...] = (dq_scr[...] * scale).astype(BF16)
            dc_ref[...] = dc_scr[...]

    return pl.pallas_call(
        body, name=name, grid=(H, nq, nq),
        in_specs=[pl.BlockSpec((None, ta, HEAD_DIM), lambda h, i, j: (0, i, h)),
                  pl.BlockSpec((None, ta, HEAD_DIM), lambda h, i, j: (1, jnp.minimum(i, j), h)),
                  pl.BlockSpec((None, ta, HEAD_DIM), lambda h, i, j: (2, jnp.minimum(i, j), h)),
                  pl.BlockSpec((ta, HEAD_DIM), lambda h, i, j: (i, h)),
                  pl.BlockSpec((None, 1, ta), lambda h, i, j: (h, 0, jnp.minimum(i, j))),
                  pl.BlockSpec((None, ta, 1), lambda h, i, j: (h, i, 0)),
                  pl.BlockSpec((None, ta, 1), lambda h, i, j: (h, i, 0))],
        out_specs=[pl.BlockSpec((ta, HEAD_DIM), lambda h, i, j: (i, h)),
                   pl.BlockSpec((None, ta, 1), lambda h, i, j: (h, i, 0))],
        out_shape=[jax.ShapeDtypeStruct((T, W), BF16), jax.ShapeDtypeStruct((H, T, 1), F32)],
        scratch_shapes=[pltpu.VMEM((ta, HEAD_DIM), F32), pltpu.VMEM((ta, 1), F32)],
        compiler_params=_params(("parallel", "parallel", "arbitrary")),
    )(z7, z7, z7, dob, c_row, lse_col, d_col)


ATTN_TILE = 512
ATTN_CHAINS = 2


def _attn_geometry(T):
    ta = _blk(T, ATTN_TILE)
    nc = ATTN_CHAINS if (T // ta) % ATTN_CHAINS == 0 else 1
    return ta, nc, T // ta


def _causal_tile(ta, keys_on_rows=False):
    rows = lax.broadcasted_iota(jnp.int32, (ta, ta), 0)
    cols = lax.broadcasted_iota(jnp.int32, (ta, ta), 1)
    return rows <= cols if keys_on_rows else cols <= rows


def _chunk(ref, j, ta):
    return ref[pl.ds(pl.multiple_of(j * ta, ta), ta), :]


def _attn_fwd_loop(z7, c_chunks, name):
    _, T, W = z7.shape
    H = W // HEAD_DIM
    ta, nc, n_chunks = _attn_geometry(T)
    scale = np.float32(1.0 / np.sqrt(HEAD_DIM))

    def body(q_ref, k_ref, v_ref, c_ref, o_ref, lse_ref, m_scr, l_scr, acc_scr):
        g = pl.program_id(1)
        m_scr[...] = jnp.full_like(m_scr, NEG_BIG)
        l_scr[...] = jnp.zeros_like(l_scr)
        acc_scr[...] = jnp.zeros_like(acc_scr)

        def update(ch, k, v, crow, diagonal):
            q = q_ref[ch * ta:(ch + 1) * ta, :]
            s = _mm_nt(q, k) * scale - crow
            if diagonal:
                s = jnp.where(_causal_tile(ta), s, NEG_BIG)
            m_prev = m_scr[ch]
            m_new = jnp.maximum(m_prev, jnp.max(s, axis=-1, keepdims=True))
            alpha = jnp.exp(m_prev - m_new)
            p = jnp.exp(s - m_new)
            l_scr[ch] = alpha * l_scr[ch] + jnp.sum(p, axis=-1, keepdims=True)
            acc_scr[ch] = alpha * acc_scr[ch] + _mm(p.astype(BF16), v)
            m_scr[ch] = m_new

        def full_chunk(j, carry):
            k = _chunk(k_ref, j, ta)
            v = _chunk(v_ref, j, ta)
            crow = c_ref[j]
            for ch in range(nc):
                update(ch, k, v, crow, False)
            return carry

        lax.fori_loop(0, nc * g, full_chunk, 0)
        for jj in range(nc):
            j = nc * g + jj
            k = _chunk(k_ref, j, ta)
            v = _chunk(v_ref, j, ta)
            crow = c_ref[j]
            for ch in range(jj, nc):
                update(ch, k, v, crow, ch == jj)
        for ch in range(nc):
            l = l_scr[ch]
            o_ref[ch * ta:(ch + 1) * ta, :] = acc_scr[ch] / l
            lse_ref[ch * ta:(ch + 1) * ta, :] = m_scr[ch] + jnp.log(l)

    tq = nc * ta
    return pl.pallas_call(
        body, name=name, grid=(H, n_chunks // nc),
        in_specs=[pl.BlockSpec((None, tq, HEAD_DIM), lambda h, g: (0, g, h)),
                  pl.BlockSpec((None, T, HEAD_DIM), lambda h, g: (1, 0, h)),
                  pl.BlockSpec((None, T, HEAD_DIM), lambda h, g: (2, 0, h)),
                  pl.BlockSpec((None, n_chunks, 1, ta), lambda h, g: (h, 0, 0, 0))],
        out_specs=[pl.BlockSpec((tq, HEAD_DIM), lambda h, g: (g, h)),
                   pl.BlockSpec((None, tq, 1), lambda h, g: (h, g, 0))],
        out_shape=[jax.ShapeDtypeStruct((T, W), F32), jax.ShapeDtypeStruct((H, T, 1), F32)],
        scratch_shapes=[pltpu.VMEM((nc, ta, 1), F32), pltpu.VMEM((nc, ta, 1), F32),
                        pltpu.VMEM((nc, ta, HEAD_DIM), F32)],
        compiler_params=_params(("parallel", "arbitrary")),
    )(z7, z7, z7, c_chunks)


def _attn_bwd_q_loop(z7, dob, c_chunks, lse_col, d_col, name):
    _, T, W = z7.shape
    H = W // HEAD_DIM
    ta, nc, n_chunks = _attn_geometry(T)
    scale = np.float32(1.0 / np.sqrt(HEAD_DIM))

    def body(q_ref, k_ref, v_ref, do_ref, c_ref, lse_ref, d_ref, dq_ref, dc_ref, dq_scr, dc_scr):
        g = pl.program_id(1)
        dq_scr[...] = jnp.zeros_like(dq_scr)
        dc_scr[...] = jnp.zeros_like(dc_scr)

        def update(ch, k, v, crow, diagonal):
            rows = slice(ch * ta, (ch + 1) * ta)
            do = do_ref[rows, :]
            s = _mm_nt(q_ref[rows, :], k) * scale - crow - lse_ref[rows, :]
            if diagonal:
                s = jnp.where(_causal_tile(ta), s, NEG_BIG)
            p = jnp.exp(s)
            ds = p * (_mm_nt(do, v) - d_ref[rows, :])
            dq_scr[ch] += _mm(ds.astype(BF16), k)
            dc_scr[ch] += jnp.sum(ds, axis=-1, keepdims=True)

        def full_chunk(j, carry):
            k = _chunk(k_ref, j, ta)
            v = _chunk(v_ref, j, ta)
            crow = c_ref[j]
            for ch in range(nc):
                update(ch, k, v, crow, False)
            return carry

        lax.fori_loop(0, nc * g, full_chunk, 0)
        for jj in range(nc):
            j = nc * g + jj
            k = _chunk(k_ref, j, ta)
            v = _chunk(v_ref, j, ta)
            crow = c_ref[j]
            for ch in range(jj, nc):
                update(ch, k, v, crow, ch == jj)
        for ch in range(nc):
            dq_ref[ch * ta:(ch + 1) * ta, :] = (dq_scr[ch] * scale).astype(BF16)
            dc_ref[ch * ta:(ch + 1) * ta, :] = dc_scr[ch]

    tq = nc * ta
    col = pl.BlockSpec((None, tq, 1), lambda h, g: (h, g, 0))
    return pl.pallas_call(
        body, name=name, grid=(H, n_chunks // nc),
        in_specs=[pl.BlockSpec((None, tq, HEAD_DIM), lambda h, g: (0, g, h)),
                  pl.BlockSpec((None, T, HEAD_DIM), lambda h, g: (1, 0, h)),
                  pl.BlockSpec((None, T, HEAD_DIM), lambda h, g: (2, 0, h)),
                  pl.BlockSpec((tq, HEAD_DIM), lambda h, g: (g, h)),
                  pl.BlockSpec((None, n_chunks, 1, ta), lambda h, g: (h, 0, 0, 0)),
                  col, col],
        out_specs=[pl.BlockSpec((tq, HEAD_DIM), lambda h, g: (g, h)), col],
        out_shape=[jax.ShapeDtypeStruct((T, W), BF16), jax.ShapeDtypeStruct((H, T, 1), F32)],
        scratch_shapes=[pltpu.VMEM((nc, ta, HEAD_DIM), F32), pltpu.VMEM((nc, ta, 1), F32)],
        compiler_params=_params(("parallel", "arbitrary")),
    )(z7, z7, z7, dob, c_chunks, lse_col, d_col)


def _attn_bwd_kv_loop(z7, dob, c_col, lse_chunks, d_chunks, name):
    _, T, W = z7.shape
    H = W // HEAD_DIM
    ta, nc, n_chunks = _attn_geometry(T)
    scale = np.float32(1.0 / np.sqrt(HEAD_DIM))

    def body(k_ref, v_ref, q_ref, do_ref, ccol_ref, lse_ref, d_ref, dk_ref, dv_ref, dc_ref, dk_scr, dv_scr, dc_scr):
        g = pl.program_id(1)
        dk_scr[...] = jnp.zeros_like(dk_scr)
        dv_scr[...] = jnp.zeros_like(dv_scr)
        dc_scr[...] = jnp.zeros_like(dc_scr)

        def update(ch, q, do, lse_row, d_row, diagonal):
            rows = slice(ch * ta, (ch + 1) * ta)
            st = _mm_nt(k_ref[rows, :], q) * scale - ccol_ref[rows, :] - lse_row
            if diagonal:
                st = jnp.where(_causal_tile(ta, keys_on_rows=True), st, NEG_BIG)
            pt = jnp.exp(st)
            dv_scr[ch] += _mm(pt.astype(BF16), do)
            dst = pt * (_mm_nt(v_ref[rows, :], do) - d_row)
            dk_scr[ch] += _mm(dst.astype(BF16), q)
            dc_scr[ch] += jnp.sum(dst, axis=-1, keepdims=True)

        for ii in range(nc):
            i = nc * g + ii
            q = _chunk(q_ref, i, ta)
            do = _chunk(do_ref, i, ta)
            for ch in range(0, ii + 1):
                update(ch, q, do, lse_ref[i], d_ref[i], ch == ii)

        def full_chunk(i, carry):
            q = _chunk(q_ref, i, ta)
            do = _chunk(do_ref, i, ta)
            for ch in range(nc):
                update(ch, q, do, lse_ref[i], d_ref[i], False)
            return carry

        lax.fori_loop(nc * (g + 1), n_chunks, full_chunk, 0)
        for ch in range(nc):
            rows = slice(ch * ta, (ch + 1) * ta)
            dk_ref[rows, :] = (dk_scr[ch] * scale).astype(BF16)
            dv_ref[rows, :] = dv_scr[ch].astype(BF16)
            dc_ref[rows, :] = -dc_scr[ch]

    tk = nc * ta
    chunks = pl.BlockSpec((None, n_chunks, 1, ta), lambda h, g: (h, 0, 0, 0))
    col = pl.BlockSpec((None, tk, 1), lambda h, g: (h, g, 0))
    tile = pl.BlockSpec((tk, HEAD_DIM), lambda h, g: (g, h))
    return pl.pallas_call(
        body, name=name, grid=(H, n_chunks // nc),
        in_specs=[pl.BlockSpec((None, tk, HEAD_DIM), lambda h, g: (1, g, h)),
                  pl.BlockSpec((None, tk, HEAD_DIM), lambda h, g: (2, g, h)),
                  pl.BlockSpec((None, T, HEAD_DIM), lambda h, g: (0, 0, h)),
                  pl.BlockSpec((T, HEAD_DIM), lambda h, g: (0, h)),
                  col, chunks, chunks],
        out_specs=[tile, tile, col],
        out_shape=[jax.ShapeDtypeStruct((T, W), BF16), jax.ShapeDtypeStruct((T, W), BF16),
                   jax.ShapeDtypeStruct((H, T, 1), F32)],
        scratch_shapes=[pltpu.VMEM((nc, ta, HEAD_DIM), F32), pltpu.VMEM((nc, ta, HEAD_DIM), F32),
                        pltpu.VMEM((nc, ta, 1), F32)],
        compiler_params=_params(("parallel", "arbitrary")),
    )(z7, z7, z7, dob, c_col, lse_chunks, d_chunks)


def _chunk_causal_mask():
    rows = lax.broadcasted_iota(jnp.int32, (SGU_LEN, SGU_LEN), 0)
    cols = lax.broadcasted_iota(jnp.int32, (SGU_LEN, SGU_LEN), 1)
    return (cols // CHUNK) <= (rows // CHUNK)


def _sgu_norm_mix(sv, lng_ref, lnb_ref, ws_ref, bs_ref, vn_scr, mixed_scr, vhat_scr=None):
    tm = sv.shape[0]
    vs = _gelu(sv)
    mask = _chunk_causal_mask()
    rstds = []
    for g in range(N_GROUPS):
        lanes = slice(g * GROUP_DIM, (g + 1) * GROUP_DIM)
        blk = vs[:, lanes]
        cen = blk - jnp.mean(blk, axis=-1, keepdims=True)
        rstd = lax.rsqrt(jnp.mean(cen * cen, axis=-1, keepdims=True) + LN_EPS)
        vhat = cen * rstd
        rstds.append(rstd)
        if vhat_scr is not None:
            vhat_scr[:, lanes] = vhat
        vn_scr[:, lanes] = (vhat * lng_ref[:, lanes] + lnb_ref[:, lanes]).astype(BF16)
        wm = jnp.where(mask, ws_ref[g], 0.0).astype(BF16)
        for w in range(tm // SGU_LEN):
            rows = slice(w * SGU_LEN, (w + 1) * SGU_LEN)
            mixed_scr[rows, lanes] = _mm(wm, vn_scr[rows, lanes]) + bs_ref[g]
    return rstds


def _mix_out_fwd(z7, o_a, x1, lng, lnb, ws, bs, w_out, g_post, name):
    _, T, W = z7.shape
    D = x1.shape[1]
    tm = _blk(T, 256)

    def body(u_ref, sv_ref, ga_ref, gb_ref, oa_ref, x1_ref, lng_ref, lnb_ref, ws_ref, bs_ref, wo_ref, gp_ref,
             x2_ref, p_ref, mb_ref, vn_scr, mixed_scr):
        _sgu_norm_mix(sv_ref[...].astype(F32), lng_ref, lnb_ref, ws_ref, bs_ref, vn_scr, mixed_scr)
        o_b = _gelu(u_ref[...].astype(F32)) * mixed_scr[...]
        merged = (jax.nn.sigmoid(ga_ref[...].astype(F32)) * oa_ref[...]
                  + jax.nn.sigmoid(gb_ref[...].astype(F32)) * o_b).astype(BF16)
        mb_ref[...] = merged
        p = _mm(merged, wo_ref[...])
        p_ref[...] = p
        x2_ref[...] = x1_ref[...] + p * _rms_scale(p) * gp_ref[...]

    def seg(idx):
        return pl.BlockSpec((None, tm, W), lambda i, idx=idx: (idx, i, 0))

    row = pl.BlockSpec((tm, D), lambda i: (i, 0))
    vec = pl.BlockSpec((1, D), lambda i: (0, 0))
    return pl.pallas_call(
        body, name=name, grid=(T // tm,),
        in_specs=[seg(3), seg(4), seg(5), seg(6), row, row, vec, vec,
                  pl.BlockSpec((N_GROUPS, SGU_LEN, SGU_LEN), lambda i: (0, 0, 0)),
                  pl.BlockSpec((N_GROUPS, SGU_LEN, 1), lambda i: (0, 0, 0)),
                  pl.BlockSpec((D, D), lambda i: (0, 0)), vec],
        out_specs=[row, row, row],
        out_shape=[jax.ShapeDtypeStruct((T, D), F32), jax.ShapeDtypeStruct((T, D), F32),
                   jax.ShapeDtypeStruct((T, D), BF16)],
        scratch_shapes=[pltpu.VMEM((tm, W), BF16), pltpu.VMEM((tm, W), F32)],
        compiler_params=_params(("parallel",)),
    )(z7, z7, z7, z7, o_a, x1, lng, lnb, ws, bs, w_out, g_post)


def _mix_out_bwd(dx2, p, z7, o_a, lng, lnb, ws, bs, w_out, g_post, name, dep=None):
    _, T, W = z7.shape
    D = dx2.shape[1]
    tm = _blk(T, 256)
    n_w = tm // SGU_LEN

    def body(dx2_ref, p_ref, u_ref, sv_ref, ga_ref, gb_ref, oa_ref, lng_ref, lnb_ref, ws_ref, bs_ref, wo_ref, gp_ref, _,
             dpb_ref, dob_ref, dvec_ref, dz_ref, dgp_ref, dlng_ref, dlnb_ref, dws_ref, dbs_ref,
             vn_scr, mixed_scr, vhat_scr, dmix_scr, dvn_scr):
        @pl.when(pl.program_id(0) == 0)
        def _():
            dgp_ref[...] = jnp.zeros_like(dgp_ref)
            dlng_ref[...] = jnp.zeros_like(dlng_ref)
            dlnb_ref[...] = jnp.zeros_like(dlnb_ref)
            dws_ref[...] = jnp.zeros_like(dws_ref)
            dbs_ref[...] = jnp.zeros_like(dbs_ref)

        pv = p_ref[...]
        s = _rms_scale(pv)
        n = pv * s
        dn = dx2_ref[...]
        dgp_ref[...] += jnp.sum(dn * n, axis=0, keepdims=True)
        dpb = _rms_bwd(dn, n, s, gp_ref[...]).astype(BF16)
        dpb_ref[...] = dpb
        dmerged = _mm_nt(dpb, wo_ref[...])

        sv = sv_ref[...].astype(F32)
        rstds = _sgu_norm_mix(sv, lng_ref, lnb_ref, ws_ref, bs_ref, vn_scr, mixed_scr, vhat_scr)
        u_pre = u_ref[...].astype(F32)
        u = _gelu(u_pre)
        mixed = mixed_scr[...]
        sa = jax.nn.sigmoid(ga_ref[...].astype(F32))
        sb = jax.nn.sigmoid(gb_ref[...].astype(F32))
        oa = oa_ref[...]
        do_a = (dmerged * sa).astype(BF16)
        dob_ref[...] = do_a
        prod = do_a.astype(F32) * oa
        for h in range(N_HEADS):
            dvec_ref[h] = jnp.sum(prod[:, h * HEAD_DIM:(h + 1) * HEAD_DIM], axis=-1, keepdims=True)
        dz_ref[2] = (dmerged * oa * (sa * (1.0 - sa))).astype(BF16)
        dz_ref[3] = (dmerged * (u * mixed) * (sb * (1.0 - sb))).astype(BF16)
        do_b = dmerged * sb
        dz_ref[0] = (do_b * mixed * _gelu_grad(u_pre)).astype(BF16)
        dmix_scr[...] = do_b * u

        mask = _chunk_causal_mask()
        for g in range(N_GROUPS):
            lanes = slice(g * GROUP_DIM, (g + 1) * GROUP_DIM)
            wm = jnp.where(mask, ws_ref[g], 0.0).astype(BF16)
            dws = jnp.zeros((SGU_LEN, SGU_LEN), F32)
            dbs = jnp.zeros((SGU_LEN, 1), F32)
            for w in range(n_w):
                rows = slice(w * SGU_LEN, (w + 1) * SGU_LEN)
                dmix = dmix_scr[rows, lanes]
                dmix_b = dmix.astype(BF16)
                dvn_scr[rows, lanes] = _mm_tn(wm, dmix_b)
                dws = dws + _mm_nt(dmix_b, vn_scr[rows, lanes])
                dbs = dbs + jnp.sum(dmix, axis=-1, keepdims=True)
            dws_ref[g] += jnp.where(mask, dws, 0.0)
            dbs_ref[g] += dbs
            dvn = dvn_scr[:, lanes]
            vhat = vhat_scr[:, lanes]
            dlng_ref[:, lanes] += jnp.sum(dvn * vhat, axis=0, keepdims=True)
            dlnb_ref[:, lanes] += jnp.sum(dvn, axis=0, keepdims=True)
            dvh = dvn * lng_ref[:, lanes]
            dvs = rstds[g] * (dvh - jnp.mean(dvh, axis=-1, keepdims=True)
                              - vhat * jnp.mean(dvh * vhat, axis=-1, keepdims=True))
            dvn_scr[:, lanes] = dvs
        dz_ref[1] = (dvn_scr[...] * _gelu_grad(sv)).astype(BF16)

    def seg(idx):
        return pl.BlockSpec((None, tm, W), lambda i, idx=idx: (idx, i, 0))

    row = pl.BlockSpec((tm, D), lambda i: (i, 0))
    vec = pl.BlockSpec((1, D), lambda i: (0, 0))
    ws_spec = pl.BlockSpec((N_GROUPS, SGU_LEN, SGU_LEN), lambda i: (0, 0, 0))
    bs_spec = pl.BlockSpec((N_GROUPS, SGU_LEN, 1), lambda i: (0, 0, 0))
    return pl.pallas_call(
        body, name=name, grid=(T // tm,),
        in_specs=[row, row, seg(3), seg(4), seg(5), seg(6), row, vec, vec, ws_spec, bs_spec,
                  pl.BlockSpec((D, D), lambda i: (0, 0)), vec, ANY],
        out_specs=[row, row, pl.BlockSpec((N_HEADS, tm, 1), lambda i: (0, i, 0)),
                   pl.BlockSpec((4, tm, W), lambda i: (0, i, 0)), vec, vec, vec, ws_spec, bs_spec],
        out_shape=[jax.ShapeDtypeStruct((T, D), BF16), jax.ShapeDtypeStruct((T, W), BF16),
                   jax.ShapeDtypeStruct((N_HEADS, T, 1), F32), jax.ShapeDtypeStruct((4, T, W), BF16),
                   jax.ShapeDtypeStruct((1, D), F32), jax.ShapeDtypeStruct((1, D), F32),
                   jax.ShapeDtypeStruct((1, D), F32),
                   jax.ShapeDtypeStruct((N_GROUPS, SGU_LEN, SGU_LEN), F32),
                   jax.ShapeDtypeStruct((N_GROUPS, SGU_LEN, 1), F32)],
        scratch_shapes=[pltpu.VMEM((tm, W), BF16), pltpu.VMEM((tm, W), F32), pltpu.VMEM((tm, W), F32),
                        pltpu.VMEM((tm, W), F32), pltpu.VMEM((tm, W), F32)],
        compiler_params=_params(("arbitrary",)),
    )(dx2, p, z7, z7, z7, z7, o_a, lng, lnb, ws, bs, w_out, g_post, _after(dep))


def _loss_head(y, target, name):
    T, D = y.shape
    tm = _blk(T, 1024)
    n_i = T // tm

    def body(y_ref, t_ref, dy_ref, loss_ref, acc_scr):
        i = pl.program_id(0)

        @pl.when(i == 0)
        def _():
            acc_scr[...] = jnp.zeros_like(acc_scr)

        e = y_ref[...] - t_ref[...]
        dy_ref[...] = e * np.float32(1.0 / D)
        acc_scr[...] += jnp.sum(e * e, axis=0, keepdims=True)

        @pl.when(i == n_i - 1)
        def _():
            total = jnp.sum(acc_scr[...], axis=-1, keepdims=True) * np.float32(0.5 / D)
            loss_ref[...] = jnp.broadcast_to(total, loss_ref.shape)

    row = pl.BlockSpec((tm, D), lambda i: (i, 0))
    return pl.pallas_call(
        body, name=name, grid=(n_i,),
        in_specs=[row, row],
        out_specs=[row, pl.BlockSpec((1, LANES), lambda i: (0, 0))],
        out_shape=[jax.ShapeDtypeStruct((T, D), F32), jax.ShapeDtypeStruct((1, LANES), F32)],
        scratch_shapes=[pltpu.VMEM((1, D), F32)],
        compiler_params=_params(("arbitrary",)),
    )(y, target)


def _adamw_math(w, g, m, v):
    m_new = ADAM_B1 * m + (1.0 - ADAM_B1) * g
    v_new = ADAM_B2 * v + (1.0 - ADAM_B2) * (g * g)
    m_hat = m_new / np.float32(1.0 - ADAM_B1 ** ADAM_STEP)
    v_hat = v_new / np.float32(1.0 - ADAM_B2 ** ADAM_STEP)
    delta = -ADAM_LR * (m_hat / (jnp.sqrt(v_hat) + ADAM_EPS) + ADAM_WD * w)
    return delta, m_new, v_new


def _sum_adamw(parts, w, m, v, name):
    n, R, C = parts.shape
    tr = _blk(R, 128)

    def body(p_ref, w_ref, m_ref, v_ref, g_ref, d_ref, mo_ref, vo_ref):
        g = p_ref[0].astype(F32)
        for s in range(1, n):
            g = g + p_ref[s].astype(F32)
        delta, m_new, v_new = _adamw_math(w_ref[...], g, m_ref[...], v_ref[...])
        g_ref[...] = g
        d_ref[...] = delta
        mo_ref[...] = m_new
        vo_ref[...] = v_new

    row = pl.BlockSpec((tr, C), lambda i: (i, 0))
    shp = jax.ShapeDtypeStruct((R, C), F32)
    return pl.pallas_call(
        body, name=name, grid=(R // tr,),
        in_specs=[pl.BlockSpec((n, tr, C), lambda i: (0, i, 0)), row, row, row],
        out_specs=[row, row, row, row], out_shape=[shp, shp, shp, shp],
        compiler_params=_params(("parallel",)),
    )(parts, w, m, v)


def _adamw(g, w, m, v, name):
    R, C = g.shape
    tr = _blk(R, 128)

    def body(g_ref, w_ref, m_ref, v_ref, d_ref, mo_ref, vo_ref):
        delta, m_new, v_new = _adamw_math(w_ref[...], g_ref[...], m_ref[...], v_ref[...])
        d_ref[...] = delta
        mo_ref[...] = m_new
        vo_ref[...] = v_new

    row = pl.BlockSpec((tr, C), lambda i: (i, 0))
    shp = jax.ShapeDtypeStruct((R, C), F32)
    return pl.pallas_call(
        body, name=name, grid=(R // tr,),
        in_specs=[row, row, row, row], out_specs=[row, row, row], out_shape=[shp, shp, shp],
        compiler_params=_params(("parallel",)),
    )(g, w, m, v)


def _position():
    return lax.axis_index("x"), lax.axis_index("y"), lax.axis_index("c")


def _slot(px, py, pc):
    return 4 * px + 2 * py + pc


def _all_gather(shards, name):
    n = len(shards)

    def body(*refs):
        ins, outs = refs[:n], refs[n:2 * n]
        send_sems, recv_sems, local_sems = refs[2 * n:]
        x, y, c = _position()
        me, sibling = (x, y, c), (x, y, 1 - c)
        chips = [(1 - x, y), (x, 1 - y), (1 - x, 1 - y)]

        def copy(a, k, block, to, src=None):
            dst = outs[a].at[_slot(*block)]
            return pltpu.make_async_remote_copy(
                src_ref=dst if src is None else src, dst_ref=dst,
                send_sem=send_sems.at[a, k], recv_sem=recv_sems.at[a, k],
                device_id=to, device_id_type=MESH)

        mine = [pltpu.make_async_copy(ins[a], outs[a].at[_slot(*me)], local_sems.at[a]) for a in range(n)]
        for cp in mine:
            cp.start()
        first = []
        for a in range(n):
            first.append(copy(a, 0, me, sibling, src=ins[a]))
            first += [copy(a, 1 + j, me, (*chip, c), src=ins[a]) for j, chip in enumerate(chips)]
        for cp in first:
            cp.start()
        passed = []
        for j, chip in enumerate(chips):
            for a in range(n):
                copy(a, 1 + j, (*chip, c), me).wait_recv()
                fwd = copy(a, 4 + j, (*chip, c), sibling)
                fwd.start()
                passed.append(fwd)
        for a in range(n):
            copy(a, 0, sibling, me).wait_recv()
            for j, chip in enumerate(chips):
                copy(a, 4 + j, (*chip, 1 - c), me).wait_recv()
        for cp in first + passed:
            cp.wait_send()
        for cp in mine:
            cp.wait()

    return pl.pallas_call(
        body, name=name,
        in_specs=[ANY] * n, out_specs=[ANY] * n,
        out_shape=[jax.ShapeDtypeStruct((N_DEV,) + s.shape, s.dtype) for s in shards],
        scratch_shapes=[pltpu.SemaphoreType.DMA((n, 7)), pltpu.SemaphoreType.DMA((n, 7)),
                        pltpu.SemaphoreType.DMA((n,))],
    )(*shards)


def _peer(x, y, c, k):
    return (1 - x if k & 4 else x, 1 - y if k & 2 else y, 1 - c if k & 1 else c)


def _exchange(parts, name):
    n = len(parts)

    def body(*refs):
        ins, outs = refs[:n], refs[n:2 * n]
        send_sems, recv_sems, local_sems = refs[2 * n:]
        x, y, c = _position()
        me = _slot(x, y, c)
        mine = [pltpu.make_async_copy(ins[a].at[me], outs[a].at[me], local_sems.at[a]) for a in range(n)]
        for cp in mine:
            cp.start()
        sends = []
        for k in range(1, N_DEV):
            to = _peer(x, y, c, k)
            for a in range(n):
                cp = pltpu.make_async_remote_copy(
                    src_ref=ins[a].at[_slot(*to)], dst_ref=outs[a].at[me],
                    send_sem=send_sems.at[a, k - 1], recv_sem=recv_sems.at[a, k - 1],
                    device_id=to, device_id_type=MESH)
                cp.start()
                sends.append(cp)
        for k in range(1, N_DEV):
            frm = _peer(x, y, c, k)
            for a in range(n):
                pltpu.make_async_remote_copy(
                    src_ref=ins[a].at[_slot(*frm)], dst_ref=outs[a].at[_slot(*frm)],
                    send_sem=send_sems.at[a, k - 1], recv_sem=recv_sems.at[a, k - 1],
                    device_id=frm, device_id_type=MESH).wait_recv()
        for cp in sends:
            cp.wait_send()
        for cp in mine:
            cp.wait()

    return pl.pallas_call(
        body, name=name,
        in_specs=[ANY] * n, out_specs=[ANY] * n,
        out_shape=[jax.ShapeDtypeStruct(p.shape, p.dtype) for p in parts],
        scratch_shapes=[pltpu.SemaphoreType.DMA((n, 7)), pltpu.SemaphoreType.DMA((n, 7)),
                        pltpu.SemaphoreType.DMA((n,))],
    )(*parts)


HBM_SPEC = pl.BlockSpec(memory_space=pltpu.HBM)
SEM_SPEC = pl.BlockSpec(memory_space=pltpu.SEMAPHORE)
SIDE_EFFECT = pltpu.SideEffectType.DATAFLOW_SIDE_EFFECTING


def _remote_copies(src_refs, land_refs, send_sems, recv_sems, gather, outgoing):
    x, y, c = _position()
    me = _slot(x, y, c)
    copies = []
    for k in range(1, N_DEV):
        peer = _peer(x, y, c, k)
        for a in range(len(src_refs)):
            src = src_refs[a] if gather else src_refs[a].at[_slot(*peer)]
            dst = land_refs[a].at[me if outgoing else _slot(*peer)]
            sem = a * (N_DEV - 1) + k - 1
            copies.append(pltpu.make_async_remote_copy(
                src_ref=src, dst_ref=dst, send_sem=send_sems.at[sem], recv_sem=recv_sems.at[sem],
                device_id=peer, device_id_type=MESH))
    return copies


def _remote_start(srcs, after, name, gather):
    n = len(srcs)
    lands = [jax.ShapeDtypeStruct(((N_DEV,) + s.shape) if gather else s.shape, s.dtype) for s in srcs]

    def body(*refs):
        src_refs, land_refs = refs[:n], refs[n:2 * n]
        send_sems, recv_sems = refs[2 * n + 1], refs[2 * n + 2]
        token, local_sems = refs[4 * n + 3], refs[4 * n + 4]
        x, y, c = _position()
        me = _slot(x, y, c)
        mine = [pltpu.make_async_copy(src_refs[a] if gather else src_refs[a].at[me], land_refs[a].at[me],
                                      local_sems.at[a]) for a in range(n)]
        for cp in mine:
            cp.start()
        for cp in _remote_copies(src_refs, land_refs, send_sems, recv_sems, gather, outgoing=True):
            cp.start()
        for cp in mine:
            cp.wait()
        token[...] = jnp.zeros_like(token)

    sem_shape = pltpu.SemaphoreType.DMA((n * (N_DEV - 1),))
    outs = pl.pallas_call(
        body, name=name,
        out_shape=(sem_shape, sem_shape, *[pltpu.HBM(s.shape, s.dtype) for s in srcs],
                   *[pltpu.HBM(l.shape, l.dtype) for l in lands], jax.ShapeDtypeStruct((8, LANES), F32)),
        in_specs=[HBM_SPEC] * (2 * n) + [ANY],
        out_specs=(SEM_SPEC, SEM_SPEC, *([HBM_SPEC] * (2 * n)), pl.BlockSpec(memory_space=pltpu.VMEM)),
        input_output_aliases={a: 2 + a for a in range(2 * n)},
        scratch_shapes=[pltpu.SemaphoreType.DMA((n,))],
        compiler_params=pltpu.CompilerParams(has_side_effects=SIDE_EFFECT),
    )(*[pltpu.with_memory_space_constraint(s, pltpu.HBM) for s in srcs],
      *[pltpu.with_memory_space_constraint(lax.empty(l.shape, l.dtype), pltpu.HBM) for l in lands], after)
    return dict(send=outs[0], recv=outs[1], srcs=outs[2:2 + n], lands=outs[2 + n:2 + 2 * n], token=outs[-1],
                gather=gather)


def _remote_wait(flight, after, name):
    n = len(flight["srcs"])
    gather = flight["gather"]

    def body(*refs):
        src_refs, land_refs = refs[:n], refs[n:2 * n]
        send_sems, recv_sems = refs[2 * n], refs[2 * n + 1]
        for cp in _remote_copies(src_refs, land_refs, send_sems, recv_sems, gather, outgoing=False):
            cp.wait_send()
            cp.wait_recv()

    both = list(flight["srcs"]) + list(flight["lands"])
    outs = pl.pallas_call(
        body, name=name,
        out_shape=tuple(pltpu.HBM(a.shape, a.dtype) for a in both),
        in_specs=[HBM_SPEC] * (2 * n) + [SEM_SPEC, SEM_SPEC, ANY],
        out_specs=tuple([HBM_SPEC] * (2 * n)),
        input_output_aliases={a: a for a in range(2 * n)},
        compiler_params=pltpu.CompilerParams(has_side_effects=SIDE_EFFECT),
    )(*both, flight["send"], flight["recv"], after)
    return list(outs[n:])


def _all_reduce_small(blob, name):
    R, C = blob.shape

    def body(in_ref, out_ref, gath, send_sems, recv_sems):
        x, y, c = _position()
        me = _slot(x, y, c)
        gath[me] = in_ref[...]
        sends = []
        for k in range(1, N_DEV):
            to = _peer(x, y, c, k)
            cp = pltpu.make_async_remote_copy(
                src_ref=in_ref, dst_ref=gath.at[me],
                send_sem=send_sems.at[k - 1], recv_sem=recv_sems.at[k - 1],
                device_id=to, device_id_type=MESH)
            cp.start()
            sends.append(cp)
        for k in range(1, N_DEV):
            frm = _peer(x, y, c, k)
            pltpu.make_async_remote_copy(
                src_ref=in_ref, dst_ref=gath.at[_slot(*frm)],
                send_sem=send_sems.at[k - 1], recv_sem=recv_sems.at[k - 1],
                device_id=frm, device_id_type=MESH).wait_recv()
        for cp in sends:
            cp.wait_send()
        total = gath[0]
        for s in range(1, N_DEV):
            total = total + gath[s]
        out_ref[...] = total

    return pl.pallas_call(
        body, name=name,
        in_specs=[pl.BlockSpec(memory_space=pltpu.VMEM)],
        out_specs=pl.BlockSpec(memory_space=pltpu.VMEM),
        out_shape=jax.ShapeDtypeStruct((R, C), F32),
        scratch_shapes=[pltpu.VMEM((N_DEV, R, C), F32), pltpu.SemaphoreType.DMA((7,)),
                        pltpu.SemaphoreType.DMA((7,))],
        compiler_params=pltpu.CompilerParams(vmem_limit_bytes=VMEM_LIMIT),
    )(blob)


SMALL_VECS = ("ffn1_pre_g", "ffn1_post_g", "mix_pre_g", "sgu_ln_g", "sgu_ln_b", "mix_post_g", "ffn2_pre_g",
              "ffn2_post_g")
ROW_BS = len(SMALL_VECS)
ROW_BF = ROW_BS + 1
ROW_LOSS = ROW_BF + 1
ROW_WS = 16
BLOB_ROWS = ROW_WS + SGU_LEN


def _pack_small(vals, D, loss_row=None):
    rows = [vals[n].reshape(1, D) for n in SMALL_VECS]
    rows.append(vals["sgu_b_s"].reshape(1, D))
    rows.append(jnp.pad(vals["b_forget"].reshape(1, N_HEADS), ((0, 0), (0, D - N_HEADS))))
    rows.append(jnp.zeros((1, D), F32) if loss_row is None else loss_row)
    rows.append(jnp.zeros((ROW_WS - ROW_LOSS - 1, D), F32))
    rows.append(vals["sgu_w_s"].reshape(SGU_LEN, D))
    return jnp.concatenate(rows, axis=0)


def _unpack_small(blob, D):
    out = {n: blob[r:r + 1] for r, n in enumerate(SMALL_VECS)}
    out["sgu_b_s"] = blob[ROW_BS].reshape(1, N_GROUPS, SGU_LEN)
    out["b_forget"] = blob[ROW_BF, :N_HEADS].reshape(1, N_HEADS)
    out["sgu_w_s"] = blob[ROW_WS:].reshape(1, N_GROUPS, SGU_LEN, SGU_LEN)
    return out


WEIGHT_NAMES = ("ffn1_pre_g", "ffn1_w_gate", "ffn1_w_up", "ffn1_w_down", "ffn1_post_g", "mix_pre_g", "w_in",
                "b_forget", "sgu_ln_g", "sgu_ln_b", "sgu_w_s", "sgu_b_s", "w_out", "mix_post_g", "ffn2_pre_g",
                "ffn2_w_gate", "ffn2_w_up", "ffn2_w_down", "ffn2_post_g")
BIG_NAMES = ("ffn1_w_gate", "ffn1_w_up", "ffn1_w_down", "w_in", "w_out", "ffn2_w_gate", "ffn2_w_up", "ffn2_w_down")
WEIGHT_GROUPS = {"ffn1": ("ffn1_w_gate", "ffn1_w_up", "ffn1_w_down"), "mix": ("w_in", "w_out"),
                 "ffn2": ("ffn2_w_gate", "ffn2_w_up", "ffn2_w_down")}
GRAD_GROUPS = (("ffn2_w_gate", "ffn2_w_up", "ffn2_w_down"), ("w_out", "w_in"), ("ffn1_w_down", "ffn1_w_gate"),
               ("ffn1_w_up",))


def _local_step(x, target, small, fetch, emit):
    T, D = x.shape
    W = N_HEADS * HEAD_DIM
    vec = lambda n: small[n].reshape(1, D)
    big = dict(fetch("ffn1", x))

    x1, y1, gate1, up1 = _ffn_fwd(x, vec("ffn1_pre_g"), big["ffn1_w_gate"], big["ffn1_w_up"], big["ffn1_w_down"],
                                  vec("ffn1_post_g"), "ffn1_fwd")

    big.update(fetch("mix", x1))
    w_in_all = big["w_in"]
    in_width = N_DEV * w_in_all.shape[2]
    w_in = w_in_all.transpose(1, 0, 2).reshape(D, in_width)
    col_f = 3 * W
    col_u = col_f + N_HEADS
    seg_starts = (0, W, 2 * W, col_u, col_u + W, col_u + 2 * W, col_u + 3 * W)
    w7 = jnp.stack([w_in[:, s:s + W] for s in seg_starts])
    wf = jnp.pad(w_in[:, col_f:col_u], ((0, 0), (0, LANES - N_HEADS)))
    w_out = big["w_out"].reshape(D, D)
    b_pad = jnp.pad(small["b_forget"].reshape(1, N_HEADS), ((0, 0), (0, LANES - N_HEADS)))
    lng, lnb = vec("sgu_ln_g"), vec("sgu_ln_b")
    ws = small["sgu_w_s"].reshape(N_GROUPS, SGU_LEN, SGU_LEN)
    bs = small["sgu_b_s"].reshape(N_GROUPS, SGU_LEN, 1)

    z7, f_logit, h2b = _mix_in_fwd(x1, vec("mix_pre_g"), w7, wf, "mix_in_fwd")
    c = _forget_cumsum(f_logit, b_pad, "forget_cumsum")
    c_heads = c[:, :N_HEADS].T
    ta, _, n_chunks = _attn_geometry(T)
    c_chunks = c_heads.reshape(N_HEADS, n_chunks, 1, ta)
    c_col = c_heads[:, :, None]
    o_a, lse = _attn_fwd_loop(z7, c_chunks, "attn_fwd")
    x2, p, merged_b = _mix_out_fwd(z7, o_a, x1, lng, lnb, ws, bs, w_out, vec("mix_post_g"), "mix_out_fwd")
    big.update(fetch("ffn2", x2))
    x3, y2, gate2, up2 = _ffn_fwd(x2, vec("ffn2_pre_g"), big["ffn2_w_gate"], big["ffn2_w_up"], big["ffn2_w_down"],
                                  vec("ffn2_post_g"), "ffn2_fwd")
    dy, loss_lanes = _loss_head(x3, target, "loss_head")

    grads_small = {}

    dx2, h3b, dy2b, act2, dgate2, dup2, dgpre, dgpost = _ffn_bwd(
        dy, x2, y2, gate2, up2, vec("ffn2_pre_g"), big["ffn2_w_gate"], big["ffn2_w_up"], big["ffn2_w_down"],
        vec("ffn2_post_g"), "ffn2_bwd")
    grads_small["ffn2_pre_g"] = jnp.sum(dgpre, axis=0)
    grads_small["ffn2_post_g"] = jnp.sum(dgpost, axis=0)
    emit("ffn2_w_gate", _wgrad(h3b, dgate2, "ffn2_wgrad_gate", shard_cols=True))
    emit("ffn2_w_up", _wgrad(h3b, dup2, "ffn2_wgrad_up", shard_cols=True))
    dep = emit("ffn2_w_down", _wgrad(act2, dy2b, "ffn2_wgrad_down").reshape(big["ffn2_w_down"].shape))

    dpb, dob, dvec, dz4, dgp, dlng, dlnb, dws, dbs = _mix_out_bwd(
        dx2, p, z7, o_a, lng, lnb, ws, bs, w_out, vec("mix_post_g"), "mix_out_bwd", dep=dep)
    grads_small["mix_post_g"] = dgp
    grads_small["sgu_ln_g"] = dlng
    grads_small["sgu_ln_b"] = dlnb
    grads_small["sgu_w_s"] = dws
    grads_small["sgu_b_s"] = dbs
    emit("w_out", _wgrad(merged_b, dpb, "w_out_wgrad").reshape(big["w_out"].shape))
    lse_chunks = lse.reshape(N_HEADS, n_chunks, 1, ta)
    d_chunks = dvec.reshape(N_HEADS, n_chunks, 1, ta)
    dk, dv, dc = _attn_bwd_kv_loop(z7, dob, c_col, lse_chunks, d_chunks, "attn_bwd_kv")
    dq, dc_q = _attn_bwd_q_loop(z7, dob, c_chunks, lse, dvec, "attn_bwd_q")
    dc_pad = jnp.pad((dc + dc_q).reshape(N_HEADS, T).T, ((0, 0), (0, LANES - N_HEADS)))
    dfb, dbf = _forget_bwd(dc_pad, f_logit, b_pad, "forget_bwd")
    grads_small["b_forget"] = dbf[:, :N_HEADS]
    segs = [(dq, None), (dk, None), (dv, None), (dz4, 0), (dz4, 1), (dz4, 2), (dz4, 3)]
    dx1, dgm = _mix_in_bwd(dx2, x1, vec("mix_pre_g"), segs, dfb, w7, wf, "mix_in_bwd")
    grads_small["mix_pre_g"] = jnp.sum(dgm, axis=0)
    seg_mats = [dq, dk, dv, dz4[0], dz4[1], dz4[2], dz4[3]]
    dw_seg = [_wgrad(h2b, sm, "w_in_wgrad_%d" % q) for q, sm in enumerate(seg_mats)]
    dwf = _wgrad(h2b, dfb, "w_in_wgrad_f")[:, :N_HEADS]
    dw_in = jnp.concatenate(dw_seg[:3] + [dwf] + dw_seg[3:], axis=1)
    dep = emit("w_in", dw_in.reshape(D, N_DEV, in_width // N_DEV).transpose(1, 0, 2))

    dx0, h1b, dy1b, act1, dgate1, dup1, dgpre1, dgpost1 = _ffn_bwd(
        dx1, x, y1, gate1, up1, vec("ffn1_pre_g"), big["ffn1_w_gate"], big["ffn1_w_up"], big["ffn1_w_down"],
        vec("ffn1_post_g"), "ffn1_bwd", dep=dep)
    grads_small["ffn1_pre_g"] = jnp.sum(dgpre1, axis=0)
    grads_small["ffn1_post_g"] = jnp.sum(dgpost1, axis=0)
    emit("ffn1_w_down", _wgrad(act1, dy1b, "ffn1_wgrad_down").reshape(big["ffn1_w_down"].shape))
    dep = emit("ffn1_w_gate", _wgrad(h1b, dgate1, "ffn1_wgrad_gate", shard_cols=True))
    emit("ffn1_w_up", _wgrad(h1b, dup1, "ffn1_wgrad_up", shard_cols=True, dep=dep))

    loss_row = jnp.pad(loss_lanes, ((0, 0), (0, D - LANES)))
    return loss_row, dx0, grads_small


def kernel(x, ffn1_pre_g, ffn1_w_gate, ffn1_w_up, ffn1_w_down, ffn1_post_g, mix_pre_g, w_in, b_forget, sgu_ln_g, sgu_ln_b, sgu_w_s, sgu_b_s, w_out, mix_post_g, ffn2_pre_g, ffn2_w_gate, ffn2_w_up, ffn2_w_down, ffn2_post_g, loss_target, m_ffn1_pre_g, m_ffn1_w_gate, m_ffn1_w_up, m_ffn1_w_down, m_ffn1_post_g, m_mix_pre_g, m_w_in, m_b_forget, m_sgu_ln_g, m_sgu_ln_b, m_sgu_w_s, m_sgu_b_s, m_w_out, m_mix_post_g, m_ffn2_pre_g, m_ffn2_w_gate, m_ffn2_w_up, m_ffn2_w_down, m_ffn2_post_g, v_ffn1_pre_g, v_ffn1_w_gate, v_ffn1_w_up, v_ffn1_w_down, v_ffn1_post_g, v_mix_pre_g, v_w_in, v_b_forget, v_sgu_ln_g, v_sgu_ln_b, v_sgu_w_s, v_sgu_b_s, v_w_out, v_mix_post_g, v_ffn2_pre_g, v_ffn2_w_gate, v_ffn2_w_up, v_ffn2_w_down, v_ffn2_post_g):
    weights = dict(zip(WEIGHT_NAMES, (ffn1_pre_g, ffn1_w_gate, ffn1_w_up, ffn1_w_down, ffn1_post_g, mix_pre_g, w_in,
                                      b_forget, sgu_ln_g, sgu_ln_b, sgu_w_s, sgu_b_s, w_out, mix_post_g, ffn2_pre_g,
                                      ffn2_w_gate, ffn2_w_up, ffn2_w_down, ffn2_post_g)))
    mom1 = dict(zip(WEIGHT_NAMES, (m_ffn1_pre_g, m_ffn1_w_gate, m_ffn1_w_up, m_ffn1_w_down, m_ffn1_post_g,
                                   m_mix_pre_g, m_w_in, m_b_forget, m_sgu_ln_g, m_sgu_ln_b, m_sgu_w_s, m_sgu_b_s,
                                   m_w_out, m_mix_post_g, m_ffn2_pre_g, m_ffn2_w_gate, m_ffn2_w_up, m_ffn2_w_down,
                                   m_ffn2_post_g)))
    mom2 = dict(zip(WEIGHT_NAMES, (v_ffn1_pre_g, v_ffn1_w_gate, v_ffn1_w_up, v_ffn1_w_down, v_ffn1_post_g,
                                   v_mix_pre_g, v_w_in, v_b_forget, v_sgu_ln_g, v_sgu_ln_b, v_sgu_w_s, v_sgu_b_s,
                                   v_w_out, v_mix_post_g, v_ffn2_pre_g, v_ffn2_w_gate, v_ffn2_w_up, v_ffn2_w_down,
                                   v_ffn2_post_g)))
    D = x.shape[-1]
    small_names = [n for n in WEIGHT_NAMES if n not in BIG_NAMES]

    small = {n: weights[n] for n in small_names}
    shard = lambda n: weights[n][0].astype(BF16)

    ffn1_full = _all_gather([shard(n) for n in WEIGHT_GROUPS["ffn1"]], "ffn1_all_gather")
    gathers = {grp: _remote_start([shard(n) for n in WEIGHT_GROUPS[grp]], ffn1_full[0], grp + "_gather_start",
                                  gather=True) for grp in ("mix", "ffn2")}

    def fetch(group, after):
        if group == "ffn1":
            return zip(WEIGHT_GROUPS[group], ffn1_full)
        return zip(WEIGHT_GROUPS[group], _remote_wait(gathers[group], after, group + "_gather_wait"))

    ready, flights = {}, []

    def emit(name, part):
        ready[name] = part
        for group in GRAD_GROUPS:
            if name == group[-1]:
                flights.append((group, _remote_start([ready[n] for n in group], part, name + "_grad_start",
                                                     gather=False)))
                return flights[-1][1]["token"]
        return None

    loss_row, grad_x, grads_small = _local_step(x[0], loss_target[0], small, fetch, emit)

    blob = _all_reduce_small(_pack_small(grads_small, D, loss_row) + flights[-1][1]["token"][:1, :1],
                             "small_all_reduce")

    out = {}
    after = blob
    for group, flight in flights:
        received = _remote_wait(flight, after, group[-1] + "_grad_wait")
        for n, rcv in zip(group, received):
            g, d, m_new, v_new = _sum_adamw(rcv, weights[n][0], mom1[n][0], mom2[n][0], "adamw_" + n)
            out[n] = tuple(a[None] for a in (g, d, m_new, v_new))
            after = g

    d_blob, m_blob, v_blob = _adamw(blob, _pack_small(small, D), _pack_small({n: mom1[n] for n in small_names}, D),
                                    _pack_small({n: mom2[n] for n in small_names}, D), "adamw_small")
    unpacked = [_unpack_small(b, D) for b in (blob, d_blob, m_blob, v_blob)]
    for n in small_names:
        out[n] = tuple(u[n].reshape(weights[n].shape) for u in unpacked)

    loss = blob[ROW_LOSS, 0]
    result = [loss, grad_x[None]]
    for k in range(4):
        result += [out[n][k] for n in WEIGHT_NAMES]
    return tuple(result)
```

```python
import functools

import numpy as np
import jax
import jax.numpy as jnp
from jax import lax
from jax.experimental import pallas as pl
from jax.experimental.pallas import tpu as pltpu

F32 = jnp.float32
BF16 = jnp.bfloat16

RMS_EPS = 1e-6
LN_EPS = 1e-5
HEAD_DIM = 128
N_HEADS = 8
GROUP_DIM = 128
N_GROUPS = 8
SGU_LEN = 128
CHUNK = 64
N_DEV = 8
LANES = 128
VMEM_LIMIT = 56 * 1024 * 1024
NEG_BIG = -1e30

ADAM_LR = 0.001
ADAM_B1 = 0.9
ADAM_B2 = 0.999
ADAM_EPS = 1e-08
ADAM_WD = 0.01
ADAM_STEP = 10

MESH = pl.DeviceIdType.MESH
ANY = pl.BlockSpec(memory_space=pl.ANY)


def _blk(n, pref):
    return pref if (n >= pref and n % pref == 0) else n


def _mm(a, b):
    return jnp.dot(a, b, preferred_element_type=F32)


def _mm_nt(a, b):
    return lax.dot_general(a, b, (((1,), (1,)), ((), ())), preferred_element_type=F32)


def _mm_tn(a, b):
    return lax.dot_general(a, b, (((0,), (0,)), ((), ())), preferred_element_type=F32)


def _params(sem):
    return pltpu.CompilerParams(dimension_semantics=sem, vmem_limit_bytes=VMEM_LIMIT)


def _gelu(x):
    return 0.5 * x * (1.0 + lax.erf(x * np.float32(1.0 / np.sqrt(2.0))))


def _gelu_grad(x):
    cdf = 0.5 * (1.0 + lax.erf(x * np.float32(1.0 / np.sqrt(2.0))))
    return cdf + x * jnp.exp(-0.5 * x * x) * np.float32(1.0 / np.sqrt(2.0 * np.pi))


def _rms_scale(v):
    return lax.rsqrt(jnp.mean(v * v, axis=-1, keepdims=True) + RMS_EPS)


def _rms_bwd(dy, xhat, r, g):
    dxh = dy * g
    return r * (dxh - xhat * jnp.mean(dxh * xhat, axis=-1, keepdims=True))


def _ffn_fwd(x, g_pre, wg, wu, wd, g_post, name):
    T, D = x.shape
    ns, _, fs = wg.shape
    tm = _blk(T, 512)

    def body(x_ref, gpre_ref, wg_ref, wu_ref, wd_ref, gpost_ref, xo_ref, y_ref, g_ref, u_ref, h_scr, acc_scr):
        j = pl.program_id(1)

        @pl.when(j == 0)
        def _():
            xv = x_ref[...]
            h_scr[...] = (xv * _rms_scale(xv) * gpre_ref[...]).astype(BF16)
            acc_scr[...] = jnp.zeros_like(acc_scr)

        h = h_scr[...]
        gg = _mm(h, wg_ref[...])
        uu = _mm(h, wu_ref[...])
        a = gg * jax.nn.sigmoid(gg) * uu
        g_ref[...] = gg.astype(BF16)
        u_ref[...] = uu.astype(BF16)
        acc_scr[...] += _mm(a.astype(BF16), wd_ref[...])

        @pl.when(j == ns - 1)
        def _():
            y = acc_scr[...]
            y_ref[...] = y
            xo_ref[...] = x_ref[...] + 0.5 * (y * _rms_scale(y) * gpost_ref[...])

    row = pl.BlockSpec((tm, D), lambda i, j: (i, 0))
    vec = pl.BlockSpec((1, D), lambda i, j: (0, 0))
    return pl.pallas_call(
        body, name=name, grid=(T // tm, ns),
        in_specs=[row, vec,
                  pl.BlockSpec((None, D, fs), lambda i, j: (j, 0, 0)),
                  pl.BlockSpec((None, D, fs), lambda i, j: (j, 0, 0)),
                  pl.BlockSpec((None, fs, D), lambda i, j: (j, 0, 0)),
                  vec],
        out_specs=[row, row,
                   pl.BlockSpec((tm, fs), lambda i, j: (i, j)),
                   pl.BlockSpec((tm, fs), lambda i, j: (i, j))],
        out_shape=[jax.ShapeDtypeStruct((T, D), F32), jax.ShapeDtypeStruct((T, D), F32),
                   jax.ShapeDtypeStruct((T, ns * fs), BF16), jax.ShapeDtypeStruct((T, ns * fs), BF16)],
        scratch_shapes=[pltpu.VMEM((tm, D), BF16), pltpu.VMEM((tm, D), F32)],
        compiler_params=_params(("parallel", "arbitrary")),
    )(x, g_pre, wg, wu, wd, g_post)


def _after(dep):
    return jnp.zeros((8, LANES), F32) if dep is None else dep


def _ffn_bwd(dxo, x, y, gate, up, g_pre, wg, wu, wd, g_post, name, dep=None):
    T, D = x.shape
    ns, _, fs = wg.shape
    tm = _blk(T, 512)
    n_i = T // tm

    def body(dxo_ref, x_ref, y_ref, g_ref, u_ref, gpre_ref, wg_ref, wu_ref, wd_ref, gpost_ref, _,
             dx_ref, hb_ref, dyb_ref, ab_ref, dgb_ref, dub_ref, dgpre_ref, dgpost_ref, dy_scr, acc_scr):
        j = pl.program_id(1)

        @pl.when(j == 0)
        def _():
            yv = y_ref[...]
            s = _rms_scale(yv)
            n = yv * s
            dn = 0.5 * dxo_ref[...]
            dgpost_ref[...] = jnp.sum(dn * n, axis=0, keepdims=True)
            dyv = _rms_bwd(dn, n, s, gpost_ref[...]).astype(BF16)
            dy_scr[...] = dyv
            dyb_ref[...] = dyv
            xv = x_ref[...]
            hb_ref[...] = (xv * _rms_scale(xv) * gpre_ref[...]).astype(BF16)
            acc_scr[...] = jnp.zeros_like(acc_scr)

        da = _mm_nt(dy_scr[...], wd_ref[...])
        gg = g_ref[...].astype(F32)
        uu = u_ref[...].astype(F32)
        sg = jax.nn.sigmoid(gg)
        silu = gg * sg
        dgate = (da * uu * (sg * (1.0 + gg * (1.0 - sg)))).astype(BF16)
        dup = (da * silu).astype(BF16)
        ab_ref[...] = (silu * uu).astype(BF16)
        dgb_ref[...] = dgate
        dub_ref[...] = dup
        acc_scr[...] += _mm_nt(dgate, wg_ref[...]) + _mm_nt(dup, wu_ref[...])

        @pl.when(j == ns - 1)
        def _():
            xv = x_ref[...]
            r = _rms_scale(xv)
            xhat = xv * r
            dh = acc_scr[...]
            dgpre_ref[...] = jnp.sum(dh * xhat, axis=0, keepdims=True)
            dx_ref[...] = _rms_bwd(dh, xhat, r, gpre_ref[...]) + dxo_ref[...]

    row = pl.BlockSpec((tm, D), lambda i, j: (i, 0))
    vec = pl.BlockSpec((1, D), lambda i, j: (0, 0))
    wide = pl.BlockSpec((tm, fs), lambda i, j: (i, j))
    part = pl.BlockSpec((None, 1, D), lambda i, j: (i, 0, 0))
    F = ns * fs
    return pl.pallas_call(
        body, name=name, grid=(n_i, ns),
        in_specs=[row, row, row, wide, wide, vec,
                  pl.BlockSpec((None, D, fs), lambda i, j: (j, 0, 0)),
                  pl.BlockSpec((None, D, fs), lambda i, j: (j, 0, 0)),
                  pl.BlockSpec((None, fs, D), lambda i, j: (j, 0, 0)),
                  vec, ANY],
        out_specs=[row, row, row, wide, wide, wide, part, part],
        out_shape=[jax.ShapeDtypeStruct((T, D), F32), jax.ShapeDtypeStruct((T, D), BF16),
                   jax.ShapeDtypeStruct((T, D), BF16), jax.ShapeDtypeStruct((T, F), BF16),
                   jax.ShapeDtypeStruct((T, F), BF16), jax.ShapeDtypeStruct((T, F), BF16),
                   jax.ShapeDtypeStruct((n_i, 1, D), F32), jax.ShapeDtypeStruct((n_i, 1, D), F32)],
        scratch_shapes=[pltpu.VMEM((tm, D), BF16), pltpu.VMEM((tm, D), F32)],
        compiler_params=_params(("parallel", "arbitrary")),
    )(dxo, x, y, gate, up, g_pre, wg, wu, wd, g_post, _after(dep))


def _wgrad(xm, ym, name, shard_cols=False, dep=None):
    T, M = xm.shape
    _, N = ym.shape
    bm = _blk(M, 1024)
    bn = N // N_DEV if shard_cols else _blk(N, 512)
    tk = _blk(T, 1024)
    n_k = T // tk

    def body(x_ref, y_ref, _, o_ref, acc_scr):
        k = pl.program_id(2)

        @pl.when(k == 0)
        def _():
            acc_scr[...] = jnp.zeros_like(acc_scr)

        acc_scr[...] += _mm_tn(x_ref[...], y_ref[...])

        @pl.when(k == n_k - 1)
        def _():
            o_ref[...] = acc_scr[...].astype(BF16)

    if shard_cols:
        out_spec = pl.BlockSpec((None, bm, bn), lambda i, j, k: (j, i, 0))
        out_shape = jax.ShapeDtypeStruct((N // bn, M, bn), BF16)
    else:
        out_spec = pl.BlockSpec((bm, bn), lambda i, j, k: (i, j))
        out_shape = jax.ShapeDtypeStruct((M, N), BF16)
    return pl.pallas_call(
        body, name=name, grid=(M // bm, N // bn, n_k),
        in_specs=[pl.BlockSpec((tk, bm), lambda i, j, k: (k, i)),
                  pl.BlockSpec((tk, bn), lambda i, j, k: (k, j)), ANY],
        out_specs=out_spec, out_shape=out_shape,
        scratch_shapes=[pltpu.VMEM((bm, bn), F32)],
        compiler_params=_params(("parallel", "parallel", "arbitrary")),
    )(xm, ym, _after(dep))


def _mix_in_fwd(x1, g, w7, wf, name):
    T, D = x1.shape
    n_seg, _, W = w7.shape
    tm = _blk(T, 1024)

    def body(x_ref, g_ref, w_ref, wf_ref, z_ref, f_ref, hb_ref, h_scr):
        s = pl.program_id(1)

        @pl.when(s == 0)
        def _():
            xv = x_ref[...]
            h = (xv * _rms_scale(xv) * g_ref[...]).astype(BF16)
            h_scr[...] = h
            hb_ref[...] = h
            f_ref[...] = _mm(h, wf_ref[...])

        z_ref[...] = _mm(h_scr[...], w_ref[...]).astype(BF16)

    return pl.pallas_call(
        body, name=name, grid=(T // tm, n_seg),
        in_specs=[pl.BlockSpec((tm, D), lambda i, s: (i, 0)),
                  pl.BlockSpec((1, D), lambda i, s: (0, 0)),
                  pl.BlockSpec((None, D, W), lambda i, s: (s, 0, 0)),
                  pl.BlockSpec((D, LANES), lambda i, s: (0, 0))],
        out_specs=[pl.BlockSpec((None, tm, W), lambda i, s: (s, i, 0)),
                   pl.BlockSpec((tm, LANES), lambda i, s: (i, 0)),
                   pl.BlockSpec((tm, D), lambda i, s: (i, 0))],
        out_shape=[jax.ShapeDtypeStruct((n_seg, T, W), BF16), jax.ShapeDtypeStruct((T, LANES), F32),
                   jax.ShapeDtypeStruct((T, D), BF16)],
        scratch_shapes=[pltpu.VMEM((tm, D), BF16)],
        compiler_params=_params(("parallel", "arbitrary")),
    )(x1, g, w7, wf)


def _mix_in_bwd(dx2, x1, g, segs, dfb, w7, wf, name):
    T, D = x1.shape
    n_seg, _, W = w7.shape
    tm = _blk(T, 512)
    n_i = T // tm

    def body(*refs):
        dx2_ref, x_ref, g_ref = refs[:3]
        seg_refs = refs[3:3 + n_seg]
        df_ref, w_ref, wf_ref, dx1_ref, dg_ref, acc_scr = refs[3 + n_seg:]
        s = pl.program_id(1)

        @pl.when(s == 0)
        def _():
            acc_scr[...] = _mm_nt(df_ref[...], wf_ref[...])

        for q in range(n_seg):
            @pl.when(s == q)
            def _(q=q):
                acc_scr[...] += _mm_nt(seg_refs[q][...], w_ref[...])

        @pl.when(s == n_seg - 1)
        def _():
            xv = x_ref[...]
            r = _rms_scale(xv)
            xhat = xv * r
            dh = acc_scr[...]
            dg_ref[...] = jnp.sum(dh * xhat, axis=0, keepdims=True)
            dx1_ref[...] = _rms_bwd(dh, xhat, r, g_ref[...]) + dx2_ref[...]

    row = pl.BlockSpec((tm, D), lambda i, s: (i, 0))
    seg_specs = []
    seg_args = []
    for arr, idx in segs:
        if idx is None:
            seg_specs.append(pl.BlockSpec((tm, W), lambda i, s: (i, 0)))
        else:
            seg_specs.append(pl.BlockSpec((None, tm, W), lambda i, s, idx=idx: (idx, i, 0)))
        seg_args.append(arr)
    return pl.pallas_call(
        body, name=name, grid=(n_i, n_seg),
        in_specs=[row, row, pl.BlockSpec((1, D), lambda i, s: (0, 0))] + seg_specs + [
            pl.BlockSpec((tm, LANES), lambda i, s: (i, 0)),
            pl.BlockSpec((None, D, W), lambda i, s: (s, 0, 0)),
            pl.BlockSpec((D, LANES), lambda i, s: (0, 0))],
        out_specs=[row, pl.BlockSpec((None, 1, D), lambda i, s: (i, 0, 0))],
        out_shape=[jax.ShapeDtypeStruct((T, D), F32), jax.ShapeDtypeStruct((n_i, 1, D), F32)],
        scratch_shapes=[pltpu.VMEM((tm, D), F32)],
        compiler_params=_params(("parallel", "arbitrary")),
    )(dx2, x1, g, *seg_args, dfb, w7, wf)


def _forget_cumsum(f, b_pad, name):
    T, L = f.shape
    tb = _blk(T, 256)

    def body(f_ref, b_ref, c_ref, carry):
        @pl.when(pl.program_id(0) == 0)
        def _():
            carry[...] = jnp.zeros_like(carry)

        lf = jax.nn.log_sigmoid(f_ref[...] + b_ref[...])
        rows = lax.broadcasted_iota(jnp.int32, (tb, tb), 0)
        cols = lax.broadcasted_iota(jnp.int32, (tb, tb), 1)
        tri = (cols <= rows).astype(F32)
        c = jnp.dot(tri, lf, preferred_element_type=F32, precision=lax.Precision.HIGHEST) + carry[...]
        c_ref[...] = c
        carry[...] = c[tb - 1:tb, :]

    return pl.pallas_call(
        body, name=name, grid=(T // tb,),
        in_specs=[pl.BlockSpec((tb, L), lambda i: (i, 0)), pl.BlockSpec((1, L), lambda i: (0, 0))],
        out_specs=pl.BlockSpec((tb, L), lambda i: (i, 0)),
        out_shape=jax.ShapeDtypeStruct((T, L), F32),
        scratch_shapes=[pltpu.VMEM((1, L), F32)],
        compiler_params=_params(("arbitrary",)),
    )(f, b_pad)


def _forget_bwd(dc, f, b_pad, name):
    T, L = f.shape
    tb = _blk(T, 256)
    nb = T // tb

    def body(dc_ref, f_ref, b_ref, df_ref, db_ref, carry):
        @pl.when(pl.program_id(0) == 0)
        def _():
            carry[...] = jnp.zeros_like(carry)
            db_ref[...] = jnp.zeros_like(db_ref)

        rows = lax.broadcasted_iota(jnp.int32, (tb, tb), 0)
        cols = lax.broadcasted_iota(jnp.int32, (tb, tb), 1)
        tri = (cols >= rows).astype(F32)
        r = jnp.dot(tri, dc_ref[...], preferred_element_type=F32, precision=lax.Precision.HIGHEST) + carry[...]
        carry[...] = r[0:1, :]
        df = r * (1.0 - jax.nn.sigmoid(f_ref[...] + b_ref[...]))
        df_ref[...] = df.astype(BF16)
        db_ref[...] += jnp.sum(df, axis=0, keepdims=True)

    rev = pl.BlockSpec((tb, L), lambda i: (nb - 1 - i, 0))
    one = pl.BlockSpec((1, L), lambda i: (0, 0))
    return pl.pallas_call(
        body, name=name, grid=(nb,),
        in_specs=[rev, rev, one], out_specs=[rev, one],
        out_shape=[jax.ShapeDtypeStruct((T, L), BF16), jax.ShapeDtypeStruct((1, L), F32)],
        scratch_shapes=[pltpu.VMEM((1, L), F32)],
        compiler_params=_params(("arbitrary",)),
    )(dc, f, b_pad)


def _attn_fwd(z7, c_row, name):
    _, T, W = z7.shape
    H = W // HEAD_DIM
    ta = _blk(T, 512)
    nq = T // ta
    scale = np.float32(1.0 / np.sqrt(HEAD_DIM))

    def body(q_ref, k_ref, v_ref, crow_ref, o_ref, lse_ref, m_scr, l_scr, acc_scr):
        i = pl.program_id(1)
        j = pl.program_id(2)

        @pl.when(j == 0)
        def _():
            m_scr[...] = jnp.full_like(m_scr, NEG_BIG)
            l_scr[...] = jnp.zeros_like(l_scr)
            acc_scr[...] = jnp.zeros_like(acc_scr)

        def step(diagonal):
            s = _mm_nt(q_ref[...], k_ref[...]) * scale - crow_ref[...]
            if diagonal:
                rows = lax.broadcasted_iota(jnp.int32, (ta, ta), 0)
                cols = lax.broadcasted_iota(jnp.int32, (ta, ta), 1)
                s = jnp.where(cols <= rows, s, NEG_BIG)
            m_prev = m_scr[...]
            m_new = jnp.maximum(m_prev, jnp.max(s, axis=-1, keepdims=True))
            alpha = jnp.exp(m_prev - m_new)
            p = jnp.exp(s - m_new)
            l_scr[...] = alpha * l_scr[...] + jnp.sum(p, axis=-1, keepdims=True)
            acc_scr[...] = alpha * acc_scr[...] + _mm(p.astype(BF16), v_ref[...])
            m_scr[...] = m_new

        @pl.when(j < i)
        def _():
            step(False)

        @pl.when(j == i)
        def _():
            step(True)
            l = l_scr[...]
            o_ref[...] = acc_scr[...] / l
            lse_ref[...] = m_scr[...] + jnp.log(l)

    return pl.pallas_call(
        body, name=name, grid=(H, nq, nq),
        in_specs=[pl.BlockSpec((None, ta, HEAD_DIM), lambda h, i, j: (0, i, h)),
                  pl.BlockSpec((None, ta, HEAD_DIM), lambda h, i, j: (1, jnp.minimum(i, j), h)),
                  pl.BlockSpec((None, ta, HEAD_DIM), lambda h, i, j: (2, jnp.minimum(i, j), h)),
                  pl.BlockSpec((None, 1, ta), lambda h, i, j: (h, 0, jnp.minimum(i, j)))],
        out_specs=[pl.BlockSpec((ta, HEAD_DIM), lambda h, i, j: (i, h)),
                   pl.BlockSpec((None, ta, 1), lambda h, i, j: (h, i, 0))],
        out_shape=[jax.ShapeDtypeStruct((T, W), F32), jax.ShapeDtypeStruct((H, T, 1), F32)],
        scratch_shapes=[pltpu.VMEM((ta, 1), F32), pltpu.VMEM((ta, 1), F32), pltpu.VMEM((ta, HEAD_DIM), F32)],
        compiler_params=_params(("parallel", "parallel", "arbitrary")),
    )(z7, z7, z7, c_row)


def _attn_bwd_kv(z7, dob, c_col, lse_row, d_row, name):
    _, T, W = z7.shape
    H = W // HEAD_DIM
    ta = _blk(T, 512)
    nq = T // ta
    scale = np.float32(1.0 / np.sqrt(HEAD_DIM))

    def body(k_ref, v_ref, q_ref, do_ref, ccol_ref, lse_ref, d_ref, dk_ref, dv_ref, dc_ref, dk_scr, dv_scr, dc_scr):
        j = pl.program_id(1)
        i = pl.program_id(2)

        @pl.when(i == 0)
        def _():
            dk_scr[...] = jnp.zeros_like(dk_scr)
            dv_scr[...] = jnp.zeros_like(dv_scr)
            dc_scr[...] = jnp.zeros_like(dc_scr)

        def step(diagonal):
            q = q_ref[...]
            do = do_ref[...]
            st = _mm_nt(k_ref[...], q) * scale - ccol_ref[...] - lse_ref[...]
            if diagonal:
                rows = lax.broadcasted_iota(jnp.int32, (ta, ta), 0)
                cols = lax.broadcasted_iota(jnp.int32, (ta, ta), 1)
                st = jnp.where(rows <= cols, st, NEG_BIG)
            pt = jnp.exp(st)
            dv_scr[...] += _mm(pt.astype(BF16), do)
            dst = pt * (_mm_nt(v_ref[...], do) - d_ref[...])
            dk_scr[...] += _mm(dst.astype(BF16), q)
            dc_scr[...] += jnp.sum(dst, axis=-1, keepdims=True)

        @pl.when(i > j)
        def _():
            step(False)

        @pl.when(i == j)
        def _():
            step(True)

        @pl.when(i == nq - 1)
        def _():
            dk_ref[...] = (dk_scr[...] * scale).astype(BF16)
            dv_ref[...] = dv_scr[...].astype(BF16)
            dc_ref[...] = -dc_scr[...]

    return pl.pallas_call(
        body, name=name, grid=(H, nq, nq),
        in_specs=[pl.BlockSpec((None, ta, HEAD_DIM), lambda h, j, i: (1, j, h)),
                  pl.BlockSpec((None, ta, HEAD_DIM), lambda h, j, i: (2, j, h)),
                  pl.BlockSpec((None, ta, HEAD_DIM), lambda h, j, i: (0, jnp.maximum(i, j), h)),
                  pl.BlockSpec((ta, HEAD_DIM), lambda h, j, i: (jnp.maximum(i, j), h)),
                  pl.BlockSpec((None, ta, 1), lambda h, j, i: (h, j, 0)),
                  pl.BlockSpec((None, 1, ta), lambda h, j, i: (h, 0, jnp.maximum(i, j))),
                  pl.BlockSpec((None, 1, ta), lambda h, j, i: (h, 0, jnp.maximum(i, j)))],
        out_specs=[pl.BlockSpec((ta, HEAD_DIM), lambda h, j, i: (j, h)),
                   pl.BlockSpec((ta, HEAD_DIM), lambda h, j, i: (j, h)),
                   pl.BlockSpec((None, ta, 1), lambda h, j, i: (h, j, 0))],
        out_shape=[jax.ShapeDtypeStruct((T, W), BF16), jax.ShapeDtypeStruct((T, W), BF16),
                   jax.ShapeDtypeStruct((H, T, 1), F32)],
        scratch_shapes=[pltpu.VMEM((ta, HEAD_DIM), F32), pltpu.VMEM((ta, HEAD_DIM), F32), pltpu.VMEM((ta, 1), F32)],
        compiler_params=_params(("parallel", "parallel", "arbitrary")),
    )(z7, z7, z7, dob, c_col, lse_row, d_row)


def _attn_bwd_q(z7, dob, c_row, lse_col, d_col, name):
    _, T, W = z7.shape
    H = W // HEAD_DIM
    ta = _blk(T, 512)
    nq = T // ta
    scale = np.float32(1.0 / np.sqrt(HEAD_DIM))

    def body(q_ref, k_ref, v_ref, do_ref, crow_ref, lse_ref, d_ref, dq_ref, dc_ref, dq_scr, dc_scr):
        i = pl.program_id(1)
        j = pl.program_id(2)

        @pl.when(j == 0)
        def _():
            dq_scr[...] = jnp.zeros_like(dq_scr)
            dc_scr[...] = jnp.zeros_like(dc_scr)

        def step(diagonal):
            k = k_ref[...]
            do = do_ref[...]
            s = _mm_nt(q_ref[...], k) * scale - crow_ref[...] - lse_ref[...]
            if diagonal:
                rows = lax.broadcasted_iota(jnp.int32, (ta, ta), 0)
                cols = lax.broadcasted_iota(jnp.int32, (ta, ta), 1)
                s = jnp.where(cols <= rows, s, NEG_BIG)
            p = jnp.exp(s)
            ds = p * (_mm_nt(do, v_ref[...]) - d_ref[...])
            dq_scr[...] += _mm(ds.astype(BF16), k)
            dc_scr[...] += jnp.sum(ds, axis=-1, keepdims=True)

        @pl.when(j < i)
        def _():
            step(False)

        @pl.when(j == i)
        def _():
            step(True)
            dq_ref[...] = (dq_scr[...] * scale).astype(BF16)
            dc_ref[...] = dc_scr[...]

    return pl.pallas_call(
        body, name=name, grid=(H, nq, nq),
        in_specs=[pl.BlockSpec((None, ta, HEAD_DIM), lambda h, i, j: (0, i, h)),
                  pl.BlockSpec((None, ta, HEAD_DIM), lambda h, i, j: (1, jnp.minimum(i, j), h)),
                  pl.BlockSpec((None, ta, HEAD_DIM), lambda h, i, j: (2, jnp.minimum(i, j), h)),
                  pl.BlockSpec((ta, HEAD_DIM), lambda h, i, j: (i, h)),
                  pl.BlockSpec((None, 1, ta), lambda h, i, j: (h, 0, jnp.minimum(i, j))),
                  pl.BlockSpec((None, ta, 1), lambda h, i, j: (h, i, 0)),
                  pl.BlockSpec((None, ta, 1), lambda h, i, j: (h, i, 0))],
        out_specs=[pl.BlockSpec((ta, HEAD_DIM), lambda h, i, j: (i, h)),
                   pl.BlockSpec((None, ta, 1), lambda h, i, j: (h, i, 0))],
        out_shape=[jax.ShapeDtypeStruct((T, W), BF16), jax.ShapeDtypeStruct((H, T, 1), F32)],
        scratch_shapes=[pltpu.VMEM((ta, HEAD_DIM), F32), pltpu.VMEM((ta, 1), F32)],
        compiler_params=_params(("parallel", "parallel", "arbitrary")),
    )(z7, z7, z7, dob, c_row, lse_col, d_col)


ATTN_TILE = 512
ATTN_CHAINS = 2


def _attn_geometry(T):
    ta = _blk(T, ATTN_TILE)
    nc = ATTN_CHAINS if (T // ta) % ATTN_CHAINS == 0 else 1
    return ta, nc, T // ta


def _causal_tile(ta, keys_on_rows=False):
    rows = lax.broadcasted_iota(jnp.int32, (ta, ta), 0)
    cols = lax.broadcasted_iota(jnp.int32, (ta, ta), 1)
    return rows <= cols if keys_on_rows else cols <= rows


def _chunk(ref, j, ta):
    return ref[pl.ds(pl.multiple_of(j * ta, ta), ta), :]


def _attn_fwd_loop(z7, c_chunks, name):
    _, T, W = z7.shape
    H = W // HEAD_DIM
    ta, nc, n_chunks = _attn_geometry(T)
    scale = np.float32(1.0 / np.sqrt(HEAD_DIM))

    def body(q_ref, k_ref, v_ref, c_ref, o_ref, lse_ref, m_scr, l_scr, acc_scr):
        g = pl.program_id(1)
        m_scr[...] = jnp.full_like(m_scr, NEG_BIG)
        l_scr[...] = jnp.zeros_like(l_scr)
        acc_scr[...] = jnp.zeros_like(acc_scr)

        def update(ch, k, v, crow, diagonal):
            q = q_ref[ch * ta:(ch + 1) * ta, :]
            s = _mm_nt(q, k) * scale - crow
            if diagonal:
                s = jnp.where(_causal_tile(ta), s, NEG_BIG)
            m_prev = m_scr[ch]
            m_new = jnp.maximum(m_prev, jnp.max(s, axis=-1, keepdims=True))
            alpha = jnp.exp(m_prev - m_new)
            p = jnp.exp(s - m_new)
            l_scr[ch] = alpha * l_scr[ch] + jnp.sum(p, axis=-1, keepdims=True)
            acc_scr[ch] = alpha * acc_scr[ch] + _mm(p.astype(BF16), v)
            m_scr[ch] = m_new

        def full_chunk(j, carry):
            k = _chunk(k_ref, j, ta)
            v = _chunk(v_ref, j, ta)
            crow = c_ref[j]
            for ch in range(nc):
                update(ch, k, v, crow, False)
            return carry

        lax.fori_loop(0, nc * g, full_chunk, 0)
        for jj in range(nc):
            j = nc * g + jj
            k = _chunk(k_ref, j, ta)
            v = _chunk(v_ref, j, ta)
            crow = c_ref[j]
            for ch in range(jj, nc):
                update(ch, k, v, crow, ch == jj)
        for ch in range(nc):
            l = l_scr[ch]
            o_ref[ch * ta:(ch + 1) * ta, :] = acc_scr[ch] / l
            lse_ref[ch * ta:(ch + 1) * ta, :] = m_scr[ch] + jnp.log(l)

    tq = nc * ta
    return pl.pallas_call(
        body, name=name, grid=(H, n_chunks // nc),
        in_specs=[pl.BlockSpec((None, tq, HEAD_DIM), lambda h, g: (0, g, h)),
                  pl.BlockSpec((None, T, HEAD_DIM), lambda h, g: (1, 0, h)),
                  pl.BlockSpec((None, T, HEAD_DIM), lambda h, g: (2, 0, h)),
                  pl.BlockSpec((None, n_chunks, 1, ta), lambda h, g: (h, 0, 0, 0))],
        out_specs=[pl.BlockSpec((tq, HEAD_DIM), lambda h, g: (g, h)),
                   pl.BlockSpec((None, tq, 1), lambda h, g: (h, g, 0))],
        out_shape=[jax.ShapeDtypeStruct((T, W), F32), jax.ShapeDtypeStruct((H, T, 1), F32)],
        scratch_shapes=[pltpu.VMEM((nc, ta, 1), F32), pltpu.VMEM((nc, ta, 1), F32),
                        pltpu.VMEM((nc, ta, HEAD_DIM), F32)],
        compiler_params=_params(("parallel", "arbitrary")),
    )(z7, z7, z7, c_chunks)


def _attn_fwd_keys_on_rows(z7, vt, c_rep, name):
    _, T, W = z7.shape
    H = W // HEAD_DIM
    ta, nc, n_chunks = _attn_geometry(T)
    scale = np.float32(1.0 / np.sqrt(HEAD_DIM))
    reps = ta // LANES

    def body(q_ref, k_ref, vt_ref, c_ref, o_ref, lse_ref, m_scr, l_scr, acc_scr):
        g = pl.program_id(1)
        m_scr[...] = jnp.full_like(m_scr, NEG_BIG)
        l_scr[...] = jnp.zeros_like(l_scr)
        acc_scr[...] = jnp.zeros_like(acc_scr)

        def update(ch, k, vt, cj, diagonal):
            q = q_ref[ch * ta:(ch + 1) * ta, :]
            st = _mm_nt(k, q) * scale - cj
            if diagonal:
                st = jnp.where(_causal_tile(ta, keys_on_rows=True), st, NEG_BIG)
            m_prev = m_scr[ch]
            m_new = jnp.maximum(m_prev, jnp.max(st, axis=0, keepdims=True))
            alpha = jnp.exp(m_prev - m_new)
            pt = jnp.exp(st - m_new)
            l_scr[ch] = alpha * l_scr[ch] + jnp.sum(pt, axis=0, keepdims=True)
            acc_scr[ch] = alpha * acc_scr[ch] + _mm(vt, pt.astype(BF16))
            m_scr[ch] = m_new

        def load(j):
            cj = _chunk(c_ref, j, ta)
            return _chunk(k_ref, j, ta), vt_ref[j], jnp.concatenate([cj] * reps, axis=1)

        def full_chunk(j, carry):
            k, vt, cj = load(j)
            for ch in range(nc):
                update(ch, k, vt, cj, False)
            return carry

        lax.fori_loop(0, nc * g, full_chunk, 0)
        for jj in range(nc):
            k, vt, cj = load(nc * g + jj)
            for ch in range(jj, nc):
                update(ch, k, vt, cj, ch == jj)
        for ch in range(nc):
            l = l_scr[ch]
            o_ref[ch * ta:(ch + 1) * ta, :] = (acc_scr[ch] / l).T
            lse_ref[ch] = m_scr[ch] + jnp.log(l)

    tq = nc * ta
    return pl.pallas_call(
        body, name=name, grid=(H, n_chunks // nc),
        in_specs=[pl.BlockSpec((None, tq, HEAD_DIM), lambda h, g: (0, g, h)),
                  pl.BlockSpec((None, T, HEAD_DIM), lambda h, g: (1, 0, h)),
                  pl.BlockSpec((None, n_chunks, HEAD_DIM, ta), lambda h, g: (h, 0, 0, 0)),
                  pl.BlockSpec((None, T, LANES), lambda h, g: (h, 0, 0))],
        out_specs=[pl.BlockSpec((tq, HEAD_DIM), lambda h, g: (g, h)),
                   pl.BlockSpec((None, nc, 1, ta), lambda h, g: (h, g, 0, 0))],
        out_shape=[jax.ShapeDtypeStruct((T, W), F32), jax.ShapeDtypeStruct((H, n_chunks, 1, ta), F32)],
        scratch_shapes=[pltpu.VMEM((nc, 1, ta), F32), pltpu.VMEM((nc, 1, ta), F32),
                        pltpu.VMEM((nc, HEAD_DIM, ta), F32)],
        compiler_params=_params(("parallel", "arbitrary")),
    )(z7, z7, vt, c_rep)


def _attn_bwd_q_loop(z7, dob, c_chunks, lse_col, d_col, name):
    _, T, W = z7.shape
    H = W // HEAD_DIM
    ta, nc, n_chunks = _attn_geometry(T)
    scale = np.float32(1.0 / np.sqrt(HEAD_DIM))

    def body(q_ref, k_ref, v_ref, do_ref, c_ref, lse_ref, d_ref, dq_ref, dc_ref, dq_scr, dc_scr):
        g = pl.program_id(1)
        dq_scr[...] = jnp.zeros_like(dq_scr)
        dc_scr[...] = jnp.zeros_like(dc_scr)

        def update(ch, k, v, crow, diagonal):
            rows = slice(ch * ta, (ch + 1) * ta)
            do = do_ref[rows, :]
            s = _mm_nt(q_ref[rows, :], k) * scale - crow - lse_ref[rows, :]
            if diagonal:
                s = jnp.where(_causal_tile(ta), s, NEG_BIG)
            p = jnp.exp(s)
            ds = p * (_mm_nt(do, v) - d_ref[rows, :])
            dq_scr[ch] += _mm(ds.astype(BF16), k)
            dc_scr[ch] += jnp.sum(ds, axis=-1, keepdims=True)

        def full_chunk(j, carry):
            k = _chunk(k_ref, j, ta)
            v = _chunk(v_ref, j, ta)
            crow = c_ref[j]
            for ch in range(nc):
                update(ch, k, v, crow, False)
            return carry

        lax.fori_loop(0, nc * g, full_chunk, 0)
        for jj in range(nc):
            j = nc * g + jj
            k = _chunk(k_ref, j, ta)
            v = _chunk(v_ref, j, ta)
            crow = c_ref[j]
            for ch in range(jj, nc):
                update(ch, k, v, crow, ch == jj)
        for ch in range(nc):
            dq_ref[ch * ta:(ch + 1) * ta, :] = (dq_scr[ch] * scale).astype(BF16)
            dc_ref[ch * ta:(ch + 1) * ta, :] = dc_scr[ch]

    tq = nc * ta
    col = pl.BlockSpec((None, tq, 1), lambda h, g: (h, g, 0))
    return pl.pallas_call(
        body, name=name, grid=(H, n_chunks // nc),
        in_specs=[pl.BlockSpec((None, tq, HEAD_DIM), lambda h, g: (0, g, h)),
                  pl.BlockSpec((None, T, HEAD_DIM), lambda h, g: (1, 0, h)),
                  pl.BlockSpec((None, T, HEAD_DIM), lambda h, g: (2, 0, h)),
                  pl.BlockSpec((tq, HEAD_DIM), lambda h, g: (g, h)),
                  pl.BlockSpec((None, n_chunks, 1, ta), lambda h, g: (h, 0, 0, 0)),
                  col, col],
        out_specs=[pl.BlockSpec((tq, HEAD_DIM), lambda h, g: (g, h)), col],
        out_shape=[jax.ShapeDtypeStruct((T, W), BF16), jax.ShapeDtypeStruct((H, T, 1), F32)],
        scratch_shapes=[pltpu.VMEM((nc, ta, HEAD_DIM), F32), pltpu.VMEM((nc, ta, 1), F32)],
        compiler_params=_params(("parallel", "arbitrary")),
    )(z7, z7, z7, dob, c_chunks, lse_col, d_col)


def _attn_bwd_kv_loop(z7, dob, c_col, lse_chunks, d_chunks, name):
    _, T, W = z7.shape
    H = W // HEAD_DIM
    ta, nc, n_chunks = _attn_geometry(T)
    scale = np.float32(1.0 / np.sqrt(HEAD_DIM))

    def body(k_ref, v_ref, q_ref, do_ref, ccol_ref, lse_ref, d_ref, dk_ref, dv_ref, dc_ref, dk_scr, dv_scr, dc_scr):
        g = pl.program_id(1)
        dk_scr[...] = jnp.zeros_like(dk_scr)
        dv_scr[...] = jnp.zeros_like(dv_scr)
        dc_scr[...] = jnp.zeros_like(dc_scr)

        def update(ch, q, do, lse_row, d_row, diagonal):
            rows = slice(ch * ta, (ch + 1) * ta)
            st = _mm_nt(k_ref[rows, :], q) * scale - ccol_ref[rows, :] - lse_row
            if diagonal:
                st = jnp.where(_causal_tile(ta, keys_on_rows=True), st, NEG_BIG)
            pt = jnp.exp(st)
            dv_scr[ch] += _mm(pt.astype(BF16), do)
            dst = pt * (_mm_nt(v_ref[rows, :], do) - d_row)
            dk_scr[ch] += _mm(dst.astype(BF16), q)
            dc_scr[ch] += jnp.sum(dst, axis=-1, keepdims=True)

        for ii in range(nc):
            i = nc * g + ii
            q = _chunk(q_ref, i, ta)
            do = _chunk(do_ref, i, ta)
            for ch in range(0, ii + 1):
                update(ch, q, do, lse_ref[i], d_ref[i], ch == ii)

        def full_chunk(i, carry):
            q = _chunk(q_ref, i, ta)
            do = _chunk(do_ref, i, ta)
            for ch in range(nc):
                update(ch, q, do, lse_ref[i], d_ref[i], False)
            return carry

        lax.fori_loop(nc * (g + 1), n_chunks, full_chunk, 0)
        for ch in range(nc):
            rows = slice(ch * ta, (ch + 1) * ta)
            dk_ref[rows, :] = (dk_scr[ch] * scale).astype(BF16)
            dv_ref[rows, :] = dv_scr[ch].astype(BF16)
            dc_ref[rows, :] = -dc_scr[ch]

    tk = nc * ta
    chunks = pl.BlockSpec((None, n_chunks, 1, ta), lambda h, g: (h, 0, 0, 0))
    col = pl.BlockSpec((None, tk, 1), lambda h, g: (h, g, 0))
    tile = pl.BlockSpec((tk, HEAD_DIM), lambda h, g: (g, h))
    return pl.pallas_call(
        body, name=name, grid=(H, n_chunks // nc),
        in_specs=[pl.BlockSpec((None, tk, HEAD_DIM), lambda h, g: (1, g, h)),
                  pl.BlockSpec((None, tk, HEAD_DIM), lambda h, g: (2, g, h)),
                  pl.BlockSpec((None, T, HEAD_DIM), lambda h, g: (0, 0, h)),
                  pl.BlockSpec((T, HEAD_DIM), lambda h, g: (0, h)),
                  col, chunks, chunks],
        out_specs=[tile, tile, col],
        out_shape=[jax.ShapeDtypeStruct((T, W), BF16), jax.ShapeDtypeStruct((T, W), BF16),
                   jax.ShapeDtypeStruct((H, T, 1), F32)],
        scratch_shapes=[pltpu.VMEM((nc, ta, HEAD_DIM), F32), pltpu.VMEM((nc, ta, HEAD_DIM), F32),
                        pltpu.VMEM((nc, ta, 1), F32)],
        compiler_params=_params(("parallel", "arbitrary")),
    )(z7, z7, z7, dob, c_col, lse_chunks, d_chunks)


def _chunk_causal_mask():
    rows = lax.broadcasted_iota(jnp.int32, (SGU_LEN, SGU_LEN), 0)
    cols = lax.broadcasted_iota(jnp.int32, (SGU_LEN, SGU_LEN), 1)
    return (cols // CHUNK) <= (rows // CHUNK)


def _sgu_norm_mix(sv, lng_ref, lnb_ref, ws_ref, bs_ref, vn_scr, mixed_scr, vhat_scr=None):
    tm = sv.shape[0]
    vs = _gelu(sv)
    mask = _chunk_causal_mask()
    rstds = []
    for g in range(N_GROUPS):
        lanes = slice(g * GROUP_DIM, (g + 1) * GROUP_DIM)
        blk = vs[:, lanes]
        cen = blk - jnp.mean(blk, axis=-1, keepdims=True)
        rstd = lax.rsqrt(jnp.mean(cen * cen, axis=-1, keepdims=True) + LN_EPS)
        vhat = cen * rstd
        rstds.append(rstd)
        if vhat_scr is not None:
            vhat_scr[:, lanes] = vhat
        vn_scr[:, lanes] = (vhat * lng_ref[:, lanes] + lnb_ref[:, lanes]).astype(BF16)
        wm = jnp.where(mask, ws_ref[g], 0.0).astype(BF16)
        for w in range(tm // SGU_LEN):
            rows = slice(w * SGU_LEN, (w + 1) * SGU_LEN)
            mixed_scr[rows, lanes] = _mm(wm, vn_scr[rows, lanes]) + bs_ref[g]
    return rstds


def _mix_out_fwd(z7, o_a, x1, lng, lnb, ws, bs, w_out, g_post, name):
    _, T, W = z7.shape
    D = x1.shape[1]
    tm = _blk(T, 256)

    def body(u_ref, sv_ref, ga_ref, gb_ref, oa_ref, x1_ref, lng_ref, lnb_ref, ws_ref, bs_ref, wo_ref, gp_ref,
             x2_ref, p_ref, mb_ref, vn_scr, mixed_scr):
        _sgu_norm_mix(sv_ref[...].astype(F32), lng_ref, lnb_ref, ws_ref, bs_ref, vn_scr, mixed_scr)
        o_b = _gelu(u_ref[...].astype(F32)) * mixed_scr[...]
        merged = (jax.nn.sigmoid(ga_ref[...].astype(F32)) * oa_ref[...]
                  + jax.nn.sigmoid(gb_ref[...].astype(F32)) * o_b).astype(BF16)
        mb_ref[...] = merged
        p = _mm(merged, wo_ref[...])
        p_ref[...] = p
        x2_ref[...] = x1_ref[...] + p * _rms_scale(p) * gp_ref[...]

    def seg(idx):
        return pl.BlockSpec((None, tm, W), lambda i, idx=idx: (idx, i, 0))

    row = pl.BlockSpec((tm, D), lambda i: (i, 0))
    vec = pl.BlockSpec((1, D), lambda i: (0, 0))
    return pl.pallas_call(
        body, name=name, grid=(T // tm,),
        in_specs=[seg(3), seg(4), seg(5), seg(6), row, row, vec, vec,
                  pl.BlockSpec((N_GROUPS, SGU_LEN, SGU_LEN), lambda i: (0, 0, 0)),
                  pl.BlockSpec((N_GROUPS, SGU_LEN, 1), lambda i: (0, 0, 0)),
                  pl.BlockSpec((D, D), lambda i: (0, 0)), vec],
        out_specs=[row, row, row],
        out_shape=[jax.ShapeDtypeStruct((T, D), F32), jax.ShapeDtypeStruct((T, D), F32),
                   jax.ShapeDtypeStruct((T, D), BF16)],
        scratch_shapes=[pltpu.VMEM((tm, W), BF16), pltpu.VMEM((tm, W), F32)],
        compiler_params=_params(("parallel",)),
    )(z7, z7, z7, z7, o_a, x1, lng, lnb, ws, bs, w_out, g_post)


def _mix_out_bwd(dx2, p, z7, o_a, lng, lnb, ws, bs, w_out, g_post, name, dep=None):
    _, T, W = z7.shape
    D = dx2.shape[1]
    tm = _blk(T, 256)
    n_w = tm // SGU_LEN

    def body(dx2_ref, p_ref, u_ref, sv_ref, ga_ref, gb_ref, oa_ref, lng_ref, lnb_ref, ws_ref, bs_ref, wo_ref, gp_ref, _,
             dpb_ref, dob_ref, dvec_ref, dz_ref, dgp_ref, dlng_ref, dlnb_ref, dws_ref, dbs_ref,
             vn_scr, mixed_scr, vhat_scr, dmix_scr, dvn_scr):
        @pl.when(pl.program_id(0) == 0)
        def _():
            dgp_ref[...] = jnp.zeros_like(dgp_ref)
            dlng_ref[...] = jnp.zeros_like(dlng_ref)
            dlnb_ref[...] = jnp.zeros_like(dlnb_ref)
            dws_ref[...] = jnp.zeros_like(dws_ref)
            dbs_ref[...] = jnp.zeros_like(dbs_ref)

        pv = p_ref[...]
        s = _rms_scale(pv)
        n = pv * s
        dn = dx2_ref[...]
        dgp_ref[...] += jnp.sum(dn * n, axis=0, keepdims=True)
        dpb = _rms_bwd(dn, n, s, gp_ref[...]).astype(BF16)
        dpb_ref[...] = dpb
        dmerged = _mm_nt(dpb, wo_ref[...])

        sv = sv_ref[...].astype(F32)
        rstds = _sgu_norm_mix(sv, lng_ref, lnb_ref, ws_ref, bs_ref, vn_scr, mixed_scr, vhat_scr)
        u_pre = u_ref[...].astype(F32)
        u = _gelu(u_pre)
        mixed = mixed_scr[...]
        sa = jax.nn.sigmoid(ga_ref[...].astype(F32))
        sb = jax.nn.sigmoid(gb_ref[...].astype(F32))
        oa = oa_ref[...]
        do_a = (dmerged * sa).astype(BF16)
        dob_ref[...] = do_a
        prod = do_a.astype(F32) * oa
        for h in range(N_HEADS):
            dvec_ref[h] = jnp.sum(prod[:, h * HEAD_DIM:(h + 1) * HEAD_DIM], axis=-1, keepdims=True)
        dz_ref[2] = (dmerged * oa * (sa * (1.0 - sa))).astype(BF16)
        dz_ref[3] = (dmerged * (u * mixed) * (sb * (1.0 - sb))).astype(BF16)
        do_b = dmerged * sb
        dz_ref[0] = (do_b * mixed * _gelu_grad(u_pre)).astype(BF16)
        dmix_scr[...] = do_b * u

        mask = _chunk_causal_mask()
        for g in range(N_GROUPS):
            lanes = slice(g * GROUP_DIM, (g + 1) * GROUP_DIM)
            wm = jnp.where(mask, ws_ref[g], 0.0).astype(BF16)
            dws = jnp.zeros((SGU_LEN, SGU_LEN), F32)
            dbs = jnp.zeros((SGU_LEN, 1), F32)
            for w in range(n_w):
                rows = slice(w * SGU_LEN, (w + 1) * SGU_LEN)
                dmix = dmix_scr[rows, lanes]
                dmix_b = dmix.astype(BF16)
                dvn_scr[rows, lanes] = _mm_tn(wm, dmix_b)
                dws = dws + _mm_nt(dmix_b, vn_scr[rows, lanes])
                dbs = dbs + jnp.sum(dmix, axis=-1, keepdims=True)
            dws_ref[g] += jnp.where(mask, dws, 0.0)
            dbs_ref[g] += dbs
            dvn = dvn_scr[:, lanes]
            vhat = vhat_scr[:, lanes]
            dlng_ref[:, lanes] += jnp.sum(dvn * vhat, axis=0, keepdims=True)
            dlnb_ref[:, lanes] += jnp.sum(dvn, axis=0, keepdims=True)
            dvh = dvn * lng_ref[:, lanes]
            dvs = rstds[g] * (dvh - jnp.mean(dvh, axis=-1, keepdims=True)
                              - vhat * jnp.mean(dvh * vhat, axis=-1, keepdims=True))
            dvn_scr[:, lanes] = dvs
        dz_ref[1] = (dvn_scr[...] * _gelu_grad(sv)).astype(BF16)

    def seg(idx):
        return pl.BlockSpec((None, tm, W), lambda i, idx=idx: (idx, i, 0))

    row = pl.BlockSpec((tm, D), lambda i: (i, 0))
    vec = pl.BlockSpec((1, D), lambda i: (0, 0))
    ws_spec = pl.BlockSpec((N_GROUPS, SGU_LEN, SGU_LEN), lambda i: (0, 0, 0))
    bs_spec = pl.BlockSpec((N_GROUPS, SGU_LEN, 1), lambda i: (0, 0, 0))
    return pl.pallas_call(
        body, name=name, grid=(T // tm,),
        in_specs=[row, row, seg(3), seg(4), seg(5), seg(6), row, vec, vec, ws_spec, bs_spec,
                  pl.BlockSpec((D, D), lambda i: (0, 0)), vec, ANY],
        out_specs=[row, row, pl.BlockSpec((N_HEADS, tm, 1), lambda i: (0, i, 0)),
                   pl.BlockSpec((4, tm, W), lambda i: (0, i, 0)), vec, vec, vec, ws_spec, bs_spec],
        out_shape=[jax.ShapeDtypeStruct((T, D), BF16), jax.ShapeDtypeStruct((T, W), BF16),
                   jax.ShapeDtypeStruct((N_HEADS, T, 1), F32), jax.ShapeDtypeStruct((4, T, W), BF16),
                   jax.ShapeDtypeStruct((1, D), F32), jax.ShapeDtypeStruct((1, D), F32),
                   jax.ShapeDtypeStruct((1, D), F32),
                   jax.ShapeDtypeStruct((N_GROUPS, SGU_LEN, SGU_LEN), F32),
                   jax.ShapeDtypeStruct((N_GROUPS, SGU_LEN, 1), F32)],
        scratch_shapes=[pltpu.VMEM((tm, W), BF16), pltpu.VMEM((tm, W), F32), pltpu.VMEM((tm, W), F32),
                        pltpu.VMEM((tm, W), F32), pltpu.VMEM((tm, W), F32)],
        compiler_params=_params(("arbitrary",)),
    )(dx2, p, z7, z7, z7, z7, o_a, lng, lnb, ws, bs, w_out, g_post, _after(dep))


def _loss_head(y, target, name):
    T, D = y.shape
    tm = _blk(T, 1024)
    n_i = T // tm

    def body(y_ref, t_ref, dy_ref, loss_ref, acc_scr):
        i = pl.program_id(0)

        @pl.when(i == 0)
        def _():
            acc_scr[...] = jnp.zeros_like(acc_scr)

        e = y_ref[...] - t_ref[...]
        dy_ref[...] = e * np.float32(1.0 / D)
        acc_scr[...] += jnp.sum(e * e, axis=0, keepdims=True)

        @pl.when(i == n_i - 1)
        def _():
            total = jnp.sum(acc_scr[...], axis=-1, keepdims=True) * np.float32(0.5 / D)
            loss_ref[...] = jnp.broadcast_to(total, loss_ref.shape)

    row = pl.BlockSpec((tm, D), lambda i: (i, 0))
    return pl.pallas_call(
        body, name=name, grid=(n_i,),
        in_specs=[row, row],
        out_specs=[row, pl.BlockSpec((1, LANES), lambda i: (0, 0))],
        out_shape=[jax.ShapeDtypeStruct((T, D), F32), jax.ShapeDtypeStruct((1, LANES), F32)],
        scratch_shapes=[pltpu.VMEM((1, D), F32)],
        compiler_params=_params(("arbitrary",)),
    )(y, target)


def _adamw_math(w, g, m, v):
    m_new = ADAM_B1 * m + (1.0 - ADAM_B1) * g
    v_new = ADAM_B2 * v + (1.0 - ADAM_B2) * (g * g)
    m_hat = m_new / np.float32(1.0 - ADAM_B1 ** ADAM_STEP)
    v_hat = v_new / np.float32(1.0 - ADAM_B2 ** ADAM_STEP)
    delta = -ADAM_LR * (m_hat / (jnp.sqrt(v_hat) + ADAM_EPS) + ADAM_WD * w)
    return delta, m_new, v_new


def _sum_adamw(parts, w, m, v, name):
    n, R, C = parts.shape
    tr = _blk(R, 128)

    def body(p_ref, w_ref, m_ref, v_ref, g_ref, d_ref, mo_ref, vo_ref):
        g = p_ref[0].astype(F32)
        for s in range(1, n):
            g = g + p_ref[s].astype(F32)
        delta, m_new, v_new = _adamw_math(w_ref[...], g, m_ref[...], v_ref[...])
        g_ref[...] = g
        d_ref[...] = delta
        mo_ref[...] = m_new
        vo_ref[...] = v_new

    row = pl.BlockSpec((tr, C), lambda i: (i, 0))
    shp = jax.ShapeDtypeStruct((R, C), F32)
    return pl.pallas_call(
        body, name=name, grid=(R // tr,),
        in_specs=[pl.BlockSpec((n, tr, C), lambda i: (0, i, 0)), row, row, row],
        out_specs=[row, row, row, row], out_shape=[shp, shp, shp, shp],
        compiler_params=_params(("parallel",)),
    )(parts, w, m, v)


def _adamw(g, w, m, v, name):
    R, C = g.shape
    tr = _blk(R, 128)

    def body(g_ref, w_ref, m_ref, v_ref, d_ref, mo_ref, vo_ref):
        delta, m_new, v_new = _adamw_math(w_ref[...], g_ref[...], m_ref[...], v_ref[...])
        d_ref[...] = delta
        mo_ref[...] = m_new
        vo_ref[...] = v_new

    row = pl.BlockSpec((tr, C), lambda i: (i, 0))
    shp = jax.ShapeDtypeStruct((R, C), F32)
    return pl.pallas_call(
        body, name=name, grid=(R // tr,),
        in_specs=[row, row, row, row], out_specs=[row, row, row], out_shape=[shp, shp, shp],
        compiler_params=_params(("parallel",)),
    )(g, w, m, v)


def _position():
    return lax.axis_index("x"), lax.axis_index("y"), lax.axis_index("c")


def _slot(px, py, pc):
    return 4 * px + 2 * py + pc


def _all_gather(shards, name):
    n = len(shards)

    def body(*refs):
        ins, outs = refs[:n], refs[n:2 * n]
        send_sems, recv_sems, local_sems = refs[2 * n:]
        x, y, c = _position()
        me, sibling = (x, y, c), (x, y, 1 - c)
        chips = [(1 - x, y), (x, 1 - y), (1 - x, 1 - y)]

        def copy(a, k, block, to, src=None):
            dst = outs[a].at[_slot(*block)]
            return pltpu.make_async_remote_copy(
                src_ref=dst if src is None else src, dst_ref=dst,
                send_sem=send_sems.at[a, k], recv_sem=recv_sems.at[a, k],
                device_id=to, device_id_type=MESH)

        mine = [pltpu.make_async_copy(ins[a], outs[a].at[_slot(*me)], local_sems.at[a]) for a in range(n)]
        for cp in mine:
            cp.start()
        first = []
        for a in range(n):
            first.append(copy(a, 0, me, sibling, src=ins[a]))
            first += [copy(a, 1 + j, me, (*chip, c), src=ins[a]) for j, chip in enumerate(chips)]
        for cp in first:
            cp.start()
        passed = []
        for j, chip in enumerate(chips):
            for a in range(n):
                copy(a, 1 + j, (*chip, c), me).wait_recv()
                fwd = copy(a, 4 + j, (*chip, c), sibling)
                fwd.start()
                passed.append(fwd)
        for a in range(n):
            copy(a, 0, sibling, me).wait_recv()
            for j, chip in enumerate(chips):
                copy(a, 4 + j, (*chip, 1 - c), me).wait_recv()
        for cp in first + passed:
            cp.wait_send()
        for cp in mine:
            cp.wait()

    return pl.pallas_call(
        body, name=name,
        in_specs=[ANY] * n, out_specs=[ANY] * n,
        out_shape=[jax.ShapeDtypeStruct((N_DEV,) + s.shape, s.dtype) for s in shards],
        scratch_shapes=[pltpu.SemaphoreType.DMA((n, 7)), pltpu.SemaphoreType.DMA((n, 7)),
                        pltpu.SemaphoreType.DMA((n,))],
    )(*shards)


def _peer(x, y, c, k):
    return (1 - x if k & 4 else x, 1 - y if k & 2 else y, 1 - c if k & 1 else c)


def _exchange(parts, name):
    n = len(parts)

    def body(*refs):
        ins, outs = refs[:n], refs[n:2 * n]
        send_sems, recv_sems, local_sems = refs[2 * n:]
        x, y, c = _position()
        me = _slot(x, y, c)
        mine = [pltpu.make_async_copy(ins[a].at[me], outs[a].at[me], local_sems.at[a]) for a in range(n)]
        for cp in mine:
            cp.start()
        sends = []
        for k in range(1, N_DEV):
            to = _peer(x, y, c, k)
            for a in range(n):
                cp = pltpu.make_async_remote_copy(
                    src_ref=ins[a].at[_slot(*to)], dst_ref=outs[a].at[me],
                    send_sem=send_sems.at[a, k - 1], recv_sem=recv_sems.at[a, k - 1],
                    device_id=to, device_id_type=MESH)
                cp.start()
                sends.append(cp)
        for k in range(1, N_DEV):
            frm = _peer(x, y, c, k)
            for a in range(n):
                pltpu.make_async_remote_copy(
                    src_ref=ins[a].at[_slot(*frm)], dst_ref=outs[a].at[_slot(*frm)],
                    send_sem=send_sems.at[a, k - 1], recv_sem=recv_sems.at[a, k - 1],
                    device_id=frm, device_id_type=MESH).wait_recv()
        for cp in sends:
            cp.wait_send()
        for cp in mine:
            cp.wait()

    return pl.pallas_call(
        body, name=name,
        in_specs=[ANY] * n, out_specs=[ANY] * n,
        out_shape=[jax.ShapeDtypeStruct(p.shape, p.dtype) for p in parts],
        scratch_shapes=[pltpu.SemaphoreType.DMA((n, 7)), pltpu.SemaphoreType.DMA((n, 7)),
                        pltpu.SemaphoreType.DMA((n,))],
    )(*parts)


HBM_SPEC = pl.BlockSpec(memory_space=pltpu.HBM)
SEM_SPEC = pl.BlockSpec(memory_space=pltpu.SEMAPHORE)
SIDE_EFFECT = pltpu.SideEffectType.DATAFLOW_SIDE_EFFECTING


def _remote_copies(src_refs, land_refs, send_sems, recv_sems, gather, outgoing):
    x, y, c = _position()
    me = _slot(x, y, c)
    copies = []
    for k in range(1, N_DEV):
        peer = _peer(x, y, c, k)
        for a in range(len(src_refs)):
            src = src_refs[a] if gather else src_refs[a].at[_slot(*peer)]
            dst = land_refs[a].at[me if outgoing else _slot(*peer)]
            sem = a * (N_DEV - 1) + k - 1
            copies.append(pltpu.make_async_remote_copy(
                src_ref=src, dst_ref=dst, send_sem=send_sems.at[sem], recv_sem=recv_sems.at[sem],
                device_id=peer, device_id_type=MESH))
    return copies


def _remote_start(srcs, after, name, gather):
    n = len(srcs)
    lands = [jax.ShapeDtypeStruct(((N_DEV,) + s.shape) if gather else s.shape, s.dtype) for s in srcs]

    def body(*refs):
        src_refs, land_refs = refs[:n], refs[n:2 * n]
        send_sems, recv_sems = refs[2 * n + 1], refs[2 * n + 2]
        token, local_sems = refs[4 * n + 3], refs[4 * n + 4]
        x, y, c = _position()
        me = _slot(x, y, c)
        mine = [pltpu.make_async_copy(src_refs[a] if gather else src_refs[a].at[me], land_refs[a].at[me],
                                      local_sems.at[a]) for a in range(n)]
        for cp in mine:
            cp.start()
        for cp in _remote_copies(src_refs, land_refs, send_sems, recv_sems, gather, outgoing=True):
            cp.start()
        for cp in mine:
            cp.wait()
        token[...] = jnp.zeros_like(token)

    sem_shape = pltpu.SemaphoreType.DMA((n * (N_DEV - 1),))
    outs = pl.pallas_call(
        body, name=name,
        out_shape=(sem_shape, sem_shape, *[pltpu.HBM(s.shape, s.dtype) for s in srcs],
                   *[pltpu.HBM(l.shape, l.dtype) for l in lands], jax.ShapeDtypeStruct((8, LANES), F32)),
        in_specs=[HBM_SPEC] * (2 * n) + [ANY],
        out_specs=(SEM_SPEC, SEM_SPEC, *([HBM_SPEC] * (2 * n)), pl.BlockSpec(memory_space=pltpu.VMEM)),
        input_output_aliases={a: 2 + a for a in range(2 * n)},
        scratch_shapes=[pltpu.SemaphoreType.DMA((n,))],
        compiler_params=pltpu.CompilerParams(has_side_effects=SIDE_EFFECT),
    )(*[pltpu.with_memory_space_constraint(s, pltpu.HBM) for s in srcs],
      *[pltpu.with_memory_space_constraint(lax.empty(l.shape, l.dtype), pltpu.HBM) for l in lands], after)
    return dict(send=outs[0], recv=outs[1], srcs=outs[2:2 + n], lands=outs[2 + n:2 + 2 * n], token=outs[-1],
                gather=gather)


def _remote_wait(flight, after, name):
    n = len(flight["srcs"])
    gather = flight["gather"]

    def body(*refs):
        src_refs, land_refs = refs[:n], refs[n:2 * n]
        send_sems, recv_sems = refs[2 * n], refs[2 * n + 1]
        for cp in _remote_copies(src_refs, land_refs, send_sems, recv_sems, gather, outgoing=False):
            cp.wait_send()
            cp.wait_recv()

    both = list(flight["srcs"]) + list(flight["lands"])
    outs = pl.pallas_call(
        body, name=name,
        out_shape=tuple(pltpu.HBM(a.shape, a.dtype) for a in both),
        in_specs=[HBM_SPEC] * (2 * n) + [SEM_SPEC, SEM_SPEC, ANY],
        out_specs=tuple([HBM_SPEC] * (2 * n)),
        input_output_aliases={a: a for a in range(2 * n)},
        compiler_params=pltpu.CompilerParams(has_side_effects=SIDE_EFFECT),
    )(*both, flight["send"], flight["recv"], after)
    return list(outs[n:])


def _all_reduce_small(blob, name):
    R, C = blob.shape

    def body(in_ref, out_ref, gath, send_sems, recv_sems):
        x, y, c = _position()
        me = _slot(x, y, c)
        gath[me] = in_ref[...]
        sends = []
        for k in range(1, N_DEV):
            to = _peer(x, y, c, k)
            cp = pltpu.make_async_remote_copy(
                src_ref=in_ref, dst_ref=gath.at[me],
                send_sem=send_sems.at[k - 1], recv_sem=recv_sems.at[k - 1],
                device_id=to, device_id_type=MESH)
            cp.start()
            sends.append(cp)
        for k in range(1, N_DEV):
            frm = _peer(x, y, c, k)
            pltpu.make_async_remote_copy(
                src_ref=in_ref, dst_ref=gath.at[_slot(*frm)],
                send_sem=send_sems.at[k - 1], recv_sem=recv_sems.at[k - 1],
                device_id=frm, device_id_type=MESH).wait_recv()
        for cp in sends:
            cp.wait_send()
        total = gath[0]
        for s in range(1, N_DEV):
            total = total + gath[s]
        out_ref[...] = total

    return pl.pallas_call(
        body, name=name,
        in_specs=[pl.BlockSpec(memory_space=pltpu.VMEM)],
        out_specs=pl.BlockSpec(memory_space=pltpu.VMEM),
        out_shape=jax.ShapeDtypeStruct((R, C), F32),
        scratch_shapes=[pltpu.VMEM((N_DEV, R, C), F32), pltpu.SemaphoreType.DMA((7,)),
                        pltpu.SemaphoreType.DMA((7,))],
        compiler_params=pltpu.CompilerParams(vmem_limit_bytes=VMEM_LIMIT),
    )(blob)


SMALL_VECS = ("ffn1_pre_g", "ffn1_post_g", "mix_pre_g", "sgu_ln_g", "sgu_ln_b", "mix_post_g", "ffn2_pre_g",
              "ffn2_post_g")
ROW_BS = len(SMALL_VECS)
ROW_BF = ROW_BS + 1
ROW_LOSS = ROW_BF + 1
ROW_WS = 16
BLOB_ROWS = ROW_WS + SGU_LEN


def _pack_small(vals, D, loss_row=None):
    rows = [vals[n].reshape(1, D) for n in SMALL_VECS]
    rows.append(vals["sgu_b_s"].reshape(1, D))
    rows.append(jnp.pad(vals["b_forget"].reshape(1, N_HEADS), ((0, 0), (0, D - N_HEADS))))
    rows.append(jnp.zeros((1, D), F32) if loss_row is None else loss_row)
    rows.append(jnp.zeros((ROW_WS - ROW_LOSS - 1, D), F32))
    rows.append(vals["sgu_w_s"].reshape(SGU_LEN, D))
    return jnp.concatenate(rows, axis=0)


def _unpack_small(blob, D):
    out = {n: blob[r:r + 1] for r, n in enumerate(SMALL_VECS)}
    out["sgu_b_s"] = blob[ROW_BS].reshape(1, N_GROUPS, SGU_LEN)
    out["b_forget"] = blob[ROW_BF, :N_HEADS].reshape(1, N_HEADS)
    out["sgu_w_s"] = blob[ROW_WS:].reshape(1, N_GROUPS, SGU_LEN, SGU_LEN)
    return out


WEIGHT_NAMES = ("ffn1_pre_g", "ffn1_w_gate", "ffn1_w_up", "ffn1_w_down", "ffn1_post_g", "mix_pre_g", "w_in",
                "b_forget", "sgu_ln_g", "sgu_ln_b", "sgu_w_s", "sgu_b_s", "w_out", "mix_post_g", "ffn2_pre_g",
                "ffn2_w_gate", "ffn2_w_up", "ffn2_w_down", "ffn2_post_g")
BIG_NAMES = ("ffn1_w_gate", "ffn1_w_up", "ffn1_w_down", "w_in", "w_out", "ffn2_w_gate", "ffn2_w_up", "ffn2_w_down")
WEIGHT_GROUPS = {"ffn1": ("ffn1_w_gate", "ffn1_w_up", "ffn1_w_down"), "mix": ("w_in", "w_out"),
                 "ffn2": ("ffn2_w_gate", "ffn2_w_up", "ffn2_w_down")}
GRAD_GROUPS = (("ffn2_w_gate", "ffn2_w_up", "ffn2_w_down"), ("w_out", "w_in"), ("ffn1_w_down", "ffn1_w_gate"),
               ("ffn1_w_up",))


def _local_step(x, target, small, fetch, emit):
    T, D = x.shape
    W = N_HEADS * HEAD_DIM
    vec = lambda n: small[n].reshape(1, D)
    big = dict(fetch("ffn1", x))

    x1, y1, gate1, up1 = _ffn_fwd(x, vec("ffn1_pre_g"), big["ffn1_w_gate"], big["ffn1_w_up"], big["ffn1_w_down"],
                                  vec("ffn1_post_g"), "ffn1_fwd")

    big.update(fetch("mix", x1))
    w_in_all = big["w_in"]
    in_width = N_DEV * w_in_all.shape[2]
    w_in = w_in_all.transpose(1, 0, 2).reshape(D, in_width)
    col_f = 3 * W
    col_u = col_f + N_HEADS
    seg_starts = (0, W, 2 * W, col_u, col_u + W, col_u + 2 * W, col_u + 3 * W)
    w7 = jnp.stack([w_in[:, s:s + W] for s in seg_starts])
    wf = jnp.pad(w_in[:, col_f:col_u], ((0, 0), (0, LANES - N_HEADS)))
    w_out = big["w_out"].reshape(D, D)
    b_pad = jnp.pad(small["b_forget"].reshape(1, N_HEADS), ((0, 0), (0, LANES - N_HEADS)))
    lng, lnb = vec("sgu_ln_g"), vec("sgu_ln_b")
    ws = small["sgu_w_s"].reshape(N_GROUPS, SGU_LEN, SGU_LEN)
    bs = small["sgu_b_s"].reshape(N_GROUPS, SGU_LEN, 1)

    z7, f_logit, h2b = _mix_in_fwd(x1, vec("mix_pre_g"), w7, wf, "mix_in_fwd")
    c = _forget_cumsum(f_logit, b_pad, "forget_cumsum")
    c_heads = c[:, :N_HEADS].T
    ta, _, n_chunks = _attn_geometry(T)
    c_chunks = c_heads.reshape(N_HEADS, n_chunks, 1, ta)
    c_col = c_heads[:, :, None]
    vt = z7[2].reshape(n_chunks, ta, N_HEADS, HEAD_DIM).transpose(2, 0, 3, 1)
    c_rep = jnp.broadcast_to(c_col, (N_HEADS, T, LANES))
    o_a, lse_chunks = _attn_fwd_keys_on_rows(z7, vt, c_rep, "attn_fwd")
    lse = lse_chunks.reshape(N_HEADS, T, 1)
    x2, p, merged_b = _mix_out_fwd(z7, o_a, x1, lng, lnb, ws, bs, w_out, vec("mix_post_g"), "mix_out_fwd")
    big.update(fetch("ffn2", x2))
    x3, y2, gate2, up2 = _ffn_fwd(x2, vec("ffn2_pre_g"), big["ffn2_w_gate"], big["ffn2_w_up"], big["ffn2_w_down"],
                                  vec("ffn2_post_g"), "ffn2_fwd")
    dy, loss_lanes = _loss_head(x3, target, "loss_head")

    grads_small = {}

    dx2, h3b, dy2b, act2, dgate2, dup2, dgpre, dgpost = _ffn_bwd(
        dy, x2, y2, gate2, up2, vec("ffn2_pre_g"), big["ffn2_w_gate"], big["ffn2_w_up"], big["ffn2_w_down"],
        vec("ffn2_post_g"), "ffn2_bwd")
    grads_small["ffn2_pre_g"] = jnp.sum(dgpre, axis=0)
    grads_small["ffn2_post_g"] = jnp.sum(dgpost, axis=0)
    emit("ffn2_w_gate", _wgrad(h3b, dgate2, "ffn2_wgrad_gate", shard_cols=True))
    emit("ffn2_w_up", _wgrad(h3b, dup2, "ffn2_wgrad_up", shard_cols=True))
    dep = emit("ffn2_w_down", _wgrad(act2, dy2b, "ffn2_wgrad_down").reshape(big["ffn2_w_down"].shape))

    dpb, dob, dvec, dz4, dgp, dlng, dlnb, dws, dbs = _mix_out_bwd(
        dx2, p, z7, o_a, lng, lnb, ws, bs, w_out, vec("mix_post_g"), "mix_out_bwd", dep=dep)
    grads_small["mix_post_g"] = dgp
    grads_small["sgu_ln_g"] = dlng
    grads_small["sgu_ln_b"] = dlnb
    grads_small["sgu_w_s"] = dws
    grads_small["sgu_b_s"] = dbs
    emit("w_out", _wgrad(merged_b, dpb, "w_out_wgrad").reshape(big["w_out"].shape))
    d_chunks = dvec.reshape(N_HEADS, n_chunks, 1, ta)
    dk, dv, dc = _attn_bwd_kv_loop(z7, dob, c_col, lse_chunks, d_chunks, "attn_bwd_kv")
    dq, dc_q = _attn_bwd_q_loop(z7, dob, c_chunks, lse, dvec, "attn_bwd_q")
    dc_pad = jnp.pad((dc + dc_q).reshape(N_HEADS, T).T, ((0, 0), (0, LANES - N_HEADS)))
    dfb, dbf = _forget_bwd(dc_pad, f_logit, b_pad, "forget_bwd")
    grads_small["b_forget"] = dbf[:, :N_HEADS]
    segs = [(dq, None), (dk, None), (dv, None), (dz4, 0), (dz4, 1), (dz4, 2), (dz4, 3)]
    dx1, dgm = _mix_in_bwd(dx2, x1, vec("mix_pre_g"), segs, dfb, w7, wf, "mix_in_bwd")
    grads_small["mix_pre_g"] = jnp.sum(dgm, axis=0)
    seg_mats = [dq, dk, dv, dz4[0], dz4[1], dz4[2], dz4[3]]
    dw_seg = [_wgrad(h2b, sm, "w_in_wgrad_%d" % q) for q, sm in enumerate(seg_mats)]
    dwf = _wgrad(h2b, dfb, "w_in_wgrad_f")[:, :N_HEADS]
    dw_in = jnp.concatenate(dw_seg[:3] + [dwf] + dw_seg[3:], axis=1)
    dep = emit("w_in", dw_in.reshape(D, N_DEV, in_width // N_DEV).transpose(1, 0, 2))

    dx0, h1b, dy1b, act1, dgate1, dup1, dgpre1, dgpost1 = _ffn_bwd(
        dx1, x, y1, gate1, up1, vec("ffn1_pre_g"), big["ffn1_w_gate"], big["ffn1_w_up"], big["ffn1_w_down"],
        vec("ffn1_post_g"), "ffn1_bwd", dep=dep)
    grads_small["ffn1_pre_g"] = jnp.sum(dgpre1, axis=0)
    grads_small["ffn1_post_g"] = jnp.sum(dgpost1, axis=0)
    emit("ffn1_w_down", _wgrad(act1, dy1b, "ffn1_wgrad_down").reshape(big["ffn1_w_down"].shape))
    dep = emit("ffn1_w_gate", _wgrad(h1b, dgate1, "ffn1_wgrad_gate", shard_cols=True))
    emit("ffn1_w_up", _wgrad(h1b, dup1, "ffn1_wgrad_up", shard_cols=True, dep=dep))

    loss_row = jnp.pad(loss_lanes, ((0, 0), (0, D - LANES)))
    return loss_row, dx0, grads_small


def kernel(x, ffn1_pre_g, ffn1_w_gate, ffn1_w_up, ffn1_w_down, ffn1_post_g, mix_pre_g, w_in, b_forget, sgu_ln_g, sgu_ln_b, sgu_w_s, sgu_b_s, w_out, mix_post_g, ffn2_pre_g, ffn2_w_gate, ffn2_w_up, ffn2_w_down, ffn2_post_g, loss_target, m_ffn1_pre_g, m_ffn1_w_gate, m_ffn1_w_up, m_ffn1_w_down, m_ffn1_post_g, m_mix_pre_g, m_w_in, m_b_forget, m_sgu_ln_g, m_sgu_ln_b, m_sgu_w_s, m_sgu_b_s, m_w_out, m_mix_post_g, m_ffn2_pre_g, m_ffn2_w_gate, m_ffn2_w_up, m_ffn2_w_down, m_ffn2_post_g, v_ffn1_pre_g, v_ffn1_w_gate, v_ffn1_w_up, v_ffn1_w_down, v_ffn1_post_g, v_mix_pre_g, v_w_in, v_b_forget, v_sgu_ln_g, v_sgu_ln_b, v_sgu_w_s, v_sgu_b_s, v_w_out, v_mix_post_g, v_ffn2_pre_g, v_ffn2_w_gate, v_ffn2_w_up, v_ffn2_w_down, v_ffn2_post_g):
    weights = dict(zip(WEIGHT_NAMES, (ffn1_pre_g, ffn1_w_gate, ffn1_w_up, ffn1_w_down, ffn1_post_g, mix_pre_g, w_in,
                                      b_forget, sgu_ln_g, sgu_ln_b, sgu_w_s, sgu_b_s, w_out, mix_post_g, ffn2_pre_g,
                                      ffn2_w_gate, ffn2_w_up, ffn2_w_down, ffn2_post_g)))
    mom1 = dict(zip(WEIGHT_NAMES, (m_ffn1_pre_g, m_ffn1_w_gate, m_ffn1_w_up, m_ffn1_w_down, m_ffn1_post_g,
                                   m_mix_pre_g, m_w_in, m_b_forget, m_sgu_ln_g, m_sgu_ln_b, m_sgu_w_s, m_sgu_b_s,
                                   m_w_out, m_mix_post_g, m_ffn2_pre_g, m_ffn2_w_gate, m_ffn2_w_up, m_ffn2_w_down,
                                   m_ffn2_post_g)))
    mom2 = dict(zip(WEIGHT_NAMES, (v_ffn1_pre_g, v_ffn1_w_gate, v_ffn1_w_up, v_ffn1_w_down, v_ffn1_post_g,
                                   v_mix_pre_g, v_w_in, v_b_forget, v_sgu_ln_g, v_sgu_ln_b, v_sgu_w_s, v_sgu_b_s,
                                   v_w_out, v_mix_post_g, v_ffn2_pre_g, v_ffn2_w_gate, v_ffn2_w_up, v_ffn2_w_down,
                                   v_ffn2_post_g)))
    D = x.shape[-1]
    small_names = [n for n in WEIGHT_NAMES if n not in BIG_NAMES]

    small = {n: weights[n] for n in small_names}
    shard = lambda n: weights[n][0].astype(BF16)

    ffn1_full = _all_gather([shard(n) for n in WEIGHT_GROUPS["ffn1"]], "ffn1_all_gather")
    gathers = {grp: _remote_start([shard(n) for n in WEIGHT_GROUPS[grp]], ffn1_full[0], grp + "_gather_start",
                                  gather=True) for grp in ("mix", "ffn2")}

    def fetch(group, after):
        if group == "ffn1":
            return zip(WEIGHT_GROUPS[group], ffn1_full)
        return zip(WEIGHT_GROUPS[group], _remote_wait(gathers[group], after, group + "_gather_wait"))

    ready, flights = {}, []

    def emit(name, part):
        ready[name] = part
        for group in GRAD_GROUPS:
            if name == group[-1]:
                flights.append((group, _remote_start([ready[n] for n in group], part, name + "_grad_start",
                                                     gather=False)))
                return flights[-1][1]["token"]
        return None

    loss_row, grad_x, grads_small = _local_step(x[0], loss_target[0], small, fetch, emit)

    blob = _all_reduce_small(_pack_small(grads_small, D, loss_row) + flights[-1][1]["token"][:1, :1],
                             "small_all_reduce")

    out = {}
    after = blob
    for group, flight in flights:
        received = _remote_wait(flight, after, group[-1] + "_grad_wait")
        for n, rcv in zip(group, received):
            g, d, m_new, v_new = _sum_adamw(rcv, weights[n][0], mom1[n][0], mom2[n][0], "adamw_" + n)
            out[n] = tuple(a[None] for a in (g, d, m_new, v_new))
            after = g

    d_blob, m_blob, v_blob = _adamw(blob, _pack_small(small, D), _pack_small({n: mom1[n] for n in small_names}, D),
                                    _pack_small({n: mom2[n] for n in small_names}, D), "adamw_small")
    unpacked = [_unpack_small(b, D) for b in (blob, d_blob, m_blob, v_blob)]
    for n in small_names:
        out[n] = tuple(u[n].reshape(weights[n].shape) for u in unpacked)

    loss = blob[ROW_LOSS, 0]
    result = [loss, grad_x[None]]
    for k in range(4):
        result += [out[n][k] for n in WEIGHT_NAMES]
    return tuple(result)
```

```python
import functools

import numpy as np
import jax
import jax.numpy as jnp
from jax import lax
from jax.experimental import pallas as pl
from jax.experimental.pallas import tpu as pltpu

F32 = jnp.float32
BF16 = jnp.bfloat16

RMS_EPS = 1e-6
LN_EPS = 1e-5
HEAD_DIM = 128
N_HEADS = 8
GROUP_DIM = 128
N_GROUPS = 8
SGU_LEN = 128
CHUNK = 64
N_DEV = 8
LANES = 128
VMEM_LIMIT = 56 * 1024 * 1024
NEG_BIG = -1e30

ADAM_LR = 0.001
ADAM_B1 = 0.9
ADAM_B2 = 0.999
ADAM_EPS = 1e-08
ADAM_WD = 0.01
ADAM_STEP = 10

MESH = pl.DeviceIdType.MESH
ANY = pl.BlockSpec(memory_space=pl.ANY)


def _blk(n, pref):
    return pref if (n >= pref and n % pref == 0) else n


def _mm(a, b):
    return jnp.dot(a, b, preferred_element_type=F32)


def _mm_nt(a, b):
    return lax.dot_general(a, b, (((1,), (1,)), ((), ())), preferred_element_type=F32)


def _mm_tn(a, b):
    return lax.dot_general(a, b, (((0,), (0,)), ((), ())), preferred_element_type=F32)


def _params(sem):
    return pltpu.CompilerParams(dimension_semantics=sem, vmem_limit_bytes=VMEM_LIMIT)


def _gelu(x):
    return 0.5 * x * (1.0 + lax.erf(x * np.float32(1.0 / np.sqrt(2.0))))


def _gelu_grad(x):
    cdf = 0.5 * (1.0 + lax.erf(x * np.float32(1.0 / np.sqrt(2.0))))
    return cdf + x * jnp.exp(-0.5 * x * x) * np.float32(1.0 / np.sqrt(2.0 * np.pi))


def _rms_scale(v):
    return lax.rsqrt(jnp.mean(v * v, axis=-1, keepdims=True) + RMS_EPS)


def _rms_bwd(dy, xhat, r, g):
    dxh = dy * g
    return r * (dxh - xhat * jnp.mean(dxh * xhat, axis=-1, keepdims=True))


def _ffn_fwd(x, g_pre, wg, wu, wd, g_post, name):
    T, D = x.shape
    ns, _, fs = wg.shape
    tm = _blk(T, 512)

    def body(x_ref, gpre_ref, wg_ref, wu_ref, wd_ref, gpost_ref, xo_ref, y_ref, g_ref, u_ref, h_scr, acc_scr):
        j = pl.program_id(1)

        @pl.when(j == 0)
        def _():
            xv = x_ref[...]
            h_scr[...] = (xv * _rms_scale(xv) * gpre_ref[...]).astype(BF16)
            acc_scr[...] = jnp.zeros_like(acc_scr)

        h = h_scr[...]
        gg = _mm(h, wg_ref[...])
        uu = _mm(h, wu_ref[...])
        a = gg * jax.nn.sigmoid(gg) * uu
        g_ref[...] = gg.astype(BF16)
        u_ref[...] = uu.astype(BF16)
        acc_scr[...] += _mm(a.astype(BF16), wd_ref[...])

        @pl.when(j == ns - 1)
        def _():
            y = acc_scr[...]
            y_ref[...] = y
            xo_ref[...] = x_ref[...] + 0.5 * (y * _rms_scale(y) * gpost_ref[...])

    row = pl.BlockSpec((tm, D), lambda i, j: (i, 0))
    vec = pl.BlockSpec((1, D), lambda i, j: (0, 0))
    return pl.pallas_call(
        body, name=name, grid=(T // tm, ns),
        in_specs=[row, vec,
                  pl.BlockSpec((None, D, fs), lambda i, j: (j, 0, 0)),
                  pl.BlockSpec((None, D, fs), lambda i, j: (j, 0, 0)),
                  pl.BlockSpec((None, fs, D), lambda i, j: (j, 0, 0)),
                  vec],
        out_specs=[row, row,
                   pl.BlockSpec((tm, fs), lambda i, j: (i, j)),
                   pl.BlockSpec((tm, fs), lambda i, j: (i, j))],
        out_shape=[jax.ShapeDtypeStruct((T, D), F32), jax.ShapeDtypeStruct((T, D), F32),
                   jax.ShapeDtypeStruct((T, ns * fs), BF16), jax.ShapeDtypeStruct((T, ns * fs), BF16)],
        scratch_shapes=[pltpu.VMEM((tm, D), BF16), pltpu.VMEM((tm, D), F32)],
        compiler_params=_params(("parallel", "arbitrary")),
    )(x, g_pre, wg, wu, wd, g_post)


def _after(dep):
    return jnp.zeros((8, LANES), F32) if dep is None else dep


def _ffn_bwd(dxo, x, y, gate, up, g_pre, wg, wu, wd, g_post, name, dep=None):
    T, D = x.shape
    ns, _, fs = wg.shape
    tm = _blk(T, 512)
    n_i = T // tm

    def body(dxo_ref, x_ref, y_ref, g_ref, u_ref, gpre_ref, wg_ref, wu_ref, wd_ref, gpost_ref, _,
             dx_ref, hb_ref, dyb_ref, ab_ref, dgb_ref, dub_ref, dgpre_ref, dgpost_ref, dy_scr, acc_scr):
        j = pl.program_id(1)

        @pl.when(j == 0)
        def _():
            yv = y_ref[...]
            s = _rms_scale(yv)
            n = yv * s
            dn = 0.5 * dxo_ref[...]
            dgpost_ref[...] = jnp.sum(dn * n, axis=0, keepdims=True)
            dyv = _rms_bwd(dn, n, s, gpost_ref[...]).astype(BF16)
            dy_scr[...] = dyv
            dyb_ref[...] = dyv
            xv = x_ref[...]
            hb_ref[...] = (xv * _rms_scale(xv) * gpre_ref[...]).astype(BF16)
            acc_scr[...] = jnp.zeros_like(acc_scr)

        da = _mm_nt(dy_scr[...], wd_ref[...])
        gg = g_ref[...].astype(F32)
        uu = u_ref[...].astype(F32)
        sg = jax.nn.sigmoid(gg)
        silu = gg * sg
        dgate = (da * uu * (sg * (1.0 + gg * (1.0 - sg)))).astype(BF16)
        dup = (da * silu).astype(BF16)
        ab_ref[...] = (silu * uu).astype(BF16)
        dgb_ref[...] = dgate
        dub_ref[...] = dup
        acc_scr[...] += _mm_nt(dgate, wg_ref[...]) + _mm_nt(dup, wu_ref[...])

        @pl.when(j == ns - 1)
        def _():
            xv = x_ref[...]
            r = _rms_scale(xv)
            xhat = xv * r
            dh = acc_scr[...]
            dgpre_ref[...] = jnp.sum(dh * xhat, axis=0, keepdims=True)
            dx_ref[...] = _rms_bwd(dh, xhat, r, gpre_ref[...]) + dxo_ref[...]

    row = pl.BlockSpec((tm, D), lambda i, j: (i, 0))
    vec = pl.BlockSpec((1, D), lambda i, j: (0, 0))
    wide = pl.BlockSpec((tm, fs), lambda i, j: (i, j))
    part = pl.BlockSpec((None, 1, D), lambda i, j: (i, 0, 0))
    F = ns * fs
    return pl.pallas_call(
        body, name=name, grid=(n_i, ns),
        in_specs=[row, row, row, wide, wide, vec,
                  pl.BlockSpec((None, D, fs), lambda i, j: (j, 0, 0)),
                  pl.BlockSpec((None, D, fs), lambda i, j: (j, 0, 0)),
                  pl.BlockSpec((None, fs, D), lambda i, j: (j, 0, 0)),
                  vec, ANY],
        out_specs=[row, row, row, wide, wide, wide, part, part],
        out_shape=[jax.ShapeDtypeStruct((T, D), F32), jax.ShapeDtypeStruct((T, D), BF16),
                   jax.ShapeDtypeStruct((T, D), BF16), jax.ShapeDtypeStruct((T, F), BF16),
                   jax.ShapeDtypeStruct((T, F), BF16), jax.ShapeDtypeStruct((T, F), BF16),
                   jax.ShapeDtypeStruct((n_i, 1, D), F32), jax.ShapeDtypeStruct((n_i, 1, D), F32)],
        scratch_shapes=[pltpu.VMEM((tm, D), BF16), pltpu.VMEM((tm, D), F32)],
        compiler_params=_params(("parallel", "arbitrary")),
    )(dxo, x, y, gate, up, g_pre, wg, wu, wd, g_post, _after(dep))


def _wgrad(xm, ym, name, shard_cols=False, dep=None):
    T, M = xm.shape
    _, N = ym.shape
    bm = _blk(M, 1024)
    bn = N // N_DEV if shard_cols else _blk(N, 512)
    tk = _blk(T, 1024)
    n_k = T // tk

    def body(x_ref, y_ref, _, o_ref, acc_scr):
        k = pl.program_id(2)

        @pl.when(k == 0)
        def _():
            acc_scr[...] = jnp.zeros_like(acc_scr)

        acc_scr[...] += _mm_tn(x_ref[...], y_ref[...])

        @pl.when(k == n_k - 1)
        def _():
            o_ref[...] = acc_scr[...].astype(BF16)

    if shard_cols:
        out_spec = pl.BlockSpec((None, bm, bn), lambda i, j, k: (j, i, 0))
        out_shape = jax.ShapeDtypeStruct((N // bn, M, bn), BF16)
    else:
        out_spec = pl.BlockSpec((bm, bn), lambda i, j, k: (i, j))
        out_shape = jax.ShapeDtypeStruct((M, N), BF16)
    return pl.pallas_call(
        body, name=name, grid=(M // bm, N // bn, n_k),
        in_specs=[pl.BlockSpec((tk, bm), lambda i, j, k: (k, i)),
                  pl.BlockSpec((tk, bn), lambda i, j, k: (k, j)), ANY],
        out_specs=out_spec, out_shape=out_shape,
        scratch_shapes=[pltpu.VMEM((bm, bn), F32)],
        compiler_params=_params(("parallel", "parallel", "arbitrary")),
    )(xm, ym, _after(dep))


def _mix_in_fwd(x1, g, w7, wf, name):
    T, D = x1.shape
    n_seg, _, W = w7.shape
    tm = _blk(T, 1024)

    def body(x_ref, g_ref, w_ref, wf_ref, z_ref, f_ref, hb_ref, h_scr):
        s = pl.program_id(1)

        @pl.when(s == 0)
        def _():
            xv = x_ref[...]
            h = (xv * _rms_scale(xv) * g_ref[...]).astype(BF16)
            h_scr[...] = h
            hb_ref[...] = h
            f_ref[...] = _mm(h, wf_ref[...])

        z_ref[...] = _mm(h_scr[...], w_ref[...]).astype(BF16)

    return pl.pallas_call(
        body, name=name, grid=(T // tm, n_seg),
        in_specs=[pl.BlockSpec((tm, D), lambda i, s: (i, 0)),
                  pl.BlockSpec((1, D), lambda i, s: (0, 0)),
                  pl.BlockSpec((None, D, W), lambda i, s: (s, 0, 0)),
                  pl.BlockSpec((D, LANES), lambda i, s: (0, 0))],
        out_specs=[pl.BlockSpec((None, tm, W), lambda i, s: (s, i, 0)),
                   pl.BlockSpec((tm, LANES), lambda i, s: (i, 0)),
                   pl.BlockSpec((tm, D), lambda i, s: (i, 0))],
        out_shape=[jax.ShapeDtypeStruct((n_seg, T, W), BF16), jax.ShapeDtypeStruct((T, LANES), F32),
                   jax.ShapeDtypeStruct((T, D), BF16)],
        scratch_shapes=[pltpu.VMEM((tm, D), BF16)],
        compiler_params=_params(("parallel", "arbitrary")),
    )(x1, g, w7, wf)


def _mix_in_bwd(dx2, x1, g, segs, dfb, w7, wf, name):
    T, D = x1.shape
    n_seg, _, W = w7.shape
    tm = _blk(T, 512)
    n_i = T // tm

    def body(*refs):
        dx2_ref, x_ref, g_ref = refs[:3]
        seg_refs = refs[3:3 + n_seg]
        df_ref, w_ref, wf_ref, dx1_ref, dg_ref, acc_scr = refs[3 + n_seg:]
        s = pl.program_id(1)

        @pl.when(s == 0)
        def _():
            acc_scr[...] = _mm_nt(df_ref[...], wf_ref[...])

        for q in range(n_seg):
            @pl.when(s == q)
            def _(q=q):
                acc_scr[...] += _mm_nt(seg_refs[q][...], w_ref[...])

        @pl.when(s == n_seg - 1)
        def _():
            xv = x_ref[...]
            r = _rms_scale(xv)
            xhat = xv * r
            dh = acc_scr[...]
            dg_ref[...] = jnp.sum(dh * xhat, axis=0, keepdims=True)
            dx1_ref[...] = _rms_bwd(dh, xhat, r, g_ref[...]) + dx2_ref[...]

    row = pl.BlockSpec((tm, D), lambda i, s: (i, 0))
    seg_specs = []
    seg_args = []
    for arr, idx in segs:
        if idx is None:
            seg_specs.append(pl.BlockSpec((tm, W), lambda i, s: (i, 0)))
        else:
            seg_specs.append(pl.BlockSpec((None, tm, W), lambda i, s, idx=idx: (idx, i, 0)))
        seg_args.append(arr)
    return pl.pallas_call(
        body, name=name, grid=(n_i, n_seg),
        in_specs=[row, row, pl.BlockSpec((1, D), lambda i, s: (0, 0))] + seg_specs + [
            pl.BlockSpec((tm, LANES), lambda i, s: (i, 0)),
            pl.BlockSpec((None, D, W), lambda i, s: (s, 0, 0)),
            pl.BlockSpec((D, LANES), lambda i, s: (0, 0))],
        out_specs=[row, pl.BlockSpec((None, 1, D), lambda i, s: (i, 0, 0))],
        out_shape=[jax.ShapeDtypeStruct((T, D), F32), jax.ShapeDtypeStruct((n_i, 1, D), F32)],
        scratch_shapes=[pltpu.VMEM((tm, D), F32)],
        compiler_params=_params(("parallel", "arbitrary")),
    )(dx2, x1, g, *seg_args, dfb, w7, wf)


def _forget_cumsum(f, b_pad, name):
    T, L = f.shape
    tb = _blk(T, 256)

    def body(f_ref, b_ref, c_ref, carry):
        @pl.when(pl.program_id(0) == 0)
        def _():
            carry[...] = jnp.zeros_like(carry)

        lf = jax.nn.log_sigmoid(f_ref[...] + b_ref[...])
        rows = lax.broadcasted_iota(jnp.int32, (tb, tb), 0)
        cols = lax.broadcasted_iota(jnp.int32, (tb, tb), 1)
        tri = (cols <= rows).astype(F32)
        c = jnp.dot(tri, lf, preferred_element_type=F32, precision=lax.Precision.HIGHEST) + carry[...]
        c_ref[...] = c
        carry[...] = c[tb - 1:tb, :]

    return pl.pallas_call(
        body, name=name, grid=(T // tb,),
        in_specs=[pl.BlockSpec((tb, L), lambda i: (i, 0)), pl.BlockSpec((1, L), lambda i: (0, 0))],
        out_specs=pl.BlockSpec((tb, L), lambda i: (i, 0)),
        out_shape=jax.ShapeDtypeStruct((T, L), F32),
        scratch_shapes=[pltpu.VMEM((1, L), F32)],
        compiler_params=_params(("arbitrary",)),
    )(f, b_pad)


def _forget_bwd(dc, f, b_pad, name):
    T, L = f.shape
    tb = _blk(T, 256)
    nb = T // tb

    def body(dc_ref, f_ref, b_ref, df_ref, db_ref, carry):
        @pl.when(pl.program_id(0) == 0)
        def _():
            carry[...] = jnp.zeros_like(carry)
            db_ref[...] = jnp.zeros_like(db_ref)

        rows = lax.broadcasted_iota(jnp.int32, (tb, tb), 0)
        cols = lax.broadcasted_iota(jnp.int32, (tb, tb), 1)
        tri = (cols >= rows).astype(F32)
        r = jnp.dot(tri, dc_ref[...], preferred_element_type=F32, precision=lax.Precision.HIGHEST) + carry[...]
        carry[...] = r[0:1, :]
        df = r * (1.0 - jax.nn.sigmoid(f_ref[...] + b_ref[...]))
        df_ref[...] = df.astype(BF16)
        db_ref[...] += jnp.sum(df, axis=0, keepdims=True)

    rev = pl.BlockSpec((tb, L), lambda i: (nb - 1 - i, 0))
    one = pl.BlockSpec((1, L), lambda i: (0, 0))
    return pl.pallas_call(
        body, name=name, grid=(nb,),
        in_specs=[rev, rev, one], out_specs=[rev, one],
        out_shape=[jax.ShapeDtypeStruct((T, L), BF16), jax.ShapeDtypeStruct((1, L), F32)],
        scratch_shapes=[pltpu.VMEM((1, L), F32)],
        compiler_params=_params(("arbitrary",)),
    )(dc, f, b_pad)


def _attn_fwd(z7, c_row, name):
    _, T, W = z7.shape
    H = W // HEAD_DIM
    ta = _blk(T, 512)
    nq = T // ta
    scale = np.float32(1.0 / np.sqrt(HEAD_DIM))

    def body(q_ref, k_ref, v_ref, crow_ref, o_ref, lse_ref, m_scr, l_scr, acc_scr):
        i = pl.program_id(1)
        j = pl.program_id(2)

        @pl.when(j == 0)
        def _():
            m_scr[...] = jnp.full_like(m_scr, NEG_BIG)
            l_scr[...] = jnp.zeros_like(l_scr)
            acc_scr[...] = jnp.zeros_like(acc_scr)

        def step(diagonal):
            s = _mm_nt(q_ref[...], k_ref[...]) * scale - crow_ref[...]
            if diagonal:
                rows = lax.broadcasted_iota(jnp.int32, (ta, ta), 0)
                cols = lax.broadcasted_iota(jnp.int32, (ta, ta), 1)
                s = jnp.where(cols <= rows, s, NEG_BIG)
            m_prev = m_scr[...]
            m_new = jnp.maximum(m_prev, jnp.max(s, axis=-1, keepdims=True))
            alpha = jnp.exp(m_prev - m_new)
            p = jnp.exp(s - m_new)
            l_scr[...] = alpha * l_scr[...] + jnp.sum(p, axis=-1, keepdims=True)
            acc_scr[...] = alpha * acc_scr[...] + _mm(p.astype(BF16), v_ref[...])
            m_scr[...] = m_new

        @pl.when(j < i)
        def _():
            step(False)

        @pl.when(j == i)
        def _():
            step(True)
            l = l_scr[...]
            o_ref[...] = acc_scr[...] / l
            lse_ref[...] = m_scr[...] + jnp.log(l)

    return pl.pallas_call(
        body, name=name, grid=(H, nq, nq),
        in_specs=[pl.BlockSpec((None, ta, HEAD_DIM), lambda h, i, j: (0, i, h)),
                  pl.BlockSpec((None, ta, HEAD_DIM), lambda h, i, j: (1, jnp.minimum(i, j), h)),
                  pl.BlockSpec((None, ta, HEAD_DIM), lambda h, i, j: (2, jnp.minimum(i, j), h)),
                  pl.BlockSpec((None, 1, ta), lambda h, i, j: (h, 0, jnp.minimum(i, j)))],
        out_specs=[pl.BlockSpec((ta, HEAD_DIM), lambda h, i, j: (i, h)),
                   pl.BlockSpec((None, ta, 1), lambda h, i, j: (h, i, 0))],
        out_shape=[jax.ShapeDtypeStruct((T, W), F32), jax.ShapeDtypeStruct((H, T, 1), F32)],
        scratch_shapes=[pltpu.VMEM((ta, 1), F32), pltpu.VMEM((ta, 1), F32), pltpu.VMEM((ta, HEAD_DIM), F32)],
        compiler_params=_params(("parallel", "parallel", "arbitrary")),
    )(z7, z7, z7, c_row)


def _attn_bwd_kv(z7, dob, c_col, lse_row, d_row, name):
    _, T, W = z7.shape
    H = W // HEAD_DIM
    ta = _blk(T, 512)
    nq = T // ta
    scale = np.float32(1.0 / np.sqrt(HEAD_DIM))

    def body(k_ref, v_ref, q_ref, do_ref, ccol_ref, lse_ref, d_ref, dk_ref, dv_ref, dc_ref, dk_scr, dv_scr, dc_scr):
        j = pl.program_id(1)
        i = pl.program_id(2)

        @pl.when(i == 0)
        def _():
            dk_scr[...] = jnp.zeros_like(dk_scr)
            dv_scr[...] = jnp.zeros_like(dv_scr)
            dc_scr[...] = jnp.zeros_like(dc_scr)

        def step(diagonal):
            q = q_ref[...]
            do = do_ref[...]
            st = _mm_nt(k_ref[...], q) * scale - ccol_ref[...] - lse_ref[...]
            if diagonal:
                rows = lax.broadcasted_iota(jnp.int32, (ta, ta), 0)
                cols = lax.broadcasted_iota(jnp.int32, (ta, ta), 1)
                st = jnp.where(rows <= cols, st, NEG_BIG)
            pt = jnp.exp(st)
            dv_scr[...] += _mm(pt.astype(BF16), do)
            dst = pt * (_mm_nt(v_ref[...], do) - d_ref[...])
            dk_scr[...] += _mm(dst.astype(BF16), q)
            dc_scr[...] += jnp.sum(dst, axis=-1, keepdims=True)

        @pl.when(i > j)
        def _():
            step(False)

        @pl.when(i == j)
        def _():
            step(True)

        @pl.when(i == nq - 1)
        def _():
            dk_ref[...] = (dk_scr[...] * scale).astype(BF16)
            dv_ref[...] = dv_scr[...].astype(BF16)
            dc_ref[...] = -dc_scr[...]

    return pl.pallas_call(
        body, name=name, grid=(H, nq, nq),
        in_specs=[pl.BlockSpec((None, ta, HEAD_DIM), lambda h, j, i: (1, j, h)),
                  pl.BlockSpec((None, ta, HEAD_DIM), lambda h, j, i: (2, j, h)),
                  pl.BlockSpec((None, ta, HEAD_DIM), lambda h, j, i: (0, jnp.maximum(i, j), h)),
                  pl.BlockSpec((ta, HEAD_DIM), lambda h, j, i: (jnp.maximum(i, j), h)),
                  pl.BlockSpec((None, ta, 1), lambda h, j, i: (h, j, 0)),
                  pl.BlockSpec((None, 1, ta), lambda h, j, i: (h, 0, jnp.maximum(i, j))),
                  pl.BlockSpec((None, 1, ta), lambda h, j, i: (h, 0, jnp.maximum(i, j)))],
        out_specs=[pl.BlockSpec((ta, HEAD_DIM), lambda h, j, i: (j, h)),
                   pl.BlockSpec((ta, HEAD_DIM), lambda h, j, i: (j, h)),
                   pl.BlockSpec((None, ta, 1), lambda h, j, i: (h, j, 0))],
        out_shape=[jax.ShapeDtypeStruct((T, W), BF16), jax.ShapeDtypeStruct((T, W), BF16),
                   jax.ShapeDtypeStruct((H, T, 1), F32)],
        scratch_shapes=[pltpu.VMEM((ta, HEAD_DIM), F32), pltpu.VMEM((ta, HEAD_DIM), F32), pltpu.VMEM((ta, 1), F32)],
        compiler_params=_params(("parallel", "parallel", "arbitrary")),
    )(z7, z7, z7, dob, c_col, lse_row, d_row)


def _attn_bwd_q(z7, dob, c_row, lse_col, d_col, name):
    _, T, W = z7.shape
    H = W // HEAD_DIM
    ta = _blk(T, 512)
    nq = T // ta
    scale = np.float32(1.0 / np.sqrt(HEAD_DIM))

    def body(q_ref, k_ref, v_ref, do_ref, crow_ref, lse_ref, d_ref, dq_ref, dc_ref, dq_scr, dc_scr):
        i = pl.program_id(1)
        j = pl.program_id(2)

        @pl.when(j == 0)
        def _():
            dq_scr[...] = jnp.zeros_like(dq_scr)
            dc_scr[...] = jnp.zeros_like(dc_scr)

        def step(diagonal):
            k = k_ref[...]
            do = do_ref[...]
            s = _mm_nt(q_ref[...], k) * scale - crow_ref[...] - lse_ref[...]
            if diagonal:
                rows = lax.broadcasted_iota(jnp.int32, (ta, ta), 0)
                cols = lax.broadcasted_iota(jnp.int32, (ta, ta), 1)
                s = jnp.where(cols <= rows, s, NEG_BIG)
            p = jnp.exp(s)
            ds = p * (_mm_nt(do, v_ref[...]) - d_ref[...])
            dq_scr[...] += _mm(ds.astype(BF16), k)
            dc_scr[...] += jnp.sum(ds, axis=-1, keepdims=True)

        @pl.when(j < i)
        def _():
            step(False)

        @pl.when(j == i)
        def _():
            step(True)
            dq_ref[...] = (dq_scr[...] * scale).astype(BF16)
            dc_ref[...] = dc_scr[...]

    return pl.pallas_call(
        body, name=name, grid=(H, nq, nq),
        in_specs=[pl.BlockSpec((None, ta, HEAD_DIM), lambda h, i, j: (0, i, h)),
                  pl.BlockSpec((None, ta, HEAD_DIM), lambda h, i, j: (1, jnp.minimum(i, j), h)),
                  pl.BlockSpec((None, ta, HEAD_DIM), lambda h, i, j: (2, jnp.minimum(i, j), h)),
                  pl.BlockSpec((ta, HEAD_DIM), lambda h, i, j: (i, h)),
                  pl.BlockSpec((None, 1, ta), lambda h, i, j: (h, 0, jnp.minimum(i, j))),
                  pl.BlockSpec((None, ta, 1), lambda h, i, j: (h, i, 0)),
                  pl.BlockSpec((None, ta, 1), lambda h, i, j: (h, i, 0))],
        out_specs=[pl.BlockSpec((ta, HEAD_DIM), lambda h, i, j: (i, h)),
                   pl.BlockSpec((None, ta, 1), lambda h, i, j: (h, i, 0))],
        out_shape=[jax.ShapeDtypeStruct((T, W), BF16), jax.ShapeDtypeStruct((H, T, 1), F32)],
        scratch_shapes=[pltpu.VMEM((ta, HEAD_DIM), F32), pltpu.VMEM((ta, 1), F32)],
        compiler_params=_params(("parallel", "parallel", "arbitrary")),
    )(z7, z7, z7, dob, c_row, lse_col, d_col)


ATTN_TILE = 512
ATTN_CHAINS = 2


def _attn_geometry(T):
    ta = _blk(T, ATTN_TILE)
    nc = ATTN_CHAINS if (T // ta) % ATTN_CHAINS == 0 else 1
    return ta, nc, T // ta


def _causal_tile(ta, keys_on_rows=False):
    rows = lax.broadcasted_iota(jnp.int32, (ta, ta), 0)
    cols = lax.broadcasted_iota(jnp.int32, (ta, ta), 1)
    return rows <= cols if keys_on_rows else cols <= rows


def _chunk(ref, j, ta):
    return ref[pl.ds(pl.multiple_of(j * ta, ta), ta), :]


def _attn_fwd_loop(z7, c_chunks, name):
    _, T, W = z7.shape
    H = W // HEAD_DIM
    ta, nc, n_chunks = _attn_geometry(T)
    scale = np.float32(1.0 / np.sqrt(HEAD_DIM))

    def body(q_ref, k_ref, v_ref, c_ref, o_ref, lse_ref, m_scr, l_scr, acc_scr):
        g = pl.program_id(1)
        m_scr[...] = jnp.full_like(m_scr, NEG_BIG)
        l_scr[...] = jnp.zeros_like(l_scr)
        acc_scr[...] = jnp.zeros_like(acc_scr)

        def update(ch, k, v, crow, diagonal):
            q = q_ref[ch * ta:(ch + 1) * ta, :]
            s = _mm_nt(q, k) * scale - crow
            if diagonal:
                s = jnp.where(_causal_tile(ta), s, NEG_BIG)
            m_prev = m_scr[ch]
            m_new = jnp.maximum(m_prev, jnp.max(s, axis=-1, keepdims=True))
            alpha = jnp.exp(m_prev - m_new)
            p = jnp.exp(s - m_new)
            l_scr[ch] = alpha * l_scr[ch] + jnp.sum(p, axis=-1, keepdims=True)
            acc_scr[ch] = alpha * acc_scr[ch] + _mm(p.astype(BF16), v)
            m_scr[ch] = m_new

        def full_chunk(j, carry):
            k = _chunk(k_ref, j, ta)
            v = _chunk(v_ref, j, ta)
            crow = c_ref[j]
            for ch in range(nc):
                update(ch, k, v, crow, False)
            return carry

        lax.fori_loop(0, nc * g, full_chunk, 0)
        for jj in range(nc):
            j = nc * g + jj
            k = _chunk(k_ref, j, ta)
            v = _chunk(v_ref, j, ta)
            crow = c_ref[j]
            for ch in range(jj, nc):
                update(ch, k, v, crow, ch == jj)
        for ch in range(nc):
            l = l_scr[ch]
            o_ref[ch * ta:(ch + 1) * ta, :] = acc_scr[ch] / l
            lse_ref[ch * ta:(ch + 1) * ta, :] = m_scr[ch] + jnp.log(l)

    tq = nc * ta
    return pl.pallas_call(
        body, name=name, grid=(H, n_chunks // nc),
        in_specs=[pl.BlockSpec((None, tq, HEAD_DIM), lambda h, g: (0, g, h)),
                  pl.BlockSpec((None, T, HEAD_DIM), lambda h, g: (1, 0, h)),
                  pl.BlockSpec((None, T, HEAD_DIM), lambda h, g: (2, 0, h)),
                  pl.BlockSpec((None, n_chunks, 1, ta), lambda h, g: (h, 0, 0, 0))],
        out_specs=[pl.BlockSpec((tq, HEAD_DIM), lambda h, g: (g, h)),
                   pl.BlockSpec((None, tq, 1), lambda h, g: (h, g, 0))],
        out_shape=[jax.ShapeDtypeStruct((T, W), F32), jax.ShapeDtypeStruct((H, T, 1), F32)],
        scratch_shapes=[pltpu.VMEM((nc, ta, 1), F32), pltpu.VMEM((nc, ta, 1), F32),
                        pltpu.VMEM((nc, ta, HEAD_DIM), F32)],
        compiler_params=_params(("parallel", "arbitrary")),
    )(z7, z7, z7, c_chunks)


def _attn_fwd_keys_on_rows(z7, vt, c_rep, name):
    _, T, W = z7.shape
    H = W // HEAD_DIM
    ta, nc, n_chunks = _attn_geometry(T)
    scale = np.float32(1.0 / np.sqrt(HEAD_DIM))
    reps = ta // LANES

    def body(q_ref, k_ref, vt_ref, c_ref, o_ref, lse_ref, m_scr, l_scr, acc_scr):
        g = pl.program_id(1)
        m_scr[...] = jnp.full_like(m_scr, NEG_BIG)
        l_scr[...] = jnp.zeros_like(l_scr)
        acc_scr[...] = jnp.zeros_like(acc_scr)

        def update(ch, k, vt, cj, diagonal):
            q = q_ref[ch * ta:(ch + 1) * ta, :]
            st = _mm_nt(k, q) * scale - cj
            if diagonal:
                st = jnp.where(_causal_tile(ta, keys_on_rows=True), st, NEG_BIG)
            m_prev = m_scr[ch]
            m_new = jnp.maximum(m_prev, jnp.max(st, axis=0, keepdims=True))
            alpha = jnp.exp(m_prev - m_new)
            pt = jnp.exp(st - m_new)
            l_scr[ch] = alpha * l_scr[ch] + jnp.sum(pt, axis=0, keepdims=True)
            acc_scr[ch] = alpha * acc_scr[ch] + _mm(vt, pt.astype(BF16))
            m_scr[ch] = m_new

        def load(j):
            cj = _chunk(c_ref, j, ta)
            return _chunk(k_ref, j, ta), vt_ref[j], jnp.concatenate([cj] * reps, axis=1)

        def full_chunk(j, carry):
            k, vt, cj = load(j)
            for ch in range(nc):
                update(ch, k, vt, cj, False)
            return carry

        lax.fori_loop(0, nc * g, full_chunk, 0)
        for jj in range(nc):
            k, vt, cj = load(nc * g + jj)
            for ch in range(jj, nc):
                update(ch, k, vt, cj, ch == jj)
        for ch in range(nc):
            l = l_scr[ch]
            o_ref[ch * ta:(ch + 1) * ta, :] = (acc_scr[ch] / l).T
            lse_ref[ch] = m_scr[ch] + jnp.log(l)

    tq = nc * ta
    return pl.pallas_call(
        body, name=name, grid=(H, n_chunks // nc),
        in_specs=[pl.BlockSpec((None, tq, HEAD_DIM), lambda h, g: (0, g, h)),
                  pl.BlockSpec((None, T, HEAD_DIM), lambda h, g: (1, 0, h)),
                  pl.BlockSpec((None, n_chunks, HEAD_DIM, ta), lambda h, g: (h, 0, 0, 0)),
                  pl.BlockSpec((None, T, LANES), lambda h, g: (h, 0, 0))],
        out_specs=[pl.BlockSpec((tq, HEAD_DIM), lambda h, g: (g, h)),
                   pl.BlockSpec((None, nc, 1, ta), lambda h, g: (h, g, 0, 0))],
        out_shape=[jax.ShapeDtypeStruct((T, W), F32), jax.ShapeDtypeStruct((H, n_chunks, 1, ta), F32)],
        scratch_shapes=[pltpu.VMEM((nc, 1, ta), F32), pltpu.VMEM((nc, 1, ta), F32),
                        pltpu.VMEM((nc, HEAD_DIM, ta), F32)],
        compiler_params=_params(("parallel", "arbitrary")),
    )(z7, z7, vt, c_rep)


def _attn_bwd_fused(z7, kt, dob, c_rep, lse_chunks, d_chunks, name):
    _, T, W = z7.shape
    H = W // HEAD_DIM
    ta, nc, n_chunks = _attn_geometry(T)
    n_steps = n_chunks // nc
    scale = np.float32(1.0 / np.sqrt(HEAD_DIM))
    reps = ta // LANES

    def body(k_ref, v_ref, kt_ref, q_ref, do_ref, c_ref, lse_ref, d_ref,
             dk_ref, dv_ref, dck_ref, dq_ref, dcq_ref, dk_scr, dv_scr, dck_scr, dqt_scr, dcq_scr):
        g = pl.program_id(1)

        @pl.when(g == 0)
        def _():
            dqt_scr[...] = jnp.zeros_like(dqt_scr)
            dcq_scr[...] = jnp.zeros_like(dcq_scr)

        dk_scr[...] = jnp.zeros_like(dk_scr)
        dv_scr[...] = jnp.zeros_like(dv_scr)
        dck_scr[...] = jnp.zeros_like(dck_scr)

        def update(ch, i, q, do, diagonal):
            rows = slice(ch * ta, (ch + 1) * ta)
            cj = c_ref[rows, :]
            st = _mm_nt(k_ref[rows, :], q) * scale - jnp.concatenate([cj] * reps, axis=1) - lse_ref[i]
            if diagonal:
                st = jnp.where(_causal_tile(ta, keys_on_rows=True), st, NEG_BIG)
            pt = jnp.exp(st)
            dv_scr[ch] += _mm(pt.astype(BF16), do)
            dst = pt * (_mm_nt(v_ref[rows, :], do) - d_ref[i])
            dst_b = dst.astype(BF16)
            dk_scr[ch] += _mm(dst_b, q)
            dqt_scr[i] += _mm(kt_ref[ch], dst_b)
            dcq_scr[i] += jnp.sum(dst, axis=0, keepdims=True)
            lane_sum = dst[:, :LANES]
            for r in range(1, reps):
                lane_sum = lane_sum + dst[:, r * LANES:(r + 1) * LANES]
            dck_scr[ch] += lane_sum

        for ii in range(nc):
            i = nc * g + ii
            q = _chunk(q_ref, i, ta)
            do = _chunk(do_ref, i, ta)
            for ch in range(0, ii + 1):
                update(ch, i, q, do, ch == ii)

        def full_chunk(i, carry):
            q = _chunk(q_ref, i, ta)
            do = _chunk(do_ref, i, ta)
            for ch in range(nc):
                update(ch, i, q, do, False)
            return carry

        lax.fori_loop(nc * (g + 1), n_chunks, full_chunk, 0)
        for ch in range(nc):
            rows = slice(ch * ta, (ch + 1) * ta)
            dk_ref[rows, :] = (dk_scr[ch] * scale).astype(BF16)
            dv_ref[rows, :] = dv_scr[ch].astype(BF16)
            dck_ref[rows, :] = -jnp.sum(dck_scr[ch], axis=-1, keepdims=True)

        @pl.when(g == n_steps - 1)
        def _():
            for i in range(n_chunks):
                dq_ref[i * ta:(i + 1) * ta, :] = (dqt_scr[i] * scale).T.astype(BF16)
            dcq_ref[...] = dcq_scr[...]

    tk = nc * ta
    chunks = pl.BlockSpec((None, n_chunks, 1, ta), lambda h, g: (h, 0, 0, 0))
    tile = pl.BlockSpec((tk, HEAD_DIM), lambda h, g: (g, h))
    return pl.pallas_call(
        body, name=name, grid=(H, n_steps),
        in_specs=[pl.BlockSpec((None, tk, HEAD_DIM), lambda h, g: (1, g, h)),
                  pl.BlockSpec((None, tk, HEAD_DIM), lambda h, g: (2, g, h)),
                  pl.BlockSpec((None, nc, HEAD_DIM, ta), lambda h, g: (h, g, 0, 0)),
                  pl.BlockSpec((None, T, HEAD_DIM), lambda h, g: (0, 0, h)),
                  pl.BlockSpec((T, HEAD_DIM), lambda h, g: (0, h)),
                  pl.BlockSpec((None, tk, LANES), lambda h, g: (h, g, 0)),
                  chunks, chunks],
        out_specs=[tile, tile, pl.BlockSpec((None, tk, 1), lambda h, g: (h, g, 0)),
                   pl.BlockSpec((T, HEAD_DIM), lambda h, g: (0, h)), chunks],
        out_shape=[jax.ShapeDtypeStruct((T, W), BF16), jax.ShapeDtypeStruct((T, W), BF16),
                   jax.ShapeDtypeStruct((H, T, 1), F32), jax.ShapeDtypeStruct((T, W), BF16),
                   jax.ShapeDtypeStruct((H, n_chunks, 1, ta), F32)],
        scratch_shapes=[pltpu.VMEM((nc, ta, HEAD_DIM), F32), pltpu.VMEM((nc, ta, HEAD_DIM), F32),
                        pltpu.VMEM((nc, ta, LANES), F32), pltpu.VMEM((n_chunks, HEAD_DIM, ta), F32),
                        pltpu.VMEM((n_chunks, 1, ta), F32)],
        compiler_params=_params(("parallel", "arbitrary")),
    )(z7, z7, kt, z7, dob, c_rep, lse_chunks, d_chunks)


def _attn_bwd_q_loop(z7, dob, c_chunks, lse_col, d_col, name):
    _, T, W = z7.shape
    H = W // HEAD_DIM
    ta, nc, n_chunks = _attn_geometry(T)
    scale = np.float32(1.0 / np.sqrt(HEAD_DIM))

    def body(q_ref, k_ref, v_ref, do_ref, c_ref, lse_ref, d_ref, dq_ref, dc_ref, dq_scr, dc_scr):
        g = pl.program_id(1)
        dq_scr[...] = jnp.zeros_like(dq_scr)
        dc_scr[...] = jnp.zeros_like(dc_scr)

        def update(ch, k, v, crow, diagonal):
            rows = slice(ch * ta, (ch + 1) * ta)
            do = do_ref[rows, :]
            s = _mm_nt(q_ref[rows, :], k) * scale - crow - lse_ref[rows, :]
            if diagonal:
                s = jnp.where(_causal_tile(ta), s, NEG_BIG)
            p = jnp.exp(s)
            ds = p * (_mm_nt(do, v) - d_ref[rows, :])
            dq_scr[ch] += _mm(ds.astype(BF16), k)
            dc_scr[ch] += jnp.sum(ds, axis=-1, keepdims=True)

        def full_chunk(j, carry):
            k = _chunk(k_ref, j, ta)
            v = _chunk(v_ref, j, ta)
            crow = c_ref[j]
            for ch in range(nc):
                update(ch, k, v, crow, False)
            return carry

        lax.fori_loop(0, nc * g, full_chunk, 0)
        for jj in range(nc):
            j = nc * g + jj
            k = _chunk(k_ref, j, ta)
            v = _chunk(v_ref, j, ta)
            crow = c_ref[j]
            for ch in range(jj, nc):
                update(ch, k, v, crow, ch == jj)
        for ch in range(nc):
            dq_ref[ch * ta:(ch + 1) * ta, :] = (dq_scr[ch] * scale).astype(BF16)
            dc_ref[ch * ta:(ch + 1) * ta, :] = dc_scr[ch]

    tq = nc * ta
    col = pl.BlockSpec((None, tq, 1), lambda h, g: (h, g, 0))
    return pl.pallas_call(
        body, name=name, grid=(H, n_chunks // nc),
        in_specs=[pl.BlockSpec((None, tq, HEAD_DIM), lambda h, g: (0, g, h)),
                  pl.BlockSpec((None, T, HEAD_DIM), lambda h, g: (1, 0, h)),
                  pl.BlockSpec((None, T, HEAD_DIM), lambda h, g: (2, 0, h)),
                  pl.BlockSpec((tq, HEAD_DIM), lambda h, g: (g, h)),
                  pl.BlockSpec((None, n_chunks, 1, ta), lambda h, g: (h, 0, 0, 0)),
                  col, col],
        out_specs=[pl.BlockSpec((tq, HEAD_DIM), lambda h, g: (g, h)), col],
        out_shape=[jax.ShapeDtypeStruct((T, W), BF16), jax.ShapeDtypeStruct((H, T, 1), F32)],
        scratch_shapes=[pltpu.VMEM((nc, ta, HEAD_DIM), F32), pltpu.VMEM((nc, ta, 1), F32)],
        compiler_params=_params(("parallel", "arbitrary")),
    )(z7, z7, z7, dob, c_chunks, lse_col, d_col)


def _attn_bwd_kv_loop(z7, dob, c_col, lse_chunks, d_chunks, name):
    _, T, W = z7.shape
    H = W // HEAD_DIM
    ta, nc, n_chunks = _attn_geometry(T)
    scale = np.float32(1.0 / np.sqrt(HEAD_DIM))

    def body(k_ref, v_ref, q_ref, do_ref, ccol_ref, lse_ref, d_ref, dk_ref, dv_ref, dc_ref, dk_scr, dv_scr, dc_scr):
        g = pl.program_id(1)
        dk_scr[...] = jnp.zeros_like(dk_scr)
        dv_scr[...] = jnp.zeros_like(dv_scr)
        dc_scr[...] = jnp.zeros_like(dc_scr)

        def update(ch, q, do, lse_row, d_row, diagonal):
            rows = slice(ch * ta, (ch + 1) * ta)
            st = _mm_nt(k_ref[rows, :], q) * scale - ccol_ref[rows, :] - lse_row
            if diagonal:
                st = jnp.where(_causal_tile(ta, keys_on_rows=True), st, NEG_BIG)
            pt = jnp.exp(st)
            dv_scr[ch] += _mm(pt.astype(BF16), do)
            dst = pt * (_mm_nt(v_ref[rows, :], do) - d_row)
            dk_scr[ch] += _mm(dst.astype(BF16), q)
            dc_scr[ch] += jnp.sum(dst, axis=-1, keepdims=True)

        for ii in range(nc):
            i = nc * g + ii
            q = _chunk(q_ref, i, ta)
            do = _chunk(do_ref, i, ta)
            for ch in range(0, ii + 1):
                update(ch, q, do, lse_ref[i], d_ref[i], ch == ii)

        def full_chunk(i, carry):
            q = _chunk(q_ref, i, ta)
            do = _chunk(do_ref, i, ta)
            for ch in range(nc):
                update(ch, q, do, lse_ref[i], d_ref[i], False)
            return carry

        lax.fori_loop(nc * (g + 1), n_chunks, full_chunk, 0)
        for ch in range(nc):
            rows = slice(ch * ta, (ch + 1) * ta)
            dk_ref[rows, :] = (dk_scr[ch] * scale).astype(BF16)
            dv_ref[rows, :] = dv_scr[ch].astype(BF16)
            dc_ref[rows, :] = -dc_scr[ch]

    tk = nc * ta
    chunks = pl.BlockSpec((None, n_chunks, 1, ta), lambda h, g: (h, 0, 0, 0))
    col = pl.BlockSpec((None, tk, 1), lambda h, g: (h, g, 0))
    tile = pl.BlockSpec((tk, HEAD_DIM), lambda h, g: (g, h))
    return pl.pallas_call(
        body, name=name, grid=(H, n_chunks // nc),
        in_specs=[pl.BlockSpec((None, tk, HEAD_DIM), lambda h, g: (1, g, h)),
                  pl.BlockSpec((None, tk, HEAD_DIM), lambda h, g: (2, g, h)),
                  pl.BlockSpec((None, T, HEAD_DIM), lambda h, g: (0, 0, h)),
                  pl.BlockSpec((T, HEAD_DIM), lambda h, g: (0, h)),
                  col, chunks, chunks],
        out_specs=[tile, tile, col],
        out_shape=[jax.ShapeDtypeStruct((T, W), BF16), jax.ShapeDtypeStruct((T, W), BF16),
                   jax.ShapeDtypeStruct((H, T, 1), F32)],
        scratch_shapes=[pltpu.VMEM((nc, ta, HEAD_DIM), F32), pltpu.VMEM((nc, ta, HEAD_DIM), F32),
                        pltpu.VMEM((nc, ta, 1), F32)],
        compiler_params=_params(("parallel", "arbitrary")),
    )(z7, z7, z7, dob, c_col, lse_chunks, d_chunks)


def _chunk_causal_mask():
    rows = lax.broadcasted_iota(jnp.int32, (SGU_LEN, SGU_LEN), 0)
    cols = lax.broadcasted_iota(jnp.int32, (SGU_LEN, SGU_LEN), 1)
    return (cols // CHUNK) <= (rows // CHUNK)


def _sgu_norm_mix(sv, lng_ref, lnb_ref, ws_ref, bs_ref, vn_scr, mixed_scr, vhat_scr=None):
    tm = sv.shape[0]
    vs = _gelu(sv)
    mask = _chunk_causal_mask()
    rstds = []
    for g in range(N_GROUPS):
        lanes = slice(g * GROUP_DIM, (g + 1) * GROUP_DIM)
        blk = vs[:, lanes]
        cen = blk - jnp.mean(blk, axis=-1, keepdims=True)
        rstd = lax.rsqrt(jnp.mean(cen * cen, axis=-1, keepdims=True) + LN_EPS)
        vhat = cen * rstd
        rstds.append(rstd)
        if vhat_scr is not None:
            vhat_scr[:, lanes] = vhat
        vn_scr[:, lanes] = (vhat * lng_ref[:, lanes] + lnb_ref[:, lanes]).astype(BF16)
        wm = jnp.where(mask, ws_ref[g], 0.0).astype(BF16)
        for w in range(tm // SGU_LEN):
            rows = slice(w * SGU_LEN, (w + 1) * SGU_LEN)
            mixed_scr[rows, lanes] = _mm(wm, vn_scr[rows, lanes]) + bs_ref[g]
    return rstds


def _mix_out_fwd(z7, o_a, x1, lng, lnb, ws, bs, w_out, g_post, name):
    _, T, W = z7.shape
    D = x1.shape[1]
    tm = _blk(T, 256)

    def body(u_ref, sv_ref, ga_ref, gb_ref, oa_ref, x1_ref, lng_ref, lnb_ref, ws_ref, bs_ref, wo_ref, gp_ref,
             x2_ref, p_ref, mb_ref, vn_scr, mixed_scr):
        _sgu_norm_mix(sv_ref[...].astype(F32), lng_ref, lnb_ref, ws_ref, bs_ref, vn_scr, mixed_scr)
        o_b = _gelu(u_ref[...].astype(F32)) * mixed_scr[...]
        merged = (jax.nn.sigmoid(ga_ref[...].astype(F32)) * oa_ref[...]
                  + jax.nn.sigmoid(gb_ref[...].astype(F32)) * o_b).astype(BF16)
        mb_ref[...] = merged
        p = _mm(merged, wo_ref[...])
        p_ref[...] = p
        x2_ref[...] = x1_ref[...] + p * _rms_scale(p) * gp_ref[...]

    def seg(idx):
        return pl.BlockSpec((None, tm, W), lambda i, idx=idx: (idx, i, 0))

    row = pl.BlockSpec((tm, D), lambda i: (i, 0))
    vec = pl.BlockSpec((1, D), lambda i: (0, 0))
    return pl.pallas_call(
        body, name=name, grid=(T // tm,),
        in_specs=[seg(3), seg(4), seg(5), seg(6), row, row, vec, vec,
                  pl.BlockSpec((N_GROUPS, SGU_LEN, SGU_LEN), lambda i: (0, 0, 0)),
                  pl.BlockSpec((N_GROUPS, SGU_LEN, 1), lambda i: (0, 0, 0)),
                  pl.BlockSpec((D, D), lambda i: (0, 0)), vec],
        out_specs=[row, row, row],
        out_shape=[jax.ShapeDtypeStruct((T, D), F32), jax.ShapeDtypeStruct((T, D), F32),
                   jax.ShapeDtypeStruct((T, D), BF16)],
        scratch_shapes=[pltpu.VMEM((tm, W), BF16), pltpu.VMEM((tm, W), F32)],
        compiler_params=_params(("parallel",)),
    )(z7, z7, z7, z7, o_a, x1, lng, lnb, ws, bs, w_out, g_post)


def _mix_out_bwd(dx2, p, z7, o_a, lng, lnb, ws, bs, w_out, g_post, name, dep=None):
    _, T, W = z7.shape
    D = dx2.shape[1]
    tm = _blk(T, 256)
    n_w = tm // SGU_LEN

    def body(dx2_ref, p_ref, u_ref, sv_ref, ga_ref, gb_ref, oa_ref, lng_ref, lnb_ref, ws_ref, bs_ref, wo_ref, gp_ref, _,
             dpb_ref, dob_ref, dvec_ref, dz_ref, dgp_ref, dlng_ref, dlnb_ref, dws_ref, dbs_ref,
             vn_scr, mixed_scr, vhat_scr, dmix_scr, dvn_scr):
        @pl.when(pl.program_id(0) == 0)
        def _():
            dgp_ref[...] = jnp.zeros_like(dgp_ref)
            dlng_ref[...] = jnp.zeros_like(dlng_ref)
            dlnb_ref[...] = jnp.zeros_like(dlnb_ref)
            dws_ref[...] = jnp.zeros_like(dws_ref)
            dbs_ref[...] = jnp.zeros_like(dbs_ref)

        pv = p_ref[...]
        s = _rms_scale(pv)
        n = pv * s
        dn = dx2_ref[...]
        dgp_ref[...] += jnp.sum(dn * n, axis=0, keepdims=True)
        dpb = _rms_bwd(dn, n, s, gp_ref[...]).astype(BF16)
        dpb_ref[...] = dpb
        dmerged = _mm_nt(dpb, wo_ref[...])

        sv = sv_ref[...].astype(F32)
        rstds = _sgu_norm_mix(sv, lng_ref, lnb_ref, ws_ref, bs_ref, vn_scr, mixed_scr, vhat_scr)
        u_pre = u_ref[...].astype(F32)
        u = _gelu(u_pre)
        mixed = mixed_scr[...]
        sa = jax.nn.sigmoid(ga_ref[...].astype(F32))
        sb = jax.nn.sigmoid(gb_ref[...].astype(F32))
        oa = oa_ref[...]
        do_a = (dmerged * sa).astype(BF16)
        dob_ref[...] = do_a
        prod = do_a.astype(F32) * oa
        for h in range(N_HEADS):
            dvec_ref[h] = jnp.sum(prod[:, h * HEAD_DIM:(h + 1) * HEAD_DIM], axis=-1, keepdims=True)
        dz_ref[2] = (dmerged * oa * (sa * (1.0 - sa))).astype(BF16)
        dz_ref[3] = (dmerged * (u * mixed) * (sb * (1.0 - sb))).astype(BF16)
        do_b = dmerged * sb
        dz_ref[0] = (do_b * mixed * _gelu_grad(u_pre)).astype(BF16)
        dmix_scr[...] = do_b * u

        mask = _chunk_causal_mask()
        for g in range(N_GROUPS):
            lanes = slice(g * GROUP_DIM, (g + 1) * GROUP_DIM)
            wm = jnp.where(mask, ws_ref[g], 0.0).astype(BF16)
            dws = jnp.zeros((SGU_LEN, SGU_LEN), F32)
            dbs = jnp.zeros((SGU_LEN, 1), F32)
            for w in range(n_w):
                rows = slice(w * SGU_LEN, (w + 1) * SGU_LEN)
                dmix = dmix_scr[rows, lanes]
                dmix_b = dmix.astype(BF16)
                dvn_scr[rows, lanes] = _mm_tn(wm, dmix_b)
                dws = dws + _mm_nt(dmix_b, vn_scr[rows, lanes])
                dbs = dbs + jnp.sum(dmix, axis=-1, keepdims=True)
            dws_ref[g] += jnp.where(mask, dws, 0.0)
            dbs_ref[g] += dbs
            dvn = dvn_scr[:, lanes]
            vhat = vhat_scr[:, lanes]
            dlng_ref[:, lanes] += jnp.sum(dvn * vhat, axis=0, keepdims=True)
            dlnb_ref[:, lanes] += jnp.sum(dvn, axis=0, keepdims=True)
            dvh = dvn * lng_ref[:, lanes]
            dvs = rstds[g] * (dvh - jnp.mean(dvh, axis=-1, keepdims=True)
                              - vhat * jnp.mean(dvh * vhat, axis=-1, keepdims=True))
            dvn_scr[:, lanes] = dvs
        dz_ref[1] = (dvn_scr[...] * _gelu_grad(sv)).astype(BF16)

    def seg(idx):
        return pl.BlockSpec((None, tm, W), lambda i, idx=idx: (idx, i, 0))

    row = pl.BlockSpec((tm, D), lambda i: (i, 0))
    vec = pl.BlockSpec((1, D), lambda i: (0, 0))
    ws_spec = pl.BlockSpec((N_GROUPS, SGU_LEN, SGU_LEN), lambda i: (0, 0, 0))
    bs_spec = pl.BlockSpec((N_GROUPS, SGU_LEN, 1), lambda i: (0, 0, 0))
    return pl.pallas_call(
        body, name=name, grid=(T // tm,),
        in_specs=[row, row, seg(3), seg(4), seg(5), seg(6), row, vec, vec, ws_spec, bs_spec,
                  pl.BlockSpec((D, D), lambda i: (0, 0)), vec, ANY],
        out_specs=[row, row, pl.BlockSpec((N_HEADS, tm, 1), lambda i: (0, i, 0)),
                   pl.BlockSpec((4, tm, W), lambda i: (0, i, 0)), vec, vec, vec, ws_spec, bs_spec],
        out_shape=[jax.ShapeDtypeStruct((T, D), BF16), jax.ShapeDtypeStruct((T, W), BF16),
                   jax.ShapeDtypeStruct((N_HEADS, T, 1), F32), jax.ShapeDtypeStruct((4, T, W), BF16),
                   jax.ShapeDtypeStruct((1, D), F32), jax.ShapeDtypeStruct((1, D), F32),
                   jax.ShapeDtypeStruct((1, D), F32),
                   jax.ShapeDtypeStruct((N_GROUPS, SGU_LEN, SGU_LEN), F32),
                   jax.ShapeDtypeStruct((N_GROUPS, SGU_LEN, 1), F32)],
        scratch_shapes=[pltpu.VMEM((tm, W), BF16), pltpu.VMEM((tm, W), F32), pltpu.VMEM((tm, W), F32),
                        pltpu.VMEM((tm, W), F32), pltpu.VMEM((tm, W), F32)],
        compiler_params=_params(("arbitrary",)),
    )(dx2, p, z7, z7, z7, z7, o_a, lng, lnb, ws, bs, w_out, g_post, _after(dep))


def _loss_head(y, target, name):
    T, D = y.shape
    tm = _blk(T, 1024)
    n_i = T // tm

    def body(y_ref, t_ref, dy_ref, loss_ref, acc_scr):
        i = pl.program_id(0)

        @pl.when(i == 0)
        def _():
            acc_scr[...] = jnp.zeros_like(acc_scr)

        e = y_ref[...] - t_ref[...]
        dy_ref[...] = e * np.float32(1.0 / D)
        acc_scr[...] += jnp.sum(e * e, axis=0, keepdims=True)

        @pl.when(i == n_i - 1)
        def _():
            total = jnp.sum(acc_scr[...], axis=-1, keepdims=True) * np.float32(0.5 / D)
            loss_ref[...] = jnp.broadcast_to(total, loss_ref.shape)

    row = pl.BlockSpec((tm, D), lambda i: (i, 0))
    return pl.pallas_call(
        body, name=name, grid=(n_i,),
        in_specs=[row, row],
        out_specs=[row, pl.BlockSpec((1, LANES), lambda i: (0, 0))],
        out_shape=[jax.ShapeDtypeStruct((T, D), F32), jax.ShapeDtypeStruct((1, LANES), F32)],
        scratch_shapes=[pltpu.VMEM((1, D), F32)],
        compiler_params=_params(("arbitrary",)),
    )(y, target)


def _adamw_math(w, g, m, v):
    m_new = ADAM_B1 * m + (1.0 - ADAM_B1) * g
    v_new = ADAM_B2 * v + (1.0 - ADAM_B2) * (g * g)
    m_hat = m_new / np.float32(1.0 - ADAM_B1 ** ADAM_STEP)
    v_hat = v_new / np.float32(1.0 - ADAM_B2 ** ADAM_STEP)
    delta = -ADAM_LR * (m_hat / (jnp.sqrt(v_hat) + ADAM_EPS) + ADAM_WD * w)
    return delta, m_new, v_new


def _sum_adamw(parts, w, m, v, name):
    n, R, C = parts.shape
    tr = _blk(R, 128)

    def body(p_ref, w_ref, m_ref, v_ref, g_ref, d_ref, mo_ref, vo_ref):
        g = p_ref[0].astype(F32)
        for s in range(1, n):
            g = g + p_ref[s].astype(F32)
        delta, m_new, v_new = _adamw_math(w_ref[...], g, m_ref[...], v_ref[...])
        g_ref[...] = g
        d_ref[...] = delta
        mo_ref[...] = m_new
        vo_ref[...] = v_new

    row = pl.BlockSpec((tr, C), lambda i: (i, 0))
    shp = jax.ShapeDtypeStruct((R, C), F32)
    return pl.pallas_call(
        body, name=name, grid=(R // tr,),
        in_specs=[pl.BlockSpec((n, tr, C), lambda i: (0, i, 0)), row, row, row],
        out_specs=[row, row, row, row], out_shape=[shp, shp, shp, shp],
        compiler_params=_params(("parallel",)),
    )(parts, w, m, v)


def _adamw(g, w, m, v, name):
    R, C = g.shape
    tr = _blk(R, 128)

    def body(g_ref, w_ref, m_ref, v_ref, d_ref, mo_ref, vo_ref):
        delta, m_new, v_new = _adamw_math(w_ref[...], g_ref[...], m_ref[...], v_ref[...])
        d_ref[...] = delta
        mo_ref[...] = m_new
        vo_ref[...] = v_new

    row = pl.BlockSpec((tr, C), lambda i: (i, 0))
    shp = jax.ShapeDtypeStruct((R, C), F32)
    return pl.pallas_call(
        body, name=name, grid=(R // tr,),
        in_specs=[row, row, row, row], out_specs=[row, row, row], out_shape=[shp, shp, shp],
        compiler_params=_params(("parallel",)),
    )(g, w, m, v)


def _position():
    return lax.axis_index("x"), lax.axis_index("y"), lax.axis_index("c")


def _slot(px, py, pc):
    return 4 * px + 2 * py + pc


def _all_gather(shards, name):
    n = len(shards)

    def body(*refs):
        ins, outs = refs[:n], refs[n:2 * n]
        send_sems, recv_sems, local_sems = refs[2 * n:]
        x, y, c = _position()
        me, sibling = (x, y, c), (x, y, 1 - c)
        chips = [(1 - x, y), (x, 1 - y), (1 - x, 1 - y)]

        def copy(a, k, block, to, src=None):
            dst = outs[a].at[_slot(*block)]
            return pltpu.make_async_remote_copy(
                src_ref=dst if src is None else src, dst_ref=dst,
                send_sem=send_sems.at[a, k], recv_sem=recv_sems.at[a, k],
                device_id=to, device_id_type=MESH)

        mine = [pltpu.make_async_copy(ins[a], outs[a].at[_slot(*me)], local_sems.at[a]) for a in range(n)]
        for cp in mine:
            cp.start()
        first = []
        for a in range(n):
            first.append(copy(a, 0, me, sibling, src=ins[a]))
            first += [copy(a, 1 + j, me, (*chip, c), src=ins[a]) for j, chip in enumerate(chips)]
        for cp in first:
            cp.start()
        passed = []
        for j, chip in enumerate(chips):
            for a in range(n):
                copy(a, 1 + j, (*chip, c), me).wait_recv()
                fwd = copy(a, 4 + j, (*chip, c), sibling)
                fwd.start()
                passed.append(fwd)
        for a in range(n):
            copy(a, 0, sibling, me).wait_recv()
            for j, chip in enumerate(chips):
                copy(a, 4 + j, (*chip, 1 - c), me).wait_recv()
        for cp in first + passed:
            cp.wait_send()
        for cp in mine:
            cp.wait()

    return pl.pallas_call(
        body, name=name,
        in_specs=[ANY] * n, out_specs=[ANY] * n,
        out_shape=[jax.ShapeDtypeStruct((N_DEV,) + s.shape, s.dtype) for s in shards],
        scratch_shapes=[pltpu.SemaphoreType.DMA((n, 7)), pltpu.SemaphoreType.DMA((n, 7)),
                        pltpu.SemaphoreType.DMA((n,))],
    )(*shards)


def _peer(x, y, c, k):
    return (1 - x if k & 4 else x, 1 - y if k & 2 else y, 1 - c if k & 1 else c)


def _exchange(parts, name):
    n = len(parts)

    def body(*refs):
        ins, outs = refs[:n], refs[n:2 * n]
        send_sems, recv_sems, local_sems = refs[2 * n:]
        x, y, c = _position()
        me = _slot(x, y, c)
        mine = [pltpu.make_async_copy(ins[a].at[me], outs[a].at[me], local_sems.at[a]) for a in range(n)]
        for cp in mine:
            cp.start()
        sends = []
        for k in range(1, N_DEV):
            to = _peer(x, y, c, k)
            for a in range(n):
                cp = pltpu.make_async_remote_copy(
                    src_ref=ins[a].at[_slot(*to)], dst_ref=outs[a].at[me],
                    send_sem=send_sems.at[a, k - 1], recv_sem=recv_sems.at[a, k - 1],
                    device_id=to, device_id_type=MESH)
                cp.start()
                sends.append(cp)
        for k in range(1, N_DEV):
            frm = _peer(x, y, c, k)
            for a in range(n):
                pltpu.make_async_remote_copy(
                    src_ref=ins[a].at[_slot(*frm)], dst_ref=outs[a].at[_slot(*frm)],
                    send_sem=send_sems.at[a, k - 1], recv_sem=recv_sems.at[a, k - 1],
                    device_id=frm, device_id_type=MESH).wait_recv()
        for cp in sends:
            cp.wait_send()
        for cp in mine:
            cp.wait()

    return pl.pallas_call(
        body, name=name,
        in_specs=[ANY] * n, out_specs=[ANY] * n,
        out_shape=[jax.ShapeDtypeStruct(p.shape, p.dtype) for p in parts],
        scratch_shapes=[pltpu.SemaphoreType.DMA((n, 7)), pltpu.SemaphoreType.DMA((n, 7)),
                        pltpu.SemaphoreType.DMA((n,))],
    )(*parts)


HBM_SPEC = pl.BlockSpec(memory_space=pltpu.HBM)
SEM_SPEC = pl.BlockSpec(memory_space=pltpu.SEMAPHORE)
SIDE_EFFECT = pltpu.SideEffectType.DATAFLOW_SIDE_EFFECTING


def _remote_copies(src_refs, land_refs, send_sems, recv_sems, gather, outgoing):
    x, y, c = _position()
    me = _slot(x, y, c)
    copies = []
    for k in range(1, N_DEV):
        peer = _peer(x, y, c, k)
        for a in range(len(src_refs)):
            src = src_refs[a] if gather else src_refs[a].at[_slot(*peer)]
            dst = land_refs[a].at[me if outgoing else _slot(*peer)]
            sem = a * (N_DEV - 1) + k - 1
            copies.append(pltpu.make_async_remote_copy(
                src_ref=src, dst_ref=dst, send_sem=send_sems.at[sem], recv_sem=recv_sems.at[sem],
                device_id=peer, device_id_type=MESH))
    return copies


def _remote_start(srcs, after, name, gather):
    n = len(srcs)
    lands = [jax.ShapeDtypeStruct(((N_DEV,) + s.shape) if gather else s.shape, s.dtype) for s in srcs]

    def body(*refs):
        src_refs, land_refs = refs[:n], refs[n:2 * n]
        send_sems, recv_sems = refs[2 * n + 1], refs[2 * n + 2]
        token, local_sems = refs[4 * n + 3], refs[4 * n + 4]
        x, y, c = _position()
        me = _slot(x, y, c)
        mine = [pltpu.make_async_copy(src_refs[a] if gather else src_refs[a].at[me], land_refs[a].at[me],
                                      local_sems.at[a]) for a in range(n)]
        for cp in mine:
            cp.start()
        for cp in _remote_copies(src_refs, land_refs, send_sems, recv_sems, gather, outgoing=True):
            cp.start()
        for cp in mine:
            cp.wait()
        token[...] = jnp.zeros_like(token)

    sem_shape = pltpu.SemaphoreType.DMA((n * (N_DEV - 1),))
    outs = pl.pallas_call(
        body, name=name,
        out_shape=(sem_shape, sem_shape, *[pltpu.HBM(s.shape, s.dtype) for s in srcs],
                   *[pltpu.HBM(l.shape, l.dtype) for l in lands], jax.ShapeDtypeStruct((8, LANES), F32)),
        in_specs=[HBM_SPEC] * (2 * n) + [ANY],
        out_specs=(SEM_SPEC, SEM_SPEC, *([HBM_SPEC] * (2 * n)), pl.BlockSpec(memory_space=pltpu.VMEM)),
        input_output_aliases={a: 2 + a for a in range(2 * n)},
        scratch_shapes=[pltpu.SemaphoreType.DMA((n,))],
        compiler_params=pltpu.CompilerParams(has_side_effects=SIDE_EFFECT),
    )(*[pltpu.with_memory_space_constraint(s, pltpu.HBM) for s in srcs],
      *[pltpu.with_memory_space_constraint(lax.empty(l.shape, l.dtype), pltpu.HBM) for l in lands], after)
    return dict(send=outs[0], recv=outs[1], srcs=outs[2:2 + n], lands=outs[2 + n:2 + 2 * n], token=outs[-1],
                gather=gather)


def _remote_wait(flight, after, name):
    n = len(flight["srcs"])
    gather = flight["gather"]

    def body(*refs):
        src_refs, land_refs = refs[:n], refs[n:2 * n]
        send_sems, recv_sems = refs[2 * n], refs[2 * n + 1]
        for cp in _remote_copies(src_refs, land_refs, send_sems, recv_sems, gather, outgoing=False):
            cp.wait_send()
            cp.wait_recv()

    both = list(flight["srcs"]) + list(flight["lands"])
    outs = pl.pallas_call(
        body, name=name,
        out_shape=tuple(pltpu.HBM(a.shape, a.dtype) for a in both),
        in_specs=[HBM_SPEC] * (2 * n) + [SEM_SPEC, SEM_SPEC, ANY],
        out_specs=tuple([HBM_SPEC] * (2 * n)),
        input_output_aliases={a: a for a in range(2 * n)},
        compiler_params=pltpu.CompilerParams(has_side_effects=SIDE_EFFECT),
    )(*both, flight["send"], flight["recv"], after)
    return list(outs[n:])


def _all_reduce_small(blob, name):
    R, C = blob.shape

    def body(in_ref, out_ref, gath, send_sems, recv_sems):
        x, y, c = _position()
        me = _slot(x, y, c)
        gath[me] = in_ref[...]
        sends = []
        for k in range(1, N_DEV):
            to = _peer(x, y, c, k)
            cp = pltpu.make_async_remote_copy(
                src_ref=in_ref, dst_ref=gath.at[me],
                send_sem=send_sems.at[k - 1], recv_sem=recv_sems.at[k - 1],
                device_id=to, device_id_type=MESH)
            cp.start()
            sends.append(cp)
        for k in range(1, N_DEV):
            frm = _peer(x, y, c, k)
            pltpu.make_async_remote_copy(
                src_ref=in_ref, dst_ref=gath.at[_slot(*frm)],
                send_sem=send_sems.at[k - 1], recv_sem=recv_sems.at[k - 1],
                device_id=frm, device_id_type=MESH).wait_recv()
        for cp in sends:
            cp.wait_send()
        total = gath[0]
        for s in range(1, N_DEV):
            total = total + gath[s]
        out_ref[...] = total

    return pl.pallas_call(
        body, name=name,
        in_specs=[pl.BlockSpec(memory_space=pltpu.VMEM)],
        out_specs=pl.BlockSpec(memory_space=pltpu.VMEM),
        out_shape=jax.ShapeDtypeStruct((R, C), F32),
        scratch_shapes=[pltpu.VMEM((N_DEV, R, C), F32), pltpu.SemaphoreType.DMA((7,)),
                        pltpu.SemaphoreType.DMA((7,))],
        compiler_params=pltpu.CompilerParams(vmem_limit_bytes=VMEM_LIMIT),
    )(blob)


SMALL_VECS = ("ffn1_pre_g", "ffn1_post_g", "mix_pre_g", "sgu_ln_g", "sgu_ln_b", "mix_post_g", "ffn2_pre_g",
              "ffn2_post_g")
ROW_BS = len(SMALL_VECS)
ROW_BF = ROW_BS + 1
ROW_LOSS = ROW_BF + 1
ROW_WS = 16
BLOB_ROWS = ROW_WS + SGU_LEN


def _pack_small(vals, D, loss_row=None):
    rows = [vals[n].reshape(1, D) for n in SMALL_VECS]
    rows.append(vals["sgu_b_s"].reshape(1, D))
    rows.append(jnp.pad(vals["b_forget"].reshape(1, N_HEADS), ((0, 0), (0, D - N_HEADS))))
    rows.append(jnp.zeros((1, D), F32) if loss_row is None else loss_row)
    rows.append(jnp.zeros((ROW_WS - ROW_LOSS - 1, D), F32))
    rows.append(vals["sgu_w_s"].reshape(SGU_LEN, D))
    return jnp.concatenate(rows, axis=0)


def _unpack_small(blob, D):
    out = {n: blob[r:r + 1] for r, n in enumerate(SMALL_VECS)}
    out["sgu_b_s"] = blob[ROW_BS].reshape(1, N_GROUPS, SGU_LEN)
    out["b_forget"] = blob[ROW_BF, :N_HEADS].reshape(1, N_HEADS)
    out["sgu_w_s"] = blob[ROW_WS:].reshape(1, N_GROUPS, SGU_LEN, SGU_LEN)
    return out


WEIGHT_NAMES = ("ffn1_pre_g", "ffn1_w_gate", "ffn1_w_up", "ffn1_w_down", "ffn1_post_g", "mix_pre_g", "w_in",
                "b_forget", "sgu_ln_g", "sgu_ln_b", "sgu_w_s", "sgu_b_s", "w_out", "mix_post_g", "ffn2_pre_g",
                "ffn2_w_gate", "ffn2_w_up", "ffn2_w_down", "ffn2_post_g")
BIG_NAMES = ("ffn1_w_gate", "ffn1_w_up", "ffn1_w_down", "w_in", "w_out", "ffn2_w_gate", "ffn2_w_up", "ffn2_w_down")
WEIGHT_GROUPS = {"ffn1": ("ffn1_w_gate", "ffn1_w_up", "ffn1_w_down"), "mix": ("w_in", "w_out"),
                 "ffn2": ("ffn2_w_gate", "ffn2_w_up", "ffn2_w_down")}
GRAD_GROUPS = (("ffn2_w_gate", "ffn2_w_up", "ffn2_w_down"), ("w_out", "w_in"), ("ffn1_w_down", "ffn1_w_gate"),
               ("ffn1_w_up",))


def _local_step(x, target, small, fetch, emit):
    T, D = x.shape
    W = N_HEADS * HEAD_DIM
    vec = lambda n: small[n].reshape(1, D)
    big = dict(fetch("ffn1", x))

    x1, y1, gate1, up1 = _ffn_fwd(x, vec("ffn1_pre_g"), big["ffn1_w_gate"], big["ffn1_w_up"], big["ffn1_w_down"],
                                  vec("ffn1_post_g"), "ffn1_fwd")

    big.update(fetch("mix", x1))
    w_in_all = big["w_in"]
    in_width = N_DEV * w_in_all.shape[2]
    w_in = w_in_all.transpose(1, 0, 2).reshape(D, in_width)
    col_f = 3 * W
    col_u = col_f + N_HEADS
    seg_starts = (0, W, 2 * W, col_u, col_u + W, col_u + 2 * W, col_u + 3 * W)
    w7 = jnp.stack([w_in[:, s:s + W] for s in seg_starts])
    wf = jnp.pad(w_in[:, col_f:col_u], ((0, 0), (0, LANES - N_HEADS)))
    w_out = big["w_out"].reshape(D, D)
    b_pad = jnp.pad(small["b_forget"].reshape(1, N_HEADS), ((0, 0), (0, LANES - N_HEADS)))
    lng, lnb = vec("sgu_ln_g"), vec("sgu_ln_b")
    ws = small["sgu_w_s"].reshape(N_GROUPS, SGU_LEN, SGU_LEN)
    bs = small["sgu_b_s"].reshape(N_GROUPS, SGU_LEN, 1)

    z7, f_logit, h2b = _mix_in_fwd(x1, vec("mix_pre_g"), w7, wf, "mix_in_fwd")
    c = _forget_cumsum(f_logit, b_pad, "forget_cumsum")
    c_heads = c[:, :N_HEADS].T
    ta, _, n_chunks = _attn_geometry(T)
    c_chunks = c_heads.reshape(N_HEADS, n_chunks, 1, ta)
    c_col = c_heads[:, :, None]
    vt = z7[2].reshape(n_chunks, ta, N_HEADS, HEAD_DIM).transpose(2, 0, 3, 1)
    c_rep = jnp.broadcast_to(c_col, (N_HEADS, T, LANES))
    o_a, lse_chunks = _attn_fwd_keys_on_rows(z7, vt, c_rep, "attn_fwd")
    lse = lse_chunks.reshape(N_HEADS, T, 1)
    x2, p, merged_b = _mix_out_fwd(z7, o_a, x1, lng, lnb, ws, bs, w_out, vec("mix_post_g"), "mix_out_fwd")
    big.update(fetch("ffn2", x2))
    x3, y2, gate2, up2 = _ffn_fwd(x2, vec("ffn2_pre_g"), big["ffn2_w_gate"], big["ffn2_w_up"], big["ffn2_w_down"],
                                  vec("ffn2_post_g"), "ffn2_fwd")
    dy, loss_lanes = _loss_head(x3, target, "loss_head")

    grads_small = {}

    dx2, h3b, dy2b, act2, dgate2, dup2, dgpre, dgpost = _ffn_bwd(
        dy, x2, y2, gate2, up2, vec("ffn2_pre_g"), big["ffn2_w_gate"], big["ffn2_w_up"], big["ffn2_w_down"],
        vec("ffn2_post_g"), "ffn2_bwd")
    grads_small["ffn2_pre_g"] = jnp.sum(dgpre, axis=0)
    grads_small["ffn2_post_g"] = jnp.sum(dgpost, axis=0)
    emit("ffn2_w_gate", _wgrad(h3b, dgate2, "ffn2_wgrad_gate", shard_cols=True))
    emit("ffn2_w_up", _wgrad(h3b, dup2, "ffn2_wgrad_up", shard_cols=True))
    dep = emit("ffn2_w_down", _wgrad(act2, dy2b, "ffn2_wgrad_down").reshape(big["ffn2_w_down"].shape))

    dpb, dob, dvec, dz4, dgp, dlng, dlnb, dws, dbs = _mix_out_bwd(
        dx2, p, z7, o_a, lng, lnb, ws, bs, w_out, vec("mix_post_g"), "mix_out_bwd", dep=dep)
    grads_small["mix_post_g"] = dgp
    grads_small["sgu_ln_g"] = dlng
    grads_small["sgu_ln_b"] = dlnb
    grads_small["sgu_w_s"] = dws
    grads_small["sgu_b_s"] = dbs
    emit("w_out", _wgrad(merged_b, dpb, "w_out_wgrad").reshape(big["w_out"].shape))
    d_chunks = dvec.reshape(N_HEADS, n_chunks, 1, ta)
    kt = z7[1].reshape(n_chunks, ta, N_HEADS, HEAD_DIM).transpose(2, 0, 3, 1)
    dk, dv, dc, dq, dc_q = _attn_bwd_fused(z7, kt, dob, c_rep, lse_chunks, d_chunks, "attn_bwd")
    dc_pad = jnp.pad((dc.reshape(N_HEADS, T) + dc_q.reshape(N_HEADS, T)).T, ((0, 0), (0, LANES - N_HEADS)))
    dfb, dbf = _forget_bwd(dc_pad, f_logit, b_pad, "forget_bwd")
    grads_small["b_forget"] = dbf[:, :N_HEADS]
    segs = [(dq, None), (dk, None), (dv, None), (dz4, 0), (dz4, 1), (dz4, 2), (dz4, 3)]
    dx1, dgm = _mix_in_bwd(dx2, x1, vec("mix_pre_g"), segs, dfb, w7, wf, "mix_in_bwd")
    grads_small["mix_pre_g"] = jnp.sum(dgm, axis=0)
    seg_mats = [dq, dk, dv, dz4[0], dz4[1], dz4[2], dz4[3]]
    dw_seg = [_wgrad(h2b, sm, "w_in_wgrad_%d" % q) for q, sm in enumerate(seg_mats)]
    dwf = _wgrad(h2b, dfb, "w_in_wgrad_f")[:, :N_HEADS]
    dw_in = jnp.concatenate(dw_seg[:3] + [dwf] + dw_seg[3:], axis=1)
    dep = emit("w_in", dw_in.reshape(D, N_DEV, in_width // N_DEV).transpose(1, 0, 2))

    dx0, h1b, dy1b, act1, dgate1, dup1, dgpre1, dgpost1 = _ffn_bwd(
        dx1, x, y1, gate1, up1, vec("ffn1_pre_g"), big["ffn1_w_gate"], big["ffn1_w_up"], big["ffn1_w_down"],
        vec("ffn1_post_g"), "ffn1_bwd", dep=dep)
    grads_small["ffn1_pre_g"] = jnp.sum(dgpre1, axis=0)
    grads_small["ffn1_post_g"] = jnp.sum(dgpost1, axis=0)
    emit("ffn1_w_down", _wgrad(act1, dy1b, "ffn1_wgrad_down").reshape(big["ffn1_w_down"].shape))
    dep = emit("ffn1_w_gate", _wgrad(h1b, dgate1, "ffn1_wgrad_gate", shard_cols=True))
    emit("ffn1_w_up", _wgrad(h1b, dup1, "ffn1_wgrad_up", shard_cols=True, dep=dep))

    loss_row = jnp.pad(loss_lanes, ((0, 0), (0, D - LANES)))
    return loss_row, dx0, grads_small


def kernel(x, ffn1_pre_g, ffn1_w_gate, ffn1_w_up, ffn1_w_down, ffn1_post_g, mix_pre_g, w_in, b_forget, sgu_ln_g, sgu_ln_b, sgu_w_s, sgu_b_s, w_out, mix_post_g, ffn2_pre_g, ffn2_w_gate, ffn2_w_up, ffn2_w_down, ffn2_post_g, loss_target, m_ffn1_pre_g, m_ffn1_w_gate, m_ffn1_w_up, m_ffn1_w_down, m_ffn1_post_g, m_mix_pre_g, m_w_in, m_b_forget, m_sgu_ln_g, m_sgu_ln_b, m_sgu_w_s, m_sgu_b_s, m_w_out, m_mix_post_g, m_ffn2_pre_g, m_ffn2_w_gate, m_ffn2_w_up, m_ffn2_w_down, m_ffn2_post_g, v_ffn1_pre_g, v_ffn1_w_gate, v_ffn1_w_up, v_ffn1_w_down, v_ffn1_post_g, v_mix_pre_g, v_w_in, v_b_forget, v_sgu_ln_g, v_sgu_ln_b, v_sgu_w_s, v_sgu_b_s, v_w_out, v_mix_post_g, v_ffn2_pre_g, v_ffn2_w_gate, v_ffn2_w_up, v_ffn2_w_down, v_ffn2_post_g):
    weights = dict(zip(WEIGHT_NAMES, (ffn1_pre_g, ffn1_w_gate, ffn1_w_up, ffn1_w_down, ffn1_post_g, mix_pre_g, w_in,
                                      b_forget, sgu_ln_g, sgu_ln_b, sgu_w_s, sgu_b_s, w_out, mix_post_g, ffn2_pre_g,
                                      ffn2_w_gate, ffn2_w_up, ffn2_w_down, ffn2_post_g)))
    mom1 = dict(zip(WEIGHT_NAMES, (m_ffn1_pre_g, m_ffn1_w_gate, m_ffn1_w_up, m_ffn1_w_down, m_ffn1_post_g,
                                   m_mix_pre_g, m_w_in, m_b_forget, m_sgu_ln_g, m_sgu_ln_b, m_sgu_w_s, m_sgu_b_s,
                                   m_w_out, m_mix_post_g, m_ffn2_pre_g, m_ffn2_w_gate, m_ffn2_w_up, m_ffn2_w_down,
                                   m_ffn2_post_g)))
    mom2 = dict(zip(WEIGHT_NAMES, (v_ffn1_pre_g, v_ffn1_w_gate, v_ffn1_w_up, v_ffn1_w_down, v_ffn1_post_g,
                                   v_mix_pre_g, v_w_in, v_b_forget, v_sgu_ln_g, v_sgu_ln_b, v_sgu_w_s, v_sgu_b_s,
                                   v_w_out, v_mix_post_g, v_ffn2_pre_g, v_ffn2_w_gate, v_ffn2_w_up, v_ffn2_w_down,
                                   v_ffn2_post_g)))
    D = x.shape[-1]
    small_names = [n for n in WEIGHT_NAMES if n not in BIG_NAMES]

    small = {n: weights[n] for n in small_names}
    shard = lambda n: weights[n][0].astype(BF16)

    ffn1_full = _all_gather([shard(n) for n in WEIGHT_GROUPS["ffn1"]], "ffn1_all_gather")
    gathers = {grp: _remote_start([shard(n) for n in WEIGHT_GROUPS[grp]], ffn1_full[0], grp + "_gather_start",
                                  gather=True) for grp in ("mix", "ffn2")}

    def fetch(group, after):
        if group == "ffn1":
            return zip(WEIGHT_GROUPS[group], ffn1_full)
        return zip(WEIGHT_GROUPS[group], _remote_wait(gathers[group], after, group + "_gather_wait"))

    ready, flights = {}, []

    def emit(name, part):
        ready[name] = part
        for group in GRAD_GROUPS:
            if name == group[-1]:
                flights.append((group, _remote_start([ready[n] for n in group], part, name + "_grad_start",
                                                     gather=False)))
                return flights[-1][1]["token"]
        return None

    loss_row, grad_x, grads_small = _local_step(x[0], loss_target[0], small, fetch, emit)

    blob = _all_reduce_small(_pack_small(grads_small, D, loss_row) + flights[-1][1]["token"][:1, :1],
                             "small_all_reduce")

    out = {}
    after = blob
    for group, flight in flights:
        received = _remote_wait(flight, after, group[-1] + "_grad_wait")
        for n, rcv in zip(group, received):
            g, d, m_new, v_new = _sum_adamw(rcv, weights[n][0], mom1[n][0], mom2[n][0], "adamw_" + n)
            out[n] = tuple(a[None] for a in (g, d, m_new, v_new))
            after = g

    d_blob, m_blob, v_blob = _adamw(blob, _pack_small(small, D), _pack_small({n: mom1[n] for n in small_names}, D),
                                    _pack_small({n: mom2[n] for n in small_names}, D), "adamw_small")
    unpacked = [_unpack_small(b, D) for b in (blob, d_blob, m_blob, v_blob)]
    for n in small_names:
        out[n] = tuple(u[n].reshape(weights[n].shape) for u in unpacked)

    loss = blob[ROW_LOSS, 0]
    result = [loss, grad_x[None]]
    for k in range(4):
        result += [out[n][k] for n in WEIGHT_NAMES]
    return tuple(result)
```

```python
import functools

import numpy as np
import jax
import jax.numpy as jnp
from jax import lax
from jax.experimental import pallas as pl
from jax.experimental.pallas import tpu as pltpu

F32 = jnp.float32
BF16 = jnp.bfloat16

RMS_EPS = 1e-6
LN_EPS = 1e-5
HEAD_DIM = 128
N_HEADS = 8
GROUP_DIM = 128
N_GROUPS = 8
SGU_LEN = 128
CHUNK = 64
N_DEV = 8
LANES = 128
VMEM_LIMIT = 56 * 1024 * 1024
NEG_BIG = -1e30

ADAM_LR = 0.001
ADAM_B1 = 0.9
ADAM_B2 = 0.999
ADAM_EPS = 1e-08
ADAM_WD = 0.01
ADAM_STEP = 10

MESH = pl.DeviceIdType.MESH
ANY = pl.BlockSpec(memory_space=pl.ANY)


def _blk(n, pref):
    return pref if (n >= pref and n % pref == 0) else n


def _mm(a, b):
    return jnp.dot(a, b, preferred_element_type=F32)


def _mm_nt(a, b):
    return lax.dot_general(a, b, (((1,), (1,)), ((), ())), preferred_element_type=F32)


def _mm_tn(a, b):
    return lax.dot_general(a, b, (((0,), (0,)), ((), ())), preferred_element_type=F32)


def _params(sem):
    return pltpu.CompilerParams(dimension_semantics=sem, vmem_limit_bytes=VMEM_LIMIT)


def _gelu(x):
    return 0.5 * x * (1.0 + lax.erf(x * np.float32(1.0 / np.sqrt(2.0))))


def _gelu_grad(x):
    cdf = 0.5 * (1.0 + lax.erf(x * np.float32(1.0 / np.sqrt(2.0))))
    return cdf + x * jnp.exp(-0.5 * x * x) * np.float32(1.0 / np.sqrt(2.0 * np.pi))


def _rms_scale(v):
    return lax.rsqrt(jnp.mean(v * v, axis=-1, keepdims=True) + RMS_EPS)


def _rms_bwd(dy, xhat, r, g):
    dxh = dy * g
    return r * (dxh - xhat * jnp.mean(dxh * xhat, axis=-1, keepdims=True))


def _ffn_fwd(x, g_pre, wg, wu, wd, g_post, name):
    T, D = x.shape
    ns, _, fs = wg.shape
    tm = _blk(T, 512)
    halves = 2 if tm % 32 == 0 else 1
    th = tm // halves

    def body(x_ref, gpre_ref, wg_ref, wu_ref, wd_ref, gpost_ref, xo_ref, y_ref, dgf_ref, silu_ref, act_ref,
             h_scr, acc_scr):
        j = pl.program_id(1)

        @pl.when(j == 0)
        def _():
            xv = x_ref[...]
            h_scr[...] = (xv * _rms_scale(xv) * gpre_ref[...]).astype(BF16)
            acc_scr[...] = jnp.zeros_like(acc_scr)

        pre = []
        for r in range(halves):
            h = h_scr[r * th:(r + 1) * th, :]
            pre.append((_mm(h, wg_ref[...]), _mm(h, wu_ref[...])))
        for r in range(halves):
            rows = slice(r * th, (r + 1) * th)
            gg, uu = pre[r]
            sg = jax.nn.sigmoid(gg)
            silu = gg * sg
            act = (silu * uu).astype(BF16)
            dgf_ref[rows, :] = (uu * (sg * (1.0 + gg * (1.0 - sg)))).astype(BF16)
            silu_ref[rows, :] = silu.astype(BF16)
            act_ref[rows, :] = act
            acc_scr[rows, :] += _mm(act, wd_ref[...])

        @pl.when(j == ns - 1)
        def _():
            y = acc_scr[...]
            y_ref[...] = y
            xo_ref[...] = x_ref[...] + 0.5 * (y * _rms_scale(y) * gpost_ref[...])

    row = pl.BlockSpec((tm, D), lambda i, j: (i, 0))
    vec = pl.BlockSpec((1, D), lambda i, j: (0, 0))
    return pl.pallas_call(
        body, name=name, grid=(T // tm, ns),
        in_specs=[row, vec,
                  pl.BlockSpec((None, D, fs), lambda i, j: (j, 0, 0)),
                  pl.BlockSpec((None, D, fs), lambda i, j: (j, 0, 0)),
                  pl.BlockSpec((None, fs, D), lambda i, j: (j, 0, 0)),
                  vec],
        out_specs=[row, row] + [pl.BlockSpec((tm, fs), lambda i, j: (i, j))] * 3,
        out_shape=[jax.ShapeDtypeStruct((T, D), F32), jax.ShapeDtypeStruct((T, D), F32)]
        + [jax.ShapeDtypeStruct((T, ns * fs), BF16)] * 3,
        scratch_shapes=[pltpu.VMEM((tm, D), BF16), pltpu.VMEM((tm, D), F32)],
        compiler_params=_params(("parallel", "arbitrary")),
    )(x, g_pre, wg, wu, wd, g_post)


def _after(dep):
    return jnp.zeros((8, LANES), F32) if dep is None else dep


def _ffn_bwd(dxo, x, y, dgf, silu, g_pre, wg, wu, wd, g_post, name, dep=None):
    T, D = x.shape
    ns, _, fs = wg.shape
    tm = _blk(T, 512)
    n_i = T // tm
    halves = 2 if tm % 32 == 0 else 1
    th = tm // halves

    def body(dxo_ref, x_ref, y_ref, dgf_ref, silu_ref, gpre_ref, wg_ref, wu_ref, wd_ref, gpost_ref, _,
             dx_ref, hb_ref, dyb_ref, dgb_ref, dub_ref, dgpre_ref, dgpost_ref, dy_scr, acc_scr):
        j = pl.program_id(1)

        @pl.when(j == 0)
        def _():
            yv = y_ref[...]
            s = _rms_scale(yv)
            n = yv * s
            dn = 0.5 * dxo_ref[...]
            dgpost_ref[...] = jnp.sum(dn * n, axis=0, keepdims=True)
            dyv = _rms_bwd(dn, n, s, gpost_ref[...]).astype(BF16)
            dy_scr[...] = dyv
            dyb_ref[...] = dyv
            xv = x_ref[...]
            hb_ref[...] = (xv * _rms_scale(xv) * gpre_ref[...]).astype(BF16)
            acc_scr[...] = jnp.zeros_like(acc_scr)

        das = [_mm_nt(dy_scr[r * th:(r + 1) * th, :], wd_ref[...]) for r in range(halves)]
        for r in range(halves):
            rows = slice(r * th, (r + 1) * th)
            dgate = (das[r] * dgf_ref[rows, :].astype(F32)).astype(BF16)
            dup = (das[r] * silu_ref[rows, :].astype(F32)).astype(BF16)
            dgb_ref[rows, :] = dgate
            dub_ref[rows, :] = dup
            acc_scr[rows, :] += _mm_nt(dgate, wg_ref[...]) + _mm_nt(dup, wu_ref[...])

        @pl.when(j == ns - 1)
        def _():
            xv = x_ref[...]
            r = _rms_scale(xv)
            xhat = xv * r
            dh = acc_scr[...]
            dgpre_ref[...] = jnp.sum(dh * xhat, axis=0, keepdims=True)
            dx_ref[...] = _rms_bwd(dh, xhat, r, gpre_ref[...]) + dxo_ref[...]

    row = pl.BlockSpec((tm, D), lambda i, j: (i, 0))
    vec = pl.BlockSpec((1, D), lambda i, j: (0, 0))
    wide = pl.BlockSpec((tm, fs), lambda i, j: (i, j))
    part = pl.BlockSpec((None, 1, D), lambda i, j: (i, 0, 0))
    F = ns * fs
    return pl.pallas_call(
        body, name=name, grid=(n_i, ns),
        in_specs=[row, row, row, wide, wide, vec,
                  pl.BlockSpec((None, D, fs), lambda i, j: (j, 0, 0)),
                  pl.BlockSpec((None, D, fs), lambda i, j: (j, 0, 0)),
                  pl.BlockSpec((None, fs, D), lambda i, j: (j, 0, 0)),
                  vec, ANY],
        out_specs=[row, row, row, wide, wide, part, part],
        out_shape=[jax.ShapeDtypeStruct((T, D), F32), jax.ShapeDtypeStruct((T, D), BF16),
                   jax.ShapeDtypeStruct((T, D), BF16), jax.ShapeDtypeStruct((T, F), BF16),
                   jax.ShapeDtypeStruct((T, F), BF16),
                   jax.ShapeDtypeStruct((n_i, 1, D), F32), jax.ShapeDtypeStruct((n_i, 1, D), F32)],
        scratch_shapes=[pltpu.VMEM((tm, D), BF16), pltpu.VMEM((tm, D), F32)],
        compiler_params=_params(("parallel", "arbitrary")),
    )(dxo, x, y, dgf, silu, g_pre, wg, wu, wd, g_post, _after(dep))


def _wgrad(xm, ym, name, shard_cols=False, dep=None):
    T, M = xm.shape
    _, N = ym.shape
    bm = _blk(M, 1024)
    bn = N // N_DEV if shard_cols else _blk(N, 512)
    tk = _blk(T, 1024)
    n_k = T // tk

    def body(x_ref, y_ref, _, o_ref, acc_scr):
        k = pl.program_id(2)

        @pl.when(k == 0)
        def _():
            acc_scr[...] = jnp.zeros_like(acc_scr)

        acc_scr[...] += _mm_tn(x_ref[...], y_ref[...])

        @pl.when(k == n_k - 1)
        def _():
            o_ref[...] = acc_scr[...].astype(BF16)

    if shard_cols:
        out_spec = pl.BlockSpec((None, bm, bn), lambda i, j, k: (j, i, 0))
        out_shape = jax.ShapeDtypeStruct((N // bn, M, bn), BF16)
    else:
        out_spec = pl.BlockSpec((bm, bn), lambda i, j, k: (i, j))
        out_shape = jax.ShapeDtypeStruct((M, N), BF16)
    return pl.pallas_call(
        body, name=name, grid=(M // bm, N // bn, n_k),
        in_specs=[pl.BlockSpec((tk, bm), lambda i, j, k: (k, i)),
                  pl.BlockSpec((tk, bn), lambda i, j, k: (k, j)), ANY],
        out_specs=out_spec, out_shape=out_shape,
        scratch_shapes=[pltpu.VMEM((bm, bn), F32)],
        compiler_params=_params(("parallel", "parallel", "arbitrary")),
    )(xm, ym, _after(dep))


def _mix_in_fwd(x1, g, w7, wf, name):
    T, D = x1.shape
    n_seg, _, W = w7.shape
    tm = _blk(T, 1024)

    def body(x_ref, g_ref, w_ref, wf_ref, z_ref, f_ref, hb_ref, h_scr):
        s = pl.program_id(1)

        @pl.when(s == 0)
        def _():
            xv = x_ref[...]
            h = (xv * _rms_scale(xv) * g_ref[...]).astype(BF16)
            h_scr[...] = h
            hb_ref[...] = h
            f_ref[...] = _mm(h, wf_ref[...])

        z_ref[...] = _mm(h_scr[...], w_ref[...]).astype(BF16)

    return pl.pallas_call(
        body, name=name, grid=(T // tm, n_seg),
        in_specs=[pl.BlockSpec((tm, D), lambda i, s: (i, 0)),
                  pl.BlockSpec((1, D), lambda i, s: (0, 0)),
                  pl.BlockSpec((None, D, W), lambda i, s: (s, 0, 0)),
                  pl.BlockSpec((D, LANES), lambda i, s: (0, 0))],
        out_specs=[pl.BlockSpec((None, tm, W), lambda i, s: (s, i, 0)),
                   pl.BlockSpec((tm, LANES), lambda i, s: (i, 0)),
                   pl.BlockSpec((tm, D), lambda i, s: (i, 0))],
        out_shape=[jax.ShapeDtypeStruct((n_seg, T, W), BF16), jax.ShapeDtypeStruct((T, LANES), F32),
                   jax.ShapeDtypeStruct((T, D), BF16)],
        scratch_shapes=[pltpu.VMEM((tm, D), BF16)],
        compiler_params=_params(("parallel", "arbitrary")),
    )(x1, g, w7, wf)


def _mix_in_bwd(dx2, x1, g, segs, dfb, w7, wf, name):
    T, D = x1.shape
    n_seg, _, W = w7.shape
    tm = _blk(T, 512)
    n_i = T // tm

    def body(*refs):
        dx2_ref, x_ref, g_ref = refs[:3]
        seg_refs = refs[3:3 + n_seg]
        df_ref, w_ref, wf_ref, dx1_ref, dg_ref, acc_scr = refs[3 + n_seg:]
        s = pl.program_id(1)

        @pl.when(s == 0)
        def _():
            acc_scr[...] = _mm_nt(df_ref[...], wf_ref[...])

        for q in range(n_seg):
            @pl.when(s == q)
            def _(q=q):
                acc_scr[...] += _mm_nt(seg_refs[q][...], w_ref[...])

        @pl.when(s == n_seg - 1)
        def _():
            xv = x_ref[...]
            r = _rms_scale(xv)
            xhat = xv * r
            dh = acc_scr[...]
            dg_ref[...] = jnp.sum(dh * xhat, axis=0, keepdims=True)
            dx1_ref[...] = _rms_bwd(dh, xhat, r, g_ref[...]) + dx2_ref[...]

    row = pl.BlockSpec((tm, D), lambda i, s: (i, 0))
    seg_specs = []
    seg_args = []
    for arr, idx in segs:
        if idx is None:
            seg_specs.append(pl.BlockSpec((tm, W), lambda i, s: (i, 0)))
        else:
            seg_specs.append(pl.BlockSpec((None, tm, W), lambda i, s, idx=idx: (idx, i, 0)))
        seg_args.append(arr)
    return pl.pallas_call(
        body, name=name, grid=(n_i, n_seg),
        in_specs=[row, row, pl.BlockSpec((1, D), lambda i, s: (0, 0))] + seg_specs + [
            pl.BlockSpec((tm, LANES), lambda i, s: (i, 0)),
            pl.BlockSpec((None, D, W), lambda i, s: (s, 0, 0)),
            pl.BlockSpec((D, LANES), lambda i, s: (0, 0))],
        out_specs=[row, pl.BlockSpec((None, 1, D), lambda i, s: (i, 0, 0))],
        out_shape=[jax.ShapeDtypeStruct((T, D), F32), jax.ShapeDtypeStruct((n_i, 1, D), F32)],
        scratch_shapes=[pltpu.VMEM((tm, D), F32)],
        compiler_params=_params(("parallel", "arbitrary")),
    )(dx2, x1, g, *seg_args, dfb, w7, wf)


def _forget_cumsum(f, b_pad, name):
    T, L = f.shape
    tb = _blk(T, 256)

    def body(f_ref, b_ref, c_ref, carry):
        @pl.when(pl.program_id(0) == 0)
        def _():
            carry[...] = jnp.zeros_like(carry)

        lf = jax.nn.log_sigmoid(f_ref[...] + b_ref[...])
        rows = lax.broadcasted_iota(jnp.int32, (tb, tb), 0)
        cols = lax.broadcasted_iota(jnp.int32, (tb, tb), 1)
        tri = (cols <= rows).astype(F32)
        c = jnp.dot(tri, lf, preferred_element_type=F32, precision=lax.Precision.HIGHEST) + carry[...]
        c_ref[...] = c
        carry[...] = c[tb - 1:tb, :]

    return pl.pallas_call(
        body, name=name, grid=(T // tb,),
        in_specs=[pl.BlockSpec((tb, L), lambda i: (i, 0)), pl.BlockSpec((1, L), lambda i: (0, 0))],
        out_specs=pl.BlockSpec((tb, L), lambda i: (i, 0)),
        out_shape=jax.ShapeDtypeStruct((T, L), F32),
        scratch_shapes=[pltpu.VMEM((1, L), F32)],
        compiler_params=_params(("arbitrary",)),
    )(f, b_pad)


def _forget_bwd(dc, f, b_pad, name):
    T, L = f.shape
    tb = _blk(T, 256)
    nb = T // tb

    def body(dc_ref, f_ref, b_ref, df_ref, db_ref, carry):
        @pl.when(pl.program_id(0) == 0)
        def _():
            carry[...] = jnp.zeros_like(carry)
            db_ref[...] = jnp.zeros_like(db_ref)

        rows = lax.broadcasted_iota(jnp.int32, (tb, tb), 0)
        cols = lax.broadcasted_iota(jnp.int32, (tb, tb), 1)
        tri = (cols >= rows).astype(F32)
        r = jnp.dot(tri, dc_ref[...], preferred_element_type=F32, precision=lax.Precision.HIGHEST) + carry[...]
        carry[...] = r[0:1, :]
        df = r * (1.0 - jax.nn.sigmoid(f_ref[...] + b_ref[...]))
        df_ref[...] = df.astype(BF16)
        db_ref[...] += jnp.sum(df, axis=0, keepdims=True)

    rev = pl.BlockSpec((tb, L), lambda i: (nb - 1 - i, 0))
    one = pl.BlockSpec((1, L), lambda i: (0, 0))
    return pl.pallas_call(
        body, name=name, grid=(nb,),
        in_specs=[rev, rev, one], out_specs=[rev, one],
        out_shape=[jax.ShapeDtypeStruct((T, L), BF16), jax.ShapeDtypeStruct((1, L), F32)],
        scratch_shapes=[pltpu.VMEM((1, L), F32)],
        compiler_params=_params(("arbitrary",)),
    )(dc, f, b_pad)


def _attn_fwd(z7, c_row, name):
    _, T, W = z7.shape
    H = W // HEAD_DIM
    ta = _blk(T, 512)
    nq = T // ta
    scale = np.float32(1.0 / np.sqrt(HEAD_DIM))

    def body(q_ref, k_ref, v_ref, crow_ref, o_ref, lse_ref, m_scr, l_scr, acc_scr):
        i = pl.program_id(1)
        j = pl.program_id(2)

        @pl.when(j == 0)
        def _():
            m_scr[...] = jnp.full_like(m_scr, NEG_BIG)
            l_scr[...] = jnp.zeros_like(l_scr)
            acc_scr[...] = jnp.zeros_like(acc_scr)

        def step(diagonal):
            s = _mm_nt(q_ref[...], k_ref[...]) * scale - crow_ref[...]
            if diagonal:
                rows = lax.broadcasted_iota(jnp.int32, (ta, ta), 0)
                cols = lax.broadcasted_iota(jnp.int32, (ta, ta), 1)
                s = jnp.where(cols <= rows, s, NEG_BIG)
            m_prev = m_scr[...]
            m_new = jnp.maximum(m_prev, jnp.max(s, axis=-1, keepdims=True))
            alpha = jnp.exp(m_prev - m_new)
            p = jnp.exp(s - m_new)
            l_scr[...] = alpha * l_scr[...] + jnp.sum(p, axis=-1, keepdims=True)
            acc_scr[...] = alpha * acc_scr[...] + _mm(p.astype(BF16), v_ref[...])
            m_scr[...] = m_new

        @pl.when(j < i)
        def _():
            step(False)

        @pl.when(j == i)
        def _():
            step(True)
            l = l_scr[...]
            o_ref[...] = acc_scr[...] / l
            lse_ref[...] = m_scr[...] + jnp.log(l)

    return pl.pallas_call(
        body, name=name, grid=(H, nq, nq),
        in_specs=[pl.BlockSpec((None, ta, HEAD_DIM), lambda h, i, j: (0, i, h)),
                  pl.BlockSpec((None, ta, HEAD_DIM), lambda h, i, j: (1, jnp.minimum(i, j), h)),
                  pl.BlockSpec((None, ta, HEAD_DIM), lambda h, i, j: (2, jnp.minimum(i, j), h)),
                  pl.BlockSpec((None, 1, ta), lambda h, i, j: (h, 0, jnp.minimum(i, j)))],
        out_specs=[pl.BlockSpec((ta, HEAD_DIM), lambda h, i, j: (i, h)),
                   pl.BlockSpec((None, ta, 1), lambda h, i, j: (h, i, 0))],
        out_shape=[jax.ShapeDtypeStruct((T, W), F32), jax.ShapeDtypeStruct((H, T, 1), F32)],
        scratch_shapes=[pltpu.VMEM((ta, 1), F32), pltpu.VMEM((ta, 1), F32), pltpu.VMEM((ta, HEAD_DIM), F32)],
        compiler_params=_params(("parallel", "parallel", "arbitrary")),
    )(z7, z7, z7, c_row)


def _attn_bwd_kv(z7, dob, c_col, lse_row, d_row, name):
    _, T, W = z7.shape
    H = W // HEAD_DIM
    ta = _blk(T, 512)
    nq = T // ta
    scale = np.float32(1.0 / np.sqrt(HEAD_DIM))

    def body(k_ref, v_ref, q_ref, do_ref, ccol_ref, lse_ref, d_ref, dk_ref, dv_ref, dc_ref, dk_scr, dv_scr, dc_scr):
        j = pl.program_id(1)
        i = pl.program_id(2)

        @pl.when(i == 0)
        def _():
            dk_scr[...] = jnp.zeros_like(dk_scr)
            dv_scr[...] = jnp.zeros_like(dv_scr)
            dc_scr[...] = jnp.zeros_like(dc_scr)

        def step(diagonal):
            q = q_ref[...]
            do = do_ref[...]
            st = _mm_nt(k_ref[...], q) * scale - ccol_ref[...] - lse_ref[...]
            if diagonal:
                rows = lax.broadcasted_iota(jnp.int32, (ta, ta), 0)
                cols = lax.broadcasted_iota(jnp.int32, (ta, ta), 1)
                st = jnp.where(rows <= cols, st, NEG_BIG)
            pt = jnp.exp(st)
            dv_scr[...] += _mm(pt.astype(BF16), do)
            dst = pt * (_mm_nt(v_ref[...], do) - d_ref[...])
            dk_scr[...] += _mm(dst.astype(BF16), q)
            dc_scr[...] += jnp.sum(dst, axis=-1, keepdims=True)

        @pl.when(i > j)
        def _():
            step(False)

        @pl.when(i == j)
        def _():
            step(True)

        @pl.when(i == nq - 1)
        def _():
            dk_ref[...] = (dk_scr[...] * scale).astype(BF16)
            dv_ref[...] = dv_scr[...].astype(BF16)
            dc_ref[...] = -dc_scr[...]

    return pl.pallas_call(
        body, name=name, grid=(H, nq, nq),
        in_specs=[pl.BlockSpec((None, ta, HEAD_DIM), lambda h, j, i: (1, j, h)),
                  pl.BlockSpec((None, ta, HEAD_DIM), lambda h, j, i: (2, j, h)),
                  pl.BlockSpec((None, ta, HEAD_DIM), lambda h, j, i: (0, jnp.maximum(i, j), h)),
                  pl.BlockSpec((ta, HEAD_DIM), lambda h, j, i: (jnp.maximum(i, j), h)),
                  pl.BlockSpec((None, ta, 1), lambda h, j, i: (h, j, 0)),
                  pl.BlockSpec((None, 1, ta), lambda h, j, i: (h, 0, jnp.maximum(i, j))),
                  pl.BlockSpec((None, 1, ta), lambda h, j, i: (h, 0, jnp.maximum(i, j)))],
        out_specs=[pl.BlockSpec((ta, HEAD_DIM), lambda h, j, i: (j, h)),
                   pl.BlockSpec((ta, HEAD_DIM), lambda h, j, i: (j, h)),
                   pl.BlockSpec((None, ta, 1), lambda h, j, i: (h, j, 0))],
        out_shape=[jax.ShapeDtypeStruct((T, W), BF16), jax.ShapeDtypeStruct((T, W), BF16),
                   jax.ShapeDtypeStruct((H, T, 1), F32)],
        scratch_shapes=[pltpu.VMEM((ta, HEAD_DIM), F32), pltpu.VMEM((ta, HEAD_DIM), F32), pltpu.VMEM((ta, 1), F32)],
        compiler_params=_params(("parallel", "parallel", "arbitrary")),
    )(z7, z7, z7, dob, c_col, lse_row, d_row)


def _attn_bwd_q(z7, dob, c_row, lse_col, d_col, name):
    _, T, W = z7.shape
    H = W // HEAD_DIM
    ta = _blk(T, 512)
    nq = T // ta
    scale = np.float32(1.0 / np.sqrt(HEAD_DIM))

    def body(q_ref, k_ref, v_ref, do_ref, crow_ref, lse_ref, d_ref, dq_ref, dc_ref, dq_scr, dc_scr):
        i = pl.program_id(1)
        j = pl.program_id(2)

        @pl.when(j == 0)
        def _():
            dq_scr[...] = jnp.zeros_like(dq_scr)
            dc_scr[...] = jnp.zeros_like(dc_scr)

        def step(diagonal):
            k = k_ref[...]
            do = do_ref[...]
            s = _mm_nt(q_ref[...], k) * scale - crow_ref[...] - lse_ref[...]
            if diagonal:
                rows = lax.broadcasted_iota(jnp.int32, (ta, ta), 0)
                cols = lax.broadcasted_iota(jnp.int32, (ta, ta), 1)
                s = jnp.where(cols <= rows, s, NEG_BIG)
            p = jnp.exp(s)
            ds = p * (_mm_nt(do, v_ref[...]) - d_ref[...])
            dq_scr[...] += _mm(ds.astype(BF16), k)
            dc_scr[...] += jnp.sum(ds, axis=-1, keepdims=True)

        @pl.when(j < i)
        def _():
            step(False)

        @pl.when(j == i)
        def _():
            step(True)
            dq_ref[...] = (dq_scr[...] * scale).astype(BF16)
            dc_ref[...] = dc_scr[...]

    return pl.pallas_call(
        body, name=name, grid=(H, nq, nq),
        in_specs=[pl.BlockSpec((None, ta, HEAD_DIM), lambda h, i, j: (0, i, h)),
                  pl.BlockSpec((None, ta, HEAD_DIM), lambda h, i, j: (1, jnp.minimum(i, j), h)),
                  pl.BlockSpec((None, ta, HEAD_DIM), lambda h, i, j: (2, jnp.minimum(i, j), h)),
                  pl.BlockSpec((ta, HEAD_DIM), lambda h, i, j: (i, h)),
                  pl.BlockSpec((None, 1, ta), lambda h, i, j: (h, 0, jnp.minimum(i, j))),
                  pl.BlockSpec((None, ta, 1), lambda h, i, j: (h, i, 0)),
                  pl.BlockSpec((None, ta, 1), lambda h, i, j: (h, i, 0))],
        out_specs=[pl.BlockSpec((ta, HEAD_DIM), lambda h, i, j: (i, h)),
                   pl.BlockSpec((None, ta, 1), lambda h, i, j: (h, i, 0))],
        out_shape=[jax.ShapeDtypeStruct((T, W), BF16), jax.ShapeDtypeStruct((H, T, 1), F32)],
        scratch_shapes=[pltpu.VMEM((ta, HEAD_DIM), F32), pltpu.VMEM((ta, 1), F32)],
        compiler_params=_params(("parallel", "parallel", "arbitrary")),
    )(z7, z7, z7, dob, c_row, lse_col, d_col)


ATTN_TILE = 512
ATTN_CHAINS = 2


def _attn_geometry(T):
    ta = _blk(T, ATTN_TILE)
    nc = ATTN_CHAINS if (T // ta) % ATTN_CHAINS == 0 else 1
    return ta, nc, T // ta


def _causal_tile(ta, keys_on_rows=False):
    rows = lax.broadcasted_iota(jnp.int32, (ta, ta), 0)
    cols = lax.broadcasted_iota(jnp.int32, (ta, ta), 1)
    return rows <= cols if keys_on_rows else cols <= rows


def _chunk(ref, j, ta):
    return ref[pl.ds(pl.multiple_of(j * ta, ta), ta), :]


def _attn_fwd_loop(z7, c_chunks, name):
    _, T, W = z7.shape
    H = W // HEAD_DIM
    ta, nc, n_chunks = _attn_geometry(T)
    scale = np.float32(1.0 / np.sqrt(HEAD_DIM))

    def body(q_ref, k_ref, v_ref, c_ref, o_ref, lse_ref, m_scr, l_scr, acc_scr):
        g = pl.program_id(1)
        m_scr[...] = jnp.full_like(m_scr, NEG_BIG)
        l_scr[...] = jnp.zeros_like(l_scr)
        acc_scr[...] = jnp.zeros_like(acc_scr)

        def update(ch, k, v, crow, diagonal):
            q = q_ref[ch * ta:(ch + 1) * ta, :]
            s = _mm_nt(q, k) * scale - crow
            if diagonal:
                s = jnp.where(_causal_tile(ta), s, NEG_BIG)
            m_prev = m_scr[ch]
            m_new = jnp.maximum(m_prev, jnp.max(s, axis=-1, keepdims=True))
            alpha = jnp.exp(m_prev - m_new)
            p = jnp.exp(s - m_new)
            l_scr[ch] = alpha * l_scr[ch] + jnp.sum(p, axis=-1, keepdims=True)
            acc_scr[ch] = alpha * acc_scr[ch] + _mm(p.astype(BF16), v)
            m_scr[ch] = m_new

        def full_chunk(j, carry):
            k = _chunk(k_ref, j, ta)
            v = _chunk(v_ref, j, ta)
            crow = c_ref[j]
            for ch in range(nc):
                update(ch, k, v, crow, False)
            return carry

        lax.fori_loop(0, nc * g, full_chunk, 0)
        for jj in range(nc):
            j = nc * g + jj
            k = _chunk(k_ref, j, ta)
            v = _chunk(v_ref, j, ta)
            crow = c_ref[j]
            for ch in range(jj, nc):
                update(ch, k, v, crow, ch == jj)
        for ch in range(nc):
            l = l_scr[ch]
            o_ref[ch * ta:(ch + 1) * ta, :] = acc_scr[ch] / l
            lse_ref[ch * ta:(ch + 1) * ta, :] = m_scr[ch] + jnp.log(l)

    tq = nc * ta
    return pl.pallas_call(
        body, name=name, grid=(H, n_chunks // nc),
        in_specs=[pl.BlockSpec((None, tq, HEAD_DIM), lambda h, g: (0, g, h)),
                  pl.BlockSpec((None, T, HEAD_DIM), lambda h, g: (1, 0, h)),
                  pl.BlockSpec((None, T, HEAD_DIM), lambda h, g: (2, 0, h)),
                  pl.BlockSpec((None, n_chunks, 1, ta), lambda h, g: (h, 0, 0, 0))],
        out_specs=[pl.BlockSpec((tq, HEAD_DIM), lambda h, g: (g, h)),
                   pl.BlockSpec((None, tq, 1), lambda h, g: (h, g, 0))],
        out_shape=[jax.ShapeDtypeStruct((T, W), F32), jax.ShapeDtypeStruct((H, T, 1), F32)],
        scratch_shapes=[pltpu.VMEM((nc, ta, 1), F32), pltpu.VMEM((nc, ta, 1), F32),
                        pltpu.VMEM((nc, ta, HEAD_DIM), F32)],
        compiler_params=_params(("parallel", "arbitrary")),
    )(z7, z7, z7, c_chunks)


def _attn_fwd_keys_on_rows(z7, vt, c_rep, name):
    _, T, W = z7.shape
    H = W // HEAD_DIM
    ta, nc, n_chunks = _attn_geometry(T)
    scale = np.float32(1.0 / np.sqrt(HEAD_DIM))
    reps = ta // LANES

    def body(q_ref, k_ref, vt_ref, c_ref, o_ref, lse_ref, m_scr, l_scr, acc_scr):
        g = pl.program_id(1)
        m_scr[...] = jnp.full_like(m_scr, NEG_BIG)
        l_scr[...] = jnp.zeros_like(l_scr)
        acc_scr[...] = jnp.zeros_like(acc_scr)

        def update(ch, k, vt, cj, diagonal):
            q = q_ref[ch * ta:(ch + 1) * ta, :]
            st = _mm_nt(k, q) * scale - cj
            if diagonal:
                st = jnp.where(_causal_tile(ta, keys_on_rows=True), st, NEG_BIG)
            m_prev = m_scr[ch]
            m_new = jnp.maximum(m_prev, jnp.max(st, axis=0, keepdims=True))
            alpha = jnp.exp(m_prev - m_new)
            pt = jnp.exp(st - m_new)
            l_scr[ch] = alpha * l_scr[ch] + jnp.sum(pt, axis=0, keepdims=True)
            acc_scr[ch] = alpha * acc_scr[ch] + _mm(vt, pt.astype(BF16))
            m_scr[ch] = m_new

        def load(j):
            cj = _chunk(c_ref, j, ta)
            return _chunk(k_ref, j, ta), vt_ref[j], jnp.concatenate([cj] * reps, axis=1)

        def full_chunk(j, carry):
            k, vt, cj = load(j)
            for ch in range(nc):
                update(ch, k, vt, cj, False)
            return carry

        lax.fori_loop(0, nc * g, full_chunk, 0)
        for jj in range(nc):
            k, vt, cj = load(nc * g + jj)
            for ch in range(jj, nc):
                update(ch, k, vt, cj, ch == jj)
        for ch in range(nc):
            l = l_scr[ch]
            o_ref[ch * ta:(ch + 1) * ta, :] = (acc_scr[ch] / l).T
            lse_ref[ch] = m_scr[ch] + jnp.log(l)

    tq = nc * ta
    return pl.pallas_call(
        body, name=name, grid=(H, n_chunks // nc),
        in_specs=[pl.BlockSpec((None, tq, HEAD_DIM), lambda h, g: (0, g, h)),
                  pl.BlockSpec((None, T, HEAD_DIM), lambda h, g: (1, 0, h)),
                  pl.BlockSpec((None, n_chunks, HEAD_DIM, ta), lambda h, g: (h, 0, 0, 0)),
                  pl.BlockSpec((None, T, LANES), lambda h, g: (h, 0, 0))],
        out_specs=[pl.BlockSpec((tq, HEAD_DIM), lambda h, g: (g, h)),
                   pl.BlockSpec((None, nc, 1, ta), lambda h, g: (h, g, 0, 0))],
        out_shape=[jax.ShapeDtypeStruct((T, W), F32), jax.ShapeDtypeStruct((H, n_chunks, 1, ta), F32)],
        scratch_shapes=[pltpu.VMEM((nc, 1, ta), F32), pltpu.VMEM((nc, 1, ta), F32),
                        pltpu.VMEM((nc, HEAD_DIM, ta), F32)],
        compiler_params=_params(("parallel", "arbitrary")),
    )(z7, z7, vt, c_rep)


def _attn_bwd_fused(z7, kt, dob, c_rep, lse_chunks, d_chunks, name):
    _, T, W = z7.shape
    H = W // HEAD_DIM
    ta, nc, n_chunks = _attn_geometry(T)
    n_steps = n_chunks // nc
    scale = np.float32(1.0 / np.sqrt(HEAD_DIM))
    reps = ta // LANES

    def body(k_ref, v_ref, kt_ref, q_ref, do_ref, c_ref, lse_ref, d_ref,
             dk_ref, dv_ref, dck_ref, dq_ref, dcq_ref, dk_scr, dv_scr, dck_scr, dqt_scr, dcq_scr):
        g = pl.program_id(1)

        @pl.when(g == 0)
        def _():
            dqt_scr[...] = jnp.zeros_like(dqt_scr)
            dcq_scr[...] = jnp.zeros_like(dcq_scr)

        dk_scr[...] = jnp.zeros_like(dk_scr)
        dv_scr[...] = jnp.zeros_like(dv_scr)
        dck_scr[...] = jnp.zeros_like(dck_scr)

        def update(ch, i, q, do, diagonal):
            rows = slice(ch * ta, (ch + 1) * ta)
            cj = c_ref[rows, :]
            st = _mm_nt(k_ref[rows, :], q) * scale - jnp.concatenate([cj] * reps, axis=1) - lse_ref[i]
            if diagonal:
                st = jnp.where(_causal_tile(ta, keys_on_rows=True), st, NEG_BIG)
            pt = jnp.exp(st)
            dv_scr[ch] += _mm(pt.astype(BF16), do)
            dst = pt * (_mm_nt(v_ref[rows, :], do) - d_ref[i])
            dst_b = dst.astype(BF16)
            dk_scr[ch] += _mm(dst_b, q)
            dqt_scr[i] += _mm(kt_ref[ch], dst_b)
            dcq_scr[i] += jnp.sum(dst, axis=0, keepdims=True)
            lane_sum = dst[:, :LANES]
            for r in range(1, reps):
                lane_sum = lane_sum + dst[:, r * LANES:(r + 1) * LANES]
            dck_scr[ch] += lane_sum

        for ii in range(nc):
            i = nc * g + ii
            q = _chunk(q_ref, i, ta)
            do = _chunk(do_ref, i, ta)
            for ch in range(0, ii + 1):
                update(ch, i, q, do, ch == ii)

        def full_chunk(i, carry):
            q = _chunk(q_ref, i, ta)
            do = _chunk(do_ref, i, ta)
            for ch in range(nc):
                update(ch, i, q, do, False)
            return carry

        lax.fori_loop(nc * (g + 1), n_chunks, full_chunk, 0)
        for ch in range(nc):
            rows = slice(ch * ta, (ch + 1) * ta)
            dk_ref[rows, :] = (dk_scr[ch] * scale).astype(BF16)
            dv_ref[rows, :] = dv_scr[ch].astype(BF16)
            dck_ref[rows, :] = -jnp.sum(dck_scr[ch], axis=-1, keepdims=True)

        @pl.when(g == n_steps - 1)
        def _():
            for i in range(n_chunks):
                dq_ref[i * ta:(i + 1) * ta, :] = (dqt_scr[i] * scale).T.astype(BF16)
            dcq_ref[...] = dcq_scr[...]

    tk = nc * ta
    chunks = pl.BlockSpec((None, n_chunks, 1, ta), lambda h, g: (h, 0, 0, 0))
    tile = pl.BlockSpec((tk, HEAD_DIM), lambda h, g: (g, h))
    return pl.pallas_call(
        body, name=name, grid=(H, n_steps),
        in_specs=[pl.BlockSpec((None, tk, HEAD_DIM), lambda h, g: (1, g, h)),
                  pl.BlockSpec((None, tk, HEAD_DIM), lambda h, g: (2, g, h)),
                  pl.BlockSpec((None, nc, HEAD_DIM, ta), lambda h, g: (h, g, 0, 0)),
                  pl.BlockSpec((None, T, HEAD_DIM), lambda h, g: (0, 0, h)),
                  pl.BlockSpec((T, HEAD_DIM), lambda h, g: (0, h)),
                  pl.BlockSpec((None, tk, LANES), lambda h, g: (h, g, 0)),
                  chunks, chunks],
        out_specs=[tile, tile, pl.BlockSpec((None, tk, 1), lambda h, g: (h, g, 0)),
                   pl.BlockSpec((T, HEAD_DIM), lambda h, g: (0, h)), chunks],
        out_shape=[jax.ShapeDtypeStruct((T, W), BF16), jax.ShapeDtypeStruct((T, W), BF16),
                   jax.ShapeDtypeStruct((H, T, 1), F32), jax.ShapeDtypeStruct((T, W), BF16),
                   jax.ShapeDtypeStruct((H, n_chunks, 1, ta), F32)],
        scratch_shapes=[pltpu.VMEM((nc, ta, HEAD_DIM), F32), pltpu.VMEM((nc, ta, HEAD_DIM), F32),
                        pltpu.VMEM((nc, ta, LANES), F32), pltpu.VMEM((n_chunks, HEAD_DIM, ta), F32),
                        pltpu.VMEM((n_chunks, 1, ta), F32)],
        compiler_params=_params(("parallel", "arbitrary")),
    )(z7, z7, kt, z7, dob, c_rep, lse_chunks, d_chunks)


def _attn_bwd_q_loop(z7, dob, c_chunks, lse_col, d_col, name):
    _, T, W = z7.shape
    H = W // HEAD_DIM
    ta, nc, n_chunks = _attn_geometry(T)
    scale = np.float32(1.0 / np.sqrt(HEAD_DIM))

    def body(q_ref, k_ref, v_ref, do_ref, c_ref, lse_ref, d_ref, dq_ref, dc_ref, dq_scr, dc_scr):
        g = pl.program_id(1)
        dq_scr[...] = jnp.zeros_like(dq_scr)
        dc_scr[...] = jnp.zeros_like(dc_scr)

        def update(ch, k, v, crow, diagonal):
            rows = slice(ch * ta, (ch + 1) * ta)
            do = do_ref[rows, :]
            s = _mm_nt(q_ref[rows, :], k) * scale - crow - lse_ref[rows, :]
            if diagonal:
                s = jnp.where(_causal_tile(ta), s, NEG_BIG)
            p = jnp.exp(s)
            ds = p * (_mm_nt(do, v) - d_ref[rows, :])
            dq_scr[ch] += _mm(ds.astype(BF16), k)
            dc_scr[ch] += jnp.sum(ds, axis=-1, keepdims=True)

        def full_chunk(j, carry):
            k = _chunk(k_ref, j, ta)
            v = _chunk(v_ref, j, ta)
            crow = c_ref[j]
            for ch in range(nc):
                update(ch, k, v, crow, False)
            return carry

        lax.fori_loop(0, nc * g, full_chunk, 0)
        for jj in range(nc):
            j = nc * g + jj
            k = _chunk(k_ref, j, ta)
            v = _chunk(v_ref, j, ta)
            crow = c_ref[j]
            for ch in range(jj, nc):
                update(ch, k, v, crow, ch == jj)
        for ch in range(nc):
            dq_ref[ch * ta:(ch + 1) * ta, :] = (dq_scr[ch] * scale).astype(BF16)
            dc_ref[ch * ta:(ch + 1) * ta, :] = dc_scr[ch]

    tq = nc * ta
    col = pl.BlockSpec((None, tq, 1), lambda h, g: (h, g, 0))
    return pl.pallas_call(
        body, name=name, grid=(H, n_chunks // nc),
        in_specs=[pl.BlockSpec((None, tq, HEAD_DIM), lambda h, g: (0, g, h)),
                  pl.BlockSpec((None, T, HEAD_DIM), lambda h, g: (1, 0, h)),
                  pl.BlockSpec((None, T, HEAD_DIM), lambda h, g: (2, 0, h)),
                  pl.BlockSpec((tq, HEAD_DIM), lambda h, g: (g, h)),
                  pl.BlockSpec((None, n_chunks, 1, ta), lambda h, g: (h, 0, 0, 0)),
                  col, col],
        out_specs=[pl.BlockSpec((tq, HEAD_DIM), lambda h, g: (g, h)), col],
        out_shape=[jax.ShapeDtypeStruct((T, W), BF16), jax.ShapeDtypeStruct((H, T, 1), F32)],
        scratch_shapes=[pltpu.VMEM((nc, ta, HEAD_DIM), F32), pltpu.VMEM((nc, ta, 1), F32)],
        compiler_params=_params(("parallel", "arbitrary")),
    )(z7, z7, z7, dob, c_chunks, lse_col, d_col)


def _attn_bwd_kv_loop(z7, dob, c_col, lse_chunks, d_chunks, name):
    _, T, W = z7.shape
    H = W // HEAD_DIM
    ta, nc, n_chunks = _attn_geometry(T)
    scale = np.float32(1.0 / np.sqrt(HEAD_DIM))

    def body(k_ref, v_ref, q_ref, do_ref, ccol_ref, lse_ref, d_ref, dk_ref, dv_ref, dc_ref, dk_scr, dv_scr, dc_scr):
        g = pl.program_id(1)
        dk_scr[...] = jnp.zeros_like(dk_scr)
        dv_scr[...] = jnp.zeros_like(dv_scr)
        dc_scr[...] = jnp.zeros_like(dc_scr)

        def update(ch, q, do, lse_row, d_row, diagonal):
            rows = slice(ch * ta, (ch + 1) * ta)
            st = _mm_nt(k_ref[rows, :], q) * scale - ccol_ref[rows, :] - lse_row
            if diagonal:
                st = jnp.where(_causal_tile(ta, keys_on_rows=True), st, NEG_BIG)
            pt = jnp.exp(st)
            dv_scr[ch] += _mm(pt.astype(BF16), do)
            dst = pt * (_mm_nt(v_ref[rows, :], do) - d_row)
            dk_scr[ch] += _mm(dst.astype(BF16), q)
            dc_scr[ch] += jnp.sum(dst, axis=-1, keepdims=True)

        for ii in range(nc):
            i = nc * g + ii
            q = _chunk(q_ref, i, ta)
            do = _chunk(do_ref, i, ta)
            for ch in range(0, ii + 1):
                update(ch, q, do, lse_ref[i], d_ref[i], ch == ii)

        def full_chunk(i, carry):
            q = _chunk(q_ref, i, ta)
            do = _chunk(do_ref, i, ta)
            for ch in range(nc):
                update(ch, q, do, lse_ref[i], d_ref[i], False)
            return carry

        lax.fori_loop(nc * (g + 1), n_chunks, full_chunk, 0)
        for ch in range(nc):
            rows = slice(ch * ta, (ch + 1) * ta)
            dk_ref[rows, :] = (dk_scr[ch] * scale).astype(BF16)
            dv_ref[rows, :] = dv_scr[ch].astype(BF16)
            dc_ref[rows, :] = -dc_scr[ch]

    tk = nc * ta
    chunks = pl.BlockSpec((None, n_chunks, 1, ta), lambda h, g: (h, 0, 0, 0))
    col = pl.BlockSpec((None, tk, 1), lambda h, g: (h, g, 0))
    tile = pl.BlockSpec((tk, HEAD_DIM), lambda h, g: (g, h))
    return pl.pallas_call(
        body, name=name, grid=(H, n_chunks // nc),
        in_specs=[pl.BlockSpec((None, tk, HEAD_DIM), lambda h, g: (1, g, h)),
                  pl.BlockSpec((None, tk, HEAD_DIM), lambda h, g: (2, g, h)),
                  pl.BlockSpec((None, T, HEAD_DIM), lambda h, g: (0, 0, h)),
                  pl.BlockSpec((T, HEAD_DIM), lambda h, g: (0, h)),
                  col, chunks, chunks],
        out_specs=[tile, tile, col],
        out_shape=[jax.ShapeDtypeStruct((T, W), BF16), jax.ShapeDtypeStruct((T, W), BF16),
                   jax.ShapeDtypeStruct((H, T, 1), F32)],
        scratch_shapes=[pltpu.VMEM((nc, ta, HEAD_DIM), F32), pltpu.VMEM((nc, ta, HEAD_DIM), F32),
                        pltpu.VMEM((nc, ta, 1), F32)],
        compiler_params=_params(("parallel", "arbitrary")),
    )(z7, z7, z7, dob, c_col, lse_chunks, d_chunks)


def _chunk_causal_mask():
    rows = lax.broadcasted_iota(jnp.int32, (SGU_LEN, SGU_LEN), 0)
    cols = lax.broadcasted_iota(jnp.int32, (SGU_LEN, SGU_LEN), 1)
    return (cols // CHUNK) <= (rows // CHUNK)


def _sgu_norm_mix(sv, lng_ref, lnb_ref, ws_ref, bs_ref, vn_scr, mixed_scr, vhat_scr=None):
    tm = sv.shape[0]
    vs = _gelu(sv)
    mask = _chunk_causal_mask()
    rstds = []
    for g in range(N_GROUPS):
        lanes = slice(g * GROUP_DIM, (g + 1) * GROUP_DIM)
        blk = vs[:, lanes]
        cen = blk - jnp.mean(blk, axis=-1, keepdims=True)
        rstd = lax.rsqrt(jnp.mean(cen * cen, axis=-1, keepdims=True) + LN_EPS)
        vhat = cen * rstd
        rstds.append(rstd)
        if vhat_scr is not None:
            vhat_scr[:, lanes] = vhat
        vn_scr[:, lanes] = (vhat * lng_ref[:, lanes] + lnb_ref[:, lanes]).astype(BF16)
        wm = jnp.where(mask, ws_ref[g], 0.0).astype(BF16)
        for w in range(tm // SGU_LEN):
            rows = slice(w * SGU_LEN, (w + 1) * SGU_LEN)
            mixed_scr[rows, lanes] = _mm(wm, vn_scr[rows, lanes]) + bs_ref[g]
    return rstds


def _mix_out_fwd(z7, o_a, x1, lng, lnb, ws, bs, w_out, g_post, name):
    _, T, W = z7.shape
    D = x1.shape[1]
    tm = _blk(T, 256)

    def body(u_ref, sv_ref, ga_ref, gb_ref, oa_ref, x1_ref, lng_ref, lnb_ref, ws_ref, bs_ref, wo_ref, gp_ref,
             x2_ref, p_ref, mb_ref, vn_scr, mixed_scr):
        _sgu_norm_mix(sv_ref[...].astype(F32), lng_ref, lnb_ref, ws_ref, bs_ref, vn_scr, mixed_scr)
        o_b = _gelu(u_ref[...].astype(F32)) * mixed_scr[...]
        merged = (jax.nn.sigmoid(ga_ref[...].astype(F32)) * oa_ref[...]
                  + jax.nn.sigmoid(gb_ref[...].astype(F32)) * o_b).astype(BF16)
        mb_ref[...] = merged
        p = _mm(merged, wo_ref[...])
        p_ref[...] = p
        x2_ref[...] = x1_ref[...] + p * _rms_scale(p) * gp_ref[...]

    def seg(idx):
        return pl.BlockSpec((None, tm, W), lambda i, idx=idx: (idx, i, 0))

    row = pl.BlockSpec((tm, D), lambda i: (i, 0))
    vec = pl.BlockSpec((1, D), lambda i: (0, 0))
    return pl.pallas_call(
        body, name=name, grid=(T // tm,),
        in_specs=[seg(3), seg(4), seg(5), seg(6), row, row, vec, vec,
                  pl.BlockSpec((N_GROUPS, SGU_LEN, SGU_LEN), lambda i: (0, 0, 0)),
                  pl.BlockSpec((N_GROUPS, SGU_LEN, 1), lambda i: (0, 0, 0)),
                  pl.BlockSpec((D, D), lambda i: (0, 0)), vec],
        out_specs=[row, row, row],
        out_shape=[jax.ShapeDtypeStruct((T, D), F32), jax.ShapeDtypeStruct((T, D), F32),
                   jax.ShapeDtypeStruct((T, D), BF16)],
        scratch_shapes=[pltpu.VMEM((tm, W), BF16), pltpu.VMEM((tm, W), F32)],
        compiler_params=_params(("parallel",)),
    )(z7, z7, z7, z7, o_a, x1, lng, lnb, ws, bs, w_out, g_post)


def _mix_out_bwd(dx2, p, z7, o_a, lng, lnb, ws, bs, w_out, g_post, name, dep=None):
    _, T, W = z7.shape
    D = dx2.shape[1]
    tm = _blk(T, 256)
    n_w = tm // SGU_LEN

    def body(dx2_ref, p_ref, u_ref, sv_ref, ga_ref, gb_ref, oa_ref, lng_ref, lnb_ref, ws_ref, bs_ref, wo_ref, gp_ref, _,
             dpb_ref, dob_ref, dvec_ref, dz_ref, dgp_ref, dlng_ref, dlnb_ref, dws_ref, dbs_ref,
             vn_scr, mixed_scr, vhat_scr, dmix_scr, dvn_scr):
        @pl.when(pl.program_id(0) == 0)
        def _():
            dgp_ref[...] = jnp.zeros_like(dgp_ref)
            dlng_ref[...] = jnp.zeros_like(dlng_ref)
            dlnb_ref[...] = jnp.zeros_like(dlnb_ref)
            dws_ref[...] = jnp.zeros_like(dws_ref)
            dbs_ref[...] = jnp.zeros_like(dbs_ref)

        pv = p_ref[...]
        s = _rms_scale(pv)
        n = pv * s
        dn = dx2_ref[...]
        dgp_ref[...] += jnp.sum(dn * n, axis=0, keepdims=True)
        dpb = _rms_bwd(dn, n, s, gp_ref[...]).astype(BF16)
        dpb_ref[...] = dpb
        dmerged = _mm_nt(dpb, wo_ref[...])

        sv = sv_ref[...].astype(F32)
        rstds = _sgu_norm_mix(sv, lng_ref, lnb_ref, ws_ref, bs_ref, vn_scr, mixed_scr, vhat_scr)
        u_pre = u_ref[...].astype(F32)
        u = _gelu(u_pre)
        mixed = mixed_scr[...]
        sa = jax.nn.sigmoid(ga_ref[...].astype(F32))
        sb = jax.nn.sigmoid(gb_ref[...].astype(F32))
        oa = oa_ref[...]
        do_a = (dmerged * sa).astype(BF16)
        dob_ref[...] = do_a
        prod = do_a.astype(F32) * oa
        for h in range(N_HEADS):
            dvec_ref[h] = jnp.sum(prod[:, h * HEAD_DIM:(h + 1) * HEAD_DIM], axis=-1, keepdims=True)
        dz_ref[2] = (dmerged * oa * (sa * (1.0 - sa))).astype(BF16)
        dz_ref[3] = (dmerged * (u * mixed) * (sb * (1.0 - sb))).astype(BF16)
        do_b = dmerged * sb
        dz_ref[0] = (do_b * mixed * _gelu_grad(u_pre)).astype(BF16)
        dmix_scr[...] = do_b * u

        mask = _chunk_causal_mask()
        for g in range(N_GROUPS):
            lanes = slice(g * GROUP_DIM, (g + 1) * GROUP_DIM)
            wm = jnp.where(mask, ws_ref[g], 0.0).astype(BF16)
            dws = jnp.zeros((SGU_LEN, SGU_LEN), F32)
            dbs = jnp.zeros((SGU_LEN, 1), F32)
            for w in range(n_w):
                rows = slice(w * SGU_LEN, (w + 1) * SGU_LEN)
                dmix = dmix_scr[rows, lanes]
                dmix_b = dmix.astype(BF16)
                dvn_scr[rows, lanes] = _mm_tn(wm, dmix_b)
                dws = dws + _mm_nt(dmix_b, vn_scr[rows, lanes])
                dbs = dbs + jnp.sum(dmix, axis=-1, keepdims=True)
            dws_ref[g] += jnp.where(mask, dws, 0.0)
            dbs_ref[g] += dbs
            dvn = dvn_scr[:, lanes]
            vhat = vhat_scr[:, lanes]
            dlng_ref[:, lanes] += jnp.sum(dvn * vhat, axis=0, keepdims=True)
            dlnb_ref[:, lanes] += jnp.sum(dvn, axis=0, keepdims=True)
            dvh = dvn * lng_ref[:, lanes]
            dvs = rstds[g] * (dvh - jnp.mean(dvh, axis=-1, keepdims=True)
                              - vhat * jnp.mean(dvh * vhat, axis=-1, keepdims=True))
            dvn_scr[:, lanes] = dvs
        dz_ref[1] = (dvn_scr[...] * _gelu_grad(sv)).astype(BF16)

    def seg(idx):
        return pl.BlockSpec((None, tm, W), lambda i, idx=idx: (idx, i, 0))

    row = pl.BlockSpec((tm, D), lambda i: (i, 0))
    vec = pl.BlockSpec((1, D), lambda i: (0, 0))
    ws_spec = pl.BlockSpec((N_GROUPS, SGU_LEN, SGU_LEN), lambda i: (0, 0, 0))
    bs_spec = pl.BlockSpec((N_GROUPS, SGU_LEN, 1), lambda i: (0, 0, 0))
    return pl.pallas_call(
        body, name=name, grid=(T // tm,),
        in_specs=[row, row, seg(3), seg(4), seg(5), seg(6), row, vec, vec, ws_spec, bs_spec,
                  pl.BlockSpec((D, D), lambda i: (0, 0)), vec, ANY],
        out_specs=[row, row, pl.BlockSpec((N_HEADS, tm, 1), lambda i: (0, i, 0)),
                   pl.BlockSpec((4, tm, W), lambda i: (0, i, 0)), vec, vec, vec, ws_spec, bs_spec],
        out_shape=[jax.ShapeDtypeStruct((T, D), BF16), jax.ShapeDtypeStruct((T, W), BF16),
                   jax.ShapeDtypeStruct((N_HEADS, T, 1), F32), jax.ShapeDtypeStruct((4, T, W), BF16),
                   jax.ShapeDtypeStruct((1, D), F32), jax.ShapeDtypeStruct((1, D), F32),
                   jax.ShapeDtypeStruct((1, D), F32),
                   jax.ShapeDtypeStruct((N_GROUPS, SGU_LEN, SGU_LEN), F32),
                   jax.ShapeDtypeStruct((N_GROUPS, SGU_LEN, 1), F32)],
        scratch_shapes=[pltpu.VMEM((tm, W), BF16), pltpu.VMEM((tm, W), F32), pltpu.VMEM((tm, W), F32),
                        pltpu.VMEM((tm, W), F32), pltpu.VMEM((tm, W), F32)],
        compiler_params=_params(("arbitrary",)),
    )(dx2, p, z7, z7, z7, z7, o_a, lng, lnb, ws, bs, w_out, g_post, _after(dep))


def _loss_head(y, target, name):
    T, D = y.shape
    tm = _blk(T, 1024)
    n_i = T // tm

    def body(y_ref, t_ref, dy_ref, loss_ref, acc_scr):
        i = pl.program_id(0)

        @pl.when(i == 0)
        def _():
            acc_scr[...] = jnp.zeros_like(acc_scr)

        e = y_ref[...] - t_ref[...]
        dy_ref[...] = e * np.float32(1.0 / D)
        acc_scr[...] += jnp.sum(e * e, axis=0, keepdims=True)

        @pl.when(i == n_i - 1)
        def _():
            total = jnp.sum(acc_scr[...], axis=-1, keepdims=True) * np.float32(0.5 / D)
            loss_ref[...] = jnp.broadcast_to(total, loss_ref.shape)

    row = pl.BlockSpec((tm, D), lambda i: (i, 0))
    return pl.pallas_call(
        body, name=name, grid=(n_i,),
        in_specs=[row, row],
        out_specs=[row, pl.BlockSpec((1, LANES), lambda i: (0, 0))],
        out_shape=[jax.ShapeDtypeStruct((T, D), F32), jax.ShapeDtypeStruct((1, LANES), F32)],
        scratch_shapes=[pltpu.VMEM((1, D), F32)],
        compiler_params=_params(("arbitrary",)),
    )(y, target)


def _adamw_math(w, g, m, v):
    m_new = ADAM_B1 * m + (1.0 - ADAM_B1) * g
    v_new = ADAM_B2 * v + (1.0 - ADAM_B2) * (g * g)
    m_hat = m_new / np.float32(1.0 - ADAM_B1 ** ADAM_STEP)
    v_hat = v_new / np.float32(1.0 - ADAM_B2 ** ADAM_STEP)
    delta = -ADAM_LR * (m_hat / (jnp.sqrt(v_hat) + ADAM_EPS) + ADAM_WD * w)
    return delta, m_new, v_new


def _sum_adamw(parts, w, m, v, name):
    n, R, C = parts.shape
    tr = _blk(R, 128)

    def body(p_ref, w_ref, m_ref, v_ref, g_ref, d_ref, mo_ref, vo_ref):
        g = p_ref[0].astype(F32)
        for s in range(1, n):
            g = g + p_ref[s].astype(F32)
        delta, m_new, v_new = _adamw_math(w_ref[...], g, m_ref[...], v_ref[...])
        g_ref[...] = g
        d_ref[...] = delta
        mo_ref[...] = m_new
        vo_ref[...] = v_new

    row = pl.BlockSpec((tr, C), lambda i: (i, 0))
    shp = jax.ShapeDtypeStruct((R, C), F32)
    return pl.pallas_call(
        body, name=name, grid=(R // tr,),
        in_specs=[pl.BlockSpec((n, tr, C), lambda i: (0, i, 0)), row, row, row],
        out_specs=[row, row, row, row], out_shape=[shp, shp, shp, shp],
        compiler_params=_params(("parallel",)),
    )(parts, w, m, v)


def _adamw(g, w, m, v, name):
    R, C = g.shape
    tr = _blk(R, 128)

    def body(g_ref, w_ref, m_ref, v_ref, d_ref, mo_ref, vo_ref):
        delta, m_new, v_new = _adamw_math(w_ref[...], g_ref[...], m_ref[...], v_ref[...])
        d_ref[...] = delta
        mo_ref[...] = m_new
        vo_ref[...] = v_new

    row = pl.BlockSpec((tr, C), lambda i: (i, 0))
    shp = jax.ShapeDtypeStruct((R, C), F32)
    return pl.pallas_call(
        body, name=name, grid=(R // tr,),
        in_specs=[row, row, row, row], out_specs=[row, row, row], out_shape=[shp, shp, shp],
        compiler_params=_params(("parallel",)),
    )(g, w, m, v)


def _position():
    return lax.axis_index("x"), lax.axis_index("y"), lax.axis_index("c")


def _slot(px, py, pc):
    return 4 * px + 2 * py + pc


def _all_gather(shards, name):
    n = len(shards)

    def body(*refs):
        ins, outs = refs[:n], refs[n:2 * n]
        send_sems, recv_sems, local_sems = refs[2 * n:]
        x, y, c = _position()
        me, sibling = (x, y, c), (x, y, 1 - c)
        chips = [(1 - x, y), (x, 1 - y), (1 - x, 1 - y)]

        def copy(a, k, block, to, src=None):
            dst = outs[a].at[_slot(*block)]
            return pltpu.make_async_remote_copy(
                src_ref=dst if src is None else src, dst_ref=dst,
                send_sem=send_sems.at[a, k], recv_sem=recv_sems.at[a, k],
                device_id=to, device_id_type=MESH)

        mine = [pltpu.make_async_copy(ins[a], outs[a].at[_slot(*me)], local_sems.at[a]) for a in range(n)]
        for cp in mine:
            cp.start()
        first = []
        for a in range(n):
            first.append(copy(a, 0, me, sibling, src=ins[a]))
            first += [copy(a, 1 + j, me, (*chip, c), src=ins[a]) for j, chip in enumerate(chips)]
        for cp in first:
            cp.start()
        passed = []
        for j, chip in enumerate(chips):
            for a in range(n):
                copy(a, 1 + j, (*chip, c), me).wait_recv()
                fwd = copy(a, 4 + j, (*chip, c), sibling)
                fwd.start()
                passed.append(fwd)
        for a in range(n):
            copy(a, 0, sibling, me).wait_recv()
            for j, chip in enumerate(chips):
                copy(a, 4 + j, (*chip, 1 - c), me).wait_recv()
        for cp in first + passed:
            cp.wait_send()
        for cp in mine:
            cp.wait()

    return pl.pallas_call(
        body, name=name,
        in_specs=[ANY] * n, out_specs=[ANY] * n,
        out_shape=[jax.ShapeDtypeStruct((N_DEV,) + s.shape, s.dtype) for s in shards],
        scratch_shapes=[pltpu.SemaphoreType.DMA((n, 7)), pltpu.SemaphoreType.DMA((n, 7)),
                        pltpu.SemaphoreType.DMA((n,))],
    )(*shards)


def _peer(x, y, c, k):
    return (1 - x if k & 4 else x, 1 - y if k & 2 else y, 1 - c if k & 1 else c)


def _exchange(parts, name):
    n = len(parts)

    def body(*refs):
        ins, outs = refs[:n], refs[n:2 * n]
        send_sems, recv_sems, local_sems = refs[2 * n:]
        x, y, c = _position()
        me = _slot(x, y, c)
        mine = [pltpu.make_async_copy(ins[a].at[me], outs[a].at[me], local_sems.at[a]) for a in range(n)]
        for cp in mine:
            cp.start()
        sends = []
        for k in range(1, N_DEV):
            to = _peer(x, y, c, k)
            for a in range(n):
                cp = pltpu.make_async_remote_copy(
                    src_ref=ins[a].at[_slot(*to)], dst_ref=outs[a].at[me],
                    send_sem=send_sems.at[a, k - 1], recv_sem=recv_sems.at[a, k - 1],
                    device_id=to, device_id_type=MESH)
                cp.start()
                sends.append(cp)
        for k in range(1, N_DEV):
            frm = _peer(x, y, c, k)
            for a in range(n):
                pltpu.make_async_remote_copy(
                    src_ref=ins[a].at[_slot(*frm)], dst_ref=outs[a].at[_slot(*frm)],
                    send_sem=send_sems.at[a, k - 1], recv_sem=recv_sems.at[a, k - 1],
                    device_id=frm, device_id_type=MESH).wait_recv()
        for cp in sends:
            cp.wait_send()
        for cp in mine:
            cp.wait()

    return pl.pallas_call(
        body, name=name,
        in_specs=[ANY] * n, out_specs=[ANY] * n,
        out_shape=[jax.ShapeDtypeStruct(p.shape, p.dtype) for p in parts],
        scratch_shapes=[pltpu.SemaphoreType.DMA((n, 7)), pltpu.SemaphoreType.DMA((n, 7)),
                        pltpu.SemaphoreType.DMA((n,))],
    )(*parts)


HBM_SPEC = pl.BlockSpec(memory_space=pltpu.HBM)
SEM_SPEC = pl.BlockSpec(memory_space=pltpu.SEMAPHORE)
SIDE_EFFECT = pltpu.SideEffectType.DATAFLOW_SIDE_EFFECTING


def _remote_copies(src_refs, land_refs, send_sems, recv_sems, gather, outgoing):
    x, y, c = _position()
    me = _slot(x, y, c)
    copies = []
    for k in range(1, N_DEV):
        peer = _peer(x, y, c, k)
        for a in range(len(src_refs)):
            src = src_refs[a] if gather else src_refs[a].at[_slot(*peer)]
            dst = land_refs[a].at[me if outgoing else _slot(*peer)]
            sem = a * (N_DEV - 1) + k - 1
            copies.append(pltpu.make_async_remote_copy(
                src_ref=src, dst_ref=dst, send_sem=send_sems.at[sem], recv_sem=recv_sems.at[sem],
                device_id=peer, device_id_type=MESH))
    return copies


def _remote_start(srcs, after, name, gather):
    n = len(srcs)
    lands = [jax.ShapeDtypeStruct(((N_DEV,) + s.shape) if gather else s.shape, s.dtype) for s in srcs]

    def body(*refs):
        src_refs, land_refs = refs[:n], refs[n:2 * n]
        send_sems, recv_sems = refs[2 * n + 1], refs[2 * n + 2]
        token, local_sems = refs[4 * n + 3], refs[4 * n + 4]
        x, y, c = _position()
        me = _slot(x, y, c)
        mine = [pltpu.make_async_copy(src_refs[a] if gather else src_refs[a].at[me], land_refs[a].at[me],
                                      local_sems.at[a]) for a in range(n)]
        for cp in mine:
            cp.start()
        for cp in _remote_copies(src_refs, land_refs, send_sems, recv_sems, gather, outgoing=True):
            cp.start()
        for cp in mine:
            cp.wait()
        token[...] = jnp.zeros_like(token)

    sem_shape = pltpu.SemaphoreType.DMA((n * (N_DEV - 1),))
    outs = pl.pallas_call(
        body, name=name,
        out_shape=(sem_shape, sem_shape, *[pltpu.HBM(s.shape, s.dtype) for s in srcs],
                   *[pltpu.HBM(l.shape, l.dtype) for l in lands], jax.ShapeDtypeStruct((8, LANES), F32)),
        in_specs=[HBM_SPEC] * (2 * n) + [ANY],
        out_specs=(SEM_SPEC, SEM_SPEC, *([HBM_SPEC] * (2 * n)), pl.BlockSpec(memory_space=pltpu.VMEM)),
        input_output_aliases={a: 2 + a for a in range(2 * n)},
        scratch_shapes=[pltpu.SemaphoreType.DMA((n,))],
        compiler_params=pltpu.CompilerParams(has_side_effects=SIDE_EFFECT),
    )(*[pltpu.with_memory_space_constraint(s, pltpu.HBM) for s in srcs],
      *[pltpu.with_memory_space_constraint(lax.empty(l.shape, l.dtype), pltpu.HBM) for l in lands], after)
    return dict(send=outs[0], recv=outs[1], srcs=outs[2:2 + n], lands=outs[2 + n:2 + 2 * n], token=outs[-1],
                gather=gather)


def _remote_wait(flight, after, name):
    n = len(flight["srcs"])
    gather = flight["gather"]

    def body(*refs):
        src_refs, land_refs = refs[:n], refs[n:2 * n]
        send_sems, recv_sems = refs[2 * n], refs[2 * n + 1]
        for cp in _remote_copies(src_refs, land_refs, send_sems, recv_sems, gather, outgoing=False):
            cp.wait_send()
            cp.wait_recv()

    both = list(flight["srcs"]) + list(flight["lands"])
    outs = pl.pallas_call(
        body, name=name,
        out_shape=tuple(pltpu.HBM(a.shape, a.dtype) for a in both),
        in_specs=[HBM_SPEC] * (2 * n) + [SEM_SPEC, SEM_SPEC, ANY],
        out_specs=tuple([HBM_SPEC] * (2 * n)),
        input_output_aliases={a: a for a in range(2 * n)},
        compiler_params=pltpu.CompilerParams(has_side_effects=SIDE_EFFECT),
    )(*both, flight["send"], flight["recv"], after)
    return list(outs[n:])


def _all_reduce_small(blob, name):
    R, C = blob.shape

    def body(in_ref, out_ref, gath, send_sems, recv_sems):
        x, y, c = _position()
        me = _slot(x, y, c)
        gath[me] = in_ref[...]
        sends = []
        for k in range(1, N_DEV):
            to = _peer(x, y, c, k)
            cp = pltpu.make_async_remote_copy(
                src_ref=in_ref, dst_ref=gath.at[me],
                send_sem=send_sems.at[k - 1], recv_sem=recv_sems.at[k - 1],
                device_id=to, device_id_type=MESH)
            cp.start()
            sends.append(cp)
        for k in range(1, N_DEV):
            frm = _peer(x, y, c, k)
            pltpu.make_async_remote_copy(
                src_ref=in_ref, dst_ref=gath.at[_slot(*frm)],
                send_sem=send_sems.at[k - 1], recv_sem=recv_sems.at[k - 1],
                device_id=frm, device_id_type=MESH).wait_recv()
        for cp in sends:
            cp.wait_send()
        total = gath[0]
        for s in range(1, N_DEV):
            total = total + gath[s]
        out_ref[...] = total

    return pl.pallas_call(
        body, name=name,
        in_specs=[pl.BlockSpec(memory_space=pltpu.VMEM)],
        out_specs=pl.BlockSpec(memory_space=pltpu.VMEM),
        out_shape=jax.ShapeDtypeStruct((R, C), F32),
        scratch_shapes=[pltpu.VMEM((N_DEV, R, C), F32), pltpu.SemaphoreType.DMA((7,)),
                        pltpu.SemaphoreType.DMA((7,))],
        compiler_params=pltpu.CompilerParams(vmem_limit_bytes=VMEM_LIMIT),
    )(blob)


SMALL_VECS = ("ffn1_pre_g", "ffn1_post_g", "mix_pre_g", "sgu_ln_g", "sgu_ln_b", "mix_post_g", "ffn2_pre_g",
              "ffn2_post_g")
ROW_BS = len(SMALL_VECS)
ROW_BF = ROW_BS + 1
ROW_LOSS = ROW_BF + 1
ROW_WS = 16
BLOB_ROWS = ROW_WS + SGU_LEN


def _pack_small(vals, D, loss_row=None):
    rows = [vals[n].reshape(1, D) for n in SMALL_VECS]
    rows.append(vals["sgu_b_s"].reshape(1, D))
    rows.append(jnp.pad(vals["b_forget"].reshape(1, N_HEADS), ((0, 0), (0, D - N_HEADS))))
    rows.append(jnp.zeros((1, D), F32) if loss_row is None else loss_row)
    rows.append(jnp.zeros((ROW_WS - ROW_LOSS - 1, D), F32))
    rows.append(vals["sgu_w_s"].reshape(SGU_LEN, D))
    return jnp.concatenate(rows, axis=0)


def _unpack_small(blob, D):
    out = {n: blob[r:r + 1] for r, n in enumerate(SMALL_VECS)}
    out["sgu_b_s"] = blob[ROW_BS].reshape(1, N_GROUPS, SGU_LEN)
    out["b_forget"] = blob[ROW_BF, :N_HEADS].reshape(1, N_HEADS)
    out["sgu_w_s"] = blob[ROW_WS:].reshape(1, N_GROUPS, SGU_LEN, SGU_LEN)
    return out


WEIGHT_NAMES = ("ffn1_pre_g", "ffn1_w_gate", "ffn1_w_up", "ffn1_w_down", "ffn1_post_g", "mix_pre_g", "w_in",
                "b_forget", "sgu_ln_g", "sgu_ln_b", "sgu_w_s", "sgu_b_s", "w_out", "mix_post_g", "ffn2_pre_g",
                "ffn2_w_gate", "ffn2_w_up", "ffn2_w_down", "ffn2_post_g")
BIG_NAMES = ("ffn1_w_gate", "ffn1_w_up", "ffn1_w_down", "w_in", "w_out", "ffn2_w_gate", "ffn2_w_up", "ffn2_w_down")
WEIGHT_GROUPS = {"ffn1": ("ffn1_w_gate", "ffn1_w_up", "ffn1_w_down"), "mix": ("w_in", "w_out"),
                 "ffn2": ("ffn2_w_gate", "ffn2_w_up", "ffn2_w_down")}
GRAD_GROUPS = (("ffn2_w_gate", "ffn2_w_up", "ffn2_w_down"), ("w_out", "w_in"), ("ffn1_w_down", "ffn1_w_gate"),
               ("ffn1_w_up",))


def _local_step(x, target, small, fetch, emit):
    T, D = x.shape
    W = N_HEADS * HEAD_DIM
    vec = lambda n: small[n].reshape(1, D)
    big = dict(fetch("ffn1", x))

    x1, y1, dgf1, silu1, act1 = _ffn_fwd(x, vec("ffn1_pre_g"), big["ffn1_w_gate"], big["ffn1_w_up"], big["ffn1_w_down"],
                                  vec("ffn1_post_g"), "ffn1_fwd")

    big.update(fetch("mix", x1))
    w_in_all = big["w_in"]
    in_width = N_DEV * w_in_all.shape[2]
    w_in = w_in_all.transpose(1, 0, 2).reshape(D, in_width)
    col_f = 3 * W
    col_u = col_f + N_HEADS
    seg_starts = (0, W, 2 * W, col_u, col_u + W, col_u + 2 * W, col_u + 3 * W)
    w7 = jnp.stack([w_in[:, s:s + W] for s in seg_starts])
    wf = jnp.pad(w_in[:, col_f:col_u], ((0, 0), (0, LANES - N_HEADS)))
    w_out = big["w_out"].reshape(D, D)
    b_pad = jnp.pad(small["b_forget"].reshape(1, N_HEADS), ((0, 0), (0, LANES - N_HEADS)))
    lng, lnb = vec("sgu_ln_g"), vec("sgu_ln_b")
    ws = small["sgu_w_s"].reshape(N_GROUPS, SGU_LEN, SGU_LEN)
    bs = small["sgu_b_s"].reshape(N_GROUPS, SGU_LEN, 1)

    z7, f_logit, h2b = _mix_in_fwd(x1, vec("mix_pre_g"), w7, wf, "mix_in_fwd")
    c = _forget_cumsum(f_logit, b_pad, "forget_cumsum")
    c_heads = c[:, :N_HEADS].T
    ta, _, n_chunks = _attn_geometry(T)
    c_chunks = c_heads.reshape(N_HEADS, n_chunks, 1, ta)
    c_col = c_heads[:, :, None]
    vt = z7[2].reshape(n_chunks, ta, N_HEADS, HEAD_DIM).transpose(2, 0, 3, 1)
    c_rep = jnp.broadcast_to(c_col, (N_HEADS, T, LANES))
    o_a, lse_chunks = _attn_fwd_keys_on_rows(z7, vt, c_rep, "attn_fwd")
    lse = lse_chunks.reshape(N_HEADS, T, 1)
    x2, p, merged_b = _mix_out_fwd(z7, o_a, x1, lng, lnb, ws, bs, w_out, vec("mix_post_g"), "mix_out_fwd")
    big.update(fetch("ffn2", x2))
    x3, y2, dgf2, silu2, act2 = _ffn_fwd(x2, vec("ffn2_pre_g"), big["ffn2_w_gate"], big["ffn2_w_up"], big["ffn2_w_down"],
                                  vec("ffn2_post_g"), "ffn2_fwd")
    dy, loss_lanes = _loss_head(x3, target, "loss_head")

    grads_small = {}

    dx2, h3b, dy2b, dgate2, dup2, dgpre, dgpost = _ffn_bwd(
        dy, x2, y2, dgf2, silu2, vec("ffn2_pre_g"), big["ffn2_w_gate"], big["ffn2_w_up"], big["ffn2_w_down"],
        vec("ffn2_post_g"), "ffn2_bwd")
    grads_small["ffn2_pre_g"] = jnp.sum(dgpre, axis=0)
    grads_small["ffn2_post_g"] = jnp.sum(dgpost, axis=0)
    emit("ffn2_w_gate", _wgrad(h3b, dgate2, "ffn2_wgrad_gate", shard_cols=True))
    emit("ffn2_w_up", _wgrad(h3b, dup2, "ffn2_wgrad_up", shard_cols=True))
    dep = emit("ffn2_w_down", _wgrad(act2, dy2b, "ffn2_wgrad_down").reshape(big["ffn2_w_down"].shape))

    dpb, dob, dvec, dz4, dgp, dlng, dlnb, dws, dbs = _mix_out_bwd(
        dx2, p, z7, o_a, lng, lnb, ws, bs, w_out, vec("mix_post_g"), "mix_out_bwd", dep=dep)
    grads_small["mix_post_g"] = dgp
    grads_small["sgu_ln_g"] = dlng
    grads_small["sgu_ln_b"] = dlnb
    grads_small["sgu_w_s"] = dws
    grads_small["sgu_b_s"] = dbs
    emit("w_out", _wgrad(merged_b, dpb, "w_out_wgrad").reshape(big["w_out"].shape))
    d_chunks = dvec.reshape(N_HEADS, n_chunks, 1, ta)
    kt = z7[1].reshape(n_chunks, ta, N_HEADS, HEAD_DIM).transpose(2, 0, 3, 1)
    dk, dv, dc, dq, dc_q = _attn_bwd_fused(z7, kt, dob, c_rep, lse_chunks, d_chunks, "attn_bwd")
    dc_pad = jnp.pad((dc.reshape(N_HEADS, T) + dc_q.reshape(N_HEADS, T)).T, ((0, 0), (0, LANES - N_HEADS)))
    dfb, dbf = _forget_bwd(dc_pad, f_logit, b_pad, "forget_bwd")
    grads_small["b_forget"] = dbf[:, :N_HEADS]
    segs = [(dq, None), (dk, None), (dv, None), (dz4, 0), (dz4, 1), (dz4, 2), (dz4, 3)]
    dx1, dgm = _mix_in_bwd(dx2, x1, vec("mix_pre_g"), segs, dfb, w7, wf, "mix_in_bwd")
    grads_small["mix_pre_g"] = jnp.sum(dgm, axis=0)
    seg_mats = [dq, dk, dv, dz4[0], dz4[1], dz4[2], dz4[3]]
    dw_seg = [_wgrad(h2b, sm, "w_in_wgrad_%d" % q) for q, sm in enumerate(seg_mats)]
    dwf = _wgrad(h2b, dfb, "w_in_wgrad_f")[:, :N_HEADS]
    dw_in = jnp.concatenate(dw_seg[:3] + [dwf] + dw_seg[3:], axis=1)
    dep = emit("w_in", dw_in.reshape(D, N_DEV, in_width // N_DEV).transpose(1, 0, 2))

    dx0, h1b, dy1b, dgate1, dup1, dgpre1, dgpost1 = _ffn_bwd(
        dx1, x, y1, dgf1, silu1, vec("ffn1_pre_g"), big["ffn1_w_gate"], big["ffn1_w_up"], big["ffn1_w_down"],
        vec("ffn1_post_g"), "ffn1_bwd", dep=dep)
    grads_small["ffn1_pre_g"] = jnp.sum(dgpre1, axis=0)
    grads_small["ffn1_post_g"] = jnp.sum(dgpost1, axis=0)
    emit("ffn1_w_down", _wgrad(act1, dy1b, "ffn1_wgrad_down").reshape(big["ffn1_w_down"].shape))
    dep = emit("ffn1_w_gate", _wgrad(h1b, dgate1, "ffn1_wgrad_gate", shard_cols=True))
    emit("ffn1_w_up", _wgrad(h1b, dup1, "ffn1_wgrad_up", shard_cols=True, dep=dep))

    loss_row = jnp.pad(loss_lanes, ((0, 0), (0, D - LANES)))
    return loss_row, dx0, grads_small


def kernel(x, ffn1_pre_g, ffn1_w_gate, ffn1_w_up, ffn1_w_down, ffn1_post_g, mix_pre_g, w_in, b_forget, sgu_ln_g, sgu_ln_b, sgu_w_s, sgu_b_s, w_out, mix_post_g, ffn2_pre_g, ffn2_w_gate, ffn2_w_up, ffn2_w_down, ffn2_post_g, loss_target, m_ffn1_pre_g, m_ffn1_w_gate, m_ffn1_w_up, m_ffn1_w_down, m_ffn1_post_g, m_mix_pre_g, m_w_in, m_b_forget, m_sgu_ln_g, m_sgu_ln_b, m_sgu_w_s, m_sgu_b_s, m_w_out, m_mix_post_g, m_ffn2_pre_g, m_ffn2_w_gate, m_ffn2_w_up, m_ffn2_w_down, m_ffn2_post_g, v_ffn1_pre_g, v_ffn1_w_gate, v_ffn1_w_up, v_ffn1_w_down, v_ffn1_post_g, v_mix_pre_g, v_w_in, v_b_forget, v_sgu_ln_g, v_sgu_ln_b, v_sgu_w_s, v_sgu_b_s, v_w_out, v_mix_post_g, v_ffn2_pre_g, v_ffn2_w_gate, v_ffn2_w_up, v_ffn2_w_down, v_ffn2_post_g):
    weights = dict(zip(WEIGHT_NAMES, (ffn1_pre_g, ffn1_w_gate, ffn1_w_up, ffn1_w_down, ffn1_post_g, mix_pre_g, w_in,
                                      b_forget, sgu_ln_g, sgu_ln_b, sgu_w_s, sgu_b_s, w_out, mix_post_g, ffn2_pre_g,
                                      ffn2_w_gate, ffn2_w_up, ffn2_w_down, ffn2_post_g)))
    mom1 = dict(zip(WEIGHT_NAMES, (m_ffn1_pre_g, m_ffn1_w_gate, m_ffn1_w_up, m_ffn1_w_down, m_ffn1_post_g,
                                   m_mix_pre_g, m_w_in, m_b_forget, m_sgu_ln_g, m_sgu_ln_b, m_sgu_w_s, m_sgu_b_s,
                                   m_w_out, m_mix_post_g, m_ffn2_pre_g, m_ffn2_w_gate, m_ffn2_w_up, m_ffn2_w_down,
                                   m_ffn2_post_g)))
    mom2 = dict(zip(WEIGHT_NAMES, (v_ffn1_pre_g, v_ffn1_w_gate, v_ffn1_w_up, v_ffn1_w_down, v_ffn1_post_g,
                                   v_mix_pre_g, v_w_in, v_b_forget, v_sgu_ln_g, v_sgu_ln_b, v_sgu_w_s, v_sgu_b_s,
                                   v_w_out, v_mix_post_g, v_ffn2_pre_g, v_ffn2_w_gate, v_ffn2_w_up, v_ffn2_w_down,
                                   v_ffn2_post_g)))
    D = x.shape[-1]
    small_names = [n for n in WEIGHT_NAMES if n not in BIG_NAMES]

    small = {n: weights[n] for n in small_names}
    shard = lambda n: weights[n][0].astype(BF16)

    ffn1_full = _all_gather([shard(n) for n in WEIGHT_GROUPS["ffn1"]], "ffn1_all_gather")
    gathers = {grp: _remote_start([shard(n) for n in WEIGHT_GROUPS[grp]], ffn1_full[0], grp + "_gather_start",
                                  gather=True) for grp in ("mix", "ffn2")}

    def fetch(group, after):
        if group == "ffn1":
            return zip(WEIGHT_GROUPS[group], ffn1_full)
        return zip(WEIGHT_GROUPS[group], _remote_wait(gathers[group], after, group + "_gather_wait"))

    ready, flights = {}, []

    def emit(name, part):
        ready[name] = part
        for group in GRAD_GROUPS:
            if name == group[-1]:
                flights.append((group, _remote_start([ready[n] for n in group], part, name + "_grad_start",
                                                     gather=False)))
                return flights[-1][1]["token"]
        return None

    loss_row, grad_x, grads_small = _local_step(x[0], loss_target[0], small, fetch, emit)

    blob = _all_reduce_small(_pack_small(grads_small, D, loss_row) + flights[-1][1]["token"][:1, :1],
                             "small_all_reduce")

    out = {}
    after = blob
    for group, flight in flights:
        received = _remote_wait(flight, after, group[-1] + "_grad_wait")
        for n, rcv in zip(group, received):
            g, d, m_new, v_new = _sum_adamw(rcv, weights[n][0], mom1[n][0], mom2[n][0], "adamw_" + n)
            out[n] = tuple(a[None] for a in (g, d, m_new, v_new))
            after = g

    d_blob, m_blob, v_blob = _adamw(blob, _pack_small(small, D), _pack_small({n: mom1[n] for n in small_names}, D),
                                    _pack_small({n: mom2[n] for n in small_names}, D), "adamw_small")
    unpacked = [_unpack_small(b, D) for b in (blob, d_blob, m_blob, v_blob)]
    for n in small_names:
        out[n] = tuple(u[n].reshape(weights[n].shape) for u in unpacked)

    loss = blob[ROW_LOSS, 0]
    result = [loss, grad_x[None]]
    for k in range(4):
        result += [out[n][k] for n in WEIGHT_NAMES]
    return tuple(result)
```

```python
import functools

import numpy as np
import jax
import jax.numpy as jnp
from jax import lax
from jax.experimental import pallas as pl
from jax.experimental.pallas import tpu as pltpu

F32 = jnp.float32
BF16 = jnp.bfloat16

RMS_EPS = 1e-6
LN_EPS = 1e-5
HEAD_DIM = 128
N_HEADS = 8
GROUP_DIM = 128
N_GROUPS = 8
SGU_LEN = 128
CHUNK = 64
N_DEV = 8
LANES = 128
VMEM_LIMIT = 56 * 1024 * 1024
NEG_BIG = -1e30

ADAM_LR = 0.001
ADAM_B1 = 0.9
ADAM_B2 = 0.999
ADAM_EPS = 1e-08
ADAM_WD = 0.01
ADAM_STEP = 10

MESH = pl.DeviceIdType.MESH
ANY = pl.BlockSpec(memory_space=pl.ANY)


def _blk(n, pref):
    return pref if (n >= pref and n % pref == 0) else n


def _mm(a, b):
    return jnp.dot(a, b, preferred_element_type=F32)


def _mm_nt(a, b):
    return lax.dot_general(a, b, (((1,), (1,)), ((), ())), preferred_element_type=F32)


def _mm_tn(a, b):
    return lax.dot_general(a, b, (((0,), (0,)), ((), ())), preferred_element_type=F32)


def _params(sem):
    return pltpu.CompilerParams(dimension_semantics=sem, vmem_limit_bytes=VMEM_LIMIT)


def _gelu(x):
    return 0.5 * x * (1.0 + lax.erf(x * np.float32(1.0 / np.sqrt(2.0))))


def _gelu_grad(x):
    cdf = 0.5 * (1.0 + lax.erf(x * np.float32(1.0 / np.sqrt(2.0))))
    return cdf + x * jnp.exp(-0.5 * x * x) * np.float32(1.0 / np.sqrt(2.0 * np.pi))


def _rms_scale(v):
    return lax.rsqrt(jnp.mean(v * v, axis=-1, keepdims=True) + RMS_EPS)


def _rms_bwd(dy, xhat, r, g):
    dxh = dy * g
    return r * (dxh - xhat * jnp.mean(dxh * xhat, axis=-1, keepdims=True))


def _ffn_rows(T):
    tm = _blk(T, 1024)
    th = _blk(tm, 512)
    return tm, th, tm // th


def _ffn_fwd(x, g_pre, wg, wu, wd, g_post, name):
    T, D = x.shape
    ns, _, fs = wg.shape
    tm, th, parts = _ffn_rows(T)

    def body(x_ref, gpre_ref, wg_ref, wu_ref, wd_ref, gpost_ref, xo_ref, y_ref, dgf_ref, silu_ref, act_ref,
             h_scr, acc_scr):
        j = pl.program_id(1)

        @pl.when(j == 0)
        def _():
            for r in range(parts):
                rows = slice(r * th, (r + 1) * th)
                xv = x_ref[rows, :]
                h_scr[rows, :] = (xv * _rms_scale(xv) * gpre_ref[...]).astype(BF16)
            acc_scr[...] = jnp.zeros_like(acc_scr)

        pre = []
        for r in range(parts):
            h = h_scr[r * th:(r + 1) * th, :]
            pre.append((_mm(h, wg_ref[...]), _mm(h, wu_ref[...])))
        for r in range(parts):
            rows = slice(r * th, (r + 1) * th)
            gg, uu = pre[r]
            sg = jax.nn.sigmoid(gg)
            silu = gg * sg
            act = (silu * uu).astype(BF16)
            dgf_ref[rows, :] = (uu * (sg * (1.0 + gg * (1.0 - sg)))).astype(BF16)
            silu_ref[rows, :] = silu.astype(BF16)
            act_ref[rows, :] = act
            acc_scr[rows, :] += _mm(act, wd_ref[...])

        @pl.when(j == ns - 1)
        def _():
            for r in range(parts):
                rows = slice(r * th, (r + 1) * th)
                y = acc_scr[rows, :]
                y_ref[rows, :] = y
                xo_ref[rows, :] = x_ref[rows, :] + 0.5 * (y * _rms_scale(y) * gpost_ref[...])

    row = pl.BlockSpec((tm, D), lambda i, j: (i, 0), pipeline_mode=pl.Buffered(1))
    vec = pl.BlockSpec((1, D), lambda i, j: (0, 0))
    return pl.pallas_call(
        body, name=name, grid=(T // tm, ns),
        in_specs=[row, vec,
                  pl.BlockSpec((None, D, fs), lambda i, j: (j, 0, 0)),
                  pl.BlockSpec((None, D, fs), lambda i, j: (j, 0, 0)),
                  pl.BlockSpec((None, fs, D), lambda i, j: (j, 0, 0)),
                  vec],
        out_specs=[row, row] + [pl.BlockSpec((tm, fs), lambda i, j: (i, j))] * 3,
        out_shape=[jax.ShapeDtypeStruct((T, D), F32), jax.ShapeDtypeStruct((T, D), F32)]
        + [jax.ShapeDtypeStruct((T, ns * fs), BF16)] * 3,
        scratch_shapes=[pltpu.VMEM((tm, D), BF16), pltpu.VMEM((tm, D), F32)],
        compiler_params=_params(("parallel", "arbitrary")),
    )(x, g_pre, wg, wu, wd, g_post)


def _after(dep):
    return jnp.zeros((8, LANES), F32) if dep is None else dep


def _ffn_bwd(dxo, x, y, dgf, silu, g_pre, wg, wu, wd, g_post, name, dep=None):
    T, D = x.shape
    ns, _, fs = wg.shape
    tm, th, parts = _ffn_rows(T)
    n_i = T // tm

    def body(dxo_ref, x_ref, y_ref, dgf_ref, silu_ref, gpre_ref, wg_ref, wu_ref, wd_ref, gpost_ref, _,
             dx_ref, hb_ref, dyb_ref, dgb_ref, dub_ref, dgpre_ref, dgpost_ref, dy_scr, acc_scr):
        j = pl.program_id(1)

        @pl.when(j == 0)
        def _():
            dgpost = jnp.zeros((1, D), F32)
            for r in range(parts):
                rows = slice(r * th, (r + 1) * th)
                yv = y_ref[rows, :]
                s = _rms_scale(yv)
                n = yv * s
                dn = 0.5 * dxo_ref[rows, :]
                dgpost = dgpost + jnp.sum(dn * n, axis=0, keepdims=True)
                dyv = _rms_bwd(dn, n, s, gpost_ref[...]).astype(BF16)
                dy_scr[rows, :] = dyv
                dyb_ref[rows, :] = dyv
                xv = x_ref[rows, :]
                hb_ref[rows, :] = (xv * _rms_scale(xv) * gpre_ref[...]).astype(BF16)
            dgpost_ref[...] = dgpost
            acc_scr[...] = jnp.zeros_like(acc_scr)

        das = [_mm_nt(dy_scr[r * th:(r + 1) * th, :], wd_ref[...]) for r in range(parts)]
        for r in range(parts):
            rows = slice(r * th, (r + 1) * th)
            dgate = (das[r] * dgf_ref[rows, :].astype(F32)).astype(BF16)
            dup = (das[r] * silu_ref[rows, :].astype(F32)).astype(BF16)
            dgb_ref[rows, :] = dgate
            dub_ref[rows, :] = dup
            acc_scr[rows, :] += _mm_nt(dgate, wg_ref[...]) + _mm_nt(dup, wu_ref[...])

        @pl.when(j == ns - 1)
        def _():
            dgpre = jnp.zeros((1, D), F32)
            for r in range(parts):
                rows = slice(r * th, (r + 1) * th)
                xv = x_ref[rows, :]
                rs = _rms_scale(xv)
                xhat = xv * rs
                dh = acc_scr[rows, :]
                dgpre = dgpre + jnp.sum(dh * xhat, axis=0, keepdims=True)
                dx_ref[rows, :] = _rms_bwd(dh, xhat, rs, gpre_ref[...]) + dxo_ref[rows, :]
            dgpre_ref[...] = dgpre

    row = pl.BlockSpec((tm, D), lambda i, j: (i, 0), pipeline_mode=pl.Buffered(1))
    vec = pl.BlockSpec((1, D), lambda i, j: (0, 0))
    wide = pl.BlockSpec((tm, fs), lambda i, j: (i, j))
    part = pl.BlockSpec((None, 1, D), lambda i, j: (i, 0, 0))
    F = ns * fs
    return pl.pallas_call(
        body, name=name, grid=(n_i, ns),
        in_specs=[row, row, row, wide, wide, vec,
                  pl.BlockSpec((None, D, fs), lambda i, j: (j, 0, 0)),
                  pl.BlockSpec((None, D, fs), lambda i, j: (j, 0, 0)),
                  pl.BlockSpec((None, fs, D), lambda i, j: (j, 0, 0)),
                  vec, ANY],
        out_specs=[row, row, row, wide, wide, part, part],
        out_shape=[jax.ShapeDtypeStruct((T, D), F32), jax.ShapeDtypeStruct((T, D), BF16),
                   jax.ShapeDtypeStruct((T, D), BF16), jax.ShapeDtypeStruct((T, F), BF16),
                   jax.ShapeDtypeStruct((T, F), BF16),
                   jax.ShapeDtypeStruct((n_i, 1, D), F32), jax.ShapeDtypeStruct((n_i, 1, D), F32)],
        scratch_shapes=[pltpu.VMEM((tm, D), BF16), pltpu.VMEM((tm, D), F32)],
        compiler_params=_params(("parallel", "arbitrary")),
    )(dxo, x, y, dgf, silu, g_pre, wg, wu, wd, g_post, _after(dep))


def _wgrad(xm, ym, name, shard_cols=False, dep=None):
    T, M = xm.shape
    _, N = ym.shape
    assert M * N * 4 <= 16 * 1024 * 1024, (M, N)
    tk = _blk(T, 512)
    n_k = T // tk
    fs = N // N_DEV

    def body(x_ref, y_ref, _, o_ref, acc_scr):
        k = pl.program_id(0)

        @pl.when(k == 0)
        def _():
            acc_scr[...] = jnp.zeros_like(acc_scr)

        acc_scr[...] += _mm_tn(x_ref[...], y_ref[...])

        @pl.when(k == n_k - 1)
        def _():
            if shard_cols:
                for s in range(N_DEV):
                    o_ref[s] = acc_scr[:, s * fs:(s + 1) * fs].astype(BF16)
            else:
                o_ref[...] = acc_scr[...].astype(BF16)

    if shard_cols:
        out_spec = pl.BlockSpec((N_DEV, M, fs), lambda k: (0, 0, 0), pipeline_mode=pl.Buffered(1))
        out_shape = jax.ShapeDtypeStruct((N_DEV, M, fs), BF16)
    else:
        out_spec = pl.BlockSpec((M, N), lambda k: (0, 0), pipeline_mode=pl.Buffered(1))
        out_shape = jax.ShapeDtypeStruct((M, N), BF16)
    return pl.pallas_call(
        body, name=name, grid=(n_k,),
        in_specs=[pl.BlockSpec((tk, M), lambda k: (k, 0)), pl.BlockSpec((tk, N), lambda k: (k, 0)), ANY],
        out_specs=out_spec, out_shape=out_shape,
        scratch_shapes=[pltpu.VMEM((M, N), F32)],
        compiler_params=_params(("arbitrary",)),
    )(xm, ym, _after(dep))


def _mix_in_fwd(x1, g, w7, wf, name):
    T, D = x1.shape
    n_seg, _, W = w7.shape
    tm = _blk(T, 1024)

    def body(x_ref, g_ref, w_ref, wf_ref, z_ref, f_ref, hb_ref, h_scr):
        s = pl.program_id(1)

        @pl.when(s == 0)
        def _():
            xv = x_ref[...]
            h = (xv * _rms_scale(xv) * g_ref[...]).astype(BF16)
            h_scr[...] = h
            hb_ref[...] = h
            f_ref[...] = _mm(h, wf_ref[...])

        z_ref[...] = _mm(h_scr[...], w_ref[...]).astype(BF16)

    return pl.pallas_call(
        body, name=name, grid=(T // tm, n_seg),
        in_specs=[pl.BlockSpec((tm, D), lambda i, s: (i, 0)),
                  pl.BlockSpec((1, D), lambda i, s: (0, 0)),
                  pl.BlockSpec((None, D, W), lambda i, s: (s, 0, 0)),
                  pl.BlockSpec((D, LANES), lambda i, s: (0, 0))],
        out_specs=[pl.BlockSpec((None, tm, W), lambda i, s: (s, i, 0)),
                   pl.BlockSpec((tm, LANES), lambda i, s: (i, 0)),
                   pl.BlockSpec((tm, D), lambda i, s: (i, 0))],
        out_shape=[jax.ShapeDtypeStruct((n_seg, T, W), BF16), jax.ShapeDtypeStruct((T, LANES), F32),
                   jax.ShapeDtypeStruct((T, D), BF16)],
        scratch_shapes=[pltpu.VMEM((tm, D), BF16)],
        compiler_params=_params(("parallel", "arbitrary")),
    )(x1, g, w7, wf)


def _mix_in_bwd(dx2, x1, g, segs, dfb, w7, wf, name):
    T, D = x1.shape
    n_seg, _, W = w7.shape
    tm = _blk(T, 512)
    n_i = T // tm

    def body(*refs):
        dx2_ref, x_ref, g_ref = refs[:3]
        seg_refs = refs[3:3 + n_seg]
        df_ref, w_ref, wf_ref, dx1_ref, dg_ref, acc_scr = refs[3 + n_seg:]
        s = pl.program_id(1)

        @pl.when(s == 0)
        def _():
            acc_scr[...] = _mm_nt(df_ref[...], wf_ref[...])

        for q in range(n_seg):
            @pl.when(s == q)
            def _(q=q):
                acc_scr[...] += _mm_nt(seg_refs[q][...], w_ref[...])

        @pl.when(s == n_seg - 1)
        def _():
            xv = x_ref[...]
            r = _rms_scale(xv)
            xhat = xv * r
            dh = acc_scr[...]
            dg_ref[...] = jnp.sum(dh * xhat, axis=0, keepdims=True)
            dx1_ref[...] = _rms_bwd(dh, xhat, r, g_ref[...]) + dx2_ref[...]

    row = pl.BlockSpec((tm, D), lambda i, s: (i, 0))
    seg_specs = []
    seg_args = []
    for arr, idx in segs:
        if idx is None:
            seg_specs.append(pl.BlockSpec((tm, W), lambda i, s: (i, 0)))
        else:
            seg_specs.append(pl.BlockSpec((None, tm, W), lambda i, s, idx=idx: (idx, i, 0)))
        seg_args.append(arr)
    return pl.pallas_call(
        body, name=name, grid=(n_i, n_seg),
        in_specs=[row, row, pl.BlockSpec((1, D), lambda i, s: (0, 0))] + seg_specs + [
            pl.BlockSpec((tm, LANES), lambda i, s: (i, 0)),
            pl.BlockSpec((None, D, W), lambda i, s: (s, 0, 0)),
            pl.BlockSpec((D, LANES), lambda i, s: (0, 0))],
        out_specs=[row, pl.BlockSpec((None, 1, D), lambda i, s: (i, 0, 0))],
        out_shape=[jax.ShapeDtypeStruct((T, D), F32), jax.ShapeDtypeStruct((n_i, 1, D), F32)],
        scratch_shapes=[pltpu.VMEM((tm, D), F32)],
        compiler_params=_params(("parallel", "arbitrary")),
    )(dx2, x1, g, *seg_args, dfb, w7, wf)


def _forget_cumsum(f, b_pad, name):
    T, L = f.shape
    tb = _blk(T, 256)

    def body(f_ref, b_ref, c_ref, carry):
        @pl.when(pl.program_id(0) == 0)
        def _():
            carry[...] = jnp.zeros_like(carry)

        lf = jax.nn.log_sigmoid(f_ref[...] + b_ref[...])
        rows = lax.broadcasted_iota(jnp.int32, (tb, tb), 0)
        cols = lax.broadcasted_iota(jnp.int32, (tb, tb), 1)
        tri = (cols <= rows).astype(F32)
        c = jnp.dot(tri, lf, preferred_element_type=F32, precision=lax.Precision.HIGHEST) + carry[...]
        c_ref[...] = c
        carry[...] = c[tb - 1:tb, :]

    return pl.pallas_call(
        body, name=name, grid=(T // tb,),
        in_specs=[pl.BlockSpec((tb, L), lambda i: (i, 0)), pl.BlockSpec((1, L), lambda i: (0, 0))],
        out_specs=pl.BlockSpec((tb, L), lambda i: (i, 0)),
        out_shape=jax.ShapeDtypeStruct((T, L), F32),
        scratch_shapes=[pltpu.VMEM((1, L), F32)],
        compiler_params=_params(("arbitrary",)),
    )(f, b_pad)


def _forget_bwd(dc, f, b_pad, name):
    T, L = f.shape
    tb = _blk(T, 256)
    nb = T // tb

    def body(dc_ref, f_ref, b_ref, df_ref, db_ref, carry):
        @pl.when(pl.program_id(0) == 0)
        def _():
            carry[...] = jnp.zeros_like(carry)
            db_ref[...] = jnp.zeros_like(db_ref)

        rows = lax.broadcasted_iota(jnp.int32, (tb, tb), 0)
        cols = lax.broadcasted_iota(jnp.int32, (tb, tb), 1)
        tri = (cols >= rows).astype(F32)
        r = jnp.dot(tri, dc_ref[...], preferred_element_type=F32, precision=lax.Precision.HIGHEST) + carry[...]
        carry[...] = r[0:1, :]
        df = r * (1.0 - jax.nn.sigmoid(f_ref[...] + b_ref[...]))
        df_ref[...] = df.astype(BF16)
        db_ref[...] += jnp.sum(df, axis=0, keepdims=True)

    rev = pl.BlockSpec((tb, L), lambda i: (nb - 1 - i, 0))
    one = pl.BlockSpec((1, L), lambda i: (0, 0))
    return pl.pallas_call(
        body, name=name, grid=(nb,),
        in_specs=[rev, rev, one], out_specs=[rev, one],
        out_shape=[jax.ShapeDtypeStruct((T, L), BF16), jax.ShapeDtypeStruct((1, L), F32)],
        scratch_shapes=[pltpu.VMEM((1, L), F32)],
        compiler_params=_params(("arbitrary",)),
    )(dc, f, b_pad)


def _attn_fwd(z7, c_row, name):
    _, T, W = z7.shape
    H = W // HEAD_DIM
    ta = _blk(T, 512)
    nq = T // ta
    scale = np.float32(1.0 / np.sqrt(HEAD_DIM))

    def body(q_ref, k_ref, v_ref, crow_ref, o_ref, lse_ref, m_scr, l_scr, acc_scr):
        i = pl.program_id(1)
        j = pl.program_id(2)

        @pl.when(j == 0)
        def _():
            m_scr[...] = jnp.full_like(m_scr, NEG_BIG)
            l_scr[...] = jnp.zeros_like(l_scr)
            acc_scr[...] = jnp.zeros_like(acc_scr)

        def step(diagonal):
            s = _mm_nt(q_ref[...], k_ref[...]) * scale - crow_ref[...]
            if diagonal:
                rows = lax.broadcasted_iota(jnp.int32, (ta, ta), 0)
                cols = lax.broadcasted_iota(jnp.int32, (ta, ta), 1)
                s = jnp.where(cols <= rows, s, NEG_BIG)
            m_prev = m_scr[...]
            m_new = jnp.maximum(m_prev, jnp.max(s, axis=-1, keepdims=True))
            alpha = jnp.exp(m_prev - m_new)
            p = jnp.exp(s - m_new)
            l_scr[...] = alpha * l_scr[...] + jnp.sum(p, axis=-1, keepdims=True)
            acc_scr[...] = alpha * acc_scr[...] + _mm(p.astype(BF16), v_ref[...])
            m_scr[...] = m_new

        @pl.when(j < i)
        def _():
            step(False)

        @pl.when(j == i)
        def _():
            step(True)
            l = l_scr[...]
            o_ref[...] = acc_scr[...] / l
            lse_ref[...] = m_scr[...] + jnp.log(l)

    return pl.pallas_call(
        body, name=name, grid=(H, nq, nq),
        in_specs=[pl.BlockSpec((None, ta, HEAD_DIM), lambda h, i, j: (0, i, h)),
                  pl.BlockSpec((None, ta, HEAD_DIM), lambda h, i, j: (1, jnp.minimum(i, j), h)),
                  pl.BlockSpec((None, ta, HEAD_DIM), lambda h, i, j: (2, jnp.minimum(i, j), h)),
                  pl.BlockSpec((None, 1, ta), lambda h, i, j: (h, 0, jnp.minimum(i, j)))],
        out_specs=[pl.BlockSpec((ta, HEAD_DIM), lambda h, i, j: (i, h)),
                   pl.BlockSpec((None, ta, 1), lambda h, i, j: (h, i, 0))],
        out_shape=[jax.ShapeDtypeStruct((T, W), F32), jax.ShapeDtypeStruct((H, T, 1), F32)],
        scratch_shapes=[pltpu.VMEM((ta, 1), F32), pltpu.VMEM((ta, 1), F32), pltpu.VMEM((ta, HEAD_DIM), F32)],
        compiler_params=_params(("parallel", "parallel", "arbitrary")),
    )(z7, z7, z7, c_row)


def _attn_bwd_kv(z7, dob, c_col, lse_row, d_row, name):
    _, T, W = z7.shape
    H = W // HEAD_DIM
    ta = _blk(T, 512)
    nq = T // ta
    scale = np.float32(1.0 / np.sqrt(HEAD_DIM))

    def body(k_ref, v_ref, q_ref, do_ref, ccol_ref, lse_ref, d_ref, dk_ref, dv_ref, dc_ref, dk_scr, dv_scr, dc_scr):
        j = pl.program_id(1)
        i = pl.program_id(2)

        @pl.when(i == 0)
        def _():
            dk_scr[...] = jnp.zeros_like(dk_scr)
            dv_scr[...] = jnp.zeros_like(dv_scr)
            dc_scr[...] = jnp.zeros_like(dc_scr)

        def step(diagonal):
            q = q_ref[...]
            do = do_ref[...]
            st = _mm_nt(k_ref[...], q) * scale - ccol_ref[...] - lse_ref[...]
            if diagonal:
                rows = lax.broadcasted_iota(jnp.int32, (ta, ta), 0)
                cols = lax.broadcasted_iota(jnp.int32, (ta, ta), 1)
                st = jnp.where(rows <= cols, st, NEG_BIG)
            pt = jnp.exp(st)
            dv_scr[...] += _mm(pt.astype(BF16), do)
            dst = pt * (_mm_nt(v_ref[...], do) - d_ref[...])
            dk_scr[...] += _mm(dst.astype(BF16), q)
            dc_scr[...] += jnp.sum(dst, axis=-1, keepdims=True)

        @pl.when(i > j)
        def _():
            step(False)

        @pl.when(i == j)
        def _():
            step(True)

        @pl.when(i == nq - 1)
        def _():
            dk_ref[...] = (dk_scr[...] * scale).astype(BF16)
            dv_ref[...] = dv_scr[...].astype(BF16)
            dc_ref[...] = -dc_scr[...]

    return pl.pallas_call(
        body, name=name, grid=(H, nq, nq),
        in_specs=[pl.BlockSpec((None, ta, HEAD_DIM), lambda h, j, i: (1, j, h)),
                  pl.BlockSpec((None, ta, HEAD_DIM), lambda h, j, i: (2, j, h)),
                  pl.BlockSpec((None, ta, HEAD_DIM), lambda h, j, i: (0, jnp.maximum(i, j), h)),
                  pl.BlockSpec((ta, HEAD_DIM), lambda h, j, i: (jnp.maximum(i, j), h)),
                  pl.BlockSpec((None, ta, 1), lambda h, j, i: (h, j, 0)),
                  pl.BlockSpec((None, 1, ta), lambda h, j, i: (h, 0, jnp.maximum(i, j))),
                  pl.BlockSpec((None, 1, ta), lambda h, j, i: (h, 0, jnp.maximum(i, j)))],
        out_specs=[pl.BlockSpec((ta, HEAD_DIM), lambda h, j, i: (j, h)),
                   pl.BlockSpec((ta, HEAD_DIM), lambda h, j, i: (j, h)),
                   pl.BlockSpec((None, ta, 1), lambda h, j, i: (h, j, 0))],
        out_shape=[jax.ShapeDtypeStruct((T, W), BF16), jax.ShapeDtypeStruct((T, W), BF16),
                   jax.ShapeDtypeStruct((H, T, 1), F32)],
        scratch_shapes=[pltpu.VMEM((ta, HEAD_DIM), F32), pltpu.VMEM((ta, HEAD_DIM), F32), pltpu.VMEM((ta, 1), F32)],
        compiler_params=_params(("parallel", "parallel", "arbitrary")),
    )(z7, z7, z7, dob, c_col, lse_row, d_row)


def _attn_bwd_q(z7, dob, c_row, lse_col, d_col, name):
    _, T, W = z7.shape
    H = W // HEAD_DIM
    ta = _blk(T, 512)
    nq = T // ta
    scale = np.float32(1.0 / np.sqrt(HEAD_DIM))

    def body(q_ref, k_ref, v_ref, do_ref, crow_ref, lse_ref, d_ref, dq_ref, dc_ref, dq_scr, dc_scr):
        i = pl.program_id(1)
        j = pl.program_id(2)

        @pl.when(j == 0)
        def _():
            dq_scr[...] = jnp.zeros_like(dq_scr)
            dc_scr[...] = jnp.zeros_like(dc_scr)

        def step(diagonal):
            k = k_ref[...]
            do = do_ref[...]
            s = _mm_nt(q_ref[...], k) * scale - crow_ref[...] - lse_ref[...]
            if diagonal:
                rows = lax.broadcasted_iota(jnp.int32, (ta, ta), 0)
                cols = lax.broadcasted_iota(jnp.int32, (ta, ta), 1)
                s = jnp.where(cols <= rows, s, NEG_BIG)
            p = jnp.exp(s)
            ds = p * (_mm_nt(do, v_ref[...]) - d_ref[...])
            dq_scr[...] += _mm(ds.astype(BF16), k)
            dc_scr[...] += jnp.sum(ds, axis=-1, keepdims=True)

        @pl.when(j < i)
        def _():
            step(False)

        @pl.when(j == i)
        def _():
            step(True)
            dq_ref[...] = (dq_scr[...] * scale).astype(BF16)
            dc_ref[...] = dc_scr[...]

    return pl.pallas_call(
        body, name=name, grid=(H, nq, nq),
        in_specs=[pl.BlockSpec((None, ta, HEAD_DIM), lambda h, i, j: (0, i, h)),
                  pl.BlockSpec((None, ta, HEAD_DIM), lambda h, i, j: (1, jnp.minimum(i, j), h)),
                  pl.BlockSpec((None, ta, HEAD_DIM), lambda h, i, j: (2, jnp.minimum(i, j), h)),
                  pl.BlockSpec((ta, HEAD_DIM), lambda h, i, j: (i, h)),
                  pl.BlockSpec((None, 1, ta), lambda h, i, j: (h, 0, jnp.minimum(i, j))),
                  pl.BlockSpec((None, ta, 1), lambda h, i, j: (h, i, 0)),
                  pl.BlockSpec((None, ta, 1), lambda h, i, j: (h, i, 0))],
        out_specs=[pl.BlockSpec((ta, HEAD_DIM), lambda h, i, j: (i, h)),
                   pl.BlockSpec((None, ta, 1), lambda h, i, j: (h, i, 0))],
        out_shape=[jax.ShapeDtypeStruct((T, W), BF16), jax.ShapeDtypeStruct((H, T, 1), F32)],
        scratch_shapes=[pltpu.VMEM((ta, HEAD_DIM), F32), pltpu.VMEM((ta, 1), F32)],
        compiler_params=_params(("parallel", "parallel", "arbitrary")),
    )(z7, z7, z7, dob, c_row, lse_col, d_col)


ATTN_TILE = 512
ATTN_CHAINS = 2


def _attn_geometry(T):
    ta = _blk(T, ATTN_TILE)
    nc = ATTN_CHAINS if (T // ta) % ATTN_CHAINS == 0 else 1
    return ta, nc, T // ta


def _causal_tile(ta, keys_on_rows=False):
    rows = lax.broadcasted_iota(jnp.int32, (ta, ta), 0)
    cols = lax.broadcasted_iota(jnp.int32, (ta, ta), 1)
    return rows <= cols if keys_on_rows else cols <= rows


def _chunk(ref, j, ta):
    return ref[pl.ds(pl.multiple_of(j * ta, ta), ta), :]


def _attn_fwd_loop(z7, c_chunks, name):
    _, T, W = z7.shape
    H = W // HEAD_DIM
    ta, nc, n_chunks = _attn_geometry(T)
    scale = np.float32(1.0 / np.sqrt(HEAD_DIM))

    def body(q_ref, k_ref, v_ref, c_ref, o_ref, lse_ref, m_scr, l_scr, acc_scr):
        g = pl.program_id(1)
        m_scr[...] = jnp.full_like(m_scr, NEG_BIG)
        l_scr[...] = jnp.zeros_like(l_scr)
        acc_scr[...] = jnp.zeros_like(acc_scr)

        def update(ch, k, v, crow, diagonal):
            q = q_ref[ch * ta:(ch + 1) * ta, :]
            s = _mm_nt(q, k) * scale - crow
            if diagonal:
                s = jnp.where(_causal_tile(ta), s, NEG_BIG)
            m_prev = m_scr[ch]
            m_new = jnp.maximum(m_prev, jnp.max(s, axis=-1, keepdims=True))
            alpha = jnp.exp(m_prev - m_new)
            p = jnp.exp(s - m_new)
            l_scr[ch] = alpha * l_scr[ch] + jnp.sum(p, axis=-1, keepdims=True)
            acc_scr[ch] = alpha * acc_scr[ch] + _mm(p.astype(BF16), v)
            m_scr[ch] = m_new

        def full_chunk(j, carry):
            k = _chunk(k_ref, j, ta)
            v = _chunk(v_ref, j, ta)
            crow = c_ref[j]
            for ch in range(nc):
                update(ch, k, v, crow, False)
            return carry

        lax.fori_loop(0, nc * g, full_chunk, 0)
        for jj in range(nc):
            j = nc * g + jj
            k = _chunk(k_ref, j, ta)
            v = _chunk(v_ref, j, ta)
            crow = c_ref[j]
            for ch in range(jj, nc):
                update(ch, k, v, crow, ch == jj)
        for ch in range(nc):
            l = l_scr[ch]
            o_ref[ch * ta:(ch + 1) * ta, :] = acc_scr[ch] / l
            lse_ref[ch * ta:(ch + 1) * ta, :] = m_scr[ch] + jnp.log(l)

    tq = nc * ta
    return pl.pallas_call(
        body, name=name, grid=(H, n_chunks // nc),
        in_specs=[pl.BlockSpec((None, tq, HEAD_DIM), lambda h, g: (0, g, h)),
                  pl.BlockSpec((None, T, HEAD_DIM), lambda h, g: (1, 0, h)),
                  pl.BlockSpec((None, T, HEAD_DIM), lambda h, g: (2, 0, h)),
                  pl.BlockSpec((None, n_chunks, 1, ta), lambda h, g: (h, 0, 0, 0))],
        out_specs=[pl.BlockSpec((tq, HEAD_DIM), lambda h, g: (g, h)),
                   pl.BlockSpec((None, tq, 1), lambda h, g: (h, g, 0))],
        out_shape=[jax.ShapeDtypeStruct((T, W), F32), jax.ShapeDtypeStruct((H, T, 1), F32)],
        scratch_shapes=[pltpu.VMEM((nc, ta, 1), F32), pltpu.VMEM((nc, ta, 1), F32),
                        pltpu.VMEM((nc, ta, HEAD_DIM), F32)],
        compiler_params=_params(("parallel", "arbitrary")),
    )(z7, z7, z7, c_chunks)


def _attn_fwd_keys_on_rows(z7, vt, c_rep, name):
    _, T, W = z7.shape
    H = W // HEAD_DIM
    ta, nc, n_chunks = _attn_geometry(T)
    scale = np.float32(1.0 / np.sqrt(HEAD_DIM))
    reps = ta // LANES

    def body(q_ref, k_ref, vt_ref, c_ref, o_ref, lse_ref, m_scr, l_scr, acc_scr):
        g = pl.program_id(1)
        m_scr[...] = jnp.full_like(m_scr, NEG_BIG)
        l_scr[...] = jnp.zeros_like(l_scr)
        acc_scr[...] = jnp.zeros_like(acc_scr)

        def update(ch, k, vt, cj, diagonal):
            q = q_ref[ch * ta:(ch + 1) * ta, :]
            st = _mm_nt(k, q) * scale - cj
            if diagonal:
                st = jnp.where(_causal_tile(ta, keys_on_rows=True), st, NEG_BIG)
            m_prev = m_scr[ch]
            m_new = jnp.maximum(m_prev, jnp.max(st, axis=0, keepdims=True))
            alpha = jnp.exp(m_prev - m_new)
            pt = jnp.exp(st - m_new)
            l_scr[ch] = alpha * l_scr[ch] + jnp.sum(pt, axis=0, keepdims=True)
            acc_scr[ch] = alpha * acc_scr[ch] + _mm(vt, pt.astype(BF16))
            m_scr[ch] = m_new

        def load(j):
            cj = _chunk(c_ref, j, ta)
            return _chunk(k_ref, j, ta), vt_ref[j], jnp.concatenate([cj] * reps, axis=1)

        def full_chunk(j, carry):
            k, vt, cj = load(j)
            for ch in range(nc):
                update(ch, k, vt, cj, False)
            return carry

        lax.fori_loop(0, nc * g, full_chunk, 0)
        for jj in range(nc):
            k, vt, cj = load(nc * g + jj)
            for ch in range(jj, nc):
                update(ch, k, vt, cj, ch == jj)
        for ch in range(nc):
            l = l_scr[ch]
            o_ref[ch * ta:(ch + 1) * ta, :] = (acc_scr[ch] / l).T
            lse_ref[ch] = m_scr[ch] + jnp.log(l)

    tq = nc * ta
    return pl.pallas_call(
        body, name=name, grid=(H, n_chunks // nc),
        in_specs=[pl.BlockSpec((None, tq, HEAD_DIM), lambda h, g: (0, g, h)),
                  pl.BlockSpec((None, T, HEAD_DIM), lambda h, g: (1, 0, h)),
                  pl.BlockSpec((None, n_chunks, HEAD_DIM, ta), lambda h, g: (h, 0, 0, 0)),
                  pl.BlockSpec((None, T, LANES), lambda h, g: (h, 0, 0))],
        out_specs=[pl.BlockSpec((tq, HEAD_DIM), lambda h, g: (g, h)),
                   pl.BlockSpec((None, nc, 1, ta), lambda h, g: (h, g, 0, 0))],
        out_shape=[jax.ShapeDtypeStruct((T, W), F32), jax.ShapeDtypeStruct((H, n_chunks, 1, ta), F32)],
        scratch_shapes=[pltpu.VMEM((nc, 1, ta), F32), pltpu.VMEM((nc, 1, ta), F32),
                        pltpu.VMEM((nc, HEAD_DIM, ta), F32)],
        compiler_params=_params(("parallel", "arbitrary")),
    )(z7, z7, vt, c_rep)


def _attn_bwd_fused(z7, kt, dob, c_rep, lse_chunks, d_chunks, name):
    _, T, W = z7.shape
    H = W // HEAD_DIM
    ta, nc, n_chunks = _attn_geometry(T)
    n_steps = n_chunks // nc
    scale = np.float32(1.0 / np.sqrt(HEAD_DIM))
    reps = ta // LANES

    def body(k_ref, v_ref, kt_ref, q_ref, do_ref, c_ref, lse_ref, d_ref,
             dk_ref, dv_ref, dck_ref, dq_ref, dcq_ref, dk_scr, dv_scr, dck_scr, dqt_scr, dcq_scr):
        g = pl.program_id(1)

        @pl.when(g == 0)
        def _():
            dqt_scr[...] = jnp.zeros_like(dqt_scr)
            dcq_scr[...] = jnp.zeros_like(dcq_scr)

        dk_scr[...] = jnp.zeros_like(dk_scr)
        dv_scr[...] = jnp.zeros_like(dv_scr)
        dck_scr[...] = jnp.zeros_like(dck_scr)

        def update(ch, i, q, do, diagonal):
            rows = slice(ch * ta, (ch + 1) * ta)
            cj = c_ref[rows, :]
            st = _mm_nt(k_ref[rows, :], q) * scale - jnp.concatenate([cj] * reps, axis=1) - lse_ref[i]
            if diagonal:
                st = jnp.where(_causal_tile(ta, keys_on_rows=True), st, NEG_BIG)
            pt = jnp.exp(st)
            dv_scr[ch] += _mm(pt.astype(BF16), do)
            dst = pt * (_mm_nt(v_ref[rows, :], do) - d_ref[i])
            dst_b = dst.astype(BF16)
            dk_scr[ch] += _mm(dst_b, q)
            dqt_scr[i] += _mm(kt_ref[ch], dst_b)
            dcq_scr[i] += jnp.sum(dst, axis=0, keepdims=True)
            lane_sum = dst[:, :LANES]
            for r in range(1, reps):
                lane_sum = lane_sum + dst[:, r * LANES:(r + 1) * LANES]
            dck_scr[ch] += lane_sum

        for ii in range(nc):
            i = nc * g + ii
            q = _chunk(q_ref, i, ta)
            do = _chunk(do_ref, i, ta)
            for ch in range(0, ii + 1):
                update(ch, i, q, do, ch == ii)

        def full_chunk(i, carry):
            q = _chunk(q_ref, i, ta)
            do = _chunk(do_ref, i, ta)
            for ch in range(nc):
                update(ch, i, q, do, False)
            return carry

        lax.fori_loop(nc * (g + 1), n_chunks, full_chunk, 0)
        for ch in range(nc):
            rows = slice(ch * ta, (ch + 1) * ta)
            dk_ref[rows, :] = (dk_scr[ch] * scale).astype(BF16)
            dv_ref[rows, :] = dv_scr[ch].astype(BF16)
            dck_ref[rows, :] = -jnp.sum(dck_scr[ch], axis=-1, keepdims=True)

        @pl.when(g == n_steps - 1)
        def _():
            for i in range(n_chunks):
                dq_ref[i * ta:(i + 1) * ta, :] = (dqt_scr[i] * scale).T.astype(BF16)
            dcq_ref[...] = dcq_scr[...]

    tk = nc * ta
    chunks = pl.BlockSpec((None, n_chunks, 1, ta), lambda h, g: (h, 0, 0, 0))
    tile = pl.BlockSpec((tk, HEAD_DIM), lambda h, g: (g, h))
    return pl.pallas_call(
        body, name=name, grid=(H, n_steps),
        in_specs=[pl.BlockSpec((None, tk, HEAD_DIM), lambda h, g: (1, g, h)),
                  pl.BlockSpec((None, tk, HEAD_DIM), lambda h, g: (2, g, h)),
                  pl.BlockSpec((None, nc, HEAD_DIM, ta), lambda h, g: (h, g, 0, 0)),
                  pl.BlockSpec((None, T, HEAD_DIM), lambda h, g: (0, 0, h)),
                  pl.BlockSpec((T, HEAD_DIM), lambda h, g: (0, h)),
                  pl.BlockSpec((None, tk, LANES), lambda h, g: (h, g, 0)),
                  chunks, chunks],
        out_specs=[tile, tile, pl.BlockSpec((None, tk, 1), lambda h, g: (h, g, 0)),
                   pl.BlockSpec((T, HEAD_DIM), lambda h, g: (0, h)), chunks],
        out_shape=[jax.ShapeDtypeStruct((T, W), BF16), jax.ShapeDtypeStruct((T, W), BF16),
                   jax.ShapeDtypeStruct((H, T, 1), F32), jax.ShapeDtypeStruct((T, W), BF16),
                   jax.ShapeDtypeStruct((H, n_chunks, 1, ta), F32)],
        scratch_shapes=[pltpu.VMEM((nc, ta, HEAD_DIM), F32), pltpu.VMEM((nc, ta, HEAD_DIM), F32),
                        pltpu.VMEM((nc, ta, LANES), F32), pltpu.VMEM((n_chunks, HEAD_DIM, ta), F32),
                        pltpu.VMEM((n_chunks, 1, ta), F32)],
        compiler_params=_params(("parallel", "arbitrary")),
    )(z7, z7, kt, z7, dob, c_rep, lse_chunks, d_chunks)


def _attn_bwd_q_loop(z7, dob, c_chunks, lse_col, d_col, name):
    _, T, W = z7.shape
    H = W // HEAD_DIM
    ta, nc, n_chunks = _attn_geometry(T)
    scale = np.float32(1.0 / np.sqrt(HEAD_DIM))

    def body(q_ref, k_ref, v_ref, do_ref, c_ref, lse_ref, d_ref, dq_ref, dc_ref, dq_scr, dc_scr):
        g = pl.program_id(1)
        dq_scr[...] = jnp.zeros_like(dq_scr)
        dc_scr[...] = jnp.zeros_like(dc_scr)

        def update(ch, k, v, crow, diagonal):
            rows = slice(ch * ta, (ch + 1) * ta)
            do = do_ref[rows, :]
            s = _mm_nt(q_ref[rows, :], k) * scale - crow - lse_ref[rows, :]
            if diagonal:
                s = jnp.where(_causal_tile(ta), s, NEG_BIG)
            p = jnp.exp(s)
            ds = p * (_mm_nt(do, v) - d_ref[rows, :])
            dq_scr[ch] += _mm(ds.astype(BF16), k)
            dc_scr[ch] += jnp.sum(ds, axis=-1, keepdims=True)

        def full_chunk(j, carry):
            k = _chunk(k_ref, j, ta)
            v = _chunk(v_ref, j, ta)
            crow = c_ref[j]
            for ch in range(nc):
                update(ch, k, v, crow, False)
            return carry

        lax.fori_loop(0, nc * g, full_chunk, 0)
        for jj in range(nc):
            j = nc * g + jj
            k = _chunk(k_ref, j, ta)
            v = _chunk(v_ref, j, ta)
            crow = c_ref[j]
            for ch in range(jj, nc):
                update(ch, k, v, crow, ch == jj)
        for ch in range(nc):
            dq_ref[ch * ta:(ch + 1) * ta, :] = (dq_scr[ch] * scale).astype(BF16)
            dc_ref[ch * ta:(ch + 1) * ta, :] = dc_scr[ch]

    tq = nc * ta
    col = pl.BlockSpec((None, tq, 1), lambda h, g: (h, g, 0))
    return pl.pallas_call(
        body, name=name, grid=(H, n_chunks // nc),
        in_specs=[pl.BlockSpec((None, tq, HEAD_DIM), lambda h, g: (0, g, h)),
                  pl.BlockSpec((None, T, HEAD_DIM), lambda h, g: (1, 0, h)),
                  pl.BlockSpec((None, T, HEAD_DIM), lambda h, g: (2, 0, h)),
                  pl.BlockSpec((tq, HEAD_DIM), lambda h, g: (g, h)),
                  pl.BlockSpec((None, n_chunks, 1, ta), lambda h, g: (h, 0, 0, 0)),
                  col, col],
        out_specs=[pl.BlockSpec((tq, HEAD_DIM), lambda h, g: (g, h)), col],
        out_shape=[jax.ShapeDtypeStruct((T, W), BF16), jax.ShapeDtypeStruct((H, T, 1), F32)],
        scratch_shapes=[pltpu.VMEM((nc, ta, HEAD_DIM), F32), pltpu.VMEM((nc, ta, 1), F32)],
        compiler_params=_params(("parallel", "arbitrary")),
    )(z7, z7, z7, dob, c_chunks, lse_col, d_col)


def _attn_bwd_kv_loop(z7, dob, c_col, lse_chunks, d_chunks, name):
    _, T, W = z7.shape
    H = W // HEAD_DIM
    ta, nc, n_chunks = _attn_geometry(T)
    scale = np.float32(1.0 / np.sqrt(HEAD_DIM))

    def body(k_ref, v_ref, q_ref, do_ref, ccol_ref, lse_ref, d_ref, dk_ref, dv_ref, dc_ref, dk_scr, dv_scr, dc_scr):
        g = pl.program_id(1)
        dk_scr[...] = jnp.zeros_like(dk_scr)
        dv_scr[...] = jnp.zeros_like(dv_scr)
        dc_scr[...] = jnp.zeros_like(dc_scr)

        def update(ch, q, do, lse_row, d_row, diagonal):
            rows = slice(ch * ta, (ch + 1) * ta)
            st = _mm_nt(k_ref[rows, :], q) * scale - ccol_ref[rows, :] - lse_row
            if diagonal:
                st = jnp.where(_causal_tile(ta, keys_on_rows=True), st, NEG_BIG)
            pt = jnp.exp(st)
            dv_scr[ch] += _mm(pt.astype(BF16), do)
            dst = pt * (_mm_nt(v_ref[rows, :], do) - d_row)
            dk_scr[ch] += _mm(dst.astype(BF16), q)
            dc_scr[ch] += jnp.sum(dst, axis=-1, keepdims=True)

        for ii in range(nc):
            i = nc * g + ii
            q = _chunk(q_ref, i, ta)
            do = _chunk(do_ref, i, ta)
            for ch in range(0, ii + 1):
                update(ch, q, do, lse_ref[i], d_ref[i], ch == ii)

        def full_chunk(i, carry):
            q = _chunk(q_ref, i, ta)
            do = _chunk(do_ref, i, ta)
            for ch in range(nc):
                update(ch, q, do, lse_ref[i], d_ref[i], False)
            return carry

        lax.fori_loop(nc * (g + 1), n_chunks, full_chunk, 0)
        for ch in range(nc):
            rows = slice(ch * ta, (ch + 1) * ta)
            dk_ref[rows, :] = (dk_scr[ch] * scale).astype(BF16)
            dv_ref[rows, :] = dv_scr[ch].astype(BF16)
            dc_ref[rows, :] = -dc_scr[ch]

    tk = nc * ta
    chunks = pl.BlockSpec((None, n_chunks, 1, ta), lambda h, g: (h, 0, 0, 0))
    col = pl.BlockSpec((None, tk, 1), lambda h, g: (h, g, 0))
    tile = pl.BlockSpec((tk, HEAD_DIM), lambda h, g: (g, h))
    return pl.pallas_call(
        body, name=name, grid=(H, n_chunks // nc),
        in_specs=[pl.BlockSpec((None, tk, HEAD_DIM), lambda h, g: (1, g, h)),
                  pl.BlockSpec((None, tk, HEAD_DIM), lambda h, g: (2, g, h)),
                  pl.BlockSpec((None, T, HEAD_DIM), lambda h, g: (0, 0, h)),
                  pl.BlockSpec((T, HEAD_DIM), lambda h, g: (0, h)),
                  col, chunks, chunks],
        out_specs=[tile, tile, col],
        out_shape=[jax.ShapeDtypeStruct((T, W), BF16), jax.ShapeDtypeStruct((T, W), BF16),
                   jax.ShapeDtypeStruct((H, T, 1), F32)],
        scratch_shapes=[pltpu.VMEM((nc, ta, HEAD_DIM), F32), pltpu.VMEM((nc, ta, HEAD_DIM), F32),
                        pltpu.VMEM((nc, ta, 1), F32)],
        compiler_params=_params(("parallel", "arbitrary")),
    )(z7, z7, z7, dob, c_col, lse_chunks, d_chunks)


def _chunk_causal_mask():
    rows = lax.broadcasted_iota(jnp.int32, (SGU_LEN, SGU_LEN), 0)
    cols = lax.broadcasted_iota(jnp.int32, (SGU_LEN, SGU_LEN), 1)
    return (cols // CHUNK) <= (rows // CHUNK)


def _sgu_norm_mix(sv, lng_ref, lnb_ref, ws_ref, bs_ref, vn_scr, mixed_scr, vhat_scr=None):
    tm = sv.shape[0]
    vs = _gelu(sv)
    mask = _chunk_causal_mask()
    rstds = []
    for g in range(N_GROUPS):
        lanes = slice(g * GROUP_DIM, (g + 1) * GROUP_DIM)
        blk = vs[:, lanes]
        cen = blk - jnp.mean(blk, axis=-1, keepdims=True)
        rstd = lax.rsqrt(jnp.mean(cen * cen, axis=-1, keepdims=True) + LN_EPS)
        vhat = cen * rstd
        rstds.append(rstd)
        if vhat_scr is not None:
            vhat_scr[:, lanes] = vhat
        vn_scr[:, lanes] = (vhat * lng_ref[:, lanes] + lnb_ref[:, lanes]).astype(BF16)
        wm = jnp.where(mask, ws_ref[g], 0.0).astype(BF16)
        for w in range(tm // SGU_LEN):
            rows = slice(w * SGU_LEN, (w + 1) * SGU_LEN)
            mixed_scr[rows, lanes] = _mm(wm, vn_scr[rows, lanes]) + bs_ref[g]
    return rstds


def _mix_out_fwd(z7, o_a, x1, lng, lnb, ws, bs, w_out, g_post, name):
    _, T, W = z7.shape
    D = x1.shape[1]
    tm = _blk(T, 256)

    def body(u_ref, sv_ref, ga_ref, gb_ref, oa_ref, x1_ref, lng_ref, lnb_ref, ws_ref, bs_ref, wo_ref, gp_ref,
             x2_ref, p_ref, mb_ref, vn_scr, mixed_scr):
        _sgu_norm_mix(sv_ref[...].astype(F32), lng_ref, lnb_ref, ws_ref, bs_ref, vn_scr, mixed_scr)
        o_b = _gelu(u_ref[...].astype(F32)) * mixed_scr[...]
        merged = (jax.nn.sigmoid(ga_ref[...].astype(F32)) * oa_ref[...]
                  + jax.nn.sigmoid(gb_ref[...].astype(F32)) * o_b).astype(BF16)
        mb_ref[...] = merged
        p = _mm(merged, wo_ref[...])
        p_ref[...] = p
        x2_ref[...] = x1_ref[...] + p * _rms_scale(p) * gp_ref[...]

    def seg(idx):
        return pl.BlockSpec((None, tm, W), lambda i, idx=idx: (idx, i, 0))

    row = pl.BlockSpec((tm, D), lambda i: (i, 0))
    vec = pl.BlockSpec((1, D), lambda i: (0, 0))
    return pl.pallas_call(
        body, name=name, grid=(T // tm,),
        in_specs=[seg(3), seg(4), seg(5), seg(6), row, row, vec, vec,
                  pl.BlockSpec((N_GROUPS, SGU_LEN, SGU_LEN), lambda i: (0, 0, 0)),
                  pl.BlockSpec((N_GROUPS, SGU_LEN, 1), lambda i: (0, 0, 0)),
                  pl.BlockSpec((D, D), lambda i: (0, 0)), vec],
        out_specs=[row, row, row],
        out_shape=[jax.ShapeDtypeStruct((T, D), F32), jax.ShapeDtypeStruct((T, D), F32),
                   jax.ShapeDtypeStruct((T, D), BF16)],
        scratch_shapes=[pltpu.VMEM((tm, W), BF16), pltpu.VMEM((tm, W), F32)],
        compiler_params=_params(("parallel",)),
    )(z7, z7, z7, z7, o_a, x1, lng, lnb, ws, bs, w_out, g_post)


def _mix_out_bwd(dx2, p, z7, o_a, lng, lnb, ws, bs, w_out, g_post, name, dep=None):
    _, T, W = z7.shape
    D = dx2.shape[1]
    tm = _blk(T, 256)
    n_w = tm // SGU_LEN

    def body(dx2_ref, p_ref, u_ref, sv_ref, ga_ref, gb_ref, oa_ref, lng_ref, lnb_ref, ws_ref, bs_ref, wo_ref, gp_ref, _,
             dpb_ref, dob_ref, dvec_ref, dz_ref, dgp_ref, dlng_ref, dlnb_ref, dws_ref, dbs_ref,
             vn_scr, mixed_scr, vhat_scr, dmix_scr, dvn_scr):
        @pl.when(pl.program_id(0) == 0)
        def _():
            dgp_ref[...] = jnp.zeros_like(dgp_ref)
            dlng_ref[...] = jnp.zeros_like(dlng_ref)
            dlnb_ref[...] = jnp.zeros_like(dlnb_ref)
            dws_ref[...] = jnp.zeros_like(dws_ref)
            dbs_ref[...] = jnp.zeros_like(dbs_ref)

        pv = p_ref[...]
        s = _rms_scale(pv)
        n = pv * s
        dn = dx2_ref[...]
        dgp_ref[...] += jnp.sum(dn * n, axis=0, keepdims=True)
        dpb = _rms_bwd(dn, n, s, gp_ref[...]).astype(BF16)
        dpb_ref[...] = dpb
        dmerged = _mm_nt(dpb, wo_ref[...])

        sv = sv_ref[...].astype(F32)
        rstds = _sgu_norm_mix(sv, lng_ref, lnb_ref, ws_ref, bs_ref, vn_scr, mixed_scr, vhat_scr)
        u_pre = u_ref[...].astype(F32)
        u = _gelu(u_pre)
        mixed = mixed_scr[...]
        sa = jax.nn.sigmoid(ga_ref[...].astype(F32))
        sb = jax.nn.sigmoid(gb_ref[...].astype(F32))
        oa = oa_ref[...]
        do_a = (dmerged * sa).astype(BF16)
        dob_ref[...] = do_a
        prod = do_a.astype(F32) * oa
        for h in range(N_HEADS):
            dvec_ref[h] = jnp.sum(prod[:, h * HEAD_DIM:(h + 1) * HEAD_DIM], axis=-1, keepdims=True)
        dz_ref[2] = (dmerged * oa * (sa * (1.0 - sa))).astype(BF16)
        dz_ref[3] = (dmerged * (u * mixed) * (sb * (1.0 - sb))).astype(BF16)
        do_b = dmerged * sb
        dz_ref[0] = (do_b * mixed * _gelu_grad(u_pre)).astype(BF16)
        dmix_scr[...] = do_b * u

        mask = _chunk_causal_mask()
        for g in range(N_GROUPS):
            lanes = slice(g * GROUP_DIM, (g + 1) * GROUP_DIM)
            wm = jnp.where(mask, ws_ref[g], 0.0).astype(BF16)
            dws = jnp.zeros((SGU_LEN, SGU_LEN), F32)
            dbs = jnp.zeros((SGU_LEN, 1), F32)
            for w in range(n_w):
                rows = slice(w * SGU_LEN, (w + 1) * SGU_LEN)
                dmix = dmix_scr[rows, lanes]
                dmix_b = dmix.astype(BF16)
                dvn_scr[rows, lanes] = _mm_tn(wm, dmix_b)
                dws = dws + _mm_nt(dmix_b, vn_scr[rows, lanes])
                dbs = dbs + jnp.sum(dmix, axis=-1, keepdims=True)
            dws_ref[g] += jnp.where(mask, dws, 0.0)
            dbs_ref[g] += dbs
            dvn = dvn_scr[:, lanes]
            vhat = vhat_scr[:, lanes]
            dlng_ref[:, lanes] += jnp.sum(dvn * vhat, axis=0, keepdims=True)
            dlnb_ref[:, lanes] += jnp.sum(dvn, axis=0, keepdims=True)
            dvh = dvn * lng_ref[:, lanes]
            dvs = rstds[g] * (dvh - jnp.mean(dvh, axis=-1, keepdims=True)
                              - vhat * jnp.mean(dvh * vhat, axis=-1, keepdims=True))
            dvn_scr[:, lanes] = dvs
        dz_ref[1] = (dvn_scr[...] * _gelu_grad(sv)).astype(BF16)

    def seg(idx):
        return pl.BlockSpec((None, tm, W), lambda i, idx=idx: (idx, i, 0))

    row = pl.BlockSpec((tm, D), lambda i: (i, 0))
    vec = pl.BlockSpec((1, D), lambda i: (0, 0))
    ws_spec = pl.BlockSpec((N_GROUPS, SGU_LEN, SGU_LEN), lambda i: (0, 0, 0))
    bs_spec = pl.BlockSpec((N_GROUPS, SGU_LEN, 1), lambda i: (0, 0, 0))
    return pl.pallas_call(
        body, name=name, grid=(T // tm,),
        in_specs=[row, row, seg(3), seg(4), seg(5), seg(6), row, vec, vec, ws_spec, bs_spec,
                  pl.BlockSpec((D, D), lambda i: (0, 0)), vec, ANY],
        out_specs=[row, row, pl.BlockSpec((N_HEADS, tm, 1), lambda i: (0, i, 0)),
                   pl.BlockSpec((4, tm, W), lambda i: (0, i, 0)), vec, vec, vec, ws_spec, bs_spec],
        out_shape=[jax.ShapeDtypeStruct((T, D), BF16), jax.ShapeDtypeStruct((T, W), BF16),
                   jax.ShapeDtypeStruct((N_HEADS, T, 1), F32), jax.ShapeDtypeStruct((4, T, W), BF16),
                   jax.ShapeDtypeStruct((1, D), F32), jax.ShapeDtypeStruct((1, D), F32),
                   jax.ShapeDtypeStruct((1, D), F32),
                   jax.ShapeDtypeStruct((N_GROUPS, SGU_LEN, SGU_LEN), F32),
                   jax.ShapeDtypeStruct((N_GROUPS, SGU_LEN, 1), F32)],
        scratch_shapes=[pltpu.VMEM((tm, W), BF16), pltpu.VMEM((tm, W), F32), pltpu.VMEM((tm, W), F32),
                        pltpu.VMEM((tm, W), F32), pltpu.VMEM((tm, W), F32)],
        compiler_params=_params(("arbitrary",)),
    )(dx2, p, z7, z7, z7, z7, o_a, lng, lnb, ws, bs, w_out, g_post, _after(dep))


def _loss_head(y, target, name):
    T, D = y.shape
    tm = _blk(T, 1024)
    n_i = T // tm

    def body(y_ref, t_ref, dy_ref, loss_ref, acc_scr):
        i = pl.program_id(0)

        @pl.when(i == 0)
        def _():
            acc_scr[...] = jnp.zeros_like(acc_scr)

        e = y_ref[...] - t_ref[...]
        dy_ref[...] = e * np.float32(1.0 / D)
        acc_scr[...] += jnp.sum(e * e, axis=0, keepdims=True)

        @pl.when(i == n_i - 1)
        def _():
            total = jnp.sum(acc_scr[...], axis=-1, keepdims=True) * np.float32(0.5 / D)
            loss_ref[...] = jnp.broadcast_to(total, loss_ref.shape)

    row = pl.BlockSpec((tm, D), lambda i: (i, 0))
    return pl.pallas_call(
        body, name=name, grid=(n_i,),
        in_specs=[row, row],
        out_specs=[row, pl.BlockSpec((1, LANES), lambda i: (0, 0))],
        out_shape=[jax.ShapeDtypeStruct((T, D), F32), jax.ShapeDtypeStruct((1, LANES), F32)],
        scratch_shapes=[pltpu.VMEM((1, D), F32)],
        compiler_params=_params(("arbitrary",)),
    )(y, target)


def _adamw_math(w, g, m, v):
    m_new = ADAM_B1 * m + (1.0 - ADAM_B1) * g
    v_new = ADAM_B2 * v + (1.0 - ADAM_B2) * (g * g)
    m_hat = m_new / np.float32(1.0 - ADAM_B1 ** ADAM_STEP)
    v_hat = v_new / np.float32(1.0 - ADAM_B2 ** ADAM_STEP)
    delta = -ADAM_LR * (m_hat / (jnp.sqrt(v_hat) + ADAM_EPS) + ADAM_WD * w)
    return delta, m_new, v_new


def _sum_adamw(parts, w, m, v, name):
    n, R, C = parts.shape
    tr = _blk(R, 128)

    def body(p_ref, w_ref, m_ref, v_ref, g_ref, d_ref, mo_ref, vo_ref):
        g = p_ref[0].astype(F32)
        for s in range(1, n):
            g = g + p_ref[s].astype(F32)
        delta, m_new, v_new = _adamw_math(w_ref[...], g, m_ref[...], v_ref[...])
        g_ref[...] = g
        d_ref[...] = delta
        mo_ref[...] = m_new
        vo_ref[...] = v_new

    row = pl.BlockSpec((tr, C), lambda i: (i, 0))
    shp = jax.ShapeDtypeStruct((R, C), F32)
    return pl.pallas_call(
        body, name=name, grid=(R // tr,),
        in_specs=[pl.BlockSpec((n, tr, C), lambda i: (0, i, 0)), row, row, row],
        out_specs=[row, row, row, row], out_shape=[shp, shp, shp, shp],
        compiler_params=_params(("parallel",)),
    )(parts, w, m, v)


def _adamw(g, w, m, v, name):
    R, C = g.shape
    tr = _blk(R, 128)

    def body(g_ref, w_ref, m_ref, v_ref, d_ref, mo_ref, vo_ref):
        delta, m_new, v_new = _adamw_math(w_ref[...], g_ref[...], m_ref[...], v_ref[...])
        d_ref[...] = delta
        mo_ref[...] = m_new
        vo_ref[...] = v_new

    row = pl.BlockSpec((tr, C), lambda i: (i, 0))
    shp = jax.ShapeDtypeStruct((R, C), F32)
    return pl.pallas_call(
        body, name=name, grid=(R // tr,),
        in_specs=[row, row, row, row], out_specs=[row, row, row], out_shape=[shp, shp, shp],
        compiler_params=_params(("parallel",)),
    )(g, w, m, v)


def _position():
    return lax.axis_index("x"), lax.axis_index("y"), lax.axis_index("c")


def _slot(px, py, pc):
    return 4 * px + 2 * py + pc


def _all_gather(shards, name):
    n = len(shards)

    def body(*refs):
        ins, outs = refs[:n], refs[n:2 * n]
        send_sems, recv_sems, local_sems = refs[2 * n:]
        x, y, c = _position()
        me, sibling = (x, y, c), (x, y, 1 - c)
        chips = [(1 - x, y), (x, 1 - y), (1 - x, 1 - y)]

        def copy(a, k, block, to, src=None):
            dst = outs[a].at[_slot(*block)]
            return pltpu.make_async_remote_copy(
                src_ref=dst if src is None else src, dst_ref=dst,
                send_sem=send_sems.at[a, k], recv_sem=recv_sems.at[a, k],
                device_id=to, device_id_type=MESH)

        mine = [pltpu.make_async_copy(ins[a], outs[a].at[_slot(*me)], local_sems.at[a]) for a in range(n)]
        for cp in mine:
            cp.start()
        first = []
        for a in range(n):
            first.append(copy(a, 0, me, sibling, src=ins[a]))
            first += [copy(a, 1 + j, me, (*chip, c), src=ins[a]) for j, chip in enumerate(chips)]
        for cp in first:
            cp.start()
        passed = []
        for j, chip in enumerate(chips):
            for a in range(n):
                copy(a, 1 + j, (*chip, c), me).wait_recv()
                fwd = copy(a, 4 + j, (*chip, c), sibling)
                fwd.start()
                passed.append(fwd)
        for a in range(n):
            copy(a, 0, sibling, me).wait_recv()
            for j, chip in enumerate(chips):
                copy(a, 4 + j, (*chip, 1 - c), me).wait_recv()
        for cp in first + passed:
            cp.wait_send()
        for cp in mine:
            cp.wait()

    return pl.pallas_call(
        body, name=name,
        in_specs=[ANY] * n, out_specs=[ANY] * n,
        out_shape=[jax.ShapeDtypeStruct((N_DEV,) + s.shape, s.dtype) for s in shards],
        scratch_shapes=[pltpu.SemaphoreType.DMA((n, 7)), pltpu.SemaphoreType.DMA((n, 7)),
                        pltpu.SemaphoreType.DMA((n,))],
    )(*shards)


def _peer(x, y, c, k):
    return (1 - x if k & 4 else x, 1 - y if k & 2 else y, 1 - c if k & 1 else c)


def _exchange(parts, name):
    n = len(parts)

    def body(*refs):
        ins, outs = refs[:n], refs[n:2 * n]
        send_sems, recv_sems, local_sems = refs[2 * n:]
        x, y, c = _position()
        me = _slot(x, y, c)
        mine = [pltpu.make_async_copy(ins[a].at[me], outs[a].at[me], local_sems.at[a]) for a in range(n)]
        for cp in mine:
            cp.start()
        sends = []
        for k in range(1, N_DEV):
            to = _peer(x, y, c, k)
            for a in range(n):
                cp = pltpu.make_async_remote_copy(
                    src_ref=ins[a].at[_slot(*to)], dst_ref=outs[a].at[me],
                    send_sem=send_sems.at[a, k - 1], recv_sem=recv_sems.at[a, k - 1],
                    device_id=to, device_id_type=MESH)
                cp.start()
                sends.append(cp)
        for k in range(1, N_DEV):
            frm = _peer(x, y, c, k)
            for a in range(n):
                pltpu.make_async_remote_copy(
                    src_ref=ins[a].at[_slot(*frm)], dst_ref=outs[a].at[_slot(*frm)],
                    send_sem=send_sems.at[a, k - 1], recv_sem=recv_sems.at[a, k - 1],
                    device_id=frm, device_id_type=MESH).wait_recv()
        for cp in sends:
            cp.wait_send()
        for cp in mine:
            cp.wait()

    return pl.pallas_call(
        body, name=name,
        in_specs=[ANY] * n, out_specs=[ANY] * n,
        out_shape=[jax.ShapeDtypeStruct(p.shape, p.dtype) for p in parts],
        scratch_shapes=[pltpu.SemaphoreType.DMA((n, 7)), pltpu.SemaphoreType.DMA((n, 7)),
                        pltpu.SemaphoreType.DMA((n,))],
    )(*parts)


HBM_SPEC = pl.BlockSpec(memory_space=pltpu.HBM)
SEM_SPEC = pl.BlockSpec(memory_space=pltpu.SEMAPHORE)
SIDE_EFFECT = pltpu.SideEffectType.DATAFLOW_SIDE_EFFECTING


def _remote_copies(src_refs, land_refs, send_sems, recv_sems, gather, outgoing):
    x, y, c = _position()
    me = _slot(x, y, c)
    copies = []
    for k in range(1, N_DEV):
        peer = _peer(x, y, c, k)
        for a in range(len(src_refs)):
            src = src_refs[a] if gather else src_refs[a].at[_slot(*peer)]
            dst = land_refs[a].at[me if outgoing else _slot(*peer)]
            sem = a * (N_DEV - 1) + k - 1
            copies.append(pltpu.make_async_remote_copy(
                src_ref=src, dst_ref=dst, send_sem=send_sems.at[sem], recv_sem=recv_sems.at[sem],
                device_id=peer, device_id_type=MESH))
    return copies


def _remote_start(srcs, after, name, gather):
    n = len(srcs)
    lands = [jax.ShapeDtypeStruct(((N_DEV,) + s.shape) if gather else s.shape, s.dtype) for s in srcs]

    def body(*refs):
        src_refs, land_refs = refs[:n], refs[n:2 * n]
        send_sems, recv_sems = refs[2 * n + 1], refs[2 * n + 2]
        token, local_sems = refs[4 * n + 3], refs[4 * n + 4]
        x, y, c = _position()
        me = _slot(x, y, c)
        mine = [pltpu.make_async_copy(src_refs[a] if gather else src_refs[a].at[me], land_refs[a].at[me],
                                      local_sems.at[a]) for a in range(n)]
        for cp in mine:
            cp.start()
        for cp in _remote_copies(src_refs, land_refs, send_sems, recv_sems, gather, outgoing=True):
            cp.start()
        for cp in mine:
            cp.wait()
        token[...] = jnp.zeros_like(token)

    sem_shape = pltpu.SemaphoreType.DMA((n * (N_DEV - 1),))
    outs = pl.pallas_call(
        body, name=name,
        out_shape=(sem_shape, sem_shape, *[pltpu.HBM(s.shape, s.dtype) for s in srcs],
                   *[pltpu.HBM(l.shape, l.dtype) for l in lands], jax.ShapeDtypeStruct((8, LANES), F32)),
        in_specs=[HBM_SPEC] * (2 * n) + [ANY],
        out_specs=(SEM_SPEC, SEM_SPEC, *([HBM_SPEC] * (2 * n)), pl.BlockSpec(memory_space=pltpu.VMEM)),
        input_output_aliases={a: 2 + a for a in range(2 * n)},
        scratch_shapes=[pltpu.SemaphoreType.DMA((n,))],
        compiler_params=pltpu.CompilerParams(has_side_effects=SIDE_EFFECT),
    )(*[pltpu.with_memory_space_constraint(s, pltpu.HBM) for s in srcs],
      *[pltpu.with_memory_space_constraint(lax.empty(l.shape, l.dtype), pltpu.HBM) for l in lands], after)
    return dict(send=outs[0], recv=outs[1], srcs=outs[2:2 + n], lands=outs[2 + n:2 + 2 * n], token=outs[-1],
                gather=gather)


def _remote_wait(flight, after, name):
    n = len(flight["srcs"])
    gather = flight["gather"]

    def body(*refs):
        src_refs, land_refs = refs[:n], refs[n:2 * n]
        send_sems, recv_sems = refs[2 * n], refs[2 * n + 1]
        for cp in _remote_copies(src_refs, land_refs, send_sems, recv_sems, gather, outgoing=False):
            cp.wait_send()
            cp.wait_recv()

    both = list(flight["srcs"]) + list(flight["lands"])
    outs = pl.pallas_call(
        body, name=name,
        out_shape=tuple(pltpu.HBM(a.shape, a.dtype) for a in both),
        in_specs=[HBM_SPEC] * (2 * n) + [SEM_SPEC, SEM_SPEC, ANY],
        out_specs=tuple([HBM_SPEC] * (2 * n)),
        input_output_aliases={a: a for a in range(2 * n)},
        compiler_params=pltpu.CompilerParams(has_side_effects=SIDE_EFFECT),
    )(*both, flight["send"], flight["recv"], after)
    return list(outs[n:])


def _all_reduce_small(blob, name):
    R, C = blob.shape

    def body(in_ref, out_ref, gath, send_sems, recv_sems):
        x, y, c = _position()
        me = _slot(x, y, c)
        gath[me] = in_ref[...]
        sends = []
        for k in range(1, N_DEV):
            to = _peer(x, y, c, k)
            cp = pltpu.make_async_remote_copy(
                src_ref=in_ref, dst_ref=gath.at[me],
                send_sem=send_sems.at[k - 1], recv_sem=recv_sems.at[k - 1],
                device_id=to, device_id_type=MESH)
            cp.start()
            sends.append(cp)
        for k in range(1, N_DEV):
            frm = _peer(x, y, c, k)
            pltpu.make_async_remote_copy(
                src_ref=in_ref, dst_ref=gath.at[_slot(*frm)],
                send_sem=send_sems.at[k - 1], recv_sem=recv_sems.at[k - 1],
                device_id=frm, device_id_type=MESH).wait_recv()
        for cp in sends:
            cp.wait_send()
        total = gath[0]
        for s in range(1, N_DEV):
            total = total + gath[s]
        out_ref[...] = total

    return pl.pallas_call(
        body, name=name,
        in_specs=[pl.BlockSpec(memory_space=pltpu.VMEM)],
        out_specs=pl.BlockSpec(memory_space=pltpu.VMEM),
        out_shape=jax.ShapeDtypeStruct((R, C), F32),
        scratch_shapes=[pltpu.VMEM((N_DEV, R, C), F32), pltpu.SemaphoreType.DMA((7,)),
                        pltpu.SemaphoreType.DMA((7,))],
        compiler_params=pltpu.CompilerParams(vmem_limit_bytes=VMEM_LIMIT),
    )(blob)


SMALL_VECS = ("ffn1_pre_g", "ffn1_post_g", "mix_pre_g", "sgu_ln_g", "sgu_ln_b", "mix_post_g", "ffn2_pre_g",
              "ffn2_post_g")
ROW_BS = len(SMALL_VECS)
ROW_BF = ROW_BS + 1
ROW_LOSS = ROW_BF + 1
ROW_WS = 16
BLOB_ROWS = ROW_WS + SGU_LEN


def _pack_small(vals, D, loss_row=None):
    rows = [vals[n].reshape(1, D) for n in SMALL_VECS]
    rows.append(vals["sgu_b_s"].reshape(1, D))
    rows.append(jnp.pad(vals["b_forget"].reshape(1, N_HEADS), ((0, 0), (0, D - N_HEADS))))
    rows.append(jnp.zeros((1, D), F32) if loss_row is None else loss_row)
    rows.append(jnp.zeros((ROW_WS - ROW_LOSS - 1, D), F32))
    rows.append(vals["sgu_w_s"].reshape(SGU_LEN, D))
    return jnp.concatenate(rows, axis=0)


def _unpack_small(blob, D):
    out = {n: blob[r:r + 1] for r, n in enumerate(SMALL_VECS)}
    out["sgu_b_s"] = blob[ROW_BS].reshape(1, N_GROUPS, SGU_LEN)
    out["b_forget"] = blob[ROW_BF, :N_HEADS].reshape(1, N_HEADS)
    out["sgu_w_s"] = blob[ROW_WS:].reshape(1, N_GROUPS, SGU_LEN, SGU_LEN)
    return out


WEIGHT_NAMES = ("ffn1_pre_g", "ffn1_w_gate", "ffn1_w_up", "ffn1_w_down", "ffn1_post_g", "mix_pre_g", "w_in",
                "b_forget", "sgu_ln_g", "sgu_ln_b", "sgu_w_s", "sgu_b_s", "w_out", "mix_post_g", "ffn2_pre_g",
                "ffn2_w_gate", "ffn2_w_up", "ffn2_w_down", "ffn2_post_g")
BIG_NAMES = ("ffn1_w_gate", "ffn1_w_up", "ffn1_w_down", "w_in", "w_out", "ffn2_w_gate", "ffn2_w_up", "ffn2_w_down")
WEIGHT_GROUPS = {"ffn1": ("ffn1_w_gate", "ffn1_w_up", "ffn1_w_down"), "mix": ("w_in", "w_out"),
                 "ffn2": ("ffn2_w_gate", "ffn2_w_up", "ffn2_w_down")}
GRAD_GROUPS = (("ffn2_w_gate", "ffn2_w_up", "ffn2_w_down"), ("w_out", "w_in"), ("ffn1_w_down", "ffn1_w_gate"),
               ("ffn1_w_up",))


def _local_step(x, target, small, fetch, emit):
    T, D = x.shape
    W = N_HEADS * HEAD_DIM
    vec = lambda n: small[n].reshape(1, D)
    big = dict(fetch("ffn1", x))

    x1, y1, dgf1, silu1, act1 = _ffn_fwd(x, vec("ffn1_pre_g"), big["ffn1_w_gate"], big["ffn1_w_up"], big["ffn1_w_down"],
                                  vec("ffn1_post_g"), "ffn1_fwd")

    big.update(fetch("mix", x1))
    w_in_all = big["w_in"]
    in_width = N_DEV * w_in_all.shape[2]
    w_in = w_in_all.transpose(1, 0, 2).reshape(D, in_width)
    col_f = 3 * W
    col_u = col_f + N_HEADS
    seg_starts = (0, W, 2 * W, col_u, col_u + W, col_u + 2 * W, col_u + 3 * W)
    w7 = jnp.stack([w_in[:, s:s + W] for s in seg_starts])
    wf = jnp.pad(w_in[:, col_f:col_u], ((0, 0), (0, LANES - N_HEADS)))
    w_out = big["w_out"].reshape(D, D)
    b_pad = jnp.pad(small["b_forget"].reshape(1, N_HEADS), ((0, 0), (0, LANES - N_HEADS)))
    lng, lnb = vec("sgu_ln_g"), vec("sgu_ln_b")
    ws = small["sgu_w_s"].reshape(N_GROUPS, SGU_LEN, SGU_LEN)
    bs = small["sgu_b_s"].reshape(N_GROUPS, SGU_LEN, 1)

    z7, f_logit, h2b = _mix_in_fwd(x1, vec("mix_pre_g"), w7, wf, "mix_in_fwd")
    c = _forget_cumsum(f_logit, b_pad, "forget_cumsum")
    c_heads = c[:, :N_HEADS].T
    ta, _, n_chunks = _attn_geometry(T)
    c_chunks = c_heads.reshape(N_HEADS, n_chunks, 1, ta)
    c_col = c_heads[:, :, None]
    vt = z7[2].reshape(n_chunks, ta, N_HEADS, HEAD_DIM).transpose(2, 0, 3, 1)
    c_rep = jnp.broadcast_to(c_col, (N_HEADS, T, LANES))
    o_a, lse_chunks = _attn_fwd_keys_on_rows(z7, vt, c_rep, "attn_fwd")
    lse = lse_chunks.reshape(N_HEADS, T, 1)
    x2, p, merged_b = _mix_out_fwd(z7, o_a, x1, lng, lnb, ws, bs, w_out, vec("mix_post_g"), "mix_out_fwd")
    big.update(fetch("ffn2", x2))
    x3, y2, dgf2, silu2, act2 = _ffn_fwd(x2, vec("ffn2_pre_g"), big["ffn2_w_gate"], big["ffn2_w_up"], big["ffn2_w_down"],
                                  vec("ffn2_post_g"), "ffn2_fwd")
    dy, loss_lanes = _loss_head(x3, target, "loss_head")

    grads_small = {}

    dx2, h3b, dy2b, dgate2, dup2, dgpre, dgpost = _ffn_bwd(
        dy, x2, y2, dgf2, silu2, vec("ffn2_pre_g"), big["ffn2_w_gate"], big["ffn2_w_up"], big["ffn2_w_down"],
        vec("ffn2_post_g"), "ffn2_bwd")
    grads_small["ffn2_pre_g"] = jnp.sum(dgpre, axis=0)
    grads_small["ffn2_post_g"] = jnp.sum(dgpost, axis=0)
    emit("ffn2_w_gate", _wgrad(h3b, dgate2, "ffn2_wgrad_gate", shard_cols=True))
    emit("ffn2_w_up", _wgrad(h3b, dup2, "ffn2_wgrad_up", shard_cols=True))
    dep = emit("ffn2_w_down", _wgrad(act2, dy2b, "ffn2_wgrad_down").reshape(big["ffn2_w_down"].shape))

    dpb, dob, dvec, dz4, dgp, dlng, dlnb, dws, dbs = _mix_out_bwd(
        dx2, p, z7, o_a, lng, lnb, ws, bs, w_out, vec("mix_post_g"), "mix_out_bwd", dep=dep)
    grads_small["mix_post_g"] = dgp
    grads_small["sgu_ln_g"] = dlng
    grads_small["sgu_ln_b"] = dlnb
    grads_small["sgu_w_s"] = dws
    grads_small["sgu_b_s"] = dbs
    emit("w_out", _wgrad(merged_b, dpb, "w_out_wgrad").reshape(big["w_out"].shape))
    d_chunks = dvec.reshape(N_HEADS, n_chunks, 1, ta)
    kt = z7[1].reshape(n_chunks, ta, N_HEADS, HEAD_DIM).transpose(2, 0, 3, 1)
    dk, dv, dc, dq, dc_q = _attn_bwd_fused(z7, kt, dob, c_rep, lse_chunks, d_chunks, "attn_bwd")
    dc_pad = jnp.pad((dc.reshape(N_HEADS, T) + dc_q.reshape(N_HEADS, T)).T, ((0, 0), (0, LANES - N_HEADS)))
    dfb, dbf = _forget_bwd(dc_pad, f_logit, b_pad, "forget_bwd")
    grads_small["b_forget"] = dbf[:, :N_HEADS]
    segs = [(dq, None), (dk, None), (dv, None), (dz4, 0), (dz4, 1), (dz4, 2), (dz4, 3)]
    dx1, dgm = _mix_in_bwd(dx2, x1, vec("mix_pre_g"), segs, dfb, w7, wf, "mix_in_bwd")
    grads_small["mix_pre_g"] = jnp.sum(dgm, axis=0)
    seg_mats = [dq, dk, dv, dz4[0], dz4[1], dz4[2], dz4[3]]
    dw_seg = [_wgrad(h2b, sm, "w_in_wgrad_%d" % q) for q, sm in enumerate(seg_mats)]
    dwf = _wgrad(h2b, dfb, "w_in_wgrad_f")[:, :N_HEADS]
    dw_in = jnp.concatenate(dw_seg[:3] + [dwf] + dw_seg[3:], axis=1)
    dep = emit("w_in", dw_in.reshape(D, N_DEV, in_width // N_DEV).transpose(1, 0, 2))

    dx0, h1b, dy1b, dgate1, dup1, dgpre1, dgpost1 = _ffn_bwd(
        dx1, x, y1, dgf1, silu1, vec("ffn1_pre_g"), big["ffn1_w_gate"], big["ffn1_w_up"], big["ffn1_w_down"],
        vec("ffn1_post_g"), "ffn1_bwd", dep=dep)
    grads_small["ffn1_pre_g"] = jnp.sum(dgpre1, axis=0)
    grads_small["ffn1_post_g"] = jnp.sum(dgpost1, axis=0)
    emit("ffn1_w_down", _wgrad(act1, dy1b, "ffn1_wgrad_down").reshape(big["ffn1_w_down"].shape))
    dep = emit("ffn1_w_gate", _wgrad(h1b, dgate1, "ffn1_wgrad_gate", shard_cols=True))
    emit("ffn1_w_up", _wgrad(h1b, dup1, "ffn1_wgrad_up", shard_cols=True, dep=dep))

    loss_row = jnp.pad(loss_lanes, ((0, 0), (0, D - LANES)))
    return loss_row, dx0, grads_small


def kernel(x, ffn1_pre_g, ffn1_w_gate, ffn1_w_up, ffn1_w_down, ffn1_post_g, mix_pre_g, w_in, b_forget, sgu_ln_g, sgu_ln_b, sgu_w_s, sgu_b_s, w_out, mix_post_g, ffn2_pre_g, ffn2_w_gate, ffn2_w_up, ffn2_w_down, ffn2_post_g, loss_target, m_ffn1_pre_g, m_ffn1_w_gate, m_ffn1_w_up, m_ffn1_w_down, m_ffn1_post_g, m_mix_pre_g, m_w_in, m_b_forget, m_sgu_ln_g, m_sgu_ln_b, m_sgu_w_s, m_sgu_b_s, m_w_out, m_mix_post_g, m_ffn2_pre_g, m_ffn2_w_gate, m_ffn2_w_up, m_ffn2_w_down, m_ffn2_post_g, v_ffn1_pre_g, v_ffn1_w_gate, v_ffn1_w_up, v_ffn1_w_down, v_ffn1_post_g, v_mix_pre_g, v_w_in, v_b_forget, v_sgu_ln_g, v_sgu_ln_b, v_sgu_w_s, v_sgu_b_s, v_w_out, v_mix_post_g, v_ffn2_pre_g, v_ffn2_w_gate, v_ffn2_w_up, v_ffn2_w_down, v_ffn2_post_g):
    weights = dict(zip(WEIGHT_NAMES, (ffn1_pre_g, ffn1_w_gate, ffn1_w_up, ffn1_w_down, ffn1_post_g, mix_pre_g, w_in,
                                      b_forget, sgu_ln_g, sgu_ln_b, sgu_w_s, sgu_b_s, w_out, mix_post_g, ffn2_pre_g,
                                      ffn2_w_gate, ffn2_w_up, ffn2_w_down, ffn2_post_g)))
    mom1 = dict(zip(WEIGHT_NAMES, (m_ffn1_pre_g, m_ffn1_w_gate, m_ffn1_w_up, m_ffn1_w_down, m_ffn1_post_g,
                                   m_mix_pre_g, m_w_in, m_b_forget, m_sgu_ln_g, m_sgu_ln_b, m_sgu_w_s, m_sgu_b_s,
                                   m_w_out, m_mix_post_g, m_ffn2_pre_g, m_ffn2_w_gate, m_ffn2_w_up, m_ffn2_w_down,
                                   m_ffn2_post_g)))
    mom2 = dict(zip(WEIGHT_NAMES, (v_ffn1_pre_g, v_ffn1_w_gate, v_ffn1_w_up, v_ffn1_w_down, v_ffn1_post_g,
                                   v_mix_pre_g, v_w_in, v_b_forget, v_sgu_ln_g, v_sgu_ln_b, v_sgu_w_s, v_sgu_b_s,
                                   v_w_out, v_mix_post_g, v_ffn2_pre_g, v_ffn2_w_gate, v_ffn2_w_up, v_ffn2_w_down,
                                   v_ffn2_post_g)))
    D = x.shape[-1]
    small_names = [n for n in WEIGHT_NAMES if n not in BIG_NAMES]

    small = {n: weights[n] for n in small_names}
    shard = lambda n: weights[n][0].astype(BF16)

    ffn1_full = _all_gather([shard(n) for n in WEIGHT_GROUPS["ffn1"]], "ffn1_all_gather")
    gathers = {grp: _remote_start([shard(n) for n in WEIGHT_GROUPS[grp]], ffn1_full[0], grp + "_gather_start",
                                  gather=True) for grp in ("mix", "ffn2")}

    def fetch(group, after):
        if group == "ffn1":
            return zip(WEIGHT_GROUPS[group], ffn1_full)
        return zip(WEIGHT_GROUPS[group], _remote_wait(gathers[group], after, group + "_gather_wait"))

    ready, flights = {}, []

    def emit(name, part):
        ready[name] = part
        for group in GRAD_GROUPS:
            if name == group[-1]:
                flights.append((group, _remote_start([ready[n] for n in group], part, name + "_grad_start",
                                                     gather=False)))
                return flights[-1][1]["token"]
        return None

    loss_row, grad_x, grads_small = _local_step(x[0], loss_target[0], small, fetch, emit)

    blob = _all_reduce_small(_pack_small(grads_small, D, loss_row) + flights[-1][1]["token"][:1, :1],
                             "small_all_reduce")

    out = {}
    after = blob
    for group, flight in flights:
        received = _remote_wait(flight, after, group[-1] + "_grad_wait")
        for n, rcv in zip(group, received):
            g, d, m_new, v_new = _sum_adamw(rcv, weights[n][0], mom1[n][0], mom2[n][0], "adamw_" + n)
            out[n] = tuple(a[None] for a in (g, d, m_new, v_new))
            after = g

    d_blob, m_blob, v_blob = _adamw(blob, _pack_small(small, D), _pack_small({n: mom1[n] for n in small_names}, D),
                                    _pack_small({n: mom2[n] for n in small_names}, D), "adamw_small")
    unpacked = [_unpack_small(b, D) for b in (blob, d_blob, m_blob, v_blob)]
    for n in small_names:
        out[n] = tuple(u[n].reshape(weights[n].shape) for u in unpacked)

    loss = blob[ROW_LOSS, 0]
    result = [loss, grad_x[None]]
    for k in range(4):
        result += [out[n][k] for n in WEIGHT_NAMES]
    return tuple(result)
```

```python
import functools

import numpy as np
import jax
import jax.numpy as jnp
from jax import lax
from jax.experimental import pallas as pl
from jax.experimental.pallas import tpu as pltpu
from jax.experimental.pallas import tpu_sc as plsc

F32 = jnp.float32
BF16 = jnp.bfloat16

RMS_EPS = 1e-6
LN_EPS = 1e-5
HEAD_DIM = 128
N_HEADS = 8
GROUP_DIM = 128
N_GROUPS = 8
SGU_LEN = 128
CHUNK = 64
N_DEV = 8
LANES = 128
VMEM_LIMIT = 56 * 1024 * 1024
NEG_BIG = -1e30

ADAM_LR = 0.001
ADAM_B1 = 0.9
ADAM_B2 = 0.999
ADAM_EPS = 1e-08
ADAM_WD = 0.01
ADAM_STEP = 10

MESH = pl.DeviceIdType.MESH
ANY = pl.BlockSpec(memory_space=pl.ANY)


def _blk(n, pref):
    return pref if (n >= pref and n % pref == 0) else n


def _mm(a, b):
    return jnp.dot(a, b, preferred_element_type=F32)


def _mm_nt(a, b):
    return lax.dot_general(a, b, (((1,), (1,)), ((), ())), preferred_element_type=F32)


def _mm_tn(a, b):
    return lax.dot_general(a, b, (((0,), (0,)), ((), ())), preferred_element_type=F32)


def _params(sem):
    return pltpu.CompilerParams(dimension_semantics=sem, vmem_limit_bytes=VMEM_LIMIT)


def _gelu(x):
    return 0.5 * x * (1.0 + lax.erf(x * np.float32(1.0 / np.sqrt(2.0))))


def _gelu_grad(x):
    cdf = 0.5 * (1.0 + lax.erf(x * np.float32(1.0 / np.sqrt(2.0))))
    return cdf + x * jnp.exp(-0.5 * x * x) * np.float32(1.0 / np.sqrt(2.0 * np.pi))


def _rms_scale(v):
    return lax.rsqrt(jnp.mean(v * v, axis=-1, keepdims=True) + RMS_EPS)


def _rms_bwd(dy, xhat, r, g):
    dxh = dy * g
    return r * (dxh - xhat * jnp.mean(dxh * xhat, axis=-1, keepdims=True))


def _ffn_rows(T):
    tm = _blk(T, 1024)
    th = _blk(tm, 512)
    return tm, th, tm // th


def _ffn_fwd(x, g_pre, wg, wu, wd, g_post, name):
    T, D = x.shape
    ns, _, fs = wg.shape
    tm, th, parts = _ffn_rows(T)

    def body(x_ref, gpre_ref, wg_ref, wu_ref, wd_ref, gpost_ref, xo_ref, y_ref, dgf_ref, silu_ref, act_ref,
             h_scr, acc_scr):
        j = pl.program_id(1)

        @pl.when(j == 0)
        def _():
            for r in range(parts):
                rows = slice(r * th, (r + 1) * th)
                xv = x_ref[rows, :]
                h_scr[rows, :] = (xv * _rms_scale(xv) * gpre_ref[...]).astype(BF16)
            acc_scr[...] = jnp.zeros_like(acc_scr)

        pre = []
        for r in range(parts):
            h = h_scr[r * th:(r + 1) * th, :]
            pre.append((_mm(h, wg_ref[...]), _mm(h, wu_ref[...])))
        for r in range(parts):
            rows = slice(r * th, (r + 1) * th)
            gg, uu = pre[r]
            sg = jax.nn.sigmoid(gg)
            silu = gg * sg
            act = (silu * uu).astype(BF16)
            dgf_ref[rows, :] = (uu * (sg * (1.0 + gg * (1.0 - sg)))).astype(BF16)
            silu_ref[rows, :] = silu.astype(BF16)
            act_ref[rows, :] = act
            acc_scr[rows, :] += _mm(act, wd_ref[...])

        @pl.when(j == ns - 1)
        def _():
            for r in range(parts):
                rows = slice(r * th, (r + 1) * th)
                y = acc_scr[rows, :]
                y_ref[rows, :] = y
                xo_ref[rows, :] = x_ref[rows, :] + 0.5 * (y * _rms_scale(y) * gpost_ref[...])

    row = pl.BlockSpec((tm, D), lambda i, j: (i, 0), pipeline_mode=pl.Buffered(1))
    vec = pl.BlockSpec((1, D), lambda i, j: (0, 0))
    return pl.pallas_call(
        body, name=name, grid=(T // tm, ns),
        in_specs=[row, vec,
                  pl.BlockSpec((None, D, fs), lambda i, j: (j, 0, 0)),
                  pl.BlockSpec((None, D, fs), lambda i, j: (j, 0, 0)),
                  pl.BlockSpec((None, fs, D), lambda i, j: (j, 0, 0)),
                  vec],
        out_specs=[row, row] + [pl.BlockSpec((tm, fs), lambda i, j: (i, j))] * 3,
        out_shape=[jax.ShapeDtypeStruct((T, D), F32), jax.ShapeDtypeStruct((T, D), F32)]
        + [jax.ShapeDtypeStruct((T, ns * fs), BF16)] * 3,
        scratch_shapes=[pltpu.VMEM((tm, D), BF16), pltpu.VMEM((tm, D), F32)],
        compiler_params=_params(("parallel", "arbitrary")),
    )(x, g_pre, wg, wu, wd, g_post)


def _after(dep):
    return jnp.zeros((8, LANES), F32) if dep is None else dep


def _ffn_bwd(dxo, x, y, dgf, silu, g_pre, wg, wu, wd, g_post, name, dep=None):
    T, D = x.shape
    ns, _, fs = wg.shape
    tm, th, parts = _ffn_rows(T)
    n_i = T // tm

    def body(dxo_ref, x_ref, y_ref, dgf_ref, silu_ref, gpre_ref, wg_ref, wu_ref, wd_ref, gpost_ref, _,
             dx_ref, hb_ref, dyb_ref, dgb_ref, dub_ref, dgpre_ref, dgpost_ref, dy_scr, acc_scr):
        j = pl.program_id(1)

        @pl.when(j == 0)
        def _():
            dgpost = jnp.zeros((1, D), F32)
            for r in range(parts):
                rows = slice(r * th, (r + 1) * th)
                yv = y_ref[rows, :]
                s = _rms_scale(yv)
                n = yv * s
                dn = 0.5 * dxo_ref[rows, :]
                dgpost = dgpost + jnp.sum(dn * n, axis=0, keepdims=True)
                dyv = _rms_bwd(dn, n, s, gpost_ref[...]).astype(BF16)
                dy_scr[rows, :] = dyv
                dyb_ref[rows, :] = dyv
                xv = x_ref[rows, :]
                hb_ref[rows, :] = (xv * _rms_scale(xv) * gpre_ref[...]).astype(BF16)
            dgpost_ref[...] = dgpost
            acc_scr[...] = jnp.zeros_like(acc_scr)

        das = [_mm_nt(dy_scr[r * th:(r + 1) * th, :], wd_ref[...]) for r in range(parts)]
        for r in range(parts):
            rows = slice(r * th, (r + 1) * th)
            dgate = (das[r] * dgf_ref[rows, :].astype(F32)).astype(BF16)
            dup = (das[r] * silu_ref[rows, :].astype(F32)).astype(BF16)
            dgb_ref[rows, :] = dgate
            dub_ref[rows, :] = dup
            acc_scr[rows, :] += _mm_nt(dgate, wg_ref[...]) + _mm_nt(dup, wu_ref[...])

        @pl.when(j == ns - 1)
        def _():
            dgpre = jnp.zeros((1, D), F32)
            for r in range(parts):
                rows = slice(r * th, (r + 1) * th)
                xv = x_ref[rows, :]
                rs = _rms_scale(xv)
                xhat = xv * rs
                dh = acc_scr[rows, :]
                dgpre = dgpre + jnp.sum(dh * xhat, axis=0, keepdims=True)
                dx_ref[rows, :] = _rms_bwd(dh, xhat, rs, gpre_ref[...]) + dxo_ref[rows, :]
            dgpre_ref[...] = dgpre

    row = pl.BlockSpec((tm, D), lambda i, j: (i, 0), pipeline_mode=pl.Buffered(1))
    vec = pl.BlockSpec((1, D), lambda i, j: (0, 0))
    wide = pl.BlockSpec((tm, fs), lambda i, j: (i, j))
    part = pl.BlockSpec((None, 1, D), lambda i, j: (i, 0, 0))
    F = ns * fs
    return pl.pallas_call(
        body, name=name, grid=(n_i, ns),
        in_specs=[row, row, row, wide, wide, vec,
                  pl.BlockSpec((None, D, fs), lambda i, j: (j, 0, 0)),
                  pl.BlockSpec((None, D, fs), lambda i, j: (j, 0, 0)),
                  pl.BlockSpec((None, fs, D), lambda i, j: (j, 0, 0)),
                  vec, ANY],
        out_specs=[row, row, row, wide, wide, part, part],
        out_shape=[jax.ShapeDtypeStruct((T, D), F32), jax.ShapeDtypeStruct((T, D), BF16),
                   jax.ShapeDtypeStruct((T, D), BF16), jax.ShapeDtypeStruct((T, F), BF16),
                   jax.ShapeDtypeStruct((T, F), BF16),
                   jax.ShapeDtypeStruct((n_i, 1, D), F32), jax.ShapeDtypeStruct((n_i, 1, D), F32)],
        scratch_shapes=[pltpu.VMEM((tm, D), BF16), pltpu.VMEM((tm, D), F32)],
        compiler_params=_params(("parallel", "arbitrary")),
    )(dxo, x, y, dgf, silu, g_pre, wg, wu, wd, g_post, _after(dep))


def _wgrad(xm, ym, name, shard_cols=False, dep=None):
    T, M = xm.shape
    _, N = ym.shape
    assert M * N * 4 <= 16 * 1024 * 1024, (M, N)
    tk = _blk(T, 512)
    n_k = T // tk
    fs = N // N_DEV

    def body(x_ref, y_ref, _, o_ref, acc_scr):
        k = pl.program_id(0)

        @pl.when(k == 0)
        def _():
            acc_scr[...] = jnp.zeros_like(acc_scr)

        acc_scr[...] += _mm_tn(x_ref[...], y_ref[...])

        @pl.when(k == n_k - 1)
        def _():
            if shard_cols:
                for s in range(N_DEV):
                    o_ref[s] = acc_scr[:, s * fs:(s + 1) * fs].astype(BF16)
            else:
                o_ref[...] = acc_scr[...].astype(BF16)

    if shard_cols:
        out_spec = pl.BlockSpec((N_DEV, M, fs), lambda k: (0, 0, 0), pipeline_mode=pl.Buffered(1))
        out_shape = jax.ShapeDtypeStruct((N_DEV, M, fs), BF16)
    else:
        out_spec = pl.BlockSpec((M, N), lambda k: (0, 0), pipeline_mode=pl.Buffered(1))
        out_shape = jax.ShapeDtypeStruct((M, N), BF16)
    return pl.pallas_call(
        body, name=name, grid=(n_k,),
        in_specs=[pl.BlockSpec((tk, M), lambda k: (k, 0)), pl.BlockSpec((tk, N), lambda k: (k, 0)), ANY],
        out_specs=out_spec, out_shape=out_shape,
        scratch_shapes=[pltpu.VMEM((M, N), F32)],
        compiler_params=_params(("arbitrary",)),
    )(xm, ym, _after(dep))


def _mix_in_fwd(x1, g, w7, wf, name):
    T, D = x1.shape
    n_seg, _, W = w7.shape
    tm = _blk(T, 1024)

    def body(x_ref, g_ref, w_ref, wf_ref, z_ref, f_ref, hb_ref, h_scr):
        s = pl.program_id(1)

        @pl.when(s == 0)
        def _():
            xv = x_ref[...]
            h = (xv * _rms_scale(xv) * g_ref[...]).astype(BF16)
            h_scr[...] = h
            hb_ref[...] = h
            f_ref[...] = _mm(h, wf_ref[...])

        z_ref[...] = _mm(h_scr[...], w_ref[...]).astype(BF16)

    return pl.pallas_call(
        body, name=name, grid=(T // tm, n_seg),
        in_specs=[pl.BlockSpec((tm, D), lambda i, s: (i, 0)),
                  pl.BlockSpec((1, D), lambda i, s: (0, 0)),
                  pl.BlockSpec((None, D, W), lambda i, s: (s, 0, 0)),
                  pl.BlockSpec((D, LANES), lambda i, s: (0, 0))],
        out_specs=[pl.BlockSpec((None, tm, W), lambda i, s: (s, i, 0)),
                   pl.BlockSpec((tm, LANES), lambda i, s: (i, 0)),
                   pl.BlockSpec((tm, D), lambda i, s: (i, 0))],
        out_shape=[jax.ShapeDtypeStruct((n_seg, T, W), BF16), jax.ShapeDtypeStruct((T, LANES), F32),
                   jax.ShapeDtypeStruct((T, D), BF16)],
        scratch_shapes=[pltpu.VMEM((tm, D), BF16)],
        compiler_params=_params(("parallel", "arbitrary")),
    )(x1, g, w7, wf)


def _mix_in_bwd(dx2, x1, g, segs, dfb, w7, wf, name):
    T, D = x1.shape
    n_seg, _, W = w7.shape
    tm = _blk(T, 512)
    n_i = T // tm

    def body(*refs):
        dx2_ref, x_ref, g_ref = refs[:3]
        seg_refs = refs[3:3 + n_seg]
        df_ref, w_ref, wf_ref, dx1_ref, dg_ref, acc_scr = refs[3 + n_seg:]
        s = pl.program_id(1)

        @pl.when(s == 0)
        def _():
            acc_scr[...] = _mm_nt(df_ref[...], wf_ref[...])

        for q in range(n_seg):
            @pl.when(s == q)
            def _(q=q):
                acc_scr[...] += _mm_nt(seg_refs[q][...], w_ref[...])

        @pl.when(s == n_seg - 1)
        def _():
            xv = x_ref[...]
            r = _rms_scale(xv)
            xhat = xv * r
            dh = acc_scr[...]
            dg_ref[...] = jnp.sum(dh * xhat, axis=0, keepdims=True)
            dx1_ref[...] = _rms_bwd(dh, xhat, r, g_ref[...]) + dx2_ref[...]

    row = pl.BlockSpec((tm, D), lambda i, s: (i, 0))
    seg_specs = []
    seg_args = []
    for arr, idx in segs:
        if idx is None:
            seg_specs.append(pl.BlockSpec((tm, W), lambda i, s: (i, 0)))
        else:
            seg_specs.append(pl.BlockSpec((None, tm, W), lambda i, s, idx=idx: (idx, i, 0)))
        seg_args.append(arr)
    return pl.pallas_call(
        body, name=name, grid=(n_i, n_seg),
        in_specs=[row, row, pl.BlockSpec((1, D), lambda i, s: (0, 0))] + seg_specs + [
            pl.BlockSpec((tm, LANES), lambda i, s: (i, 0)),
            pl.BlockSpec((None, D, W), lambda i, s: (s, 0, 0)),
            pl.BlockSpec((D, LANES), lambda i, s: (0, 0))],
        out_specs=[row, pl.BlockSpec((None, 1, D), lambda i, s: (i, 0, 0))],
        out_shape=[jax.ShapeDtypeStruct((T, D), F32), jax.ShapeDtypeStruct((n_i, 1, D), F32)],
        scratch_shapes=[pltpu.VMEM((tm, D), F32)],
        compiler_params=_params(("parallel", "arbitrary")),
    )(dx2, x1, g, *seg_args, dfb, w7, wf)


def _forget_cumsum(f, b_pad, name):
    T, L = f.shape
    tb = _blk(T, 256)

    def body(f_ref, b_ref, c_ref, carry):
        @pl.when(pl.program_id(0) == 0)
        def _():
            carry[...] = jnp.zeros_like(carry)

        lf = jax.nn.log_sigmoid(f_ref[...] + b_ref[...])
        rows = lax.broadcasted_iota(jnp.int32, (tb, tb), 0)
        cols = lax.broadcasted_iota(jnp.int32, (tb, tb), 1)
        tri = (cols <= rows).astype(F32)
        c = jnp.dot(tri, lf, preferred_element_type=F32, precision=lax.Precision.HIGHEST) + carry[...]
        c_ref[...] = c
        carry[...] = c[tb - 1:tb, :]

    return pl.pallas_call(
        body, name=name, grid=(T // tb,),
        in_specs=[pl.BlockSpec((tb, L), lambda i: (i, 0)), pl.BlockSpec((1, L), lambda i: (0, 0))],
        out_specs=pl.BlockSpec((tb, L), lambda i: (i, 0)),
        out_shape=jax.ShapeDtypeStruct((T, L), F32),
        scratch_shapes=[pltpu.VMEM((1, L), F32)],
        compiler_params=_params(("arbitrary",)),
    )(f, b_pad)


def _forget_bwd(dc, f, b_pad, name):
    T, L = f.shape
    tb = _blk(T, 256)
    nb = T // tb

    def body(dc_ref, f_ref, b_ref, df_ref, db_ref, carry):
        @pl.when(pl.program_id(0) == 0)
        def _():
            carry[...] = jnp.zeros_like(carry)
            db_ref[...] = jnp.zeros_like(db_ref)

        rows = lax.broadcasted_iota(jnp.int32, (tb, tb), 0)
        cols = lax.broadcasted_iota(jnp.int32, (tb, tb), 1)
        tri = (cols >= rows).astype(F32)
        r = jnp.dot(tri, dc_ref[...], preferred_element_type=F32, precision=lax.Precision.HIGHEST) + carry[...]
        carry[...] = r[0:1, :]
        df = r * (1.0 - jax.nn.sigmoid(f_ref[...] + b_ref[...]))
        df_ref[...] = df.astype(BF16)
        db_ref[...] += jnp.sum(df, axis=0, keepdims=True)

    rev = pl.BlockSpec((tb, L), lambda i: (nb - 1 - i, 0))
    one = pl.BlockSpec((1, L), lambda i: (0, 0))
    return pl.pallas_call(
        body, name=name, grid=(nb,),
        in_specs=[rev, rev, one], out_specs=[rev, one],
        out_shape=[jax.ShapeDtypeStruct((T, L), BF16), jax.ShapeDtypeStruct((1, L), F32)],
        scratch_shapes=[pltpu.VMEM((1, L), F32)],
        compiler_params=_params(("arbitrary",)),
    )(dc, f, b_pad)


def _attn_fwd(z7, c_row, name):
    _, T, W = z7.shape
    H = W // HEAD_DIM
    ta = _blk(T, 512)
    nq = T // ta
    scale = np.float32(1.0 / np.sqrt(HEAD_DIM))

    def body(q_ref, k_ref, v_ref, crow_ref, o_ref, lse_ref, m_scr, l_scr, acc_scr):
        i = pl.program_id(1)
        j = pl.program_id(2)

        @pl.when(j == 0)
        def _():
            m_scr[...] = jnp.full_like(m_scr, NEG_BIG)
            l_scr[...] = jnp.zeros_like(l_scr)
            acc_scr[...] = jnp.zeros_like(acc_scr)

        def step(diagonal):
            s = _mm_nt(q_ref[...], k_ref[...]) * scale - crow_ref[...]
            if diagonal:
                rows = lax.broadcasted_iota(jnp.int32, (ta, ta), 0)
                cols = lax.broadcasted_iota(jnp.int32, (ta, ta), 1)
                s = jnp.where(cols <= rows, s, NEG_BIG)
            m_prev = m_scr[...]
            m_new = jnp.maximum(m_prev, jnp.max(s, axis=-1, keepdims=True))
            alpha = jnp.exp(m_prev - m_new)
            p = jnp.exp(s - m_new)
            l_scr[...] = alpha * l_scr[...] + jnp.sum(p, axis=-1, keepdims=True)
            acc_scr[...] = alpha * acc_scr[...] + _mm(p.astype(BF16), v_ref[...])
            m_scr[...] = m_new

        @pl.when(j < i)
        def _():
            step(False)

        @pl.when(j == i)
        def _():
            step(True)
            l = l_scr[...]
            o_ref[...] = acc_scr[...] / l
            lse_ref[...] = m_scr[...] + jnp.log(l)

    return pl.pallas_call(
        body, name=name, grid=(H, nq, nq),
        in_specs=[pl.BlockSpec((None, ta, HEAD_DIM), lambda h, i, j: (0, i, h)),
                  pl.BlockSpec((None, ta, HEAD_DIM), lambda h, i, j: (1, jnp.minimum(i, j), h)),
                  pl.BlockSpec((None, ta, HEAD_DIM), lambda h, i, j: (2, jnp.minimum(i, j), h)),
                  pl.BlockSpec((None, 1, ta), lambda h, i, j: (h, 0, jnp.minimum(i, j)))],
        out_specs=[pl.BlockSpec((ta, HEAD_DIM), lambda h, i, j: (i, h)),
                   pl.BlockSpec((None, ta, 1), lambda h, i, j: (h, i, 0))],
        out_shape=[jax.ShapeDtypeStruct((T, W), F32), jax.ShapeDtypeStruct((H, T, 1), F32)],
        scratch_shapes=[pltpu.VMEM((ta, 1), F32), pltpu.VMEM((ta, 1), F32), pltpu.VMEM((ta, HEAD_DIM), F32)],
        compiler_params=_params(("parallel", "parallel", "arbitrary")),
    )(z7, z7, z7, c_row)


def _attn_bwd_kv(z7, dob, c_col, lse_row, d_row, name):
    _, T, W = z7.shape
    H = W // HEAD_DIM
    ta = _blk(T, 512)
    nq = T // ta
    scale = np.float32(1.0 / np.sqrt(HEAD_DIM))

    def body(k_ref, v_ref, q_ref, do_ref, ccol_ref, lse_ref, d_ref, dk_ref, dv_ref, dc_ref, dk_scr, dv_scr, dc_scr):
        j = pl.program_id(1)
        i = pl.program_id(2)

        @pl.when(i == 0)
        def _():
            dk_scr[...] = jnp.zeros_like(dk_scr)
            dv_scr[...] = jnp.zeros_like(dv_scr)
            dc_scr[...] = jnp.zeros_like(dc_scr)

        def step(diagonal):
            q = q_ref[...]
            do = do_ref[...]
            st = _mm_nt(k_ref[...], q) * scale - ccol_ref[...] - lse_ref[...]
            if diagonal:
                rows = lax.broadcasted_iota(jnp.int32, (ta, ta), 0)
                cols = lax.broadcasted_iota(jnp.int32, (ta, ta), 1)
                st = jnp.where(rows <= cols, st, NEG_BIG)
            pt = jnp.exp(st)
            dv_scr[...] += _mm(pt.astype(BF16), do)
            dst = pt * (_mm_nt(v_ref[...], do) - d_ref[...])
            dk_scr[...] += _mm(dst.astype(BF16), q)
            dc_scr[...] += jnp.sum(dst, axis=-1, keepdims=True)

        @pl.when(i > j)
        def _():
            step(False)

        @pl.when(i == j)
        def _():
            step(True)

        @pl.when(i == nq - 1)
        def _():
            dk_ref[...] = (dk_scr[...] * scale).astype(BF16)
            dv_ref[...] = dv_scr[...].astype(BF16)
            dc_ref[...] = -dc_scr[...]

    return pl.pallas_call(
        body, name=name, grid=(H, nq, nq),
        in_specs=[pl.BlockSpec((None, ta, HEAD_DIM), lambda h, j, i: (1, j, h)),
                  pl.BlockSpec((None, ta, HEAD_DIM), lambda h, j, i: (2, j, h)),
                  pl.BlockSpec((None, ta, HEAD_DIM), lambda h, j, i: (0, jnp.maximum(i, j), h)),
                  pl.BlockSpec((ta, HEAD_DIM), lambda h, j, i: (jnp.maximum(i, j), h)),
                  pl.BlockSpec((None, ta, 1), lambda h, j, i: (h, j, 0)),
                  pl.BlockSpec((None, 1, ta), lambda h, j, i: (h, 0, jnp.maximum(i, j))),
                  pl.BlockSpec((None, 1, ta), lambda h, j, i: (h, 0, jnp.maximum(i, j)))],
        out_specs=[pl.BlockSpec((ta, HEAD_DIM), lambda h, j, i: (j, h)),
                   pl.BlockSpec((ta, HEAD_DIM), lambda h, j, i: (j, h)),
                   pl.BlockSpec((None, ta, 1), lambda h, j, i: (h, j, 0))],
        out_shape=[jax.ShapeDtypeStruct((T, W), BF16), jax.ShapeDtypeStruct((T, W), BF16),
                   jax.ShapeDtypeStruct((H, T, 1), F32)],
        scratch_shapes=[pltpu.VMEM((ta, HEAD_DIM), F32), pltpu.VMEM((ta, HEAD_DIM), F32), pltpu.VMEM((ta, 1), F32)],
        compiler_params=_params(("parallel", "parallel", "arbitrary")),
    )(z7, z7, z7, dob, c_col, lse_row, d_row)


def _attn_bwd_q(z7, dob, c_row, lse_col, d_col, name):
    _, T, W = z7.shape
    H = W // HEAD_DIM
    ta = _blk(T, 512)
    nq = T // ta
    scale = np.float32(1.0 / np.sqrt(HEAD_DIM))

    def body(q_ref, k_ref, v_ref, do_ref, crow_ref, lse_ref, d_ref, dq_ref, dc_ref, dq_scr, dc_scr):
        i = pl.program_id(1)
        j = pl.program_id(2)

        @pl.when(j == 0)
        def _():
            dq_scr[...] = jnp.zeros_like(dq_scr)
            dc_scr[...] = jnp.zeros_like(dc_scr)

        def step(diagonal):
            k = k_ref[...]
            do = do_ref[...]
            s = _mm_nt(q_ref[...], k) * scale - crow_ref[...] - lse_ref[...]
            if diagonal:
                rows = lax.broadcasted_iota(jnp.int32, (ta, ta), 0)
                cols = lax.broadcasted_iota(jnp.int32, (ta, ta), 1)
                s = jnp.where(cols <= rows, s, NEG_BIG)
            p = jnp.exp(s)
            ds = p * (_mm_nt(do, v_ref[...]) - d_ref[...])
            dq_scr[...] += _mm(ds.astype(BF16), k)
            dc_scr[...] += jnp.sum(ds, axis=-1, keepdims=True)

        @pl.when(j < i)
        def _():
            step(False)

        @pl.when(j == i)
        def _():
            step(True)
            dq_ref[...] = (dq_scr[...] * scale).astype(BF16)
            dc_ref[...] = dc_scr[...]

    return pl.pallas_call(
        body, name=name, grid=(H, nq, nq),
        in_specs=[pl.BlockSpec((None, ta, HEAD_DIM), lambda h, i, j: (0, i, h)),
                  pl.BlockSpec((None, ta, HEAD_DIM), lambda h, i, j: (1, jnp.minimum(i, j), h)),
                  pl.BlockSpec((None, ta, HEAD_DIM), lambda h, i, j: (2, jnp.minimum(i, j), h)),
                  pl.BlockSpec((ta, HEAD_DIM), lambda h, i, j: (i, h)),
                  pl.BlockSpec((None, 1, ta), lambda h, i, j: (h, 0, jnp.minimum(i, j))),
                  pl.BlockSpec((None, ta, 1), lambda h, i, j: (h, i, 0)),
                  pl.BlockSpec((None, ta, 1), lambda h, i, j: (h, i, 0))],
        out_specs=[pl.BlockSpec((ta, HEAD_DIM), lambda h, i, j: (i, h)),
                   pl.BlockSpec((None, ta, 1), lambda h, i, j: (h, i, 0))],
        out_shape=[jax.ShapeDtypeStruct((T, W), BF16), jax.ShapeDtypeStruct((H, T, 1), F32)],
        scratch_shapes=[pltpu.VMEM((ta, HEAD_DIM), F32), pltpu.VMEM((ta, 1), F32)],
        compiler_params=_params(("parallel", "parallel", "arbitrary")),
    )(z7, z7, z7, dob, c_row, lse_col, d_col)


ATTN_TILE = 512
ATTN_CHAINS = 2


def _attn_geometry(T):
    ta = _blk(T, ATTN_TILE)
    nc = ATTN_CHAINS if (T // ta) % ATTN_CHAINS == 0 else 1
    return ta, nc, T // ta


def _causal_tile(ta, keys_on_rows=False):
    rows = lax.broadcasted_iota(jnp.int32, (ta, ta), 0)
    cols = lax.broadcasted_iota(jnp.int32, (ta, ta), 1)
    return rows <= cols if keys_on_rows else cols <= rows


def _chunk(ref, j, ta):
    return ref[pl.ds(pl.multiple_of(j * ta, ta), ta), :]


def _attn_fwd_loop(z7, c_chunks, name):
    _, T, W = z7.shape
    H = W // HEAD_DIM
    ta, nc, n_chunks = _attn_geometry(T)
    scale = np.float32(1.0 / np.sqrt(HEAD_DIM))

    def body(q_ref, k_ref, v_ref, c_ref, o_ref, lse_ref, m_scr, l_scr, acc_scr):
        g = pl.program_id(1)
        m_scr[...] = jnp.full_like(m_scr, NEG_BIG)
        l_scr[...] = jnp.zeros_like(l_scr)
        acc_scr[...] = jnp.zeros_like(acc_scr)

        def update(ch, k, v, crow, diagonal):
            q = q_ref[ch * ta:(ch + 1) * ta, :]
            s = _mm_nt(q, k) * scale - crow
            if diagonal:
                s = jnp.where(_causal_tile(ta), s, NEG_BIG)
            m_prev = m_scr[ch]
            m_new = jnp.maximum(m_prev, jnp.max(s, axis=-1, keepdims=True))
            alpha = jnp.exp(m_prev - m_new)
            p = jnp.exp(s - m_new)
            l_scr[ch] = alpha * l_scr[ch] + jnp.sum(p, axis=-1, keepdims=True)
            acc_scr[ch] = alpha * acc_scr[ch] + _mm(p.astype(BF16), v)
            m_scr[ch] = m_new

        def full_chunk(j, carry):
            k = _chunk(k_ref, j, ta)
            v = _chunk(v_ref, j, ta)
            crow = c_ref[j]
            for ch in range(nc):
                update(ch, k, v, crow, False)
            return carry

        lax.fori_loop(0, nc * g, full_chunk, 0)
        for jj in range(nc):
            j = nc * g + jj
            k = _chunk(k_ref, j, ta)
            v = _chunk(v_ref, j, ta)
            crow = c_ref[j]
            for ch in range(jj, nc):
                update(ch, k, v, crow, ch == jj)
        for ch in range(nc):
            l = l_scr[ch]
            o_ref[ch * ta:(ch + 1) * ta, :] = acc_scr[ch] / l
            lse_ref[ch * ta:(ch + 1) * ta, :] = m_scr[ch] + jnp.log(l)

    tq = nc * ta
    return pl.pallas_call(
        body, name=name, grid=(H, n_chunks // nc),
        in_specs=[pl.BlockSpec((None, tq, HEAD_DIM), lambda h, g: (0, g, h)),
                  pl.BlockSpec((None, T, HEAD_DIM), lambda h, g: (1, 0, h)),
                  pl.BlockSpec((None, T, HEAD_DIM), lambda h, g: (2, 0, h)),
                  pl.BlockSpec((None, n_chunks, 1, ta), lambda h, g: (h, 0, 0, 0))],
        out_specs=[pl.BlockSpec((tq, HEAD_DIM), lambda h, g: (g, h)),
                   pl.BlockSpec((None, tq, 1), lambda h, g: (h, g, 0))],
        out_shape=[jax.ShapeDtypeStruct((T, W), F32), jax.ShapeDtypeStruct((H, T, 1), F32)],
        scratch_shapes=[pltpu.VMEM((nc, ta, 1), F32), pltpu.VMEM((nc, ta, 1), F32),
                        pltpu.VMEM((nc, ta, HEAD_DIM), F32)],
        compiler_params=_params(("parallel", "arbitrary")),
    )(z7, z7, z7, c_chunks)


def _attn_fwd_keys_on_rows(z7, vt, c_rep, name):
    _, T, W = z7.shape
    H = W // HEAD_DIM
    ta, nc, n_chunks = _attn_geometry(T)
    scale = np.float32(1.0 / np.sqrt(HEAD_DIM))
    reps = ta // LANES

    def body(q_ref, k_ref, vt_ref, c_ref, o_ref, lse_ref, m_scr, l_scr, acc_scr):
        g = pl.program_id(1)
        m_scr[...] = jnp.full_like(m_scr, NEG_BIG)
        l_scr[...] = jnp.zeros_like(l_scr)
        acc_scr[...] = jnp.zeros_like(acc_scr)

        def update(ch, k, vt, cj, diagonal):
            q = q_ref[ch * ta:(ch + 1) * ta, :]
            st = _mm_nt(k, q) * scale - cj
            if diagonal:
                st = jnp.where(_causal_tile(ta, keys_on_rows=True), st, NEG_BIG)
            m_prev = m_scr[ch]
            m_new = jnp.maximum(m_prev, jnp.max(st, axis=0, keepdims=True))
            alpha = jnp.exp(m_prev - m_new)
            pt = jnp.exp(st - m_new)
            l_scr[ch] = alpha * l_scr[ch] + jnp.sum(pt, axis=0, keepdims=True)
            acc_scr[ch] = alpha * acc_scr[ch] + _mm(vt, pt.astype(BF16))
            m_scr[ch] = m_new

        def load(j):
            cj = _chunk(c_ref, j, ta)
            return _chunk(k_ref, j, ta), vt_ref[j], jnp.concatenate([cj] * reps, axis=1)

        def full_chunk(j, carry):
            k, vt, cj = load(j)
            for ch in range(nc):
                update(ch, k, vt, cj, False)
            return carry

        lax.fori_loop(0, nc * g, full_chunk, 0)
        for jj in range(nc):
            k, vt, cj = load(nc * g + jj)
            for ch in range(jj, nc):
                update(ch, k, vt, cj, ch == jj)
        for ch in range(nc):
            l = l_scr[ch]
            o_ref[ch * ta:(ch + 1) * ta, :] = (acc_scr[ch] / l).T
            lse_ref[ch] = m_scr[ch] + jnp.log(l)

    tq = nc * ta
    return pl.pallas_call(
        body, name=name, grid=(H, n_chunks // nc),
        in_specs=[pl.BlockSpec((None, tq, HEAD_DIM), lambda h, g: (0, g, h)),
                  pl.BlockSpec((None, T, HEAD_DIM), lambda h, g: (1, 0, h)),
                  pl.BlockSpec((None, n_chunks, HEAD_DIM, ta), lambda h, g: (h, 0, 0, 0)),
                  pl.BlockSpec((None, T, LANES), lambda h, g: (h, 0, 0))],
        out_specs=[pl.BlockSpec((tq, HEAD_DIM), lambda h, g: (g, h)),
                   pl.BlockSpec((None, nc, 1, ta), lambda h, g: (h, g, 0, 0))],
        out_shape=[jax.ShapeDtypeStruct((T, W), F32), jax.ShapeDtypeStruct((H, n_chunks, 1, ta), F32)],
        scratch_shapes=[pltpu.VMEM((nc, 1, ta), F32), pltpu.VMEM((nc, 1, ta), F32),
                        pltpu.VMEM((nc, HEAD_DIM, ta), F32)],
        compiler_params=_params(("parallel", "arbitrary")),
    )(z7, z7, vt, c_rep)


def _attn_bwd_fused(z7, kt, dob, c_rep, lse_chunks, d_chunks, name):
    _, T, W = z7.shape
    H = W // HEAD_DIM
    ta, nc, n_chunks = _attn_geometry(T)
    n_steps = n_chunks // nc
    scale = np.float32(1.0 / np.sqrt(HEAD_DIM))
    reps = ta // LANES

    def body(k_ref, v_ref, kt_ref, q_ref, do_ref, c_ref, lse_ref, d_ref,
             dk_ref, dv_ref, dck_ref, dq_ref, dcq_ref, dk_scr, dv_scr, dck_scr, dqt_scr, dcq_scr):
        g = pl.program_id(1)

        @pl.when(g == 0)
        def _():
            dqt_scr[...] = jnp.zeros_like(dqt_scr)
            dcq_scr[...] = jnp.zeros_like(dcq_scr)

        dk_scr[...] = jnp.zeros_like(dk_scr)
        dv_scr[...] = jnp.zeros_like(dv_scr)
        dck_scr[...] = jnp.zeros_like(dck_scr)

        def update(ch, i, q, do, diagonal):
            rows = slice(ch * ta, (ch + 1) * ta)
            cj = c_ref[rows, :]
            st = _mm_nt(k_ref[rows, :], q) * scale - jnp.concatenate([cj] * reps, axis=1) - lse_ref[i]
            if diagonal:
                st = jnp.where(_causal_tile(ta, keys_on_rows=True), st, NEG_BIG)
            pt = jnp.exp(st)
            dv_scr[ch] += _mm(pt.astype(BF16), do)
            dst = pt * (_mm_nt(v_ref[rows, :], do) - d_ref[i])
            dst_b = dst.astype(BF16)
            dk_scr[ch] += _mm(dst_b, q)
            dqt_scr[i] += _mm(kt_ref[ch], dst_b)
            dcq_scr[i] += jnp.sum(dst, axis=0, keepdims=True)
            lane_sum = dst[:, :LANES]
            for r in range(1, reps):
                lane_sum = lane_sum + dst[:, r * LANES:(r + 1) * LANES]
            dck_scr[ch] += lane_sum

        for ii in range(nc):
            i = nc * g + ii
            q = _chunk(q_ref, i, ta)
            do = _chunk(do_ref, i, ta)
            for ch in range(0, ii + 1):
                update(ch, i, q, do, ch == ii)

        def full_chunk(i, carry):
            q = _chunk(q_ref, i, ta)
            do = _chunk(do_ref, i, ta)
            for ch in range(nc):
                update(ch, i, q, do, False)
            return carry

        lax.fori_loop(nc * (g + 1), n_chunks, full_chunk, 0)
        for ch in range(nc):
            rows = slice(ch * ta, (ch + 1) * ta)
            dk_ref[rows, :] = (dk_scr[ch] * scale).astype(BF16)
            dv_ref[rows, :] = dv_scr[ch].astype(BF16)
            dck_ref[rows, :] = -jnp.sum(dck_scr[ch], axis=-1, keepdims=True)

        @pl.when(g == n_steps - 1)
        def _():
            for i in range(n_chunks):
                dq_ref[i * ta:(i + 1) * ta, :] = (dqt_scr[i] * scale).T.astype(BF16)
            dcq_ref[...] = dcq_scr[...]

    tk = nc * ta
    chunks = pl.BlockSpec((None, n_chunks, 1, ta), lambda h, g: (h, 0, 0, 0))
    tile = pl.BlockSpec((tk, HEAD_DIM), lambda h, g: (g, h))
    return pl.pallas_call(
        body, name=name, grid=(H, n_steps),
        in_specs=[pl.BlockSpec((None, tk, HEAD_DIM), lambda h, g: (1, g, h)),
                  pl.BlockSpec((None, tk, HEAD_DIM), lambda h, g: (2, g, h)),
                  pl.BlockSpec((None, nc, HEAD_DIM, ta), lambda h, g: (h, g, 0, 0)),
                  pl.BlockSpec((None, T, HEAD_DIM), lambda h, g: (0, 0, h)),
                  pl.BlockSpec((T, HEAD_DIM), lambda h, g: (0, h)),
                  pl.BlockSpec((None, tk, LANES), lambda h, g: (h, g, 0)),
                  chunks, chunks],
        out_specs=[tile, tile, pl.BlockSpec((None, tk, 1), lambda h, g: (h, g, 0)),
                   pl.BlockSpec((T, HEAD_DIM), lambda h, g: (0, h)), chunks],
        out_shape=[jax.ShapeDtypeStruct((T, W), BF16), jax.ShapeDtypeStruct((T, W), BF16),
                   jax.ShapeDtypeStruct((H, T, 1), F32), jax.ShapeDtypeStruct((T, W), BF16),
                   jax.ShapeDtypeStruct((H, n_chunks, 1, ta), F32)],
        scratch_shapes=[pltpu.VMEM((nc, ta, HEAD_DIM), F32), pltpu.VMEM((nc, ta, HEAD_DIM), F32),
                        pltpu.VMEM((nc, ta, LANES), F32), pltpu.VMEM((n_chunks, HEAD_DIM, ta), F32),
                        pltpu.VMEM((n_chunks, 1, ta), F32)],
        compiler_params=_params(("parallel", "arbitrary")),
    )(z7, z7, kt, z7, dob, c_rep, lse_chunks, d_chunks)


def _attn_bwd_q_loop(z7, dob, c_chunks, lse_col, d_col, name):
    _, T, W = z7.shape
    H = W // HEAD_DIM
    ta, nc, n_chunks = _attn_geometry(T)
    scale = np.float32(1.0 / np.sqrt(HEAD_DIM))

    def body(q_ref, k_ref, v_ref, do_ref, c_ref, lse_ref, d_ref, dq_ref, dc_ref, dq_scr, dc_scr):
        g = pl.program_id(1)
        dq_scr[...] = jnp.zeros_like(dq_scr)
        dc_scr[...] = jnp.zeros_like(dc_scr)

        def update(ch, k, v, crow, diagonal):
            rows = slice(ch * ta, (ch + 1) * ta)
            do = do_ref[rows, :]
            s = _mm_nt(q_ref[rows, :], k) * scale - crow - lse_ref[rows, :]
            if diagonal:
                s = jnp.where(_causal_tile(ta), s, NEG_BIG)
            p = jnp.exp(s)
            ds = p * (_mm_nt(do, v) - d_ref[rows, :])
            dq_scr[ch] += _mm(ds.astype(BF16), k)
            dc_scr[ch] += jnp.sum(ds, axis=-1, keepdims=True)

        def full_chunk(j, carry):
            k = _chunk(k_ref, j, ta)
            v = _chunk(v_ref, j, ta)
            crow = c_ref[j]
            for ch in range(nc):
                update(ch, k, v, crow, False)
            return carry

        lax.fori_loop(0, nc * g, full_chunk, 0)
        for jj in range(nc):
            j = nc * g + jj
            k = _chunk(k_ref, j, ta)
            v = _chunk(v_ref, j, ta)
            crow = c_ref[j]
            for ch in range(jj, nc):
                update(ch, k, v, crow, ch == jj)
        for ch in range(nc):
            dq_ref[ch * ta:(ch + 1) * ta, :] = (dq_scr[ch] * scale).astype(BF16)
            dc_ref[ch * ta:(ch + 1) * ta, :] = dc_scr[ch]

    tq = nc * ta
    col = pl.BlockSpec((None, tq, 1), lambda h, g: (h, g, 0))
    return pl.pallas_call(
        body, name=name, grid=(H, n_chunks // nc),
        in_specs=[pl.BlockSpec((None, tq, HEAD_DIM), lambda h, g: (0, g, h)),
                  pl.BlockSpec((None, T, HEAD_DIM), lambda h, g: (1, 0, h)),
                  pl.BlockSpec((None, T, HEAD_DIM), lambda h, g: (2, 0, h)),
                  pl.BlockSpec((tq, HEAD_DIM), lambda h, g: (g, h)),
                  pl.BlockSpec((None, n_chunks, 1, ta), lambda h, g: (h, 0, 0, 0)),
                  col, col],
        out_specs=[pl.BlockSpec((tq, HEAD_DIM), lambda h, g: (g, h)), col],
        out_shape=[jax.ShapeDtypeStruct((T, W), BF16), jax.ShapeDtypeStruct((H, T, 1), F32)],
        scratch_shapes=[pltpu.VMEM((nc, ta, HEAD_DIM), F32), pltpu.VMEM((nc, ta, 1), F32)],
        compiler_params=_params(("parallel", "arbitrary")),
    )(z7, z7, z7, dob, c_chunks, lse_col, d_col)


def _attn_bwd_kv_loop(z7, dob, c_col, lse_chunks, d_chunks, name):
    _, T, W = z7.shape
    H = W // HEAD_DIM
    ta, nc, n_chunks = _attn_geometry(T)
    scale = np.float32(1.0 / np.sqrt(HEAD_DIM))

    def body(k_ref, v_ref, q_ref, do_ref, ccol_ref, lse_ref, d_ref, dk_ref, dv_ref, dc_ref, dk_scr, dv_scr, dc_scr):
        g = pl.program_id(1)
        dk_scr[...] = jnp.zeros_like(dk_scr)
        dv_scr[...] = jnp.zeros_like(dv_scr)
        dc_scr[...] = jnp.zeros_like(dc_scr)

        def update(ch, q, do, lse_row, d_row, diagonal):
            rows = slice(ch * ta, (ch + 1) * ta)
            st = _mm_nt(k_ref[rows, :], q) * scale - ccol_ref[rows, :] - lse_row
            if diagonal:
                st = jnp.where(_causal_tile(ta, keys_on_rows=True), st, NEG_BIG)
            pt = jnp.exp(st)
            dv_scr[ch] += _mm(pt.astype(BF16), do)
            dst = pt * (_mm_nt(v_ref[rows, :], do) - d_row)
            dk_scr[ch] += _mm(dst.astype(BF16), q)
            dc_scr[ch] += jnp.sum(dst, axis=-1, keepdims=True)

        for ii in range(nc):
            i = nc * g + ii
            q = _chunk(q_ref, i, ta)
            do = _chunk(do_ref, i, ta)
            for ch in range(0, ii + 1):
                update(ch, q, do, lse_ref[i], d_ref[i], ch == ii)

        def full_chunk(i, carry):
            q = _chunk(q_ref, i, ta)
            do = _chunk(do_ref, i, ta)
            for ch in range(nc):
                update(ch, q, do, lse_ref[i], d_ref[i], False)
            return carry

        lax.fori_loop(nc * (g + 1), n_chunks, full_chunk, 0)
        for ch in range(nc):
            rows = slice(ch * ta, (ch + 1) * ta)
            dk_ref[rows, :] = (dk_scr[ch] * scale).astype(BF16)
            dv_ref[rows, :] = dv_scr[ch].astype(BF16)
            dc_ref[rows, :] = -dc_scr[ch]

    tk = nc * ta
    chunks = pl.BlockSpec((None, n_chunks, 1, ta), lambda h, g: (h, 0, 0, 0))
    col = pl.BlockSpec((None, tk, 1), lambda h, g: (h, g, 0))
    tile = pl.BlockSpec((tk, HEAD_DIM), lambda h, g: (g, h))
    return pl.pallas_call(
        body, name=name, grid=(H, n_chunks // nc),
        in_specs=[pl.BlockSpec((None, tk, HEAD_DIM), lambda h, g: (1, g, h)),
                  pl.BlockSpec((None, tk, HEAD_DIM), lambda h, g: (2, g, h)),
                  pl.BlockSpec((None, T, HEAD_DIM), lambda h, g: (0, 0, h)),
                  pl.BlockSpec((T, HEAD_DIM), lambda h, g: (0, h)),
                  col, chunks, chunks],
        out_specs=[tile, tile, col],
        out_shape=[jax.ShapeDtypeStruct((T, W), BF16), jax.ShapeDtypeStruct((T, W), BF16),
                   jax.ShapeDtypeStruct((H, T, 1), F32)],
        scratch_shapes=[pltpu.VMEM((nc, ta, HEAD_DIM), F32), pltpu.VMEM((nc, ta, HEAD_DIM), F32),
                        pltpu.VMEM((nc, ta, 1), F32)],
        compiler_params=_params(("parallel", "arbitrary")),
    )(z7, z7, z7, dob, c_col, lse_chunks, d_chunks)


def _chunk_causal_mask():
    rows = lax.broadcasted_iota(jnp.int32, (SGU_LEN, SGU_LEN), 0)
    cols = lax.broadcasted_iota(jnp.int32, (SGU_LEN, SGU_LEN), 1)
    return (cols // CHUNK) <= (rows // CHUNK)


def _sgu_norm_mix(sv, lng_ref, lnb_ref, ws_ref, bs_ref, vn_scr, mixed_scr, vhat_scr=None):
    tm = sv.shape[0]
    vs = _gelu(sv)
    mask = _chunk_causal_mask()
    rstds = []
    for g in range(N_GROUPS):
        lanes = slice(g * GROUP_DIM, (g + 1) * GROUP_DIM)
        blk = vs[:, lanes]
        cen = blk - jnp.mean(blk, axis=-1, keepdims=True)
        rstd = lax.rsqrt(jnp.mean(cen * cen, axis=-1, keepdims=True) + LN_EPS)
        vhat = cen * rstd
        rstds.append(rstd)
        if vhat_scr is not None:
            vhat_scr[:, lanes] = vhat
        vn_scr[:, lanes] = (vhat * lng_ref[:, lanes] + lnb_ref[:, lanes]).astype(BF16)
        wm = jnp.where(mask, ws_ref[g], 0.0).astype(BF16)
        for w in range(tm // SGU_LEN):
            rows = slice(w * SGU_LEN, (w + 1) * SGU_LEN)
            mixed_scr[rows, lanes] = _mm(wm, vn_scr[rows, lanes]) + bs_ref[g]
    return rstds


def _mix_out_fwd(z7, o_a, x1, lng, lnb, ws, bs, w_out, g_post, name):
    _, T, W = z7.shape
    D = x1.shape[1]
    tm = _blk(T, 256)

    def body(u_ref, sv_ref, ga_ref, gb_ref, oa_ref, x1_ref, lng_ref, lnb_ref, ws_ref, bs_ref, wo_ref, gp_ref,
             x2_ref, p_ref, mb_ref, vn_scr, mixed_scr):
        _sgu_norm_mix(sv_ref[...].astype(F32), lng_ref, lnb_ref, ws_ref, bs_ref, vn_scr, mixed_scr)
        o_b = _gelu(u_ref[...].astype(F32)) * mixed_scr[...]
        merged = (jax.nn.sigmoid(ga_ref[...].astype(F32)) * oa_ref[...]
                  + jax.nn.sigmoid(gb_ref[...].astype(F32)) * o_b).astype(BF16)
        mb_ref[...] = merged
        p = _mm(merged, wo_ref[...])
        p_ref[...] = p
        x2_ref[...] = x1_ref[...] + p * _rms_scale(p) * gp_ref[...]

    def seg(idx):
        return pl.BlockSpec((None, tm, W), lambda i, idx=idx: (idx, i, 0))

    row = pl.BlockSpec((tm, D), lambda i: (i, 0))
    vec = pl.BlockSpec((1, D), lambda i: (0, 0))
    return pl.pallas_call(
        body, name=name, grid=(T // tm,),
        in_specs=[seg(3), seg(4), seg(5), seg(6), row, row, vec, vec,
                  pl.BlockSpec((N_GROUPS, SGU_LEN, SGU_LEN), lambda i: (0, 0, 0)),
                  pl.BlockSpec((N_GROUPS, SGU_LEN, 1), lambda i: (0, 0, 0)),
                  pl.BlockSpec((D, D), lambda i: (0, 0)), vec],
        out_specs=[row, row, row],
        out_shape=[jax.ShapeDtypeStruct((T, D), F32), jax.ShapeDtypeStruct((T, D), F32),
                   jax.ShapeDtypeStruct((T, D), BF16)],
        scratch_shapes=[pltpu.VMEM((tm, W), BF16), pltpu.VMEM((tm, W), F32)],
        compiler_params=_params(("parallel",)),
    )(z7, z7, z7, z7, o_a, x1, lng, lnb, ws, bs, w_out, g_post)


def _mix_out_bwd(dx2, p, z7, o_a, lng, lnb, ws, bs, w_out, g_post, name, dep=None):
    _, T, W = z7.shape
    D = dx2.shape[1]
    tm = _blk(T, 256)
    n_w = tm // SGU_LEN

    def body(dx2_ref, p_ref, u_ref, sv_ref, ga_ref, gb_ref, oa_ref, lng_ref, lnb_ref, ws_ref, bs_ref, wo_ref, gp_ref, _,
             dpb_ref, dob_ref, dvec_ref, dz_ref, dgp_ref, dlng_ref, dlnb_ref, dws_ref, dbs_ref,
             vn_scr, mixed_scr, vhat_scr, dmix_scr, dvn_scr):
        @pl.when(pl.program_id(0) == 0)
        def _():
            dgp_ref[...] = jnp.zeros_like(dgp_ref)
            dlng_ref[...] = jnp.zeros_like(dlng_ref)
            dlnb_ref[...] = jnp.zeros_like(dlnb_ref)
            dws_ref[...] = jnp.zeros_like(dws_ref)
            dbs_ref[...] = jnp.zeros_like(dbs_ref)

        pv = p_ref[...]
        s = _rms_scale(pv)
        n = pv * s
        dn = dx2_ref[...]
        dgp_ref[...] += jnp.sum(dn * n, axis=0, keepdims=True)
        dpb = _rms_bwd(dn, n, s, gp_ref[...]).astype(BF16)
        dpb_ref[...] = dpb
        dmerged = _mm_nt(dpb, wo_ref[...])

        sv = sv_ref[...].astype(F32)
        rstds = _sgu_norm_mix(sv, lng_ref, lnb_ref, ws_ref, bs_ref, vn_scr, mixed_scr, vhat_scr)
        u_pre = u_ref[...].astype(F32)
        u = _gelu(u_pre)
        mixed = mixed_scr[...]
        sa = jax.nn.sigmoid(ga_ref[...].astype(F32))
        sb = jax.nn.sigmoid(gb_ref[...].astype(F32))
        oa = oa_ref[...]
        do_a = (dmerged * sa).astype(BF16)
        dob_ref[...] = do_a
        prod = do_a.astype(F32) * oa
        for h in range(N_HEADS):
            dvec_ref[h] = jnp.sum(prod[:, h * HEAD_DIM:(h + 1) * HEAD_DIM], axis=-1, keepdims=True)
        dz_ref[2] = (dmerged * oa * (sa * (1.0 - sa))).astype(BF16)
        dz_ref[3] = (dmerged * (u * mixed) * (sb * (1.0 - sb))).astype(BF16)
        do_b = dmerged * sb
        dz_ref[0] = (do_b * mixed * _gelu_grad(u_pre)).astype(BF16)
        dmix_scr[...] = do_b * u

        mask = _chunk_causal_mask()
        for g in range(N_GROUPS):
            lanes = slice(g * GROUP_DIM, (g + 1) * GROUP_DIM)
            wm = jnp.where(mask, ws_ref[g], 0.0).astype(BF16)
            dws = jnp.zeros((SGU_LEN, SGU_LEN), F32)
            dbs = jnp.zeros((SGU_LEN, 1), F32)
            for w in range(n_w):
                rows = slice(w * SGU_LEN, (w + 1) * SGU_LEN)
                dmix = dmix_scr[rows, lanes]
                dmix_b = dmix.astype(BF16)
                dvn_scr[rows, lanes] = _mm_tn(wm, dmix_b)
                dws = dws + _mm_nt(dmix_b, vn_scr[rows, lanes])
                dbs = dbs + jnp.sum(dmix, axis=-1, keepdims=True)
            dws_ref[g] += jnp.where(mask, dws, 0.0)
            dbs_ref[g] += dbs
            dvn = dvn_scr[:, lanes]
            vhat = vhat_scr[:, lanes]
            dlng_ref[:, lanes] += jnp.sum(dvn * vhat, axis=0, keepdims=True)
            dlnb_ref[:, lanes] += jnp.sum(dvn, axis=0, keepdims=True)
            dvh = dvn * lng_ref[:, lanes]
            dvs = rstds[g] * (dvh - jnp.mean(dvh, axis=-1, keepdims=True)
                              - vhat * jnp.mean(dvh * vhat, axis=-1, keepdims=True))
            dvn_scr[:, lanes] = dvs
        dz_ref[1] = (dvn_scr[...] * _gelu_grad(sv)).astype(BF16)

    def seg(idx):
        return pl.BlockSpec((None, tm, W), lambda i, idx=idx: (idx, i, 0))

    row = pl.BlockSpec((tm, D), lambda i: (i, 0))
    vec = pl.BlockSpec((1, D), lambda i: (0, 0))
    ws_spec = pl.BlockSpec((N_GROUPS, SGU_LEN, SGU_LEN), lambda i: (0, 0, 0))
    bs_spec = pl.BlockSpec((N_GROUPS, SGU_LEN, 1), lambda i: (0, 0, 0))
    return pl.pallas_call(
        body, name=name, grid=(T // tm,),
        in_specs=[row, row, seg(3), seg(4), seg(5), seg(6), row, vec, vec, ws_spec, bs_spec,
                  pl.BlockSpec((D, D), lambda i: (0, 0)), vec, ANY],
        out_specs=[row, row, pl.BlockSpec((N_HEADS, tm, 1), lambda i: (0, i, 0)),
                   pl.BlockSpec((4, tm, W), lambda i: (0, i, 0)), vec, vec, vec, ws_spec, bs_spec],
        out_shape=[jax.ShapeDtypeStruct((T, D), BF16), jax.ShapeDtypeStruct((T, W), BF16),
                   jax.ShapeDtypeStruct((N_HEADS, T, 1), F32), jax.ShapeDtypeStruct((4, T, W), BF16),
                   jax.ShapeDtypeStruct((1, D), F32), jax.ShapeDtypeStruct((1, D), F32),
                   jax.ShapeDtypeStruct((1, D), F32),
                   jax.ShapeDtypeStruct((N_GROUPS, SGU_LEN, SGU_LEN), F32),
                   jax.ShapeDtypeStruct((N_GROUPS, SGU_LEN, 1), F32)],
        scratch_shapes=[pltpu.VMEM((tm, W), BF16), pltpu.VMEM((tm, W), F32), pltpu.VMEM((tm, W), F32),
                        pltpu.VMEM((tm, W), F32), pltpu.VMEM((tm, W), F32)],
        compiler_params=_params(("arbitrary",)),
    )(dx2, p, z7, z7, z7, z7, o_a, lng, lnb, ws, bs, w_out, g_post, _after(dep))


def _loss_head(y, target, name):
    T, D = y.shape
    tm = _blk(T, 1024)
    n_i = T // tm

    def body(y_ref, t_ref, dy_ref, loss_ref, acc_scr):
        i = pl.program_id(0)

        @pl.when(i == 0)
        def _():
            acc_scr[...] = jnp.zeros_like(acc_scr)

        e = y_ref[...] - t_ref[...]
        dy_ref[...] = e * np.float32(1.0 / D)
        acc_scr[...] += jnp.sum(e * e, axis=0, keepdims=True)

        @pl.when(i == n_i - 1)
        def _():
            total = jnp.sum(acc_scr[...], axis=-1, keepdims=True) * np.float32(0.5 / D)
            loss_ref[...] = jnp.broadcast_to(total, loss_ref.shape)

    row = pl.BlockSpec((tm, D), lambda i: (i, 0))
    return pl.pallas_call(
        body, name=name, grid=(n_i,),
        in_specs=[row, row],
        out_specs=[row, pl.BlockSpec((1, LANES), lambda i: (0, 0))],
        out_shape=[jax.ShapeDtypeStruct((T, D), F32), jax.ShapeDtypeStruct((1, LANES), F32)],
        scratch_shapes=[pltpu.VMEM((1, D), F32)],
        compiler_params=_params(("arbitrary",)),
    )(y, target)


def _adamw_math(w, g, m, v):
    m_new = ADAM_B1 * m + (1.0 - ADAM_B1) * g
    v_new = ADAM_B2 * v + (1.0 - ADAM_B2) * (g * g)
    m_hat = m_new / np.float32(1.0 - ADAM_B1 ** ADAM_STEP)
    v_hat = v_new / np.float32(1.0 - ADAM_B2 ** ADAM_STEP)
    delta = -ADAM_LR * (m_hat / (jnp.sqrt(v_hat) + ADAM_EPS) + ADAM_WD * w)
    return delta, m_new, v_new


def _sum_adamw(parts, w, m, v, name):
    n, R, C = parts.shape
    tr = _blk(R, 128)

    def body(p_ref, w_ref, m_ref, v_ref, g_ref, d_ref, mo_ref, vo_ref):
        g = p_ref[0].astype(F32)
        for s in range(1, n):
            g = g + p_ref[s].astype(F32)
        delta, m_new, v_new = _adamw_math(w_ref[...], g, m_ref[...], v_ref[...])
        g_ref[...] = g
        d_ref[...] = delta
        mo_ref[...] = m_new
        vo_ref[...] = v_new

    row = pl.BlockSpec((tr, C), lambda i: (i, 0))
    shp = jax.ShapeDtypeStruct((R, C), F32)
    return pl.pallas_call(
        body, name=name, grid=(R // tr,),
        in_specs=[pl.BlockSpec((n, tr, C), lambda i: (0, i, 0)), row, row, row],
        out_specs=[row, row, row, row], out_shape=[shp, shp, shp, shp],
        compiler_params=_params(("parallel",)),
    )(parts, w, m, v)


def _adamw(g, w, m, v, name):
    R, C = g.shape
    tr = _blk(R, 128)

    def body(g_ref, w_ref, m_ref, v_ref, d_ref, mo_ref, vo_ref):
        delta, m_new, v_new = _adamw_math(w_ref[...], g_ref[...], m_ref[...], v_ref[...])
        d_ref[...] = delta
        mo_ref[...] = m_new
        vo_ref[...] = v_new

    row = pl.BlockSpec((tr, C), lambda i: (i, 0))
    shp = jax.ShapeDtypeStruct((R, C), F32)
    return pl.pallas_call(
        body, name=name, grid=(R // tr,),
        in_specs=[row, row, row, row], out_specs=[row, row, row], out_shape=[shp, shp, shp],
        compiler_params=_params(("parallel",)),
    )(g, w, m, v)


def _position():
    return lax.axis_index("x"), lax.axis_index("y"), lax.axis_index("c")


def _slot(px, py, pc):
    return 4 * px + 2 * py + pc


def _all_gather(shards, name):
    n = len(shards)

    def body(*refs):
        ins, outs = refs[:n], refs[n:2 * n]
        send_sems, recv_sems, local_sems = refs[2 * n:]
        x, y, c = _position()
        me, sibling = (x, y, c), (x, y, 1 - c)
        chips = [(1 - x, y), (x, 1 - y), (1 - x, 1 - y)]

        def copy(a, k, block, to, src=None):
            dst = outs[a].at[_slot(*block)]
            return pltpu.make_async_remote_copy(
                src_ref=dst if src is None else src, dst_ref=dst,
                send_sem=send_sems.at[a, k], recv_sem=recv_sems.at[a, k],
                device_id=to, device_id_type=MESH)

        mine = [pltpu.make_async_copy(ins[a], outs[a].at[_slot(*me)], local_sems.at[a]) for a in range(n)]
        for cp in mine:
            cp.start()
        first = []
        for a in range(n):
            first.append(copy(a, 0, me, sibling, src=ins[a]))
            first += [copy(a, 1 + j, me, (*chip, c), src=ins[a]) for j, chip in enumerate(chips)]
        for cp in first:
            cp.start()
        passed = []
        for j, chip in enumerate(chips):
            for a in range(n):
                copy(a, 1 + j, (*chip, c), me).wait_recv()
                fwd = copy(a, 4 + j, (*chip, c), sibling)
                fwd.start()
                passed.append(fwd)
        for a in range(n):
            copy(a, 0, sibling, me).wait_recv()
            for j, chip in enumerate(chips):
                copy(a, 4 + j, (*chip, 1 - c), me).wait_recv()
        for cp in first + passed:
            cp.wait_send()
        for cp in mine:
            cp.wait()

    return pl.pallas_call(
        body, name=name,
        in_specs=[ANY] * n, out_specs=[ANY] * n,
        out_shape=[jax.ShapeDtypeStruct((N_DEV,) + s.shape, s.dtype) for s in shards],
        scratch_shapes=[pltpu.SemaphoreType.DMA((n, 7)), pltpu.SemaphoreType.DMA((n, 7)),
                        pltpu.SemaphoreType.DMA((n,))],
    )(*shards)


def _peer(x, y, c, k):
    return (1 - x if k & 4 else x, 1 - y if k & 2 else y, 1 - c if k & 1 else c)


def _exchange(parts, name):
    n = len(parts)

    def body(*refs):
        ins, outs = refs[:n], refs[n:2 * n]
        send_sems, recv_sems, local_sems = refs[2 * n:]
        x, y, c = _position()
        me = _slot(x, y, c)
        mine = [pltpu.make_async_copy(ins[a].at[me], outs[a].at[me], local_sems.at[a]) for a in range(n)]
        for cp in mine:
            cp.start()
        sends = []
        for k in range(1, N_DEV):
            to = _peer(x, y, c, k)
            for a in range(n):
                cp = pltpu.make_async_remote_copy(
                    src_ref=ins[a].at[_slot(*to)], dst_ref=outs[a].at[me],
                    send_sem=send_sems.at[a, k - 1], recv_sem=recv_sems.at[a, k - 1],
                    device_id=to, device_id_type=MESH)
                cp.start()
                sends.append(cp)
        for k in range(1, N_DEV):
            frm = _peer(x, y, c, k)
            for a in range(n):
                pltpu.make_async_remote_copy(
                    src_ref=ins[a].at[_slot(*frm)], dst_ref=outs[a].at[_slot(*frm)],
                    send_sem=send_sems.at[a, k - 1], recv_sem=recv_sems.at[a, k - 1],
                    device_id=frm, device_id_type=MESH).wait_recv()
        for cp in sends:
            cp.wait_send()
        for cp in mine:
            cp.wait()

    return pl.pallas_call(
        body, name=name,
        in_specs=[ANY] * n, out_specs=[ANY] * n,
        out_shape=[jax.ShapeDtypeStruct(p.shape, p.dtype) for p in parts],
        scratch_shapes=[pltpu.SemaphoreType.DMA((n, 7)), pltpu.SemaphoreType.DMA((n, 7)),
                        pltpu.SemaphoreType.DMA((n,))],
    )(*parts)


HBM_SPEC = pl.BlockSpec(memory_space=pltpu.HBM)
SEM_SPEC = pl.BlockSpec(memory_space=pltpu.SEMAPHORE)
SIDE_EFFECT = pltpu.SideEffectType.DATAFLOW_SIDE_EFFECTING


def _remote_copies(src_refs, land_refs, send_sems, recv_sems, gather, outgoing):
    x, y, c = _position()
    me = _slot(x, y, c)
    copies = []
    for k in range(1, N_DEV):
        peer = _peer(x, y, c, k)
        for a in range(len(src_refs)):
            src = src_refs[a] if gather else src_refs[a].at[_slot(*peer)]
            dst = land_refs[a].at[me if outgoing else _slot(*peer)]
            sem = a * (N_DEV - 1) + k - 1
            copies.append(pltpu.make_async_remote_copy(
                src_ref=src, dst_ref=dst, send_sem=send_sems.at[sem], recv_sem=recv_sems.at[sem],
                device_id=peer, device_id_type=MESH))
    return copies


def _remote_start(srcs, after, name, gather):
    n = len(srcs)
    lands = [jax.ShapeDtypeStruct(((N_DEV,) + s.shape) if gather else s.shape, s.dtype) for s in srcs]

    def body(*refs):
        src_refs, land_refs = refs[:n], refs[n:2 * n]
        send_sems, recv_sems = refs[2 * n + 1], refs[2 * n + 2]
        token, local_sems = refs[4 * n + 3], refs[4 * n + 4]
        x, y, c = _position()
        me = _slot(x, y, c)
        mine = [pltpu.make_async_copy(src_refs[a] if gather else src_refs[a].at[me], land_refs[a].at[me],
                                      local_sems.at[a]) for a in range(n)]
        for cp in mine:
            cp.start()
        for cp in _remote_copies(src_refs, land_refs, send_sems, recv_sems, gather, outgoing=True):
            cp.start()
        for cp in mine:
            cp.wait()
        token[...] = jnp.zeros_like(token)

    sem_shape = pltpu.SemaphoreType.DMA((n * (N_DEV - 1),))
    outs = pl.pallas_call(
        body, name=name,
        out_shape=(sem_shape, sem_shape, *[pltpu.HBM(s.shape, s.dtype) for s in srcs],
                   *[pltpu.HBM(l.shape, l.dtype) for l in lands], jax.ShapeDtypeStruct((8, LANES), F32)),
        in_specs=[HBM_SPEC] * (2 * n) + [ANY],
        out_specs=(SEM_SPEC, SEM_SPEC, *([HBM_SPEC] * (2 * n)), pl.BlockSpec(memory_space=pltpu.VMEM)),
        input_output_aliases={a: 2 + a for a in range(2 * n)},
        scratch_shapes=[pltpu.SemaphoreType.DMA((n,))],
        compiler_params=pltpu.CompilerParams(has_side_effects=SIDE_EFFECT),
    )(*[pltpu.with_memory_space_constraint(s, pltpu.HBM) for s in srcs],
      *[pltpu.with_memory_space_constraint(lax.empty(l.shape, l.dtype), pltpu.HBM) for l in lands], after)
    return dict(send=outs[0], recv=outs[1], srcs=outs[2:2 + n], lands=outs[2 + n:2 + 2 * n], token=outs[-1],
                gather=gather)


def _remote_wait(flight, after, name):
    n = len(flight["srcs"])
    gather = flight["gather"]

    def body(*refs):
        src_refs, land_refs = refs[:n], refs[n:2 * n]
        send_sems, recv_sems = refs[2 * n], refs[2 * n + 1]
        for cp in _remote_copies(src_refs, land_refs, send_sems, recv_sems, gather, outgoing=False):
            cp.wait_send()
            cp.wait_recv()

    both = list(flight["srcs"]) + list(flight["lands"])
    outs = pl.pallas_call(
        body, name=name,
        out_shape=tuple(pltpu.HBM(a.shape, a.dtype) for a in both),
        in_specs=[HBM_SPEC] * (2 * n) + [SEM_SPEC, SEM_SPEC, ANY],
        out_specs=tuple([HBM_SPEC] * (2 * n)),
        input_output_aliases={a: a for a in range(2 * n)},
        compiler_params=pltpu.CompilerParams(has_side_effects=SIDE_EFFECT),
    )(*both, flight["send"], flight["recv"], after)
    return list(outs[n:])


def _sequencer_exchange(srcs, name, gather, collective_id):
    n = len(srcs)
    hbm = pltpu.MemorySpace.HBM
    src_refs = [jax.new_ref(s, memory_space=hbm) for s in srcs]
    land_refs = [jax.empty_ref(jax.ShapeDtypeStruct(((N_DEV,) + s.shape) if gather else s.shape, s.dtype),
                               memory_space=hbm) for s in srcs]
    n_sems = n * (N_DEV - 1)

    @pl.kernel(mesh=plsc.ScalarSubcoreMesh(axis_name="sequencer", num_cores=1), name=name,
               scratch_types=(pltpu.SemaphoreType.DMA((n_sems,)), pltpu.SemaphoreType.DMA((n_sems,)),
                              pltpu.SemaphoreType.DMA((n,))),
               compiler_params=pltpu.CompilerParams(collective_id=collective_id))
    def launch(send_sems, recv_sems, local_sems):
        x, y, c = _position()
        me = _slot(x, y, c)
        barrier = pltpu.get_barrier_semaphore()
        for k in range(1, N_DEV):
            pl.semaphore_signal(barrier, inc=1, device_id=_peer(x, y, c, k), device_id_type=MESH)
        pl.semaphore_wait(barrier, N_DEV - 1)
        mine = [pltpu.make_async_copy(src_refs[a] if gather else src_refs[a].at[me], land_refs[a].at[me],
                                      local_sems.at[a]) for a in range(n)]
        for cp in mine:
            cp.start()
        sends = _remote_copies(src_refs, land_refs, send_sems, recv_sems, gather, outgoing=True)
        for cp in sends:
            cp.start()
        for cp in _remote_copies(src_refs, land_refs, send_sems, recv_sems, gather, outgoing=False):
            cp.wait_recv()
        for cp in sends:
            cp.wait_send()
        for cp in mine:
            cp.wait()

    launch()
    return [r[...] for r in land_refs]


def _all_reduce_small(blob, name):
    R, C = blob.shape

    def body(in_ref, out_ref, gath, send_sems, recv_sems):
        x, y, c = _position()
        me = _slot(x, y, c)
        gath[me] = in_ref[...]
        sends = []
        for k in range(1, N_DEV):
            to = _peer(x, y, c, k)
            cp = pltpu.make_async_remote_copy(
                src_ref=in_ref, dst_ref=gath.at[me],
                send_sem=send_sems.at[k - 1], recv_sem=recv_sems.at[k - 1],
                device_id=to, device_id_type=MESH)
            cp.start()
            sends.append(cp)
        for k in range(1, N_DEV):
            frm = _peer(x, y, c, k)
            pltpu.make_async_remote_copy(
                src_ref=in_ref, dst_ref=gath.at[_slot(*frm)],
                send_sem=send_sems.at[k - 1], recv_sem=recv_sems.at[k - 1],
                device_id=frm, device_id_type=MESH).wait_recv()
        for cp in sends:
            cp.wait_send()
        total = gath[0]
        for s in range(1, N_DEV):
            total = total + gath[s]
        out_ref[...] = total

    return pl.pallas_call(
        body, name=name,
        in_specs=[pl.BlockSpec(memory_space=pltpu.VMEM)],
        out_specs=pl.BlockSpec(memory_space=pltpu.VMEM),
        out_shape=jax.ShapeDtypeStruct((R, C), F32),
        scratch_shapes=[pltpu.VMEM((N_DEV, R, C), F32), pltpu.SemaphoreType.DMA((7,)),
                        pltpu.SemaphoreType.DMA((7,))],
        compiler_params=pltpu.CompilerParams(vmem_limit_bytes=VMEM_LIMIT),
    )(blob)


SMALL_VECS = ("ffn1_pre_g", "ffn1_post_g", "mix_pre_g", "sgu_ln_g", "sgu_ln_b", "mix_post_g", "ffn2_pre_g",
              "ffn2_post_g")
ROW_BS = len(SMALL_VECS)
ROW_BF = ROW_BS + 1
ROW_LOSS = ROW_BF + 1
ROW_WS = 16
BLOB_ROWS = ROW_WS + SGU_LEN


def _pack_small(vals, D, loss_row=None):
    rows = [vals[n].reshape(1, D) for n in SMALL_VECS]
    rows.append(vals["sgu_b_s"].reshape(1, D))
    rows.append(jnp.pad(vals["b_forget"].reshape(1, N_HEADS), ((0, 0), (0, D - N_HEADS))))
    rows.append(jnp.zeros((1, D), F32) if loss_row is None else loss_row)
    rows.append(jnp.zeros((ROW_WS - ROW_LOSS - 1, D), F32))
    rows.append(vals["sgu_w_s"].reshape(SGU_LEN, D))
    return jnp.concatenate(rows, axis=0)


def _unpack_small(blob, D):
    out = {n: blob[r:r + 1] for r, n in enumerate(SMALL_VECS)}
    out["sgu_b_s"] = blob[ROW_BS].reshape(1, N_GROUPS, SGU_LEN)
    out["b_forget"] = blob[ROW_BF, :N_HEADS].reshape(1, N_HEADS)
    out["sgu_w_s"] = blob[ROW_WS:].reshape(1, N_GROUPS, SGU_LEN, SGU_LEN)
    return out


WEIGHT_NAMES = ("ffn1_pre_g", "ffn1_w_gate", "ffn1_w_up", "ffn1_w_down", "ffn1_post_g", "mix_pre_g", "w_in",
                "b_forget", "sgu_ln_g", "sgu_ln_b", "sgu_w_s", "sgu_b_s", "w_out", "mix_post_g", "ffn2_pre_g",
                "ffn2_w_gate", "ffn2_w_up", "ffn2_w_down", "ffn2_post_g")
BIG_NAMES = ("ffn1_w_gate", "ffn1_w_up", "ffn1_w_down", "w_in", "w_out", "ffn2_w_gate", "ffn2_w_up", "ffn2_w_down")
WEIGHT_GROUPS = {"ffn1": ("ffn1_w_gate", "ffn1_w_up", "ffn1_w_down"), "mix": ("w_in", "w_out"),
                 "ffn2": ("ffn2_w_gate", "ffn2_w_up", "ffn2_w_down")}
GRAD_GROUPS = (("ffn2_w_gate", "ffn2_w_up", "ffn2_w_down"), ("w_out", "w_in"), ("ffn1_w_down", "ffn1_w_gate"),
               ("ffn1_w_up",))


def _local_step(x, target, small, fetch, emit):
    T, D = x.shape
    W = N_HEADS * HEAD_DIM
    vec = lambda n: small[n].reshape(1, D)
    big = dict(fetch("ffn1", x))

    x1, y1, dgf1, silu1, act1 = _ffn_fwd(x, vec("ffn1_pre_g"), big["ffn1_w_gate"], big["ffn1_w_up"], big["ffn1_w_down"],
                                  vec("ffn1_post_g"), "ffn1_fwd")

    big.update(fetch("mix", x1))
    w_in_all = big["w_in"]
    in_width = N_DEV * w_in_all.shape[2]
    w_in = w_in_all.transpose(1, 0, 2).reshape(D, in_width)
    col_f = 3 * W
    col_u = col_f + N_HEADS
    seg_starts = (0, W, 2 * W, col_u, col_u + W, col_u + 2 * W, col_u + 3 * W)
    w7 = jnp.stack([w_in[:, s:s + W] for s in seg_starts])
    wf = jnp.pad(w_in[:, col_f:col_u], ((0, 0), (0, LANES - N_HEADS)))
    w_out = big["w_out"].reshape(D, D)
    b_pad = jnp.pad(small["b_forget"].reshape(1, N_HEADS), ((0, 0), (0, LANES - N_HEADS)))
    lng, lnb = vec("sgu_ln_g"), vec("sgu_ln_b")
    ws = small["sgu_w_s"].reshape(N_GROUPS, SGU_LEN, SGU_LEN)
    bs = small["sgu_b_s"].reshape(N_GROUPS, SGU_LEN, 1)

    z7, f_logit, h2b = _mix_in_fwd(x1, vec("mix_pre_g"), w7, wf, "mix_in_fwd")
    c = _forget_cumsum(f_logit, b_pad, "forget_cumsum")
    c_heads = c[:, :N_HEADS].T
    ta, _, n_chunks = _attn_geometry(T)
    c_chunks = c_heads.reshape(N_HEADS, n_chunks, 1, ta)
    c_col = c_heads[:, :, None]
    vt = z7[2].reshape(n_chunks, ta, N_HEADS, HEAD_DIM).transpose(2, 0, 3, 1)
    c_rep = jnp.broadcast_to(c_col, (N_HEADS, T, LANES))
    o_a, lse_chunks = _attn_fwd_keys_on_rows(z7, vt, c_rep, "attn_fwd")
    lse = lse_chunks.reshape(N_HEADS, T, 1)
    x2, p, merged_b = _mix_out_fwd(z7, o_a, x1, lng, lnb, ws, bs, w_out, vec("mix_post_g"), "mix_out_fwd")
    big.update(fetch("ffn2", x2))
    x3, y2, dgf2, silu2, act2 = _ffn_fwd(x2, vec("ffn2_pre_g"), big["ffn2_w_gate"], big["ffn2_w_up"], big["ffn2_w_down"],
                                  vec("ffn2_post_g"), "ffn2_fwd")
    dy, loss_lanes = _loss_head(x3, target, "loss_head")

    grads_small = {}

    dx2, h3b, dy2b, dgate2, dup2, dgpre, dgpost = _ffn_bwd(
        dy, x2, y2, dgf2, silu2, vec("ffn2_pre_g"), big["ffn2_w_gate"], big["ffn2_w_up"], big["ffn2_w_down"],
        vec("ffn2_post_g"), "ffn2_bwd")
    grads_small["ffn2_pre_g"] = jnp.sum(dgpre, axis=0)
    grads_small["ffn2_post_g"] = jnp.sum(dgpost, axis=0)
    emit("ffn2_w_gate", _wgrad(h3b, dgate2, "ffn2_wgrad_gate", shard_cols=True))
    emit("ffn2_w_up", _wgrad(h3b, dup2, "ffn2_wgrad_up", shard_cols=True))
    dep = emit("ffn2_w_down", _wgrad(act2, dy2b, "ffn2_wgrad_down").reshape(big["ffn2_w_down"].shape))

    dpb, dob, dvec, dz4, dgp, dlng, dlnb, dws, dbs = _mix_out_bwd(
        dx2, p, z7, o_a, lng, lnb, ws, bs, w_out, vec("mix_post_g"), "mix_out_bwd", dep=dep)
    grads_small["mix_post_g"] = dgp
    grads_small["sgu_ln_g"] = dlng
    grads_small["sgu_ln_b"] = dlnb
    grads_small["sgu_w_s"] = dws
    grads_small["sgu_b_s"] = dbs
    emit("w_out", _wgrad(merged_b, dpb, "w_out_wgrad").reshape(big["w_out"].shape))
    d_chunks = dvec.reshape(N_HEADS, n_chunks, 1, ta)
    kt = z7[1].reshape(n_chunks, ta, N_HEADS, HEAD_DIM).transpose(2, 0, 3, 1)
    dk, dv, dc, dq, dc_q = _attn_bwd_fused(z7, kt, dob, c_rep, lse_chunks, d_chunks, "attn_bwd")
    dc_pad = jnp.pad((dc.reshape(N_HEADS, T) + dc_q.reshape(N_HEADS, T)).T, ((0, 0), (0, LANES - N_HEADS)))
    dfb, dbf = _forget_bwd(dc_pad, f_logit, b_pad, "forget_bwd")
    grads_small["b_forget"] = dbf[:, :N_HEADS]
    segs = [(dq, None), (dk, None), (dv, None), (dz4, 0), (dz4, 1), (dz4, 2), (dz4, 3)]
    dx1, dgm = _mix_in_bwd(dx2, x1, vec("mix_pre_g"), segs, dfb, w7, wf, "mix_in_bwd")
    grads_small["mix_pre_g"] = jnp.sum(dgm, axis=0)
    seg_mats = [dq, dk, dv, dz4[0], dz4[1], dz4[2], dz4[3]]
    dw_seg = [_wgrad(h2b, sm, "w_in_wgrad_%d" % q) for q, sm in enumerate(seg_mats)]
    dwf = _wgrad(h2b, dfb, "w_in_wgrad_f")[:, :N_HEADS]
    dw_in = jnp.concatenate(dw_seg[:3] + [dwf] + dw_seg[3:], axis=1)
    dep = emit("w_in", dw_in.reshape(D, N_DEV, in_width // N_DEV).transpose(1, 0, 2))

    dx0, h1b, dy1b, dgate1, dup1, dgpre1, dgpost1 = _ffn_bwd(
        dx1, x, y1, dgf1, silu1, vec("ffn1_pre_g"), big["ffn1_w_gate"], big["ffn1_w_up"], big["ffn1_w_down"],
        vec("ffn1_post_g"), "ffn1_bwd", dep=dep)
    grads_small["ffn1_pre_g"] = jnp.sum(dgpre1, axis=0)
    grads_small["ffn1_post_g"] = jnp.sum(dgpost1, axis=0)
    emit("ffn1_w_down", _wgrad(act1, dy1b, "ffn1_wgrad_down").reshape(big["ffn1_w_down"].shape))
    dep = emit("ffn1_w_gate", _wgrad(h1b, dgate1, "ffn1_wgrad_gate", shard_cols=True))
    emit("ffn1_w_up", _wgrad(h1b, dup1, "ffn1_wgrad_up", shard_cols=True, dep=dep))

    loss_row = jnp.pad(loss_lanes, ((0, 0), (0, D - LANES)))
    return loss_row, dx0, grads_small


def kernel(x, ffn1_pre_g, ffn1_w_gate, ffn1_w_up, ffn1_w_down, ffn1_post_g, mix_pre_g, w_in, b_forget, sgu_ln_g, sgu_ln_b, sgu_w_s, sgu_b_s, w_out, mix_post_g, ffn2_pre_g, ffn2_w_gate, ffn2_w_up, ffn2_w_down, ffn2_post_g, loss_target, m_ffn1_pre_g, m_ffn1_w_gate, m_ffn1_w_up, m_ffn1_w_down, m_ffn1_post_g, m_mix_pre_g, m_w_in, m_b_forget, m_sgu_ln_g, m_sgu_ln_b, m_sgu_w_s, m_sgu_b_s, m_w_out, m_mix_post_g, m_ffn2_pre_g, m_ffn2_w_gate, m_ffn2_w_up, m_ffn2_w_down, m_ffn2_post_g, v_ffn1_pre_g, v_ffn1_w_gate, v_ffn1_w_up, v_ffn1_w_down, v_ffn1_post_g, v_mix_pre_g, v_w_in, v_b_forget, v_sgu_ln_g, v_sgu_ln_b, v_sgu_w_s, v_sgu_b_s, v_w_out, v_mix_post_g, v_ffn2_pre_g, v_ffn2_w_gate, v_ffn2_w_up, v_ffn2_w_down, v_ffn2_post_g):
    weights = dict(zip(WEIGHT_NAMES, (ffn1_pre_g, ffn1_w_gate, ffn1_w_up, ffn1_w_down, ffn1_post_g, mix_pre_g, w_in,
                                      b_forget, sgu_ln_g, sgu_ln_b, sgu_w_s, sgu_b_s, w_out, mix_post_g, ffn2_pre_g,
                                      ffn2_w_gate, ffn2_w_up, ffn2_w_down, ffn2_post_g)))
    mom1 = dict(zip(WEIGHT_NAMES, (m_ffn1_pre_g, m_ffn1_w_gate, m_ffn1_w_up, m_ffn1_w_down, m_ffn1_post_g,
                                   m_mix_pre_g, m_w_in, m_b_forget, m_sgu_ln_g, m_sgu_ln_b, m_sgu_w_s, m_sgu_b_s,
                                   m_w_out, m_mix_post_g, m_ffn2_pre_g, m_ffn2_w_gate, m_ffn2_w_up, m_ffn2_w_down,
                                   m_ffn2_post_g)))
    mom2 = dict(zip(WEIGHT_NAMES, (v_ffn1_pre_g, v_ffn1_w_gate, v_ffn1_w_up, v_ffn1_w_down, v_ffn1_post_g,
                                   v_mix_pre_g, v_w_in, v_b_forget, v_sgu_ln_g, v_sgu_ln_b, v_sgu_w_s, v_sgu_b_s,
                                   v_w_out, v_mix_post_g, v_ffn2_pre_g, v_ffn2_w_gate, v_ffn2_w_up, v_ffn2_w_down,
                                   v_ffn2_post_g)))
    D = x.shape[-1]
    small_names = [n for n in WEIGHT_NAMES if n not in BIG_NAMES]

    small = {n: weights[n] for n in small_names}
    shard = lambda n: weights[n][0].astype(BF16)

    ffn1_full = _all_gather([shard(n) for n in WEIGHT_GROUPS["ffn1"]], "ffn1_all_gather")
    gathered = {}
    for cid, grp in ((1, "mix"), (2, "ffn2")):
        shards, _ = lax.optimization_barrier(([shard(n) for n in WEIGHT_GROUPS[grp]], ffn1_full[0]))
        gathered[grp] = _sequencer_exchange(shards, grp + "_gather", True, cid)

    def fetch(group, after):
        return zip(WEIGHT_GROUPS[group], ffn1_full if group == "ffn1" else gathered[group])

    ready, received = {}, {}

    def emit(name, part):
        ready[name] = part
        for gi, group in enumerate(GRAD_GROUPS):
            if name == group[-1]:
                lands = _sequencer_exchange([ready[n] for n in group], name + "_grad_exchange", False, 3 + gi)
                received.update(zip(group, lands))
        return part

    loss_row, grad_x, grads_small = _local_step(x[0], loss_target[0], small, fetch, emit)

    blob = _all_reduce_small(_pack_small(grads_small, D, loss_row), "small_all_reduce")

    out = {}
    for group in GRAD_GROUPS:
        for n in group:
            g, d, m_new, v_new = _sum_adamw(received[n], weights[n][0], mom1[n][0], mom2[n][0], "adamw_" + n)
            out[n] = tuple(a[None] for a in (g, d, m_new, v_new))

    d_blob, m_blob, v_blob = _adamw(blob, _pack_small(small, D), _pack_small({n: mom1[n] for n in small_names}, D),
                                    _pack_small({n: mom2[n] for n in small_names}, D), "adamw_small")
    unpacked = [_unpack_small(b, D) for b in (blob, d_blob, m_blob, v_blob)]
    for n in small_names:
        out[n] = tuple(u[n].reshape(weights[n].shape) for u in unpacked)

    loss = blob[ROW_LOSS, 0]
    result = [loss, grad_x[None]]
    for k in range(4):
        result += [out[n][k] for n in WEIGHT_NAMES]
    return tuple(result)
```

```python
import functools

import numpy as np
import jax
import jax.numpy as jnp
from jax import lax
from jax.experimental import pallas as pl
from jax.experimental.pallas import tpu as pltpu
from jax.experimental.pallas import tpu_sc as plsc

F32 = jnp.float32
BF16 = jnp.bfloat16

RMS_EPS = 1e-6
LN_EPS = 1e-5
HEAD_DIM = 128
N_HEADS = 8
GROUP_DIM = 128
N_GROUPS = 8
SGU_LEN = 128
CHUNK = 64
N_DEV = 8
LANES = 128
VMEM_LIMIT = 56 * 1024 * 1024
NEG_BIG = -1e30

ADAM_LR = 0.001
ADAM_B1 = 0.9
ADAM_B2 = 0.999
ADAM_EPS = 1e-08
ADAM_WD = 0.01
ADAM_STEP = 10

MESH = pl.DeviceIdType.MESH
ANY = pl.BlockSpec(memory_space=pl.ANY)


def _blk(n, pref):
    return pref if (n >= pref and n % pref == 0) else n


def _mm(a, b):
    return jnp.dot(a, b, preferred_element_type=F32)


def _mm_nt(a, b):
    return lax.dot_general(a, b, (((1,), (1,)), ((), ())), preferred_element_type=F32)


def _mm_tn(a, b):
    return lax.dot_general(a, b, (((0,), (0,)), ((), ())), preferred_element_type=F32)


def _params(sem):
    return pltpu.CompilerParams(dimension_semantics=sem, vmem_limit_bytes=VMEM_LIMIT)


def _gelu(x):
    return 0.5 * x * (1.0 + lax.erf(x * np.float32(1.0 / np.sqrt(2.0))))


def _gelu_grad(x):
    cdf = 0.5 * (1.0 + lax.erf(x * np.float32(1.0 / np.sqrt(2.0))))
    return cdf + x * jnp.exp(-0.5 * x * x) * np.float32(1.0 / np.sqrt(2.0 * np.pi))


def _rms_scale(v):
    return lax.rsqrt(jnp.mean(v * v, axis=-1, keepdims=True) + RMS_EPS)


def _rms_bwd(dy, xhat, r, g):
    dxh = dy * g
    return r * (dxh - xhat * jnp.mean(dxh * xhat, axis=-1, keepdims=True))


def _ffn_rows(T):
    tm = _blk(T, 1024)
    th = _blk(tm, 512)
    return tm, th, tm // th


def _ffn_fwd(x, g_pre, wg, wu, wd, g_post, name):
    T, D = x.shape
    ns, _, fs = wg.shape
    tm, th, parts = _ffn_rows(T)

    def body(x_ref, gpre_ref, wg_ref, wu_ref, wd_ref, gpost_ref, xo_ref, y_ref, dgf_ref, silu_ref, act_ref,
             h_scr, acc_scr):
        j = pl.program_id(1)

        @pl.when(j == 0)
        def _():
            for r in range(parts):
                rows = slice(r * th, (r + 1) * th)
                xv = x_ref[rows, :]
                h_scr[rows, :] = (xv * _rms_scale(xv) * gpre_ref[...]).astype(BF16)
            acc_scr[...] = jnp.zeros_like(acc_scr)

        pre = []
        for r in range(parts):
            h = h_scr[r * th:(r + 1) * th, :]
            pre.append((_mm(h, wg_ref[...]), _mm(h, wu_ref[...])))
        for r in range(parts):
            rows = slice(r * th, (r + 1) * th)
            gg, uu = pre[r]
            sg = jax.nn.sigmoid(gg)
            silu = gg * sg
            act = (silu * uu).astype(BF16)
            dgf_ref[rows, :] = (uu * (sg * (1.0 + gg * (1.0 - sg)))).astype(BF16)
            silu_ref[rows, :] = silu.astype(BF16)
            act_ref[rows, :] = act
            acc_scr[rows, :] += _mm(act, wd_ref[...])

        @pl.when(j == ns - 1)
        def _():
            for r in range(parts):
                rows = slice(r * th, (r + 1) * th)
                y = acc_scr[rows, :]
                y_ref[rows, :] = y
                xo_ref[rows, :] = x_ref[rows, :] + 0.5 * (y * _rms_scale(y) * gpost_ref[...])

    row = pl.BlockSpec((tm, D), lambda i, j: (i, 0), pipeline_mode=pl.Buffered(1))
    vec = pl.BlockSpec((1, D), lambda i, j: (0, 0))
    return pl.pallas_call(
        body, name=name, grid=(T // tm, ns),
        in_specs=[row, vec,
                  pl.BlockSpec((None, D, fs), lambda i, j: (j, 0, 0)),
                  pl.BlockSpec((None, D, fs), lambda i, j: (j, 0, 0)),
                  pl.BlockSpec((None, fs, D), lambda i, j: (j, 0, 0)),
                  vec],
        out_specs=[row, row] + [pl.BlockSpec((tm, fs), lambda i, j: (i, j))] * 3,
        out_shape=[jax.ShapeDtypeStruct((T, D), F32), jax.ShapeDtypeStruct((T, D), F32)]
        + [jax.ShapeDtypeStruct((T, ns * fs), BF16)] * 3,
        scratch_shapes=[pltpu.VMEM((tm, D), BF16), pltpu.VMEM((tm, D), F32)],
        compiler_params=_params(("parallel", "arbitrary")),
    )(x, g_pre, wg, wu, wd, g_post)


def _after(dep):
    return jnp.zeros((8, LANES), F32) if dep is None else dep


def _ffn_bwd(dxo, x, y, dgf, silu, g_pre, wg, wu, wd, g_post, name, dep=None):
    T, D = x.shape
    ns, _, fs = wg.shape
    tm, th, parts = _ffn_rows(T)
    n_i = T // tm

    def body(dxo_ref, x_ref, y_ref, dgf_ref, silu_ref, gpre_ref, wg_ref, wu_ref, wd_ref, gpost_ref, _,
             dx_ref, hb_ref, dyb_ref, dgb_ref, dub_ref, dgpre_ref, dgpost_ref, dy_scr, acc_scr):
        j = pl.program_id(1)

        @pl.when(j == 0)
        def _():
            dgpost = jnp.zeros((1, D), F32)
            for r in range(parts):
                rows = slice(r * th, (r + 1) * th)
                yv = y_ref[rows, :]
                s = _rms_scale(yv)
                n = yv * s
                dn = 0.5 * dxo_ref[rows, :]
                dgpost = dgpost + jnp.sum(dn * n, axis=0, keepdims=True)
                dyv = _rms_bwd(dn, n, s, gpost_ref[...]).astype(BF16)
                dy_scr[rows, :] = dyv
                dyb_ref[rows, :] = dyv
                xv = x_ref[rows, :]
                hb_ref[rows, :] = (xv * _rms_scale(xv) * gpre_ref[...]).astype(BF16)
            dgpost_ref[...] = dgpost
            acc_scr[...] = jnp.zeros_like(acc_scr)

        das = [_mm_nt(dy_scr[r * th:(r + 1) * th, :], wd_ref[...]) for r in range(parts)]
        for r in range(parts):
            rows = slice(r * th, (r + 1) * th)
            dgate = (das[r] * dgf_ref[rows, :].astype(F32)).astype(BF16)
            dup = (das[r] * silu_ref[rows, :].astype(F32)).astype(BF16)
            dgb_ref[rows, :] = dgate
            dub_ref[rows, :] = dup
            acc_scr[rows, :] += _mm_nt(dgate, wg_ref[...]) + _mm_nt(dup, wu_ref[...])

        @pl.when(j == ns - 1)
        def _():
            dgpre = jnp.zeros((1, D), F32)
            for r in range(parts):
                rows = slice(r * th, (r + 1) * th)
                xv = x_ref[rows, :]
                rs = _rms_scale(xv)
                xhat = xv * rs
                dh = acc_scr[rows, :]
                dgpre = dgpre + jnp.sum(dh * xhat, axis=0, keepdims=True)
                dx_ref[rows, :] = _rms_bwd(dh, xhat, rs, gpre_ref[...]) + dxo_ref[rows, :]
            dgpre_ref[...] = dgpre

    row = pl.BlockSpec((tm, D), lambda i, j: (i, 0), pipeline_mode=pl.Buffered(1))
    vec = pl.BlockSpec((1, D), lambda i, j: (0, 0))
    wide = pl.BlockSpec((tm, fs), lambda i, j: (i, j))
    part = pl.BlockSpec((None, 1, D), lambda i, j: (i, 0, 0))
    F = ns * fs
    return pl.pallas_call(
        body, name=name, grid=(n_i, ns),
        in_specs=[row, row, row, wide, wide, vec,
                  pl.BlockSpec((None, D, fs), lambda i, j: (j, 0, 0)),
                  pl.BlockSpec((None, D, fs), lambda i, j: (j, 0, 0)),
                  pl.BlockSpec((None, fs, D), lambda i, j: (j, 0, 0)),
                  vec, ANY],
        out_specs=[row, row, row, wide, wide, part, part],
        out_shape=[jax.ShapeDtypeStruct((T, D), F32), jax.ShapeDtypeStruct((T, D), BF16),
                   jax.ShapeDtypeStruct((T, D), BF16), jax.ShapeDtypeStruct((T, F), BF16),
                   jax.ShapeDtypeStruct((T, F), BF16),
                   jax.ShapeDtypeStruct((n_i, 1, D), F32), jax.ShapeDtypeStruct((n_i, 1, D), F32)],
        scratch_shapes=[pltpu.VMEM((tm, D), BF16), pltpu.VMEM((tm, D), F32)],
        compiler_params=_params(("parallel", "arbitrary")),
    )(dxo, x, y, dgf, silu, g_pre, wg, wu, wd, g_post, _after(dep))


def _wgrad(xm, ym, name, shard_cols=False, dep=None):
    T, M = xm.shape
    _, N = ym.shape
    assert M * N * 4 <= 16 * 1024 * 1024, (M, N)
    tk = _blk(T, 512)
    n_k = T // tk
    fs = N // N_DEV

    def body(x_ref, y_ref, _, o_ref, acc_scr):
        k = pl.program_id(0)

        @pl.when(k == 0)
        def _():
            acc_scr[...] = jnp.zeros_like(acc_scr)

        acc_scr[...] += _mm_tn(x_ref[...], y_ref[...])

        @pl.when(k == n_k - 1)
        def _():
            if shard_cols:
                for s in range(N_DEV):
                    o_ref[s] = acc_scr[:, s * fs:(s + 1) * fs].astype(BF16)
            else:
                o_ref[...] = acc_scr[...].astype(BF16)

    if shard_cols:
        out_spec = pl.BlockSpec((N_DEV, M, fs), lambda k: (0, 0, 0), pipeline_mode=pl.Buffered(1))
        out_shape = jax.ShapeDtypeStruct((N_DEV, M, fs), BF16)
    else:
        out_spec = pl.BlockSpec((M, N), lambda k: (0, 0), pipeline_mode=pl.Buffered(1))
        out_shape = jax.ShapeDtypeStruct((M, N), BF16)
    return pl.pallas_call(
        body, name=name, grid=(n_k,),
        in_specs=[pl.BlockSpec((tk, M), lambda k: (k, 0)), pl.BlockSpec((tk, N), lambda k: (k, 0)), ANY],
        out_specs=out_spec, out_shape=out_shape,
        scratch_shapes=[pltpu.VMEM((M, N), F32)],
        compiler_params=_params(("arbitrary",)),
    )(xm, ym, _after(dep))


def _mix_in_fwd(x1, g, w7, wf, name):
    T, D = x1.shape
    n_seg, _, W = w7.shape
    tm = _blk(T, 1024)

    def body(x_ref, g_ref, w_ref, wf_ref, z_ref, f_ref, hb_ref, h_scr):
        s = pl.program_id(1)

        @pl.when(s == 0)
        def _():
            xv = x_ref[...]
            h = (xv * _rms_scale(xv) * g_ref[...]).astype(BF16)
            h_scr[...] = h
            hb_ref[...] = h
            f_ref[...] = _mm(h, wf_ref[...])

        z_ref[...] = _mm(h_scr[...], w_ref[...]).astype(BF16)

    return pl.pallas_call(
        body, name=name, grid=(T // tm, n_seg),
        in_specs=[pl.BlockSpec((tm, D), lambda i, s: (i, 0)),
                  pl.BlockSpec((1, D), lambda i, s: (0, 0)),
                  pl.BlockSpec((None, D, W), lambda i, s: (s, 0, 0)),
                  pl.BlockSpec((D, LANES), lambda i, s: (0, 0))],
        out_specs=[pl.BlockSpec((None, tm, W), lambda i, s: (s, i, 0)),
                   pl.BlockSpec((tm, LANES), lambda i, s: (i, 0)),
                   pl.BlockSpec((tm, D), lambda i, s: (i, 0))],
        out_shape=[jax.ShapeDtypeStruct((n_seg, T, W), BF16), jax.ShapeDtypeStruct((T, LANES), F32),
                   jax.ShapeDtypeStruct((T, D), BF16)],
        scratch_shapes=[pltpu.VMEM((tm, D), BF16)],
        compiler_params=_params(("parallel", "arbitrary")),
    )(x1, g, w7, wf)


def _mix_in_bwd(dx2, x1, g, segs, dfb, w7, wf, name):
    T, D = x1.shape
    n_seg, _, W = w7.shape
    tm = _blk(T, 512)
    n_i = T // tm

    def body(*refs):
        dx2_ref, x_ref, g_ref = refs[:3]
        seg_refs = refs[3:3 + n_seg]
        df_ref, w_ref, wf_ref, dx1_ref, dg_ref, acc_scr = refs[3 + n_seg:]
        s = pl.program_id(1)

        @pl.when(s == 0)
        def _():
            acc_scr[...] = _mm_nt(df_ref[...], wf_ref[...])

        for q in range(n_seg):
            @pl.when(s == q)
            def _(q=q):
                acc_scr[...] += _mm_nt(seg_refs[q][...], w_ref[...])

        @pl.when(s == n_seg - 1)
        def _():
            xv = x_ref[...]
            r = _rms_scale(xv)
            xhat = xv * r
            dh = acc_scr[...]
            dg_ref[...] = jnp.sum(dh * xhat, axis=0, keepdims=True)
            dx1_ref[...] = _rms_bwd(dh, xhat, r, g_ref[...]) + dx2_ref[...]

    row = pl.BlockSpec((tm, D), lambda i, s: (i, 0))
    seg_specs = []
    seg_args = []
    for arr, idx in segs:
        if idx is None:
            seg_specs.append(pl.BlockSpec((tm, W), lambda i, s: (i, 0)))
        else:
            seg_specs.append(pl.BlockSpec((None, tm, W), lambda i, s, idx=idx: (idx, i, 0)))
        seg_args.append(arr)
    return pl.pallas_call(
        body, name=name, grid=(n_i, n_seg),
        in_specs=[row, row, pl.BlockSpec((1, D), lambda i, s: (0, 0))] + seg_specs + [
            pl.BlockSpec((tm, LANES), lambda i, s: (i, 0)),
            pl.BlockSpec((None, D, W), lambda i, s: (s, 0, 0)),
            pl.BlockSpec((D, LANES), lambda i, s: (0, 0))],
        out_specs=[row, pl.BlockSpec((None, 1, D), lambda i, s: (i, 0, 0))],
        out_shape=[jax.ShapeDtypeStruct((T, D), F32), jax.ShapeDtypeStruct((n_i, 1, D), F32)],
        scratch_shapes=[pltpu.VMEM((tm, D), F32)],
        compiler_params=_params(("parallel", "arbitrary")),
    )(dx2, x1, g, *seg_args, dfb, w7, wf)


def _forget_cumsum(f, b_pad, name):
    T, L = f.shape
    tb = _blk(T, 256)

    def body(f_ref, b_ref, c_ref, carry):
        @pl.when(pl.program_id(0) == 0)
        def _():
            carry[...] = jnp.zeros_like(carry)

        lf = jax.nn.log_sigmoid(f_ref[...] + b_ref[...])
        rows = lax.broadcasted_iota(jnp.int32, (tb, tb), 0)
        cols = lax.broadcasted_iota(jnp.int32, (tb, tb), 1)
        tri = (cols <= rows).astype(F32)
        c = jnp.dot(tri, lf, preferred_element_type=F32, precision=lax.Precision.HIGHEST) + carry[...]
        c_ref[...] = c
        carry[...] = c[tb - 1:tb, :]

    return pl.pallas_call(
        body, name=name, grid=(T // tb,),
        in_specs=[pl.BlockSpec((tb, L), lambda i: (i, 0)), pl.BlockSpec((1, L), lambda i: (0, 0))],
        out_specs=pl.BlockSpec((tb, L), lambda i: (i, 0)),
        out_shape=jax.ShapeDtypeStruct((T, L), F32),
        scratch_shapes=[pltpu.VMEM((1, L), F32)],
        compiler_params=_params(("arbitrary",)),
    )(f, b_pad)


def _forget_bwd(dc, f, b_pad, name):
    T, L = f.shape
    tb = _blk(T, 256)
    nb = T // tb

    def body(dc_ref, f_ref, b_ref, df_ref, db_ref, carry):
        @pl.when(pl.program_id(0) == 0)
        def _():
            carry[...] = jnp.zeros_like(carry)
            db_ref[...] = jnp.zeros_like(db_ref)

        rows = lax.broadcasted_iota(jnp.int32, (tb, tb), 0)
        cols = lax.broadcasted_iota(jnp.int32, (tb, tb), 1)
        tri = (cols >= rows).astype(F32)
        r = jnp.dot(tri, dc_ref[...], preferred_element_type=F32, precision=lax.Precision.HIGHEST) + carry[...]
        carry[...] = r[0:1, :]
        df = r * (1.0 - jax.nn.sigmoid(f_ref[...] + b_ref[...]))
        df_ref[...] = df.astype(BF16)
        db_ref[...] += jnp.sum(df, axis=0, keepdims=True)

    rev = pl.BlockSpec((tb, L), lambda i: (nb - 1 - i, 0))
    one = pl.BlockSpec((1, L), lambda i: (0, 0))
    return pl.pallas_call(
        body, name=name, grid=(nb,),
        in_specs=[rev, rev, one], out_specs=[rev, one],
        out_shape=[jax.ShapeDtypeStruct((T, L), BF16), jax.ShapeDtypeStruct((1, L), F32)],
        scratch_shapes=[pltpu.VMEM((1, L), F32)],
        compiler_params=_params(("arbitrary",)),
    )(dc, f, b_pad)


def _attn_fwd(z7, c_row, name):
    _, T, W = z7.shape
    H = W // HEAD_DIM
    ta = _blk(T, 512)
    nq = T // ta
    scale = np.float32(1.0 / np.sqrt(HEAD_DIM))

    def body(q_ref, k_ref, v_ref, crow_ref, o_ref, lse_ref, m_scr, l_scr, acc_scr):
        i = pl.program_id(1)
        j = pl.program_id(2)

        @pl.when(j == 0)
        def _():
            m_scr[...] = jnp.full_like(m_scr, NEG_BIG)
            l_scr[...] = jnp.zeros_like(l_scr)
            acc_scr[...] = jnp.zeros_like(acc_scr)

        def step(diagonal):
            s = _mm_nt(q_ref[...], k_ref[...]) * scale - crow_ref[...]
            if diagonal:
                rows = lax.broadcasted_iota(jnp.int32, (ta, ta), 0)
                cols = lax.broadcasted_iota(jnp.int32, (ta, ta), 1)
                s = jnp.where(cols <= rows, s, NEG_BIG)
            m_prev = m_scr[...]
            m_new = jnp.maximum(m_prev, jnp.max(s, axis=-1, keepdims=True))
            alpha = jnp.exp(m_prev - m_new)
            p = jnp.exp(s - m_new)
            l_scr[...] = alpha * l_scr[...] + jnp.sum(p, axis=-1, keepdims=True)
            acc_scr[...] = alpha * acc_scr[...] + _mm(p.astype(BF16), v_ref[...])
            m_scr[...] = m_new

        @pl.when(j < i)
        def _():
            step(False)

        @pl.when(j == i)
        def _():
            step(True)
            l = l_scr[...]
            o_ref[...] = acc_scr[...] / l
            lse_ref[...] = m_scr[...] + jnp.log(l)

    return pl.pallas_call(
        body, name=name, grid=(H, nq, nq),
        in_specs=[pl.BlockSpec((None, ta, HEAD_DIM), lambda h, i, j: (0, i, h)),
                  pl.BlockSpec((None, ta, HEAD_DIM), lambda h, i, j: (1, jnp.minimum(i, j), h)),
                  pl.BlockSpec((None, ta, HEAD_DIM), lambda h, i, j: (2, jnp.minimum(i, j), h)),
                  pl.BlockSpec((None, 1, ta), lambda h, i, j: (h, 0, jnp.minimum(i, j)))],
        out_specs=[pl.BlockSpec((ta, HEAD_DIM), lambda h, i, j: (i, h)),
                   pl.BlockSpec((None, ta, 1), lambda h, i, j: (h, i, 0))],
        out_shape=[jax.ShapeDtypeStruct((T, W), F32), jax.ShapeDtypeStruct((H, T, 1), F32)],
        scratch_shapes=[pltpu.VMEM((ta, 1), F32), pltpu.VMEM((ta, 1), F32), pltpu.VMEM((ta, HEAD_DIM), F32)],
        compiler_params=_params(("parallel", "parallel", "arbitrary")),
    )(z7, z7, z7, c_row)


def _attn_bwd_kv(z7, dob, c_col, lse_row, d_row, name):
    _, T, W = z7.shape
    H = W // HEAD_DIM
    ta = _blk(T, 512)
    nq = T // ta
    scale = np.float32(1.0 / np.sqrt(HEAD_DIM))

    def body(k_ref, v_ref, q_ref, do_ref, ccol_ref, lse_ref, d_ref, dk_ref, dv_ref, dc_ref, dk_scr, dv_scr, dc_scr):
        j = pl.program_id(1)
        i = pl.program_id(2)

        @pl.when(i == 0)
        def _():
            dk_scr[...] = jnp.zeros_like(dk_scr)
            dv_scr[...] = jnp.zeros_like(dv_scr)
            dc_scr[...] = jnp.zeros_like(dc_scr)

        def step(diagonal):
            q = q_ref[...]
            do = do_ref[...]
            st = _mm_nt(k_ref[...], q) * scale - ccol_ref[...] - lse_ref[...]
            if diagonal:
                rows = lax.broadcasted_iota(jnp.int32, (ta, ta), 0)
                cols = lax.broadcasted_iota(jnp.int32, (ta, ta), 1)
                st = jnp.where(rows <= cols, st, NEG_BIG)
            pt = jnp.exp(st)
            dv_scr[...] += _mm(pt.astype(BF16), do)
            dst = pt * (_mm_nt(v_ref[...], do) - d_ref[...])
            dk_scr[...] += _mm(dst.astype(BF16), q)
            dc_scr[...] += jnp.sum(dst, axis=-1, keepdims=True)

        @pl.when(i > j)
        def _():
            step(False)

        @pl.when(i == j)
        def _():
            step(True)

        @pl.when(i == nq - 1)
        def _():
            dk_ref[...] = (dk_scr[...] * scale).astype(BF16)
            dv_ref[...] = dv_scr[...].astype(BF16)
            dc_ref[...] = -dc_scr[...]

    return pl.pallas_call(
        body, name=name, grid=(H, nq, nq),
        in_specs=[pl.BlockSpec((None, ta, HEAD_DIM), lambda h, j, i: (1, j, h)),
                  pl.BlockSpec((None, ta, HEAD_DIM), lambda h, j, i: (2, j, h)),
                  pl.BlockSpec((None, ta, HEAD_DIM), lambda h, j, i: (0, jnp.maximum(i, j), h)),
                  pl.BlockSpec((ta, HEAD_DIM), lambda h, j, i: (jnp.maximum(i, j), h)),
                  pl.BlockSpec((None, ta, 1), lambda h, j, i: (h, j, 0)),
                  pl.BlockSpec((None, 1, ta), lambda h, j, i: (h, 0, jnp.maximum(i, j))),
                  pl.BlockSpec((None, 1, ta), lambda h, j, i: (h, 0, jnp.maximum(i, j)))],
        out_specs=[pl.BlockSpec((ta, HEAD_DIM), lambda h, j, i: (j, h)),
                   pl.BlockSpec((ta, HEAD_DIM), lambda h, j, i: (j, h)),
                   pl.BlockSpec((None, ta, 1), lambda h, j, i: (h, j, 0))],
        out_shape=[jax.ShapeDtypeStruct((T, W), BF16), jax.ShapeDtypeStruct((T, W), BF16),
                   jax.ShapeDtypeStruct((H, T, 1), F32)],
        scratch_shapes=[pltpu.VMEM((ta, HEAD_DIM), F32), pltpu.VMEM((ta, HEAD_DIM), F32), pltpu.VMEM((ta, 1), F32)],
        compiler_params=_params(("parallel", "parallel", "arbitrary")),
    )(z7, z7, z7, dob, c_col, lse_row, d_row)


def _attn_bwd_q(z7, dob, c_row, lse_col, d_col, name):
    _, T, W = z7.shape
    H = W // HEAD_DIM
    ta = _blk(T, 512)
    nq = T // ta
    scale = np.float32(1.0 / np.sqrt(HEAD_DIM))

    def body(q_ref, k_ref, v_ref, do_ref, crow_ref, lse_ref, d_ref, dq_ref, dc_ref, dq_scr, dc_scr):
        i = pl.program_id(1)
        j = pl.program_id(2)

        @pl.when(j == 0)
        def _():
            dq_scr[...] = jnp.zeros_like(dq_scr)
            dc_scr[...] = jnp.zeros_like(dc_scr)

        def step(diagonal):
            k = k_ref[...]
            do = do_ref[...]
            s = _mm_nt(q_ref[...], k) * scale - crow_ref[...] - lse_ref[...]
            if diagonal:
                rows = lax.broadcasted_iota(jnp.int32, (ta, ta), 0)
                cols = lax.broadcasted_iota(jnp.int32, (ta, ta), 1)
                s = jnp.where(cols <= rows, s, NEG_BIG)
            p = jnp.exp(s)
            ds = p * (_mm_nt(do, v_ref[...]) - d_ref[...])
            dq_scr[...] += _mm(ds.astype(BF16), k)
            dc_scr[...] += jnp.sum(ds, axis=-1, keepdims=True)

        @pl.when(j < i)
        def _():
            step(False)

        @pl.when(j == i)
        def _():
            step(True)
            dq_ref[...] = (dq_scr[...] * scale).astype(BF16)
            dc_ref[...] = dc_scr[...]

    return pl.pallas_call(
        body, name=name, grid=(H, nq, nq),
        in_specs=[pl.BlockSpec((None, ta, HEAD_DIM), lambda h, i, j: (0, i, h)),
                  pl.BlockSpec((None, ta, HEAD_DIM), lambda h, i, j: (1, jnp.minimum(i, j), h)),
                  pl.BlockSpec((None, ta, HEAD_DIM), lambda h, i, j: (2, jnp.minimum(i, j), h)),
                  pl.BlockSpec((ta, HEAD_DIM), lambda h, i, j: (i, h)),
                  pl.BlockSpec((None, 1, ta), lambda h, i, j: (h, 0, jnp.minimum(i, j))),
                  pl.BlockSpec((None, ta, 1), lambda h, i, j: (h, i, 0)),
                  pl.BlockSpec((None, ta, 1), lambda h, i, j: (h, i, 0))],
        out_specs=[pl.BlockSpec((ta, HEAD_DIM), lambda h, i, j: (i, h)),
                   pl.BlockSpec((None, ta, 1), lambda h, i, j: (h, i, 0))],
        out_shape=[jax.ShapeDtypeStruct((T, W), BF16), jax.ShapeDtypeStruct((H, T, 1), F32)],
        scratch_shapes=[pltpu.VMEM((ta, HEAD_DIM), F32), pltpu.VMEM((ta, 1), F32)],
        compiler_params=_params(("parallel", "parallel", "arbitrary")),
    )(z7, z7, z7, dob, c_row, lse_col, d_col)


ATTN_TILE = 512
ATTN_CHAINS = 2


def _attn_geometry(T):
    ta = _blk(T, ATTN_TILE)
    nc = ATTN_CHAINS if (T // ta) % ATTN_CHAINS == 0 else 1
    return ta, nc, T // ta


def _causal_tile(ta, keys_on_rows=False):
    rows = lax.broadcasted_iota(jnp.int32, (ta, ta), 0)
    cols = lax.broadcasted_iota(jnp.int32, (ta, ta), 1)
    return rows <= cols if keys_on_rows else cols <= rows


def _chunk(ref, j, ta):
    return ref[pl.ds(pl.multiple_of(j * ta, ta), ta), :]


def _attn_fwd_loop(z7, c_chunks, name):
    _, T, W = z7.shape
    H = W // HEAD_DIM
    ta, nc, n_chunks = _attn_geometry(T)
    scale = np.float32(1.0 / np.sqrt(HEAD_DIM))

    def body(q_ref, k_ref, v_ref, c_ref, o_ref, lse_ref, m_scr, l_scr, acc_scr):
        g = pl.program_id(1)
        m_scr[...] = jnp.full_like(m_scr, NEG_BIG)
        l_scr[...] = jnp.zeros_like(l_scr)
        acc_scr[...] = jnp.zeros_like(acc_scr)

        def update(ch, k, v, crow, diagonal):
            q = q_ref[ch * ta:(ch + 1) * ta, :]
            s = _mm_nt(q, k) * scale - crow
            if diagonal:
                s = jnp.where(_causal_tile(ta), s, NEG_BIG)
            m_prev = m_scr[ch]
            m_new = jnp.maximum(m_prev, jnp.max(s, axis=-1, keepdims=True))
            alpha = jnp.exp(m_prev - m_new)
            p = jnp.exp(s - m_new)
            l_scr[ch] = alpha * l_scr[ch] + jnp.sum(p, axis=-1, keepdims=True)
            acc_scr[ch] = alpha * acc_scr[ch] + _mm(p.astype(BF16), v)
            m_scr[ch] = m_new

        def full_chunk(j, carry):
            k = _chunk(k_ref, j, ta)
            v = _chunk(v_ref, j, ta)
            crow = c_ref[j]
            for ch in range(nc):
                update(ch, k, v, crow, False)
            return carry

        lax.fori_loop(0, nc * g, full_chunk, 0)
        for jj in range(nc):
            j = nc * g + jj
            k = _chunk(k_ref, j, ta)
            v = _chunk(v_ref, j, ta)
            crow = c_ref[j]
            for ch in range(jj, nc):
                update(ch, k, v, crow, ch == jj)
        for ch in range(nc):
            l = l_scr[ch]
            o_ref[ch * ta:(ch + 1) * ta, :] = acc_scr[ch] / l
            lse_ref[ch * ta:(ch + 1) * ta, :] = m_scr[ch] + jnp.log(l)

    tq = nc * ta
    return pl.pallas_call(
        body, name=name, grid=(H, n_chunks // nc),
        in_specs=[pl.BlockSpec((None, tq, HEAD_DIM), lambda h, g: (0, g, h)),
                  pl.BlockSpec((None, T, HEAD_DIM), lambda h, g: (1, 0, h)),
                  pl.BlockSpec((None, T, HEAD_DIM), lambda h, g: (2, 0, h)),
                  pl.BlockSpec((None, n_chunks, 1, ta), lambda h, g: (h, 0, 0, 0))],
        out_specs=[pl.BlockSpec((tq, HEAD_DIM), lambda h, g: (g, h)),
                   pl.BlockSpec((None, tq, 1), lambda h, g: (h, g, 0))],
        out_shape=[jax.ShapeDtypeStruct((T, W), F32), jax.ShapeDtypeStruct((H, T, 1), F32)],
        scratch_shapes=[pltpu.VMEM((nc, ta, 1), F32), pltpu.VMEM((nc, ta, 1), F32),
                        pltpu.VMEM((nc, ta, HEAD_DIM), F32)],
        compiler_params=_params(("parallel", "arbitrary")),
    )(z7, z7, z7, c_chunks)


def _attn_fwd_keys_on_rows(z7, vt, c_rep, name):
    _, T, W = z7.shape
    H = W // HEAD_DIM
    ta, nc, n_chunks = _attn_geometry(T)
    scale = np.float32(1.0 / np.sqrt(HEAD_DIM))
    reps = ta // LANES

    def body(q_ref, k_ref, vt_ref, c_ref, o_ref, lse_ref, m_scr, l_scr, acc_scr):
        g = pl.program_id(1)
        m_scr[...] = jnp.full_like(m_scr, NEG_BIG)
        l_scr[...] = jnp.zeros_like(l_scr)
        acc_scr[...] = jnp.zeros_like(acc_scr)

        def update(ch, k, vt, cj, diagonal):
            q = q_ref[ch * ta:(ch + 1) * ta, :]
            st = _mm_nt(k, q) * scale - cj
            if diagonal:
                st = jnp.where(_causal_tile(ta, keys_on_rows=True), st, NEG_BIG)
            m_prev = m_scr[ch]
            m_new = jnp.maximum(m_prev, jnp.max(st, axis=0, keepdims=True))
            alpha = jnp.exp(m_prev - m_new)
            pt = jnp.exp(st - m_new)
            l_scr[ch] = alpha * l_scr[ch] + jnp.sum(pt, axis=0, keepdims=True)
            acc_scr[ch] = alpha * acc_scr[ch] + _mm(vt, pt.astype(BF16))
            m_scr[ch] = m_new

        def load(j):
            cj = _chunk(c_ref, j, ta)
            return _chunk(k_ref, j, ta), vt_ref[j], jnp.concatenate([cj] * reps, axis=1)

        def full_chunk(j, carry):
            k, vt, cj = load(j)
            for ch in range(nc):
                update(ch, k, vt, cj, False)
            return carry

        lax.fori_loop(0, nc * g, full_chunk, 0)
        for jj in range(nc):
            k, vt, cj = load(nc * g + jj)
            for ch in range(jj, nc):
                update(ch, k, vt, cj, ch == jj)
        for ch in range(nc):
            l = l_scr[ch]
            o_ref[ch * ta:(ch + 1) * ta, :] = (acc_scr[ch] / l).T
            lse_ref[ch] = m_scr[ch] + jnp.log(l)

    tq = nc * ta
    return pl.pallas_call(
        body, name=name, grid=(H, n_chunks // nc),
        in_specs=[pl.BlockSpec((None, tq, HEAD_DIM), lambda h, g: (0, g, h)),
                  pl.BlockSpec((None, T, HEAD_DIM), lambda h, g: (1, 0, h)),
                  pl.BlockSpec((None, n_chunks, HEAD_DIM, ta), lambda h, g: (h, 0, 0, 0)),
                  pl.BlockSpec((None, T, LANES), lambda h, g: (h, 0, 0))],
        out_specs=[pl.BlockSpec((tq, HEAD_DIM), lambda h, g: (g, h)),
                   pl.BlockSpec((None, nc, 1, ta), lambda h, g: (h, g, 0, 0))],
        out_shape=[jax.ShapeDtypeStruct((T, W), F32), jax.ShapeDtypeStruct((H, n_chunks, 1, ta), F32)],
        scratch_shapes=[pltpu.VMEM((nc, 1, ta), F32), pltpu.VMEM((nc, 1, ta), F32),
                        pltpu.VMEM((nc, HEAD_DIM, ta), F32)],
        compiler_params=_params(("parallel", "arbitrary")),
    )(z7, z7, vt, c_rep)


def _attn_bwd_fused(z7, kt, dob, c_rep, lse_chunks, d_chunks, name):
    _, T, W = z7.shape
    H = W // HEAD_DIM
    ta, nc, n_chunks = _attn_geometry(T)
    n_steps = n_chunks // nc
    scale = np.float32(1.0 / np.sqrt(HEAD_DIM))
    reps = ta // LANES

    def body(k_ref, v_ref, kt_ref, q_ref, do_ref, c_ref, lse_ref, d_ref,
             dk_ref, dv_ref, dck_ref, dq_ref, dcq_ref, dk_scr, dv_scr, dck_scr, dqt_scr, dcq_scr):
        g = pl.program_id(1)

        @pl.when(g == 0)
        def _():
            dqt_scr[...] = jnp.zeros_like(dqt_scr)
            dcq_scr[...] = jnp.zeros_like(dcq_scr)

        dk_scr[...] = jnp.zeros_like(dk_scr)
        dv_scr[...] = jnp.zeros_like(dv_scr)
        dck_scr[...] = jnp.zeros_like(dck_scr)

        def update(ch, i, q, do, diagonal):
            rows = slice(ch * ta, (ch + 1) * ta)
            cj = c_ref[rows, :]
            st = _mm_nt(k_ref[rows, :], q) * scale - jnp.concatenate([cj] * reps, axis=1) - lse_ref[i]
            if diagonal:
                st = jnp.where(_causal_tile(ta, keys_on_rows=True), st, NEG_BIG)
            pt = jnp.exp(st)
            dv_scr[ch] += _mm(pt.astype(BF16), do)
            dst = pt * (_mm_nt(v_ref[rows, :], do) - d_ref[i])
            dst_b = dst.astype(BF16)
            dk_scr[ch] += _mm(dst_b, q)
            dqt_scr[i] += _mm(kt_ref[ch], dst_b)
            dcq_scr[i] += jnp.sum(dst, axis=0, keepdims=True)
            lane_sum = dst[:, :LANES]
            for r in range(1, reps):
                lane_sum = lane_sum + dst[:, r * LANES:(r + 1) * LANES]
            dck_scr[ch] += lane_sum

        for ii in range(nc):
            i = nc * g + ii
            q = _chunk(q_ref, i, ta)
            do = _chunk(do_ref, i, ta)
            for ch in range(0, ii + 1):
                update(ch, i, q, do, ch == ii)

        def full_chunk(i, carry):
            q = _chunk(q_ref, i, ta)
            do = _chunk(do_ref, i, ta)
            for ch in range(nc):
                update(ch, i, q, do, False)
            return carry

        lax.fori_loop(nc * (g + 1), n_chunks, full_chunk, 0)
        for ch in range(nc):
            rows = slice(ch * ta, (ch + 1) * ta)
            dk_ref[rows, :] = (dk_scr[ch] * scale).astype(BF16)
            dv_ref[rows, :] = dv_scr[ch].astype(BF16)
            dck_ref[rows, :] = -jnp.sum(dck_scr[ch], axis=-1, keepdims=True)

        @pl.when(g == n_steps - 1)
        def _():
            for i in range(n_chunks):
                dq_ref[i * ta:(i + 1) * ta, :] = (dqt_scr[i] * scale).T.astype(BF16)
            dcq_ref[...] = dcq_scr[...]

    tk = nc * ta
    chunks = pl.BlockSpec((None, n_chunks, 1, ta), lambda h, g: (h, 0, 0, 0))
    tile = pl.BlockSpec((tk, HEAD_DIM), lambda h, g: (g, h))
    return pl.pallas_call(
        body, name=name, grid=(H, n_steps),
        in_specs=[pl.BlockSpec((None, tk, HEAD_DIM), lambda h, g: (1, g, h)),
                  pl.BlockSpec((None, tk, HEAD_DIM), lambda h, g: (2, g, h)),
                  pl.BlockSpec((None, nc, HEAD_DIM, ta), lambda h, g: (h, g, 0, 0)),
                  pl.BlockSpec((None, T, HEAD_DIM), lambda h, g: (0, 0, h)),
                  pl.BlockSpec((T, HEAD_DIM), lambda h, g: (0, h)),
                  pl.BlockSpec((None, tk, LANES), lambda h, g: (h, g, 0)),
                  chunks, chunks],
        out_specs=[tile, tile, pl.BlockSpec((None, tk, 1), lambda h, g: (h, g, 0)),
                   pl.BlockSpec((T, HEAD_DIM), lambda h, g: (0, h)), chunks],
        out_shape=[jax.ShapeDtypeStruct((T, W), BF16), jax.ShapeDtypeStruct((T, W), BF16),
                   jax.ShapeDtypeStruct((H, T, 1), F32), jax.ShapeDtypeStruct((T, W), BF16),
                   jax.ShapeDtypeStruct((H, n_chunks, 1, ta), F32)],
        scratch_shapes=[pltpu.VMEM((nc, ta, HEAD_DIM), F32), pltpu.VMEM((nc, ta, HEAD_DIM), F32),
                        pltpu.VMEM((nc, ta, LANES), F32), pltpu.VMEM((n_chunks, HEAD_DIM, ta), F32),
                        pltpu.VMEM((n_chunks, 1, ta), F32)],
        compiler_params=_params(("parallel", "arbitrary")),
    )(z7, z7, kt, z7, dob, c_rep, lse_chunks, d_chunks)


def _attn_bwd_q_loop(z7, dob, c_chunks, lse_col, d_col, name):
    _, T, W = z7.shape
    H = W // HEAD_DIM
    ta, nc, n_chunks = _attn_geometry(T)
    scale = np.float32(1.0 / np.sqrt(HEAD_DIM))

    def body(q_ref, k_ref, v_ref, do_ref, c_ref, lse_ref, d_ref, dq_ref, dc_ref, dq_scr, dc_scr):
        g = pl.program_id(1)
        dq_scr[...] = jnp.zeros_like(dq_scr)
        dc_scr[...] = jnp.zeros_like(dc_scr)

        def update(ch, k, v, crow, diagonal):
            rows = slice(ch * ta, (ch + 1) * ta)
            do = do_ref[rows, :]
            s = _mm_nt(q_ref[rows, :], k) * scale - crow - lse_ref[rows, :]
            if diagonal:
                s = jnp.where(_causal_tile(ta), s, NEG_BIG)
            p = jnp.exp(s)
            ds = p * (_mm_nt(do, v) - d_ref[rows, :])
            dq_scr[ch] += _mm(ds.astype(BF16), k)
            dc_scr[ch] += jnp.sum(ds, axis=-1, keepdims=True)

        def full_chunk(j, carry):
            k = _chunk(k_ref, j, ta)
            v = _chunk(v_ref, j, ta)
            crow = c_ref[j]
            for ch in range(nc):
                update(ch, k, v, crow, False)
            return carry

        lax.fori_loop(0, nc * g, full_chunk, 0)
        for jj in range(nc):
            j = nc * g + jj
            k = _chunk(k_ref, j, ta)
            v = _chunk(v_ref, j, ta)
            crow = c_ref[j]
            for ch in range(jj, nc):
                update(ch, k, v, crow, ch == jj)
        for ch in range(nc):
            dq_ref[ch * ta:(ch + 1) * ta, :] = (dq_scr[ch] * scale).astype(BF16)
            dc_ref[ch * ta:(ch + 1) * ta, :] = dc_scr[ch]

    tq = nc * ta
    col = pl.BlockSpec((None, tq, 1), lambda h, g: (h, g, 0))
    return pl.pallas_call(
        body, name=name, grid=(H, n_chunks // nc),
        in_specs=[pl.BlockSpec((None, tq, HEAD_DIM), lambda h, g: (0, g, h)),
                  pl.BlockSpec((None, T, HEAD_DIM), lambda h, g: (1, 0, h)),
                  pl.BlockSpec((None, T, HEAD_DIM), lambda h, g: (2, 0, h)),
                  pl.BlockSpec((tq, HEAD_DIM), lambda h, g: (g, h)),
                  pl.BlockSpec((None, n_chunks, 1, ta), lambda h, g: (h, 0, 0, 0)),
                  col, col],
        out_specs=[pl.BlockSpec((tq, HEAD_DIM), lambda h, g: (g, h)), col],
        out_shape=[jax.ShapeDtypeStruct((T, W), BF16), jax.ShapeDtypeStruct((H, T, 1), F32)],
        scratch_shapes=[pltpu.VMEM((nc, ta, HEAD_DIM), F32), pltpu.VMEM((nc, ta, 1), F32)],
        compiler_params=_params(("parallel", "arbitrary")),
    )(z7, z7, z7, dob, c_chunks, lse_col, d_col)


def _attn_bwd_kv_loop(z7, dob, c_col, lse_chunks, d_chunks, name):
    _, T, W = z7.shape
    H = W // HEAD_DIM
    ta, nc, n_chunks = _attn_geometry(T)
    scale = np.float32(1.0 / np.sqrt(HEAD_DIM))

    def body(k_ref, v_ref, q_ref, do_ref, ccol_ref, lse_ref, d_ref, dk_ref, dv_ref, dc_ref, dk_scr, dv_scr, dc_scr):
        g = pl.program_id(1)
        dk_scr[...] = jnp.zeros_like(dk_scr)
        dv_scr[...] = jnp.zeros_like(dv_scr)
        dc_scr[...] = jnp.zeros_like(dc_scr)

        def update(ch, q, do, lse_row, d_row, diagonal):
            rows = slice(ch * ta, (ch + 1) * ta)
            st = _mm_nt(k_ref[rows, :], q) * scale - ccol_ref[rows, :] - lse_row
            if diagonal:
                st = jnp.where(_causal_tile(ta, keys_on_rows=True), st, NEG_BIG)
            pt = jnp.exp(st)
            dv_scr[ch] += _mm(pt.astype(BF16), do)
            dst = pt * (_mm_nt(v_ref[rows, :], do) - d_row)
            dk_scr[ch] += _mm(dst.astype(BF16), q)
            dc_scr[ch] += jnp.sum(dst, axis=-1, keepdims=True)

        for ii in range(nc):
            i = nc * g + ii
            q = _chunk(q_ref, i, ta)
            do = _chunk(do_ref, i, ta)
            for ch in range(0, ii + 1):
                update(ch, q, do, lse_ref[i], d_ref[i], ch == ii)

        def full_chunk(i, carry):
            q = _chunk(q_ref, i, ta)
            do = _chunk(do_ref, i, ta)
            for ch in range(nc):
                update(ch, q, do, lse_ref[i], d_ref[i], False)
            return carry

        lax.fori_loop(nc * (g + 1), n_chunks, full_chunk, 0)
        for ch in range(nc):
            rows = slice(ch * ta, (ch + 1) * ta)
            dk_ref[rows, :] = (dk_scr[ch] * scale).astype(BF16)
            dv_ref[rows, :] = dv_scr[ch].astype(BF16)
            dc_ref[rows, :] = -dc_scr[ch]

    tk = nc * ta
    chunks = pl.BlockSpec((None, n_chunks, 1, ta), lambda h, g: (h, 0, 0, 0))
    col = pl.BlockSpec((None, tk, 1), lambda h, g: (h, g, 0))
    tile = pl.BlockSpec((tk, HEAD_DIM), lambda h, g: (g, h))
    return pl.pallas_call(
        body, name=name, grid=(H, n_chunks // nc),
        in_specs=[pl.BlockSpec((None, tk, HEAD_DIM), lambda h, g: (1, g, h)),
                  pl.BlockSpec((None, tk, HEAD_DIM), lambda h, g: (2, g, h)),
                  pl.BlockSpec((None, T, HEAD_DIM), lambda h, g: (0, 0, h)),
                  pl.BlockSpec((T, HEAD_DIM), lambda h, g: (0, h)),
                  col, chunks, chunks],
        out_specs=[tile, tile, col],
        out_shape=[jax.ShapeDtypeStruct((T, W), BF16), jax.ShapeDtypeStruct((T, W), BF16),
                   jax.ShapeDtypeStruct((H, T, 1), F32)],
        scratch_shapes=[pltpu.VMEM((nc, ta, HEAD_DIM), F32), pltpu.VMEM((nc, ta, HEAD_DIM), F32),
                        pltpu.VMEM((nc, ta, 1), F32)],
        compiler_params=_params(("parallel", "arbitrary")),
    )(z7, z7, z7, dob, c_col, lse_chunks, d_chunks)


def _chunk_causal_mask():
    rows = lax.broadcasted_iota(jnp.int32, (SGU_LEN, SGU_LEN), 0)
    cols = lax.broadcasted_iota(jnp.int32, (SGU_LEN, SGU_LEN), 1)
    return (cols // CHUNK) <= (rows // CHUNK)


def _sgu_norm_mix(sv, lng_ref, lnb_ref, ws_ref, bs_ref, vn_scr, mixed_scr, vhat_scr=None):
    tm = sv.shape[0]
    vs = _gelu(sv)
    mask = _chunk_causal_mask()
    rstds = []
    for g in range(N_GROUPS):
        lanes = slice(g * GROUP_DIM, (g + 1) * GROUP_DIM)
        blk = vs[:, lanes]
        cen = blk - jnp.mean(blk, axis=-1, keepdims=True)
        rstd = lax.rsqrt(jnp.mean(cen * cen, axis=-1, keepdims=True) + LN_EPS)
        vhat = cen * rstd
        rstds.append(rstd)
        if vhat_scr is not None:
            vhat_scr[:, lanes] = vhat
        vn_scr[:, lanes] = (vhat * lng_ref[:, lanes] + lnb_ref[:, lanes]).astype(BF16)
        wm = jnp.where(mask, ws_ref[g], 0.0).astype(BF16)
        for w in range(tm // SGU_LEN):
            rows = slice(w * SGU_LEN, (w + 1) * SGU_LEN)
            mixed_scr[rows, lanes] = _mm(wm, vn_scr[rows, lanes]) + bs_ref[g]
    return rstds


def _mix_out_fwd(z7, o_a, x1, lng, lnb, ws, bs, w_out, g_post, name):
    _, T, W = z7.shape
    D = x1.shape[1]
    tm = _blk(T, 256)

    def body(u_ref, sv_ref, ga_ref, gb_ref, oa_ref, x1_ref, lng_ref, lnb_ref, ws_ref, bs_ref, wo_ref, gp_ref,
             x2_ref, p_ref, mb_ref, vn_scr, mixed_scr):
        _sgu_norm_mix(sv_ref[...].astype(F32), lng_ref, lnb_ref, ws_ref, bs_ref, vn_scr, mixed_scr)
        o_b = _gelu(u_ref[...].astype(F32)) * mixed_scr[...]
        merged = (jax.nn.sigmoid(ga_ref[...].astype(F32)) * oa_ref[...]
                  + jax.nn.sigmoid(gb_ref[...].astype(F32)) * o_b).astype(BF16)
        mb_ref[...] = merged
        p = _mm(merged, wo_ref[...])
        p_ref[...] = p
        x2_ref[...] = x1_ref[...] + p * _rms_scale(p) * gp_ref[...]

    def seg(idx):
        return pl.BlockSpec((None, tm, W), lambda i, idx=idx: (idx, i, 0))

    row = pl.BlockSpec((tm, D), lambda i: (i, 0))
    vec = pl.BlockSpec((1, D), lambda i: (0, 0))
    return pl.pallas_call(
        body, name=name, grid=(T // tm,),
        in_specs=[seg(3), seg(4), seg(5), seg(6), row, row, vec, vec,
                  pl.BlockSpec((N_GROUPS, SGU_LEN, SGU_LEN), lambda i: (0, 0, 0)),
                  pl.BlockSpec((N_GROUPS, SGU_LEN, 1), lambda i: (0, 0, 0)),
                  pl.BlockSpec((D, D), lambda i: (0, 0)), vec],
        out_specs=[row, row, row],
        out_shape=[jax.ShapeDtypeStruct((T, D), F32), jax.ShapeDtypeStruct((T, D), F32),
                   jax.ShapeDtypeStruct((T, D), BF16)],
        scratch_shapes=[pltpu.VMEM((tm, W), BF16), pltpu.VMEM((tm, W), F32)],
        compiler_params=_params(("parallel",)),
    )(z7, z7, z7, z7, o_a, x1, lng, lnb, ws, bs, w_out, g_post)


def _mix_out_bwd(dx2, p, z7, o_a, lng, lnb, ws, bs, w_out, g_post, name, dep=None):
    _, T, W = z7.shape
    D = dx2.shape[1]
    tm = _blk(T, 256)
    n_w = tm // SGU_LEN

    def body(dx2_ref, p_ref, u_ref, sv_ref, ga_ref, gb_ref, oa_ref, lng_ref, lnb_ref, ws_ref, bs_ref, wo_ref, gp_ref, _,
             dpb_ref, dob_ref, dvec_ref, dz_ref, dgp_ref, dlng_ref, dlnb_ref, dws_ref, dbs_ref,
             vn_scr, mixed_scr, vhat_scr, dmix_scr, dvn_scr):
        @pl.when(pl.program_id(0) == 0)
        def _():
            dgp_ref[...] = jnp.zeros_like(dgp_ref)
            dlng_ref[...] = jnp.zeros_like(dlng_ref)
            dlnb_ref[...] = jnp.zeros_like(dlnb_ref)
            dws_ref[...] = jnp.zeros_like(dws_ref)
            dbs_ref[...] = jnp.zeros_like(dbs_ref)

        pv = p_ref[...]
        s = _rms_scale(pv)
        n = pv * s
        dn = dx2_ref[...]
        dgp_ref[...] += jnp.sum(dn * n, axis=0, keepdims=True)
        dpb = _rms_bwd(dn, n, s, gp_ref[...]).astype(BF16)
        dpb_ref[...] = dpb
        dmerged = _mm_nt(dpb, wo_ref[...])

        sv = sv_ref[...].astype(F32)
        rstds = _sgu_norm_mix(sv, lng_ref, lnb_ref, ws_ref, bs_ref, vn_scr, mixed_scr, vhat_scr)
        u_pre = u_ref[...].astype(F32)
        u = _gelu(u_pre)
        mixed = mixed_scr[...]
        sa = jax.nn.sigmoid(ga_ref[...].astype(F32))
        sb = jax.nn.sigmoid(gb_ref[...].astype(F32))
        oa = oa_ref[...]
        do_a = (dmerged * sa).astype(BF16)
        dob_ref[...] = do_a
        prod = do_a.astype(F32) * oa
        for h in range(N_HEADS):
            dvec_ref[h] = jnp.sum(prod[:, h * HEAD_DIM:(h + 1) * HEAD_DIM], axis=-1, keepdims=True)
        dz_ref[2] = (dmerged * oa * (sa * (1.0 - sa))).astype(BF16)
        dz_ref[3] = (dmerged * (u * mixed) * (sb * (1.0 - sb))).astype(BF16)
        do_b = dmerged * sb
        dz_ref[0] = (do_b * mixed * _gelu_grad(u_pre)).astype(BF16)
        dmix_scr[...] = do_b * u

        mask = _chunk_causal_mask()
        for g in range(N_GROUPS):
            lanes = slice(g * GROUP_DIM, (g + 1) * GROUP_DIM)
            wm = jnp.where(mask, ws_ref[g], 0.0).astype(BF16)
            dws = jnp.zeros((SGU_LEN, SGU_LEN), F32)
            dbs = jnp.zeros((SGU_LEN, 1), F32)
            for w in range(n_w):
                rows = slice(w * SGU_LEN, (w + 1) * SGU_LEN)
                dmix = dmix_scr[rows, lanes]
                dmix_b = dmix.astype(BF16)
                dvn_scr[rows, lanes] = _mm_tn(wm, dmix_b)
                dws = dws + _mm_nt(dmix_b, vn_scr[rows, lanes])
                dbs = dbs + jnp.sum(dmix, axis=-1, keepdims=True)
            dws_ref[g] += jnp.where(mask, dws, 0.0)
            dbs_ref[g] += dbs
            dvn = dvn_scr[:, lanes]
            vhat = vhat_scr[:, lanes]
            dlng_ref[:, lanes] += jnp.sum(dvn * vhat, axis=0, keepdims=True)
            dlnb_ref[:, lanes] += jnp.sum(dvn, axis=0, keepdims=True)
            dvh = dvn * lng_ref[:, lanes]
            dvs = rstds[g] * (dvh - jnp.mean(dvh, axis=-1, keepdims=True)
                              - vhat * jnp.mean(dvh * vhat, axis=-1, keepdims=True))
            dvn_scr[:, lanes] = dvs
        dz_ref[1] = (dvn_scr[...] * _gelu_grad(sv)).astype(BF16)

    def seg(idx):
        return pl.BlockSpec((None, tm, W), lambda i, idx=idx: (idx, i, 0))

    row = pl.BlockSpec((tm, D), lambda i: (i, 0))
    vec = pl.BlockSpec((1, D), lambda i: (0, 0))
    ws_spec = pl.BlockSpec((N_GROUPS, SGU_LEN, SGU_LEN), lambda i: (0, 0, 0))
    bs_spec = pl.BlockSpec((N_GROUPS, SGU_LEN, 1), lambda i: (0, 0, 0))
    return pl.pallas_call(
        body, name=name, grid=(T // tm,),
        in_specs=[row, row, seg(3), seg(4), seg(5), seg(6), row, vec, vec, ws_spec, bs_spec,
                  pl.BlockSpec((D, D), lambda i: (0, 0)), vec, ANY],
        out_specs=[row, row, pl.BlockSpec((N_HEADS, tm, 1), lambda i: (0, i, 0)),
                   pl.BlockSpec((4, tm, W), lambda i: (0, i, 0)), vec, vec, vec, ws_spec, bs_spec],
        out_shape=[jax.ShapeDtypeStruct((T, D), BF16), jax.ShapeDtypeStruct((T, W), BF16),
                   jax.ShapeDtypeStruct((N_HEADS, T, 1), F32), jax.ShapeDtypeStruct((4, T, W), BF16),
                   jax.ShapeDtypeStruct((1, D), F32), jax.ShapeDtypeStruct((1, D), F32),
                   jax.ShapeDtypeStruct((1, D), F32),
                   jax.ShapeDtypeStruct((N_GROUPS, SGU_LEN, SGU_LEN), F32),
                   jax.ShapeDtypeStruct((N_GROUPS, SGU_LEN, 1), F32)],
        scratch_shapes=[pltpu.VMEM((tm, W), BF16), pltpu.VMEM((tm, W), F32), pltpu.VMEM((tm, W), F32),
                        pltpu.VMEM((tm, W), F32), pltpu.VMEM((tm, W), F32)],
        compiler_params=_params(("arbitrary",)),
    )(dx2, p, z7, z7, z7, z7, o_a, lng, lnb, ws, bs, w_out, g_post, _after(dep))


def _loss_head(y, target, name):
    T, D = y.shape
    tm = _blk(T, 1024)
    n_i = T // tm

    def body(y_ref, t_ref, dy_ref, loss_ref, acc_scr):
        i = pl.program_id(0)

        @pl.when(i == 0)
        def _():
            acc_scr[...] = jnp.zeros_like(acc_scr)

        e = y_ref[...] - t_ref[...]
        dy_ref[...] = e * np.float32(1.0 / D)
        acc_scr[...] += jnp.sum(e * e, axis=0, keepdims=True)

        @pl.when(i == n_i - 1)
        def _():
            total = jnp.sum(acc_scr[...], axis=-1, keepdims=True) * np.float32(0.5 / D)
            loss_ref[...] = jnp.broadcast_to(total, loss_ref.shape)

    row = pl.BlockSpec((tm, D), lambda i: (i, 0))
    return pl.pallas_call(
        body, name=name, grid=(n_i,),
        in_specs=[row, row],
        out_specs=[row, pl.BlockSpec((1, LANES), lambda i: (0, 0))],
        out_shape=[jax.ShapeDtypeStruct((T, D), F32), jax.ShapeDtypeStruct((1, LANES), F32)],
        scratch_shapes=[pltpu.VMEM((1, D), F32)],
        compiler_params=_params(("arbitrary",)),
    )(y, target)


def _adamw_math(w, g, m, v):
    m_new = ADAM_B1 * m + (1.0 - ADAM_B1) * g
    v_new = ADAM_B2 * v + (1.0 - ADAM_B2) * (g * g)
    m_hat = m_new / np.float32(1.0 - ADAM_B1 ** ADAM_STEP)
    v_hat = v_new / np.float32(1.0 - ADAM_B2 ** ADAM_STEP)
    delta = -ADAM_LR * (m_hat / (jnp.sqrt(v_hat) + ADAM_EPS) + ADAM_WD * w)
    return delta, m_new, v_new


def _sum_adamw(parts, w, m, v, name):
    n, R, C = parts.shape
    tr = _blk(R, 128)

    def body(p_ref, w_ref, m_ref, v_ref, g_ref, d_ref, mo_ref, vo_ref):
        g = p_ref[0].astype(F32)
        for s in range(1, n):
            g = g + p_ref[s].astype(F32)
        delta, m_new, v_new = _adamw_math(w_ref[...], g, m_ref[...], v_ref[...])
        g_ref[...] = g
        d_ref[...] = delta
        mo_ref[...] = m_new
        vo_ref[...] = v_new

    row = pl.BlockSpec((tr, C), lambda i: (i, 0))
    shp = jax.ShapeDtypeStruct((R, C), F32)
    return pl.pallas_call(
        body, name=name, grid=(R // tr,),
        in_specs=[pl.BlockSpec((n, tr, C), lambda i: (0, i, 0)), row, row, row],
        out_specs=[row, row, row, row], out_shape=[shp, shp, shp, shp],
        compiler_params=_params(("parallel",)),
    )(parts, w, m, v)


def _adamw(g, w, m, v, name):
    R, C = g.shape
    tr = _blk(R, 128)

    def body(g_ref, w_ref, m_ref, v_ref, d_ref, mo_ref, vo_ref):
        delta, m_new, v_new = _adamw_math(w_ref[...], g_ref[...], m_ref[...], v_ref[...])
        d_ref[...] = delta
        mo_ref[...] = m_new
        vo_ref[...] = v_new

    row = pl.BlockSpec((tr, C), lambda i: (i, 0))
    shp = jax.ShapeDtypeStruct((R, C), F32)
    return pl.pallas_call(
        body, name=name, grid=(R // tr,),
        in_specs=[row, row, row, row], out_specs=[row, row, row], out_shape=[shp, shp, shp],
        compiler_params=_params(("parallel",)),
    )(g, w, m, v)


def _position():
    return lax.axis_index("x"), lax.axis_index("y"), lax.axis_index("c")


def _slot(px, py, pc):
    return 4 * px + 2 * py + pc


def _all_gather(shards, name):
    n = len(shards)

    def body(*refs):
        ins, outs = refs[:n], refs[n:2 * n]
        send_sems, recv_sems, local_sems = refs[2 * n:]
        x, y, c = _position()
        me, sibling = (x, y, c), (x, y, 1 - c)
        chips = [(1 - x, y), (x, 1 - y), (1 - x, 1 - y)]

        def copy(a, k, block, to, src=None):
            dst = outs[a].at[_slot(*block)]
            return pltpu.make_async_remote_copy(
                src_ref=dst if src is None else src, dst_ref=dst,
                send_sem=send_sems.at[a, k], recv_sem=recv_sems.at[a, k],
                device_id=to, device_id_type=MESH)

        mine = [pltpu.make_async_copy(ins[a], outs[a].at[_slot(*me)], local_sems.at[a]) for a in range(n)]
        for cp in mine:
            cp.start()
        first = []
        for a in range(n):
            first.append(copy(a, 0, me, sibling, src=ins[a]))
            first += [copy(a, 1 + j, me, (*chip, c), src=ins[a]) for j, chip in enumerate(chips)]
        for cp in first:
            cp.start()
        passed = []
        for j, chip in enumerate(chips):
            for a in range(n):
                copy(a, 1 + j, (*chip, c), me).wait_recv()
                fwd = copy(a, 4 + j, (*chip, c), sibling)
                fwd.start()
                passed.append(fwd)
        for a in range(n):
            copy(a, 0, sibling, me).wait_recv()
            for j, chip in enumerate(chips):
                copy(a, 4 + j, (*chip, 1 - c), me).wait_recv()
        for cp in first + passed:
            cp.wait_send()
        for cp in mine:
            cp.wait()

    return pl.pallas_call(
        body, name=name,
        in_specs=[ANY] * n, out_specs=[ANY] * n,
        out_shape=[jax.ShapeDtypeStruct((N_DEV,) + s.shape, s.dtype) for s in shards],
        scratch_shapes=[pltpu.SemaphoreType.DMA((n, 7)), pltpu.SemaphoreType.DMA((n, 7)),
                        pltpu.SemaphoreType.DMA((n,))],
    )(*shards)


def _peer(x, y, c, k):
    return (1 - x if k & 4 else x, 1 - y if k & 2 else y, 1 - c if k & 1 else c)


def _exchange(parts, name):
    n = len(parts)

    def body(*refs):
        ins, outs = refs[:n], refs[n:2 * n]
        send_sems, recv_sems, local_sems = refs[2 * n:]
        x, y, c = _position()
        me = _slot(x, y, c)
        mine = [pltpu.make_async_copy(ins[a].at[me], outs[a].at[me], local_sems.at[a]) for a in range(n)]
        for cp in mine:
            cp.start()
        sends = []
        for k in range(1, N_DEV):
            to = _peer(x, y, c, k)
            for a in range(n):
                cp = pltpu.make_async_remote_copy(
                    src_ref=ins[a].at[_slot(*to)], dst_ref=outs[a].at[me],
                    send_sem=send_sems.at[a, k - 1], recv_sem=recv_sems.at[a, k - 1],
                    device_id=to, device_id_type=MESH)
                cp.start()
                sends.append(cp)
        for k in range(1, N_DEV):
            frm = _peer(x, y, c, k)
            for a in range(n):
                pltpu.make_async_remote_copy(
                    src_ref=ins[a].at[_slot(*frm)], dst_ref=outs[a].at[_slot(*frm)],
                    send_sem=send_sems.at[a, k - 1], recv_sem=recv_sems.at[a, k - 1],
                    device_id=frm, device_id_type=MESH).wait_recv()
        for cp in sends:
            cp.wait_send()
        for cp in mine:
            cp.wait()

    return pl.pallas_call(
        body, name=name,
        in_specs=[ANY] * n, out_specs=[ANY] * n,
        out_shape=[jax.ShapeDtypeStruct(p.shape, p.dtype) for p in parts],
        scratch_shapes=[pltpu.SemaphoreType.DMA((n, 7)), pltpu.SemaphoreType.DMA((n, 7)),
                        pltpu.SemaphoreType.DMA((n,))],
    )(*parts)


HBM_SPEC = pl.BlockSpec(memory_space=pltpu.HBM)
SEM_SPEC = pl.BlockSpec(memory_space=pltpu.SEMAPHORE)
SIDE_EFFECT = pltpu.SideEffectType.DATAFLOW_SIDE_EFFECTING


def _remote_copies(src_refs, land_refs, send_sems, recv_sems, gather, outgoing):
    x, y, c = _position()
    me = _slot(x, y, c)
    copies = []
    for k in range(1, N_DEV):
        peer = _peer(x, y, c, k)
        for a in range(len(src_refs)):
            src = src_refs[a] if gather else src_refs[a].at[_slot(*peer)]
            dst = land_refs[a].at[me if outgoing else _slot(*peer)]
            sem = a * (N_DEV - 1) + k - 1
            copies.append(pltpu.make_async_remote_copy(
                src_ref=src, dst_ref=dst, send_sem=send_sems.at[sem], recv_sem=recv_sems.at[sem],
                device_id=peer, device_id_type=MESH))
    return copies


def _remote_start(srcs, after, name, gather):
    n = len(srcs)
    lands = [jax.ShapeDtypeStruct(((N_DEV,) + s.shape) if gather else s.shape, s.dtype) for s in srcs]

    def body(*refs):
        src_refs, land_refs = refs[:n], refs[n:2 * n]
        send_sems, recv_sems = refs[2 * n + 1], refs[2 * n + 2]
        token, local_sems = refs[4 * n + 3], refs[4 * n + 4]
        x, y, c = _position()
        me = _slot(x, y, c)
        mine = [pltpu.make_async_copy(src_refs[a] if gather else src_refs[a].at[me], land_refs[a].at[me],
                                      local_sems.at[a]) for a in range(n)]
        for cp in mine:
            cp.start()
        for cp in _remote_copies(src_refs, land_refs, send_sems, recv_sems, gather, outgoing=True):
            cp.start()
        for cp in mine:
            cp.wait()
        token[...] = jnp.zeros_like(token)

    sem_shape = pltpu.SemaphoreType.DMA((n * (N_DEV - 1),))
    outs = pl.pallas_call(
        body, name=name,
        out_shape=(sem_shape, sem_shape, *[pltpu.HBM(s.shape, s.dtype) for s in srcs],
                   *[pltpu.HBM(l.shape, l.dtype) for l in lands], jax.ShapeDtypeStruct((8, LANES), F32)),
        in_specs=[HBM_SPEC] * (2 * n) + [ANY],
        out_specs=(SEM_SPEC, SEM_SPEC, *([HBM_SPEC] * (2 * n)), pl.BlockSpec(memory_space=pltpu.VMEM)),
        input_output_aliases={a: 2 + a for a in range(2 * n)},
        scratch_shapes=[pltpu.SemaphoreType.DMA((n,))],
        compiler_params=pltpu.CompilerParams(has_side_effects=SIDE_EFFECT),
    )(*[pltpu.with_memory_space_constraint(s, pltpu.HBM) for s in srcs],
      *[pltpu.with_memory_space_constraint(lax.empty(l.shape, l.dtype), pltpu.HBM) for l in lands], after)
    return dict(send=outs[0], recv=outs[1], srcs=outs[2:2 + n], lands=outs[2 + n:2 + 2 * n], token=outs[-1],
                gather=gather)


def _remote_wait(flight, after, name):
    n = len(flight["srcs"])
    gather = flight["gather"]

    def body(*refs):
        src_refs, land_refs = refs[:n], refs[n:2 * n]
        send_sems, recv_sems = refs[2 * n], refs[2 * n + 1]
        for cp in _remote_copies(src_refs, land_refs, send_sems, recv_sems, gather, outgoing=False):
            cp.wait_send()
            cp.wait_recv()

    both = list(flight["srcs"]) + list(flight["lands"])
    outs = pl.pallas_call(
        body, name=name,
        out_shape=tuple(pltpu.HBM(a.shape, a.dtype) for a in both),
        in_specs=[HBM_SPEC] * (2 * n) + [SEM_SPEC, SEM_SPEC, ANY],
        out_specs=tuple([HBM_SPEC] * (2 * n)),
        input_output_aliases={a: a for a in range(2 * n)},
        compiler_params=pltpu.CompilerParams(has_side_effects=SIDE_EFFECT),
    )(*both, flight["send"], flight["recv"], after)
    return list(outs[n:])


def _sequencer_exchange(srcs, name, gather, collective_id):
    n = len(srcs)
    hbm = pltpu.MemorySpace.HBM
    src_refs = [jax.new_ref(s, memory_space=hbm) for s in srcs]
    land_refs = [jax.empty_ref(jax.ShapeDtypeStruct(((N_DEV,) + s.shape) if gather else s.shape, s.dtype),
                               memory_space=hbm) for s in srcs]
    n_sems = n * (N_DEV - 1)

    @pl.kernel(mesh=plsc.ScalarSubcoreMesh(axis_name="sequencer", num_cores=1), name=name,
               scratch_types=(pltpu.SemaphoreType.DMA((n_sems,)), pltpu.SemaphoreType.DMA((n_sems,)),
                              pltpu.SemaphoreType.DMA((n,))),
               compiler_params=pltpu.CompilerParams(collective_id=collective_id))
    def launch(send_sems, recv_sems, local_sems):
        x, y, c = _position()
        me = _slot(x, y, c)
        barrier = pltpu.get_barrier_semaphore()
        for k in range(1, N_DEV):
            pl.semaphore_signal(barrier, inc=1, device_id=_peer(x, y, c, k), device_id_type=MESH)
        pl.semaphore_wait(barrier, N_DEV - 1)
        mine = [pltpu.make_async_copy(src_refs[a] if gather else src_refs[a].at[me], land_refs[a].at[me],
                                      local_sems.at[a]) for a in range(n)]
        for cp in mine:
            cp.start()
        sends = _remote_copies(src_refs, land_refs, send_sems, recv_sems, gather, outgoing=True)
        for cp in sends:
            cp.start()
        for cp in _remote_copies(src_refs, land_refs, send_sems, recv_sems, gather, outgoing=False):
            cp.wait_recv()
        for cp in sends:
            cp.wait_send()
        for cp in mine:
            cp.wait()

    launch()
    return [r[...] for r in land_refs]


def _all_reduce_small(blob, name):
    R, C = blob.shape

    def body(in_ref, out_ref, gath, send_sems, recv_sems):
        x, y, c = _position()
        me = _slot(x, y, c)
        gath[me] = in_ref[...]
        sends = []
        for k in range(1, N_DEV):
            to = _peer(x, y, c, k)
            cp = pltpu.make_async_remote_copy(
                src_ref=in_ref, dst_ref=gath.at[me],
                send_sem=send_sems.at[k - 1], recv_sem=recv_sems.at[k - 1],
                device_id=to, device_id_type=MESH)
            cp.start()
            sends.append(cp)
        for k in range(1, N_DEV):
            frm = _peer(x, y, c, k)
            pltpu.make_async_remote_copy(
                src_ref=in_ref, dst_ref=gath.at[_slot(*frm)],
                send_sem=send_sems.at[k - 1], recv_sem=recv_sems.at[k - 1],
                device_id=frm, device_id_type=MESH).wait_recv()
        for cp in sends:
            cp.wait_send()
        total = gath[0]
        for s in range(1, N_DEV):
            total = total + gath[s]
        out_ref[...] = total

    return pl.pallas_call(
        body, name=name,
        in_specs=[pl.BlockSpec(memory_space=pltpu.VMEM)],
        out_specs=pl.BlockSpec(memory_space=pltpu.VMEM),
        out_shape=jax.ShapeDtypeStruct((R, C), F32),
        scratch_shapes=[pltpu.VMEM((N_DEV, R, C), F32), pltpu.SemaphoreType.DMA((7,)),
                        pltpu.SemaphoreType.DMA((7,))],
        compiler_params=pltpu.CompilerParams(vmem_limit_bytes=VMEM_LIMIT),
    )(blob)


SMALL_VECS = ("ffn1_pre_g", "ffn1_post_g", "mix_pre_g", "sgu_ln_g", "sgu_ln_b", "mix_post_g", "ffn2_pre_g",
              "ffn2_post_g")
ROW_BS = len(SMALL_VECS)
ROW_BF = ROW_BS + 1
ROW_LOSS = ROW_BF + 1
ROW_WS = 16
BLOB_ROWS = ROW_WS + SGU_LEN


def _pack_small(vals, D, loss_row=None):
    rows = [vals[n].reshape(1, D) for n in SMALL_VECS]
    rows.append(vals["sgu_b_s"].reshape(1, D))
    rows.append(jnp.pad(vals["b_forget"].reshape(1, N_HEADS), ((0, 0), (0, D - N_HEADS))))
    rows.append(jnp.zeros((1, D), F32) if loss_row is None else loss_row)
    rows.append(jnp.zeros((ROW_WS - ROW_LOSS - 1, D), F32))
    rows.append(vals["sgu_w_s"].reshape(SGU_LEN, D))
    return jnp.concatenate(rows, axis=0)


def _unpack_small(blob, D):
    out = {n: blob[r:r + 1] for r, n in enumerate(SMALL_VECS)}
    out["sgu_b_s"] = blob[ROW_BS].reshape(1, N_GROUPS, SGU_LEN)
    out["b_forget"] = blob[ROW_BF, :N_HEADS].reshape(1, N_HEADS)
    out["sgu_w_s"] = blob[ROW_WS:].reshape(1, N_GROUPS, SGU_LEN, SGU_LEN)
    return out


WEIGHT_NAMES = ("ffn1_pre_g", "ffn1_w_gate", "ffn1_w_up", "ffn1_w_down", "ffn1_post_g", "mix_pre_g", "w_in",
                "b_forget", "sgu_ln_g", "sgu_ln_b", "sgu_w_s", "sgu_b_s", "w_out", "mix_post_g", "ffn2_pre_g",
                "ffn2_w_gate", "ffn2_w_up", "ffn2_w_down", "ffn2_post_g")
BIG_NAMES = ("ffn1_w_gate", "ffn1_w_up", "ffn1_w_down", "w_in", "w_out", "ffn2_w_gate", "ffn2_w_up", "ffn2_w_down")
WEIGHT_GROUPS = {"ffn1": ("ffn1_w_gate", "ffn1_w_up", "ffn1_w_down"), "mix": ("w_in", "w_out"),
                 "ffn2": ("ffn2_w_gate", "ffn2_w_up", "ffn2_w_down")}
GRAD_GROUPS = (("ffn2_w_gate", "ffn2_w_up", "ffn2_w_down"), ("ffn1_w_down", "ffn1_w_gate"), ("ffn1_w_up",),
               ("w_in",), ("w_out",))


def _local_step(x, target, small, fetch, emit):
    T, D = x.shape
    W = N_HEADS * HEAD_DIM
    vec = lambda n: small[n].reshape(1, D)
    big = dict(fetch("ffn1", x))

    x1, y1, dgf1, silu1, act1 = _ffn_fwd(x, vec("ffn1_pre_g"), big["ffn1_w_gate"], big["ffn1_w_up"], big["ffn1_w_down"],
                                  vec("ffn1_post_g"), "ffn1_fwd")

    big.update(fetch("mix", x1))
    w_in_all = big["w_in"]
    in_width = N_DEV * w_in_all.shape[2]
    w_in = w_in_all.transpose(1, 0, 2).reshape(D, in_width)
    col_f = 3 * W
    col_u = col_f + N_HEADS
    seg_starts = (0, W, 2 * W, col_u, col_u + W, col_u + 2 * W, col_u + 3 * W)
    w7 = jnp.stack([w_in[:, s:s + W] for s in seg_starts])
    wf = jnp.pad(w_in[:, col_f:col_u], ((0, 0), (0, LANES - N_HEADS)))
    w_out = big["w_out"].reshape(D, D)
    b_pad = jnp.pad(small["b_forget"].reshape(1, N_HEADS), ((0, 0), (0, LANES - N_HEADS)))
    lng, lnb = vec("sgu_ln_g"), vec("sgu_ln_b")
    ws = small["sgu_w_s"].reshape(N_GROUPS, SGU_LEN, SGU_LEN)
    bs = small["sgu_b_s"].reshape(N_GROUPS, SGU_LEN, 1)

    z7, f_logit, h2b = _mix_in_fwd(x1, vec("mix_pre_g"), w7, wf, "mix_in_fwd")
    c = _forget_cumsum(f_logit, b_pad, "forget_cumsum")
    c_heads = c[:, :N_HEADS].T
    ta, _, n_chunks = _attn_geometry(T)
    c_chunks = c_heads.reshape(N_HEADS, n_chunks, 1, ta)
    c_col = c_heads[:, :, None]
    vt = z7[2].reshape(n_chunks, ta, N_HEADS, HEAD_DIM).transpose(2, 0, 3, 1)
    c_rep = jnp.broadcast_to(c_col, (N_HEADS, T, LANES))
    o_a, lse_chunks = _attn_fwd_keys_on_rows(z7, vt, c_rep, "attn_fwd")
    lse = lse_chunks.reshape(N_HEADS, T, 1)
    x2, p, merged_b = _mix_out_fwd(z7, o_a, x1, lng, lnb, ws, bs, w_out, vec("mix_post_g"), "mix_out_fwd")
    big.update(fetch("ffn2", x2))
    x3, y2, dgf2, silu2, act2 = _ffn_fwd(x2, vec("ffn2_pre_g"), big["ffn2_w_gate"], big["ffn2_w_up"], big["ffn2_w_down"],
                                  vec("ffn2_post_g"), "ffn2_fwd")
    dy, loss_lanes = _loss_head(x3, target, "loss_head")

    grads_small = {}

    dx2, h3b, dy2b, dgate2, dup2, dgpre, dgpost = _ffn_bwd(
        dy, x2, y2, dgf2, silu2, vec("ffn2_pre_g"), big["ffn2_w_gate"], big["ffn2_w_up"], big["ffn2_w_down"],
        vec("ffn2_post_g"), "ffn2_bwd")
    grads_small["ffn2_pre_g"] = jnp.sum(dgpre, axis=0)
    grads_small["ffn2_post_g"] = jnp.sum(dgpost, axis=0)
    dep = emit("ffn2_w_gate", _wgrad(h3b, dgate2, "ffn2_wgrad_gate", shard_cols=True))
    dep = emit("ffn2_w_up", _wgrad(h3b, dup2, "ffn2_wgrad_up", shard_cols=True, dep=dep))
    dep = emit("ffn2_w_down", _wgrad(act2, dy2b, "ffn2_wgrad_down", dep=dep).reshape(big["ffn2_w_down"].shape))

    dpb, dob, dvec, dz4, dgp, dlng, dlnb, dws, dbs = _mix_out_bwd(
        dx2, p, z7, o_a, lng, lnb, ws, bs, w_out, vec("mix_post_g"), "mix_out_bwd", dep=dep)
    grads_small["mix_post_g"] = dgp
    grads_small["sgu_ln_g"] = dlng
    grads_small["sgu_ln_b"] = dlnb
    grads_small["sgu_w_s"] = dws
    grads_small["sgu_b_s"] = dbs
    d_chunks = dvec.reshape(N_HEADS, n_chunks, 1, ta)
    kt = z7[1].reshape(n_chunks, ta, N_HEADS, HEAD_DIM).transpose(2, 0, 3, 1)
    dk, dv, dc, dq, dc_q = _attn_bwd_fused(z7, kt, dob, c_rep, lse_chunks, d_chunks, "attn_bwd")
    dc_pad = jnp.pad((dc.reshape(N_HEADS, T) + dc_q.reshape(N_HEADS, T)).T, ((0, 0), (0, LANES - N_HEADS)))
    dfb, dbf = _forget_bwd(dc_pad, f_logit, b_pad, "forget_bwd")
    grads_small["b_forget"] = dbf[:, :N_HEADS]
    segs = [(dq, None), (dk, None), (dv, None), (dz4, 0), (dz4, 1), (dz4, 2), (dz4, 3)]
    dx1, dgm = _mix_in_bwd(dx2, x1, vec("mix_pre_g"), segs, dfb, w7, wf, "mix_in_bwd")
    grads_small["mix_pre_g"] = jnp.sum(dgm, axis=0)

    dx0, h1b, dy1b, dgate1, dup1, dgpre1, dgpost1 = _ffn_bwd(
        dx1, x, y1, dgf1, silu1, vec("ffn1_pre_g"), big["ffn1_w_gate"], big["ffn1_w_up"], big["ffn1_w_down"],
        vec("ffn1_post_g"), "ffn1_bwd")
    grads_small["ffn1_pre_g"] = jnp.sum(dgpre1, axis=0)
    grads_small["ffn1_post_g"] = jnp.sum(dgpost1, axis=0)
    dep = emit("ffn1_w_down", _wgrad(act1, dy1b, "ffn1_wgrad_down").reshape(big["ffn1_w_down"].shape))
    dep = emit("ffn1_w_gate", _wgrad(h1b, dgate1, "ffn1_wgrad_gate", shard_cols=True, dep=dep))
    dep = emit("ffn1_w_up", _wgrad(h1b, dup1, "ffn1_wgrad_up", shard_cols=True, dep=dep))

    seg_mats = [dq, dk, dv, dz4[0], dz4[1], dz4[2], dz4[3]]
    dw_seg = []
    for q, sm in enumerate(seg_mats):
        dw_seg.append(_wgrad(h2b, sm, "w_in_wgrad_%d" % q, dep=dep))
        dep = dw_seg[-1]
    dwf = _wgrad(h2b, dfb, "w_in_wgrad_f", dep=dep)
    dw_in = jnp.concatenate(dw_seg[:3] + [dwf[:, :N_HEADS]] + dw_seg[3:], axis=1)
    emit("w_in", dw_in.reshape(D, N_DEV, in_width // N_DEV).transpose(1, 0, 2))
    emit("w_out", _wgrad(merged_b, dpb, "w_out_wgrad", dep=dwf).reshape(big["w_out"].shape))

    loss_row = jnp.pad(loss_lanes, ((0, 0), (0, D - LANES)))
    return loss_row, dx0, grads_small


def kernel(x, ffn1_pre_g, ffn1_w_gate, ffn1_w_up, ffn1_w_down, ffn1_post_g, mix_pre_g, w_in, b_forget, sgu_ln_g, sgu_ln_b, sgu_w_s, sgu_b_s, w_out, mix_post_g, ffn2_pre_g, ffn2_w_gate, ffn2_w_up, ffn2_w_down, ffn2_post_g, loss_target, m_ffn1_pre_g, m_ffn1_w_gate, m_ffn1_w_up, m_ffn1_w_down, m_ffn1_post_g, m_mix_pre_g, m_w_in, m_b_forget, m_sgu_ln_g, m_sgu_ln_b, m_sgu_w_s, m_sgu_b_s, m_w_out, m_mix_post_g, m_ffn2_pre_g, m_ffn2_w_gate, m_ffn2_w_up, m_ffn2_w_down, m_ffn2_post_g, v_ffn1_pre_g, v_ffn1_w_gate, v_ffn1_w_up, v_ffn1_w_down, v_ffn1_post_g, v_mix_pre_g, v_w_in, v_b_forget, v_sgu_ln_g, v_sgu_ln_b, v_sgu_w_s, v_sgu_b_s, v_w_out, v_mix_post_g, v_ffn2_pre_g, v_ffn2_w_gate, v_ffn2_w_up, v_ffn2_w_down, v_ffn2_post_g):
    weights = dict(zip(WEIGHT_NAMES, (ffn1_pre_g, ffn1_w_gate, ffn1_w_up, ffn1_w_down, ffn1_post_g, mix_pre_g, w_in,
                                      b_forget, sgu_ln_g, sgu_ln_b, sgu_w_s, sgu_b_s, w_out, mix_post_g, ffn2_pre_g,
                                      ffn2_w_gate, ffn2_w_up, ffn2_w_down, ffn2_post_g)))
    mom1 = dict(zip(WEIGHT_NAMES, (m_ffn1_pre_g, m_ffn1_w_gate, m_ffn1_w_up, m_ffn1_w_down, m_ffn1_post_g,
                                   m_mix_pre_g, m_w_in, m_b_forget, m_sgu_ln_g, m_sgu_ln_b, m_sgu_w_s, m_sgu_b_s,
                                   m_w_out, m_mix_post_g, m_ffn2_pre_g, m_ffn2_w_gate, m_ffn2_w_up, m_ffn2_w_down,
                                   m_ffn2_post_g)))
    mom2 = dict(zip(WEIGHT_NAMES, (v_ffn1_pre_g, v_ffn1_w_gate, v_ffn1_w_up, v_ffn1_w_down, v_ffn1_post_g,
                                   v_mix_pre_g, v_w_in, v_b_forget, v_sgu_ln_g, v_sgu_ln_b, v_sgu_w_s, v_sgu_b_s,
                                   v_w_out, v_mix_post_g, v_ffn2_pre_g, v_ffn2_w_gate, v_ffn2_w_up, v_ffn2_w_down,
                                   v_ffn2_post_g)))
    D = x.shape[-1]
    small_names = [n for n in WEIGHT_NAMES if n not in BIG_NAMES]

    small = {n: weights[n] for n in small_names}
    shard = lambda n: weights[n][0].astype(BF16)

    ffn1_full = _all_gather([shard(n) for n in WEIGHT_GROUPS["ffn1"]], "ffn1_all_gather")
    gathered = {}
    for cid, grp in ((1, "mix"), (2, "ffn2")):
        shards, _ = lax.optimization_barrier(([shard(n) for n in WEIGHT_GROUPS[grp]], ffn1_full[0]))
        gathered[grp] = _sequencer_exchange(shards, grp + "_gather", True, cid)

    def fetch(group, after):
        if group == "ffn1":
            return zip(WEIGHT_GROUPS[group], ffn1_full)
        arrived, _ = lax.optimization_barrier((gathered[group], after))
        return zip(WEIGHT_GROUPS[group], arrived)

    ready, received = {}, {}

    def emit(name, part):
        ready[name] = part
        for gi, group in enumerate(GRAD_GROUPS):
            if name == group[-1]:
                lands = _sequencer_exchange([ready[n] for n in group], name + "_grad_exchange", False, 3 + gi)
                received.update(zip(group, lands))
        return part

    loss_row, grad_x, grads_small = _local_step(x[0], loss_target[0], small, fetch, emit)

    blobs = _sequencer_exchange([_pack_small(grads_small, D, loss_row)], "small_gather", True,
                                3 + len(GRAD_GROUPS))[0]

    out = {}
    for group in GRAD_GROUPS:
        for n in group:
            g, d, m_new, v_new = _sum_adamw(received[n], weights[n][0], mom1[n][0], mom2[n][0], "adamw_" + n)
            out[n] = tuple(a[None] for a in (g, d, m_new, v_new))

    blob, d_blob, m_blob, v_blob = _sum_adamw(
        blobs, _pack_small(small, D), _pack_small({n: mom1[n] for n in small_names}, D),
        _pack_small({n: mom2[n] for n in small_names}, D), "adamw_small")
    unpacked = [_unpack_small(b, D) for b in (blob, d_blob, m_blob, v_blob)]
    for n in small_names:
        out[n] = tuple(u[n].reshape(weights[n].shape) for u in unpacked)

    loss = blob[ROW_LOSS, 0]
    result = [loss, grad_x[None]]
    for k in range(4):
        result += [out[n][k] for n in WEIGHT_NAMES]
    return tuple(result)
```

```python
import functools

import numpy as np
import jax
import jax.numpy as jnp
from jax import lax
from jax.experimental import pallas as pl
from jax.experimental.pallas import tpu as pltpu
from jax.experimental.pallas import tpu_sc as plsc

F32 = jnp.float32
BF16 = jnp.bfloat16

RMS_EPS = 1e-6
LN_EPS = 1e-5
HEAD_DIM = 128
N_HEADS = 8
GROUP_DIM = 128
N_GROUPS = 8
SGU_LEN = 128
CHUNK = 64
N_DEV = 8
LANES = 128
VMEM_LIMIT = 56 * 1024 * 1024
NEG_BIG = -1e30

ADAM_LR = 0.001
ADAM_B1 = 0.9
ADAM_B2 = 0.999
ADAM_EPS = 1e-08
ADAM_WD = 0.01
ADAM_STEP = 10

MESH = pl.DeviceIdType.MESH
ANY = pl.BlockSpec(memory_space=pl.ANY)


def _blk(n, pref):
    return pref if (n >= pref and n % pref == 0) else n


def _mm(a, b):
    return jnp.dot(a, b, preferred_element_type=F32)


def _mm_nt(a, b):
    return lax.dot_general(a, b, (((1,), (1,)), ((), ())), preferred_element_type=F32)


def _mm_tn(a, b):
    return lax.dot_general(a, b, (((0,), (0,)), ((), ())), preferred_element_type=F32)


def _params(sem):
    return pltpu.CompilerParams(dimension_semantics=sem, vmem_limit_bytes=VMEM_LIMIT)


def _gelu(x):
    return 0.5 * x * (1.0 + lax.erf(x * np.float32(1.0 / np.sqrt(2.0))))


def _gelu_grad(x):
    cdf = 0.5 * (1.0 + lax.erf(x * np.float32(1.0 / np.sqrt(2.0))))
    return cdf + x * jnp.exp(-0.5 * x * x) * np.float32(1.0 / np.sqrt(2.0 * np.pi))


def _rms_scale(v):
    return lax.rsqrt(jnp.mean(v * v, axis=-1, keepdims=True) + RMS_EPS)


def _rms_bwd(dy, xhat, r, g):
    dxh = dy * g
    return r * (dxh - xhat * jnp.mean(dxh * xhat, axis=-1, keepdims=True))


def _ffn_rows(T):
    tm = _blk(T, 1024)
    th = _blk(tm, 512)
    return tm, th, tm // th


def _ffn_fwd(x, g_pre, wg, wu, wd, g_post, name):
    T, D = x.shape
    ns, _, fs = wg.shape
    tm, th, parts = _ffn_rows(T)

    def body(x_ref, gpre_ref, wg_ref, wu_ref, wd_ref, gpost_ref, xo_ref, y_ref, dgf_ref, silu_ref, act_ref,
             h_scr, acc_scr):
        j = pl.program_id(1)

        @pl.when(j == 0)
        def _():
            for r in range(parts):
                rows = slice(r * th, (r + 1) * th)
                xv = x_ref[rows, :]
                h_scr[rows, :] = (xv * _rms_scale(xv) * gpre_ref[...]).astype(BF16)
            acc_scr[...] = jnp.zeros_like(acc_scr)

        pre = []
        for r in range(parts):
            h = h_scr[r * th:(r + 1) * th, :]
            pre.append((_mm(h, wg_ref[...]), _mm(h, wu_ref[...])))
        for r in range(parts):
            rows = slice(r * th, (r + 1) * th)
            gg, uu = pre[r]
            sg = jax.nn.sigmoid(gg)
            silu = gg * sg
            act = (silu * uu).astype(BF16)
            dgf_ref[rows, :] = (uu * (sg * (1.0 + gg * (1.0 - sg)))).astype(BF16)
            silu_ref[rows, :] = silu.astype(BF16)
            act_ref[rows, :] = act
            acc_scr[rows, :] += _mm(act, wd_ref[...])

        @pl.when(j == ns - 1)
        def _():
            for r in range(parts):
                rows = slice(r * th, (r + 1) * th)
                y = acc_scr[rows, :]
                y_ref[rows, :] = y
                xo_ref[rows, :] = x_ref[rows, :] + 0.5 * (y * _rms_scale(y) * gpost_ref[...])

    row = pl.BlockSpec((tm, D), lambda i, j: (i, 0), pipeline_mode=pl.Buffered(1))
    vec = pl.BlockSpec((1, D), lambda i, j: (0, 0))
    return pl.pallas_call(
        body, name=name, grid=(T // tm, ns),
        in_specs=[row, vec,
                  pl.BlockSpec((None, D, fs), lambda i, j: (j, 0, 0)),
                  pl.BlockSpec((None, D, fs), lambda i, j: (j, 0, 0)),
                  pl.BlockSpec((None, fs, D), lambda i, j: (j, 0, 0)),
                  vec],
        out_specs=[row, row] + [pl.BlockSpec((tm, fs), lambda i, j: (i, j))] * 3,
        out_shape=[jax.ShapeDtypeStruct((T, D), F32), jax.ShapeDtypeStruct((T, D), F32)]
        + [jax.ShapeDtypeStruct((T, ns * fs), BF16)] * 3,
        scratch_shapes=[pltpu.VMEM((tm, D), BF16), pltpu.VMEM((tm, D), F32)],
        compiler_params=_params(("parallel", "arbitrary")),
    )(x, g_pre, wg, wu, wd, g_post)


def _after(dep):
    return jnp.zeros((8, LANES), F32) if dep is None else dep


def _ffn_bwd(dxo, x, y, dgf, silu, g_pre, wg, wu, wd, g_post, name, dep=None):
    T, D = x.shape
    ns, _, fs = wg.shape
    tm, th, parts = _ffn_rows(T)
    n_i = T // tm

    def body(dxo_ref, x_ref, y_ref, dgf_ref, silu_ref, gpre_ref, wg_ref, wu_ref, wd_ref, gpost_ref, _,
             dx_ref, hb_ref, dyb_ref, dgb_ref, dub_ref, dgpre_ref, dgpost_ref, dy_scr, acc_scr):
        j = pl.program_id(1)

        @pl.when(j == 0)
        def _():
            dgpost = jnp.zeros((1, D), F32)
            for r in range(parts):
                rows = slice(r * th, (r + 1) * th)
                yv = y_ref[rows, :]
                s = _rms_scale(yv)
                n = yv * s
                dn = 0.5 * dxo_ref[rows, :]
                dgpost = dgpost + jnp.sum(dn * n, axis=0, keepdims=True)
                dyv = _rms_bwd(dn, n, s, gpost_ref[...]).astype(BF16)
                dy_scr[rows, :] = dyv
                dyb_ref[rows, :] = dyv
                xv = x_ref[rows, :]
                hb_ref[rows, :] = (xv * _rms_scale(xv) * gpre_ref[...]).astype(BF16)
            dgpost_ref[...] = dgpost
            acc_scr[...] = jnp.zeros_like(acc_scr)

        das = [_mm_nt(dy_scr[r * th:(r + 1) * th, :], wd_ref[...]) for r in range(parts)]
        for r in range(parts):
            rows = slice(r * th, (r + 1) * th)
            dgate = (das[r] * dgf_ref[rows, :].astype(F32)).astype(BF16)
            dup = (das[r] * silu_ref[rows, :].astype(F32)).astype(BF16)
            dgb_ref[rows, :] = dgate
            dub_ref[rows, :] = dup
            acc_scr[rows, :] += _mm_nt(dgate, wg_ref[...]) + _mm_nt(dup, wu_ref[...])

        @pl.when(j == ns - 1)
        def _():
            dgpre = jnp.zeros((1, D), F32)
            for r in range(parts):
                rows = slice(r * th, (r + 1) * th)
                xv = x_ref[rows, :]
                rs = _rms_scale(xv)
                xhat = xv * rs
                dh = acc_scr[rows, :]
                dgpre = dgpre + jnp.sum(dh * xhat, axis=0, keepdims=True)
                dx_ref[rows, :] = _rms_bwd(dh, xhat, rs, gpre_ref[...]) + dxo_ref[rows, :]
            dgpre_ref[...] = dgpre

    row = pl.BlockSpec((tm, D), lambda i, j: (i, 0), pipeline_mode=pl.Buffered(1))
    vec = pl.BlockSpec((1, D), lambda i, j: (0, 0))
    wide = pl.BlockSpec((tm, fs), lambda i, j: (i, j))
    part = pl.BlockSpec((None, 1, D), lambda i, j: (i, 0, 0))
    F = ns * fs
    return pl.pallas_call(
        body, name=name, grid=(n_i, ns),
        in_specs=[row, row, row, wide, wide, vec,
                  pl.BlockSpec((None, D, fs), lambda i, j: (j, 0, 0)),
                  pl.BlockSpec((None, D, fs), lambda i, j: (j, 0, 0)),
                  pl.BlockSpec((None, fs, D), lambda i, j: (j, 0, 0)),
                  vec, ANY],
        out_specs=[row, row, row, wide, wide, part, part],
        out_shape=[jax.ShapeDtypeStruct((T, D), F32), jax.ShapeDtypeStruct((T, D), BF16),
                   jax.ShapeDtypeStruct((T, D), BF16), jax.ShapeDtypeStruct((T, F), BF16),
                   jax.ShapeDtypeStruct((T, F), BF16),
                   jax.ShapeDtypeStruct((n_i, 1, D), F32), jax.ShapeDtypeStruct((n_i, 1, D), F32)],
        scratch_shapes=[pltpu.VMEM((tm, D), BF16), pltpu.VMEM((tm, D), F32)],
        compiler_params=_params(("parallel", "arbitrary")),
    )(dxo, x, y, dgf, silu, g_pre, wg, wu, wd, g_post, _after(dep))


def _wgrad(xm, ym, name, shard_cols=False, dep=None):
    T, M = xm.shape
    _, N = ym.shape
    assert M * N * 4 <= 16 * 1024 * 1024, (M, N)
    tk = _blk(T, 512)
    n_k = T // tk
    fs = N // N_DEV

    def body(x_ref, y_ref, _, o_ref, acc_scr):
        k = pl.program_id(0)

        @pl.when(k == 0)
        def _():
            acc_scr[...] = jnp.zeros_like(acc_scr)

        acc_scr[...] += _mm_tn(x_ref[...], y_ref[...])

        @pl.when(k == n_k - 1)
        def _():
            if shard_cols:
                for s in range(N_DEV):
                    o_ref[s] = acc_scr[:, s * fs:(s + 1) * fs].astype(BF16)
            else:
                o_ref[...] = acc_scr[...].astype(BF16)

    if shard_cols:
        out_spec = pl.BlockSpec((N_DEV, M, fs), lambda k: (0, 0, 0), pipeline_mode=pl.Buffered(1))
        out_shape = jax.ShapeDtypeStruct((N_DEV, M, fs), BF16)
    else:
        out_spec = pl.BlockSpec((M, N), lambda k: (0, 0), pipeline_mode=pl.Buffered(1))
        out_shape = jax.ShapeDtypeStruct((M, N), BF16)
    return pl.pallas_call(
        body, name=name, grid=(n_k,),
        in_specs=[pl.BlockSpec((tk, M), lambda k: (k, 0)), pl.BlockSpec((tk, N), lambda k: (k, 0)), ANY],
        out_specs=out_spec, out_shape=out_shape,
        scratch_shapes=[pltpu.VMEM((M, N), F32)],
        compiler_params=_params(("arbitrary",)),
    )(xm, ym, _after(dep))


def _mix_in_fwd(x1, g, w7, wf, name):
    T, D = x1.shape
    n_seg, _, W = w7.shape
    tm = _blk(T, 1024)

    def body(x_ref, g_ref, w_ref, wf_ref, z_ref, f_ref, hb_ref, h_scr):
        s = pl.program_id(1)

        @pl.when(s == 0)
        def _():
            xv = x_ref[...]
            h = (xv * _rms_scale(xv) * g_ref[...]).astype(BF16)
            h_scr[...] = h
            hb_ref[...] = h
            f_ref[...] = _mm(h, wf_ref[...])

        z_ref[...] = _mm(h_scr[...], w_ref[...]).astype(BF16)

    return pl.pallas_call(
        body, name=name, grid=(T // tm, n_seg),
        in_specs=[pl.BlockSpec((tm, D), lambda i, s: (i, 0)),
                  pl.BlockSpec((1, D), lambda i, s: (0, 0)),
                  pl.BlockSpec((None, D, W), lambda i, s: (s, 0, 0)),
                  pl.BlockSpec((D, LANES), lambda i, s: (0, 0))],
        out_specs=[pl.BlockSpec((None, tm, W), lambda i, s: (s, i, 0)),
                   pl.BlockSpec((tm, LANES), lambda i, s: (i, 0)),
                   pl.BlockSpec((tm, D), lambda i, s: (i, 0))],
        out_shape=[jax.ShapeDtypeStruct((n_seg, T, W), BF16), jax.ShapeDtypeStruct((T, LANES), F32),
                   jax.ShapeDtypeStruct((T, D), BF16)],
        scratch_shapes=[pltpu.VMEM((tm, D), BF16)],
        compiler_params=_params(("parallel", "arbitrary")),
    )(x1, g, w7, wf)


def _mix_in_bwd(dx2, x1, g, segs, dfb, w7, wf, name, dep=None):
    T, D = x1.shape
    n_seg, _, W = w7.shape
    tm = _blk(T, 512)
    n_i = T // tm

    def body(*refs):
        dx2_ref, x_ref, g_ref = refs[:3]
        seg_refs = refs[3:3 + n_seg]
        df_ref, w_ref, wf_ref, _, dx1_ref, dg_ref, acc_scr = refs[3 + n_seg:]
        s = pl.program_id(1)

        @pl.when(s == 0)
        def _():
            acc_scr[...] = _mm_nt(df_ref[...], wf_ref[...])

        for q in range(n_seg):
            @pl.when(s == q)
            def _(q=q):
                acc_scr[...] += _mm_nt(seg_refs[q][...], w_ref[...])

        @pl.when(s == n_seg - 1)
        def _():
            xv = x_ref[...]
            r = _rms_scale(xv)
            xhat = xv * r
            dh = acc_scr[...]
            dg_ref[...] = jnp.sum(dh * xhat, axis=0, keepdims=True)
            dx1_ref[...] = _rms_bwd(dh, xhat, r, g_ref[...]) + dx2_ref[...]

    row = pl.BlockSpec((tm, D), lambda i, s: (i, 0))
    seg_specs = []
    seg_args = []
    for arr, idx in segs:
        if idx is None:
            seg_specs.append(pl.BlockSpec((tm, W), lambda i, s: (i, 0)))
        else:
            seg_specs.append(pl.BlockSpec((None, tm, W), lambda i, s, idx=idx: (idx, i, 0)))
        seg_args.append(arr)
    return pl.pallas_call(
        body, name=name, grid=(n_i, n_seg),
        in_specs=[row, row, pl.BlockSpec((1, D), lambda i, s: (0, 0))] + seg_specs + [
            pl.BlockSpec((tm, LANES), lambda i, s: (i, 0)),
            pl.BlockSpec((None, D, W), lambda i, s: (s, 0, 0)),
            pl.BlockSpec((D, LANES), lambda i, s: (0, 0)), ANY],
        out_specs=[row, pl.BlockSpec((None, 1, D), lambda i, s: (i, 0, 0))],
        out_shape=[jax.ShapeDtypeStruct((T, D), F32), jax.ShapeDtypeStruct((n_i, 1, D), F32)],
        scratch_shapes=[pltpu.VMEM((tm, D), F32)],
        compiler_params=_params(("parallel", "arbitrary")),
    )(dx2, x1, g, *seg_args, dfb, w7, wf, _after(dep))


def _forget_cumsum(f, b_pad, name):
    T, L = f.shape
    tb = _blk(T, 256)

    def body(f_ref, b_ref, c_ref, carry):
        @pl.when(pl.program_id(0) == 0)
        def _():
            carry[...] = jnp.zeros_like(carry)

        lf = jax.nn.log_sigmoid(f_ref[...] + b_ref[...])
        rows = lax.broadcasted_iota(jnp.int32, (tb, tb), 0)
        cols = lax.broadcasted_iota(jnp.int32, (tb, tb), 1)
        tri = (cols <= rows).astype(F32)
        c = jnp.dot(tri, lf, preferred_element_type=F32, precision=lax.Precision.HIGHEST) + carry[...]
        c_ref[...] = c
        carry[...] = c[tb - 1:tb, :]

    return pl.pallas_call(
        body, name=name, grid=(T // tb,),
        in_specs=[pl.BlockSpec((tb, L), lambda i: (i, 0)), pl.BlockSpec((1, L), lambda i: (0, 0))],
        out_specs=pl.BlockSpec((tb, L), lambda i: (i, 0)),
        out_shape=jax.ShapeDtypeStruct((T, L), F32),
        scratch_shapes=[pltpu.VMEM((1, L), F32)],
        compiler_params=_params(("arbitrary",)),
    )(f, b_pad)


def _forget_bwd(dc, f, b_pad, name):
    T, L = f.shape
    tb = _blk(T, 256)
    nb = T // tb

    def body(dc_ref, f_ref, b_ref, df_ref, db_ref, carry):
        @pl.when(pl.program_id(0) == 0)
        def _():
            carry[...] = jnp.zeros_like(carry)
            db_ref[...] = jnp.zeros_like(db_ref)

        rows = lax.broadcasted_iota(jnp.int32, (tb, tb), 0)
        cols = lax.broadcasted_iota(jnp.int32, (tb, tb), 1)
        tri = (cols >= rows).astype(F32)
        r = jnp.dot(tri, dc_ref[...], preferred_element_type=F32, precision=lax.Precision.HIGHEST) + carry[...]
        carry[...] = r[0:1, :]
        df = r * (1.0 - jax.nn.sigmoid(f_ref[...] + b_ref[...]))
        df_ref[...] = df.astype(BF16)
        db_ref[...] += jnp.sum(df, axis=0, keepdims=True)

    rev = pl.BlockSpec((tb, L), lambda i: (nb - 1 - i, 0))
    one = pl.BlockSpec((1, L), lambda i: (0, 0))
    return pl.pallas_call(
        body, name=name, grid=(nb,),
        in_specs=[rev, rev, one], out_specs=[rev, one],
        out_shape=[jax.ShapeDtypeStruct((T, L), BF16), jax.ShapeDtypeStruct((1, L), F32)],
        scratch_shapes=[pltpu.VMEM((1, L), F32)],
        compiler_params=_params(("arbitrary",)),
    )(dc, f, b_pad)


def _attn_fwd(z7, c_row, name):
    _, T, W = z7.shape
    H = W // HEAD_DIM
    ta = _blk(T, 512)
    nq = T // ta
    scale = np.float32(1.0 / np.sqrt(HEAD_DIM))

    def body(q_ref, k_ref, v_ref, crow_ref, o_ref, lse_ref, m_scr, l_scr, acc_scr):
        i = pl.program_id(1)
        j = pl.program_id(2)

        @pl.when(j == 0)
        def _():
            m_scr[...] = jnp.full_like(m_scr, NEG_BIG)
            l_scr[...] = jnp.zeros_like(l_scr)
            acc_scr[...] = jnp.zeros_like(acc_scr)

        def step(diagonal):
            s = _mm_nt(q_ref[...], k_ref[...]) * scale - crow_ref[...]
            if diagonal:
                rows = lax.broadcasted_iota(jnp.int32, (ta, ta), 0)
                cols = lax.broadcasted_iota(jnp.int32, (ta, ta), 1)
                s = jnp.where(cols <= rows, s, NEG_BIG)
            m_prev = m_scr[...]
            m_new = jnp.maximum(m_prev, jnp.max(s, axis=-1, keepdims=True))
            alpha = jnp.exp(m_prev - m_new)
            p = jnp.exp(s - m_new)
            l_scr[...] = alpha * l_scr[...] + jnp.sum(p, axis=-1, keepdims=True)
            acc_scr[...] = alpha * acc_scr[...] + _mm(p.astype(BF16), v_ref[...])
            m_scr[...] = m_new

        @pl.when(j < i)
        def _():
            step(False)

        @pl.when(j == i)
        def _():
            step(True)
            l = l_scr[...]
            o_ref[...] = acc_scr[...] / l
            lse_ref[...] = m_scr[...] + jnp.log(l)

    return pl.pallas_call(
        body, name=name, grid=(H, nq, nq),
        in_specs=[pl.BlockSpec((None, ta, HEAD_DIM), lambda h, i, j: (0, i, h)),
                  pl.BlockSpec((None, ta, HEAD_DIM), lambda h, i, j: (1, jnp.minimum(i, j), h)),
                  pl.BlockSpec((None, ta, HEAD_DIM), lambda h, i, j: (2, jnp.minimum(i, j), h)),
                  pl.BlockSpec((None, 1, ta), lambda h, i, j: (h, 0, jnp.minimum(i, j)))],
        out_specs=[pl.BlockSpec((ta, HEAD_DIM), lambda h, i, j: (i, h)),
                   pl.BlockSpec((None, ta, 1), lambda h, i, j: (h, i, 0))],
        out_shape=[jax.ShapeDtypeStruct((T, W), F32), jax.ShapeDtypeStruct((H, T, 1), F32)],
        scratch_shapes=[pltpu.VMEM((ta, 1), F32), pltpu.VMEM((ta, 1), F32), pltpu.VMEM((ta, HEAD_DIM), F32)],
        compiler_params=_params(("parallel", "parallel", "arbitrary")),
    )(z7, z7, z7, c_row)


def _attn_bwd_kv(z7, dob, c_col, lse_row, d_row, name):
    _, T, W = z7.shape
    H = W // HEAD_DIM
    ta = _blk(T, 512)
    nq = T // ta
    scale = np.float32(1.0 / np.sqrt(HEAD_DIM))

    def body(k_ref, v_ref, q_ref, do_ref, ccol_ref, lse_ref, d_ref, dk_ref, dv_ref, dc_ref, dk_scr, dv_scr, dc_scr):
        j = pl.program_id(1)
        i = pl.program_id(2)

        @pl.when(i == 0)
        def _():
            dk_scr[...] = jnp.zeros_like(dk_scr)
            dv_scr[...] = jnp.zeros_like(dv_scr)
            dc_scr[...] = jnp.zeros_like(dc_scr)

        def step(diagonal):
            q = q_ref[...]
            do = do_ref[...]
            st = _mm_nt(k_ref[...], q) * scale - ccol_ref[...] - lse_ref[...]
            if diagonal:
                rows = lax.broadcasted_iota(jnp.int32, (ta, ta), 0)
                cols = lax.broadcasted_iota(jnp.int32, (ta, ta), 1)
                st = jnp.where(rows <= cols, st, NEG_BIG)
            pt = jnp.exp(st)
            dv_scr[...] += _mm(pt.astype(BF16), do)
            dst = pt * (_mm_nt(v_ref[...], do) - d_ref[...])
            dk_scr[...] += _mm(dst.astype(BF16), q)
            dc_scr[...] += jnp.sum(dst, axis=-1, keepdims=True)

        @pl.when(i > j)
        def _():
            step(False)

        @pl.when(i == j)
        def _():
            step(True)

        @pl.when(i == nq - 1)
        def _():
            dk_ref[...] = (dk_scr[...] * scale).astype(BF16)
            dv_ref[...] = dv_scr[...].astype(BF16)
            dc_ref[...] = -dc_scr[...]

    return pl.pallas_call(
        body, name=name, grid=(H, nq, nq),
        in_specs=[pl.BlockSpec((None, ta, HEAD_DIM), lambda h, j, i: (1, j, h)),
                  pl.BlockSpec((None, ta, HEAD_DIM), lambda h, j, i: (2, j, h)),
                  pl.BlockSpec((None, ta, HEAD_DIM), lambda h, j, i: (0, jnp.maximum(i, j), h)),
                  pl.BlockSpec((ta, HEAD_DIM), lambda h, j, i: (jnp.maximum(i, j), h)),
                  pl.BlockSpec((None, ta, 1), lambda h, j, i: (h, j, 0)),
                  pl.BlockSpec((None, 1, ta), lambda h, j, i: (h, 0, jnp.maximum(i, j))),
                  pl.BlockSpec((None, 1, ta), lambda h, j, i: (h, 0, jnp.maximum(i, j)))],
        out_specs=[pl.BlockSpec((ta, HEAD_DIM), lambda h, j, i: (j, h)),
                   pl.BlockSpec((ta, HEAD_DIM), lambda h, j, i: (j, h)),
                   pl.BlockSpec((None, ta, 1), lambda h, j, i: (h, j, 0))],
        out_shape=[jax.ShapeDtypeStruct((T, W), BF16), jax.ShapeDtypeStruct((T, W), BF16),
                   jax.ShapeDtypeStruct((H, T, 1), F32)],
        scratch_shapes=[pltpu.VMEM((ta, HEAD_DIM), F32), pltpu.VMEM((ta, HEAD_DIM), F32), pltpu.VMEM((ta, 1), F32)],
        compiler_params=_params(("parallel", "parallel", "arbitrary")),
    )(z7, z7, z7, dob, c_col, lse_row, d_row)


def _attn_bwd_q(z7, dob, c_row, lse_col, d_col, name):
    _, T, W = z7.shape
    H = W // HEAD_DIM
    ta = _blk(T, 512)
    nq = T // ta
    scale = np.float32(1.0 / np.sqrt(HEAD_DIM))

    def body(q_ref, k_ref, v_ref, do_ref, crow_ref, lse_ref, d_ref, dq_ref, dc_ref, dq_scr, dc_scr):
        i = pl.program_id(1)
        j = pl.program_id(2)

        @pl.when(j == 0)
        def _():
            dq_scr[...] = jnp.zeros_like(dq_scr)
            dc_scr[...] = jnp.zeros_like(dc_scr)

        def step(diagonal):
            k = k_ref[...]
            do = do_ref[...]
            s = _mm_nt(q_ref[...], k) * scale - crow_ref[...] - lse_ref[...]
            if diagonal:
                rows = lax.broadcasted_iota(jnp.int32, (ta, ta), 0)
                cols = lax.broadcasted_iota(jnp.int32, (ta, ta), 1)
                s = jnp.where(cols <= rows, s, NEG_BIG)
            p = jnp.exp(s)
            ds = p * (_mm_nt(do, v_ref[...]) - d_ref[...])
            dq_scr[...] += _mm(ds.astype(BF16), k)
            dc_scr[...] += jnp.sum(ds, axis=-1, keepdims=True)

        @pl.when(j < i)
        def _():
            step(False)

        @pl.when(j == i)
        def _():
            step(True)
            dq_ref[...] = (dq_scr[...] * scale).astype(BF16)
            dc_ref[...] = dc_scr[...]

    return pl.pallas_call(
        body, name=name, grid=(H, nq, nq),
        in_specs=[pl.BlockSpec((None, ta, HEAD_DIM), lambda h, i, j: (0, i, h)),
                  pl.BlockSpec((None, ta, HEAD_DIM), lambda h, i, j: (1, jnp.minimum(i, j), h)),
                  pl.BlockSpec((None, ta, HEAD_DIM), lambda h, i, j: (2, jnp.minimum(i, j), h)),
                  pl.BlockSpec((ta, HEAD_DIM), lambda h, i, j: (i, h)),
                  pl.BlockSpec((None, 1, ta), lambda h, i, j: (h, 0, jnp.minimum(i, j))),
                  pl.BlockSpec((None, ta, 1), lambda h, i, j: (h, i, 0)),
                  pl.BlockSpec((None, ta, 1), lambda h, i, j: (h, i, 0))],
        out_specs=[pl.BlockSpec((ta, HEAD_DIM), lambda h, i, j: (i, h)),
                   pl.BlockSpec((None, ta, 1), lambda h, i, j: (h, i, 0))],
        out_shape=[jax.ShapeDtypeStruct((T, W), BF16), jax.ShapeDtypeStruct((H, T, 1), F32)],
        scratch_shapes=[pltpu.VMEM((ta, HEAD_DIM), F32), pltpu.VMEM((ta, 1), F32)],
        compiler_params=_params(("parallel", "parallel", "arbitrary")),
    )(z7, z7, z7, dob, c_row, lse_col, d_col)


ATTN_TILE = 512
ATTN_CHAINS = 2


def _attn_geometry(T):
    ta = _blk(T, ATTN_TILE)
    nc = ATTN_CHAINS if (T // ta) % ATTN_CHAINS == 0 else 1
    return ta, nc, T // ta


def _causal_tile(ta, keys_on_rows=False):
    rows = lax.broadcasted_iota(jnp.int32, (ta, ta), 0)
    cols = lax.broadcasted_iota(jnp.int32, (ta, ta), 1)
    return rows <= cols if keys_on_rows else cols <= rows


def _chunk(ref, j, ta):
    return ref[pl.ds(pl.multiple_of(j * ta, ta), ta), :]


def _attn_fwd_loop(z7, c_chunks, name):
    _, T, W = z7.shape
    H = W // HEAD_DIM
    ta, nc, n_chunks = _attn_geometry(T)
    scale = np.float32(1.0 / np.sqrt(HEAD_DIM))

    def body(q_ref, k_ref, v_ref, c_ref, o_ref, lse_ref, m_scr, l_scr, acc_scr):
        g = pl.program_id(1)
        m_scr[...] = jnp.full_like(m_scr, NEG_BIG)
        l_scr[...] = jnp.zeros_like(l_scr)
        acc_scr[...] = jnp.zeros_like(acc_scr)

        def update(ch, k, v, crow, diagonal):
            q = q_ref[ch * ta:(ch + 1) * ta, :]
            s = _mm_nt(q, k) * scale - crow
            if diagonal:
                s = jnp.where(_causal_tile(ta), s, NEG_BIG)
            m_prev = m_scr[ch]
            m_new = jnp.maximum(m_prev, jnp.max(s, axis=-1, keepdims=True))
            alpha = jnp.exp(m_prev - m_new)
            p = jnp.exp(s - m_new)
            l_scr[ch] = alpha * l_scr[ch] + jnp.sum(p, axis=-1, keepdims=True)
            acc_scr[ch] = alpha * acc_scr[ch] + _mm(p.astype(BF16), v)
            m_scr[ch] = m_new

        def full_chunk(j, carry):
            k = _chunk(k_ref, j, ta)
            v = _chunk(v_ref, j, ta)
            crow = c_ref[j]
            for ch in range(nc):
                update(ch, k, v, crow, False)
            return carry

        lax.fori_loop(0, nc * g, full_chunk, 0)
        for jj in range(nc):
            j = nc * g + jj
            k = _chunk(k_ref, j, ta)
            v = _chunk(v_ref, j, ta)
            crow = c_ref[j]
            for ch in range(jj, nc):
                update(ch, k, v, crow, ch == jj)
        for ch in range(nc):
            l = l_scr[ch]
            o_ref[ch * ta:(ch + 1) * ta, :] = acc_scr[ch] / l
            lse_ref[ch * ta:(ch + 1) * ta, :] = m_scr[ch] + jnp.log(l)

    tq = nc * ta
    return pl.pallas_call(
        body, name=name, grid=(H, n_chunks // nc),
        in_specs=[pl.BlockSpec((None, tq, HEAD_DIM), lambda h, g: (0, g, h)),
                  pl.BlockSpec((None, T, HEAD_DIM), lambda h, g: (1, 0, h)),
                  pl.BlockSpec((None, T, HEAD_DIM), lambda h, g: (2, 0, h)),
                  pl.BlockSpec((None, n_chunks, 1, ta), lambda h, g: (h, 0, 0, 0))],
        out_specs=[pl.BlockSpec((tq, HEAD_DIM), lambda h, g: (g, h)),
                   pl.BlockSpec((None, tq, 1), lambda h, g: (h, g, 0))],
        out_shape=[jax.ShapeDtypeStruct((T, W), F32), jax.ShapeDtypeStruct((H, T, 1), F32)],
        scratch_shapes=[pltpu.VMEM((nc, ta, 1), F32), pltpu.VMEM((nc, ta, 1), F32),
                        pltpu.VMEM((nc, ta, HEAD_DIM), F32)],
        compiler_params=_params(("parallel", "arbitrary")),
    )(z7, z7, z7, c_chunks)


def _attn_fwd_keys_on_rows(z7, vt, c_rep, name):
    _, T, W = z7.shape
    H = W // HEAD_DIM
    ta, nc, n_chunks = _attn_geometry(T)
    scale = np.float32(1.0 / np.sqrt(HEAD_DIM))
    reps = ta // LANES

    def body(q_ref, k_ref, vt_ref, c_ref, o_ref, lse_ref, m_scr, l_scr, acc_scr):
        g = pl.program_id(1)
        m_scr[...] = jnp.full_like(m_scr, NEG_BIG)
        l_scr[...] = jnp.zeros_like(l_scr)
        acc_scr[...] = jnp.zeros_like(acc_scr)

        def update(ch, k, vt, cj, diagonal):
            q = q_ref[ch * ta:(ch + 1) * ta, :]
            st = _mm_nt(k, q) * scale - cj
            if diagonal:
                st = jnp.where(_causal_tile(ta, keys_on_rows=True), st, NEG_BIG)
            m_prev = m_scr[ch]
            m_new = jnp.maximum(m_prev, jnp.max(st, axis=0, keepdims=True))
            alpha = jnp.exp(m_prev - m_new)
            pt = jnp.exp(st - m_new)
            l_scr[ch] = alpha * l_scr[ch] + jnp.sum(pt, axis=0, keepdims=True)
            acc_scr[ch] = alpha * acc_scr[ch] + _mm(vt, pt.astype(BF16))
            m_scr[ch] = m_new

        def load(j):
            cj = _chunk(c_ref, j, ta)
            return _chunk(k_ref, j, ta), vt_ref[j], jnp.concatenate([cj] * reps, axis=1)

        def full_chunk(j, carry):
            k, vt, cj = load(j)
            for ch in range(nc):
                update(ch, k, vt, cj, False)
            return carry

        lax.fori_loop(0, nc * g, full_chunk, 0)
        for jj in range(nc):
            k, vt, cj = load(nc * g + jj)
            for ch in range(jj, nc):
                update(ch, k, vt, cj, ch == jj)
        for ch in range(nc):
            l = l_scr[ch]
            o_ref[ch * ta:(ch + 1) * ta, :] = (acc_scr[ch] / l).T
            lse_ref[ch] = m_scr[ch] + jnp.log(l)

    tq = nc * ta
    return pl.pallas_call(
        body, name=name, grid=(H, n_chunks // nc),
        in_specs=[pl.BlockSpec((None, tq, HEAD_DIM), lambda h, g: (0, g, h)),
                  pl.BlockSpec((None, T, HEAD_DIM), lambda h, g: (1, 0, h)),
                  pl.BlockSpec((None, n_chunks, HEAD_DIM, ta), lambda h, g: (h, 0, 0, 0)),
                  pl.BlockSpec((None, T, LANES), lambda h, g: (h, 0, 0))],
        out_specs=[pl.BlockSpec((tq, HEAD_DIM), lambda h, g: (g, h)),
                   pl.BlockSpec((None, nc, 1, ta), lambda h, g: (h, g, 0, 0))],
        out_shape=[jax.ShapeDtypeStruct((T, W), F32), jax.ShapeDtypeStruct((H, n_chunks, 1, ta), F32)],
        scratch_shapes=[pltpu.VMEM((nc, 1, ta), F32), pltpu.VMEM((nc, 1, ta), F32),
                        pltpu.VMEM((nc, HEAD_DIM, ta), F32)],
        compiler_params=_params(("parallel", "arbitrary")),
    )(z7, z7, vt, c_rep)


def _attn_bwd_fused(z7, kt, dob, c_rep, lse_chunks, d_chunks, name):
    _, T, W = z7.shape
    H = W // HEAD_DIM
    ta, nc, n_chunks = _attn_geometry(T)
    n_steps = n_chunks // nc
    scale = np.float32(1.0 / np.sqrt(HEAD_DIM))
    reps = ta // LANES

    def body(k_ref, v_ref, kt_ref, q_ref, do_ref, c_ref, lse_ref, d_ref,
             dk_ref, dv_ref, dck_ref, dq_ref, dcq_ref, dk_scr, dv_scr, dck_scr, dqt_scr, dcq_scr):
        g = pl.program_id(1)

        @pl.when(g == 0)
        def _():
            dqt_scr[...] = jnp.zeros_like(dqt_scr)
            dcq_scr[...] = jnp.zeros_like(dcq_scr)

        dk_scr[...] = jnp.zeros_like(dk_scr)
        dv_scr[...] = jnp.zeros_like(dv_scr)
        dck_scr[...] = jnp.zeros_like(dck_scr)

        def update(ch, i, q, do, diagonal):
            rows = slice(ch * ta, (ch + 1) * ta)
            cj = c_ref[rows, :]
            st = _mm_nt(k_ref[rows, :], q) * scale - jnp.concatenate([cj] * reps, axis=1) - lse_ref[i]
            if diagonal:
                st = jnp.where(_causal_tile(ta, keys_on_rows=True), st, NEG_BIG)
            pt = jnp.exp(st)
            dv_scr[ch] += _mm(pt.astype(BF16), do)
            dst = pt * (_mm_nt(v_ref[rows, :], do) - d_ref[i])
            dst_b = dst.astype(BF16)
            dk_scr[ch] += _mm(dst_b, q)
            dqt_scr[i] += _mm(kt_ref[ch], dst_b)
            dcq_scr[i] += jnp.sum(dst, axis=0, keepdims=True)
            lane_sum = dst[:, :LANES]
            for r in range(1, reps):
                lane_sum = lane_sum + dst[:, r * LANES:(r + 1) * LANES]
            dck_scr[ch] += lane_sum

        for ii in range(nc):
            i = nc * g + ii
            q = _chunk(q_ref, i, ta)
            do = _chunk(do_ref, i, ta)
            for ch in range(0, ii + 1):
                update(ch, i, q, do, ch == ii)

        def full_chunk(i, carry):
            q = _chunk(q_ref, i, ta)
            do = _chunk(do_ref, i, ta)
            for ch in range(nc):
                update(ch, i, q, do, False)
            return carry

        lax.fori_loop(nc * (g + 1), n_chunks, full_chunk, 0)
        for ch in range(nc):
            rows = slice(ch * ta, (ch + 1) * ta)
            dk_ref[rows, :] = (dk_scr[ch] * scale).astype(BF16)
            dv_ref[rows, :] = dv_scr[ch].astype(BF16)
            dck_ref[rows, :] = -jnp.sum(dck_scr[ch], axis=-1, keepdims=True)

        @pl.when(g == n_steps - 1)
        def _():
            for i in range(n_chunks):
                dq_ref[i * ta:(i + 1) * ta, :] = (dqt_scr[i] * scale).T.astype(BF16)
            dcq_ref[...] = dcq_scr[...]

    tk = nc * ta
    chunks = pl.BlockSpec((None, n_chunks, 1, ta), lambda h, g: (h, 0, 0, 0))
    tile = pl.BlockSpec((tk, HEAD_DIM), lambda h, g: (g, h))
    return pl.pallas_call(
        body, name=name, grid=(H, n_steps),
        in_specs=[pl.BlockSpec((None, tk, HEAD_DIM), lambda h, g: (1, g, h)),
                  pl.BlockSpec((None, tk, HEAD_DIM), lambda h, g: (2, g, h)),
                  pl.BlockSpec((None, nc, HEAD_DIM, ta), lambda h, g: (h, g, 0, 0)),
                  pl.BlockSpec((None, T, HEAD_DIM), lambda h, g: (0, 0, h)),
                  pl.BlockSpec((T, HEAD_DIM), lambda h, g: (0, h)),
                  pl.BlockSpec((None, tk, LANES), lambda h, g: (h, g, 0)),
                  chunks, chunks],
        out_specs=[tile, tile, pl.BlockSpec((None, tk, 1), lambda h, g: (h, g, 0)),
                   pl.BlockSpec((T, HEAD_DIM), lambda h, g: (0, h)), chunks],
        out_shape=[jax.ShapeDtypeStruct((T, W), BF16), jax.ShapeDtypeStruct((T, W), BF16),
                   jax.ShapeDtypeStruct((H, T, 1), F32), jax.ShapeDtypeStruct((T, W), BF16),
                   jax.ShapeDtypeStruct((H, n_chunks, 1, ta), F32)],
        scratch_shapes=[pltpu.VMEM((nc, ta, HEAD_DIM), F32), pltpu.VMEM((nc, ta, HEAD_DIM), F32),
                        pltpu.VMEM((nc, ta, LANES), F32), pltpu.VMEM((n_chunks, HEAD_DIM, ta), F32),
                        pltpu.VMEM((n_chunks, 1, ta), F32)],
        compiler_params=_params(("parallel", "arbitrary")),
    )(z7, z7, kt, z7, dob, c_rep, lse_chunks, d_chunks)


def _attn_bwd_q_loop(z7, dob, c_chunks, lse_col, d_col, name):
    _, T, W = z7.shape
    H = W // HEAD_DIM
    ta, nc, n_chunks = _attn_geometry(T)
    scale = np.float32(1.0 / np.sqrt(HEAD_DIM))

    def body(q_ref, k_ref, v_ref, do_ref, c_ref, lse_ref, d_ref, dq_ref, dc_ref, dq_scr, dc_scr):
        g = pl.program_id(1)
        dq_scr[...] = jnp.zeros_like(dq_scr)
        dc_scr[...] = jnp.zeros_like(dc_scr)

        def update(ch, k, v, crow, diagonal):
            rows = slice(ch * ta, (ch + 1) * ta)
            do = do_ref[rows, :]
            s = _mm_nt(q_ref[rows, :], k) * scale - crow - lse_ref[rows, :]
            if diagonal:
                s = jnp.where(_causal_tile(ta), s, NEG_BIG)
            p = jnp.exp(s)
            ds = p * (_mm_nt(do, v) - d_ref[rows, :])
            dq_scr[ch] += _mm(ds.astype(BF16), k)
            dc_scr[ch] += jnp.sum(ds, axis=-1, keepdims=True)

        def full_chunk(j, carry):
            k = _chunk(k_ref, j, ta)
            v = _chunk(v_ref, j, ta)
            crow = c_ref[j]
            for ch in range(nc):
                update(ch, k, v, crow, False)
            return carry

        lax.fori_loop(0, nc * g, full_chunk, 0)
        for jj in range(nc):
            j = nc * g + jj
            k = _chunk(k_ref, j, ta)
            v = _chunk(v_ref, j, ta)
            crow = c_ref[j]
            for ch in range(jj, nc):
                update(ch, k, v, crow, ch == jj)
        for ch in range(nc):
            dq_ref[ch * ta:(ch + 1) * ta, :] = (dq_scr[ch] * scale).astype(BF16)
            dc_ref[ch * ta:(ch + 1) * ta, :] = dc_scr[ch]

    tq = nc * ta
    col = pl.BlockSpec((None, tq, 1), lambda h, g: (h, g, 0))
    return pl.pallas_call(
        body, name=name, grid=(H, n_chunks // nc),
        in_specs=[pl.BlockSpec((None, tq, HEAD_DIM), lambda h, g: (0, g, h)),
                  pl.BlockSpec((None, T, HEAD_DIM), lambda h, g: (1, 0, h)),
                  pl.BlockSpec((None, T, HEAD_DIM), lambda h, g: (2, 0, h)),
                  pl.BlockSpec((tq, HEAD_DIM), lambda h, g: (g, h)),
                  pl.BlockSpec((None, n_chunks, 1, ta), lambda h, g: (h, 0, 0, 0)),
                  col, col],
        out_specs=[pl.BlockSpec((tq, HEAD_DIM), lambda h, g: (g, h)), col],
        out_shape=[jax.ShapeDtypeStruct((T, W), BF16), jax.ShapeDtypeStruct((H, T, 1), F32)],
        scratch_shapes=[pltpu.VMEM((nc, ta, HEAD_DIM), F32), pltpu.VMEM((nc, ta, 1), F32)],
        compiler_params=_params(("parallel", "arbitrary")),
    )(z7, z7, z7, dob, c_chunks, lse_col, d_col)


def _attn_bwd_kv_loop(z7, dob, c_col, lse_chunks, d_chunks, name):
    _, T, W = z7.shape
    H = W // HEAD_DIM
    ta, nc, n_chunks = _attn_geometry(T)
    scale = np.float32(1.0 / np.sqrt(HEAD_DIM))

    def body(k_ref, v_ref, q_ref, do_ref, ccol_ref, lse_ref, d_ref, dk_ref, dv_ref, dc_ref, dk_scr, dv_scr, dc_scr):
        g = pl.program_id(1)
        dk_scr[...] = jnp.zeros_like(dk_scr)
        dv_scr[...] = jnp.zeros_like(dv_scr)
        dc_scr[...] = jnp.zeros_like(dc_scr)

        def update(ch, q, do, lse_row, d_row, diagonal):
            rows = slice(ch * ta, (ch + 1) * ta)
            st = _mm_nt(k_ref[rows, :], q) * scale - ccol_ref[rows, :] - lse_row
            if diagonal:
                st = jnp.where(_causal_tile(ta, keys_on_rows=True), st, NEG_BIG)
            pt = jnp.exp(st)
            dv_scr[ch] += _mm(pt.astype(BF16), do)
            dst = pt * (_mm_nt(v_ref[rows, :], do) - d_row)
            dk_scr[ch] += _mm(dst.astype(BF16), q)
            dc_scr[ch] += jnp.sum(dst, axis=-1, keepdims=True)

        for ii in range(nc):
            i = nc * g + ii
            q = _chunk(q_ref, i, ta)
            do = _chunk(do_ref, i, ta)
            for ch in range(0, ii + 1):
                update(ch, q, do, lse_ref[i], d_ref[i], ch == ii)

        def full_chunk(i, carry):
            q = _chunk(q_ref, i, ta)
            do = _chunk(do_ref, i, ta)
            for ch in range(nc):
                update(ch, q, do, lse_ref[i], d_ref[i], False)
            return carry

        lax.fori_loop(nc * (g + 1), n_chunks, full_chunk, 0)
        for ch in range(nc):
            rows = slice(ch * ta, (ch + 1) * ta)
            dk_ref[rows, :] = (dk_scr[ch] * scale).astype(BF16)
            dv_ref[rows, :] = dv_scr[ch].astype(BF16)
            dc_ref[rows, :] = -dc_scr[ch]

    tk = nc * ta
    chunks = pl.BlockSpec((None, n_chunks, 1, ta), lambda h, g: (h, 0, 0, 0))
    col = pl.BlockSpec((None, tk, 1), lambda h, g: (h, g, 0))
    tile = pl.BlockSpec((tk, HEAD_DIM), lambda h, g: (g, h))
    return pl.pallas_call(
        body, name=name, grid=(H, n_chunks // nc),
        in_specs=[pl.BlockSpec((None, tk, HEAD_DIM), lambda h, g: (1, g, h)),
                  pl.BlockSpec((None, tk, HEAD_DIM), lambda h, g: (2, g, h)),
                  pl.BlockSpec((None, T, HEAD_DIM), lambda h, g: (0, 0, h)),
                  pl.BlockSpec((T, HEAD_DIM), lambda h, g: (0, h)),
                  col, chunks, chunks],
        out_specs=[tile, tile, col],
        out_shape=[jax.ShapeDtypeStruct((T, W), BF16), jax.ShapeDtypeStruct((T, W), BF16),
                   jax.ShapeDtypeStruct((H, T, 1), F32)],
        scratch_shapes=[pltpu.VMEM((nc, ta, HEAD_DIM), F32), pltpu.VMEM((nc, ta, HEAD_DIM), F32),
                        pltpu.VMEM((nc, ta, 1), F32)],
        compiler_params=_params(("parallel", "arbitrary")),
    )(z7, z7, z7, dob, c_col, lse_chunks, d_chunks)


def _chunk_causal_mask():
    rows = lax.broadcasted_iota(jnp.int32, (SGU_LEN, SGU_LEN), 0)
    cols = lax.broadcasted_iota(jnp.int32, (SGU_LEN, SGU_LEN), 1)
    return (cols // CHUNK) <= (rows // CHUNK)


def _sgu_norm_mix(sv, lng_ref, lnb_ref, ws_ref, bs_ref, vn_scr, mixed_scr, vhat_scr=None):
    tm = sv.shape[0]
    vs = _gelu(sv)
    mask = _chunk_causal_mask()
    rstds = []
    for g in range(N_GROUPS):
        lanes = slice(g * GROUP_DIM, (g + 1) * GROUP_DIM)
        blk = vs[:, lanes]
        cen = blk - jnp.mean(blk, axis=-1, keepdims=True)
        rstd = lax.rsqrt(jnp.mean(cen * cen, axis=-1, keepdims=True) + LN_EPS)
        vhat = cen * rstd
        rstds.append(rstd)
        if vhat_scr is not None:
            vhat_scr[:, lanes] = vhat
        vn_scr[:, lanes] = (vhat * lng_ref[:, lanes] + lnb_ref[:, lanes]).astype(BF16)
        wm = jnp.where(mask, ws_ref[g], 0.0).astype(BF16)
        for w in range(tm // SGU_LEN):
            rows = slice(w * SGU_LEN, (w + 1) * SGU_LEN)
            mixed_scr[rows, lanes] = _mm(wm, vn_scr[rows, lanes]) + bs_ref[g]
    return rstds


def _mix_out_fwd(z7, o_a, x1, lng, lnb, ws, bs, w_out, g_post, name):
    _, T, W = z7.shape
    D = x1.shape[1]
    tm = _blk(T, 256)

    def body(u_ref, sv_ref, ga_ref, gb_ref, oa_ref, x1_ref, lng_ref, lnb_ref, ws_ref, bs_ref, wo_ref, gp_ref,
             x2_ref, p_ref, mb_ref, vn_scr, mixed_scr):
        _sgu_norm_mix(sv_ref[...].astype(F32), lng_ref, lnb_ref, ws_ref, bs_ref, vn_scr, mixed_scr)
        o_b = _gelu(u_ref[...].astype(F32)) * mixed_scr[...]
        merged = (jax.nn.sigmoid(ga_ref[...].astype(F32)) * oa_ref[...]
                  + jax.nn.sigmoid(gb_ref[...].astype(F32)) * o_b).astype(BF16)
        mb_ref[...] = merged
        p = _mm(merged, wo_ref[...])
        p_ref[...] = p
        x2_ref[...] = x1_ref[...] + p * _rms_scale(p) * gp_ref[...]

    def seg(idx):
        return pl.BlockSpec((None, tm, W), lambda i, idx=idx: (idx, i, 0))

    row = pl.BlockSpec((tm, D), lambda i: (i, 0))
    vec = pl.BlockSpec((1, D), lambda i: (0, 0))
    return pl.pallas_call(
        body, name=name, grid=(T // tm,),
        in_specs=[seg(3), seg(4), seg(5), seg(6), row, row, vec, vec,
                  pl.BlockSpec((N_GROUPS, SGU_LEN, SGU_LEN), lambda i: (0, 0, 0)),
                  pl.BlockSpec((N_GROUPS, SGU_LEN, 1), lambda i: (0, 0, 0)),
                  pl.BlockSpec((D, D), lambda i: (0, 0)), vec],
        out_specs=[row, row, row],
        out_shape=[jax.ShapeDtypeStruct((T, D), F32), jax.ShapeDtypeStruct((T, D), F32),
                   jax.ShapeDtypeStruct((T, D), BF16)],
        scratch_shapes=[pltpu.VMEM((tm, W), BF16), pltpu.VMEM((tm, W), F32)],
        compiler_params=_params(("parallel",)),
    )(z7, z7, z7, z7, o_a, x1, lng, lnb, ws, bs, w_out, g_post)


def _mix_out_bwd(dx2, p, z7, o_a, lng, lnb, ws, bs, w_out, g_post, name, dep=None):
    _, T, W = z7.shape
    D = dx2.shape[1]
    tm = _blk(T, 256)
    n_w = tm // SGU_LEN

    def body(dx2_ref, p_ref, u_ref, sv_ref, ga_ref, gb_ref, oa_ref, lng_ref, lnb_ref, ws_ref, bs_ref, wo_ref, gp_ref, _,
             dpb_ref, dob_ref, dvec_ref, dz_ref, dgp_ref, dlng_ref, dlnb_ref, dws_ref, dbs_ref,
             vn_scr, mixed_scr, vhat_scr, dmix_scr, dvn_scr):
        @pl.when(pl.program_id(0) == 0)
        def _():
            dgp_ref[...] = jnp.zeros_like(dgp_ref)
            dlng_ref[...] = jnp.zeros_like(dlng_ref)
            dlnb_ref[...] = jnp.zeros_like(dlnb_ref)
            dws_ref[...] = jnp.zeros_like(dws_ref)
            dbs_ref[...] = jnp.zeros_like(dbs_ref)

        pv = p_ref[...]
        s = _rms_scale(pv)
        n = pv * s
        dn = dx2_ref[...]
        dgp_ref[...] += jnp.sum(dn * n, axis=0, keepdims=True)
        dpb = _rms_bwd(dn, n, s, gp_ref[...]).astype(BF16)
        dpb_ref[...] = dpb
        dmerged = _mm_nt(dpb, wo_ref[...])

        sv = sv_ref[...].astype(F32)
        rstds = _sgu_norm_mix(sv, lng_ref, lnb_ref, ws_ref, bs_ref, vn_scr, mixed_scr, vhat_scr)
        u_pre = u_ref[...].astype(F32)
        u = _gelu(u_pre)
        mixed = mixed_scr[...]
        sa = jax.nn.sigmoid(ga_ref[...].astype(F32))
        sb = jax.nn.sigmoid(gb_ref[...].astype(F32))
        oa = oa_ref[...]
        do_a = (dmerged * sa).astype(BF16)
        dob_ref[...] = do_a
        prod = do_a.astype(F32) * oa
        for h in range(N_HEADS):
            dvec_ref[h] = jnp.sum(prod[:, h * HEAD_DIM:(h + 1) * HEAD_DIM], axis=-1, keepdims=True)
        dz_ref[2] = (dmerged * oa * (sa * (1.0 - sa))).astype(BF16)
        dz_ref[3] = (dmerged * (u * mixed) * (sb * (1.0 - sb))).astype(BF16)
        do_b = dmerged * sb
        dz_ref[0] = (do_b * mixed * _gelu_grad(u_pre)).astype(BF16)
        dmix_scr[...] = do_b * u

        mask = _chunk_causal_mask()
        for g in range(N_GROUPS):
            lanes = slice(g * GROUP_DIM, (g + 1) * GROUP_DIM)
            wm = jnp.where(mask, ws_ref[g], 0.0).astype(BF16)
            dws = jnp.zeros((SGU_LEN, SGU_LEN), F32)
            dbs = jnp.zeros((SGU_LEN, 1), F32)
            for w in range(n_w):
                rows = slice(w * SGU_LEN, (w + 1) * SGU_LEN)
                dmix = dmix_scr[rows, lanes]
                dmix_b = dmix.astype(BF16)
                dvn_scr[rows, lanes] = _mm_tn(wm, dmix_b)
                dws = dws + _mm_nt(dmix_b, vn_scr[rows, lanes])
                dbs = dbs + jnp.sum(dmix, axis=-1, keepdims=True)
            dws_ref[g] += jnp.where(mask, dws, 0.0)
            dbs_ref[g] += dbs
            dvn = dvn_scr[:, lanes]
            vhat = vhat_scr[:, lanes]
            dlng_ref[:, lanes] += jnp.sum(dvn * vhat, axis=0, keepdims=True)
            dlnb_ref[:, lanes] += jnp.sum(dvn, axis=0, keepdims=True)
            dvh = dvn * lng_ref[:, lanes]
            dvs = rstds[g] * (dvh - jnp.mean(dvh, axis=-1, keepdims=True)
                              - vhat * jnp.mean(dvh * vhat, axis=-1, keepdims=True))
            dvn_scr[:, lanes] = dvs
        dz_ref[1] = (dvn_scr[...] * _gelu_grad(sv)).astype(BF16)

    def seg(idx):
        return pl.BlockSpec((None, tm, W), lambda i, idx=idx: (idx, i, 0))

    row = pl.BlockSpec((tm, D), lambda i: (i, 0))
    vec = pl.BlockSpec((1, D), lambda i: (0, 0))
    ws_spec = pl.BlockSpec((N_GROUPS, SGU_LEN, SGU_LEN), lambda i: (0, 0, 0))
    bs_spec = pl.BlockSpec((N_GROUPS, SGU_LEN, 1), lambda i: (0, 0, 0))
    return pl.pallas_call(
        body, name=name, grid=(T // tm,),
        in_specs=[row, row, seg(3), seg(4), seg(5), seg(6), row, vec, vec, ws_spec, bs_spec,
                  pl.BlockSpec((D, D), lambda i: (0, 0)), vec, ANY],
        out_specs=[row, row, pl.BlockSpec((N_HEADS, tm, 1), lambda i: (0, i, 0)),
                   pl.BlockSpec((4, tm, W), lambda i: (0, i, 0)), vec, vec, vec, ws_spec, bs_spec],
        out_shape=[jax.ShapeDtypeStruct((T, D), BF16), jax.ShapeDtypeStruct((T, W), BF16),
                   jax.ShapeDtypeStruct((N_HEADS, T, 1), F32), jax.ShapeDtypeStruct((4, T, W), BF16),
                   jax.ShapeDtypeStruct((1, D), F32), jax.ShapeDtypeStruct((1, D), F32),
                   jax.ShapeDtypeStruct((1, D), F32),
                   jax.ShapeDtypeStruct((N_GROUPS, SGU_LEN, SGU_LEN), F32),
                   jax.ShapeDtypeStruct((N_GROUPS, SGU_LEN, 1), F32)],
        scratch_shapes=[pltpu.VMEM((tm, W), BF16), pltpu.VMEM((tm, W), F32), pltpu.VMEM((tm, W), F32),
                        pltpu.VMEM((tm, W), F32), pltpu.VMEM((tm, W), F32)],
        compiler_params=_params(("arbitrary",)),
    )(dx2, p, z7, z7, z7, z7, o_a, lng, lnb, ws, bs, w_out, g_post, _after(dep))


def _loss_head(y, target, name):
    T, D = y.shape
    tm = _blk(T, 1024)
    n_i = T // tm

    def body(y_ref, t_ref, dy_ref, loss_ref, acc_scr):
        i = pl.program_id(0)

        @pl.when(i == 0)
        def _():
            acc_scr[...] = jnp.zeros_like(acc_scr)

        e = y_ref[...] - t_ref[...]
        dy_ref[...] = e * np.float32(1.0 / D)
        acc_scr[...] += jnp.sum(e * e, axis=0, keepdims=True)

        @pl.when(i == n_i - 1)
        def _():
            total = jnp.sum(acc_scr[...], axis=-1, keepdims=True) * np.float32(0.5 / D)
            loss_ref[...] = jnp.broadcast_to(total, loss_ref.shape)

    row = pl.BlockSpec((tm, D), lambda i: (i, 0))
    return pl.pallas_call(
        body, name=name, grid=(n_i,),
        in_specs=[row, row],
        out_specs=[row, pl.BlockSpec((1, LANES), lambda i: (0, 0))],
        out_shape=[jax.ShapeDtypeStruct((T, D), F32), jax.ShapeDtypeStruct((1, LANES), F32)],
        scratch_shapes=[pltpu.VMEM((1, D), F32)],
        compiler_params=_params(("arbitrary",)),
    )(y, target)


def _adamw_math(w, g, m, v):
    m_new = ADAM_B1 * m + (1.0 - ADAM_B1) * g
    v_new = ADAM_B2 * v + (1.0 - ADAM_B2) * (g * g)
    m_hat = m_new / np.float32(1.0 - ADAM_B1 ** ADAM_STEP)
    v_hat = v_new / np.float32(1.0 - ADAM_B2 ** ADAM_STEP)
    delta = -ADAM_LR * (m_hat / (jnp.sqrt(v_hat) + ADAM_EPS) + ADAM_WD * w)
    return delta, m_new, v_new


def _sum_adamw(parts, w, m, v, name, dep=None):
    n, R, C = parts.shape
    tr = _blk(R, 128)

    def body(p_ref, w_ref, m_ref, v_ref, _, g_ref, d_ref, mo_ref, vo_ref):
        g = p_ref[0].astype(F32)
        for s in range(1, n):
            g = g + p_ref[s].astype(F32)
        delta, m_new, v_new = _adamw_math(w_ref[...], g, m_ref[...], v_ref[...])
        g_ref[...] = g
        d_ref[...] = delta
        mo_ref[...] = m_new
        vo_ref[...] = v_new

    row = pl.BlockSpec((tr, C), lambda i: (i, 0))
    shp = jax.ShapeDtypeStruct((R, C), F32)
    return pl.pallas_call(
        body, name=name, grid=(R // tr,),
        in_specs=[pl.BlockSpec((n, tr, C), lambda i: (0, i, 0)), row, row, row, ANY],
        out_specs=[row, row, row, row], out_shape=[shp, shp, shp, shp],
        compiler_params=_params(("parallel",)),
    )(parts, w, m, v, _after(dep))


def _adamw(g, w, m, v, name):
    R, C = g.shape
    tr = _blk(R, 128)

    def body(g_ref, w_ref, m_ref, v_ref, d_ref, mo_ref, vo_ref):
        delta, m_new, v_new = _adamw_math(w_ref[...], g_ref[...], m_ref[...], v_ref[...])
        d_ref[...] = delta
        mo_ref[...] = m_new
        vo_ref[...] = v_new

    row = pl.BlockSpec((tr, C), lambda i: (i, 0))
    shp = jax.ShapeDtypeStruct((R, C), F32)
    return pl.pallas_call(
        body, name=name, grid=(R // tr,),
        in_specs=[row, row, row, row], out_specs=[row, row, row], out_shape=[shp, shp, shp],
        compiler_params=_params(("parallel",)),
    )(g, w, m, v)


def _position():
    return lax.axis_index("x"), lax.axis_index("y"), lax.axis_index("c")


def _slot(px, py, pc):
    return 4 * px + 2 * py + pc


def _all_gather(shards, name):
    n = len(shards)

    def body(*refs):
        ins, outs = refs[:n], refs[n:2 * n]
        send_sems, recv_sems, local_sems = refs[2 * n:]
        x, y, c = _position()
        me, sibling = (x, y, c), (x, y, 1 - c)
        chips = [(1 - x, y), (x, 1 - y), (1 - x, 1 - y)]

        def copy(a, k, block, to, src=None):
            dst = outs[a].at[_slot(*block)]
            return pltpu.make_async_remote_copy(
                src_ref=dst if src is None else src, dst_ref=dst,
                send_sem=send_sems.at[a, k], recv_sem=recv_sems.at[a, k],
                device_id=to, device_id_type=MESH)

        mine = [pltpu.make_async_copy(ins[a], outs[a].at[_slot(*me)], local_sems.at[a]) for a in range(n)]
        for cp in mine:
            cp.start()
        first = []
        for a in range(n):
            first.append(copy(a, 0, me, sibling, src=ins[a]))
            first += [copy(a, 1 + j, me, (*chip, c), src=ins[a]) for j, chip in enumerate(chips)]
        for cp in first:
            cp.start()
        passed = []
        for j, chip in enumerate(chips):
            for a in range(n):
                copy(a, 1 + j, (*chip, c), me).wait_recv()
                fwd = copy(a, 4 + j, (*chip, c), sibling)
                fwd.start()
                passed.append(fwd)
        for a in range(n):
            copy(a, 0, sibling, me).wait_recv()
            for j, chip in enumerate(chips):
                copy(a, 4 + j, (*chip, 1 - c), me).wait_recv()
        for cp in first + passed:
            cp.wait_send()
        for cp in mine:
            cp.wait()

    return pl.pallas_call(
        body, name=name,
        in_specs=[ANY] * n, out_specs=[ANY] * n,
        out_shape=[jax.ShapeDtypeStruct((N_DEV,) + s.shape, s.dtype) for s in shards],
        scratch_shapes=[pltpu.SemaphoreType.DMA((n, 7)), pltpu.SemaphoreType.DMA((n, 7)),
                        pltpu.SemaphoreType.DMA((n,))],
    )(*shards)


def _peer(x, y, c, k):
    return (1 - x if k & 4 else x, 1 - y if k & 2 else y, 1 - c if k & 1 else c)


def _exchange(parts, name):
    n = len(parts)

    def body(*refs):
        ins, outs = refs[:n], refs[n:2 * n]
        send_sems, recv_sems, local_sems = refs[2 * n:]
        x, y, c = _position()
        me = _slot(x, y, c)
        mine = [pltpu.make_async_copy(ins[a].at[me], outs[a].at[me], local_sems.at[a]) for a in range(n)]
        for cp in mine:
            cp.start()
        sends = []
        for k in range(1, N_DEV):
            to = _peer(x, y, c, k)
            for a in range(n):
                cp = pltpu.make_async_remote_copy(
                    src_ref=ins[a].at[_slot(*to)], dst_ref=outs[a].at[me],
                    send_sem=send_sems.at[a, k - 1], recv_sem=recv_sems.at[a, k - 1],
                    device_id=to, device_id_type=MESH)
                cp.start()
                sends.append(cp)
        for k in range(1, N_DEV):
            frm = _peer(x, y, c, k)
            for a in range(n):
                pltpu.make_async_remote_copy(
                    src_ref=ins[a].at[_slot(*frm)], dst_ref=outs[a].at[_slot(*frm)],
                    send_sem=send_sems.at[a, k - 1], recv_sem=recv_sems.at[a, k - 1],
                    device_id=frm, device_id_type=MESH).wait_recv()
        for cp in sends:
            cp.wait_send()
        for cp in mine:
            cp.wait()

    return pl.pallas_call(
        body, name=name,
        in_specs=[ANY] * n, out_specs=[ANY] * n,
        out_shape=[jax.ShapeDtypeStruct(p.shape, p.dtype) for p in parts],
        scratch_shapes=[pltpu.SemaphoreType.DMA((n, 7)), pltpu.SemaphoreType.DMA((n, 7)),
                        pltpu.SemaphoreType.DMA((n,))],
    )(*parts)


HBM_SPEC = pl.BlockSpec(memory_space=pltpu.HBM)
SEM_SPEC = pl.BlockSpec(memory_space=pltpu.SEMAPHORE)
SIDE_EFFECT = pltpu.SideEffectType.DATAFLOW_SIDE_EFFECTING


def _remote_copies(src_refs, land_refs, send_sems, recv_sems, gather, outgoing):
    x, y, c = _position()
    me = _slot(x, y, c)
    copies = []
    for k in range(1, N_DEV):
        peer = _peer(x, y, c, k)
        for a in range(len(src_refs)):
            src = src_refs[a] if gather else src_refs[a].at[_slot(*peer)]
            dst = land_refs[a].at[me if outgoing else _slot(*peer)]
            sem = a * (N_DEV - 1) + k - 1
            copies.append(pltpu.make_async_remote_copy(
                src_ref=src, dst_ref=dst, send_sem=send_sems.at[sem], recv_sem=recv_sems.at[sem],
                device_id=peer, device_id_type=MESH))
    return copies


def _remote_start(srcs, after, name, gather):
    n = len(srcs)
    lands = [jax.ShapeDtypeStruct(((N_DEV,) + s.shape) if gather else s.shape, s.dtype) for s in srcs]

    def body(*refs):
        src_refs, land_refs = refs[:n], refs[n:2 * n]
        send_sems, recv_sems = refs[2 * n + 1], refs[2 * n + 2]
        token, local_sems = refs[4 * n + 3], refs[4 * n + 4]
        x, y, c = _position()
        me = _slot(x, y, c)
        mine = [pltpu.make_async_copy(src_refs[a] if gather else src_refs[a].at[me], land_refs[a].at[me],
                                      local_sems.at[a]) for a in range(n)]
        for cp in mine:
            cp.start()
        for cp in _remote_copies(src_refs, land_refs, send_sems, recv_sems, gather, outgoing=True):
            cp.start()
        for cp in mine:
            cp.wait()
        token[...] = jnp.zeros_like(token)

    sem_shape = pltpu.SemaphoreType.DMA((n * (N_DEV - 1),))
    outs = pl.pallas_call(
        body, name=name,
        out_shape=(sem_shape, sem_shape, *[pltpu.HBM(s.shape, s.dtype) for s in srcs],
                   *[pltpu.HBM(l.shape, l.dtype) for l in lands], jax.ShapeDtypeStruct((8, LANES), F32)),
        in_specs=[HBM_SPEC] * (2 * n) + [ANY],
        out_specs=(SEM_SPEC, SEM_SPEC, *([HBM_SPEC] * (2 * n)), pl.BlockSpec(memory_space=pltpu.VMEM)),
        input_output_aliases={a: 2 + a for a in range(2 * n)},
        scratch_shapes=[pltpu.SemaphoreType.DMA((n,))],
        compiler_params=pltpu.CompilerParams(has_side_effects=SIDE_EFFECT),
    )(*[pltpu.with_memory_space_constraint(s, pltpu.HBM) for s in srcs],
      *[pltpu.with_memory_space_constraint(lax.empty(l.shape, l.dtype), pltpu.HBM) for l in lands], after)
    return dict(send=outs[0], recv=outs[1], srcs=outs[2:2 + n], lands=outs[2 + n:2 + 2 * n], token=outs[-1],
                gather=gather)


def _remote_wait(flight, after, name):
    n = len(flight["srcs"])
    gather = flight["gather"]

    def body(*refs):
        src_refs, land_refs = refs[:n], refs[n:2 * n]
        send_sems, recv_sems = refs[2 * n], refs[2 * n + 1]
        for cp in _remote_copies(src_refs, land_refs, send_sems, recv_sems, gather, outgoing=False):
            cp.wait_send()
            cp.wait_recv()

    both = list(flight["srcs"]) + list(flight["lands"])
    outs = pl.pallas_call(
        body, name=name,
        out_shape=tuple(pltpu.HBM(a.shape, a.dtype) for a in both),
        in_specs=[HBM_SPEC] * (2 * n) + [SEM_SPEC, SEM_SPEC, ANY],
        out_specs=tuple([HBM_SPEC] * (2 * n)),
        input_output_aliases={a: a for a in range(2 * n)},
        compiler_params=pltpu.CompilerParams(has_side_effects=SIDE_EFFECT),
    )(*both, flight["send"], flight["recv"], after)
    return list(outs[n:])


def _sequencer_exchange(srcs, name, gather, collective_id):
    n = len(srcs)
    hbm = pltpu.MemorySpace.HBM
    src_refs = [jax.new_ref(s, memory_space=hbm) for s in srcs]
    land_refs = [jax.empty_ref(jax.ShapeDtypeStruct(((N_DEV,) + s.shape) if gather else s.shape, s.dtype),
                               memory_space=hbm) for s in srcs]
    n_sems = n * (N_DEV - 1)
    block_bytes = sum(s.size * s.dtype.itemsize // (1 if gather else N_DEV) for s in srcs)
    cost = pl.CostEstimate(flops=0, transcendentals=0, bytes_accessed=2 * N_DEV * block_bytes,
                           remote_bytes_transferred=(N_DEV - 1) * block_bytes)

    @pl.kernel(mesh=plsc.ScalarSubcoreMesh(axis_name="sequencer", num_cores=1), name=name,
               scratch_types=(pltpu.SemaphoreType.DMA((n_sems,)), pltpu.SemaphoreType.DMA((n_sems,)),
                              pltpu.SemaphoreType.DMA((n,))),
               cost_estimate=cost,
               compiler_params=pltpu.CompilerParams(collective_id=collective_id))
    def launch(send_sems, recv_sems, local_sems):
        x, y, c = _position()
        me = _slot(x, y, c)
        barrier = pltpu.get_barrier_semaphore()
        for k in range(1, N_DEV):
            pl.semaphore_signal(barrier, inc=1, device_id=_peer(x, y, c, k), device_id_type=MESH)
        pl.semaphore_wait(barrier, N_DEV - 1)
        mine = [pltpu.make_async_copy(src_refs[a] if gather else src_refs[a].at[me], land_refs[a].at[me],
                                      local_sems.at[a]) for a in range(n)]
        for cp in mine:
            cp.start()
        sends = _remote_copies(src_refs, land_refs, send_sems, recv_sems, gather, outgoing=True)
        for cp in sends:
            cp.start()
        for cp in _remote_copies(src_refs, land_refs, send_sems, recv_sems, gather, outgoing=False):
            cp.wait_recv()
        for cp in sends:
            cp.wait_send()
        for cp in mine:
            cp.wait()

    launch()
    return [r[...] for r in land_refs]


def _all_reduce_small(blob, name):
    R, C = blob.shape

    def body(in_ref, out_ref, gath, send_sems, recv_sems):
        x, y, c = _position()
        me = _slot(x, y, c)
        gath[me] = in_ref[...]
        sends = []
        for k in range(1, N_DEV):
            to = _peer(x, y, c, k)
            cp = pltpu.make_async_remote_copy(
                src_ref=in_ref, dst_ref=gath.at[me],
                send_sem=send_sems.at[k - 1], recv_sem=recv_sems.at[k - 1],
                device_id=to, device_id_type=MESH)
            cp.start()
            sends.append(cp)
        for k in range(1, N_DEV):
            frm = _peer(x, y, c, k)
            pltpu.make_async_remote_copy(
                src_ref=in_ref, dst_ref=gath.at[_slot(*frm)],
                send_sem=send_sems.at[k - 1], recv_sem=recv_sems.at[k - 1],
                device_id=frm, device_id_type=MESH).wait_recv()
        for cp in sends:
            cp.wait_send()
        total = gath[0]
        for s in range(1, N_DEV):
            total = total + gath[s]
        out_ref[...] = total

    return pl.pallas_call(
        body, name=name,
        in_specs=[pl.BlockSpec(memory_space=pltpu.VMEM)],
        out_specs=pl.BlockSpec(memory_space=pltpu.VMEM),
        out_shape=jax.ShapeDtypeStruct((R, C), F32),
        scratch_shapes=[pltpu.VMEM((N_DEV, R, C), F32), pltpu.SemaphoreType.DMA((7,)),
                        pltpu.SemaphoreType.DMA((7,))],
        compiler_params=pltpu.CompilerParams(vmem_limit_bytes=VMEM_LIMIT),
    )(blob)


SMALL_VECS = ("ffn1_pre_g", "ffn1_post_g", "mix_pre_g", "sgu_ln_g", "sgu_ln_b", "mix_post_g", "ffn2_pre_g",
              "ffn2_post_g")
ROW_BS = len(SMALL_VECS)
ROW_BF = ROW_BS + 1
ROW_LOSS = ROW_BF + 1
ROW_WS = 16
BLOB_ROWS = ROW_WS + SGU_LEN


def _pack_small(vals, D, loss_row=None):
    rows = [vals[n].reshape(1, D) for n in SMALL_VECS]
    rows.append(vals["sgu_b_s"].reshape(1, D))
    rows.append(jnp.pad(vals["b_forget"].reshape(1, N_HEADS), ((0, 0), (0, D - N_HEADS))))
    rows.append(jnp.zeros((1, D), F32) if loss_row is None else loss_row)
    rows.append(jnp.zeros((ROW_WS - ROW_LOSS - 1, D), F32))
    rows.append(vals["sgu_w_s"].reshape(SGU_LEN, D))
    return jnp.concatenate(rows, axis=0)


def _unpack_small(blob, D):
    out = {n: blob[r:r + 1] for r, n in enumerate(SMALL_VECS)}
    out["sgu_b_s"] = blob[ROW_BS].reshape(1, N_GROUPS, SGU_LEN)
    out["b_forget"] = blob[ROW_BF, :N_HEADS].reshape(1, N_HEADS)
    out["sgu_w_s"] = blob[ROW_WS:].reshape(1, N_GROUPS, SGU_LEN, SGU_LEN)
    return out


WEIGHT_NAMES = ("ffn1_pre_g", "ffn1_w_gate", "ffn1_w_up", "ffn1_w_down", "ffn1_post_g", "mix_pre_g", "w_in",
                "b_forget", "sgu_ln_g", "sgu_ln_b", "sgu_w_s", "sgu_b_s", "w_out", "mix_post_g", "ffn2_pre_g",
                "ffn2_w_gate", "ffn2_w_up", "ffn2_w_down", "ffn2_post_g")
BIG_NAMES = ("ffn1_w_gate", "ffn1_w_up", "ffn1_w_down", "w_in", "w_out", "ffn2_w_gate", "ffn2_w_up", "ffn2_w_down")
WEIGHT_GROUPS = {"ffn1": ("ffn1_w_gate", "ffn1_w_up", "ffn1_w_down"), "mix": ("w_in", "w_out"),
                 "ffn2": ("ffn2_w_gate", "ffn2_w_up", "ffn2_w_down")}
GRAD_GROUPS = (("ffn2_w_gate", "ffn2_w_up", "ffn2_w_down"), ("w_in", "w_out"), ("ffn1_w_down",), ("ffn1_w_gate",),
               ("ffn1_w_up",))


def _local_step(x, target, small, fetch, emit, consume):
    T, D = x.shape
    W = N_HEADS * HEAD_DIM
    vec = lambda n: small[n].reshape(1, D)
    big = dict(fetch("ffn1", x))

    x1, y1, dgf1, silu1, act1 = _ffn_fwd(x, vec("ffn1_pre_g"), big["ffn1_w_gate"], big["ffn1_w_up"], big["ffn1_w_down"],
                                  vec("ffn1_post_g"), "ffn1_fwd")

    big.update(fetch("mix", x1))
    w_in_all = big["w_in"]
    in_width = N_DEV * w_in_all.shape[2]
    w_in = w_in_all.transpose(1, 0, 2).reshape(D, in_width)
    col_f = 3 * W
    col_u = col_f + N_HEADS
    seg_starts = (0, W, 2 * W, col_u, col_u + W, col_u + 2 * W, col_u + 3 * W)
    w7 = jnp.stack([w_in[:, s:s + W] for s in seg_starts])
    wf = jnp.pad(w_in[:, col_f:col_u], ((0, 0), (0, LANES - N_HEADS)))
    w_out = big["w_out"].reshape(D, D)
    b_pad = jnp.pad(small["b_forget"].reshape(1, N_HEADS), ((0, 0), (0, LANES - N_HEADS)))
    lng, lnb = vec("sgu_ln_g"), vec("sgu_ln_b")
    ws = small["sgu_w_s"].reshape(N_GROUPS, SGU_LEN, SGU_LEN)
    bs = small["sgu_b_s"].reshape(N_GROUPS, SGU_LEN, 1)

    z7, f_logit, h2b = _mix_in_fwd(x1, vec("mix_pre_g"), w7, wf, "mix_in_fwd")
    c = _forget_cumsum(f_logit, b_pad, "forget_cumsum")
    c_heads = c[:, :N_HEADS].T
    ta, _, n_chunks = _attn_geometry(T)
    c_chunks = c_heads.reshape(N_HEADS, n_chunks, 1, ta)
    c_col = c_heads[:, :, None]
    vt = z7[2].reshape(n_chunks, ta, N_HEADS, HEAD_DIM).transpose(2, 0, 3, 1)
    c_rep = jnp.broadcast_to(c_col, (N_HEADS, T, LANES))
    o_a, lse_chunks = _attn_fwd_keys_on_rows(z7, vt, c_rep, "attn_fwd")
    lse = lse_chunks.reshape(N_HEADS, T, 1)
    x2, p, merged_b = _mix_out_fwd(z7, o_a, x1, lng, lnb, ws, bs, w_out, vec("mix_post_g"), "mix_out_fwd")
    big.update(fetch("ffn2", x2))
    x3, y2, dgf2, silu2, act2 = _ffn_fwd(x2, vec("ffn2_pre_g"), big["ffn2_w_gate"], big["ffn2_w_up"], big["ffn2_w_down"],
                                  vec("ffn2_post_g"), "ffn2_fwd")
    dy, loss_lanes = _loss_head(x3, target, "loss_head")

    grads_small = {}

    dx2, h3b, dy2b, dgate2, dup2, dgpre, dgpost = _ffn_bwd(
        dy, x2, y2, dgf2, silu2, vec("ffn2_pre_g"), big["ffn2_w_gate"], big["ffn2_w_up"], big["ffn2_w_down"],
        vec("ffn2_post_g"), "ffn2_bwd")
    grads_small["ffn2_pre_g"] = jnp.sum(dgpre, axis=0)
    grads_small["ffn2_post_g"] = jnp.sum(dgpost, axis=0)
    dep = emit("ffn2_w_gate", _wgrad(h3b, dgate2, "ffn2_wgrad_gate", shard_cols=True))
    dep = emit("ffn2_w_up", _wgrad(h3b, dup2, "ffn2_wgrad_up", shard_cols=True, dep=dep))
    dep = emit("ffn2_w_down", _wgrad(act2, dy2b, "ffn2_wgrad_down", dep=dep).reshape(big["ffn2_w_down"].shape))

    dpb, dob, dvec, dz4, dgp, dlng, dlnb, dws, dbs = _mix_out_bwd(
        dx2, p, z7, o_a, lng, lnb, ws, bs, w_out, vec("mix_post_g"), "mix_out_bwd", dep=dep)
    grads_small["mix_post_g"] = dgp
    grads_small["sgu_ln_g"] = dlng
    grads_small["sgu_ln_b"] = dlnb
    grads_small["sgu_w_s"] = dws
    grads_small["sgu_b_s"] = dbs
    d_chunks = dvec.reshape(N_HEADS, n_chunks, 1, ta)
    kt = z7[1].reshape(n_chunks, ta, N_HEADS, HEAD_DIM).transpose(2, 0, 3, 1)
    dk, dv, dc, dq, dc_q = _attn_bwd_fused(z7, kt, dob, c_rep, lse_chunks, d_chunks, "attn_bwd")
    dc_pad = jnp.pad((dc.reshape(N_HEADS, T) + dc_q.reshape(N_HEADS, T)).T, ((0, 0), (0, LANES - N_HEADS)))
    dfb, dbf = _forget_bwd(dc_pad, f_logit, b_pad, "forget_bwd")
    grads_small["b_forget"] = dbf[:, :N_HEADS]
    segs = [(dq, None), (dk, None), (dv, None), (dz4, 0), (dz4, 1), (dz4, 2), (dz4, 3)]
    dep = consume(("ffn2_w_gate", "ffn2_w_up", "ffn2_w_down"))
    dx1, dgm = _mix_in_bwd(dx2, x1, vec("mix_pre_g"), segs, dfb, w7, wf, "mix_in_bwd", dep=dep)
    grads_small["mix_pre_g"] = jnp.sum(dgm, axis=0)
    seg_mats = [dq, dk, dv, dz4[0], dz4[1], dz4[2], dz4[3]]
    dw_seg, dep = [], dx1
    for q, sm in enumerate(seg_mats):
        dw_seg.append(_wgrad(h2b, sm, "w_in_wgrad_%d" % q, dep=dep))
        dep = dw_seg[-1]
    dwf = _wgrad(h2b, dfb, "w_in_wgrad_f", dep=dep)
    dw_in = jnp.concatenate(dw_seg[:3] + [dwf[:, :N_HEADS]] + dw_seg[3:], axis=1)
    emit("w_in", dw_in.reshape(D, N_DEV, in_width // N_DEV).transpose(1, 0, 2))
    dep = emit("w_out", _wgrad(merged_b, dpb, "w_out_wgrad", dep=dwf).reshape(big["w_out"].shape))

    dx0, h1b, dy1b, dgate1, dup1, dgpre1, dgpost1 = _ffn_bwd(
        dx1, x, y1, dgf1, silu1, vec("ffn1_pre_g"), big["ffn1_w_gate"], big["ffn1_w_up"], big["ffn1_w_down"],
        vec("ffn1_post_g"), "ffn1_bwd", dep=dep)
    grads_small["ffn1_pre_g"] = jnp.sum(dgpre1, axis=0)
    grads_small["ffn1_post_g"] = jnp.sum(dgpost1, axis=0)
    dep = consume(("w_in", "w_out"))
    dep = emit("ffn1_w_down", _wgrad(act1, dy1b, "ffn1_wgrad_down", dep=dep).reshape(big["ffn1_w_down"].shape))
    dep = emit("ffn1_w_gate", _wgrad(h1b, dgate1, "ffn1_wgrad_gate", shard_cols=True, dep=dep))
    dep = emit("ffn1_w_up", _wgrad(h1b, dup1, "ffn1_wgrad_up", shard_cols=True, dep=dep))

    loss_row = jnp.pad(loss_lanes, ((0, 0), (0, D - LANES)))
    return loss_row, dx0, grads_small


def kernel(x, ffn1_pre_g, ffn1_w_gate, ffn1_w_up, ffn1_w_down, ffn1_post_g, mix_pre_g, w_in, b_forget, sgu_ln_g, sgu_ln_b, sgu_w_s, sgu_b_s, w_out, mix_post_g, ffn2_pre_g, ffn2_w_gate, ffn2_w_up, ffn2_w_down, ffn2_post_g, loss_target, m_ffn1_pre_g, m_ffn1_w_gate, m_ffn1_w_up, m_ffn1_w_down, m_ffn1_post_g, m_mix_pre_g, m_w_in, m_b_forget, m_sgu_ln_g, m_sgu_ln_b, m_sgu_w_s, m_sgu_b_s, m_w_out, m_mix_post_g, m_ffn2_pre_g, m_ffn2_w_gate, m_ffn2_w_up, m_ffn2_w_down, m_ffn2_post_g, v_ffn1_pre_g, v_ffn1_w_gate, v_ffn1_w_up, v_ffn1_w_down, v_ffn1_post_g, v_mix_pre_g, v_w_in, v_b_forget, v_sgu_ln_g, v_sgu_ln_b, v_sgu_w_s, v_sgu_b_s, v_w_out, v_mix_post_g, v_ffn2_pre_g, v_ffn2_w_gate, v_ffn2_w_up, v_ffn2_w_down, v_ffn2_post_g):
    weights = dict(zip(WEIGHT_NAMES, (ffn1_pre_g, ffn1_w_gate, ffn1_w_up, ffn1_w_down, ffn1_post_g, mix_pre_g, w_in,
                                      b_forget, sgu_ln_g, sgu_ln_b, sgu_w_s, sgu_b_s, w_out, mix_post_g, ffn2_pre_g,
                                      ffn2_w_gate, ffn2_w_up, ffn2_w_down, ffn2_post_g)))
    mom1 = dict(zip(WEIGHT_NAMES, (m_ffn1_pre_g, m_ffn1_w_gate, m_ffn1_w_up, m_ffn1_w_down, m_ffn1_post_g,
                                   m_mix_pre_g, m_w_in, m_b_forget, m_sgu_ln_g, m_sgu_ln_b, m_sgu_w_s, m_sgu_b_s,
                                   m_w_out, m_mix_post_g, m_ffn2_pre_g, m_ffn2_w_gate, m_ffn2_w_up, m_ffn2_w_down,
                                   m_ffn2_post_g)))
    mom2 = dict(zip(WEIGHT_NAMES, (v_ffn1_pre_g, v_ffn1_w_gate, v_ffn1_w_up, v_ffn1_w_down, v_ffn1_post_g,
                                   v_mix_pre_g, v_w_in, v_b_forget, v_sgu_ln_g, v_sgu_ln_b, v_sgu_w_s, v_sgu_b_s,
                                   v_w_out, v_mix_post_g, v_ffn2_pre_g, v_ffn2_w_gate, v_ffn2_w_up, v_ffn2_w_down,
                                   v_ffn2_post_g)))
    D = x.shape[-1]
    small_names = [n for n in WEIGHT_NAMES if n not in BIG_NAMES]

    small = {n: weights[n] for n in small_names}
    shard = lambda n: weights[n][0].astype(BF16)

    ffn1_full = _all_gather([shard(n) for n in WEIGHT_GROUPS["ffn1"]], "ffn1_all_gather")
    gathered = {}
    for cid, grp in ((1, "mix"), (2, "ffn2")):
        shards, _ = lax.optimization_barrier(([shard(n) for n in WEIGHT_GROUPS[grp]], ffn1_full[0]))
        gathered[grp] = _sequencer_exchange(shards, grp + "_gather", True, cid)

    def fetch(group, after):
        if group == "ffn1":
            return zip(WEIGHT_GROUPS[group], ffn1_full)
        arrived, _ = lax.optimization_barrier((gathered[group], after))
        return zip(WEIGHT_GROUPS[group], arrived)

    ready, received = {}, {}

    def emit(name, part):
        ready[name] = part
        for gi, group in enumerate(GRAD_GROUPS):
            if name == group[-1]:
                lands = _sequencer_exchange([ready[n] for n in group], name + "_grad_exchange", False, 3 + gi)
                received.update(zip(group, lands))
        return part

    out = {}

    def consume(names, dep=None):
        for n in names:
            g, d, m_new, v_new = _sum_adamw(received[n], weights[n][0], mom1[n][0], mom2[n][0], "adamw_" + n, dep=dep)
            out[n] = tuple(a[None] for a in (g, d, m_new, v_new))
            dep = g
        return dep

    loss_row, grad_x, grads_small = _local_step(x[0], loss_target[0], small, fetch, emit, consume)

    blobs = _sequencer_exchange([_pack_small(grads_small, D, loss_row)], "small_gather", True,
                                3 + len(GRAD_GROUPS))[0]
    blob, d_blob, m_blob, v_blob = _sum_adamw(
        blobs, _pack_small(small, D), _pack_small({n: mom1[n] for n in small_names}, D),
        _pack_small({n: mom2[n] for n in small_names}, D), "adamw_small")
    consume(("ffn1_w_down", "ffn1_w_gate", "ffn1_w_up"), dep=blob)
    unpacked = [_unpack_small(b, D) for b in (blob, d_blob, m_blob, v_blob)]
    for n in small_names:
        out[n] = tuple(u[n].reshape(weights[n].shape) for u in unpacked)

    loss = blob[ROW_LOSS, 0]
    result = [loss, grad_x[None]]
    for k in range(4):
        result += [out[n][k] for n in WEIGHT_NAMES]
    return tuple(result)
```

```python
import functools

import numpy as np
import jax
import jax.numpy as jnp
from jax import lax
from jax.experimental import pallas as pl
from jax.experimental.pallas import tpu as pltpu
from jax.experimental.pallas import tpu_sc as plsc

F32 = jnp.float32
BF16 = jnp.bfloat16

RMS_EPS = 1e-6
LN_EPS = 1e-5
HEAD_DIM = 128
N_HEADS = 8
GROUP_DIM = 128
N_GROUPS = 8
SGU_LEN = 128
CHUNK = 64
N_DEV = 8
LANES = 128
VMEM_LIMIT = 56 * 1024 * 1024
NEG_BIG = -1e30

ADAM_LR = 0.001
ADAM_B1 = 0.9
ADAM_B2 = 0.999
ADAM_EPS = 1e-08
ADAM_WD = 0.01
ADAM_STEP = 10

MESH = pl.DeviceIdType.MESH
ANY = pl.BlockSpec(memory_space=pl.ANY)


def _blk(n, pref):
    return pref if (n >= pref and n % pref == 0) else n


def _mm(a, b):
    return jnp.dot(a, b, preferred_element_type=F32)


def _mm_nt(a, b):
    return lax.dot_general(a, b, (((1,), (1,)), ((), ())), preferred_element_type=F32)


def _mm_tn(a, b):
    return lax.dot_general(a, b, (((0,), (0,)), ((), ())), preferred_element_type=F32)


def _params(sem):
    return pltpu.CompilerParams(dimension_semantics=sem, vmem_limit_bytes=VMEM_LIMIT)


def _gelu(x):
    return 0.5 * x * (1.0 + lax.erf(x * np.float32(1.0 / np.sqrt(2.0))))


def _gelu_grad(x):
    cdf = 0.5 * (1.0 + lax.erf(x * np.float32(1.0 / np.sqrt(2.0))))
    return cdf + x * jnp.exp(-0.5 * x * x) * np.float32(1.0 / np.sqrt(2.0 * np.pi))


def _rms_scale(v):
    return lax.rsqrt(jnp.mean(v * v, axis=-1, keepdims=True) + RMS_EPS)


def _rms_bwd(dy, xhat, r, g):
    dxh = dy * g
    return r * (dxh - xhat * jnp.mean(dxh * xhat, axis=-1, keepdims=True))


def _ffn_rows(T):
    tm = _blk(T, 1024)
    th = _blk(tm, 512)
    return tm, th, tm // th


def _ffn_fwd(x, g_pre, wg, wu, wd, g_post, name):
    T, D = x.shape
    ns, _, fs = wg.shape
    tm, th, parts = _ffn_rows(T)

    def body(x_ref, gpre_ref, wg_ref, wu_ref, wd_ref, gpost_ref, xo_ref, y_ref, dgf_ref, silu_ref, act_ref,
             h_scr, acc_scr):
        j = pl.program_id(1)

        @pl.when(j == 0)
        def _():
            for r in range(parts):
                rows = slice(r * th, (r + 1) * th)
                xv = x_ref[rows, :]
                h_scr[rows, :] = (xv * _rms_scale(xv) * gpre_ref[...]).astype(BF16)
            acc_scr[...] = jnp.zeros_like(acc_scr)

        pre = []
        for r in range(parts):
            h = h_scr[r * th:(r + 1) * th, :]
            pre.append((_mm(h, wg_ref[...]), _mm(h, wu_ref[...])))
        for r in range(parts):
            rows = slice(r * th, (r + 1) * th)
            gg, uu = pre[r]
            sg = jax.nn.sigmoid(gg)
            silu = gg * sg
            act = (silu * uu).astype(BF16)
            dgf_ref[rows, :] = (uu * (sg * (1.0 + gg * (1.0 - sg)))).astype(BF16)
            silu_ref[rows, :] = silu.astype(BF16)
            act_ref[rows, :] = act
            acc_scr[rows, :] += _mm(act, wd_ref[...])

        @pl.when(j == ns - 1)
        def _():
            for r in range(parts):
                rows = slice(r * th, (r + 1) * th)
                y = acc_scr[rows, :]
                y_ref[rows, :] = y
                xo_ref[rows, :] = x_ref[rows, :] + 0.5 * (y * _rms_scale(y) * gpost_ref[...])

    row = pl.BlockSpec((tm, D), lambda i, j: (i, 0), pipeline_mode=pl.Buffered(1))
    vec = pl.BlockSpec((1, D), lambda i, j: (0, 0))
    return pl.pallas_call(
        body, name=name, grid=(T // tm, ns),
        in_specs=[row, vec,
                  pl.BlockSpec((None, D, fs), lambda i, j: (j, 0, 0)),
                  pl.BlockSpec((None, D, fs), lambda i, j: (j, 0, 0)),
                  pl.BlockSpec((None, fs, D), lambda i, j: (j, 0, 0)),
                  vec],
        out_specs=[row, row] + [pl.BlockSpec((tm, fs), lambda i, j: (i, j))] * 3,
        out_shape=[jax.ShapeDtypeStruct((T, D), F32), jax.ShapeDtypeStruct((T, D), F32)]
        + [jax.ShapeDtypeStruct((T, ns * fs), BF16)] * 3,
        scratch_shapes=[pltpu.VMEM((tm, D), BF16), pltpu.VMEM((tm, D), F32)],
        compiler_params=_params(("parallel", "arbitrary")),
    )(x, g_pre, wg, wu, wd, g_post)


def _after(dep):
    return jnp.zeros((8, LANES), F32) if dep is None else dep


def _ffn_bwd(dxo, x, y, dgf, silu, g_pre, wg, wu, wd, g_post, name, dep=None):
    T, D = x.shape
    ns, _, fs = wg.shape
    tm, th, parts = _ffn_rows(T)
    n_i = T // tm

    def body(dxo_ref, x_ref, y_ref, dgf_ref, silu_ref, gpre_ref, wg_ref, wu_ref, wd_ref, gpost_ref, _,
             dx_ref, hb_ref, dyb_ref, dgb_ref, dub_ref, dgpre_ref, dgpost_ref, dy_scr, acc_scr):
        j = pl.program_id(1)

        @pl.when(j == 0)
        def _():
            dgpost = jnp.zeros((1, D), F32)
            for r in range(parts):
                rows = slice(r * th, (r + 1) * th)
                yv = y_ref[rows, :]
                s = _rms_scale(yv)
                n = yv * s
                dn = 0.5 * dxo_ref[rows, :]
                dgpost = dgpost + jnp.sum(dn * n, axis=0, keepdims=True)
                dyv = _rms_bwd(dn, n, s, gpost_ref[...]).astype(BF16)
                dy_scr[rows, :] = dyv
                dyb_ref[rows, :] = dyv
                xv = x_ref[rows, :]
                hb_ref[rows, :] = (xv * _rms_scale(xv) * gpre_ref[...]).astype(BF16)
            dgpost_ref[...] = dgpost
            acc_scr[...] = jnp.zeros_like(acc_scr)

        das = [_mm_nt(dy_scr[r * th:(r + 1) * th, :], wd_ref[...]) for r in range(parts)]
        for r in range(parts):
            rows = slice(r * th, (r + 1) * th)
            dgate = (das[r] * dgf_ref[rows, :].astype(F32)).astype(BF16)
            dup = (das[r] * silu_ref[rows, :].astype(F32)).astype(BF16)
            dgb_ref[rows, :] = dgate
            dub_ref[rows, :] = dup
            acc_scr[rows, :] += _mm_nt(dgate, wg_ref[...]) + _mm_nt(dup, wu_ref[...])

        @pl.when(j == ns - 1)
        def _():
            dgpre = jnp.zeros((1, D), F32)
            for r in range(parts):
                rows = slice(r * th, (r + 1) * th)
                xv = x_ref[rows, :]
                rs = _rms_scale(xv)
                xhat = xv * rs
                dh = acc_scr[rows, :]
                dgpre = dgpre + jnp.sum(dh * xhat, axis=0, keepdims=True)
                dx_ref[rows, :] = _rms_bwd(dh, xhat, rs, gpre_ref[...]) + dxo_ref[rows, :]
            dgpre_ref[...] = dgpre

    row = pl.BlockSpec((tm, D), lambda i, j: (i, 0), pipeline_mode=pl.Buffered(1))
    vec = pl.BlockSpec((1, D), lambda i, j: (0, 0))
    wide = pl.BlockSpec((tm, fs), lambda i, j: (i, j))
    part = pl.BlockSpec((None, 1, D), lambda i, j: (i, 0, 0))
    F = ns * fs
    return pl.pallas_call(
        body, name=name, grid=(n_i, ns),
        in_specs=[row, row, row, wide, wide, vec,
                  pl.BlockSpec((None, D, fs), lambda i, j: (j, 0, 0)),
                  pl.BlockSpec((None, D, fs), lambda i, j: (j, 0, 0)),
                  pl.BlockSpec((None, fs, D), lambda i, j: (j, 0, 0)),
                  vec, ANY],
        out_specs=[row, row, row, wide, wide, part, part],
        out_shape=[jax.ShapeDtypeStruct((T, D), F32), jax.ShapeDtypeStruct((T, D), BF16),
                   jax.ShapeDtypeStruct((T, D), BF16), jax.ShapeDtypeStruct((T, F), BF16),
                   jax.ShapeDtypeStruct((T, F), BF16),
                   jax.ShapeDtypeStruct((n_i, 1, D), F32), jax.ShapeDtypeStruct((n_i, 1, D), F32)],
        scratch_shapes=[pltpu.VMEM((tm, D), BF16), pltpu.VMEM((tm, D), F32)],
        compiler_params=_params(("parallel", "arbitrary")),
    )(dxo, x, y, dgf, silu, g_pre, wg, wu, wd, g_post, _after(dep))


def _wgrad(xm, ym, name, shard_cols=False, dep=None, y_index=None):
    T, M = xm.shape
    N = ym.shape[-1]
    if y_index is None:
        y_spec = pl.BlockSpec((_blk(T, 512), N), lambda k: (k, 0))
    else:
        y_spec = pl.BlockSpec((None, _blk(T, 512), N), lambda k: (y_index, k, 0))
    assert M * N * 4 <= 16 * 1024 * 1024, (M, N)
    tk = _blk(T, 512)
    n_k = T // tk
    fs = N // N_DEV

    def body(x_ref, y_ref, _, o_ref, acc_scr):
        k = pl.program_id(0)

        @pl.when(k == 0)
        def _():
            acc_scr[...] = jnp.zeros_like(acc_scr)

        acc_scr[...] += _mm_tn(x_ref[...], y_ref[...])

        @pl.when(k == n_k - 1)
        def _():
            if shard_cols:
                for s in range(N_DEV):
                    o_ref[s] = acc_scr[:, s * fs:(s + 1) * fs].astype(BF16)
            else:
                o_ref[...] = acc_scr[...].astype(BF16)

    if shard_cols:
        out_spec = pl.BlockSpec((N_DEV, M, fs), lambda k: (0, 0, 0), pipeline_mode=pl.Buffered(1))
        out_shape = jax.ShapeDtypeStruct((N_DEV, M, fs), BF16)
    else:
        out_spec = pl.BlockSpec((M, N), lambda k: (0, 0), pipeline_mode=pl.Buffered(1))
        out_shape = jax.ShapeDtypeStruct((M, N), BF16)
    return pl.pallas_call(
        body, name=name, grid=(n_k,),
        in_specs=[pl.BlockSpec((tk, M), lambda k: (k, 0)), y_spec, ANY],
        out_specs=out_spec, out_shape=out_shape,
        scratch_shapes=[pltpu.VMEM((M, N), F32)],
        compiler_params=_params(("arbitrary",)),
    )(xm, ym, _after(dep))


def _mix_in_fwd(x1, g, w7, wf, name):
    T, D = x1.shape
    n_seg, _, W = w7.shape
    tm = _blk(T, 1024)

    def body(x_ref, g_ref, w_ref, wf_ref, z_ref, f_ref, hb_ref, h_scr):
        s = pl.program_id(1)

        @pl.when(s == 0)
        def _():
            xv = x_ref[...]
            h = (xv * _rms_scale(xv) * g_ref[...]).astype(BF16)
            h_scr[...] = h
            hb_ref[...] = h
            f_ref[...] = _mm(h, wf_ref[...])

        z_ref[...] = _mm(h_scr[...], w_ref[...]).astype(BF16)

    return pl.pallas_call(
        body, name=name, grid=(T // tm, n_seg),
        in_specs=[pl.BlockSpec((tm, D), lambda i, s: (i, 0)),
                  pl.BlockSpec((1, D), lambda i, s: (0, 0)),
                  pl.BlockSpec((None, D, W), lambda i, s: (s, 0, 0)),
                  pl.BlockSpec((D, LANES), lambda i, s: (0, 0))],
        out_specs=[pl.BlockSpec((None, tm, W), lambda i, s: (s, i, 0)),
                   pl.BlockSpec((tm, LANES), lambda i, s: (i, 0)),
                   pl.BlockSpec((tm, D), lambda i, s: (i, 0))],
        out_shape=[jax.ShapeDtypeStruct((n_seg, T, W), BF16), jax.ShapeDtypeStruct((T, LANES), F32),
                   jax.ShapeDtypeStruct((T, D), BF16)],
        scratch_shapes=[pltpu.VMEM((tm, D), BF16)],
        compiler_params=_params(("parallel", "arbitrary")),
    )(x1, g, w7, wf)


def _mix_in_bwd(dx2, x1, g, segs, dfb, w7, wf, name, dep=None):
    T, D = x1.shape
    n_seg, _, W = w7.shape
    tm, th, parts = _ffn_rows(T)
    n_i = T // tm

    def body(*refs):
        dx2_ref, x_ref, g_ref = refs[:3]
        seg_refs = refs[3:3 + n_seg]
        df_ref, w_ref, wf_ref, _, dx1_ref, dg_ref, acc_scr = refs[3 + n_seg:]
        s = pl.program_id(1)

        @pl.when(s == 0)
        def _():
            acc_scr[...] = _mm_nt(df_ref[...], wf_ref[...])

        for q in range(n_seg):
            @pl.when(s == q)
            def _(q=q):
                acc_scr[...] += _mm_nt(seg_refs[q][...], w_ref[...])

        @pl.when(s == n_seg - 1)
        def _():
            dg = jnp.zeros((1, D), F32)
            for p in range(parts):
                rows = slice(p * th, (p + 1) * th)
                xv = x_ref[rows, :]
                r = _rms_scale(xv)
                xhat = xv * r
                dh = acc_scr[rows, :]
                dg = dg + jnp.sum(dh * xhat, axis=0, keepdims=True)
                dx1_ref[rows, :] = _rms_bwd(dh, xhat, r, g_ref[...]) + dx2_ref[rows, :]
            dg_ref[...] = dg

    row = pl.BlockSpec((tm, D), lambda i, s: (i, 0), pipeline_mode=pl.Buffered(1))
    seg_specs = []
    seg_args = []
    for arr, idx in segs:
        if idx is None:
            seg_specs.append(pl.BlockSpec((tm, W), lambda i, s: (i, 0)))
        else:
            seg_specs.append(pl.BlockSpec((None, tm, W), lambda i, s, idx=idx: (idx, i, 0)))
        seg_args.append(arr)
    return pl.pallas_call(
        body, name=name, grid=(n_i, n_seg),
        in_specs=[row, row, pl.BlockSpec((1, D), lambda i, s: (0, 0))] + seg_specs + [
            pl.BlockSpec((tm, LANES), lambda i, s: (i, 0)),
            pl.BlockSpec((None, D, W), lambda i, s: (s, 0, 0)),
            pl.BlockSpec((D, LANES), lambda i, s: (0, 0)), ANY],
        out_specs=[row, pl.BlockSpec((None, 1, D), lambda i, s: (i, 0, 0))],
        out_shape=[jax.ShapeDtypeStruct((T, D), F32), jax.ShapeDtypeStruct((n_i, 1, D), F32)],
        scratch_shapes=[pltpu.VMEM((tm, D), F32)],
        compiler_params=_params(("parallel", "arbitrary")),
    )(dx2, x1, g, *seg_args, dfb, w7, wf, _after(dep))


def _forget_cumsum(f, b_pad, name):
    T, L = f.shape
    tb = _blk(T, 256)

    def body(f_ref, b_ref, c_ref, carry):
        @pl.when(pl.program_id(0) == 0)
        def _():
            carry[...] = jnp.zeros_like(carry)

        lf = jax.nn.log_sigmoid(f_ref[...] + b_ref[...])
        rows = lax.broadcasted_iota(jnp.int32, (tb, tb), 0)
        cols = lax.broadcasted_iota(jnp.int32, (tb, tb), 1)
        tri = (cols <= rows).astype(F32)
        c = jnp.dot(tri, lf, preferred_element_type=F32, precision=lax.Precision.HIGHEST) + carry[...]
        c_ref[...] = c
        carry[...] = c[tb - 1:tb, :]

    return pl.pallas_call(
        body, name=name, grid=(T // tb,),
        in_specs=[pl.BlockSpec((tb, L), lambda i: (i, 0)), pl.BlockSpec((1, L), lambda i: (0, 0))],
        out_specs=pl.BlockSpec((tb, L), lambda i: (i, 0)),
        out_shape=jax.ShapeDtypeStruct((T, L), F32),
        scratch_shapes=[pltpu.VMEM((1, L), F32)],
        compiler_params=_params(("arbitrary",)),
    )(f, b_pad)


def _forget_bwd(dc, f, b_pad, name):
    T, L = f.shape
    tb = _blk(T, 256)
    nb = T // tb

    def body(dc_ref, f_ref, b_ref, df_ref, db_ref, carry):
        @pl.when(pl.program_id(0) == 0)
        def _():
            carry[...] = jnp.zeros_like(carry)
            db_ref[...] = jnp.zeros_like(db_ref)

        rows = lax.broadcasted_iota(jnp.int32, (tb, tb), 0)
        cols = lax.broadcasted_iota(jnp.int32, (tb, tb), 1)
        tri = (cols >= rows).astype(F32)
        r = jnp.dot(tri, dc_ref[...], preferred_element_type=F32, precision=lax.Precision.HIGHEST) + carry[...]
        carry[...] = r[0:1, :]
        df = r * (1.0 - jax.nn.sigmoid(f_ref[...] + b_ref[...]))
        df_ref[...] = df.astype(BF16)
        db_ref[...] += jnp.sum(df, axis=0, keepdims=True)

    rev = pl.BlockSpec((tb, L), lambda i: (nb - 1 - i, 0))
    one = pl.BlockSpec((1, L), lambda i: (0, 0))
    return pl.pallas_call(
        body, name=name, grid=(nb,),
        in_specs=[rev, rev, one], out_specs=[rev, one],
        out_shape=[jax.ShapeDtypeStruct((T, L), BF16), jax.ShapeDtypeStruct((1, L), F32)],
        scratch_shapes=[pltpu.VMEM((1, L), F32)],
        compiler_params=_params(("arbitrary",)),
    )(dc, f, b_pad)


def _attn_fwd(z7, c_row, name):
    _, T, W = z7.shape
    H = W // HEAD_DIM
    ta = _blk(T, 512)
    nq = T // ta
    scale = np.float32(1.0 / np.sqrt(HEAD_DIM))

    def body(q_ref, k_ref, v_ref, crow_ref, o_ref, lse_ref, m_scr, l_scr, acc_scr):
        i = pl.program_id(1)
        j = pl.program_id(2)

        @pl.when(j == 0)
        def _():
            m_scr[...] = jnp.full_like(m_scr, NEG_BIG)
            l_scr[...] = jnp.zeros_like(l_scr)
            acc_scr[...] = jnp.zeros_like(acc_scr)

        def step(diagonal):
            s = _mm_nt(q_ref[...], k_ref[...]) * scale - crow_ref[...]
            if diagonal:
                rows = lax.broadcasted_iota(jnp.int32, (ta, ta), 0)
                cols = lax.broadcasted_iota(jnp.int32, (ta, ta), 1)
                s = jnp.where(cols <= rows, s, NEG_BIG)
            m_prev = m_scr[...]
            m_new = jnp.maximum(m_prev, jnp.max(s, axis=-1, keepdims=True))
            alpha = jnp.exp(m_prev - m_new)
            p = jnp.exp(s - m_new)
            l_scr[...] = alpha * l_scr[...] + jnp.sum(p, axis=-1, keepdims=True)
            acc_scr[...] = alpha * acc_scr[...] + _mm(p.astype(BF16), v_ref[...])
            m_scr[...] = m_new

        @pl.when(j < i)
        def _():
            step(False)

        @pl.when(j == i)
        def _():
            step(True)
            l = l_scr[...]
            o_ref[...] = acc_scr[...] / l
            lse_ref[...] = m_scr[...] + jnp.log(l)

    return pl.pallas_call(
        body, name=name, grid=(H, nq, nq),
        in_specs=[pl.BlockSpec((None, ta, HEAD_DIM), lambda h, i, j: (0, i, h)),
                  pl.BlockSpec((None, ta, HEAD_DIM), lambda h, i, j: (1, jnp.minimum(i, j), h)),
                  pl.BlockSpec((None, ta, HEAD_DIM), lambda h, i, j: (2, jnp.minimum(i, j), h)),
                  pl.BlockSpec((None, 1, ta), lambda h, i, j: (h, 0, jnp.minimum(i, j)))],
        out_specs=[pl.BlockSpec((ta, HEAD_DIM), lambda h, i, j: (i, h)),
                   pl.BlockSpec((None, ta, 1), lambda h, i, j: (h, i, 0))],
        out_shape=[jax.ShapeDtypeStruct((T, W), F32), jax.ShapeDtypeStruct((H, T, 1), F32)],
        scratch_shapes=[pltpu.VMEM((ta, 1), F32), pltpu.VMEM((ta, 1), F32), pltpu.VMEM((ta, HEAD_DIM), F32)],
        compiler_params=_params(("parallel", "parallel", "arbitrary")),
    )(z7, z7, z7, c_row)


def _attn_bwd_kv(z7, dob, c_col, lse_row, d_row, name):
    _, T, W = z7.shape
    H = W // HEAD_DIM
    ta = _blk(T, 512)
    nq = T // ta
    scale = np.float32(1.0 / np.sqrt(HEAD_DIM))

    def body(k_ref, v_ref, q_ref, do_ref, ccol_ref, lse_ref, d_ref, dk_ref, dv_ref, dc_ref, dk_scr, dv_scr, dc_scr):
        j = pl.program_id(1)
        i = pl.program_id(2)

        @pl.when(i == 0)
        def _():
            dk_scr[...] = jnp.zeros_like(dk_scr)
            dv_scr[...] = jnp.zeros_like(dv_scr)
            dc_scr[...] = jnp.zeros_like(dc_scr)

        def step(diagonal):
            q = q_ref[...]
            do = do_ref[...]
            st = _mm_nt(k_ref[...], q) * scale - ccol_ref[...] - lse_ref[...]
            if diagonal:
                rows = lax.broadcasted_iota(jnp.int32, (ta, ta), 0)
                cols = lax.broadcasted_iota(jnp.int32, (ta, ta), 1)
                st = jnp.where(rows <= cols, st, NEG_BIG)
            pt = jnp.exp(st)
            dv_scr[...] += _mm(pt.astype(BF16), do)
            dst = pt * (_mm_nt(v_ref[...], do) - d_ref[...])
            dk_scr[...] += _mm(dst.astype(BF16), q)
            dc_scr[...] += jnp.sum(dst, axis=-1, keepdims=True)

        @pl.when(i > j)
        def _():
            step(False)

        @pl.when(i == j)
        def _():
            step(True)

        @pl.when(i == nq - 1)
        def _():
            dk_ref[...] = (dk_scr[...] * scale).astype(BF16)
            dv_ref[...] = dv_scr[...].astype(BF16)
            dc_ref[...] = -dc_scr[...]

    return pl.pallas_call(
        body, name=name, grid=(H, nq, nq),
        in_specs=[pl.BlockSpec((None, ta, HEAD_DIM), lambda h, j, i: (1, j, h)),
                  pl.BlockSpec((None, ta, HEAD_DIM), lambda h, j, i: (2, j, h)),
                  pl.BlockSpec((None, ta, HEAD_DIM), lambda h, j, i: (0, jnp.maximum(i, j), h)),
                  pl.BlockSpec((ta, HEAD_DIM), lambda h, j, i: (jnp.maximum(i, j), h)),
                  pl.BlockSpec((None, ta, 1), lambda h, j, i: (h, j, 0)),
                  pl.BlockSpec((None, 1, ta), lambda h, j, i: (h, 0, jnp.maximum(i, j))),
                  pl.BlockSpec((None, 1, ta), lambda h, j, i: (h, 0, jnp.maximum(i, j)))],
        out_specs=[pl.BlockSpec((ta, HEAD_DIM), lambda h, j, i: (j, h)),
                   pl.BlockSpec((ta, HEAD_DIM), lambda h, j, i: (j, h)),
                   pl.BlockSpec((None, ta, 1), lambda h, j, i: (h, j, 0))],
        out_shape=[jax.ShapeDtypeStruct((T, W), BF16), jax.ShapeDtypeStruct((T, W), BF16),
                   jax.ShapeDtypeStruct((H, T, 1), F32)],
        scratch_shapes=[pltpu.VMEM((ta, HEAD_DIM), F32), pltpu.VMEM((ta, HEAD_DIM), F32), pltpu.VMEM((ta, 1), F32)],
        compiler_params=_params(("parallel", "parallel", "arbitrary")),
    )(z7, z7, z7, dob, c_col, lse_row, d_row)


def _attn_bwd_q(z7, dob, c_row, lse_col, d_col, name):
    _, T, W = z7.shape
    H = W // HEAD_DIM
    ta = _blk(T, 512)
    nq = T // ta
    scale = np.float32(1.0 / np.sqrt(HEAD_DIM))

    def body(q_ref, k_ref, v_ref, do_ref, crow_ref, lse_ref, d_ref, dq_ref, dc_ref, dq_scr, dc_scr):
        i = pl.program_id(1)
        j = pl.program_id(2)

        @pl.when(j == 0)
        def _():
            dq_scr[...] = jnp.zeros_like(dq_scr)
            dc_scr[...] = jnp.zeros_like(dc_scr)

        def step(diagonal):
            k = k_ref[...]
            do = do_ref[...]
            s = _mm_nt(q_ref[...], k) * scale - crow_ref[...] - lse_ref[...]
            if diagonal:
                rows = lax.broadcasted_iota(jnp.int32, (ta, ta), 0)
                cols = lax.broadcasted_iota(jnp.int32, (ta, ta), 1)
                s = jnp.where(cols <= rows, s, NEG_BIG)
            p = jnp.exp(s)
            ds = p * (_mm_nt(do, v_ref[...]) - d_ref[...])
            dq_scr[...] += _mm(ds.astype(BF16), k)
            dc_scr[...] += jnp.sum(ds, axis=-1, keepdims=True)

        @pl.when(j < i)
        def _():
            step(False)

        @pl.when(j == i)
        def _():
            step(True)
            dq_ref[...] = (dq_scr[...] * scale).astype(BF16)
            dc_ref[...] = dc_scr[...]

    return pl.pallas_call(
        body, name=name, grid=(H, nq, nq),
        in_specs=[pl.BlockSpec((None, ta, HEAD_DIM), lambda h, i, j: (0, i, h)),
                  pl.BlockSpec((None, ta, HEAD_DIM), lambda h, i, j: (1, jnp.minimum(i, j), h)),
                  pl.BlockSpec((None, ta, HEAD_DIM), lambda h, i, j: (2, jnp.minimum(i, j), h)),
                  pl.BlockSpec((ta, HEAD_DIM), lambda h, i, j: (i, h)),
                  pl.BlockSpec((None, 1, ta), lambda h, i, j: (h, 0, jnp.minimum(i, j))),
                  pl.BlockSpec((None, ta, 1), lambda h, i, j: (h, i, 0)),
                  pl.BlockSpec((None, ta, 1), lambda h, i, j: (h, i, 0))],
        out_specs=[pl.BlockSpec((ta, HEAD_DIM), lambda h, i, j: (i, h)),
                   pl.BlockSpec((None, ta, 1), lambda h, i, j: (h, i, 0))],
        out_shape=[jax.ShapeDtypeStruct((T, W), BF16), jax.ShapeDtypeStruct((H, T, 1), F32)],
        scratch_shapes=[pltpu.VMEM((ta, HEAD_DIM), F32), pltpu.VMEM((ta, 1), F32)],
        compiler_params=_params(("parallel", "parallel", "arbitrary")),
    )(z7, z7, z7, dob, c_row, lse_col, d_col)


ATTN_TILE = 512
ATTN_CHAINS = 2


def _attn_geometry(T):
    ta = _blk(T, ATTN_TILE)
    nc = ATTN_CHAINS if (T // ta) % ATTN_CHAINS == 0 else 1
    return ta, nc, T // ta


def _causal_tile(ta, keys_on_rows=False):
    rows = lax.broadcasted_iota(jnp.int32, (ta, ta), 0)
    cols = lax.broadcasted_iota(jnp.int32, (ta, ta), 1)
    return rows <= cols if keys_on_rows else cols <= rows


def _chunk(ref, j, ta):
    return ref[pl.ds(pl.multiple_of(j * ta, ta), ta), :]


def _attn_fwd_loop(z7, c_chunks, name):
    _, T, W = z7.shape
    H = W // HEAD_DIM
    ta, nc, n_chunks = _attn_geometry(T)
    scale = np.float32(1.0 / np.sqrt(HEAD_DIM))

    def body(q_ref, k_ref, v_ref, c_ref, o_ref, lse_ref, m_scr, l_scr, acc_scr):
        g = pl.program_id(1)
        m_scr[...] = jnp.full_like(m_scr, NEG_BIG)
        l_scr[...] = jnp.zeros_like(l_scr)
        acc_scr[...] = jnp.zeros_like(acc_scr)

        def update(ch, k, v, crow, diagonal):
            q = q_ref[ch * ta:(ch + 1) * ta, :]
            s = _mm_nt(q, k) * scale - crow
            if diagonal:
                s = jnp.where(_causal_tile(ta), s, NEG_BIG)
            m_prev = m_scr[ch]
            m_new = jnp.maximum(m_prev, jnp.max(s, axis=-1, keepdims=True))
            alpha = jnp.exp(m_prev - m_new)
            p = jnp.exp(s - m_new)
            l_scr[ch] = alpha * l_scr[ch] + jnp.sum(p, axis=-1, keepdims=True)
            acc_scr[ch] = alpha * acc_scr[ch] + _mm(p.astype(BF16), v)
            m_scr[ch] = m_new

        def full_chunk(j, carry):
            k = _chunk(k_ref, j, ta)
            v = _chunk(v_ref, j, ta)
            crow = c_ref[j]
            for ch in range(nc):
                update(ch, k, v, crow, False)
            return carry

        lax.fori_loop(0, nc * g, full_chunk, 0)
        for jj in range(nc):
            j = nc * g + jj
            k = _chunk(k_ref, j, ta)
            v = _chunk(v_ref, j, ta)
            crow = c_ref[j]
            for ch in range(jj, nc):
                update(ch, k, v, crow, ch == jj)
        for ch in range(nc):
            l = l_scr[ch]
            o_ref[ch * ta:(ch + 1) * ta, :] = acc_scr[ch] / l
            lse_ref[ch * ta:(ch + 1) * ta, :] = m_scr[ch] + jnp.log(l)

    tq = nc * ta
    return pl.pallas_call(
        body, name=name, grid=(H, n_chunks // nc),
        in_specs=[pl.BlockSpec((None, tq, HEAD_DIM), lambda h, g: (0, g, h)),
                  pl.BlockSpec((None, T, HEAD_DIM), lambda h, g: (1, 0, h)),
                  pl.BlockSpec((None, T, HEAD_DIM), lambda h, g: (2, 0, h)),
                  pl.BlockSpec((None, n_chunks, 1, ta), lambda h, g: (h, 0, 0, 0))],
        out_specs=[pl.BlockSpec((tq, HEAD_DIM), lambda h, g: (g, h)),
                   pl.BlockSpec((None, tq, 1), lambda h, g: (h, g, 0))],
        out_shape=[jax.ShapeDtypeStruct((T, W), F32), jax.ShapeDtypeStruct((H, T, 1), F32)],
        scratch_shapes=[pltpu.VMEM((nc, ta, 1), F32), pltpu.VMEM((nc, ta, 1), F32),
                        pltpu.VMEM((nc, ta, HEAD_DIM), F32)],
        compiler_params=_params(("parallel", "arbitrary")),
    )(z7, z7, z7, c_chunks)


def _attn_fwd_keys_on_rows(z7, vt, c_rep, name):
    _, T, W = z7.shape
    H = W // HEAD_DIM
    ta, nc, n_chunks = _attn_geometry(T)
    scale = np.float32(1.0 / np.sqrt(HEAD_DIM))
    reps = ta // LANES

    def body(q_ref, k_ref, vt_ref, c_ref, o_ref, lse_ref, m_scr, l_scr, acc_scr):
        g = pl.program_id(1)
        m_scr[...] = jnp.full_like(m_scr, NEG_BIG)
        l_scr[...] = jnp.zeros_like(l_scr)
        acc_scr[...] = jnp.zeros_like(acc_scr)

        def update(ch, k, vt, cj, diagonal):
            q = q_ref[ch * ta:(ch + 1) * ta, :]
            st = _mm_nt(k, q) * scale - cj
            if diagonal:
                st = jnp.where(_causal_tile(ta, keys_on_rows=True), st, NEG_BIG)
            m_prev = m_scr[ch]
            m_new = jnp.maximum(m_prev, jnp.max(st, axis=0, keepdims=True))
            alpha = jnp.exp(m_prev - m_new)
            pt = jnp.exp(st - m_new)
            l_scr[ch] = alpha * l_scr[ch] + jnp.sum(pt, axis=0, keepdims=True)
            acc_scr[ch] = alpha * acc_scr[ch] + _mm(vt, pt.astype(BF16))
            m_scr[ch] = m_new

        def load(j):
            cj = _chunk(c_ref, j, ta)
            return _chunk(k_ref, j, ta), vt_ref[j], jnp.concatenate([cj] * reps, axis=1)

        def full_chunk(j, carry):
            k, vt, cj = load(j)
            for ch in range(nc):
                update(ch, k, vt, cj, False)
            return carry

        lax.fori_loop(0, nc * g, full_chunk, 0)
        for jj in range(nc):
            k, vt, cj = load(nc * g + jj)
            for ch in range(jj, nc):
                update(ch, k, vt, cj, ch == jj)
        for ch in range(nc):
            l = l_scr[ch]
            o_ref[ch * ta:(ch + 1) * ta, :] = (acc_scr[ch] / l).T
            lse_ref[ch] = m_scr[ch] + jnp.log(l)

    tq = nc * ta
    return pl.pallas_call(
        body, name=name, grid=(H, n_chunks // nc),
        in_specs=[pl.BlockSpec((None, tq, HEAD_DIM), lambda h, g: (0, g, h)),
                  pl.BlockSpec((None, T, HEAD_DIM), lambda h, g: (1, 0, h)),
                  pl.BlockSpec((None, n_chunks, HEAD_DIM, ta), lambda h, g: (h, 0, 0, 0)),
                  pl.BlockSpec((None, T, LANES), lambda h, g: (h, 0, 0))],
        out_specs=[pl.BlockSpec((tq, HEAD_DIM), lambda h, g: (g, h)),
                   pl.BlockSpec((None, nc, 1, ta), lambda h, g: (h, g, 0, 0))],
        out_shape=[jax.ShapeDtypeStruct((T, W), F32), jax.ShapeDtypeStruct((H, n_chunks, 1, ta), F32)],
        scratch_shapes=[pltpu.VMEM((nc, 1, ta), F32), pltpu.VMEM((nc, 1, ta), F32),
                        pltpu.VMEM((nc, HEAD_DIM, ta), F32)],
        compiler_params=_params(("parallel", "arbitrary")),
    )(z7, z7, vt, c_rep)


def _attn_bwd_fused(z7, kt, dob, c_rep, lse_chunks, d_chunks, name):
    _, T, W = z7.shape
    H = W // HEAD_DIM
    ta, nc, n_chunks = _attn_geometry(T)
    n_steps = n_chunks // nc
    scale = np.float32(1.0 / np.sqrt(HEAD_DIM))
    reps = ta // LANES

    def body(k_ref, v_ref, kt_ref, q_ref, do_ref, c_ref, lse_ref, d_ref,
             dk_ref, dv_ref, dck_ref, dq_ref, dcq_ref, dk_scr, dv_scr, dck_scr, dqt_scr, dcq_scr):
        g = pl.program_id(1)

        @pl.when(g == 0)
        def _():
            dqt_scr[...] = jnp.zeros_like(dqt_scr)
            dcq_scr[...] = jnp.zeros_like(dcq_scr)

        dk_scr[...] = jnp.zeros_like(dk_scr)
        dv_scr[...] = jnp.zeros_like(dv_scr)
        dck_scr[...] = jnp.zeros_like(dck_scr)

        def update(ch, i, q, do, diagonal):
            rows = slice(ch * ta, (ch + 1) * ta)
            cj = c_ref[rows, :]
            st = _mm_nt(k_ref[rows, :], q) * scale - jnp.concatenate([cj] * reps, axis=1) - lse_ref[i]
            if diagonal:
                st = jnp.where(_causal_tile(ta, keys_on_rows=True), st, NEG_BIG)
            pt = jnp.exp(st)
            dv_scr[ch] += _mm(pt.astype(BF16), do)
            dst = pt * (_mm_nt(v_ref[rows, :], do) - d_ref[i])
            dst_b = dst.astype(BF16)
            dk_scr[ch] += _mm(dst_b, q)
            dqt_scr[i] += _mm(kt_ref[ch], dst_b)
            dcq_scr[i] += jnp.sum(dst, axis=0, keepdims=True)
            lane_sum = dst[:, :LANES]
            for r in range(1, reps):
                lane_sum = lane_sum + dst[:, r * LANES:(r + 1) * LANES]
            dck_scr[ch] += lane_sum

        for ii in range(nc):
            i = nc * g + ii
            q = _chunk(q_ref, i, ta)
            do = _chunk(do_ref, i, ta)
            for ch in range(0, ii + 1):
                update(ch, i, q, do, ch == ii)

        def full_chunk(i, carry):
            q = _chunk(q_ref, i, ta)
            do = _chunk(do_ref, i, ta)
            for ch in range(nc):
                update(ch, i, q, do, False)
            return carry

        lax.fori_loop(nc * (g + 1), n_chunks, full_chunk, 0)
        for ch in range(nc):
            rows = slice(ch * ta, (ch + 1) * ta)
            dk_ref[rows, :] = (dk_scr[ch] * scale).astype(BF16)
            dv_ref[rows, :] = dv_scr[ch].astype(BF16)
            ones = jnp.ones((8, LANES), F32)
            sums = lax.dot_general(ones, dck_scr[ch], (((1,), (1,)), ((), ())), preferred_element_type=F32,
                                   precision=lax.Precision.HIGHEST)
            dck_ref[ch] = -sums[0:1, :]

        @pl.when(g == n_steps - 1)
        def _():
            for i in range(n_chunks):
                dq_ref[i * ta:(i + 1) * ta, :] = (dqt_scr[i] * scale).T.astype(BF16)
            dcq_ref[...] = dcq_scr[...]

    tk = nc * ta
    chunks = pl.BlockSpec((None, n_chunks, 1, ta), lambda h, g: (h, 0, 0, 0))
    tile = pl.BlockSpec((tk, HEAD_DIM), lambda h, g: (g, h))
    return pl.pallas_call(
        body, name=name, grid=(H, n_steps),
        in_specs=[pl.BlockSpec((None, tk, HEAD_DIM), lambda h, g: (1, g, h)),
                  pl.BlockSpec((None, tk, HEAD_DIM), lambda h, g: (2, g, h)),
                  pl.BlockSpec((None, nc, HEAD_DIM, ta), lambda h, g: (h, g, 0, 0)),
                  pl.BlockSpec((None, T, HEAD_DIM), lambda h, g: (0, 0, h)),
                  pl.BlockSpec((T, HEAD_DIM), lambda h, g: (0, h)),
                  pl.BlockSpec((None, tk, LANES), lambda h, g: (h, g, 0)),
                  chunks, chunks],
        out_specs=[tile, tile, pl.BlockSpec((None, nc, 1, ta), lambda h, g: (h, g, 0, 0)),
                   pl.BlockSpec((T, HEAD_DIM), lambda h, g: (0, h)), chunks],
        out_shape=[jax.ShapeDtypeStruct((T, W), BF16), jax.ShapeDtypeStruct((T, W), BF16),
                   jax.ShapeDtypeStruct((H, n_chunks, 1, ta), F32), jax.ShapeDtypeStruct((T, W), BF16),
                   jax.ShapeDtypeStruct((H, n_chunks, 1, ta), F32)],
        scratch_shapes=[pltpu.VMEM((nc, ta, HEAD_DIM), F32), pltpu.VMEM((nc, ta, HEAD_DIM), F32),
                        pltpu.VMEM((nc, ta, LANES), F32), pltpu.VMEM((n_chunks, HEAD_DIM, ta), F32),
                        pltpu.VMEM((n_chunks, 1, ta), F32)],
        compiler_params=_params(("parallel", "arbitrary")),
    )(z7, z7, kt, z7, dob, c_rep, lse_chunks, d_chunks)


def _attn_bwd_q_loop(z7, dob, c_chunks, lse_col, d_col, name):
    _, T, W = z7.shape
    H = W // HEAD_DIM
    ta, nc, n_chunks = _attn_geometry(T)
    scale = np.float32(1.0 / np.sqrt(HEAD_DIM))

    def body(q_ref, k_ref, v_ref, do_ref, c_ref, lse_ref, d_ref, dq_ref, dc_ref, dq_scr, dc_scr):
        g = pl.program_id(1)
        dq_scr[...] = jnp.zeros_like(dq_scr)
        dc_scr[...] = jnp.zeros_like(dc_scr)

        def update(ch, k, v, crow, diagonal):
            rows = slice(ch * ta, (ch + 1) * ta)
            do = do_ref[rows, :]
            s = _mm_nt(q_ref[rows, :], k) * scale - crow - lse_ref[rows, :]
            if diagonal:
                s = jnp.where(_causal_tile(ta), s, NEG_BIG)
            p = jnp.exp(s)
            ds = p * (_mm_nt(do, v) - d_ref[rows, :])
            dq_scr[ch] += _mm(ds.astype(BF16), k)
            dc_scr[ch] += jnp.sum(ds, axis=-1, keepdims=True)

        def full_chunk(j, carry):
            k = _chunk(k_ref, j, ta)
            v = _chunk(v_ref, j, ta)
            crow = c_ref[j]
            for ch in range(nc):
                update(ch, k, v, crow, False)
            return carry

        lax.fori_loop(0, nc * g, full_chunk, 0)
        for jj in range(nc):
            j = nc * g + jj
            k = _chunk(k_ref, j, ta)
            v = _chunk(v_ref, j, ta)
            crow = c_ref[j]
            for ch in range(jj, nc):
                update(ch, k, v, crow, ch == jj)
        for ch in range(nc):
            dq_ref[ch * ta:(ch + 1) * ta, :] = (dq_scr[ch] * scale).astype(BF16)
            dc_ref[ch * ta:(ch + 1) * ta, :] = dc_scr[ch]

    tq = nc * ta
    col = pl.BlockSpec((None, tq, 1), lambda h, g: (h, g, 0))
    return pl.pallas_call(
        body, name=name, grid=(H, n_chunks // nc),
        in_specs=[pl.BlockSpec((None, tq, HEAD_DIM), lambda h, g: (0, g, h)),
                  pl.BlockSpec((None, T, HEAD_DIM), lambda h, g: (1, 0, h)),
                  pl.BlockSpec((None, T, HEAD_DIM), lambda h, g: (2, 0, h)),
                  pl.BlockSpec((tq, HEAD_DIM), lambda h, g: (g, h)),
                  pl.BlockSpec((None, n_chunks, 1, ta), lambda h, g: (h, 0, 0, 0)),
                  col, col],
        out_specs=[pl.BlockSpec((tq, HEAD_DIM), lambda h, g: (g, h)), col],
        out_shape=[jax.ShapeDtypeStruct((T, W), BF16), jax.ShapeDtypeStruct((H, T, 1), F32)],
        scratch_shapes=[pltpu.VMEM((nc, ta, HEAD_DIM), F32), pltpu.VMEM((nc, ta, 1), F32)],
        compiler_params=_params(("parallel", "arbitrary")),
    )(z7, z7, z7, dob, c_chunks, lse_col, d_col)


def _attn_bwd_kv_loop(z7, dob, c_col, lse_chunks, d_chunks, name):
    _, T, W = z7.shape
    H = W // HEAD_DIM
    ta, nc, n_chunks = _attn_geometry(T)
    scale = np.float32(1.0 / np.sqrt(HEAD_DIM))

    def body(k_ref, v_ref, q_ref, do_ref, ccol_ref, lse_ref, d_ref, dk_ref, dv_ref, dc_ref, dk_scr, dv_scr, dc_scr):
        g = pl.program_id(1)
        dk_scr[...] = jnp.zeros_like(dk_scr)
        dv_scr[...] = jnp.zeros_like(dv_scr)
        dc_scr[...] = jnp.zeros_like(dc_scr)

        def update(ch, q, do, lse_row, d_row, diagonal):
            rows = slice(ch * ta, (ch + 1) * ta)
            st = _mm_nt(k_ref[rows, :], q) * scale - ccol_ref[rows, :] - lse_row
            if diagonal:
                st = jnp.where(_causal_tile(ta, keys_on_rows=True), st, NEG_BIG)
            pt = jnp.exp(st)
            dv_scr[ch] += _mm(pt.astype(BF16), do)
            dst = pt * (_mm_nt(v_ref[rows, :], do) - d_row)
            dk_scr[ch] += _mm(dst.astype(BF16), q)
            dc_scr[ch] += jnp.sum(dst, axis=-1, keepdims=True)

        for ii in range(nc):
            i = nc * g + ii
            q = _chunk(q_ref, i, ta)
            do = _chunk(do_ref, i, ta)
            for ch in range(0, ii + 1):
                update(ch, q, do, lse_ref[i], d_ref[i], ch == ii)

        def full_chunk(i, carry):
            q = _chunk(q_ref, i, ta)
            do = _chunk(do_ref, i, ta)
            for ch in range(nc):
                update(ch, q, do, lse_ref[i], d_ref[i], False)
            return carry

        lax.fori_loop(nc * (g + 1), n_chunks, full_chunk, 0)
        for ch in range(nc):
            rows = slice(ch * ta, (ch + 1) * ta)
            dk_ref[rows, :] = (dk_scr[ch] * scale).astype(BF16)
            dv_ref[rows, :] = dv_scr[ch].astype(BF16)
            dc_ref[rows, :] = -dc_scr[ch]

    tk = nc * ta
    chunks = pl.BlockSpec((None, n_chunks, 1, ta), lambda h, g: (h, 0, 0, 0))
    col = pl.BlockSpec((None, tk, 1), lambda h, g: (h, g, 0))
    tile = pl.BlockSpec((tk, HEAD_DIM), lambda h, g: (g, h))
    return pl.pallas_call(
        body, name=name, grid=(H, n_chunks // nc),
        in_specs=[pl.BlockSpec((None, tk, HEAD_DIM), lambda h, g: (1, g, h)),
                  pl.BlockSpec((None, tk, HEAD_DIM), lambda h, g: (2, g, h)),
                  pl.BlockSpec((None, T, HEAD_DIM), lambda h, g: (0, 0, h)),
                  pl.BlockSpec((T, HEAD_DIM), lambda h, g: (0, h)),
                  col, chunks, chunks],
        out_specs=[tile, tile, col],
        out_shape=[jax.ShapeDtypeStruct((T, W), BF16), jax.ShapeDtypeStruct((T, W), BF16),
                   jax.ShapeDtypeStruct((H, T, 1), F32)],
        scratch_shapes=[pltpu.VMEM((nc, ta, HEAD_DIM), F32), pltpu.VMEM((nc, ta, HEAD_DIM), F32),
                        pltpu.VMEM((nc, ta, 1), F32)],
        compiler_params=_params(("parallel", "arbitrary")),
    )(z7, z7, z7, dob, c_col, lse_chunks, d_chunks)


def _chunk_causal_mask():
    rows = lax.broadcasted_iota(jnp.int32, (SGU_LEN, SGU_LEN), 0)
    cols = lax.broadcasted_iota(jnp.int32, (SGU_LEN, SGU_LEN), 1)
    return (cols // CHUNK) <= (rows // CHUNK)


def _sgu_norm_mix(sv, lng_ref, lnb_ref, ws_ref, bs_ref, vn_scr, mixed_scr, vhat_scr=None):
    tm = sv.shape[0]
    vs = _gelu(sv)
    mask = _chunk_causal_mask()
    rstds = []
    for g in range(N_GROUPS):
        lanes = slice(g * GROUP_DIM, (g + 1) * GROUP_DIM)
        blk = vs[:, lanes]
        cen = blk - jnp.mean(blk, axis=-1, keepdims=True)
        rstd = lax.rsqrt(jnp.mean(cen * cen, axis=-1, keepdims=True) + LN_EPS)
        vhat = cen * rstd
        rstds.append(rstd)
        if vhat_scr is not None:
            vhat_scr[:, lanes] = vhat
        vn_scr[:, lanes] = (vhat * lng_ref[:, lanes] + lnb_ref[:, lanes]).astype(BF16)
        wm = jnp.where(mask, ws_ref[g], 0.0).astype(BF16)
        for w in range(tm // SGU_LEN):
            rows = slice(w * SGU_LEN, (w + 1) * SGU_LEN)
            mixed_scr[rows, lanes] = _mm(wm, vn_scr[rows, lanes]) + bs_ref[g]
    return rstds


def _mix_out_fwd(z7, o_a, x1, lng, lnb, ws, bs, w_out, g_post, name):
    _, T, W = z7.shape
    D = x1.shape[1]
    tm = _blk(T, 256)

    def body(u_ref, sv_ref, ga_ref, gb_ref, oa_ref, x1_ref, lng_ref, lnb_ref, ws_ref, bs_ref, wo_ref, gp_ref,
             x2_ref, p_ref, mb_ref, vn_scr, mixed_scr):
        _sgu_norm_mix(sv_ref[...].astype(F32), lng_ref, lnb_ref, ws_ref, bs_ref, vn_scr, mixed_scr)
        o_b = _gelu(u_ref[...].astype(F32)) * mixed_scr[...]
        merged = (jax.nn.sigmoid(ga_ref[...].astype(F32)) * oa_ref[...]
                  + jax.nn.sigmoid(gb_ref[...].astype(F32)) * o_b).astype(BF16)
        mb_ref[...] = merged
        p = _mm(merged, wo_ref[...])
        p_ref[...] = p
        x2_ref[...] = x1_ref[...] + p * _rms_scale(p) * gp_ref[...]

    def seg(idx):
        return pl.BlockSpec((None, tm, W), lambda i, idx=idx: (idx, i, 0))

    row = pl.BlockSpec((tm, D), lambda i: (i, 0))
    vec = pl.BlockSpec((1, D), lambda i: (0, 0))
    return pl.pallas_call(
        body, name=name, grid=(T // tm,),
        in_specs=[seg(3), seg(4), seg(5), seg(6), row, row, vec, vec,
                  pl.BlockSpec((N_GROUPS, SGU_LEN, SGU_LEN), lambda i: (0, 0, 0)),
                  pl.BlockSpec((N_GROUPS, SGU_LEN, 1), lambda i: (0, 0, 0)),
                  pl.BlockSpec((D, D), lambda i: (0, 0)), vec],
        out_specs=[row, row, row],
        out_shape=[jax.ShapeDtypeStruct((T, D), F32), jax.ShapeDtypeStruct((T, D), F32),
                   jax.ShapeDtypeStruct((T, D), BF16)],
        scratch_shapes=[pltpu.VMEM((tm, W), BF16), pltpu.VMEM((tm, W), F32)],
        compiler_params=_params(("parallel",)),
    )(z7, z7, z7, z7, o_a, x1, lng, lnb, ws, bs, w_out, g_post)


def _mix_out_bwd(dx2, p, z7, o_a, lng, lnb, ws, bs, w_out, g_post, name, dep=None):
    _, T, W = z7.shape
    D = dx2.shape[1]
    tm = _blk(T, 256)
    n_w = tm // SGU_LEN

    def body(dx2_ref, p_ref, u_ref, sv_ref, ga_ref, gb_ref, oa_ref, lng_ref, lnb_ref, ws_ref, bs_ref, wo_ref, gp_ref, _,
             dpb_ref, dob_ref, dvec_ref, dz_ref, dgp_ref, dlng_ref, dlnb_ref, dws_ref, dbs_ref,
             vn_scr, mixed_scr, vhat_scr, dmix_scr, dvn_scr):
        @pl.when(pl.program_id(0) == 0)
        def _():
            dgp_ref[...] = jnp.zeros_like(dgp_ref)
            dlng_ref[...] = jnp.zeros_like(dlng_ref)
            dlnb_ref[...] = jnp.zeros_like(dlnb_ref)
            dws_ref[...] = jnp.zeros_like(dws_ref)
            dbs_ref[...] = jnp.zeros_like(dbs_ref)

        pv = p_ref[...]
        s = _rms_scale(pv)
        n = pv * s
        dn = dx2_ref[...]
        dgp_ref[...] += jnp.sum(dn * n, axis=0, keepdims=True)
        dpb = _rms_bwd(dn, n, s, gp_ref[...]).astype(BF16)
        dpb_ref[...] = dpb
        dmerged = _mm_nt(dpb, wo_ref[...])

        sv = sv_ref[...].astype(F32)
        rstds = _sgu_norm_mix(sv, lng_ref, lnb_ref, ws_ref, bs_ref, vn_scr, mixed_scr, vhat_scr)
        u_pre = u_ref[...].astype(F32)
        u = _gelu(u_pre)
        mixed = mixed_scr[...]
        sa = jax.nn.sigmoid(ga_ref[...].astype(F32))
        sb = jax.nn.sigmoid(gb_ref[...].astype(F32))
        oa = oa_ref[...]
        do_a = (dmerged * sa).astype(BF16)
        dob_ref[...] = do_a
        prod = do_a.astype(F32) * oa
        for h in range(N_HEADS):
            dvec_ref[h] = jnp.sum(prod[:, h * HEAD_DIM:(h + 1) * HEAD_DIM], axis=-1, keepdims=True)
        dz_ref[2] = (dmerged * oa * (sa * (1.0 - sa))).astype(BF16)
        dz_ref[3] = (dmerged * (u * mixed) * (sb * (1.0 - sb))).astype(BF16)
        do_b = dmerged * sb
        dz_ref[0] = (do_b * mixed * _gelu_grad(u_pre)).astype(BF16)
        dmix_scr[...] = do_b * u

        mask = _chunk_causal_mask()
        for g in range(N_GROUPS):
            lanes = slice(g * GROUP_DIM, (g + 1) * GROUP_DIM)
            wm = jnp.where(mask, ws_ref[g], 0.0).astype(BF16)
            dws = jnp.zeros((SGU_LEN, SGU_LEN), F32)
            dbs = jnp.zeros((SGU_LEN, 1), F32)
            for w in range(n_w):
                rows = slice(w * SGU_LEN, (w + 1) * SGU_LEN)
                dmix = dmix_scr[rows, lanes]
                dmix_b = dmix.astype(BF16)
                dvn_scr[rows, lanes] = _mm_tn(wm, dmix_b)
                dws = dws + _mm_nt(dmix_b, vn_scr[rows, lanes])
                dbs = dbs + jnp.sum(dmix, axis=-1, keepdims=True)
            dws_ref[g] += jnp.where(mask, dws, 0.0)
            dbs_ref[g] += dbs
            dvn = dvn_scr[:, lanes]
            vhat = vhat_scr[:, lanes]
            dlng_ref[:, lanes] += jnp.sum(dvn * vhat, axis=0, keepdims=True)
            dlnb_ref[:, lanes] += jnp.sum(dvn, axis=0, keepdims=True)
            dvh = dvn * lng_ref[:, lanes]
            dvs = rstds[g] * (dvh - jnp.mean(dvh, axis=-1, keepdims=True)
                              - vhat * jnp.mean(dvh * vhat, axis=-1, keepdims=True))
            dvn_scr[:, lanes] = dvs
        dz_ref[1] = (dvn_scr[...] * _gelu_grad(sv)).astype(BF16)

    def seg(idx):
        return pl.BlockSpec((None, tm, W), lambda i, idx=idx: (idx, i, 0))

    row = pl.BlockSpec((tm, D), lambda i: (i, 0))
    vec = pl.BlockSpec((1, D), lambda i: (0, 0))
    ws_spec = pl.BlockSpec((N_GROUPS, SGU_LEN, SGU_LEN), lambda i: (0, 0, 0))
    bs_spec = pl.BlockSpec((N_GROUPS, SGU_LEN, 1), lambda i: (0, 0, 0))
    return pl.pallas_call(
        body, name=name, grid=(T // tm,),
        in_specs=[row, row, seg(3), seg(4), seg(5), seg(6), row, vec, vec, ws_spec, bs_spec,
                  pl.BlockSpec((D, D), lambda i: (0, 0)), vec, ANY],
        out_specs=[row, row, pl.BlockSpec((N_HEADS, tm, 1), lambda i: (0, i, 0)),
                   pl.BlockSpec((4, tm, W), lambda i: (0, i, 0)), vec, vec, vec, ws_spec, bs_spec],
        out_shape=[jax.ShapeDtypeStruct((T, D), BF16), jax.ShapeDtypeStruct((T, W), BF16),
                   jax.ShapeDtypeStruct((N_HEADS, T, 1), F32), jax.ShapeDtypeStruct((4, T, W), BF16),
                   jax.ShapeDtypeStruct((1, D), F32), jax.ShapeDtypeStruct((1, D), F32),
                   jax.ShapeDtypeStruct((1, D), F32),
                   jax.ShapeDtypeStruct((N_GROUPS, SGU_LEN, SGU_LEN), F32),
                   jax.ShapeDtypeStruct((N_GROUPS, SGU_LEN, 1), F32)],
        scratch_shapes=[pltpu.VMEM((tm, W), BF16), pltpu.VMEM((tm, W), F32), pltpu.VMEM((tm, W), F32),
                        pltpu.VMEM((tm, W), F32), pltpu.VMEM((tm, W), F32)],
        compiler_params=_params(("arbitrary",)),
    )(dx2, p, z7, z7, z7, z7, o_a, lng, lnb, ws, bs, w_out, g_post, _after(dep))


def _loss_head(y, target, name):
    T, D = y.shape
    tm = _blk(T, 1024)
    n_i = T // tm

    def body(y_ref, t_ref, dy_ref, loss_ref, acc_scr):
        i = pl.program_id(0)

        @pl.when(i == 0)
        def _():
            acc_scr[...] = jnp.zeros_like(acc_scr)

        e = y_ref[...] - t_ref[...]
        dy_ref[...] = e * np.float32(1.0 / D)
        acc_scr[...] += jnp.sum(e * e, axis=0, keepdims=True)

        @pl.when(i == n_i - 1)
        def _():
            total = jnp.sum(acc_scr[...], axis=-1, keepdims=True) * np.float32(0.5 / D)
            loss_ref[...] = jnp.broadcast_to(total, loss_ref.shape)

    row = pl.BlockSpec((tm, D), lambda i: (i, 0))
    return pl.pallas_call(
        body, name=name, grid=(n_i,),
        in_specs=[row, row],
        out_specs=[row, pl.BlockSpec((1, LANES), lambda i: (0, 0))],
        out_shape=[jax.ShapeDtypeStruct((T, D), F32), jax.ShapeDtypeStruct((1, LANES), F32)],
        scratch_shapes=[pltpu.VMEM((1, D), F32)],
        compiler_params=_params(("arbitrary",)),
    )(y, target)


def _adamw_math(w, g, m, v):
    m_new = ADAM_B1 * m + (1.0 - ADAM_B1) * g
    v_new = ADAM_B2 * v + (1.0 - ADAM_B2) * (g * g)
    m_hat = m_new / np.float32(1.0 - ADAM_B1 ** ADAM_STEP)
    v_hat = v_new / np.float32(1.0 - ADAM_B2 ** ADAM_STEP)
    delta = -ADAM_LR * (m_hat / (jnp.sqrt(v_hat) + ADAM_EPS) + ADAM_WD * w)
    return delta, m_new, v_new


def _sum_adamw(parts, w, m, v, name, dep=None):
    n, R, C = parts.shape
    tr = _blk(R, 128)

    def body(p_ref, w_ref, m_ref, v_ref, _, g_ref, d_ref, mo_ref, vo_ref):
        g = p_ref[0].astype(F32)
        for s in range(1, n):
            g = g + p_ref[s].astype(F32)
        delta, m_new, v_new = _adamw_math(w_ref[...], g, m_ref[...], v_ref[...])
        g_ref[...] = g
        d_ref[...] = delta
        mo_ref[...] = m_new
        vo_ref[...] = v_new

    row = pl.BlockSpec((tr, C), lambda i: (i, 0))
    shp = jax.ShapeDtypeStruct((R, C), F32)
    return pl.pallas_call(
        body, name=name, grid=(R // tr,),
        in_specs=[pl.BlockSpec((n, tr, C), lambda i: (0, i, 0)), row, row, row, ANY],
        out_specs=[row, row, row, row], out_shape=[shp, shp, shp, shp],
        compiler_params=_params(("parallel",)),
    )(parts, w, m, v, _after(dep))


def _adamw(g, w, m, v, name):
    R, C = g.shape
    tr = _blk(R, 128)

    def body(g_ref, w_ref, m_ref, v_ref, d_ref, mo_ref, vo_ref):
        delta, m_new, v_new = _adamw_math(w_ref[...], g_ref[...], m_ref[...], v_ref[...])
        d_ref[...] = delta
        mo_ref[...] = m_new
        vo_ref[...] = v_new

    row = pl.BlockSpec((tr, C), lambda i: (i, 0))
    shp = jax.ShapeDtypeStruct((R, C), F32)
    return pl.pallas_call(
        body, name=name, grid=(R // tr,),
        in_specs=[row, row, row, row], out_specs=[row, row, row], out_shape=[shp, shp, shp],
        compiler_params=_params(("parallel",)),
    )(g, w, m, v)


def _position():
    return lax.axis_index("x"), lax.axis_index("y"), lax.axis_index("c")


def _slot(px, py, pc):
    return 4 * px + 2 * py + pc


def _all_gather(shards, name):
    n = len(shards)

    def body(*refs):
        ins, outs = refs[:n], refs[n:2 * n]
        send_sems, recv_sems, local_sems = refs[2 * n:]
        x, y, c = _position()
        me, sibling = (x, y, c), (x, y, 1 - c)
        chips = [(1 - x, y), (x, 1 - y), (1 - x, 1 - y)]

        def copy(a, k, block, to, src=None):
            dst = outs[a].at[_slot(*block)]
            return pltpu.make_async_remote_copy(
                src_ref=dst if src is None else src, dst_ref=dst,
                send_sem=send_sems.at[a, k], recv_sem=recv_sems.at[a, k],
                device_id=to, device_id_type=MESH)

        mine = [pltpu.make_async_copy(ins[a], outs[a].at[_slot(*me)], local_sems.at[a]) for a in range(n)]
        for cp in mine:
            cp.start()
        first = []
        for a in range(n):
            first.append(copy(a, 0, me, sibling, src=ins[a]))
            first += [copy(a, 1 + j, me, (*chip, c), src=ins[a]) for j, chip in enumerate(chips)]
        for cp in first:
            cp.start()
        passed = []
        for j, chip in enumerate(chips):
            for a in range(n):
                copy(a, 1 + j, (*chip, c), me).wait_recv()
                fwd = copy(a, 4 + j, (*chip, c), sibling)
                fwd.start()
                passed.append(fwd)
        for a in range(n):
            copy(a, 0, sibling, me).wait_recv()
            for j, chip in enumerate(chips):
                copy(a, 4 + j, (*chip, 1 - c), me).wait_recv()
        for cp in first + passed:
            cp.wait_send()
        for cp in mine:
            cp.wait()

    return pl.pallas_call(
        body, name=name,
        in_specs=[ANY] * n, out_specs=[ANY] * n,
        out_shape=[jax.ShapeDtypeStruct((N_DEV,) + s.shape, s.dtype) for s in shards],
        scratch_shapes=[pltpu.SemaphoreType.DMA((n, 7)), pltpu.SemaphoreType.DMA((n, 7)),
                        pltpu.SemaphoreType.DMA((n,))],
    )(*shards)


def _peer(x, y, c, k):
    return (1 - x if k & 4 else x, 1 - y if k & 2 else y, 1 - c if k & 1 else c)


def _exchange(parts, name):
    n = len(parts)

    def body(*refs):
        ins, outs = refs[:n], refs[n:2 * n]
        send_sems, recv_sems, local_sems = refs[2 * n:]
        x, y, c = _position()
        me = _slot(x, y, c)
        mine = [pltpu.make_async_copy(ins[a].at[me], outs[a].at[me], local_sems.at[a]) for a in range(n)]
        for cp in mine:
            cp.start()
        sends = []
        for k in range(1, N_DEV):
            to = _peer(x, y, c, k)
            for a in range(n):
                cp = pltpu.make_async_remote_copy(
                    src_ref=ins[a].at[_slot(*to)], dst_ref=outs[a].at[me],
                    send_sem=send_sems.at[a, k - 1], recv_sem=recv_sems.at[a, k - 1],
                    device_id=to, device_id_type=MESH)
                cp.start()
                sends.append(cp)
        for k in range(1, N_DEV):
            frm = _peer(x, y, c, k)
            for a in range(n):
                pltpu.make_async_remote_copy(
                    src_ref=ins[a].at[_slot(*frm)], dst_ref=outs[a].at[_slot(*frm)],
                    send_sem=send_sems.at[a, k - 1], recv_sem=recv_sems.at[a, k - 1],
                    device_id=frm, device_id_type=MESH).wait_recv()
        for cp in sends:
            cp.wait_send()
        for cp in mine:
            cp.wait()

    return pl.pallas_call(
        body, name=name,
        in_specs=[ANY] * n, out_specs=[ANY] * n,
        out_shape=[jax.ShapeDtypeStruct(p.shape, p.dtype) for p in parts],
        scratch_shapes=[pltpu.SemaphoreType.DMA((n, 7)), pltpu.SemaphoreType.DMA((n, 7)),
                        pltpu.SemaphoreType.DMA((n,))],
    )(*parts)


HBM_SPEC = pl.BlockSpec(memory_space=pltpu.HBM)
SEM_SPEC = pl.BlockSpec(memory_space=pltpu.SEMAPHORE)
SIDE_EFFECT = pltpu.SideEffectType.DATAFLOW_SIDE_EFFECTING


def _remote_copies(src_refs, land_refs, send_sems, recv_sems, gather, outgoing):
    x, y, c = _position()
    me = _slot(x, y, c)
    copies = []
    for k in range(1, N_DEV):
        peer = _peer(x, y, c, k)
        for a in range(len(src_refs)):
            src = src_refs[a] if gather else src_refs[a].at[_slot(*peer)]
            dst = land_refs[a].at[me if outgoing else _slot(*peer)]
            sem = a * (N_DEV - 1) + k - 1
            copies.append(pltpu.make_async_remote_copy(
                src_ref=src, dst_ref=dst, send_sem=send_sems.at[sem], recv_sem=recv_sems.at[sem],
                device_id=peer, device_id_type=MESH))
    return copies


def _remote_start(srcs, after, name, gather):
    n = len(srcs)
    lands = [jax.ShapeDtypeStruct(((N_DEV,) + s.shape) if gather else s.shape, s.dtype) for s in srcs]

    def body(*refs):
        src_refs, land_refs = refs[:n], refs[n:2 * n]
        send_sems, recv_sems = refs[2 * n + 1], refs[2 * n + 2]
        token, local_sems = refs[4 * n + 3], refs[4 * n + 4]
        x, y, c = _position()
        me = _slot(x, y, c)
        mine = [pltpu.make_async_copy(src_refs[a] if gather else src_refs[a].at[me], land_refs[a].at[me],
                                      local_sems.at[a]) for a in range(n)]
        for cp in mine:
            cp.start()
        for cp in _remote_copies(src_refs, land_refs, send_sems, recv_sems, gather, outgoing=True):
            cp.start()
        for cp in mine:
            cp.wait()
        token[...] = jnp.zeros_like(token)

    sem_shape = pltpu.SemaphoreType.DMA((n * (N_DEV - 1),))
    outs = pl.pallas_call(
        body, name=name,
        out_shape=(sem_shape, sem_shape, *[pltpu.HBM(s.shape, s.dtype) for s in srcs],
                   *[pltpu.HBM(l.shape, l.dtype) for l in lands], jax.ShapeDtypeStruct((8, LANES), F32)),
        in_specs=[HBM_SPEC] * (2 * n) + [ANY],
        out_specs=(SEM_SPEC, SEM_SPEC, *([HBM_SPEC] * (2 * n)), pl.BlockSpec(memory_space=pltpu.VMEM)),
        input_output_aliases={a: 2 + a for a in range(2 * n)},
        scratch_shapes=[pltpu.SemaphoreType.DMA((n,))],
        compiler_params=pltpu.CompilerParams(has_side_effects=SIDE_EFFECT),
    )(*[pltpu.with_memory_space_constraint(s, pltpu.HBM) for s in srcs],
      *[pltpu.with_memory_space_constraint(lax.empty(l.shape, l.dtype), pltpu.HBM) for l in lands], after)
    return dict(send=outs[0], recv=outs[1], srcs=outs[2:2 + n], lands=outs[2 + n:2 + 2 * n], token=outs[-1],
                gather=gather)


def _remote_wait(flight, after, name):
    n = len(flight["srcs"])
    gather = flight["gather"]

    def body(*refs):
        src_refs, land_refs = refs[:n], refs[n:2 * n]
        send_sems, recv_sems = refs[2 * n], refs[2 * n + 1]
        for cp in _remote_copies(src_refs, land_refs, send_sems, recv_sems, gather, outgoing=False):
            cp.wait_send()
            cp.wait_recv()

    both = list(flight["srcs"]) + list(flight["lands"])
    outs = pl.pallas_call(
        body, name=name,
        out_shape=tuple(pltpu.HBM(a.shape, a.dtype) for a in both),
        in_specs=[HBM_SPEC] * (2 * n) + [SEM_SPEC, SEM_SPEC, ANY],
        out_specs=tuple([HBM_SPEC] * (2 * n)),
        input_output_aliases={a: a for a in range(2 * n)},
        compiler_params=pltpu.CompilerParams(has_side_effects=SIDE_EFFECT),
    )(*both, flight["send"], flight["recv"], after)
    return list(outs[n:])


def _sequencer_exchange(srcs, name, gather, collective_id):
    n = len(srcs)
    hbm = pltpu.MemorySpace.HBM
    src_refs = [jax.new_ref(s, memory_space=hbm) for s in srcs]
    land_refs = [jax.empty_ref(jax.ShapeDtypeStruct(((N_DEV,) + s.shape) if gather else s.shape, s.dtype),
                               memory_space=hbm) for s in srcs]
    n_sems = n * (N_DEV - 1)
    block_bytes = sum(s.size * s.dtype.itemsize // (1 if gather else N_DEV) for s in srcs)
    cost = pl.CostEstimate(flops=0, transcendentals=0, bytes_accessed=2 * N_DEV * block_bytes,
                           remote_bytes_transferred=(N_DEV - 1) * block_bytes)

    @pl.kernel(mesh=plsc.ScalarSubcoreMesh(axis_name="sequencer", num_cores=1), name=name,
               scratch_types=(pltpu.SemaphoreType.DMA((n_sems,)), pltpu.SemaphoreType.DMA((n_sems,)),
                              pltpu.SemaphoreType.DMA((n,))),
               cost_estimate=cost,
               compiler_params=pltpu.CompilerParams(collective_id=collective_id))
    def launch(send_sems, recv_sems, local_sems):
        x, y, c = _position()
        me = _slot(x, y, c)
        barrier = pltpu.get_barrier_semaphore()
        for k in range(1, N_DEV):
            pl.semaphore_signal(barrier, inc=1, device_id=_peer(x, y, c, k), device_id_type=MESH)
        pl.semaphore_wait(barrier, N_DEV - 1)
        mine = [pltpu.make_async_copy(src_refs[a] if gather else src_refs[a].at[me], land_refs[a].at[me],
                                      local_sems.at[a]) for a in range(n)]
        for cp in mine:
            cp.start()
        sends = _remote_copies(src_refs, land_refs, send_sems, recv_sems, gather, outgoing=True)
        for cp in sends:
            cp.start()
        for cp in _remote_copies(src_refs, land_refs, send_sems, recv_sems, gather, outgoing=False):
            cp.wait_recv()
        for cp in sends:
            cp.wait_send()
        for cp in mine:
            cp.wait()

    launch()
    return [r[...] for r in land_refs]


def _all_reduce_small(blob, name):
    R, C = blob.shape

    def body(in_ref, out_ref, gath, send_sems, recv_sems):
        x, y, c = _position()
        me = _slot(x, y, c)
        gath[me] = in_ref[...]
        sends = []
        for k in range(1, N_DEV):
            to = _peer(x, y, c, k)
            cp = pltpu.make_async_remote_copy(
                src_ref=in_ref, dst_ref=gath.at[me],
                send_sem=send_sems.at[k - 1], recv_sem=recv_sems.at[k - 1],
                device_id=to, device_id_type=MESH)
            cp.start()
            sends.append(cp)
        for k in range(1, N_DEV):
            frm = _peer(x, y, c, k)
            pltpu.make_async_remote_copy(
                src_ref=in_ref, dst_ref=gath.at[_slot(*frm)],
                send_sem=send_sems.at[k - 1], recv_sem=recv_sems.at[k - 1],
                device_id=frm, device_id_type=MESH).wait_recv()
        for cp in sends:
            cp.wait_send()
        total = gath[0]
        for s in range(1, N_DEV):
            total = total + gath[s]
        out_ref[...] = total

    return pl.pallas_call(
        body, name=name,
        in_specs=[pl.BlockSpec(memory_space=pltpu.VMEM)],
        out_specs=pl.BlockSpec(memory_space=pltpu.VMEM),
        out_shape=jax.ShapeDtypeStruct((R, C), F32),
        scratch_shapes=[pltpu.VMEM((N_DEV, R, C), F32), pltpu.SemaphoreType.DMA((7,)),
                        pltpu.SemaphoreType.DMA((7,))],
        compiler_params=pltpu.CompilerParams(vmem_limit_bytes=VMEM_LIMIT),
    )(blob)


SMALL_VECS = ("ffn1_pre_g", "ffn1_post_g", "mix_pre_g", "sgu_ln_g", "sgu_ln_b", "mix_post_g", "ffn2_pre_g",
              "ffn2_post_g")
ROW_BS = len(SMALL_VECS)
ROW_BF = ROW_BS + 1
ROW_LOSS = ROW_BF + 1
ROW_WS = 16
BLOB_ROWS = ROW_WS + SGU_LEN


def _pack_small(vals, D, loss_row=None):
    rows = [vals[n].reshape(1, D) for n in SMALL_VECS]
    rows.append(vals["sgu_b_s"].reshape(1, D))
    rows.append(jnp.pad(vals["b_forget"].reshape(1, N_HEADS), ((0, 0), (0, D - N_HEADS))))
    rows.append(jnp.zeros((1, D), F32) if loss_row is None else loss_row)
    rows.append(jnp.zeros((ROW_WS - ROW_LOSS - 1, D), F32))
    rows.append(vals["sgu_w_s"].reshape(SGU_LEN, D))
    return jnp.concatenate(rows, axis=0)


def _unpack_small(blob, D):
    out = {n: blob[r:r + 1] for r, n in enumerate(SMALL_VECS)}
    out["sgu_b_s"] = blob[ROW_BS].reshape(1, N_GROUPS, SGU_LEN)
    out["b_forget"] = blob[ROW_BF, :N_HEADS].reshape(1, N_HEADS)
    out["sgu_w_s"] = blob[ROW_WS:].reshape(1, N_GROUPS, SGU_LEN, SGU_LEN)
    return out


WEIGHT_NAMES = ("ffn1_pre_g", "ffn1_w_gate", "ffn1_w_up", "ffn1_w_down", "ffn1_post_g", "mix_pre_g", "w_in",
                "b_forget", "sgu_ln_g", "sgu_ln_b", "sgu_w_s", "sgu_b_s", "w_out", "mix_post_g", "ffn2_pre_g",
                "ffn2_w_gate", "ffn2_w_up", "ffn2_w_down", "ffn2_post_g")
BIG_NAMES = ("ffn1_w_gate", "ffn1_w_up", "ffn1_w_down", "w_in", "w_out", "ffn2_w_gate", "ffn2_w_up", "ffn2_w_down")
WEIGHT_GROUPS = {"ffn1": ("ffn1_w_gate", "ffn1_w_up", "ffn1_w_down"), "mix": ("w_in", "w_out"),
                 "ffn2": ("ffn2_w_gate", "ffn2_w_up", "ffn2_w_down")}
GRAD_GROUPS = (("ffn2_w_gate", "ffn2_w_up", "ffn2_w_down"), ("w_in", "w_out"), ("ffn1_w_down",), ("ffn1_w_gate",),
               ("ffn1_w_up",))


def _local_step(x, target, small, fetch, emit, consume):
    T, D = x.shape
    W = N_HEADS * HEAD_DIM
    vec = lambda n: small[n].reshape(1, D)
    big = dict(fetch("ffn1", x))

    x1, y1, dgf1, silu1, act1 = _ffn_fwd(x, vec("ffn1_pre_g"), big["ffn1_w_gate"], big["ffn1_w_up"], big["ffn1_w_down"],
                                  vec("ffn1_post_g"), "ffn1_fwd")

    big.update(fetch("mix", x1))
    w_in_all = big["w_in"]
    in_width = N_DEV * w_in_all.shape[2]
    w_in = w_in_all.transpose(1, 0, 2).reshape(D, in_width)
    col_f = 3 * W
    col_u = col_f + N_HEADS
    seg_starts = (0, W, 2 * W, col_u, col_u + W, col_u + 2 * W, col_u + 3 * W)
    w7 = jnp.stack([w_in[:, s:s + W] for s in seg_starts])
    wf = jnp.pad(w_in[:, col_f:col_u], ((0, 0), (0, LANES - N_HEADS)))
    w_out = big["w_out"].reshape(D, D)
    b_pad = jnp.pad(small["b_forget"].reshape(1, N_HEADS), ((0, 0), (0, LANES - N_HEADS)))
    lng, lnb = vec("sgu_ln_g"), vec("sgu_ln_b")
    ws = small["sgu_w_s"].reshape(N_GROUPS, SGU_LEN, SGU_LEN)
    bs = small["sgu_b_s"].reshape(N_GROUPS, SGU_LEN, 1)

    z7, f_logit, h2b = _mix_in_fwd(x1, vec("mix_pre_g"), w7, wf, "mix_in_fwd")
    c = _forget_cumsum(f_logit, b_pad, "forget_cumsum")
    c_heads = c[:, :N_HEADS].T
    ta, _, n_chunks = _attn_geometry(T)
    c_chunks = c_heads.reshape(N_HEADS, n_chunks, 1, ta)
    c_col = c_heads[:, :, None]
    vt = z7[2].reshape(n_chunks, ta, N_HEADS, HEAD_DIM).transpose(2, 0, 3, 1)
    c_rep = jnp.broadcast_to(c_col, (N_HEADS, T, LANES))
    o_a, lse_chunks = _attn_fwd_keys_on_rows(z7, vt, c_rep, "attn_fwd")
    lse = lse_chunks.reshape(N_HEADS, T, 1)
    x2, p, merged_b = _mix_out_fwd(z7, o_a, x1, lng, lnb, ws, bs, w_out, vec("mix_post_g"), "mix_out_fwd")
    big.update(fetch("ffn2", x2))
    x3, y2, dgf2, silu2, act2 = _ffn_fwd(x2, vec("ffn2_pre_g"), big["ffn2_w_gate"], big["ffn2_w_up"], big["ffn2_w_down"],
                                  vec("ffn2_post_g"), "ffn2_fwd")
    dy, loss_lanes = _loss_head(x3, target, "loss_head")

    grads_small = {}

    dx2, h3b, dy2b, dgate2, dup2, dgpre, dgpost = _ffn_bwd(
        dy, x2, y2, dgf2, silu2, vec("ffn2_pre_g"), big["ffn2_w_gate"], big["ffn2_w_up"], big["ffn2_w_down"],
        vec("ffn2_post_g"), "ffn2_bwd")
    grads_small["ffn2_pre_g"] = jnp.sum(dgpre, axis=0)
    grads_small["ffn2_post_g"] = jnp.sum(dgpost, axis=0)
    dep = emit("ffn2_w_gate", _wgrad(h3b, dgate2, "ffn2_wgrad_gate", shard_cols=True))
    dep = emit("ffn2_w_up", _wgrad(h3b, dup2, "ffn2_wgrad_up", shard_cols=True, dep=dep))
    dep = emit("ffn2_w_down", _wgrad(act2, dy2b, "ffn2_wgrad_down", dep=dep).reshape(big["ffn2_w_down"].shape))

    dpb, dob, dvec, dz4, dgp, dlng, dlnb, dws, dbs = _mix_out_bwd(
        dx2, p, z7, o_a, lng, lnb, ws, bs, w_out, vec("mix_post_g"), "mix_out_bwd", dep=dep)
    grads_small["mix_post_g"] = dgp
    grads_small["sgu_ln_g"] = dlng
    grads_small["sgu_ln_b"] = dlnb
    grads_small["sgu_w_s"] = dws
    grads_small["sgu_b_s"] = dbs
    d_chunks = dvec.reshape(N_HEADS, n_chunks, 1, ta)
    kt = z7[1].reshape(n_chunks, ta, N_HEADS, HEAD_DIM).transpose(2, 0, 3, 1)
    dk, dv, dc, dq, dc_q = _attn_bwd_fused(z7, kt, dob, c_rep, lse_chunks, d_chunks, "attn_bwd")
    dc_pad = jnp.pad((dc + dc_q).reshape(N_HEADS, T).T, ((0, 0), (0, LANES - N_HEADS)))
    dfb, dbf = _forget_bwd(dc_pad, f_logit, b_pad, "forget_bwd")
    grads_small["b_forget"] = dbf[:, :N_HEADS]
    segs = [(dq, None), (dk, None), (dv, None), (dz4, 0), (dz4, 1), (dz4, 2), (dz4, 3)]
    dep = consume(("ffn2_w_gate", "ffn2_w_up", "ffn2_w_down"))
    dx1, dgm = _mix_in_bwd(dx2, x1, vec("mix_pre_g"), segs, dfb, w7, wf, "mix_in_bwd", dep=dep)
    grads_small["mix_pre_g"] = jnp.sum(dgm, axis=0)
    dw_seg, dep = [], dx1
    for q, (sm, idx) in enumerate(segs):
        dw_seg.append(_wgrad(h2b, sm, "w_in_wgrad_%d" % q, dep=dep, y_index=idx))
        dep = dw_seg[-1]
    dwf = _wgrad(h2b, dfb, "w_in_wgrad_f", dep=dep)
    dw_in = jnp.concatenate(dw_seg[:3] + [dwf[:, :N_HEADS]] + dw_seg[3:], axis=1)
    emit("w_in", dw_in.reshape(D, N_DEV, in_width // N_DEV).transpose(1, 0, 2))
    dep = emit("w_out", _wgrad(merged_b, dpb, "w_out_wgrad", dep=dwf).reshape(big["w_out"].shape))

    dx0, h1b, dy1b, dgate1, dup1, dgpre1, dgpost1 = _ffn_bwd(
        dx1, x, y1, dgf1, silu1, vec("ffn1_pre_g"), big["ffn1_w_gate"], big["ffn1_w_up"], big["ffn1_w_down"],
        vec("ffn1_post_g"), "ffn1_bwd", dep=dep)
    grads_small["ffn1_pre_g"] = jnp.sum(dgpre1, axis=0)
    grads_small["ffn1_post_g"] = jnp.sum(dgpost1, axis=0)
    dep = consume(("w_in", "w_out"))
    dep = emit("ffn1_w_down", _wgrad(act1, dy1b, "ffn1_wgrad_down", dep=dep).reshape(big["ffn1_w_down"].shape))
    dep = emit("ffn1_w_gate", _wgrad(h1b, dgate1, "ffn1_wgrad_gate", shard_cols=True, dep=dep))
    dep = emit("ffn1_w_up", _wgrad(h1b, dup1, "ffn1_wgrad_up", shard_cols=True, dep=dep))

    loss_row = jnp.pad(loss_lanes, ((0, 0), (0, D - LANES)))
    return loss_row, dx0, grads_small


def kernel(x, ffn1_pre_g, ffn1_w_gate, ffn1_w_up, ffn1_w_down, ffn1_post_g, mix_pre_g, w_in, b_forget, sgu_ln_g, sgu_ln_b, sgu_w_s, sgu_b_s, w_out, mix_post_g, ffn2_pre_g, ffn2_w_gate, ffn2_w_up, ffn2_w_down, ffn2_post_g, loss_target, m_ffn1_pre_g, m_ffn1_w_gate, m_ffn1_w_up, m_ffn1_w_down, m_ffn1_post_g, m_mix_pre_g, m_w_in, m_b_forget, m_sgu_ln_g, m_sgu_ln_b, m_sgu_w_s, m_sgu_b_s, m_w_out, m_mix_post_g, m_ffn2_pre_g, m_ffn2_w_gate, m_ffn2_w_up, m_ffn2_w_down, m_ffn2_post_g, v_ffn1_pre_g, v_ffn1_w_gate, v_ffn1_w_up, v_ffn1_w_down, v_ffn1_post_g, v_mix_pre_g, v_w_in, v_b_forget, v_sgu_ln_g, v_sgu_ln_b, v_sgu_w_s, v_sgu_b_s, v_w_out, v_mix_post_g, v_ffn2_pre_g, v_ffn2_w_gate, v_ffn2_w_up, v_ffn2_w_down, v_ffn2_post_g):
    weights = dict(zip(WEIGHT_NAMES, (ffn1_pre_g, ffn1_w_gate, ffn1_w_up, ffn1_w_down, ffn1_post_g, mix_pre_g, w_in,
                                      b_forget, sgu_ln_g, sgu_ln_b, sgu_w_s, sgu_b_s, w_out, mix_post_g, ffn2_pre_g,
                                      ffn2_w_gate, ffn2_w_up, ffn2_w_down, ffn2_post_g)))
    mom1 = dict(zip(WEIGHT_NAMES, (m_ffn1_pre_g, m_ffn1_w_gate, m_ffn1_w_up, m_ffn1_w_down, m_ffn1_post_g,
                                   m_mix_pre_g, m_w_in, m_b_forget, m_sgu_ln_g, m_sgu_ln_b, m_sgu_w_s, m_sgu_b_s,
                                   m_w_out, m_mix_post_g, m_ffn2_pre_g, m_ffn2_w_gate, m_ffn2_w_up, m_ffn2_w_down,
                                   m_ffn2_post_g)))
    mom2 = dict(zip(WEIGHT_NAMES, (v_ffn1_pre_g, v_ffn1_w_gate, v_ffn1_w_up, v_ffn1_w_down, v_ffn1_post_g,
                                   v_mix_pre_g, v_w_in, v_b_forget, v_sgu_ln_g, v_sgu_ln_b, v_sgu_w_s, v_sgu_b_s,
                                   v_w_out, v_mix_post_g, v_ffn2_pre_g, v_ffn2_w_gate, v_ffn2_w_up, v_ffn2_w_down,
                                   v_ffn2_post_g)))
    D = x.shape[-1]
    small_names = [n for n in WEIGHT_NAMES if n not in BIG_NAMES]

    small = {n: weights[n] for n in small_names}
    shard = lambda n: weights[n][0].astype(BF16)

    ffn1_full = _all_gather([shard(n) for n in WEIGHT_GROUPS["ffn1"]], "ffn1_all_gather")
    gathered = {}
    for cid, grp in ((1, "mix"), (2, "ffn2")):
        shards, _ = lax.optimization_barrier(([shard(n) for n in WEIGHT_GROUPS[grp]], ffn1_full[0]))
        gathered[grp] = _sequencer_exchange(shards, grp + "_gather", True, cid)

    def fetch(group, after):
        if group == "ffn1":
            return zip(WEIGHT_GROUPS[group], ffn1_full)
        arrived, _ = lax.optimization_barrier((gathered[group], after))
        return zip(WEIGHT_GROUPS[group], arrived)

    ready, received = {}, {}

    def emit(name, part):
        ready[name] = part
        for gi, group in enumerate(GRAD_GROUPS):
            if name == group[-1]:
                lands = _sequencer_exchange([ready[n] for n in group], name + "_grad_exchange", False, 3 + gi)
                received.update(zip(group, lands))
        return part

    out = {}

    def consume(names, dep=None):
        for n in names:
            g, d, m_new, v_new = _sum_adamw(received[n], weights[n][0], mom1[n][0], mom2[n][0], "adamw_" + n, dep=dep)
            out[n] = tuple(a[None] for a in (g, d, m_new, v_new))
            dep = g
        return dep

    loss_row, grad_x, grads_small = _local_step(x[0], loss_target[0], small, fetch, emit, consume)

    blobs = _sequencer_exchange([_pack_small(grads_small, D, loss_row)], "small_gather", True,
                                3 + len(GRAD_GROUPS))[0]
    blob, d_blob, m_blob, v_blob = _sum_adamw(
        blobs, _pack_small(small, D), _pack_small({n: mom1[n] for n in small_names}, D),
        _pack_small({n: mom2[n] for n in small_names}, D), "adamw_small")
    consume(("ffn1_w_down", "ffn1_w_gate", "ffn1_w_up"), dep=blob)
    unpacked = [_unpack_small(b, D) for b in (blob, d_blob, m_blob, v_blob)]
    for n in small_names:
        out[n] = tuple(u[n].reshape(weights[n].shape) for u in unpacked)

    loss = blob[ROW_LOSS, 0]
    result = [loss, grad_x[None]]
    for k in range(4):
        result += [out[n][k] for n in WEIGHT_NAMES]
    return tuple(result)
```

```python
import functools

import numpy as np
import jax
import jax.numpy as jnp
from jax import lax
from jax.experimental import pallas as pl
from jax.experimental.pallas import tpu as pltpu
from jax.experimental.pallas import tpu_sc as plsc

F32 = jnp.float32
BF16 = jnp.bfloat16

RMS_EPS = 1e-6
LN_EPS = 1e-5
HEAD_DIM = 128
N_HEADS = 8
GROUP_DIM = 128
N_GROUPS = 8
SGU_LEN = 128
CHUNK = 64
N_DEV = 8
LANES = 128
VMEM_LIMIT = 56 * 1024 * 1024
NEG_BIG = -1e30
LOG2E = np.float32(1.0 / np.log(2.0))

ADAM_LR = 0.001
ADAM_B1 = 0.9
ADAM_B2 = 0.999
ADAM_EPS = 1e-08
ADAM_WD = 0.01
ADAM_STEP = 10

MESH = pl.DeviceIdType.MESH
ANY = pl.BlockSpec(memory_space=pl.ANY)


def _blk(n, pref):
    return pref if (n >= pref and n % pref == 0) else n


def _mm(a, b):
    return jnp.dot(a, b, preferred_element_type=F32)


def _mm_nt(a, b):
    return lax.dot_general(a, b, (((1,), (1,)), ((), ())), preferred_element_type=F32)


def _mm_tn(a, b):
    return lax.dot_general(a, b, (((0,), (0,)), ((), ())), preferred_element_type=F32)


def _params(sem):
    return pltpu.CompilerParams(dimension_semantics=sem, vmem_limit_bytes=VMEM_LIMIT)


def _gelu(x):
    return 0.5 * x * (1.0 + lax.erf(x * np.float32(1.0 / np.sqrt(2.0))))


def _gelu_grad(x):
    cdf = 0.5 * (1.0 + lax.erf(x * np.float32(1.0 / np.sqrt(2.0))))
    return cdf + x * jnp.exp(-0.5 * x * x) * np.float32(1.0 / np.sqrt(2.0 * np.pi))


def _rms_scale(v):
    return lax.rsqrt(jnp.mean(v * v, axis=-1, keepdims=True) + RMS_EPS)


def _rms_bwd(dy, xhat, r, g):
    dxh = dy * g
    return r * (dxh - xhat * jnp.mean(dxh * xhat, axis=-1, keepdims=True))


def _ffn_rows(T):
    tm = _blk(T, 1024)
    th = _blk(tm, 512)
    return tm, th, tm // th


def _ffn_fwd(x, g_pre, wg, wu, wd, g_post, name):
    T, D = x.shape
    ns, _, fs = wg.shape
    tm, th, parts = _ffn_rows(T)

    def body(x_ref, gpre_ref, wg_ref, wu_ref, wd_ref, gpost_ref, xo_ref, y_ref, dgf_ref, silu_ref, act_ref,
             h_scr, acc_scr):
        j = pl.program_id(1)

        @pl.when(j == 0)
        def _():
            for r in range(parts):
                rows = slice(r * th, (r + 1) * th)
                xv = x_ref[rows, :]
                h_scr[rows, :] = (xv * _rms_scale(xv) * gpre_ref[...]).astype(BF16)
            acc_scr[...] = jnp.zeros_like(acc_scr)

        pre = []
        for r in range(parts):
            h = h_scr[r * th:(r + 1) * th, :]
            pre.append((_mm(h, wg_ref[...]), _mm(h, wu_ref[...])))
        for r in range(parts):
            rows = slice(r * th, (r + 1) * th)
            gg, uu = pre[r]
            sg = jax.nn.sigmoid(gg)
            silu = gg * sg
            act = (silu * uu).astype(BF16)
            dgf_ref[rows, :] = (uu * (sg * (1.0 + gg * (1.0 - sg)))).astype(BF16)
            silu_ref[rows, :] = silu.astype(BF16)
            act_ref[rows, :] = act
            acc_scr[rows, :] += _mm(act, wd_ref[...])

        @pl.when(j == ns - 1)
        def _():
            for r in range(parts):
                rows = slice(r * th, (r + 1) * th)
                y = acc_scr[rows, :]
                y_ref[rows, :] = y
                xo_ref[rows, :] = x_ref[rows, :] + 0.5 * (y * _rms_scale(y) * gpost_ref[...])

    row = pl.BlockSpec((tm, D), lambda i, j: (i, 0), pipeline_mode=pl.Buffered(1))
    vec = pl.BlockSpec((1, D), lambda i, j: (0, 0))
    return pl.pallas_call(
        body, name=name, grid=(T // tm, ns),
        in_specs=[row, vec,
                  pl.BlockSpec((None, D, fs), lambda i, j: (j, 0, 0)),
                  pl.BlockSpec((None, D, fs), lambda i, j: (j, 0, 0)),
                  pl.BlockSpec((None, fs, D), lambda i, j: (j, 0, 0)),
                  vec],
        out_specs=[row, row] + [pl.BlockSpec((tm, fs), lambda i, j: (i, j))] * 3,
        out_shape=[jax.ShapeDtypeStruct((T, D), F32), jax.ShapeDtypeStruct((T, D), F32)]
        + [jax.ShapeDtypeStruct((T, ns * fs), BF16)] * 3,
        scratch_shapes=[pltpu.VMEM((tm, D), BF16), pltpu.VMEM((tm, D), F32)],
        compiler_params=_params(("parallel", "arbitrary")),
    )(x, g_pre, wg, wu, wd, g_post)


def _after(dep):
    return jnp.zeros((8, LANES), F32) if dep is None else dep


def _ffn_bwd(dxo, x, y, dgf, silu, g_pre, wg, wu, wd, g_post, name, dep=None):
    T, D = x.shape
    ns, _, fs = wg.shape
    tm, th, parts = _ffn_rows(T)
    n_i = T // tm

    def body(dxo_ref, x_ref, y_ref, dgf_ref, silu_ref, gpre_ref, wg_ref, wu_ref, wd_ref, gpost_ref, _,
             dx_ref, hb_ref, dyb_ref, dgb_ref, dub_ref, dgpre_ref, dgpost_ref, dy_scr, acc_scr):
        j = pl.program_id(1)

        @pl.when(j == 0)
        def _():
            dgpost = jnp.zeros((1, D), F32)
            for r in range(parts):
                rows = slice(r * th, (r + 1) * th)
                yv = y_ref[rows, :]
                s = _rms_scale(yv)
                n = yv * s
                dn = 0.5 * dxo_ref[rows, :]
                dgpost = dgpost + jnp.sum(dn * n, axis=0, keepdims=True)
                dyv = _rms_bwd(dn, n, s, gpost_ref[...]).astype(BF16)
                dy_scr[rows, :] = dyv
                dyb_ref[rows, :] = dyv
                xv = x_ref[rows, :]
                hb_ref[rows, :] = (xv * _rms_scale(xv) * gpre_ref[...]).astype(BF16)
            dgpost_ref[...] = dgpost
            acc_scr[...] = jnp.zeros_like(acc_scr)

        das = [_mm_nt(dy_scr[r * th:(r + 1) * th, :], wd_ref[...]) for r in range(parts)]
        for r in range(parts):
            rows = slice(r * th, (r + 1) * th)
            dgate = (das[r] * dgf_ref[rows, :].astype(F32)).astype(BF16)
            dup = (das[r] * silu_ref[rows, :].astype(F32)).astype(BF16)
            dgb_ref[rows, :] = dgate
            dub_ref[rows, :] = dup
            acc_scr[rows, :] += _mm_nt(dgate, wg_ref[...]) + _mm_nt(dup, wu_ref[...])

        @pl.when(j == ns - 1)
        def _():
            dgpre = jnp.zeros((1, D), F32)
            for r in range(parts):
                rows = slice(r * th, (r + 1) * th)
                xv = x_ref[rows, :]
                rs = _rms_scale(xv)
                xhat = xv * rs
                dh = acc_scr[rows, :]
                dgpre = dgpre + jnp.sum(dh * xhat, axis=0, keepdims=True)
                dx_ref[rows, :] = _rms_bwd(dh, xhat, rs, gpre_ref[...]) + dxo_ref[rows, :]
            dgpre_ref[...] = dgpre

    row = pl.BlockSpec((tm, D), lambda i, j: (i, 0), pipeline_mode=pl.Buffered(1))
    vec = pl.BlockSpec((1, D), lambda i, j: (0, 0))
    wide = pl.BlockSpec((tm, fs), lambda i, j: (i, j))
    part = pl.BlockSpec((None, 1, D), lambda i, j: (i, 0, 0))
    F = ns * fs
    return pl.pallas_call(
        body, name=name, grid=(n_i, ns),
        in_specs=[row, row, row, wide, wide, vec,
                  pl.BlockSpec((None, D, fs), lambda i, j: (j, 0, 0)),
                  pl.BlockSpec((None, D, fs), lambda i, j: (j, 0, 0)),
                  pl.BlockSpec((None, fs, D), lambda i, j: (j, 0, 0)),
                  vec, ANY],
        out_specs=[row, row, row, wide, wide, part, part],
        out_shape=[jax.ShapeDtypeStruct((T, D), F32), jax.ShapeDtypeStruct((T, D), BF16),
                   jax.ShapeDtypeStruct((T, D), BF16), jax.ShapeDtypeStruct((T, F), BF16),
                   jax.ShapeDtypeStruct((T, F), BF16),
                   jax.ShapeDtypeStruct((n_i, 1, D), F32), jax.ShapeDtypeStruct((n_i, 1, D), F32)],
        scratch_shapes=[pltpu.VMEM((tm, D), BF16), pltpu.VMEM((tm, D), F32)],
        compiler_params=_params(("parallel", "arbitrary")),
    )(dxo, x, y, dgf, silu, g_pre, wg, wu, wd, g_post, _after(dep))


def _wgrad(xm, ym, name, shard_cols=False, dep=None, y_index=None):
    T, M = xm.shape
    N = ym.shape[-1]
    if y_index is None:
        y_spec = pl.BlockSpec((_blk(T, 512), N), lambda k: (k, 0))
    else:
        y_spec = pl.BlockSpec((None, _blk(T, 512), N), lambda k: (y_index, k, 0))
    assert M * N * 4 <= 16 * 1024 * 1024, (M, N)
    tk = _blk(T, 512)
    n_k = T // tk
    fs = N // N_DEV

    def body(x_ref, y_ref, _, o_ref, acc_scr):
        k = pl.program_id(0)

        @pl.when(k == 0)
        def _():
            acc_scr[...] = jnp.zeros_like(acc_scr)

        acc_scr[...] += _mm_tn(x_ref[...], y_ref[...])

        @pl.when(k == n_k - 1)
        def _():
            if shard_cols:
                for s in range(N_DEV):
                    o_ref[s] = acc_scr[:, s * fs:(s + 1) * fs].astype(BF16)
            else:
                o_ref[...] = acc_scr[...].astype(BF16)

    if shard_cols:
        out_spec = pl.BlockSpec((N_DEV, M, fs), lambda k: (0, 0, 0), pipeline_mode=pl.Buffered(1))
        out_shape = jax.ShapeDtypeStruct((N_DEV, M, fs), BF16)
    else:
        out_spec = pl.BlockSpec((M, N), lambda k: (0, 0), pipeline_mode=pl.Buffered(1))
        out_shape = jax.ShapeDtypeStruct((M, N), BF16)
    return pl.pallas_call(
        body, name=name, grid=(n_k,),
        in_specs=[pl.BlockSpec((tk, M), lambda k: (k, 0)), y_spec, ANY],
        out_specs=out_spec, out_shape=out_shape,
        scratch_shapes=[pltpu.VMEM((M, N), F32)],
        compiler_params=_params(("arbitrary",)),
    )(xm, ym, _after(dep))


def _mix_in_fwd(x1, g, w7, wf, name):
    T, D = x1.shape
    n_seg, _, W = w7.shape
    tm = _blk(T, 1024)

    def body(x_ref, g_ref, w_ref, wf_ref, z_ref, f_ref, hb_ref, h_scr):
        s = pl.program_id(1)

        @pl.when(s == 0)
        def _():
            xv = x_ref[...]
            h = (xv * _rms_scale(xv) * g_ref[...]).astype(BF16)
            h_scr[...] = h
            hb_ref[...] = h
            f_ref[...] = _mm(h, wf_ref[...])

        z_ref[...] = _mm(h_scr[...], w_ref[...]).astype(BF16)

    return pl.pallas_call(
        body, name=name, grid=(T // tm, n_seg),
        in_specs=[pl.BlockSpec((tm, D), lambda i, s: (i, 0)),
                  pl.BlockSpec((1, D), lambda i, s: (0, 0)),
                  pl.BlockSpec((None, D, W), lambda i, s: (s, 0, 0)),
                  pl.BlockSpec((D, LANES), lambda i, s: (0, 0))],
        out_specs=[pl.BlockSpec((None, tm, W), lambda i, s: (s, i, 0)),
                   pl.BlockSpec((tm, LANES), lambda i, s: (i, 0)),
                   pl.BlockSpec((tm, D), lambda i, s: (i, 0))],
        out_shape=[jax.ShapeDtypeStruct((n_seg, T, W), BF16), jax.ShapeDtypeStruct((T, LANES), F32),
                   jax.ShapeDtypeStruct((T, D), BF16)],
        scratch_shapes=[pltpu.VMEM((tm, D), BF16)],
        compiler_params=_params(("parallel", "arbitrary")),
    )(x1, g, w7, wf)


def _mix_in_bwd(dx2, x1, g, segs, dfb, w7, wf, name, dep=None):
    T, D = x1.shape
    n_seg, _, W = w7.shape
    tm, th, parts = _ffn_rows(T)
    n_i = T // tm

    def body(*refs):
        dx2_ref, x_ref, g_ref = refs[:3]
        seg_refs = refs[3:3 + n_seg]
        df_ref, w_ref, wf_ref, _, dx1_ref, dg_ref, acc_scr = refs[3 + n_seg:]
        s = pl.program_id(1)

        @pl.when(s == 0)
        def _():
            acc_scr[...] = _mm_nt(df_ref[...], wf_ref[...])

        for q in range(n_seg):
            @pl.when(s == q)
            def _(q=q):
                acc_scr[...] += _mm_nt(seg_refs[q][...], w_ref[...])

        @pl.when(s == n_seg - 1)
        def _():
            dg = jnp.zeros((1, D), F32)
            for p in range(parts):
                rows = slice(p * th, (p + 1) * th)
                xv = x_ref[rows, :]
                r = _rms_scale(xv)
                xhat = xv * r
                dh = acc_scr[rows, :]
                dg = dg + jnp.sum(dh * xhat, axis=0, keepdims=True)
                dx1_ref[rows, :] = _rms_bwd(dh, xhat, r, g_ref[...]) + dx2_ref[rows, :]
            dg_ref[...] = dg

    row = pl.BlockSpec((tm, D), lambda i, s: (i, 0), pipeline_mode=pl.Buffered(1))
    seg_specs = []
    seg_args = []
    for arr, idx in segs:
        if idx is None:
            seg_specs.append(pl.BlockSpec((tm, W), lambda i, s: (i, 0)))
        else:
            seg_specs.append(pl.BlockSpec((None, tm, W), lambda i, s, idx=idx: (idx, i, 0)))
        seg_args.append(arr)
    return pl.pallas_call(
        body, name=name, grid=(n_i, n_seg),
        in_specs=[row, row, pl.BlockSpec((1, D), lambda i, s: (0, 0))] + seg_specs + [
            pl.BlockSpec((tm, LANES), lambda i, s: (i, 0)),
            pl.BlockSpec((None, D, W), lambda i, s: (s, 0, 0)),
            pl.BlockSpec((D, LANES), lambda i, s: (0, 0)), ANY],
        out_specs=[row, pl.BlockSpec((None, 1, D), lambda i, s: (i, 0, 0))],
        out_shape=[jax.ShapeDtypeStruct((T, D), F32), jax.ShapeDtypeStruct((n_i, 1, D), F32)],
        scratch_shapes=[pltpu.VMEM((tm, D), F32)],
        compiler_params=_params(("parallel", "arbitrary")),
    )(dx2, x1, g, *seg_args, dfb, w7, wf, _after(dep))


def _forget_cumsum(f, b_pad, name):
    T, L = f.shape
    tb = _blk(T, 256)

    def body(f_ref, b_ref, c_ref, carry):
        @pl.when(pl.program_id(0) == 0)
        def _():
            carry[...] = jnp.zeros_like(carry)

        lf = jax.nn.log_sigmoid(f_ref[...] + b_ref[...])
        rows = lax.broadcasted_iota(jnp.int32, (tb, tb), 0)
        cols = lax.broadcasted_iota(jnp.int32, (tb, tb), 1)
        tri = (cols <= rows).astype(F32)
        c = jnp.dot(tri, lf, preferred_element_type=F32, precision=lax.Precision.HIGHEST) + carry[...]
        c_ref[...] = c
        carry[...] = c[tb - 1:tb, :]

    return pl.pallas_call(
        body, name=name, grid=(T // tb,),
        in_specs=[pl.BlockSpec((tb, L), lambda i: (i, 0)), pl.BlockSpec((1, L), lambda i: (0, 0))],
        out_specs=pl.BlockSpec((tb, L), lambda i: (i, 0)),
        out_shape=jax.ShapeDtypeStruct((T, L), F32),
        scratch_shapes=[pltpu.VMEM((1, L), F32)],
        compiler_params=_params(("arbitrary",)),
    )(f, b_pad)


def _forget_bwd(dc, f, b_pad, name):
    T, L = f.shape
    tb = _blk(T, 256)
    nb = T // tb

    def body(dc_ref, f_ref, b_ref, df_ref, db_ref, carry):
        @pl.when(pl.program_id(0) == 0)
        def _():
            carry[...] = jnp.zeros_like(carry)
            db_ref[...] = jnp.zeros_like(db_ref)

        rows = lax.broadcasted_iota(jnp.int32, (tb, tb), 0)
        cols = lax.broadcasted_iota(jnp.int32, (tb, tb), 1)
        tri = (cols >= rows).astype(F32)
        r = jnp.dot(tri, dc_ref[...], preferred_element_type=F32, precision=lax.Precision.HIGHEST) + carry[...]
        carry[...] = r[0:1, :]
        df = r * (1.0 - jax.nn.sigmoid(f_ref[...] + b_ref[...]))
        df_ref[...] = df.astype(BF16)
        db_ref[...] += jnp.sum(df, axis=0, keepdims=True)

    rev = pl.BlockSpec((tb, L), lambda i: (nb - 1 - i, 0))
    one = pl.BlockSpec((1, L), lambda i: (0, 0))
    return pl.pallas_call(
        body, name=name, grid=(nb,),
        in_specs=[rev, rev, one], out_specs=[rev, one],
        out_shape=[jax.ShapeDtypeStruct((T, L), BF16), jax.ShapeDtypeStruct((1, L), F32)],
        scratch_shapes=[pltpu.VMEM((1, L), F32)],
        compiler_params=_params(("arbitrary",)),
    )(dc, f, b_pad)


def _attn_fwd(z7, c_row, name):
    _, T, W = z7.shape
    H = W // HEAD_DIM
    ta = _blk(T, 512)
    nq = T // ta
    scale = np.float32(1.0 / np.sqrt(HEAD_DIM))

    def body(q_ref, k_ref, v_ref, crow_ref, o_ref, lse_ref, m_scr, l_scr, acc_scr):
        i = pl.program_id(1)
        j = pl.program_id(2)

        @pl.when(j == 0)
        def _():
            m_scr[...] = jnp.full_like(m_scr, NEG_BIG)
            l_scr[...] = jnp.zeros_like(l_scr)
            acc_scr[...] = jnp.zeros_like(acc_scr)

        def step(diagonal):
            s = _mm_nt(q_ref[...], k_ref[...]) * scale - crow_ref[...]
            if diagonal:
                rows = lax.broadcasted_iota(jnp.int32, (ta, ta), 0)
                cols = lax.broadcasted_iota(jnp.int32, (ta, ta), 1)
                s = jnp.where(cols <= rows, s, NEG_BIG)
            m_prev = m_scr[...]
            m_new = jnp.maximum(m_prev, jnp.max(s, axis=-1, keepdims=True))
            alpha = jnp.exp(m_prev - m_new)
            p = jnp.exp(s - m_new)
            l_scr[...] = alpha * l_scr[...] + jnp.sum(p, axis=-1, keepdims=True)
            acc_scr[...] = alpha * acc_scr[...] + _mm(p.astype(BF16), v_ref[...])
            m_scr[...] = m_new

        @pl.when(j < i)
        def _():
            step(False)

        @pl.when(j == i)
        def _():
            step(True)
            l = l_scr[...]
            o_ref[...] = acc_scr[...] / l
            lse_ref[...] = m_scr[...] + jnp.log(l)

    return pl.pallas_call(
        body, name=name, grid=(H, nq, nq),
        in_specs=[pl.BlockSpec((None, ta, HEAD_DIM), lambda h, i, j: (0, i, h)),
                  pl.BlockSpec((None, ta, HEAD_DIM), lambda h, i, j: (1, jnp.minimum(i, j), h)),
                  pl.BlockSpec((None, ta, HEAD_DIM), lambda h, i, j: (2, jnp.minimum(i, j), h)),
                  pl.BlockSpec((None, 1, ta), lambda h, i, j: (h, 0, jnp.minimum(i, j)))],
        out_specs=[pl.BlockSpec((ta, HEAD_DIM), lambda h, i, j: (i, h)),
                   pl.BlockSpec((None, ta, 1), lambda h, i, j: (h, i, 0))],
        out_shape=[jax.ShapeDtypeStruct((T, W), F32), jax.ShapeDtypeStruct((H, T, 1), F32)],
        scratch_shapes=[pltpu.VMEM((ta, 1), F32), pltpu.VMEM((ta, 1), F32), pltpu.VMEM((ta, HEAD_DIM), F32)],
        compiler_params=_params(("parallel", "parallel", "arbitrary")),
    )(z7, z7, z7, c_row)


def _attn_bwd_kv(z7, dob, c_col, lse_row, d_row, name):
    _, T, W = z7.shape
    H = W // HEAD_DIM
    ta = _blk(T, 512)
    nq = T // ta
    scale = np.float32(1.0 / np.sqrt(HEAD_DIM))

    def body(k_ref, v_ref, q_ref, do_ref, ccol_ref, lse_ref, d_ref, dk_ref, dv_ref, dc_ref, dk_scr, dv_scr, dc_scr):
        j = pl.program_id(1)
        i = pl.program_id(2)

        @pl.when(i == 0)
        def _():
            dk_scr[...] = jnp.zeros_like(dk_scr)
            dv_scr[...] = jnp.zeros_like(dv_scr)
            dc_scr[...] = jnp.zeros_like(dc_scr)

        def step(diagonal):
            q = q_ref[...]
            do = do_ref[...]
            st = _mm_nt(k_ref[...], q) * scale - ccol_ref[...] - lse_ref[...]
            if diagonal:
                rows = lax.broadcasted_iota(jnp.int32, (ta, ta), 0)
                cols = lax.broadcasted_iota(jnp.int32, (ta, ta), 1)
                st = jnp.where(rows <= cols, st, NEG_BIG)
            pt = jnp.exp(st)
            dv_scr[...] += _mm(pt.astype(BF16), do)
            dst = pt * (_mm_nt(v_ref[...], do) - d_ref[...])
            dk_scr[...] += _mm(dst.astype(BF16), q)
            dc_scr[...] += jnp.sum(dst, axis=-1, keepdims=True)

        @pl.when(i > j)
        def _():
            step(False)

        @pl.when(i == j)
        def _():
            step(True)

        @pl.when(i == nq - 1)
        def _():
            dk_ref[...] = (dk_scr[...] * scale).astype(BF16)
            dv_ref[...] = dv_scr[...].astype(BF16)
            dc_ref[...] = -dc_scr[...]

    return pl.pallas_call(
        body, name=name, grid=(H, nq, nq),
        in_specs=[pl.BlockSpec((None, ta, HEAD_DIM), lambda h, j, i: (1, j, h)),
                  pl.BlockSpec((None, ta, HEAD_DIM), lambda h, j, i: (2, j, h)),
                  pl.BlockSpec((None, ta, HEAD_DIM), lambda h, j, i: (0, jnp.maximum(i, j), h)),
                  pl.BlockSpec((ta, HEAD_DIM), lambda h, j, i: (jnp.maximum(i, j), h)),
                  pl.BlockSpec((None, ta, 1), lambda h, j, i: (h, j, 0)),
                  pl.BlockSpec((None, 1, ta), lambda h, j, i: (h, 0, jnp.maximum(i, j))),
                  pl.BlockSpec((None, 1, ta), lambda h, j, i: (h, 0, jnp.maximum(i, j)))],
        out_specs=[pl.BlockSpec((ta, HEAD_DIM), lambda h, j, i: (j, h)),
                   pl.BlockSpec((ta, HEAD_DIM), lambda h, j, i: (j, h)),
                   pl.BlockSpec((None, ta, 1), lambda h, j, i: (h, j, 0))],
        out_shape=[jax.ShapeDtypeStruct((T, W), BF16), jax.ShapeDtypeStruct((T, W), BF16),
                   jax.ShapeDtypeStruct((H, T, 1), F32)],
        scratch_shapes=[pltpu.VMEM((ta, HEAD_DIM), F32), pltpu.VMEM((ta, HEAD_DIM), F32), pltpu.VMEM((ta, 1), F32)],
        compiler_params=_params(("parallel", "parallel", "arbitrary")),
    )(z7, z7, z7, dob, c_col, lse_row, d_row)


def _attn_bwd_q(z7, dob, c_row, lse_col, d_col, name):
    _, T, W = z7.shape
    H = W // HEAD_DIM
    ta = _blk(T, 512)
    nq = T // ta
    scale = np.float32(1.0 / np.sqrt(HEAD_DIM))

    def body(q_ref, k_ref, v_ref, do_ref, crow_ref, lse_ref, d_ref, dq_ref, dc_ref, dq_scr, dc_scr):
        i = pl.program_id(1)
        j = pl.program_id(2)

        @pl.when(j == 0)
        def _():
            dq_scr[...] = jnp.zeros_like(dq_scr)
            dc_scr[...] = jnp.zeros_like(dc_scr)

        def step(diagonal):
            k = k_ref[...]
            do = do_ref[...]
            s = _mm_nt(q_ref[...], k) * scale - crow_ref[...] - lse_ref[...]
            if diagonal:
                rows = lax.broadcasted_iota(jnp.int32, (ta, ta), 0)
                cols = lax.broadcasted_iota(jnp.int32, (ta, ta), 1)
                s = jnp.where(cols <= rows, s, NEG_BIG)
            p = jnp.exp(s)
            ds = p * (_mm_nt(do, v_ref[...]) - d_ref[...])
            dq_scr[...] += _mm(ds.astype(BF16), k)
            dc_scr[...] += jnp.sum(ds, axis=-1, keepdims=True)

        @pl.when(j < i)
        def _():
            step(False)

        @pl.when(j == i)
        def _():
            step(True)
            dq_ref[...] = (dq_scr[...] * scale).astype(BF16)
            dc_ref[...] = dc_scr[...]

    return pl.pallas_call(
        body, name=name, grid=(H, nq, nq),
        in_specs=[pl.BlockSpec((None, ta, HEAD_DIM), lambda h, i, j: (0, i, h)),
                  pl.BlockSpec((None, ta, HEAD_DIM), lambda h, i, j: (1, jnp.minimum(i, j), h)),
                  pl.BlockSpec((None, ta, HEAD_DIM), lambda h, i, j: (2, jnp.minimum(i, j), h)),
                  pl.BlockSpec((ta, HEAD_DIM), lambda h, i, j: (i, h)),
                  pl.BlockSpec((None, 1, ta), lambda h, i, j: (h, 0, jnp.minimum(i, j))),
                  pl.BlockSpec((None, ta, 1), lambda h, i, j: (h, i, 0)),
                  pl.BlockSpec((None, ta, 1), lambda h, i, j: (h, i, 0))],
        out_specs=[pl.BlockSpec((ta, HEAD_DIM), lambda h, i, j: (i, h)),
                   pl.BlockSpec((None, ta, 1), lambda h, i, j: (h, i, 0))],
        out_shape=[jax.ShapeDtypeStruct((T, W), BF16), jax.ShapeDtypeStruct((H, T, 1), F32)],
        scratch_shapes=[pltpu.VMEM((ta, HEAD_DIM), F32), pltpu.VMEM((ta, 1), F32)],
        compiler_params=_params(("parallel", "parallel", "arbitrary")),
    )(z7, z7, z7, dob, c_row, lse_col, d_col)


ATTN_TILE = 512
ATTN_CHAINS = 2


def _attn_geometry(T):
    ta = _blk(T, ATTN_TILE)
    nc = ATTN_CHAINS if (T // ta) % ATTN_CHAINS == 0 else 1
    return ta, nc, T // ta


def _causal_tile(ta, keys_on_rows=False):
    rows = lax.broadcasted_iota(jnp.int32, (ta, ta), 0)
    cols = lax.broadcasted_iota(jnp.int32, (ta, ta), 1)
    return rows <= cols if keys_on_rows else cols <= rows


def _chunk(ref, j, ta):
    return ref[pl.ds(pl.multiple_of(j * ta, ta), ta), :]


def _attn_fwd_loop(z7, c_chunks, name):
    _, T, W = z7.shape
    H = W // HEAD_DIM
    ta, nc, n_chunks = _attn_geometry(T)
    scale = np.float32(1.0 / np.sqrt(HEAD_DIM))

    def body(q_ref, k_ref, v_ref, c_ref, o_ref, lse_ref, m_scr, l_scr, acc_scr):
        g = pl.program_id(1)
        m_scr[...] = jnp.full_like(m_scr, NEG_BIG)
        l_scr[...] = jnp.zeros_like(l_scr)
        acc_scr[...] = jnp.zeros_like(acc_scr)

        def update(ch, k, v, crow, diagonal):
            q = q_ref[ch * ta:(ch + 1) * ta, :]
            s = _mm_nt(q, k) * scale - crow
            if diagonal:
                s = jnp.where(_causal_tile(ta), s, NEG_BIG)
            m_prev = m_scr[ch]
            m_new = jnp.maximum(m_prev, jnp.max(s, axis=-1, keepdims=True))
            alpha = jnp.exp(m_prev - m_new)
            p = jnp.exp(s - m_new)
            l_scr[ch] = alpha * l_scr[ch] + jnp.sum(p, axis=-1, keepdims=True)
            acc_scr[ch] = alpha * acc_scr[ch] + _mm(p.astype(BF16), v)
            m_scr[ch] = m_new

        def full_chunk(j, carry):
            k = _chunk(k_ref, j, ta)
            v = _chunk(v_ref, j, ta)
            crow = c_ref[j]
            for ch in range(nc):
                update(ch, k, v, crow, False)
            return carry

        lax.fori_loop(0, nc * g, full_chunk, 0)
        for jj in range(nc):
            j = nc * g + jj
            k = _chunk(k_ref, j, ta)
            v = _chunk(v_ref, j, ta)
            crow = c_ref[j]
            for ch in range(jj, nc):
                update(ch, k, v, crow, ch == jj)
        for ch in range(nc):
            l = l_scr[ch]
            o_ref[ch * ta:(ch + 1) * ta, :] = acc_scr[ch] / l
            lse_ref[ch * ta:(ch + 1) * ta, :] = m_scr[ch] + jnp.log(l)

    tq = nc * ta
    return pl.pallas_call(
        body, name=name, grid=(H, n_chunks // nc),
        in_specs=[pl.BlockSpec((None, tq, HEAD_DIM), lambda h, g: (0, g, h)),
                  pl.BlockSpec((None, T, HEAD_DIM), lambda h, g: (1, 0, h)),
                  pl.BlockSpec((None, T, HEAD_DIM), lambda h, g: (2, 0, h)),
                  pl.BlockSpec((None, n_chunks, 1, ta), lambda h, g: (h, 0, 0, 0))],
        out_specs=[pl.BlockSpec((tq, HEAD_DIM), lambda h, g: (g, h)),
                   pl.BlockSpec((None, tq, 1), lambda h, g: (h, g, 0))],
        out_shape=[jax.ShapeDtypeStruct((T, W), F32), jax.ShapeDtypeStruct((H, T, 1), F32)],
        scratch_shapes=[pltpu.VMEM((nc, ta, 1), F32), pltpu.VMEM((nc, ta, 1), F32),
                        pltpu.VMEM((nc, ta, HEAD_DIM), F32)],
        compiler_params=_params(("parallel", "arbitrary")),
    )(z7, z7, z7, c_chunks)


def _attn_fwd_keys_on_rows(z7, vt, c_rep, name):
    _, T, W = z7.shape
    H = W // HEAD_DIM
    ta, nc, n_chunks = _attn_geometry(T)
    scale = np.float32(1.0 / np.sqrt(HEAD_DIM))
    reps = ta // LANES

    def body(q_ref, k_ref, vt_ref, c_ref, o_ref, lse_ref):
        g = pl.program_id(1)

        def scores(ch, k):
            return _mm_nt(k, q_ref[ch * ta:(ch + 1) * ta, :])

        def update(state, raw, vt, cj, diagonal):
            m_prev, l_prev, acc_prev = state
            st = raw * (scale * LOG2E) - cj
            if diagonal:
                st = jnp.where(_causal_tile(ta, keys_on_rows=True), st, NEG_BIG)
            m_new = jnp.maximum(m_prev, jnp.max(st, axis=0, keepdims=True))
            alpha = jnp.exp2(m_prev - m_new)
            pt = jnp.exp2(st - m_new)
            l_new = alpha * l_prev + jnp.sum(pt, axis=0, keepdims=True)
            acc_new = alpha * acc_prev + _mm(vt, pt.astype(BF16))
            return m_new, l_new, acc_new

        def load(j):
            cj = _chunk(c_ref, j, ta)
            return _chunk(k_ref, j, ta), vt_ref[j], jnp.concatenate([cj] * reps, axis=1)

        def full_chunk(j, states):
            k, vt, cj = load(j)
            raws = [scores(ch, k) for ch in range(nc)]
            return tuple(update(states[ch], raws[ch], vt, cj, False) for ch in range(nc))

        first = (jnp.full((1, ta), NEG_BIG, F32), jnp.zeros((1, ta), F32), jnp.zeros((HEAD_DIM, ta), F32))
        states = list(lax.fori_loop(0, nc * g, full_chunk, (first,) * nc))
        for jj in range(nc):
            k, vt, cj = load(nc * g + jj)
            raws = {ch: scores(ch, k) for ch in range(jj, nc)}
            for ch in range(jj, nc):
                states[ch] = update(states[ch], raws[ch], vt, cj, ch == jj)
        for ch in range(nc):
            m, l, acc = states[ch]
            o_ref[ch * ta:(ch + 1) * ta, :] = (acc / l).T
            lse_ref[ch] = m + jnp.log2(l)

    tq = nc * ta
    return pl.pallas_call(
        body, name=name, grid=(H, n_chunks // nc),
        in_specs=[pl.BlockSpec((None, tq, HEAD_DIM), lambda h, g: (0, g, h)),
                  pl.BlockSpec((None, T, HEAD_DIM), lambda h, g: (1, 0, h)),
                  pl.BlockSpec((None, n_chunks, HEAD_DIM, ta), lambda h, g: (h, 0, 0, 0)),
                  pl.BlockSpec((None, T, LANES), lambda h, g: (h, 0, 0))],
        out_specs=[pl.BlockSpec((tq, HEAD_DIM), lambda h, g: (g, h)),
                   pl.BlockSpec((None, nc, 1, ta), lambda h, g: (h, g, 0, 0))],
        out_shape=[jax.ShapeDtypeStruct((T, W), F32), jax.ShapeDtypeStruct((H, n_chunks, 1, ta), F32)],
        compiler_params=_params(("parallel", "arbitrary")),
    )(z7, z7, vt, c_rep)


def _attn_bwd_fused(z7, kt, dob, c_rep, lse_chunks, d_chunks, name):
    _, T, W = z7.shape
    H = W // HEAD_DIM
    ta, nc, n_chunks = _attn_geometry(T)
    n_steps = n_chunks // nc
    scale = np.float32(1.0 / np.sqrt(HEAD_DIM))
    reps = ta // LANES

    def body(k_ref, v_ref, kt_ref, q_ref, do_ref, c_ref, lse_ref, d_ref,
             dk_ref, dv_ref, dck_ref, dq_ref, dcq_ref, dk_scr, dv_scr, dck_scr, dqt_scr, dcq_scr):
        g = pl.program_id(1)

        @pl.when(g == 0)
        def _():
            dqt_scr[...] = jnp.zeros_like(dqt_scr)
            dcq_scr[...] = jnp.zeros_like(dcq_scr)

        dk_scr[...] = jnp.zeros_like(dk_scr)
        dv_scr[...] = jnp.zeros_like(dv_scr)
        dck_scr[...] = jnp.zeros_like(dck_scr)

        def products(ch, q, do):
            rows = slice(ch * ta, (ch + 1) * ta)
            return _mm_nt(k_ref[rows, :], q), _mm_nt(v_ref[rows, :], do)

        def update(ch, i, q, do, prods, diagonal):
            rows = slice(ch * ta, (ch + 1) * ta)
            cj = c_ref[rows, :]
            st = prods[0] * (scale * LOG2E) - jnp.concatenate([cj] * reps, axis=1) - lse_ref[i]
            if diagonal:
                st = jnp.where(_causal_tile(ta, keys_on_rows=True), st, NEG_BIG)
            pt = jnp.exp2(st)
            dv_scr[ch] += _mm(pt.astype(BF16), do)
            dst = pt * (prods[1] - d_ref[i])
            dst_b = dst.astype(BF16)
            dk_scr[ch] += _mm(dst_b, q)
            dqt_scr[i] += _mm(kt_ref[ch], dst_b)
            dcq_scr[i] += jnp.sum(dst, axis=0, keepdims=True)
            lane_sum = dst[:, :LANES]
            for r in range(1, reps):
                lane_sum = lane_sum + dst[:, r * LANES:(r + 1) * LANES]
            dck_scr[ch] += lane_sum

        for ii in range(nc):
            i = nc * g + ii
            q = _chunk(q_ref, i, ta)
            do = _chunk(do_ref, i, ta)
            prods = [products(ch, q, do) for ch in range(0, ii + 1)]
            for ch in range(0, ii + 1):
                update(ch, i, q, do, prods[ch], ch == ii)

        def full_chunk(i, carry):
            q = _chunk(q_ref, i, ta)
            do = _chunk(do_ref, i, ta)
            prods = [products(ch, q, do) for ch in range(nc)]
            for ch in range(nc):
                update(ch, i, q, do, prods[ch], False)
            return carry

        lax.fori_loop(nc * (g + 1), n_chunks, full_chunk, 0)
        for ch in range(nc):
            rows = slice(ch * ta, (ch + 1) * ta)
            dk_ref[rows, :] = (dk_scr[ch] * scale).astype(BF16)
            dv_ref[rows, :] = dv_scr[ch].astype(BF16)
            ones = jnp.ones((8, LANES), F32)
            sums = lax.dot_general(ones, dck_scr[ch], (((1,), (1,)), ((), ())), preferred_element_type=F32,
                                   precision=lax.Precision.HIGHEST)
            dck_ref[ch] = -sums[0:1, :]

        @pl.when(g == n_steps - 1)
        def _():
            for i in range(n_chunks):
                dq_ref[i * ta:(i + 1) * ta, :] = (dqt_scr[i] * scale).T.astype(BF16)
            dcq_ref[...] = dcq_scr[...]

    tk = nc * ta
    chunks = pl.BlockSpec((None, n_chunks, 1, ta), lambda h, g: (h, 0, 0, 0))
    tile = pl.BlockSpec((tk, HEAD_DIM), lambda h, g: (g, h))
    return pl.pallas_call(
        body, name=name, grid=(H, n_steps),
        in_specs=[pl.BlockSpec((None, tk, HEAD_DIM), lambda h, g: (1, g, h)),
                  pl.BlockSpec((None, tk, HEAD_DIM), lambda h, g: (2, g, h)),
                  pl.BlockSpec((None, nc, HEAD_DIM, ta), lambda h, g: (h, g, 0, 0)),
                  pl.BlockSpec((None, T, HEAD_DIM), lambda h, g: (0, 0, h)),
                  pl.BlockSpec((T, HEAD_DIM), lambda h, g: (0, h)),
                  pl.BlockSpec((None, tk, LANES), lambda h, g: (h, g, 0)),
                  chunks, chunks],
        out_specs=[tile, tile, pl.BlockSpec((None, nc, 1, ta), lambda h, g: (h, g, 0, 0)),
                   pl.BlockSpec((T, HEAD_DIM), lambda h, g: (0, h)), chunks],
        out_shape=[jax.ShapeDtypeStruct((T, W), BF16), jax.ShapeDtypeStruct((T, W), BF16),
                   jax.ShapeDtypeStruct((H, n_chunks, 1, ta), F32), jax.ShapeDtypeStruct((T, W), BF16),
                   jax.ShapeDtypeStruct((H, n_chunks, 1, ta), F32)],
        scratch_shapes=[pltpu.VMEM((nc, ta, HEAD_DIM), F32), pltpu.VMEM((nc, ta, HEAD_DIM), F32),
                        pltpu.VMEM((nc, ta, LANES), F32), pltpu.VMEM((n_chunks, HEAD_DIM, ta), F32),
                        pltpu.VMEM((n_chunks, 1, ta), F32)],
        compiler_params=_params(("parallel", "arbitrary")),
    )(z7, z7, kt, z7, dob, c_rep, lse_chunks, d_chunks)


def _attn_bwd_q_loop(z7, dob, c_chunks, lse_col, d_col, name):
    _, T, W = z7.shape
    H = W // HEAD_DIM
    ta, nc, n_chunks = _attn_geometry(T)
    scale = np.float32(1.0 / np.sqrt(HEAD_DIM))

    def body(q_ref, k_ref, v_ref, do_ref, c_ref, lse_ref, d_ref, dq_ref, dc_ref, dq_scr, dc_scr):
        g = pl.program_id(1)
        dq_scr[...] = jnp.zeros_like(dq_scr)
        dc_scr[...] = jnp.zeros_like(dc_scr)

        def update(ch, k, v, crow, diagonal):
            rows = slice(ch * ta, (ch + 1) * ta)
            do = do_ref[rows, :]
            s = _mm_nt(q_ref[rows, :], k) * scale - crow - lse_ref[rows, :]
            if diagonal:
                s = jnp.where(_causal_tile(ta), s, NEG_BIG)
            p = jnp.exp(s)
            ds = p * (_mm_nt(do, v) - d_ref[rows, :])
            dq_scr[ch] += _mm(ds.astype(BF16), k)
            dc_scr[ch] += jnp.sum(ds, axis=-1, keepdims=True)

        def full_chunk(j, carry):
            k = _chunk(k_ref, j, ta)
            v = _chunk(v_ref, j, ta)
            crow = c_ref[j]
            for ch in range(nc):
                update(ch, k, v, crow, False)
            return carry

        lax.fori_loop(0, nc * g, full_chunk, 0)
        for jj in range(nc):
            j = nc * g + jj
            k = _chunk(k_ref, j, ta)
            v = _chunk(v_ref, j, ta)
            crow = c_ref[j]
            for ch in range(jj, nc):
                update(ch, k, v, crow, ch == jj)
        for ch in range(nc):
            dq_ref[ch * ta:(ch + 1) * ta, :] = (dq_scr[ch] * scale).astype(BF16)
            dc_ref[ch * ta:(ch + 1) * ta, :] = dc_scr[ch]

    tq = nc * ta
    col = pl.BlockSpec((None, tq, 1), lambda h, g: (h, g, 0))
    return pl.pallas_call(
        body, name=name, grid=(H, n_chunks // nc),
        in_specs=[pl.BlockSpec((None, tq, HEAD_DIM), lambda h, g: (0, g, h)),
                  pl.BlockSpec((None, T, HEAD_DIM), lambda h, g: (1, 0, h)),
                  pl.BlockSpec((None, T, HEAD_DIM), lambda h, g: (2, 0, h)),
                  pl.BlockSpec((tq, HEAD_DIM), lambda h, g: (g, h)),
                  pl.BlockSpec((None, n_chunks, 1, ta), lambda h, g: (h, 0, 0, 0)),
                  col, col],
        out_specs=[pl.BlockSpec((tq, HEAD_DIM), lambda h, g: (g, h)), col],
        out_shape=[jax.ShapeDtypeStruct((T, W), BF16), jax.ShapeDtypeStruct((H, T, 1), F32)],
        scratch_shapes=[pltpu.VMEM((nc, ta, HEAD_DIM), F32), pltpu.VMEM((nc, ta, 1), F32)],
        compiler_params=_params(("parallel", "arbitrary")),
    )(z7, z7, z7, dob, c_chunks, lse_col, d_col)


def _attn_bwd_kv_loop(z7, dob, c_col, lse_chunks, d_chunks, name):
    _, T, W = z7.shape
    H = W // HEAD_DIM
    ta, nc, n_chunks = _attn_geometry(T)
    scale = np.float32(1.0 / np.sqrt(HEAD_DIM))

    def body(k_ref, v_ref, q_ref, do_ref, ccol_ref, lse_ref, d_ref, dk_ref, dv_ref, dc_ref, dk_scr, dv_scr, dc_scr):
        g = pl.program_id(1)
        dk_scr[...] = jnp.zeros_like(dk_scr)
        dv_scr[...] = jnp.zeros_like(dv_scr)
        dc_scr[...] = jnp.zeros_like(dc_scr)

        def update(ch, q, do, lse_row, d_row, diagonal):
            rows = slice(ch * ta, (ch + 1) * ta)
            st = _mm_nt(k_ref[rows, :], q) * scale - ccol_ref[rows, :] - lse_row
            if diagonal:
                st = jnp.where(_causal_tile(ta, keys_on_rows=True), st, NEG_BIG)
            pt = jnp.exp(st)
            dv_scr[ch] += _mm(pt.astype(BF16), do)
            dst = pt * (_mm_nt(v_ref[rows, :], do) - d_row)
            dk_scr[ch] += _mm(dst.astype(BF16), q)
            dc_scr[ch] += jnp.sum(dst, axis=-1, keepdims=True)

        for ii in range(nc):
            i = nc * g + ii
            q = _chunk(q_ref, i, ta)
            do = _chunk(do_ref, i, ta)
            for ch in range(0, ii + 1):
                update(ch, q, do, lse_ref[i], d_ref[i], ch == ii)

        def full_chunk(i, carry):
            q = _chunk(q_ref, i, ta)
            do = _chunk(do_ref, i, ta)
            for ch in range(nc):
                update(ch, q, do, lse_ref[i], d_ref[i], False)
            return carry

        lax.fori_loop(nc * (g + 1), n_chunks, full_chunk, 0)
        for ch in range(nc):
            rows = slice(ch * ta, (ch + 1) * ta)
            dk_ref[rows, :] = (dk_scr[ch] * scale).astype(BF16)
            dv_ref[rows, :] = dv_scr[ch].astype(BF16)
            dc_ref[rows, :] = -dc_scr[ch]

    tk = nc * ta
    chunks = pl.BlockSpec((None, n_chunks, 1, ta), lambda h, g: (h, 0, 0, 0))
    col = pl.BlockSpec((None, tk, 1), lambda h, g: (h, g, 0))
    tile = pl.BlockSpec((tk, HEAD_DIM), lambda h, g: (g, h))
    return pl.pallas_call(
        body, name=name, grid=(H, n_chunks // nc),
        in_specs=[pl.BlockSpec((None, tk, HEAD_DIM), lambda h, g: (1, g, h)),
                  pl.BlockSpec((None, tk, HEAD_DIM), lambda h, g: (2, g, h)),
                  pl.BlockSpec((None, T, HEAD_DIM), lambda h, g: (0, 0, h)),
                  pl.BlockSpec((T, HEAD_DIM), lambda h, g: (0, h)),
                  col, chunks, chunks],
        out_specs=[tile, tile, col],
        out_shape=[jax.ShapeDtypeStruct((T, W), BF16), jax.ShapeDtypeStruct((T, W), BF16),
                   jax.ShapeDtypeStruct((H, T, 1), F32)],
        scratch_shapes=[pltpu.VMEM((nc, ta, HEAD_DIM), F32), pltpu.VMEM((nc, ta, HEAD_DIM), F32),
                        pltpu.VMEM((nc, ta, 1), F32)],
        compiler_params=_params(("parallel", "arbitrary")),
    )(z7, z7, z7, dob, c_col, lse_chunks, d_chunks)


def _chunk_causal_mask():
    rows = lax.broadcasted_iota(jnp.int32, (SGU_LEN, SGU_LEN), 0)
    cols = lax.broadcasted_iota(jnp.int32, (SGU_LEN, SGU_LEN), 1)
    return (cols // CHUNK) <= (rows // CHUNK)


def _sgu_norm_mix(sv, lng_ref, lnb_ref, ws_ref, bs_ref, vn_scr, mixed_scr, vhat_scr=None):
    tm = sv.shape[0]
    vs = _gelu(sv)
    mask = _chunk_causal_mask()
    rstds = []
    for g in range(N_GROUPS):
        lanes = slice(g * GROUP_DIM, (g + 1) * GROUP_DIM)
        blk = vs[:, lanes]
        cen = blk - jnp.mean(blk, axis=-1, keepdims=True)
        rstd = lax.rsqrt(jnp.mean(cen * cen, axis=-1, keepdims=True) + LN_EPS)
        vhat = cen * rstd
        rstds.append(rstd)
        if vhat_scr is not None:
            vhat_scr[:, lanes] = vhat
        vn_scr[:, lanes] = (vhat * lng_ref[:, lanes] + lnb_ref[:, lanes]).astype(BF16)
        wm = jnp.where(mask, ws_ref[g], 0.0).astype(BF16)
        for w in range(tm // SGU_LEN):
            rows = slice(w * SGU_LEN, (w + 1) * SGU_LEN)
            mixed_scr[rows, lanes] = _mm(wm, vn_scr[rows, lanes]) + bs_ref[g]
    return rstds


def _mix_out_fwd(z7, o_a, x1, lng, lnb, ws, bs, w_out, g_post, name):
    _, T, W = z7.shape
    D = x1.shape[1]
    tm = _blk(T, 256)

    def body(u_ref, sv_ref, ga_ref, gb_ref, oa_ref, x1_ref, lng_ref, lnb_ref, ws_ref, bs_ref, wo_ref, gp_ref,
             x2_ref, p_ref, mb_ref, vn_scr, mixed_scr):
        _sgu_norm_mix(sv_ref[...].astype(F32), lng_ref, lnb_ref, ws_ref, bs_ref, vn_scr, mixed_scr)
        o_b = _gelu(u_ref[...].astype(F32)) * mixed_scr[...]
        merged = (jax.nn.sigmoid(ga_ref[...].astype(F32)) * oa_ref[...]
                  + jax.nn.sigmoid(gb_ref[...].astype(F32)) * o_b).astype(BF16)
        mb_ref[...] = merged
        p = _mm(merged, wo_ref[...])
        p_ref[...] = p
        x2_ref[...] = x1_ref[...] + p * _rms_scale(p) * gp_ref[...]

    def seg(idx):
        return pl.BlockSpec((None, tm, W), lambda i, idx=idx: (idx, i, 0))

    row = pl.BlockSpec((tm, D), lambda i: (i, 0))
    vec = pl.BlockSpec((1, D), lambda i: (0, 0))
    return pl.pallas_call(
        body, name=name, grid=(T // tm,),
        in_specs=[seg(3), seg(4), seg(5), seg(6), row, row, vec, vec,
                  pl.BlockSpec((N_GROUPS, SGU_LEN, SGU_LEN), lambda i: (0, 0, 0)),
                  pl.BlockSpec((N_GROUPS, SGU_LEN, 1), lambda i: (0, 0, 0)),
                  pl.BlockSpec((D, D), lambda i: (0, 0)), vec],
        out_specs=[row, row, row],
        out_shape=[jax.ShapeDtypeStruct((T, D), F32), jax.ShapeDtypeStruct((T, D), F32),
                   jax.ShapeDtypeStruct((T, D), BF16)],
        scratch_shapes=[pltpu.VMEM((tm, W), BF16), pltpu.VMEM((tm, W), F32)],
        compiler_params=_params(("parallel",)),
    )(z7, z7, z7, z7, o_a, x1, lng, lnb, ws, bs, w_out, g_post)


def _mix_out_bwd(dx2, p, z7, o_a, lng, lnb, ws, bs, w_out, g_post, name, dep=None):
    _, T, W = z7.shape
    D = dx2.shape[1]
    tm = _blk(T, 256)
    n_w = tm // SGU_LEN

    def body(dx2_ref, p_ref, u_ref, sv_ref, ga_ref, gb_ref, oa_ref, lng_ref, lnb_ref, ws_ref, bs_ref, wo_ref, gp_ref, _,
             dpb_ref, dob_ref, dvec_ref, dz_ref, dgp_ref, dlng_ref, dlnb_ref, dws_ref, dbs_ref,
             vn_scr, mixed_scr, vhat_scr, dmix_scr, dvn_scr):
        @pl.when(pl.program_id(0) == 0)
        def _():
            dgp_ref[...] = jnp.zeros_like(dgp_ref)
            dlng_ref[...] = jnp.zeros_like(dlng_ref)
            dlnb_ref[...] = jnp.zeros_like(dlnb_ref)
            dws_ref[...] = jnp.zeros_like(dws_ref)
            dbs_ref[...] = jnp.zeros_like(dbs_ref)

        pv = p_ref[...]
        s = _rms_scale(pv)
        n = pv * s
        dn = dx2_ref[...]
        dgp_ref[...] += jnp.sum(dn * n, axis=0, keepdims=True)
        dpb = _rms_bwd(dn, n, s, gp_ref[...]).astype(BF16)
        dpb_ref[...] = dpb
        dmerged = _mm_nt(dpb, wo_ref[...])

        sv = sv_ref[...].astype(F32)
        rstds = _sgu_norm_mix(sv, lng_ref, lnb_ref, ws_ref, bs_ref, vn_scr, mixed_scr, vhat_scr)
        u_pre = u_ref[...].astype(F32)
        u = _gelu(u_pre)
        mixed = mixed_scr[...]
        sa = jax.nn.sigmoid(ga_ref[...].astype(F32))
        sb = jax.nn.sigmoid(gb_ref[...].astype(F32))
        oa = oa_ref[...]
        do_a = (dmerged * sa).astype(BF16)
        dob_ref[...] = do_a
        prod = do_a.astype(F32) * oa
        for h in range(N_HEADS):
            dvec_ref[h] = jnp.sum(prod[:, h * HEAD_DIM:(h + 1) * HEAD_DIM], axis=-1, keepdims=True)
        dz_ref[2] = (dmerged * oa * (sa * (1.0 - sa))).astype(BF16)
        dz_ref[3] = (dmerged * (u * mixed) * (sb * (1.0 - sb))).astype(BF16)
        do_b = dmerged * sb
        dz_ref[0] = (do_b * mixed * _gelu_grad(u_pre)).astype(BF16)
        dmix_scr[...] = do_b * u

        mask = _chunk_causal_mask()
        for g in range(N_GROUPS):
            lanes = slice(g * GROUP_DIM, (g + 1) * GROUP_DIM)
            wm = jnp.where(mask, ws_ref[g], 0.0).astype(BF16)
            dws = jnp.zeros((SGU_LEN, SGU_LEN), F32)
            dbs = jnp.zeros((SGU_LEN, 1), F32)
            for w in range(n_w):
                rows = slice(w * SGU_LEN, (w + 1) * SGU_LEN)
                dmix = dmix_scr[rows, lanes]
                dmix_b = dmix.astype(BF16)
                dvn_scr[rows, lanes] = _mm_tn(wm, dmix_b)
                dws = dws + _mm_nt(dmix_b, vn_scr[rows, lanes])
                dbs = dbs + jnp.sum(dmix, axis=-1, keepdims=True)
            dws_ref[g] += jnp.where(mask, dws, 0.0)
            dbs_ref[g] += dbs
            dvn = dvn_scr[:, lanes]
            vhat = vhat_scr[:, lanes]
            dlng_ref[:, lanes] += jnp.sum(dvn * vhat, axis=0, keepdims=True)
            dlnb_ref[:, lanes] += jnp.sum(dvn, axis=0, keepdims=True)
            dvh = dvn * lng_ref[:, lanes]
            dvs = rstds[g] * (dvh - jnp.mean(dvh, axis=-1, keepdims=True)
                              - vhat * jnp.mean(dvh * vhat, axis=-1, keepdims=True))
            dvn_scr[:, lanes] = dvs
        dz_ref[1] = (dvn_scr[...] * _gelu_grad(sv)).astype(BF16)

    def seg(idx):
        return pl.BlockSpec((None, tm, W), lambda i, idx=idx: (idx, i, 0))

    row = pl.BlockSpec((tm, D), lambda i: (i, 0))
    vec = pl.BlockSpec((1, D), lambda i: (0, 0))
    ws_spec = pl.BlockSpec((N_GROUPS, SGU_LEN, SGU_LEN), lambda i: (0, 0, 0))
    bs_spec = pl.BlockSpec((N_GROUPS, SGU_LEN, 1), lambda i: (0, 0, 0))
    return pl.pallas_call(
        body, name=name, grid=(T // tm,),
        in_specs=[row, row, seg(3), seg(4), seg(5), seg(6), row, vec, vec, ws_spec, bs_spec,
                  pl.BlockSpec((D, D), lambda i: (0, 0)), vec, ANY],
        out_specs=[row, row, pl.BlockSpec((N_HEADS, tm, 1), lambda i: (0, i, 0)),
                   pl.BlockSpec((4, tm, W), lambda i: (0, i, 0)), vec, vec, vec, ws_spec, bs_spec],
        out_shape=[jax.ShapeDtypeStruct((T, D), BF16), jax.ShapeDtypeStruct((T, W), BF16),
                   jax.ShapeDtypeStruct((N_HEADS, T, 1), F32), jax.ShapeDtypeStruct((4, T, W), BF16),
                   jax.ShapeDtypeStruct((1, D), F32), jax.ShapeDtypeStruct((1, D), F32),
                   jax.ShapeDtypeStruct((1, D), F32),
                   jax.ShapeDtypeStruct((N_GROUPS, SGU_LEN, SGU_LEN), F32),
                   jax.ShapeDtypeStruct((N_GROUPS, SGU_LEN, 1), F32)],
        scratch_shapes=[pltpu.VMEM((tm, W), BF16), pltpu.VMEM((tm, W), F32), pltpu.VMEM((tm, W), F32),
                        pltpu.VMEM((tm, W), F32), pltpu.VMEM((tm, W), F32)],
        compiler_params=_params(("arbitrary",)),
    )(dx2, p, z7, z7, z7, z7, o_a, lng, lnb, ws, bs, w_out, g_post, _after(dep))


def _loss_head(y, target, name):
    T, D = y.shape
    tm = _blk(T, 1024)
    n_i = T // tm

    def body(y_ref, t_ref, dy_ref, loss_ref, acc_scr):
        i = pl.program_id(0)

        @pl.when(i == 0)
        def _():
            acc_scr[...] = jnp.zeros_like(acc_scr)

        e = y_ref[...] - t_ref[...]
        dy_ref[...] = e * np.float32(1.0 / D)
        acc_scr[...] += jnp.sum(e * e, axis=0, keepdims=True)

        @pl.when(i == n_i - 1)
        def _():
            total = jnp.sum(acc_scr[...], axis=-1, keepdims=True) * np.float32(0.5 / D)
            loss_ref[...] = jnp.broadcast_to(total, loss_ref.shape)

    row = pl.BlockSpec((tm, D), lambda i: (i, 0))
    return pl.pallas_call(
        body, name=name, grid=(n_i,),
        in_specs=[row, row],
        out_specs=[row, pl.BlockSpec((1, LANES), lambda i: (0, 0))],
        out_shape=[jax.ShapeDtypeStruct((T, D), F32), jax.ShapeDtypeStruct((1, LANES), F32)],
        scratch_shapes=[pltpu.VMEM((1, D), F32)],
        compiler_params=_params(("arbitrary",)),
    )(y, target)


def _adamw_math(w, g, m, v):
    m_new = ADAM_B1 * m + (1.0 - ADAM_B1) * g
    v_new = ADAM_B2 * v + (1.0 - ADAM_B2) * (g * g)
    m_hat = m_new / np.float32(1.0 - ADAM_B1 ** ADAM_STEP)
    v_hat = v_new / np.float32(1.0 - ADAM_B2 ** ADAM_STEP)
    delta = -ADAM_LR * (m_hat / (jnp.sqrt(v_hat) + ADAM_EPS) + ADAM_WD * w)
    return delta, m_new, v_new


def _sum_adamw(parts, w, m, v, name, dep=None):
    n, R, C = parts.shape
    tr = _blk(R, 128)

    def body(p_ref, w_ref, m_ref, v_ref, _, g_ref, d_ref, mo_ref, vo_ref):
        g = p_ref[0].astype(F32)
        for s in range(1, n):
            g = g + p_ref[s].astype(F32)
        delta, m_new, v_new = _adamw_math(w_ref[...], g, m_ref[...], v_ref[...])
        g_ref[...] = g
        d_ref[...] = delta
        mo_ref[...] = m_new
        vo_ref[...] = v_new

    row = pl.BlockSpec((tr, C), lambda i: (i, 0))
    shp = jax.ShapeDtypeStruct((R, C), F32)
    return pl.pallas_call(
        body, name=name, grid=(R // tr,),
        in_specs=[pl.BlockSpec((n, tr, C), lambda i: (0, i, 0)), row, row, row, ANY],
        out_specs=[row, row, row, row], out_shape=[shp, shp, shp, shp],
        compiler_params=_params(("parallel",)),
    )(parts, w, m, v, _after(dep))


def _adamw(g, w, m, v, name):
    R, C = g.shape
    tr = _blk(R, 128)

    def body(g_ref, w_ref, m_ref, v_ref, d_ref, mo_ref, vo_ref):
        delta, m_new, v_new = _adamw_math(w_ref[...], g_ref[...], m_ref[...], v_ref[...])
        d_ref[...] = delta
        mo_ref[...] = m_new
        vo_ref[...] = v_new

    row = pl.BlockSpec((tr, C), lambda i: (i, 0))
    shp = jax.ShapeDtypeStruct((R, C), F32)
    return pl.pallas_call(
        body, name=name, grid=(R // tr,),
        in_specs=[row, row, row, row], out_specs=[row, row, row], out_shape=[shp, shp, shp],
        compiler_params=_params(("parallel",)),
    )(g, w, m, v)


def _position():
    return lax.axis_index("x"), lax.axis_index("y"), lax.axis_index("c")


def _slot(px, py, pc):
    return 4 * px + 2 * py + pc


def _all_gather(shards, name):
    n = len(shards)

    def body(*refs):
        ins, outs = refs[:n], refs[n:2 * n]
        send_sems, recv_sems, local_sems = refs[2 * n:]
        x, y, c = _position()
        me, sibling = (x, y, c), (x, y, 1 - c)
        chips = [(1 - x, y), (x, 1 - y), (1 - x, 1 - y)]

        def copy(a, k, block, to, src=None):
            dst = outs[a].at[_slot(*block)]
            return pltpu.make_async_remote_copy(
                src_ref=dst if src is None else src, dst_ref=dst,
                send_sem=send_sems.at[a, k], recv_sem=recv_sems.at[a, k],
                device_id=to, device_id_type=MESH)

        mine = [pltpu.make_async_copy(ins[a], outs[a].at[_slot(*me)], local_sems.at[a]) for a in range(n)]
        for cp in mine:
            cp.start()
        first = []
        for a in range(n):
            first.append(copy(a, 0, me, sibling, src=ins[a]))
            first += [copy(a, 1 + j, me, (*chip, c), src=ins[a]) for j, chip in enumerate(chips)]
        for cp in first:
            cp.start()
        passed = []
        for j, chip in enumerate(chips):
            for a in range(n):
                copy(a, 1 + j, (*chip, c), me).wait_recv()
                fwd = copy(a, 4 + j, (*chip, c), sibling)
                fwd.start()
                passed.append(fwd)
        for a in range(n):
            copy(a, 0, sibling, me).wait_recv()
            for j, chip in enumerate(chips):
                copy(a, 4 + j, (*chip, 1 - c), me).wait_recv()
        for cp in first + passed:
            cp.wait_send()
        for cp in mine:
            cp.wait()

    return pl.pallas_call(
        body, name=name,
        in_specs=[ANY] * n, out_specs=[ANY] * n,
        out_shape=[jax.ShapeDtypeStruct((N_DEV,) + s.shape, s.dtype) for s in shards],
        scratch_shapes=[pltpu.SemaphoreType.DMA((n, 7)), pltpu.SemaphoreType.DMA((n, 7)),
                        pltpu.SemaphoreType.DMA((n,))],
    )(*shards)


def _peer(x, y, c, k):
    return (1 - x if k & 4 else x, 1 - y if k & 2 else y, 1 - c if k & 1 else c)


def _exchange(parts, name):
    n = len(parts)

    def body(*refs):
        ins, outs = refs[:n], refs[n:2 * n]
        send_sems, recv_sems, local_sems = refs[2 * n:]
        x, y, c = _position()
        me = _slot(x, y, c)
        mine = [pltpu.make_async_copy(ins[a].at[me], outs[a].at[me], local_sems.at[a]) for a in range(n)]
        for cp in mine:
            cp.start()
        sends = []
        for k in range(1, N_DEV):
            to = _peer(x, y, c, k)
            for a in range(n):
                cp = pltpu.make_async_remote_copy(
                    src_ref=ins[a].at[_slot(*to)], dst_ref=outs[a].at[me],
                    send_sem=send_sems.at[a, k - 1], recv_sem=recv_sems.at[a, k - 1],
                    device_id=to, device_id_type=MESH)
                cp.start()
                sends.append(cp)
        for k in range(1, N_DEV):
            frm = _peer(x, y, c, k)
            for a in range(n):
                pltpu.make_async_remote_copy(
                    src_ref=ins[a].at[_slot(*frm)], dst_ref=outs[a].at[_slot(*frm)],
                    send_sem=send_sems.at[a, k - 1], recv_sem=recv_sems.at[a, k - 1],
                    device_id=frm, device_id_type=MESH).wait_recv()
        for cp in sends:
            cp.wait_send()
        for cp in mine:
            cp.wait()

    return pl.pallas_call(
        body, name=name,
        in_specs=[ANY] * n, out_specs=[ANY] * n,
        out_shape=[jax.ShapeDtypeStruct(p.shape, p.dtype) for p in parts],
        scratch_shapes=[pltpu.SemaphoreType.DMA((n, 7)), pltpu.SemaphoreType.DMA((n, 7)),
                        pltpu.SemaphoreType.DMA((n,))],
    )(*parts)


HBM_SPEC = pl.BlockSpec(memory_space=pltpu.HBM)
SEM_SPEC = pl.BlockSpec(memory_space=pltpu.SEMAPHORE)
SIDE_EFFECT = pltpu.SideEffectType.DATAFLOW_SIDE_EFFECTING


def _remote_copies(src_refs, land_refs, send_sems, recv_sems, gather, outgoing):
    x, y, c = _position()
    me = _slot(x, y, c)
    copies = []
    for k in range(1, N_DEV):
        peer = _peer(x, y, c, k)
        for a in range(len(src_refs)):
            src = src_refs[a] if gather else src_refs[a].at[_slot(*peer)]
            dst = land_refs[a].at[me if outgoing else _slot(*peer)]
            sem = a * (N_DEV - 1) + k - 1
            copies.append(pltpu.make_async_remote_copy(
                src_ref=src, dst_ref=dst, send_sem=send_sems.at[sem], recv_sem=recv_sems.at[sem],
                device_id=peer, device_id_type=MESH))
    return copies


def _remote_start(srcs, after, name, gather):
    n = len(srcs)
    lands = [jax.ShapeDtypeStruct(((N_DEV,) + s.shape) if gather else s.shape, s.dtype) for s in srcs]

    def body(*refs):
        src_refs, land_refs = refs[:n], refs[n:2 * n]
        send_sems, recv_sems = refs[2 * n + 1], refs[2 * n + 2]
        token, local_sems = refs[4 * n + 3], refs[4 * n + 4]
        x, y, c = _position()
        me = _slot(x, y, c)
        mine = [pltpu.make_async_copy(src_refs[a] if gather else src_refs[a].at[me], land_refs[a].at[me],
                                      local_sems.at[a]) for a in range(n)]
        for cp in mine:
            cp.start()
        for cp in _remote_copies(src_refs, land_refs, send_sems, recv_sems, gather, outgoing=True):
            cp.start()
        for cp in mine:
            cp.wait()
        token[...] = jnp.zeros_like(token)

    sem_shape = pltpu.SemaphoreType.DMA((n * (N_DEV - 1),))
    outs = pl.pallas_call(
        body, name=name,
        out_shape=(sem_shape, sem_shape, *[pltpu.HBM(s.shape, s.dtype) for s in srcs],
                   *[pltpu.HBM(l.shape, l.dtype) for l in lands], jax.ShapeDtypeStruct((8, LANES), F32)),
        in_specs=[HBM_SPEC] * (2 * n) + [ANY],
        out_specs=(SEM_SPEC, SEM_SPEC, *([HBM_SPEC] * (2 * n)), pl.BlockSpec(memory_space=pltpu.VMEM)),
        input_output_aliases={a: 2 + a for a in range(2 * n)},
        scratch_shapes=[pltpu.SemaphoreType.DMA((n,))],
        compiler_params=pltpu.CompilerParams(has_side_effects=SIDE_EFFECT),
    )(*[pltpu.with_memory_space_constraint(s, pltpu.HBM) for s in srcs],
      *[pltpu.with_memory_space_constraint(lax.empty(l.shape, l.dtype), pltpu.HBM) for l in lands], after)
    return dict(send=outs[0], recv=outs[1], srcs=outs[2:2 + n], lands=outs[2 + n:2 + 2 * n], token=outs[-1],
                gather=gather)


def _remote_wait(flight, after, name):
    n = len(flight["srcs"])
    gather = flight["gather"]

    def body(*refs):
        src_refs, land_refs = refs[:n], refs[n:2 * n]
        send_sems, recv_sems = refs[2 * n], refs[2 * n + 1]
        for cp in _remote_copies(src_refs, land_refs, send_sems, recv_sems, gather, outgoing=False):
            cp.wait_send()
            cp.wait_recv()

    both = list(flight["srcs"]) + list(flight["lands"])
    outs = pl.pallas_call(
        body, name=name,
        out_shape=tuple(pltpu.HBM(a.shape, a.dtype) for a in both),
        in_specs=[HBM_SPEC] * (2 * n) + [SEM_SPEC, SEM_SPEC, ANY],
        out_specs=tuple([HBM_SPEC] * (2 * n)),
        input_output_aliases={a: a for a in range(2 * n)},
        compiler_params=pltpu.CompilerParams(has_side_effects=SIDE_EFFECT),
    )(*both, flight["send"], flight["recv"], after)
    return list(outs[n:])


def _sequencer_exchange(srcs, name, gather, collective_id):
    n = len(srcs)
    hbm = pltpu.MemorySpace.HBM
    src_refs = [jax.new_ref(s, memory_space=hbm) for s in srcs]
    land_refs = [jax.empty_ref(jax.ShapeDtypeStruct(((N_DEV,) + s.shape) if gather else s.shape, s.dtype),
                               memory_space=hbm) for s in srcs]
    n_sems = n * (N_DEV - 1)
    block_bytes = sum(s.size * s.dtype.itemsize // (1 if gather else N_DEV) for s in srcs)
    cost = pl.CostEstimate(flops=0, transcendentals=0, bytes_accessed=2 * N_DEV * block_bytes,
                           remote_bytes_transferred=(N_DEV - 1) * block_bytes)

    @pl.kernel(mesh=plsc.ScalarSubcoreMesh(axis_name="sequencer", num_cores=1), name=name,
               scratch_types=(pltpu.SemaphoreType.DMA((n_sems,)), pltpu.SemaphoreType.DMA((n_sems,)),
                              pltpu.SemaphoreType.DMA((n,))),
               cost_estimate=cost,
               compiler_params=pltpu.CompilerParams(collective_id=collective_id))
    def launch(send_sems, recv_sems, local_sems):
        x, y, c = _position()
        me = _slot(x, y, c)
        barrier = pltpu.get_barrier_semaphore()
        for k in range(1, N_DEV):
            pl.semaphore_signal(barrier, inc=1, device_id=_peer(x, y, c, k), device_id_type=MESH)
        pl.semaphore_wait(barrier, N_DEV - 1)
        mine = [pltpu.make_async_copy(src_refs[a] if gather else src_refs[a].at[me], land_refs[a].at[me],
                                      local_sems.at[a]) for a in range(n)]
        for cp in mine:
            cp.start()
        sends = _remote_copies(src_refs, land_refs, send_sems, recv_sems, gather, outgoing=True)
        for cp in sends:
            cp.start()
        for cp in _remote_copies(src_refs, land_refs, send_sems, recv_sems, gather, outgoing=False):
            cp.wait_recv()
        for cp in sends:
            cp.wait_send()
        for cp in mine:
            cp.wait()

    launch()
    return [r[...] for r in land_refs]


def _all_reduce_small(blob, name):
    R, C = blob.shape

    def body(in_ref, out_ref, gath, send_sems, recv_sems):
        x, y, c = _position()
        me = _slot(x, y, c)
        gath[me] = in_ref[...]
        sends = []
        for k in range(1, N_DEV):
            to = _peer(x, y, c, k)
            cp = pltpu.make_async_remote_copy(
                src_ref=in_ref, dst_ref=gath.at[me],
                send_sem=send_sems.at[k - 1], recv_sem=recv_sems.at[k - 1],
                device_id=to, device_id_type=MESH)
            cp.start()
            sends.append(cp)
        for k in range(1, N_DEV):
            frm = _peer(x, y, c, k)
            pltpu.make_async_remote_copy(
                src_ref=in_ref, dst_ref=gath.at[_slot(*frm)],
                send_sem=send_sems.at[k - 1], recv_sem=recv_sems.at[k - 1],
                device_id=frm, device_id_type=MESH).wait_recv()
        for cp in sends:
            cp.wait_send()
        total = gath[0]
        for s in range(1, N_DEV):
            total = total + gath[s]
        out_ref[...] = total

    return pl.pallas_call(
        body, name=name,
        in_specs=[pl.BlockSpec(memory_space=pltpu.VMEM)],
        out_specs=pl.BlockSpec(memory_space=pltpu.VMEM),
        out_shape=jax.ShapeDtypeStruct((R, C), F32),
        scratch_shapes=[pltpu.VMEM((N_DEV, R, C), F32), pltpu.SemaphoreType.DMA((7,)),
                        pltpu.SemaphoreType.DMA((7,))],
        compiler_params=pltpu.CompilerParams(vmem_limit_bytes=VMEM_LIMIT),
    )(blob)


SMALL_VECS = ("ffn1_pre_g", "ffn1_post_g", "mix_pre_g", "sgu_ln_g", "sgu_ln_b", "mix_post_g", "ffn2_pre_g",
              "ffn2_post_g")
ROW_BS = len(SMALL_VECS)
ROW_BF = ROW_BS + 1
ROW_LOSS = ROW_BF + 1
ROW_WS = 16
BLOB_ROWS = ROW_WS + SGU_LEN


def _pack_small(vals, D, loss_row=None):
    rows = [vals[n].reshape(1, D) for n in SMALL_VECS]
    rows.append(vals["sgu_b_s"].reshape(1, D))
    rows.append(jnp.pad(vals["b_forget"].reshape(1, N_HEADS), ((0, 0), (0, D - N_HEADS))))
    rows.append(jnp.zeros((1, D), F32) if loss_row is None else loss_row)
    rows.append(jnp.zeros((ROW_WS - ROW_LOSS - 1, D), F32))
    rows.append(vals["sgu_w_s"].reshape(SGU_LEN, D))
    return jnp.concatenate(rows, axis=0)


def _unpack_small(blob, D):
    out = {n: blob[r:r + 1] for r, n in enumerate(SMALL_VECS)}
    out["sgu_b_s"] = blob[ROW_BS].reshape(1, N_GROUPS, SGU_LEN)
    out["b_forget"] = blob[ROW_BF, :N_HEADS].reshape(1, N_HEADS)
    out["sgu_w_s"] = blob[ROW_WS:].reshape(1, N_GROUPS, SGU_LEN, SGU_LEN)
    return out


WEIGHT_NAMES = ("ffn1_pre_g", "ffn1_w_gate", "ffn1_w_up", "ffn1_w_down", "ffn1_post_g", "mix_pre_g", "w_in",
                "b_forget", "sgu_ln_g", "sgu_ln_b", "sgu_w_s", "sgu_b_s", "w_out", "mix_post_g", "ffn2_pre_g",
                "ffn2_w_gate", "ffn2_w_up", "ffn2_w_down", "ffn2_post_g")
BIG_NAMES = ("ffn1_w_gate", "ffn1_w_up", "ffn1_w_down", "w_in", "w_out", "ffn2_w_gate", "ffn2_w_up", "ffn2_w_down")
WEIGHT_GROUPS = {"ffn1": ("ffn1_w_gate", "ffn1_w_up", "ffn1_w_down"), "mix": ("w_in", "w_out"),
                 "ffn2": ("ffn2_w_gate", "ffn2_w_up", "ffn2_w_down")}
GRAD_GROUPS = (("ffn2_w_gate", "ffn2_w_up", "ffn2_w_down"), ("w_in", "w_out"), ("ffn1_w_down",), ("ffn1_w_gate",),
               ("ffn1_w_up",))


def _local_step(x, target, small, fetch, emit, consume):
    T, D = x.shape
    W = N_HEADS * HEAD_DIM
    vec = lambda n: small[n].reshape(1, D)
    big = dict(fetch("ffn1", x))

    x1, y1, dgf1, silu1, act1 = _ffn_fwd(x, vec("ffn1_pre_g"), big["ffn1_w_gate"], big["ffn1_w_up"], big["ffn1_w_down"],
                                  vec("ffn1_post_g"), "ffn1_fwd")

    big.update(fetch("mix", x1))
    w_in_all = big["w_in"]
    in_width = N_DEV * w_in_all.shape[2]
    w_in = w_in_all.transpose(1, 0, 2).reshape(D, in_width)
    col_f = 3 * W
    col_u = col_f + N_HEADS
    seg_starts = (0, W, 2 * W, col_u, col_u + W, col_u + 2 * W, col_u + 3 * W)
    w7 = jnp.stack([w_in[:, s:s + W] for s in seg_starts])
    wf = jnp.pad(w_in[:, col_f:col_u], ((0, 0), (0, LANES - N_HEADS)))
    w_out = big["w_out"].reshape(D, D)
    b_pad = jnp.pad(small["b_forget"].reshape(1, N_HEADS), ((0, 0), (0, LANES - N_HEADS)))
    lng, lnb = vec("sgu_ln_g"), vec("sgu_ln_b")
    ws = small["sgu_w_s"].reshape(N_GROUPS, SGU_LEN, SGU_LEN)
    bs = small["sgu_b_s"].reshape(N_GROUPS, SGU_LEN, 1)

    z7, f_logit, h2b = _mix_in_fwd(x1, vec("mix_pre_g"), w7, wf, "mix_in_fwd")
    c = _forget_cumsum(f_logit, b_pad, "forget_cumsum")
    c_heads = c[:, :N_HEADS].T
    ta, _, n_chunks = _attn_geometry(T)
    c_chunks = c_heads.reshape(N_HEADS, n_chunks, 1, ta)
    c_col = c_heads[:, :, None]
    vt = z7[2].reshape(n_chunks, ta, N_HEADS, HEAD_DIM).transpose(2, 0, 3, 1)
    c_rep = jnp.broadcast_to(c_col * LOG2E, (N_HEADS, T, LANES))
    o_a, lse_chunks = _attn_fwd_keys_on_rows(z7, vt, c_rep, "attn_fwd")
    lse = lse_chunks.reshape(N_HEADS, T, 1)
    x2, p, merged_b = _mix_out_fwd(z7, o_a, x1, lng, lnb, ws, bs, w_out, vec("mix_post_g"), "mix_out_fwd")
    big.update(fetch("ffn2", x2))
    x3, y2, dgf2, silu2, act2 = _ffn_fwd(x2, vec("ffn2_pre_g"), big["ffn2_w_gate"], big["ffn2_w_up"], big["ffn2_w_down"],
                                  vec("ffn2_post_g"), "ffn2_fwd")
    dy, loss_lanes = _loss_head(x3, target, "loss_head")

    grads_small = {}

    dx2, h3b, dy2b, dgate2, dup2, dgpre, dgpost = _ffn_bwd(
        dy, x2, y2, dgf2, silu2, vec("ffn2_pre_g"), big["ffn2_w_gate"], big["ffn2_w_up"], big["ffn2_w_down"],
        vec("ffn2_post_g"), "ffn2_bwd")
    grads_small["ffn2_pre_g"] = jnp.sum(dgpre, axis=0)
    grads_small["ffn2_post_g"] = jnp.sum(dgpost, axis=0)
    dep = emit("ffn2_w_gate", _wgrad(h3b, dgate2, "ffn2_wgrad_gate", shard_cols=True))
    dep = emit("ffn2_w_up", _wgrad(h3b, dup2, "ffn2_wgrad_up", shard_cols=True, dep=dep))
    dep = emit("ffn2_w_down", _wgrad(act2, dy2b, "ffn2_wgrad_down", dep=dep).reshape(big["ffn2_w_down"].shape))

    dpb, dob, dvec, dz4, dgp, dlng, dlnb, dws, dbs = _mix_out_bwd(
        dx2, p, z7, o_a, lng, lnb, ws, bs, w_out, vec("mix_post_g"), "mix_out_bwd", dep=dep)
    grads_small["mix_post_g"] = dgp
    grads_small["sgu_ln_g"] = dlng
    grads_small["sgu_ln_b"] = dlnb
    grads_small["sgu_w_s"] = dws
    grads_small["sgu_b_s"] = dbs
    d_chunks = dvec.reshape(N_HEADS, n_chunks, 1, ta)
    kt = z7[1].reshape(n_chunks, ta, N_HEADS, HEAD_DIM).transpose(2, 0, 3, 1)
    dk, dv, dc, dq, dc_q = _attn_bwd_fused(z7, kt, dob, c_rep, lse_chunks, d_chunks, "attn_bwd")
    dc_pad = jnp.pad((dc + dc_q).reshape(N_HEADS, T).T, ((0, 0), (0, LANES - N_HEADS)))
    dfb, dbf = _forget_bwd(dc_pad, f_logit, b_pad, "forget_bwd")
    grads_small["b_forget"] = dbf[:, :N_HEADS]
    segs = [(dq, None), (dk, None), (dv, None), (dz4, 0), (dz4, 1), (dz4, 2), (dz4, 3)]
    dep = consume(("ffn2_w_gate", "ffn2_w_up", "ffn2_w_down"))
    dx1, dgm = _mix_in_bwd(dx2, x1, vec("mix_pre_g"), segs, dfb, w7, wf, "mix_in_bwd", dep=dep)
    grads_small["mix_pre_g"] = jnp.sum(dgm, axis=0)
    dw_seg, dep = [], dx1
    for q, (sm, idx) in enumerate(segs):
        dw_seg.append(_wgrad(h2b, sm, "w_in_wgrad_%d" % q, dep=dep, y_index=idx))
        dep = dw_seg[-1]
    dwf = _wgrad(h2b, dfb, "w_in_wgrad_f", dep=dep)
    dw_in = jnp.concatenate(dw_seg[:3] + [dwf[:, :N_HEADS]] + dw_seg[3:], axis=1)
    emit("w_in", dw_in.reshape(D, N_DEV, in_width // N_DEV).transpose(1, 0, 2))
    dep = emit("w_out", _wgrad(merged_b, dpb, "w_out_wgrad", dep=dwf).reshape(big["w_out"].shape))

    dx0, h1b, dy1b, dgate1, dup1, dgpre1, dgpost1 = _ffn_bwd(
        dx1, x, y1, dgf1, silu1, vec("ffn1_pre_g"), big["ffn1_w_gate"], big["ffn1_w_up"], big["ffn1_w_down"],
        vec("ffn1_post_g"), "ffn1_bwd", dep=dep)
    grads_small["ffn1_pre_g"] = jnp.sum(dgpre1, axis=0)
    grads_small["ffn1_post_g"] = jnp.sum(dgpost1, axis=0)
    dep = consume(("w_in", "w_out"))
    dep = emit("ffn1_w_down", _wgrad(act1, dy1b, "ffn1_wgrad_down", dep=dep).reshape(big["ffn1_w_down"].shape))
    dep = emit("ffn1_w_gate", _wgrad(h1b, dgate1, "ffn1_wgrad_gate", shard_cols=True, dep=dep))
    dep = emit("ffn1_w_up", _wgrad(h1b, dup1, "ffn1_wgrad_up", shard_cols=True, dep=dep))

    loss_row = jnp.pad(loss_lanes, ((0, 0), (0, D - LANES)))
    return loss_row, dx0, grads_small


def kernel(x, ffn1_pre_g, ffn1_w_gate, ffn1_w_up, ffn1_w_down, ffn1_post_g, mix_pre_g, w_in, b_forget, sgu_ln_g, sgu_ln_b, sgu_w_s, sgu_b_s, w_out, mix_post_g, ffn2_pre_g, ffn2_w_gate, ffn2_w_up, ffn2_w_down, ffn2_post_g, loss_target, m_ffn1_pre_g, m_ffn1_w_gate, m_ffn1_w_up, m_ffn1_w_down, m_ffn1_post_g, m_mix_pre_g, m_w_in, m_b_forget, m_sgu_ln_g, m_sgu_ln_b, m_sgu_w_s, m_sgu_b_s, m_w_out, m_mix_post_g, m_ffn2_pre_g, m_ffn2_w_gate, m_ffn2_w_up, m_ffn2_w_down, m_ffn2_post_g, v_ffn1_pre_g, v_ffn1_w_gate, v_ffn1_w_up, v_ffn1_w_down, v_ffn1_post_g, v_mix_pre_g, v_w_in, v_b_forget, v_sgu_ln_g, v_sgu_ln_b, v_sgu_w_s, v_sgu_b_s, v_w_out, v_mix_post_g, v_ffn2_pre_g, v_ffn2_w_gate, v_ffn2_w_up, v_ffn2_w_down, v_ffn2_post_g):
    weights = dict(zip(WEIGHT_NAMES, (ffn1_pre_g, ffn1_w_gate, ffn1_w_up, ffn1_w_down, ffn1_post_g, mix_pre_g, w_in,
                                      b_forget, sgu_ln_g, sgu_ln_b, sgu_w_s, sgu_b_s, w_out, mix_post_g, ffn2_pre_g,
                                      ffn2_w_gate, ffn2_w_up, ffn2_w_down, ffn2_post_g)))
    mom1 = dict(zip(WEIGHT_NAMES, (m_ffn1_pre_g, m_ffn1_w_gate, m_ffn1_w_up, m_ffn1_w_down, m_ffn1_post_g,
                                   m_mix_pre_g, m_w_in, m_b_forget, m_sgu_ln_g, m_sgu_ln_b, m_sgu_w_s, m_sgu_b_s,
                                   m_w_out, m_mix_post_g, m_ffn2_pre_g, m_ffn2_w_gate, m_ffn2_w_up, m_ffn2_w_down,
                                   m_ffn2_post_g)))
    mom2 = dict(zip(WEIGHT_NAMES, (v_ffn1_pre_g, v_ffn1_w_gate, v_ffn1_w_up, v_ffn1_w_down, v_ffn1_post_g,
                                   v_mix_pre_g, v_w_in, v_b_forget, v_sgu_ln_g, v_sgu_ln_b, v_sgu_w_s, v_sgu_b_s,
                                   v_w_out, v_mix_post_g, v_ffn2_pre_g, v_ffn2_w_gate, v_ffn2_w_up, v_ffn2_w_down,
                                   v_ffn2_post_g)))
    D = x.shape[-1]
    small_names = [n for n in WEIGHT_NAMES if n not in BIG_NAMES]

    small = {n: weights[n] for n in small_names}
    shard = lambda n: weights[n][0].astype(BF16)

    ffn1_full = _all_gather([shard(n) for n in WEIGHT_GROUPS["ffn1"]], "ffn1_all_gather")
    gathered = {}
    for cid, grp in ((1, "mix"), (2, "ffn2")):
        shards, _ = lax.optimization_barrier(([shard(n) for n in WEIGHT_GROUPS[grp]], ffn1_full[0]))
        gathered[grp] = _sequencer_exchange(shards, grp + "_gather", True, cid)

    def fetch(group, after):
        if group == "ffn1":
            return zip(WEIGHT_GROUPS[group], ffn1_full)
        arrived, _ = lax.optimization_barrier((gathered[group], after))
        return zip(WEIGHT_GROUPS[group], arrived)

    ready, received = {}, {}

    def emit(name, part):
        ready[name] = part
        for gi, group in enumerate(GRAD_GROUPS):
            if name == group[-1]:
                lands = _sequencer_exchange([ready[n] for n in group], name + "_grad_exchange", False, 3 + gi)
                received.update(zip(group, lands))
        return part

    out = {}

    def consume(names, dep=None):
        for n in names:
            g, d, m_new, v_new = _sum_adamw(received[n], weights[n][0], mom1[n][0], mom2[n][0], "adamw_" + n, dep=dep)
            out[n] = tuple(a[None] for a in (g, d, m_new, v_new))
            dep = g
        return dep

    loss_row, grad_x, grads_small = _local_step(x[0], loss_target[0], small, fetch, emit, consume)

    blobs = _sequencer_exchange([_pack_small(grads_small, D, loss_row)], "small_gather", True,
                                3 + len(GRAD_GROUPS))[0]
    blob, d_blob, m_blob, v_blob = _sum_adamw(
        blobs, _pack_small(small, D), _pack_small({n: mom1[n] for n in small_names}, D),
        _pack_small({n: mom2[n] for n in small_names}, D), "adamw_small")
    consume(("ffn1_w_down", "ffn1_w_gate", "ffn1_w_up"), dep=blob)
    unpacked = [_unpack_small(b, D) for b in (blob, d_blob, m_blob, v_blob)]
    for n in small_names:
        out[n] = tuple(u[n].reshape(weights[n].shape) for u in unpacked)

    loss = blob[ROW_LOSS, 0]
    result = [loss, grad_x[None]]
    for k in range(4):
        result += [out[n][k] for n in WEIGHT_NAMES]
    return tuple(result)
```

```python
import functools

import numpy as np
import jax
import jax.numpy as jnp
from jax import lax
from jax.experimental import pallas as pl
from jax.experimental.pallas import tpu as pltpu
from jax.experimental.pallas import tpu_sc as plsc

F32 = jnp.float32
BF16 = jnp.bfloat16

RMS_EPS = 1e-6
LN_EPS = 1e-5
HEAD_DIM = 128
N_HEADS = 8
GROUP_DIM = 128
N_GROUPS = 8
SGU_LEN = 128
CHUNK = 64
N_DEV = 8
LANES = 128
VMEM_LIMIT = 56 * 1024 * 1024
NEG_BIG = -1e30
LOG2E = np.float32(1.0 / np.log(2.0))

ADAM_LR = 0.001
ADAM_B1 = 0.9
ADAM_B2 = 0.999
ADAM_EPS = 1e-08
ADAM_WD = 0.01
ADAM_STEP = 10

MESH = pl.DeviceIdType.MESH
ANY = pl.BlockSpec(memory_space=pl.ANY)


def _blk(n, pref):
    return pref if (n >= pref and n % pref == 0) else n


def _mm(a, b):
    return jnp.dot(a, b, preferred_element_type=F32)


def _mm_nt(a, b):
    return lax.dot_general(a, b, (((1,), (1,)), ((), ())), preferred_element_type=F32)


def _mm_tn(a, b):
    return lax.dot_general(a, b, (((0,), (0,)), ((), ())), preferred_element_type=F32)


def _params(sem):
    return pltpu.CompilerParams(dimension_semantics=sem, vmem_limit_bytes=VMEM_LIMIT)


def _gelu(x):
    return 0.5 * x * (1.0 + lax.erf(x * np.float32(1.0 / np.sqrt(2.0))))


def _gelu_grad(x):
    cdf = 0.5 * (1.0 + lax.erf(x * np.float32(1.0 / np.sqrt(2.0))))
    return cdf + x * jnp.exp(-0.5 * x * x) * np.float32(1.0 / np.sqrt(2.0 * np.pi))


def _rms_scale(v):
    return lax.rsqrt(jnp.mean(v * v, axis=-1, keepdims=True) + RMS_EPS)


def _rms_bwd(dy, xhat, r, g):
    dxh = dy * g
    return r * (dxh - xhat * jnp.mean(dxh * xhat, axis=-1, keepdims=True))


def _ffn_rows(T):
    tm = _blk(T, 1024)
    th = _blk(tm, 512)
    return tm, th, tm // th


def _ffn_fwd(x, g_pre, wg, wu, wd, g_post, name):
    T, D = x.shape
    ns, _, fs = wg.shape
    tm, th, parts = _ffn_rows(T)

    def body(x_ref, gpre_ref, wg_ref, wu_ref, wd_ref, gpost_ref, xo_ref, y_ref, dgf_ref, silu_ref, act_ref,
             h_scr, acc_scr):
        j = pl.program_id(1)

        @pl.when(j == 0)
        def _():
            for r in range(parts):
                rows = slice(r * th, (r + 1) * th)
                xv = x_ref[rows, :]
                h_scr[rows, :] = (xv * _rms_scale(xv) * gpre_ref[...]).astype(BF16)
            acc_scr[...] = jnp.zeros_like(acc_scr)

        pre = []
        for r in range(parts):
            h = h_scr[r * th:(r + 1) * th, :]
            pre.append((_mm(h, wg_ref[...]), _mm(h, wu_ref[...])))
        for r in range(parts):
            rows = slice(r * th, (r + 1) * th)
            gg, uu = pre[r]
            sg = jax.nn.sigmoid(gg)
            silu = gg * sg
            act = (silu * uu).astype(BF16)
            dgf_ref[rows, :] = (uu * (sg * (1.0 + gg * (1.0 - sg)))).astype(BF16)
            silu_ref[rows, :] = silu.astype(BF16)
            act_ref[rows, :] = act
            acc_scr[rows, :] += _mm(act, wd_ref[...])

        @pl.when(j == ns - 1)
        def _():
            for r in range(parts):
                rows = slice(r * th, (r + 1) * th)
                y = acc_scr[rows, :]
                y_ref[rows, :] = y
                xo_ref[rows, :] = x_ref[rows, :] + 0.5 * (y * _rms_scale(y) * gpost_ref[...])

    row = pl.BlockSpec((tm, D), lambda i, j: (i, 0), pipeline_mode=pl.Buffered(1))
    vec = pl.BlockSpec((1, D), lambda i, j: (0, 0))
    return pl.pallas_call(
        body, name=name, grid=(T // tm, ns),
        in_specs=[pl.BlockSpec((tm, D), lambda i, j: (i, 0)), vec,
                  pl.BlockSpec((None, D, fs), lambda i, j: (j, 0, 0)),
                  pl.BlockSpec((None, D, fs), lambda i, j: (j, 0, 0)),
                  pl.BlockSpec((None, fs, D), lambda i, j: (j, 0, 0)),
                  vec],
        out_specs=[row, row] + [pl.BlockSpec((tm, fs), lambda i, j: (i, j))] * 3,
        out_shape=[jax.ShapeDtypeStruct((T, D), F32), jax.ShapeDtypeStruct((T, D), F32)]
        + [jax.ShapeDtypeStruct((T, ns * fs), BF16)] * 3,
        scratch_shapes=[pltpu.VMEM((tm, D), BF16), pltpu.VMEM((tm, D), F32)],
        compiler_params=_params(("parallel", "arbitrary")),
    )(x, g_pre, wg, wu, wd, g_post)


def _after(dep):
    return jnp.zeros((8, LANES), F32) if dep is None else dep


def _ffn_bwd(dxo, x, y, dgf, silu, g_pre, wg, wu, wd, g_post, name, dep=None):
    T, D = x.shape
    ns, _, fs = wg.shape
    tm, th, parts = _ffn_rows(T)
    n_i = T // tm

    def body(dxo_ref, x_ref, y_ref, dgf_ref, silu_ref, gpre_ref, wg_ref, wu_ref, wd_ref, gpost_ref, _,
             dx_ref, hb_ref, dyb_ref, dgb_ref, dub_ref, dgpre_ref, dgpost_ref, dy_scr, acc_scr):
        j = pl.program_id(1)

        @pl.when(j == 0)
        def _():
            dgpost = jnp.zeros((1, D), F32)
            for r in range(parts):
                rows = slice(r * th, (r + 1) * th)
                yv = y_ref[rows, :]
                s = _rms_scale(yv)
                n = yv * s
                dn = 0.5 * dxo_ref[rows, :]
                dgpost = dgpost + jnp.sum(dn * n, axis=0, keepdims=True)
                dyv = _rms_bwd(dn, n, s, gpost_ref[...]).astype(BF16)
                dy_scr[rows, :] = dyv
                dyb_ref[rows, :] = dyv
                xv = x_ref[rows, :]
                hb_ref[rows, :] = (xv * _rms_scale(xv) * gpre_ref[...]).astype(BF16)
            dgpost_ref[...] = dgpost
            acc_scr[...] = jnp.zeros_like(acc_scr)

        das = [_mm_nt(dy_scr[r * th:(r + 1) * th, :], wd_ref[...]) for r in range(parts)]
        for r in range(parts):
            rows = slice(r * th, (r + 1) * th)
            dgate = (das[r] * dgf_ref[rows, :].astype(F32)).astype(BF16)
            dup = (das[r] * silu_ref[rows, :].astype(F32)).astype(BF16)
            dgb_ref[rows, :] = dgate
            dub_ref[rows, :] = dup
            acc_scr[rows, :] += _mm_nt(dgate, wg_ref[...]) + _mm_nt(dup, wu_ref[...])

        @pl.when(j == ns - 1)
        def _():
            dgpre = jnp.zeros((1, D), F32)
            for r in range(parts):
                rows = slice(r * th, (r + 1) * th)
                xv = x_ref[rows, :]
                rs = _rms_scale(xv)
                xhat = xv * rs
                dh = acc_scr[rows, :]
                dgpre = dgpre + jnp.sum(dh * xhat, axis=0, keepdims=True)
                dx_ref[rows, :] = _rms_bwd(dh, xhat, rs, gpre_ref[...]) + dxo_ref[rows, :]
            dgpre_ref[...] = dgpre

    row = pl.BlockSpec((tm, D), lambda i, j: (i, 0), pipeline_mode=pl.Buffered(1))
    vec = pl.BlockSpec((1, D), lambda i, j: (0, 0))
    wide = pl.BlockSpec((tm, fs), lambda i, j: (i, j))
    part = pl.BlockSpec((None, 1, D), lambda i, j: (i, 0, 0))
    F = ns * fs
    return pl.pallas_call(
        body, name=name, grid=(n_i, ns),
        in_specs=[row, row, row, wide, wide, vec,
                  pl.BlockSpec((None, D, fs), lambda i, j: (j, 0, 0)),
                  pl.BlockSpec((None, D, fs), lambda i, j: (j, 0, 0)),
                  pl.BlockSpec((None, fs, D), lambda i, j: (j, 0, 0)),
                  vec, ANY],
        out_specs=[row, row, row, wide, wide, part, part],
        out_shape=[jax.ShapeDtypeStruct((T, D), F32), jax.ShapeDtypeStruct((T, D), BF16),
                   jax.ShapeDtypeStruct((T, D), BF16), jax.ShapeDtypeStruct((T, F), BF16),
                   jax.ShapeDtypeStruct((T, F), BF16),
                   jax.ShapeDtypeStruct((n_i, 1, D), F32), jax.ShapeDtypeStruct((n_i, 1, D), F32)],
        scratch_shapes=[pltpu.VMEM((tm, D), BF16), pltpu.VMEM((tm, D), F32)],
        compiler_params=_params(("parallel", "arbitrary")),
    )(dxo, x, y, dgf, silu, g_pre, wg, wu, wd, g_post, _after(dep))


def _wgrad(xm, ym, name, shard_cols=False, dep=None, y_index=None):
    T, M = xm.shape
    N = ym.shape[-1]
    if y_index is None:
        y_spec = pl.BlockSpec((_blk(T, 512), N), lambda k: (k, 0))
    else:
        y_spec = pl.BlockSpec((None, _blk(T, 512), N), lambda k: (y_index, k, 0))
    assert M * N * 4 <= 16 * 1024 * 1024, (M, N)
    tk = _blk(T, 512)
    n_k = T // tk
    fs = N // N_DEV

    def body(x_ref, y_ref, _, o_ref, acc_scr):
        k = pl.program_id(0)

        @pl.when(k == 0)
        def _():
            acc_scr[...] = jnp.zeros_like(acc_scr)

        acc_scr[...] += _mm_tn(x_ref[...], y_ref[...])

        @pl.when(k == n_k - 1)
        def _():
            if shard_cols:
                for s in range(N_DEV):
                    o_ref[s] = acc_scr[:, s * fs:(s + 1) * fs].astype(BF16)
            else:
                o_ref[...] = acc_scr[...].astype(BF16)

    if shard_cols:
        out_spec = pl.BlockSpec((N_DEV, M, fs), lambda k: (0, 0, 0), pipeline_mode=pl.Buffered(1))
        out_shape = jax.ShapeDtypeStruct((N_DEV, M, fs), BF16)
    else:
        out_spec = pl.BlockSpec((M, N), lambda k: (0, 0), pipeline_mode=pl.Buffered(1))
        out_shape = jax.ShapeDtypeStruct((M, N), BF16)
    return pl.pallas_call(
        body, name=name, grid=(n_k,),
        in_specs=[pl.BlockSpec((tk, M), lambda k: (k, 0)), y_spec, ANY],
        out_specs=out_spec, out_shape=out_shape,
        scratch_shapes=[pltpu.VMEM((M, N), F32)],
        compiler_params=_params(("arbitrary",)),
    )(xm, ym, _after(dep))


def _mix_in_fwd(x1, g, w7, wf, name):
    T, D = x1.shape
    n_seg, _, W = w7.shape
    tm = _blk(T, 1024)

    def body(x_ref, g_ref, w_ref, wf_ref, z_ref, f_ref, hb_ref, h_scr):
        s = pl.program_id(1)

        @pl.when(s == 0)
        def _():
            xv = x_ref[...]
            h = (xv * _rms_scale(xv) * g_ref[...]).astype(BF16)
            h_scr[...] = h
            hb_ref[...] = h
            f_ref[...] = _mm(h, wf_ref[...])

        z_ref[...] = _mm(h_scr[...], w_ref[...]).astype(BF16)

    return pl.pallas_call(
        body, name=name, grid=(T // tm, n_seg),
        in_specs=[pl.BlockSpec((tm, D), lambda i, s: (i, 0)),
                  pl.BlockSpec((1, D), lambda i, s: (0, 0)),
                  pl.BlockSpec((None, D, W), lambda i, s: (s, 0, 0)),
                  pl.BlockSpec((D, LANES), lambda i, s: (0, 0))],
        out_specs=[pl.BlockSpec((None, tm, W), lambda i, s: (s, i, 0)),
                   pl.BlockSpec((tm, LANES), lambda i, s: (i, 0)),
                   pl.BlockSpec((tm, D), lambda i, s: (i, 0))],
        out_shape=[jax.ShapeDtypeStruct((n_seg, T, W), BF16), jax.ShapeDtypeStruct((T, LANES), F32),
                   jax.ShapeDtypeStruct((T, D), BF16)],
        scratch_shapes=[pltpu.VMEM((tm, D), BF16)],
        compiler_params=_params(("parallel", "arbitrary")),
    )(x1, g, w7, wf)


def _mix_in_bwd(dx2, x1, g, segs, dfb, w7, wf, name, dep=None):
    T, D = x1.shape
    n_seg, _, W = w7.shape
    tm, th, parts = _ffn_rows(T)
    n_i = T // tm

    def body(*refs):
        dx2_ref, x_ref, g_ref = refs[:3]
        seg_refs = refs[3:3 + n_seg]
        df_ref, w_ref, wf_ref, _, dx1_ref, dg_ref, acc_scr = refs[3 + n_seg:]
        s = pl.program_id(1)

        @pl.when(s == 0)
        def _():
            acc_scr[...] = _mm_nt(df_ref[...], wf_ref[...])

        for q in range(n_seg):
            @pl.when(s == q)
            def _(q=q):
                acc_scr[...] += _mm_nt(seg_refs[q][...], w_ref[...])

        @pl.when(s == n_seg - 1)
        def _():
            dg = jnp.zeros((1, D), F32)
            for p in range(parts):
                rows = slice(p * th, (p + 1) * th)
                xv = x_ref[rows, :]
                r = _rms_scale(xv)
                xhat = xv * r
                dh = acc_scr[rows, :]
                dg = dg + jnp.sum(dh * xhat, axis=0, keepdims=True)
                dx1_ref[rows, :] = _rms_bwd(dh, xhat, r, g_ref[...]) + dx2_ref[rows, :]
            dg_ref[...] = dg

    row = pl.BlockSpec((tm, D), lambda i, s: (i, 0), pipeline_mode=pl.Buffered(1))
    seg_specs = []
    seg_args = []
    for arr, idx in segs:
        if idx is None:
            seg_specs.append(pl.BlockSpec((tm, W), lambda i, s: (i, 0)))
        else:
            seg_specs.append(pl.BlockSpec((None, tm, W), lambda i, s, idx=idx: (idx, i, 0)))
        seg_args.append(arr)
    return pl.pallas_call(
        body, name=name, grid=(n_i, n_seg),
        in_specs=[row, row, pl.BlockSpec((1, D), lambda i, s: (0, 0))] + seg_specs + [
            pl.BlockSpec((tm, LANES), lambda i, s: (i, 0)),
            pl.BlockSpec((None, D, W), lambda i, s: (s, 0, 0)),
            pl.BlockSpec((D, LANES), lambda i, s: (0, 0)), ANY],
        out_specs=[row, pl.BlockSpec((None, 1, D), lambda i, s: (i, 0, 0))],
        out_shape=[jax.ShapeDtypeStruct((T, D), F32), jax.ShapeDtypeStruct((n_i, 1, D), F32)],
        scratch_shapes=[pltpu.VMEM((tm, D), F32)],
        compiler_params=_params(("parallel", "arbitrary")),
    )(dx2, x1, g, *seg_args, dfb, w7, wf, _after(dep))


def _forget_cumsum(f, b_pad, name):
    T, L = f.shape
    tb = _blk(T, 256)

    def body(f_ref, b_ref, c_ref, carry):
        @pl.when(pl.program_id(0) == 0)
        def _():
            carry[...] = jnp.zeros_like(carry)

        lf = jax.nn.log_sigmoid(f_ref[...] + b_ref[...])
        rows = lax.broadcasted_iota(jnp.int32, (tb, tb), 0)
        cols = lax.broadcasted_iota(jnp.int32, (tb, tb), 1)
        tri = (cols <= rows).astype(F32)
        c = jnp.dot(tri, lf, preferred_element_type=F32, precision=lax.Precision.HIGHEST) + carry[...]
        c_ref[...] = c
        carry[...] = c[tb - 1:tb, :]

    return pl.pallas_call(
        body, name=name, grid=(T // tb,),
        in_specs=[pl.BlockSpec((tb, L), lambda i: (i, 0)), pl.BlockSpec((1, L), lambda i: (0, 0))],
        out_specs=pl.BlockSpec((tb, L), lambda i: (i, 0)),
        out_shape=jax.ShapeDtypeStruct((T, L), F32),
        scratch_shapes=[pltpu.VMEM((1, L), F32)],
        compiler_params=_params(("arbitrary",)),
    )(f, b_pad)


def _forget_bwd(dc, f, b_pad, name):
    T, L = f.shape
    tb = _blk(T, 256)
    nb = T // tb

    def body(dc_ref, f_ref, b_ref, df_ref, db_ref, carry):
        @pl.when(pl.program_id(0) == 0)
        def _():
            carry[...] = jnp.zeros_like(carry)
            db_ref[...] = jnp.zeros_like(db_ref)

        rows = lax.broadcasted_iota(jnp.int32, (tb, tb), 0)
        cols = lax.broadcasted_iota(jnp.int32, (tb, tb), 1)
        tri = (cols >= rows).astype(F32)
        r = jnp.dot(tri, dc_ref[...], preferred_element_type=F32, precision=lax.Precision.HIGHEST) + carry[...]
        carry[...] = r[0:1, :]
        df = r * (1.0 - jax.nn.sigmoid(f_ref[...] + b_ref[...]))
        df_ref[...] = df.astype(BF16)
        db_ref[...] += jnp.sum(df, axis=0, keepdims=True)

    rev = pl.BlockSpec((tb, L), lambda i: (nb - 1 - i, 0))
    one = pl.BlockSpec((1, L), lambda i: (0, 0))
    return pl.pallas_call(
        body, name=name, grid=(nb,),
        in_specs=[rev, rev, one], out_specs=[rev, one],
        out_shape=[jax.ShapeDtypeStruct((T, L), BF16), jax.ShapeDtypeStruct((1, L), F32)],
        scratch_shapes=[pltpu.VMEM((1, L), F32)],
        compiler_params=_params(("arbitrary",)),
    )(dc, f, b_pad)


def _attn_fwd(z7, c_row, name):
    _, T, W = z7.shape
    H = W // HEAD_DIM
    ta = _blk(T, 512)
    nq = T // ta
    scale = np.float32(1.0 / np.sqrt(HEAD_DIM))

    def body(q_ref, k_ref, v_ref, crow_ref, o_ref, lse_ref, m_scr, l_scr, acc_scr):
        i = pl.program_id(1)
        j = pl.program_id(2)

        @pl.when(j == 0)
        def _():
            m_scr[...] = jnp.full_like(m_scr, NEG_BIG)
            l_scr[...] = jnp.zeros_like(l_scr)
            acc_scr[...] = jnp.zeros_like(acc_scr)

        def step(diagonal):
            s = _mm_nt(q_ref[...], k_ref[...]) * scale - crow_ref[...]
            if diagonal:
                rows = lax.broadcasted_iota(jnp.int32, (ta, ta), 0)
                cols = lax.broadcasted_iota(jnp.int32, (ta, ta), 1)
                s = jnp.where(cols <= rows, s, NEG_BIG)
            m_prev = m_scr[...]
            m_new = jnp.maximum(m_prev, jnp.max(s, axis=-1, keepdims=True))
            alpha = jnp.exp(m_prev - m_new)
            p = jnp.exp(s - m_new)
            l_scr[...] = alpha * l_scr[...] + jnp.sum(p, axis=-1, keepdims=True)
            acc_scr[...] = alpha * acc_scr[...] + _mm(p.astype(BF16), v_ref[...])
            m_scr[...] = m_new

        @pl.when(j < i)
        def _():
            step(False)

        @pl.when(j == i)
        def _():
            step(True)
            l = l_scr[...]
            o_ref[...] = acc_scr[...] / l
            lse_ref[...] = m_scr[...] + jnp.log(l)

    return pl.pallas_call(
        body, name=name, grid=(H, nq, nq),
        in_specs=[pl.BlockSpec((None, ta, HEAD_DIM), lambda h, i, j: (0, i, h)),
                  pl.BlockSpec((None, ta, HEAD_DIM), lambda h, i, j: (1, jnp.minimum(i, j), h)),
                  pl.BlockSpec((None, ta, HEAD_DIM), lambda h, i, j: (2, jnp.minimum(i, j), h)),
                  pl.BlockSpec((None, 1, ta), lambda h, i, j: (h, 0, jnp.minimum(i, j)))],
        out_specs=[pl.BlockSpec((ta, HEAD_DIM), lambda h, i, j: (i, h)),
                   pl.BlockSpec((None, ta, 1), lambda h, i, j: (h, i, 0))],
        out_shape=[jax.ShapeDtypeStruct((T, W), F32), jax.ShapeDtypeStruct((H, T, 1), F32)],
        scratch_shapes=[pltpu.VMEM((ta, 1), F32), pltpu.VMEM((ta, 1), F32), pltpu.VMEM((ta, HEAD_DIM), F32)],
        compiler_params=_params(("parallel", "parallel", "arbitrary")),
    )(z7, z7, z7, c_row)


def _attn_bwd_kv(z7, dob, c_col, lse_row, d_row, name):
    _, T, W = z7.shape
    H = W // HEAD_DIM
    ta = _blk(T, 512)
    nq = T // ta
    scale = np.float32(1.0 / np.sqrt(HEAD_DIM))

    def body(k_ref, v_ref, q_ref, do_ref, ccol_ref, lse_ref, d_ref, dk_ref, dv_ref, dc_ref, dk_scr, dv_scr, dc_scr):
        j = pl.program_id(1)
        i = pl.program_id(2)

        @pl.when(i == 0)
        def _():
            dk_scr[...] = jnp.zeros_like(dk_scr)
            dv_scr[...] = jnp.zeros_like(dv_scr)
            dc_scr[...] = jnp.zeros_like(dc_scr)

        def step(diagonal):
            q = q_ref[...]
            do = do_ref[...]
            st = _mm_nt(k_ref[...], q) * scale - ccol_ref[...] - lse_ref[...]
            if diagonal:
                rows = lax.broadcasted_iota(jnp.int32, (ta, ta), 0)
                cols = lax.broadcasted_iota(jnp.int32, (ta, ta), 1)
                st = jnp.where(rows <= cols, st, NEG_BIG)
            pt = jnp.exp(st)
            dv_scr[...] += _mm(pt.astype(BF16), do)
            dst = pt * (_mm_nt(v_ref[...], do) - d_ref[...])
            dk_scr[...] += _mm(dst.astype(BF16), q)
            dc_scr[...] += jnp.sum(dst, axis=-1, keepdims=True)

        @pl.when(i > j)
        def _():
            step(False)

        @pl.when(i == j)
        def _():
            step(True)

        @pl.when(i == nq - 1)
        def _():
            dk_ref[...] = (dk_scr[...] * scale).astype(BF16)
            dv_ref[...] = dv_scr[...].astype(BF16)
            dc_ref[...] = -dc_scr[...]

    return pl.pallas_call(
        body, name=name, grid=(H, nq, nq),
        in_specs=[pl.BlockSpec((None, ta, HEAD_DIM), lambda h, j, i: (1, j, h)),
                  pl.BlockSpec((None, ta, HEAD_DIM), lambda h, j, i: (2, j, h)),
                  pl.BlockSpec((None, ta, HEAD_DIM), lambda h, j, i: (0, jnp.maximum(i, j), h)),
                  pl.BlockSpec((ta, HEAD_DIM), lambda h, j, i: (jnp.maximum(i, j), h)),
                  pl.BlockSpec((None, ta, 1), lambda h, j, i: (h, j, 0)),
                  pl.BlockSpec((None, 1, ta), lambda h, j, i: (h, 0, jnp.maximum(i, j))),
                  pl.BlockSpec((None, 1, ta), lambda h, j, i: (h, 0, jnp.maximum(i, j)))],
        out_specs=[pl.BlockSpec((ta, HEAD_DIM), lambda h, j, i: (j, h)),
                   pl.BlockSpec((ta, HEAD_DIM), lambda h, j, i: (j, h)),
                   pl.BlockSpec((None, ta, 1), lambda h, j, i: (h, j, 0))],
        out_shape=[jax.ShapeDtypeStruct((T, W), BF16), jax.ShapeDtypeStruct((T, W), BF16),
                   jax.ShapeDtypeStruct((H, T, 1), F32)],
        scratch_shapes=[pltpu.VMEM((ta, HEAD_DIM), F32), pltpu.VMEM((ta, HEAD_DIM), F32), pltpu.VMEM((ta, 1), F32)],
        compiler_params=_params(("parallel", "parallel", "arbitrary")),
    )(z7, z7, z7, dob, c_col, lse_row, d_row)


def _attn_bwd_q(z7, dob, c_row, lse_col, d_col, name):
    _, T, W = z7.shape
    H = W // HEAD_DIM
    ta = _blk(T, 512)
    nq = T // ta
    scale = np.float32(1.0 / np.sqrt(HEAD_DIM))

    def body(q_ref, k_ref, v_ref, do_ref, crow_ref, lse_ref, d_ref, dq_ref, dc_ref, dq_scr, dc_scr):
        i = pl.program_id(1)
        j = pl.program_id(2)

        @pl.when(j == 0)
        def _():
            dq_scr[...] = jnp.zeros_like(dq_scr)
            dc_scr[...] = jnp.zeros_like(dc_scr)

        def step(diagonal):
            k = k_ref[...]
            do = do_ref[...]
            s = _mm_nt(q_ref[...], k) * scale - crow_ref[...] - lse_ref[...]
            if diagonal:
                rows = lax.broadcasted_iota(jnp.int32, (ta, ta), 0)
                cols = lax.broadcasted_iota(jnp.int32, (ta, ta), 1)
                s = jnp.where(cols <= rows, s, NEG_BIG)
            p = jnp.exp(s)
            ds = p * (_mm_nt(do, v_ref[...]) - d_ref[...])
            dq_scr[...] += _mm(ds.astype(BF16), k)
            dc_scr[...] += jnp.sum(ds, axis=-1, keepdims=True)

        @pl.when(j < i)
        def _():
            step(False)

        @pl.when(j == i)
        def _():
            step(True)
            dq_ref[...] = (dq_scr[...] * scale).astype(BF16)
            dc_ref[...] = dc_scr[...]

    return pl.pallas_call(
        body, name=name, grid=(H, nq, nq),
        in_specs=[pl.BlockSpec((None, ta, HEAD_DIM), lambda h, i, j: (0, i, h)),
                  pl.BlockSpec((None, ta, HEAD_DIM), lambda h, i, j: (1, jnp.minimum(i, j), h)),
                  pl.BlockSpec((None, ta, HEAD_DIM), lambda h, i, j: (2, jnp.minimum(i, j), h)),
                  pl.BlockSpec((ta, HEAD_DIM), lambda h, i, j: (i, h)),
                  pl.BlockSpec((None, 1, ta), lambda h, i, j: (h, 0, jnp.minimum(i, j))),
                  pl.BlockSpec((None, ta, 1), lambda h, i, j: (h, i, 0)),
                  pl.BlockSpec((None, ta, 1), lambda h, i, j: (h, i, 0))],
        out_specs=[pl.BlockSpec((ta, HEAD_DIM), lambda h, i, j: (i, h)),
                   pl.BlockSpec((None, ta, 1), lambda h, i, j: (h, i, 0))],
        out_shape=[jax.ShapeDtypeStruct((T, W), BF16), jax.ShapeDtypeStruct((H, T, 1), F32)],
        scratch_shapes=[pltpu.VMEM((ta, HEAD_DIM), F32), pltpu.VMEM((ta, 1), F32)],
        compiler_params=_params(("parallel", "parallel", "arbitrary")),
    )(z7, z7, z7, dob, c_row, lse_col, d_col)


ATTN_TILE = 512
ATTN_CHAINS = 4


def _attn_geometry(T):
    ta = _blk(T, ATTN_TILE)
    nc = ATTN_CHAINS if (T // ta) % ATTN_CHAINS == 0 else 1
    return ta, nc, T // ta


def _causal_tile(ta, keys_on_rows=False):
    rows = lax.broadcasted_iota(jnp.int32, (ta, ta), 0)
    cols = lax.broadcasted_iota(jnp.int32, (ta, ta), 1)
    return rows <= cols if keys_on_rows else cols <= rows


def _chunk(ref, j, ta):
    return ref[pl.ds(pl.multiple_of(j * ta, ta), ta), :]


def _attn_fwd_loop(z7, c_chunks, name):
    _, T, W = z7.shape
    H = W // HEAD_DIM
    ta, nc, n_chunks = _attn_geometry(T)
    scale = np.float32(1.0 / np.sqrt(HEAD_DIM))

    def body(q_ref, k_ref, v_ref, c_ref, o_ref, lse_ref, m_scr, l_scr, acc_scr):
        g = pl.program_id(1)
        m_scr[...] = jnp.full_like(m_scr, NEG_BIG)
        l_scr[...] = jnp.zeros_like(l_scr)
        acc_scr[...] = jnp.zeros_like(acc_scr)

        def update(ch, k, v, crow, diagonal):
            q = q_ref[ch * ta:(ch + 1) * ta, :]
            s = _mm_nt(q, k) * scale - crow
            if diagonal:
                s = jnp.where(_causal_tile(ta), s, NEG_BIG)
            m_prev = m_scr[ch]
            m_new = jnp.maximum(m_prev, jnp.max(s, axis=-1, keepdims=True))
            alpha = jnp.exp(m_prev - m_new)
            p = jnp.exp(s - m_new)
            l_scr[ch] = alpha * l_scr[ch] + jnp.sum(p, axis=-1, keepdims=True)
            acc_scr[ch] = alpha * acc_scr[ch] + _mm(p.astype(BF16), v)
            m_scr[ch] = m_new

        def full_chunk(j, carry):
            k = _chunk(k_ref, j, ta)
            v = _chunk(v_ref, j, ta)
            crow = c_ref[j]
            for ch in range(nc):
                update(ch, k, v, crow, False)
            return carry

        lax.fori_loop(0, nc * g, full_chunk, 0)
        for jj in range(nc):
            j = nc * g + jj
            k = _chunk(k_ref, j, ta)
            v = _chunk(v_ref, j, ta)
            crow = c_ref[j]
            for ch in range(jj, nc):
                update(ch, k, v, crow, ch == jj)
        for ch in range(nc):
            l = l_scr[ch]
            o_ref[ch * ta:(ch + 1) * ta, :] = acc_scr[ch] / l
            lse_ref[ch * ta:(ch + 1) * ta, :] = m_scr[ch] + jnp.log(l)

    tq = nc * ta
    return pl.pallas_call(
        body, name=name, grid=(H, n_chunks // nc),
        in_specs=[pl.BlockSpec((None, tq, HEAD_DIM), lambda h, g: (0, g, h)),
                  pl.BlockSpec((None, T, HEAD_DIM), lambda h, g: (1, 0, h)),
                  pl.BlockSpec((None, T, HEAD_DIM), lambda h, g: (2, 0, h)),
                  pl.BlockSpec((None, n_chunks, 1, ta), lambda h, g: (h, 0, 0, 0))],
        out_specs=[pl.BlockSpec((tq, HEAD_DIM), lambda h, g: (g, h)),
                   pl.BlockSpec((None, tq, 1), lambda h, g: (h, g, 0))],
        out_shape=[jax.ShapeDtypeStruct((T, W), F32), jax.ShapeDtypeStruct((H, T, 1), F32)],
        scratch_shapes=[pltpu.VMEM((nc, ta, 1), F32), pltpu.VMEM((nc, ta, 1), F32),
                        pltpu.VMEM((nc, ta, HEAD_DIM), F32)],
        compiler_params=_params(("parallel", "arbitrary")),
    )(z7, z7, z7, c_chunks)


def _attn_fwd_keys_on_rows(z7, vt, c_rep, name):
    _, T, W = z7.shape
    H = W // HEAD_DIM
    ta, nc, n_chunks = _attn_geometry(T)
    scale = np.float32(1.0 / np.sqrt(HEAD_DIM))
    reps = ta // LANES

    def body(q_ref, k_ref, vt_ref, c_ref, o_ref, lse_ref):
        g = pl.program_id(1)

        def scores(ch, k):
            return _mm_nt(k, q_ref[ch * ta:(ch + 1) * ta, :])

        def update(state, raw, vt, cj, diagonal):
            m_prev, l_prev, acc_prev = state
            st = raw * (scale * LOG2E) - cj
            if diagonal:
                st = jnp.where(_causal_tile(ta, keys_on_rows=True), st, NEG_BIG)
            m_new = jnp.maximum(m_prev, jnp.max(st, axis=0, keepdims=True))
            alpha = jnp.exp2(m_prev - m_new)
            pt = jnp.exp2(st - m_new)
            l_new = alpha * l_prev + jnp.sum(pt, axis=0, keepdims=True)
            acc_new = alpha * acc_prev + _mm(vt, pt.astype(BF16))
            return m_new, l_new, acc_new

        def load(j):
            cj = _chunk(c_ref, j, ta)
            return _chunk(k_ref, j, ta), vt_ref[j], jnp.concatenate([cj] * reps, axis=1)

        def full_chunk(j, states):
            k, vt, cj = load(j)
            raws = [scores(ch, k) for ch in range(nc)]
            return tuple(update(states[ch], raws[ch], vt, cj, False) for ch in range(nc))

        first = (jnp.full((1, ta), NEG_BIG, F32), jnp.zeros((1, ta), F32), jnp.zeros((HEAD_DIM, ta), F32))
        states = list(lax.fori_loop(0, nc * g, full_chunk, (first,) * nc))
        for jj in range(nc):
            k, vt, cj = load(nc * g + jj)
            raws = {ch: scores(ch, k) for ch in range(jj, nc)}
            for ch in range(jj, nc):
                states[ch] = update(states[ch], raws[ch], vt, cj, ch == jj)
        for ch in range(nc):
            m, l, acc = states[ch]
            o_ref[ch * ta:(ch + 1) * ta, :] = (acc / l).T
            lse_ref[ch] = m + jnp.log2(l)

    tq = nc * ta
    return pl.pallas_call(
        body, name=name, grid=(H, n_chunks // nc),
        in_specs=[pl.BlockSpec((None, tq, HEAD_DIM), lambda h, g: (0, g, h)),
                  pl.BlockSpec((None, T, HEAD_DIM), lambda h, g: (1, 0, h)),
                  pl.BlockSpec((None, n_chunks, HEAD_DIM, ta), lambda h, g: (h, 0, 0, 0)),
                  pl.BlockSpec((None, T, LANES), lambda h, g: (h, 0, 0))],
        out_specs=[pl.BlockSpec((tq, HEAD_DIM), lambda h, g: (g, h)),
                   pl.BlockSpec((None, nc, 1, ta), lambda h, g: (h, g, 0, 0))],
        out_shape=[jax.ShapeDtypeStruct((T, W), F32), jax.ShapeDtypeStruct((H, n_chunks, 1, ta), F32)],
        compiler_params=_params(("parallel", "arbitrary")),
    )(z7, z7, vt, c_rep)


def _attn_bwd_fused(z7, kt, dob, c_rep, lse_chunks, d_chunks, name):
    _, T, W = z7.shape
    H = W // HEAD_DIM
    ta, nc, n_chunks = _attn_geometry(T)
    n_steps = n_chunks // nc
    scale = np.float32(1.0 / np.sqrt(HEAD_DIM))
    reps = ta // LANES

    def body(k_ref, v_ref, kt_ref, q_ref, do_ref, c_ref, lse_ref, d_ref,
             dk_ref, dv_ref, dck_ref, dq_ref, dcq_ref, dk_scr, dv_scr, dck_scr, dqt_scr, dcq_scr):
        g = pl.program_id(1)

        @pl.when(g == 0)
        def _():
            dqt_scr[...] = jnp.zeros_like(dqt_scr)
            dcq_scr[...] = jnp.zeros_like(dcq_scr)

        dk_scr[...] = jnp.zeros_like(dk_scr)
        dv_scr[...] = jnp.zeros_like(dv_scr)
        dck_scr[...] = jnp.zeros_like(dck_scr)

        def products(ch, q, do):
            rows = slice(ch * ta, (ch + 1) * ta)
            return _mm_nt(k_ref[rows, :], q), _mm_nt(v_ref[rows, :], do)

        def update(ch, i, q, do, prods, diagonal):
            rows = slice(ch * ta, (ch + 1) * ta)
            cj = c_ref[rows, :]
            st = prods[0] * (scale * LOG2E) - jnp.concatenate([cj] * reps, axis=1) - lse_ref[i]
            if diagonal:
                st = jnp.where(_causal_tile(ta, keys_on_rows=True), st, NEG_BIG)
            pt = jnp.exp2(st)
            dv_scr[ch] += _mm(pt.astype(BF16), do)
            dst = pt * (prods[1] - d_ref[i])
            dst_b = dst.astype(BF16)
            dk_scr[ch] += _mm(dst_b, q)
            dqt_scr[i] += _mm(kt_ref[ch], dst_b)
            dcq_scr[i] += jnp.sum(dst, axis=0, keepdims=True)
            lane_sum = dst[:, :LANES]
            for r in range(1, reps):
                lane_sum = lane_sum + dst[:, r * LANES:(r + 1) * LANES]
            dck_scr[ch] += lane_sum

        for ii in range(nc):
            i = nc * g + ii
            q = _chunk(q_ref, i, ta)
            do = _chunk(do_ref, i, ta)
            prods = [products(ch, q, do) for ch in range(0, ii + 1)]
            for ch in range(0, ii + 1):
                update(ch, i, q, do, prods[ch], ch == ii)

        def full_chunk(i, carry):
            q = _chunk(q_ref, i, ta)
            do = _chunk(do_ref, i, ta)
            prods = [products(ch, q, do) for ch in range(nc)]
            for ch in range(nc):
                update(ch, i, q, do, prods[ch], False)
            return carry

        lax.fori_loop(nc * (g + 1), n_chunks, full_chunk, 0)
        for ch in range(nc):
            rows = slice(ch * ta, (ch + 1) * ta)
            dk_ref[rows, :] = (dk_scr[ch] * scale).astype(BF16)
            dv_ref[rows, :] = dv_scr[ch].astype(BF16)
            ones = jnp.ones((8, LANES), F32)
            sums = lax.dot_general(ones, dck_scr[ch], (((1,), (1,)), ((), ())), preferred_element_type=F32,
                                   precision=lax.Precision.HIGHEST)
            dck_ref[ch] = -sums[0:1, :]

        @pl.when(g == n_steps - 1)
        def _():
            for i in range(n_chunks):
                dq_ref[i * ta:(i + 1) * ta, :] = (dqt_scr[i] * scale).T.astype(BF16)
            dcq_ref[...] = dcq_scr[...]

    tk = nc * ta
    chunks = pl.BlockSpec((None, n_chunks, 1, ta), lambda h, g: (h, 0, 0, 0))
    tile = pl.BlockSpec((tk, HEAD_DIM), lambda h, g: (g, h))
    return pl.pallas_call(
        body, name=name, grid=(H, n_steps),
        in_specs=[pl.BlockSpec((None, tk, HEAD_DIM), lambda h, g: (1, g, h)),
                  pl.BlockSpec((None, tk, HEAD_DIM), lambda h, g: (2, g, h)),
                  pl.BlockSpec((None, nc, HEAD_DIM, ta), lambda h, g: (h, g, 0, 0)),
                  pl.BlockSpec((None, T, HEAD_DIM), lambda h, g: (0, 0, h)),
                  pl.BlockSpec((T, HEAD_DIM), lambda h, g: (0, h)),
                  pl.BlockSpec((None, tk, LANES), lambda h, g: (h, g, 0)),
                  chunks, chunks],
        out_specs=[tile, tile, pl.BlockSpec((None, nc, 1, ta), lambda h, g: (h, g, 0, 0)),
                   pl.BlockSpec((T, HEAD_DIM), lambda h, g: (0, h)), chunks],
        out_shape=[jax.ShapeDtypeStruct((T, W), BF16), jax.ShapeDtypeStruct((T, W), BF16),
                   jax.ShapeDtypeStruct((H, n_chunks, 1, ta), F32), jax.ShapeDtypeStruct((T, W), BF16),
                   jax.ShapeDtypeStruct((H, n_chunks, 1, ta), F32)],
        scratch_shapes=[pltpu.VMEM((nc, ta, HEAD_DIM), F32), pltpu.VMEM((nc, ta, HEAD_DIM), F32),
                        pltpu.VMEM((nc, ta, LANES), F32), pltpu.VMEM((n_chunks, HEAD_DIM, ta), F32),
                        pltpu.VMEM((n_chunks, 1, ta), F32)],
        compiler_params=_params(("parallel", "arbitrary")),
    )(z7, z7, kt, z7, dob, c_rep, lse_chunks, d_chunks)


def _attn_bwd_q_loop(z7, dob, c_chunks, lse_col, d_col, name):
    _, T, W = z7.shape
    H = W // HEAD_DIM
    ta, nc, n_chunks = _attn_geometry(T)
    scale = np.float32(1.0 / np.sqrt(HEAD_DIM))

    def body(q_ref, k_ref, v_ref, do_ref, c_ref, lse_ref, d_ref, dq_ref, dc_ref, dq_scr, dc_scr):
        g = pl.program_id(1)
        dq_scr[...] = jnp.zeros_like(dq_scr)
        dc_scr[...] = jnp.zeros_like(dc_scr)

        def update(ch, k, v, crow, diagonal):
            rows = slice(ch * ta, (ch + 1) * ta)
            do = do_ref[rows, :]
            s = _mm_nt(q_ref[rows, :], k) * scale - crow - lse_ref[rows, :]
            if diagonal:
                s = jnp.where(_causal_tile(ta), s, NEG_BIG)
            p = jnp.exp(s)
            ds = p * (_mm_nt(do, v) - d_ref[rows, :])
            dq_scr[ch] += _mm(ds.astype(BF16), k)
            dc_scr[ch] += jnp.sum(ds, axis=-1, keepdims=True)

        def full_chunk(j, carry):
            k = _chunk(k_ref, j, ta)
            v = _chunk(v_ref, j, ta)
            crow = c_ref[j]
            for ch in range(nc):
                update(ch, k, v, crow, False)
            return carry

        lax.fori_loop(0, nc * g, full_chunk, 0)
        for jj in range(nc):
            j = nc * g + jj
            k = _chunk(k_ref, j, ta)
            v = _chunk(v_ref, j, ta)
            crow = c_ref[j]
            for ch in range(jj, nc):
                update(ch, k, v, crow, ch == jj)
        for ch in range(nc):
            dq_ref[ch * ta:(ch + 1) * ta, :] = (dq_scr[ch] * scale).astype(BF16)
            dc_ref[ch * ta:(ch + 1) * ta, :] = dc_scr[ch]

    tq = nc * ta
    col = pl.BlockSpec((None, tq, 1), lambda h, g: (h, g, 0))
    return pl.pallas_call(
        body, name=name, grid=(H, n_chunks // nc),
        in_specs=[pl.BlockSpec((None, tq, HEAD_DIM), lambda h, g: (0, g, h)),
                  pl.BlockSpec((None, T, HEAD_DIM), lambda h, g: (1, 0, h)),
                  pl.BlockSpec((None, T, HEAD_DIM), lambda h, g: (2, 0, h)),
                  pl.BlockSpec((tq, HEAD_DIM), lambda h, g: (g, h)),
                  pl.BlockSpec((None, n_chunks, 1, ta), lambda h, g: (h, 0, 0, 0)),
                  col, col],
        out_specs=[pl.BlockSpec((tq, HEAD_DIM), lambda h, g: (g, h)), col],
        out_shape=[jax.ShapeDtypeStruct((T, W), BF16), jax.ShapeDtypeStruct((H, T, 1), F32)],
        scratch_shapes=[pltpu.VMEM((nc, ta, HEAD_DIM), F32), pltpu.VMEM((nc, ta, 1), F32)],
        compiler_params=_params(("parallel", "arbitrary")),
    )(z7, z7, z7, dob, c_chunks, lse_col, d_col)


def _attn_bwd_kv_loop(z7, dob, c_col, lse_chunks, d_chunks, name):
    _, T, W = z7.shape
    H = W // HEAD_DIM
    ta, nc, n_chunks = _attn_geometry(T)
    scale = np.float32(1.0 / np.sqrt(HEAD_DIM))

    def body(k_ref, v_ref, q_ref, do_ref, ccol_ref, lse_ref, d_ref, dk_ref, dv_ref, dc_ref, dk_scr, dv_scr, dc_scr):
        g = pl.program_id(1)
        dk_scr[...] = jnp.zeros_like(dk_scr)
        dv_scr[...] = jnp.zeros_like(dv_scr)
        dc_scr[...] = jnp.zeros_like(dc_scr)

        def update(ch, q, do, lse_row, d_row, diagonal):
            rows = slice(ch * ta, (ch + 1) * ta)
            st = _mm_nt(k_ref[rows, :], q) * scale - ccol_ref[rows, :] - lse_row
            if diagonal:
                st = jnp.where(_causal_tile(ta, keys_on_rows=True), st, NEG_BIG)
            pt = jnp.exp(st)
            dv_scr[ch] += _mm(pt.astype(BF16), do)
            dst = pt * (_mm_nt(v_ref[rows, :], do) - d_row)
            dk_scr[ch] += _mm(dst.astype(BF16), q)
            dc_scr[ch] += jnp.sum(dst, axis=-1, keepdims=True)

        for ii in range(nc):
            i = nc * g + ii
            q = _chunk(q_ref, i, ta)
            do = _chunk(do_ref, i, ta)
            for ch in range(0, ii + 1):
                update(ch, q, do, lse_ref[i], d_ref[i], ch == ii)

        def full_chunk(i, carry):
            q = _chunk(q_ref, i, ta)
            do = _chunk(do_ref, i, ta)
            for ch in range(nc):
                update(ch, q, do, lse_ref[i], d_ref[i], False)
            return carry

        lax.fori_loop(nc * (g + 1), n_chunks, full_chunk, 0)
        for ch in range(nc):
            rows = slice(ch * ta, (ch + 1) * ta)
            dk_ref[rows, :] = (dk_scr[ch] * scale).astype(BF16)
            dv_ref[rows, :] = dv_scr[ch].astype(BF16)
            dc_ref[rows, :] = -dc_scr[ch]

    tk = nc * ta
    chunks = pl.BlockSpec((None, n_chunks, 1, ta), lambda h, g: (h, 0, 0, 0))
    col = pl.BlockSpec((None, tk, 1), lambda h, g: (h, g, 0))
    tile = pl.BlockSpec((tk, HEAD_DIM), lambda h, g: (g, h))
    return pl.pallas_call(
        body, name=name, grid=(H, n_chunks // nc),
        in_specs=[pl.BlockSpec((None, tk, HEAD_DIM), lambda h, g: (1, g, h)),
                  pl.BlockSpec((None, tk, HEAD_DIM), lambda h, g: (2, g, h)),
                  pl.BlockSpec((None, T, HEAD_DIM), lambda h, g: (0, 0, h)),
                  pl.BlockSpec((T, HEAD_DIM), lambda h, g: (0, h)),
                  col, chunks, chunks],
        out_specs=[tile, tile, col],
        out_shape=[jax.ShapeDtypeStruct((T, W), BF16), jax.ShapeDtypeStruct((T, W), BF16),
                   jax.ShapeDtypeStruct((H, T, 1), F32)],
        scratch_shapes=[pltpu.VMEM((nc, ta, HEAD_DIM), F32), pltpu.VMEM((nc, ta, HEAD_DIM), F32),
                        pltpu.VMEM((nc, ta, 1), F32)],
        compiler_params=_params(("parallel", "arbitrary")),
    )(z7, z7, z7, dob, c_col, lse_chunks, d_chunks)


def _chunk_causal_mask():
    rows = lax.broadcasted_iota(jnp.int32, (SGU_LEN, SGU_LEN), 0)
    cols = lax.broadcasted_iota(jnp.int32, (SGU_LEN, SGU_LEN), 1)
    return (cols // CHUNK) <= (rows // CHUNK)


def _sgu_norm_mix(sv, lng_ref, lnb_ref, ws_ref, bs_ref, vn_scr, mixed_scr, vhat_scr=None):
    tm = sv.shape[0]
    vs = _gelu(sv)
    mask = _chunk_causal_mask()
    rstds = []
    for g in range(N_GROUPS):
        lanes = slice(g * GROUP_DIM, (g + 1) * GROUP_DIM)
        blk = vs[:, lanes]
        cen = blk - jnp.mean(blk, axis=-1, keepdims=True)
        rstd = lax.rsqrt(jnp.mean(cen * cen, axis=-1, keepdims=True) + LN_EPS)
        vhat = cen * rstd
        rstds.append(rstd)
        if vhat_scr is not None:
            vhat_scr[:, lanes] = vhat
        vn_scr[:, lanes] = (vhat * lng_ref[:, lanes] + lnb_ref[:, lanes]).astype(BF16)
        wm = jnp.where(mask, ws_ref[g], 0.0).astype(BF16)
        for w in range(tm // SGU_LEN):
            rows = slice(w * SGU_LEN, (w + 1) * SGU_LEN)
            mixed_scr[rows, lanes] = _mm(wm, vn_scr[rows, lanes]) + bs_ref[g]
    return rstds


def _mix_out_fwd(z7, o_a, x1, lng, lnb, ws, bs, w_out, g_post, name):
    _, T, W = z7.shape
    D = x1.shape[1]
    tm = _blk(T, 256)

    def body(u_ref, sv_ref, ga_ref, gb_ref, oa_ref, x1_ref, lng_ref, lnb_ref, ws_ref, bs_ref, wo_ref, gp_ref,
             x2_ref, p_ref, mb_ref, vn_scr, mixed_scr):
        _sgu_norm_mix(sv_ref[...].astype(F32), lng_ref, lnb_ref, ws_ref, bs_ref, vn_scr, mixed_scr)
        o_b = _gelu(u_ref[...].astype(F32)) * mixed_scr[...]
        merged = (jax.nn.sigmoid(ga_ref[...].astype(F32)) * oa_ref[...]
                  + jax.nn.sigmoid(gb_ref[...].astype(F32)) * o_b).astype(BF16)
        mb_ref[...] = merged
        p = _mm(merged, wo_ref[...])
        p_ref[...] = p
        x2_ref[...] = x1_ref[...] + p * _rms_scale(p) * gp_ref[...]

    def seg(idx):
        return pl.BlockSpec((None, tm, W), lambda i, idx=idx: (idx, i, 0))

    row = pl.BlockSpec((tm, D), lambda i: (i, 0))
    vec = pl.BlockSpec((1, D), lambda i: (0, 0))
    return pl.pallas_call(
        body, name=name, grid=(T // tm,),
        in_specs=[seg(3), seg(4), seg(5), seg(6), row, row, vec, vec,
                  pl.BlockSpec((N_GROUPS, SGU_LEN, SGU_LEN), lambda i: (0, 0, 0)),
                  pl.BlockSpec((N_GROUPS, SGU_LEN, 1), lambda i: (0, 0, 0)),
                  pl.BlockSpec((D, D), lambda i: (0, 0)), vec],
        out_specs=[row, row, row],
        out_shape=[jax.ShapeDtypeStruct((T, D), F32), jax.ShapeDtypeStruct((T, D), F32),
                   jax.ShapeDtypeStruct((T, D), BF16)],
        scratch_shapes=[pltpu.VMEM((tm, W), BF16), pltpu.VMEM((tm, W), F32)],
        compiler_params=_params(("parallel",)),
    )(z7, z7, z7, z7, o_a, x1, lng, lnb, ws, bs, w_out, g_post)


def _mix_out_bwd(dx2, p, z7, o_a, lng, lnb, ws, bs, w_out, g_post, name, dep=None):
    _, T, W = z7.shape
    D = dx2.shape[1]
    tm = _blk(T, 256)
    n_w = tm // SGU_LEN

    def body(dx2_ref, p_ref, u_ref, sv_ref, ga_ref, gb_ref, oa_ref, lng_ref, lnb_ref, ws_ref, bs_ref, wo_ref, gp_ref, _,
             dpb_ref, dob_ref, dvec_ref, dz_ref, dgp_ref, dlng_ref, dlnb_ref, dws_ref, dbs_ref,
             vn_scr, mixed_scr, vhat_scr, dmix_scr, dvn_scr):
        @pl.when(pl.program_id(0) == 0)
        def _():
            dgp_ref[...] = jnp.zeros_like(dgp_ref)
            dlng_ref[...] = jnp.zeros_like(dlng_ref)
            dlnb_ref[...] = jnp.zeros_like(dlnb_ref)
            dws_ref[...] = jnp.zeros_like(dws_ref)
            dbs_ref[...] = jnp.zeros_like(dbs_ref)

        pv = p_ref[...]
        s = _rms_scale(pv)
        n = pv * s
        dn = dx2_ref[...]
        dgp_ref[...] += jnp.sum(dn * n, axis=0, keepdims=True)
        dpb = _rms_bwd(dn, n, s, gp_ref[...]).astype(BF16)
        dpb_ref[...] = dpb
        dmerged = _mm_nt(dpb, wo_ref[...])

        sv = sv_ref[...].astype(F32)
        rstds = _sgu_norm_mix(sv, lng_ref, lnb_ref, ws_ref, bs_ref, vn_scr, mixed_scr, vhat_scr)
        u_pre = u_ref[...].astype(F32)
        u = _gelu(u_pre)
        mixed = mixed_scr[...]
        sa = jax.nn.sigmoid(ga_ref[...].astype(F32))
        sb = jax.nn.sigmoid(gb_ref[...].astype(F32))
        oa = oa_ref[...]
        do_a = (dmerged * sa).astype(BF16)
        dob_ref[...] = do_a
        prod = do_a.astype(F32) * oa
        for h in range(N_HEADS):
            dvec_ref[h] = jnp.sum(prod[:, h * HEAD_DIM:(h + 1) * HEAD_DIM], axis=-1, keepdims=True)
        dz_ref[2] = (dmerged * oa * (sa * (1.0 - sa))).astype(BF16)
        dz_ref[3] = (dmerged * (u * mixed) * (sb * (1.0 - sb))).astype(BF16)
        do_b = dmerged * sb
        dz_ref[0] = (do_b * mixed * _gelu_grad(u_pre)).astype(BF16)
        dmix_scr[...] = do_b * u

        mask = _chunk_causal_mask()
        for g in range(N_GROUPS):
            lanes = slice(g * GROUP_DIM, (g + 1) * GROUP_DIM)
            wm = jnp.where(mask, ws_ref[g], 0.0).astype(BF16)
            dws = jnp.zeros((SGU_LEN, SGU_LEN), F32)
            dbs = jnp.zeros((SGU_LEN, 1), F32)
            for w in range(n_w):
                rows = slice(w * SGU_LEN, (w + 1) * SGU_LEN)
                dmix = dmix_scr[rows, lanes]
                dmix_b = dmix.astype(BF16)
                dvn_scr[rows, lanes] = _mm_tn(wm, dmix_b)
                dws = dws + _mm_nt(dmix_b, vn_scr[rows, lanes])
                dbs = dbs + jnp.sum(dmix, axis=-1, keepdims=True)
            dws_ref[g] += jnp.where(mask, dws, 0.0)
            dbs_ref[g] += dbs
            dvn = dvn_scr[:, lanes]
            vhat = vhat_scr[:, lanes]
            dlng_ref[:, lanes] += jnp.sum(dvn * vhat, axis=0, keepdims=True)
            dlnb_ref[:, lanes] += jnp.sum(dvn, axis=0, keepdims=True)
            dvh = dvn * lng_ref[:, lanes]
            dvs = rstds[g] * (dvh - jnp.mean(dvh, axis=-1, keepdims=True)
                              - vhat * jnp.mean(dvh * vhat, axis=-1, keepdims=True))
            dvn_scr[:, lanes] = dvs
        dz_ref[1] = (dvn_scr[...] * _gelu_grad(sv)).astype(BF16)

    def seg(idx):
        return pl.BlockSpec((None, tm, W), lambda i, idx=idx: (idx, i, 0))

    row = pl.BlockSpec((tm, D), lambda i: (i, 0))
    vec = pl.BlockSpec((1, D), lambda i: (0, 0))
    ws_spec = pl.BlockSpec((N_GROUPS, SGU_LEN, SGU_LEN), lambda i: (0, 0, 0))
    bs_spec = pl.BlockSpec((N_GROUPS, SGU_LEN, 1), lambda i: (0, 0, 0))
    return pl.pallas_call(
        body, name=name, grid=(T // tm,),
        in_specs=[row, row, seg(3), seg(4), seg(5), seg(6), row, vec, vec, ws_spec, bs_spec,
                  pl.BlockSpec((D, D), lambda i: (0, 0)), vec, ANY],
        out_specs=[row, row, pl.BlockSpec((N_HEADS, tm, 1), lambda i: (0, i, 0)),
                   pl.BlockSpec((4, tm, W), lambda i: (0, i, 0)), vec, vec, vec, ws_spec, bs_spec],
        out_shape=[jax.ShapeDtypeStruct((T, D), BF16), jax.ShapeDtypeStruct((T, W), BF16),
                   jax.ShapeDtypeStruct((N_HEADS, T, 1), F32), jax.ShapeDtypeStruct((4, T, W), BF16),
                   jax.ShapeDtypeStruct((1, D), F32), jax.ShapeDtypeStruct((1, D), F32),
                   jax.ShapeDtypeStruct((1, D), F32),
                   jax.ShapeDtypeStruct((N_GROUPS, SGU_LEN, SGU_LEN), F32),
                   jax.ShapeDtypeStruct((N_GROUPS, SGU_LEN, 1), F32)],
        scratch_shapes=[pltpu.VMEM((tm, W), BF16), pltpu.VMEM((tm, W), F32), pltpu.VMEM((tm, W), F32),
                        pltpu.VMEM((tm, W), F32), pltpu.VMEM((tm, W), F32)],
        compiler_params=_params(("arbitrary",)),
    )(dx2, p, z7, z7, z7, z7, o_a, lng, lnb, ws, bs, w_out, g_post, _after(dep))


def _loss_head(y, target, name):
    T, D = y.shape
    tm = _blk(T, 1024)
    n_i = T // tm

    def body(y_ref, t_ref, dy_ref, loss_ref, acc_scr):
        i = pl.program_id(0)

        @pl.when(i == 0)
        def _():
            acc_scr[...] = jnp.zeros_like(acc_scr)

        e = y_ref[...] - t_ref[...]
        dy_ref[...] = e * np.float32(1.0 / D)
        acc_scr[...] += jnp.sum(e * e, axis=0, keepdims=True)

        @pl.when(i == n_i - 1)
        def _():
            total = jnp.sum(acc_scr[...], axis=-1, keepdims=True) * np.float32(0.5 / D)
            loss_ref[...] = jnp.broadcast_to(total, loss_ref.shape)

    row = pl.BlockSpec((tm, D), lambda i: (i, 0))
    return pl.pallas_call(
        body, name=name, grid=(n_i,),
        in_specs=[row, row],
        out_specs=[row, pl.BlockSpec((1, LANES), lambda i: (0, 0))],
        out_shape=[jax.ShapeDtypeStruct((T, D), F32), jax.ShapeDtypeStruct((1, LANES), F32)],
        scratch_shapes=[pltpu.VMEM((1, D), F32)],
        compiler_params=_params(("arbitrary",)),
    )(y, target)


def _adamw_math(w, g, m, v):
    m_new = ADAM_B1 * m + (1.0 - ADAM_B1) * g
    v_new = ADAM_B2 * v + (1.0 - ADAM_B2) * (g * g)
    m_hat = m_new / np.float32(1.0 - ADAM_B1 ** ADAM_STEP)
    v_hat = v_new / np.float32(1.0 - ADAM_B2 ** ADAM_STEP)
    delta = -ADAM_LR * (m_hat / (jnp.sqrt(v_hat) + ADAM_EPS) + ADAM_WD * w)
    return delta, m_new, v_new


def _sum_adamw(parts, w, m, v, name, dep=None):
    n, R, C = parts.shape
    tr = _blk(R, 128)

    def body(p_ref, w_ref, m_ref, v_ref, _, g_ref, d_ref, mo_ref, vo_ref):
        g = p_ref[0].astype(F32)
        for s in range(1, n):
            g = g + p_ref[s].astype(F32)
        delta, m_new, v_new = _adamw_math(w_ref[...], g, m_ref[...], v_ref[...])
        g_ref[...] = g
        d_ref[...] = delta
        mo_ref[...] = m_new
        vo_ref[...] = v_new

    row = pl.BlockSpec((tr, C), lambda i: (i, 0))
    shp = jax.ShapeDtypeStruct((R, C), F32)
    return pl.pallas_call(
        body, name=name, grid=(R // tr,),
        in_specs=[pl.BlockSpec((n, tr, C), lambda i: (0, i, 0)), row, row, row, ANY],
        out_specs=[row, row, row, row], out_shape=[shp, shp, shp, shp],
        compiler_params=_params(("parallel",)),
    )(parts, w, m, v, _after(dep))


def _adamw(g, w, m, v, name):
    R, C = g.shape
    tr = _blk(R, 128)

    def body(g_ref, w_ref, m_ref, v_ref, d_ref, mo_ref, vo_ref):
        delta, m_new, v_new = _adamw_math(w_ref[...], g_ref[...], m_ref[...], v_ref[...])
        d_ref[...] = delta
        mo_ref[...] = m_new
        vo_ref[...] = v_new

    row = pl.BlockSpec((tr, C), lambda i: (i, 0))
    shp = jax.ShapeDtypeStruct((R, C), F32)
    return pl.pallas_call(
        body, name=name, grid=(R // tr,),
        in_specs=[row, row, row, row], out_specs=[row, row, row], out_shape=[shp, shp, shp],
        compiler_params=_params(("parallel",)),
    )(g, w, m, v)


def _position():
    return lax.axis_index("x"), lax.axis_index("y"), lax.axis_index("c")


def _slot(px, py, pc):
    return 4 * px + 2 * py + pc


def _all_gather(shards, name):
    n = len(shards)

    def body(*refs):
        ins, outs = refs[:n], refs[n:2 * n]
        send_sems, recv_sems, local_sems = refs[2 * n:]
        x, y, c = _position()
        me, sibling = (x, y, c), (x, y, 1 - c)
        chips = [(1 - x, y), (x, 1 - y), (1 - x, 1 - y)]

        def copy(a, k, block, to, src=None):
            dst = outs[a].at[_slot(*block)]
            return pltpu.make_async_remote_copy(
                src_ref=dst if src is None else src, dst_ref=dst,
                send_sem=send_sems.at[a, k], recv_sem=recv_sems.at[a, k],
                device_id=to, device_id_type=MESH)

        mine = [pltpu.make_async_copy(ins[a], outs[a].at[_slot(*me)], local_sems.at[a]) for a in range(n)]
        for cp in mine:
            cp.start()
        first = []
        for a in range(n):
            first.append(copy(a, 0, me, sibling, src=ins[a]))
            first += [copy(a, 1 + j, me, (*chip, c), src=ins[a]) for j, chip in enumerate(chips)]
        for cp in first:
            cp.start()
        passed = []
        for j, chip in enumerate(chips):
            for a in range(n):
                copy(a, 1 + j, (*chip, c), me).wait_recv()
                fwd = copy(a, 4 + j, (*chip, c), sibling)
                fwd.start()
                passed.append(fwd)
        for a in range(n):
            copy(a, 0, sibling, me).wait_recv()
            for j, chip in enumerate(chips):
                copy(a, 4 + j, (*chip, 1 - c), me).wait_recv()
        for cp in first + passed:
            cp.wait_send()
        for cp in mine:
            cp.wait()

    return pl.pallas_call(
        body, name=name,
        in_specs=[ANY] * n, out_specs=[ANY] * n,
        out_shape=[jax.ShapeDtypeStruct((N_DEV,) + s.shape, s.dtype) for s in shards],
        scratch_shapes=[pltpu.SemaphoreType.DMA((n, 7)), pltpu.SemaphoreType.DMA((n, 7)),
                        pltpu.SemaphoreType.DMA((n,))],
    )(*shards)


def _peer(x, y, c, k):
    return (1 - x if k & 4 else x, 1 - y if k & 2 else y, 1 - c if k & 1 else c)


def _exchange(parts, name):
    n = len(parts)

    def body(*refs):
        ins, outs = refs[:n], refs[n:2 * n]
        send_sems, recv_sems, local_sems = refs[2 * n:]
        x, y, c = _position()
        me = _slot(x, y, c)
        mine = [pltpu.make_async_copy(ins[a].at[me], outs[a].at[me], local_sems.at[a]) for a in range(n)]
        for cp in mine:
            cp.start()
        sends = []
        for k in range(1, N_DEV):
            to = _peer(x, y, c, k)
            for a in range(n):
                cp = pltpu.make_async_remote_copy(
                    src_ref=ins[a].at[_slot(*to)], dst_ref=outs[a].at[me],
                    send_sem=send_sems.at[a, k - 1], recv_sem=recv_sems.at[a, k - 1],
                    device_id=to, device_id_type=MESH)
                cp.start()
                sends.append(cp)
        for k in range(1, N_DEV):
            frm = _peer(x, y, c, k)
            for a in range(n):
                pltpu.make_async_remote_copy(
                    src_ref=ins[a].at[_slot(*frm)], dst_ref=outs[a].at[_slot(*frm)],
                    send_sem=send_sems.at[a, k - 1], recv_sem=recv_sems.at[a, k - 1],
                    device_id=frm, device_id_type=MESH).wait_recv()
        for cp in sends:
            cp.wait_send()
        for cp in mine:
            cp.wait()

    return pl.pallas_call(
        body, name=name,
        in_specs=[ANY] * n, out_specs=[ANY] * n,
        out_shape=[jax.ShapeDtypeStruct(p.shape, p.dtype) for p in parts],
        scratch_shapes=[pltpu.SemaphoreType.DMA((n, 7)), pltpu.SemaphoreType.DMA((n, 7)),
                        pltpu.SemaphoreType.DMA((n,))],
    )(*parts)


HBM_SPEC = pl.BlockSpec(memory_space=pltpu.HBM)
SEM_SPEC = pl.BlockSpec(memory_space=pltpu.SEMAPHORE)
SIDE_EFFECT = pltpu.SideEffectType.DATAFLOW_SIDE_EFFECTING


def _remote_copies(src_refs, land_refs, send_sems, recv_sems, gather, outgoing):
    x, y, c = _position()
    me = _slot(x, y, c)
    copies = []
    for k in range(1, N_DEV):
        peer = _peer(x, y, c, k)
        for a in range(len(src_refs)):
            src = src_refs[a] if gather else src_refs[a].at[_slot(*peer)]
            dst = land_refs[a].at[me if outgoing else _slot(*peer)]
            sem = a * (N_DEV - 1) + k - 1
            copies.append(pltpu.make_async_remote_copy(
                src_ref=src, dst_ref=dst, send_sem=send_sems.at[sem], recv_sem=recv_sems.at[sem],
                device_id=peer, device_id_type=MESH))
    return copies


def _remote_start(srcs, after, name, gather):
    n = len(srcs)
    lands = [jax.ShapeDtypeStruct(((N_DEV,) + s.shape) if gather else s.shape, s.dtype) for s in srcs]

    def body(*refs):
        src_refs, land_refs = refs[:n], refs[n:2 * n]
        send_sems, recv_sems = refs[2 * n + 1], refs[2 * n + 2]
        token, local_sems = refs[4 * n + 3], refs[4 * n + 4]
        x, y, c = _position()
        me = _slot(x, y, c)
        mine = [pltpu.make_async_copy(src_refs[a] if gather else src_refs[a].at[me], land_refs[a].at[me],
                                      local_sems.at[a]) for a in range(n)]
        for cp in mine:
            cp.start()
        for cp in _remote_copies(src_refs, land_refs, send_sems, recv_sems, gather, outgoing=True):
            cp.start()
        for cp in mine:
            cp.wait()
        token[...] = jnp.zeros_like(token)

    sem_shape = pltpu.SemaphoreType.DMA((n * (N_DEV - 1),))
    outs = pl.pallas_call(
        body, name=name,
        out_shape=(sem_shape, sem_shape, *[pltpu.HBM(s.shape, s.dtype) for s in srcs],
                   *[pltpu.HBM(l.shape, l.dtype) for l in lands], jax.ShapeDtypeStruct((8, LANES), F32)),
        in_specs=[HBM_SPEC] * (2 * n) + [ANY],
        out_specs=(SEM_SPEC, SEM_SPEC, *([HBM_SPEC] * (2 * n)), pl.BlockSpec(memory_space=pltpu.VMEM)),
        input_output_aliases={a: 2 + a for a in range(2 * n)},
        scratch_shapes=[pltpu.SemaphoreType.DMA((n,))],
        compiler_params=pltpu.CompilerParams(has_side_effects=SIDE_EFFECT),
    )(*[pltpu.with_memory_space_constraint(s, pltpu.HBM) for s in srcs],
      *[pltpu.with_memory_space_constraint(lax.empty(l.shape, l.dtype), pltpu.HBM) for l in lands], after)
    return dict(send=outs[0], recv=outs[1], srcs=outs[2:2 + n], lands=outs[2 + n:2 + 2 * n], token=outs[-1],
                gather=gather)


def _remote_wait(flight, after, name):
    n = len(flight["srcs"])
    gather = flight["gather"]

    def body(*refs):
        src_refs, land_refs = refs[:n], refs[n:2 * n]
        send_sems, recv_sems = refs[2 * n], refs[2 * n + 1]
        for cp in _remote_copies(src_refs, land_refs, send_sems, recv_sems, gather, outgoing=False):
            cp.wait_send()
            cp.wait_recv()

    both = list(flight["srcs"]) + list(flight["lands"])
    outs = pl.pallas_call(
        body, name=name,
        out_shape=tuple(pltpu.HBM(a.shape, a.dtype) for a in both),
        in_specs=[HBM_SPEC] * (2 * n) + [SEM_SPEC, SEM_SPEC, ANY],
        out_specs=tuple([HBM_SPEC] * (2 * n)),
        input_output_aliases={a: a for a in range(2 * n)},
        compiler_params=pltpu.CompilerParams(has_side_effects=SIDE_EFFECT),
    )(*both, flight["send"], flight["recv"], after)
    return list(outs[n:])


def _sequencer_exchange(srcs, name, gather, collective_id):
    n = len(srcs)
    hbm = pltpu.MemorySpace.HBM
    src_refs = [jax.new_ref(s, memory_space=hbm) for s in srcs]
    land_refs = [jax.empty_ref(jax.ShapeDtypeStruct(((N_DEV,) + s.shape) if gather else s.shape, s.dtype),
                               memory_space=hbm) for s in srcs]
    n_sems = n * (N_DEV - 1)
    block_bytes = sum(s.size * s.dtype.itemsize // (1 if gather else N_DEV) for s in srcs)
    cost = pl.CostEstimate(flops=0, transcendentals=0, bytes_accessed=2 * N_DEV * block_bytes,
                           remote_bytes_transferred=(N_DEV - 1) * block_bytes)

    @pl.kernel(mesh=plsc.ScalarSubcoreMesh(axis_name="sequencer", num_cores=1), name=name,
               scratch_types=(pltpu.SemaphoreType.DMA((n_sems,)), pltpu.SemaphoreType.DMA((n_sems,)),
                              pltpu.SemaphoreType.DMA((n,))),
               cost_estimate=cost,
               compiler_params=pltpu.CompilerParams(collective_id=collective_id))
    def launch(send_sems, recv_sems, local_sems):
        x, y, c = _position()
        me = _slot(x, y, c)
        barrier = pltpu.get_barrier_semaphore()
        for k in range(1, N_DEV):
            pl.semaphore_signal(barrier, inc=1, device_id=_peer(x, y, c, k), device_id_type=MESH)
        pl.semaphore_wait(barrier, N_DEV - 1)
        mine = [pltpu.make_async_copy(src_refs[a] if gather else src_refs[a].at[me], land_refs[a].at[me],
                                      local_sems.at[a]) for a in range(n)]
        for cp in mine:
            cp.start()
        sends = _remote_copies(src_refs, land_refs, send_sems, recv_sems, gather, outgoing=True)
        for cp in sends:
            cp.start()
        for cp in _remote_copies(src_refs, land_refs, send_sems, recv_sems, gather, outgoing=False):
            cp.wait_recv()
        for cp in sends:
            cp.wait_send()
        for cp in mine:
            cp.wait()

    launch()
    return [r[...] for r in land_refs]


def _all_reduce_small(blob, name):
    R, C = blob.shape

    def body(in_ref, out_ref, gath, send_sems, recv_sems):
        x, y, c = _position()
        me = _slot(x, y, c)
        gath[me] = in_ref[...]
        sends = []
        for k in range(1, N_DEV):
            to = _peer(x, y, c, k)
            cp = pltpu.make_async_remote_copy(
                src_ref=in_ref, dst_ref=gath.at[me],
                send_sem=send_sems.at[k - 1], recv_sem=recv_sems.at[k - 1],
                device_id=to, device_id_type=MESH)
            cp.start()
            sends.append(cp)
        for k in range(1, N_DEV):
            frm = _peer(x, y, c, k)
            pltpu.make_async_remote_copy(
                src_ref=in_ref, dst_ref=gath.at[_slot(*frm)],
                send_sem=send_sems.at[k - 1], recv_sem=recv_sems.at[k - 1],
                device_id=frm, device_id_type=MESH).wait_recv()
        for cp in sends:
            cp.wait_send()
        total = gath[0]
        for s in range(1, N_DEV):
            total = total + gath[s]
        out_ref[...] = total

    return pl.pallas_call(
        body, name=name,
        in_specs=[pl.BlockSpec(memory_space=pltpu.VMEM)],
        out_specs=pl.BlockSpec(memory_space=pltpu.VMEM),
        out_shape=jax.ShapeDtypeStruct((R, C), F32),
        scratch_shapes=[pltpu.VMEM((N_DEV, R, C), F32), pltpu.SemaphoreType.DMA((7,)),
                        pltpu.SemaphoreType.DMA((7,))],
        compiler_params=pltpu.CompilerParams(vmem_limit_bytes=VMEM_LIMIT),
    )(blob)


SMALL_VECS = ("ffn1_pre_g", "ffn1_post_g", "mix_pre_g", "sgu_ln_g", "sgu_ln_b", "mix_post_g", "ffn2_pre_g",
              "ffn2_post_g")
ROW_BS = len(SMALL_VECS)
ROW_BF = ROW_BS + 1
ROW_LOSS = ROW_BF + 1
ROW_WS = 16
BLOB_ROWS = ROW_WS + SGU_LEN


def _pack_small(vals, D, loss_row=None):
    rows = [vals[n].reshape(1, D) for n in SMALL_VECS]
    rows.append(vals["sgu_b_s"].reshape(1, D))
    rows.append(jnp.pad(vals["b_forget"].reshape(1, N_HEADS), ((0, 0), (0, D - N_HEADS))))
    rows.append(jnp.zeros((1, D), F32) if loss_row is None else loss_row)
    rows.append(jnp.zeros((ROW_WS - ROW_LOSS - 1, D), F32))
    rows.append(vals["sgu_w_s"].reshape(SGU_LEN, D))
    return jnp.concatenate(rows, axis=0)


def _unpack_small(blob, D):
    out = {n: blob[r:r + 1] for r, n in enumerate(SMALL_VECS)}
    out["sgu_b_s"] = blob[ROW_BS].reshape(1, N_GROUPS, SGU_LEN)
    out["b_forget"] = blob[ROW_BF, :N_HEADS].reshape(1, N_HEADS)
    out["sgu_w_s"] = blob[ROW_WS:].reshape(1, N_GROUPS, SGU_LEN, SGU_LEN)
    return out


WEIGHT_NAMES = ("ffn1_pre_g", "ffn1_w_gate", "ffn1_w_up", "ffn1_w_down", "ffn1_post_g", "mix_pre_g", "w_in",
                "b_forget", "sgu_ln_g", "sgu_ln_b", "sgu_w_s", "sgu_b_s", "w_out", "mix_post_g", "ffn2_pre_g",
                "ffn2_w_gate", "ffn2_w_up", "ffn2_w_down", "ffn2_post_g")
BIG_NAMES = ("ffn1_w_gate", "ffn1_w_up", "ffn1_w_down", "w_in", "w_out", "ffn2_w_gate", "ffn2_w_up", "ffn2_w_down")
WEIGHT_GROUPS = {"ffn1": ("ffn1_w_gate", "ffn1_w_up", "ffn1_w_down"), "mix": ("w_in", "w_out"),
                 "ffn2": ("ffn2_w_gate", "ffn2_w_up", "ffn2_w_down")}
GRAD_GROUPS = (("ffn2_w_gate", "ffn2_w_up", "ffn2_w_down"), ("w_in", "w_out"), ("ffn1_w_down",), ("ffn1_w_gate",),
               ("ffn1_w_up",))


def _local_step(x, target, small, fetch, emit, consume):
    T, D = x.shape
    W = N_HEADS * HEAD_DIM
    vec = lambda n: small[n].reshape(1, D)
    big = dict(fetch("ffn1", x))

    x1, y1, dgf1, silu1, act1 = _ffn_fwd(x, vec("ffn1_pre_g"), big["ffn1_w_gate"], big["ffn1_w_up"], big["ffn1_w_down"],
                                  vec("ffn1_post_g"), "ffn1_fwd")

    big.update(fetch("mix", x1))
    w_in_all = big["w_in"]
    in_width = N_DEV * w_in_all.shape[2]
    w_in = w_in_all.transpose(1, 0, 2).reshape(D, in_width)
    col_f = 3 * W
    col_u = col_f + N_HEADS
    seg_starts = (0, W, 2 * W, col_u, col_u + W, col_u + 2 * W, col_u + 3 * W)
    w7 = jnp.stack([w_in[:, s:s + W] for s in seg_starts])
    wf = jnp.pad(w_in[:, col_f:col_u], ((0, 0), (0, LANES - N_HEADS)))
    w_out = big["w_out"].reshape(D, D)
    b_pad = jnp.pad(small["b_forget"].reshape(1, N_HEADS), ((0, 0), (0, LANES - N_HEADS)))
    lng, lnb = vec("sgu_ln_g"), vec("sgu_ln_b")
    ws = small["sgu_w_s"].reshape(N_GROUPS, SGU_LEN, SGU_LEN)
    bs = small["sgu_b_s"].reshape(N_GROUPS, SGU_LEN, 1)

    z7, f_logit, h2b = _mix_in_fwd(x1, vec("mix_pre_g"), w7, wf, "mix_in_fwd")
    c = _forget_cumsum(f_logit, b_pad, "forget_cumsum")
    c_heads = c[:, :N_HEADS].T
    ta, _, n_chunks = _attn_geometry(T)
    c_chunks = c_heads.reshape(N_HEADS, n_chunks, 1, ta)
    c_col = c_heads[:, :, None]
    vt = z7[2].reshape(n_chunks, ta, N_HEADS, HEAD_DIM).transpose(2, 0, 3, 1)
    c_rep = jnp.broadcast_to(c_col * LOG2E, (N_HEADS, T, LANES))
    o_a, lse_chunks = _attn_fwd_keys_on_rows(z7, vt, c_rep, "attn_fwd")
    lse = lse_chunks.reshape(N_HEADS, T, 1)
    x2, p, merged_b = _mix_out_fwd(z7, o_a, x1, lng, lnb, ws, bs, w_out, vec("mix_post_g"), "mix_out_fwd")
    big.update(fetch("ffn2", x2))
    x3, y2, dgf2, silu2, act2 = _ffn_fwd(x2, vec("ffn2_pre_g"), big["ffn2_w_gate"], big["ffn2_w_up"], big["ffn2_w_down"],
                                  vec("ffn2_post_g"), "ffn2_fwd")
    dy, loss_lanes = _loss_head(x3, target, "loss_head")

    grads_small = {}

    dx2, h3b, dy2b, dgate2, dup2, dgpre, dgpost = _ffn_bwd(
        dy, x2, y2, dgf2, silu2, vec("ffn2_pre_g"), big["ffn2_w_gate"], big["ffn2_w_up"], big["ffn2_w_down"],
        vec("ffn2_post_g"), "ffn2_bwd")
    grads_small["ffn2_pre_g"] = jnp.sum(dgpre, axis=0)
    grads_small["ffn2_post_g"] = jnp.sum(dgpost, axis=0)
    dep = emit("ffn2_w_gate", _wgrad(h3b, dgate2, "ffn2_wgrad_gate", shard_cols=True))
    dep = emit("ffn2_w_up", _wgrad(h3b, dup2, "ffn2_wgrad_up", shard_cols=True, dep=dep))
    dep = emit("ffn2_w_down", _wgrad(act2, dy2b, "ffn2_wgrad_down", dep=dep).reshape(big["ffn2_w_down"].shape))

    dpb, dob, dvec, dz4, dgp, dlng, dlnb, dws, dbs = _mix_out_bwd(
        dx2, p, z7, o_a, lng, lnb, ws, bs, w_out, vec("mix_post_g"), "mix_out_bwd", dep=dep)
    grads_small["mix_post_g"] = dgp
    grads_small["sgu_ln_g"] = dlng
    grads_small["sgu_ln_b"] = dlnb
    grads_small["sgu_w_s"] = dws
    grads_small["sgu_b_s"] = dbs
    d_chunks = dvec.reshape(N_HEADS, n_chunks, 1, ta)
    kt = z7[1].reshape(n_chunks, ta, N_HEADS, HEAD_DIM).transpose(2, 0, 3, 1)
    dk, dv, dc, dq, dc_q = _attn_bwd_fused(z7, kt, dob, c_rep, lse_chunks, d_chunks, "attn_bwd")
    dc_pad = jnp.pad((dc + dc_q).reshape(N_HEADS, T).T, ((0, 0), (0, LANES - N_HEADS)))
    dfb, dbf = _forget_bwd(dc_pad, f_logit, b_pad, "forget_bwd")
    grads_small["b_forget"] = dbf[:, :N_HEADS]
    segs = [(dq, None), (dk, None), (dv, None), (dz4, 0), (dz4, 1), (dz4, 2), (dz4, 3)]
    dep = consume(("ffn2_w_gate", "ffn2_w_up", "ffn2_w_down"))
    dx1, dgm = _mix_in_bwd(dx2, x1, vec("mix_pre_g"), segs, dfb, w7, wf, "mix_in_bwd", dep=dep)
    grads_small["mix_pre_g"] = jnp.sum(dgm, axis=0)
    dw_seg, dep = [], dx1
    for q, (sm, idx) in enumerate(segs):
        dw_seg.append(_wgrad(h2b, sm, "w_in_wgrad_%d" % q, dep=dep, y_index=idx))
        dep = dw_seg[-1]
    dwf = _wgrad(h2b, dfb, "w_in_wgrad_f", dep=dep)
    dw_in = jnp.concatenate(dw_seg[:3] + [dwf[:, :N_HEADS]] + dw_seg[3:], axis=1)
    emit("w_in", dw_in.reshape(D, N_DEV, in_width // N_DEV).transpose(1, 0, 2))
    dep = emit("w_out", _wgrad(merged_b, dpb, "w_out_wgrad", dep=dwf).reshape(big["w_out"].shape))

    dx0, h1b, dy1b, dgate1, dup1, dgpre1, dgpost1 = _ffn_bwd(
        dx1, x, y1, dgf1, silu1, vec("ffn1_pre_g"), big["ffn1_w_gate"], big["ffn1_w_up"], big["ffn1_w_down"],
        vec("ffn1_post_g"), "ffn1_bwd", dep=dep)
    grads_small["ffn1_pre_g"] = jnp.sum(dgpre1, axis=0)
    grads_small["ffn1_post_g"] = jnp.sum(dgpost1, axis=0)
    dep = consume(("w_in", "w_out"))
    dep = emit("ffn1_w_down", _wgrad(act1, dy1b, "ffn1_wgrad_down", dep=dep).reshape(big["ffn1_w_down"].shape))
    dep = emit("ffn1_w_gate", _wgrad(h1b, dgate1, "ffn1_wgrad_gate", shard_cols=True, dep=dep))
    dep = emit("ffn1_w_up", _wgrad(h1b, dup1, "ffn1_wgrad_up", shard_cols=True, dep=dep))

    loss_row = jnp.pad(loss_lanes, ((0, 0), (0, D - LANES)))
    return loss_row, dx0, grads_small


def kernel(x, ffn1_pre_g, ffn1_w_gate, ffn1_w_up, ffn1_w_down, ffn1_post_g, mix_pre_g, w_in, b_forget, sgu_ln_g, sgu_ln_b, sgu_w_s, sgu_b_s, w_out, mix_post_g, ffn2_pre_g, ffn2_w_gate, ffn2_w_up, ffn2_w_down, ffn2_post_g, loss_target, m_ffn1_pre_g, m_ffn1_w_gate, m_ffn1_w_up, m_ffn1_w_down, m_ffn1_post_g, m_mix_pre_g, m_w_in, m_b_forget, m_sgu_ln_g, m_sgu_ln_b, m_sgu_w_s, m_sgu_b_s, m_w_out, m_mix_post_g, m_ffn2_pre_g, m_ffn2_w_gate, m_ffn2_w_up, m_ffn2_w_down, m_ffn2_post_g, v_ffn1_pre_g, v_ffn1_w_gate, v_ffn1_w_up, v_ffn1_w_down, v_ffn1_post_g, v_mix_pre_g, v_w_in, v_b_forget, v_sgu_ln_g, v_sgu_ln_b, v_sgu_w_s, v_sgu_b_s, v_w_out, v_mix_post_g, v_ffn2_pre_g, v_ffn2_w_gate, v_ffn2_w_up, v_ffn2_w_down, v_ffn2_post_g):
    weights = dict(zip(WEIGHT_NAMES, (ffn1_pre_g, ffn1_w_gate, ffn1_w_up, ffn1_w_down, ffn1_post_g, mix_pre_g, w_in,
                                      b_forget, sgu_ln_g, sgu_ln_b, sgu_w_s, sgu_b_s, w_out, mix_post_g, ffn2_pre_g,
                                      ffn2_w_gate, ffn2_w_up, ffn2_w_down, ffn2_post_g)))
    mom1 = dict(zip(WEIGHT_NAMES, (m_ffn1_pre_g, m_ffn1_w_gate, m_ffn1_w_up, m_ffn1_w_down, m_ffn1_post_g,
                                   m_mix_pre_g, m_w_in, m_b_forget, m_sgu_ln_g, m_sgu_ln_b, m_sgu_w_s, m_sgu_b_s,
                                   m_w_out, m_mix_post_g, m_ffn2_pre_g, m_ffn2_w_gate, m_ffn2_w_up, m_ffn2_w_down,
                                   m_ffn2_post_g)))
    mom2 = dict(zip(WEIGHT_NAMES, (v_ffn1_pre_g, v_ffn1_w_gate, v_ffn1_w_up, v_ffn1_w_down, v_ffn1_post_g,
                                   v_mix_pre_g, v_w_in, v_b_forget, v_sgu_ln_g, v_sgu_ln_b, v_sgu_w_s, v_sgu_b_s,
                                   v_w_out, v_mix_post_g, v_ffn2_pre_g, v_ffn2_w_gate, v_ffn2_w_up, v_ffn2_w_down,
                                   v_ffn2_post_g)))
    D = x.shape[-1]
    small_names = [n for n in WEIGHT_NAMES if n not in BIG_NAMES]

    small = {n: weights[n] for n in small_names}
    shard = lambda n: weights[n][0].astype(BF16)

    ffn1_full = _all_gather([shard(n) for n in WEIGHT_GROUPS["ffn1"]], "ffn1_all_gather")
    gathered = {}
    for cid, grp in ((1, "mix"), (2, "ffn2")):
        shards, _ = lax.optimization_barrier(([shard(n) for n in WEIGHT_GROUPS[grp]], ffn1_full[0]))
        gathered[grp] = _sequencer_exchange(shards, grp + "_gather", True, cid)

    def fetch(group, after):
        if group == "ffn1":
            return zip(WEIGHT_GROUPS[group], ffn1_full)
        arrived, _ = lax.optimization_barrier((gathered[group], after))
        return zip(WEIGHT_GROUPS[group], arrived)

    ready, received = {}, {}

    def emit(name, part):
        ready[name] = part
        for gi, group in enumerate(GRAD_GROUPS):
            if name == group[-1]:
                lands = _sequencer_exchange([ready[n] for n in group], name + "_grad_exchange", False, 3 + gi)
                received.update(zip(group, lands))
        return part

    out = {}

    def consume(names, dep=None):
        for n in names:
            g, d, m_new, v_new = _sum_adamw(received[n], weights[n][0], mom1[n][0], mom2[n][0], "adamw_" + n, dep=dep)
            out[n] = tuple(a[None] for a in (g, d, m_new, v_new))
            dep = g
        return dep

    loss_row, grad_x, grads_small = _local_step(x[0], loss_target[0], small, fetch, emit, consume)

    blobs = _sequencer_exchange([_pack_small(grads_small, D, loss_row)], "small_gather", True,
                                3 + len(GRAD_GROUPS))[0]
    blob, d_blob, m_blob, v_blob = _sum_adamw(
        blobs, _pack_small(small, D), _pack_small({n: mom1[n] for n in small_names}, D),
        _pack_small({n: mom2[n] for n in small_names}, D), "adamw_small")
    consume(("ffn1_w_down", "ffn1_w_gate", "ffn1_w_up"), dep=blob)
    unpacked = [_unpack_small(b, D) for b in (blob, d_blob, m_blob, v_blob)]
    for n in small_names:
        out[n] = tuple(u[n].reshape(weights[n].shape) for u in unpacked)

    loss = blob[ROW_LOSS, 0]
    result = [loss, grad_x[None]]
    for k in range(4):
        result += [out[n][k] for n in WEIGHT_NAMES]
    return tuple(result)
```

```python
import numpy as np
import jax
import jax.numpy as jnp
from jax import lax
from jax.experimental import pallas as pl
from jax.experimental.pallas import tpu as pltpu
from jax.experimental.pallas import tpu_sc as plsc

F32 = jnp.float32
BF16 = jnp.bfloat16

RMS_EPS = 1e-6
LN_EPS = 1e-5
HEAD_DIM = 128
N_HEADS = 8
GROUP_DIM = 128
N_GROUPS = 8
SGU_LEN = 128
CHUNK = 64
N_DEV = 8
LANES = 128
VMEM_LIMIT = 56 * 1024 * 1024
NEG_BIG = -1e30
LOG2E = np.float32(1.0 / np.log(2.0))

ADAM_LR = 0.001
ADAM_B1 = 0.9
ADAM_B2 = 0.999
ADAM_EPS = 1e-08
ADAM_WD = 0.01
ADAM_STEP = 10

MESH = pl.DeviceIdType.MESH
ANY = pl.BlockSpec(memory_space=pl.ANY)


def _blk(n, pref):
    return pref if (n >= pref and n % pref == 0) else n


def _mm(a, b):
    return jnp.dot(a, b, preferred_element_type=F32)


def _mm_nt(a, b):
    return lax.dot_general(a, b, (((1,), (1,)), ((), ())), preferred_element_type=F32)


def _mm_tn(a, b):
    return lax.dot_general(a, b, (((0,), (0,)), ((), ())), preferred_element_type=F32)


def _params(sem):
    return pltpu.CompilerParams(dimension_semantics=sem, vmem_limit_bytes=VMEM_LIMIT)


def _gelu(x):
    return 0.5 * x * (1.0 + lax.erf(x * np.float32(1.0 / np.sqrt(2.0))))


def _gelu_grad(x):
    cdf = 0.5 * (1.0 + lax.erf(x * np.float32(1.0 / np.sqrt(2.0))))
    return cdf + x * jnp.exp(-0.5 * x * x) * np.float32(1.0 / np.sqrt(2.0 * np.pi))


def _rms_scale(v):
    return lax.rsqrt(jnp.mean(v * v, axis=-1, keepdims=True) + RMS_EPS)


def _rms_bwd(dy, xhat, r, g):
    dxh = dy * g
    return r * (dxh - xhat * jnp.mean(dxh * xhat, axis=-1, keepdims=True))


def _ffn_rows(T):
    tm = _blk(T, 1024)
    th = _blk(tm, 512)
    return tm, th, tm // th


def _ffn_fwd(x, g_pre, wg, wu, wd, g_post, name):
    T, D = x.shape
    ns, _, fs = wg.shape
    tm, th, parts = _ffn_rows(T)

    def body(x_ref, gpre_ref, wg_ref, wu_ref, wd_ref, gpost_ref, xo_ref, y_ref, dgf_ref, silu_ref, act_ref,
             h_scr, acc_scr):
        j = pl.program_id(1)

        @pl.when(j == 0)
        def _():
            for r in range(parts):
                rows = slice(r * th, (r + 1) * th)
                xv = x_ref[rows, :]
                h_scr[rows, :] = (xv * _rms_scale(xv) * gpre_ref[...]).astype(BF16)
            acc_scr[...] = jnp.zeros_like(acc_scr)

        pre = []
        for r in range(parts):
            h = h_scr[r * th:(r + 1) * th, :]
            pre.append((_mm(h, wg_ref[...]), _mm(h, wu_ref[...])))
        for r in range(parts):
            rows = slice(r * th, (r + 1) * th)
            gg, uu = pre[r]
            sg = jax.nn.sigmoid(gg)
            silu = gg * sg
            act = (silu * uu).astype(BF16)
            dgf_ref[rows, :] = (uu * (sg * (1.0 + gg * (1.0 - sg)))).astype(BF16)
            silu_ref[rows, :] = silu.astype(BF16)
            act_ref[rows, :] = act
            acc_scr[rows, :] += _mm(act, wd_ref[...])

        @pl.when(j == ns - 1)
        def _():
            for r in range(parts):
                rows = slice(r * th, (r + 1) * th)
                y = acc_scr[rows, :]
                y_ref[rows, :] = y
                xo_ref[rows, :] = x_ref[rows, :] + 0.5 * (y * _rms_scale(y) * gpost_ref[...])

    row = pl.BlockSpec((tm, D), lambda i, j: (i, 0), pipeline_mode=pl.Buffered(1))
    vec = pl.BlockSpec((1, D), lambda i, j: (0, 0))
    return pl.pallas_call(
        body, name=name, grid=(T // tm, ns),
        in_specs=[pl.BlockSpec((tm, D), lambda i, j: (i, 0)), vec,
                  pl.BlockSpec((None, D, fs), lambda i, j: (j, 0, 0)),
                  pl.BlockSpec((None, D, fs), lambda i, j: (j, 0, 0)),
                  pl.BlockSpec((None, fs, D), lambda i, j: (j, 0, 0)),
                  vec],
        out_specs=[row, row] + [pl.BlockSpec((tm, fs), lambda i, j: (i, j))] * 3,
        out_shape=[jax.ShapeDtypeStruct((T, D), F32), jax.ShapeDtypeStruct((T, D), F32)]
        + [jax.ShapeDtypeStruct((T, ns * fs), BF16)] * 3,
        scratch_shapes=[pltpu.VMEM((tm, D), BF16), pltpu.VMEM((tm, D), F32)],
        compiler_params=_params(("parallel", "arbitrary")),
    )(x, g_pre, wg, wu, wd, g_post)


def _after(dep):
    return jnp.zeros((8, LANES), F32) if dep is None else dep


def _ffn_bwd(dxo, x, y, dgf, silu, g_pre, wg, wu, wd, g_post, name, dep=None):
    T, D = x.shape
    ns, _, fs = wg.shape
    tm, th, parts = _ffn_rows(T)
    n_i = T // tm

    def body(dxo_ref, x_ref, y_ref, dgf_ref, silu_ref, gpre_ref, wg_ref, wu_ref, wd_ref, gpost_ref, _,
             dx_ref, hb_ref, dyb_ref, dgb_ref, dub_ref, dgpre_ref, dgpost_ref, dy_scr, acc_scr):
        j = pl.program_id(1)

        @pl.when(j == 0)
        def _():
            dgpost = jnp.zeros((1, D), F32)
            for r in range(parts):
                rows = slice(r * th, (r + 1) * th)
                yv = y_ref[rows, :]
                s = _rms_scale(yv)
                n = yv * s
                dn = 0.5 * dxo_ref[rows, :]
                dgpost = dgpost + jnp.sum(dn * n, axis=0, keepdims=True)
                dyv = _rms_bwd(dn, n, s, gpost_ref[...]).astype(BF16)
                dy_scr[rows, :] = dyv
                dyb_ref[rows, :] = dyv
                xv = x_ref[rows, :]
                hb_ref[rows, :] = (xv * _rms_scale(xv) * gpre_ref[...]).astype(BF16)
            dgpost_ref[...] = dgpost
            acc_scr[...] = jnp.zeros_like(acc_scr)

        das = [_mm_nt(dy_scr[r * th:(r + 1) * th, :], wd_ref[...]) for r in range(parts)]
        for r in range(parts):
            rows = slice(r * th, (r + 1) * th)
            dgate = (das[r] * dgf_ref[rows, :].astype(F32)).astype(BF16)
            dup = (das[r] * silu_ref[rows, :].astype(F32)).astype(BF16)
            dgb_ref[rows, :] = dgate
            dub_ref[rows, :] = dup
            acc_scr[rows, :] += _mm_nt(dgate, wg_ref[...]) + _mm_nt(dup, wu_ref[...])

        @pl.when(j == ns - 1)
        def _():
            dgpre = jnp.zeros((1, D), F32)
            for r in range(parts):
                rows = slice(r * th, (r + 1) * th)
                xv = x_ref[rows, :]
                rs = _rms_scale(xv)
                xhat = xv * rs
                dh = acc_scr[rows, :]
                dgpre = dgpre + jnp.sum(dh * xhat, axis=0, keepdims=True)
                dx_ref[rows, :] = _rms_bwd(dh, xhat, rs, gpre_ref[...]) + dxo_ref[rows, :]
            dgpre_ref[...] = dgpre

    row = pl.BlockSpec((tm, D), lambda i, j: (i, 0), pipeline_mode=pl.Buffered(1))
    vec = pl.BlockSpec((1, D), lambda i, j: (0, 0))
    wide = pl.BlockSpec((tm, fs), lambda i, j: (i, j))
    part = pl.BlockSpec((None, 1, D), lambda i, j: (i, 0, 0))
    F = ns * fs
    return pl.pallas_call(
        body, name=name, grid=(n_i, ns),
        in_specs=[row, row, row, wide, wide, vec,
                  pl.BlockSpec((None, D, fs), lambda i, j: (j, 0, 0)),
                  pl.BlockSpec((None, D, fs), lambda i, j: (j, 0, 0)),
                  pl.BlockSpec((None, fs, D), lambda i, j: (j, 0, 0)),
                  vec, ANY],
        out_specs=[row, row, row, wide, wide, part, part],
        out_shape=[jax.ShapeDtypeStruct((T, D), F32), jax.ShapeDtypeStruct((T, D), BF16),
                   jax.ShapeDtypeStruct((T, D), BF16), jax.ShapeDtypeStruct((T, F), BF16),
                   jax.ShapeDtypeStruct((T, F), BF16),
                   jax.ShapeDtypeStruct((n_i, 1, D), F32), jax.ShapeDtypeStruct((n_i, 1, D), F32)],
        scratch_shapes=[pltpu.VMEM((tm, D), BF16), pltpu.VMEM((tm, D), F32)],
        compiler_params=_params(("parallel", "arbitrary")),
    )(dxo, x, y, dgf, silu, g_pre, wg, wu, wd, g_post, _after(dep))


def _wgrad(xm, ym, name, shard_cols=False, dep=None, y_index=None):
    T, M = xm.shape
    N = ym.shape[-1]
    if y_index is None:
        y_spec = pl.BlockSpec((_blk(T, 512), N), lambda k: (k, 0))
    else:
        y_spec = pl.BlockSpec((None, _blk(T, 512), N), lambda k: (y_index, k, 0))
    assert M * N * 4 <= 16 * 1024 * 1024, (M, N)
    tk = _blk(T, 512)
    n_k = T // tk
    fs = N // N_DEV

    def body(x_ref, y_ref, _, o_ref, acc_scr):
        k = pl.program_id(0)

        @pl.when(k == 0)
        def _():
            acc_scr[...] = jnp.zeros_like(acc_scr)

        acc_scr[...] += _mm_tn(x_ref[...], y_ref[...])

        @pl.when(k == n_k - 1)
        def _():
            if shard_cols:
                for s in range(N_DEV):
                    o_ref[s] = acc_scr[:, s * fs:(s + 1) * fs].astype(BF16)
            else:
                o_ref[...] = acc_scr[...].astype(BF16)

    if shard_cols:
        out_spec = pl.BlockSpec((N_DEV, M, fs), lambda k: (0, 0, 0), pipeline_mode=pl.Buffered(1))
        out_shape = jax.ShapeDtypeStruct((N_DEV, M, fs), BF16)
    else:
        out_spec = pl.BlockSpec((M, N), lambda k: (0, 0), pipeline_mode=pl.Buffered(1))
        out_shape = jax.ShapeDtypeStruct((M, N), BF16)
    return pl.pallas_call(
        body, name=name, grid=(n_k,),
        in_specs=[pl.BlockSpec((tk, M), lambda k: (k, 0)), y_spec, ANY],
        out_specs=out_spec, out_shape=out_shape,
        scratch_shapes=[pltpu.VMEM((M, N), F32)],
        compiler_params=_params(("arbitrary",)),
    )(xm, ym, _after(dep))


def _mix_in_fwd(x1, g, w7, wf, name):
    T, D = x1.shape
    n_seg, _, W = w7.shape
    tm = _blk(T, 1024)

    def body(x_ref, g_ref, w_ref, wf_ref, z_ref, f_ref, hb_ref, h_scr):
        s = pl.program_id(1)

        @pl.when(s == 0)
        def _():
            xv = x_ref[...]
            h = (xv * _rms_scale(xv) * g_ref[...]).astype(BF16)
            h_scr[...] = h
            hb_ref[...] = h
            f_ref[...] = _mm(h, wf_ref[...])

        z_ref[...] = _mm(h_scr[...], w_ref[...]).astype(BF16)

    return pl.pallas_call(
        body, name=name, grid=(T // tm, n_seg),
        in_specs=[pl.BlockSpec((tm, D), lambda i, s: (i, 0)),
                  pl.BlockSpec((1, D), lambda i, s: (0, 0)),
                  pl.BlockSpec((None, D, W), lambda i, s: (s, 0, 0)),
                  pl.BlockSpec((D, LANES), lambda i, s: (0, 0))],
        out_specs=[pl.BlockSpec((None, tm, W), lambda i, s: (s, i, 0)),
                   pl.BlockSpec((tm, LANES), lambda i, s: (i, 0)),
                   pl.BlockSpec((tm, D), lambda i, s: (i, 0))],
        out_shape=[jax.ShapeDtypeStruct((n_seg, T, W), BF16), jax.ShapeDtypeStruct((T, LANES), F32),
                   jax.ShapeDtypeStruct((T, D), BF16)],
        scratch_shapes=[pltpu.VMEM((tm, D), BF16)],
        compiler_params=_params(("parallel", "arbitrary")),
    )(x1, g, w7, wf)


def _mix_in_bwd(dx2, x1, g, segs, dfb, w7, wf, name, dep=None):
    T, D = x1.shape
    n_seg, _, W = w7.shape
    tm, th, parts = _ffn_rows(T)
    n_i = T // tm

    def body(*refs):
        dx2_ref, x_ref, g_ref = refs[:3]
        seg_refs = refs[3:3 + n_seg]
        df_ref, w_ref, wf_ref, _, dx1_ref, dg_ref, acc_scr = refs[3 + n_seg:]
        s = pl.program_id(1)

        @pl.when(s == 0)
        def _():
            acc_scr[...] = _mm_nt(df_ref[...], wf_ref[...])

        for q in range(n_seg):
            @pl.when(s == q)
            def _(q=q):
                acc_scr[...] += _mm_nt(seg_refs[q][...], w_ref[...])

        @pl.when(s == n_seg - 1)
        def _():
            dg = jnp.zeros((1, D), F32)
            for p in range(parts):
                rows = slice(p * th, (p + 1) * th)
                xv = x_ref[rows, :]
                r = _rms_scale(xv)
                xhat = xv * r
                dh = acc_scr[rows, :]
                dg = dg + jnp.sum(dh * xhat, axis=0, keepdims=True)
                dx1_ref[rows, :] = _rms_bwd(dh, xhat, r, g_ref[...]) + dx2_ref[rows, :]
            dg_ref[...] = dg

    row = pl.BlockSpec((tm, D), lambda i, s: (i, 0), pipeline_mode=pl.Buffered(1))
    seg_specs = []
    seg_args = []
    for arr, idx in segs:
        if idx is None:
            seg_specs.append(pl.BlockSpec((tm, W), lambda i, s: (i, 0)))
        else:
            seg_specs.append(pl.BlockSpec((None, tm, W), lambda i, s, idx=idx: (idx, i, 0)))
        seg_args.append(arr)
    return pl.pallas_call(
        body, name=name, grid=(n_i, n_seg),
        in_specs=[row, row, pl.BlockSpec((1, D), lambda i, s: (0, 0))] + seg_specs + [
            pl.BlockSpec((tm, LANES), lambda i, s: (i, 0)),
            pl.BlockSpec((None, D, W), lambda i, s: (s, 0, 0)),
            pl.BlockSpec((D, LANES), lambda i, s: (0, 0)), ANY],
        out_specs=[row, pl.BlockSpec((None, 1, D), lambda i, s: (i, 0, 0))],
        out_shape=[jax.ShapeDtypeStruct((T, D), F32), jax.ShapeDtypeStruct((n_i, 1, D), F32)],
        scratch_shapes=[pltpu.VMEM((tm, D), F32)],
        compiler_params=_params(("parallel", "arbitrary")),
    )(dx2, x1, g, *seg_args, dfb, w7, wf, _after(dep))


def _forget_cumsum(f, b_pad, name):
    T, L = f.shape
    tb = _blk(T, 256)

    def body(f_ref, b_ref, c_ref, carry):
        @pl.when(pl.program_id(0) == 0)
        def _():
            carry[...] = jnp.zeros_like(carry)

        lf = jax.nn.log_sigmoid(f_ref[...] + b_ref[...])
        rows = lax.broadcasted_iota(jnp.int32, (tb, tb), 0)
        cols = lax.broadcasted_iota(jnp.int32, (tb, tb), 1)
        tri = (cols <= rows).astype(F32)
        c = jnp.dot(tri, lf, preferred_element_type=F32, precision=lax.Precision.HIGHEST) + carry[...]
        c_ref[...] = c
        carry[...] = c[tb - 1:tb, :]

    return pl.pallas_call(
        body, name=name, grid=(T // tb,),
        in_specs=[pl.BlockSpec((tb, L), lambda i: (i, 0)), pl.BlockSpec((1, L), lambda i: (0, 0))],
        out_specs=pl.BlockSpec((tb, L), lambda i: (i, 0)),
        out_shape=jax.ShapeDtypeStruct((T, L), F32),
        scratch_shapes=[pltpu.VMEM((1, L), F32)],
        compiler_params=_params(("arbitrary",)),
    )(f, b_pad)


def _forget_bwd(dc, f, b_pad, name):
    T, L = f.shape
    tb = _blk(T, 256)
    nb = T // tb

    def body(dc_ref, f_ref, b_ref, df_ref, db_ref, carry):
        @pl.when(pl.program_id(0) == 0)
        def _():
            carry[...] = jnp.zeros_like(carry)
            db_ref[...] = jnp.zeros_like(db_ref)

        rows = lax.broadcasted_iota(jnp.int32, (tb, tb), 0)
        cols = lax.broadcasted_iota(jnp.int32, (tb, tb), 1)
        tri = (cols >= rows).astype(F32)
        r = jnp.dot(tri, dc_ref[...], preferred_element_type=F32, precision=lax.Precision.HIGHEST) + carry[...]
        carry[...] = r[0:1, :]
        df = r * (1.0 - jax.nn.sigmoid(f_ref[...] + b_ref[...]))
        df_ref[...] = df.astype(BF16)
        db_ref[...] += jnp.sum(df, axis=0, keepdims=True)

    rev = pl.BlockSpec((tb, L), lambda i: (nb - 1 - i, 0))
    one = pl.BlockSpec((1, L), lambda i: (0, 0))
    return pl.pallas_call(
        body, name=name, grid=(nb,),
        in_specs=[rev, rev, one], out_specs=[rev, one],
        out_shape=[jax.ShapeDtypeStruct((T, L), BF16), jax.ShapeDtypeStruct((1, L), F32)],
        scratch_shapes=[pltpu.VMEM((1, L), F32)],
        compiler_params=_params(("arbitrary",)),
    )(dc, f, b_pad)


ATTN_TILE = 512
ATTN_CHAINS = 4


def _attn_geometry(T):
    ta = _blk(T, ATTN_TILE)
    nc = ATTN_CHAINS if (T // ta) % ATTN_CHAINS == 0 else 1
    return ta, nc, T // ta


def _causal_tile(ta, keys_on_rows=False):
    rows = lax.broadcasted_iota(jnp.int32, (ta, ta), 0)
    cols = lax.broadcasted_iota(jnp.int32, (ta, ta), 1)
    return rows <= cols if keys_on_rows else cols <= rows


def _chunk(ref, j, ta):
    return ref[pl.ds(pl.multiple_of(j * ta, ta), ta), :]


def _attn_fwd_keys_on_rows(z7, vt, c_rep, name):
    _, T, W = z7.shape
    H = W // HEAD_DIM
    ta, nc, n_chunks = _attn_geometry(T)
    scale = np.float32(1.0 / np.sqrt(HEAD_DIM))
    reps = ta // LANES

    def body(q_ref, k_ref, vt_ref, c_ref, o_ref, lse_ref):
        g = pl.program_id(1)

        def scores(ch, k):
            return _mm_nt(k, q_ref[ch * ta:(ch + 1) * ta, :])

        def update(state, raw, vt, cj, diagonal):
            m_prev, l_prev, acc_prev = state
            st = raw * (scale * LOG2E) - cj
            if diagonal:
                st = jnp.where(_causal_tile(ta, keys_on_rows=True), st, NEG_BIG)
            m_new = jnp.maximum(m_prev, jnp.max(st, axis=0, keepdims=True))
            alpha = jnp.exp2(m_prev - m_new)
            pt = jnp.exp2(st - m_new)
            l_new = alpha * l_prev + jnp.sum(pt, axis=0, keepdims=True)
            acc_new = alpha * acc_prev + _mm(vt, pt.astype(BF16))
            return m_new, l_new, acc_new

        def load(j):
            cj = _chunk(c_ref, j, ta)
            return _chunk(k_ref, j, ta), vt_ref[j], jnp.concatenate([cj] * reps, axis=1)

        def full_chunk(j, states):
            k, vt, cj = load(j)
            raws = [scores(ch, k) for ch in range(nc)]
            return tuple(update(states[ch], raws[ch], vt, cj, False) for ch in range(nc))

        first = (jnp.full((1, ta), NEG_BIG, F32), jnp.zeros((1, ta), F32), jnp.zeros((HEAD_DIM, ta), F32))
        states = list(lax.fori_loop(0, nc * g, full_chunk, (first,) * nc))
        for jj in range(nc):
            k, vt, cj = load(nc * g + jj)
            raws = {ch: scores(ch, k) for ch in range(jj, nc)}
            for ch in range(jj, nc):
                states[ch] = update(states[ch], raws[ch], vt, cj, ch == jj)
        for ch in range(nc):
            m, l, acc = states[ch]
            o_ref[ch * ta:(ch + 1) * ta, :] = (acc / l).T
            lse_ref[ch] = m + jnp.log2(l)

    tq = nc * ta
    return pl.pallas_call(
        body, name=name, grid=(H, n_chunks // nc),
        in_specs=[pl.BlockSpec((None, tq, HEAD_DIM), lambda h, g: (0, g, h)),
                  pl.BlockSpec((None, T, HEAD_DIM), lambda h, g: (1, 0, h)),
                  pl.BlockSpec((None, n_chunks, HEAD_DIM, ta), lambda h, g: (h, 0, 0, 0)),
                  pl.BlockSpec((None, T, LANES), lambda h, g: (h, 0, 0))],
        out_specs=[pl.BlockSpec((tq, HEAD_DIM), lambda h, g: (g, h)),
                   pl.BlockSpec((None, nc, 1, ta), lambda h, g: (h, g, 0, 0))],
        out_shape=[jax.ShapeDtypeStruct((T, W), F32), jax.ShapeDtypeStruct((H, n_chunks, 1, ta), F32)],
        compiler_params=_params(("parallel", "arbitrary")),
    )(z7, z7, vt, c_rep)


def _attn_bwd_fused(z7, kt, dob, c_rep, lse_chunks, d_chunks, name):
    _, T, W = z7.shape
    H = W // HEAD_DIM
    ta, nc, n_chunks = _attn_geometry(T)
    n_steps = n_chunks // nc
    scale = np.float32(1.0 / np.sqrt(HEAD_DIM))
    reps = ta // LANES

    def body(k_ref, v_ref, kt_ref, q_ref, do_ref, c_ref, lse_ref, d_ref,
             dk_ref, dv_ref, dck_ref, dq_ref, dcq_ref, dk_scr, dv_scr, dck_scr, dqt_scr, dcq_scr):
        g = pl.program_id(1)

        @pl.when(g == 0)
        def _():
            dqt_scr[...] = jnp.zeros_like(dqt_scr)
            dcq_scr[...] = jnp.zeros_like(dcq_scr)

        dk_scr[...] = jnp.zeros_like(dk_scr)
        dv_scr[...] = jnp.zeros_like(dv_scr)
        dck_scr[...] = jnp.zeros_like(dck_scr)

        def products(ch, q, do):
            rows = slice(ch * ta, (ch + 1) * ta)
            return _mm_nt(k_ref[rows, :], q), _mm_nt(v_ref[rows, :], do)

        def update(ch, i, q, do, prods, diagonal):
            rows = slice(ch * ta, (ch + 1) * ta)
            cj = c_ref[rows, :]
            st = prods[0] * (scale * LOG2E) - jnp.concatenate([cj] * reps, axis=1) - lse_ref[i]
            if diagonal:
                st = jnp.where(_causal_tile(ta, keys_on_rows=True), st, NEG_BIG)
            pt = jnp.exp2(st)
            dv_scr[ch] += _mm(pt.astype(BF16), do)
            dst = pt * (prods[1] - d_ref[i])
            dst_b = dst.astype(BF16)
            dk_scr[ch] += _mm(dst_b, q)
            dqt_scr[i] += _mm(kt_ref[ch], dst_b)
            dcq_scr[i] += jnp.sum(dst, axis=0, keepdims=True)
            lane_sum = dst[:, :LANES]
            for r in range(1, reps):
                lane_sum = lane_sum + dst[:, r * LANES:(r + 1) * LANES]
            dck_scr[ch] += lane_sum

        for ii in range(nc):
            i = nc * g + ii
            q = _chunk(q_ref, i, ta)
            do = _chunk(do_ref, i, ta)
            prods = [products(ch, q, do) for ch in range(0, ii + 1)]
            for ch in range(0, ii + 1):
                update(ch, i, q, do, prods[ch], ch == ii)

        def full_chunk(i, carry):
            q = _chunk(q_ref, i, ta)
            do = _chunk(do_ref, i, ta)
            prods = [products(ch, q, do) for ch in range(nc)]
            for ch in range(nc):
                update(ch, i, q, do, prods[ch], False)
            return carry

        lax.fori_loop(nc * (g + 1), n_chunks, full_chunk, 0)
        for ch in range(nc):
            rows = slice(ch * ta, (ch + 1) * ta)
            dk_ref[rows, :] = (dk_scr[ch] * scale).astype(BF16)
            dv_ref[rows, :] = dv_scr[ch].astype(BF16)
            ones = jnp.ones((8, LANES), F32)
            sums = lax.dot_general(ones, dck_scr[ch], (((1,), (1,)), ((), ())), preferred_element_type=F32,
                                   precision=lax.Precision.HIGHEST)
            dck_ref[ch] = -sums[0:1, :]

        @pl.when(g == n_steps - 1)
        def _():
            for i in range(n_chunks):
                dq_ref[i * ta:(i + 1) * ta, :] = (dqt_scr[i] * scale).T.astype(BF16)
            dcq_ref[...] = dcq_scr[...]

    tk = nc * ta
    chunks = pl.BlockSpec((None, n_chunks, 1, ta), lambda h, g: (h, 0, 0, 0))
    tile = pl.BlockSpec((tk, HEAD_DIM), lambda h, g: (g, h))
    return pl.pallas_call(
        body, name=name, grid=(H, n_steps),
        in_specs=[pl.BlockSpec((None, tk, HEAD_DIM), lambda h, g: (1, g, h)),
                  pl.BlockSpec((None, tk, HEAD_DIM), lambda h, g: (2, g, h)),
                  pl.BlockSpec((None, nc, HEAD_DIM, ta), lambda h, g: (h, g, 0, 0)),
                  pl.BlockSpec((None, T, HEAD_DIM), lambda h, g: (0, 0, h)),
                  pl.BlockSpec((T, HEAD_DIM), lambda h, g: (0, h)),
                  pl.BlockSpec((None, tk, LANES), lambda h, g: (h, g, 0)),
                  chunks, chunks],
        out_specs=[tile, tile, pl.BlockSpec((None, nc, 1, ta), lambda h, g: (h, g, 0, 0)),
                   pl.BlockSpec((T, HEAD_DIM), lambda h, g: (0, h)), chunks],
        out_shape=[jax.ShapeDtypeStruct((T, W), BF16), jax.ShapeDtypeStruct((T, W), BF16),
                   jax.ShapeDtypeStruct((H, n_chunks, 1, ta), F32), jax.ShapeDtypeStruct((T, W), BF16),
                   jax.ShapeDtypeStruct((H, n_chunks, 1, ta), F32)],
        scratch_shapes=[pltpu.VMEM((nc, ta, HEAD_DIM), F32), pltpu.VMEM((nc, ta, HEAD_DIM), F32),
                        pltpu.VMEM((nc, ta, LANES), F32), pltpu.VMEM((n_chunks, HEAD_DIM, ta), F32),
                        pltpu.VMEM((n_chunks, 1, ta), F32)],
        compiler_params=_params(("parallel", "arbitrary")),
    )(z7, z7, kt, z7, dob, c_rep, lse_chunks, d_chunks)


def _chunk_causal_mask():
    rows = lax.broadcasted_iota(jnp.int32, (SGU_LEN, SGU_LEN), 0)
    cols = lax.broadcasted_iota(jnp.int32, (SGU_LEN, SGU_LEN), 1)
    return (cols // CHUNK) <= (rows // CHUNK)


def _sgu_norm_mix(sv, lng_ref, lnb_ref, ws_ref, bs_ref, vn_scr, mixed_scr, vhat_scr=None):
    tm = sv.shape[0]
    vs = _gelu(sv)
    mask = _chunk_causal_mask()
    rstds = []
    for g in range(N_GROUPS):
        lanes = slice(g * GROUP_DIM, (g + 1) * GROUP_DIM)
        blk = vs[:, lanes]
        cen = blk - jnp.mean(blk, axis=-1, keepdims=True)
        rstd = lax.rsqrt(jnp.mean(cen * cen, axis=-1, keepdims=True) + LN_EPS)
        vhat = cen * rstd
        rstds.append(rstd)
        if vhat_scr is not None:
            vhat_scr[:, lanes] = vhat
        vn_scr[:, lanes] = (vhat * lng_ref[:, lanes] + lnb_ref[:, lanes]).astype(BF16)
        wm = jnp.where(mask, ws_ref[g], 0.0).astype(BF16)
        for w in range(tm // SGU_LEN):
            rows = slice(w * SGU_LEN, (w + 1) * SGU_LEN)
            mixed_scr[rows, lanes] = _mm(wm, vn_scr[rows, lanes]) + bs_ref[g]
    return rstds


def _mix_out_fwd(z7, o_a, x1, lng, lnb, ws, bs, w_out, g_post, name):
    _, T, W = z7.shape
    D = x1.shape[1]
    tm = _blk(T, 256)

    def body(u_ref, sv_ref, ga_ref, gb_ref, oa_ref, x1_ref, lng_ref, lnb_ref, ws_ref, bs_ref, wo_ref, gp_ref,
             x2_ref, p_ref, mb_ref, vn_scr, mixed_scr):
        _sgu_norm_mix(sv_ref[...].astype(F32), lng_ref, lnb_ref, ws_ref, bs_ref, vn_scr, mixed_scr)
        o_b = _gelu(u_ref[...].astype(F32)) * mixed_scr[...]
        merged = (jax.nn.sigmoid(ga_ref[...].astype(F32)) * oa_ref[...]
                  + jax.nn.sigmoid(gb_ref[...].astype(F32)) * o_b).astype(BF16)
        mb_ref[...] = merged
        p = _mm(merged, wo_ref[...])
        p_ref[...] = p
        x2_ref[...] = x1_ref[...] + p * _rms_scale(p) * gp_ref[...]

    def seg(idx):
        return pl.BlockSpec((None, tm, W), lambda i, idx=idx: (idx, i, 0))

    row = pl.BlockSpec((tm, D), lambda i: (i, 0))
    vec = pl.BlockSpec((1, D), lambda i: (0, 0))
    return pl.pallas_call(
        body, name=name, grid=(T // tm,),
        in_specs=[seg(3), seg(4), seg(5), seg(6), row, row, vec, vec,
                  pl.BlockSpec((N_GROUPS, SGU_LEN, SGU_LEN), lambda i: (0, 0, 0)),
                  pl.BlockSpec((N_GROUPS, SGU_LEN, 1), lambda i: (0, 0, 0)),
                  pl.BlockSpec((D, D), lambda i: (0, 0)), vec],
        out_specs=[row, row, row],
        out_shape=[jax.ShapeDtypeStruct((T, D), F32), jax.ShapeDtypeStruct((T, D), F32),
                   jax.ShapeDtypeStruct((T, D), BF16)],
        scratch_shapes=[pltpu.VMEM((tm, W), BF16), pltpu.VMEM((tm, W), F32)],
        compiler_params=_params(("parallel",)),
    )(z7, z7, z7, z7, o_a, x1, lng, lnb, ws, bs, w_out, g_post)


def _mix_out_bwd(dx2, p, z7, o_a, lng, lnb, ws, bs, w_out, g_post, name, dep=None):
    _, T, W = z7.shape
    D = dx2.shape[1]
    tm = _blk(T, 256)
    n_w = tm // SGU_LEN

    def body(dx2_ref, p_ref, u_ref, sv_ref, ga_ref, gb_ref, oa_ref, lng_ref, lnb_ref, ws_ref, bs_ref, wo_ref, gp_ref, _,
             dpb_ref, dob_ref, dvec_ref, dz_ref, dgp_ref, dlng_ref, dlnb_ref, dws_ref, dbs_ref,
             vn_scr, mixed_scr, vhat_scr, dmix_scr, dvn_scr):
        @pl.when(pl.program_id(0) == 0)
        def _():
            dgp_ref[...] = jnp.zeros_like(dgp_ref)
            dlng_ref[...] = jnp.zeros_like(dlng_ref)
            dlnb_ref[...] = jnp.zeros_like(dlnb_ref)
            dws_ref[...] = jnp.zeros_like(dws_ref)
            dbs_ref[...] = jnp.zeros_like(dbs_ref)

        pv = p_ref[...]
        s = _rms_scale(pv)
        n = pv * s
        dn = dx2_ref[...]
        dgp_ref[...] += jnp.sum(dn * n, axis=0, keepdims=True)
        dpb = _rms_bwd(dn, n, s, gp_ref[...]).astype(BF16)
        dpb_ref[...] = dpb
        dmerged = _mm_nt(dpb, wo_ref[...])

        sv = sv_ref[...].astype(F32)
        rstds = _sgu_norm_mix(sv, lng_ref, lnb_ref, ws_ref, bs_ref, vn_scr, mixed_scr, vhat_scr)
        u_pre = u_ref[...].astype(F32)
        u = _gelu(u_pre)
        mixed = mixed_scr[...]
        sa = jax.nn.sigmoid(ga_ref[...].astype(F32))
        sb = jax.nn.sigmoid(gb_ref[...].astype(F32))
        oa = oa_ref[...]
        do_a = (dmerged * sa).astype(BF16)
        dob_ref[...] = do_a
        prod = do_a.astype(F32) * oa
        for h in range(N_HEADS):
            sums = lax.dot_general(jnp.ones((8, LANES), F32), prod[:, h * HEAD_DIM:(h + 1) * HEAD_DIM],
                                   (((1,), (1,)), ((), ())), preferred_element_type=F32,
                                   precision=lax.Precision.HIGHEST)
            dvec_ref[h, 0] = sums[0:1, :]
        dz_ref[2] = (dmerged * oa * (sa * (1.0 - sa))).astype(BF16)
        dz_ref[3] = (dmerged * (u * mixed) * (sb * (1.0 - sb))).astype(BF16)
        do_b = dmerged * sb
        dz_ref[0] = (do_b * mixed * _gelu_grad(u_pre)).astype(BF16)
        dmix_scr[...] = do_b * u

        mask = _chunk_causal_mask()
        for g in range(N_GROUPS):
            lanes = slice(g * GROUP_DIM, (g + 1) * GROUP_DIM)
            wm = jnp.where(mask, ws_ref[g], 0.0).astype(BF16)
            dws = jnp.zeros((SGU_LEN, SGU_LEN), F32)
            dbs = jnp.zeros((SGU_LEN, 1), F32)
            for w in range(n_w):
                rows = slice(w * SGU_LEN, (w + 1) * SGU_LEN)
                dmix = dmix_scr[rows, lanes]
                dmix_b = dmix.astype(BF16)
                dvn_scr[rows, lanes] = _mm_tn(wm, dmix_b)
                dws = dws + _mm_nt(dmix_b, vn_scr[rows, lanes])
                dbs = dbs + jnp.sum(dmix, axis=-1, keepdims=True)
            dws_ref[g] += jnp.where(mask, dws, 0.0)
            dbs_ref[g] += dbs
            dvn = dvn_scr[:, lanes]
            vhat = vhat_scr[:, lanes]
            dlng_ref[:, lanes] += jnp.sum(dvn * vhat, axis=0, keepdims=True)
            dlnb_ref[:, lanes] += jnp.sum(dvn, axis=0, keepdims=True)
            dvh = dvn * lng_ref[:, lanes]
            dvs = rstds[g] * (dvh - jnp.mean(dvh, axis=-1, keepdims=True)
                              - vhat * jnp.mean(dvh * vhat, axis=-1, keepdims=True))
            dvn_scr[:, lanes] = dvs
        dz_ref[1] = (dvn_scr[...] * _gelu_grad(sv)).astype(BF16)

    def seg(idx):
        return pl.BlockSpec((None, tm, W), lambda i, idx=idx: (idx, i, 0))

    row = pl.BlockSpec((tm, D), lambda i: (i, 0))
    vec = pl.BlockSpec((1, D), lambda i: (0, 0))
    ws_spec = pl.BlockSpec((N_GROUPS, SGU_LEN, SGU_LEN), lambda i: (0, 0, 0))
    bs_spec = pl.BlockSpec((N_GROUPS, SGU_LEN, 1), lambda i: (0, 0, 0))
    return pl.pallas_call(
        body, name=name, grid=(T // tm,),
        in_specs=[row, row, seg(3), seg(4), seg(5), seg(6), row, vec, vec, ws_spec, bs_spec,
                  pl.BlockSpec((D, D), lambda i: (0, 0)), vec, ANY],
        out_specs=[row, row, pl.BlockSpec((N_HEADS, 1, 1, tm), lambda i: (0, i, 0, 0)),
                   pl.BlockSpec((4, tm, W), lambda i: (0, i, 0)), vec, vec, vec, ws_spec, bs_spec],
        out_shape=[jax.ShapeDtypeStruct((T, D), BF16), jax.ShapeDtypeStruct((T, W), BF16),
                   jax.ShapeDtypeStruct((N_HEADS, T // tm, 1, tm), F32), jax.ShapeDtypeStruct((4, T, W), BF16),
                   jax.ShapeDtypeStruct((1, D), F32), jax.ShapeDtypeStruct((1, D), F32),
                   jax.ShapeDtypeStruct((1, D), F32),
                   jax.ShapeDtypeStruct((N_GROUPS, SGU_LEN, SGU_LEN), F32),
                   jax.ShapeDtypeStruct((N_GROUPS, SGU_LEN, 1), F32)],
        scratch_shapes=[pltpu.VMEM((tm, W), BF16), pltpu.VMEM((tm, W), F32), pltpu.VMEM((tm, W), F32),
                        pltpu.VMEM((tm, W), F32), pltpu.VMEM((tm, W), F32)],
        compiler_params=_params(("arbitrary",)),
    )(dx2, p, z7, z7, z7, z7, o_a, lng, lnb, ws, bs, w_out, g_post, _after(dep))


def _loss_head(y, target, name):
    T, D = y.shape
    tm = _blk(T, 1024)
    n_i = T // tm

    def body(y_ref, t_ref, dy_ref, loss_ref, acc_scr):
        i = pl.program_id(0)

        @pl.when(i == 0)
        def _():
            acc_scr[...] = jnp.zeros_like(acc_scr)

        e = y_ref[...] - t_ref[...]
        dy_ref[...] = e * np.float32(1.0 / D)
        acc_scr[...] += jnp.sum(e * e, axis=0, keepdims=True)

        @pl.when(i == n_i - 1)
        def _():
            total = jnp.sum(acc_scr[...], axis=-1, keepdims=True) * np.float32(0.5 / D)
            loss_ref[...] = jnp.broadcast_to(total, loss_ref.shape)

    row = pl.BlockSpec((tm, D), lambda i: (i, 0))
    return pl.pallas_call(
        body, name=name, grid=(n_i,),
        in_specs=[row, row],
        out_specs=[row, pl.BlockSpec((1, LANES), lambda i: (0, 0))],
        out_shape=[jax.ShapeDtypeStruct((T, D), F32), jax.ShapeDtypeStruct((1, LANES), F32)],
        scratch_shapes=[pltpu.VMEM((1, D), F32)],
        compiler_params=_params(("arbitrary",)),
    )(y, target)


def _adamw_math(w, g, m, v):
    m_new = ADAM_B1 * m + (1.0 - ADAM_B1) * g
    v_new = ADAM_B2 * v + (1.0 - ADAM_B2) * (g * g)
    m_hat = m_new / np.float32(1.0 - ADAM_B1 ** ADAM_STEP)
    v_hat = v_new / np.float32(1.0 - ADAM_B2 ** ADAM_STEP)
    delta = -ADAM_LR * (m_hat / (jnp.sqrt(v_hat) + ADAM_EPS) + ADAM_WD * w)
    return delta, m_new, v_new


def _sum_adamw(parts, w, m, v, name, dep=None):
    n, R, C = parts.shape
    tr = _blk(R, 128)

    def body(p_ref, w_ref, m_ref, v_ref, _, g_ref, d_ref, mo_ref, vo_ref):
        g = p_ref[0].astype(F32)
        for s in range(1, n):
            g = g + p_ref[s].astype(F32)
        delta, m_new, v_new = _adamw_math(w_ref[...], g, m_ref[...], v_ref[...])
        g_ref[...] = g
        d_ref[...] = delta
        mo_ref[...] = m_new
        vo_ref[...] = v_new

    row = pl.BlockSpec((tr, C), lambda i: (i, 0))
    shp = jax.ShapeDtypeStruct((R, C), F32)
    return pl.pallas_call(
        body, name=name, grid=(R // tr,),
        in_specs=[pl.BlockSpec((n, tr, C), lambda i: (0, i, 0)), row, row, row, ANY],
        out_specs=[row, row, row, row], out_shape=[shp, shp, shp, shp],
        compiler_params=_params(("parallel",)),
    )(parts, w, m, v, _after(dep))


def _position():
    return lax.axis_index("x"), lax.axis_index("y"), lax.axis_index("c")


def _slot(px, py, pc):
    return 4 * px + 2 * py + pc


def _all_gather(shards, name):
    n = len(shards)

    def body(*refs):
        ins, outs = refs[:n], refs[n:2 * n]
        send_sems, recv_sems, local_sems = refs[2 * n:]
        x, y, c = _position()
        me, sibling = (x, y, c), (x, y, 1 - c)
        chips = [(1 - x, y), (x, 1 - y), (1 - x, 1 - y)]

        def copy(a, k, block, to, src=None):
            dst = outs[a].at[_slot(*block)]
            return pltpu.make_async_remote_copy(
                src_ref=dst if src is None else src, dst_ref=dst,
                send_sem=send_sems.at[a, k], recv_sem=recv_sems.at[a, k],
                device_id=to, device_id_type=MESH)

        mine = [pltpu.make_async_copy(ins[a], outs[a].at[_slot(*me)], local_sems.at[a]) for a in range(n)]
        for cp in mine:
            cp.start()
        first = []
        for a in range(n):
            first.append(copy(a, 0, me, sibling, src=ins[a]))
            first += [copy(a, 1 + j, me, (*chip, c), src=ins[a]) for j, chip in enumerate(chips)]
        for cp in first:
            cp.start()
        passed = []
        for j, chip in enumerate(chips):
            for a in range(n):
                copy(a, 1 + j, (*chip, c), me).wait_recv()
                fwd = copy(a, 4 + j, (*chip, c), sibling)
                fwd.start()
                passed.append(fwd)
        for a in range(n):
            copy(a, 0, sibling, me).wait_recv()
            for j, chip in enumerate(chips):
                copy(a, 4 + j, (*chip, 1 - c), me).wait_recv()
        for cp in first + passed:
            cp.wait_send()
        for cp in mine:
            cp.wait()

    return pl.pallas_call(
        body, name=name,
        in_specs=[ANY] * n, out_specs=[ANY] * n,
        out_shape=[jax.ShapeDtypeStruct((N_DEV,) + s.shape, s.dtype) for s in shards],
        scratch_shapes=[pltpu.SemaphoreType.DMA((n, 7)), pltpu.SemaphoreType.DMA((n, 7)),
                        pltpu.SemaphoreType.DMA((n,))],
    )(*shards)


def _peer(x, y, c, k):
    return (1 - x if k & 4 else x, 1 - y if k & 2 else y, 1 - c if k & 1 else c)


def _remote_copies(src_refs, land_refs, send_sems, recv_sems, gather, outgoing):
    x, y, c = _position()
    me = _slot(x, y, c)
    copies = []
    for k in range(1, N_DEV):
        peer = _peer(x, y, c, k)
        for a in range(len(src_refs)):
            src = src_refs[a] if gather else src_refs[a].at[_slot(*peer)]
            dst = land_refs[a].at[me if outgoing else _slot(*peer)]
            sem = a * (N_DEV - 1) + k - 1
            copies.append(pltpu.make_async_remote_copy(
                src_ref=src, dst_ref=dst, send_sem=send_sems.at[sem], recv_sem=recv_sems.at[sem],
                device_id=peer, device_id_type=MESH))
    return copies


def _sequencer_exchange(srcs, name, gather, collective_id):
    n = len(srcs)
    hbm = pltpu.MemorySpace.HBM
    src_refs = [jax.new_ref(s, memory_space=hbm) for s in srcs]
    land_refs = [jax.empty_ref(jax.ShapeDtypeStruct(((N_DEV,) + s.shape) if gather else s.shape, s.dtype),
                               memory_space=hbm) for s in srcs]
    n_sems = n * (N_DEV - 1)
    block_bytes = sum(s.size * s.dtype.itemsize // (1 if gather else N_DEV) for s in srcs)
    cost = pl.CostEstimate(flops=0, transcendentals=0, bytes_accessed=2 * N_DEV * block_bytes,
                           remote_bytes_transferred=(N_DEV - 1) * block_bytes)

    @pl.kernel(mesh=plsc.ScalarSubcoreMesh(axis_name="sequencer", num_cores=1), name=name,
               scratch_types=(pltpu.SemaphoreType.DMA((n_sems,)), pltpu.SemaphoreType.DMA((n_sems,)),
                              pltpu.SemaphoreType.DMA((n,))),
               cost_estimate=cost,
               compiler_params=pltpu.CompilerParams(collective_id=collective_id))
    def launch(send_sems, recv_sems, local_sems):
        x, y, c = _position()
        me = _slot(x, y, c)
        barrier = pltpu.get_barrier_semaphore()
        for k in range(1, N_DEV):
            pl.semaphore_signal(barrier, inc=1, device_id=_peer(x, y, c, k), device_id_type=MESH)
        pl.semaphore_wait(barrier, N_DEV - 1)
        mine = [pltpu.make_async_copy(src_refs[a] if gather else src_refs[a].at[me], land_refs[a].at[me],
                                      local_sems.at[a]) for a in range(n)]
        for cp in mine:
            cp.start()
        sends = _remote_copies(src_refs, land_refs, send_sems, recv_sems, gather, outgoing=True)
        for cp in sends:
            cp.start()
        for cp in _remote_copies(src_refs, land_refs, send_sems, recv_sems, gather, outgoing=False):
            cp.wait_recv()
        for cp in sends:
            cp.wait_send()
        for cp in mine:
            cp.wait()

    launch()
    return [r[...] for r in land_refs]


SMALL_VECS = ("ffn1_pre_g", "ffn1_post_g", "mix_pre_g", "sgu_ln_g", "sgu_ln_b", "mix_post_g", "ffn2_pre_g",
              "ffn2_post_g")
ROW_BS = len(SMALL_VECS)
ROW_BF = ROW_BS + 1
ROW_LOSS = ROW_BF + 1
ROW_WS = 16
BLOB_ROWS = ROW_WS + SGU_LEN


def _pack_small(vals, D, loss_row=None):
    rows = [vals[n].reshape(1, D) for n in SMALL_VECS]
    rows.append(vals["sgu_b_s"].reshape(1, D))
    rows.append(jnp.pad(vals["b_forget"].reshape(1, N_HEADS), ((0, 0), (0, D - N_HEADS))))
    rows.append(jnp.zeros((1, D), F32) if loss_row is None else loss_row)
    rows.append(jnp.zeros((ROW_WS - ROW_LOSS - 1, D), F32))
    rows.append(vals["sgu_w_s"].reshape(SGU_LEN, D))
    return jnp.concatenate(rows, axis=0)


def _unpack_small(blob, D):
    out = {n: blob[r:r + 1] for r, n in enumerate(SMALL_VECS)}
    out["sgu_b_s"] = blob[ROW_BS].reshape(1, N_GROUPS, SGU_LEN)
    out["b_forget"] = blob[ROW_BF, :N_HEADS].reshape(1, N_HEADS)
    out["sgu_w_s"] = blob[ROW_WS:].reshape(1, N_GROUPS, SGU_LEN, SGU_LEN)
    return out


WEIGHT_NAMES = ("ffn1_pre_g", "ffn1_w_gate", "ffn1_w_up", "ffn1_w_down", "ffn1_post_g", "mix_pre_g", "w_in",
                "b_forget", "sgu_ln_g", "sgu_ln_b", "sgu_w_s", "sgu_b_s", "w_out", "mix_post_g", "ffn2_pre_g",
                "ffn2_w_gate", "ffn2_w_up", "ffn2_w_down", "ffn2_post_g")
BIG_NAMES = ("ffn1_w_gate", "ffn1_w_up", "ffn1_w_down", "w_in", "w_out", "ffn2_w_gate", "ffn2_w_up", "ffn2_w_down")
WEIGHT_GROUPS = {"ffn1": ("ffn1_w_gate", "ffn1_w_up", "ffn1_w_down"), "mix": ("w_in", "w_out"),
                 "ffn2": ("ffn2_w_gate", "ffn2_w_up", "ffn2_w_down")}
GRAD_GROUPS = (("ffn2_w_gate", "ffn2_w_up", "ffn2_w_down"), ("w_in", "w_out"), ("ffn1_w_down",), ("ffn1_w_gate",),
               ("ffn1_w_up",))


def _local_step(x, target, small, fetch, emit, consume):
    T, D = x.shape
    W = N_HEADS * HEAD_DIM
    vec = lambda n: small[n].reshape(1, D)
    big = dict(fetch("ffn1", x))

    x1, y1, dgf1, silu1, act1 = _ffn_fwd(x, vec("ffn1_pre_g"), big["ffn1_w_gate"], big["ffn1_w_up"], big["ffn1_w_down"],
                                  vec("ffn1_post_g"), "ffn1_fwd")

    big.update(fetch("mix", x1))
    w_in_all = big["w_in"]
    in_width = N_DEV * w_in_all.shape[2]
    w_in = w_in_all.transpose(1, 0, 2).reshape(D, in_width)
    col_f = 3 * W
    col_u = col_f + N_HEADS
    seg_starts = (0, W, 2 * W, col_u, col_u + W, col_u + 2 * W, col_u + 3 * W)
    w7 = jnp.stack([w_in[:, s:s + W] for s in seg_starts])
    wf = jnp.pad(w_in[:, col_f:col_u], ((0, 0), (0, LANES - N_HEADS)))
    w_out = big["w_out"].reshape(D, D)
    b_pad = jnp.pad(small["b_forget"].reshape(1, N_HEADS), ((0, 0), (0, LANES - N_HEADS)))
    lng, lnb = vec("sgu_ln_g"), vec("sgu_ln_b")
    ws = small["sgu_w_s"].reshape(N_GROUPS, SGU_LEN, SGU_LEN)
    bs = small["sgu_b_s"].reshape(N_GROUPS, SGU_LEN, 1)

    z7, f_logit, h2b = _mix_in_fwd(x1, vec("mix_pre_g"), w7, wf, "mix_in_fwd")
    c = _forget_cumsum(f_logit, b_pad, "forget_cumsum")
    c_heads = c[:, :N_HEADS].T
    ta, _, n_chunks = _attn_geometry(T)
    vt = z7[2].reshape(n_chunks, ta, N_HEADS, HEAD_DIM).transpose(2, 0, 3, 1)
    c_rep = jnp.broadcast_to(c_heads[:, :, None] * LOG2E, (N_HEADS, T, LANES))
    o_a, lse_chunks = _attn_fwd_keys_on_rows(z7, vt, c_rep, "attn_fwd")
    x2, p, merged_b = _mix_out_fwd(z7, o_a, x1, lng, lnb, ws, bs, w_out, vec("mix_post_g"), "mix_out_fwd")
    big.update(fetch("ffn2", x2))
    x3, y2, dgf2, silu2, act2 = _ffn_fwd(x2, vec("ffn2_pre_g"), big["ffn2_w_gate"], big["ffn2_w_up"], big["ffn2_w_down"],
                                  vec("ffn2_post_g"), "ffn2_fwd")
    dy, loss_lanes = _loss_head(x3, target, "loss_head")

    grads_small = {}

    dx2, h3b, dy2b, dgate2, dup2, dgpre, dgpost = _ffn_bwd(
        dy, x2, y2, dgf2, silu2, vec("ffn2_pre_g"), big["ffn2_w_gate"], big["ffn2_w_up"], big["ffn2_w_down"],
        vec("ffn2_post_g"), "ffn2_bwd")
    grads_small["ffn2_pre_g"] = jnp.sum(dgpre, axis=0)
    grads_small["ffn2_post_g"] = jnp.sum(dgpost, axis=0)
    dep = emit("ffn2_w_gate", _wgrad(h3b, dgate2, "ffn2_wgrad_gate", shard_cols=True))
    dep = emit("ffn2_w_up", _wgrad(h3b, dup2, "ffn2_wgrad_up", shard_cols=True, dep=dep))
    dep = emit("ffn2_w_down", _wgrad(act2, dy2b, "ffn2_wgrad_down", dep=dep).reshape(big["ffn2_w_down"].shape))

    dpb, dob, dvec, dz4, dgp, dlng, dlnb, dws, dbs = _mix_out_bwd(
        dx2, p, z7, o_a, lng, lnb, ws, bs, w_out, vec("mix_post_g"), "mix_out_bwd", dep=dep)
    grads_small["mix_post_g"] = dgp
    grads_small["sgu_ln_g"] = dlng
    grads_small["sgu_ln_b"] = dlnb
    grads_small["sgu_w_s"] = dws
    grads_small["sgu_b_s"] = dbs
    d_chunks = dvec.reshape(N_HEADS, n_chunks, 1, ta)
    kt = z7[1].reshape(n_chunks, ta, N_HEADS, HEAD_DIM).transpose(2, 0, 3, 1)
    dk, dv, dc, dq, dc_q = _attn_bwd_fused(z7, kt, dob, c_rep, lse_chunks, d_chunks, "attn_bwd")
    dc_pad = jnp.pad((dc + dc_q).reshape(N_HEADS, T).T, ((0, 0), (0, LANES - N_HEADS)))
    dfb, dbf = _forget_bwd(dc_pad, f_logit, b_pad, "forget_bwd")
    grads_small["b_forget"] = dbf[:, :N_HEADS]
    segs = [(dq, None), (dk, None), (dv, None), (dz4, 0), (dz4, 1), (dz4, 2), (dz4, 3)]
    dep = consume(("ffn2_w_gate", "ffn2_w_up", "ffn2_w_down"))
    dx1, dgm = _mix_in_bwd(dx2, x1, vec("mix_pre_g"), segs, dfb, w7, wf, "mix_in_bwd", dep=dep)
    grads_small["mix_pre_g"] = jnp.sum(dgm, axis=0)
    dw_seg, dep = [], dx1
    for q, (sm, idx) in enumerate(segs):
        dw_seg.append(_wgrad(h2b, sm, "w_in_wgrad_%d" % q, dep=dep, y_index=idx))
        dep = dw_seg[-1]
    dwf = _wgrad(h2b, dfb, "w_in_wgrad_f", dep=dep)
    dw_in = jnp.concatenate(dw_seg[:3] + [dwf[:, :N_HEADS]] + dw_seg[3:], axis=1)
    emit("w_in", dw_in.reshape(D, N_DEV, in_width // N_DEV).transpose(1, 0, 2))
    dep = emit("w_out", _wgrad(merged_b, dpb, "w_out_wgrad", dep=dwf).reshape(big["w_out"].shape))

    dx0, h1b, dy1b, dgate1, dup1, dgpre1, dgpost1 = _ffn_bwd(
        dx1, x, y1, dgf1, silu1, vec("ffn1_pre_g"), big["ffn1_w_gate"], big["ffn1_w_up"], big["ffn1_w_down"],
        vec("ffn1_post_g"), "ffn1_bwd", dep=dep)
    grads_small["ffn1_pre_g"] = jnp.sum(dgpre1, axis=0)
    grads_small["ffn1_post_g"] = jnp.sum(dgpost1, axis=0)
    dep = consume(("w_in", "w_out"))
    dep = emit("ffn1_w_down", _wgrad(act1, dy1b, "ffn1_wgrad_down", dep=dep).reshape(big["ffn1_w_down"].shape))
    dep = emit("ffn1_w_gate", _wgrad(h1b, dgate1, "ffn1_wgrad_gate", shard_cols=True, dep=dep))
    dep = emit("ffn1_w_up", _wgrad(h1b, dup1, "ffn1_wgrad_up", shard_cols=True, dep=dep))

    loss_row = jnp.pad(loss_lanes, ((0, 0), (0, D - LANES)))
    return loss_row, dx0, grads_small


def kernel(x, ffn1_pre_g, ffn1_w_gate, ffn1_w_up, ffn1_w_down, ffn1_post_g, mix_pre_g, w_in, b_forget, sgu_ln_g, sgu_ln_b, sgu_w_s, sgu_b_s, w_out, mix_post_g, ffn2_pre_g, ffn2_w_gate, ffn2_w_up, ffn2_w_down, ffn2_post_g, loss_target, m_ffn1_pre_g, m_ffn1_w_gate, m_ffn1_w_up, m_ffn1_w_down, m_ffn1_post_g, m_mix_pre_g, m_w_in, m_b_forget, m_sgu_ln_g, m_sgu_ln_b, m_sgu_w_s, m_sgu_b_s, m_w_out, m_mix_post_g, m_ffn2_pre_g, m_ffn2_w_gate, m_ffn2_w_up, m_ffn2_w_down, m_ffn2_post_g, v_ffn1_pre_g, v_ffn1_w_gate, v_ffn1_w_up, v_ffn1_w_down, v_ffn1_post_g, v_mix_pre_g, v_w_in, v_b_forget, v_sgu_ln_g, v_sgu_ln_b, v_sgu_w_s, v_sgu_b_s, v_w_out, v_mix_post_g, v_ffn2_pre_g, v_ffn2_w_gate, v_ffn2_w_up, v_ffn2_w_down, v_ffn2_post_g):
    weights = dict(zip(WEIGHT_NAMES, (ffn1_pre_g, ffn1_w_gate, ffn1_w_up, ffn1_w_down, ffn1_post_g, mix_pre_g, w_in,
                                      b_forget, sgu_ln_g, sgu_ln_b, sgu_w_s, sgu_b_s, w_out, mix_post_g, ffn2_pre_g,
                                      ffn2_w_gate, ffn2_w_up, ffn2_w_down, ffn2_post_g)))
    mom1 = dict(zip(WEIGHT_NAMES, (m_ffn1_pre_g, m_ffn1_w_gate, m_ffn1_w_up, m_ffn1_w_down, m_ffn1_post_g,
                                   m_mix_pre_g, m_w_in, m_b_forget, m_sgu_ln_g, m_sgu_ln_b, m_sgu_w_s, m_sgu_b_s,
                                   m_w_out, m_mix_post_g, m_ffn2_pre_g, m_ffn2_w_gate, m_ffn2_w_up, m_ffn2_w_down,
                                   m_ffn2_post_g)))
    mom2 = dict(zip(WEIGHT_NAMES, (v_ffn1_pre_g, v_ffn1_w_gate, v_ffn1_w_up, v_ffn1_w_down, v_ffn1_post_g,
                                   v_mix_pre_g, v_w_in, v_b_forget, v_sgu_ln_g, v_sgu_ln_b, v_sgu_w_s, v_sgu_b_s,
                                   v_w_out, v_mix_post_g, v_ffn2_pre_g, v_ffn2_w_gate, v_ffn2_w_up, v_ffn2_w_down,
                                   v_ffn2_post_g)))
    D = x.shape[-1]
    small_names = [n for n in WEIGHT_NAMES if n not in BIG_NAMES]

    small = {n: weights[n] for n in small_names}
    shard = lambda n: weights[n][0].astype(BF16)

    ffn1_full = _all_gather([shard(n) for n in WEIGHT_GROUPS["ffn1"]], "ffn1_all_gather")
    gathered = {}
    for cid, grp in ((1, "mix"), (2, "ffn2")):
        shards, _ = lax.optimization_barrier(([shard(n) for n in WEIGHT_GROUPS[grp]], ffn1_full[0]))
        gathered[grp] = _sequencer_exchange(shards, grp + "_gather", True, cid)

    def fetch(group, after):
        if group == "ffn1":
            return zip(WEIGHT_GROUPS[group], ffn1_full)
        arrived, _ = lax.optimization_barrier((gathered[group], after))
        return zip(WEIGHT_GROUPS[group], arrived)

    ready, received = {}, {}

    def emit(name, part):
        ready[name] = part
        for gi, group in enumerate(GRAD_GROUPS):
            if name == group[-1]:
                lands = _sequencer_exchange([ready[n] for n in group], name + "_grad_exchange", False, 3 + gi)
                received.update(zip(group, lands))
        return part

    out = {}

    def consume(names, dep=None):
        for n in names:
            g, d, m_new, v_new = _sum_adamw(received[n], weights[n][0], mom1[n][0], mom2[n][0], "adamw_" + n, dep=dep)
            out[n] = tuple(a[None] for a in (g, d, m_new, v_new))
            dep = g
        return dep

    loss_row, grad_x, grads_small = _local_step(x[0], loss_target[0], small, fetch, emit, consume)

    blobs = _sequencer_exchange([_pack_small(grads_small, D, loss_row)], "small_gather", True,
                                3 + len(GRAD_GROUPS))[0]
    blob, d_blob, m_blob, v_blob = _sum_adamw(
        blobs, _pack_small(small, D), _pack_small({n: mom1[n] for n in small_names}, D),
        _pack_small({n: mom2[n] for n in small_names}, D), "adamw_small")
    consume(("ffn1_w_down", "ffn1_w_gate", "ffn1_w_up"), dep=blob)
    unpacked = [_unpack_small(b, D) for b in (blob, d_blob, m_blob, v_blob)]
    for n in small_names:
        out[n] = tuple(u[n].reshape(weights[n].shape) for u in unpacked)

    loss = blob[ROW_LOSS, 0]
    result = [loss, grad_x[None]]
    for k in range(4):
        result += [out[n][k] for n in WEIGHT_NAMES]
    return tuple(result)
```

```python
import numpy as np
import jax
import jax.numpy as jnp
from jax import lax
from jax.experimental import pallas as pl
from jax.experimental.pallas import tpu as pltpu
from jax.experimental.pallas import tpu_sc as plsc

F32 = jnp.float32
BF16 = jnp.bfloat16

RMS_EPS = 1e-6
LN_EPS = 1e-5
HEAD_DIM = 128
N_HEADS = 8
GROUP_DIM = 128
N_GROUPS = 8
SGU_LEN = 128
CHUNK = 64
N_DEV = 8
LANES = 128
VMEM_LIMIT = 56 * 1024 * 1024
NEG_BIG = -1e30
LOG2E = np.float32(1.0 / np.log(2.0))

ADAM_LR = 0.001
ADAM_B1 = 0.9
ADAM_B2 = 0.999
ADAM_EPS = 1e-08
ADAM_WD = 0.01
ADAM_STEP = 10

MESH = pl.DeviceIdType.MESH
ANY = pl.BlockSpec(memory_space=pl.ANY)


def _blk(n, pref):
    return pref if (n >= pref and n % pref == 0) else n


def _mm(a, b):
    return jnp.dot(a, b, preferred_element_type=F32)


def _mm_nt(a, b):
    return lax.dot_general(a, b, (((1,), (1,)), ((), ())), preferred_element_type=F32)


def _mm_tn(a, b):
    return lax.dot_general(a, b, (((0,), (0,)), ((), ())), preferred_element_type=F32)


def _params(sem):
    return pltpu.CompilerParams(dimension_semantics=sem, vmem_limit_bytes=VMEM_LIMIT)


def _gelu(x):
    return 0.5 * x * (1.0 + lax.erf(x * np.float32(1.0 / np.sqrt(2.0))))


def _gelu_grad(x):
    cdf = 0.5 * (1.0 + lax.erf(x * np.float32(1.0 / np.sqrt(2.0))))
    return cdf + x * jnp.exp(-0.5 * x * x) * np.float32(1.0 / np.sqrt(2.0 * np.pi))


def _rms_scale(v):
    return lax.rsqrt(jnp.mean(v * v, axis=-1, keepdims=True) + RMS_EPS)


def _rms_bwd(dy, xhat, r, g):
    dxh = dy * g
    return r * (dxh - xhat * jnp.mean(dxh * xhat, axis=-1, keepdims=True))


def _ffn_rows(T):
    tm = _blk(T, 1024)
    th = _blk(tm, 512)
    return tm, th, tm // th


def _ffn_fwd(x, g_pre, wg, wu, wd, g_post, name):
    T, D = x.shape
    ns, _, fs = wg.shape
    tm, th, parts = _ffn_rows(T)

    def body(x_ref, gpre_ref, wg_ref, wu_ref, wd_ref, gpost_ref, xo_ref, y_ref, dgf_ref, silu_ref, act_ref,
             h_scr, acc_scr):
        j = pl.program_id(1)

        @pl.when(j == 0)
        def _():
            for r in range(parts):
                rows = slice(r * th, (r + 1) * th)
                xv = x_ref[rows, :]
                h_scr[rows, :] = (xv * _rms_scale(xv) * gpre_ref[...]).astype(BF16)
            acc_scr[...] = jnp.zeros_like(acc_scr)

        pre = []
        for r in range(parts):
            h = h_scr[r * th:(r + 1) * th, :]
            pre.append((_mm(h, wg_ref[...]), _mm(h, wu_ref[...])))
        for r in range(parts):
            rows = slice(r * th, (r + 1) * th)
            gg, uu = pre[r]
            sg = jax.nn.sigmoid(gg)
            silu = gg * sg
            act = (silu * uu).astype(BF16)
            dgf_ref[rows, :] = (uu * (sg * (1.0 + gg * (1.0 - sg)))).astype(BF16)
            silu_ref[rows, :] = silu.astype(BF16)
            act_ref[rows, :] = act
            acc_scr[rows, :] += _mm(act, wd_ref[...])

        @pl.when(j == ns - 1)
        def _():
            for r in range(parts):
                rows = slice(r * th, (r + 1) * th)
                y = acc_scr[rows, :]
                y_ref[rows, :] = y
                xo_ref[rows, :] = x_ref[rows, :] + 0.5 * (y * _rms_scale(y) * gpost_ref[...])

    row = pl.BlockSpec((tm, D), lambda i, j: (i, 0), pipeline_mode=pl.Buffered(1))
    vec = pl.BlockSpec((1, D), lambda i, j: (0, 0))
    return pl.pallas_call(
        body, name=name, grid=(T // tm, ns),
        in_specs=[pl.BlockSpec((tm, D), lambda i, j: (i, 0)), vec,
                  pl.BlockSpec((None, D, fs), lambda i, j: (j, 0, 0)),
                  pl.BlockSpec((None, D, fs), lambda i, j: (j, 0, 0)),
                  pl.BlockSpec((None, fs, D), lambda i, j: (j, 0, 0)),
                  vec],
        out_specs=[row, row] + [pl.BlockSpec((tm, fs), lambda i, j: (i, j))] * 3,
        out_shape=[jax.ShapeDtypeStruct((T, D), F32), jax.ShapeDtypeStruct((T, D), F32)]
        + [jax.ShapeDtypeStruct((T, ns * fs), BF16)] * 3,
        scratch_shapes=[pltpu.VMEM((tm, D), BF16), pltpu.VMEM((tm, D), F32)],
        compiler_params=_params(("parallel", "arbitrary")),
    )(x, g_pre, wg, wu, wd, g_post)


def _after(dep):
    return jnp.zeros((8, LANES), F32) if dep is None else dep


def _ffn_bwd(dxo, x, y, dgf, silu, g_pre, wg, wu, wd, g_post, name, dep=None):
    T, D = x.shape
    ns, _, fs = wg.shape
    tm, th, parts = _ffn_rows(T)
    n_i = T // tm

    def body(dxo_ref, x_ref, y_ref, dgf_ref, silu_ref, gpre_ref, wg_ref, wu_ref, wd_ref, gpost_ref, _,
             dx_ref, hb_ref, dyb_ref, dgb_ref, dub_ref, dgpre_ref, dgpost_ref, dy_scr, acc_scr):
        j = pl.program_id(1)

        @pl.when(j == 0)
        def _():
            dgpost = jnp.zeros((1, D), F32)
            for r in range(parts):
                rows = slice(r * th, (r + 1) * th)
                yv = y_ref[rows, :]
                s = _rms_scale(yv)
                n = yv * s
                dn = 0.5 * dxo_ref[rows, :]
                dgpost = dgpost + jnp.sum(dn * n, axis=0, keepdims=True)
                dyv = _rms_bwd(dn, n, s, gpost_ref[...]).astype(BF16)
                dy_scr[rows, :] = dyv
                dyb_ref[rows, :] = dyv
                xv = x_ref[rows, :]
                hb_ref[rows, :] = (xv * _rms_scale(xv) * gpre_ref[...]).astype(BF16)
            dgpost_ref[...] = dgpost
            acc_scr[...] = jnp.zeros_like(acc_scr)

        das = [_mm_nt(dy_scr[r * th:(r + 1) * th, :], wd_ref[...]) for r in range(parts)]
        for r in range(parts):
            rows = slice(r * th, (r + 1) * th)
            dgate = (das[r] * dgf_ref[rows, :].astype(F32)).astype(BF16)
            dup = (das[r] * silu_ref[rows, :].astype(F32)).astype(BF16)
            dgb_ref[rows, :] = dgate
            dub_ref[rows, :] = dup
            acc_scr[rows, :] += _mm_nt(dgate, wg_ref[...]) + _mm_nt(dup, wu_ref[...])

        @pl.when(j == ns - 1)
        def _():
            dgpre = jnp.zeros((1, D), F32)
            for r in range(parts):
                rows = slice(r * th, (r + 1) * th)
                xv = x_ref[rows, :]
                rs = _rms_scale(xv)
                xhat = xv * rs
                dh = acc_scr[rows, :]
                dgpre = dgpre + jnp.sum(dh * xhat, axis=0, keepdims=True)
                dx_ref[rows, :] = _rms_bwd(dh, xhat, rs, gpre_ref[...]) + dxo_ref[rows, :]
            dgpre_ref[...] = dgpre

    row = pl.BlockSpec((tm, D), lambda i, j: (i, 0), pipeline_mode=pl.Buffered(1))
    vec = pl.BlockSpec((1, D), lambda i, j: (0, 0))
    wide = pl.BlockSpec((tm, fs), lambda i, j: (i, j))
    part = pl.BlockSpec((None, 1, D), lambda i, j: (i, 0, 0))
    F = ns * fs
    return pl.pallas_call(
        body, name=name, grid=(n_i, ns),
        in_specs=[row, row, row, wide, wide, vec,
                  pl.BlockSpec((None, D, fs), lambda i, j: (j, 0, 0)),
                  pl.BlockSpec((None, D, fs), lambda i, j: (j, 0, 0)),
                  pl.BlockSpec((None, fs, D), lambda i, j: (j, 0, 0)),
                  vec, ANY],
        out_specs=[row, row, row, wide, wide, part, part],
        out_shape=[jax.ShapeDtypeStruct((T, D), F32), jax.ShapeDtypeStruct((T, D), BF16),
                   jax.ShapeDtypeStruct((T, D), BF16), jax.ShapeDtypeStruct((T, F), BF16),
                   jax.ShapeDtypeStruct((T, F), BF16),
                   jax.ShapeDtypeStruct((n_i, 1, D), F32), jax.ShapeDtypeStruct((n_i, 1, D), F32)],
        scratch_shapes=[pltpu.VMEM((tm, D), BF16), pltpu.VMEM((tm, D), F32)],
        compiler_params=_params(("parallel", "arbitrary")),
    )(dxo, x, y, dgf, silu, g_pre, wg, wu, wd, g_post, _after(dep))


def _wgrad(xm, ym, name, shard_cols=False, dep=None):
    T, M = xm.shape
    N = ym.shape[-1]
    assert M * N * 4 <= 16 * 1024 * 1024, (M, N)
    tk = _blk(T, 512)
    n_k = T // tk
    fs = N // N_DEV

    def body(x_ref, y_ref, _, o_ref, acc_scr):
        k = pl.program_id(0)

        @pl.when(k == 0)
        def _():
            acc_scr[...] = jnp.zeros_like(acc_scr)

        acc_scr[...] += _mm_tn(x_ref[...], y_ref[...])

        @pl.when(k == n_k - 1)
        def _():
            if shard_cols:
                for s in range(N_DEV):
                    o_ref[s] = acc_scr[:, s * fs:(s + 1) * fs].astype(BF16)
            else:
                o_ref[...] = acc_scr[...].astype(BF16)

    if shard_cols:
        out_spec = pl.BlockSpec((N_DEV, M, fs), lambda k: (0, 0, 0), pipeline_mode=pl.Buffered(1))
        out_shape = jax.ShapeDtypeStruct((N_DEV, M, fs), BF16)
    else:
        out_spec = pl.BlockSpec((M, N), lambda k: (0, 0), pipeline_mode=pl.Buffered(1))
        out_shape = jax.ShapeDtypeStruct((M, N), BF16)
    return pl.pallas_call(
        body, name=name, grid=(n_k,),
        in_specs=[pl.BlockSpec((tk, M), lambda k: (k, 0)), pl.BlockSpec((tk, N), lambda k: (k, 0)), ANY],
        out_specs=out_spec, out_shape=out_shape,
        scratch_shapes=[pltpu.VMEM((M, N), F32)],
        compiler_params=_params(("arbitrary",)),
    )(xm, ym, _after(dep))


def _wgrad_multi(xm, segs, name, dep=None):
    T, M = xm.shape
    N = segs[0][0].shape[-1]
    n_seg = len(segs)
    assert M * N * n_seg * 4 <= 16 * 1024 * 1024, (M, N, n_seg)
    tk = _blk(T, 512)
    n_k = T // tk

    def body(*refs):
        x_ref, y_refs = refs[0], refs[1:1 + n_seg]
        o_ref, acc_scr = refs[2 + n_seg], refs[3 + n_seg]
        k = pl.program_id(0)

        @pl.when(k == 0)
        def _():
            acc_scr[...] = jnp.zeros_like(acc_scr)

        x = x_ref[...]
        for s in range(n_seg):
            acc_scr[:, s * N:(s + 1) * N] += _mm_tn(x, y_refs[s][...])

        @pl.when(k == n_k - 1)
        def _():
            o_ref[...] = acc_scr[...].astype(BF16)

    y_specs = [pl.BlockSpec((tk, N), lambda k: (k, 0)) if idx is None
               else pl.BlockSpec((None, tk, N), lambda k, idx=idx: (idx, k, 0)) for _, idx in segs]
    return pl.pallas_call(
        body, name=name, grid=(n_k,),
        in_specs=[pl.BlockSpec((tk, M), lambda k: (k, 0))] + y_specs + [ANY],
        out_specs=pl.BlockSpec((M, n_seg * N), lambda k: (0, 0), pipeline_mode=pl.Buffered(1)),
        out_shape=jax.ShapeDtypeStruct((M, n_seg * N), BF16),
        scratch_shapes=[pltpu.VMEM((M, n_seg * N), F32)],
        compiler_params=_params(("arbitrary",)),
    )(xm, *[arr for arr, _ in segs], _after(dep))


def _mix_in_fwd(x1, g, w7, wf, name):
    T, D = x1.shape
    n_seg, _, W = w7.shape
    tm = _blk(T, 1024)

    def body(x_ref, g_ref, w_ref, wf_ref, z_ref, f_ref, hb_ref, h_scr):
        s = pl.program_id(1)

        @pl.when(s == 0)
        def _():
            xv = x_ref[...]
            h = (xv * _rms_scale(xv) * g_ref[...]).astype(BF16)
            h_scr[...] = h
            hb_ref[...] = h
            f_ref[...] = _mm(h, wf_ref[...])

        z_ref[...] = _mm(h_scr[...], w_ref[...]).astype(BF16)

    return pl.pallas_call(
        body, name=name, grid=(T // tm, n_seg),
        in_specs=[pl.BlockSpec((tm, D), lambda i, s: (i, 0)),
                  pl.BlockSpec((1, D), lambda i, s: (0, 0)),
                  pl.BlockSpec((None, D, W), lambda i, s: (s, 0, 0)),
                  pl.BlockSpec((D, LANES), lambda i, s: (0, 0))],
        out_specs=[pl.BlockSpec((None, tm, W), lambda i, s: (s, i, 0)),
                   pl.BlockSpec((tm, LANES), lambda i, s: (i, 0)),
                   pl.BlockSpec((tm, D), lambda i, s: (i, 0))],
        out_shape=[jax.ShapeDtypeStruct((n_seg, T, W), BF16), jax.ShapeDtypeStruct((T, LANES), F32),
                   jax.ShapeDtypeStruct((T, D), BF16)],
        scratch_shapes=[pltpu.VMEM((tm, D), BF16)],
        compiler_params=_params(("parallel", "arbitrary")),
    )(x1, g, w7, wf)


def _mix_in_bwd(dx2, x1, g, segs, dfb, w7, wf, name, dep=None):
    T, D = x1.shape
    n_seg, _, W = w7.shape
    tm, th, parts = _ffn_rows(T)
    n_i = T // tm

    def body(*refs):
        dx2_ref, x_ref, g_ref = refs[:3]
        seg_refs = refs[3:3 + n_seg]
        df_ref, w_ref, wf_ref, _, dx1_ref, dg_ref, acc_scr = refs[3 + n_seg:]
        s = pl.program_id(1)

        @pl.when(s == 0)
        def _():
            acc_scr[...] = _mm_nt(df_ref[...], wf_ref[...])

        for q in range(n_seg):
            @pl.when(s == q)
            def _(q=q):
                acc_scr[...] += _mm_nt(seg_refs[q][...], w_ref[...])

        @pl.when(s == n_seg - 1)
        def _():
            dg = jnp.zeros((1, D), F32)
            for p in range(parts):
                rows = slice(p * th, (p + 1) * th)
                xv = x_ref[rows, :]
                r = _rms_scale(xv)
                xhat = xv * r
                dh = acc_scr[rows, :]
                dg = dg + jnp.sum(dh * xhat, axis=0, keepdims=True)
                dx1_ref[rows, :] = _rms_bwd(dh, xhat, r, g_ref[...]) + dx2_ref[rows, :]
            dg_ref[...] = dg

    row = pl.BlockSpec((tm, D), lambda i, s: (i, 0), pipeline_mode=pl.Buffered(1))
    seg_specs = []
    seg_args = []
    for arr, idx in segs:
        if idx is None:
            seg_specs.append(pl.BlockSpec((tm, W), lambda i, s: (i, 0)))
        else:
            seg_specs.append(pl.BlockSpec((None, tm, W), lambda i, s, idx=idx: (idx, i, 0)))
        seg_args.append(arr)
    return pl.pallas_call(
        body, name=name, grid=(n_i, n_seg),
        in_specs=[row, row, pl.BlockSpec((1, D), lambda i, s: (0, 0))] + seg_specs + [
            pl.BlockSpec((tm, LANES), lambda i, s: (i, 0)),
            pl.BlockSpec((None, D, W), lambda i, s: (s, 0, 0)),
            pl.BlockSpec((D, LANES), lambda i, s: (0, 0)), ANY],
        out_specs=[row, pl.BlockSpec((None, 1, D), lambda i, s: (i, 0, 0))],
        out_shape=[jax.ShapeDtypeStruct((T, D), F32), jax.ShapeDtypeStruct((n_i, 1, D), F32)],
        scratch_shapes=[pltpu.VMEM((tm, D), F32)],
        compiler_params=_params(("parallel", "arbitrary")),
    )(dx2, x1, g, *seg_args, dfb, w7, wf, _after(dep))


def _forget_cumsum(f, b_pad, name):
    T, L = f.shape
    tb = _blk(T, 256)

    def body(f_ref, b_ref, c_ref, carry):
        @pl.when(pl.program_id(0) == 0)
        def _():
            carry[...] = jnp.zeros_like(carry)

        lf = jax.nn.log_sigmoid(f_ref[...] + b_ref[...])
        rows = lax.broadcasted_iota(jnp.int32, (tb, tb), 0)
        cols = lax.broadcasted_iota(jnp.int32, (tb, tb), 1)
        tri = (cols <= rows).astype(F32)
        c = jnp.dot(tri, lf, preferred_element_type=F32, precision=lax.Precision.HIGHEST) + carry[...]
        c_ref[...] = c
        carry[...] = c[tb - 1:tb, :]

    return pl.pallas_call(
        body, name=name, grid=(T // tb,),
        in_specs=[pl.BlockSpec((tb, L), lambda i: (i, 0)), pl.BlockSpec((1, L), lambda i: (0, 0))],
        out_specs=pl.BlockSpec((tb, L), lambda i: (i, 0)),
        out_shape=jax.ShapeDtypeStruct((T, L), F32),
        scratch_shapes=[pltpu.VMEM((1, L), F32)],
        compiler_params=_params(("arbitrary",)),
    )(f, b_pad)


def _forget_bwd(dc, f, b_pad, name):
    T, L = f.shape
    tb = _blk(T, 256)
    nb = T // tb

    def body(dc_ref, f_ref, b_ref, df_ref, db_ref, carry):
        @pl.when(pl.program_id(0) == 0)
        def _():
            carry[...] = jnp.zeros_like(carry)
            db_ref[...] = jnp.zeros_like(db_ref)

        rows = lax.broadcasted_iota(jnp.int32, (tb, tb), 0)
        cols = lax.broadcasted_iota(jnp.int32, (tb, tb), 1)
        tri = (cols >= rows).astype(F32)
        r = jnp.dot(tri, dc_ref[...], preferred_element_type=F32, precision=lax.Precision.HIGHEST) + carry[...]
        carry[...] = r[0:1, :]
        df = r * (1.0 - jax.nn.sigmoid(f_ref[...] + b_ref[...]))
        df_ref[...] = df.astype(BF16)
        db_ref[...] += jnp.sum(df, axis=0, keepdims=True)

    rev = pl.BlockSpec((tb, L), lambda i: (nb - 1 - i, 0))
    one = pl.BlockSpec((1, L), lambda i: (0, 0))
    return pl.pallas_call(
        body, name=name, grid=(nb,),
        in_specs=[rev, rev, one], out_specs=[rev, one],
        out_shape=[jax.ShapeDtypeStruct((T, L), BF16), jax.ShapeDtypeStruct((1, L), F32)],
        scratch_shapes=[pltpu.VMEM((1, L), F32)],
        compiler_params=_params(("arbitrary",)),
    )(dc, f, b_pad)


ATTN_TILE = 512
ATTN_CHAINS = 4


def _attn_geometry(T):
    ta = _blk(T, ATTN_TILE)
    nc = ATTN_CHAINS if (T // ta) % ATTN_CHAINS == 0 else 1
    return ta, nc, T // ta


def _causal_tile(ta, keys_on_rows=False):
    rows = lax.broadcasted_iota(jnp.int32, (ta, ta), 0)
    cols = lax.broadcasted_iota(jnp.int32, (ta, ta), 1)
    return rows <= cols if keys_on_rows else cols <= rows


def _chunk(ref, j, ta):
    return ref[pl.ds(pl.multiple_of(j * ta, ta), ta), :]


def _attn_fwd_keys_on_rows(z7, vt, c_rep, name):
    _, T, W = z7.shape
    H = W // HEAD_DIM
    ta, nc, n_chunks = _attn_geometry(T)
    scale = np.float32(1.0 / np.sqrt(HEAD_DIM))
    reps = ta // LANES

    def body(q_ref, k_ref, vt_ref, c_ref, o_ref, lse_ref):
        g = pl.program_id(1)

        def scores(ch, k):
            return _mm_nt(k, q_ref[ch * ta:(ch + 1) * ta, :])

        def update(state, raw, vt, cj, diagonal):
            m_prev, l_prev, acc_prev = state
            st = raw * (scale * LOG2E) - cj
            if diagonal:
                st = jnp.where(_causal_tile(ta, keys_on_rows=True), st, NEG_BIG)
            m_new = jnp.maximum(m_prev, jnp.max(st, axis=0, keepdims=True))
            alpha = jnp.exp2(m_prev - m_new)
            pt = jnp.exp2(st - m_new)
            l_new = alpha * l_prev + jnp.sum(pt, axis=0, keepdims=True)
            acc_new = alpha * acc_prev + _mm(vt, pt.astype(BF16))
            return m_new, l_new, acc_new

        def load(j):
            cj = _chunk(c_ref, j, ta)
            return _chunk(k_ref, j, ta), vt_ref[j], jnp.concatenate([cj] * reps, axis=1)

        def full_chunk(j, states):
            k, vt, cj = load(j)
            raws = [scores(ch, k) for ch in range(nc)]
            return tuple(update(states[ch], raws[ch], vt, cj, False) for ch in range(nc))

        first = (jnp.full((1, ta), NEG_BIG, F32), jnp.zeros((1, ta), F32), jnp.zeros((HEAD_DIM, ta), F32))
        states = list(lax.fori_loop(0, nc * g, full_chunk, (first,) * nc))
        for jj in range(nc):
            k, vt, cj = load(nc * g + jj)
            raws = {ch: scores(ch, k) for ch in range(jj, nc)}
            for ch in range(jj, nc):
                states[ch] = update(states[ch], raws[ch], vt, cj, ch == jj)
        for ch in range(nc):
            m, l, acc = states[ch]
            o_ref[ch * ta:(ch + 1) * ta, :] = (acc / l).T
            lse_ref[ch] = m + jnp.log2(l)

    tq = nc * ta
    return pl.pallas_call(
        body, name=name, grid=(H, n_chunks // nc),
        in_specs=[pl.BlockSpec((None, tq, HEAD_DIM), lambda h, g: (0, g, h)),
                  pl.BlockSpec((None, T, HEAD_DIM), lambda h, g: (1, 0, h)),
                  pl.BlockSpec((None, n_chunks, HEAD_DIM, ta), lambda h, g: (h, 0, 0, 0)),
                  pl.BlockSpec((None, T, LANES), lambda h, g: (h, 0, 0))],
        out_specs=[pl.BlockSpec((tq, HEAD_DIM), lambda h, g: (g, h)),
                   pl.BlockSpec((None, nc, 1, ta), lambda h, g: (h, g, 0, 0))],
        out_shape=[jax.ShapeDtypeStruct((T, W), F32), jax.ShapeDtypeStruct((H, n_chunks, 1, ta), F32)],
        compiler_params=_params(("parallel", "arbitrary")),
    )(z7, z7, vt, c_rep)


def _attn_bwd_fused(z7, kt, dob, c_rep, lse_chunks, d_chunks, name):
    _, T, W = z7.shape
    H = W // HEAD_DIM
    ta, nc, n_chunks = _attn_geometry(T)
    n_steps = n_chunks // nc
    scale = np.float32(1.0 / np.sqrt(HEAD_DIM))
    reps = ta // LANES

    def body(k_ref, v_ref, kt_ref, q_ref, do_ref, c_ref, lse_ref, d_ref,
             dk_ref, dv_ref, dck_ref, dq_ref, dcq_ref, dk_scr, dv_scr, dck_scr, dqt_scr, dcq_scr):
        g = pl.program_id(1)

        @pl.when(g == 0)
        def _():
            dqt_scr[...] = jnp.zeros_like(dqt_scr)
            dcq_scr[...] = jnp.zeros_like(dcq_scr)

        dk_scr[...] = jnp.zeros_like(dk_scr)
        dv_scr[...] = jnp.zeros_like(dv_scr)
        dck_scr[...] = jnp.zeros_like(dck_scr)

        def products(ch, q, do):
            rows = slice(ch * ta, (ch + 1) * ta)
            return _mm_nt(k_ref[rows, :], q), _mm_nt(v_ref[rows, :], do)

        def update(ch, i, q, do, prods, diagonal):
            rows = slice(ch * ta, (ch + 1) * ta)
            cj = c_ref[rows, :]
            st = prods[0] * (scale * LOG2E) - jnp.concatenate([cj] * reps, axis=1) - lse_ref[i]
            if diagonal:
                st = jnp.where(_causal_tile(ta, keys_on_rows=True), st, NEG_BIG)
            pt = jnp.exp2(st)
            dv_scr[ch] += _mm(pt.astype(BF16), do)
            dst = pt * (prods[1] - d_ref[i])
            dst_b = dst.astype(BF16)
            dk_scr[ch] += _mm(dst_b, q)
            dqt_scr[i] += _mm(kt_ref[ch], dst_b)
            dcq_scr[i] += jnp.sum(dst, axis=0, keepdims=True)
            lane_sum = dst[:, :LANES]
            for r in range(1, reps):
                lane_sum = lane_sum + dst[:, r * LANES:(r + 1) * LANES]
            dck_scr[ch] += lane_sum

        for ii in range(nc):
            i = nc * g + ii
            q = _chunk(q_ref, i, ta)
            do = _chunk(do_ref, i, ta)
            prods = [products(ch, q, do) for ch in range(0, ii + 1)]
            for ch in range(0, ii + 1):
                update(ch, i, q, do, prods[ch], ch == ii)

        def full_chunk(i, carry):
            q = _chunk(q_ref, i, ta)
            do = _chunk(do_ref, i, ta)
            prods = [products(ch, q, do) for ch in range(nc)]
            for ch in range(nc):
                update(ch, i, q, do, prods[ch], False)
            return carry

        lax.fori_loop(nc * (g + 1), n_chunks, full_chunk, 0)
        for ch in range(nc):
            rows = slice(ch * ta, (ch + 1) * ta)
            dk_ref[rows, :] = (dk_scr[ch] * scale).astype(BF16)
            dv_ref[rows, :] = dv_scr[ch].astype(BF16)
            ones = jnp.ones((8, LANES), F32)
            sums = lax.dot_general(ones, dck_scr[ch], (((1,), (1,)), ((), ())), preferred_element_type=F32,
                                   precision=lax.Precision.HIGHEST)
            dck_ref[ch] = -sums[0:1, :]

        @pl.when(g == n_steps - 1)
        def _():
            for i in range(n_chunks):
                dq_ref[i * ta:(i + 1) * ta, :] = (dqt_scr[i] * scale).T.astype(BF16)
            dcq_ref[...] = dcq_scr[...]

    tk = nc * ta
    chunks = pl.BlockSpec((None, n_chunks, 1, ta), lambda h, g: (h, 0, 0, 0))
    tile = pl.BlockSpec((tk, HEAD_DIM), lambda h, g: (g, h))
    return pl.pallas_call(
        body, name=name, grid=(H, n_steps),
        in_specs=[pl.BlockSpec((None, tk, HEAD_DIM), lambda h, g: (1, g, h)),
                  pl.BlockSpec((None, tk, HEAD_DIM), lambda h, g: (2, g, h)),
                  pl.BlockSpec((None, nc, HEAD_DIM, ta), lambda h, g: (h, g, 0, 0)),
                  pl.BlockSpec((None, T, HEAD_DIM), lambda h, g: (0, 0, h)),
                  pl.BlockSpec((T, HEAD_DIM), lambda h, g: (0, h)),
                  pl.BlockSpec((None, tk, LANES), lambda h, g: (h, g, 0)),
                  chunks, chunks],
        out_specs=[tile, tile, pl.BlockSpec((None, nc, 1, ta), lambda h, g: (h, g, 0, 0)),
                   pl.BlockSpec((T, HEAD_DIM), lambda h, g: (0, h)), chunks],
        out_shape=[jax.ShapeDtypeStruct((T, W), BF16), jax.ShapeDtypeStruct((T, W), BF16),
                   jax.ShapeDtypeStruct((H, n_chunks, 1, ta), F32), jax.ShapeDtypeStruct((T, W), BF16),
                   jax.ShapeDtypeStruct((H, n_chunks, 1, ta), F32)],
        scratch_shapes=[pltpu.VMEM((nc, ta, HEAD_DIM), F32), pltpu.VMEM((nc, ta, HEAD_DIM), F32),
                        pltpu.VMEM((nc, ta, LANES), F32), pltpu.VMEM((n_chunks, HEAD_DIM, ta), F32),
                        pltpu.VMEM((n_chunks, 1, ta), F32)],
        compiler_params=_params(("parallel", "arbitrary")),
    )(z7, z7, kt, z7, dob, c_rep, lse_chunks, d_chunks)


def _chunk_causal_mask():
    rows = lax.broadcasted_iota(jnp.int32, (SGU_LEN, SGU_LEN), 0)
    cols = lax.broadcasted_iota(jnp.int32, (SGU_LEN, SGU_LEN), 1)
    return (cols // CHUNK) <= (rows // CHUNK)


def _sgu_norm_mix(sv, lng_ref, lnb_ref, ws_ref, bs_ref, vn_scr, mixed_scr, vhat_scr=None):
    tm = sv.shape[0]
    vs = _gelu(sv)
    mask = _chunk_causal_mask()
    rstds = []
    for g in range(N_GROUPS):
        lanes = slice(g * GROUP_DIM, (g + 1) * GROUP_DIM)
        blk = vs[:, lanes]
        cen = blk - jnp.mean(blk, axis=-1, keepdims=True)
        rstd = lax.rsqrt(jnp.mean(cen * cen, axis=-1, keepdims=True) + LN_EPS)
        vhat = cen * rstd
        rstds.append(rstd)
        if vhat_scr is not None:
            vhat_scr[:, lanes] = vhat
        vn_scr[:, lanes] = (vhat * lng_ref[:, lanes] + lnb_ref[:, lanes]).astype(BF16)
        wm = jnp.where(mask, ws_ref[g], 0.0).astype(BF16)
        for w in range(tm // SGU_LEN):
            rows = slice(w * SGU_LEN, (w + 1) * SGU_LEN)
            mixed_scr[rows, lanes] = _mm(wm, vn_scr[rows, lanes]) + bs_ref[g]
    return rstds


def _mix_out_fwd(z7, o_a, x1, lng, lnb, ws, bs, w_out, g_post, name):
    _, T, W = z7.shape
    D = x1.shape[1]
    tm = _blk(T, 256)

    def body(u_ref, sv_ref, ga_ref, gb_ref, oa_ref, x1_ref, lng_ref, lnb_ref, ws_ref, bs_ref, wo_ref, gp_ref,
             x2_ref, p_ref, mb_ref, vn_scr, mixed_scr):
        _sgu_norm_mix(sv_ref[...].astype(F32), lng_ref, lnb_ref, ws_ref, bs_ref, vn_scr, mixed_scr)
        o_b = _gelu(u_ref[...].astype(F32)) * mixed_scr[...]
        merged = (jax.nn.sigmoid(ga_ref[...].astype(F32)) * oa_ref[...]
                  + jax.nn.sigmoid(gb_ref[...].astype(F32)) * o_b).astype(BF16)
        mb_ref[...] = merged
        p = _mm(merged, wo_ref[...])
        p_ref[...] = p
        x2_ref[...] = x1_ref[...] + p * _rms_scale(p) * gp_ref[...]

    def seg(idx):
        return pl.BlockSpec((None, tm, W), lambda i, idx=idx: (idx, i, 0))

    row = pl.BlockSpec((tm, D), lambda i: (i, 0))
    vec = pl.BlockSpec((1, D), lambda i: (0, 0))
    return pl.pallas_call(
        body, name=name, grid=(T // tm,),
        in_specs=[seg(3), seg(4), seg(5), seg(6), row, row, vec, vec,
                  pl.BlockSpec((N_GROUPS, SGU_LEN, SGU_LEN), lambda i: (0, 0, 0)),
                  pl.BlockSpec((N_GROUPS, SGU_LEN, 1), lambda i: (0, 0, 0)),
                  pl.BlockSpec((D, D), lambda i: (0, 0)), vec],
        out_specs=[row, row, row],
        out_shape=[jax.ShapeDtypeStruct((T, D), F32), jax.ShapeDtypeStruct((T, D), F32),
                   jax.ShapeDtypeStruct((T, D), BF16)],
        scratch_shapes=[pltpu.VMEM((tm, W), BF16), pltpu.VMEM((tm, W), F32)],
        compiler_params=_params(("parallel",)),
    )(z7, z7, z7, z7, o_a, x1, lng, lnb, ws, bs, w_out, g_post)


def _mix_out_bwd(dx2, p, z7, o_a, lng, lnb, ws, bs, w_out, g_post, name, dep=None):
    _, T, W = z7.shape
    D = dx2.shape[1]
    tm = _blk(T, 256)
    n_w = tm // SGU_LEN

    def body(dx2_ref, p_ref, u_ref, sv_ref, ga_ref, gb_ref, oa_ref, lng_ref, lnb_ref, ws_ref, bs_ref, wo_ref, gp_ref, _,
             dpb_ref, dob_ref, dvec_ref, dz_ref, dgp_ref, dlng_ref, dlnb_ref, dws_ref, dbs_ref,
             vn_scr, mixed_scr, vhat_scr, dmix_scr, dvn_scr):
        @pl.when(pl.program_id(0) == 0)
        def _():
            dgp_ref[...] = jnp.zeros_like(dgp_ref)
            dlng_ref[...] = jnp.zeros_like(dlng_ref)
            dlnb_ref[...] = jnp.zeros_like(dlnb_ref)
            dws_ref[...] = jnp.zeros_like(dws_ref)
            dbs_ref[...] = jnp.zeros_like(dbs_ref)

        pv = p_ref[...]
        s = _rms_scale(pv)
        n = pv * s
        dn = dx2_ref[...]
        dgp_ref[...] += jnp.sum(dn * n, axis=0, keepdims=True)
        dpb = _rms_bwd(dn, n, s, gp_ref[...]).astype(BF16)
        dpb_ref[...] = dpb
        dmerged = _mm_nt(dpb, wo_ref[...])

        sv = sv_ref[...].astype(F32)
        rstds = _sgu_norm_mix(sv, lng_ref, lnb_ref, ws_ref, bs_ref, vn_scr, mixed_scr, vhat_scr)
        u_pre = u_ref[...].astype(F32)
        u = _gelu(u_pre)
        mixed = mixed_scr[...]
        sa = jax.nn.sigmoid(ga_ref[...].astype(F32))
        sb = jax.nn.sigmoid(gb_ref[...].astype(F32))
        oa = oa_ref[...]
        do_a = (dmerged * sa).astype(BF16)
        dob_ref[...] = do_a
        prod = do_a.astype(F32) * oa
        for h in range(N_HEADS):
            sums = lax.dot_general(jnp.ones((8, LANES), F32), prod[:, h * HEAD_DIM:(h + 1) * HEAD_DIM],
                                   (((1,), (1,)), ((), ())), preferred_element_type=F32,
                                   precision=lax.Precision.HIGHEST)
            dvec_ref[h, 0] = sums[0:1, :]
        dz_ref[2] = (dmerged * oa * (sa * (1.0 - sa))).astype(BF16)
        dz_ref[3] = (dmerged * (u * mixed) * (sb * (1.0 - sb))).astype(BF16)
        do_b = dmerged * sb
        dz_ref[0] = (do_b * mixed * _gelu_grad(u_pre)).astype(BF16)
        dmix_scr[...] = do_b * u

        mask = _chunk_causal_mask()
        for g in range(N_GROUPS):
            lanes = slice(g * GROUP_DIM, (g + 1) * GROUP_DIM)
            wm = jnp.where(mask, ws_ref[g], 0.0).astype(BF16)
            dws = jnp.zeros((SGU_LEN, SGU_LEN), F32)
            dbs = jnp.zeros((SGU_LEN, 1), F32)
            for w in range(n_w):
                rows = slice(w * SGU_LEN, (w + 1) * SGU_LEN)
                dmix = dmix_scr[rows, lanes]
                dmix_b = dmix.astype(BF16)
                dvn_scr[rows, lanes] = _mm_tn(wm, dmix_b)
                dws = dws + _mm_nt(dmix_b, vn_scr[rows, lanes])
                dbs = dbs + jnp.sum(dmix, axis=-1, keepdims=True)
            dws_ref[g] += jnp.where(mask, dws, 0.0)
            dbs_ref[g] += dbs
            dvn = dvn_scr[:, lanes]
            vhat = vhat_scr[:, lanes]
            dlng_ref[:, lanes] += jnp.sum(dvn * vhat, axis=0, keepdims=True)
            dlnb_ref[:, lanes] += jnp.sum(dvn, axis=0, keepdims=True)
            dvh = dvn * lng_ref[:, lanes]
            dvs = rstds[g] * (dvh - jnp.mean(dvh, axis=-1, keepdims=True)
                              - vhat * jnp.mean(dvh * vhat, axis=-1, keepdims=True))
            dvn_scr[:, lanes] = dvs
        dz_ref[1] = (dvn_scr[...] * _gelu_grad(sv)).astype(BF16)

    def seg(idx):
        return pl.BlockSpec((None, tm, W), lambda i, idx=idx: (idx, i, 0))

    row = pl.BlockSpec((tm, D), lambda i: (i, 0))
    vec = pl.BlockSpec((1, D), lambda i: (0, 0))
    ws_spec = pl.BlockSpec((N_GROUPS, SGU_LEN, SGU_LEN), lambda i: (0, 0, 0))
    bs_spec = pl.BlockSpec((N_GROUPS, SGU_LEN, 1), lambda i: (0, 0, 0))
    return pl.pallas_call(
        body, name=name, grid=(T // tm,),
        in_specs=[row, row, seg(3), seg(4), seg(5), seg(6), row, vec, vec, ws_spec, bs_spec,
                  pl.BlockSpec((D, D), lambda i: (0, 0)), vec, ANY],
        out_specs=[row, row, pl.BlockSpec((N_HEADS, 1, 1, tm), lambda i: (0, i, 0, 0)),
                   pl.BlockSpec((4, tm, W), lambda i: (0, i, 0)), vec, vec, vec, ws_spec, bs_spec],
        out_shape=[jax.ShapeDtypeStruct((T, D), BF16), jax.ShapeDtypeStruct((T, W), BF16),
                   jax.ShapeDtypeStruct((N_HEADS, T // tm, 1, tm), F32), jax.ShapeDtypeStruct((4, T, W), BF16),
                   jax.ShapeDtypeStruct((1, D), F32), jax.ShapeDtypeStruct((1, D), F32),
                   jax.ShapeDtypeStruct((1, D), F32),
                   jax.ShapeDtypeStruct((N_GROUPS, SGU_LEN, SGU_LEN), F32),
                   jax.ShapeDtypeStruct((N_GROUPS, SGU_LEN, 1), F32)],
        scratch_shapes=[pltpu.VMEM((tm, W), BF16), pltpu.VMEM((tm, W), F32), pltpu.VMEM((tm, W), F32),
                        pltpu.VMEM((tm, W), F32), pltpu.VMEM((tm, W), F32)],
        compiler_params=_params(("arbitrary",)),
    )(dx2, p, z7, z7, z7, z7, o_a, lng, lnb, ws, bs, w_out, g_post, _after(dep))


def _loss_head(y, target, name):
    T, D = y.shape
    tm = _blk(T, 1024)
    n_i = T // tm

    def body(y_ref, t_ref, dy_ref, loss_ref, acc_scr):
        i = pl.program_id(0)

        @pl.when(i == 0)
        def _():
            acc_scr[...] = jnp.zeros_like(acc_scr)

        e = y_ref[...] - t_ref[...]
        dy_ref[...] = e * np.float32(1.0 / D)
        acc_scr[...] += jnp.sum(e * e, axis=0, keepdims=True)

        @pl.when(i == n_i - 1)
        def _():
            total = jnp.sum(acc_scr[...], axis=-1, keepdims=True) * np.float32(0.5 / D)
            loss_ref[...] = jnp.broadcast_to(total, loss_ref.shape)

    row = pl.BlockSpec((tm, D), lambda i: (i, 0))
    return pl.pallas_call(
        body, name=name, grid=(n_i,),
        in_specs=[row, row],
        out_specs=[row, pl.BlockSpec((1, LANES), lambda i: (0, 0))],
        out_shape=[jax.ShapeDtypeStruct((T, D), F32), jax.ShapeDtypeStruct((1, LANES), F32)],
        scratch_shapes=[pltpu.VMEM((1, D), F32)],
        compiler_params=_params(("arbitrary",)),
    )(y, target)


def _adamw_math(w, g, m, v):
    m_new = ADAM_B1 * m + (1.0 - ADAM_B1) * g
    v_new = ADAM_B2 * v + (1.0 - ADAM_B2) * (g * g)
    m_hat = m_new / np.float32(1.0 - ADAM_B1 ** ADAM_STEP)
    v_hat = v_new / np.float32(1.0 - ADAM_B2 ** ADAM_STEP)
    delta = -ADAM_LR * (m_hat / (jnp.sqrt(v_hat) + ADAM_EPS) + ADAM_WD * w)
    return delta, m_new, v_new


def _sum_adamw(parts, w, m, v, name, dep=None):
    n, R, C = parts.shape
    tr = _blk(R, 512)

    def body(p_ref, w_ref, m_ref, v_ref, _, g_ref, d_ref, mo_ref, vo_ref):
        g = p_ref[0].astype(F32)
        for s in range(1, n):
            g = g + p_ref[s].astype(F32)
        delta, m_new, v_new = _adamw_math(w_ref[...], g, m_ref[...], v_ref[...])
        g_ref[...] = g
        d_ref[...] = delta
        mo_ref[...] = m_new
        vo_ref[...] = v_new

    row = pl.BlockSpec((tr, C), lambda i: (i, 0))
    shp = jax.ShapeDtypeStruct((R, C), F32)
    return pl.pallas_call(
        body, name=name, grid=(R // tr,),
        in_specs=[pl.BlockSpec((n, tr, C), lambda i: (0, i, 0)), row, row, row, ANY],
        out_specs=[row, row, row, row], out_shape=[shp, shp, shp, shp],
        compiler_params=_params(("parallel",)),
    )(parts, w, m, v, _after(dep))


def _position():
    return lax.axis_index("x"), lax.axis_index("y"), lax.axis_index("c")


def _slot(px, py, pc):
    return 4 * px + 2 * py + pc


def _all_gather(shards, name):
    n = len(shards)

    def body(*refs):
        ins, outs = refs[:n], refs[n:2 * n]
        send_sems, recv_sems, local_sems = refs[2 * n:]
        x, y, c = _position()
        me, sibling = (x, y, c), (x, y, 1 - c)
        chips = [(1 - x, y), (x, 1 - y), (1 - x, 1 - y)]

        def copy(a, k, block, to, src=None):
            dst = outs[a].at[_slot(*block)]
            return pltpu.make_async_remote_copy(
                src_ref=dst if src is None else src, dst_ref=dst,
                send_sem=send_sems.at[a, k], recv_sem=recv_sems.at[a, k],
                device_id=to, device_id_type=MESH)

        mine = [pltpu.make_async_copy(ins[a], outs[a].at[_slot(*me)], local_sems.at[a]) for a in range(n)]
        for cp in mine:
            cp.start()
        first = []
        for a in range(n):
            first.append(copy(a, 0, me, sibling, src=ins[a]))
            first += [copy(a, 1 + j, me, (*chip, c), src=ins[a]) for j, chip in enumerate(chips)]
        for cp in first:
            cp.start()
        passed = []
        for j, chip in enumerate(chips):
            for a in range(n):
                copy(a, 1 + j, (*chip, c), me).wait_recv()
                fwd = copy(a, 4 + j, (*chip, c), sibling)
                fwd.start()
                passed.append(fwd)
        for a in range(n):
            copy(a, 0, sibling, me).wait_recv()
            for j, chip in enumerate(chips):
                copy(a, 4 + j, (*chip, 1 - c), me).wait_recv()
        for cp in first + passed:
            cp.wait_send()
        for cp in mine:
            cp.wait()

    return pl.pallas_call(
        body, name=name,
        in_specs=[ANY] * n, out_specs=[ANY] * n,
        out_shape=[jax.ShapeDtypeStruct((N_DEV,) + s.shape, s.dtype) for s in shards],
        scratch_shapes=[pltpu.SemaphoreType.DMA((n, 7)), pltpu.SemaphoreType.DMA((n, 7)),
                        pltpu.SemaphoreType.DMA((n,))],
    )(*shards)


def _peer(x, y, c, k):
    return (1 - x if k & 4 else x, 1 - y if k & 2 else y, 1 - c if k & 1 else c)


def _remote_copies(src_refs, land_refs, send_sems, recv_sems, gather, outgoing):
    x, y, c = _position()
    me = _slot(x, y, c)
    copies = []
    for k in range(1, N_DEV):
        peer = _peer(x, y, c, k)
        for a in range(len(src_refs)):
            src = src_refs[a] if gather else src_refs[a].at[_slot(*peer)]
            dst = land_refs[a].at[me if outgoing else _slot(*peer)]
            sem = a * (N_DEV - 1) + k - 1
            copies.append(pltpu.make_async_remote_copy(
                src_ref=src, dst_ref=dst, send_sem=send_sems.at[sem], recv_sem=recv_sems.at[sem],
                device_id=peer, device_id_type=MESH))
    return copies


def _sequencer_exchange(srcs, name, gather, collective_id):
    n = len(srcs)
    hbm = pltpu.MemorySpace.HBM
    src_refs = [jax.new_ref(s, memory_space=hbm) for s in srcs]
    land_refs = [jax.empty_ref(jax.ShapeDtypeStruct(((N_DEV,) + s.shape) if gather else s.shape, s.dtype),
                               memory_space=hbm) for s in srcs]
    n_sems = n * (N_DEV - 1)
    block_bytes = sum(s.size * s.dtype.itemsize // (1 if gather else N_DEV) for s in srcs)
    cost = pl.CostEstimate(flops=0, transcendentals=0, bytes_accessed=2 * N_DEV * block_bytes,
                           remote_bytes_transferred=(N_DEV - 1) * block_bytes)

    @pl.kernel(mesh=plsc.ScalarSubcoreMesh(axis_name="sequencer", num_cores=1), name=name,
               scratch_types=(pltpu.SemaphoreType.DMA((n_sems,)), pltpu.SemaphoreType.DMA((n_sems,)),
                              pltpu.SemaphoreType.DMA((n,))),
               cost_estimate=cost,
               compiler_params=pltpu.CompilerParams(collective_id=collective_id))
    def launch(send_sems, recv_sems, local_sems):
        x, y, c = _position()
        me = _slot(x, y, c)
        barrier = pltpu.get_barrier_semaphore()
        for k in range(1, N_DEV):
            pl.semaphore_signal(barrier, inc=1, device_id=_peer(x, y, c, k), device_id_type=MESH)
        pl.semaphore_wait(barrier, N_DEV - 1)
        mine = [pltpu.make_async_copy(src_refs[a] if gather else src_refs[a].at[me], land_refs[a].at[me],
                                      local_sems.at[a]) for a in range(n)]
        for cp in mine:
            cp.start()
        sends = _remote_copies(src_refs, land_refs, send_sems, recv_sems, gather, outgoing=True)
        for cp in sends:
            cp.start()
        for cp in _remote_copies(src_refs, land_refs, send_sems, recv_sems, gather, outgoing=False):
            cp.wait_recv()
        for cp in sends:
            cp.wait_send()
        for cp in mine:
            cp.wait()

    launch()
    return [r[...] for r in land_refs]


SMALL_VECS = ("ffn1_pre_g", "ffn1_post_g", "mix_pre_g", "sgu_ln_g", "sgu_ln_b", "mix_post_g", "ffn2_pre_g",
              "ffn2_post_g")
ROW_BS = len(SMALL_VECS)
ROW_BF = ROW_BS + 1
ROW_LOSS = ROW_BF + 1
ROW_WS = 16
BLOB_ROWS = ROW_WS + SGU_LEN


def _pack_small(vals, D, loss_row=None):
    rows = [vals[n].reshape(1, D) for n in SMALL_VECS]
    rows.append(vals["sgu_b_s"].reshape(1, D))
    rows.append(jnp.pad(vals["b_forget"].reshape(1, N_HEADS), ((0, 0), (0, D - N_HEADS))))
    rows.append(jnp.zeros((1, D), F32) if loss_row is None else loss_row)
    rows.append(jnp.zeros((ROW_WS - ROW_LOSS - 1, D), F32))
    rows.append(vals["sgu_w_s"].reshape(SGU_LEN, D))
    return jnp.concatenate(rows, axis=0)


def _unpack_small(blob, D):
    out = {n: blob[r:r + 1] for r, n in enumerate(SMALL_VECS)}
    out["sgu_b_s"] = blob[ROW_BS].reshape(1, N_GROUPS, SGU_LEN)
    out["b_forget"] = blob[ROW_BF, :N_HEADS].reshape(1, N_HEADS)
    out["sgu_w_s"] = blob[ROW_WS:].reshape(1, N_GROUPS, SGU_LEN, SGU_LEN)
    return out


WEIGHT_NAMES = ("ffn1_pre_g", "ffn1_w_gate", "ffn1_w_up", "ffn1_w_down", "ffn1_post_g", "mix_pre_g", "w_in",
                "b_forget", "sgu_ln_g", "sgu_ln_b", "sgu_w_s", "sgu_b_s", "w_out", "mix_post_g", "ffn2_pre_g",
                "ffn2_w_gate", "ffn2_w_up", "ffn2_w_down", "ffn2_post_g")
BIG_NAMES = ("ffn1_w_gate", "ffn1_w_up", "ffn1_w_down", "w_in", "w_out", "ffn2_w_gate", "ffn2_w_up", "ffn2_w_down")
WEIGHT_GROUPS = {"ffn1": ("ffn1_w_gate", "ffn1_w_up", "ffn1_w_down"), "mix": ("w_in", "w_out"),
                 "ffn2": ("ffn2_w_gate", "ffn2_w_up", "ffn2_w_down")}
GRAD_GROUPS = (("ffn2_w_gate", "ffn2_w_up", "ffn2_w_down"), ("w_in", "w_out"), ("ffn1_w_down",), ("ffn1_w_gate",),
               ("ffn1_w_up",))


def _local_step(x, target, small, fetch, emit, consume):
    T, D = x.shape
    W = N_HEADS * HEAD_DIM
    vec = lambda n: small[n].reshape(1, D)
    big = dict(fetch("ffn1", x))

    x1, y1, dgf1, silu1, act1 = _ffn_fwd(x, vec("ffn1_pre_g"), big["ffn1_w_gate"], big["ffn1_w_up"], big["ffn1_w_down"],
                                  vec("ffn1_post_g"), "ffn1_fwd")

    big.update(fetch("mix", x1))
    w_in_all = big["w_in"]
    in_width = N_DEV * w_in_all.shape[2]
    w_in = w_in_all.transpose(1, 0, 2).reshape(D, in_width)
    col_f = 3 * W
    col_u = col_f + N_HEADS
    seg_starts = (0, W, 2 * W, col_u, col_u + W, col_u + 2 * W, col_u + 3 * W)
    w7 = jnp.stack([w_in[:, s:s + W] for s in seg_starts])
    wf = jnp.pad(w_in[:, col_f:col_u], ((0, 0), (0, LANES - N_HEADS)))
    w_out = big["w_out"].reshape(D, D)
    b_pad = jnp.pad(small["b_forget"].reshape(1, N_HEADS), ((0, 0), (0, LANES - N_HEADS)))
    lng, lnb = vec("sgu_ln_g"), vec("sgu_ln_b")
    ws = small["sgu_w_s"].reshape(N_GROUPS, SGU_LEN, SGU_LEN)
    bs = small["sgu_b_s"].reshape(N_GROUPS, SGU_LEN, 1)

    z7, f_logit, h2b = _mix_in_fwd(x1, vec("mix_pre_g"), w7, wf, "mix_in_fwd")
    c = _forget_cumsum(f_logit, b_pad, "forget_cumsum")
    c_heads = c[:, :N_HEADS].T
    ta, _, n_chunks = _attn_geometry(T)
    vt = z7[2].reshape(n_chunks, ta, N_HEADS, HEAD_DIM).transpose(2, 0, 3, 1)
    c_rep = jnp.broadcast_to(c_heads[:, :, None] * LOG2E, (N_HEADS, T, LANES))
    o_a, lse_chunks = _attn_fwd_keys_on_rows(z7, vt, c_rep, "attn_fwd")
    x2, p, merged_b = _mix_out_fwd(z7, o_a, x1, lng, lnb, ws, bs, w_out, vec("mix_post_g"), "mix_out_fwd")
    big.update(fetch("ffn2", x2))
    x3, y2, dgf2, silu2, act2 = _ffn_fwd(x2, vec("ffn2_pre_g"), big["ffn2_w_gate"], big["ffn2_w_up"], big["ffn2_w_down"],
                                  vec("ffn2_post_g"), "ffn2_fwd")
    dy, loss_lanes = _loss_head(x3, target, "loss_head")

    grads_small = {}

    dx2, h3b, dy2b, dgate2, dup2, dgpre, dgpost = _ffn_bwd(
        dy, x2, y2, dgf2, silu2, vec("ffn2_pre_g"), big["ffn2_w_gate"], big["ffn2_w_up"], big["ffn2_w_down"],
        vec("ffn2_post_g"), "ffn2_bwd")
    grads_small["ffn2_pre_g"] = jnp.sum(dgpre, axis=0)
    grads_small["ffn2_post_g"] = jnp.sum(dgpost, axis=0)
    dep = emit("ffn2_w_gate", _wgrad(h3b, dgate2, "ffn2_wgrad_gate", shard_cols=True))
    dep = emit("ffn2_w_up", _wgrad(h3b, dup2, "ffn2_wgrad_up", shard_cols=True, dep=dep))
    dep = emit("ffn2_w_down", _wgrad(act2, dy2b, "ffn2_wgrad_down", dep=dep).reshape(big["ffn2_w_down"].shape))

    dpb, dob, dvec, dz4, dgp, dlng, dlnb, dws, dbs = _mix_out_bwd(
        dx2, p, z7, o_a, lng, lnb, ws, bs, w_out, vec("mix_post_g"), "mix_out_bwd", dep=dep)
    grads_small["mix_post_g"] = dgp
    grads_small["sgu_ln_g"] = dlng
    grads_small["sgu_ln_b"] = dlnb
    grads_small["sgu_w_s"] = dws
    grads_small["sgu_b_s"] = dbs
    d_chunks = dvec.reshape(N_HEADS, n_chunks, 1, ta)
    kt = z7[1].reshape(n_chunks, ta, N_HEADS, HEAD_DIM).transpose(2, 0, 3, 1)
    dk, dv, dc, dq, dc_q = _attn_bwd_fused(z7, kt, dob, c_rep, lse_chunks, d_chunks, "attn_bwd")
    dc_pad = jnp.pad((dc + dc_q).reshape(N_HEADS, T).T, ((0, 0), (0, LANES - N_HEADS)))
    dfb, dbf = _forget_bwd(dc_pad, f_logit, b_pad, "forget_bwd")
    grads_small["b_forget"] = dbf[:, :N_HEADS]
    segs = [(dq, None), (dk, None), (dv, None), (dz4, 0), (dz4, 1), (dz4, 2), (dz4, 3)]
    dep = consume(("ffn2_w_gate", "ffn2_w_up", "ffn2_w_down"))
    dx1, dgm = _mix_in_bwd(dx2, x1, vec("mix_pre_g"), segs, dfb, w7, wf, "mix_in_bwd", dep=dep)
    grads_small["mix_pre_g"] = jnp.sum(dgm, axis=0)
    dw_qkv = _wgrad_multi(h2b, segs[:3], "w_in_wgrad_qkv", dep=dx1)
    dw_rest = _wgrad_multi(h2b, segs[3:], "w_in_wgrad_gates", dep=dw_qkv)
    dw_seg = [dw_qkv[:, q * W:(q + 1) * W] for q in range(3)] + [dw_rest[:, q * W:(q + 1) * W] for q in range(4)]
    dwf = _wgrad(h2b, dfb, "w_in_wgrad_f", dep=dw_rest)
    dw_in = jnp.concatenate(dw_seg[:3] + [dwf[:, :N_HEADS]] + dw_seg[3:], axis=1)
    emit("w_in", dw_in.reshape(D, N_DEV, in_width // N_DEV).transpose(1, 0, 2))
    dep = emit("w_out", _wgrad(merged_b, dpb, "w_out_wgrad", dep=dwf).reshape(big["w_out"].shape))

    dx0, h1b, dy1b, dgate1, dup1, dgpre1, dgpost1 = _ffn_bwd(
        dx1, x, y1, dgf1, silu1, vec("ffn1_pre_g"), big["ffn1_w_gate"], big["ffn1_w_up"], big["ffn1_w_down"],
        vec("ffn1_post_g"), "ffn1_bwd", dep=dep)
    grads_small["ffn1_pre_g"] = jnp.sum(dgpre1, axis=0)
    grads_small["ffn1_post_g"] = jnp.sum(dgpost1, axis=0)
    dep = consume(("w_in", "w_out"))
    dep = emit("ffn1_w_down", _wgrad(act1, dy1b, "ffn1_wgrad_down", dep=dep).reshape(big["ffn1_w_down"].shape))
    dep = emit("ffn1_w_gate", _wgrad(h1b, dgate1, "ffn1_wgrad_gate", shard_cols=True, dep=dep))
    dep = emit("ffn1_w_up", _wgrad(h1b, dup1, "ffn1_wgrad_up", shard_cols=True, dep=dep))

    loss_row = jnp.pad(loss_lanes, ((0, 0), (0, D - LANES)))
    return loss_row, dx0, grads_small


def kernel(x, ffn1_pre_g, ffn1_w_gate, ffn1_w_up, ffn1_w_down, ffn1_post_g, mix_pre_g, w_in, b_forget, sgu_ln_g, sgu_ln_b, sgu_w_s, sgu_b_s, w_out, mix_post_g, ffn2_pre_g, ffn2_w_gate, ffn2_w_up, ffn2_w_down, ffn2_post_g, loss_target, m_ffn1_pre_g, m_ffn1_w_gate, m_ffn1_w_up, m_ffn1_w_down, m_ffn1_post_g, m_mix_pre_g, m_w_in, m_b_forget, m_sgu_ln_g, m_sgu_ln_b, m_sgu_w_s, m_sgu_b_s, m_w_out, m_mix_post_g, m_ffn2_pre_g, m_ffn2_w_gate, m_ffn2_w_up, m_ffn2_w_down, m_ffn2_post_g, v_ffn1_pre_g, v_ffn1_w_gate, v_ffn1_w_up, v_ffn1_w_down, v_ffn1_post_g, v_mix_pre_g, v_w_in, v_b_forget, v_sgu_ln_g, v_sgu_ln_b, v_sgu_w_s, v_sgu_b_s, v_w_out, v_mix_post_g, v_ffn2_pre_g, v_ffn2_w_gate, v_ffn2_w_up, v_ffn2_w_down, v_ffn2_post_g):
    weights = dict(zip(WEIGHT_NAMES, (ffn1_pre_g, ffn1_w_gate, ffn1_w_up, ffn1_w_down, ffn1_post_g, mix_pre_g, w_in,
                                      b_forget, sgu_ln_g, sgu_ln_b, sgu_w_s, sgu_b_s, w_out, mix_post_g, ffn2_pre_g,
                                      ffn2_w_gate, ffn2_w_up, ffn2_w_down, ffn2_post_g)))
    mom1 = dict(zip(WEIGHT_NAMES, (m_ffn1_pre_g, m_ffn1_w_gate, m_ffn1_w_up, m_ffn1_w_down, m_ffn1_post_g,
                                   m_mix_pre_g, m_w_in, m_b_forget, m_sgu_ln_g, m_sgu_ln_b, m_sgu_w_s, m_sgu_b_s,
                                   m_w_out, m_mix_post_g, m_ffn2_pre_g, m_ffn2_w_gate, m_ffn2_w_up, m_ffn2_w_down,
                                   m_ffn2_post_g)))
    mom2 = dict(zip(WEIGHT_NAMES, (v_ffn1_pre_g, v_ffn1_w_gate, v_ffn1_w_up, v_ffn1_w_down, v_ffn1_post_g,
                                   v_mix_pre_g, v_w_in, v_b_forget, v_sgu_ln_g, v_sgu_ln_b, v_sgu_w_s, v_sgu_b_s,
                                   v_w_out, v_mix_post_g, v_ffn2_pre_g, v_ffn2_w_gate, v_ffn2_w_up, v_ffn2_w_down,
                                   v_ffn2_post_g)))
    D = x.shape[-1]
    small_names = [n for n in WEIGHT_NAMES if n not in BIG_NAMES]

    small = {n: weights[n] for n in small_names}
    shard = lambda n: weights[n][0].astype(BF16)

    ffn1_full = _all_gather([shard(n) for n in WEIGHT_GROUPS["ffn1"]], "ffn1_all_gather")
    gathered = {}
    for cid, grp in ((1, "mix"), (2, "ffn2")):
        shards, _ = lax.optimization_barrier(([shard(n) for n in WEIGHT_GROUPS[grp]], ffn1_full[0]))
        gathered[grp] = _sequencer_exchange(shards, grp + "_gather", True, cid)

    def fetch(group, after):
        if group == "ffn1":
            return zip(WEIGHT_GROUPS[group], ffn1_full)
        arrived, _ = lax.optimization_barrier((gathered[group], after))
        return zip(WEIGHT_GROUPS[group], arrived)

    ready, received = {}, {}

    def emit(name, part):
        ready[name] = part
        for gi, group in enumerate(GRAD_GROUPS):
            if name == group[-1]:
                lands = _sequencer_exchange([ready[n] for n in group], name + "_grad_exchange", False, 3 + gi)
                received.update(zip(group, lands))
        return part

    out = {}

    def consume(names, dep=None):
        for n in names:
            g, d, m_new, v_new = _sum_adamw(received[n], weights[n][0], mom1[n][0], mom2[n][0], "adamw_" + n, dep=dep)
            out[n] = tuple(a[None] for a in (g, d, m_new, v_new))
            dep = g
        return dep

    loss_row, grad_x, grads_small = _local_step(x[0], loss_target[0], small, fetch, emit, consume)

    blobs = _sequencer_exchange([_pack_small(grads_small, D, loss_row)], "small_gather", True,
                                3 + len(GRAD_GROUPS))[0]
    blob, d_blob, m_blob, v_blob = _sum_adamw(
        blobs, _pack_small(small, D), _pack_small({n: mom1[n] for n in small_names}, D),
        _pack_small({n: mom2[n] for n in small_names}, D), "adamw_small")
    consume(("ffn1_w_down", "ffn1_w_gate", "ffn1_w_up"), dep=blob)
    unpacked = [_unpack_small(b, D) for b in (blob, d_blob, m_blob, v_blob)]
    for n in small_names:
        out[n] = tuple(u[n].reshape(weights[n].shape) for u in unpacked)

    loss = blob[ROW_LOSS, 0]
    result = [loss, grad_x[None]]
    for k in range(4):
        result += [out[n][k] for n in WEIGHT_NAMES]
    return tuple(result)
```

```python
import numpy as np
import jax
import jax.numpy as jnp
from jax import lax
from jax.experimental import pallas as pl
from jax.experimental.pallas import tpu as pltpu
from jax.experimental.pallas import tpu_sc as plsc

F32 = jnp.float32
BF16 = jnp.bfloat16

RMS_EPS = 1e-6
LN_EPS = 1e-5
HEAD_DIM = 128
N_HEADS = 8
GROUP_DIM = 128
N_GROUPS = 8
SGU_LEN = 128
CHUNK = 64
N_DEV = 8
LANES = 128
VMEM_LIMIT = 56 * 1024 * 1024
NEG_BIG = -1e30
LOG2E = np.float32(1.0 / np.log(2.0))

ADAM_LR = 0.001
ADAM_B1 = 0.9
ADAM_B2 = 0.999
ADAM_EPS = 1e-08
ADAM_WD = 0.01
ADAM_STEP = 10

MESH = pl.DeviceIdType.MESH
ANY = pl.BlockSpec(memory_space=pl.ANY)


def _blk(n, pref):
    return pref if (n >= pref and n % pref == 0) else n


def _mm(a, b):
    return jnp.dot(a, b, preferred_element_type=F32)


def _mm_nt(a, b):
    return lax.dot_general(a, b, (((1,), (1,)), ((), ())), preferred_element_type=F32)


def _mm_tn(a, b):
    return lax.dot_general(a, b, (((0,), (0,)), ((), ())), preferred_element_type=F32)


def _params(sem):
    return pltpu.CompilerParams(dimension_semantics=sem, vmem_limit_bytes=VMEM_LIMIT)


def _gelu(x):
    return 0.5 * x * (1.0 + lax.erf(x * np.float32(1.0 / np.sqrt(2.0))))


def _gelu_grad(x):
    cdf = 0.5 * (1.0 + lax.erf(x * np.float32(1.0 / np.sqrt(2.0))))
    return cdf + x * jnp.exp(-0.5 * x * x) * np.float32(1.0 / np.sqrt(2.0 * np.pi))


def _rms_scale(v):
    return lax.rsqrt(jnp.mean(v * v, axis=-1, keepdims=True) + RMS_EPS)


def _rms_bwd(dy, xhat, r, g):
    dxh = dy * g
    return r * (dxh - xhat * jnp.mean(dxh * xhat, axis=-1, keepdims=True))


def _ffn_rows(T):
    tm = _blk(T, 1024)
    th = _blk(tm, 512)
    return tm, th, tm // th


def _ffn_fwd(x, g_pre, wg, wu, wd, g_post, name):
    T, D = x.shape
    ns, _, fs = wg.shape
    tm, th, parts = _ffn_rows(T)

    def body(x_ref, gpre_ref, wg_ref, wu_ref, wd_ref, gpost_ref, xo_ref, y_ref, dgf_ref, silu_ref, act_ref,
             h_scr, acc_scr):
        j = pl.program_id(1)

        @pl.when(j == 0)
        def _():
            for r in range(parts):
                rows = slice(r * th, (r + 1) * th)
                xv = x_ref[rows, :]
                h_scr[rows, :] = (xv * _rms_scale(xv) * gpre_ref[...]).astype(BF16)
            acc_scr[...] = jnp.zeros_like(acc_scr)

        pre = []
        for r in range(parts):
            h = h_scr[r * th:(r + 1) * th, :]
            pre.append((_mm(h, wg_ref[...]), _mm(h, wu_ref[...])))
        for r in range(parts):
            rows = slice(r * th, (r + 1) * th)
            gg, uu = pre[r]
            sg = jax.nn.sigmoid(gg)
            silu = gg * sg
            act = (silu * uu).astype(BF16)
            dgf_ref[rows, :] = (uu * (sg * (1.0 + gg * (1.0 - sg)))).astype(BF16)
            silu_ref[rows, :] = silu.astype(BF16)
            act_ref[rows, :] = act
            acc_scr[rows, :] += _mm(act, wd_ref[...])

        @pl.when(j == ns - 1)
        def _():
            for r in range(parts):
                rows = slice(r * th, (r + 1) * th)
                y = acc_scr[rows, :]
                y_ref[rows, :] = y
                xo_ref[rows, :] = x_ref[rows, :] + 0.5 * (y * _rms_scale(y) * gpost_ref[...])

    row = pl.BlockSpec((tm, D), lambda i, j: (i, 0), pipeline_mode=pl.Buffered(1))
    vec = pl.BlockSpec((1, D), lambda i, j: (0, 0))
    return pl.pallas_call(
        body, name=name, grid=(T // tm, ns),
        in_specs=[pl.BlockSpec((tm, D), lambda i, j: (i, 0)), vec,
                  pl.BlockSpec((None, D, fs), lambda i, j: (j, 0, 0)),
                  pl.BlockSpec((None, D, fs), lambda i, j: (j, 0, 0)),
                  pl.BlockSpec((None, fs, D), lambda i, j: (j, 0, 0)),
                  vec],
        out_specs=[row, row] + [pl.BlockSpec((tm, fs), lambda i, j: (i, j))] * 3,
        out_shape=[jax.ShapeDtypeStruct((T, D), F32), jax.ShapeDtypeStruct((T, D), F32)]
        + [jax.ShapeDtypeStruct((T, ns * fs), BF16)] * 3,
        scratch_shapes=[pltpu.VMEM((tm, D), BF16), pltpu.VMEM((tm, D), F32)],
        compiler_params=_params(("parallel", "arbitrary")),
    )(x, g_pre, wg, wu, wd, g_post)


def _after(dep):
    return jnp.zeros((8, LANES), F32) if dep is None else dep


def _ffn_bwd(dxo, x, y, dgf, silu, g_pre, wg, wu, wd, g_post, name, dep=None):
    T, D = x.shape
    ns, _, fs = wg.shape
    tm, th, parts = _ffn_rows(T)
    n_i = T // tm

    def body(dxo_ref, x_ref, y_ref, dgf_ref, silu_ref, gpre_ref, wg_ref, wu_ref, wd_ref, gpost_ref, _,
             dx_ref, hb_ref, dyb_ref, dgb_ref, dub_ref, dgpre_ref, dgpost_ref, dy_scr, acc_scr):
        j = pl.program_id(1)

        @pl.when(j == 0)
        def _():
            dgpost = jnp.zeros((1, D), F32)
            for r in range(parts):
                rows = slice(r * th, (r + 1) * th)
                yv = y_ref[rows, :]
                s = _rms_scale(yv)
                n = yv * s
                dn = 0.5 * dxo_ref[rows, :]
                dgpost = dgpost + jnp.sum(dn * n, axis=0, keepdims=True)
                dyv = _rms_bwd(dn, n, s, gpost_ref[...]).astype(BF16)
                dy_scr[rows, :] = dyv
                dyb_ref[rows, :] = dyv
                xv = x_ref[rows, :]
                hb_ref[rows, :] = (xv * _rms_scale(xv) * gpre_ref[...]).astype(BF16)
            dgpost_ref[...] = dgpost
            acc_scr[...] = jnp.zeros_like(acc_scr)

        das = [_mm_nt(dy_scr[r * th:(r + 1) * th, :], wd_ref[...]) for r in range(parts)]
        for r in range(parts):
            rows = slice(r * th, (r + 1) * th)
            dgate = (das[r] * dgf_ref[rows, :].astype(F32)).astype(BF16)
            dup = (das[r] * silu_ref[rows, :].astype(F32)).astype(BF16)
            dgb_ref[rows, :] = dgate
            dub_ref[rows, :] = dup
            acc_scr[rows, :] += _mm_nt(dgate, wg_ref[...]) + _mm_nt(dup, wu_ref[...])

        @pl.when(j == ns - 1)
        def _():
            dgpre = jnp.zeros((1, D), F32)
            for r in range(parts):
                rows = slice(r * th, (r + 1) * th)
                xv = x_ref[rows, :]
                rs = _rms_scale(xv)
                xhat = xv * rs
                dh = acc_scr[rows, :]
                dgpre = dgpre + jnp.sum(dh * xhat, axis=0, keepdims=True)
                dx_ref[rows, :] = _rms_bwd(dh, xhat, rs, gpre_ref[...]) + dxo_ref[rows, :]
            dgpre_ref[...] = dgpre

    row = pl.BlockSpec((tm, D), lambda i, j: (i, 0), pipeline_mode=pl.Buffered(1))
    vec = pl.BlockSpec((1, D), lambda i, j: (0, 0))
    wide = pl.BlockSpec((tm, fs), lambda i, j: (i, j))
    part = pl.BlockSpec((None, 1, D), lambda i, j: (i, 0, 0))
    F = ns * fs
    return pl.pallas_call(
        body, name=name, grid=(n_i, ns),
        in_specs=[row, row, row, wide, wide, vec,
                  pl.BlockSpec((None, D, fs), lambda i, j: (j, 0, 0)),
                  pl.BlockSpec((None, D, fs), lambda i, j: (j, 0, 0)),
                  pl.BlockSpec((None, fs, D), lambda i, j: (j, 0, 0)),
                  vec, ANY],
        out_specs=[row, row, row, wide, wide, part, part],
        out_shape=[jax.ShapeDtypeStruct((T, D), F32), jax.ShapeDtypeStruct((T, D), BF16),
                   jax.ShapeDtypeStruct((T, D), BF16), jax.ShapeDtypeStruct((T, F), BF16),
                   jax.ShapeDtypeStruct((T, F), BF16),
                   jax.ShapeDtypeStruct((n_i, 1, D), F32), jax.ShapeDtypeStruct((n_i, 1, D), F32)],
        scratch_shapes=[pltpu.VMEM((tm, D), BF16), pltpu.VMEM((tm, D), F32)],
        compiler_params=_params(("parallel", "arbitrary")),
    )(dxo, x, y, dgf, silu, g_pre, wg, wu, wd, g_post, _after(dep))


def _wgrad(xm, ym, name, shard_cols=False, dep=None):
    T, M = xm.shape
    N = ym.shape[-1]
    assert M * N * 4 <= 16 * 1024 * 1024, (M, N)
    tk = _blk(T, 512)
    n_k = T // tk
    fs = N // N_DEV

    def body(x_ref, y_ref, _, o_ref, acc_scr):
        k = pl.program_id(0)

        @pl.when(k == 0)
        def _():
            acc_scr[...] = jnp.zeros_like(acc_scr)

        acc_scr[...] += _mm_tn(x_ref[...], y_ref[...])

        @pl.when(k == n_k - 1)
        def _():
            if shard_cols:
                for s in range(N_DEV):
                    o_ref[s] = acc_scr[:, s * fs:(s + 1) * fs].astype(BF16)
            else:
                o_ref[...] = acc_scr[...].astype(BF16)

    if shard_cols:
        out_spec = pl.BlockSpec((N_DEV, M, fs), lambda k: (0, 0, 0), pipeline_mode=pl.Buffered(1))
        out_shape = jax.ShapeDtypeStruct((N_DEV, M, fs), BF16)
    else:
        out_spec = pl.BlockSpec((M, N), lambda k: (0, 0), pipeline_mode=pl.Buffered(1))
        out_shape = jax.ShapeDtypeStruct((M, N), BF16)
    return pl.pallas_call(
        body, name=name, grid=(n_k,),
        in_specs=[pl.BlockSpec((tk, M), lambda k: (k, 0)), pl.BlockSpec((tk, N), lambda k: (k, 0)), ANY],
        out_specs=out_spec, out_shape=out_shape,
        scratch_shapes=[pltpu.VMEM((M, N), F32)],
        compiler_params=_params(("arbitrary",)),
    )(xm, ym, _after(dep))


def _wgrad_multi(xm, segs, name, dep=None):
    T, M = xm.shape
    N = segs[0][0].shape[-1]
    n_seg = len(segs)
    assert M * N * n_seg * 4 <= 16 * 1024 * 1024, (M, N, n_seg)
    tk = _blk(T, 512)
    n_k = T // tk

    def body(*refs):
        x_ref, y_refs = refs[0], refs[1:1 + n_seg]
        o_ref, acc_scr = refs[2 + n_seg], refs[3 + n_seg]
        k = pl.program_id(0)

        @pl.when(k == 0)
        def _():
            acc_scr[...] = jnp.zeros_like(acc_scr)

        x = x_ref[...]
        for s in range(n_seg):
            acc_scr[:, s * N:(s + 1) * N] += _mm_tn(x, y_refs[s][...])

        @pl.when(k == n_k - 1)
        def _():
            o_ref[...] = acc_scr[...].astype(BF16)

    y_specs = [pl.BlockSpec((tk, N), lambda k: (k, 0)) if idx is None
               else pl.BlockSpec((None, tk, N), lambda k, idx=idx: (idx, k, 0)) for _, idx in segs]
    return pl.pallas_call(
        body, name=name, grid=(n_k,),
        in_specs=[pl.BlockSpec((tk, M), lambda k: (k, 0))] + y_specs + [ANY],
        out_specs=pl.BlockSpec((M, n_seg * N), lambda k: (0, 0), pipeline_mode=pl.Buffered(1)),
        out_shape=jax.ShapeDtypeStruct((M, n_seg * N), BF16),
        scratch_shapes=[pltpu.VMEM((M, n_seg * N), F32)],
        compiler_params=_params(("arbitrary",)),
    )(xm, *[arr for arr, _ in segs], _after(dep))


def _mix_in_fwd(x1, g, w7, wf, name):
    T, D = x1.shape
    n_seg, _, W = w7.shape
    tm = _blk(T, 512)

    def body(x_ref, g_ref, w_ref, wf_ref, z_ref, f_ref, hb_ref):
        xv = x_ref[...]
        h = (xv * _rms_scale(xv) * g_ref[...]).astype(BF16)
        hb_ref[...] = h
        f_ref[...] = _mm(h, wf_ref[...])
        for s in range(n_seg):
            z_ref[s] = _mm(h, w_ref[s]).astype(BF16)

    return pl.pallas_call(
        body, name=name, grid=(T // tm,),
        in_specs=[pl.BlockSpec((tm, D), lambda i: (i, 0)),
                  pl.BlockSpec((1, D), lambda i: (0, 0)),
                  pl.BlockSpec((n_seg, D, W), lambda i: (0, 0, 0), pipeline_mode=pl.Buffered(1)),
                  pl.BlockSpec((D, LANES), lambda i: (0, 0))],
        out_specs=[pl.BlockSpec((n_seg, tm, W), lambda i: (0, i, 0)),
                   pl.BlockSpec((tm, LANES), lambda i: (i, 0)),
                   pl.BlockSpec((tm, D), lambda i: (i, 0))],
        out_shape=[jax.ShapeDtypeStruct((n_seg, T, W), BF16), jax.ShapeDtypeStruct((T, LANES), F32),
                   jax.ShapeDtypeStruct((T, D), BF16)],
        compiler_params=_params(("parallel",)),
    )(x1, g, w7, wf)


def _mix_in_bwd(dx2, x1, g, segs, dfb, w7, wf, name, dep=None):
    T, D = x1.shape
    n_seg, _, W = w7.shape
    tm = _blk(T, 512)
    n_i = T // tm

    def body(*refs):
        dx2_ref, x_ref, g_ref = refs[:3]
        seg_refs = refs[3:3 + n_seg]
        df_ref, w_ref, wf_ref, _, dx1_ref, dg_ref = refs[3 + n_seg:]
        dh = _mm_nt(df_ref[...], wf_ref[...])
        for q in range(n_seg):
            dh = dh + _mm_nt(seg_refs[q][...], w_ref[q])
        xv = x_ref[...]
        r = _rms_scale(xv)
        xhat = xv * r
        dg_ref[...] = jnp.sum(dh * xhat, axis=0, keepdims=True)
        dx1_ref[...] = _rms_bwd(dh, xhat, r, g_ref[...]) + dx2_ref[...]

    row = pl.BlockSpec((tm, D), lambda i: (i, 0))
    seg_specs = []
    seg_args = []
    for arr, idx in segs:
        if idx is None:
            seg_specs.append(pl.BlockSpec((tm, W), lambda i: (i, 0)))
        else:
            seg_specs.append(pl.BlockSpec((None, tm, W), lambda i, idx=idx: (idx, i, 0)))
        seg_args.append(arr)
    return pl.pallas_call(
        body, name=name, grid=(n_i,),
        in_specs=[row, row, pl.BlockSpec((1, D), lambda i: (0, 0))] + seg_specs + [
            pl.BlockSpec((tm, LANES), lambda i: (i, 0)),
            pl.BlockSpec((n_seg, D, W), lambda i: (0, 0, 0), pipeline_mode=pl.Buffered(1)),
            pl.BlockSpec((D, LANES), lambda i: (0, 0)), ANY],
        out_specs=[row, pl.BlockSpec((None, 1, D), lambda i: (i, 0, 0))],
        out_shape=[jax.ShapeDtypeStruct((T, D), F32), jax.ShapeDtypeStruct((n_i, 1, D), F32)],
        compiler_params=_params(("parallel",)),
    )(dx2, x1, g, *seg_args, dfb, w7, wf, _after(dep))


def _forget_cumsum(f, b_pad, name):
    T, L = f.shape
    tb = _blk(T, 256)

    def body(f_ref, b_ref, c_ref, carry):
        @pl.when(pl.program_id(0) == 0)
        def _():
            carry[...] = jnp.zeros_like(carry)

        lf = jax.nn.log_sigmoid(f_ref[...] + b_ref[...])
        rows = lax.broadcasted_iota(jnp.int32, (tb, tb), 0)
        cols = lax.broadcasted_iota(jnp.int32, (tb, tb), 1)
        tri = (cols <= rows).astype(F32)
        c = jnp.dot(tri, lf, preferred_element_type=F32, precision=lax.Precision.HIGHEST) + carry[...]
        carry[...] = c[tb - 1:tb, :]
        for h in range(N_HEADS):
            c_ref[h] = jnp.broadcast_to(c[:, h:h + 1] * LOG2E, (tb, L))

    return pl.pallas_call(
        body, name=name, grid=(T // tb,),
        in_specs=[pl.BlockSpec((tb, L), lambda i: (i, 0)), pl.BlockSpec((1, L), lambda i: (0, 0))],
        out_specs=pl.BlockSpec((N_HEADS, tb, L), lambda i: (0, i, 0)),
        out_shape=jax.ShapeDtypeStruct((N_HEADS, T, L), F32),
        scratch_shapes=[pltpu.VMEM((1, L), F32)],
        compiler_params=_params(("arbitrary",)),
    )(f, b_pad)


def _forget_bwd(dc, f, b_pad, name):
    T, L = f.shape
    tb = _blk(T, 256)
    nb = T // tb

    def body(dc_ref, f_ref, b_ref, df_ref, db_ref, carry):
        @pl.when(pl.program_id(0) == 0)
        def _():
            carry[...] = jnp.zeros_like(carry)
            db_ref[...] = jnp.zeros_like(db_ref)

        rows = lax.broadcasted_iota(jnp.int32, (tb, tb), 0)
        cols = lax.broadcasted_iota(jnp.int32, (tb, tb), 1)
        tri = (cols >= rows).astype(F32)
        r = jnp.dot(tri, dc_ref[...], preferred_element_type=F32, precision=lax.Precision.HIGHEST) + carry[...]
        carry[...] = r[0:1, :]
        df = r * (1.0 - jax.nn.sigmoid(f_ref[...] + b_ref[...]))
        df_ref[...] = df.astype(BF16)
        db_ref[...] += jnp.sum(df, axis=0, keepdims=True)

    rev = pl.BlockSpec((tb, L), lambda i: (nb - 1 - i, 0))
    one = pl.BlockSpec((1, L), lambda i: (0, 0))
    return pl.pallas_call(
        body, name=name, grid=(nb,),
        in_specs=[rev, rev, one], out_specs=[rev, one],
        out_shape=[jax.ShapeDtypeStruct((T, L), BF16), jax.ShapeDtypeStruct((1, L), F32)],
        scratch_shapes=[pltpu.VMEM((1, L), F32)],
        compiler_params=_params(("arbitrary",)),
    )(dc, f, b_pad)


ATTN_TILE = 512
ATTN_CHAINS = 4


def _attn_geometry(T):
    ta = _blk(T, ATTN_TILE)
    nc = ATTN_CHAINS if (T // ta) % ATTN_CHAINS == 0 else 1
    return ta, nc, T // ta


def _causal_tile(ta, keys_on_rows=False):
    rows = lax.broadcasted_iota(jnp.int32, (ta, ta), 0)
    cols = lax.broadcasted_iota(jnp.int32, (ta, ta), 1)
    return rows <= cols if keys_on_rows else cols <= rows


def _chunk(ref, j, ta):
    return ref[pl.ds(pl.multiple_of(j * ta, ta), ta), :]


def _attn_fwd_keys_on_rows(z7, vt, c_rep, name):
    _, T, W = z7.shape
    H = W // HEAD_DIM
    ta, nc, n_chunks = _attn_geometry(T)
    scale = np.float32(1.0 / np.sqrt(HEAD_DIM))
    reps = ta // LANES

    def body(q_ref, k_ref, vt_ref, c_ref, o_ref, lse_ref):
        g = pl.program_id(1)

        def scores(ch, k):
            return _mm_nt(k, q_ref[ch * ta:(ch + 1) * ta, :])

        def update(state, raw, vt, cj, diagonal):
            m_prev, l_prev, acc_prev = state
            st = raw * (scale * LOG2E) - cj
            if diagonal:
                st = jnp.where(_causal_tile(ta, keys_on_rows=True), st, NEG_BIG)
            m_new = jnp.maximum(m_prev, jnp.max(st, axis=0, keepdims=True))
            alpha = jnp.exp2(m_prev - m_new)
            pt = jnp.exp2(st - m_new)
            l_new = alpha * l_prev + jnp.sum(pt, axis=0, keepdims=True)
            acc_new = alpha * acc_prev + _mm(vt, pt.astype(BF16))
            return m_new, l_new, acc_new

        def load(j):
            cj = _chunk(c_ref, j, ta)
            return _chunk(k_ref, j, ta), vt_ref[j], jnp.concatenate([cj] * reps, axis=1)

        def full_chunk(j, states):
            k, vt, cj = load(j)
            raws = [scores(ch, k) for ch in range(nc)]
            return tuple(update(states[ch], raws[ch], vt, cj, False) for ch in range(nc))

        first = (jnp.full((1, ta), NEG_BIG, F32), jnp.zeros((1, ta), F32), jnp.zeros((HEAD_DIM, ta), F32))
        states = list(lax.fori_loop(0, nc * g, full_chunk, (first,) * nc))
        for jj in range(nc):
            k, vt, cj = load(nc * g + jj)
            raws = {ch: scores(ch, k) for ch in range(jj, nc)}
            for ch in range(jj, nc):
                states[ch] = update(states[ch], raws[ch], vt, cj, ch == jj)
        for ch in range(nc):
            m, l, acc = states[ch]
            o_ref[ch * ta:(ch + 1) * ta, :] = (acc / l).T
            lse_ref[ch] = m + jnp.log2(l)

    tq = nc * ta
    return pl.pallas_call(
        body, name=name, grid=(H, n_chunks // nc),
        in_specs=[pl.BlockSpec((None, tq, HEAD_DIM), lambda h, g: (0, g, h)),
                  pl.BlockSpec((None, T, HEAD_DIM), lambda h, g: (1, 0, h)),
                  pl.BlockSpec((None, n_chunks, HEAD_DIM, ta), lambda h, g: (h, 0, 0, 0)),
                  pl.BlockSpec((None, T, LANES), lambda h, g: (h, 0, 0))],
        out_specs=[pl.BlockSpec((tq, HEAD_DIM), lambda h, g: (g, h)),
                   pl.BlockSpec((None, nc, 1, ta), lambda h, g: (h, g, 0, 0))],
        out_shape=[jax.ShapeDtypeStruct((T, W), F32), jax.ShapeDtypeStruct((H, n_chunks, 1, ta), F32)],
        compiler_params=_params(("parallel", "arbitrary")),
    )(z7, z7, vt, c_rep)


def _attn_bwd_fused(z7, kt, dob, c_rep, lse_chunks, d_chunks, name):
    _, T, W = z7.shape
    H = W // HEAD_DIM
    ta, nc, n_chunks = _attn_geometry(T)
    n_steps = n_chunks // nc
    scale = np.float32(1.0 / np.sqrt(HEAD_DIM))
    reps = ta // LANES

    def body(k_ref, v_ref, kt_ref, q_ref, do_ref, c_ref, lse_ref, d_ref,
             dk_ref, dv_ref, dck_ref, dq_ref, dcq_ref, dk_scr, dv_scr, dck_scr, dqt_scr, dcq_scr):
        g = pl.program_id(1)

        @pl.when(g == 0)
        def _():
            dqt_scr[...] = jnp.zeros_like(dqt_scr)
            dcq_scr[...] = jnp.zeros_like(dcq_scr)

        dk_scr[...] = jnp.zeros_like(dk_scr)
        dv_scr[...] = jnp.zeros_like(dv_scr)
        dck_scr[...] = jnp.zeros_like(dck_scr)

        def products(ch, q, do):
            rows = slice(ch * ta, (ch + 1) * ta)
            return _mm_nt(k_ref[rows, :], q), _mm_nt(v_ref[rows, :], do)

        def update(ch, i, q, do, prods, diagonal):
            rows = slice(ch * ta, (ch + 1) * ta)
            cj = c_ref[rows, :]
            st = prods[0] * (scale * LOG2E) - jnp.concatenate([cj] * reps, axis=1) - lse_ref[i]
            if diagonal:
                st = jnp.where(_causal_tile(ta, keys_on_rows=True), st, NEG_BIG)
            pt = jnp.exp2(st)
            dv_scr[ch] += _mm(pt.astype(BF16), do)
            dst = pt * (prods[1] - d_ref[i])
            dst_b = dst.astype(BF16)
            dk_scr[ch] += _mm(dst_b, q)
            dqt_scr[i] += _mm(kt_ref[ch], dst_b)
            dcq_scr[i] += jnp.sum(dst, axis=0, keepdims=True)
            lane_sum = dst[:, :LANES]
            for r in range(1, reps):
                lane_sum = lane_sum + dst[:, r * LANES:(r + 1) * LANES]
            dck_scr[ch] += lane_sum

        for ii in range(nc):
            i = nc * g + ii
            q = _chunk(q_ref, i, ta)
            do = _chunk(do_ref, i, ta)
            prods = [products(ch, q, do) for ch in range(0, ii + 1)]
            for ch in range(0, ii + 1):
                update(ch, i, q, do, prods[ch], ch == ii)

        def full_chunk(i, carry):
            q = _chunk(q_ref, i, ta)
            do = _chunk(do_ref, i, ta)
            prods = [products(ch, q, do) for ch in range(nc)]
            for ch in range(nc):
                update(ch, i, q, do, prods[ch], False)
            return carry

        lax.fori_loop(nc * (g + 1), n_chunks, full_chunk, 0)
        for ch in range(nc):
            rows = slice(ch * ta, (ch + 1) * ta)
            dk_ref[rows, :] = (dk_scr[ch] * scale).astype(BF16)
            dv_ref[rows, :] = dv_scr[ch].astype(BF16)
            ones = jnp.ones((8, LANES), F32)
            sums = lax.dot_general(ones, dck_scr[ch], (((1,), (1,)), ((), ())), preferred_element_type=F32,
                                   precision=lax.Precision.HIGHEST)
            dck_ref[ch] = -sums[0:1, :]

        @pl.when(g == n_steps - 1)
        def _():
            for i in range(n_chunks):
                dq_ref[i * ta:(i + 1) * ta, :] = (dqt_scr[i] * scale).T.astype(BF16)
            dcq_ref[...] = dcq_scr[...]

    tk = nc * ta
    chunks = pl.BlockSpec((None, n_chunks, 1, ta), lambda h, g: (h, 0, 0, 0))
    tile = pl.BlockSpec((tk, HEAD_DIM), lambda h, g: (g, h))
    return pl.pallas_call(
        body, name=name, grid=(H, n_steps),
        in_specs=[pl.BlockSpec((None, tk, HEAD_DIM), lambda h, g: (1, g, h)),
                  pl.BlockSpec((None, tk, HEAD_DIM), lambda h, g: (2, g, h)),
                  pl.BlockSpec((None, nc, HEAD_DIM, ta), lambda h, g: (h, g, 0, 0)),
                  pl.BlockSpec((None, T, HEAD_DIM), lambda h, g: (0, 0, h)),
                  pl.BlockSpec((T, HEAD_DIM), lambda h, g: (0, h)),
                  pl.BlockSpec((None, tk, LANES), lambda h, g: (h, g, 0)),
                  chunks, chunks],
        out_specs=[tile, tile, pl.BlockSpec((None, nc, 1, ta), lambda h, g: (h, g, 0, 0)),
                   pl.BlockSpec((T, HEAD_DIM), lambda h, g: (0, h)), chunks],
        out_shape=[jax.ShapeDtypeStruct((T, W), BF16), jax.ShapeDtypeStruct((T, W), BF16),
                   jax.ShapeDtypeStruct((H, n_chunks, 1, ta), F32), jax.ShapeDtypeStruct((T, W), BF16),
                   jax.ShapeDtypeStruct((H, n_chunks, 1, ta), F32)],
        scratch_shapes=[pltpu.VMEM((nc, ta, HEAD_DIM), F32), pltpu.VMEM((nc, ta, HEAD_DIM), F32),
                        pltpu.VMEM((nc, ta, LANES), F32), pltpu.VMEM((n_chunks, HEAD_DIM, ta), F32),
                        pltpu.VMEM((n_chunks, 1, ta), F32)],
        compiler_params=_params(("parallel", "arbitrary")),
    )(z7, z7, kt, z7, dob, c_rep, lse_chunks, d_chunks)


def _chunk_causal_mask():
    rows = lax.broadcasted_iota(jnp.int32, (SGU_LEN, SGU_LEN), 0)
    cols = lax.broadcasted_iota(jnp.int32, (SGU_LEN, SGU_LEN), 1)
    return (cols // CHUNK) <= (rows // CHUNK)


def _sgu_norm_mix(sv, lng_ref, lnb_ref, ws_ref, bs_ref, vn_scr, mixed_scr, vhat_scr=None):
    tm = sv.shape[0]
    vs = _gelu(sv)
    mask = _chunk_causal_mask()
    rstds = []
    for g in range(N_GROUPS):
        lanes = slice(g * GROUP_DIM, (g + 1) * GROUP_DIM)
        blk = vs[:, lanes]
        cen = blk - jnp.mean(blk, axis=-1, keepdims=True)
        rstd = lax.rsqrt(jnp.mean(cen * cen, axis=-1, keepdims=True) + LN_EPS)
        vhat = cen * rstd
        rstds.append(rstd)
        if vhat_scr is not None:
            vhat_scr[:, lanes] = vhat
        vn_scr[:, lanes] = (vhat * lng_ref[:, lanes] + lnb_ref[:, lanes]).astype(BF16)
        wm = jnp.where(mask, ws_ref[g], 0.0).astype(BF16)
        for w in range(tm // SGU_LEN):
            rows = slice(w * SGU_LEN, (w + 1) * SGU_LEN)
            mixed_scr[rows, lanes] = _mm(wm, vn_scr[rows, lanes]) + bs_ref[g]
    return rstds


def _mix_out_fwd(z7, o_a, x1, lng, lnb, ws, bs, w_out, g_post, name):
    _, T, W = z7.shape
    D = x1.shape[1]
    tm = _blk(T, 256)

    def body(u_ref, sv_ref, ga_ref, gb_ref, oa_ref, x1_ref, lng_ref, lnb_ref, ws_ref, bs_ref, wo_ref, gp_ref,
             x2_ref, p_ref, mb_ref, vn_scr, mixed_scr):
        _sgu_norm_mix(sv_ref[...].astype(F32), lng_ref, lnb_ref, ws_ref, bs_ref, vn_scr, mixed_scr)
        o_b = _gelu(u_ref[...].astype(F32)) * mixed_scr[...]
        merged = (jax.nn.sigmoid(ga_ref[...].astype(F32)) * oa_ref[...]
                  + jax.nn.sigmoid(gb_ref[...].astype(F32)) * o_b).astype(BF16)
        mb_ref[...] = merged
        p = _mm(merged, wo_ref[...])
        p_ref[...] = p
        x2_ref[...] = x1_ref[...] + p * _rms_scale(p) * gp_ref[...]

    def seg(idx):
        return pl.BlockSpec((None, tm, W), lambda i, idx=idx: (idx, i, 0))

    row = pl.BlockSpec((tm, D), lambda i: (i, 0))
    vec = pl.BlockSpec((1, D), lambda i: (0, 0))
    return pl.pallas_call(
        body, name=name, grid=(T // tm,),
        in_specs=[seg(3), seg(4), seg(5), seg(6), row, row, vec, vec,
                  pl.BlockSpec((N_GROUPS, SGU_LEN, SGU_LEN), lambda i: (0, 0, 0)),
                  pl.BlockSpec((N_GROUPS, SGU_LEN, 1), lambda i: (0, 0, 0)),
                  pl.BlockSpec((D, D), lambda i: (0, 0)), vec],
        out_specs=[row, row, row],
        out_shape=[jax.ShapeDtypeStruct((T, D), F32), jax.ShapeDtypeStruct((T, D), F32),
                   jax.ShapeDtypeStruct((T, D), BF16)],
        scratch_shapes=[pltpu.VMEM((tm, W), BF16), pltpu.VMEM((tm, W), F32)],
        compiler_params=_params(("parallel",)),
    )(z7, z7, z7, z7, o_a, x1, lng, lnb, ws, bs, w_out, g_post)


def _mix_out_bwd(dx2, p, z7, o_a, lng, lnb, ws, bs, w_out, g_post, name, dep=None):
    _, T, W = z7.shape
    D = dx2.shape[1]
    tm = _blk(T, 256)
    n_w = tm // SGU_LEN

    def body(dx2_ref, p_ref, u_ref, sv_ref, ga_ref, gb_ref, oa_ref, lng_ref, lnb_ref, ws_ref, bs_ref, wo_ref, gp_ref, _,
             dpb_ref, dob_ref, dvec_ref, dz_ref, dgp_ref, dlng_ref, dlnb_ref, dws_ref, dbs_ref,
             vn_scr, mixed_scr, vhat_scr, dmix_scr, dvn_scr):
        @pl.when(pl.program_id(0) == 0)
        def _():
            dgp_ref[...] = jnp.zeros_like(dgp_ref)
            dlng_ref[...] = jnp.zeros_like(dlng_ref)
            dlnb_ref[...] = jnp.zeros_like(dlnb_ref)
            dws_ref[...] = jnp.zeros_like(dws_ref)
            dbs_ref[...] = jnp.zeros_like(dbs_ref)

        pv = p_ref[...]
        s = _rms_scale(pv)
        n = pv * s
        dn = dx2_ref[...]
        dgp_ref[...] += jnp.sum(dn * n, axis=0, keepdims=True)
        dpb = _rms_bwd(dn, n, s, gp_ref[...]).astype(BF16)
        dpb_ref[...] = dpb
        dmerged = _mm_nt(dpb, wo_ref[...])

        sv = sv_ref[...].astype(F32)
        rstds = _sgu_norm_mix(sv, lng_ref, lnb_ref, ws_ref, bs_ref, vn_scr, mixed_scr, vhat_scr)
        u_pre = u_ref[...].astype(F32)
        u = _gelu(u_pre)
        mixed = mixed_scr[...]
        sa = jax.nn.sigmoid(ga_ref[...].astype(F32))
        sb = jax.nn.sigmoid(gb_ref[...].astype(F32))
        oa = oa_ref[...]
        do_a = (dmerged * sa).astype(BF16)
        dob_ref[...] = do_a
        prod = do_a.astype(F32) * oa
        for h in range(N_HEADS):
            sums = lax.dot_general(jnp.ones((8, LANES), F32), prod[:, h * HEAD_DIM:(h + 1) * HEAD_DIM],
                                   (((1,), (1,)), ((), ())), preferred_element_type=F32,
                                   precision=lax.Precision.HIGHEST)
            dvec_ref[h, 0] = sums[0:1, :]
        dz_ref[2] = (dmerged * oa * (sa * (1.0 - sa))).astype(BF16)
        dz_ref[3] = (dmerged * (u * mixed) * (sb * (1.0 - sb))).astype(BF16)
        do_b = dmerged * sb
        dz_ref[0] = (do_b * mixed * _gelu_grad(u_pre)).astype(BF16)
        dmix_scr[...] = do_b * u

        mask = _chunk_causal_mask()
        for g in range(N_GROUPS):
            lanes = slice(g * GROUP_DIM, (g + 1) * GROUP_DIM)
            wm = jnp.where(mask, ws_ref[g], 0.0).astype(BF16)
            dws = jnp.zeros((SGU_LEN, SGU_LEN), F32)
            dbs = jnp.zeros((SGU_LEN, 1), F32)
            for w in range(n_w):
                rows = slice(w * SGU_LEN, (w + 1) * SGU_LEN)
                dmix = dmix_scr[rows, lanes]
                dmix_b = dmix.astype(BF16)
                dvn_scr[rows, lanes] = _mm_tn(wm, dmix_b)
                dws = dws + _mm_nt(dmix_b, vn_scr[rows, lanes])
                dbs = dbs + jnp.sum(dmix, axis=-1, keepdims=True)
            dws_ref[g] += jnp.where(mask, dws, 0.0)
            dbs_ref[g] += dbs
            dvn = dvn_scr[:, lanes]
            vhat = vhat_scr[:, lanes]
            dlng_ref[:, lanes] += jnp.sum(dvn * vhat, axis=0, keepdims=True)
            dlnb_ref[:, lanes] += jnp.sum(dvn, axis=0, keepdims=True)
            dvh = dvn * lng_ref[:, lanes]
            dvs = rstds[g] * (dvh - jnp.mean(dvh, axis=-1, keepdims=True)
                              - vhat * jnp.mean(dvh * vhat, axis=-1, keepdims=True))
            dvn_scr[:, lanes] = dvs
        dz_ref[1] = (dvn_scr[...] * _gelu_grad(sv)).astype(BF16)

    def seg(idx):
        return pl.BlockSpec((None, tm, W), lambda i, idx=idx: (idx, i, 0))

    row = pl.BlockSpec((tm, D), lambda i: (i, 0))
    vec = pl.BlockSpec((1, D), lambda i: (0, 0))
    ws_spec = pl.BlockSpec((N_GROUPS, SGU_LEN, SGU_LEN), lambda i: (0, 0, 0))
    bs_spec = pl.BlockSpec((N_GROUPS, SGU_LEN, 1), lambda i: (0, 0, 0))
    return pl.pallas_call(
        body, name=name, grid=(T // tm,),
        in_specs=[row, row, seg(3), seg(4), seg(5), seg(6), row, vec, vec, ws_spec, bs_spec,
                  pl.BlockSpec((D, D), lambda i: (0, 0)), vec, ANY],
        out_specs=[row, row, pl.BlockSpec((N_HEADS, 1, 1, tm), lambda i: (0, i, 0, 0)),
                   pl.BlockSpec((4, tm, W), lambda i: (0, i, 0)), vec, vec, vec, ws_spec, bs_spec],
        out_shape=[jax.ShapeDtypeStruct((T, D), BF16), jax.ShapeDtypeStruct((T, W), BF16),
                   jax.ShapeDtypeStruct((N_HEADS, T // tm, 1, tm), F32), jax.ShapeDtypeStruct((4, T, W), BF16),
                   jax.ShapeDtypeStruct((1, D), F32), jax.ShapeDtypeStruct((1, D), F32),
                   jax.ShapeDtypeStruct((1, D), F32),
                   jax.ShapeDtypeStruct((N_GROUPS, SGU_LEN, SGU_LEN), F32),
                   jax.ShapeDtypeStruct((N_GROUPS, SGU_LEN, 1), F32)],
        scratch_shapes=[pltpu.VMEM((tm, W), BF16), pltpu.VMEM((tm, W), F32), pltpu.VMEM((tm, W), F32),
                        pltpu.VMEM((tm, W), F32), pltpu.VMEM((tm, W), F32)],
        compiler_params=_params(("arbitrary",)),
    )(dx2, p, z7, z7, z7, z7, o_a, lng, lnb, ws, bs, w_out, g_post, _after(dep))


def _loss_head(y, target, name):
    T, D = y.shape
    tm = _blk(T, 1024)
    n_i = T // tm

    def body(y_ref, t_ref, dy_ref, loss_ref, acc_scr):
        i = pl.program_id(0)

        @pl.when(i == 0)
        def _():
            acc_scr[...] = jnp.zeros_like(acc_scr)

        e = y_ref[...] - t_ref[...]
        dy_ref[...] = e * np.float32(1.0 / D)
        acc_scr[...] += jnp.sum(e * e, axis=0, keepdims=True)

        @pl.when(i == n_i - 1)
        def _():
            total = jnp.sum(acc_scr[...], axis=-1, keepdims=True) * np.float32(0.5 / D)
            loss_ref[...] = jnp.broadcast_to(total, loss_ref.shape)

    row = pl.BlockSpec((tm, D), lambda i: (i, 0))
    return pl.pallas_call(
        body, name=name, grid=(n_i,),
        in_specs=[row, row],
        out_specs=[row, pl.BlockSpec((1, LANES), lambda i: (0, 0))],
        out_shape=[jax.ShapeDtypeStruct((T, D), F32), jax.ShapeDtypeStruct((1, LANES), F32)],
        scratch_shapes=[pltpu.VMEM((1, D), F32)],
        compiler_params=_params(("arbitrary",)),
    )(y, target)


def _adamw_math(w, g, m, v):
    m_new = ADAM_B1 * m + (1.0 - ADAM_B1) * g
    v_new = ADAM_B2 * v + (1.0 - ADAM_B2) * (g * g)
    m_hat = m_new / np.float32(1.0 - ADAM_B1 ** ADAM_STEP)
    v_hat = v_new / np.float32(1.0 - ADAM_B2 ** ADAM_STEP)
    delta = -ADAM_LR * (m_hat / (jnp.sqrt(v_hat) + ADAM_EPS) + ADAM_WD * w)
    return delta, m_new, v_new


def _sum_adamw(parts, w, m, v, name, dep=None):
    n, R, C = parts.shape
    tr = _blk(R, 512)

    def body(p_ref, w_ref, m_ref, v_ref, _, g_ref, d_ref, mo_ref, vo_ref):
        g = p_ref[0].astype(F32)
        for s in range(1, n):
            g = g + p_ref[s].astype(F32)
        delta, m_new, v_new = _adamw_math(w_ref[...], g, m_ref[...], v_ref[...])
        g_ref[...] = g
        d_ref[...] = delta
        mo_ref[...] = m_new
        vo_ref[...] = v_new

    row = pl.BlockSpec((tr, C), lambda i: (i, 0))
    shp = jax.ShapeDtypeStruct((R, C), F32)
    return pl.pallas_call(
        body, name=name, grid=(R // tr,),
        in_specs=[pl.BlockSpec((n, tr, C), lambda i: (0, i, 0)), row, row, row, ANY],
        out_specs=[row, row, row, row], out_shape=[shp, shp, shp, shp],
        compiler_params=_params(("parallel",)),
    )(parts, w, m, v, _after(dep))


def _position():
    return lax.axis_index("x"), lax.axis_index("y"), lax.axis_index("c")


def _slot(px, py, pc):
    return 4 * px + 2 * py + pc


def _all_gather(shards, name):
    n = len(shards)

    def body(*refs):
        ins, outs = refs[:n], refs[n:2 * n]
        send_sems, recv_sems, local_sems = refs[2 * n:]
        x, y, c = _position()
        me, sibling = (x, y, c), (x, y, 1 - c)
        chips = [(1 - x, y), (x, 1 - y), (1 - x, 1 - y)]

        def copy(a, k, block, to, src=None):
            dst = outs[a].at[_slot(*block)]
            return pltpu.make_async_remote_copy(
                src_ref=dst if src is None else src, dst_ref=dst,
                send_sem=send_sems.at[a, k], recv_sem=recv_sems.at[a, k],
                device_id=to, device_id_type=MESH)

        mine = [pltpu.make_async_copy(ins[a], outs[a].at[_slot(*me)], local_sems.at[a]) for a in range(n)]
        for cp in mine:
            cp.start()
        first = []
        for a in range(n):
            first.append(copy(a, 0, me, sibling, src=ins[a]))
            first += [copy(a, 1 + j, me, (*chip, c), src=ins[a]) for j, chip in enumerate(chips)]
        for cp in first:
            cp.start()
        passed = []
        for j, chip in enumerate(chips):
            for a in range(n):
                copy(a, 1 + j, (*chip, c), me).wait_recv()
                fwd = copy(a, 4 + j, (*chip, c), sibling)
                fwd.start()
                passed.append(fwd)
        for a in range(n):
            copy(a, 0, sibling, me).wait_recv()
            for j, chip in enumerate(chips):
                copy(a, 4 + j, (*chip, 1 - c), me).wait_recv()
        for cp in first + passed:
            cp.wait_send()
        for cp in mine:
            cp.wait()

    return pl.pallas_call(
        body, name=name,
        in_specs=[ANY] * n, out_specs=[ANY] * n,
        out_shape=[jax.ShapeDtypeStruct((N_DEV,) + s.shape, s.dtype) for s in shards],
        scratch_shapes=[pltpu.SemaphoreType.DMA((n, 7)), pltpu.SemaphoreType.DMA((n, 7)),
                        pltpu.SemaphoreType.DMA((n,))],
    )(*shards)


def _peer(x, y, c, k):
    return (1 - x if k & 4 else x, 1 - y if k & 2 else y, 1 - c if k & 1 else c)


def _remote_copies(src_refs, land_refs, send_sems, recv_sems, gather, outgoing):
    x, y, c = _position()
    me = _slot(x, y, c)
    copies = []
    for k in range(1, N_DEV):
        peer = _peer(x, y, c, k)
        for a in range(len(src_refs)):
            src = src_refs[a] if gather else src_refs[a].at[_slot(*peer)]
            dst = land_refs[a].at[me if outgoing else _slot(*peer)]
            sem = a * (N_DEV - 1) + k - 1
            copies.append(pltpu.make_async_remote_copy(
                src_ref=src, dst_ref=dst, send_sem=send_sems.at[sem], recv_sem=recv_sems.at[sem],
                device_id=peer, device_id_type=MESH))
    return copies


def _sequencer_exchange(srcs, name, gather, collective_id):
    n = len(srcs)
    hbm = pltpu.MemorySpace.HBM
    src_refs = [jax.new_ref(s, memory_space=hbm) for s in srcs]
    land_refs = [jax.empty_ref(jax.ShapeDtypeStruct(((N_DEV,) + s.shape) if gather else s.shape, s.dtype),
                               memory_space=hbm) for s in srcs]
    n_sems = n * (N_DEV - 1)
    block_bytes = sum(s.size * s.dtype.itemsize // (1 if gather else N_DEV) for s in srcs)
    cost = pl.CostEstimate(flops=0, transcendentals=0, bytes_accessed=2 * N_DEV * block_bytes,
                           remote_bytes_transferred=(N_DEV - 1) * block_bytes)

    @pl.kernel(mesh=plsc.ScalarSubcoreMesh(axis_name="sequencer", num_cores=1), name=name,
               scratch_types=(pltpu.SemaphoreType.DMA((n_sems,)), pltpu.SemaphoreType.DMA((n_sems,)),
                              pltpu.SemaphoreType.DMA((n,))),
               cost_estimate=cost,
               compiler_params=pltpu.CompilerParams(collective_id=collective_id))
    def launch(send_sems, recv_sems, local_sems):
        x, y, c = _position()
        me = _slot(x, y, c)
        barrier = pltpu.get_barrier_semaphore()
        for k in range(1, N_DEV):
            pl.semaphore_signal(barrier, inc=1, device_id=_peer(x, y, c, k), device_id_type=MESH)
        pl.semaphore_wait(barrier, N_DEV - 1)
        mine = [pltpu.make_async_copy(src_refs[a] if gather else src_refs[a].at[me], land_refs[a].at[me],
                                      local_sems.at[a]) for a in range(n)]
        for cp in mine:
            cp.start()
        sends = _remote_copies(src_refs, land_refs, send_sems, recv_sems, gather, outgoing=True)
        for cp in sends:
            cp.start()
        for cp in _remote_copies(src_refs, land_refs, send_sems, recv_sems, gather, outgoing=False):
            cp.wait_recv()
        for cp in sends:
            cp.wait_send()
        for cp in mine:
            cp.wait()

    launch()
    return [r[...] for r in land_refs]


SMALL_VECS = ("ffn1_pre_g", "ffn1_post_g", "mix_pre_g", "sgu_ln_g", "sgu_ln_b", "mix_post_g", "ffn2_pre_g",
              "ffn2_post_g")
ROW_BS = len(SMALL_VECS)
ROW_BF = ROW_BS + 1
ROW_LOSS = ROW_BF + 1
ROW_WS = 16
BLOB_ROWS = ROW_WS + SGU_LEN


def _pack_small(vals, D, loss_row=None):
    rows = [vals[n].reshape(1, D) for n in SMALL_VECS]
    rows.append(vals["sgu_b_s"].reshape(1, D))
    rows.append(jnp.pad(vals["b_forget"].reshape(1, N_HEADS), ((0, 0), (0, D - N_HEADS))))
    rows.append(jnp.zeros((1, D), F32) if loss_row is None else loss_row)
    rows.append(jnp.zeros((ROW_WS - ROW_LOSS - 1, D), F32))
    rows.append(vals["sgu_w_s"].reshape(SGU_LEN, D))
    return jnp.concatenate(rows, axis=0)


def _unpack_small(blob, D):
    out = {n: blob[r:r + 1] for r, n in enumerate(SMALL_VECS)}
    out["sgu_b_s"] = blob[ROW_BS].reshape(1, N_GROUPS, SGU_LEN)
    out["b_forget"] = blob[ROW_BF, :N_HEADS].reshape(1, N_HEADS)
    out["sgu_w_s"] = blob[ROW_WS:].reshape(1, N_GROUPS, SGU_LEN, SGU_LEN)
    return out


WEIGHT_NAMES = ("ffn1_pre_g", "ffn1_w_gate", "ffn1_w_up", "ffn1_w_down", "ffn1_post_g", "mix_pre_g", "w_in",
                "b_forget", "sgu_ln_g", "sgu_ln_b", "sgu_w_s", "sgu_b_s", "w_out", "mix_post_g", "ffn2_pre_g",
                "ffn2_w_gate", "ffn2_w_up", "ffn2_w_down", "ffn2_post_g")
BIG_NAMES = ("ffn1_w_gate", "ffn1_w_up", "ffn1_w_down", "w_in", "w_out", "ffn2_w_gate", "ffn2_w_up", "ffn2_w_down")
WEIGHT_GROUPS = {"ffn1": ("ffn1_w_gate", "ffn1_w_up", "ffn1_w_down"), "mix": ("w_in", "w_out"),
                 "ffn2": ("ffn2_w_gate", "ffn2_w_up", "ffn2_w_down")}
GRAD_GROUPS = (("ffn2_w_gate", "ffn2_w_up", "ffn2_w_down"), ("w_in", "w_out"), ("ffn1_w_down",), ("ffn1_w_gate",),
               ("ffn1_w_up",))


def _local_step(x, target, small, fetch, emit, consume):
    T, D = x.shape
    W = N_HEADS * HEAD_DIM
    vec = lambda n: small[n].reshape(1, D)
    big = dict(fetch("ffn1", x))

    x1, y1, dgf1, silu1, act1 = _ffn_fwd(x, vec("ffn1_pre_g"), big["ffn1_w_gate"], big["ffn1_w_up"], big["ffn1_w_down"],
                                  vec("ffn1_post_g"), "ffn1_fwd")

    big.update(fetch("mix", x1))
    w_in_all = big["w_in"]
    in_width = N_DEV * w_in_all.shape[2]
    w_in = w_in_all.transpose(1, 0, 2).reshape(D, in_width)
    col_f = 3 * W
    col_u = col_f + N_HEADS
    seg_starts = (0, W, 2 * W, col_u, col_u + W, col_u + 2 * W, col_u + 3 * W)
    w7 = jnp.stack([w_in[:, s:s + W] for s in seg_starts])
    wf = jnp.pad(w_in[:, col_f:col_u], ((0, 0), (0, LANES - N_HEADS)))
    w_out = big["w_out"].reshape(D, D)
    b_pad = jnp.pad(small["b_forget"].reshape(1, N_HEADS), ((0, 0), (0, LANES - N_HEADS)))
    lng, lnb = vec("sgu_ln_g"), vec("sgu_ln_b")
    ws = small["sgu_w_s"].reshape(N_GROUPS, SGU_LEN, SGU_LEN)
    bs = small["sgu_b_s"].reshape(N_GROUPS, SGU_LEN, 1)

    z7, f_logit, h2b = _mix_in_fwd(x1, vec("mix_pre_g"), w7, wf, "mix_in_fwd")
    c_rep = _forget_cumsum(f_logit, b_pad, "forget_cumsum")
    ta, _, n_chunks = _attn_geometry(T)
    vt = z7[2].reshape(n_chunks, ta, N_HEADS, HEAD_DIM).transpose(2, 0, 3, 1)
    o_a, lse_chunks = _attn_fwd_keys_on_rows(z7, vt, c_rep, "attn_fwd")
    x2, p, merged_b = _mix_out_fwd(z7, o_a, x1, lng, lnb, ws, bs, w_out, vec("mix_post_g"), "mix_out_fwd")
    big.update(fetch("ffn2", x2))
    x3, y2, dgf2, silu2, act2 = _ffn_fwd(x2, vec("ffn2_pre_g"), big["ffn2_w_gate"], big["ffn2_w_up"], big["ffn2_w_down"],
                                  vec("ffn2_post_g"), "ffn2_fwd")
    dy, loss_lanes = _loss_head(x3, target, "loss_head")

    grads_small = {}

    dx2, h3b, dy2b, dgate2, dup2, dgpre, dgpost = _ffn_bwd(
        dy, x2, y2, dgf2, silu2, vec("ffn2_pre_g"), big["ffn2_w_gate"], big["ffn2_w_up"], big["ffn2_w_down"],
        vec("ffn2_post_g"), "ffn2_bwd")
    grads_small["ffn2_pre_g"] = jnp.sum(dgpre, axis=0)
    grads_small["ffn2_post_g"] = jnp.sum(dgpost, axis=0)
    dep = emit("ffn2_w_gate", _wgrad(h3b, dgate2, "ffn2_wgrad_gate", shard_cols=True))
    dep = emit("ffn2_w_up", _wgrad(h3b, dup2, "ffn2_wgrad_up", shard_cols=True, dep=dep))
    dep = emit("ffn2_w_down", _wgrad(act2, dy2b, "ffn2_wgrad_down", dep=dep).reshape(big["ffn2_w_down"].shape))

    dpb, dob, dvec, dz4, dgp, dlng, dlnb, dws, dbs = _mix_out_bwd(
        dx2, p, z7, o_a, lng, lnb, ws, bs, w_out, vec("mix_post_g"), "mix_out_bwd", dep=dep)
    grads_small["mix_post_g"] = dgp
    grads_small["sgu_ln_g"] = dlng
    grads_small["sgu_ln_b"] = dlnb
    grads_small["sgu_w_s"] = dws
    grads_small["sgu_b_s"] = dbs
    d_chunks = dvec.reshape(N_HEADS, n_chunks, 1, ta)
    kt = z7[1].reshape(n_chunks, ta, N_HEADS, HEAD_DIM).transpose(2, 0, 3, 1)
    dk, dv, dc, dq, dc_q = _attn_bwd_fused(z7, kt, dob, c_rep, lse_chunks, d_chunks, "attn_bwd")
    dc_pad = jnp.pad((dc + dc_q).reshape(N_HEADS, T).T, ((0, 0), (0, LANES - N_HEADS)))
    dfb, dbf = _forget_bwd(dc_pad, f_logit, b_pad, "forget_bwd")
    grads_small["b_forget"] = dbf[:, :N_HEADS]
    segs = [(dq, None), (dk, None), (dv, None), (dz4, 0), (dz4, 1), (dz4, 2), (dz4, 3)]
    dep = consume(("ffn2_w_gate", "ffn2_w_up", "ffn2_w_down"))
    dx1, dgm = _mix_in_bwd(dx2, x1, vec("mix_pre_g"), segs, dfb, w7, wf, "mix_in_bwd", dep=dep)
    grads_small["mix_pre_g"] = jnp.sum(dgm, axis=0)
    dw_qkv = _wgrad_multi(h2b, segs[:3], "w_in_wgrad_qkv", dep=dx1)
    dw_rest = _wgrad_multi(h2b, segs[3:], "w_in_wgrad_gates", dep=dw_qkv)
    dw_seg = [dw_qkv[:, q * W:(q + 1) * W] for q in range(3)] + [dw_rest[:, q * W:(q + 1) * W] for q in range(4)]
    dwf = _wgrad(h2b, dfb, "w_in_wgrad_f", dep=dw_rest)
    dw_in = jnp.concatenate(dw_seg[:3] + [dwf[:, :N_HEADS]] + dw_seg[3:], axis=1)
    emit("w_in", dw_in.reshape(D, N_DEV, in_width // N_DEV).transpose(1, 0, 2))
    dep = emit("w_out", _wgrad(merged_b, dpb, "w_out_wgrad", dep=dwf).reshape(big["w_out"].shape))

    dx0, h1b, dy1b, dgate1, dup1, dgpre1, dgpost1 = _ffn_bwd(
        dx1, x, y1, dgf1, silu1, vec("ffn1_pre_g"), big["ffn1_w_gate"], big["ffn1_w_up"], big["ffn1_w_down"],
        vec("ffn1_post_g"), "ffn1_bwd", dep=dep)
    grads_small["ffn1_pre_g"] = jnp.sum(dgpre1, axis=0)
    grads_small["ffn1_post_g"] = jnp.sum(dgpost1, axis=0)
    dep = consume(("w_in", "w_out"))
    dep = emit("ffn1_w_down", _wgrad(act1, dy1b, "ffn1_wgrad_down", dep=dep).reshape(big["ffn1_w_down"].shape))
    dep = emit("ffn1_w_gate", _wgrad(h1b, dgate1, "ffn1_wgrad_gate", shard_cols=True, dep=dep))
    dep = emit("ffn1_w_up", _wgrad(h1b, dup1, "ffn1_wgrad_up", shard_cols=True, dep=dep))

    loss_row = jnp.pad(loss_lanes, ((0, 0), (0, D - LANES)))
    return loss_row, dx0, grads_small


def kernel(x, ffn1_pre_g, ffn1_w_gate, ffn1_w_up, ffn1_w_down, ffn1_post_g, mix_pre_g, w_in, b_forget, sgu_ln_g, sgu_ln_b, sgu_w_s, sgu_b_s, w_out, mix_post_g, ffn2_pre_g, ffn2_w_gate, ffn2_w_up, ffn2_w_down, ffn2_post_g, loss_target, m_ffn1_pre_g, m_ffn1_w_gate, m_ffn1_w_up, m_ffn1_w_down, m_ffn1_post_g, m_mix_pre_g, m_w_in, m_b_forget, m_sgu_ln_g, m_sgu_ln_b, m_sgu_w_s, m_sgu_b_s, m_w_out, m_mix_post_g, m_ffn2_pre_g, m_ffn2_w_gate, m_ffn2_w_up, m_ffn2_w_down, m_ffn2_post_g, v_ffn1_pre_g, v_ffn1_w_gate, v_ffn1_w_up, v_ffn1_w_down, v_ffn1_post_g, v_mix_pre_g, v_w_in, v_b_forget, v_sgu_ln_g, v_sgu_ln_b, v_sgu_w_s, v_sgu_b_s, v_w_out, v_mix_post_g, v_ffn2_pre_g, v_ffn2_w_gate, v_ffn2_w_up, v_ffn2_w_down, v_ffn2_post_g):
    weights = dict(zip(WEIGHT_NAMES, (ffn1_pre_g, ffn1_w_gate, ffn1_w_up, ffn1_w_down, ffn1_post_g, mix_pre_g, w_in,
                                      b_forget, sgu_ln_g, sgu_ln_b, sgu_w_s, sgu_b_s, w_out, mix_post_g, ffn2_pre_g,
                                      ffn2_w_gate, ffn2_w_up, ffn2_w_down, ffn2_post_g)))
    mom1 = dict(zip(WEIGHT_NAMES, (m_ffn1_pre_g, m_ffn1_w_gate, m_ffn1_w_up, m_ffn1_w_down, m_ffn1_post_g,
                                   m_mix_pre_g, m_w_in, m_b_forget, m_sgu_ln_g, m_sgu_ln_b, m_sgu_w_s, m_sgu_b_s,
                                   m_w_out, m_mix_post_g, m_ffn2_pre_g, m_ffn2_w_gate, m_ffn2_w_up, m_ffn2_w_down,
                                   m_ffn2_post_g)))
    mom2 = dict(zip(WEIGHT_NAMES, (v_ffn1_pre_g, v_ffn1_w_gate, v_ffn1_w_up, v_ffn1_w_down, v_ffn1_post_g,
                                   v_mix_pre_g, v_w_in, v_b_forget, v_sgu_ln_g, v_sgu_ln_b, v_sgu_w_s, v_sgu_b_s,
                                   v_w_out, v_mix_post_g, v_ffn2_pre_g, v_ffn2_w_gate, v_ffn2_w_up, v_ffn2_w_down,
                                   v_ffn2_post_g)))
    D = x.shape[-1]
    small_names = [n for n in WEIGHT_NAMES if n not in BIG_NAMES]

    small = {n: weights[n] for n in small_names}
    shard = lambda n: weights[n][0].astype(BF16)

    ffn1_full = _all_gather([shard(n) for n in WEIGHT_GROUPS["ffn1"]], "ffn1_all_gather")
    gathered = {}
    for cid, grp in ((1, "mix"), (2, "ffn2")):
        shards, _ = lax.optimization_barrier(([shard(n) for n in WEIGHT_GROUPS[grp]], ffn1_full[0]))
        gathered[grp] = _sequencer_exchange(shards, grp + "_gather", True, cid)

    def fetch(group, after):
        if group == "ffn1":
            return zip(WEIGHT_GROUPS[group], ffn1_full)
        arrived, _ = lax.optimization_barrier((gathered[group], after))
        return zip(WEIGHT_GROUPS[group], arrived)

    ready, received = {}, {}

    def emit(name, part):
        ready[name] = part
        for gi, group in enumerate(GRAD_GROUPS):
            if name == group[-1]:
                lands = _sequencer_exchange([ready[n] for n in group], name + "_grad_exchange", False, 3 + gi)
                received.update(zip(group, lands))
        return part

    out = {}

    def consume(names, dep=None):
        for n in names:
            g, d, m_new, v_new = _sum_adamw(received[n], weights[n][0], mom1[n][0], mom2[n][0], "adamw_" + n, dep=dep)
            out[n] = tuple(a[None] for a in (g, d, m_new, v_new))
            dep = g
        return dep

    loss_row, grad_x, grads_small = _local_step(x[0], loss_target[0], small, fetch, emit, consume)

    blobs = _sequencer_exchange([_pack_small(grads_small, D, loss_row)], "small_gather", True,
                                3 + len(GRAD_GROUPS))[0]
    blob, d_blob, m_blob, v_blob = _sum_adamw(
        blobs, _pack_small(small, D), _pack_small({n: mom1[n] for n in small_names}, D),
        _pack_small({n: mom2[n] for n in small_names}, D), "adamw_small")
    consume(("ffn1_w_down", "ffn1_w_gate", "ffn1_w_up"), dep=blob)
    unpacked = [_unpack_small(b, D) for b in (blob, d_blob, m_blob, v_blob)]
    for n in small_names:
        out[n] = tuple(u[n].reshape(weights[n].shape) for u in unpacked)

    loss = blob[ROW_LOSS, 0]
    result = [loss, grad_x[None]]
    for k in range(4):
        result += [out[n][k] for n in WEIGHT_NAMES]
    return tuple(result)
```

```python
import numpy as np
import jax
import jax.numpy as jnp
from jax import lax
from jax.experimental import pallas as pl
from jax.experimental.pallas import tpu as pltpu
from jax.experimental.pallas import tpu_sc as plsc

F32 = jnp.float32
BF16 = jnp.bfloat16

RMS_EPS = 1e-6
LN_EPS = 1e-5
HEAD_DIM = 128
N_HEADS = 8
GROUP_DIM = 128
N_GROUPS = 8
SGU_LEN = 128
CHUNK = 64
N_DEV = 8
LANES = 128
VMEM_LIMIT = 56 * 1024 * 1024
NEG_BIG = -1e30
LOG2E = np.float32(1.0 / np.log(2.0))

ADAM_LR = 0.001
ADAM_B1 = 0.9
ADAM_B2 = 0.999
ADAM_EPS = 1e-08
ADAM_WD = 0.01
ADAM_STEP = 10

MESH = pl.DeviceIdType.MESH
ANY = pl.BlockSpec(memory_space=pl.ANY)


def _blk(n, pref):
    return pref if (n >= pref and n % pref == 0) else n


def _mm(a, b):
    return jnp.dot(a, b, preferred_element_type=F32)


def _mm_nt(a, b):
    return lax.dot_general(a, b, (((1,), (1,)), ((), ())), preferred_element_type=F32)


def _mm_tn(a, b):
    return lax.dot_general(a, b, (((0,), (0,)), ((), ())), preferred_element_type=F32)


def _params(sem):
    return pltpu.CompilerParams(dimension_semantics=sem, vmem_limit_bytes=VMEM_LIMIT)


def _gelu(x):
    return 0.5 * x * (1.0 + lax.erf(x * np.float32(1.0 / np.sqrt(2.0))))


def _gelu_grad(x):
    cdf = 0.5 * (1.0 + lax.erf(x * np.float32(1.0 / np.sqrt(2.0))))
    return cdf + x * jnp.exp(-0.5 * x * x) * np.float32(1.0 / np.sqrt(2.0 * np.pi))


def _rms_scale(v):
    return lax.rsqrt(jnp.mean(v * v, axis=-1, keepdims=True) + RMS_EPS)


def _rms_bwd(dy, xhat, r, g):
    dxh = dy * g
    return r * (dxh - xhat * jnp.mean(dxh * xhat, axis=-1, keepdims=True))


def _ffn_fwd(x, g_pre, wg, wu, wd, g_post, name):
    T, D = x.shape
    ns, _, fs = wg.shape
    tm = _blk(T, 256)

    def body(x_ref, gpre_ref, wg_ref, wu_ref, wd_ref, gpost_ref, xo_ref, y_ref, dgf_ref, silu_ref, act_ref):
        xv = x_ref[...]
        h = (xv * _rms_scale(xv) * gpre_ref[...]).astype(BF16)
        y = jnp.zeros((tm, D), F32)
        pre = (_mm(h, wg_ref[0]), _mm(h, wu_ref[0]))
        for j in range(ns):
            gg, uu = pre
            if j + 1 < ns:
                pre = (_mm(h, wg_ref[j + 1]), _mm(h, wu_ref[j + 1]))
            cols = slice(j * fs, (j + 1) * fs)
            sg = jax.nn.sigmoid(gg)
            silu = gg * sg
            act = (silu * uu).astype(BF16)
            dgf_ref[:, cols] = (uu * (sg * (1.0 + gg * (1.0 - sg)))).astype(BF16)
            silu_ref[:, cols] = silu.astype(BF16)
            act_ref[:, cols] = act
            y = y + _mm(act, wd_ref[j])
        y_ref[...] = y
        xo_ref[...] = xv + 0.5 * (y * _rms_scale(y) * gpost_ref[...])

    row = pl.BlockSpec((tm, D), lambda i: (i, 0))
    vec = pl.BlockSpec((1, D), lambda i: (0, 0))
    wide = pl.BlockSpec((tm, ns * fs), lambda i: (i, 0))
    return pl.pallas_call(
        body, name=name, grid=(T // tm,),
        in_specs=[row, vec,
                  pl.BlockSpec((ns, D, fs), lambda i: (0, 0, 0), pipeline_mode=pl.Buffered(1)),
                  pl.BlockSpec((ns, D, fs), lambda i: (0, 0, 0), pipeline_mode=pl.Buffered(1)),
                  pl.BlockSpec((ns, fs, D), lambda i: (0, 0, 0), pipeline_mode=pl.Buffered(1)),
                  vec],
        out_specs=[row, row, wide, wide, wide],
        out_shape=[jax.ShapeDtypeStruct((T, D), F32), jax.ShapeDtypeStruct((T, D), F32)]
        + [jax.ShapeDtypeStruct((T, ns * fs), BF16)] * 3,
        compiler_params=_params(("parallel",)),
    )(x, g_pre, wg, wu, wd, g_post)


def _after(dep):
    return jnp.zeros((8, LANES), F32) if dep is None else dep


def _ffn_bwd(dxo, x, y, dgf, silu, g_pre, wg, wu, wd, g_post, name, dep=None):
    T, D = x.shape
    ns, _, fs = wg.shape
    tm = _blk(T, 256)
    n_i = T // tm

    def body(dxo_ref, x_ref, y_ref, dgf_ref, silu_ref, gpre_ref, wg_ref, wu_ref, wd_ref, gpost_ref, _,
             dx_ref, hb_ref, dyb_ref, dgb_ref, dub_ref, dgpre_ref, dgpost_ref):
        yv = y_ref[...]
        s = _rms_scale(yv)
        n = yv * s
        dxo = dxo_ref[...]
        dn = 0.5 * dxo
        dgpost_ref[...] = jnp.sum(dn * n, axis=0, keepdims=True)
        dyv = _rms_bwd(dn, n, s, gpost_ref[...]).astype(BF16)
        dyb_ref[...] = dyv
        xv = x_ref[...]
        rs = _rms_scale(xv)
        xhat = xv * rs
        hb_ref[...] = (xhat * gpre_ref[...]).astype(BF16)

        dh = jnp.zeros((tm, D), F32)
        da = _mm_nt(dyv, wd_ref[0])
        for j in range(ns):
            cur = da
            if j + 1 < ns:
                da = _mm_nt(dyv, wd_ref[j + 1])
            cols = slice(j * fs, (j + 1) * fs)
            dgate = (cur * dgf_ref[:, cols].astype(F32)).astype(BF16)
            dup = (cur * silu_ref[:, cols].astype(F32)).astype(BF16)
            dgb_ref[:, cols] = dgate
            dub_ref[:, cols] = dup
            dh = dh + _mm_nt(dgate, wg_ref[j]) + _mm_nt(dup, wu_ref[j])

        dgpre_ref[...] = jnp.sum(dh * xhat, axis=0, keepdims=True)
        dx_ref[...] = _rms_bwd(dh, xhat, rs, gpre_ref[...]) + dxo

    F = ns * fs
    row = pl.BlockSpec((tm, D), lambda i: (i, 0))
    vec = pl.BlockSpec((1, D), lambda i: (0, 0))
    wide = pl.BlockSpec((tm, F), lambda i: (i, 0))
    part = pl.BlockSpec((None, 1, D), lambda i: (i, 0, 0))
    return pl.pallas_call(
        body, name=name, grid=(n_i,),
        in_specs=[row, row, row, wide, wide, vec,
                  pl.BlockSpec((ns, D, fs), lambda i: (0, 0, 0), pipeline_mode=pl.Buffered(1)),
                  pl.BlockSpec((ns, D, fs), lambda i: (0, 0, 0), pipeline_mode=pl.Buffered(1)),
                  pl.BlockSpec((ns, fs, D), lambda i: (0, 0, 0), pipeline_mode=pl.Buffered(1)),
                  vec, ANY],
        out_specs=[row, row, row, wide, wide, part, part],
        out_shape=[jax.ShapeDtypeStruct((T, D), F32), jax.ShapeDtypeStruct((T, D), BF16),
                   jax.ShapeDtypeStruct((T, D), BF16), jax.ShapeDtypeStruct((T, F), BF16),
                   jax.ShapeDtypeStruct((T, F), BF16),
                   jax.ShapeDtypeStruct((n_i, 1, D), F32), jax.ShapeDtypeStruct((n_i, 1, D), F32)],
        compiler_params=_params(("parallel",)),
    )(dxo, x, y, dgf, silu, g_pre, wg, wu, wd, g_post, _after(dep))


def _wgrad(xm, ym, name, shard_cols=False, dep=None):
    T, M = xm.shape
    N = ym.shape[-1]
    assert M * N * 4 <= 16 * 1024 * 1024, (M, N)
    tk = _blk(T, 512)
    n_k = T // tk
    fs = N // N_DEV

    def body(x_ref, y_ref, _, o_ref, acc_scr):
        k = pl.program_id(0)

        @pl.when(k == 0)
        def _():
            acc_scr[...] = jnp.zeros_like(acc_scr)

        acc_scr[...] += _mm_tn(x_ref[...], y_ref[...])

        @pl.when(k == n_k - 1)
        def _():
            if shard_cols:
                for s in range(N_DEV):
                    o_ref[s] = acc_scr[:, s * fs:(s + 1) * fs].astype(BF16)
            else:
                o_ref[...] = acc_scr[...].astype(BF16)

    if shard_cols:
        out_spec = pl.BlockSpec((N_DEV, M, fs), lambda k: (0, 0, 0), pipeline_mode=pl.Buffered(1))
        out_shape = jax.ShapeDtypeStruct((N_DEV, M, fs), BF16)
    else:
        out_spec = pl.BlockSpec((M, N), lambda k: (0, 0), pipeline_mode=pl.Buffered(1))
        out_shape = jax.ShapeDtypeStruct((M, N), BF16)
    return pl.pallas_call(
        body, name=name, grid=(n_k,),
        in_specs=[pl.BlockSpec((tk, M), lambda k: (k, 0)), pl.BlockSpec((tk, N), lambda k: (k, 0)), ANY],
        out_specs=out_spec, out_shape=out_shape,
        scratch_shapes=[pltpu.VMEM((M, N), F32)],
        compiler_params=_params(("arbitrary",)),
    )(xm, ym, _after(dep))


def _wgrad_multi(xm, segs, name, dep=None):
    T, M = xm.shape
    N = segs[0][0].shape[-1]
    n_seg = len(segs)
    assert M * N * n_seg * 4 <= 16 * 1024 * 1024, (M, N, n_seg)
    tk = _blk(T, 512)
    n_k = T // tk

    def body(*refs):
        x_ref, y_refs = refs[0], refs[1:1 + n_seg]
        o_ref, acc_scr = refs[2 + n_seg], refs[3 + n_seg]
        k = pl.program_id(0)

        @pl.when(k == 0)
        def _():
            acc_scr[...] = jnp.zeros_like(acc_scr)

        x = x_ref[...]
        for s in range(n_seg):
            acc_scr[:, s * N:(s + 1) * N] += _mm_tn(x, y_refs[s][...])

        @pl.when(k == n_k - 1)
        def _():
            o_ref[...] = acc_scr[...].astype(BF16)

    y_specs = [pl.BlockSpec((tk, N), lambda k: (k, 0)) if idx is None
               else pl.BlockSpec((None, tk, N), lambda k, idx=idx: (idx, k, 0)) for _, idx in segs]
    return pl.pallas_call(
        body, name=name, grid=(n_k,),
        in_specs=[pl.BlockSpec((tk, M), lambda k: (k, 0))] + y_specs + [ANY],
        out_specs=pl.BlockSpec((M, n_seg * N), lambda k: (0, 0), pipeline_mode=pl.Buffered(1)),
        out_shape=jax.ShapeDtypeStruct((M, n_seg * N), BF16),
        scratch_shapes=[pltpu.VMEM((M, n_seg * N), F32)],
        compiler_params=_params(("arbitrary",)),
    )(xm, *[arr for arr, _ in segs], _after(dep))


def _mix_in_fwd(x1, g, w7, wf, name):
    T, D = x1.shape
    n_seg, _, W = w7.shape
    tm = _blk(T, 512)

    def body(x_ref, g_ref, w_ref, wf_ref, z_ref, f_ref, hb_ref):
        xv = x_ref[...]
        h = (xv * _rms_scale(xv) * g_ref[...]).astype(BF16)
        hb_ref[...] = h
        f_ref[...] = _mm(h, wf_ref[...])
        for s in range(n_seg):
            z_ref[s] = _mm(h, w_ref[s]).astype(BF16)

    return pl.pallas_call(
        body, name=name, grid=(T // tm,),
        in_specs=[pl.BlockSpec((tm, D), lambda i: (i, 0)),
                  pl.BlockSpec((1, D), lambda i: (0, 0)),
                  pl.BlockSpec((n_seg, D, W), lambda i: (0, 0, 0), pipeline_mode=pl.Buffered(1)),
                  pl.BlockSpec((D, LANES), lambda i: (0, 0))],
        out_specs=[pl.BlockSpec((n_seg, tm, W), lambda i: (0, i, 0)),
                   pl.BlockSpec((tm, LANES), lambda i: (i, 0)),
                   pl.BlockSpec((tm, D), lambda i: (i, 0))],
        out_shape=[jax.ShapeDtypeStruct((n_seg, T, W), BF16), jax.ShapeDtypeStruct((T, LANES), F32),
                   jax.ShapeDtypeStruct((T, D), BF16)],
        compiler_params=_params(("parallel",)),
    )(x1, g, w7, wf)


def _mix_in_bwd(dx2, x1, g, segs, dfb, w7, wf, name, dep=None):
    T, D = x1.shape
    n_seg, _, W = w7.shape
    tm = _blk(T, 512)
    n_i = T // tm

    def body(*refs):
        dx2_ref, x_ref, g_ref = refs[:3]
        seg_refs = refs[3:3 + n_seg]
        df_ref, w_ref, wf_ref, _, dx1_ref, dg_ref = refs[3 + n_seg:]
        dh = _mm_nt(df_ref[...], wf_ref[...])
        for q in range(n_seg):
            dh = dh + _mm_nt(seg_refs[q][...], w_ref[q])
        xv = x_ref[...]
        r = _rms_scale(xv)
        xhat = xv * r
        dg_ref[...] = jnp.sum(dh * xhat, axis=0, keepdims=True)
        dx1_ref[...] = _rms_bwd(dh, xhat, r, g_ref[...]) + dx2_ref[...]

    row = pl.BlockSpec((tm, D), lambda i: (i, 0))
    seg_specs = []
    seg_args = []
    for arr, idx in segs:
        if idx is None:
            seg_specs.append(pl.BlockSpec((tm, W), lambda i: (i, 0)))
        else:
            seg_specs.append(pl.BlockSpec((None, tm, W), lambda i, idx=idx: (idx, i, 0)))
        seg_args.append(arr)
    return pl.pallas_call(
        body, name=name, grid=(n_i,),
        in_specs=[row, row, pl.BlockSpec((1, D), lambda i: (0, 0))] + seg_specs + [
            pl.BlockSpec((tm, LANES), lambda i: (i, 0)),
            pl.BlockSpec((n_seg, D, W), lambda i: (0, 0, 0), pipeline_mode=pl.Buffered(1)),
            pl.BlockSpec((D, LANES), lambda i: (0, 0)), ANY],
        out_specs=[row, pl.BlockSpec((None, 1, D), lambda i: (i, 0, 0))],
        out_shape=[jax.ShapeDtypeStruct((T, D), F32), jax.ShapeDtypeStruct((n_i, 1, D), F32)],
        compiler_params=_params(("parallel",)),
    )(dx2, x1, g, *seg_args, dfb, w7, wf, _after(dep))


def _forget_cumsum(f, b_pad, name):
    T, L = f.shape
    tb = _blk(T, 256)

    def body(f_ref, b_ref, c_ref, carry):
        @pl.when(pl.program_id(0) == 0)
        def _():
            carry[...] = jnp.zeros_like(carry)

        lf = jax.nn.log_sigmoid(f_ref[...] + b_ref[...])
        rows = lax.broadcasted_iota(jnp.int32, (tb, tb), 0)
        cols = lax.broadcasted_iota(jnp.int32, (tb, tb), 1)
        tri = (cols <= rows).astype(F32)
        c = jnp.dot(tri, lf, preferred_element_type=F32, precision=lax.Precision.HIGHEST) + carry[...]
        carry[...] = c[tb - 1:tb, :]
        for h in range(N_HEADS):
            c_ref[h] = jnp.broadcast_to(c[:, h:h + 1] * LOG2E, (tb, L))

    return pl.pallas_call(
        body, name=name, grid=(T // tb,),
        in_specs=[pl.BlockSpec((tb, L), lambda i: (i, 0)), pl.BlockSpec((1, L), lambda i: (0, 0))],
        out_specs=pl.BlockSpec((N_HEADS, tb, L), lambda i: (0, i, 0)),
        out_shape=jax.ShapeDtypeStruct((N_HEADS, T, L), F32),
        scratch_shapes=[pltpu.VMEM((1, L), F32)],
        compiler_params=_params(("arbitrary",)),
    )(f, b_pad)


def _forget_bwd(dc, f, b_pad, name):
    T, L = f.shape
    tb = _blk(T, 256)
    nb = T // tb

    def body(dc_ref, f_ref, b_ref, df_ref, db_ref, carry):
        @pl.when(pl.program_id(0) == 0)
        def _():
            carry[...] = jnp.zeros_like(carry)
            db_ref[...] = jnp.zeros_like(db_ref)

        rows = lax.broadcasted_iota(jnp.int32, (tb, tb), 0)
        cols = lax.broadcasted_iota(jnp.int32, (tb, tb), 1)
        tri = (cols >= rows).astype(F32)
        r = jnp.dot(tri, dc_ref[...], preferred_element_type=F32, precision=lax.Precision.HIGHEST) + carry[...]
        carry[...] = r[0:1, :]
        df = r * (1.0 - jax.nn.sigmoid(f_ref[...] + b_ref[...]))
        df_ref[...] = df.astype(BF16)
        db_ref[...] += jnp.sum(df, axis=0, keepdims=True)

    rev = pl.BlockSpec((tb, L), lambda i: (nb - 1 - i, 0))
    one = pl.BlockSpec((1, L), lambda i: (0, 0))
    return pl.pallas_call(
        body, name=name, grid=(nb,),
        in_specs=[rev, rev, one], out_specs=[rev, one],
        out_shape=[jax.ShapeDtypeStruct((T, L), BF16), jax.ShapeDtypeStruct((1, L), F32)],
        scratch_shapes=[pltpu.VMEM((1, L), F32)],
        compiler_params=_params(("arbitrary",)),
    )(dc, f, b_pad)


ATTN_TILE = 512
ATTN_CHAINS = 4


def _attn_geometry(T):
    ta = _blk(T, ATTN_TILE)
    nc = ATTN_CHAINS if (T // ta) % ATTN_CHAINS == 0 else 1
    return ta, nc, T // ta


def _causal_tile(ta, keys_on_rows=False):
    rows = lax.broadcasted_iota(jnp.int32, (ta, ta), 0)
    cols = lax.broadcasted_iota(jnp.int32, (ta, ta), 1)
    return rows <= cols if keys_on_rows else cols <= rows


def _chunk(ref, j, ta):
    return ref[pl.ds(pl.multiple_of(j * ta, ta), ta), :]


def _attn_fwd_keys_on_rows(z7, vt, c_rep, name):
    _, T, W = z7.shape
    H = W // HEAD_DIM
    ta, nc, n_chunks = _attn_geometry(T)
    scale = np.float32(1.0 / np.sqrt(HEAD_DIM))
    reps = ta // LANES

    def body(q_ref, k_ref, vt_ref, c_ref, o_ref, lse_ref):
        g = pl.program_id(1)

        def scores(ch, k):
            return _mm_nt(k, q_ref[ch * ta:(ch + 1) * ta, :])

        def update(state, raw, vt, cj, diagonal):
            m_prev, l_prev, acc_prev = state
            st = raw * (scale * LOG2E) - cj
            if diagonal:
                st = jnp.where(_causal_tile(ta, keys_on_rows=True), st, NEG_BIG)
            m_new = jnp.maximum(m_prev, jnp.max(st, axis=0, keepdims=True))
            alpha = jnp.exp2(m_prev - m_new)
            pt = jnp.exp2(st - m_new)
            l_new = alpha * l_prev + jnp.sum(pt, axis=0, keepdims=True)
            acc_new = alpha * acc_prev + _mm(vt, pt.astype(BF16))
            return m_new, l_new, acc_new

        def load(j):
            cj = _chunk(c_ref, j, ta)
            return _chunk(k_ref, j, ta), vt_ref[j], jnp.concatenate([cj] * reps, axis=1)

        def full_chunk(j, states):
            k, vt, cj = load(j)
            raws = [scores(ch, k) for ch in range(nc)]
            return tuple(update(states[ch], raws[ch], vt, cj, False) for ch in range(nc))

        first = (jnp.full((1, ta), NEG_BIG, F32), jnp.zeros((1, ta), F32), jnp.zeros((HEAD_DIM, ta), F32))
        states = list(lax.fori_loop(0, nc * g, full_chunk, (first,) * nc))
        for jj in range(nc):
            k, vt, cj = load(nc * g + jj)
            raws = {ch: scores(ch, k) for ch in range(jj, nc)}
            for ch in range(jj, nc):
                states[ch] = update(states[ch], raws[ch], vt, cj, ch == jj)
        for ch in range(nc):
            m, l, acc = states[ch]
            o_ref[ch * ta:(ch + 1) * ta, :] = (acc / l).T
            lse_ref[ch] = m + jnp.log2(l)

    tq = nc * ta
    return pl.pallas_call(
        body, name=name, grid=(H, n_chunks // nc),
        in_specs=[pl.BlockSpec((None, tq, HEAD_DIM), lambda h, g: (0, g, h)),
                  pl.BlockSpec((None, T, HEAD_DIM), lambda h, g: (1, 0, h)),
                  pl.BlockSpec((None, n_chunks, HEAD_DIM, ta), lambda h, g: (h, 0, 0, 0)),
                  pl.BlockSpec((None, T, LANES), lambda h, g: (h, 0, 0))],
        out_specs=[pl.BlockSpec((tq, HEAD_DIM), lambda h, g: (g, h)),
                   pl.BlockSpec((None, nc, 1, ta), lambda h, g: (h, g, 0, 0))],
        out_shape=[jax.ShapeDtypeStruct((T, W), F32), jax.ShapeDtypeStruct((H, n_chunks, 1, ta), F32)],
        compiler_params=_params(("parallel", "arbitrary")),
    )(z7, z7, vt, c_rep)


def _attn_bwd_fused(z7, kt, dob, c_rep, lse_chunks, d_chunks, name):
    _, T, W = z7.shape
    H = W // HEAD_DIM
    ta, nc, n_chunks = _attn_geometry(T)
    n_steps = n_chunks // nc
    scale = np.float32(1.0 / np.sqrt(HEAD_DIM))
    reps = ta // LANES

    def body(k_ref, v_ref, kt_ref, q_ref, do_ref, c_ref, lse_ref, d_ref,
             dk_ref, dv_ref, dck_ref, dq_ref, dcq_ref, dk_scr, dv_scr, dck_scr, dqt_scr, dcq_scr):
        g = pl.program_id(1)

        @pl.when(g == 0)
        def _():
            dqt_scr[...] = jnp.zeros_like(dqt_scr)
            dcq_scr[...] = jnp.zeros_like(dcq_scr)

        dk_scr[...] = jnp.zeros_like(dk_scr)
        dv_scr[...] = jnp.zeros_like(dv_scr)
        dck_scr[...] = jnp.zeros_like(dck_scr)

        def products(ch, q, do):
            rows = slice(ch * ta, (ch + 1) * ta)
            return _mm_nt(k_ref[rows, :], q), _mm_nt(v_ref[rows, :], do)

        def update(ch, i, q, do, prods, diagonal):
            rows = slice(ch * ta, (ch + 1) * ta)
            cj = c_ref[rows, :]
            st = prods[0] * (scale * LOG2E) - jnp.concatenate([cj] * reps, axis=1) - lse_ref[i]
            if diagonal:
                st = jnp.where(_causal_tile(ta, keys_on_rows=True), st, NEG_BIG)
            pt = jnp.exp2(st)
            dv_scr[ch] += _mm(pt.astype(BF16), do)
            dst = pt * (prods[1] - d_ref[i])
            dst_b = dst.astype(BF16)
            dk_scr[ch] += _mm(dst_b, q)
            dqt_scr[i] += _mm(kt_ref[ch], dst_b)
            dcq_scr[i] += jnp.sum(dst, axis=0, keepdims=True)
            lane_sum = dst[:, :LANES]
            for r in range(1, reps):
                lane_sum = lane_sum + dst[:, r * LANES:(r + 1) * LANES]
            dck_scr[ch] += lane_sum

        for ii in range(nc):
            i = nc * g + ii
            q = _chunk(q_ref, i, ta)
            do = _chunk(do_ref, i, ta)
            prods = [products(ch, q, do) for ch in range(0, ii + 1)]
            for ch in range(0, ii + 1):
                update(ch, i, q, do, prods[ch], ch == ii)

        def full_chunk(i, carry):
            q = _chunk(q_ref, i, ta)
            do = _chunk(do_ref, i, ta)
            prods = [products(ch, q, do) for ch in range(nc)]
            for ch in range(nc):
                update(ch, i, q, do, prods[ch], False)
            return carry

        lax.fori_loop(nc * (g + 1), n_chunks, full_chunk, 0)
        for ch in range(nc):
            rows = slice(ch * ta, (ch + 1) * ta)
            dk_ref[rows, :] = (dk_scr[ch] * scale).astype(BF16)
            dv_ref[rows, :] = dv_scr[ch].astype(BF16)
            ones = jnp.ones((8, LANES), F32)
            sums = lax.dot_general(ones, dck_scr[ch], (((1,), (1,)), ((), ())), preferred_element_type=F32,
                                   precision=lax.Precision.HIGHEST)
            dck_ref[ch] = -sums[0:1, :]

        @pl.when(g == n_steps - 1)
        def _():
            for i in range(n_chunks):
                dq_ref[i * ta:(i + 1) * ta, :] = (dqt_scr[i] * scale).T.astype(BF16)
            dcq_ref[...] = dcq_scr[...]

    tk = nc * ta
    chunks = pl.BlockSpec((None, n_chunks, 1, ta), lambda h, g: (h, 0, 0, 0))
    tile = pl.BlockSpec((tk, HEAD_DIM), lambda h, g: (g, h))
    return pl.pallas_call(
        body, name=name, grid=(H, n_steps),
        in_specs=[pl.BlockSpec((None, tk, HEAD_DIM), lambda h, g: (1, g, h)),
                  pl.BlockSpec((None, tk, HEAD_DIM), lambda h, g: (2, g, h)),
                  pl.BlockSpec((None, nc, HEAD_DIM, ta), lambda h, g: (h, g, 0, 0)),
                  pl.BlockSpec((None, T, HEAD_DIM), lambda h, g: (0, 0, h)),
                  pl.BlockSpec((T, HEAD_DIM), lambda h, g: (0, h)),
                  pl.BlockSpec((None, tk, LANES), lambda h, g: (h, g, 0)),
                  chunks, chunks],
        out_specs=[tile, tile, pl.BlockSpec((None, nc, 1, ta), lambda h, g: (h, g, 0, 0)),
                   pl.BlockSpec((T, HEAD_DIM), lambda h, g: (0, h)), chunks],
        out_shape=[jax.ShapeDtypeStruct((T, W), BF16), jax.ShapeDtypeStruct((T, W), BF16),
                   jax.ShapeDtypeStruct((H, n_chunks, 1, ta), F32), jax.ShapeDtypeStruct((T, W), BF16),
                   jax.ShapeDtypeStruct((H, n_chunks, 1, ta), F32)],
        scratch_shapes=[pltpu.VMEM((nc, ta, HEAD_DIM), F32), pltpu.VMEM((nc, ta, HEAD_DIM), F32),
                        pltpu.VMEM((nc, ta, LANES), F32), pltpu.VMEM((n_chunks, HEAD_DIM, ta), F32),
                        pltpu.VMEM((n_chunks, 1, ta), F32)],
        compiler_params=_params(("parallel", "arbitrary")),
    )(z7, z7, kt, z7, dob, c_rep, lse_chunks, d_chunks)


def _chunk_causal_mask():
    rows = lax.broadcasted_iota(jnp.int32, (SGU_LEN, SGU_LEN), 0)
    cols = lax.broadcasted_iota(jnp.int32, (SGU_LEN, SGU_LEN), 1)
    return (cols // CHUNK) <= (rows // CHUNK)


def _sgu_norm_mix(sv, lng_ref, lnb_ref, ws_ref, bs_ref, vn_scr, mixed_scr, vhat_scr=None):
    tm = sv.shape[0]
    vs = _gelu(sv)
    mask = _chunk_causal_mask()
    rstds = []
    for g in range(N_GROUPS):
        lanes = slice(g * GROUP_DIM, (g + 1) * GROUP_DIM)
        blk = vs[:, lanes]
        cen = blk - jnp.mean(blk, axis=-1, keepdims=True)
        rstd = lax.rsqrt(jnp.mean(cen * cen, axis=-1, keepdims=True) + LN_EPS)
        vhat = cen * rstd
        rstds.append(rstd)
        if vhat_scr is not None:
            vhat_scr[:, lanes] = vhat
        vn_scr[:, lanes] = (vhat * lng_ref[:, lanes] + lnb_ref[:, lanes]).astype(BF16)
        wm = jnp.where(mask, ws_ref[g], 0.0).astype(BF16)
        for w in range(tm // SGU_LEN):
            rows = slice(w * SGU_LEN, (w + 1) * SGU_LEN)
            mixed_scr[rows, lanes] = _mm(wm, vn_scr[rows, lanes]) + bs_ref[g]
    return rstds


def _mix_out_fwd(z7, o_a, x1, lng, lnb, ws, bs, w_out, g_post, name):
    _, T, W = z7.shape
    D = x1.shape[1]
    tm = _blk(T, 256)

    def body(u_ref, sv_ref, ga_ref, gb_ref, oa_ref, x1_ref, lng_ref, lnb_ref, ws_ref, bs_ref, wo_ref, gp_ref,
             x2_ref, p_ref, mb_ref, vn_scr, mixed_scr):
        _sgu_norm_mix(sv_ref[...].astype(F32), lng_ref, lnb_ref, ws_ref, bs_ref, vn_scr, mixed_scr)
        o_b = _gelu(u_ref[...].astype(F32)) * mixed_scr[...]
        merged = (jax.nn.sigmoid(ga_ref[...].astype(F32)) * oa_ref[...]
                  + jax.nn.sigmoid(gb_ref[...].astype(F32)) * o_b).astype(BF16)
        mb_ref[...] = merged
        p = _mm(merged, wo_ref[...])
        p_ref[...] = p
        x2_ref[...] = x1_ref[...] + p * _rms_scale(p) * gp_ref[...]

    def seg(idx):
        return pl.BlockSpec((None, tm, W), lambda i, idx=idx: (idx, i, 0))

    row = pl.BlockSpec((tm, D), lambda i: (i, 0))
    vec = pl.BlockSpec((1, D), lambda i: (0, 0))
    return pl.pallas_call(
        body, name=name, grid=(T // tm,),
        in_specs=[seg(3), seg(4), seg(5), seg(6), row, row, vec, vec,
                  pl.BlockSpec((N_GROUPS, SGU_LEN, SGU_LEN), lambda i: (0, 0, 0)),
                  pl.BlockSpec((N_GROUPS, SGU_LEN, 1), lambda i: (0, 0, 0)),
                  pl.BlockSpec((D, D), lambda i: (0, 0)), vec],
        out_specs=[row, row, row],
        out_shape=[jax.ShapeDtypeStruct((T, D), F32), jax.ShapeDtypeStruct((T, D), F32),
                   jax.ShapeDtypeStruct((T, D), BF16)],
        scratch_shapes=[pltpu.VMEM((tm, W), BF16), pltpu.VMEM((tm, W), F32)],
        compiler_params=_params(("parallel",)),
    )(z7, z7, z7, z7, o_a, x1, lng, lnb, ws, bs, w_out, g_post)


def _mix_out_bwd(dx2, p, z7, o_a, lng, lnb, ws, bs, w_out, g_post, name, dep=None):
    _, T, W = z7.shape
    D = dx2.shape[1]
    tm = _blk(T, 256)
    n_w = tm // SGU_LEN

    def body(dx2_ref, p_ref, u_ref, sv_ref, ga_ref, gb_ref, oa_ref, lng_ref, lnb_ref, ws_ref, bs_ref, wo_ref, gp_ref, _,
             dpb_ref, dob_ref, dvec_ref, dz_ref, dgp_ref, dlng_ref, dlnb_ref, dws_ref, dbs_ref,
             vn_scr, mixed_scr, vhat_scr, dmix_scr, dvn_scr):
        @pl.when(pl.program_id(0) == 0)
        def _():
            dgp_ref[...] = jnp.zeros_like(dgp_ref)
            dlng_ref[...] = jnp.zeros_like(dlng_ref)
            dlnb_ref[...] = jnp.zeros_like(dlnb_ref)
            dws_ref[...] = jnp.zeros_like(dws_ref)
            dbs_ref[...] = jnp.zeros_like(dbs_ref)

        pv = p_ref[...]
        s = _rms_scale(pv)
        n = pv * s
        dn = dx2_ref[...]
        dgp_ref[...] += jnp.sum(dn * n, axis=0, keepdims=True)
        dpb = _rms_bwd(dn, n, s, gp_ref[...]).astype(BF16)
        dpb_ref[...] = dpb
        dmerged = _mm_nt(dpb, wo_ref[...])

        sv = sv_ref[...].astype(F32)
        rstds = _sgu_norm_mix(sv, lng_ref, lnb_ref, ws_ref, bs_ref, vn_scr, mixed_scr, vhat_scr)
        u_pre = u_ref[...].astype(F32)
        u = _gelu(u_pre)
        mixed = mixed_scr[...]
        sa = jax.nn.sigmoid(ga_ref[...].astype(F32))
        sb = jax.nn.sigmoid(gb_ref[...].astype(F32))
        oa = oa_ref[...]
        do_a = (dmerged * sa).astype(BF16)
        dob_ref[...] = do_a
        prod = do_a.astype(F32) * oa
        for h in range(N_HEADS):
            sums = lax.dot_general(jnp.ones((8, LANES), F32), prod[:, h * HEAD_DIM:(h + 1) * HEAD_DIM],
                                   (((1,), (1,)), ((), ())), preferred_element_type=F32,
                                   precision=lax.Precision.HIGHEST)
            dvec_ref[h, 0] = sums[0:1, :]
        dz_ref[2] = (dmerged * oa * (sa * (1.0 - sa))).astype(BF16)
        dz_ref[3] = (dmerged * (u * mixed) * (sb * (1.0 - sb))).astype(BF16)
        do_b = dmerged * sb
        dz_ref[0] = (do_b * mixed * _gelu_grad(u_pre)).astype(BF16)
        dmix_scr[...] = do_b * u

        mask = _chunk_causal_mask()
        for g in range(N_GROUPS):
            lanes = slice(g * GROUP_DIM, (g + 1) * GROUP_DIM)
            wm = jnp.where(mask, ws_ref[g], 0.0).astype(BF16)
            dws = jnp.zeros((SGU_LEN, SGU_LEN), F32)
            dbs = jnp.zeros((SGU_LEN, 1), F32)
            for w in range(n_w):
                rows = slice(w * SGU_LEN, (w + 1) * SGU_LEN)
                dmix = dmix_scr[rows, lanes]
                dmix_b = dmix.astype(BF16)
                dvn_scr[rows, lanes] = _mm_tn(wm, dmix_b)
                dws = dws + _mm_nt(dmix_b, vn_scr[rows, lanes])
                dbs = dbs + jnp.sum(dmix, axis=-1, keepdims=True)
            dws_ref[g] += jnp.where(mask, dws, 0.0)
            dbs_ref[g] += dbs
            dvn = dvn_scr[:, lanes]
            vhat = vhat_scr[:, lanes]
            dlng_ref[:, lanes] += jnp.sum(dvn * vhat, axis=0, keepdims=True)
            dlnb_ref[:, lanes] += jnp.sum(dvn, axis=0, keepdims=True)
            dvh = dvn * lng_ref[:, lanes]
            dvs = rstds[g] * (dvh - jnp.mean(dvh, axis=-1, keepdims=True)
                              - vhat * jnp.mean(dvh * vhat, axis=-1, keepdims=True))
            dvn_scr[:, lanes] = dvs
        dz_ref[1] = (dvn_scr[...] * _gelu_grad(sv)).astype(BF16)

    def seg(idx):
        return pl.BlockSpec((None, tm, W), lambda i, idx=idx: (idx, i, 0))

    row = pl.BlockSpec((tm, D), lambda i: (i, 0))
    vec = pl.BlockSpec((1, D), lambda i: (0, 0))
    ws_spec = pl.BlockSpec((N_GROUPS, SGU_LEN, SGU_LEN), lambda i: (0, 0, 0))
    bs_spec = pl.BlockSpec((N_GROUPS, SGU_LEN, 1), lambda i: (0, 0, 0))
    return pl.pallas_call(
        body, name=name, grid=(T // tm,),
        in_specs=[row, row, seg(3), seg(4), seg(5), seg(6), row, vec, vec, ws_spec, bs_spec,
                  pl.BlockSpec((D, D), lambda i: (0, 0)), vec, ANY],
        out_specs=[row, row, pl.BlockSpec((N_HEADS, 1, 1, tm), lambda i: (0, i, 0, 0)),
                   pl.BlockSpec((4, tm, W), lambda i: (0, i, 0)), vec, vec, vec, ws_spec, bs_spec],
        out_shape=[jax.ShapeDtypeStruct((T, D), BF16), jax.ShapeDtypeStruct((T, W), BF16),
                   jax.ShapeDtypeStruct((N_HEADS, T // tm, 1, tm), F32), jax.ShapeDtypeStruct((4, T, W), BF16),
                   jax.ShapeDtypeStruct((1, D), F32), jax.ShapeDtypeStruct((1, D), F32),
                   jax.ShapeDtypeStruct((1, D), F32),
                   jax.ShapeDtypeStruct((N_GROUPS, SGU_LEN, SGU_LEN), F32),
                   jax.ShapeDtypeStruct((N_GROUPS, SGU_LEN, 1), F32)],
        scratch_shapes=[pltpu.VMEM((tm, W), BF16), pltpu.VMEM((tm, W), F32), pltpu.VMEM((tm, W), F32),
                        pltpu.VMEM((tm, W), F32), pltpu.VMEM((tm, W), F32)],
        compiler_params=_params(("arbitrary",)),
    )(dx2, p, z7, z7, z7, z7, o_a, lng, lnb, ws, bs, w_out, g_post, _after(dep))


def _loss_head(y, target, name):
    T, D = y.shape
    tm = _blk(T, 1024)
    n_i = T // tm

    def body(y_ref, t_ref, dy_ref, loss_ref, acc_scr):
        i = pl.program_id(0)

        @pl.when(i == 0)
        def _():
            acc_scr[...] = jnp.zeros_like(acc_scr)

        e = y_ref[...] - t_ref[...]
        dy_ref[...] = e * np.float32(1.0 / D)
        acc_scr[...] += jnp.sum(e * e, axis=0, keepdims=True)

        @pl.when(i == n_i - 1)
        def _():
            total = jnp.sum(acc_scr[...], axis=-1, keepdims=True) * np.float32(0.5 / D)
            loss_ref[...] = jnp.broadcast_to(total, loss_ref.shape)

    row = pl.BlockSpec((tm, D), lambda i: (i, 0))
    return pl.pallas_call(
        body, name=name, grid=(n_i,),
        in_specs=[row, row],
        out_specs=[row, pl.BlockSpec((1, LANES), lambda i: (0, 0))],
        out_shape=[jax.ShapeDtypeStruct((T, D), F32), jax.ShapeDtypeStruct((1, LANES), F32)],
        scratch_shapes=[pltpu.VMEM((1, D), F32)],
        compiler_params=_params(("arbitrary",)),
    )(y, target)


def _adamw_math(w, g, m, v):
    m_new = ADAM_B1 * m + (1.0 - ADAM_B1) * g
    v_new = ADAM_B2 * v + (1.0 - ADAM_B2) * (g * g)
    m_hat = m_new / np.float32(1.0 - ADAM_B1 ** ADAM_STEP)
    v_hat = v_new / np.float32(1.0 - ADAM_B2 ** ADAM_STEP)
    delta = -ADAM_LR * (m_hat / (jnp.sqrt(v_hat) + ADAM_EPS) + ADAM_WD * w)
    return delta, m_new, v_new


def _sum_adamw(parts, w, m, v, name, dep=None):
    n, R, C = parts.shape
    tr = _blk(R, 512)

    def body(p_ref, w_ref, m_ref, v_ref, _, g_ref, d_ref, mo_ref, vo_ref):
        g = p_ref[0].astype(F32)
        for s in range(1, n):
            g = g + p_ref[s].astype(F32)
        delta, m_new, v_new = _adamw_math(w_ref[...], g, m_ref[...], v_ref[...])
        g_ref[...] = g
        d_ref[...] = delta
        mo_ref[...] = m_new
        vo_ref[...] = v_new

    row = pl.BlockSpec((tr, C), lambda i: (i, 0))
    shp = jax.ShapeDtypeStruct((R, C), F32)
    return pl.pallas_call(
        body, name=name, grid=(R // tr,),
        in_specs=[pl.BlockSpec((n, tr, C), lambda i: (0, i, 0)), row, row, row, ANY],
        out_specs=[row, row, row, row], out_shape=[shp, shp, shp, shp],
        compiler_params=_params(("parallel",)),
    )(parts, w, m, v, _after(dep))


def _position():
    return lax.axis_index("x"), lax.axis_index("y"), lax.axis_index("c")


def _slot(px, py, pc):
    return 4 * px + 2 * py + pc


def _all_gather(shards, name):
    n = len(shards)

    def body(*refs):
        ins, outs = refs[:n], refs[n:2 * n]
        send_sems, recv_sems, local_sems = refs[2 * n:]
        x, y, c = _position()
        me, sibling = (x, y, c), (x, y, 1 - c)
        chips = [(1 - x, y), (x, 1 - y), (1 - x, 1 - y)]

        def copy(a, k, block, to, src=None):
            dst = outs[a].at[_slot(*block)]
            return pltpu.make_async_remote_copy(
                src_ref=dst if src is None else src, dst_ref=dst,
                send_sem=send_sems.at[a, k], recv_sem=recv_sems.at[a, k],
                device_id=to, device_id_type=MESH)

        mine = [pltpu.make_async_copy(ins[a], outs[a].at[_slot(*me)], local_sems.at[a]) for a in range(n)]
        for cp in mine:
            cp.start()
        first = []
        for a in range(n):
            first.append(copy(a, 0, me, sibling, src=ins[a]))
            first += [copy(a, 1 + j, me, (*chip, c), src=ins[a]) for j, chip in enumerate(chips)]
        for cp in first:
            cp.start()
        passed = []
        for j, chip in enumerate(chips):
            for a in range(n):
                copy(a, 1 + j, (*chip, c), me).wait_recv()
                fwd = copy(a, 4 + j, (*chip, c), sibling)
                fwd.start()
                passed.append(fwd)
        for a in range(n):
            copy(a, 0, sibling, me).wait_recv()
            for j, chip in enumerate(chips):
                copy(a, 4 + j, (*chip, 1 - c), me).wait_recv()
        for cp in first + passed:
            cp.wait_send()
        for cp in mine:
            cp.wait()

    return pl.pallas_call(
        body, name=name,
        in_specs=[ANY] * n, out_specs=[ANY] * n,
        out_shape=[jax.ShapeDtypeStruct((N_DEV,) + s.shape, s.dtype) for s in shards],
        scratch_shapes=[pltpu.SemaphoreType.DMA((n, 7)), pltpu.SemaphoreType.DMA((n, 7)),
                        pltpu.SemaphoreType.DMA((n,))],
    )(*shards)


def _peer(x, y, c, k):
    return (1 - x if k & 4 else x, 1 - y if k & 2 else y, 1 - c if k & 1 else c)


def _remote_copies(src_refs, land_refs, send_sems, recv_sems, gather, outgoing):
    x, y, c = _position()
    me = _slot(x, y, c)
    copies = []
    for k in range(1, N_DEV):
        peer = _peer(x, y, c, k)
        for a in range(len(src_refs)):
            src = src_refs[a] if gather else src_refs[a].at[_slot(*peer)]
            dst = land_refs[a].at[me if outgoing else _slot(*peer)]
            sem = a * (N_DEV - 1) + k - 1
            copies.append(pltpu.make_async_remote_copy(
                src_ref=src, dst_ref=dst, send_sem=send_sems.at[sem], recv_sem=recv_sems.at[sem],
                device_id=peer, device_id_type=MESH))
    return copies


def _sequencer_exchange(srcs, name, gather, collective_id):
    n = len(srcs)
    hbm = pltpu.MemorySpace.HBM
    src_refs = [jax.new_ref(s, memory_space=hbm) for s in srcs]
    land_refs = [jax.empty_ref(jax.ShapeDtypeStruct(((N_DEV,) + s.shape) if gather else s.shape, s.dtype),
                               memory_space=hbm) for s in srcs]
    n_sems = n * (N_DEV - 1)
    block_bytes = sum(s.size * s.dtype.itemsize // (1 if gather else N_DEV) for s in srcs)
    cost = pl.CostEstimate(flops=0, transcendentals=0, bytes_accessed=2 * N_DEV * block_bytes,
                           remote_bytes_transferred=(N_DEV - 1) * block_bytes)

    @pl.kernel(mesh=plsc.ScalarSubcoreMesh(axis_name="sequencer", num_cores=1), name=name,
               scratch_types=(pltpu.SemaphoreType.DMA((n_sems,)), pltpu.SemaphoreType.DMA((n_sems,)),
                              pltpu.SemaphoreType.DMA((n,))),
               cost_estimate=cost,
               compiler_params=pltpu.CompilerParams(collective_id=collective_id))
    def launch(send_sems, recv_sems, local_sems):
        x, y, c = _position()
        me = _slot(x, y, c)
        barrier = pltpu.get_barrier_semaphore()
        for k in range(1, N_DEV):
            pl.semaphore_signal(barrier, inc=1, device_id=_peer(x, y, c, k), device_id_type=MESH)
        pl.semaphore_wait(barrier, N_DEV - 1)
        mine = [pltpu.make_async_copy(src_refs[a] if gather else src_refs[a].at[me], land_refs[a].at[me],
                                      local_sems.at[a]) for a in range(n)]
        for cp in mine:
            cp.start()
        sends = _remote_copies(src_refs, land_refs, send_sems, recv_sems, gather, outgoing=True)
        for cp in sends:
            cp.start()
        for cp in _remote_copies(src_refs, land_refs, send_sems, recv_sems, gather, outgoing=False):
            cp.wait_recv()
        for cp in sends:
            cp.wait_send()
        for cp in mine:
            cp.wait()

    launch()
    return [r[...] for r in land_refs]


SMALL_VECS = ("ffn1_pre_g", "ffn1_post_g", "mix_pre_g", "sgu_ln_g", "sgu_ln_b", "mix_post_g", "ffn2_pre_g",
              "ffn2_post_g")
ROW_BS = len(SMALL_VECS)
ROW_BF = ROW_BS + 1
ROW_LOSS = ROW_BF + 1
ROW_WS = 16
BLOB_ROWS = ROW_WS + SGU_LEN


def _pack_small(vals, D, loss_row=None):
    rows = [vals[n].reshape(1, D) for n in SMALL_VECS]
    rows.append(vals["sgu_b_s"].reshape(1, D))
    rows.append(jnp.pad(vals["b_forget"].reshape(1, N_HEADS), ((0, 0), (0, D - N_HEADS))))
    rows.append(jnp.zeros((1, D), F32) if loss_row is None else loss_row)
    rows.append(jnp.zeros((ROW_WS - ROW_LOSS - 1, D), F32))
    rows.append(vals["sgu_w_s"].reshape(SGU_LEN, D))
    return jnp.concatenate(rows, axis=0)


def _unpack_small(blob, D):
    out = {n: blob[r:r + 1] for r, n in enumerate(SMALL_VECS)}
    out["sgu_b_s"] = blob[ROW_BS].reshape(1, N_GROUPS, SGU_LEN)
    out["b_forget"] = blob[ROW_BF, :N_HEADS].reshape(1, N_HEADS)
    out["sgu_w_s"] = blob[ROW_WS:].reshape(1, N_GROUPS, SGU_LEN, SGU_LEN)
    return out


WEIGHT_NAMES = ("ffn1_pre_g", "ffn1_w_gate", "ffn1_w_up", "ffn1_w_down", "ffn1_post_g", "mix_pre_g", "w_in",
                "b_forget", "sgu_ln_g", "sgu_ln_b", "sgu_w_s", "sgu_b_s", "w_out", "mix_post_g", "ffn2_pre_g",
                "ffn2_w_gate", "ffn2_w_up", "ffn2_w_down", "ffn2_post_g")
BIG_NAMES = ("ffn1_w_gate", "ffn1_w_up", "ffn1_w_down", "w_in", "w_out", "ffn2_w_gate", "ffn2_w_up", "ffn2_w_down")
WEIGHT_GROUPS = {"ffn1": ("ffn1_w_gate", "ffn1_w_up", "ffn1_w_down"), "mix": ("w_in", "w_out"),
                 "ffn2": ("ffn2_w_gate", "ffn2_w_up", "ffn2_w_down")}
GRAD_GROUPS = (("ffn2_w_gate", "ffn2_w_up", "ffn2_w_down"), ("w_in", "w_out"), ("ffn1_w_down",), ("ffn1_w_gate",),
               ("ffn1_w_up",))


def _local_step(x, target, small, fetch, emit, consume):
    T, D = x.shape
    W = N_HEADS * HEAD_DIM
    vec = lambda n: small[n].reshape(1, D)
    big = dict(fetch("ffn1", x))

    x1, y1, dgf1, silu1, act1 = _ffn_fwd(x, vec("ffn1_pre_g"), big["ffn1_w_gate"], big["ffn1_w_up"], big["ffn1_w_down"],
                                  vec("ffn1_post_g"), "ffn1_fwd")

    big.update(fetch("mix", x1))
    w_in_all = big["w_in"]
    in_width = N_DEV * w_in_all.shape[2]
    w_in = w_in_all.transpose(1, 0, 2).reshape(D, in_width)
    col_f = 3 * W
    col_u = col_f + N_HEADS
    seg_starts = (0, W, 2 * W, col_u, col_u + W, col_u + 2 * W, col_u + 3 * W)
    w7 = jnp.stack([w_in[:, s:s + W] for s in seg_starts])
    wf = jnp.pad(w_in[:, col_f:col_u], ((0, 0), (0, LANES - N_HEADS)))
    w_out = big["w_out"].reshape(D, D)
    b_pad = jnp.pad(small["b_forget"].reshape(1, N_HEADS), ((0, 0), (0, LANES - N_HEADS)))
    lng, lnb = vec("sgu_ln_g"), vec("sgu_ln_b")
    ws = small["sgu_w_s"].reshape(N_GROUPS, SGU_LEN, SGU_LEN)
    bs = small["sgu_b_s"].reshape(N_GROUPS, SGU_LEN, 1)

    z7, f_logit, h2b = _mix_in_fwd(x1, vec("mix_pre_g"), w7, wf, "mix_in_fwd")
    c_rep = _forget_cumsum(f_logit, b_pad, "forget_cumsum")
    ta, _, n_chunks = _attn_geometry(T)
    vt = z7[2].reshape(n_chunks, ta, N_HEADS, HEAD_DIM).transpose(2, 0, 3, 1)
    o_a, lse_chunks = _attn_fwd_keys_on_rows(z7, vt, c_rep, "attn_fwd")
    x2, p, merged_b = _mix_out_fwd(z7, o_a, x1, lng, lnb, ws, bs, w_out, vec("mix_post_g"), "mix_out_fwd")
    big.update(fetch("ffn2", x2))
    x3, y2, dgf2, silu2, act2 = _ffn_fwd(x2, vec("ffn2_pre_g"), big["ffn2_w_gate"], big["ffn2_w_up"], big["ffn2_w_down"],
                                  vec("ffn2_post_g"), "ffn2_fwd")
    dy, loss_lanes = _loss_head(x3, target, "loss_head")

    grads_small = {}

    dx2, h3b, dy2b, dgate2, dup2, dgpre, dgpost = _ffn_bwd(
        dy, x2, y2, dgf2, silu2, vec("ffn2_pre_g"), big["ffn2_w_gate"], big["ffn2_w_up"], big["ffn2_w_down"],
        vec("ffn2_post_g"), "ffn2_bwd")
    grads_small["ffn2_pre_g"] = jnp.sum(dgpre, axis=0)
    grads_small["ffn2_post_g"] = jnp.sum(dgpost, axis=0)
    dep = emit("ffn2_w_gate", _wgrad(h3b, dgate2, "ffn2_wgrad_gate", shard_cols=True))
    dep = emit("ffn2_w_up", _wgrad(h3b, dup2, "ffn2_wgrad_up", shard_cols=True, dep=dep))
    dep = emit("ffn2_w_down", _wgrad(act2, dy2b, "ffn2_wgrad_down", dep=dep).reshape(big["ffn2_w_down"].shape))

    dpb, dob, dvec, dz4, dgp, dlng, dlnb, dws, dbs = _mix_out_bwd(
        dx2, p, z7, o_a, lng, lnb, ws, bs, w_out, vec("mix_post_g"), "mix_out_bwd", dep=dep)
    grads_small["mix_post_g"] = dgp
    grads_small["sgu_ln_g"] = dlng
    grads_small["sgu_ln_b"] = dlnb
    grads_small["sgu_w_s"] = dws
    grads_small["sgu_b_s"] = dbs
    d_chunks = dvec.reshape(N_HEADS, n_chunks, 1, ta)
    kt = z7[1].reshape(n_chunks, ta, N_HEADS, HEAD_DIM).transpose(2, 0, 3, 1)
    dk, dv, dc, dq, dc_q = _attn_bwd_fused(z7, kt, dob, c_rep, lse_chunks, d_chunks, "attn_bwd")
    dc_pad = jnp.pad((dc + dc_q).reshape(N_HEADS, T).T, ((0, 0), (0, LANES - N_HEADS)))
    dfb, dbf = _forget_bwd(dc_pad, f_logit, b_pad, "forget_bwd")
    grads_small["b_forget"] = dbf[:, :N_HEADS]
    segs = [(dq, None), (dk, None), (dv, None), (dz4, 0), (dz4, 1), (dz4, 2), (dz4, 3)]
    dep = consume(("ffn2_w_gate", "ffn2_w_up", "ffn2_w_down"))
    dx1, dgm = _mix_in_bwd(dx2, x1, vec("mix_pre_g"), segs, dfb, w7, wf, "mix_in_bwd", dep=dep)
    grads_small["mix_pre_g"] = jnp.sum(dgm, axis=0)
    dw_qkv = _wgrad_multi(h2b, segs[:3], "w_in_wgrad_qkv", dep=dx1)
    dw_rest = _wgrad_multi(h2b, segs[3:], "w_in_wgrad_gates", dep=dw_qkv)
    dw_seg = [dw_qkv[:, q * W:(q + 1) * W] for q in range(3)] + [dw_rest[:, q * W:(q + 1) * W] for q in range(4)]
    dwf = _wgrad(h2b, dfb, "w_in_wgrad_f", dep=dw_rest)
    dw_in = jnp.concatenate(dw_seg[:3] + [dwf[:, :N_HEADS]] + dw_seg[3:], axis=1)
    emit("w_in", dw_in.reshape(D, N_DEV, in_width // N_DEV).transpose(1, 0, 2))
    dep = emit("w_out", _wgrad(merged_b, dpb, "w_out_wgrad", dep=dwf).reshape(big["w_out"].shape))

    dx0, h1b, dy1b, dgate1, dup1, dgpre1, dgpost1 = _ffn_bwd(
        dx1, x, y1, dgf1, silu1, vec("ffn1_pre_g"), big["ffn1_w_gate"], big["ffn1_w_up"], big["ffn1_w_down"],
        vec("ffn1_post_g"), "ffn1_bwd", dep=dep)
    grads_small["ffn1_pre_g"] = jnp.sum(dgpre1, axis=0)
    grads_small["ffn1_post_g"] = jnp.sum(dgpost1, axis=0)
    dep = consume(("w_in", "w_out"))
    dep = emit("ffn1_w_down", _wgrad(act1, dy1b, "ffn1_wgrad_down", dep=dep).reshape(big["ffn1_w_down"].shape))
    dep = emit("ffn1_w_gate", _wgrad(h1b, dgate1, "ffn1_wgrad_gate", shard_cols=True, dep=dep))
    dep = emit("ffn1_w_up", _wgrad(h1b, dup1, "ffn1_wgrad_up", shard_cols=True, dep=dep))

    loss_row = jnp.pad(loss_lanes, ((0, 0), (0, D - LANES)))
    return loss_row, dx0, grads_small


def kernel(x, ffn1_pre_g, ffn1_w_gate, ffn1_w_up, ffn1_w_down, ffn1_post_g, mix_pre_g, w_in, b_forget, sgu_ln_g, sgu_ln_b, sgu_w_s, sgu_b_s, w_out, mix_post_g, ffn2_pre_g, ffn2_w_gate, ffn2_w_up, ffn2_w_down, ffn2_post_g, loss_target, m_ffn1_pre_g, m_ffn1_w_gate, m_ffn1_w_up, m_ffn1_w_down, m_ffn1_post_g, m_mix_pre_g, m_w_in, m_b_forget, m_sgu_ln_g, m_sgu_ln_b, m_sgu_w_s, m_sgu_b_s, m_w_out, m_mix_post_g, m_ffn2_pre_g, m_ffn2_w_gate, m_ffn2_w_up, m_ffn2_w_down, m_ffn2_post_g, v_ffn1_pre_g, v_ffn1_w_gate, v_ffn1_w_up, v_ffn1_w_down, v_ffn1_post_g, v_mix_pre_g, v_w_in, v_b_forget, v_sgu_ln_g, v_sgu_ln_b, v_sgu_w_s, v_sgu_b_s, v_w_out, v_mix_post_g, v_ffn2_pre_g, v_ffn2_w_gate, v_ffn2_w_up, v_ffn2_w_down, v_ffn2_post_g):
    weights = dict(zip(WEIGHT_NAMES, (ffn1_pre_g, ffn1_w_gate, ffn1_w_up, ffn1_w_down, ffn1_post_g, mix_pre_g, w_in,
                                      b_forget, sgu_ln_g, sgu_ln_b, sgu_w_s, sgu_b_s, w_out, mix_post_g, ffn2_pre_g,
                                      ffn2_w_gate, ffn2_w_up, ffn2_w_down, ffn2_post_g)))
    mom1 = dict(zip(WEIGHT_NAMES, (m_ffn1_pre_g, m_ffn1_w_gate, m_ffn1_w_up, m_ffn1_w_down, m_ffn1_post_g,
                                   m_mix_pre_g, m_w_in, m_b_forget, m_sgu_ln_g, m_sgu_ln_b, m_sgu_w_s, m_sgu_b_s,
                                   m_w_out, m_mix_post_g, m_ffn2_pre_g, m_ffn2_w_gate, m_ffn2_w_up, m_ffn2_w_down,
                                   m_ffn2_post_g)))
    mom2 = dict(zip(WEIGHT_NAMES, (v_ffn1_pre_g, v_ffn1_w_gate, v_ffn1_w_up, v_ffn1_w_down, v_ffn1_post_g,
                                   v_mix_pre_g, v_w_in, v_b_forget, v_sgu_ln_g, v_sgu_ln_b, v_sgu_w_s, v_sgu_b_s,
                                   v_w_out, v_mix_post_g, v_ffn2_pre_g, v_ffn2_w_gate, v_ffn2_w_up, v_ffn2_w_down,
                                   v_ffn2_post_g)))
    D = x.shape[-1]
    small_names = [n for n in WEIGHT_NAMES if n not in BIG_NAMES]

    small = {n: weights[n] for n in small_names}
    shard = lambda n: weights[n][0].astype(BF16)

    ffn1_full = _all_gather([shard(n) for n in WEIGHT_GROUPS["ffn1"]], "ffn1_all_gather")
    gathered = {}
    for cid, grp in ((1, "mix"), (2, "ffn2")):
        shards, _ = lax.optimization_barrier(([shard(n) for n in WEIGHT_GROUPS[grp]], ffn1_full[0]))
        gathered[grp] = _sequencer_exchange(shards, grp + "_gather", True, cid)

    def fetch(group, after):
        if group == "ffn1":
            return zip(WEIGHT_GROUPS[group], ffn1_full)
        arrived, _ = lax.optimization_barrier((gathered[group], after))
        return zip(WEIGHT_GROUPS[group], arrived)

    ready, received = {}, {}

    def emit(name, part):
        ready[name] = part
        for gi, group in enumerate(GRAD_GROUPS):
            if name == group[-1]:
                lands = _sequencer_exchange([ready[n] for n in group], name + "_grad_exchange", False, 3 + gi)
                received.update(zip(group, lands))
        return part

    out = {}

    def consume(names, dep=None):
        for n in names:
            g, d, m_new, v_new = _sum_adamw(received[n], weights[n][0], mom1[n][0], mom2[n][0], "adamw_" + n, dep=dep)
            out[n] = tuple(a[None] for a in (g, d, m_new, v_new))
            dep = g
        return dep

    loss_row, grad_x, grads_small = _local_step(x[0], loss_target[0], small, fetch, emit, consume)

    blobs = _sequencer_exchange([_pack_small(grads_small, D, loss_row)], "small_gather", True,
                                3 + len(GRAD_GROUPS))[0]
    blob, d_blob, m_blob, v_blob = _sum_adamw(
        blobs, _pack_small(small, D), _pack_small({n: mom1[n] for n in small_names}, D),
        _pack_small({n: mom2[n] for n in small_names}, D), "adamw_small")
    consume(("ffn1_w_down", "ffn1_w_gate", "ffn1_w_up"), dep=blob)
    unpacked = [_unpack_small(b, D) for b in (blob, d_blob, m_blob, v_blob)]
    for n in small_names:
        out[n] = tuple(u[n].reshape(weights[n].shape) for u in unpacked)

    loss = blob[ROW_LOSS, 0]
    result = [loss, grad_x[None]]
    for k in range(4):
        result += [out[n][k] for n in WEIGHT_NAMES]
    return tuple(result)
```

```python
import numpy as np
import jax
import jax.numpy as jnp
from jax import lax
from jax.experimental import pallas as pl
from jax.experimental.pallas import tpu as pltpu
from jax.experimental.pallas import tpu_sc as plsc

F32 = jnp.float32
BF16 = jnp.bfloat16

RMS_EPS = 1e-6
LN_EPS = 1e-5
HEAD_DIM = 128
N_HEADS = 8
GROUP_DIM = 128
N_GROUPS = 8
SGU_LEN = 128
CHUNK = 64
N_DEV = 8
LANES = 128
VMEM_LIMIT = 56 * 1024 * 1024
NEG_BIG = -1e30
LOG2E = np.float32(1.0 / np.log(2.0))

ADAM_LR = 0.001
ADAM_B1 = 0.9
ADAM_B2 = 0.999
ADAM_EPS = 1e-08
ADAM_WD = 0.01
ADAM_STEP = 10

MESH = pl.DeviceIdType.MESH
ANY = pl.BlockSpec(memory_space=pl.ANY)


def _blk(n, pref):
    return pref if (n >= pref and n % pref == 0) else n


def _mm(a, b):
    return jnp.dot(a, b, preferred_element_type=F32)


def _mm_nt(a, b):
    return lax.dot_general(a, b, (((1,), (1,)), ((), ())), preferred_element_type=F32)


def _mm_tn(a, b):
    return lax.dot_general(a, b, (((0,), (0,)), ((), ())), preferred_element_type=F32)


def _params(sem):
    return pltpu.CompilerParams(dimension_semantics=sem, vmem_limit_bytes=VMEM_LIMIT)


def _normal_cdf(x):
    return 0.5 * (1.0 + lax.erf(x * np.float32(1.0 / np.sqrt(2.0))))


def _gelu(x):
    return x * _normal_cdf(x)


def _gelu_and_grad(x):
    cdf = _normal_cdf(x)
    return x * cdf, cdf + x * jnp.exp(-0.5 * x * x) * np.float32(1.0 / np.sqrt(2.0 * np.pi))


def _rms_scale(v):
    return lax.rsqrt(jnp.mean(v * v, axis=-1, keepdims=True) + RMS_EPS)


def _rms_bwd(dy, xhat, r, g):
    dxh = dy * g
    return r * (dxh - xhat * jnp.mean(dxh * xhat, axis=-1, keepdims=True))


def _ffn_fwd(x, g_pre, wg, wu, wd, g_post, name):
    T, D = x.shape
    ns, _, fs = wg.shape
    tm = _blk(T, 256)

    def body(x_ref, gpre_ref, wg_ref, wu_ref, wd_ref, gpost_ref, xo_ref, y_ref, dgf_ref, silu_ref, act_ref):
        xv = x_ref[...]
        h = (xv * _rms_scale(xv) * gpre_ref[...]).astype(BF16)
        y = jnp.zeros((tm, D), F32)
        pre = (_mm(h, wg_ref[0]), _mm(h, wu_ref[0]))
        for j in range(ns):
            gg, uu = pre
            if j + 1 < ns:
                pre = (_mm(h, wg_ref[j + 1]), _mm(h, wu_ref[j + 1]))
            cols = slice(j * fs, (j + 1) * fs)
            sg = jax.nn.sigmoid(gg)
            silu = gg * sg
            act = (silu * uu).astype(BF16)
            dgf_ref[:, cols] = (uu * (sg * (1.0 + gg * (1.0 - sg)))).astype(BF16)
            silu_ref[:, cols] = silu.astype(BF16)
            act_ref[:, cols] = act
            y = y + _mm(act, wd_ref[j])
        y_ref[...] = y
        xo_ref[...] = xv + 0.5 * (y * _rms_scale(y) * gpost_ref[...])

    row = pl.BlockSpec((tm, D), lambda i: (i, 0))
    vec = pl.BlockSpec((1, D), lambda i: (0, 0))
    wide = pl.BlockSpec((tm, ns * fs), lambda i: (i, 0))
    return pl.pallas_call(
        body, name=name, grid=(T // tm,),
        in_specs=[row, vec,
                  pl.BlockSpec((ns, D, fs), lambda i: (0, 0, 0), pipeline_mode=pl.Buffered(1)),
                  pl.BlockSpec((ns, D, fs), lambda i: (0, 0, 0), pipeline_mode=pl.Buffered(1)),
                  pl.BlockSpec((ns, fs, D), lambda i: (0, 0, 0), pipeline_mode=pl.Buffered(1)),
                  vec],
        out_specs=[row, row, wide, wide, wide],
        out_shape=[jax.ShapeDtypeStruct((T, D), F32), jax.ShapeDtypeStruct((T, D), F32)]
        + [jax.ShapeDtypeStruct((T, ns * fs), BF16)] * 3,
        compiler_params=_params(("parallel",)),
    )(x, g_pre, wg, wu, wd, g_post)


def _after(dep):
    return jnp.zeros((8, LANES), F32) if dep is None else dep


def _ffn_bwd(dxo, x, y, dgf, silu, g_pre, wg, wu, wd, g_post, name, dep=None):
    T, D = x.shape
    ns, _, fs = wg.shape
    tm = _blk(T, 256)
    n_i = T // tm

    def body(dxo_ref, x_ref, y_ref, dgf_ref, silu_ref, gpre_ref, wg_ref, wu_ref, wd_ref, gpost_ref, _,
             dx_ref, hb_ref, dyb_ref, dgb_ref, dub_ref, dgpre_ref, dgpost_ref):
        yv = y_ref[...]
        s = _rms_scale(yv)
        n = yv * s
        dxo = dxo_ref[...]
        dn = 0.5 * dxo
        dgpost_ref[...] = jnp.sum(dn * n, axis=0, keepdims=True)
        dyv = _rms_bwd(dn, n, s, gpost_ref[...]).astype(BF16)
        dyb_ref[...] = dyv
        xv = x_ref[...]
        rs = _rms_scale(xv)
        xhat = xv * rs
        hb_ref[...] = (xhat * gpre_ref[...]).astype(BF16)

        dh = jnp.zeros((tm, D), F32)
        da = _mm_nt(dyv, wd_ref[0])
        for j in range(ns):
            cur = da
            if j + 1 < ns:
                da = _mm_nt(dyv, wd_ref[j + 1])
            cols = slice(j * fs, (j + 1) * fs)
            dgate = (cur * dgf_ref[:, cols].astype(F32)).astype(BF16)
            dup = (cur * silu_ref[:, cols].astype(F32)).astype(BF16)
            dgb_ref[:, cols] = dgate
            dub_ref[:, cols] = dup
            dh = dh + _mm_nt(dgate, wg_ref[j]) + _mm_nt(dup, wu_ref[j])

        dgpre_ref[...] = jnp.sum(dh * xhat, axis=0, keepdims=True)
        dx_ref[...] = _rms_bwd(dh, xhat, rs, gpre_ref[...]) + dxo

    F = ns * fs
    row = pl.BlockSpec((tm, D), lambda i: (i, 0))
    vec = pl.BlockSpec((1, D), lambda i: (0, 0))
    wide = pl.BlockSpec((tm, F), lambda i: (i, 0))
    part = pl.BlockSpec((None, 1, D), lambda i: (i, 0, 0))
    return pl.pallas_call(
        body, name=name, grid=(n_i,),
        in_specs=[row, row, row, wide, wide, vec,
                  pl.BlockSpec((ns, D, fs), lambda i: (0, 0, 0), pipeline_mode=pl.Buffered(1)),
                  pl.BlockSpec((ns, D, fs), lambda i: (0, 0, 0), pipeline_mode=pl.Buffered(1)),
                  pl.BlockSpec((ns, fs, D), lambda i: (0, 0, 0), pipeline_mode=pl.Buffered(1)),
                  vec, ANY],
        out_specs=[row, row, row, wide, wide, part, part],
        out_shape=[jax.ShapeDtypeStruct((T, D), F32), jax.ShapeDtypeStruct((T, D), BF16),
                   jax.ShapeDtypeStruct((T, D), BF16), jax.ShapeDtypeStruct((T, F), BF16),
                   jax.ShapeDtypeStruct((T, F), BF16),
                   jax.ShapeDtypeStruct((n_i, 1, D), F32), jax.ShapeDtypeStruct((n_i, 1, D), F32)],
        compiler_params=_params(("parallel",)),
    )(dxo, x, y, dgf, silu, g_pre, wg, wu, wd, g_post, _after(dep))


def _wgrad(xm, ym, name, shard_cols=False, dep=None):
    T, M = xm.shape
    N = ym.shape[-1]
    assert M * N * 4 <= 16 * 1024 * 1024, (M, N)
    tk = _blk(T, 512)
    n_k = T // tk
    fs = N // N_DEV

    def body(x_ref, y_ref, _, o_ref, acc_scr):
        k = pl.program_id(0)

        @pl.when(k == 0)
        def _():
            acc_scr[...] = jnp.zeros_like(acc_scr)

        acc_scr[...] += _mm_tn(x_ref[...], y_ref[...])

        @pl.when(k == n_k - 1)
        def _():
            if shard_cols:
                for s in range(N_DEV):
                    o_ref[s] = acc_scr[:, s * fs:(s + 1) * fs].astype(BF16)
            else:
                o_ref[...] = acc_scr[...].astype(BF16)

    if shard_cols:
        out_spec = pl.BlockSpec((N_DEV, M, fs), lambda k: (0, 0, 0), pipeline_mode=pl.Buffered(1))
        out_shape = jax.ShapeDtypeStruct((N_DEV, M, fs), BF16)
    else:
        out_spec = pl.BlockSpec((M, N), lambda k: (0, 0), pipeline_mode=pl.Buffered(1))
        out_shape = jax.ShapeDtypeStruct((M, N), BF16)
    return pl.pallas_call(
        body, name=name, grid=(n_k,),
        in_specs=[pl.BlockSpec((tk, M), lambda k: (k, 0)), pl.BlockSpec((tk, N), lambda k: (k, 0)), ANY],
        out_specs=out_spec, out_shape=out_shape,
        scratch_shapes=[pltpu.VMEM((M, N), F32)],
        compiler_params=_params(("arbitrary",)),
    )(xm, ym, _after(dep))


def _wgrad_multi(xm, segs, name, dep=None):
    T, M = xm.shape
    N = segs[0][0].shape[-1]
    n_seg = len(segs)
    assert M * N * n_seg * 4 <= 16 * 1024 * 1024, (M, N, n_seg)
    tk = _blk(T, 512)
    n_k = T // tk

    def body(*refs):
        x_ref, y_refs = refs[0], refs[1:1 + n_seg]
        o_ref, acc_scr = refs[2 + n_seg], refs[3 + n_seg]
        k = pl.program_id(0)

        @pl.when(k == 0)
        def _():
            acc_scr[...] = jnp.zeros_like(acc_scr)

        x = x_ref[...]
        for s in range(n_seg):
            acc_scr[:, s * N:(s + 1) * N] += _mm_tn(x, y_refs[s][...])

        @pl.when(k == n_k - 1)
        def _():
            o_ref[...] = acc_scr[...].astype(BF16)

    y_specs = [pl.BlockSpec((tk, N), lambda k: (k, 0)) if idx is None
               else pl.BlockSpec((None, tk, N), lambda k, idx=idx: (idx, k, 0)) for _, idx in segs]
    return pl.pallas_call(
        body, name=name, grid=(n_k,),
        in_specs=[pl.BlockSpec((tk, M), lambda k: (k, 0))] + y_specs + [ANY],
        out_specs=pl.BlockSpec((M, n_seg * N), lambda k: (0, 0), pipeline_mode=pl.Buffered(1)),
        out_shape=jax.ShapeDtypeStruct((M, n_seg * N), BF16),
        scratch_shapes=[pltpu.VMEM((M, n_seg * N), F32)],
        compiler_params=_params(("arbitrary",)),
    )(xm, *[arr for arr, _ in segs], _after(dep))


def _mix_in_fwd(x1, g, w7, wf, name):
    T, D = x1.shape
    n_seg, _, W = w7.shape
    tm = _blk(T, 512)

    def body(x_ref, g_ref, w_ref, wf_ref, z_ref, f_ref, hb_ref):
        xv = x_ref[...]
        h = (xv * _rms_scale(xv) * g_ref[...]).astype(BF16)
        hb_ref[...] = h
        f_ref[...] = _mm(h, wf_ref[...])
        for s in range(n_seg):
            z_ref[s] = _mm(h, w_ref[s]).astype(BF16)

    return pl.pallas_call(
        body, name=name, grid=(T // tm,),
        in_specs=[pl.BlockSpec((tm, D), lambda i: (i, 0)),
                  pl.BlockSpec((1, D), lambda i: (0, 0)),
                  pl.BlockSpec((n_seg, D, W), lambda i: (0, 0, 0), pipeline_mode=pl.Buffered(1)),
                  pl.BlockSpec((D, LANES), lambda i: (0, 0))],
        out_specs=[pl.BlockSpec((n_seg, tm, W), lambda i: (0, i, 0)),
                   pl.BlockSpec((tm, LANES), lambda i: (i, 0)),
                   pl.BlockSpec((tm, D), lambda i: (i, 0))],
        out_shape=[jax.ShapeDtypeStruct((n_seg, T, W), BF16), jax.ShapeDtypeStruct((T, LANES), F32),
                   jax.ShapeDtypeStruct((T, D), BF16)],
        compiler_params=_params(("parallel",)),
    )(x1, g, w7, wf)


def _mix_in_bwd(dx2, x1, g, segs, dfb, w7, wf, name, dep=None):
    T, D = x1.shape
    n_seg, _, W = w7.shape
    tm = _blk(T, 512)
    n_i = T // tm

    def body(*refs):
        dx2_ref, x_ref, g_ref = refs[:3]
        seg_refs = refs[3:3 + n_seg]
        df_ref, w_ref, wf_ref, _, dx1_ref, dg_ref = refs[3 + n_seg:]
        dh = _mm_nt(df_ref[...], wf_ref[...])
        for q in range(n_seg):
            dh = dh + _mm_nt(seg_refs[q][...], w_ref[q])
        xv = x_ref[...]
        r = _rms_scale(xv)
        xhat = xv * r
        dg_ref[...] = jnp.sum(dh * xhat, axis=0, keepdims=True)
        dx1_ref[...] = _rms_bwd(dh, xhat, r, g_ref[...]) + dx2_ref[...]

    row = pl.BlockSpec((tm, D), lambda i: (i, 0))
    seg_specs = []
    seg_args = []
    for arr, idx in segs:
        if idx is None:
            seg_specs.append(pl.BlockSpec((tm, W), lambda i: (i, 0)))
        else:
            seg_specs.append(pl.BlockSpec((None, tm, W), lambda i, idx=idx: (idx, i, 0)))
        seg_args.append(arr)
    return pl.pallas_call(
        body, name=name, grid=(n_i,),
        in_specs=[row, row, pl.BlockSpec((1, D), lambda i: (0, 0))] + seg_specs + [
            pl.BlockSpec((tm, LANES), lambda i: (i, 0)),
            pl.BlockSpec((n_seg, D, W), lambda i: (0, 0, 0), pipeline_mode=pl.Buffered(1)),
            pl.BlockSpec((D, LANES), lambda i: (0, 0)), ANY],
        out_specs=[row, pl.BlockSpec((None, 1, D), lambda i: (i, 0, 0))],
        out_shape=[jax.ShapeDtypeStruct((T, D), F32), jax.ShapeDtypeStruct((n_i, 1, D), F32)],
        compiler_params=_params(("parallel",)),
    )(dx2, x1, g, *seg_args, dfb, w7, wf, _after(dep))


def _forget_cumsum(f, b_pad, name):
    T, L = f.shape
    tb = _blk(T, 256)

    def body(f_ref, b_ref, c_ref, carry):
        @pl.when(pl.program_id(0) == 0)
        def _():
            carry[...] = jnp.zeros_like(carry)

        lf = jax.nn.log_sigmoid(f_ref[...] + b_ref[...])
        rows = lax.broadcasted_iota(jnp.int32, (tb, tb), 0)
        cols = lax.broadcasted_iota(jnp.int32, (tb, tb), 1)
        tri = (cols <= rows).astype(F32)
        c = jnp.dot(tri, lf, preferred_element_type=F32, precision=lax.Precision.HIGHEST) + carry[...]
        carry[...] = c[tb - 1:tb, :]
        for h in range(N_HEADS):
            c_ref[h] = jnp.broadcast_to(c[:, h:h + 1] * LOG2E, (tb, L))

    return pl.pallas_call(
        body, name=name, grid=(T // tb,),
        in_specs=[pl.BlockSpec((tb, L), lambda i: (i, 0)), pl.BlockSpec((1, L), lambda i: (0, 0))],
        out_specs=pl.BlockSpec((N_HEADS, tb, L), lambda i: (0, i, 0)),
        out_shape=jax.ShapeDtypeStruct((N_HEADS, T, L), F32),
        scratch_shapes=[pltpu.VMEM((1, L), F32)],
        compiler_params=_params(("arbitrary",)),
    )(f, b_pad)


def _forget_bwd(dc, f, b_pad, name):
    T, L = f.shape
    tb = _blk(T, 256)
    nb = T // tb

    def body(dc_ref, f_ref, b_ref, df_ref, db_ref, carry):
        @pl.when(pl.program_id(0) == 0)
        def _():
            carry[...] = jnp.zeros_like(carry)
            db_ref[...] = jnp.zeros_like(db_ref)

        rows = lax.broadcasted_iota(jnp.int32, (tb, tb), 0)
        cols = lax.broadcasted_iota(jnp.int32, (tb, tb), 1)
        tri = (cols >= rows).astype(F32)
        r = jnp.dot(tri, dc_ref[...], preferred_element_type=F32, precision=lax.Precision.HIGHEST) + carry[...]
        carry[...] = r[0:1, :]
        df = r * (1.0 - jax.nn.sigmoid(f_ref[...] + b_ref[...]))
        df_ref[...] = df.astype(BF16)
        db_ref[...] += jnp.sum(df, axis=0, keepdims=True)

    rev = pl.BlockSpec((tb, L), lambda i: (nb - 1 - i, 0))
    one = pl.BlockSpec((1, L), lambda i: (0, 0))
    return pl.pallas_call(
        body, name=name, grid=(nb,),
        in_specs=[rev, rev, one], out_specs=[rev, one],
        out_shape=[jax.ShapeDtypeStruct((T, L), BF16), jax.ShapeDtypeStruct((1, L), F32)],
        scratch_shapes=[pltpu.VMEM((1, L), F32)],
        compiler_params=_params(("arbitrary",)),
    )(dc, f, b_pad)


ATTN_TILE = 512
ATTN_CHAINS = 4


def _attn_geometry(T):
    ta = _blk(T, ATTN_TILE)
    nc = ATTN_CHAINS if (T // ta) % ATTN_CHAINS == 0 else 1
    return ta, nc, T // ta


def _causal_tile(ta, keys_on_rows=False):
    rows = lax.broadcasted_iota(jnp.int32, (ta, ta), 0)
    cols = lax.broadcasted_iota(jnp.int32, (ta, ta), 1)
    return rows <= cols if keys_on_rows else cols <= rows


def _chunk(ref, j, ta):
    return ref[pl.ds(pl.multiple_of(j * ta, ta), ta), :]


def _attn_fwd_keys_on_rows(z7, vt, c_rep, name):
    _, T, W = z7.shape
    H = W // HEAD_DIM
    ta, nc, n_chunks = _attn_geometry(T)
    scale = np.float32(1.0 / np.sqrt(HEAD_DIM))
    reps = ta // LANES

    def body(q_ref, k_ref, vt_ref, c_ref, o_ref, lse_ref):
        g = pl.program_id(1)

        def scores(ch, k):
            return _mm_nt(k, q_ref[ch * ta:(ch + 1) * ta, :])

        def update(state, raw, vt, cj, diagonal):
            m_prev, l_prev, acc_prev = state
            st = raw * (scale * LOG2E) - cj
            if diagonal:
                st = jnp.where(_causal_tile(ta, keys_on_rows=True), st, NEG_BIG)
            m_new = jnp.maximum(m_prev, jnp.max(st, axis=0, keepdims=True))
            alpha = jnp.exp2(m_prev - m_new)
            pt = jnp.exp2(st - m_new)
            l_new = alpha * l_prev + jnp.sum(pt, axis=0, keepdims=True)
            acc_new = alpha * acc_prev + _mm(vt, pt.astype(BF16))
            return m_new, l_new, acc_new

        def load(j):
            cj = _chunk(c_ref, j, ta)
            return _chunk(k_ref, j, ta), vt_ref[j], jnp.concatenate([cj] * reps, axis=1)

        def full_chunk(j, states):
            k, vt, cj = load(j)
            raws = [scores(ch, k) for ch in range(nc)]
            return tuple(update(states[ch], raws[ch], vt, cj, False) for ch in range(nc))

        first = (jnp.full((1, ta), NEG_BIG, F32), jnp.zeros((1, ta), F32), jnp.zeros((HEAD_DIM, ta), F32))
        states = list(lax.fori_loop(0, nc * g, full_chunk, (first,) * nc))
        for jj in range(nc):
            k, vt, cj = load(nc * g + jj)
            raws = {ch: scores(ch, k) for ch in range(jj, nc)}
            for ch in range(jj, nc):
                states[ch] = update(states[ch], raws[ch], vt, cj, ch == jj)
        for ch in range(nc):
            m, l, acc = states[ch]
            o_ref[ch * ta:(ch + 1) * ta, :] = (acc / l).T
            lse_ref[ch] = m + jnp.log2(l)

    tq = nc * ta
    return pl.pallas_call(
        body, name=name, grid=(H, n_chunks // nc),
        in_specs=[pl.BlockSpec((None, tq, HEAD_DIM), lambda h, g: (0, g, h)),
                  pl.BlockSpec((None, T, HEAD_DIM), lambda h, g: (1, 0, h)),
                  pl.BlockSpec((None, n_chunks, HEAD_DIM, ta), lambda h, g: (h, 0, 0, 0)),
                  pl.BlockSpec((None, T, LANES), lambda h, g: (h, 0, 0))],
        out_specs=[pl.BlockSpec((tq, HEAD_DIM), lambda h, g: (g, h)),
                   pl.BlockSpec((None, nc, 1, ta), lambda h, g: (h, g, 0, 0))],
        out_shape=[jax.ShapeDtypeStruct((T, W), F32), jax.ShapeDtypeStruct((H, n_chunks, 1, ta), F32)],
        compiler_params=_params(("parallel", "arbitrary")),
    )(z7, z7, vt, c_rep)


def _attn_bwd_fused(z7, kt, dob, c_rep, lse_chunks, d_chunks, name):
    _, T, W = z7.shape
    H = W // HEAD_DIM
    ta, nc, n_chunks = _attn_geometry(T)
    n_steps = n_chunks // nc
    scale = np.float32(1.0 / np.sqrt(HEAD_DIM))
    reps = ta // LANES

    def body(k_ref, v_ref, kt_ref, q_ref, do_ref, c_ref, lse_ref, d_ref,
             dk_ref, dv_ref, dck_ref, dq_ref, dcq_ref, dk_scr, dv_scr, dck_scr, dqt_scr, dcq_scr):
        g = pl.program_id(1)

        @pl.when(g == 0)
        def _():
            dqt_scr[...] = jnp.zeros_like(dqt_scr)
            dcq_scr[...] = jnp.zeros_like(dcq_scr)

        dk_scr[...] = jnp.zeros_like(dk_scr)
        dv_scr[...] = jnp.zeros_like(dv_scr)
        dck_scr[...] = jnp.zeros_like(dck_scr)

        def products(ch, q, do):
            rows = slice(ch * ta, (ch + 1) * ta)
            return _mm_nt(k_ref[rows, :], q), _mm_nt(v_ref[rows, :], do)

        def update(ch, i, q, do, prods, diagonal):
            rows = slice(ch * ta, (ch + 1) * ta)
            cj = c_ref[rows, :]
            st = prods[0] * (scale * LOG2E) - jnp.concatenate([cj] * reps, axis=1) - lse_ref[i]
            if diagonal:
                st = jnp.where(_causal_tile(ta, keys_on_rows=True), st, NEG_BIG)
            pt = jnp.exp2(st)
            dv_scr[ch] += _mm(pt.astype(BF16), do)
            dst = pt * (prods[1] - d_ref[i])
            dst_b = dst.astype(BF16)
            dk_scr[ch] += _mm(dst_b, q)
            dqt_scr[i] += _mm(kt_ref[ch], dst_b)
            dcq_scr[i] += jnp.sum(dst, axis=0, keepdims=True)
            lane_sum = dst[:, :LANES]
            for r in range(1, reps):
                lane_sum = lane_sum + dst[:, r * LANES:(r + 1) * LANES]
            dck_scr[ch] += lane_sum

        for ii in range(nc):
            i = nc * g + ii
            q = _chunk(q_ref, i, ta)
            do = _chunk(do_ref, i, ta)
            prods = [products(ch, q, do) for ch in range(0, ii + 1)]
            for ch in range(0, ii + 1):
                update(ch, i, q, do, prods[ch], ch == ii)

        def full_chunk(i, carry):
            q = _chunk(q_ref, i, ta)
            do = _chunk(do_ref, i, ta)
            prods = [products(ch, q, do) for ch in range(nc)]
            for ch in range(nc):
                update(ch, i, q, do, prods[ch], False)
            return carry

        lax.fori_loop(nc * (g + 1), n_chunks, full_chunk, 0)
        for ch in range(nc):
            rows = slice(ch * ta, (ch + 1) * ta)
            dk_ref[rows, :] = (dk_scr[ch] * scale).astype(BF16)
            dv_ref[rows, :] = dv_scr[ch].astype(BF16)
            ones = jnp.ones((8, LANES), F32)
            sums = lax.dot_general(ones, dck_scr[ch], (((1,), (1,)), ((), ())), preferred_element_type=F32,
                                   precision=lax.Precision.HIGHEST)
            dck_ref[ch] = -sums[0:1, :]

        @pl.when(g == n_steps - 1)
        def _():
            for i in range(n_chunks):
                dq_ref[i * ta:(i + 1) * ta, :] = (dqt_scr[i] * scale).T.astype(BF16)
            dcq_ref[...] = dcq_scr[...]

    tk = nc * ta
    chunks = pl.BlockSpec((None, n_chunks, 1, ta), lambda h, g: (h, 0, 0, 0))
    tile = pl.BlockSpec((tk, HEAD_DIM), lambda h, g: (g, h))
    return pl.pallas_call(
        body, name=name, grid=(H, n_steps),
        in_specs=[pl.BlockSpec((None, tk, HEAD_DIM), lambda h, g: (1, g, h)),
                  pl.BlockSpec((None, tk, HEAD_DIM), lambda h, g: (2, g, h)),
                  pl.BlockSpec((None, nc, HEAD_DIM, ta), lambda h, g: (h, g, 0, 0)),
                  pl.BlockSpec((None, T, HEAD_DIM), lambda h, g: (0, 0, h)),
                  pl.BlockSpec((T, HEAD_DIM), lambda h, g: (0, h)),
                  pl.BlockSpec((None, tk, LANES), lambda h, g: (h, g, 0)),
                  chunks, chunks],
        out_specs=[tile, tile, pl.BlockSpec((None, nc, 1, ta), lambda h, g: (h, g, 0, 0)),
                   pl.BlockSpec((T, HEAD_DIM), lambda h, g: (0, h)), chunks],
        out_shape=[jax.ShapeDtypeStruct((T, W), BF16), jax.ShapeDtypeStruct((T, W), BF16),
                   jax.ShapeDtypeStruct((H, n_chunks, 1, ta), F32), jax.ShapeDtypeStruct((T, W), BF16),
                   jax.ShapeDtypeStruct((H, n_chunks, 1, ta), F32)],
        scratch_shapes=[pltpu.VMEM((nc, ta, HEAD_DIM), F32), pltpu.VMEM((nc, ta, HEAD_DIM), F32),
                        pltpu.VMEM((nc, ta, LANES), F32), pltpu.VMEM((n_chunks, HEAD_DIM, ta), F32),
                        pltpu.VMEM((n_chunks, 1, ta), F32)],
        compiler_params=_params(("parallel", "arbitrary")),
    )(z7, z7, kt, z7, dob, c_rep, lse_chunks, d_chunks)


def _chunk_causal_mask():
    rows = lax.broadcasted_iota(jnp.int32, (SGU_LEN, SGU_LEN), 0)
    cols = lax.broadcasted_iota(jnp.int32, (SGU_LEN, SGU_LEN), 1)
    return (cols // CHUNK) <= (rows // CHUNK)


def _sgu_norm_mix(vs, lng_ref, lnb_ref, ws_ref, bs_ref, vn_scr, mixed_scr, vhat_scr=None):
    tm = vs.shape[0]
    mask = _chunk_causal_mask()
    rstds = []
    for g in range(N_GROUPS):
        lanes = slice(g * GROUP_DIM, (g + 1) * GROUP_DIM)
        blk = vs[:, lanes]
        cen = blk - jnp.mean(blk, axis=-1, keepdims=True)
        rstd = lax.rsqrt(jnp.mean(cen * cen, axis=-1, keepdims=True) + LN_EPS)
        vhat = cen * rstd
        rstds.append(rstd)
        if vhat_scr is not None:
            vhat_scr[:, lanes] = vhat
        vn_scr[:, lanes] = (vhat * lng_ref[:, lanes] + lnb_ref[:, lanes]).astype(BF16)
        wm = jnp.where(mask, ws_ref[g], 0.0).astype(BF16)
        for w in range(tm // SGU_LEN):
            rows = slice(w * SGU_LEN, (w + 1) * SGU_LEN)
            mixed_scr[rows, lanes] = _mm(wm, vn_scr[rows, lanes]) + bs_ref[g]
    return rstds


def _mix_out_fwd(z7, o_a, x1, lng, lnb, ws, bs, w_out, g_post, name):
    _, T, W = z7.shape
    D = x1.shape[1]
    tm = _blk(T, 256)

    def body(u_ref, sv_ref, ga_ref, gb_ref, oa_ref, x1_ref, lng_ref, lnb_ref, ws_ref, bs_ref, wo_ref, gp_ref,
             x2_ref, p_ref, mb_ref, vn_scr, mixed_scr):
        _sgu_norm_mix(_gelu(sv_ref[...].astype(F32)), lng_ref, lnb_ref, ws_ref, bs_ref, vn_scr, mixed_scr)
        o_b = _gelu(u_ref[...].astype(F32)) * mixed_scr[...]
        merged = (jax.nn.sigmoid(ga_ref[...].astype(F32)) * oa_ref[...]
                  + jax.nn.sigmoid(gb_ref[...].astype(F32)) * o_b).astype(BF16)
        mb_ref[...] = merged
        p = _mm(merged, wo_ref[...])
        p_ref[...] = p
        x2_ref[...] = x1_ref[...] + p * _rms_scale(p) * gp_ref[...]

    def seg(idx):
        return pl.BlockSpec((None, tm, W), lambda i, idx=idx: (idx, i, 0))

    row = pl.BlockSpec((tm, D), lambda i: (i, 0))
    vec = pl.BlockSpec((1, D), lambda i: (0, 0))
    return pl.pallas_call(
        body, name=name, grid=(T // tm,),
        in_specs=[seg(3), seg(4), seg(5), seg(6), row, row, vec, vec,
                  pl.BlockSpec((N_GROUPS, SGU_LEN, SGU_LEN), lambda i: (0, 0, 0)),
                  pl.BlockSpec((N_GROUPS, SGU_LEN, 1), lambda i: (0, 0, 0)),
                  pl.BlockSpec((D, D), lambda i: (0, 0)), vec],
        out_specs=[row, row, row],
        out_shape=[jax.ShapeDtypeStruct((T, D), F32), jax.ShapeDtypeStruct((T, D), F32),
                   jax.ShapeDtypeStruct((T, D), BF16)],
        scratch_shapes=[pltpu.VMEM((tm, W), BF16), pltpu.VMEM((tm, W), F32)],
        compiler_params=_params(("parallel",)),
    )(z7, z7, z7, z7, o_a, x1, lng, lnb, ws, bs, w_out, g_post)


def _mix_out_bwd(dx2, p, z7, o_a, lng, lnb, ws, bs, w_out, g_post, name, dep=None):
    _, T, W = z7.shape
    D = dx2.shape[1]
    tm = _blk(T, 256)
    n_w = tm // SGU_LEN

    def body(dx2_ref, p_ref, u_ref, sv_ref, ga_ref, gb_ref, oa_ref, lng_ref, lnb_ref, ws_ref, bs_ref, wo_ref, gp_ref, _,
             dpb_ref, dob_ref, dvec_ref, dz_ref, dgp_ref, dlng_ref, dlnb_ref, dws_ref, dbs_ref,
             vn_scr, mixed_scr, vhat_scr, dmix_scr, dvn_scr):
        @pl.when(pl.program_id(0) == 0)
        def _():
            dgp_ref[...] = jnp.zeros_like(dgp_ref)
            dlng_ref[...] = jnp.zeros_like(dlng_ref)
            dlnb_ref[...] = jnp.zeros_like(dlnb_ref)
            dws_ref[...] = jnp.zeros_like(dws_ref)
            dbs_ref[...] = jnp.zeros_like(dbs_ref)

        pv = p_ref[...]
        s = _rms_scale(pv)
        n = pv * s
        dn = dx2_ref[...]
        dgp_ref[...] += jnp.sum(dn * n, axis=0, keepdims=True)
        dpb = _rms_bwd(dn, n, s, gp_ref[...]).astype(BF16)
        dpb_ref[...] = dpb
        dmerged = _mm_nt(dpb, wo_ref[...])

        vs, vs_grad = _gelu_and_grad(sv_ref[...].astype(F32))
        rstds = _sgu_norm_mix(vs, lng_ref, lnb_ref, ws_ref, bs_ref, vn_scr, mixed_scr, vhat_scr)
        u, u_grad = _gelu_and_grad(u_ref[...].astype(F32))
        mixed = mixed_scr[...]
        sa = jax.nn.sigmoid(ga_ref[...].astype(F32))
        sb = jax.nn.sigmoid(gb_ref[...].astype(F32))
        oa = oa_ref[...]
        do_a = (dmerged * sa).astype(BF16)
        dob_ref[...] = do_a
        prod = do_a.astype(F32) * oa
        for h in range(N_HEADS):
            sums = lax.dot_general(jnp.ones((8, LANES), F32), prod[:, h * HEAD_DIM:(h + 1) * HEAD_DIM],
                                   (((1,), (1,)), ((), ())), preferred_element_type=F32,
                                   precision=lax.Precision.HIGHEST)
            dvec_ref[h, 0] = sums[0:1, :]
        dz_ref[2] = (dmerged * oa * (sa * (1.0 - sa))).astype(BF16)
        dz_ref[3] = (dmerged * (u * mixed) * (sb * (1.0 - sb))).astype(BF16)
        do_b = dmerged * sb
        dz_ref[0] = (do_b * mixed * u_grad).astype(BF16)
        dmix_scr[...] = do_b * u

        mask = _chunk_causal_mask()
        for g in range(N_GROUPS):
            lanes = slice(g * GROUP_DIM, (g + 1) * GROUP_DIM)
            wm = jnp.where(mask, ws_ref[g], 0.0).astype(BF16)
            dws = jnp.zeros((SGU_LEN, SGU_LEN), F32)
            dbs = jnp.zeros((SGU_LEN, 1), F32)
            for w in range(n_w):
                rows = slice(w * SGU_LEN, (w + 1) * SGU_LEN)
                dmix = dmix_scr[rows, lanes]
                dmix_b = dmix.astype(BF16)
                dvn_scr[rows, lanes] = _mm_tn(wm, dmix_b)
                dws = dws + _mm_nt(dmix_b, vn_scr[rows, lanes])
                dbs = dbs + jnp.sum(dmix, axis=-1, keepdims=True)
            dws_ref[g] += jnp.where(mask, dws, 0.0)
            dbs_ref[g] += dbs
            dvn = dvn_scr[:, lanes]
            vhat = vhat_scr[:, lanes]
            dlng_ref[:, lanes] += jnp.sum(dvn * vhat, axis=0, keepdims=True)
            dlnb_ref[:, lanes] += jnp.sum(dvn, axis=0, keepdims=True)
            dvh = dvn * lng_ref[:, lanes]
            dvs = rstds[g] * (dvh - jnp.mean(dvh, axis=-1, keepdims=True)
                              - vhat * jnp.mean(dvh * vhat, axis=-1, keepdims=True))
            dvn_scr[:, lanes] = dvs
        dz_ref[1] = (dvn_scr[...] * vs_grad).astype(BF16)

    def seg(idx):
        return pl.BlockSpec((None, tm, W), lambda i, idx=idx: (idx, i, 0))

    row = pl.BlockSpec((tm, D), lambda i: (i, 0))
    vec = pl.BlockSpec((1, D), lambda i: (0, 0))
    ws_spec = pl.BlockSpec((N_GROUPS, SGU_LEN, SGU_LEN), lambda i: (0, 0, 0))
    bs_spec = pl.BlockSpec((N_GROUPS, SGU_LEN, 1), lambda i: (0, 0, 0))
    return pl.pallas_call(
        body, name=name, grid=(T // tm,),
        in_specs=[row, row, seg(3), seg(4), seg(5), seg(6), row, vec, vec, ws_spec, bs_spec,
                  pl.BlockSpec((D, D), lambda i: (0, 0)), vec, ANY],
        out_specs=[row, row, pl.BlockSpec((N_HEADS, 1, 1, tm), lambda i: (0, i, 0, 0)),
                   pl.BlockSpec((4, tm, W), lambda i: (0, i, 0)), vec, vec, vec, ws_spec, bs_spec],
        out_shape=[jax.ShapeDtypeStruct((T, D), BF16), jax.ShapeDtypeStruct((T, W), BF16),
                   jax.ShapeDtypeStruct((N_HEADS, T // tm, 1, tm), F32), jax.ShapeDtypeStruct((4, T, W), BF16),
                   jax.ShapeDtypeStruct((1, D), F32), jax.ShapeDtypeStruct((1, D), F32),
                   jax.ShapeDtypeStruct((1, D), F32),
                   jax.ShapeDtypeStruct((N_GROUPS, SGU_LEN, SGU_LEN), F32),
                   jax.ShapeDtypeStruct((N_GROUPS, SGU_LEN, 1), F32)],
        scratch_shapes=[pltpu.VMEM((tm, W), BF16), pltpu.VMEM((tm, W), F32), pltpu.VMEM((tm, W), F32),
                        pltpu.VMEM((tm, W), F32), pltpu.VMEM((tm, W), F32)],
        compiler_params=_params(("arbitrary",)),
    )(dx2, p, z7, z7, z7, z7, o_a, lng, lnb, ws, bs, w_out, g_post, _after(dep))


def _loss_head(y, target, name):
    T, D = y.shape
    tm = _blk(T, 1024)
    n_i = T // tm

    def body(y_ref, t_ref, dy_ref, loss_ref, acc_scr):
        i = pl.program_id(0)

        @pl.when(i == 0)
        def _():
            acc_scr[...] = jnp.zeros_like(acc_scr)

        e = y_ref[...] - t_ref[...]
        dy_ref[...] = e * np.float32(1.0 / D)
        acc_scr[...] += jnp.sum(e * e, axis=0, keepdims=True)

        @pl.when(i == n_i - 1)
        def _():
            total = jnp.sum(acc_scr[...], axis=-1, keepdims=True) * np.float32(0.5 / D)
            loss_ref[...] = jnp.broadcast_to(total, loss_ref.shape)

    row = pl.BlockSpec((tm, D), lambda i: (i, 0))
    return pl.pallas_call(
        body, name=name, grid=(n_i,),
        in_specs=[row, row],
        out_specs=[row, pl.BlockSpec((1, LANES), lambda i: (0, 0))],
        out_shape=[jax.ShapeDtypeStruct((T, D), F32), jax.ShapeDtypeStruct((1, LANES), F32)],
        scratch_shapes=[pltpu.VMEM((1, D), F32)],
        compiler_params=_params(("arbitrary",)),
    )(y, target)


def _adamw_math(w, g, m, v):
    m_new = ADAM_B1 * m + (1.0 - ADAM_B1) * g
    v_new = ADAM_B2 * v + (1.0 - ADAM_B2) * (g * g)
    m_hat = m_new / np.float32(1.0 - ADAM_B1 ** ADAM_STEP)
    v_hat = v_new / np.float32(1.0 - ADAM_B2 ** ADAM_STEP)
    delta = -ADAM_LR * (m_hat / (jnp.sqrt(v_hat) + ADAM_EPS) + ADAM_WD * w)
    return delta, m_new, v_new


def _sum_adamw(parts, w, m, v, name, dep=None):
    n, R, C = parts.shape
    tr = _blk(R, 512)

    def body(p_ref, w_ref, m_ref, v_ref, _, g_ref, d_ref, mo_ref, vo_ref):
        g = p_ref[0].astype(F32)
        for s in range(1, n):
            g = g + p_ref[s].astype(F32)
        delta, m_new, v_new = _adamw_math(w_ref[...], g, m_ref[...], v_ref[...])
        g_ref[...] = g
        d_ref[...] = delta
        mo_ref[...] = m_new
        vo_ref[...] = v_new

    row = pl.BlockSpec((tr, C), lambda i: (i, 0))
    shp = jax.ShapeDtypeStruct((R, C), F32)
    return pl.pallas_call(
        body, name=name, grid=(R // tr,),
        in_specs=[pl.BlockSpec((n, tr, C), lambda i: (0, i, 0)), row, row, row, ANY],
        out_specs=[row, row, row, row], out_shape=[shp, shp, shp, shp],
        compiler_params=_params(("parallel",)),
    )(parts, w, m, v, _after(dep))


def _position():
    return lax.axis_index("x"), lax.axis_index("y"), lax.axis_index("c")


def _slot(px, py, pc):
    return 4 * px + 2 * py + pc


def _all_gather(shards, name):
    n = len(shards)

    def body(*refs):
        ins, outs = refs[:n], refs[n:2 * n]
        send_sems, recv_sems, local_sems = refs[2 * n:]
        x, y, c = _position()
        me, sibling = (x, y, c), (x, y, 1 - c)
        chips = [(1 - x, y), (x, 1 - y), (1 - x, 1 - y)]

        def copy(a, k, block, to, src=None):
            dst = outs[a].at[_slot(*block)]
            return pltpu.make_async_remote_copy(
                src_ref=dst if src is None else src, dst_ref=dst,
                send_sem=send_sems.at[a, k], recv_sem=recv_sems.at[a, k],
                device_id=to, device_id_type=MESH)

        mine = [pltpu.make_async_copy(ins[a], outs[a].at[_slot(*me)], local_sems.at[a]) for a in range(n)]
        for cp in mine:
            cp.start()
        first = []
        for a in range(n):
            first.append(copy(a, 0, me, sibling, src=ins[a]))
            first += [copy(a, 1 + j, me, (*chip, c), src=ins[a]) for j, chip in enumerate(chips)]
        for cp in first:
            cp.start()
        passed = []
        for j, chip in enumerate(chips):
            for a in range(n):
                copy(a, 1 + j, (*chip, c), me).wait_recv()
                fwd = copy(a, 4 + j, (*chip, c), sibling)
                fwd.start()
                passed.append(fwd)
        for a in range(n):
            copy(a, 0, sibling, me).wait_recv()
            for j, chip in enumerate(chips):
                copy(a, 4 + j, (*chip, 1 - c), me).wait_recv()
        for cp in first + passed:
            cp.wait_send()
        for cp in mine:
            cp.wait()

    return pl.pallas_call(
        body, name=name,
        in_specs=[ANY] * n, out_specs=[ANY] * n,
        out_shape=[jax.ShapeDtypeStruct((N_DEV,) + s.shape, s.dtype) for s in shards],
        scratch_shapes=[pltpu.SemaphoreType.DMA((n, 7)), pltpu.SemaphoreType.DMA((n, 7)),
                        pltpu.SemaphoreType.DMA((n,))],
    )(*shards)


def _peer(x, y, c, k):
    return (1 - x if k & 4 else x, 1 - y if k & 2 else y, 1 - c if k & 1 else c)


def _remote_copies(src_refs, land_refs, send_sems, recv_sems, gather, outgoing):
    x, y, c = _position()
    me = _slot(x, y, c)
    copies = []
    for k in range(1, N_DEV):
        peer = _peer(x, y, c, k)
        for a in range(len(src_refs)):
            src = src_refs[a] if gather else src_refs[a].at[_slot(*peer)]
            dst = land_refs[a].at[me if outgoing else _slot(*peer)]
            sem = a * (N_DEV - 1) + k - 1
            copies.append(pltpu.make_async_remote_copy(
                src_ref=src, dst_ref=dst, send_sem=send_sems.at[sem], recv_sem=recv_sems.at[sem],
                device_id=peer, device_id_type=MESH))
    return copies


def _sequencer_exchange(srcs, name, gather, collective_id):
    n = len(srcs)
    hbm = pltpu.MemorySpace.HBM
    src_refs = [jax.new_ref(s, memory_space=hbm) for s in srcs]
    land_refs = [jax.empty_ref(jax.ShapeDtypeStruct(((N_DEV,) + s.shape) if gather else s.shape, s.dtype),
                               memory_space=hbm) for s in srcs]
    n_sems = n * (N_DEV - 1)
    block_bytes = sum(s.size * s.dtype.itemsize // (1 if gather else N_DEV) for s in srcs)
    cost = pl.CostEstimate(flops=0, transcendentals=0, bytes_accessed=2 * N_DEV * block_bytes,
                           remote_bytes_transferred=(N_DEV - 1) * block_bytes)

    @pl.kernel(mesh=plsc.ScalarSubcoreMesh(axis_name="sequencer", num_cores=1), name=name,
               scratch_types=(pltpu.SemaphoreType.DMA((n_sems,)), pltpu.SemaphoreType.DMA((n_sems,)),
                              pltpu.SemaphoreType.DMA((n,))),
               cost_estimate=cost,
               compiler_params=pltpu.CompilerParams(collective_id=collective_id))
    def launch(send_sems, recv_sems, local_sems):
        x, y, c = _position()
        me = _slot(x, y, c)
        barrier = pltpu.get_barrier_semaphore()
        for k in range(1, N_DEV):
            pl.semaphore_signal(barrier, inc=1, device_id=_peer(x, y, c, k), device_id_type=MESH)
        pl.semaphore_wait(barrier, N_DEV - 1)
        mine = [pltpu.make_async_copy(src_refs[a] if gather else src_refs[a].at[me], land_refs[a].at[me],
                                      local_sems.at[a]) for a in range(n)]
        for cp in mine:
            cp.start()
        sends = _remote_copies(src_refs, land_refs, send_sems, recv_sems, gather, outgoing=True)
        for cp in sends:
            cp.start()
        for cp in _remote_copies(src_refs, land_refs, send_sems, recv_sems, gather, outgoing=False):
            cp.wait_recv()
        for cp in sends:
            cp.wait_send()
        for cp in mine:
            cp.wait()

    launch()
    return [r[...] for r in land_refs]


SMALL_VECS = ("ffn1_pre_g", "ffn1_post_g", "mix_pre_g", "sgu_ln_g", "sgu_ln_b", "mix_post_g", "ffn2_pre_g",
              "ffn2_post_g")
ROW_BS = len(SMALL_VECS)
ROW_BF = ROW_BS + 1
ROW_LOSS = ROW_BF + 1
ROW_WS = 16
BLOB_ROWS = ROW_WS + SGU_LEN


def _pack_small(vals, D, loss_row=None):
    rows = [vals[n].reshape(1, D) for n in SMALL_VECS]
    rows.append(vals["sgu_b_s"].reshape(1, D))
    rows.append(jnp.pad(vals["b_forget"].reshape(1, N_HEADS), ((0, 0), (0, D - N_HEADS))))
    rows.append(jnp.zeros((1, D), F32) if loss_row is None else loss_row)
    rows.append(jnp.zeros((ROW_WS - ROW_LOSS - 1, D), F32))
    rows.append(vals["sgu_w_s"].reshape(SGU_LEN, D))
    return jnp.concatenate(rows, axis=0)


def _unpack_small(blob, D):
    out = {n: blob[r:r + 1] for r, n in enumerate(SMALL_VECS)}
    out["sgu_b_s"] = blob[ROW_BS].reshape(1, N_GROUPS, SGU_LEN)
    out["b_forget"] = blob[ROW_BF, :N_HEADS].reshape(1, N_HEADS)
    out["sgu_w_s"] = blob[ROW_WS:].reshape(1, N_GROUPS, SGU_LEN, SGU_LEN)
    return out


WEIGHT_NAMES = ("ffn1_pre_g", "ffn1_w_gate", "ffn1_w_up", "ffn1_w_down", "ffn1_post_g", "mix_pre_g", "w_in",
                "b_forget", "sgu_ln_g", "sgu_ln_b", "sgu_w_s", "sgu_b_s", "w_out", "mix_post_g", "ffn2_pre_g",
                "ffn2_w_gate", "ffn2_w_up", "ffn2_w_down", "ffn2_post_g")
BIG_NAMES = ("ffn1_w_gate", "ffn1_w_up", "ffn1_w_down", "w_in", "w_out", "ffn2_w_gate", "ffn2_w_up", "ffn2_w_down")
WEIGHT_GROUPS = {"ffn1": ("ffn1_w_gate", "ffn1_w_up", "ffn1_w_down"), "mix": ("w_in", "w_out"),
                 "ffn2": ("ffn2_w_gate", "ffn2_w_up", "ffn2_w_down")}
GRAD_GROUPS = (("ffn2_w_gate", "ffn2_w_up", "ffn2_w_down"), ("w_in", "w_out"), ("ffn1_w_down",), ("ffn1_w_gate",),
               ("ffn1_w_up",))


def _local_step(x, target, small, fetch, emit, consume):
    T, D = x.shape
    W = N_HEADS * HEAD_DIM
    vec = lambda n: small[n].reshape(1, D)
    big = dict(fetch("ffn1", x))

    x1, y1, dgf1, silu1, act1 = _ffn_fwd(x, vec("ffn1_pre_g"), big["ffn1_w_gate"], big["ffn1_w_up"], big["ffn1_w_down"],
                                  vec("ffn1_post_g"), "ffn1_fwd")

    big.update(fetch("mix", x1))
    w_in_all = big["w_in"]
    in_width = N_DEV * w_in_all.shape[2]
    w_in = w_in_all.transpose(1, 0, 2).reshape(D, in_width)
    col_f = 3 * W
    col_u = col_f + N_HEADS
    seg_starts = (0, W, 2 * W, col_u, col_u + W, col_u + 2 * W, col_u + 3 * W)
    w7 = jnp.stack([w_in[:, s:s + W] for s in seg_starts])
    wf = jnp.pad(w_in[:, col_f:col_u], ((0, 0), (0, LANES - N_HEADS)))
    w_out = big["w_out"].reshape(D, D)
    b_pad = jnp.pad(small["b_forget"].reshape(1, N_HEADS), ((0, 0), (0, LANES - N_HEADS)))
    lng, lnb = vec("sgu_ln_g"), vec("sgu_ln_b")
    ws = small["sgu_w_s"].reshape(N_GROUPS, SGU_LEN, SGU_LEN)
    bs = small["sgu_b_s"].reshape(N_GROUPS, SGU_LEN, 1)

    z7, f_logit, h2b = _mix_in_fwd(x1, vec("mix_pre_g"), w7, wf, "mix_in_fwd")
    c_rep = _forget_cumsum(f_logit, b_pad, "forget_cumsum")
    ta, _, n_chunks = _attn_geometry(T)
    vt = z7[2].reshape(n_chunks, ta, N_HEADS, HEAD_DIM).transpose(2, 0, 3, 1)
    o_a, lse_chunks = _attn_fwd_keys_on_rows(z7, vt, c_rep, "attn_fwd")
    x2, p, merged_b = _mix_out_fwd(z7, o_a, x1, lng, lnb, ws, bs, w_out, vec("mix_post_g"), "mix_out_fwd")
    big.update(fetch("ffn2", x2))
    x3, y2, dgf2, silu2, act2 = _ffn_fwd(x2, vec("ffn2_pre_g"), big["ffn2_w_gate"], big["ffn2_w_up"], big["ffn2_w_down"],
                                  vec("ffn2_post_g"), "ffn2_fwd")
    dy, loss_lanes = _loss_head(x3, target, "loss_head")

    grads_small = {}

    dx2, h3b, dy2b, dgate2, dup2, dgpre, dgpost = _ffn_bwd(
        dy, x2, y2, dgf2, silu2, vec("ffn2_pre_g"), big["ffn2_w_gate"], big["ffn2_w_up"], big["ffn2_w_down"],
        vec("ffn2_post_g"), "ffn2_bwd")
    grads_small["ffn2_pre_g"] = jnp.sum(dgpre, axis=0)
    grads_small["ffn2_post_g"] = jnp.sum(dgpost, axis=0)
    dep = emit("ffn2_w_gate", _wgrad(h3b, dgate2, "ffn2_wgrad_gate", shard_cols=True))
    dep = emit("ffn2_w_up", _wgrad(h3b, dup2, "ffn2_wgrad_up", shard_cols=True, dep=dep))
    dep = emit("ffn2_w_down", _wgrad(act2, dy2b, "ffn2_wgrad_down", dep=dep).reshape(big["ffn2_w_down"].shape))

    dpb, dob, dvec, dz4, dgp, dlng, dlnb, dws, dbs = _mix_out_bwd(
        dx2, p, z7, o_a, lng, lnb, ws, bs, w_out, vec("mix_post_g"), "mix_out_bwd", dep=dep)
    grads_small["mix_post_g"] = dgp
    grads_small["sgu_ln_g"] = dlng
    grads_small["sgu_ln_b"] = dlnb
    grads_small["sgu_w_s"] = dws
    grads_small["sgu_b_s"] = dbs
    d_chunks = dvec.reshape(N_HEADS, n_chunks, 1, ta)
    kt = z7[1].reshape(n_chunks, ta, N_HEADS, HEAD_DIM).transpose(2, 0, 3, 1)
    dk, dv, dc, dq, dc_q = _attn_bwd_fused(z7, kt, dob, c_rep, lse_chunks, d_chunks, "attn_bwd")
    dc_pad = jnp.pad((dc + dc_q).reshape(N_HEADS, T).T, ((0, 0), (0, LANES - N_HEADS)))
    dfb, dbf = _forget_bwd(dc_pad, f_logit, b_pad, "forget_bwd")
    grads_small["b_forget"] = dbf[:, :N_HEADS]
    segs = [(dq, None), (dk, None), (dv, None), (dz4, 0), (dz4, 1), (dz4, 2), (dz4, 3)]
    dep = consume(("ffn2_w_gate", "ffn2_w_up", "ffn2_w_down"))
    dx1, dgm = _mix_in_bwd(dx2, x1, vec("mix_pre_g"), segs, dfb, w7, wf, "mix_in_bwd", dep=dep)
    grads_small["mix_pre_g"] = jnp.sum(dgm, axis=0)
    dw_qkv = _wgrad_multi(h2b, segs[:3], "w_in_wgrad_qkv", dep=dx1)
    dw_rest = _wgrad_multi(h2b, segs[3:], "w_in_wgrad_gates", dep=dw_qkv)
    dw_seg = [dw_qkv[:, q * W:(q + 1) * W] for q in range(3)] + [dw_rest[:, q * W:(q + 1) * W] for q in range(4)]
    dwf = _wgrad(h2b, dfb, "w_in_wgrad_f", dep=dw_rest)
    dw_in = jnp.concatenate(dw_seg[:3] + [dwf[:, :N_HEADS]] + dw_seg[3:], axis=1)
    emit("w_in", dw_in.reshape(D, N_DEV, in_width // N_DEV).transpose(1, 0, 2))
    dep = emit("w_out", _wgrad(merged_b, dpb, "w_out_wgrad", dep=dwf).reshape(big["w_out"].shape))

    dx0, h1b, dy1b, dgate1, dup1, dgpre1, dgpost1 = _ffn_bwd(
        dx1, x, y1, dgf1, silu1, vec("ffn1_pre_g"), big["ffn1_w_gate"], big["ffn1_w_up"], big["ffn1_w_down"],
        vec("ffn1_post_g"), "ffn1_bwd", dep=dep)
    grads_small["ffn1_pre_g"] = jnp.sum(dgpre1, axis=0)
    grads_small["ffn1_post_g"] = jnp.sum(dgpost1, axis=0)
    dep = consume(("w_in", "w_out"))
    dep = emit("ffn1_w_down", _wgrad(act1, dy1b, "ffn1_wgrad_down", dep=dep).reshape(big["ffn1_w_down"].shape))
    dep = emit("ffn1_w_gate", _wgrad(h1b, dgate1, "ffn1_wgrad_gate", shard_cols=True, dep=dep))
    dep = emit("ffn1_w_up", _wgrad(h1b, dup1, "ffn1_wgrad_up", shard_cols=True, dep=dep))

    loss_row = jnp.pad(loss_lanes, ((0, 0), (0, D - LANES)))
    return loss_row, dx0, grads_small


def kernel(x, ffn1_pre_g, ffn1_w_gate, ffn1_w_up, ffn1_w_down, ffn1_post_g, mix_pre_g, w_in, b_forget, sgu_ln_g, sgu_ln_b, sgu_w_s, sgu_b_s, w_out, mix_post_g, ffn2_pre_g, ffn2_w_gate, ffn2_w_up, ffn2_w_down, ffn2_post_g, loss_target, m_ffn1_pre_g, m_ffn1_w_gate, m_ffn1_w_up, m_ffn1_w_down, m_ffn1_post_g, m_mix_pre_g, m_w_in, m_b_forget, m_sgu_ln_g, m_sgu_ln_b, m_sgu_w_s, m_sgu_b_s, m_w_out, m_mix_post_g, m_ffn2_pre_g, m_ffn2_w_gate, m_ffn2_w_up, m_ffn2_w_down, m_ffn2_post_g, v_ffn1_pre_g, v_ffn1_w_gate, v_ffn1_w_up, v_ffn1_w_down, v_ffn1_post_g, v_mix_pre_g, v_w_in, v_b_forget, v_sgu_ln_g, v_sgu_ln_b, v_sgu_w_s, v_sgu_b_s, v_w_out, v_mix_post_g, v_ffn2_pre_g, v_ffn2_w_gate, v_ffn2_w_up, v_ffn2_w_down, v_ffn2_post_g):
    weights = dict(zip(WEIGHT_NAMES, (ffn1_pre_g, ffn1_w_gate, ffn1_w_up, ffn1_w_down, ffn1_post_g, mix_pre_g, w_in,
                                      b_forget, sgu_ln_g, sgu_ln_b, sgu_w_s, sgu_b_s, w_out, mix_post_g, ffn2_pre_g,
                                      ffn2_w_gate, ffn2_w_up, ffn2_w_down, ffn2_post_g)))
    mom1 = dict(zip(WEIGHT_NAMES, (m_ffn1_pre_g, m_ffn1_w_gate, m_ffn1_w_up, m_ffn1_w_down, m_ffn1_post_g,
                                   m_mix_pre_g, m_w_in, m_b_forget, m_sgu_ln_g, m_sgu_ln_b, m_sgu_w_s, m_sgu_b_s,
                                   m_w_out, m_mix_post_g, m_ffn2_pre_g, m_ffn2_w_gate, m_ffn2_w_up, m_ffn2_w_down,
                                   m_ffn2_post_g)))
    mom2 = dict(zip(WEIGHT_NAMES, (v_ffn1_pre_g, v_ffn1_w_gate, v_ffn1_w_up, v_ffn1_w_down, v_ffn1_post_g,
                                   v_mix_pre_g, v_w_in, v_b_forget, v_sgu_ln_g, v_sgu_ln_b, v_sgu_w_s, v_sgu_b_s,
                                   v_w_out, v_mix_post_g, v_ffn2_pre_g, v_ffn2_w_gate, v_ffn2_w_up, v_ffn2_w_down,
                                   v_ffn2_post_g)))
    D = x.shape[-1]
    small_names = [n for n in WEIGHT_NAMES if n not in BIG_NAMES]

    small = {n: weights[n] for n in small_names}
    shard = lambda n: weights[n][0].astype(BF16)

    ffn1_full = _all_gather([shard(n) for n in WEIGHT_GROUPS["ffn1"]], "ffn1_all_gather")
    gathered = {}
    for cid, grp in ((1, "mix"), (2, "ffn2")):
        shards, _ = lax.optimization_barrier(([shard(n) for n in WEIGHT_GROUPS[grp]], ffn1_full[0]))
        gathered[grp] = _sequencer_exchange(shards, grp + "_gather", True, cid)

    def fetch(group, after):
        if group == "ffn1":
            return zip(WEIGHT_GROUPS[group], ffn1_full)
        arrived, _ = lax.optimization_barrier((gathered[group], after))
        return zip(WEIGHT_GROUPS[group], arrived)

    ready, received = {}, {}

    def emit(name, part):
        ready[name] = part
        for gi, group in enumerate(GRAD_GROUPS):
            if name == group[-1]:
                lands = _sequencer_exchange([ready[n] for n in group], name + "_grad_exchange", False, 3 + gi)
                received.update(zip(group, lands))
        return part

    out = {}

    def consume(names, dep=None):
        for n in names:
            g, d, m_new, v_new = _sum_adamw(received[n], weights[n][0], mom1[n][0], mom2[n][0], "adamw_" + n, dep=dep)
            out[n] = tuple(a[None] for a in (g, d, m_new, v_new))
            dep = g
        return dep

    loss_row, grad_x, grads_small = _local_step(x[0], loss_target[0], small, fetch, emit, consume)

    blobs = _sequencer_exchange([_pack_small(grads_small, D, loss_row)], "small_gather", True,
                                3 + len(GRAD_GROUPS))[0]
    blob, d_blob, m_blob, v_blob = _sum_adamw(
        blobs, _pack_small(small, D), _pack_small({n: mom1[n] for n in small_names}, D),
        _pack_small({n: mom2[n] for n in small_names}, D), "adamw_small")
    consume(("ffn1_w_down", "ffn1_w_gate", "ffn1_w_up"), dep=blob)
    unpacked = [_unpack_small(b, D) for b in (blob, d_blob, m_blob, v_blob)]
    for n in small_names:
        out[n] = tuple(u[n].reshape(weights[n].shape) for u in unpacked)

    loss = blob[ROW_LOSS, 0]
    result = [loss, grad_x[None]]
    for k in range(4):
        result += [out[n][k] for n in WEIGHT_NAMES]
    return tuple(result)
```

```python
import numpy as np
import jax
import jax.numpy as jnp
from jax import lax
from jax.experimental import pallas as pl
from jax.experimental.pallas import tpu as pltpu
from jax.experimental.pallas import tpu_sc as plsc

F32 = jnp.float32
BF16 = jnp.bfloat16

RMS_EPS = 1e-6
LN_EPS = 1e-5
HEAD_DIM = 128
N_HEADS = 8
GROUP_DIM = 128
N_GROUPS = 8
SGU_LEN = 128
CHUNK = 64
N_DEV = 8
LANES = 128
VMEM_LIMIT = 56 * 1024 * 1024
NEG_BIG = -1e30
LOG2E = np.float32(1.0 / np.log(2.0))

ADAM_LR = 0.001
ADAM_B1 = 0.9
ADAM_B2 = 0.999
ADAM_EPS = 1e-08
ADAM_WD = 0.01
ADAM_STEP = 10

MESH = pl.DeviceIdType.MESH
ANY = pl.BlockSpec(memory_space=pl.ANY)


def _blk(n, pref):
    return pref if (n >= pref and n % pref == 0) else n


def _mm(a, b):
    return jnp.dot(a, b, preferred_element_type=F32)


def _mm_nt(a, b):
    return lax.dot_general(a, b, (((1,), (1,)), ((), ())), preferred_element_type=F32)


def _mm_tn(a, b):
    return lax.dot_general(a, b, (((0,), (0,)), ((), ())), preferred_element_type=F32)


def _params(sem):
    return pltpu.CompilerParams(dimension_semantics=sem, vmem_limit_bytes=VMEM_LIMIT)


def _gelu(x):
    return 0.5 * x * (1.0 + lax.erf(x * np.float32(1.0 / np.sqrt(2.0))))


def _gelu_grad(x):
    cdf = 0.5 * (1.0 + lax.erf(x * np.float32(1.0 / np.sqrt(2.0))))
    return cdf + x * jnp.exp(-0.5 * x * x) * np.float32(1.0 / np.sqrt(2.0 * np.pi))


def _rms_scale(v):
    return lax.rsqrt(jnp.mean(v * v, axis=-1, keepdims=True) + RMS_EPS)


def _rms_bwd(dy, xhat, r, g):
    dxh = dy * g
    return r * (dxh - xhat * jnp.mean(dxh * xhat, axis=-1, keepdims=True))


def _ffn_fwd(x, g_pre, wg, wu, wd, g_post, name):
    T, D = x.shape
    ns, _, fs = wg.shape
    tm = _blk(T, 256)

    def body(x_ref, gpre_ref, wg_ref, wu_ref, wd_ref, gpost_ref, xo_ref, y_ref, dgf_ref, silu_ref, act_ref):
        xv = x_ref[...]
        h = (xv * _rms_scale(xv) * gpre_ref[...]).astype(BF16)
        y = jnp.zeros((tm, D), F32)
        pre = (_mm(h, wg_ref[0]), _mm(h, wu_ref[0]))
        for j in range(ns):
            gg, uu = pre
            if j + 1 < ns:
                pre = (_mm(h, wg_ref[j + 1]), _mm(h, wu_ref[j + 1]))
            cols = slice(j * fs, (j + 1) * fs)
            sg = jax.nn.sigmoid(gg)
            silu = gg * sg
            act = (silu * uu).astype(BF16)
            dgf_ref[:, cols] = (uu * (sg * (1.0 + gg * (1.0 - sg)))).astype(BF16)
            silu_ref[:, cols] = silu.astype(BF16)
            act_ref[:, cols] = act
            y = y + _mm(act, wd_ref[j])
        y_ref[...] = y
        xo_ref[...] = xv + 0.5 * (y * _rms_scale(y) * gpost_ref[...])

    row = pl.BlockSpec((tm, D), lambda i: (i, 0))
    vec = pl.BlockSpec((1, D), lambda i: (0, 0))
    wide = pl.BlockSpec((tm, ns * fs), lambda i: (i, 0))
    return pl.pallas_call(
        body, name=name, grid=(T // tm,),
        in_specs=[row, vec,
                  pl.BlockSpec((ns, D, fs), lambda i: (0, 0, 0), pipeline_mode=pl.Buffered(1)),
                  pl.BlockSpec((ns, D, fs), lambda i: (0, 0, 0), pipeline_mode=pl.Buffered(1)),
                  pl.BlockSpec((ns, fs, D), lambda i: (0, 0, 0), pipeline_mode=pl.Buffered(1)),
                  vec],
        out_specs=[row, row, wide, wide, wide],
        out_shape=[jax.ShapeDtypeStruct((T, D), F32), jax.ShapeDtypeStruct((T, D), F32)]
        + [jax.ShapeDtypeStruct((T, ns * fs), BF16)] * 3,
        compiler_params=_params(("parallel",)),
    )(x, g_pre, wg, wu, wd, g_post)


def _after(dep):
    return jnp.zeros((8, LANES), F32) if dep is None else dep


def _ffn_bwd(dxo, x, y, dgf, silu, g_pre, wg, wu, wd, g_post, name, dep=None):
    T, D = x.shape
    ns, _, fs = wg.shape
    tm = _blk(T, 256)
    n_i = T // tm

    def body(dxo_ref, x_ref, y_ref, dgf_ref, silu_ref, gpre_ref, wg_ref, wu_ref, wd_ref, gpost_ref, _,
             dx_ref, hb_ref, dyb_ref, dgb_ref, dub_ref, dgpre_ref, dgpost_ref):
        yv = y_ref[...]
        s = _rms_scale(yv)
        n = yv * s
        dxo = dxo_ref[...]
        dn = 0.5 * dxo
        dgpost_ref[...] = jnp.sum(dn * n, axis=0, keepdims=True)
        dyv = _rms_bwd(dn, n, s, gpost_ref[...]).astype(BF16)
        dyb_ref[...] = dyv
        xv = x_ref[...]
        rs = _rms_scale(xv)
        xhat = xv * rs
        hb_ref[...] = (xhat * gpre_ref[...]).astype(BF16)

        dh = jnp.zeros((tm, D), F32)
        da = _mm_nt(dyv, wd_ref[0])
        for j in range(ns):
            cur = da
            if j + 1 < ns:
                da = _mm_nt(dyv, wd_ref[j + 1])
            cols = slice(j * fs, (j + 1) * fs)
            dgate = (cur * dgf_ref[:, cols].astype(F32)).astype(BF16)
            dup = (cur * silu_ref[:, cols].astype(F32)).astype(BF16)
            dgb_ref[:, cols] = dgate
            dub_ref[:, cols] = dup
            dh = dh + _mm_nt(dgate, wg_ref[j]) + _mm_nt(dup, wu_ref[j])

        dgpre_ref[...] = jnp.sum(dh * xhat, axis=0, keepdims=True)
        dx_ref[...] = _rms_bwd(dh, xhat, rs, gpre_ref[...]) + dxo

    F = ns * fs
    row = pl.BlockSpec((tm, D), lambda i: (i, 0))
    vec = pl.BlockSpec((1, D), lambda i: (0, 0))
    wide = pl.BlockSpec((tm, F), lambda i: (i, 0))
    part = pl.BlockSpec((None, 1, D), lambda i: (i, 0, 0))
    return pl.pallas_call(
        body, name=name, grid=(n_i,),
        in_specs=[row, row, row, wide, wide, vec,
                  pl.BlockSpec((ns, D, fs), lambda i: (0, 0, 0), pipeline_mode=pl.Buffered(1)),
                  pl.BlockSpec((ns, D, fs), lambda i: (0, 0, 0), pipeline_mode=pl.Buffered(1)),
                  pl.BlockSpec((ns, fs, D), lambda i: (0, 0, 0), pipeline_mode=pl.Buffered(1)),
                  vec, ANY],
        out_specs=[row, row, row, wide, wide, part, part],
        out_shape=[jax.ShapeDtypeStruct((T, D), F32), jax.ShapeDtypeStruct((T, D), BF16),
                   jax.ShapeDtypeStruct((T, D), BF16), jax.ShapeDtypeStruct((T, F), BF16),
                   jax.ShapeDtypeStruct((T, F), BF16),
                   jax.ShapeDtypeStruct((n_i, 1, D), F32), jax.ShapeDtypeStruct((n_i, 1, D), F32)],
        compiler_params=_params(("parallel",)),
    )(dxo, x, y, dgf, silu, g_pre, wg, wu, wd, g_post, _after(dep))


def _wgrad(xm, ym, name, shard_cols=False, dep=None):
    T, M = xm.shape
    N = ym.shape[-1]
    assert M * N * 4 <= 16 * 1024 * 1024, (M, N)
    tk = _blk(T, 1024)
    n_k = T // tk
    fs = N // N_DEV
    cw = _blk(max(M, N), 512)

    def body(x_ref, y_ref, _, o_ref, acc_scr):
        k = pl.program_id(0)

        @pl.when(k == 0)
        def _():
            acc_scr[...] = jnp.zeros_like(acc_scr)

        if N >= M:
            x = x_ref[...]
            for c in range(N // cw):
                acc_scr[:, c * cw:(c + 1) * cw] += _mm_tn(x, y_ref[:, c * cw:(c + 1) * cw])
        else:
            y = y_ref[...]
            for c in range(M // cw):
                acc_scr[c * cw:(c + 1) * cw, :] += _mm_tn(x_ref[:, c * cw:(c + 1) * cw], y)

        @pl.when(k == n_k - 1)
        def _():
            if shard_cols:
                for s in range(N_DEV):
                    o_ref[s] = acc_scr[:, s * fs:(s + 1) * fs].astype(BF16)
            else:
                o_ref[...] = acc_scr[...].astype(BF16)

    if shard_cols:
        out_spec = pl.BlockSpec((N_DEV, M, fs), lambda k: (0, 0, 0), pipeline_mode=pl.Buffered(1))
        out_shape = jax.ShapeDtypeStruct((N_DEV, M, fs), BF16)
    else:
        out_spec = pl.BlockSpec((M, N), lambda k: (0, 0), pipeline_mode=pl.Buffered(1))
        out_shape = jax.ShapeDtypeStruct((M, N), BF16)
    return pl.pallas_call(
        body, name=name, grid=(n_k,),
        in_specs=[pl.BlockSpec((tk, M), lambda k: (k, 0)), pl.BlockSpec((tk, N), lambda k: (k, 0)), ANY],
        out_specs=out_spec, out_shape=out_shape,
        scratch_shapes=[pltpu.VMEM((M, N), F32)],
        compiler_params=_params(("arbitrary",)),
    )(xm, ym, _after(dep))


def _wgrad_multi(xm, segs, name, dep=None):
    T, M = xm.shape
    N = segs[0][0].shape[-1]
    n_seg = len(segs)
    assert M * N * n_seg * 4 <= 16 * 1024 * 1024, (M, N, n_seg)
    tk = _blk(T, 512)
    n_k = T // tk

    def body(*refs):
        x_ref, y_refs = refs[0], refs[1:1 + n_seg]
        o_ref, acc_scr = refs[2 + n_seg], refs[3 + n_seg]
        k = pl.program_id(0)

        @pl.when(k == 0)
        def _():
            acc_scr[...] = jnp.zeros_like(acc_scr)

        x = x_ref[...]
        for s in range(n_seg):
            acc_scr[:, s * N:(s + 1) * N] += _mm_tn(x, y_refs[s][...])

        @pl.when(k == n_k - 1)
        def _():
            o_ref[...] = acc_scr[...].astype(BF16)

    y_specs = [pl.BlockSpec((tk, N), lambda k: (k, 0)) if idx is None
               else pl.BlockSpec((None, tk, N), lambda k, idx=idx: (idx, k, 0)) for _, idx in segs]
    return pl.pallas_call(
        body, name=name, grid=(n_k,),
        in_specs=[pl.BlockSpec((tk, M), lambda k: (k, 0))] + y_specs + [ANY],
        out_specs=pl.BlockSpec((M, n_seg * N), lambda k: (0, 0), pipeline_mode=pl.Buffered(1)),
        out_shape=jax.ShapeDtypeStruct((M, n_seg * N), BF16),
        scratch_shapes=[pltpu.VMEM((M, n_seg * N), F32)],
        compiler_params=_params(("arbitrary",)),
    )(xm, *[arr for arr, _ in segs], _after(dep))


def _mix_in_fwd(x1, g, w7, wf, name):
    T, D = x1.shape
    n_seg, _, W = w7.shape
    tm = _blk(T, 512)

    def body(x_ref, g_ref, w_ref, wf_ref, z_ref, f_ref, hb_ref):
        xv = x_ref[...]
        h = (xv * _rms_scale(xv) * g_ref[...]).astype(BF16)
        hb_ref[...] = h
        f_ref[...] = _mm(h, wf_ref[...])
        for s in range(n_seg):
            z_ref[s] = _mm(h, w_ref[s]).astype(BF16)

    return pl.pallas_call(
        body, name=name, grid=(T // tm,),
        in_specs=[pl.BlockSpec((tm, D), lambda i: (i, 0)),
                  pl.BlockSpec((1, D), lambda i: (0, 0)),
                  pl.BlockSpec((n_seg, D, W), lambda i: (0, 0, 0), pipeline_mode=pl.Buffered(1)),
                  pl.BlockSpec((D, LANES), lambda i: (0, 0))],
        out_specs=[pl.BlockSpec((n_seg, tm, W), lambda i: (0, i, 0)),
                   pl.BlockSpec((tm, LANES), lambda i: (i, 0)),
                   pl.BlockSpec((tm, D), lambda i: (i, 0))],
        out_shape=[jax.ShapeDtypeStruct((n_seg, T, W), BF16), jax.ShapeDtypeStruct((T, LANES), F32),
                   jax.ShapeDtypeStruct((T, D), BF16)],
        compiler_params=_params(("parallel",)),
    )(x1, g, w7, wf)


def _mix_in_bwd(dx2, x1, g, segs, dfb, w7, wf, name, dep=None):
    T, D = x1.shape
    n_seg, _, W = w7.shape
    tm = _blk(T, 512)
    n_i = T // tm

    def body(*refs):
        dx2_ref, x_ref, g_ref = refs[:3]
        seg_refs = refs[3:3 + n_seg]
        df_ref, w_ref, wf_ref, _, dx1_ref, dg_ref = refs[3 + n_seg:]
        dh = _mm_nt(df_ref[...], wf_ref[...])
        for q in range(n_seg):
            dh = dh + _mm_nt(seg_refs[q][...], w_ref[q])
        xv = x_ref[...]
        r = _rms_scale(xv)
        xhat = xv * r
        dg_ref[...] = jnp.sum(dh * xhat, axis=0, keepdims=True)
        dx1_ref[...] = _rms_bwd(dh, xhat, r, g_ref[...]) + dx2_ref[...]

    row = pl.BlockSpec((tm, D), lambda i: (i, 0))
    seg_specs = []
    seg_args = []
    for arr, idx in segs:
        if idx is None:
            seg_specs.append(pl.BlockSpec((tm, W), lambda i: (i, 0)))
        else:
            seg_specs.append(pl.BlockSpec((None, tm, W), lambda i, idx=idx: (idx, i, 0)))
        seg_args.append(arr)
    return pl.pallas_call(
        body, name=name, grid=(n_i,),
        in_specs=[row, row, pl.BlockSpec((1, D), lambda i: (0, 0))] + seg_specs + [
            pl.BlockSpec((tm, LANES), lambda i: (i, 0)),
            pl.BlockSpec((n_seg, D, W), lambda i: (0, 0, 0), pipeline_mode=pl.Buffered(1)),
            pl.BlockSpec((D, LANES), lambda i: (0, 0)), ANY],
        out_specs=[row, pl.BlockSpec((None, 1, D), lambda i: (i, 0, 0))],
        out_shape=[jax.ShapeDtypeStruct((T, D), F32), jax.ShapeDtypeStruct((n_i, 1, D), F32)],
        compiler_params=_params(("parallel",)),
    )(dx2, x1, g, *seg_args, dfb, w7, wf, _after(dep))


def _forget_cumsum(f, b_pad, name):
    T, L = f.shape
    tb = _blk(T, 256)

    def body(f_ref, b_ref, c_ref, carry):
        @pl.when(pl.program_id(0) == 0)
        def _():
            carry[...] = jnp.zeros_like(carry)

        lf = jax.nn.log_sigmoid(f_ref[...] + b_ref[...])
        rows = lax.broadcasted_iota(jnp.int32, (tb, tb), 0)
        cols = lax.broadcasted_iota(jnp.int32, (tb, tb), 1)
        tri = (cols <= rows).astype(F32)
        c = jnp.dot(tri, lf, preferred_element_type=F32, precision=lax.Precision.HIGHEST) + carry[...]
        carry[...] = c[tb - 1:tb, :]
        for h in range(N_HEADS):
            c_ref[h] = jnp.broadcast_to(c[:, h:h + 1] * LOG2E, (tb, L))

    return pl.pallas_call(
        body, name=name, grid=(T // tb,),
        in_specs=[pl.BlockSpec((tb, L), lambda i: (i, 0)), pl.BlockSpec((1, L), lambda i: (0, 0))],
        out_specs=pl.BlockSpec((N_HEADS, tb, L), lambda i: (0, i, 0)),
        out_shape=jax.ShapeDtypeStruct((N_HEADS, T, L), F32),
        scratch_shapes=[pltpu.VMEM((1, L), F32)],
        compiler_params=_params(("arbitrary",)),
    )(f, b_pad)


def _forget_bwd(dc, f, b_pad, name):
    T, L = f.shape
    tb = _blk(T, 256)
    nb = T // tb

    def body(dc_ref, f_ref, b_ref, df_ref, db_ref, carry):
        @pl.when(pl.program_id(0) == 0)
        def _():
            carry[...] = jnp.zeros_like(carry)
            db_ref[...] = jnp.zeros_like(db_ref)

        rows = lax.broadcasted_iota(jnp.int32, (tb, tb), 0)
        cols = lax.broadcasted_iota(jnp.int32, (tb, tb), 1)
        tri = (cols >= rows).astype(F32)
        r = jnp.dot(tri, dc_ref[...], preferred_element_type=F32, precision=lax.Precision.HIGHEST) + carry[...]
        carry[...] = r[0:1, :]
        df = r * (1.0 - jax.nn.sigmoid(f_ref[...] + b_ref[...]))
        df_ref[...] = df.astype(BF16)
        db_ref[...] += jnp.sum(df, axis=0, keepdims=True)

    rev = pl.BlockSpec((tb, L), lambda i: (nb - 1 - i, 0))
    one = pl.BlockSpec((1, L), lambda i: (0, 0))
    return pl.pallas_call(
        body, name=name, grid=(nb,),
        in_specs=[rev, rev, one], out_specs=[rev, one],
        out_shape=[jax.ShapeDtypeStruct((T, L), BF16), jax.ShapeDtypeStruct((1, L), F32)],
        scratch_shapes=[pltpu.VMEM((1, L), F32)],
        compiler_params=_params(("arbitrary",)),
    )(dc, f, b_pad)


ATTN_TILE = 512
ATTN_CHAINS = 4


def _attn_geometry(T):
    ta = _blk(T, ATTN_TILE)
    nc = ATTN_CHAINS if (T // ta) % ATTN_CHAINS == 0 else 1
    return ta, nc, T // ta


def _causal_tile(ta, keys_on_rows=False):
    rows = lax.broadcasted_iota(jnp.int32, (ta, ta), 0)
    cols = lax.broadcasted_iota(jnp.int32, (ta, ta), 1)
    return rows <= cols if keys_on_rows else cols <= rows


def _chunk(ref, j, ta):
    return ref[pl.ds(pl.multiple_of(j * ta, ta), ta), :]


def _attn_fwd_keys_on_rows(z7, vt, c_rep, name):
    _, T, W = z7.shape
    H = W // HEAD_DIM
    ta, nc, n_chunks = _attn_geometry(T)
    scale = np.float32(1.0 / np.sqrt(HEAD_DIM))
    reps = ta // LANES

    def body(q_ref, k_ref, vt_ref, c_ref, o_ref, lse_ref):
        g = pl.program_id(1)

        def scores(ch, k):
            return _mm_nt(k, q_ref[ch * ta:(ch + 1) * ta, :])

        def update(state, raw, vt, cj, diagonal):
            m_prev, l_prev, acc_prev = state
            st = raw * (scale * LOG2E) - cj
            if diagonal:
                st = jnp.where(_causal_tile(ta, keys_on_rows=True), st, NEG_BIG)
            m_new = jnp.maximum(m_prev, jnp.max(st, axis=0, keepdims=True))
            alpha = jnp.exp2(m_prev - m_new)
            pt = jnp.exp2(st - m_new)
            l_new = alpha * l_prev + jnp.sum(pt, axis=0, keepdims=True)
            acc_new = alpha * acc_prev + _mm(vt, pt.astype(BF16))
            return m_new, l_new, acc_new

        def load(j):
            cj = _chunk(c_ref, j, ta)
            return _chunk(k_ref, j, ta), vt_ref[j], jnp.concatenate([cj] * reps, axis=1)

        def full_chunk(j, states):
            k, vt, cj = load(j)
            raws = [scores(ch, k) for ch in range(nc)]
            return tuple(update(states[ch], raws[ch], vt, cj, False) for ch in range(nc))

        first = (jnp.full((1, ta), NEG_BIG, F32), jnp.zeros((1, ta), F32), jnp.zeros((HEAD_DIM, ta), F32))
        states = list(lax.fori_loop(0, nc * g, full_chunk, (first,) * nc))
        for jj in range(nc):
            k, vt, cj = load(nc * g + jj)
            raws = {ch: scores(ch, k) for ch in range(jj, nc)}
            for ch in range(jj, nc):
                states[ch] = update(states[ch], raws[ch], vt, cj, ch == jj)
        for ch in range(nc):
            m, l, acc = states[ch]
            o_ref[ch * ta:(ch + 1) * ta, :] = (acc / l).T
            lse_ref[ch] = m + jnp.log2(l)

    tq = nc * ta
    return pl.pallas_call(
        body, name=name, grid=(H, n_chunks // nc),
        in_specs=[pl.BlockSpec((None, tq, HEAD_DIM), lambda h, g: (0, g, h)),
                  pl.BlockSpec((None, T, HEAD_DIM), lambda h, g: (1, 0, h)),
                  pl.BlockSpec((None, n_chunks, HEAD_DIM, ta), lambda h, g: (h, 0, 0, 0)),
                  pl.BlockSpec((None, T, LANES), lambda h, g: (h, 0, 0))],
        out_specs=[pl.BlockSpec((tq, HEAD_DIM), lambda h, g: (g, h)),
                   pl.BlockSpec((None, nc, 1, ta), lambda h, g: (h, g, 0, 0))],
        out_shape=[jax.ShapeDtypeStruct((T, W), F32), jax.ShapeDtypeStruct((H, n_chunks, 1, ta), F32)],
        compiler_params=_params(("parallel", "arbitrary")),
    )(z7, z7, vt, c_rep)


def _attn_bwd_fused(z7, kt, dob, c_rep, lse_chunks, d_chunks, name):
    _, T, W = z7.shape
    H = W // HEAD_DIM
    ta, nc, n_chunks = _attn_geometry(T)
    n_steps = n_chunks // nc
    scale = np.float32(1.0 / np.sqrt(HEAD_DIM))
    reps = ta // LANES

    def body(k_ref, v_ref, kt_ref, q_ref, do_ref, c_ref, lse_ref, d_ref,
             dk_ref, dv_ref, dck_ref, dq_ref, dcq_ref, dk_scr, dv_scr, dck_scr, dqt_scr, dcq_scr):
        g = pl.program_id(1)

        @pl.when(g == 0)
        def _():
            dqt_scr[...] = jnp.zeros_like(dqt_scr)
            dcq_scr[...] = jnp.zeros_like(dcq_scr)

        dk_scr[...] = jnp.zeros_like(dk_scr)
        dv_scr[...] = jnp.zeros_like(dv_scr)
        dck_scr[...] = jnp.zeros_like(dck_scr)

        def products(ch, q, do):
            rows = slice(ch * ta, (ch + 1) * ta)
            return _mm_nt(k_ref[rows, :], q), _mm_nt(v_ref[rows, :], do)

        def update(ch, i, q, do, prods, diagonal):
            rows = slice(ch * ta, (ch + 1) * ta)
            cj = c_ref[rows, :]
            st = prods[0] * (scale * LOG2E) - jnp.concatenate([cj] * reps, axis=1) - lse_ref[i]
            if diagonal:
                st = jnp.where(_causal_tile(ta, keys_on_rows=True), st, NEG_BIG)
            pt = jnp.exp2(st)
            dv_scr[ch] += _mm(pt.astype(BF16), do)
            dst = pt * (prods[1] - d_ref[i])
            dst_b = dst.astype(BF16)
            dk_scr[ch] += _mm(dst_b, q)
            dqt_scr[i] += _mm(kt_ref[ch], dst_b)
            dcq_scr[i] += jnp.sum(dst, axis=0, keepdims=True)
            lane_sum = dst[:, :LANES]
            for r in range(1, reps):
                lane_sum = lane_sum + dst[:, r * LANES:(r + 1) * LANES]
            dck_scr[ch] += lane_sum

        for ii in range(nc):
            i = nc * g + ii
            q = _chunk(q_ref, i, ta)
            do = _chunk(do_ref, i, ta)
            prods = [products(ch, q, do) for ch in range(0, ii + 1)]
            for ch in range(0, ii + 1):
                update(ch, i, q, do, prods[ch], ch == ii)

        def full_chunk(i, carry):
            q = _chunk(q_ref, i, ta)
            do = _chunk(do_ref, i, ta)
            prods = [products(ch, q, do) for ch in range(nc)]
            for ch in range(nc):
                update(ch, i, q, do, prods[ch], False)
            return carry

        lax.fori_loop(nc * (g + 1), n_chunks, full_chunk, 0)
        for ch in range(nc):
            rows = slice(ch * ta, (ch + 1) * ta)
            dk_ref[rows, :] = (dk_scr[ch] * scale).astype(BF16)
            dv_ref[rows, :] = dv_scr[ch].astype(BF16)
            ones = jnp.ones((8, LANES), F32)
            sums = lax.dot_general(ones, dck_scr[ch], (((1,), (1,)), ((), ())), preferred_element_type=F32,
                                   precision=lax.Precision.HIGHEST)
            dck_ref[ch] = -sums[0:1, :]

        @pl.when(g == n_steps - 1)
        def _():
            for i in range(n_chunks):
                dq_ref[i * ta:(i + 1) * ta, :] = (dqt_scr[i] * scale).T.astype(BF16)
            dcq_ref[...] = dcq_scr[...]

    tk = nc * ta
    chunks = pl.BlockSpec((None, n_chunks, 1, ta), lambda h, g: (h, 0, 0, 0))
    tile = pl.BlockSpec((tk, HEAD_DIM), lambda h, g: (g, h))
    return pl.pallas_call(
        body, name=name, grid=(H, n_steps),
        in_specs=[pl.BlockSpec((None, tk, HEAD_DIM), lambda h, g: (1, g, h)),
                  pl.BlockSpec((None, tk, HEAD_DIM), lambda h, g: (2, g, h)),
                  pl.BlockSpec((None, nc, HEAD_DIM, ta), lambda h, g: (h, g, 0, 0)),
                  pl.BlockSpec((None, T, HEAD_DIM), lambda h, g: (0, 0, h)),
                  pl.BlockSpec((T, HEAD_DIM), lambda h, g: (0, h)),
                  pl.BlockSpec((None, tk, LANES), lambda h, g: (h, g, 0)),
                  chunks, chunks],
        out_specs=[tile, tile, pl.BlockSpec((None, nc, 1, ta), lambda h, g: (h, g, 0, 0)),
                   pl.BlockSpec((T, HEAD_DIM), lambda h, g: (0, h)), chunks],
        out_shape=[jax.ShapeDtypeStruct((T, W), BF16), jax.ShapeDtypeStruct((T, W), BF16),
                   jax.ShapeDtypeStruct((H, n_chunks, 1, ta), F32), jax.ShapeDtypeStruct((T, W), BF16),
                   jax.ShapeDtypeStruct((H, n_chunks, 1, ta), F32)],
        scratch_shapes=[pltpu.VMEM((nc, ta, HEAD_DIM), F32), pltpu.VMEM((nc, ta, HEAD_DIM), F32),
                        pltpu.VMEM((nc, ta, LANES), F32), pltpu.VMEM((n_chunks, HEAD_DIM, ta), F32),
                        pltpu.VMEM((n_chunks, 1, ta), F32)],
        compiler_params=_params(("parallel", "arbitrary")),
    )(z7, z7, kt, z7, dob, c_rep, lse_chunks, d_chunks)


def _chunk_causal_mask():
    rows = lax.broadcasted_iota(jnp.int32, (SGU_LEN, SGU_LEN), 0)
    cols = lax.broadcasted_iota(jnp.int32, (SGU_LEN, SGU_LEN), 1)
    return (cols // CHUNK) <= (rows // CHUNK)


def _sgu_norm_mix(sv, lng_ref, lnb_ref, ws_ref, bs_ref, vn_scr, mixed_scr, vhat_scr=None):
    tm = sv.shape[0]
    vs = _gelu(sv)
    mask = _chunk_causal_mask()
    rstds = []
    for g in range(N_GROUPS):
        lanes = slice(g * GROUP_DIM, (g + 1) * GROUP_DIM)
        blk = vs[:, lanes]
        cen = blk - jnp.mean(blk, axis=-1, keepdims=True)
        rstd = lax.rsqrt(jnp.mean(cen * cen, axis=-1, keepdims=True) + LN_EPS)
        vhat = cen * rstd
        rstds.append(rstd)
        if vhat_scr is not None:
            vhat_scr[:, lanes] = vhat
        vn_scr[:, lanes] = (vhat * lng_ref[:, lanes] + lnb_ref[:, lanes]).astype(BF16)
        wm = jnp.where(mask, ws_ref[g], 0.0).astype(BF16)
        for w in range(tm // SGU_LEN):
            rows = slice(w * SGU_LEN, (w + 1) * SGU_LEN)
            mixed_scr[rows, lanes] = _mm(wm, vn_scr[rows, lanes]) + bs_ref[g]
    return rstds


def _mix_out_fwd(z7, o_a, x1, lng, lnb, ws, bs, w_out, g_post, name):
    _, T, W = z7.shape
    D = x1.shape[1]
    tm = _blk(T, 256)

    def body(u_ref, sv_ref, ga_ref, gb_ref, oa_ref, x1_ref, lng_ref, lnb_ref, ws_ref, bs_ref, wo_ref, gp_ref,
             x2_ref, p_ref, mb_ref, vn_scr, mixed_scr):
        _sgu_norm_mix(sv_ref[...].astype(F32), lng_ref, lnb_ref, ws_ref, bs_ref, vn_scr, mixed_scr)
        o_b = _gelu(u_ref[...].astype(F32)) * mixed_scr[...]
        merged = (jax.nn.sigmoid(ga_ref[...].astype(F32)) * oa_ref[...]
                  + jax.nn.sigmoid(gb_ref[...].astype(F32)) * o_b).astype(BF16)
        mb_ref[...] = merged
        p = _mm(merged, wo_ref[...])
        p_ref[...] = p
        x2_ref[...] = x1_ref[...] + p * _rms_scale(p) * gp_ref[...]

    def seg(idx):
        return pl.BlockSpec((None, tm, W), lambda i, idx=idx: (idx, i, 0))

    row = pl.BlockSpec((tm, D), lambda i: (i, 0))
    vec = pl.BlockSpec((1, D), lambda i: (0, 0))
    return pl.pallas_call(
        body, name=name, grid=(T // tm,),
        in_specs=[seg(3), seg(4), seg(5), seg(6), row, row, vec, vec,
                  pl.BlockSpec((N_GROUPS, SGU_LEN, SGU_LEN), lambda i: (0, 0, 0)),
                  pl.BlockSpec((N_GROUPS, SGU_LEN, 1), lambda i: (0, 0, 0)),
                  pl.BlockSpec((D, D), lambda i: (0, 0)), vec],
        out_specs=[row, row, row],
        out_shape=[jax.ShapeDtypeStruct((T, D), F32), jax.ShapeDtypeStruct((T, D), F32),
                   jax.ShapeDtypeStruct((T, D), BF16)],
        scratch_shapes=[pltpu.VMEM((tm, W), BF16), pltpu.VMEM((tm, W), F32)],
        compiler_params=_params(("parallel",)),
    )(z7, z7, z7, z7, o_a, x1, lng, lnb, ws, bs, w_out, g_post)


def _mix_out_bwd(dx2, p, z7, o_a, lng, lnb, ws, bs, w_out, g_post, name, dep=None):
    _, T, W = z7.shape
    D = dx2.shape[1]
    tm = _blk(T, 256)
    n_w = tm // SGU_LEN

    def body(dx2_ref, p_ref, u_ref, sv_ref, ga_ref, gb_ref, oa_ref, lng_ref, lnb_ref, ws_ref, bs_ref, wo_ref, gp_ref, _,
             dpb_ref, dob_ref, dvec_ref, dz_ref, dgp_ref, dlng_ref, dlnb_ref, dws_ref, dbs_ref,
             vn_scr, mixed_scr, vhat_scr, dmix_scr, dvn_scr):
        @pl.when(pl.program_id(0) == 0)
        def _():
            dgp_ref[...] = jnp.zeros_like(dgp_ref)
            dlng_ref[...] = jnp.zeros_like(dlng_ref)
            dlnb_ref[...] = jnp.zeros_like(dlnb_ref)
            dws_ref[...] = jnp.zeros_like(dws_ref)
            dbs_ref[...] = jnp.zeros_like(dbs_ref)

        pv = p_ref[...]
        s = _rms_scale(pv)
        n = pv * s
        dn = dx2_ref[...]
        dgp_ref[...] += jnp.sum(dn * n, axis=0, keepdims=True)
        dpb = _rms_bwd(dn, n, s, gp_ref[...]).astype(BF16)
        dpb_ref[...] = dpb
        dmerged = _mm_nt(dpb, wo_ref[...])

        sv = sv_ref[...].astype(F32)
        rstds = _sgu_norm_mix(sv, lng_ref, lnb_ref, ws_ref, bs_ref, vn_scr, mixed_scr, vhat_scr)
        u_pre = u_ref[...].astype(F32)
        u = _gelu(u_pre)
        mixed = mixed_scr[...]
        sa = jax.nn.sigmoid(ga_ref[...].astype(F32))
        sb = jax.nn.sigmoid(gb_ref[...].astype(F32))
        oa = oa_ref[...]
        do_a = (dmerged * sa).astype(BF16)
        dob_ref[...] = do_a
        prod = do_a.astype(F32) * oa
        for h in range(N_HEADS):
            sums = lax.dot_general(jnp.ones((8, LANES), F32), prod[:, h * HEAD_DIM:(h + 1) * HEAD_DIM],
                                   (((1,), (1,)), ((), ())), preferred_element_type=F32,
                                   precision=lax.Precision.HIGHEST)
            dvec_ref[h, 0] = sums[0:1, :]
        dz_ref[2] = (dmerged * oa * (sa * (1.0 - sa))).astype(BF16)
        dz_ref[3] = (dmerged * (u * mixed) * (sb * (1.0 - sb))).astype(BF16)
        do_b = dmerged * sb
        dz_ref[0] = (do_b * mixed * _gelu_grad(u_pre)).astype(BF16)
        dmix_scr[...] = do_b * u

        mask = _chunk_causal_mask()
        for g in range(N_GROUPS):
            lanes = slice(g * GROUP_DIM, (g + 1) * GROUP_DIM)
            wm = jnp.where(mask, ws_ref[g], 0.0).astype(BF16)
            dws = jnp.zeros((SGU_LEN, SGU_LEN), F32)
            dbs = jnp.zeros((SGU_LEN, 1), F32)
            for w in range(n_w):
                rows = slice(w * SGU_LEN, (w + 1) * SGU_LEN)
                dmix = dmix_scr[rows, lanes]
                dmix_b = dmix.astype(BF16)
                dvn_scr[rows, lanes] = _mm_tn(wm, dmix_b)
                dws = dws + _mm_nt(dmix_b, vn_scr[rows, lanes])
                dbs = dbs + jnp.sum(dmix, axis=-1, keepdims=True)
            dws_ref[g] += jnp.where(mask, dws, 0.0)
            dbs_ref[g] += dbs
            dvn = dvn_scr[:, lanes]
            vhat = vhat_scr[:, lanes]
            dlng_ref[:, lanes] += jnp.sum(dvn * vhat, axis=0, keepdims=True)
            dlnb_ref[:, lanes] += jnp.sum(dvn, axis=0, keepdims=True)
            dvh = dvn * lng_ref[:, lanes]
            dvs = rstds[g] * (dvh - jnp.mean(dvh, axis=-1, keepdims=True)
                              - vhat * jnp.mean(dvh * vhat, axis=-1, keepdims=True))
            dvn_scr[:, lanes] = dvs
        dz_ref[1] = (dvn_scr[...] * _gelu_grad(sv)).astype(BF16)

    def seg(idx):
        return pl.BlockSpec((None, tm, W), lambda i, idx=idx: (idx, i, 0))

    row = pl.BlockSpec((tm, D), lambda i: (i, 0))
    vec = pl.BlockSpec((1, D), lambda i: (0, 0))
    ws_spec = pl.BlockSpec((N_GROUPS, SGU_LEN, SGU_LEN), lambda i: (0, 0, 0))
    bs_spec = pl.BlockSpec((N_GROUPS, SGU_LEN, 1), lambda i: (0, 0, 0))
    return pl.pallas_call(
        body, name=name, grid=(T // tm,),
        in_specs=[row, row, seg(3), seg(4), seg(5), seg(6), row, vec, vec, ws_spec, bs_spec,
                  pl.BlockSpec((D, D), lambda i: (0, 0)), vec, ANY],
        out_specs=[row, row, pl.BlockSpec((N_HEADS, 1, 1, tm), lambda i: (0, i, 0, 0)),
                   pl.BlockSpec((4, tm, W), lambda i: (0, i, 0)), vec, vec, vec, ws_spec, bs_spec],
        out_shape=[jax.ShapeDtypeStruct((T, D), BF16), jax.ShapeDtypeStruct((T, W), BF16),
                   jax.ShapeDtypeStruct((N_HEADS, T // tm, 1, tm), F32), jax.ShapeDtypeStruct((4, T, W), BF16),
                   jax.ShapeDtypeStruct((1, D), F32), jax.ShapeDtypeStruct((1, D), F32),
                   jax.ShapeDtypeStruct((1, D), F32),
                   jax.ShapeDtypeStruct((N_GROUPS, SGU_LEN, SGU_LEN), F32),
                   jax.ShapeDtypeStruct((N_GROUPS, SGU_LEN, 1), F32)],
        scratch_shapes=[pltpu.VMEM((tm, W), BF16), pltpu.VMEM((tm, W), F32), pltpu.VMEM((tm, W), F32),
                        pltpu.VMEM((tm, W), F32), pltpu.VMEM((tm, W), F32)],
        compiler_params=_params(("arbitrary",)),
    )(dx2, p, z7, z7, z7, z7, o_a, lng, lnb, ws, bs, w_out, g_post, _after(dep))


def _loss_head(y, target, name):
    T, D = y.shape
    tm = _blk(T, 1024)
    n_i = T // tm

    def body(y_ref, t_ref, dy_ref, loss_ref, acc_scr):
        i = pl.program_id(0)

        @pl.when(i == 0)
        def _():
            acc_scr[...] = jnp.zeros_like(acc_scr)

        e = y_ref[...] - t_ref[...]
        dy_ref[...] = e * np.float32(1.0 / D)
        acc_scr[...] += jnp.sum(e * e, axis=0, keepdims=True)

        @pl.when(i == n_i - 1)
        def _():
            total = jnp.sum(acc_scr[...], axis=-1, keepdims=True) * np.float32(0.5 / D)
            loss_ref[...] = jnp.broadcast_to(total, loss_ref.shape)

    row = pl.BlockSpec((tm, D), lambda i: (i, 0))
    return pl.pallas_call(
        body, name=name, grid=(n_i,),
        in_specs=[row, row],
        out_specs=[row, pl.BlockSpec((1, LANES), lambda i: (0, 0))],
        out_shape=[jax.ShapeDtypeStruct((T, D), F32), jax.ShapeDtypeStruct((1, LANES), F32)],
        scratch_shapes=[pltpu.VMEM((1, D), F32)],
        compiler_params=_params(("arbitrary",)),
    )(y, target)


def _adamw_math(w, g, m, v):
    m_new = ADAM_B1 * m + (1.0 - ADAM_B1) * g
    v_new = ADAM_B2 * v + (1.0 - ADAM_B2) * (g * g)
    m_hat = m_new / np.float32(1.0 - ADAM_B1 ** ADAM_STEP)
    v_hat = v_new / np.float32(1.0 - ADAM_B2 ** ADAM_STEP)
    delta = -ADAM_LR * (m_hat / (jnp.sqrt(v_hat) + ADAM_EPS) + ADAM_WD * w)
    return delta, m_new, v_new


def _sum_adamw(parts, w, m, v, name, dep=None):
    n, R, C = parts.shape
    tr = _blk(R, 512)

    def body(p_ref, w_ref, m_ref, v_ref, _, g_ref, d_ref, mo_ref, vo_ref):
        g = p_ref[0].astype(F32)
        for s in range(1, n):
            g = g + p_ref[s].astype(F32)
        delta, m_new, v_new = _adamw_math(w_ref[...], g, m_ref[...], v_ref[...])
        g_ref[...] = g
        d_ref[...] = delta
        mo_ref[...] = m_new
        vo_ref[...] = v_new

    row = pl.BlockSpec((tr, C), lambda i: (i, 0))
    shp = jax.ShapeDtypeStruct((R, C), F32)
    return pl.pallas_call(
        body, name=name, grid=(R // tr,),
        in_specs=[pl.BlockSpec((n, tr, C), lambda i: (0, i, 0)), row, row, row, ANY],
        out_specs=[row, row, row, row], out_shape=[shp, shp, shp, shp],
        compiler_params=_params(("parallel",)),
    )(parts, w, m, v, _after(dep))


def _position():
    return lax.axis_index("x"), lax.axis_index("y"), lax.axis_index("c")


def _slot(px, py, pc):
    return 4 * px + 2 * py + pc


def _all_gather(shards, name):
    n = len(shards)

    def body(*refs):
        ins, outs = refs[:n], refs[n:2 * n]
        send_sems, recv_sems, local_sems = refs[2 * n:]
        x, y, c = _position()
        me, sibling = (x, y, c), (x, y, 1 - c)
        chips = [(1 - x, y), (x, 1 - y), (1 - x, 1 - y)]

        def copy(a, k, block, to, src=None):
            dst = outs[a].at[_slot(*block)]
            return pltpu.make_async_remote_copy(
                src_ref=dst if src is None else src, dst_ref=dst,
                send_sem=send_sems.at[a, k], recv_sem=recv_sems.at[a, k],
                device_id=to, device_id_type=MESH)

        mine = [pltpu.make_async_copy(ins[a], outs[a].at[_slot(*me)], local_sems.at[a]) for a in range(n)]
        for cp in mine:
            cp.start()
        first = []
        for a in range(n):
            first.append(copy(a, 0, me, sibling, src=ins[a]))
            first += [copy(a, 1 + j, me, (*chip, c), src=ins[a]) for j, chip in enumerate(chips)]
        for cp in first:
            cp.start()
        passed = []
        for j, chip in enumerate(chips):
            for a in range(n):
                copy(a, 1 + j, (*chip, c), me).wait_recv()
                fwd = copy(a, 4 + j, (*chip, c), sibling)
                fwd.start()
                passed.append(fwd)
        for a in range(n):
            copy(a, 0, sibling, me).wait_recv()
            for j, chip in enumerate(chips):
                copy(a, 4 + j, (*chip, 1 - c), me).wait_recv()
        for cp in first + passed:
            cp.wait_send()
        for cp in mine:
            cp.wait()

    return pl.pallas_call(
        body, name=name,
        in_specs=[ANY] * n, out_specs=[ANY] * n,
        out_shape=[jax.ShapeDtypeStruct((N_DEV,) + s.shape, s.dtype) for s in shards],
        scratch_shapes=[pltpu.SemaphoreType.DMA((n, 7)), pltpu.SemaphoreType.DMA((n, 7)),
                        pltpu.SemaphoreType.DMA((n,))],
    )(*shards)


def _peer(x, y, c, k):
    return (1 - x if k & 4 else x, 1 - y if k & 2 else y, 1 - c if k & 1 else c)


def _remote_copies(src_refs, land_refs, send_sems, recv_sems, gather, outgoing):
    x, y, c = _position()
    me = _slot(x, y, c)
    copies = []
    for k in range(1, N_DEV):
        peer = _peer(x, y, c, k)
        for a in range(len(src_refs)):
            src = src_refs[a] if gather else src_refs[a].at[_slot(*peer)]
            dst = land_refs[a].at[me if outgoing else _slot(*peer)]
            sem = a * (N_DEV - 1) + k - 1
            copies.append(pltpu.make_async_remote_copy(
                src_ref=src, dst_ref=dst, send_sem=send_sems.at[sem], recv_sem=recv_sems.at[sem],
                device_id=peer, device_id_type=MESH))
    return copies


def _sequencer_exchange(srcs, name, gather, collective_id):
    n = len(srcs)
    hbm = pltpu.MemorySpace.HBM
    src_refs = [jax.new_ref(s, memory_space=hbm) for s in srcs]
    land_refs = [jax.empty_ref(jax.ShapeDtypeStruct(((N_DEV,) + s.shape) if gather else s.shape, s.dtype),
                               memory_space=hbm) for s in srcs]
    n_sems = n * (N_DEV - 1)
    block_bytes = sum(s.size * s.dtype.itemsize // (1 if gather else N_DEV) for s in srcs)
    cost = pl.CostEstimate(flops=0, transcendentals=0, bytes_accessed=2 * N_DEV * block_bytes,
                           remote_bytes_transferred=(N_DEV - 1) * block_bytes)

    @pl.kernel(mesh=plsc.ScalarSubcoreMesh(axis_name="sequencer", num_cores=1), name=name,
               scratch_types=(pltpu.SemaphoreType.DMA((n_sems,)), pltpu.SemaphoreType.DMA((n_sems,)),
                              pltpu.SemaphoreType.DMA((n,))),
               cost_estimate=cost,
               compiler_params=pltpu.CompilerParams(collective_id=collective_id))
    def launch(send_sems, recv_sems, local_sems):
        x, y, c = _position()
        me = _slot(x, y, c)
        barrier = pltpu.get_barrier_semaphore()
        for k in range(1, N_DEV):
            pl.semaphore_signal(barrier, inc=1, device_id=_peer(x, y, c, k), device_id_type=MESH)
        pl.semaphore_wait(barrier, N_DEV - 1)
        mine = [pltpu.make_async_copy(src_refs[a] if gather else src_refs[a].at[me], land_refs[a].at[me],
                                      local_sems.at[a]) for a in range(n)]
        for cp in mine:
            cp.start()
        sends = _remote_copies(src_refs, land_refs, send_sems, recv_sems, gather, outgoing=True)
        for cp in sends:
            cp.start()
        for cp in _remote_copies(src_refs, land_refs, send_sems, recv_sems, gather, outgoing=False):
            cp.wait_recv()
        for cp in sends:
            cp.wait_send()
        for cp in mine:
            cp.wait()

    launch()
    return [r[...] for r in land_refs]


SMALL_VECS = ("ffn1_pre_g", "ffn1_post_g", "mix_pre_g", "sgu_ln_g", "sgu_ln_b", "mix_post_g", "ffn2_pre_g",
              "ffn2_post_g")
ROW_BS = len(SMALL_VECS)
ROW_BF = ROW_BS + 1
ROW_LOSS = ROW_BF + 1
ROW_WS = 16
BLOB_ROWS = ROW_WS + SGU_LEN


def _pack_small(vals, D, loss_row=None):
    rows = [vals[n].reshape(1, D) for n in SMALL_VECS]
    rows.append(vals["sgu_b_s"].reshape(1, D))
    rows.append(jnp.pad(vals["b_forget"].reshape(1, N_HEADS), ((0, 0), (0, D - N_HEADS))))
    rows.append(jnp.zeros((1, D), F32) if loss_row is None else loss_row)
    rows.append(jnp.zeros((ROW_WS - ROW_LOSS - 1, D), F32))
    rows.append(vals["sgu_w_s"].reshape(SGU_LEN, D))
    return jnp.concatenate(rows, axis=0)


def _unpack_small(blob, D):
    out = {n: blob[r:r + 1] for r, n in enumerate(SMALL_VECS)}
    out["sgu_b_s"] = blob[ROW_BS].reshape(1, N_GROUPS, SGU_LEN)
    out["b_forget"] = blob[ROW_BF, :N_HEADS].reshape(1, N_HEADS)
    out["sgu_w_s"] = blob[ROW_WS:].reshape(1, N_GROUPS, SGU_LEN, SGU_LEN)
    return out


WEIGHT_NAMES = ("ffn1_pre_g", "ffn1_w_gate", "ffn1_w_up", "ffn1_w_down", "ffn1_post_g", "mix_pre_g", "w_in",
                "b_forget", "sgu_ln_g", "sgu_ln_b", "sgu_w_s", "sgu_b_s", "w_out", "mix_post_g", "ffn2_pre_g",
                "ffn2_w_gate", "ffn2_w_up", "ffn2_w_down", "ffn2_post_g")
BIG_NAMES = ("ffn1_w_gate", "ffn1_w_up", "ffn1_w_down", "w_in", "w_out", "ffn2_w_gate", "ffn2_w_up", "ffn2_w_down")
WEIGHT_GROUPS = {"ffn1": ("ffn1_w_gate", "ffn1_w_up", "ffn1_w_down"), "mix": ("w_in", "w_out"),
                 "ffn2": ("ffn2_w_gate", "ffn2_w_up", "ffn2_w_down")}
GRAD_GROUPS = (("ffn2_w_gate", "ffn2_w_up", "ffn2_w_down"), ("w_in", "w_out"), ("ffn1_w_down",), ("ffn1_w_gate",),
               ("ffn1_w_up",))


def _local_step(x, target, small, fetch, emit, consume):
    T, D = x.shape
    W = N_HEADS * HEAD_DIM
    vec = lambda n: small[n].reshape(1, D)
    big = dict(fetch("ffn1", x))

    x1, y1, dgf1, silu1, act1 = _ffn_fwd(x, vec("ffn1_pre_g"), big["ffn1_w_gate"], big["ffn1_w_up"], big["ffn1_w_down"],
                                  vec("ffn1_post_g"), "ffn1_fwd")

    big.update(fetch("mix", x1))
    w_in_all = big["w_in"]
    in_width = N_DEV * w_in_all.shape[2]
    w_in = w_in_all.transpose(1, 0, 2).reshape(D, in_width)
    col_f = 3 * W
    col_u = col_f + N_HEADS
    seg_starts = (0, W, 2 * W, col_u, col_u + W, col_u + 2 * W, col_u + 3 * W)
    w7 = jnp.stack([w_in[:, s:s + W] for s in seg_starts])
    wf = jnp.pad(w_in[:, col_f:col_u], ((0, 0), (0, LANES - N_HEADS)))
    w_out = big["w_out"].reshape(D, D)
    b_pad = jnp.pad(small["b_forget"].reshape(1, N_HEADS), ((0, 0), (0, LANES - N_HEADS)))
    lng, lnb = vec("sgu_ln_g"), vec("sgu_ln_b")
    ws = small["sgu_w_s"].reshape(N_GROUPS, SGU_LEN, SGU_LEN)
    bs = small["sgu_b_s"].reshape(N_GROUPS, SGU_LEN, 1)

    z7, f_logit, h2b = _mix_in_fwd(x1, vec("mix_pre_g"), w7, wf, "mix_in_fwd")
    c_rep = _forget_cumsum(f_logit, b_pad, "forget_cumsum")
    ta, _, n_chunks = _attn_geometry(T)
    vt = z7[2].reshape(n_chunks, ta, N_HEADS, HEAD_DIM).transpose(2, 0, 3, 1)
    o_a, lse_chunks = _attn_fwd_keys_on_rows(z7, vt, c_rep, "attn_fwd")
    x2, p, merged_b = _mix_out_fwd(z7, o_a, x1, lng, lnb, ws, bs, w_out, vec("mix_post_g"), "mix_out_fwd")
    big.update(fetch("ffn2", x2))
    x3, y2, dgf2, silu2, act2 = _ffn_fwd(x2, vec("ffn2_pre_g"), big["ffn2_w_gate"], big["ffn2_w_up"], big["ffn2_w_down"],
                                  vec("ffn2_post_g"), "ffn2_fwd")
    dy, loss_lanes = _loss_head(x3, target, "loss_head")

    grads_small = {}

    dx2, h3b, dy2b, dgate2, dup2, dgpre, dgpost = _ffn_bwd(
        dy, x2, y2, dgf2, silu2, vec("ffn2_pre_g"), big["ffn2_w_gate"], big["ffn2_w_up"], big["ffn2_w_down"],
        vec("ffn2_post_g"), "ffn2_bwd")
    grads_small["ffn2_pre_g"] = jnp.sum(dgpre, axis=0)
    grads_small["ffn2_post_g"] = jnp.sum(dgpost, axis=0)
    dep = emit("ffn2_w_gate", _wgrad(h3b, dgate2, "ffn2_wgrad_gate", shard_cols=True))
    dep = emit("ffn2_w_up", _wgrad(h3b, dup2, "ffn2_wgrad_up", shard_cols=True, dep=dep))
    dep = emit("ffn2_w_down", _wgrad(act2, dy2b, "ffn2_wgrad_down", dep=dep).reshape(big["ffn2_w_down"].shape))

    dpb, dob, dvec, dz4, dgp, dlng, dlnb, dws, dbs = _mix_out_bwd(
        dx2, p, z7, o_a, lng, lnb, ws, bs, w_out, vec("mix_post_g"), "mix_out_bwd", dep=dep)
    grads_small["mix_post_g"] = dgp
    grads_small["sgu_ln_g"] = dlng
    grads_small["sgu_ln_b"] = dlnb
    grads_small["sgu_w_s"] = dws
    grads_small["sgu_b_s"] = dbs
    d_chunks = dvec.reshape(N_HEADS, n_chunks, 1, ta)
    kt = z7[1].reshape(n_chunks, ta, N_HEADS, HEAD_DIM).transpose(2, 0, 3, 1)
    dk, dv, dc, dq, dc_q = _attn_bwd_fused(z7, kt, dob, c_rep, lse_chunks, d_chunks, "attn_bwd")
    dc_pad = jnp.pad((dc + dc_q).reshape(N_HEADS, T).T, ((0, 0), (0, LANES - N_HEADS)))
    dfb, dbf = _forget_bwd(dc_pad, f_logit, b_pad, "forget_bwd")
    grads_small["b_forget"] = dbf[:, :N_HEADS]
    segs = [(dq, None), (dk, None), (dv, None), (dz4, 0), (dz4, 1), (dz4, 2), (dz4, 3)]
    dep = consume(("ffn2_w_gate", "ffn2_w_up", "ffn2_w_down"))
    dx1, dgm = _mix_in_bwd(dx2, x1, vec("mix_pre_g"), segs, dfb, w7, wf, "mix_in_bwd", dep=dep)
    grads_small["mix_pre_g"] = jnp.sum(dgm, axis=0)
    dw_qkv = _wgrad_multi(h2b, segs[:3], "w_in_wgrad_qkv", dep=dx1)
    dw_rest = _wgrad_multi(h2b, segs[3:], "w_in_wgrad_gates", dep=dw_qkv)
    dw_seg = [dw_qkv[:, q * W:(q + 1) * W] for q in range(3)] + [dw_rest[:, q * W:(q + 1) * W] for q in range(4)]
    dwf = _wgrad(h2b, dfb, "w_in_wgrad_f", dep=dw_rest)
    dw_in = jnp.concatenate(dw_seg[:3] + [dwf[:, :N_HEADS]] + dw_seg[3:], axis=1)
    emit("w_in", dw_in.reshape(D, N_DEV, in_width // N_DEV).transpose(1, 0, 2))
    dep = emit("w_out", _wgrad(merged_b, dpb, "w_out_wgrad", dep=dwf).reshape(big["w_out"].shape))

    dx0, h1b, dy1b, dgate1, dup1, dgpre1, dgpost1 = _ffn_bwd(
        dx1, x, y1, dgf1, silu1, vec("ffn1_pre_g"), big["ffn1_w_gate"], big["ffn1_w_up"], big["ffn1_w_down"],
        vec("ffn1_post_g"), "ffn1_bwd", dep=dep)
    grads_small["ffn1_pre_g"] = jnp.sum(dgpre1, axis=0)
    grads_small["ffn1_post_g"] = jnp.sum(dgpost1, axis=0)
    dep = consume(("w_in", "w_out"))
    dep = emit("ffn1_w_down", _wgrad(act1, dy1b, "ffn1_wgrad_down", dep=dep).reshape(big["ffn1_w_down"].shape))
    dep = emit("ffn1_w_gate", _wgrad(h1b, dgate1, "ffn1_wgrad_gate", shard_cols=True, dep=dep))
    dep = emit("ffn1_w_up", _wgrad(h1b, dup1, "ffn1_wgrad_up", shard_cols=True, dep=dep))

    loss_row = jnp.pad(loss_lanes, ((0, 0), (0, D - LANES)))
    return loss_row, dx0, grads_small


def kernel(x, ffn1_pre_g, ffn1_w_gate, ffn1_w_up, ffn1_w_down, ffn1_post_g, mix_pre_g, w_in, b_forget, sgu_ln_g, sgu_ln_b, sgu_w_s, sgu_b_s, w_out, mix_post_g, ffn2_pre_g, ffn2_w_gate, ffn2_w_up, ffn2_w_down, ffn2_post_g, loss_target, m_ffn1_pre_g, m_ffn1_w_gate, m_ffn1_w_up, m_ffn1_w_down, m_ffn1_post_g, m_mix_pre_g, m_w_in, m_b_forget, m_sgu_ln_g, m_sgu_ln_b, m_sgu_w_s, m_sgu_b_s, m_w_out, m_mix_post_g, m_ffn2_pre_g, m_ffn2_w_gate, m_ffn2_w_up, m_ffn2_w_down, m_ffn2_post_g, v_ffn1_pre_g, v_ffn1_w_gate, v_ffn1_w_up, v_ffn1_w_down, v_ffn1_post_g, v_mix_pre_g, v_w_in, v_b_forget, v_sgu_ln_g, v_sgu_ln_b, v_sgu_w_s, v_sgu_b_s, v_w_out, v_mix_post_g, v_ffn2_pre_g, v_ffn2_w_gate, v_ffn2_w_up, v_ffn2_w_down, v_ffn2_post_g):
    weights = dict(zip(WEIGHT_NAMES, (ffn1_pre_g, ffn1_w_gate, ffn1_w_up, ffn1_w_down, ffn1_post_g, mix_pre_g, w_in,
                                      b_forget, sgu_ln_g, sgu_ln_b, sgu_w_s, sgu_b_s, w_out, mix_post_g, ffn2_pre_g,
                                      ffn2_w_gate, ffn2_w_up, ffn2_w_down, ffn2_post_g)))
    mom1 = dict(zip(WEIGHT_NAMES, (m_ffn1_pre_g, m_ffn1_w_gate, m_ffn1_w_up, m_ffn1_w_down, m_ffn1_post_g,
                                   m_mix_pre_g, m_w_in, m_b_forget, m_sgu_ln_g, m_sgu_ln_b, m_sgu_w_s, m_sgu_b_s,
                                   m_w_out, m_mix_post_g, m_ffn2_pre_g, m_ffn2_w_gate, m_ffn2_w_up, m_ffn2_w_down,
                                   m_ffn2_post_g)))
    mom2 = dict(zip(WEIGHT_NAMES, (v_ffn1_pre_g, v_ffn1_w_gate, v_ffn1_w_up, v_ffn1_w_down, v_ffn1_post_g,
                                   v_mix_pre_g, v_w_in, v_b_forget, v_sgu_ln_g, v_sgu_ln_b, v_sgu_w_s, v_sgu_b_s,
                                   v_w_out, v_mix_post_g, v_ffn2_pre_g, v_ffn2_w_gate, v_ffn2_w_up, v_ffn2_w_down,
                                   v_ffn2_post_g)))
    D = x.shape[-1]
    small_names = [n for n in WEIGHT_NAMES if n not in BIG_NAMES]

    small = {n: weights[n] for n in small_names}
    shard = lambda n: weights[n][0].astype(BF16)

    ffn1_full = _all_gather([shard(n) for n in WEIGHT_GROUPS["ffn1"]], "ffn1_all_gather")
    gathered = {}
    for cid, grp in ((1, "mix"), (2, "ffn2")):
        shards, _ = lax.optimization_barrier(([shard(n) for n in WEIGHT_GROUPS[grp]], ffn1_full[0]))
        gathered[grp] = _sequencer_exchange(shards, grp + "_gather", True, cid)

    def fetch(group, after):
        if group == "ffn1":
            return zip(WEIGHT_GROUPS[group], ffn1_full)
        arrived, _ = lax.optimization_barrier((gathered[group], after))
        return zip(WEIGHT_GROUPS[group], arrived)

    ready, received = {}, {}

    def emit(name, part):
        ready[name] = part
        for gi, group in enumerate(GRAD_GROUPS):
            if name == group[-1]:
                lands = _sequencer_exchange([ready[n] for n in group], name + "_grad_exchange", False, 3 + gi)
                received.update(zip(group, lands))
        return part

    out = {}

    def consume(names, dep=None):
        for n in names:
            g, d, m_new, v_new = _sum_adamw(received[n], weights[n][0], mom1[n][0], mom2[n][0], "adamw_" + n, dep=dep)
            out[n] = tuple(a[None] for a in (g, d, m_new, v_new))
            dep = g
        return dep

    loss_row, grad_x, grads_small = _local_step(x[0], loss_target[0], small, fetch, emit, consume)

    blobs = _sequencer_exchange([_pack_small(grads_small, D, loss_row)], "small_gather", True,
                                3 + len(GRAD_GROUPS))[0]
    blob, d_blob, m_blob, v_blob = _sum_adamw(
        blobs, _pack_small(small, D), _pack_small({n: mom1[n] for n in small_names}, D),
        _pack_small({n: mom2[n] for n in small_names}, D), "adamw_small")
    consume(("ffn1_w_down", "ffn1_w_gate", "ffn1_w_up"), dep=blob)
    unpacked = [_unpack_small(b, D) for b in (blob, d_blob, m_blob, v_blob)]
    for n in small_names:
        out[n] = tuple(u[n].reshape(weights[n].shape) for u in unpacked)

    loss = blob[ROW_LOSS, 0]
    result = [loss, grad_x[None]]
    for k in range(4):
        result += [out[n][k] for n in WEIGHT_NAMES]
    return tuple(result)
```

```python
import numpy as np
import jax
import jax.numpy as jnp
from jax import lax
from jax.experimental import pallas as pl
from jax.experimental.pallas import tpu as pltpu
from jax.experimental.pallas import tpu_sc as plsc

F32 = jnp.float32
BF16 = jnp.bfloat16

RMS_EPS = 1e-6
LN_EPS = 1e-5
HEAD_DIM = 128
N_HEADS = 8
GROUP_DIM = 128
N_GROUPS = 8
SGU_LEN = 128
CHUNK = 64
N_DEV = 8
LANES = 128
VMEM_LIMIT = 56 * 1024 * 1024
NEG_BIG = -1e30
LOG2E = np.float32(1.0 / np.log(2.0))
Q_PRESCALE = np.float32(LOG2E / np.sqrt(HEAD_DIM))

ADAM_LR = 0.001
ADAM_B1 = 0.9
ADAM_B2 = 0.999
ADAM_EPS = 1e-08
ADAM_WD = 0.01
ADAM_STEP = 10

MESH = pl.DeviceIdType.MESH
ANY = pl.BlockSpec(memory_space=pl.ANY)


def _blk(n, pref):
    return pref if (n >= pref and n % pref == 0) else n


def _mm(a, b):
    return jnp.dot(a, b, preferred_element_type=F32)


def _mm_nt(a, b):
    return lax.dot_general(a, b, (((1,), (1,)), ((), ())), preferred_element_type=F32)


def _mm_tn(a, b):
    return lax.dot_general(a, b, (((0,), (0,)), ((), ())), preferred_element_type=F32)


def _params(sem):
    return pltpu.CompilerParams(dimension_semantics=sem, vmem_limit_bytes=VMEM_LIMIT)


def _gelu(x):
    return 0.5 * x * (1.0 + lax.erf(x * np.float32(1.0 / np.sqrt(2.0))))


def _gelu_grad(x):
    cdf = 0.5 * (1.0 + lax.erf(x * np.float32(1.0 / np.sqrt(2.0))))
    return cdf + x * jnp.exp(-0.5 * x * x) * np.float32(1.0 / np.sqrt(2.0 * np.pi))


def _rms_scale(v):
    return lax.rsqrt(jnp.mean(v * v, axis=-1, keepdims=True) + RMS_EPS)


def _rms_bwd(dy, xhat, r, g):
    dxh = dy * g
    return r * (dxh - xhat * jnp.mean(dxh * xhat, axis=-1, keepdims=True))


def _ffn_fwd(x, g_pre, wg, wu, wd, g_post, name):
    T, D = x.shape
    ns, _, fs = wg.shape
    tm = _blk(T, 256)

    def body(x_ref, gpre_ref, wg_ref, wu_ref, wd_ref, gpost_ref, xo_ref, y_ref, dgf_ref, silu_ref, act_ref):
        xv = x_ref[...]
        h = (xv * _rms_scale(xv) * gpre_ref[...]).astype(BF16)
        y = jnp.zeros((tm, D), F32)
        pre = (_mm(h, wg_ref[0]), _mm(h, wu_ref[0]))
        for j in range(ns):
            gg, uu = pre
            if j + 1 < ns:
                pre = (_mm(h, wg_ref[j + 1]), _mm(h, wu_ref[j + 1]))
            cols = slice(j * fs, (j + 1) * fs)
            sg = jax.nn.sigmoid(gg)
            silu = gg * sg
            act = (silu * uu).astype(BF16)
            dgf_ref[:, cols] = (uu * (sg * (1.0 + gg * (1.0 - sg)))).astype(BF16)
            silu_ref[:, cols] = silu.astype(BF16)
            act_ref[:, cols] = act
            y = y + _mm(act, wd_ref[j])
        y_ref[...] = y
        xo_ref[...] = xv + 0.5 * (y * _rms_scale(y) * gpost_ref[...])

    row = pl.BlockSpec((tm, D), lambda i: (i, 0))
    vec = pl.BlockSpec((1, D), lambda i: (0, 0))
    wide = pl.BlockSpec((tm, ns * fs), lambda i: (i, 0))
    return pl.pallas_call(
        body, name=name, grid=(T // tm,),
        in_specs=[row, vec,
                  pl.BlockSpec((ns, D, fs), lambda i: (0, 0, 0), pipeline_mode=pl.Buffered(1)),
                  pl.BlockSpec((ns, D, fs), lambda i: (0, 0, 0), pipeline_mode=pl.Buffered(1)),
                  pl.BlockSpec((ns, fs, D), lambda i: (0, 0, 0), pipeline_mode=pl.Buffered(1)),
                  vec],
        out_specs=[row, row, wide, wide, wide],
        out_shape=[jax.ShapeDtypeStruct((T, D), F32), jax.ShapeDtypeStruct((T, D), F32)]
        + [jax.ShapeDtypeStruct((T, ns * fs), BF16)] * 3,
        compiler_params=_params(("parallel",)),
    )(x, g_pre, wg, wu, wd, g_post)


def _after(dep):
    return jnp.zeros((8, LANES), F32) if dep is None else dep


def _ffn_bwd(dxo, x, y, dgf, silu, g_pre, wg, wu, wd, g_post, name, dep=None):
    T, D = x.shape
    ns, _, fs = wg.shape
    tm = _blk(T, 256)
    n_i = T // tm

    def body(dxo_ref, x_ref, y_ref, dgf_ref, silu_ref, gpre_ref, wg_ref, wu_ref, wd_ref, gpost_ref, _,
             dx_ref, hb_ref, dyb_ref, dgb_ref, dub_ref, dgpre_ref, dgpost_ref):
        yv = y_ref[...]
        s = _rms_scale(yv)
        n = yv * s
        dxo = dxo_ref[...]
        dn = 0.5 * dxo
        dgpost_ref[...] = jnp.sum(dn * n, axis=0, keepdims=True)
        dyv = _rms_bwd(dn, n, s, gpost_ref[...]).astype(BF16)
        dyb_ref[...] = dyv
        xv = x_ref[...]
        rs = _rms_scale(xv)
        xhat = xv * rs
        hb_ref[...] = (xhat * gpre_ref[...]).astype(BF16)

        dh = jnp.zeros((tm, D), F32)
        da = _mm_nt(dyv, wd_ref[0])
        for j in range(ns):
            cur = da
            if j + 1 < ns:
                da = _mm_nt(dyv, wd_ref[j + 1])
            cols = slice(j * fs, (j + 1) * fs)
            dgate = (cur * dgf_ref[:, cols].astype(F32)).astype(BF16)
            dup = (cur * silu_ref[:, cols].astype(F32)).astype(BF16)
            dgb_ref[:, cols] = dgate
            dub_ref[:, cols] = dup
            dh = dh + _mm_nt(dgate, wg_ref[j]) + _mm_nt(dup, wu_ref[j])

        dgpre_ref[...] = jnp.sum(dh * xhat, axis=0, keepdims=True)
        dx_ref[...] = _rms_bwd(dh, xhat, rs, gpre_ref[...]) + dxo

    F = ns * fs
    row = pl.BlockSpec((tm, D), lambda i: (i, 0))
    vec = pl.BlockSpec((1, D), lambda i: (0, 0))
    wide = pl.BlockSpec((tm, F), lambda i: (i, 0))
    part = pl.BlockSpec((None, 1, D), lambda i: (i, 0, 0))
    return pl.pallas_call(
        body, name=name, grid=(n_i,),
        in_specs=[row, row, row, wide, wide, vec,
                  pl.BlockSpec((ns, D, fs), lambda i: (0, 0, 0), pipeline_mode=pl.Buffered(1)),
                  pl.BlockSpec((ns, D, fs), lambda i: (0, 0, 0), pipeline_mode=pl.Buffered(1)),
                  pl.BlockSpec((ns, fs, D), lambda i: (0, 0, 0), pipeline_mode=pl.Buffered(1)),
                  vec, ANY],
        out_specs=[row, row, row, wide, wide, part, part],
        out_shape=[jax.ShapeDtypeStruct((T, D), F32), jax.ShapeDtypeStruct((T, D), BF16),
                   jax.ShapeDtypeStruct((T, D), BF16), jax.ShapeDtypeStruct((T, F), BF16),
                   jax.ShapeDtypeStruct((T, F), BF16),
                   jax.ShapeDtypeStruct((n_i, 1, D), F32), jax.ShapeDtypeStruct((n_i, 1, D), F32)],
        compiler_params=_params(("parallel",)),
    )(dxo, x, y, dgf, silu, g_pre, wg, wu, wd, g_post, _after(dep))


def _wgrad(xm, ym, name, shard_cols=False, dep=None):
    T, M = xm.shape
    N = ym.shape[-1]
    assert M * N * 4 <= 16 * 1024 * 1024, (M, N)
    tk = _blk(T, 1024)
    n_k = T // tk
    fs = N // N_DEV
    cw = _blk(max(M, N), 512)

    def body(x_ref, y_ref, _, o_ref, acc_scr):
        k = pl.program_id(0)

        @pl.when(k == 0)
        def _():
            acc_scr[...] = jnp.zeros_like(acc_scr)

        if N >= M:
            x = x_ref[...]
            for c in range(N // cw):
                acc_scr[:, c * cw:(c + 1) * cw] += _mm_tn(x, y_ref[:, c * cw:(c + 1) * cw])
        else:
            y = y_ref[...]
            for c in range(M // cw):
                acc_scr[c * cw:(c + 1) * cw, :] += _mm_tn(x_ref[:, c * cw:(c + 1) * cw], y)

        @pl.when(k == n_k - 1)
        def _():
            if shard_cols:
                for s in range(N_DEV):
                    o_ref[s] = acc_scr[:, s * fs:(s + 1) * fs].astype(BF16)
            else:
                o_ref[...] = acc_scr[...].astype(BF16)

    if shard_cols:
        out_spec = pl.BlockSpec((N_DEV, M, fs), lambda k: (0, 0, 0), pipeline_mode=pl.Buffered(1))
        out_shape = jax.ShapeDtypeStruct((N_DEV, M, fs), BF16)
    else:
        out_spec = pl.BlockSpec((M, N), lambda k: (0, 0), pipeline_mode=pl.Buffered(1))
        out_shape = jax.ShapeDtypeStruct((M, N), BF16)
    return pl.pallas_call(
        body, name=name, grid=(n_k,),
        in_specs=[pl.BlockSpec((tk, M), lambda k: (k, 0)), pl.BlockSpec((tk, N), lambda k: (k, 0)), ANY],
        out_specs=out_spec, out_shape=out_shape,
        scratch_shapes=[pltpu.VMEM((M, N), F32)],
        compiler_params=_params(("arbitrary",)),
    )(xm, ym, _after(dep))


def _wgrad_multi(xm, segs, name, dep=None):
    T, M = xm.shape
    N = segs[0][0].shape[-1]
    n_seg = len(segs)
    assert M * N * n_seg * 4 <= 16 * 1024 * 1024, (M, N, n_seg)
    tk = _blk(T, 512)
    n_k = T // tk

    def body(*refs):
        x_ref, y_refs = refs[0], refs[1:1 + n_seg]
        o_ref, acc_scr = refs[2 + n_seg], refs[3 + n_seg]
        k = pl.program_id(0)

        @pl.when(k == 0)
        def _():
            acc_scr[...] = jnp.zeros_like(acc_scr)

        x = x_ref[...]
        for s in range(n_seg):
            acc_scr[:, s * N:(s + 1) * N] += _mm_tn(x, y_refs[s][...])

        @pl.when(k == n_k - 1)
        def _():
            o_ref[...] = acc_scr[...].astype(BF16)

    y_specs = [pl.BlockSpec((tk, N), lambda k: (k, 0)) if idx is None
               else pl.BlockSpec((None, tk, N), lambda k, idx=idx: (idx, k, 0)) for _, idx in segs]
    return pl.pallas_call(
        body, name=name, grid=(n_k,),
        in_specs=[pl.BlockSpec((tk, M), lambda k: (k, 0))] + y_specs + [ANY],
        out_specs=pl.BlockSpec((M, n_seg * N), lambda k: (0, 0), pipeline_mode=pl.Buffered(1)),
        out_shape=jax.ShapeDtypeStruct((M, n_seg * N), BF16),
        scratch_shapes=[pltpu.VMEM((M, n_seg * N), F32)],
        compiler_params=_params(("arbitrary",)),
    )(xm, *[arr for arr, _ in segs], _after(dep))


def _mix_in_fwd(x1, g, w7, wf, name):
    T, D = x1.shape
    n_seg, _, W = w7.shape
    tm = _blk(T, 512)

    def body(x_ref, g_ref, w_ref, wf_ref, z_ref, f_ref, hb_ref):
        xv = x_ref[...]
        h = (xv * _rms_scale(xv) * g_ref[...]).astype(BF16)
        hb_ref[...] = h
        f_ref[...] = _mm(h, wf_ref[...])
        for s in range(n_seg):
            z = _mm(h, w_ref[s])
            z_ref[s] = (z * Q_PRESCALE if s == 0 else z).astype(BF16)

    return pl.pallas_call(
        body, name=name, grid=(T // tm,),
        in_specs=[pl.BlockSpec((tm, D), lambda i: (i, 0)),
                  pl.BlockSpec((1, D), lambda i: (0, 0)),
                  pl.BlockSpec((n_seg, D, W), lambda i: (0, 0, 0), pipeline_mode=pl.Buffered(1)),
                  pl.BlockSpec((D, LANES), lambda i: (0, 0))],
        out_specs=[pl.BlockSpec((n_seg, tm, W), lambda i: (0, i, 0)),
                   pl.BlockSpec((tm, LANES), lambda i: (i, 0)),
                   pl.BlockSpec((tm, D), lambda i: (i, 0))],
        out_shape=[jax.ShapeDtypeStruct((n_seg, T, W), BF16), jax.ShapeDtypeStruct((T, LANES), F32),
                   jax.ShapeDtypeStruct((T, D), BF16)],
        compiler_params=_params(("parallel",)),
    )(x1, g, w7, wf)


def _mix_in_bwd(dx2, x1, g, segs, dfb, w7, wf, name, dep=None):
    T, D = x1.shape
    n_seg, _, W = w7.shape
    tm = _blk(T, 512)
    n_i = T // tm

    def body(*refs):
        dx2_ref, x_ref, g_ref = refs[:3]
        seg_refs = refs[3:3 + n_seg]
        df_ref, w_ref, wf_ref, _, dx1_ref, dg_ref = refs[3 + n_seg:]
        dh = _mm_nt(df_ref[...], wf_ref[...])
        for q in range(n_seg):
            dh = dh + _mm_nt(seg_refs[q][...], w_ref[q])
        xv = x_ref[...]
        r = _rms_scale(xv)
        xhat = xv * r
        dg_ref[...] = jnp.sum(dh * xhat, axis=0, keepdims=True)
        dx1_ref[...] = _rms_bwd(dh, xhat, r, g_ref[...]) + dx2_ref[...]

    row = pl.BlockSpec((tm, D), lambda i: (i, 0))
    seg_specs = []
    seg_args = []
    for arr, idx in segs:
        if idx is None:
            seg_specs.append(pl.BlockSpec((tm, W), lambda i: (i, 0)))
        else:
            seg_specs.append(pl.BlockSpec((None, tm, W), lambda i, idx=idx: (idx, i, 0)))
        seg_args.append(arr)
    return pl.pallas_call(
        body, name=name, grid=(n_i,),
        in_specs=[row, row, pl.BlockSpec((1, D), lambda i: (0, 0))] + seg_specs + [
            pl.BlockSpec((tm, LANES), lambda i: (i, 0)),
            pl.BlockSpec((n_seg, D, W), lambda i: (0, 0, 0), pipeline_mode=pl.Buffered(1)),
            pl.BlockSpec((D, LANES), lambda i: (0, 0)), ANY],
        out_specs=[row, pl.BlockSpec((None, 1, D), lambda i: (i, 0, 0))],
        out_shape=[jax.ShapeDtypeStruct((T, D), F32), jax.ShapeDtypeStruct((n_i, 1, D), F32)],
        compiler_params=_params(("parallel",)),
    )(dx2, x1, g, *seg_args, dfb, w7, wf, _after(dep))


def _forget_cumsum(f, b_pad, name):
    T, L = f.shape
    tb = _blk(T, 256)

    def body(f_ref, b_ref, c_ref, carry):
        @pl.when(pl.program_id(0) == 0)
        def _():
            carry[...] = jnp.zeros_like(carry)

        lf = jax.nn.log_sigmoid(f_ref[...] + b_ref[...])
        rows = lax.broadcasted_iota(jnp.int32, (tb, tb), 0)
        cols = lax.broadcasted_iota(jnp.int32, (tb, tb), 1)
        tri = (cols <= rows).astype(F32)
        c = jnp.dot(tri, lf, preferred_element_type=F32, precision=lax.Precision.HIGHEST) + carry[...]
        carry[...] = c[tb - 1:tb, :]
        for h in range(N_HEADS):
            c_ref[h] = jnp.broadcast_to(c[:, h:h + 1] * LOG2E, (tb, L))

    return pl.pallas_call(
        body, name=name, grid=(T // tb,),
        in_specs=[pl.BlockSpec((tb, L), lambda i: (i, 0)), pl.BlockSpec((1, L), lambda i: (0, 0))],
        out_specs=pl.BlockSpec((N_HEADS, tb, L), lambda i: (0, i, 0)),
        out_shape=jax.ShapeDtypeStruct((N_HEADS, T, L), F32),
        scratch_shapes=[pltpu.VMEM((1, L), F32)],
        compiler_params=_params(("arbitrary",)),
    )(f, b_pad)


def _forget_bwd(dc, f, b_pad, name):
    T, L = f.shape
    tb = _blk(T, 256)
    nb = T // tb

    def body(dc_ref, f_ref, b_ref, df_ref, db_ref, carry):
        @pl.when(pl.program_id(0) == 0)
        def _():
            carry[...] = jnp.zeros_like(carry)
            db_ref[...] = jnp.zeros_like(db_ref)

        rows = lax.broadcasted_iota(jnp.int32, (tb, tb), 0)
        cols = lax.broadcasted_iota(jnp.int32, (tb, tb), 1)
        tri = (cols >= rows).astype(F32)
        r = jnp.dot(tri, dc_ref[...], preferred_element_type=F32, precision=lax.Precision.HIGHEST) + carry[...]
        carry[...] = r[0:1, :]
        df = r * (1.0 - jax.nn.sigmoid(f_ref[...] + b_ref[...]))
        df_ref[...] = df.astype(BF16)
        db_ref[...] += jnp.sum(df, axis=0, keepdims=True)

    rev = pl.BlockSpec((tb, L), lambda i: (nb - 1 - i, 0))
    one = pl.BlockSpec((1, L), lambda i: (0, 0))
    return pl.pallas_call(
        body, name=name, grid=(nb,),
        in_specs=[rev, rev, one], out_specs=[rev, one],
        out_shape=[jax.ShapeDtypeStruct((T, L), BF16), jax.ShapeDtypeStruct((1, L), F32)],
        scratch_shapes=[pltpu.VMEM((1, L), F32)],
        compiler_params=_params(("arbitrary",)),
    )(dc, f, b_pad)


ATTN_TILE = 512
ATTN_CHAINS = 4


def _attn_geometry(T):
    ta = _blk(T, ATTN_TILE)
    nc = ATTN_CHAINS if (T // ta) % ATTN_CHAINS == 0 else 1
    return ta, nc, T // ta


def _causal_tile(ta, keys_on_rows=False):
    rows = lax.broadcasted_iota(jnp.int32, (ta, ta), 0)
    cols = lax.broadcasted_iota(jnp.int32, (ta, ta), 1)
    return rows <= cols if keys_on_rows else cols <= rows


def _chunk(ref, j, ta):
    return ref[pl.ds(pl.multiple_of(j * ta, ta), ta), :]


def _attn_fwd_keys_on_rows(z7, vt, c_rep, name):
    _, T, W = z7.shape
    H = W // HEAD_DIM
    ta, nc, n_chunks = _attn_geometry(T)
    scale = np.float32(1.0 / np.sqrt(HEAD_DIM))
    reps = ta // LANES

    def body(q_ref, k_ref, vt_ref, c_ref, o_ref, lse_ref):
        g = pl.program_id(1)

        def scores(ch, k):
            return _mm_nt(k, q_ref[ch * ta:(ch + 1) * ta, :])

        def update(state, raw, vt, cj, diagonal):
            m_prev, l_prev, acc_prev = state
            st = raw - cj
            if diagonal:
                st = jnp.where(_causal_tile(ta, keys_on_rows=True), st, NEG_BIG)
            m_new = jnp.maximum(m_prev, jnp.max(st, axis=0, keepdims=True))
            alpha = jnp.exp2(m_prev - m_new)
            pt = jnp.exp2(st - m_new)
            l_new = alpha * l_prev + jnp.sum(pt, axis=0, keepdims=True)
            acc_new = alpha * acc_prev + _mm(vt, pt.astype(BF16))
            return m_new, l_new, acc_new

        def load(j):
            cj = _chunk(c_ref, j, ta)
            return _chunk(k_ref, j, ta), vt_ref[j], jnp.concatenate([cj] * reps, axis=1)

        def full_chunk(j, states):
            k, vt, cj = load(j)
            raws = [scores(ch, k) for ch in range(nc)]
            return tuple(update(states[ch], raws[ch], vt, cj, False) for ch in range(nc))

        first = (jnp.full((1, ta), NEG_BIG, F32), jnp.zeros((1, ta), F32), jnp.zeros((HEAD_DIM, ta), F32))
        states = list(lax.fori_loop(0, nc * g, full_chunk, (first,) * nc))
        for jj in range(nc):
            k, vt, cj = load(nc * g + jj)
            raws = {ch: scores(ch, k) for ch in range(jj, nc)}
            for ch in range(jj, nc):
                states[ch] = update(states[ch], raws[ch], vt, cj, ch == jj)
        for ch in range(nc):
            m, l, acc = states[ch]
            o_ref[ch * ta:(ch + 1) * ta, :] = (acc / l).T
            lse_ref[ch] = m + jnp.log2(l)

    tq = nc * ta
    return pl.pallas_call(
        body, name=name, grid=(H, n_chunks // nc),
        in_specs=[pl.BlockSpec((None, tq, HEAD_DIM), lambda h, g: (0, g, h)),
                  pl.BlockSpec((None, T, HEAD_DIM), lambda h, g: (1, 0, h)),
                  pl.BlockSpec((None, n_chunks, HEAD_DIM, ta), lambda h, g: (h, 0, 0, 0)),
                  pl.BlockSpec((None, T, LANES), lambda h, g: (h, 0, 0))],
        out_specs=[pl.BlockSpec((tq, HEAD_DIM), lambda h, g: (g, h)),
                   pl.BlockSpec((None, nc, 1, ta), lambda h, g: (h, g, 0, 0))],
        out_shape=[jax.ShapeDtypeStruct((T, W), F32), jax.ShapeDtypeStruct((H, n_chunks, 1, ta), F32)],
        compiler_params=_params(("parallel", "arbitrary")),
    )(z7, z7, vt, c_rep)


def _attn_bwd_fused(z7, kt, dob, c_rep, lse_chunks, d_chunks, name):
    _, T, W = z7.shape
    H = W // HEAD_DIM
    ta, nc, n_chunks = _attn_geometry(T)
    n_steps = n_chunks // nc
    scale = np.float32(1.0 / np.sqrt(HEAD_DIM))
    reps = ta // LANES

    def body(k_ref, v_ref, kt_ref, q_ref, do_ref, c_ref, lse_ref, d_ref,
             dk_ref, dv_ref, dck_ref, dq_ref, dcq_ref, dk_scr, dv_scr, dck_scr, dqt_scr, dcq_scr):
        g = pl.program_id(1)

        @pl.when(g == 0)
        def _():
            dqt_scr[...] = jnp.zeros_like(dqt_scr)
            dcq_scr[...] = jnp.zeros_like(dcq_scr)

        dk_scr[...] = jnp.zeros_like(dk_scr)
        dv_scr[...] = jnp.zeros_like(dv_scr)
        dck_scr[...] = jnp.zeros_like(dck_scr)

        def products(ch, q, do):
            rows = slice(ch * ta, (ch + 1) * ta)
            return _mm_nt(k_ref[rows, :], q), _mm_nt(v_ref[rows, :], do)

        def update(ch, i, q, do, prods, diagonal):
            rows = slice(ch * ta, (ch + 1) * ta)
            cj = c_ref[rows, :]
            st = prods[0] - jnp.concatenate([cj] * reps, axis=1) - lse_ref[i]
            if diagonal:
                st = jnp.where(_causal_tile(ta, keys_on_rows=True), st, NEG_BIG)
            pt = jnp.exp2(st)
            dv_scr[ch] += _mm(pt.astype(BF16), do)
            dst = pt * (prods[1] - d_ref[i])
            dst_b = dst.astype(BF16)
            dk_scr[ch] += _mm(dst_b, q)
            dqt_scr[i] += _mm(kt_ref[ch], dst_b)
            dcq_scr[i] += jnp.sum(dst, axis=0, keepdims=True)
            lane_sum = dst[:, :LANES]
            for r in range(1, reps):
                lane_sum = lane_sum + dst[:, r * LANES:(r + 1) * LANES]
            dck_scr[ch] += lane_sum

        for ii in range(nc):
            i = nc * g + ii
            q = _chunk(q_ref, i, ta)
            do = _chunk(do_ref, i, ta)
            prods = [products(ch, q, do) for ch in range(0, ii + 1)]
            for ch in range(0, ii + 1):
                update(ch, i, q, do, prods[ch], ch == ii)

        def full_chunk(i, carry):
            q = _chunk(q_ref, i, ta)
            do = _chunk(do_ref, i, ta)
            prods = [products(ch, q, do) for ch in range(nc)]
            for ch in range(nc):
                update(ch, i, q, do, prods[ch], False)
            return carry

        lax.fori_loop(nc * (g + 1), n_chunks, full_chunk, 0)
        for ch in range(nc):
            rows = slice(ch * ta, (ch + 1) * ta)
            dk_ref[rows, :] = (dk_scr[ch] * np.float32(1.0 / LOG2E)).astype(BF16)
            dv_ref[rows, :] = dv_scr[ch].astype(BF16)
            ones = jnp.ones((8, LANES), F32)
            sums = lax.dot_general(ones, dck_scr[ch], (((1,), (1,)), ((), ())), preferred_element_type=F32,
                                   precision=lax.Precision.HIGHEST)
            dck_ref[ch] = -sums[0:1, :]

        @pl.when(g == n_steps - 1)
        def _():
            for i in range(n_chunks):
                dq_ref[i * ta:(i + 1) * ta, :] = (dqt_scr[i] * scale).T.astype(BF16)
            dcq_ref[...] = dcq_scr[...]

    tk = nc * ta
    chunks = pl.BlockSpec((None, n_chunks, 1, ta), lambda h, g: (h, 0, 0, 0))
    tile = pl.BlockSpec((tk, HEAD_DIM), lambda h, g: (g, h))
    return pl.pallas_call(
        body, name=name, grid=(H, n_steps),
        in_specs=[pl.BlockSpec((None, tk, HEAD_DIM), lambda h, g: (1, g, h)),
                  pl.BlockSpec((None, tk, HEAD_DIM), lambda h, g: (2, g, h)),
                  pl.BlockSpec((None, nc, HEAD_DIM, ta), lambda h, g: (h, g, 0, 0)),
                  pl.BlockSpec((None, T, HEAD_DIM), lambda h, g: (0, 0, h)),
                  pl.BlockSpec((T, HEAD_DIM), lambda h, g: (0, h)),
                  pl.BlockSpec((None, tk, LANES), lambda h, g: (h, g, 0)),
                  chunks, chunks],
        out_specs=[tile, tile, pl.BlockSpec((None, nc, 1, ta), lambda h, g: (h, g, 0, 0)),
                   pl.BlockSpec((T, HEAD_DIM), lambda h, g: (0, h)), chunks],
        out_shape=[jax.ShapeDtypeStruct((T, W), BF16), jax.ShapeDtypeStruct((T, W), BF16),
                   jax.ShapeDtypeStruct((H, n_chunks, 1, ta), F32), jax.ShapeDtypeStruct((T, W), BF16),
                   jax.ShapeDtypeStruct((H, n_chunks, 1, ta), F32)],
        scratch_shapes=[pltpu.VMEM((nc, ta, HEAD_DIM), F32), pltpu.VMEM((nc, ta, HEAD_DIM), F32),
                        pltpu.VMEM((nc, ta, LANES), F32), pltpu.VMEM((n_chunks, HEAD_DIM, ta), F32),
                        pltpu.VMEM((n_chunks, 1, ta), F32)],
        compiler_params=_params(("parallel", "arbitrary")),
    )(z7, z7, kt, z7, dob, c_rep, lse_chunks, d_chunks)


def _chunk_causal_mask():
    rows = lax.broadcasted_iota(jnp.int32, (SGU_LEN, SGU_LEN), 0)
    cols = lax.broadcasted_iota(jnp.int32, (SGU_LEN, SGU_LEN), 1)
    return (cols // CHUNK) <= (rows // CHUNK)


def _sgu_norm_mix(sv, lng_ref, lnb_ref, ws_ref, bs_ref, vn_scr, mixed_scr, vhat_scr=None):
    tm = sv.shape[0]
    vs = _gelu(sv)
    mask = _chunk_causal_mask()
    rstds = []
    for g in range(N_GROUPS):
        lanes = slice(g * GROUP_DIM, (g + 1) * GROUP_DIM)
        blk = vs[:, lanes]
        cen = blk - jnp.mean(blk, axis=-1, keepdims=True)
        rstd = lax.rsqrt(jnp.mean(cen * cen, axis=-1, keepdims=True) + LN_EPS)
        vhat = cen * rstd
        rstds.append(rstd)
        if vhat_scr is not None:
            vhat_scr[:, lanes] = vhat
        vn_scr[:, lanes] = (vhat * lng_ref[:, lanes] + lnb_ref[:, lanes]).astype(BF16)
        wm = jnp.where(mask, ws_ref[g], 0.0).astype(BF16)
        for w in range(tm // SGU_LEN):
            rows = slice(w * SGU_LEN, (w + 1) * SGU_LEN)
            mixed_scr[rows, lanes] = _mm(wm, vn_scr[rows, lanes]) + bs_ref[g]
    return rstds


def _mix_out_fwd(z7, o_a, x1, lng, lnb, ws, bs, w_out, g_post, name):
    _, T, W = z7.shape
    D = x1.shape[1]
    tm = _blk(T, 256)

    def body(u_ref, sv_ref, ga_ref, gb_ref, oa_ref, x1_ref, lng_ref, lnb_ref, ws_ref, bs_ref, wo_ref, gp_ref,
             x2_ref, p_ref, mb_ref, vn_scr, mixed_scr):
        _sgu_norm_mix(sv_ref[...].astype(F32), lng_ref, lnb_ref, ws_ref, bs_ref, vn_scr, mixed_scr)
        o_b = _gelu(u_ref[...].astype(F32)) * mixed_scr[...]
        merged = (jax.nn.sigmoid(ga_ref[...].astype(F32)) * oa_ref[...]
                  + jax.nn.sigmoid(gb_ref[...].astype(F32)) * o_b).astype(BF16)
        mb_ref[...] = merged
        p = _mm(merged, wo_ref[...])
        p_ref[...] = p
        x2_ref[...] = x1_ref[...] + p * _rms_scale(p) * gp_ref[...]

    def seg(idx):
        return pl.BlockSpec((None, tm, W), lambda i, idx=idx: (idx, i, 0))

    row = pl.BlockSpec((tm, D), lambda i: (i, 0))
    vec = pl.BlockSpec((1, D), lambda i: (0, 0))
    return pl.pallas_call(
        body, name=name, grid=(T // tm,),
        in_specs=[seg(3), seg(4), seg(5), seg(6), row, row, vec, vec,
                  pl.BlockSpec((N_GROUPS, SGU_LEN, SGU_LEN), lambda i: (0, 0, 0)),
                  pl.BlockSpec((N_GROUPS, SGU_LEN, 1), lambda i: (0, 0, 0)),
                  pl.BlockSpec((D, D), lambda i: (0, 0)), vec],
        out_specs=[row, row, row],
        out_shape=[jax.ShapeDtypeStruct((T, D), F32), jax.ShapeDtypeStruct((T, D), F32),
                   jax.ShapeDtypeStruct((T, D), BF16)],
        scratch_shapes=[pltpu.VMEM((tm, W), BF16), pltpu.VMEM((tm, W), F32)],
        compiler_params=_params(("parallel",)),
    )(z7, z7, z7, z7, o_a, x1, lng, lnb, ws, bs, w_out, g_post)


def _mix_out_bwd(dx2, p, z7, o_a, lng, lnb, ws, bs, w_out, g_post, name, dep=None):
    _, T, W = z7.shape
    D = dx2.shape[1]
    tm = _blk(T, 256)
    n_w = tm // SGU_LEN

    def body(dx2_ref, p_ref, u_ref, sv_ref, ga_ref, gb_ref, oa_ref, lng_ref, lnb_ref, ws_ref, bs_ref, wo_ref, gp_ref, _,
             dpb_ref, dob_ref, dvec_ref, dz_ref, dgp_ref, dlng_ref, dlnb_ref, dws_ref, dbs_ref,
             vn_scr, mixed_scr, vhat_scr, dmix_scr, dvn_scr):
        @pl.when(pl.program_id(0) == 0)
        def _():
            dgp_ref[...] = jnp.zeros_like(dgp_ref)
            dlng_ref[...] = jnp.zeros_like(dlng_ref)
            dlnb_ref[...] = jnp.zeros_like(dlnb_ref)
            dws_ref[...] = jnp.zeros_like(dws_ref)
            dbs_ref[...] = jnp.zeros_like(dbs_ref)

        pv = p_ref[...]
        s = _rms_scale(pv)
        n = pv * s
        dn = dx2_ref[...]
        dgp_ref[...] += jnp.sum(dn * n, axis=0, keepdims=True)
        dpb = _rms_bwd(dn, n, s, gp_ref[...]).astype(BF16)
        dpb_ref[...] = dpb
        dmerged = _mm_nt(dpb, wo_ref[...])

        sv = sv_ref[...].astype(F32)
        rstds = _sgu_norm_mix(sv, lng_ref, lnb_ref, ws_ref, bs_ref, vn_scr, mixed_scr, vhat_scr)
        u_pre = u_ref[...].astype(F32)
        u = _gelu(u_pre)
        mixed = mixed_scr[...]
        sa = jax.nn.sigmoid(ga_ref[...].astype(F32))
        sb = jax.nn.sigmoid(gb_ref[...].astype(F32))
        oa = oa_ref[...]
        do_a = (dmerged * sa).astype(BF16)
        dob_ref[...] = do_a
        prod = do_a.astype(F32) * oa
        for h in range(N_HEADS):
            sums = lax.dot_general(jnp.ones((8, LANES), F32), prod[:, h * HEAD_DIM:(h + 1) * HEAD_DIM],
                                   (((1,), (1,)), ((), ())), preferred_element_type=F32,
                                   precision=lax.Precision.HIGHEST)
            dvec_ref[h, 0] = sums[0:1, :]
        dz_ref[2] = (dmerged * oa * (sa * (1.0 - sa))).astype(BF16)
        dz_ref[3] = (dmerged * (u * mixed) * (sb * (1.0 - sb))).astype(BF16)
        do_b = dmerged * sb
        dz_ref[0] = (do_b * mixed * _gelu_grad(u_pre)).astype(BF16)
        dmix_scr[...] = do_b * u

        mask = _chunk_causal_mask()
        for g in range(N_GROUPS):
            lanes = slice(g * GROUP_DIM, (g + 1) * GROUP_DIM)
            wm = jnp.where(mask, ws_ref[g], 0.0).astype(BF16)
            dws = jnp.zeros((SGU_LEN, SGU_LEN), F32)
            dbs = jnp.zeros((SGU_LEN, 1), F32)
            for w in range(n_w):
                rows = slice(w * SGU_LEN, (w + 1) * SGU_LEN)
                dmix = dmix_scr[rows, lanes]
                dmix_b = dmix.astype(BF16)
                dvn_scr[rows, lanes] = _mm_tn(wm, dmix_b)
                dws = dws + _mm_nt(dmix_b, vn_scr[rows, lanes])
                dbs = dbs + jnp.sum(dmix, axis=-1, keepdims=True)
            dws_ref[g] += jnp.where(mask, dws, 0.0)
            dbs_ref[g] += dbs
            dvn = dvn_scr[:, lanes]
            vhat = vhat_scr[:, lanes]
            dlng_ref[:, lanes] += jnp.sum(dvn * vhat, axis=0, keepdims=True)
            dlnb_ref[:, lanes] += jnp.sum(dvn, axis=0, keepdims=True)
            dvh = dvn * lng_ref[:, lanes]
            dvs = rstds[g] * (dvh - jnp.mean(dvh, axis=-1, keepdims=True)
                              - vhat * jnp.mean(dvh * vhat, axis=-1, keepdims=True))
            dvn_scr[:, lanes] = dvs
        dz_ref[1] = (dvn_scr[...] * _gelu_grad(sv)).astype(BF16)

    def seg(idx):
        return pl.BlockSpec((None, tm, W), lambda i, idx=idx: (idx, i, 0))

    row = pl.BlockSpec((tm, D), lambda i: (i, 0))
    vec = pl.BlockSpec((1, D), lambda i: (0, 0))
    ws_spec = pl.BlockSpec((N_GROUPS, SGU_LEN, SGU_LEN), lambda i: (0, 0, 0))
    bs_spec = pl.BlockSpec((N_GROUPS, SGU_LEN, 1), lambda i: (0, 0, 0))
    return pl.pallas_call(
        body, name=name, grid=(T // tm,),
        in_specs=[row, row, seg(3), seg(4), seg(5), seg(6), row, vec, vec, ws_spec, bs_spec,
                  pl.BlockSpec((D, D), lambda i: (0, 0)), vec, ANY],
        out_specs=[row, row, pl.BlockSpec((N_HEADS, 1, 1, tm), lambda i: (0, i, 0, 0)),
                   pl.BlockSpec((4, tm, W), lambda i: (0, i, 0)), vec, vec, vec, ws_spec, bs_spec],
        out_shape=[jax.ShapeDtypeStruct((T, D), BF16), jax.ShapeDtypeStruct((T, W), BF16),
                   jax.ShapeDtypeStruct((N_HEADS, T // tm, 1, tm), F32), jax.ShapeDtypeStruct((4, T, W), BF16),
                   jax.ShapeDtypeStruct((1, D), F32), jax.ShapeDtypeStruct((1, D), F32),
                   jax.ShapeDtypeStruct((1, D), F32),
                   jax.ShapeDtypeStruct((N_GROUPS, SGU_LEN, SGU_LEN), F32),
                   jax.ShapeDtypeStruct((N_GROUPS, SGU_LEN, 1), F32)],
        scratch_shapes=[pltpu.VMEM((tm, W), BF16), pltpu.VMEM((tm, W), F32), pltpu.VMEM((tm, W), F32),
                        pltpu.VMEM((tm, W), F32), pltpu.VMEM((tm, W), F32)],
        compiler_params=_params(("arbitrary",)),
    )(dx2, p, z7, z7, z7, z7, o_a, lng, lnb, ws, bs, w_out, g_post, _after(dep))


def _loss_head(y, target, name):
    T, D = y.shape
    tm = _blk(T, 1024)
    n_i = T // tm

    def body(y_ref, t_ref, dy_ref, loss_ref, acc_scr):
        i = pl.program_id(0)

        @pl.when(i == 0)
        def _():
            acc_scr[...] = jnp.zeros_like(acc_scr)

        e = y_ref[...] - t_ref[...]
        dy_ref[...] = e * np.float32(1.0 / D)
        acc_scr[...] += jnp.sum(e * e, axis=0, keepdims=True)

        @pl.when(i == n_i - 1)
        def _():
            total = jnp.sum(acc_scr[...], axis=-1, keepdims=True) * np.float32(0.5 / D)
            loss_ref[...] = jnp.broadcast_to(total, loss_ref.shape)

    row = pl.BlockSpec((tm, D), lambda i: (i, 0))
    return pl.pallas_call(
        body, name=name, grid=(n_i,),
        in_specs=[row, row],
        out_specs=[row, pl.BlockSpec((1, LANES), lambda i: (0, 0))],
        out_shape=[jax.ShapeDtypeStruct((T, D), F32), jax.ShapeDtypeStruct((1, LANES), F32)],
        scratch_shapes=[pltpu.VMEM((1, D), F32)],
        compiler_params=_params(("arbitrary",)),
    )(y, target)


def _adamw_math(w, g, m, v):
    m_new = ADAM_B1 * m + (1.0 - ADAM_B1) * g
    v_new = ADAM_B2 * v + (1.0 - ADAM_B2) * (g * g)
    m_hat = m_new / np.float32(1.0 - ADAM_B1 ** ADAM_STEP)
    v_hat = v_new / np.float32(1.0 - ADAM_B2 ** ADAM_STEP)
    delta = -ADAM_LR * (m_hat / (jnp.sqrt(v_hat) + ADAM_EPS) + ADAM_WD * w)
    return delta, m_new, v_new


def _sum_adamw(parts, w, m, v, name, dep=None):
    n, R, C = parts.shape
    tr = _blk(R, 512)

    def body(p_ref, w_ref, m_ref, v_ref, _, g_ref, d_ref, mo_ref, vo_ref):
        g = p_ref[0].astype(F32)
        for s in range(1, n):
            g = g + p_ref[s].astype(F32)
        delta, m_new, v_new = _adamw_math(w_ref[...], g, m_ref[...], v_ref[...])
        g_ref[...] = g
        d_ref[...] = delta
        mo_ref[...] = m_new
        vo_ref[...] = v_new

    row = pl.BlockSpec((tr, C), lambda i: (i, 0))
    shp = jax.ShapeDtypeStruct((R, C), F32)
    return pl.pallas_call(
        body, name=name, grid=(R // tr,),
        in_specs=[pl.BlockSpec((n, tr, C), lambda i: (0, i, 0)), row, row, row, ANY],
        out_specs=[row, row, row, row], out_shape=[shp, shp, shp, shp],
        compiler_params=_params(("parallel",)),
    )(parts, w, m, v, _after(dep))


def _position():
    return lax.axis_index("x"), lax.axis_index("y"), lax.axis_index("c")


def _slot(px, py, pc):
    return 4 * px + 2 * py + pc


def _all_gather(shards, name):
    n = len(shards)

    def body(*refs):
        ins, outs = refs[:n], refs[n:2 * n]
        send_sems, recv_sems, local_sems = refs[2 * n:]
        x, y, c = _position()
        me, sibling = (x, y, c), (x, y, 1 - c)
        chips = [(1 - x, y), (x, 1 - y), (1 - x, 1 - y)]

        def copy(a, k, block, to, src=None):
            dst = outs[a].at[_slot(*block)]
            return pltpu.make_async_remote_copy(
                src_ref=dst if src is None else src, dst_ref=dst,
                send_sem=send_sems.at[a, k], recv_sem=recv_sems.at[a, k],
                device_id=to, device_id_type=MESH)

        mine = [pltpu.make_async_copy(ins[a], outs[a].at[_slot(*me)], local_sems.at[a]) for a in range(n)]
        for cp in mine:
            cp.start()
        first = []
        for a in range(n):
            first.append(copy(a, 0, me, sibling, src=ins[a]))
            first += [copy(a, 1 + j, me, (*chip, c), src=ins[a]) for j, chip in enumerate(chips)]
        for cp in first:
            cp.start()
        passed = []
        for j, chip in enumerate(chips):
            for a in range(n):
                copy(a, 1 + j, (*chip, c), me).wait_recv()
                fwd = copy(a, 4 + j, (*chip, c), sibling)
                fwd.start()
                passed.append(fwd)
        for a in range(n):
            copy(a, 0, sibling, me).wait_recv()
            for j, chip in enumerate(chips):
                copy(a, 4 + j, (*chip, 1 - c), me).wait_recv()
        for cp in first + passed:
            cp.wait_send()
        for cp in mine:
            cp.wait()

    return pl.pallas_call(
        body, name=name,
        in_specs=[ANY] * n, out_specs=[ANY] * n,
        out_shape=[jax.ShapeDtypeStruct((N_DEV,) + s.shape, s.dtype) for s in shards],
        scratch_shapes=[pltpu.SemaphoreType.DMA((n, 7)), pltpu.SemaphoreType.DMA((n, 7)),
                        pltpu.SemaphoreType.DMA((n,))],
    )(*shards)


def _peer(x, y, c, k):
    return (1 - x if k & 4 else x, 1 - y if k & 2 else y, 1 - c if k & 1 else c)


def _remote_copies(src_refs, land_refs, send_sems, recv_sems, gather, outgoing):
    x, y, c = _position()
    me = _slot(x, y, c)
    copies = []
    for k in range(1, N_DEV):
        peer = _peer(x, y, c, k)
        for a in range(len(src_refs)):
            src = src_refs[a] if gather else src_refs[a].at[_slot(*peer)]
            dst = land_refs[a].at[me if outgoing else _slot(*peer)]
            sem = a * (N_DEV - 1) + k - 1
            copies.append(pltpu.make_async_remote_copy(
                src_ref=src, dst_ref=dst, send_sem=send_sems.at[sem], recv_sem=recv_sems.at[sem],
                device_id=peer, device_id_type=MESH))
    return copies


def _sequencer_exchange(srcs, name, gather, collective_id):
    n = len(srcs)
    hbm = pltpu.MemorySpace.HBM
    src_refs = [jax.new_ref(s, memory_space=hbm) for s in srcs]
    land_refs = [jax.empty_ref(jax.ShapeDtypeStruct(((N_DEV,) + s.shape) if gather else s.shape, s.dtype),
                               memory_space=hbm) for s in srcs]
    n_sems = n * (N_DEV - 1)
    block_bytes = sum(s.size * s.dtype.itemsize // (1 if gather else N_DEV) for s in srcs)
    cost = pl.CostEstimate(flops=0, transcendentals=0, bytes_accessed=2 * N_DEV * block_bytes,
                           remote_bytes_transferred=(N_DEV - 1) * block_bytes)

    @pl.kernel(mesh=plsc.ScalarSubcoreMesh(axis_name="sequencer", num_cores=1), name=name,
               scratch_types=(pltpu.SemaphoreType.DMA((n_sems,)), pltpu.SemaphoreType.DMA((n_sems,)),
                              pltpu.SemaphoreType.DMA((n,))),
               cost_estimate=cost,
               compiler_params=pltpu.CompilerParams(collective_id=collective_id))
    def launch(send_sems, recv_sems, local_sems):
        x, y, c = _position()
        me = _slot(x, y, c)
        barrier = pltpu.get_barrier_semaphore()
        for k in range(1, N_DEV):
            pl.semaphore_signal(barrier, inc=1, device_id=_peer(x, y, c, k), device_id_type=MESH)
        pl.semaphore_wait(barrier, N_DEV - 1)
        mine = [pltpu.make_async_copy(src_refs[a] if gather else src_refs[a].at[me], land_refs[a].at[me],
                                      local_sems.at[a]) for a in range(n)]
        for cp in mine:
            cp.start()
        sends = _remote_copies(src_refs, land_refs, send_sems, recv_sems, gather, outgoing=True)
        for cp in sends:
            cp.start()
        for cp in _remote_copies(src_refs, land_refs, send_sems, recv_sems, gather, outgoing=False):
            cp.wait_recv()
        for cp in sends:
            cp.wait_send()
        for cp in mine:
            cp.wait()

    launch()
    return [r[...] for r in land_refs]


SMALL_VECS = ("ffn1_pre_g", "ffn1_post_g", "mix_pre_g", "sgu_ln_g", "sgu_ln_b", "mix_post_g", "ffn2_pre_g",
              "ffn2_post_g")
ROW_BS = len(SMALL_VECS)
ROW_BF = ROW_BS + 1
ROW_LOSS = ROW_BF + 1
ROW_WS = 16
BLOB_ROWS = ROW_WS + SGU_LEN


def _pack_small(vals, D, loss_row=None):
    rows = [vals[n].reshape(1, D) for n in SMALL_VECS]
    rows.append(vals["sgu_b_s"].reshape(1, D))
    rows.append(jnp.pad(vals["b_forget"].reshape(1, N_HEADS), ((0, 0), (0, D - N_HEADS))))
    rows.append(jnp.zeros((1, D), F32) if loss_row is None else loss_row)
    rows.append(jnp.zeros((ROW_WS - ROW_LOSS - 1, D), F32))
    rows.append(vals["sgu_w_s"].reshape(SGU_LEN, D))
    return jnp.concatenate(rows, axis=0)


def _unpack_small(blob, D):
    out = {n: blob[r:r + 1] for r, n in enumerate(SMALL_VECS)}
    out["sgu_b_s"] = blob[ROW_BS].reshape(1, N_GROUPS, SGU_LEN)
    out["b_forget"] = blob[ROW_BF, :N_HEADS].reshape(1, N_HEADS)
    out["sgu_w_s"] = blob[ROW_WS:].reshape(1, N_GROUPS, SGU_LEN, SGU_LEN)
    return out


WEIGHT_NAMES = ("ffn1_pre_g", "ffn1_w_gate", "ffn1_w_up", "ffn1_w_down", "ffn1_post_g", "mix_pre_g", "w_in",
                "b_forget", "sgu_ln_g", "sgu_ln_b", "sgu_w_s", "sgu_b_s", "w_out", "mix_post_g", "ffn2_pre_g",
                "ffn2_w_gate", "ffn2_w_up", "ffn2_w_down", "ffn2_post_g")
BIG_NAMES = ("ffn1_w_gate", "ffn1_w_up", "ffn1_w_down", "w_in", "w_out", "ffn2_w_gate", "ffn2_w_up", "ffn2_w_down")
WEIGHT_GROUPS = {"ffn1": ("ffn1_w_gate", "ffn1_w_up", "ffn1_w_down"), "mix": ("w_in", "w_out"),
                 "ffn2": ("ffn2_w_gate", "ffn2_w_up", "ffn2_w_down")}
GRAD_GROUPS = (("ffn2_w_gate", "ffn2_w_up", "ffn2_w_down"), ("w_in", "w_out"), ("ffn1_w_down",), ("ffn1_w_gate",),
               ("ffn1_w_up",))


def _local_step(x, target, small, fetch, emit, consume):
    T, D = x.shape
    W = N_HEADS * HEAD_DIM
    vec = lambda n: small[n].reshape(1, D)
    big = dict(fetch("ffn1", x))

    x1, y1, dgf1, silu1, act1 = _ffn_fwd(x, vec("ffn1_pre_g"), big["ffn1_w_gate"], big["ffn1_w_up"], big["ffn1_w_down"],
                                  vec("ffn1_post_g"), "ffn1_fwd")

    big.update(fetch("mix", x1))
    w_in_all = big["w_in"]
    in_width = N_DEV * w_in_all.shape[2]
    w_in = w_in_all.transpose(1, 0, 2).reshape(D, in_width)
    col_f = 3 * W
    col_u = col_f + N_HEADS
    seg_starts = (0, W, 2 * W, col_u, col_u + W, col_u + 2 * W, col_u + 3 * W)
    w7 = jnp.stack([w_in[:, s:s + W] for s in seg_starts])
    wf = jnp.pad(w_in[:, col_f:col_u], ((0, 0), (0, LANES - N_HEADS)))
    w_out = big["w_out"].reshape(D, D)
    b_pad = jnp.pad(small["b_forget"].reshape(1, N_HEADS), ((0, 0), (0, LANES - N_HEADS)))
    lng, lnb = vec("sgu_ln_g"), vec("sgu_ln_b")
    ws = small["sgu_w_s"].reshape(N_GROUPS, SGU_LEN, SGU_LEN)
    bs = small["sgu_b_s"].reshape(N_GROUPS, SGU_LEN, 1)

    z7, f_logit, h2b = _mix_in_fwd(x1, vec("mix_pre_g"), w7, wf, "mix_in_fwd")
    c_rep = _forget_cumsum(f_logit, b_pad, "forget_cumsum")
    ta, _, n_chunks = _attn_geometry(T)
    vt = z7[2].reshape(n_chunks, ta, N_HEADS, HEAD_DIM).transpose(2, 0, 3, 1)
    o_a, lse_chunks = _attn_fwd_keys_on_rows(z7, vt, c_rep, "attn_fwd")
    x2, p, merged_b = _mix_out_fwd(z7, o_a, x1, lng, lnb, ws, bs, w_out, vec("mix_post_g"), "mix_out_fwd")
    big.update(fetch("ffn2", x2))
    x3, y2, dgf2, silu2, act2 = _ffn_fwd(x2, vec("ffn2_pre_g"), big["ffn2_w_gate"], big["ffn2_w_up"], big["ffn2_w_down"],
                                  vec("ffn2_post_g"), "ffn2_fwd")
    dy, loss_lanes = _loss_head(x3, target, "loss_head")

    grads_small = {}

    dx2, h3b, dy2b, dgate2, dup2, dgpre, dgpost = _ffn_bwd(
        dy, x2, y2, dgf2, silu2, vec("ffn2_pre_g"), big["ffn2_w_gate"], big["ffn2_w_up"], big["ffn2_w_down"],
        vec("ffn2_post_g"), "ffn2_bwd")
    grads_small["ffn2_pre_g"] = jnp.sum(dgpre, axis=0)
    grads_small["ffn2_post_g"] = jnp.sum(dgpost, axis=0)
    dep = emit("ffn2_w_gate", _wgrad(h3b, dgate2, "ffn2_wgrad_gate", shard_cols=True))
    dep = emit("ffn2_w_up", _wgrad(h3b, dup2, "ffn2_wgrad_up", shard_cols=True, dep=dep))
    dep = emit("ffn2_w_down", _wgrad(act2, dy2b, "ffn2_wgrad_down", dep=dep).reshape(big["ffn2_w_down"].shape))

    dpb, dob, dvec, dz4, dgp, dlng, dlnb, dws, dbs = _mix_out_bwd(
        dx2, p, z7, o_a, lng, lnb, ws, bs, w_out, vec("mix_post_g"), "mix_out_bwd", dep=dep)
    grads_small["mix_post_g"] = dgp
    grads_small["sgu_ln_g"] = dlng
    grads_small["sgu_ln_b"] = dlnb
    grads_small["sgu_w_s"] = dws
    grads_small["sgu_b_s"] = dbs
    d_chunks = dvec.reshape(N_HEADS, n_chunks, 1, ta)
    kt = z7[1].reshape(n_chunks, ta, N_HEADS, HEAD_DIM).transpose(2, 0, 3, 1)
    dk, dv, dc, dq, dc_q = _attn_bwd_fused(z7, kt, dob, c_rep, lse_chunks, d_chunks, "attn_bwd")
    dc_pad = jnp.pad((dc + dc_q).reshape(N_HEADS, T).T, ((0, 0), (0, LANES - N_HEADS)))
    dfb, dbf = _forget_bwd(dc_pad, f_logit, b_pad, "forget_bwd")
    grads_small["b_forget"] = dbf[:, :N_HEADS]
    segs = [(dq, None), (dk, None), (dv, None), (dz4, 0), (dz4, 1), (dz4, 2), (dz4, 3)]
    dep = consume(("ffn2_w_gate", "ffn2_w_up", "ffn2_w_down"))
    dx1, dgm = _mix_in_bwd(dx2, x1, vec("mix_pre_g"), segs, dfb, w7, wf, "mix_in_bwd", dep=dep)
    grads_small["mix_pre_g"] = jnp.sum(dgm, axis=0)
    dw_qkv = _wgrad_multi(h2b, segs[:3], "w_in_wgrad_qkv", dep=dx1)
    dw_rest = _wgrad_multi(h2b, segs[3:], "w_in_wgrad_gates", dep=dw_qkv)
    dw_seg = [dw_qkv[:, q * W:(q + 1) * W] for q in range(3)] + [dw_rest[:, q * W:(q + 1) * W] for q in range(4)]
    dwf = _wgrad(h2b, dfb, "w_in_wgrad_f", dep=dw_rest)
    dw_in = jnp.concatenate(dw_seg[:3] + [dwf[:, :N_HEADS]] + dw_seg[3:], axis=1)
    emit("w_in", dw_in.reshape(D, N_DEV, in_width // N_DEV).transpose(1, 0, 2))
    dep = emit("w_out", _wgrad(merged_b, dpb, "w_out_wgrad", dep=dwf).reshape(big["w_out"].shape))

    dx0, h1b, dy1b, dgate1, dup1, dgpre1, dgpost1 = _ffn_bwd(
        dx1, x, y1, dgf1, silu1, vec("ffn1_pre_g"), big["ffn1_w_gate"], big["ffn1_w_up"], big["ffn1_w_down"],
        vec("ffn1_post_g"), "ffn1_bwd", dep=dep)
    grads_small["ffn1_pre_g"] = jnp.sum(dgpre1, axis=0)
    grads_small["ffn1_post_g"] = jnp.sum(dgpost1, axis=0)
    dep = consume(("w_in", "w_out"))
    dep = emit("ffn1_w_down", _wgrad(act1, dy1b, "ffn1_wgrad_down", dep=dep).reshape(big["ffn1_w_down"].shape))
    dep = emit("ffn1_w_gate", _wgrad(h1b, dgate1, "ffn1_wgrad_gate", shard_cols=True, dep=dep))
    dep = emit("ffn1_w_up", _wgrad(h1b, dup1, "ffn1_wgrad_up", shard_cols=True, dep=dep))

    loss_row = jnp.pad(loss_lanes, ((0, 0), (0, D - LANES)))
    return loss_row, dx0, grads_small


def kernel(x, ffn1_pre_g, ffn1_w_gate, ffn1_w_up, ffn1_w_down, ffn1_post_g, mix_pre_g, w_in, b_forget, sgu_ln_g, sgu_ln_b, sgu_w_s, sgu_b_s, w_out, mix_post_g, ffn2_pre_g, ffn2_w_gate, ffn2_w_up, ffn2_w_down, ffn2_post_g, loss_target, m_ffn1_pre_g, m_ffn1_w_gate, m_ffn1_w_up, m_ffn1_w_down, m_ffn1_post_g, m_mix_pre_g, m_w_in, m_b_forget, m_sgu_ln_g, m_sgu_ln_b, m_sgu_w_s, m_sgu_b_s, m_w_out, m_mix_post_g, m_ffn2_pre_g, m_ffn2_w_gate, m_ffn2_w_up, m_ffn2_w_down, m_ffn2_post_g, v_ffn1_pre_g, v_ffn1_w_gate, v_ffn1_w_up, v_ffn1_w_down, v_ffn1_post_g, v_mix_pre_g, v_w_in, v_b_forget, v_sgu_ln_g, v_sgu_ln_b, v_sgu_w_s, v_sgu_b_s, v_w_out, v_mix_post_g, v_ffn2_pre_g, v_ffn2_w_gate, v_ffn2_w_up, v_ffn2_w_down, v_ffn2_post_g):
    weights = dict(zip(WEIGHT_NAMES, (ffn1_pre_g, ffn1_w_gate, ffn1_w_up, ffn1_w_down, ffn1_post_g, mix_pre_g, w_in,
                                      b_forget, sgu_ln_g, sgu_ln_b, sgu_w_s, sgu_b_s, w_out, mix_post_g, ffn2_pre_g,
                                      ffn2_w_gate, ffn2_w_up, ffn2_w_down, ffn2_post_g)))
    mom1 = dict(zip(WEIGHT_NAMES, (m_ffn1_pre_g, m_ffn1_w_gate, m_ffn1_w_up, m_ffn1_w_down, m_ffn1_post_g,
                                   m_mix_pre_g, m_w_in, m_b_forget, m_sgu_ln_g, m_sgu_ln_b, m_sgu_w_s, m_sgu_b_s,
                                   m_w_out, m_mix_post_g, m_ffn2_pre_g, m_ffn2_w_gate, m_ffn2_w_up, m_ffn2_w_down,
                                   m_ffn2_post_g)))
    mom2 = dict(zip(WEIGHT_NAMES, (v_ffn1_pre_g, v_ffn1_w_gate, v_ffn1_w_up, v_ffn1_w_down, v_ffn1_post_g,
                                   v_mix_pre_g, v_w_in, v_b_forget, v_sgu_ln_g, v_sgu_ln_b, v_sgu_w_s, v_sgu_b_s,
                                   v_w_out, v_mix_post_g, v_ffn2_pre_g, v_ffn2_w_gate, v_ffn2_w_up, v_ffn2_w_down,
                                   v_ffn2_post_g)))
    D = x.shape[-1]
    small_names = [n for n in WEIGHT_NAMES if n not in BIG_NAMES]

    small = {n: weights[n] for n in small_names}
    shard = lambda n: weights[n][0].astype(BF16)

    ffn1_full = _all_gather([shard(n) for n in WEIGHT_GROUPS["ffn1"]], "ffn1_all_gather")
    gathered = {}
    for cid, grp in ((1, "mix"), (2, "ffn2")):
        shards, _ = lax.optimization_barrier(([shard(n) for n in WEIGHT_GROUPS[grp]], ffn1_full[0]))
        gathered[grp] = _sequencer_exchange(shards, grp + "_gather", True, cid)

    def fetch(group, after):
        if group == "ffn1":
            return zip(WEIGHT_GROUPS[group], ffn1_full)
        arrived, _ = lax.optimization_barrier((gathered[group], after))
        return zip(WEIGHT_GROUPS[group], arrived)

    ready, received = {}, {}

    def emit(name, part):
        ready[name] = part
        for gi, group in enumerate(GRAD_GROUPS):
            if name == group[-1]:
                lands = _sequencer_exchange([ready[n] for n in group], name + "_grad_exchange", False, 3 + gi)
                received.update(zip(group, lands))
        return part

    out = {}

    def consume(names, dep=None):
        for n in names:
            g, d, m_new, v_new = _sum_adamw(received[n], weights[n][0], mom1[n][0], mom2[n][0], "adamw_" + n, dep=dep)
            out[n] = tuple(a[None] for a in (g, d, m_new, v_new))
            dep = g
        return dep

    loss_row, grad_x, grads_small = _local_step(x[0], loss_target[0], small, fetch, emit, consume)

    blobs = _sequencer_exchange([_pack_small(grads_small, D, loss_row)], "small_gather", True,
                                3 + len(GRAD_GROUPS))[0]
    blob, d_blob, m_blob, v_blob = _sum_adamw(
        blobs, _pack_small(small, D), _pack_small({n: mom1[n] for n in small_names}, D),
        _pack_small({n: mom2[n] for n in small_names}, D), "adamw_small")
    consume(("ffn1_w_down", "ffn1_w_gate", "ffn1_w_up"), dep=blob)
    unpacked = [_unpack_small(b, D) for b in (blob, d_blob, m_blob, v_blob)]
    for n in small_names:
        out[n] = tuple(u[n].reshape(weights[n].shape) for u in unpacked)

    loss = blob[ROW_LOSS, 0]
    result = [loss, grad_x[None]]
    for k in range(4):
        result += [out[n][k] for n in WEIGHT_NAMES]
    return tuple(result)
```

```python
import numpy as np
import jax
import jax.numpy as jnp
from jax import lax
from jax.experimental import pallas as pl
from jax.experimental.pallas import tpu as pltpu
from jax.experimental.pallas import tpu_sc as plsc

F32 = jnp.float32
BF16 = jnp.bfloat16

RMS_EPS = 1e-6
LN_EPS = 1e-5
HEAD_DIM = 128
N_HEADS = 8
GROUP_DIM = 128
N_GROUPS = 8
SGU_LEN = 128
CHUNK = 64
N_DEV = 8
LANES = 128
VMEM_LIMIT = 56 * 1024 * 1024
NEG_BIG = -1e30
LOG2E = np.float32(1.0 / np.log(2.0))
Q_PRESCALE = np.float32(LOG2E / np.sqrt(HEAD_DIM))

ADAM_LR = 0.001
ADAM_B1 = 0.9
ADAM_B2 = 0.999
ADAM_EPS = 1e-08
ADAM_WD = 0.01
ADAM_STEP = 10

MESH = pl.DeviceIdType.MESH
ANY = pl.BlockSpec(memory_space=pl.ANY)


def _blk(n, pref):
    return pref if (n >= pref and n % pref == 0) else n


def _mm(a, b):
    return jnp.dot(a, b, preferred_element_type=F32)


def _mm_nt(a, b):
    return lax.dot_general(a, b, (((1,), (1,)), ((), ())), preferred_element_type=F32)


def _mm_tn(a, b):
    return lax.dot_general(a, b, (((0,), (0,)), ((), ())), preferred_element_type=F32)


def _params(sem):
    return pltpu.CompilerParams(dimension_semantics=sem, vmem_limit_bytes=VMEM_LIMIT)


def _gelu(x):
    return 0.5 * x * (1.0 + lax.erf(x * np.float32(1.0 / np.sqrt(2.0))))


def _gelu_grad(x):
    cdf = 0.5 * (1.0 + lax.erf(x * np.float32(1.0 / np.sqrt(2.0))))
    return cdf + x * jnp.exp(-0.5 * x * x) * np.float32(1.0 / np.sqrt(2.0 * np.pi))


def _rms_scale(v):
    return lax.rsqrt(jnp.mean(v * v, axis=-1, keepdims=True) + RMS_EPS)


def _rms_bwd(dy, xhat, r, g):
    dxh = dy * g
    return r * (dxh - xhat * jnp.mean(dxh * xhat, axis=-1, keepdims=True))


def _ffn_fwd(x, g_pre, wg, wu, wd, g_post, name):
    T, D = x.shape
    ns, _, fs = wg.shape
    tm = _blk(T, 256)

    def body(x_ref, gpre_ref, wg_ref, wu_ref, wd_ref, gpost_ref, xo_ref, y_ref, dgf_ref, silu_ref, act_ref):
        xv = x_ref[...]
        h = (xv * _rms_scale(xv) * gpre_ref[...]).astype(BF16)
        y = jnp.zeros((tm, D), F32)
        pre = (_mm(h, wg_ref[0]), _mm(h, wu_ref[0]))
        for j in range(ns):
            gg, uu = pre
            if j + 1 < ns:
                pre = (_mm(h, wg_ref[j + 1]), _mm(h, wu_ref[j + 1]))
            cols = slice(j * fs, (j + 1) * fs)
            sg = jax.nn.sigmoid(gg)
            silu = gg * sg
            act = (silu * uu).astype(BF16)
            dgf_ref[:, cols] = (uu * (sg * (1.0 + gg * (1.0 - sg)))).astype(BF16)
            silu_ref[:, cols] = silu.astype(BF16)
            act_ref[:, cols] = act
            y = y + _mm(act, wd_ref[j])
        y_ref[...] = y
        xo_ref[...] = xv + 0.5 * (y * _rms_scale(y) * gpost_ref[...])

    row = pl.BlockSpec((tm, D), lambda i: (i, 0))
    vec = pl.BlockSpec((1, D), lambda i: (0, 0))
    wide = pl.BlockSpec((tm, ns * fs), lambda i: (i, 0))
    return pl.pallas_call(
        body, name=name, grid=(T // tm,),
        in_specs=[row, vec,
                  pl.BlockSpec((ns, D, fs), lambda i: (0, 0, 0), pipeline_mode=pl.Buffered(1)),
                  pl.BlockSpec((ns, D, fs), lambda i: (0, 0, 0), pipeline_mode=pl.Buffered(1)),
                  pl.BlockSpec((ns, fs, D), lambda i: (0, 0, 0), pipeline_mode=pl.Buffered(1)),
                  vec],
        out_specs=[row, row, wide, wide, wide],
        out_shape=[jax.ShapeDtypeStruct((T, D), F32), jax.ShapeDtypeStruct((T, D), F32)]
        + [jax.ShapeDtypeStruct((T, ns * fs), BF16)] * 3,
        compiler_params=_params(("parallel",)),
    )(x, g_pre, wg, wu, wd, g_post)


def _after(dep):
    return jnp.zeros((8, LANES), F32) if dep is None else dep


def _ffn_bwd(dxo, x, y, dgf, silu, g_pre, wg, wu, wd, g_post, name, dep=None):
    T, D = x.shape
    ns, _, fs = wg.shape
    tm = _blk(T, 256)
    n_i = T // tm

    def body(dxo_ref, x_ref, y_ref, dgf_ref, silu_ref, gpre_ref, wg_ref, wu_ref, wd_ref, gpost_ref, _,
             dx_ref, hb_ref, dyb_ref, dgb_ref, dub_ref, dgpre_ref, dgpost_ref):
        yv = y_ref[...]
        s = _rms_scale(yv)
        n = yv * s
        dxo = dxo_ref[...]
        dn = 0.5 * dxo
        dgpost_ref[...] = jnp.sum(dn * n, axis=0, keepdims=True)
        dyv = _rms_bwd(dn, n, s, gpost_ref[...]).astype(BF16)
        dyb_ref[...] = dyv
        xv = x_ref[...]
        rs = _rms_scale(xv)
        xhat = xv * rs
        hb_ref[...] = (xhat * gpre_ref[...]).astype(BF16)

        dh = jnp.zeros((tm, D), F32)
        da = _mm_nt(dyv, wd_ref[0])
        for j in range(ns):
            cur = da
            if j + 1 < ns:
                da = _mm_nt(dyv, wd_ref[j + 1])
            cols = slice(j * fs, (j + 1) * fs)
            dgate = (cur * dgf_ref[:, cols].astype(F32)).astype(BF16)
            dup = (cur * silu_ref[:, cols].astype(F32)).astype(BF16)
            dgb_ref[:, cols] = dgate
            dub_ref[:, cols] = dup
            dh = dh + _mm_nt(dgate, wg_ref[j]) + _mm_nt(dup, wu_ref[j])

        dgpre_ref[...] = jnp.sum(dh * xhat, axis=0, keepdims=True)
        dx_ref[...] = _rms_bwd(dh, xhat, rs, gpre_ref[...]) + dxo

    F = ns * fs
    row = pl.BlockSpec((tm, D), lambda i: (i, 0))
    vec = pl.BlockSpec((1, D), lambda i: (0, 0))
    wide = pl.BlockSpec((tm, F), lambda i: (i, 0))
    part = pl.BlockSpec((None, 1, D), lambda i: (i, 0, 0))
    return pl.pallas_call(
        body, name=name, grid=(n_i,),
        in_specs=[row, row, row, wide, wide, vec,
                  pl.BlockSpec((ns, D, fs), lambda i: (0, 0, 0), pipeline_mode=pl.Buffered(1)),
                  pl.BlockSpec((ns, D, fs), lambda i: (0, 0, 0), pipeline_mode=pl.Buffered(1)),
                  pl.BlockSpec((ns, fs, D), lambda i: (0, 0, 0), pipeline_mode=pl.Buffered(1)),
                  vec, ANY],
        out_specs=[row, row, row, wide, wide, part, part],
        out_shape=[jax.ShapeDtypeStruct((T, D), F32), jax.ShapeDtypeStruct((T, D), BF16),
                   jax.ShapeDtypeStruct((T, D), BF16), jax.ShapeDtypeStruct((T, F), BF16),
                   jax.ShapeDtypeStruct((T, F), BF16),
                   jax.ShapeDtypeStruct((n_i, 1, D), F32), jax.ShapeDtypeStruct((n_i, 1, D), F32)],
        compiler_params=_params(("parallel",)),
    )(dxo, x, y, dgf, silu, g_pre, wg, wu, wd, g_post, _after(dep))


def _wgrad(xm, ym, name, shard_cols=False, dep=None):
    T, M = xm.shape
    N = ym.shape[-1]
    assert M * N * 4 <= 16 * 1024 * 1024, (M, N)
    tk = _blk(T, 1024)
    n_k = T // tk
    fs = N // N_DEV
    cw = _blk(max(M, N), 512)

    def body(x_ref, y_ref, _, o_ref, acc_scr):
        k = pl.program_id(0)

        @pl.when(k == 0)
        def _():
            acc_scr[...] = jnp.zeros_like(acc_scr)

        if N >= M:
            x = x_ref[...]
            for c in range(N // cw):
                acc_scr[:, c * cw:(c + 1) * cw] += _mm_tn(x, y_ref[:, c * cw:(c + 1) * cw])
        else:
            y = y_ref[...]
            for c in range(M // cw):
                acc_scr[c * cw:(c + 1) * cw, :] += _mm_tn(x_ref[:, c * cw:(c + 1) * cw], y)

        @pl.when(k == n_k - 1)
        def _():
            if shard_cols:
                for s in range(N_DEV):
                    o_ref[s] = acc_scr[:, s * fs:(s + 1) * fs].astype(BF16)
            else:
                o_ref[...] = acc_scr[...].astype(BF16)

    if shard_cols:
        out_spec = pl.BlockSpec((N_DEV, M, fs), lambda k: (0, 0, 0), pipeline_mode=pl.Buffered(1))
        out_shape = jax.ShapeDtypeStruct((N_DEV, M, fs), BF16)
    else:
        out_spec = pl.BlockSpec((M, N), lambda k: (0, 0), pipeline_mode=pl.Buffered(1))
        out_shape = jax.ShapeDtypeStruct((M, N), BF16)
    return pl.pallas_call(
        body, name=name, grid=(n_k,),
        in_specs=[pl.BlockSpec((tk, M), lambda k: (k, 0)), pl.BlockSpec((tk, N), lambda k: (k, 0)), ANY],
        out_specs=out_spec, out_shape=out_shape,
        scratch_shapes=[pltpu.VMEM((M, N), F32)],
        compiler_params=_params(("arbitrary",)),
    )(xm, ym, _after(dep))


def _wgrad_multi(xm, segs, name, dep=None):
    T, M = xm.shape
    N = segs[0][0].shape[-1]
    n_seg = len(segs)
    assert M * N * n_seg * 4 <= 16 * 1024 * 1024, (M, N, n_seg)
    tk = _blk(T, 512)
    n_k = T // tk

    def body(*refs):
        x_ref, y_refs = refs[0], refs[1:1 + n_seg]
        o_ref, acc_scr = refs[2 + n_seg], refs[3 + n_seg]
        k = pl.program_id(0)

        @pl.when(k == 0)
        def _():
            acc_scr[...] = jnp.zeros_like(acc_scr)

        x = x_ref[...]
        for s in range(n_seg):
            acc_scr[:, s * N:(s + 1) * N] += _mm_tn(x, y_refs[s][...])

        @pl.when(k == n_k - 1)
        def _():
            o_ref[...] = acc_scr[...].astype(BF16)

    y_specs = [pl.BlockSpec((tk, N), lambda k: (k, 0)) if idx is None
               else pl.BlockSpec((None, tk, N), lambda k, idx=idx: (idx, k, 0)) for _, idx in segs]
    return pl.pallas_call(
        body, name=name, grid=(n_k,),
        in_specs=[pl.BlockSpec((tk, M), lambda k: (k, 0))] + y_specs + [ANY],
        out_specs=pl.BlockSpec((M, n_seg * N), lambda k: (0, 0), pipeline_mode=pl.Buffered(1)),
        out_shape=jax.ShapeDtypeStruct((M, n_seg * N), BF16),
        scratch_shapes=[pltpu.VMEM((M, n_seg * N), F32)],
        compiler_params=_params(("arbitrary",)),
    )(xm, *[arr for arr, _ in segs], _after(dep))


def _mix_in_fwd(x1, g, w7, wf, name):
    T, D = x1.shape
    n_seg, _, W = w7.shape
    tm = _blk(T, 512)

    def body(x_ref, g_ref, w_ref, wf_ref, z_ref, f_ref, hb_ref):
        xv = x_ref[...]
        h = (xv * _rms_scale(xv) * g_ref[...]).astype(BF16)
        hb_ref[...] = h
        f_ref[...] = _mm(h, wf_ref[...])
        for s in range(n_seg):
            z = _mm(h, w_ref[s])
            z_ref[s] = (z * Q_PRESCALE if s == 0 else z).astype(BF16)

    return pl.pallas_call(
        body, name=name, grid=(T // tm,),
        in_specs=[pl.BlockSpec((tm, D), lambda i: (i, 0)),
                  pl.BlockSpec((1, D), lambda i: (0, 0)),
                  pl.BlockSpec((n_seg, D, W), lambda i: (0, 0, 0), pipeline_mode=pl.Buffered(1)),
                  pl.BlockSpec((D, LANES), lambda i: (0, 0))],
        out_specs=[pl.BlockSpec((n_seg, tm, W), lambda i: (0, i, 0)),
                   pl.BlockSpec((tm, LANES), lambda i: (i, 0)),
                   pl.BlockSpec((tm, D), lambda i: (i, 0))],
        out_shape=[jax.ShapeDtypeStruct((n_seg, T, W), BF16), jax.ShapeDtypeStruct((T, LANES), F32),
                   jax.ShapeDtypeStruct((T, D), BF16)],
        compiler_params=_params(("parallel",)),
    )(x1, g, w7, wf)


def _mix_in_bwd(dx2, x1, g, segs, dfb, w7, wf, name, dep=None):
    T, D = x1.shape
    n_seg, _, W = w7.shape
    tm = _blk(T, 512)
    n_i = T // tm

    def body(*refs):
        dx2_ref, x_ref, g_ref = refs[:3]
        seg_refs = refs[3:3 + n_seg]
        df_ref, w_ref, wf_ref, _, dx1_ref, dg_ref = refs[3 + n_seg:]
        dh = _mm_nt(df_ref[...], wf_ref[...])
        for q in range(n_seg):
            dh = dh + _mm_nt(seg_refs[q][...], w_ref[q])
        xv = x_ref[...]
        r = _rms_scale(xv)
        xhat = xv * r
        dg_ref[...] = jnp.sum(dh * xhat, axis=0, keepdims=True)
        dx1_ref[...] = _rms_bwd(dh, xhat, r, g_ref[...]) + dx2_ref[...]

    row = pl.BlockSpec((tm, D), lambda i: (i, 0))
    seg_specs = []
    seg_args = []
    for arr, idx in segs:
        if idx is None:
            seg_specs.append(pl.BlockSpec((tm, W), lambda i: (i, 0)))
        else:
            seg_specs.append(pl.BlockSpec((None, tm, W), lambda i, idx=idx: (idx, i, 0)))
        seg_args.append(arr)
    return pl.pallas_call(
        body, name=name, grid=(n_i,),
        in_specs=[row, row, pl.BlockSpec((1, D), lambda i: (0, 0))] + seg_specs + [
            pl.BlockSpec((tm, LANES), lambda i: (i, 0)),
            pl.BlockSpec((n_seg, D, W), lambda i: (0, 0, 0), pipeline_mode=pl.Buffered(1)),
            pl.BlockSpec((D, LANES), lambda i: (0, 0)), ANY],
        out_specs=[row, pl.BlockSpec((None, 1, D), lambda i: (i, 0, 0))],
        out_shape=[jax.ShapeDtypeStruct((T, D), F32), jax.ShapeDtypeStruct((n_i, 1, D), F32)],
        compiler_params=_params(("parallel",)),
    )(dx2, x1, g, *seg_args, dfb, w7, wf, _after(dep))


def _forget_cumsum(f, b_pad, name):
    T, L = f.shape
    tb = _blk(T, 256)

    def body(f_ref, b_ref, c_ref, carry):
        @pl.when(pl.program_id(0) == 0)
        def _():
            carry[...] = jnp.zeros_like(carry)

        lf = jax.nn.log_sigmoid(f_ref[...] + b_ref[...])
        rows = lax.broadcasted_iota(jnp.int32, (tb, tb), 0)
        cols = lax.broadcasted_iota(jnp.int32, (tb, tb), 1)
        tri = (cols <= rows).astype(F32)
        c = jnp.dot(tri, lf, preferred_element_type=F32, precision=lax.Precision.HIGHEST) + carry[...]
        carry[...] = c[tb - 1:tb, :]
        for h in range(N_HEADS):
            c_ref[h] = jnp.broadcast_to(c[:, h:h + 1] * LOG2E, (tb, L))

    return pl.pallas_call(
        body, name=name, grid=(T // tb,),
        in_specs=[pl.BlockSpec((tb, L), lambda i: (i, 0)), pl.BlockSpec((1, L), lambda i: (0, 0))],
        out_specs=pl.BlockSpec((N_HEADS, tb, L), lambda i: (0, i, 0)),
        out_shape=jax.ShapeDtypeStruct((N_HEADS, T, L), F32),
        scratch_shapes=[pltpu.VMEM((1, L), F32)],
        compiler_params=_params(("arbitrary",)),
    )(f, b_pad)


def _forget_bwd(dc, f, b_pad, name):
    T, L = f.shape
    tb = _blk(T, 256)
    nb = T // tb

    def body(dc_ref, f_ref, b_ref, df_ref, db_ref, carry):
        @pl.when(pl.program_id(0) == 0)
        def _():
            carry[...] = jnp.zeros_like(carry)
            db_ref[...] = jnp.zeros_like(db_ref)

        rows = lax.broadcasted_iota(jnp.int32, (tb, tb), 0)
        cols = lax.broadcasted_iota(jnp.int32, (tb, tb), 1)
        tri = (cols >= rows).astype(F32)
        r = jnp.dot(tri, dc_ref[...], preferred_element_type=F32, precision=lax.Precision.HIGHEST) + carry[...]
        carry[...] = r[0:1, :]
        df = r * (1.0 - jax.nn.sigmoid(f_ref[...] + b_ref[...]))
        df_ref[...] = df.astype(BF16)
        db_ref[...] += jnp.sum(df, axis=0, keepdims=True)

    rev = pl.BlockSpec((tb, L), lambda i: (nb - 1 - i, 0))
    one = pl.BlockSpec((1, L), lambda i: (0, 0))
    return pl.pallas_call(
        body, name=name, grid=(nb,),
        in_specs=[rev, rev, one], out_specs=[rev, one],
        out_shape=[jax.ShapeDtypeStruct((T, L), BF16), jax.ShapeDtypeStruct((1, L), F32)],
        scratch_shapes=[pltpu.VMEM((1, L), F32)],
        compiler_params=_params(("arbitrary",)),
    )(dc, f, b_pad)


ATTN_TILE = 512
ATTN_CHAINS = 4


def _attn_geometry(T):
    ta = _blk(T, ATTN_TILE)
    nc = ATTN_CHAINS if (T // ta) % ATTN_CHAINS == 0 else 1
    return ta, nc, T // ta


def _causal_tile(ta, keys_on_rows=False):
    rows = lax.broadcasted_iota(jnp.int32, (ta, ta), 0)
    cols = lax.broadcasted_iota(jnp.int32, (ta, ta), 1)
    return rows <= cols if keys_on_rows else cols <= rows


def _chunk(ref, j, ta):
    return ref[pl.ds(pl.multiple_of(j * ta, ta), ta), :]


def _attn_fwd_keys_on_rows(z7, vt, c_rep, name):
    _, T, W = z7.shape
    H = W // HEAD_DIM
    ta, nc, n_chunks = _attn_geometry(T)
    scale = np.float32(1.0 / np.sqrt(HEAD_DIM))
    reps = ta // LANES

    def body(q_ref, k_ref, vt_ref, c_ref, o_ref, lse_ref):
        g = pl.program_id(1)

        def scores(ch, k):
            return _mm_nt(k, q_ref[ch * ta:(ch + 1) * ta, :])

        def update(state, raw, vt, cj, diagonal):
            m_prev, l_prev, acc_prev = state
            st = jnp.concatenate([raw[:, b * LANES:(b + 1) * LANES] - cj for b in range(reps)], axis=1)
            if diagonal:
                st = jnp.where(_causal_tile(ta, keys_on_rows=True), st, NEG_BIG)
            m_new = jnp.maximum(m_prev, jnp.max(st, axis=0, keepdims=True))
            alpha = jnp.exp2(m_prev - m_new)
            pt = jnp.exp2(st - m_new)
            l_new = alpha * l_prev + jnp.sum(pt, axis=0, keepdims=True)
            acc_new = alpha * acc_prev + _mm(vt, pt.astype(BF16))
            return m_new, l_new, acc_new

        def load(j):
            cj = _chunk(c_ref, j, ta)
            return _chunk(k_ref, j, ta), vt_ref[j], cj

        def full_chunk(j, states):
            k, vt, cj = load(j)
            raws = [scores(ch, k) for ch in range(nc)]
            return tuple(update(states[ch], raws[ch], vt, cj, False) for ch in range(nc))

        first = (jnp.full((1, ta), NEG_BIG, F32), jnp.zeros((1, ta), F32), jnp.zeros((HEAD_DIM, ta), F32))
        states = list(lax.fori_loop(0, nc * g, full_chunk, (first,) * nc))
        for jj in range(nc):
            k, vt, cj = load(nc * g + jj)
            raws = {ch: scores(ch, k) for ch in range(jj, nc)}
            for ch in range(jj, nc):
                states[ch] = update(states[ch], raws[ch], vt, cj, ch == jj)
        for ch in range(nc):
            m, l, acc = states[ch]
            o_ref[ch * ta:(ch + 1) * ta, :] = (acc / l).T
            lse_ref[ch] = m + jnp.log2(l)

    tq = nc * ta
    return pl.pallas_call(
        body, name=name, grid=(H, n_chunks // nc),
        in_specs=[pl.BlockSpec((None, tq, HEAD_DIM), lambda h, g: (0, g, h)),
                  pl.BlockSpec((None, T, HEAD_DIM), lambda h, g: (1, 0, h)),
                  pl.BlockSpec((None, n_chunks, HEAD_DIM, ta), lambda h, g: (h, 0, 0, 0)),
                  pl.BlockSpec((None, T, LANES), lambda h, g: (h, 0, 0))],
        out_specs=[pl.BlockSpec((tq, HEAD_DIM), lambda h, g: (g, h)),
                   pl.BlockSpec((None, nc, 1, ta), lambda h, g: (h, g, 0, 0))],
        out_shape=[jax.ShapeDtypeStruct((T, W), F32), jax.ShapeDtypeStruct((H, n_chunks, 1, ta), F32)],
        compiler_params=_params(("parallel", "arbitrary")),
    )(z7, z7, vt, c_rep)


def _attn_bwd_fused(z7, kt, dob, c_rep, lse_chunks, d_chunks, name):
    _, T, W = z7.shape
    H = W // HEAD_DIM
    ta, nc, n_chunks = _attn_geometry(T)
    n_steps = n_chunks // nc
    scale = np.float32(1.0 / np.sqrt(HEAD_DIM))
    reps = ta // LANES

    def body(k_ref, v_ref, kt_ref, q_ref, do_ref, c_ref, lse_ref, d_ref,
             dk_ref, dv_ref, dck_ref, dq_ref, dcq_ref, dk_scr, dv_scr, dck_scr, dqt_scr, dcq_scr):
        g = pl.program_id(1)

        @pl.when(g == 0)
        def _():
            dqt_scr[...] = jnp.zeros_like(dqt_scr)
            dcq_scr[...] = jnp.zeros_like(dcq_scr)

        dk_scr[...] = jnp.zeros_like(dk_scr)
        dv_scr[...] = jnp.zeros_like(dv_scr)
        dck_scr[...] = jnp.zeros_like(dck_scr)

        def products(ch, q, do):
            rows = slice(ch * ta, (ch + 1) * ta)
            return _mm_nt(k_ref[rows, :], q), _mm_nt(v_ref[rows, :], do)

        def update(ch, i, q, do, prods, diagonal):
            rows = slice(ch * ta, (ch + 1) * ta)
            cj = c_ref[rows, :]
            raw = prods[0]
            st = jnp.concatenate([raw[:, b * LANES:(b + 1) * LANES] - cj for b in range(reps)], axis=1) - lse_ref[i]
            if diagonal:
                st = jnp.where(_causal_tile(ta, keys_on_rows=True), st, NEG_BIG)
            pt = jnp.exp2(st)
            dv_scr[ch] += _mm(pt.astype(BF16), do)
            dst = pt * (prods[1] - d_ref[i])
            dst_b = dst.astype(BF16)
            dk_scr[ch] += _mm(dst_b, q)
            dqt_scr[i] += _mm(kt_ref[ch], dst_b)
            dcq_scr[i] += jnp.sum(dst, axis=0, keepdims=True)
            lane_sum = dst[:, :LANES]
            for r in range(1, reps):
                lane_sum = lane_sum + dst[:, r * LANES:(r + 1) * LANES]
            dck_scr[ch] += lane_sum

        for ii in range(nc):
            i = nc * g + ii
            q = _chunk(q_ref, i, ta)
            do = _chunk(do_ref, i, ta)
            prods = [products(ch, q, do) for ch in range(0, ii + 1)]
            for ch in range(0, ii + 1):
                update(ch, i, q, do, prods[ch], ch == ii)

        def full_chunk(i, carry):
            q = _chunk(q_ref, i, ta)
            do = _chunk(do_ref, i, ta)
            prods = [products(ch, q, do) for ch in range(nc)]
            for ch in range(nc):
                update(ch, i, q, do, prods[ch], False)
            return carry

        lax.fori_loop(nc * (g + 1), n_chunks, full_chunk, 0)
        for ch in range(nc):
            rows = slice(ch * ta, (ch + 1) * ta)
            dk_ref[rows, :] = (dk_scr[ch] * np.float32(1.0 / LOG2E)).astype(BF16)
            dv_ref[rows, :] = dv_scr[ch].astype(BF16)
            ones = jnp.ones((8, LANES), F32)
            sums = lax.dot_general(ones, dck_scr[ch], (((1,), (1,)), ((), ())), preferred_element_type=F32,
                                   precision=lax.Precision.HIGHEST)
            dck_ref[ch] = -sums[0:1, :]

        @pl.when(g == n_steps - 1)
        def _():
            for i in range(n_chunks):
                dq_ref[i * ta:(i + 1) * ta, :] = (dqt_scr[i] * scale).T.astype(BF16)
            dcq_ref[...] = dcq_scr[...]

    tk = nc * ta
    chunks = pl.BlockSpec((None, n_chunks, 1, ta), lambda h, g: (h, 0, 0, 0))
    tile = pl.BlockSpec((tk, HEAD_DIM), lambda h, g: (g, h))
    return pl.pallas_call(
        body, name=name, grid=(H, n_steps),
        in_specs=[pl.BlockSpec((None, tk, HEAD_DIM), lambda h, g: (1, g, h)),
                  pl.BlockSpec((None, tk, HEAD_DIM), lambda h, g: (2, g, h)),
                  pl.BlockSpec((None, nc, HEAD_DIM, ta), lambda h, g: (h, g, 0, 0)),
                  pl.BlockSpec((None, T, HEAD_DIM), lambda h, g: (0, 0, h)),
                  pl.BlockSpec((T, HEAD_DIM), lambda h, g: (0, h)),
                  pl.BlockSpec((None, tk, LANES), lambda h, g: (h, g, 0)),
                  chunks, chunks],
        out_specs=[tile, tile, pl.BlockSpec((None, nc, 1, ta), lambda h, g: (h, g, 0, 0)),
                   pl.BlockSpec((T, HEAD_DIM), lambda h, g: (0, h)), chunks],
        out_shape=[jax.ShapeDtypeStruct((T, W), BF16), jax.ShapeDtypeStruct((T, W), BF16),
                   jax.ShapeDtypeStruct((H, n_chunks, 1, ta), F32), jax.ShapeDtypeStruct((T, W), BF16),
                   jax.ShapeDtypeStruct((H, n_chunks, 1, ta), F32)],
        scratch_shapes=[pltpu.VMEM((nc, ta, HEAD_DIM), F32), pltpu.VMEM((nc, ta, HEAD_DIM), F32),
                        pltpu.VMEM((nc, ta, LANES), F32), pltpu.VMEM((n_chunks, HEAD_DIM, ta), F32),
                        pltpu.VMEM((n_chunks, 1, ta), F32)],
        compiler_params=_params(("parallel", "arbitrary")),
    )(z7, z7, kt, z7, dob, c_rep, lse_chunks, d_chunks)


def _chunk_causal_mask():
    rows = lax.broadcasted_iota(jnp.int32, (SGU_LEN, SGU_LEN), 0)
    cols = lax.broadcasted_iota(jnp.int32, (SGU_LEN, SGU_LEN), 1)
    return (cols // CHUNK) <= (rows // CHUNK)


def _sgu_norm_mix(sv, lng_ref, lnb_ref, ws_ref, bs_ref, vn_scr, mixed_scr, vhat_scr=None):
    tm = sv.shape[0]
    vs = _gelu(sv)
    mask = _chunk_causal_mask()
    rstds = []
    for g in range(N_GROUPS):
        lanes = slice(g * GROUP_DIM, (g + 1) * GROUP_DIM)
        blk = vs[:, lanes]
        cen = blk - jnp.mean(blk, axis=-1, keepdims=True)
        rstd = lax.rsqrt(jnp.mean(cen * cen, axis=-1, keepdims=True) + LN_EPS)
        vhat = cen * rstd
        rstds.append(rstd)
        if vhat_scr is not None:
            vhat_scr[:, lanes] = vhat
        vn_scr[:, lanes] = (vhat * lng_ref[:, lanes] + lnb_ref[:, lanes]).astype(BF16)
        wm = jnp.where(mask, ws_ref[g], 0.0).astype(BF16)
        for w in range(tm // SGU_LEN):
            rows = slice(w * SGU_LEN, (w + 1) * SGU_LEN)
            mixed_scr[rows, lanes] = _mm(wm, vn_scr[rows, lanes]) + bs_ref[g]
    return rstds


def _mix_out_fwd(z7, o_a, x1, lng, lnb, ws, bs, w_out, g_post, name):
    _, T, W = z7.shape
    D = x1.shape[1]
    tm = _blk(T, 256)

    def body(u_ref, sv_ref, ga_ref, gb_ref, oa_ref, x1_ref, lng_ref, lnb_ref, ws_ref, bs_ref, wo_ref, gp_ref,
             x2_ref, p_ref, mb_ref, vn_scr, mixed_scr):
        _sgu_norm_mix(sv_ref[...].astype(F32), lng_ref, lnb_ref, ws_ref, bs_ref, vn_scr, mixed_scr)
        o_b = _gelu(u_ref[...].astype(F32)) * mixed_scr[...]
        merged = (jax.nn.sigmoid(ga_ref[...].astype(F32)) * oa_ref[...]
                  + jax.nn.sigmoid(gb_ref[...].astype(F32)) * o_b).astype(BF16)
        mb_ref[...] = merged
        p = _mm(merged, wo_ref[...])
        p_ref[...] = p
        x2_ref[...] = x1_ref[...] + p * _rms_scale(p) * gp_ref[...]

    def seg(idx):
        return pl.BlockSpec((None, tm, W), lambda i, idx=idx: (idx, i, 0))

    row = pl.BlockSpec((tm, D), lambda i: (i, 0))
    vec = pl.BlockSpec((1, D), lambda i: (0, 0))
    return pl.pallas_call(
        body, name=name, grid=(T // tm,),
        in_specs=[seg(3), seg(4), seg(5), seg(6), row, row, vec, vec,
                  pl.BlockSpec((N_GROUPS, SGU_LEN, SGU_LEN), lambda i: (0, 0, 0)),
                  pl.BlockSpec((N_GROUPS, SGU_LEN, 1), lambda i: (0, 0, 0)),
                  pl.BlockSpec((D, D), lambda i: (0, 0)), vec],
        out_specs=[row, row, row],
        out_shape=[jax.ShapeDtypeStruct((T, D), F32), jax.ShapeDtypeStruct((T, D), F32),
                   jax.ShapeDtypeStruct((T, D), BF16)],
        scratch_shapes=[pltpu.VMEM((tm, W), BF16), pltpu.VMEM((tm, W), F32)],
        compiler_params=_params(("parallel",)),
    )(z7, z7, z7, z7, o_a, x1, lng, lnb, ws, bs, w_out, g_post)


def _mix_out_bwd(dx2, p, z7, o_a, lng, lnb, ws, bs, w_out, g_post, name, dep=None):
    _, T, W = z7.shape
    D = dx2.shape[1]
    tm = _blk(T, 256)
    n_w = tm // SGU_LEN

    def body(dx2_ref, p_ref, u_ref, sv_ref, ga_ref, gb_ref, oa_ref, lng_ref, lnb_ref, ws_ref, bs_ref, wo_ref, gp_ref, _,
             dpb_ref, dob_ref, dvec_ref, dz_ref, dgp_ref, dlng_ref, dlnb_ref, dws_ref, dbs_ref,
             vn_scr, mixed_scr, vhat_scr, dmix_scr, dvn_scr):
        @pl.when(pl.program_id(0) == 0)
        def _():
            dgp_ref[...] = jnp.zeros_like(dgp_ref)
            dlng_ref[...] = jnp.zeros_like(dlng_ref)
            dlnb_ref[...] = jnp.zeros_like(dlnb_ref)
            dws_ref[...] = jnp.zeros_like(dws_ref)
            dbs_ref[...] = jnp.zeros_like(dbs_ref)

        pv = p_ref[...]
        s = _rms_scale(pv)
        n = pv * s
        dn = dx2_ref[...]
        dgp_ref[...] += jnp.sum(dn * n, axis=0, keepdims=True)
        dpb = _rms_bwd(dn, n, s, gp_ref[...]).astype(BF16)
        dpb_ref[...] = dpb
        dmerged = _mm_nt(dpb, wo_ref[...])

        sv = sv_ref[...].astype(F32)
        rstds = _sgu_norm_mix(sv, lng_ref, lnb_ref, ws_ref, bs_ref, vn_scr, mixed_scr, vhat_scr)
        u_pre = u_ref[...].astype(F32)
        u = _gelu(u_pre)
        mixed = mixed_scr[...]
        sa = jax.nn.sigmoid(ga_ref[...].astype(F32))
        sb = jax.nn.sigmoid(gb_ref[...].astype(F32))
        oa = oa_ref[...]
        do_a = (dmerged * sa).astype(BF16)
        dob_ref[...] = do_a
        prod = do_a.astype(F32) * oa
        for h in range(N_HEADS):
            sums = lax.dot_general(jnp.ones((8, LANES), F32), prod[:, h * HEAD_DIM:(h + 1) * HEAD_DIM],
                                   (((1,), (1,)), ((), ())), preferred_element_type=F32,
                                   precision=lax.Precision.HIGHEST)
            dvec_ref[h, 0] = sums[0:1, :]
        dz_ref[2] = (dmerged * oa * (sa * (1.0 - sa))).astype(BF16)
        dz_ref[3] = (dmerged * (u * mixed) * (sb * (1.0 - sb))).astype(BF16)
        do_b = dmerged * sb
        dz_ref[0] = (do_b * mixed * _gelu_grad(u_pre)).astype(BF16)
        dmix_scr[...] = do_b * u

        mask = _chunk_causal_mask()
        for g in range(N_GROUPS):
            lanes = slice(g * GROUP_DIM, (g + 1) * GROUP_DIM)
            wm = jnp.where(mask, ws_ref[g], 0.0).astype(BF16)
            dws = jnp.zeros((SGU_LEN, SGU_LEN), F32)
            dbs = jnp.zeros((SGU_LEN, 1), F32)
            for w in range(n_w):
                rows = slice(w * SGU_LEN, (w + 1) * SGU_LEN)
                dmix = dmix_scr[rows, lanes]
                dmix_b = dmix.astype(BF16)
                dvn_scr[rows, lanes] = _mm_tn(wm, dmix_b)
                dws = dws + _mm_nt(dmix_b, vn_scr[rows, lanes])
                dbs = dbs + jnp.sum(dmix, axis=-1, keepdims=True)
            dws_ref[g] += jnp.where(mask, dws, 0.0)
            dbs_ref[g] += dbs
            dvn = dvn_scr[:, lanes]
            vhat = vhat_scr[:, lanes]
            dlng_ref[:, lanes] += jnp.sum(dvn * vhat, axis=0, keepdims=True)
            dlnb_ref[:, lanes] += jnp.sum(dvn, axis=0, keepdims=True)
            dvh = dvn * lng_ref[:, lanes]
            dvs = rstds[g] * (dvh - jnp.mean(dvh, axis=-1, keepdims=True)
                              - vhat * jnp.mean(dvh * vhat, axis=-1, keepdims=True))
            dvn_scr[:, lanes] = dvs
        dz_ref[1] = (dvn_scr[...] * _gelu_grad(sv)).astype(BF16)

    def seg(idx):
        return pl.BlockSpec((None, tm, W), lambda i, idx=idx: (idx, i, 0))

    row = pl.BlockSpec((tm, D), lambda i: (i, 0))
    vec = pl.BlockSpec((1, D), lambda i: (0, 0))
    ws_spec = pl.BlockSpec((N_GROUPS, SGU_LEN, SGU_LEN), lambda i: (0, 0, 0))
    bs_spec = pl.BlockSpec((N_GROUPS, SGU_LEN, 1), lambda i: (0, 0, 0))
    return pl.pallas_call(
        body, name=name, grid=(T // tm,),
        in_specs=[row, row, seg(3), seg(4), seg(5), seg(6), row, vec, vec, ws_spec, bs_spec,
                  pl.BlockSpec((D, D), lambda i: (0, 0)), vec, ANY],
        out_specs=[row, row, pl.BlockSpec((N_HEADS, 1, 1, tm), lambda i: (0, i, 0, 0)),
                   pl.BlockSpec((4, tm, W), lambda i: (0, i, 0)), vec, vec, vec, ws_spec, bs_spec],
        out_shape=[jax.ShapeDtypeStruct((T, D), BF16), jax.ShapeDtypeStruct((T, W), BF16),
                   jax.ShapeDtypeStruct((N_HEADS, T // tm, 1, tm), F32), jax.ShapeDtypeStruct((4, T, W), BF16),
                   jax.ShapeDtypeStruct((1, D), F32), jax.ShapeDtypeStruct((1, D), F32),
                   jax.ShapeDtypeStruct((1, D), F32),
                   jax.ShapeDtypeStruct((N_GROUPS, SGU_LEN, SGU_LEN), F32),
                   jax.ShapeDtypeStruct((N_GROUPS, SGU_LEN, 1), F32)],
        scratch_shapes=[pltpu.VMEM((tm, W), BF16), pltpu.VMEM((tm, W), F32), pltpu.VMEM((tm, W), F32),
                        pltpu.VMEM((tm, W), F32), pltpu.VMEM((tm, W), F32)],
        compiler_params=_params(("arbitrary",)),
    )(dx2, p, z7, z7, z7, z7, o_a, lng, lnb, ws, bs, w_out, g_post, _after(dep))


def _loss_head(y, target, name):
    T, D = y.shape
    tm = _blk(T, 1024)
    n_i = T // tm

    def body(y_ref, t_ref, dy_ref, loss_ref, acc_scr):
        i = pl.program_id(0)

        @pl.when(i == 0)
        def _():
            acc_scr[...] = jnp.zeros_like(acc_scr)

        e = y_ref[...] - t_ref[...]
        dy_ref[...] = e * np.float32(1.0 / D)
        acc_scr[...] += jnp.sum(e * e, axis=0, keepdims=True)

        @pl.when(i == n_i - 1)
        def _():
            total = jnp.sum(acc_scr[...], axis=-1, keepdims=True) * np.float32(0.5 / D)
            loss_ref[...] = jnp.broadcast_to(total, loss_ref.shape)

    row = pl.BlockSpec((tm, D), lambda i: (i, 0))
    return pl.pallas_call(
        body, name=name, grid=(n_i,),
        in_specs=[row, row],
        out_specs=[row, pl.BlockSpec((1, LANES), lambda i: (0, 0))],
        out_shape=[jax.ShapeDtypeStruct((T, D), F32), jax.ShapeDtypeStruct((1, LANES), F32)],
        scratch_shapes=[pltpu.VMEM((1, D), F32)],
        compiler_params=_params(("arbitrary",)),
    )(y, target)


def _adamw_math(w, g, m, v):
    m_new = ADAM_B1 * m + (1.0 - ADAM_B1) * g
    v_new = ADAM_B2 * v + (1.0 - ADAM_B2) * (g * g)
    m_hat = m_new / np.float32(1.0 - ADAM_B1 ** ADAM_STEP)
    v_hat = v_new / np.float32(1.0 - ADAM_B2 ** ADAM_STEP)
    delta = -ADAM_LR * (m_hat / (jnp.sqrt(v_hat) + ADAM_EPS) + ADAM_WD * w)
    return delta, m_new, v_new


def _sum_adamw(parts, w, m, v, name, dep=None):
    n, R, C = parts.shape
    tr = _blk(R, 512)

    def body(p_ref, w_ref, m_ref, v_ref, _, g_ref, d_ref, mo_ref, vo_ref):
        g = p_ref[0].astype(F32)
        for s in range(1, n):
            g = g + p_ref[s].astype(F32)
        delta, m_new, v_new = _adamw_math(w_ref[...], g, m_ref[...], v_ref[...])
        g_ref[...] = g
        d_ref[...] = delta
        mo_ref[...] = m_new
        vo_ref[...] = v_new

    row = pl.BlockSpec((tr, C), lambda i: (i, 0))
    shp = jax.ShapeDtypeStruct((R, C), F32)
    return pl.pallas_call(
        body, name=name, grid=(R // tr,),
        in_specs=[pl.BlockSpec((n, tr, C), lambda i: (0, i, 0)), row, row, row, ANY],
        out_specs=[row, row, row, row], out_shape=[shp, shp, shp, shp],
        compiler_params=_params(("parallel",)),
    )(parts, w, m, v, _after(dep))


def _position():
    return lax.axis_index("x"), lax.axis_index("y"), lax.axis_index("c")


def _slot(px, py, pc):
    return 4 * px + 2 * py + pc


def _all_gather(shards, name):
    n = len(shards)

    def body(*refs):
        ins, outs = refs[:n], refs[n:2 * n]
        send_sems, recv_sems, local_sems = refs[2 * n:]
        x, y, c = _position()
        me, sibling = (x, y, c), (x, y, 1 - c)
        chips = [(1 - x, y), (x, 1 - y), (1 - x, 1 - y)]

        def copy(a, k, block, to, src=None):
            dst = outs[a].at[_slot(*block)]
            return pltpu.make_async_remote_copy(
                src_ref=dst if src is None else src, dst_ref=dst,
                send_sem=send_sems.at[a, k], recv_sem=recv_sems.at[a, k],
                device_id=to, device_id_type=MESH)

        mine = [pltpu.make_async_copy(ins[a], outs[a].at[_slot(*me)], local_sems.at[a]) for a in range(n)]
        for cp in mine:
            cp.start()
        first = []
        for a in range(n):
            first.append(copy(a, 0, me, sibling, src=ins[a]))
            first += [copy(a, 1 + j, me, (*chip, c), src=ins[a]) for j, chip in enumerate(chips)]
        for cp in first:
            cp.start()
        passed = []
        for j, chip in enumerate(chips):
            for a in range(n):
                copy(a, 1 + j, (*chip, c), me).wait_recv()
                fwd = copy(a, 4 + j, (*chip, c), sibling)
                fwd.start()
                passed.append(fwd)
        for a in range(n):
            copy(a, 0, sibling, me).wait_recv()
            for j, chip in enumerate(chips):
                copy(a, 4 + j, (*chip, 1 - c), me).wait_recv()
        for cp in first + passed:
            cp.wait_send()
        for cp in mine:
            cp.wait()

    return pl.pallas_call(
        body, name=name,
        in_specs=[ANY] * n, out_specs=[ANY] * n,
        out_shape=[jax.ShapeDtypeStruct((N_DEV,) + s.shape, s.dtype) for s in shards],
        scratch_shapes=[pltpu.SemaphoreType.DMA((n, 7)), pltpu.SemaphoreType.DMA((n, 7)),
                        pltpu.SemaphoreType.DMA((n,))],
    )(*shards)


def _peer(x, y, c, k):
    return (1 - x if k & 4 else x, 1 - y if k & 2 else y, 1 - c if k & 1 else c)


def _remote_copies(src_refs, land_refs, send_sems, recv_sems, gather, outgoing):
    x, y, c = _position()
    me = _slot(x, y, c)
    copies = []
    for k in range(1, N_DEV):
        peer = _peer(x, y, c, k)
        for a in range(len(src_refs)):
            src = src_refs[a] if gather else src_refs[a].at[_slot(*peer)]
            dst = land_refs[a].at[me if outgoing else _slot(*peer)]
            sem = a * (N_DEV - 1) + k - 1
            copies.append(pltpu.make_async_remote_copy(
                src_ref=src, dst_ref=dst, send_sem=send_sems.at[sem], recv_sem=recv_sems.at[sem],
                device_id=peer, device_id_type=MESH))
    return copies


def _sequencer_exchange(srcs, name, gather, collective_id):
    n = len(srcs)
    hbm = pltpu.MemorySpace.HBM
    src_refs = [jax.new_ref(s, memory_space=hbm) for s in srcs]
    land_refs = [jax.empty_ref(jax.ShapeDtypeStruct(((N_DEV,) + s.shape) if gather else s.shape, s.dtype),
                               memory_space=hbm) for s in srcs]
    n_sems = n * (N_DEV - 1)
    block_bytes = sum(s.size * s.dtype.itemsize // (1 if gather else N_DEV) for s in srcs)
    cost = pl.CostEstimate(flops=0, transcendentals=0, bytes_accessed=2 * N_DEV * block_bytes,
                           remote_bytes_transferred=(N_DEV - 1) * block_bytes)

    @pl.kernel(mesh=plsc.ScalarSubcoreMesh(axis_name="sequencer", num_cores=1), name=name,
               scratch_types=(pltpu.SemaphoreType.DMA((n_sems,)), pltpu.SemaphoreType.DMA((n_sems,)),
                              pltpu.SemaphoreType.DMA((n,))),
               cost_estimate=cost,
               compiler_params=pltpu.CompilerParams(collective_id=collective_id))
    def launch(send_sems, recv_sems, local_sems):
        x, y, c = _position()
        me = _slot(x, y, c)
        barrier = pltpu.get_barrier_semaphore()
        for k in range(1, N_DEV):
            pl.semaphore_signal(barrier, inc=1, device_id=_peer(x, y, c, k), device_id_type=MESH)
        pl.semaphore_wait(barrier, N_DEV - 1)
        mine = [pltpu.make_async_copy(src_refs[a] if gather else src_refs[a].at[me], land_refs[a].at[me],
                                      local_sems.at[a]) for a in range(n)]
        for cp in mine:
            cp.start()
        sends = _remote_copies(src_refs, land_refs, send_sems, recv_sems, gather, outgoing=True)
        for cp in sends:
            cp.start()
        for cp in _remote_copies(src_refs, land_refs, send_sems, recv_sems, gather, outgoing=False):
            cp.wait_recv()
        for cp in sends:
            cp.wait_send()
        for cp in mine:
            cp.wait()

    launch()
    return [r[...] for r in land_refs]


SMALL_VECS = ("ffn1_pre_g", "ffn1_post_g", "mix_pre_g", "sgu_ln_g", "sgu_ln_b", "mix_post_g", "ffn2_pre_g",
              "ffn2_post_g")
ROW_BS = len(SMALL_VECS)
ROW_BF = ROW_BS + 1
ROW_LOSS = ROW_BF + 1
ROW_WS = 16
BLOB_ROWS = ROW_WS + SGU_LEN


def _pack_small(vals, D, loss_row=None):
    rows = [vals[n].reshape(1, D) for n in SMALL_VECS]
    rows.append(vals["sgu_b_s"].reshape(1, D))
    rows.append(jnp.pad(vals["b_forget"].reshape(1, N_HEADS), ((0, 0), (0, D - N_HEADS))))
    rows.append(jnp.zeros((1, D), F32) if loss_row is None else loss_row)
    rows.append(jnp.zeros((ROW_WS - ROW_LOSS - 1, D), F32))
    rows.append(vals["sgu_w_s"].reshape(SGU_LEN, D))
    return jnp.concatenate(rows, axis=0)


def _unpack_small(blob, D):
    out = {n: blob[r:r + 1] for r, n in enumerate(SMALL_VECS)}
    out["sgu_b_s"] = blob[ROW_BS].reshape(1, N_GROUPS, SGU_LEN)
    out["b_forget"] = blob[ROW_BF, :N_HEADS].reshape(1, N_HEADS)
    out["sgu_w_s"] = blob[ROW_WS:].reshape(1, N_GROUPS, SGU_LEN, SGU_LEN)
    return out


WEIGHT_NAMES = ("ffn1_pre_g", "ffn1_w_gate", "ffn1_w_up", "ffn1_w_down", "ffn1_post_g", "mix_pre_g", "w_in",
                "b_forget", "sgu_ln_g", "sgu_ln_b", "sgu_w_s", "sgu_b_s", "w_out", "mix_post_g", "ffn2_pre_g",
                "ffn2_w_gate", "ffn2_w_up", "ffn2_w_down", "ffn2_post_g")
BIG_NAMES = ("ffn1_w_gate", "ffn1_w_up", "ffn1_w_down", "w_in", "w_out", "ffn2_w_gate", "ffn2_w_up", "ffn2_w_down")
WEIGHT_GROUPS = {"ffn1": ("ffn1_w_gate", "ffn1_w_up", "ffn1_w_down"), "mix": ("w_in", "w_out"),
                 "ffn2": ("ffn2_w_gate", "ffn2_w_up", "ffn2_w_down")}
GRAD_GROUPS = (("ffn2_w_gate", "ffn2_w_up", "ffn2_w_down"), ("w_in", "w_out"), ("ffn1_w_down",), ("ffn1_w_gate",),
               ("ffn1_w_up",))


def _local_step(x, target, small, fetch, emit, consume):
    T, D = x.shape
    W = N_HEADS * HEAD_DIM
    vec = lambda n: small[n].reshape(1, D)
    big = dict(fetch("ffn1", x))

    x1, y1, dgf1, silu1, act1 = _ffn_fwd(x, vec("ffn1_pre_g"), big["ffn1_w_gate"], big["ffn1_w_up"], big["ffn1_w_down"],
                                  vec("ffn1_post_g"), "ffn1_fwd")

    big.update(fetch("mix", x1))
    w_in_all = big["w_in"]
    in_width = N_DEV * w_in_all.shape[2]
    w_in = w_in_all.transpose(1, 0, 2).reshape(D, in_width)
    col_f = 3 * W
    col_u = col_f + N_HEADS
    seg_starts = (0, W, 2 * W, col_u, col_u + W, col_u + 2 * W, col_u + 3 * W)
    w7 = jnp.stack([w_in[:, s:s + W] for s in seg_starts])
    wf = jnp.pad(w_in[:, col_f:col_u], ((0, 0), (0, LANES - N_HEADS)))
    w_out = big["w_out"].reshape(D, D)
    b_pad = jnp.pad(small["b_forget"].reshape(1, N_HEADS), ((0, 0), (0, LANES - N_HEADS)))
    lng, lnb = vec("sgu_ln_g"), vec("sgu_ln_b")
    ws = small["sgu_w_s"].reshape(N_GROUPS, SGU_LEN, SGU_LEN)
    bs = small["sgu_b_s"].reshape(N_GROUPS, SGU_LEN, 1)

    z7, f_logit, h2b = _mix_in_fwd(x1, vec("mix_pre_g"), w7, wf, "mix_in_fwd")
    c_rep = _forget_cumsum(f_logit, b_pad, "forget_cumsum")
    ta, _, n_chunks = _attn_geometry(T)
    vt = z7[2].reshape(n_chunks, ta, N_HEADS, HEAD_DIM).transpose(2, 0, 3, 1)
    o_a, lse_chunks = _attn_fwd_keys_on_rows(z7, vt, c_rep, "attn_fwd")
    x2, p, merged_b = _mix_out_fwd(z7, o_a, x1, lng, lnb, ws, bs, w_out, vec("mix_post_g"), "mix_out_fwd")
    big.update(fetch("ffn2", x2))
    x3, y2, dgf2, silu2, act2 = _ffn_fwd(x2, vec("ffn2_pre_g"), big["ffn2_w_gate"], big["ffn2_w_up"], big["ffn2_w_down"],
                                  vec("ffn2_post_g"), "ffn2_fwd")
    dy, loss_lanes = _loss_head(x3, target, "loss_head")

    grads_small = {}

    dx2, h3b, dy2b, dgate2, dup2, dgpre, dgpost = _ffn_bwd(
        dy, x2, y2, dgf2, silu2, vec("ffn2_pre_g"), big["ffn2_w_gate"], big["ffn2_w_up"], big["ffn2_w_down"],
        vec("ffn2_post_g"), "ffn2_bwd")
    grads_small["ffn2_pre_g"] = jnp.sum(dgpre, axis=0)
    grads_small["ffn2_post_g"] = jnp.sum(dgpost, axis=0)
    dep = emit("ffn2_w_gate", _wgrad(h3b, dgate2, "ffn2_wgrad_gate", shard_cols=True))
    dep = emit("ffn2_w_up", _wgrad(h3b, dup2, "ffn2_wgrad_up", shard_cols=True, dep=dep))
    dep = emit("ffn2_w_down", _wgrad(act2, dy2b, "ffn2_wgrad_down", dep=dep).reshape(big["ffn2_w_down"].shape))

    dpb, dob, dvec, dz4, dgp, dlng, dlnb, dws, dbs = _mix_out_bwd(
        dx2, p, z7, o_a, lng, lnb, ws, bs, w_out, vec("mix_post_g"), "mix_out_bwd", dep=dep)
    grads_small["mix_post_g"] = dgp
    grads_small["sgu_ln_g"] = dlng
    grads_small["sgu_ln_b"] = dlnb
    grads_small["sgu_w_s"] = dws
    grads_small["sgu_b_s"] = dbs
    d_chunks = dvec.reshape(N_HEADS, n_chunks, 1, ta)
    kt = z7[1].reshape(n_chunks, ta, N_HEADS, HEAD_DIM).transpose(2, 0, 3, 1)
    dk, dv, dc, dq, dc_q = _attn_bwd_fused(z7, kt, dob, c_rep, lse_chunks, d_chunks, "attn_bwd")
    dc_pad = jnp.pad((dc + dc_q).reshape(N_HEADS, T).T, ((0, 0), (0, LANES - N_HEADS)))
    dfb, dbf = _forget_bwd(dc_pad, f_logit, b_pad, "forget_bwd")
    grads_small["b_forget"] = dbf[:, :N_HEADS]
    segs = [(dq, None), (dk, None), (dv, None), (dz4, 0), (dz4, 1), (dz4, 2), (dz4, 3)]
    dep = consume(("ffn2_w_gate", "ffn2_w_up", "ffn2_w_down"))
    dx1, dgm = _mix_in_bwd(dx2, x1, vec("mix_pre_g"), segs, dfb, w7, wf, "mix_in_bwd", dep=dep)
    grads_small["mix_pre_g"] = jnp.sum(dgm, axis=0)
    dw_qkv = _wgrad_multi(h2b, segs[:3], "w_in_wgrad_qkv", dep=dx1)
    dw_rest = _wgrad_multi(h2b, segs[3:], "w_in_wgrad_gates", dep=dw_qkv)
    dw_seg = [dw_qkv[:, q * W:(q + 1) * W] for q in range(3)] + [dw_rest[:, q * W:(q + 1) * W] for q in range(4)]
    dwf = _wgrad(h2b, dfb, "w_in_wgrad_f", dep=dw_rest)
    dw_in = jnp.concatenate(dw_seg[:3] + [dwf[:, :N_HEADS]] + dw_seg[3:], axis=1)
    emit("w_in", dw_in.reshape(D, N_DEV, in_width // N_DEV).transpose(1, 0, 2))
    dep = emit("w_out", _wgrad(merged_b, dpb, "w_out_wgrad", dep=dwf).reshape(big["w_out"].shape))

    dx0, h1b, dy1b, dgate1, dup1, dgpre1, dgpost1 = _ffn_bwd(
        dx1, x, y1, dgf1, silu1, vec("ffn1_pre_g"), big["ffn1_w_gate"], big["ffn1_w_up"], big["ffn1_w_down"],
        vec("ffn1_post_g"), "ffn1_bwd", dep=dep)
    grads_small["ffn1_pre_g"] = jnp.sum(dgpre1, axis=0)
    grads_small["ffn1_post_g"] = jnp.sum(dgpost1, axis=0)
    dep = consume(("w_in", "w_out"))
    dep = emit("ffn1_w_down", _wgrad(act1, dy1b, "ffn1_wgrad_down", dep=dep).reshape(big["ffn1_w_down"].shape))
    dep = emit("ffn1_w_gate", _wgrad(h1b, dgate1, "ffn1_wgrad_gate", shard_cols=True, dep=dep))
    dep = emit("ffn1_w_up", _wgrad(h1b, dup1, "ffn1_wgrad_up", shard_cols=True, dep=dep))

    loss_row = jnp.pad(loss_lanes, ((0, 0), (0, D - LANES)))
    return loss_row, dx0, grads_small


def kernel(x, ffn1_pre_g, ffn1_w_gate, ffn1_w_up, ffn1_w_down, ffn1_post_g, mix_pre_g, w_in, b_forget, sgu_ln_g, sgu_ln_b, sgu_w_s, sgu_b_s, w_out, mix_post_g, ffn2_pre_g, ffn2_w_gate, ffn2_w_up, ffn2_w_down, ffn2_post_g, loss_target, m_ffn1_pre_g, m_ffn1_w_gate, m_ffn1_w_up, m_ffn1_w_down, m_ffn1_post_g, m_mix_pre_g, m_w_in, m_b_forget, m_sgu_ln_g, m_sgu_ln_b, m_sgu_w_s, m_sgu_b_s, m_w_out, m_mix_post_g, m_ffn2_pre_g, m_ffn2_w_gate, m_ffn2_w_up, m_ffn2_w_down, m_ffn2_post_g, v_ffn1_pre_g, v_ffn1_w_gate, v_ffn1_w_up, v_ffn1_w_down, v_ffn1_post_g, v_mix_pre_g, v_w_in, v_b_forget, v_sgu_ln_g, v_sgu_ln_b, v_sgu_w_s, v_sgu_b_s, v_w_out, v_mix_post_g, v_ffn2_pre_g, v_ffn2_w_gate, v_ffn2_w_up, v_ffn2_w_down, v_ffn2_post_g):
    weights = dict(zip(WEIGHT_NAMES, (ffn1_pre_g, ffn1_w_gate, ffn1_w_up, ffn1_w_down, ffn1_post_g, mix_pre_g, w_in,
                                      b_forget, sgu_ln_g, sgu_ln_b, sgu_w_s, sgu_b_s, w_out, mix_post_g, ffn2_pre_g,
                                      ffn2_w_gate, ffn2_w_up, ffn2_w_down, ffn2_post_g)))
    mom1 = dict(zip(WEIGHT_NAMES, (m_ffn1_pre_g, m_ffn1_w_gate, m_ffn1_w_up, m_ffn1_w_down, m_ffn1_post_g,
                                   m_mix_pre_g, m_w_in, m_b_forget, m_sgu_ln_g, m_sgu_ln_b, m_sgu_w_s, m_sgu_b_s,
                                   m_w_out, m_mix_post_g, m_ffn2_pre_g, m_ffn2_w_gate, m_ffn2_w_up, m_ffn2_w_down,
                                   m_ffn2_post_g)))
    mom2 = dict(zip(WEIGHT_NAMES, (v_ffn1_pre_g, v_ffn1_w_gate, v_ffn1_w_up, v_ffn1_w_down, v_ffn1_post_g,
                                   v_mix_pre_g, v_w_in, v_b_forget, v_sgu_ln_g, v_sgu_ln_b, v_sgu_w_s, v_sgu_b_s,
                                   v_w_out, v_mix_post_g, v_ffn2_pre_g, v_ffn2_w_gate, v_ffn2_w_up, v_ffn2_w_down,
                                   v_ffn2_post_g)))
    D = x.shape[-1]
    small_names = [n for n in WEIGHT_NAMES if n not in BIG_NAMES]

    small = {n: weights[n] for n in small_names}
    shard = lambda n: weights[n][0].astype(BF16)

    ffn1_full = _all_gather([shard(n) for n in WEIGHT_GROUPS["ffn1"]], "ffn1_all_gather")
    gathered = {}
    for cid, grp in ((1, "mix"), (2, "ffn2")):
        shards, _ = lax.optimization_barrier(([shard(n) for n in WEIGHT_GROUPS[grp]], ffn1_full[0]))
        gathered[grp] = _sequencer_exchange(shards, grp + "_gather", True, cid)

    def fetch(group, after):
        if group == "ffn1":
            return zip(WEIGHT_GROUPS[group], ffn1_full)
        arrived, _ = lax.optimization_barrier((gathered[group], after))
        return zip(WEIGHT_GROUPS[group], arrived)

    ready, received = {}, {}

    def emit(name, part):
        ready[name] = part
        for gi, group in enumerate(GRAD_GROUPS):
            if name == group[-1]:
                lands = _sequencer_exchange([ready[n] for n in group], name + "_grad_exchange", False, 3 + gi)
                received.update(zip(group, lands))
        return part

    out = {}

    def consume(names, dep=None):
        for n in names:
            g, d, m_new, v_new = _sum_adamw(received[n], weights[n][0], mom1[n][0], mom2[n][0], "adamw_" + n, dep=dep)
            out[n] = tuple(a[None] for a in (g, d, m_new, v_new))
            dep = g
        return dep

    loss_row, grad_x, grads_small = _local_step(x[0], loss_target[0], small, fetch, emit, consume)

    blobs = _sequencer_exchange([_pack_small(grads_small, D, loss_row)], "small_gather", True,
                                3 + len(GRAD_GROUPS))[0]
    blob, d_blob, m_blob, v_blob = _sum_adamw(
        blobs, _pack_small(small, D), _pack_small({n: mom1[n] for n in small_names}, D),
        _pack_small({n: mom2[n] for n in small_names}, D), "adamw_small")
    consume(("ffn1_w_down", "ffn1_w_gate", "ffn1_w_up"), dep=blob)
    unpacked = [_unpack_small(b, D) for b in (blob, d_blob, m_blob, v_blob)]
    for n in small_names:
        out[n] = tuple(u[n].reshape(weights[n].shape) for u in unpacked)

    loss = blob[ROW_LOSS, 0]
    result = [loss, grad_x[None]]
    for k in range(4):
        result += [out[n][k] for n in WEIGHT_NAMES]
    return tuple(result)
```
